```python
import jax, jax.numpy as jnp
from jax import lax
import numpy as np

D_MODEL = 1024
BATCH = 8
SEQ = 4096
DEPTH = 1

CHUNK = 64
GLA_HEADS = 4
GLA_DK = D_MODEL // 2 // GLA_HEADS
GLA_DV = D_MODEL // GLA_HEADS
GLA_QK = GLA_HEADS * GLA_DK
GLA_V = GLA_HEADS * GLA_DV
GLA_GATE_RANK = 16
GLA_TAU = 16.0
SGU_GROUPS = 4
SGU_BLOCK = 128
SGU_WIDTH = D_MODEL
SGU_DG = SGU_WIDTH // SGU_GROUPS
D_FF = -(-8 * D_MODEL // (3 * 256)) * 256
EPS = 1e-6

IN_SPLITS = (GLA_QK, GLA_QK, GLA_V, GLA_V, GLA_GATE_RANK,
             SGU_WIDTH, SGU_WIDTH, D_MODEL, D_MODEL)
D_IN = int(sum(IN_SPLITS))
IN_OFFSETS = tuple(int(o) for o in np.cumsum(IN_SPLITS)[:-1])

kernel_name = "hybrid_gla_sgu_swiglu_sandwich"


def rmsnorm(x, g):
    xf = x.astype(jnp.float32)
    y = xf * lax.rsqrt(jnp.mean(xf * xf, axis=-1, keepdims=True) + EPS)
    return (y * g.astype(jnp.float32)).astype(x.dtype)


def gla_branch(q, k, v, r, a_low, w_gate_up, b_gate, gla_norm):
    B, S, _ = q.shape
    N = S // CHUNK
    f32 = jnp.float32

    def heads(t, d):
        return t.reshape(B, N, CHUNK, GLA_HEADS, d)

    qh = heads(q, GLA_DK).astype(f32) * (GLA_DK ** -0.5)
    kh = heads(k, GLA_DK).astype(f32)
    vh = heads(v, GLA_DV).astype(f32)
    logit = jnp.einsum('bsr,rk->bsk', a_low, w_gate_up) + b_gate
    log_a = heads(jax.nn.log_sigmoid(logit.astype(f32)) / GLA_TAU, GLA_DK)
    cum = jnp.cumsum(log_a, axis=2)
    tot = cum[:, :, -1]
    k_dec = kh * jnp.exp(tot[:, :, None] - cum)
    upd = jnp.einsum('bnchk,bnchv->bnhkv', k_dec, vh)
    decay = jnp.exp(tot)

    def step(state, inp):
        a_c, u_c = inp
        state = a_c[..., None] * state + u_c
        return state, state

    s0 = jnp.zeros((B, GLA_HEADS, GLA_DK, GLA_DV), f32)
    _, states = lax.scan(step, s0, (jnp.moveaxis(decay, 1, 0), jnp.moveaxis(upd, 1, 0)))
    states = jnp.moveaxis(states, 0, 1)
    o = jnp.einsum('bnchk,bnhkv->bnchv', qh, states)
    o = o * lax.rsqrt(jnp.mean(o * o, axis=-1, keepdims=True) + EPS) * gla_norm.astype(f32)
    o = o.reshape(B, S, GLA_V).astype(q.dtype)
    return o * jax.nn.silu(r)


def sgu_branch(u, v, ln_g, ln_b, w_spatial, b_spatial):
    B, S, _ = u.shape
    u = jax.nn.gelu(u)
    vf = jax.nn.gelu(v).reshape(B, S, SGU_GROUPS, SGU_DG).astype(jnp.float32)
    mu = jnp.mean(vf, axis=-1, keepdims=True)
    var = jnp.mean(jnp.square(vf - mu), axis=-1, keepdims=True)
    vn = (vf - mu) * lax.rsqrt(var + EPS) * ln_g.astype(jnp.float32) + ln_b.astype(jnp.float32)
    vn = vn.astype(u.dtype).reshape(B, S // SGU_BLOCK, SGU_BLOCK, SGU_GROUPS, SGU_DG)
    pos = np.arange(SGU_BLOCK)
    mask = (pos[None, :] // CHUNK) <= (pos[:, None] // CHUNK)
    w = jnp.where(mask[None], w_spatial, 0.0)
    mixed = jnp.einsum('gij,bnjgc->bnigc', w, vn) + b_spatial.T[:, :, None]
    return u * mixed.reshape(B, S, SGU_WIDTH)


def _fwd_setup_inputs(seed: int = 0) -> dict:
    key = jax.random.key(seed)
    ks = jax.random.split(key, 20)
    L = DEPTH
    nrm = jax.random.normal

    def gain(k, shape):
        return 1.0 + 0.05 * nrm(k, shape, jnp.float32)

    return {
        "x": nrm(ks[0], (BATCH, SEQ, D_MODEL), jnp.float32),
        "norm_pre_mix": gain(ks[1], (L, D_MODEL)),
        "w_in": nrm(ks[2], (L, D_MODEL, D_IN), jnp.float32) * D_MODEL ** -0.5,
        "w_gate_up": nrm(ks[3], (L, GLA_GATE_RANK, GLA_QK), jnp.float32) * GLA_GATE_RANK ** -0.5,
        "b_gate": 0.01 * nrm(ks[4], (L, GLA_QK), jnp.float32),
        "gla_norm": gain(ks[5], (L, GLA_HEADS, GLA_DV)),
        "sgu_ln_g": gain(ks[6], (L, SGU_GROUPS, SGU_DG)),
        "sgu_ln_b": 0.02 * nrm(ks[7], (L, SGU_GROUPS, SGU_DG), jnp.float32),
        "w_spatial": nrm(ks[8], (L, SGU_GROUPS, SGU_BLOCK, SGU_BLOCK), jnp.float32) * SGU_BLOCK ** -0.5,
        "b_spatial": 1.0 + 0.02 * nrm(ks[9], (L, SGU_GROUPS, SGU_BLOCK), jnp.float32),
        "w_branch_gla": nrm(ks[10], (L, GLA_V, D_MODEL), jnp.float32) * GLA_V ** -0.5,
        "w_branch_sgu": nrm(ks[11], (L, SGU_WIDTH, D_MODEL), jnp.float32) * SGU_WIDTH ** -0.5,
        "w_out": nrm(ks[12], (L, D_MODEL, D_MODEL), jnp.float32) * D_MODEL ** -0.5,
        "norm_post_mix": gain(ks[13], (L, D_MODEL)),
        "norm_pre_ffn": gain(ks[14], (L, D_MODEL)),
        "w_ffn_in": nrm(ks[15], (L, D_MODEL, 2 * D_FF), jnp.float32) * D_MODEL ** -0.5,
        "w_ffn_out": nrm(ks[16], (L, D_FF, D_MODEL), jnp.float32) * D_FF ** -0.5,
        "norm_post_ffn": gain(ks[17], (L, D_MODEL)),
    }


def _fwd_reference(x, norm_pre_mix, w_in, w_gate_up, b_gate, gla_norm, sgu_ln_g, sgu_ln_b,
              w_spatial, b_spatial, w_branch_gla, w_branch_sgu, w_out, norm_post_mix,
              norm_pre_ffn, w_ffn_in, w_ffn_out, norm_post_ffn):
    for l in range(DEPTH):
        a = rmsnorm(x, norm_pre_mix[l])
        proj = jnp.einsum('bsd,de->bse', a, w_in[l])
        q, k, v, r, a_low, su, sv, g_gla, g_sgu = jnp.split(proj, IN_OFFSETS, axis=-1)
        y_gla = gla_branch(q, k, v, r, a_low, w_gate_up[l], b_gate[l], gla_norm[l])
        y_sgu = sgu_branch(su, sv, sgu_ln_g[l], sgu_ln_b[l], w_spatial[l], b_spatial[l])
        merged = (jax.nn.sigmoid(g_gla) * jnp.einsum('bsv,vd->bsd', y_gla, w_branch_gla[l])
                  + jax.nn.sigmoid(g_sgu) * jnp.einsum('bsv,vd->bsd', y_sgu, w_branch_sgu[l]))
        mix = jnp.einsum('bsd,de->bse', merged, w_out[l])
        x = x + rmsnorm(mix, norm_post_mix[l])
        h = rmsnorm(x, norm_pre_ffn[l])
        gate, up = jnp.split(jnp.einsum('bsd,df->bsf', h, w_ffn_in[l]), 2, axis=-1)
        y = jnp.einsum('bsf,fd->bsd', jax.nn.silu(gate) * up, w_ffn_out[l])
        x = x + rmsnorm(y, norm_post_ffn[l])
    return x


import jax as _jax
import jax.numpy as _jnp

TWIN_FORMAT = 'train_step'
FWD_PARAMS = ['x', 'norm_pre_mix', 'w_in', 'w_gate_up', 'b_gate', 'gla_norm', 'sgu_ln_g', 'sgu_ln_b', 'w_spatial', 'b_spatial', 'w_branch_gla', 'w_branch_sgu', 'w_out', 'norm_post_mix', 'norm_pre_ffn', 'w_ffn_in', 'w_ffn_out', 'norm_post_ffn']
TWIN_WEIGHTS = ['norm_pre_mix', 'w_in', 'w_gate_up', 'b_gate', 'gla_norm', 'sgu_ln_g', 'sgu_ln_b', 'w_spatial', 'b_spatial', 'w_branch_gla', 'w_branch_sgu', 'w_out', 'norm_post_mix', 'norm_pre_ffn', 'w_ffn_in', 'w_ffn_out', 'norm_post_ffn']
TWIN_DIFF_INPUT = 'x'
TWIN_INPUTS = ['x', 'norm_pre_mix', 'w_in', 'w_gate_up', 'b_gate', 'gla_norm', 'sgu_ln_g', 'sgu_ln_b', 'w_spatial', 'b_spatial', 'w_branch_gla', 'w_branch_sgu', 'w_out', 'norm_post_mix', 'norm_pre_ffn', 'w_ffn_in', 'w_ffn_out', 'norm_post_ffn', 'loss_target', 'm_norm_pre_mix', 'm_w_in', 'm_w_gate_up', 'm_b_gate', 'm_gla_norm', 'm_sgu_ln_g', 'm_sgu_ln_b', 'm_w_spatial', 'm_b_spatial', 'm_w_branch_gla', 'm_w_branch_sgu', 'm_w_out', 'm_norm_post_mix', 'm_norm_pre_ffn', 'm_w_ffn_in', 'm_w_ffn_out', 'm_norm_post_ffn', 'v_norm_pre_mix', 'v_w_in', 'v_w_gate_up', 'v_b_gate', 'v_gla_norm', 'v_sgu_ln_g', 'v_sgu_ln_b', 'v_w_spatial', 'v_b_spatial', 'v_w_branch_gla', 'v_w_branch_sgu', 'v_w_out', 'v_norm_post_mix', 'v_norm_pre_ffn', 'v_w_ffn_in', 'v_w_ffn_out', 'v_norm_post_ffn']
TWIN_OUTPUTS = ['loss', 'grad_x', 'grad_norm_pre_mix', 'grad_w_in', 'grad_w_gate_up', 'grad_b_gate', 'grad_gla_norm', 'grad_sgu_ln_g', 'grad_sgu_ln_b', 'grad_w_spatial', 'grad_b_spatial', 'grad_w_branch_gla', 'grad_w_branch_sgu', 'grad_w_out', 'grad_norm_post_mix', 'grad_norm_pre_ffn', 'grad_w_ffn_in', 'grad_w_ffn_out', 'grad_norm_post_ffn', 'delta_norm_pre_mix', 'delta_w_in', 'delta_w_gate_up', 'delta_b_gate', 'delta_gla_norm', 'delta_sgu_ln_g', 'delta_sgu_ln_b', 'delta_w_spatial', 'delta_b_spatial', 'delta_w_branch_gla', 'delta_w_branch_sgu', 'delta_w_out', 'delta_norm_post_mix', 'delta_norm_pre_ffn', 'delta_w_ffn_in', 'delta_w_ffn_out', 'delta_norm_post_ffn', 'new_m_norm_pre_mix', 'new_m_w_in', 'new_m_w_gate_up', 'new_m_b_gate', 'new_m_gla_norm', 'new_m_sgu_ln_g', 'new_m_sgu_ln_b', 'new_m_w_spatial', 'new_m_b_spatial', 'new_m_w_branch_gla', 'new_m_w_branch_sgu', 'new_m_w_out', 'new_m_norm_post_mix', 'new_m_norm_pre_ffn', 'new_m_w_ffn_in', 'new_m_w_ffn_out', 'new_m_norm_post_ffn', 'new_v_norm_pre_mix', 'new_v_w_in', 'new_v_w_gate_up', 'new_v_b_gate', 'new_v_gla_norm', 'new_v_sgu_ln_g', 'new_v_sgu_ln_b', 'new_v_w_spatial', 'new_v_b_spatial', 'new_v_w_branch_gla', 'new_v_w_branch_sgu', 'new_v_w_out', 'new_v_norm_post_mix', 'new_v_norm_pre_ffn', 'new_v_w_ffn_in', 'new_v_w_ffn_out', 'new_v_norm_post_ffn']
TWIN_LEAF_KINDS = {'loss': 'loss', 'grad_x': 'grad_x', 'grad_norm_pre_mix': 'grad_w', 'grad_w_in': 'grad_w', 'grad_w_gate_up': 'grad_w', 'grad_b_gate': 'grad_w', 'grad_gla_norm': 'grad_w', 'grad_sgu_ln_g': 'grad_w', 'grad_sgu_ln_b': 'grad_w', 'grad_w_spatial': 'grad_w', 'grad_b_spatial': 'grad_w', 'grad_w_branch_gla': 'grad_w', 'grad_w_branch_sgu': 'grad_w', 'grad_w_out': 'grad_w', 'grad_norm_post_mix': 'grad_w', 'grad_norm_pre_ffn': 'grad_w', 'grad_w_ffn_in': 'grad_w', 'grad_w_ffn_out': 'grad_w', 'grad_norm_post_ffn': 'grad_w', 'delta_norm_pre_mix': 'delta_w', 'delta_w_in': 'delta_w', 'delta_w_gate_up': 'delta_w', 'delta_b_gate': 'delta_w', 'delta_gla_norm': 'delta_w', 'delta_sgu_ln_g': 'delta_w', 'delta_sgu_ln_b': 'delta_w', 'delta_w_spatial': 'delta_w', 'delta_b_spatial': 'delta_w', 'delta_w_branch_gla': 'delta_w', 'delta_w_branch_sgu': 'delta_w', 'delta_w_out': 'delta_w', 'delta_norm_post_mix': 'delta_w', 'delta_norm_pre_ffn': 'delta_w', 'delta_w_ffn_in': 'delta_w', 'delta_w_ffn_out': 'delta_w', 'delta_norm_post_ffn': 'delta_w', 'new_m_norm_pre_mix': 'new_m', 'new_m_w_in': 'new_m', 'new_m_w_gate_up': 'new_m', 'new_m_b_gate': 'new_m', 'new_m_gla_norm': 'new_m', 'new_m_sgu_ln_g': 'new_m', 'new_m_sgu_ln_b': 'new_m', 'new_m_w_spatial': 'new_m', 'new_m_b_spatial': 'new_m', 'new_m_w_branch_gla': 'new_m', 'new_m_w_branch_sgu': 'new_m', 'new_m_w_out': 'new_m', 'new_m_norm_post_mix': 'new_m', 'new_m_norm_pre_ffn': 'new_m', 'new_m_w_ffn_in': 'new_m', 'new_m_w_ffn_out': 'new_m', 'new_m_norm_post_ffn': 'new_m', 'new_v_norm_pre_mix': 'new_v', 'new_v_w_in': 'new_v', 'new_v_w_gate_up': 'new_v', 'new_v_b_gate': 'new_v', 'new_v_gla_norm': 'new_v', 'new_v_sgu_ln_g': 'new_v', 'new_v_sgu_ln_b': 'new_v', 'new_v_w_spatial': 'new_v', 'new_v_b_spatial': 'new_v', 'new_v_w_branch_gla': 'new_v', 'new_v_w_branch_sgu': 'new_v', 'new_v_w_out': 'new_v', 'new_v_norm_post_mix': 'new_v', 'new_v_norm_pre_ffn': 'new_v', 'new_v_w_ffn_in': 'new_v', 'new_v_w_ffn_out': 'new_v', 'new_v_norm_post_ffn': 'new_v'}


def _forward(args):
    return _fwd_reference(*[args[k] for k in FWD_PARAMS])


def _output_shape():
    out = _jax.eval_shape(lambda: _forward(_fwd_setup_inputs(0)))
    return out.shape, out.dtype

N_MICROBATCH = 1
ADAM_LR = 0.001
ADAM_B1 = 0.9
ADAM_B2 = 0.999
ADAM_EPS = 1e-08
ADAM_WD = 0.01
ADAM_STEP = 10
PER_EXAMPLE_BATCH_AXIS = {'x': 0, 'loss_target': 0}
SHARED_INPUTS = []
_WEIGHT_DTYPES = {'norm_pre_mix': _jnp.float32, 'w_in': _jnp.float32, 'w_gate_up': _jnp.float32, 'b_gate': _jnp.float32, 'gla_norm': _jnp.float32, 'sgu_ln_g': _jnp.float32, 'sgu_ln_b': _jnp.float32, 'w_spatial': _jnp.float32, 'b_spatial': _jnp.float32, 'w_branch_gla': _jnp.float32, 'w_branch_sgu': _jnp.float32, 'w_out': _jnp.float32, 'norm_post_mix': _jnp.float32, 'norm_pre_ffn': _jnp.float32, 'w_ffn_in': _jnp.float32, 'w_ffn_out': _jnp.float32, 'norm_post_ffn': _jnp.float32}
MOMENT_SCALE = {'norm_pre_mix': 7.388240e-01, 'w_in': 2.856043e-01, 'w_gate_up': 4.418158e-02, 'b_gate': 1.565405e-01, 'gla_norm': 2.605351e-01, 'sgu_ln_g': 2.431475e-01, 'sgu_ln_b': 2.394231e-01, 'w_spatial': 3.304164e-01, 'b_spatial': 3.670520e-01, 'w_branch_gla': 2.605432e-01, 'w_branch_sgu': 7.818281e-01, 'w_out': 8.957911e-01, 'norm_post_mix': 3.231572e+01, 'norm_pre_ffn': 8.130397e-01, 'w_ffn_in': 3.527537e-01, 'w_ffn_out': 7.124003e-01, 'norm_post_ffn': 3.199571e+01}


def _to_microbatches(a, axis):
    t = _jnp.moveaxis(a, axis, 0)
    t = t.reshape((N_MICROBATCH, t.shape[0] // N_MICROBATCH) + t.shape[1:])
    return _jnp.moveaxis(t, 1, axis + 1)


def setup_inputs(seed: int = 0) -> dict:
    inp = _fwd_setup_inputs(seed)
    key = _jax.random.fold_in(_jax.random.key(seed), 7919)
    shape, _ = _output_shape()
    out = dict(inp)
    out["loss_target"] = _jax.random.normal(_jax.random.fold_in(key, 0), shape, _jnp.float32)
    for i, name in enumerate(TWIN_WEIGHTS):
        w = inp[name].astype(_jnp.float32)
        if MOMENT_SCALE is None:
            s = _jnp.sqrt(_jnp.mean(_jnp.square(w)) + 1e-30)
        else:
            s = MOMENT_SCALE[name]
        km, kv = _jax.random.split(_jax.random.fold_in(key, i + 1))
        out[name] = w
        out["m_" + name] = s * _jax.random.normal(km, w.shape, _jnp.float32)
        out["v_" + name] = (s * s) * _jax.random.uniform(kv, w.shape, _jnp.float32, 0.5, 1.5)
    if N_MICROBATCH > 1:
        for name, axis in PER_EXAMPLE_BATCH_AXIS.items():
            out[name] = _to_microbatches(out[name], axis)
    return {'x': out['x'], 'norm_pre_mix': out['norm_pre_mix'], 'w_in': out['w_in'], 'w_gate_up': out['w_gate_up'], 'b_gate': out['b_gate'], 'gla_norm': out['gla_norm'], 'sgu_ln_g': out['sgu_ln_g'], 'sgu_ln_b': out['sgu_ln_b'], 'w_spatial': out['w_spatial'], 'b_spatial': out['b_spatial'], 'w_branch_gla': out['w_branch_gla'], 'w_branch_sgu': out['w_branch_sgu'], 'w_out': out['w_out'], 'norm_post_mix': out['norm_post_mix'], 'norm_pre_ffn': out['norm_pre_ffn'], 'w_ffn_in': out['w_ffn_in'], 'w_ffn_out': out['w_ffn_out'], 'norm_post_ffn': out['norm_post_ffn'], 'loss_target': out['loss_target'], 'm_norm_pre_mix': out['m_norm_pre_mix'], 'm_w_in': out['m_w_in'], 'm_w_gate_up': out['m_w_gate_up'], 'm_b_gate': out['m_b_gate'], 'm_gla_norm': out['m_gla_norm'], 'm_sgu_ln_g': out['m_sgu_ln_g'], 'm_sgu_ln_b': out['m_sgu_ln_b'], 'm_w_spatial': out['m_w_spatial'], 'm_b_spatial': out['m_b_spatial'], 'm_w_branch_gla': out['m_w_branch_gla'], 'm_w_branch_sgu': out['m_w_branch_sgu'], 'm_w_out': out['m_w_out'], 'm_norm_post_mix': out['m_norm_post_mix'], 'm_norm_pre_ffn': out['m_norm_pre_ffn'], 'm_w_ffn_in': out['m_w_ffn_in'], 'm_w_ffn_out': out['m_w_ffn_out'], 'm_norm_post_ffn': out['m_norm_post_ffn'], 'v_norm_pre_mix': out['v_norm_pre_mix'], 'v_w_in': out['v_w_in'], 'v_w_gate_up': out['v_w_gate_up'], 'v_b_gate': out['v_b_gate'], 'v_gla_norm': out['v_gla_norm'], 'v_sgu_ln_g': out['v_sgu_ln_g'], 'v_sgu_ln_b': out['v_sgu_ln_b'], 'v_w_spatial': out['v_w_spatial'], 'v_b_spatial': out['v_b_spatial'], 'v_w_branch_gla': out['v_w_branch_gla'], 'v_w_branch_sgu': out['v_w_branch_sgu'], 'v_w_out': out['v_w_out'], 'v_norm_post_mix': out['v_norm_post_mix'], 'v_norm_pre_ffn': out['v_norm_pre_ffn'], 'v_w_ffn_in': out['v_w_ffn_in'], 'v_w_ffn_out': out['v_w_ffn_out'], 'v_norm_post_ffn': out['v_norm_post_ffn']}


def _loss(weights, diff, rest, loss_target):
    with _jax.named_scope("forward"):
        args = {**rest, TWIN_DIFF_INPUT: diff, **{k: w.astype(_WEIGHT_DTYPES[k]) for k, w in weights.items()}}
        y = _forward(args)
    with _jax.named_scope("loss_head"):
        err = _jnp.square(y.astype(_jnp.float32) - loss_target)
        return 0.5 * _jnp.sum(_jnp.mean(err, axis=-1)) if err.ndim else 0.5 * err


def _adamw(w, g, m, v):
    m = ADAM_B1 * m + (1.0 - ADAM_B1) * g
    v = ADAM_B2 * v + (1.0 - ADAM_B2) * _jnp.square(g)
    m_hat = m / (1.0 - ADAM_B1 ** ADAM_STEP)
    v_hat = v / (1.0 - ADAM_B2 ** ADAM_STEP)
    delta = -ADAM_LR * (m_hat / (_jnp.sqrt(v_hat) + ADAM_EPS) + ADAM_WD * w)
    return delta, m, v


def reference(x, norm_pre_mix, w_in, w_gate_up, b_gate, gla_norm, sgu_ln_g, sgu_ln_b, w_spatial, b_spatial, w_branch_gla, w_branch_sgu, w_out, norm_post_mix, norm_pre_ffn, w_ffn_in, w_ffn_out, norm_post_ffn, loss_target, m_norm_pre_mix, m_w_in, m_w_gate_up, m_b_gate, m_gla_norm, m_sgu_ln_g, m_sgu_ln_b, m_w_spatial, m_b_spatial, m_w_branch_gla, m_w_branch_sgu, m_w_out, m_norm_post_mix, m_norm_pre_ffn, m_w_ffn_in, m_w_ffn_out, m_norm_post_ffn, v_norm_pre_mix, v_w_in, v_w_gate_up, v_b_gate, v_gla_norm, v_sgu_ln_g, v_sgu_ln_b, v_w_spatial, v_b_spatial, v_w_branch_gla, v_w_branch_sgu, v_w_out, v_norm_post_mix, v_norm_pre_ffn, v_w_ffn_in, v_w_ffn_out, v_norm_post_ffn):
    given = dict(x=x, norm_pre_mix=norm_pre_mix, w_in=w_in, w_gate_up=w_gate_up, b_gate=b_gate, gla_norm=gla_norm, sgu_ln_g=sgu_ln_g, sgu_ln_b=sgu_ln_b, w_spatial=w_spatial, b_spatial=b_spatial, w_branch_gla=w_branch_gla, w_branch_sgu=w_branch_sgu, w_out=w_out, norm_post_mix=norm_post_mix, norm_pre_ffn=norm_pre_ffn, w_ffn_in=w_ffn_in, w_ffn_out=w_ffn_out, norm_post_ffn=norm_post_ffn, loss_target=loss_target, m_norm_pre_mix=m_norm_pre_mix, m_w_in=m_w_in, m_w_gate_up=m_w_gate_up, m_b_gate=m_b_gate, m_gla_norm=m_gla_norm, m_sgu_ln_g=m_sgu_ln_g, m_sgu_ln_b=m_sgu_ln_b, m_w_spatial=m_w_spatial, m_b_spatial=m_b_spatial, m_w_branch_gla=m_w_branch_gla, m_w_branch_sgu=m_w_branch_sgu, m_w_out=m_w_out, m_norm_post_mix=m_norm_post_mix, m_norm_pre_ffn=m_norm_pre_ffn, m_w_ffn_in=m_w_ffn_in, m_w_ffn_out=m_w_ffn_out, m_norm_post_ffn=m_norm_post_ffn, v_norm_pre_mix=v_norm_pre_mix, v_w_in=v_w_in, v_w_gate_up=v_w_gate_up, v_b_gate=v_b_gate, v_gla_norm=v_gla_norm, v_sgu_ln_g=v_sgu_ln_g, v_sgu_ln_b=v_sgu_ln_b, v_w_spatial=v_w_spatial, v_b_spatial=v_b_spatial, v_w_branch_gla=v_w_branch_gla, v_w_branch_sgu=v_w_branch_sgu, v_w_out=v_w_out, v_norm_post_mix=v_norm_post_mix, v_norm_pre_ffn=v_norm_pre_ffn, v_w_ffn_in=v_w_ffn_in, v_w_ffn_out=v_w_ffn_out, v_norm_post_ffn=v_norm_post_ffn)
    weights = {n: given[n] for n in TWIN_WEIGHTS}
    shared = {n: given[n] for n in SHARED_INPUTS}
    per_example = {n: given[n] for n in ['x']}
    grad_fn = _jax.value_and_grad(_loss, argnums=(0, 1))

    def one_microbatch(ex, loss_target):
        ex = dict(ex)
        diff = ex.pop(TWIN_DIFF_INPUT)
        return grad_fn(weights, diff, {**shared, **ex}, loss_target)

    if N_MICROBATCH == 1:
        loss, (grad_w, grad_x) = one_microbatch(per_example, given["loss_target"])
    else:
        def body(carry, xs):
            loss_sum, grad_sum = carry
            l_k, (gw_k, gx_k) = one_microbatch(xs[0], xs[1])
            with _jax.named_scope("update"):
                return (loss_sum + l_k, _jax.tree.map(_jnp.add, grad_sum, gw_k)), gx_k

        init = (_jnp.zeros((), _jnp.float32), _jax.tree.map(_jnp.zeros_like, weights))
        (loss, grad_w), grad_x = _jax.lax.scan(body, init, (per_example, given["loss_target"]))
    with _jax.named_scope("update"):
        delta_w, new_m, new_v = {}, {}, {}
        for n in TWIN_WEIGHTS:
            delta_w[n], new_m[n], new_v[n] = _adamw(weights[n], grad_w[n], given["m_" + n], given["v_" + n])
    return (loss, grad_x, *[grad_w[n] for n in TWIN_WEIGHTS], *[delta_w[n] for n in TWIN_WEIGHTS],
            *[new_m[n] for n in TWIN_WEIGHTS], *[new_v[n] for n in TWIN_WEIGHTS])
```

```python
import functools

import jax
import jax.numpy as jnp
from jax import lax
from jax.experimental import pallas as pl
from jax.experimental.pallas import tpu as pltpu

F32 = jnp.float32
BF16 = jnp.bfloat16

D_MODEL = 1024
GLA_HEADS = 4
GLA_DK = 128
GLA_DV = 256
GLA_QK = GLA_HEADS * GLA_DK
GLA_V = GLA_HEADS * GLA_DV
GLA_RANK = 16
GLA_TAU = 16.0
CHUNK = 64
SGU_GROUPS = 4
SGU_BLOCK = 128
SGU_DG = 256
D_FF = 2816
EPS = 1e-6
LANES = 128

OFF_Q, OFF_K, OFF_V, OFF_R, OFF_SU, OFF_SV, OFF_GG, OFF_GS, OFF_AL = 0, 512, 1024, 2048, 3072, 4096, 5120, 6144, 7168
N_MAIN = 7168
N_ALL = N_MAIN + LANES

ADAM_LR = 0.001
ADAM_B1 = 0.9
ADAM_B2 = 0.999
ADAM_EPS = 1e-08
ADAM_WD = 0.01
ADAM_STEP = 10

VMEM_LIMIT_BYTES = 56 * 1024 * 1024
N_CHIPS = 4
N_DEV = 8
MESH = pl.DeviceIdType.MESH

_NN = (((1,), (0,)), ((), ()))
_NT = (((1,), (1,)), ((), ()))
_TN = (((0,), (0,)), ((), ()))


def _dot(a, b, dims=_NN):
    return lax.dot_general(a, b, dims, preferred_element_type=F32)


def _split(x):
    hi = x.astype(BF16)
    lo = (x - hi.astype(F32)).astype(BF16)
    return hi, lo


def _dot_f32(a, b, dims=_NN):
    ah, al = _split(a)
    bh, bl = _split(b)
    return _dot(ah, bh, dims) + (_dot(al, bh, dims) + _dot(ah, bl, dims))


def _dot_exact_lhs(m, x):
    xh, xl = _split(x)
    return _dot(m, xh) + _dot(m, xl)


def _sigmoid(x):
    return 1.0 / (1.0 + jnp.exp(-x))


def _log_sigmoid(x):
    return jnp.minimum(x, 0.0) - jnp.log(1.0 + jnp.exp(-jnp.abs(x)))


_GELU_C = 0.7978845608028654
_GELU_A = 0.044715


def _gelu_and_grad(x):
    x2 = x * x
    t = jnp.tanh(_GELU_C * (x + _GELU_A * x * x2))
    g = 0.5 * x * (1.0 + t)
    dg = 0.5 * (1.0 + t) + 0.5 * x * (1.0 - t * t) * (_GELU_C * (1.0 + 3.0 * _GELU_A * x2))
    return g, dg


def _gelu(x):
    t = jnp.tanh(_GELU_C * (x + _GELU_A * x * x * x))
    return 0.5 * x * (1.0 + t)


def _rms_stats(x):
    return lax.rsqrt(jnp.mean(x * x, axis=-1, keepdims=True) + EPS)


def _rms_bwd(dout, y, r, g):
    yhat = y * r
    dn = dout * g
    dy = r * (dn - yhat * jnp.mean(dn * yhat, axis=-1, keepdims=True))
    return dy, dout * yhat


def _params(*sem):
    return pltpu.CompilerParams(dimension_semantics=sem, vmem_limit_bytes=VMEM_LIMIT_BYTES)


def _whole():
    return pl.BlockSpec(memory_space=pltpu.VMEM)


def _row_tile(T, want):
    t = min(T, want)
    assert T % t == 0
    return t


def _chunk_masks(tT, upper):
    row = lax.broadcasted_iota(jnp.int32, (tT, tT), 0)
    col = lax.broadcasted_iota(jnp.int32, (tT, tT), 1)
    same = lax.shift_right_logical(row, 6) == lax.shift_right_logical(col, 6)
    tri = (col > row) if upper else (col < row)
    return jnp.where(same & tri, 1.0, 0.0).astype(BF16)


def _inproj_fwd(x, g1, w_all):
    T = x.shape[0]
    tT = _row_tile(T, 512)

    def body(x_ref, g_ref, w_ref, a_ref, proj_ref, alow_ref):
        xv = x_ref[...]
        a = (xv * _rms_stats(xv) * g_ref[...]).astype(BF16)
        a_ref[...] = a
        for j in range(N_MAIN // 1024):
            cols = slice(j * 1024, (j + 1) * 1024)
            proj_ref[:, cols] = _dot(a, w_ref[:, cols]).astype(BF16)
        alow_ref[...] = _dot(a, w_ref[:, N_MAIN:N_ALL])

    return pl.pallas_call(
        body, name="inproj_fwd", grid=(T // tT,),
        in_specs=[pl.BlockSpec((tT, D_MODEL), lambda i: (i, 0)), pl.BlockSpec((1, D_MODEL), lambda i: (0, 0)), _whole()],
        out_specs=[pl.BlockSpec((tT, D_MODEL), lambda i: (i, 0)), pl.BlockSpec((tT, N_MAIN), lambda i: (i, 0)),
                   pl.BlockSpec((tT, LANES), lambda i: (i, 0))],
        out_shape=[jax.ShapeDtypeStruct((T, D_MODEL), BF16), jax.ShapeDtypeStruct((T, N_MAIN), BF16),
                   jax.ShapeDtypeStruct((T, LANES), F32)],
        compiler_params=_params("parallel"),
    )(x, g1, w_all)


def _gla_decay_terms(al_ref, wgu_ref, bg_ref, tT):
    logit = _dot_f32(al_ref[...], wgu_ref[...]) + bg_ref[...]
    la = _log_sigmoid(logit) * (1.0 / GLA_TAU)
    delta = _dot_exact_lhs(_chunk_masks(tT, upper=True), la)
    return logit, la, delta


def _gla_fwd(proj, alow, wgu, b_gate, gn):
    T = proj.shape[0]
    tT = _row_tile(T, 512)
    nc = tT // CHUNK

    def body(q_ref, k_ref, v_ref, r_ref, al_ref, wgu_ref, bg_ref, gn_ref, y_ref, st_ref, s_scr):
        @pl.when(pl.program_id(0) == 0)
        def _():
            s_scr[...] = jnp.zeros_like(s_scr)

        _, la, delta = _gla_decay_terms(al_ref, wgu_ref, bg_ref, tT)
        kdec = (k_ref[...].astype(F32) * jnp.exp(delta)).astype(BF16)
        for c in range(nc):
            rows = slice(c * CHUNK, (c + 1) * CHUNK)
            first = slice(c * CHUNK, c * CHUNK + 1)
            dec = jnp.exp(la[first, :] + delta[first, :])
            for h in range(GLA_HEADS):
                kc = slice(h * GLA_DK, (h + 1) * GLA_DK)
                vc = slice(h * GLA_DV, (h + 1) * GLA_DV)
                upd_t = _dot(v_ref[rows, vc], kdec[rows, kc], _TN)
                s_t = s_scr[h] * dec[:, kc] + upd_t
                s_scr[h] = s_t
                st_ref[c, h] = s_t
                qs = (q_ref[rows, kc].astype(F32) * (GLA_DK ** -0.5)).astype(BF16)
                o = _dot(qs, s_t.astype(BF16), _NT)
                on = o * _rms_stats(o) * gn_ref[:, vc]
                rr = r_ref[rows, vc].astype(F32)
                y_ref[rows, vc] = (on * (rr * _sigmoid(rr))).astype(BF16)

    blk = lambda w, j: pl.BlockSpec((tT, w), lambda i: (i, j))
    return pl.pallas_call(
        body, name="gla_fwd", grid=(T // tT,),
        in_specs=[blk(512, 0), blk(512, 1), blk(1024, 1), blk(1024, 2), blk(LANES, 0), _whole(), _whole(), _whole()],
        out_specs=[pl.BlockSpec((tT, GLA_V), lambda i: (i, 0)),
                   pl.BlockSpec((nc, GLA_HEADS, GLA_DV, GLA_DK), lambda i: (i, 0, 0, 0))],
        out_shape=[jax.ShapeDtypeStruct((T, GLA_V), BF16),
                   jax.ShapeDtypeStruct((T // CHUNK, GLA_HEADS, GLA_DV, GLA_DK), F32)],
        scratch_shapes=[pltpu.VMEM((GLA_HEADS, GLA_DV, GLA_DK), F32)],
        compiler_params=_params("arbitrary"),
    )(proj, proj, proj, proj, alow, wgu, b_gate, gn)


def _sgu_mask():
    i = lax.broadcasted_iota(jnp.int32, (SGU_BLOCK, SGU_BLOCK), 0)
    j = lax.broadcasted_iota(jnp.int32, (SGU_BLOCK, SGU_BLOCK), 1)
    return lax.shift_right_logical(j, 6) <= lax.shift_right_logical(i, 6)


def _sgu_fwd(proj, ln_g, ln_b, w_sp, b_sp_t):
    T = proj.shape[0]
    tT = _row_tile(T, 512)
    nb = tT // SGU_BLOCK

    def body(su_ref, sv_ref, lg_ref, lb_ref, w_ref, b_ref, y_ref):
        mask = _sgu_mask()
        for g in range(SGU_GROUPS):
            gc = slice(g * SGU_DG, (g + 1) * SGU_DG)
            wm = jnp.where(mask, w_ref[g], 0.0).astype(BF16)
            vf = _gelu(sv_ref[:, gc].astype(F32))
            mu = jnp.mean(vf, axis=-1, keepdims=True)
            vc = vf - mu
            rstd = lax.rsqrt(jnp.mean(vc * vc, axis=-1, keepdims=True) + EPS)
            vn = (vc * rstd * lg_ref[:, gc] + lb_ref[:, gc]).astype(BF16)
            u = _gelu(su_ref[:, gc].astype(F32))
            for b in range(nb):
                rows = slice(b * SGU_BLOCK, (b + 1) * SGU_BLOCK)
                mixed = _dot(wm, vn[rows, :]) + b_ref[:, g:g + 1]
                y_ref[rows, gc] = (u[rows, :] * mixed).astype(BF16)

    blk = lambda j: pl.BlockSpec((tT, 1024), lambda i: (i, j))
    return pl.pallas_call(
        body, name="sgu_fwd", grid=(T // tT,),
        in_specs=[blk(3), blk(4), _whole(), _whole(), _whole(), _whole()],
        out_specs=pl.BlockSpec((tT, 1024), lambda i: (i, 0)),
        out_shape=jax.ShapeDtypeStruct((T, 1024), BF16),
        compiler_params=_params("parallel"),
    )(proj, proj, ln_g, ln_b, w_sp, b_sp_t)


def _merge_fwd(x, proj, y_gla, y_sgu, w_bg, w_bs, w_o, g_pm):
    T = x.shape[0]
    tT = _row_tile(T, 512)

    def body(x_ref, gg_ref, gs_ref, yg_ref, ys_ref, wbg_ref, wbs_ref, wo_ref, g_ref,
             zg_ref, zs_ref, mg_ref, mix_ref, x1_ref):
        zg = _dot(yg_ref[...], wbg_ref[...])
        zs = _dot(ys_ref[...], wbs_ref[...])
        zg_ref[...] = zg.astype(BF16)
        zs_ref[...] = zs.astype(BF16)
        merged = (_sigmoid(gg_ref[...].astype(F32)) * zg + _sigmoid(gs_ref[...].astype(F32)) * zs).astype(BF16)
        mg_ref[...] = merged
        mix = _dot(merged, wo_ref[...])
        mix_ref[...] = mix
        x1_ref[...] = x_ref[...] + mix * _rms_stats(mix) * g_ref[...]

    row = lambda dt: pl.BlockSpec((tT, D_MODEL), lambda i: (i, 0))
    blk = lambda j: pl.BlockSpec((tT, 1024), lambda i: (i, j))
    sds = lambda dt: jax.ShapeDtypeStruct((T, D_MODEL), dt)
    return pl.pallas_call(
        body, name="merge_fwd", grid=(T // tT,),
        in_specs=[row(F32), blk(5), blk(6), row(BF16), row(BF16), _whole(), _whole(), _whole(),
                  pl.BlockSpec((1, D_MODEL), lambda i: (0, 0))],
        out_specs=[row(BF16), row(BF16), row(BF16), row(F32), row(F32)],
        out_shape=[sds(BF16), sds(BF16), sds(BF16), sds(F32), sds(F32)],
        compiler_params=_params("parallel"),
    )(x, proj, proj, y_gla, y_sgu, w_bg, w_bs, w_o, g_pm)


def _ffn_fwd_bwd(x1, tgt, w_fi, w_fo, g_pf, g_po):
    T = x1.shape[0]
    tT = _row_tile(T, 256)
    half = D_FF // 2

    def body(x1_ref, t_ref, wfi_ref, wfo_ref, gpf_ref, gpo_ref,
             h_ref, f_ref, dgu_ref, dy_ref, dx1_ref, loss_ref, dgpf_ref, dgpo_ref, gu_scr):
        @pl.when(pl.program_id(0) == 0)
        def _():
            loss_ref[...] = jnp.zeros_like(loss_ref)
            dgpf_ref[...] = jnp.zeros_like(dgpf_ref)
            dgpo_ref[...] = jnp.zeros_like(dgpo_ref)

        x1v = x1_ref[...]
        r2 = _rms_stats(x1v)
        h = (x1v * r2 * gpf_ref[...]).astype(BF16)
        h_ref[...] = h
        y = jnp.zeros((tT, D_MODEL), F32)
        for j in range(2):
            gc = slice(j * half, (j + 1) * half)
            uc = slice(D_FF + j * half, D_FF + (j + 1) * half)
            gate = _dot(h, wfi_ref[:, gc])
            up = _dot(h, wfi_ref[:, uc])
            gu_scr[:, gc] = gate
            gu_scr[:, uc] = up
            f = (gate * _sigmoid(gate) * up).astype(BF16)
            f_ref[:, gc] = f
            y = y + _dot(f, wfo_ref[gc, :])
        r3 = _rms_stats(y)
        x2 = x1v + y * r3 * gpo_ref[...]
        err = x2 - t_ref[...]
        loss_ref[...] += jnp.sum(err * err) * (0.5 / D_MODEL)
        dx2 = err * (1.0 / D_MODEL)
        dy, dg = _rms_bwd(dx2, y, r3, gpo_ref[...])
        dgpo_ref[...] += jnp.sum(dg, axis=0, keepdims=True)
        dyb = dy.astype(BF16)
        dy_ref[...] = dyb
        dh = jnp.zeros((tT, D_MODEL), F32)
        for j in range(2):
            gc = slice(j * half, (j + 1) * half)
            uc = slice(D_FF + j * half, D_FF + (j + 1) * half)
            df = _dot(dyb, wfo_ref[gc, :], _NT)
            gate = gu_scr[:, gc]
            up = gu_scr[:, uc]
            sg = _sigmoid(gate)
            dgate = (df * up * (sg * (1.0 + gate * (1.0 - sg)))).astype(BF16)
            dup = (df * (gate * sg)).astype(BF16)
            dgu_ref[:, gc] = dgate
            dgu_ref[:, uc] = dup
            dh = dh + _dot(dgate, wfi_ref[:, gc], _NT) + _dot(dup, wfi_ref[:, uc], _NT)
        dx1n, dg2 = _rms_bwd(dh, x1v, r2, gpf_ref[...])
        dgpf_ref[...] += jnp.sum(dg2, axis=0, keepdims=True)
        dx1_ref[...] = dx2 + dx1n

    row = lambda w: pl.BlockSpec((tT, w), lambda i: (i, 0))
    vec = pl.BlockSpec((1, D_MODEL), lambda i: (0, 0))
    return pl.pallas_call(
        body, name="ffn_fwd_bwd", grid=(T // tT,),
        in_specs=[row(D_MODEL), row(D_MODEL), _whole(), _whole(), vec, vec],
        out_specs=[row(D_MODEL), row(D_FF), row(2 * D_FF), row(D_MODEL), row(D_MODEL),
                   pl.BlockSpec((1, LANES), lambda i: (0, 0)), vec, vec],
        out_shape=[jax.ShapeDtypeStruct((T, D_MODEL), BF16), jax.ShapeDtypeStruct((T, D_FF), BF16),
                   jax.ShapeDtypeStruct((T, 2 * D_FF), BF16), jax.ShapeDtypeStruct((T, D_MODEL), BF16),
                   jax.ShapeDtypeStruct((T, D_MODEL), F32), jax.ShapeDtypeStruct((1, LANES), F32),
                   jax.ShapeDtypeStruct((1, D_MODEL), F32), jax.ShapeDtypeStruct((1, D_MODEL), F32)],
        scratch_shapes=[pltpu.VMEM((tT, 2 * D_FF), F32)],
        compiler_params=_params("arbitrary"),
    )(x1, tgt, w_fi, w_fo, g_pf, g_po)


def _merge_bwd(dx1, mix, proj, zg, zs, w_bg, w_bs, w_o, g_pm):
    T = dx1.shape[0]
    tT = _row_tile(T, 512)

    def body(dx1_ref, mix_ref, gg_ref, gs_ref, zg_ref, zs_ref, wbg_ref, wbs_ref, wo_ref, g_ref,
             dmix_ref, dzg_ref, dzs_ref, dgg_ref, dgs_ref, dyg_ref, dys_ref, dgpm_ref):
        @pl.when(pl.program_id(0) == 0)
        def _():
            dgpm_ref[...] = jnp.zeros_like(dgpm_ref)

        mix = mix_ref[...]
        dmix, dg = _rms_bwd(dx1_ref[...], mix, _rms_stats(mix), g_ref[...])
        dgpm_ref[...] += jnp.sum(dg, axis=0, keepdims=True)
        dmb = dmix.astype(BF16)
        dmix_ref[...] = dmb
        dmerged = _dot(dmb, wo_ref[...], _NT)
        for gate_ref, z_ref, w_ref, dz_ref, dgate_ref, dy_ref in (
                (gg_ref, zg_ref, wbg_ref, dzg_ref, dgg_ref, dyg_ref),
                (gs_ref, zs_ref, wbs_ref, dzs_ref, dgs_ref, dys_ref)):
            sg = _sigmoid(gate_ref[...].astype(F32))
            dz = (dmerged * sg).astype(BF16)
            dz_ref[...] = dz
            dgate_ref[...] = (dmerged * z_ref[...].astype(F32) * (sg * (1.0 - sg))).astype(BF16)
            dy_ref[...] = _dot(dz, w_ref[...], _NT).astype(BF16)

    row = pl.BlockSpec((tT, D_MODEL), lambda i: (i, 0))
    blk = lambda j: pl.BlockSpec((tT, 1024), lambda i: (i, j))
    vec = pl.BlockSpec((1, D_MODEL), lambda i: (0, 0))
    sds = jax.ShapeDtypeStruct((T, D_MODEL), BF16)
    return pl.pallas_call(
        body, name="merge_bwd", grid=(T // tT,),
        in_specs=[row, row, blk(5), blk(6), row, row, _whole(), _whole(), _whole(), vec],
        out_specs=[row] * 7 + [vec],
        out_shape=[sds] * 7 + [jax.ShapeDtypeStruct((1, D_MODEL), F32)],
        compiler_params=_params("arbitrary"),
    )(dx1, mix, proj, proj, zg, zs, w_bg, w_bs, w_o, g_pm)


def _sgu_bwd(proj, dy_sgu, ln_g, ln_b, w_sp, b_sp_t):
    T = proj.shape[0]
    tT = _row_tile(T, 512)
    nb = tT // SGU_BLOCK

    def body(su_ref, sv_ref, dy_ref, lg_ref, lb_ref, w_ref, b_ref,
             dsu_ref, dsv_ref, dw_ref, dbt_ref, dlg_ref, dlb_ref):
        step = pl.program_id(0)

        @pl.when(step == 0)
        def _():
            dw_ref[...] = jnp.zeros_like(dw_ref)
            dbt_ref[...] = jnp.zeros_like(dbt_ref)
            dlg_ref[...] = jnp.zeros_like(dlg_ref)
            dlb_ref[...] = jnp.zeros_like(dlb_ref)

        mask = _sgu_mask()
        lane = lax.broadcasted_iota(jnp.int32, (SGU_BLOCK, LANES), 1)
        for g in range(SGU_GROUPS):
            gc = slice(g * SGU_DG, (g + 1) * SGU_DG)
            wm = jnp.where(mask, w_ref[g], 0.0).astype(BF16)
            vf, dvf_dsv = _gelu_and_grad(sv_ref[:, gc].astype(F32))
            mu = jnp.mean(vf, axis=-1, keepdims=True)
            vc = vf - mu
            rstd = lax.rsqrt(jnp.mean(vc * vc, axis=-1, keepdims=True) + EPS)
            vhat = vc * rstd
            vn = (vhat * lg_ref[:, gc] + lb_ref[:, gc]).astype(BF16)
            u, du_dsu = _gelu_and_grad(su_ref[:, gc].astype(F32))
            dy = dy_ref[:, gc].astype(F32)
            dmixed = (dy * u).astype(BF16)
            dvn_parts = []
            dw_acc = jnp.zeros((SGU_BLOCK, SGU_BLOCK), F32)
            db_acc = jnp.zeros((SGU_BLOCK, 1), F32)
            for b in range(nb):
                rows = slice(b * SGU_BLOCK, (b + 1) * SGU_BLOCK)
                mixed = _dot(wm, vn[rows, :]) + b_ref[:, g:g + 1]
                dsu_ref[rows, gc] = (dy[rows, :] * mixed * du_dsu[rows, :]).astype(BF16)
                dvn_parts.append(_dot(wm, dmixed[rows, :], _TN))
                dw_acc = dw_acc + _dot(dmixed[rows, :], vn[rows, :], _NT)
                db_acc = db_acc + jnp.sum(dmixed[rows, :].astype(F32), axis=-1, keepdims=True)
            dw_ref[g] += jnp.where(mask, dw_acc, 0.0)
            dbt_ref[...] += jnp.where(lane == g, db_acc, 0.0)
            dvn = jnp.concatenate(dvn_parts, axis=0)
            dlg_ref[:, gc] += jnp.sum(dvn * vhat, axis=0, keepdims=True)
            dlb_ref[:, gc] += jnp.sum(dvn, axis=0, keepdims=True)
            dvh = dvn * lg_ref[:, gc]
            dvf = rstd * (dvh - jnp.mean(dvh, axis=-1, keepdims=True)
                          - vhat * jnp.mean(dvh * vhat, axis=-1, keepdims=True))
            dsv_ref[:, gc] = (dvf * dvf_dsv).astype(BF16)

    blk = lambda j: pl.BlockSpec((tT, 1024), lambda i: (i, j))
    row = pl.BlockSpec((tT, 1024), lambda i: (i, 0))
    vec = pl.BlockSpec((1, 1024), lambda i: (0, 0))
    return pl.pallas_call(
        body, name="sgu_bwd", grid=(T // tT,),
        in_specs=[blk(3), blk(4), row, _whole(), _whole(), _whole(), _whole()],
        out_specs=[row, row, pl.BlockSpec((SGU_GROUPS, SGU_BLOCK, SGU_BLOCK), lambda i: (0, 0, 0)),
                   pl.BlockSpec((SGU_BLOCK, LANES), lambda i: (0, 0)), vec, vec],
        out_shape=[jax.ShapeDtypeStruct((T, 1024), BF16), jax.ShapeDtypeStruct((T, 1024), BF16),
                   jax.ShapeDtypeStruct((SGU_GROUPS, SGU_BLOCK, SGU_BLOCK), F32),
                   jax.ShapeDtypeStruct((SGU_BLOCK, LANES), F32),
                   jax.ShapeDtypeStruct((1, 1024), F32), jax.ShapeDtypeStruct((1, 1024), F32)],
        compiler_params=_params("arbitrary"),
    )(proj, proj, dy_sgu, ln_g, ln_b, w_sp, b_sp_t)


def _gla_bwd(proj, alow, wgu, b_gate, gn, states, dy_gla):
    T = proj.shape[0]
    tT = _row_tile(T, 512)
    nc = tT // CHUNK
    nt = T // tT

    def body(q_ref, k_ref, v_ref, r_ref, al_ref, wgu_ref, bg_ref, gn_ref, st_ref, sp_ref, dy_ref,
             dq_ref, dk_ref, dv_ref, dr_ref, dal_ref, dgn_ref, dbg_ref, dwgu_ref,
             g_scr, dd_scr, dt_scr):
        step = pl.program_id(0)

        @pl.when(step == 0)
        def _():
            g_scr[...] = jnp.zeros_like(g_scr)
            dgn_ref[...] = jnp.zeros_like(dgn_ref)
            dbg_ref[...] = jnp.zeros_like(dbg_ref)
            dwgu_ref[...] = jnp.zeros_like(dwgu_ref)

        has_prev = jnp.where(step == nt - 1, 0.0, 1.0)
        logit, la, delta = _gla_decay_terms(al_ref, wgu_ref, bg_ref, tT)
        e = jnp.exp(delta)
        kdec_f = k_ref[...].astype(F32) * e
        kdec = kdec_f.astype(BF16)
        for c in reversed(range(nc)):
            rows = slice(c * CHUNK, (c + 1) * CHUNK)
            first = slice(c * CHUNK, c * CHUNK + 1)
            dec = jnp.exp(la[first, :] + delta[first, :])
            for h in range(GLA_HEADS):
                kc = slice(h * GLA_DK, (h + 1) * GLA_DK)
                vc = slice(h * GLA_DV, (h + 1) * GLA_DV)
                s_t = st_ref[c, h]
                s_prev = st_ref[c - 1, h] if c > 0 else sp_ref[0, h] * has_prev
                qs = (q_ref[rows, kc].astype(F32) * (GLA_DK ** -0.5)).astype(BF16)
                s_b = s_t.astype(BF16)
                o = _dot(qs, s_b, _NT)
                rstd = _rms_stats(o)
                ohat = o * rstd
                gnh = gn_ref[:, vc]
                dy = dy_ref[rows, vc].astype(F32)
                rr = r_ref[rows, vc].astype(F32)
                sg = _sigmoid(rr)
                don = dy * (rr * sg)
                dr_ref[rows, vc] = (dy * (ohat * gnh) * (sg * (1.0 + rr * (1.0 - sg)))).astype(BF16)
                dgn_ref[:, vc] += jnp.sum(don * ohat, axis=0, keepdims=True)
                dn = don * gnh
                do = (rstd * (dn - ohat * jnp.mean(dn * ohat, axis=-1, keepdims=True))).astype(BF16)
                dq_ref[rows, kc] = (_dot(do, s_b) * (GLA_DK ** -0.5)).astype(BF16)
                g_t = _dot(do, qs, _TN) + g_scr[h]
                g_b = g_t.astype(BF16)
                dv_ref[rows, vc] = _dot(kdec[rows, kc], g_b, _NT).astype(BF16)
                dkdec = _dot(v_ref[rows, vc], g_b)
                ddec = jnp.sum(g_t * s_prev, axis=0, keepdims=True)
                dk_ref[rows, kc] = (dkdec * e[rows, kc]).astype(BF16)
                dd_scr[rows, kc] = dkdec * kdec_f[rows, kc]
                dt_scr[rows, kc] = jnp.broadcast_to(ddec * dec[:, kc], (CHUNK, GLA_DK))
                g_scr[h] = g_t * dec[:, kc]
        dla = _dot_exact_lhs(_chunk_masks(tT, upper=False), dd_scr[...]) + dt_scr[...]
        dlogit = dla * (1.0 / GLA_TAU) * _sigmoid(-logit)
        dbg_ref[...] += jnp.sum(dlogit, axis=0, keepdims=True)
        dwgu_ref[...] += _dot_f32(al_ref[...], dlogit, _TN)
        dal_ref[...] = _dot_f32(dlogit, wgu_ref[...], _NT).astype(BF16)

    rev = lambda i: nt - 1 - i
    blk = lambda w, j: pl.BlockSpec((tT, w), lambda i: (rev(i), j))
    st_blk = pl.BlockSpec((nc, GLA_HEADS, GLA_DV, GLA_DK), lambda i: (rev(i), 0, 0, 0))
    sp_blk = pl.BlockSpec((1, GLA_HEADS, GLA_DV, GLA_DK), lambda i: (jnp.maximum(rev(i) * nc - 1, 0), 0, 0, 0))
    return pl.pallas_call(
        body, name="gla_bwd", grid=(nt,),
        in_specs=[blk(512, 0), blk(512, 1), blk(1024, 1), blk(1024, 2), blk(LANES, 0), _whole(), _whole(), _whole(),
                  st_blk, sp_blk, blk(GLA_V, 0)],
        out_specs=[blk(512, 0), blk(512, 0), blk(1024, 0), blk(1024, 0), blk(LANES, 0),
                   pl.BlockSpec((1, GLA_V), lambda i: (0, 0)), pl.BlockSpec((1, GLA_QK), lambda i: (0, 0)),
                   pl.BlockSpec((LANES, GLA_QK), lambda i: (0, 0))],
        out_shape=[jax.ShapeDtypeStruct((T, 512), BF16), jax.ShapeDtypeStruct((T, 512), BF16),
                   jax.ShapeDtypeStruct((T, 1024), BF16), jax.ShapeDtypeStruct((T, 1024), BF16),
                   jax.ShapeDtypeStruct((T, LANES), BF16), jax.ShapeDtypeStruct((1, GLA_V), F32),
                   jax.ShapeDtypeStruct((1, GLA_QK), F32), jax.ShapeDtypeStruct((LANES, GLA_QK), F32)],
        scratch_shapes=[pltpu.VMEM((GLA_HEADS, GLA_DV, GLA_DK), F32), pltpu.VMEM((tT, GLA_QK), F32),
                        pltpu.VMEM((tT, GLA_QK), F32)],
        compiler_params=_params("arbitrary"),
    )(proj, proj, proj, proj, alow, wgu, b_gate, gn, states, states, dy_gla)


def _inproj_bwd(x, dx1, g1, w_all, dsegs):
    T = x.shape[0]
    tT = _row_tile(T, 512)
    offs = (OFF_Q, OFF_K, OFF_V, OFF_R, OFF_SU, OFF_SV, OFF_GG, OFF_GS, OFF_AL)

    def body(x_ref, dx1_ref, g_ref, w_ref, *rest):
        seg_refs, (dx_ref, dg_ref) = rest[:len(offs)], rest[len(offs):]

        @pl.when(pl.program_id(0) == 0)
        def _():
            dg_ref[...] = jnp.zeros_like(dg_ref)

        da = jnp.zeros((tT, D_MODEL), F32)
        for off, s_ref in zip(offs, seg_refs):
            da = da + _dot(s_ref[...], w_ref[:, off:off + s_ref.shape[1]], _NT)
        xv = x_ref[...]
        dx, dg = _rms_bwd(da, xv, _rms_stats(xv), g_ref[...])
        dg_ref[...] += jnp.sum(dg, axis=0, keepdims=True)
        dx_ref[...] = dx1_ref[...] + dx

    row = lambda w: pl.BlockSpec((tT, w), lambda i: (i, 0))
    vec = pl.BlockSpec((1, D_MODEL), lambda i: (0, 0))
    return pl.pallas_call(
        body, name="inproj_bwd", grid=(T // tT,),
        in_specs=[row(D_MODEL), row(D_MODEL), vec, _whole()] + [row(s.shape[1]) for s in dsegs],
        out_specs=[row(D_MODEL), vec],
        out_shape=[jax.ShapeDtypeStruct((T, D_MODEL), F32), jax.ShapeDtypeStruct((1, D_MODEL), F32)],
        compiler_params=_params("arbitrary"),
    )(x, dx1, g1, w_all, *dsegs)


def _tn_matmul(a, b, name):
    T, M = a.shape
    N = b.shape[1]
    tk = _row_tile(T, 1024)
    tm = M if M <= 1024 else 1408
    tn = N if N <= 1024 else 1408
    assert M % tm == 0 and N % tn == 0

    def body(a_ref, b_ref, o_ref):
        @pl.when(pl.program_id(2) == 0)
        def _():
            o_ref[...] = jnp.zeros_like(o_ref)

        o_ref[...] += _dot(a_ref[...], b_ref[...], _TN)

    return pl.pallas_call(
        body, name=name, grid=(M // tm, N // tn, T // tk),
        in_specs=[pl.BlockSpec((tk, tm), lambda i, j, k: (k, i)), pl.BlockSpec((tk, tn), lambda i, j, k: (k, j))],
        out_specs=pl.BlockSpec((tm, tn), lambda i, j, k: (i, j)),
        out_shape=jax.ShapeDtypeStruct((M, N), F32),
        compiler_params=_params("parallel", "parallel", "arbitrary"),
    )(a, b)


def _local_step(x, tgt, w_all, wgu, b_gate, gn, ln_g, ln_b, w_sp, b_sp_t, w_bg, w_bs, w_o, w_fi, w_fo,
                g1, g_pm, g_pf, g_po):
    a, proj, alow = _inproj_fwd(x, g1, w_all)
    y_gla, states = _gla_fwd(proj, alow, wgu, b_gate, gn)
    y_sgu = _sgu_fwd(proj, ln_g, ln_b, w_sp, b_sp_t)
    zg, zs, merged, mix, x1 = _merge_fwd(x, proj, y_gla, y_sgu, w_bg, w_bs, w_o, g_pm)
    h, f, dgu, dy, dx1, loss, d_gpf, d_gpo = _ffn_fwd_bwd(x1, tgt, w_fi, w_fo, g_pf, g_po)
    dmix, dzg, dzs, dgg, dgs, dyg, dys, d_gpm = _merge_bwd(dx1, mix, proj, zg, zs, w_bg, w_bs, w_o, g_pm)
    dsu, dsv, d_wsp, d_bsp_t, d_lng, d_lnb = _sgu_bwd(proj, dys, ln_g, ln_b, w_sp, b_sp_t)
    dq, dk, dv, dr, dal, d_gn, d_bg, d_wgu = _gla_bwd(proj, alow, wgu, b_gate, gn, states, dyg)
    dsegs = (dq, dk, dv, dr, dsu, dsv, dgg, dgs, dal)
    grad_x, d_g1 = _inproj_bwd(x, dx1, g1, w_all, dsegs)
    grads = dict(
        w_all=[_tn_matmul(a, s, "dw_in_%d" % i) for i, s in enumerate(dsegs)],
        w_bg=_tn_matmul(y_gla, dzg, "dw_branch_gla"), w_bs=_tn_matmul(y_sgu, dzs, "dw_branch_sgu"),
        w_o=_tn_matmul(merged, dmix, "dw_out"), w_fi=_tn_matmul(h, dgu, "dw_ffn_in"),
        w_fo=_tn_matmul(f, dy, "dw_ffn_out"),
        g1=d_g1, g_pm=d_gpm, g_pf=d_gpf, g_po=d_gpo, wgu=d_wgu, b_gate=d_bg, gn=d_gn, ln_g=d_lng, ln_b=d_lnb,
        w_sp=d_wsp, b_sp_t=d_bsp_t)
    return loss, grad_x, grads


def _mesh_pos():
    return lax.axis_index("x"), lax.axis_index("y"), lax.axis_index("c")


def _chip_exchange(arrs, scatter, name):
    n = len(arrs)
    n_peer = N_CHIPS - 1

    def body(*refs):
        ins, outs = refs[:n], refs[n:2 * n]
        send_sems, recv_sems, local_sems = refs[2 * n:]
        xi, yi, ci = _mesh_pos()
        me = 2 * xi + yi
        peers = [(1 - xi, yi), (xi, 1 - yi), (1 - xi, 1 - yi)]

        def src(k, dest_chip):
            return ins[k].at[dest_chip] if scatter else ins[k]

        def remote(k, j):
            px, py = peers[j]
            return pltpu.make_async_remote_copy(
                src_ref=src(k, 2 * px + py), dst_ref=outs[k].at[me],
                send_sem=send_sems.at[k * n_peer + j], recv_sem=recv_sems.at[k * n_peer + j],
                device_id=(px, py, ci), device_id_type=MESH)

        def arrival(k, j):
            px, py = peers[j]
            return pltpu.make_async_remote_copy(
                src_ref=src(k, me), dst_ref=outs[k].at[2 * px + py],
                send_sem=send_sems.at[k * n_peer + j], recv_sem=recv_sems.at[k * n_peer + j],
                device_id=(px, py, ci), device_id_type=MESH)

        own = [pltpu.make_async_copy(src(k, me), outs[k].at[me], local_sems.at[k]) for k in range(n)]
        sends = [remote(k, j) for k in range(n) for j in range(n_peer)]
        for cp in own + sends:
            cp.start()
        for k in range(n):
            for j in range(n_peer):
                arrival(k, j).wait_recv()
        for cp in sends:
            cp.wait_send()
        for cp in own:
            cp.wait()

    any_spec = pl.BlockSpec(memory_space=pl.ANY)
    out_shape = [jax.ShapeDtypeStruct(((N_CHIPS,) + a.shape[1:]) if scatter else ((N_CHIPS,) + a.shape), a.dtype)
                 for a in arrs]
    return pl.pallas_call(
        body, name=name, in_specs=[any_spec] * n, out_specs=[any_spec] * n, out_shape=out_shape,
        scratch_shapes=[pltpu.SemaphoreType.DMA((n * n_peer,)), pltpu.SemaphoreType.DMA((n * n_peer,)),
                        pltpu.SemaphoreType.DMA((n,))],
    )(*arrs)


def _core_swap(arrs, name):
    n = len(arrs)

    def body(*refs):
        ins, outs = refs[:n], refs[n:2 * n]
        send_sems, recv_sems = refs[2 * n:]
        xi, yi, ci = _mesh_pos()
        copies = [pltpu.make_async_remote_copy(
            src_ref=ins[k], dst_ref=outs[k], send_sem=send_sems.at[k], recv_sem=recv_sems.at[k],
            device_id=(xi, yi, 1 - ci), device_id_type=MESH) for k in range(n)]
        for cp in copies:
            cp.start()
        for cp in copies:
            cp.wait_recv()
        for cp in copies:
            cp.wait_send()

    any_spec = pl.BlockSpec(memory_space=pl.ANY)
    return pl.pallas_call(
        body, name=name, in_specs=[any_spec] * n, out_specs=[any_spec] * n,
        out_shape=[jax.ShapeDtypeStruct(a.shape, a.dtype) for a in arrs],
        scratch_shapes=[pltpu.SemaphoreType.DMA((n,)), pltpu.SemaphoreType.DMA((n,))],
    )(*arrs)


def _allreduce_small(pack):
    rows = pack.shape[0]
    n_peer = N_DEV - 1

    def body(pack_ref, out_ref, slots, send_sems, recv_sems):
        xi, yi, ci = _mesh_pos()
        me = 4 * xi + 2 * yi + ci
        slots[me] = pack_ref[...]

        def copy(r, slot):
            peer = (jnp.bitwise_xor(xi, (r >> 2) & 1), jnp.bitwise_xor(yi, (r >> 1) & 1), jnp.bitwise_xor(ci, r & 1))
            return peer, pltpu.make_async_remote_copy(
                src_ref=pack_ref, dst_ref=slots.at[slot(peer)], send_sem=send_sems.at[r - 1],
                recv_sem=recv_sems.at[r - 1], device_id=peer, device_id_type=MESH)

        sends = [copy(r, lambda peer: me)[1] for r in range(1, N_DEV)]
        for cp in sends:
            cp.start()
        for r in range(1, N_DEV):
            copy(r, lambda peer: 4 * peer[0] + 2 * peer[1] + peer[2])[1].wait_recv()
        for cp in sends:
            cp.wait_send()
        acc = slots[0]
        for d in range(1, N_DEV):
            acc = acc + slots[d]
        out_ref[...] = acc

    return pl.pallas_call(
        body, name="allreduce_small", in_specs=[_whole()], out_specs=_whole(),
        out_shape=jax.ShapeDtypeStruct((rows, LANES), F32),
        scratch_shapes=[pltpu.VMEM((N_DEV, rows, LANES), F32), pltpu.SemaphoreType.DMA((n_peer,)),
                        pltpu.SemaphoreType.DMA((n_peer,))],
        compiler_params=pltpu.CompilerParams(vmem_limit_bytes=VMEM_LIMIT_BYTES),
    )(pack)


def _update_row_tile(rows):
    for t in range(min(rows, 256), 7, -8):
        if rows % t == 0:
            return t
    return rows


def _sum_slots(slots, name):
    _, rows, cols = slots.shape
    tr = _update_row_tile(rows)

    def body(s_ref, o_ref):
        acc = s_ref[0].astype(F32)
        for j in range(1, N_CHIPS):
            acc = acc + s_ref[j].astype(F32)
        o_ref[...] = acc

    return pl.pallas_call(
        body, name=name, grid=(rows // tr,),
        in_specs=[pl.BlockSpec((N_CHIPS, tr, cols), lambda i: (0, i, 0))],
        out_specs=pl.BlockSpec((tr, cols), lambda i: (i, 0)),
        out_shape=jax.ShapeDtypeStruct((rows, cols), F32),
        compiler_params=_params("parallel"),
    )(slots)


def _adamw(w, m, v, g_parts, name):
    rows, cols = w.shape
    tr = _update_row_tile(rows)
    n_g = len(g_parts)
    bc1 = 1.0 - ADAM_B1 ** ADAM_STEP
    bc2 = 1.0 - ADAM_B2 ** ADAM_STEP

    def body(w_ref, m_ref, v_ref, *rest):
        g_refs, (g_out, d_out, m_out, v_out) = rest[:n_g], rest[n_g:]
        g = g_refs[0][...]
        for r in g_refs[1:]:
            g = g + r[...]
        m2 = ADAM_B1 * m_ref[...] + (1.0 - ADAM_B1) * g
        v2 = ADAM_B2 * v_ref[...] + (1.0 - ADAM_B2) * (g * g)
        g_out[...] = g
        m_out[...] = m2
        v_out[...] = v2
        d_out[...] = -ADAM_LR * ((m2 / bc1) / (jnp.sqrt(v2 / bc2) + ADAM_EPS) + ADAM_WD * w_ref[...])

    spec = pl.BlockSpec((tr, cols), lambda i: (i, 0))
    sds = jax.ShapeDtypeStruct((rows, cols), F32)
    return pl.pallas_call(
        body, name=name, grid=(rows // tr,), in_specs=[spec] * (3 + n_g), out_specs=[spec] * 4,
        out_shape=[sds] * 4, compiler_params=_params("parallel"),
    )(w, m, v, *g_parts)


_IN_SPLITS = (GLA_QK, GLA_QK, GLA_V, GLA_V, GLA_RANK, 1024, 1024, 1024, 1024)
_IN_STARTS = tuple(sum(_IN_SPLITS[:i]) for i in range(len(_IN_SPLITS) + 1))
D_IN = _IN_STARTS[-1]


def _relay_w_in(w_in):
    q, k, v, r, al, su, sv, gg, gs = [w_in[:, _IN_STARTS[i]:_IN_STARTS[i + 1]] for i in range(9)]
    al = jnp.pad(al, ((0, 0), (0, LANES - GLA_RANK)))
    return jnp.concatenate([q, k, v, r, su, sv, gg, gs, al], axis=1)


def _pad_rows(a, rows=8):
    return jnp.pad(a, ((0, rows - a.shape[0]), (0, LANES - a.shape[1])))


def kernel(x, norm_pre_mix, w_in, w_gate_up, b_gate, gla_norm, sgu_ln_g, sgu_ln_b, w_spatial, b_spatial, w_branch_gla, w_branch_sgu, w_out, norm_post_mix, norm_pre_ffn, w_ffn_in, w_ffn_out, norm_post_ffn, loss_target, m_norm_pre_mix, m_w_in, m_w_gate_up, m_b_gate, m_gla_norm, m_sgu_ln_g, m_sgu_ln_b, m_w_spatial, m_b_spatial, m_w_branch_gla, m_w_branch_sgu, m_w_out, m_norm_post_mix, m_norm_pre_ffn, m_w_ffn_in, m_w_ffn_out, m_norm_post_ffn, v_norm_pre_mix, v_w_in, v_w_gate_up, v_b_gate, v_gla_norm, v_sgu_ln_g, v_sgu_ln_b, v_w_spatial, v_b_spatial, v_w_branch_gla, v_w_branch_sgu, v_w_out, v_norm_post_mix, v_norm_pre_ffn, v_w_ffn_in, v_w_ffn_out, v_norm_post_ffn):
    chip = 2 * lax.axis_index("x") + lax.axis_index("y")
    T = x.shape[1]

    tiny = jnp.concatenate([w_gate_up[0], _pad_rows(gla_norm[0]), _pad_rows(sgu_ln_g[0]), _pad_rows(sgu_ln_b[0])], axis=0)
    big_shards = [w_in[0].astype(BF16), w_branch_gla[0].astype(BF16), w_branch_sgu[0].astype(BF16),
                  w_out[0].astype(BF16), w_ffn_in[0].astype(BF16), w_ffn_out[0].astype(BF16)]
    g_in, g_bg, g_bs, g_o, g_fi, g_fo, g_tiny = _chip_exchange(big_shards + [tiny], False, "gather_weights")
    cols = lambda a: a.transpose(1, 0, 2).reshape(a.shape[1], N_CHIPS * a.shape[2])
    rows = lambda a: a.reshape(N_CHIPS * a.shape[1], a.shape[2])
    w_all = _relay_w_in(cols(g_in))
    wgu = jnp.pad(cols(g_tiny[:, 0:16]), ((0, LANES - GLA_RANK), (0, 0)))
    gn = cols(g_tiny[:, 16:20, :64]).reshape(1, GLA_V)
    ln_g = cols(g_tiny[:, 24:28, :64]).reshape(1, 1024)
    ln_b = cols(g_tiny[:, 32:36, :64]).reshape(1, 1024)
    b_sp_t = jnp.pad(b_spatial[0].T, ((0, 0), (0, LANES - SGU_GROUPS)))

    loss, grad_x, g = _local_step(
        x[0], loss_target[0], w_all, wgu, b_gate, gn, ln_g, ln_b, w_spatial[0], b_sp_t,
        rows(g_bg), rows(g_bs), rows(g_o), cols(g_fi), rows(g_fo),
        norm_pre_mix, norm_post_mix, norm_pre_ffn, norm_post_ffn)

    dq, dk, dv, dr, dsu, dsv, dgg, dgs, dal = g["w_all"]
    d_w_in = jnp.concatenate([dq, dk, dv, dr, dal[:, :GLA_RANK], dsu, dsv, dgg, dgs], axis=1)
    by_cols = lambda a: a.reshape(a.shape[0], N_CHIPS, a.shape[1] // N_CHIPS).transpose(1, 0, 2).astype(BF16)
    by_rows = lambda a: a.reshape(N_CHIPS, a.shape[0] // N_CHIPS, a.shape[1]).astype(BF16)
    parts = [by_cols(d_w_in), by_rows(g["w_bg"]), by_rows(g["w_bs"]), by_rows(g["w_o"]), by_cols(g["w_fi"]),
             by_rows(g["w_fo"])]
    slots = _chip_exchange(parts, True, "scatter_grads")
    half = [_sum_slots(s, "sum_chips_%d" % i) for i, s in enumerate(slots)]
    other = _core_swap(half, "swap_cores")

    pack = jnp.concatenate([
        g["w_sp"].reshape(512, LANES), g["wgu"][:GLA_RANK].reshape(64, LANES),
        g["g1"].reshape(8, LANES), g["g_pm"].reshape(8, LANES), g["g_pf"].reshape(8, LANES), g["g_po"].reshape(8, LANES),
        _pad_rows(g["b_gate"].reshape(4, LANES)), _pad_rows(g["b_sp_t"][:, :SGU_GROUPS].T),
        g["gn"].reshape(8, LANES), g["ln_g"].reshape(8, LANES), g["ln_b"].reshape(8, LANES), _pad_rows(loss)], axis=0)
    tot = _allreduce_small(pack)
    loss_out = tot[648, 0]
    shard64 = lambda r0: _pad_rows(lax.dynamic_slice(tot[r0:r0 + 8].reshape(4, 256), (0, chip * 64), (4, 64)))
    g_small = jnp.concatenate([
        tot[0:512], lax.dynamic_slice(tot[512:576].reshape(16, 512), (0, chip * LANES), (16, LANES)),
        tot[576:608], tot[608:624], shard64(624), shard64(632), shard64(640)], axis=0)

    def small_pack(w_sp, wg, n1, n2, n3, n4, bg, bs, gl, lg, lb):
        return jnp.concatenate([
            w_sp[0].reshape(512, LANES), wg[0], n1.reshape(8, LANES), n2.reshape(8, LANES), n3.reshape(8, LANES),
            n4.reshape(8, LANES), _pad_rows(bg.reshape(4, LANES)), _pad_rows(bs[0]),
            _pad_rows(gl[0]), _pad_rows(lg[0]), _pad_rows(lb[0])], axis=0)

    sp_w = small_pack(w_spatial, w_gate_up, norm_pre_mix, norm_post_mix, norm_pre_ffn, norm_post_ffn, b_gate,
                      b_spatial, gla_norm, sgu_ln_g, sgu_ln_b)
    sp_m = small_pack(m_w_spatial, m_w_gate_up, m_norm_pre_mix, m_norm_post_mix, m_norm_pre_ffn, m_norm_post_ffn,
                      m_b_gate, m_b_spatial, m_gla_norm, m_sgu_ln_g, m_sgu_ln_b)
    sp_v = small_pack(v_w_spatial, v_w_gate_up, v_norm_pre_mix, v_norm_post_mix, v_norm_pre_ffn, v_norm_post_ffn,
                      v_b_gate, v_b_spatial, v_gla_norm, v_sgu_ln_g, v_sgu_ln_b)

    small = _adamw(sp_w, sp_m, sp_v, [g_small], "adamw_small")

    def unpack(p):
        return dict(
            w_spatial=p[0:512].reshape(1, 4, 128, 128), w_gate_up=p[512:528].reshape(1, 16, 128),
            norm_pre_mix=p[528:536].reshape(1, 1024), norm_post_mix=p[536:544].reshape(1, 1024),
            norm_pre_ffn=p[544:552].reshape(1, 1024), norm_post_ffn=p[552:560].reshape(1, 1024),
            b_gate=p[560:564].reshape(1, 512), b_spatial=p[568:572].reshape(1, 4, 128),
            gla_norm=p[576:580, :64].reshape(1, 4, 64), sgu_ln_g=p[584:588, :64].reshape(1, 4, 64),
            sgu_ln_b=p[592:596, :64].reshape(1, 4, 64))

    res = [unpack(p) for p in small]
    big = dict(
        w_in=(w_in, m_w_in, v_w_in, 0), w_branch_gla=(w_branch_gla, m_w_branch_gla, v_w_branch_gla, 1),
        w_branch_sgu=(w_branch_sgu, m_w_branch_sgu, v_w_branch_sgu, 2), w_out=(w_out, m_w_out, v_w_out, 3),
        w_ffn_in=(w_ffn_in, m_w_ffn_in, v_w_ffn_in, 4), w_ffn_out=(w_ffn_out, m_w_ffn_out, v_w_ffn_out, 5))
    for name, (w, m, v, i) in big.items():
        outs = _adamw(w[0], m[0], v[0], [half[i], other[i]], "adamw_" + name)
        for r, o in zip(res, outs):
            r[name] = o[None]

    order = ["norm_pre_mix", "w_in", "w_gate_up", "b_gate", "gla_norm", "sgu_ln_g", "sgu_ln_b", "w_spatial", "b_spatial",
             "w_branch_gla", "w_branch_sgu", "w_out", "norm_post_mix", "norm_pre_ffn", "w_ffn_in", "w_ffn_out",
             "norm_post_ffn"]
    out = [loss_out, grad_x[None]]
    for r in res:
        out.extend(r[n] for n in order)
    return tuple(out)
```

```python
import jax
import jax.numpy as jnp
from jax import lax
from jax.experimental import pallas as pl
from jax.experimental.pallas import tpu as pltpu

F32 = jnp.float32
BF16 = jnp.bfloat16

D_MODEL = 1024
GLA_HEADS = 4
GLA_DK = 128
GLA_DV = 256
GLA_QK = GLA_HEADS * GLA_DK
GLA_V = GLA_HEADS * GLA_DV
GLA_RANK = 16
GLA_TAU = 16.0
CHUNK = 64
SGU_GROUPS = 4
SGU_BLOCK = 128
SGU_DG = 256
D_FF = 2816
EPS = 1e-6
LANES = 128

OFF_Q, OFF_K, OFF_V, OFF_R, OFF_SU, OFF_SV, OFF_GG, OFF_GS, OFF_AL = 0, 512, 1024, 2048, 3072, 4096, 5120, 6144, 7168
W_GLA, W_SGU, W_MRG = 3072, 2048, 2048
N_MAIN = 7168
N_ALL = N_MAIN + LANES
_IN_SPLITS = (GLA_QK, GLA_QK, GLA_V, GLA_V, GLA_RANK, 1024, 1024, 1024, 1024)
_IN_STARTS = tuple(sum(_IN_SPLITS[:i]) for i in range(len(_IN_SPLITS) + 1))
_IN_DST = (OFF_Q, OFF_K, OFF_V, OFF_R, OFF_AL, OFF_SU, OFF_SV, OFF_GG, OFF_GS)
D_IN = _IN_STARTS[-1]

ADAM_LR = 0.001
ADAM_B1 = 0.9
ADAM_B2 = 0.999
ADAM_EPS = 1e-08
ADAM_WD = 0.01
ADAM_STEP = 10

VMEM_LIMIT_BYTES = 56 * 1024 * 1024
N_CHIPS = 4
N_PEER = N_CHIPS - 1
N_DEV = 8
MESH = pl.DeviceIdType.MESH

_NN = (((1,), (0,)), ((), ()))
_NT = (((1,), (1,)), ((), ()))
_TN = (((0,), (0,)), ((), ()))


def _dot(a, b, dims=_NN):
    return lax.dot_general(a, b, dims, preferred_element_type=F32)


def _split(x):
    hi = x.astype(BF16)
    lo = (x - hi.astype(F32)).astype(BF16)
    return hi, lo


def _dot_f32(a, b, dims=_NN):
    ah, al = _split(a)
    bh, bl = _split(b)
    return _dot(ah, bh, dims) + (_dot(al, bh, dims) + _dot(ah, bl, dims))


def _dot_exact_lhs(m, x):
    xh, xl = _split(x)
    return _dot(m, xh) + _dot(m, xl)


def _sigmoid(x):
    return 1.0 / (1.0 + jnp.exp(-x))


def _log_sigmoid(x):
    return jnp.minimum(x, 0.0) - jnp.log(1.0 + jnp.exp(-jnp.abs(x)))


_GELU_C = 0.7978845608028654
_GELU_A = 0.044715


def _gelu_and_grad(x):
    x2 = x * x
    t = jnp.tanh(_GELU_C * (x + _GELU_A * x * x2))
    g = 0.5 * x * (1.0 + t)
    dg = 0.5 * (1.0 + t) + 0.5 * x * (1.0 - t * t) * (_GELU_C * (1.0 + 3.0 * _GELU_A * x2))
    return g, dg


def _gelu(x):
    t = jnp.tanh(_GELU_C * (x + _GELU_A * x * x * x))
    return 0.5 * x * (1.0 + t)


def _rms_stats(x):
    return lax.rsqrt(jnp.mean(x * x, axis=-1, keepdims=True) + EPS)


def _rms_bwd(dout, y, r, g):
    yhat = y * r
    dn = dout * g
    dy = r * (dn - yhat * jnp.mean(dn * yhat, axis=-1, keepdims=True))
    return dy, dout * yhat


def _whole():
    return pl.BlockSpec(memory_space=pltpu.VMEM)


def _row_tile(T, want):
    t = min(T, want)
    assert T % t == 0
    return t


def _chunk_masks(tT, upper):
    row = lax.broadcasted_iota(jnp.int32, (tT, tT), 0)
    col = lax.broadcasted_iota(jnp.int32, (tT, tT), 1)
    same = lax.shift_right_logical(row, 6) == lax.shift_right_logical(col, 6)
    tri = (col > row) if upper else (col < row)
    return jnp.where(same & tri, 1.0, 0.0).astype(BF16)


class _Job:
    def __init__(self, ins, out_shapes, scratch, start, finish):
        self.ins, self.out_shapes, self.scratch, self.start, self.finish = list(ins), list(out_shapes), list(scratch), start, finish


def _join(*jobs):
    def split(refs, counts):
        out, at = [], 0
        for n in counts:
            out.append(refs[at:at + n])
            at += n
        return out

    ni, no, ns = [len(j.ins) for j in jobs], [len(j.out_shapes) for j in jobs], [len(j.scratch) for j in jobs]

    def start(ins, outs, scr):
        for j, a, b, c in zip(jobs, split(ins, ni), split(outs, no), split(scr, ns)):
            j.start(a, b, c)

    def finish(ins, outs, scr):
        for j, a, b, c in zip(jobs, split(ins, ni), split(outs, no), split(scr, ns)):
            j.finish(a, b, c)

    return _Job(sum((j.ins for j in jobs), []), sum((j.out_shapes for j in jobs), []),
                sum((j.scratch for j in jobs), []), start, finish)


def _mesh_pos():
    return lax.axis_index("x"), lax.axis_index("y"), lax.axis_index("c")


def _peer_chips(xi, yi):
    return [(1 - xi, yi), (xi, 1 - yi), (1 - xi, 1 - yi)]


def _half(ci, rows):
    return pl.ds(pl.multiple_of(ci * rows, 8), rows)


def _sds(shape, dtype):
    return jax.ShapeDtypeStruct(tuple(shape), dtype)


def _job_gather(arrs, by_cols):
    n = len(arrs)

    def dst(o, k, chip, rows):
        if by_cols[k]:
            c = arrs[k].shape[1]
            return o.at[rows, pl.ds(pl.multiple_of(chip * c, LANES), c)]
        return o.at[chip, rows]

    def copies(ins, outs, scr):
        ici_send, ici_recv, d2d_send, d2d_recv, local = scr
        xi, yi, ci = _mesh_pos()
        me = 2 * xi + yi
        res = []
        for k in range(n):
            r = arrs[k].shape[0]
            mine, other = _half(ci, r // 2), _half(1 - ci, r // 2)
            own = pltpu.make_async_copy(ins[k], dst(outs[k], k, me, pl.ds(0, r)), local.at[k])
            per_peer = []
            for j, (px, py) in enumerate(_peer_chips(xi, yi)):
                s = k * N_PEER + j
                pc = 2 * px + py
                send = pltpu.make_async_remote_copy(
                    src_ref=ins[k].at[mine], dst_ref=dst(outs[k], k, me, mine), send_sem=ici_send.at[s],
                    recv_sem=ici_recv.at[s], device_id=(px, py, ci), device_id_type=MESH)
                arrive = pltpu.make_async_remote_copy(
                    src_ref=ins[k].at[mine], dst_ref=dst(outs[k], k, pc, mine), send_sem=ici_send.at[s],
                    recv_sem=ici_recv.at[s], device_id=(px, py, ci), device_id_type=MESH)
                forward = pltpu.make_async_remote_copy(
                    src_ref=dst(outs[k], k, pc, mine), dst_ref=dst(outs[k], k, pc, mine), send_sem=d2d_send.at[s],
                    recv_sem=d2d_recv.at[s], device_id=(xi, yi, 1 - ci), device_id_type=MESH)
                handed = pltpu.make_async_remote_copy(
                    src_ref=dst(outs[k], k, pc, other), dst_ref=dst(outs[k], k, pc, other), send_sem=d2d_send.at[s],
                    recv_sem=d2d_recv.at[s], device_id=(xi, yi, 1 - ci), device_id_type=MESH)
                per_peer.append((send, arrive, forward, handed))
            res.append((own, per_peer))
        return res

    def start(ins, outs, scr):
        for own, per_peer in copies(ins, outs, scr):
            own.start()
            for send, _, _, _ in per_peer:
                send.start()

    def finish(ins, outs, scr):
        cps = copies(ins, outs, scr)
        for _, per_peer in cps:
            for _, arrive, forward, _ in per_peer:
                arrive.wait_recv()
                forward.start()
        for own, per_peer in cps:
            for send, _, forward, handed in per_peer:
                handed.wait_recv()
                send.wait_send()
                forward.wait_send()
            own.wait()

    shapes = [_sds((a.shape[0], N_CHIPS * a.shape[1]) if bc else (N_CHIPS,) + a.shape, a.dtype)
              for a, bc in zip(arrs, by_cols)]
    dma = pltpu.SemaphoreType.DMA
    return _Job(arrs, shapes, [dma((n * N_PEER,))] * 4 + [dma((n,))], start, finish)


def _job_scatter(parts):
    n = len(parts)

    def copies(ins, outs, scr):
        send_sems, recv_sems, local = scr
        xi, yi, ci = _mesh_pos()
        me = 2 * xi + yi
        res = []
        for k in range(n):
            own = pltpu.make_async_copy(ins[k].at[me], outs[k].at[me], local.at[k])
            per_peer = []
            for j, (px, py) in enumerate(_peer_chips(xi, yi)):
                s = k * N_PEER + j
                pc = 2 * px + py
                send = pltpu.make_async_remote_copy(
                    src_ref=ins[k].at[pc], dst_ref=outs[k].at[me], send_sem=send_sems.at[s], recv_sem=recv_sems.at[s],
                    device_id=(px, py, ci), device_id_type=MESH)
                arrive = pltpu.make_async_remote_copy(
                    src_ref=ins[k].at[pc], dst_ref=outs[k].at[pc], send_sem=send_sems.at[s], recv_sem=recv_sems.at[s],
                    device_id=(px, py, ci), device_id_type=MESH)
                per_peer.append((send, arrive))
            res.append((own, per_peer))
        return res

    def start(ins, outs, scr):
        for own, per_peer in copies(ins, outs, scr):
            own.start()
            for send, _ in per_peer:
                send.start()

    def finish(ins, outs, scr):
        for own, per_peer in copies(ins, outs, scr):
            for send, arrive in per_peer:
                arrive.wait_recv()
                send.wait_send()
            own.wait()

    dma = pltpu.SemaphoreType.DMA
    return _Job(parts, [_sds(p.shape, p.dtype) for p in parts], [dma((n * N_PEER,))] * 2 + [dma((n,))], start, finish)


def _job_split_cores(groups):
    pieces = [(g, a, off) for g, group in enumerate(groups) for a, off in group]
    n = len(pieces)

    def geometry(group):
        a0 = group[0][0]
        if a0.ndim == 4:
            return (N_CHIPS, a0.shape[2], a0.shape[3])
        return (a0.shape[0] // 2, sum(a.shape[1] for a, _ in group))

    def copies(ins, outs, scr):
        send_sems, recv_sems, local = scr
        xi, yi, ci = _mesh_pos()
        res = []
        for p, (g, a, off) in enumerate(pieces):
            mine_o, theirs_o = outs[2 * g], outs[2 * g + 1]
            if a.ndim == 4:
                keep, give = ins[p].at[pl.ds(0, N_CHIPS), ci], ins[p].at[pl.ds(0, N_CHIPS), 1 - ci]
                d_mine, d_theirs = mine_o, theirs_o
            else:
                hr, w = a.shape[0] // 2, a.shape[1]
                keep, give = ins[p].at[_half(ci, hr)], ins[p].at[_half(1 - ci, hr)]
                win = (pl.ds(0, hr), pl.ds(off, w))
                d_mine, d_theirs = mine_o.at[win], theirs_o.at[win]
            own = pltpu.make_async_copy(keep, d_mine, local.at[p])
            send = pltpu.make_async_remote_copy(
                src_ref=give, dst_ref=d_theirs, send_sem=send_sems.at[p], recv_sem=recv_sems.at[p],
                device_id=(xi, yi, 1 - ci), device_id_type=MESH)
            res.append((own, send))
        return res

    def start(ins, outs, scr):
        for own, send in copies(ins, outs, scr):
            own.start()
            send.start()

    def finish(ins, outs, scr):
        for own, send in copies(ins, outs, scr):
            send.wait_recv()
            send.wait_send()
            own.wait()

    dma = pltpu.SemaphoreType.DMA
    shapes = []
    for group in groups:
        shapes += [_sds(geometry(group), F32)] * 2
    return _Job([a for _, a, _ in pieces], shapes, [dma((n,))] * 3, start, finish)


def _job_join_cores(arrs):
    n = len(arrs)

    def copies(ins, outs, scr):
        send_sems, recv_sems, local = scr
        xi, yi, ci = _mesh_pos()
        res = []
        for k in range(n):
            own = pltpu.make_async_copy(ins[k], outs[k].at[ci], local.at[k])
            send = pltpu.make_async_remote_copy(
                src_ref=ins[k], dst_ref=outs[k].at[ci], send_sem=send_sems.at[k], recv_sem=recv_sems.at[k],
                device_id=(xi, yi, 1 - ci), device_id_type=MESH)
            arrive = pltpu.make_async_remote_copy(
                src_ref=ins[k], dst_ref=outs[k].at[1 - ci], send_sem=send_sems.at[k], recv_sem=recv_sems.at[k],
                device_id=(xi, yi, 1 - ci), device_id_type=MESH)
            res.append((own, send, arrive))
        return res

    def start(ins, outs, scr):
        for own, send, _ in copies(ins, outs, scr):
            own.start()
            send.start()

    def finish(ins, outs, scr):
        for own, send, arrive in copies(ins, outs, scr):
            arrive.wait_recv()
            send.wait_send()
            own.wait()

    dma = pltpu.SemaphoreType.DMA
    return _Job(arrs, [_sds((2,) + a.shape, a.dtype) for a in arrs], [dma((n,))] * 3, start, finish)


def _call(body, *, name, grid, in_specs, out_specs, out_shape, args, scratch_shapes=(), parallel=False, job=None):
    n_in, n_out, n_scr = len(in_specs), len(out_specs), len(scratch_shapes)
    if job is None:
        sem = ("parallel" if parallel else "arbitrary",) * len(grid)
        res = pl.pallas_call(
            body, name=name, grid=grid, in_specs=in_specs, out_specs=out_specs, out_shape=out_shape,
            scratch_shapes=list(scratch_shapes),
            compiler_params=pltpu.CompilerParams(dimension_semantics=sem, vmem_limit_bytes=VMEM_LIMIT_BYTES))(*args)
        return list(res), []
    n_ji, n_jo = len(job.ins), len(job.out_shapes)

    def carried(*refs):
        ins, refs = refs[:n_in], refs[n_in:]
        j_ins, refs = refs[:n_ji], refs[n_ji:]
        outs, refs = refs[:n_out], refs[n_out:]
        j_outs, refs = refs[:n_jo], refs[n_jo:]
        scr, j_scr = refs[:n_scr], refs[n_scr:]
        ids = [pl.program_id(d) for d in range(len(grid))]
        first = ids[0] == 0
        last = ids[0] == grid[0] - 1
        for d in range(1, len(grid)):
            first = first & (ids[d] == 0)
            last = last & (ids[d] == grid[d] - 1)

        @pl.when(first)
        def _():
            job.start(j_ins, j_outs, j_scr)

        body(*ins, *outs, *scr)

        @pl.when(last)
        def _():
            job.finish(j_ins, j_outs, j_scr)

    hbm = pl.BlockSpec(memory_space=pl.ANY)
    res = pl.pallas_call(
        carried, name=name, grid=grid, in_specs=list(in_specs) + [hbm] * n_ji, out_specs=list(out_specs) + [hbm] * n_jo,
        out_shape=list(out_shape) + job.out_shapes, scratch_shapes=list(scratch_shapes) + job.scratch,
        compiler_params=pltpu.CompilerParams(dimension_semantics=("arbitrary",) * len(grid),
                                             vmem_limit_bytes=VMEM_LIMIT_BYTES))(*args, *job.ins)
    return list(res[:n_out]), list(res[n_out:])


def _run_job(job, name):
    n_i, n_o = len(job.ins), len(job.out_shapes)

    def body(*refs):
        ins, outs, scr = refs[:n_i], refs[n_i:n_i + n_o], refs[n_i + n_o:]
        job.start(ins, outs, scr)
        job.finish(ins, outs, scr)

    hbm = pl.BlockSpec(memory_space=pl.ANY)
    return list(pl.pallas_call(body, name=name, in_specs=[hbm] * n_i, out_specs=[hbm] * n_o, out_shape=job.out_shapes,
                               scratch_shapes=job.scratch)(*job.ins))


def _allreduce_small(pack):
    rows = pack.shape[0]

    def body(pack_ref, out_ref, slots, send_sems, recv_sems):
        xi, yi, ci = _mesh_pos()
        me = 4 * xi + 2 * yi + ci
        slots[me] = pack_ref[...]

        def copy(r, slot):
            peer = (jnp.bitwise_xor(xi, (r >> 2) & 1), jnp.bitwise_xor(yi, (r >> 1) & 1), jnp.bitwise_xor(ci, r & 1))
            return pltpu.make_async_remote_copy(
                src_ref=pack_ref, dst_ref=slots.at[slot(peer)], send_sem=send_sems.at[r - 1],
                recv_sem=recv_sems.at[r - 1], device_id=peer, device_id_type=MESH)

        sends = [copy(r, lambda peer: me) for r in range(1, N_DEV)]
        for cp in sends:
            cp.start()
        for r in range(1, N_DEV):
            copy(r, lambda peer: 4 * peer[0] + 2 * peer[1] + peer[2]).wait_recv()
        for cp in sends:
            cp.wait_send()
        acc = slots[0]
        for d in range(1, N_DEV):
            acc = acc + slots[d]
        out_ref[...] = acc

    return pl.pallas_call(
        body, name="allreduce_small", in_specs=[_whole()], out_specs=_whole(),
        out_shape=_sds((rows, LANES), F32),
        scratch_shapes=[pltpu.VMEM((N_DEV, rows, LANES), F32), pltpu.SemaphoreType.DMA((N_DEV - 1,)),
                        pltpu.SemaphoreType.DMA((N_DEV - 1,))],
        compiler_params=pltpu.CompilerParams(vmem_limit_bytes=VMEM_LIMIT_BYTES),
    )(pack)


def _w_in_pieces():
    blk = D_IN // N_CHIPS
    pieces = []
    for s in range(len(_IN_SPLITS)):
        lo_s, hi_s = _IN_STARTS[s], _IN_STARTS[s + 1]
        for j in range(N_CHIPS):
            lo, hi = max(lo_s, j * blk), min(hi_s, (j + 1) * blk)
            if lo < hi:
                pieces.append((j, lo - j * blk, _IN_DST[s] + lo - lo_s, hi - lo))
    return pieces


def _relayout_w_in(gathered):
    _, rows, blk = gathered.shape
    tr = 256

    def body(g_ref, o_ref):
        o_ref[:, OFF_AL:N_ALL] = jnp.zeros((tr, LANES), BF16)
        for j, src, dst, w in _w_in_pieces():
            o_ref[:, dst:dst + w] = g_ref[j, :, src:src + w]

    res, _ = _call(body, name="relayout_w_in", grid=(rows // tr,), parallel=True,
                   in_specs=[pl.BlockSpec((N_CHIPS, tr, blk), lambda i: (0, i, 0))],
                   out_specs=[pl.BlockSpec((tr, N_ALL), lambda i: (i, 0))],
                   out_shape=[_sds((rows, N_ALL), BF16)], args=(gathered,))
    return res[0]


def _presum_w_in(mine, theirs):
    hr = mine.shape[0]
    blk = D_IN // N_CHIPS
    tr = 128

    def body(a_ref, b_ref, o_ref):
        s = (a_ref[...] + b_ref[...]).astype(BF16)
        for j, src, dst, w in _w_in_pieces():
            o_ref[j, :, src:src + w] = s[:, dst:dst + w]

    spec = pl.BlockSpec((tr, N_ALL), lambda i: (i, 0))
    res, _ = _call(body, name="presum_w_in", grid=(hr // tr,), parallel=True, in_specs=[spec, spec],
                   out_specs=[pl.BlockSpec((N_CHIPS, tr, blk), lambda i: (0, i, 0))],
                   out_shape=[_sds((N_CHIPS, hr, blk), BF16)], args=(mine, theirs))
    return res[0]


def _update_row_tile(rows):
    for t in range(min(rows, 256), 7, -8):
        if rows % t == 0:
            return t
    return rows


def _presum(mine, theirs, name):
    if mine.ndim == 3:
        _, hr, c = mine.shape
        tr = _update_row_tile(hr)
        spec = pl.BlockSpec((1, tr, c), lambda j, i: (j, i, 0))
    else:
        hr, c = mine.shape[0], mine.shape[1] // N_CHIPS
        tr = _update_row_tile(hr)
        spec = pl.BlockSpec((tr, c), lambda j, i: (i, j))

    def body(a_ref, b_ref, o_ref):
        o_ref[...] = (a_ref[...] + b_ref[...]).astype(BF16).reshape(o_ref.shape)

    res, _ = _call(body, name=name, grid=(N_CHIPS, hr // tr), parallel=True, in_specs=[spec, spec],
                   out_specs=[pl.BlockSpec((1, tr, c), lambda j, i: (j, i, 0))],
                   out_shape=[_sds((N_CHIPS, hr, c), BF16)], args=(mine, theirs))
    return res[0]


def _sum_slots(slots, name):
    _, rows, cols = slots.shape
    tr = _update_row_tile(rows)

    def body(s_ref, o_ref):
        acc = s_ref[0].astype(F32)
        for j in range(1, N_CHIPS):
            acc = acc + s_ref[j].astype(F32)
        o_ref[...] = acc

    res, _ = _call(body, name=name, grid=(rows // tr,), parallel=True,
                   in_specs=[pl.BlockSpec((N_CHIPS, tr, cols), lambda i: (0, i, 0))],
                   out_specs=[pl.BlockSpec((tr, cols), lambda i: (i, 0))], out_shape=[_sds((rows, cols), F32)],
                   args=(slots,))
    return res[0]


def _adamw(w, m, v, g, name, job=None):
    rows, cols = w.shape
    tr = _update_row_tile(rows)
    bc1 = 1.0 - ADAM_B1 ** ADAM_STEP
    bc2 = 1.0 - ADAM_B2 ** ADAM_STEP

    def body(w_ref, m_ref, v_ref, g_ref, d_out, m_out, v_out):
        g = g_ref[...]
        m2 = ADAM_B1 * m_ref[...] + (1.0 - ADAM_B1) * g
        v2 = ADAM_B2 * v_ref[...] + (1.0 - ADAM_B2) * (g * g)
        m_out[...] = m2
        v_out[...] = v2
        d_out[...] = -ADAM_LR * ((m2 / bc1) / (jnp.sqrt(v2 / bc2) + ADAM_EPS) + ADAM_WD * w_ref[...])

    spec = pl.BlockSpec((tr, cols), lambda i: (i, 0))
    return _call(body, name=name, grid=(rows // tr,), parallel=True, in_specs=[spec] * 4, out_specs=[spec] * 3,
                 out_shape=[_sds((rows, cols), F32)] * 3, args=(w, m, v, g), job=job)


def _inproj_fwd(x, g1, w_all, job=None):
    T = x.shape[0]
    tT = _row_tile(T, 512)

    def body(x_ref, g_ref, w_ref, a_ref, proj_ref, alow_ref):
        xv = x_ref[...]
        a = (xv * _rms_stats(xv) * g_ref[...]).astype(BF16)
        a_ref[...] = a
        for j in range(N_MAIN // 1024):
            cols = slice(j * 1024, (j + 1) * 1024)
            proj_ref[:, cols] = _dot(a, w_ref[:, cols]).astype(BF16)
        alow_ref[...] = _dot(a, w_ref[:, N_MAIN:N_ALL])

    row = lambda w: pl.BlockSpec((tT, w), lambda i: (i, 0))
    return _call(
        body, name="inproj_fwd", grid=(T // tT,), parallel=True,
        in_specs=[row(D_MODEL), pl.BlockSpec((1, D_MODEL), lambda i: (0, 0)), _whole()],
        out_specs=[row(D_MODEL), row(N_MAIN), row(LANES)],
        out_shape=[_sds((T, D_MODEL), BF16), _sds((T, N_MAIN), BF16), _sds((T, LANES), F32)],
        args=(x, g1, w_all), job=job)


def _gla_decay_terms(al_ref, wgu_ref, bg_ref, tT):
    logit = _dot_f32(al_ref[...], wgu_ref[...]) + bg_ref[...]
    la = _log_sigmoid(logit) * (1.0 / GLA_TAU)
    delta = _dot_exact_lhs(_chunk_masks(tT, upper=True), la)
    return logit, la, delta


def _gla_fwd(proj, alow, wgu, b_gate, gn, job=None):
    T = proj.shape[0]
    tT = _row_tile(T, 512)
    nc = tT // CHUNK

    def body(q_ref, k_ref, v_ref, r_ref, al_ref, wgu_ref, bg_ref, gn_ref, y_ref, st_ref, s_scr):
        @pl.when(pl.program_id(0) == 0)
        def _():
            s_scr[...] = jnp.zeros_like(s_scr)

        _, la, delta = _gla_decay_terms(al_ref, wgu_ref, bg_ref, tT)
        kdec = (k_ref[...].astype(F32) * jnp.exp(delta)).astype(BF16)
        for c in range(nc):
            rows = slice(c * CHUNK, (c + 1) * CHUNK)
            first = slice(c * CHUNK, c * CHUNK + 1)
            dec = jnp.exp(la[first, :] + delta[first, :])
            for h in range(GLA_HEADS):
                kc = slice(h * GLA_DK, (h + 1) * GLA_DK)
                vc = slice(h * GLA_DV, (h + 1) * GLA_DV)
                upd_t = _dot(v_ref[rows, vc], kdec[rows, kc], _TN)
                s_t = s_scr[h] * dec[:, kc] + upd_t
                s_scr[h] = s_t
                st_ref[c, h] = s_t
                qs = (q_ref[rows, kc].astype(F32) * (GLA_DK ** -0.5)).astype(BF16)
                o = _dot(qs, s_t.astype(BF16), _NT)
                on = o * _rms_stats(o) * gn_ref[:, vc]
                rr = r_ref[rows, vc].astype(F32)
                y_ref[rows, vc] = (on * (rr * _sigmoid(rr))).astype(BF16)

    blk = lambda w, j: pl.BlockSpec((tT, w), lambda i: (i, j))
    return _call(
        body, name="gla_fwd", grid=(T // tT,),
        in_specs=[blk(512, 0), blk(512, 1), blk(1024, 1), blk(1024, 2), blk(LANES, 0), _whole(), _whole(), _whole()],
        out_specs=[pl.BlockSpec((tT, GLA_V), lambda i: (i, 0)),
                   pl.BlockSpec((nc, GLA_HEADS, GLA_DV, GLA_DK), lambda i: (i, 0, 0, 0))],
        out_shape=[_sds((T, GLA_V), BF16), _sds((T // CHUNK, GLA_HEADS, GLA_DV, GLA_DK), F32)],
        scratch_shapes=[pltpu.VMEM((GLA_HEADS, GLA_DV, GLA_DK), F32)],
        args=(proj, proj, proj, proj, alow, wgu, b_gate, gn), job=job)


def _sgu_mask():
    i = lax.broadcasted_iota(jnp.int32, (SGU_BLOCK, SGU_BLOCK), 0)
    j = lax.broadcasted_iota(jnp.int32, (SGU_BLOCK, SGU_BLOCK), 1)
    return lax.shift_right_logical(j, 6) <= lax.shift_right_logical(i, 6)


def _sgu_fwd(proj, ln_g, ln_b, w_sp, b_sp_t):
    T = proj.shape[0]
    tT = _row_tile(T, 512)
    nb = tT // SGU_BLOCK

    def body(su_ref, sv_ref, lg_ref, lb_ref, w_ref, b_ref, y_ref):
        mask = _sgu_mask()
        for g in range(SGU_GROUPS):
            gc = slice(g * SGU_DG, (g + 1) * SGU_DG)
            wm = jnp.where(mask, w_ref[g], 0.0).astype(BF16)
            vf = _gelu(sv_ref[:, gc].astype(F32))
            mu = jnp.mean(vf, axis=-1, keepdims=True)
            vc = vf - mu
            rstd = lax.rsqrt(jnp.mean(vc * vc, axis=-1, keepdims=True) + EPS)
            vn = (vc * rstd * lg_ref[:, gc] + lb_ref[:, gc]).astype(BF16)
            u = _gelu(su_ref[:, gc].astype(F32))
            for b in range(nb):
                rows = slice(b * SGU_BLOCK, (b + 1) * SGU_BLOCK)
                mixed = _dot(wm, vn[rows, :]) + b_ref[:, g:g + 1]
                y_ref[rows, gc] = (u[rows, :] * mixed).astype(BF16)

    blk = lambda j: pl.BlockSpec((tT, 1024), lambda i: (i, j))
    res, _ = _call(body, name="sgu_fwd", grid=(T // tT,), parallel=True,
                   in_specs=[blk(3), blk(4), _whole(), _whole(), _whole(), _whole()],
                   out_specs=[pl.BlockSpec((tT, 1024), lambda i: (i, 0))], out_shape=[_sds((T, 1024), BF16)],
                   args=(proj, proj, ln_g, ln_b, w_sp, b_sp_t))
    return res[0]


def _merge_fwd(x, proj, y_gla, y_sgu, w_bg, w_bs, w_o, g_pm):
    T = x.shape[0]
    tT = _row_tile(T, 512)

    def body(x_ref, gg_ref, gs_ref, yg_ref, ys_ref, wbg_ref, wbs_ref, wo_ref, g_ref,
             zg_ref, zs_ref, mg_ref, mix_ref, x1_ref):
        zg = _dot(yg_ref[...], wbg_ref[...])
        zs = _dot(ys_ref[...], wbs_ref[...])
        zg_ref[...] = zg.astype(BF16)
        zs_ref[...] = zs.astype(BF16)
        merged = (_sigmoid(gg_ref[...].astype(F32)) * zg + _sigmoid(gs_ref[...].astype(F32)) * zs).astype(BF16)
        mg_ref[...] = merged
        mix = _dot(merged, wo_ref[...])
        mix_ref[...] = mix
        x1_ref[...] = x_ref[...] + mix * _rms_stats(mix) * g_ref[...]

    row = pl.BlockSpec((tT, D_MODEL), lambda i: (i, 0))
    blk = lambda j: pl.BlockSpec((tT, 1024), lambda i: (i, j))
    sds = lambda dt: _sds((T, D_MODEL), dt)
    res, _ = _call(body, name="merge_fwd", grid=(T // tT,), parallel=True,
                   in_specs=[row, blk(5), blk(6), row, row, _whole(), _whole(), _whole(),
                             pl.BlockSpec((1, D_MODEL), lambda i: (0, 0))],
                   out_specs=[row] * 5, out_shape=[sds(BF16), sds(BF16), sds(BF16), sds(F32), sds(F32)],
                   args=(x, proj, proj, y_gla, y_sgu, w_bg, w_bs, w_o, g_pm))
    return res


def _ffn_fwd_bwd(x1, tgt, w_fi, w_fo, g_pf, g_po):
    T = x1.shape[0]
    tT = _row_tile(T, 256)
    half = D_FF // 2

    def body(x1_ref, t_ref, wfi_ref, wfo_ref, gpf_ref, gpo_ref,
             h_ref, f_ref, dgu_ref, dy_ref, dx1_ref, loss_ref, dgpf_ref, dgpo_ref, gu_scr):
        @pl.when(pl.program_id(0) == 0)
        def _():
            loss_ref[...] = jnp.zeros_like(loss_ref)
            dgpf_ref[...] = jnp.zeros_like(dgpf_ref)
            dgpo_ref[...] = jnp.zeros_like(dgpo_ref)

        x1v = x1_ref[...]
        r2 = _rms_stats(x1v)
        h = (x1v * r2 * gpf_ref[...]).astype(BF16)
        h_ref[...] = h
        y = jnp.zeros((tT, D_MODEL), F32)
        for j in range(2):
            gc = slice(j * half, (j + 1) * half)
            uc = slice(D_FF + j * half, D_FF + (j + 1) * half)
            gate = _dot(h, wfi_ref[:, gc])
            up = _dot(h, wfi_ref[:, uc])
            gu_scr[:, gc] = gate
            gu_scr[:, uc] = up
            f = (gate * _sigmoid(gate) * up).astype(BF16)
            f_ref[:, gc] = f
            y = y + _dot(f, wfo_ref[gc, :])
        r3 = _rms_stats(y)
        x2 = x1v + y * r3 * gpo_ref[...]
        err = x2 - t_ref[...]
        loss_ref[...] += jnp.sum(err * err) * (0.5 / D_MODEL)
        dx2 = err * (1.0 / D_MODEL)
        dy, dg = _rms_bwd(dx2, y, r3, gpo_ref[...])
        dgpo_ref[...] += jnp.sum(dg, axis=0, keepdims=True)
        dyb = dy.astype(BF16)
        dy_ref[...] = dyb
        dh = jnp.zeros((tT, D_MODEL), F32)
        for j in range(2):
            gc = slice(j * half, (j + 1) * half)
            uc = slice(D_FF + j * half, D_FF + (j + 1) * half)
            df = _dot(dyb, wfo_ref[gc, :], _NT)
            gate = gu_scr[:, gc]
            up = gu_scr[:, uc]
            sg = _sigmoid(gate)
            dgate = (df * up * (sg * (1.0 + gate * (1.0 - sg)))).astype(BF16)
            dup = (df * (gate * sg)).astype(BF16)
            dgu_ref[:, gc] = dgate
            dgu_ref[:, uc] = dup
            dh = dh + _dot(dgate, wfi_ref[:, gc], _NT) + _dot(dup, wfi_ref[:, uc], _NT)
        dx1n, dg2 = _rms_bwd(dh, x1v, r2, gpf_ref[...])
        dgpf_ref[...] += jnp.sum(dg2, axis=0, keepdims=True)
        dx1_ref[...] = dx2 + dx1n

    row = lambda w: pl.BlockSpec((tT, w), lambda i: (i, 0))
    vec = pl.BlockSpec((1, D_MODEL), lambda i: (0, 0))
    res, _ = _call(
        body, name="ffn_fwd_bwd", grid=(T // tT,),
        in_specs=[row(D_MODEL), row(D_MODEL), _whole(), _whole(), vec, vec],
        out_specs=[row(D_MODEL), row(D_FF), row(2 * D_FF), row(D_MODEL), row(D_MODEL),
                   pl.BlockSpec((1, LANES), lambda i: (0, 0)), vec, vec],
        out_shape=[_sds((T, D_MODEL), BF16), _sds((T, D_FF), BF16), _sds((T, 2 * D_FF), BF16), _sds((T, D_MODEL), BF16),
                   _sds((T, D_MODEL), F32), _sds((1, LANES), F32), _sds((1, D_MODEL), F32), _sds((1, D_MODEL), F32)],
        scratch_shapes=[pltpu.VMEM((tT, 2 * D_FF), F32)], args=(x1, tgt, w_fi, w_fo, g_pf, g_po))
    return res


def _merge_bwd(dx1, mix, proj, zg, zs, w_bg, w_bs, w_o, g_pm, job=None):
    T = dx1.shape[0]
    tT = _row_tile(T, 512)

    def body(dx1_ref, mix_ref, gg_ref, gs_ref, zg_ref, zs_ref, wbg_ref, wbs_ref, wo_ref, g_ref,
             dmix_ref, dzg_ref, dzs_ref, dgate_ref, dyg_ref, dys_ref, dgpm_ref):
        @pl.when(pl.program_id(0) == 0)
        def _():
            dgpm_ref[...] = jnp.zeros_like(dgpm_ref)

        mix = mix_ref[...]
        dmix, dg = _rms_bwd(dx1_ref[...], mix, _rms_stats(mix), g_ref[...])
        dgpm_ref[...] += jnp.sum(dg, axis=0, keepdims=True)
        dmb = dmix.astype(BF16)
        dmix_ref[...] = dmb
        dmerged = _dot(dmb, wo_ref[...], _NT)
        for k, (gate_ref, z_ref, w_ref, dz_ref, dy_ref) in enumerate((
                (gg_ref, zg_ref, wbg_ref, dzg_ref, dyg_ref), (gs_ref, zs_ref, wbs_ref, dzs_ref, dys_ref))):
            sg = _sigmoid(gate_ref[...].astype(F32))
            dz = (dmerged * sg).astype(BF16)
            dz_ref[...] = dz
            dgate_ref[:, k * 1024:(k + 1) * 1024] = (dmerged * z_ref[...].astype(F32) * (sg * (1.0 - sg))).astype(BF16)
            dy_ref[...] = _dot(dz, w_ref[...], _NT).astype(BF16)

    row = pl.BlockSpec((tT, D_MODEL), lambda i: (i, 0))
    blk = lambda j: pl.BlockSpec((tT, 1024), lambda i: (i, j))
    vec = pl.BlockSpec((1, D_MODEL), lambda i: (0, 0))
    sds = _sds((T, D_MODEL), BF16)
    return _call(
        body, name="merge_bwd", grid=(T // tT,),
        in_specs=[row, row, blk(5), blk(6), row, row, _whole(), _whole(), _whole(), vec],
        out_specs=[row, row, row, pl.BlockSpec((tT, W_MRG), lambda i: (i, 0)), row, row, vec],
        out_shape=[sds, sds, sds, _sds((T, W_MRG), BF16), sds, sds, _sds((1, D_MODEL), F32)],
        args=(dx1, mix, proj, proj, zg, zs, w_bg, w_bs, w_o, g_pm), job=job)


def _sgu_bwd(proj, dy_sgu, ln_g, ln_b, w_sp, b_sp_t, job=None):
    T = proj.shape[0]
    tT = _row_tile(T, 512)
    nb = tT // SGU_BLOCK

    def body(su_ref, sv_ref, dy_ref, lg_ref, lb_ref, w_ref, b_ref, dp_ref, dw_ref, dbt_ref, dlg_ref, dlb_ref):
        @pl.when(pl.program_id(0) == 0)
        def _():
            dw_ref[...] = jnp.zeros_like(dw_ref)
            dbt_ref[...] = jnp.zeros_like(dbt_ref)
            dlg_ref[...] = jnp.zeros_like(dlg_ref)
            dlb_ref[...] = jnp.zeros_like(dlb_ref)

        mask = _sgu_mask()
        lane = lax.broadcasted_iota(jnp.int32, (SGU_BLOCK, LANES), 1)
        for g in range(SGU_GROUPS):
            gc = slice(g * SGU_DG, (g + 1) * SGU_DG)
            gc_v = slice(1024 + g * SGU_DG, 1024 + (g + 1) * SGU_DG)
            wm = jnp.where(mask, w_ref[g], 0.0).astype(BF16)
            vf, dvf_dsv = _gelu_and_grad(sv_ref[:, gc].astype(F32))
            mu = jnp.mean(vf, axis=-1, keepdims=True)
            vc = vf - mu
            rstd = lax.rsqrt(jnp.mean(vc * vc, axis=-1, keepdims=True) + EPS)
            vhat = vc * rstd
            vn = (vhat * lg_ref[:, gc] + lb_ref[:, gc]).astype(BF16)
            u, du_dsu = _gelu_and_grad(su_ref[:, gc].astype(F32))
            dy = dy_ref[:, gc].astype(F32)
            dmixed = (dy * u).astype(BF16)
            dvn_parts = []
            dw_acc = jnp.zeros((SGU_BLOCK, SGU_BLOCK), F32)
            db_acc = jnp.zeros((SGU_BLOCK, 1), F32)
            for b in range(nb):
                rows = slice(b * SGU_BLOCK, (b + 1) * SGU_BLOCK)
                mixed = _dot(wm, vn[rows, :]) + b_ref[:, g:g + 1]
                dp_ref[rows, gc] = (dy[rows, :] * mixed * du_dsu[rows, :]).astype(BF16)
                dvn_parts.append(_dot(wm, dmixed[rows, :], _TN))
                dw_acc = dw_acc + _dot(dmixed[rows, :], vn[rows, :], _NT)
                db_acc = db_acc + jnp.sum(dmixed[rows, :].astype(F32), axis=-1, keepdims=True)
            dw_ref[g] += jnp.where(mask, dw_acc, 0.0)
            dbt_ref[...] += jnp.where(lane == g, db_acc, 0.0)
            dvn = jnp.concatenate(dvn_parts, axis=0)
            dlg_ref[:, gc] += jnp.sum(dvn * vhat, axis=0, keepdims=True)
            dlb_ref[:, gc] += jnp.sum(dvn, axis=0, keepdims=True)
            dvh = dvn * lg_ref[:, gc]
            dvf = rstd * (dvh - jnp.mean(dvh, axis=-1, keepdims=True)
                          - vhat * jnp.mean(dvh * vhat, axis=-1, keepdims=True))
            dp_ref[:, gc_v] = (dvf * dvf_dsv).astype(BF16)

    blk = lambda j: pl.BlockSpec((tT, 1024), lambda i: (i, j))
    row = lambda w: pl.BlockSpec((tT, w), lambda i: (i, 0))
    vec = pl.BlockSpec((1, 1024), lambda i: (0, 0))
    return _call(
        body, name="sgu_bwd", grid=(T // tT,),
        in_specs=[blk(3), blk(4), row(1024), _whole(), _whole(), _whole(), _whole()],
        out_specs=[row(W_SGU), pl.BlockSpec((SGU_GROUPS, SGU_BLOCK, SGU_BLOCK), lambda i: (0, 0, 0)),
                   pl.BlockSpec((SGU_BLOCK, LANES), lambda i: (0, 0)), vec, vec],
        out_shape=[_sds((T, W_SGU), BF16), _sds((SGU_GROUPS, SGU_BLOCK, SGU_BLOCK), F32), _sds((SGU_BLOCK, LANES), F32),
                   _sds((1, 1024), F32), _sds((1, 1024), F32)],
        args=(proj, proj, dy_sgu, ln_g, ln_b, w_sp, b_sp_t), job=job)


def _gla_bwd(proj, alow, wgu, b_gate, gn, states, dy_gla, job=None):
    T = proj.shape[0]
    tT = _row_tile(T, 512)
    nc = tT // CHUNK
    nt = T // tT

    def body(q_ref, k_ref, v_ref, r_ref, al_ref, wgu_ref, bg_ref, gn_ref, st_ref, sp_ref, dy_ref,
             dp_ref, dal_ref, dgn_ref, dbg_ref, dwgu_ref, g_scr, dd_scr, dt_scr):
        step = pl.program_id(0)

        @pl.when(step == 0)
        def _():
            g_scr[...] = jnp.zeros_like(g_scr)
            dgn_ref[...] = jnp.zeros_like(dgn_ref)
            dbg_ref[...] = jnp.zeros_like(dbg_ref)
            dwgu_ref[...] = jnp.zeros_like(dwgu_ref)

        has_prev = jnp.where(step == nt - 1, 0.0, 1.0)
        logit, la, delta = _gla_decay_terms(al_ref, wgu_ref, bg_ref, tT)
        e = jnp.exp(delta)
        kdec_f = k_ref[...].astype(F32) * e
        kdec = kdec_f.astype(BF16)
        for c in reversed(range(nc)):
            rows = slice(c * CHUNK, (c + 1) * CHUNK)
            first = slice(c * CHUNK, c * CHUNK + 1)
            dec = jnp.exp(la[first, :] + delta[first, :])
            for h in range(GLA_HEADS):
                kc = slice(h * GLA_DK, (h + 1) * GLA_DK)
                vc = slice(h * GLA_DV, (h + 1) * GLA_DV)
                dq_c = slice(OFF_Q + h * GLA_DK, OFF_Q + (h + 1) * GLA_DK)
                dk_c = slice(OFF_K + h * GLA_DK, OFF_K + (h + 1) * GLA_DK)
                dv_c = slice(OFF_V + h * GLA_DV, OFF_V + (h + 1) * GLA_DV)
                dr_c = slice(OFF_R + h * GLA_DV, OFF_R + (h + 1) * GLA_DV)
                s_t = st_ref[c, h]
                s_prev = st_ref[c - 1, h] if c > 0 else sp_ref[0, h] * has_prev
                qs = (q_ref[rows, kc].astype(F32) * (GLA_DK ** -0.5)).astype(BF16)
                s_b = s_t.astype(BF16)
                o = _dot(qs, s_b, _NT)
                rstd = _rms_stats(o)
                ohat = o * rstd
                gnh = gn_ref[:, vc]
                dy = dy_ref[rows, vc].astype(F32)
                rr = r_ref[rows, vc].astype(F32)
                sg = _sigmoid(rr)
                don = dy * (rr * sg)
                dp_ref[rows, dr_c] = (dy * (ohat * gnh) * (sg * (1.0 + rr * (1.0 - sg)))).astype(BF16)
                dgn_ref[:, vc] += jnp.sum(don * ohat, axis=0, keepdims=True)
                dn = don * gnh
                do = (rstd * (dn - ohat * jnp.mean(dn * ohat, axis=-1, keepdims=True))).astype(BF16)
                dp_ref[rows, dq_c] = (_dot(do, s_b) * (GLA_DK ** -0.5)).astype(BF16)
                g_t = _dot(do, qs, _TN) + g_scr[h]
                g_b = g_t.astype(BF16)
                dp_ref[rows, dv_c] = _dot(kdec[rows, kc], g_b, _NT).astype(BF16)
                dkdec = _dot(v_ref[rows, vc], g_b)
                ddec = jnp.sum(g_t * s_prev, axis=0, keepdims=True)
                dp_ref[rows, dk_c] = (dkdec * e[rows, kc]).astype(BF16)
                dd_scr[rows, kc] = dkdec * kdec_f[rows, kc]
                dt_scr[rows, kc] = jnp.broadcast_to(ddec * dec[:, kc], (CHUNK, GLA_DK))
                g_scr[h] = g_t * dec[:, kc]
        dla = _dot_exact_lhs(_chunk_masks(tT, upper=False), dd_scr[...]) + dt_scr[...]
        dlogit = dla * (1.0 / GLA_TAU) * _sigmoid(-logit)
        dbg_ref[...] += jnp.sum(dlogit, axis=0, keepdims=True)
        dwgu_ref[...] += _dot_f32(al_ref[...], dlogit, _TN)
        dal_ref[...] = _dot_f32(dlogit, wgu_ref[...], _NT).astype(BF16)

    rev = lambda i: nt - 1 - i
    blk = lambda w, j: pl.BlockSpec((tT, w), lambda i: (rev(i), j))
    st_blk = pl.BlockSpec((nc, GLA_HEADS, GLA_DV, GLA_DK), lambda i: (rev(i), 0, 0, 0))
    sp_blk = pl.BlockSpec((1, GLA_HEADS, GLA_DV, GLA_DK), lambda i: (jnp.maximum(rev(i) * nc - 1, 0), 0, 0, 0))
    return _call(
        body, name="gla_bwd", grid=(nt,),
        in_specs=[blk(512, 0), blk(512, 1), blk(1024, 1), blk(1024, 2), blk(LANES, 0), _whole(), _whole(), _whole(),
                  st_blk, sp_blk, blk(GLA_V, 0)],
        out_specs=[blk(W_GLA, 0), blk(LANES, 0), pl.BlockSpec((1, GLA_V), lambda i: (0, 0)),
                   pl.BlockSpec((1, GLA_QK), lambda i: (0, 0)), pl.BlockSpec((LANES, GLA_QK), lambda i: (0, 0))],
        out_shape=[_sds((T, W_GLA), BF16), _sds((T, LANES), BF16), _sds((1, GLA_V), F32), _sds((1, GLA_QK), F32),
                   _sds((LANES, GLA_QK), F32)],
        scratch_shapes=[pltpu.VMEM((GLA_HEADS, GLA_DV, GLA_DK), F32), pltpu.VMEM((tT, GLA_QK), F32),
                        pltpu.VMEM((tT, GLA_QK), F32)],
        args=(proj, proj, proj, proj, alow, wgu, b_gate, gn, states, states, dy_gla), job=job)


def _inproj_bwd(x, dx1, g1, w_all, dparts, job=None):
    T = x.shape[0]
    tT = _row_tile(T, 512)
    offs = (0, W_GLA, W_GLA + W_SGU, N_MAIN)

    def body(x_ref, dx1_ref, g_ref, w_ref, *rest):
        part_refs, (dx_ref, dg_ref) = rest[:len(offs)], rest[len(offs):]

        @pl.when(pl.program_id(0) == 0)
        def _():
            dg_ref[...] = jnp.zeros_like(dg_ref)

        da = jnp.zeros((tT, D_MODEL), F32)
        for off, p_ref in zip(offs, part_refs):
            da = da + _dot(p_ref[...], w_ref[:, off:off + p_ref.shape[1]], _NT)
        xv = x_ref[...]
        dx, dg = _rms_bwd(da, xv, _rms_stats(xv), g_ref[...])
        dg_ref[...] += jnp.sum(dg, axis=0, keepdims=True)
        dx_ref[...] = dx1_ref[...] + dx

    row = lambda w: pl.BlockSpec((tT, w), lambda i: (i, 0))
    vec = pl.BlockSpec((1, D_MODEL), lambda i: (0, 0))
    return _call(
        body, name="inproj_bwd", grid=(T // tT,),
        in_specs=[row(D_MODEL), row(D_MODEL), vec, _whole()] + [row(p.shape[1]) for p in dparts],
        out_specs=[row(D_MODEL), vec], out_shape=[_sds((T, D_MODEL), F32), _sds((1, D_MODEL), F32)],
        args=(x, dx1, g1, w_all, *dparts), job=job)


def _tn_matmul(a, b, name, job=None):
    T, M = a.shape
    N = b.shape[1]
    tk = _row_tile(T, 1024)
    tm = M if M <= 1024 else 1408
    tn = N if N <= 1024 else (1024 if N % 1024 == 0 else 1408)
    assert M % tm == 0 and N % tn == 0

    def body(a_ref, b_ref, o_ref):
        @pl.when(pl.program_id(2) == 0)
        def _():
            o_ref[...] = jnp.zeros_like(o_ref)

        o_ref[...] += _dot(a_ref[...], b_ref[...], _TN)

    res, jres = _call(
        body, name=name, grid=(M // tm, N // tn, T // tk),
        in_specs=[pl.BlockSpec((tk, tm), lambda i, j, k: (k, i)), pl.BlockSpec((tk, tn), lambda i, j, k: (k, j))],
        out_specs=[pl.BlockSpec((tm, tn), lambda i, j, k: (i, j))], out_shape=[_sds((M, N), F32)], args=(a, b), job=job)
    return res[0], jres


def _pad_rows(a, rows=8):
    return jnp.pad(a, ((0, rows - a.shape[0]), (0, LANES - a.shape[1])))


def _halves_view(dw):
    r = dw.shape[0] // N_CHIPS
    return dw.reshape(N_CHIPS, 2, r // 2, dw.shape[1])


def kernel(x, norm_pre_mix, w_in, w_gate_up, b_gate, gla_norm, sgu_ln_g, sgu_ln_b, w_spatial, b_spatial, w_branch_gla, w_branch_sgu, w_out, norm_post_mix, norm_pre_ffn, w_ffn_in, w_ffn_out, norm_post_ffn, loss_target, m_norm_pre_mix, m_w_in, m_w_gate_up, m_b_gate, m_gla_norm, m_sgu_ln_g, m_sgu_ln_b, m_w_spatial, m_b_spatial, m_w_branch_gla, m_w_branch_sgu, m_w_out, m_norm_post_mix, m_norm_pre_ffn, m_w_ffn_in, m_w_ffn_out, m_norm_post_ffn, v_norm_pre_mix, v_w_in, v_w_gate_up, v_b_gate, v_gla_norm, v_sgu_ln_g, v_sgu_ln_b, v_w_spatial, v_b_spatial, v_w_branch_gla, v_w_branch_sgu, v_w_out, v_norm_post_mix, v_norm_pre_ffn, v_w_ffn_in, v_w_ffn_out, v_norm_post_ffn):
    chip = 2 * lax.axis_index("x") + lax.axis_index("y")
    xt, tgt = x[0], loss_target[0]

    tiny = jnp.concatenate([w_gate_up[0], _pad_rows(gla_norm[0]), _pad_rows(sgu_ln_g[0]), _pad_rows(sgu_ln_b[0]),
                            jnp.zeros((8, LANES), F32)], axis=0)
    g_in, g_tiny = _run_job(_job_gather([w_in[0].astype(BF16), tiny], [False, False]), "gather_w_in")
    w_all = _relayout_w_in(g_in)
    cols = lambda a: a.transpose(1, 0, 2).reshape(a.shape[1], N_CHIPS * a.shape[2])
    wgu = jnp.pad(cols(g_tiny[:, 0:16]), ((0, LANES - GLA_RANK), (0, 0)))
    gn = cols(g_tiny[:, 16:20, :64]).reshape(1, GLA_V)
    ln_g = cols(g_tiny[:, 24:28, :64]).reshape(1, 1024)
    ln_b = cols(g_tiny[:, 32:36, :64]).reshape(1, 1024)
    b_sp_t = jnp.pad(b_spatial[0].T, ((0, 0), (0, LANES - SGU_GROUPS)))
    w_sp = w_spatial[0]

    (a, proj, alow), (g_bg, g_bs, g_o, g_fo) = _inproj_fwd(
        xt, norm_pre_mix, w_all,
        job=_job_gather([w_branch_gla[0].astype(BF16), w_branch_sgu[0].astype(BF16), w_out[0].astype(BF16),
                         w_ffn_out[0].astype(BF16)], [False] * 4))
    rows = lambda g: g.reshape(N_CHIPS * g.shape[1], g.shape[2])
    w_bg, w_bs, w_o, w_fo = rows(g_bg), rows(g_bs), rows(g_o), rows(g_fo)
    (y_gla, states), (w_fi,) = _gla_fwd(proj, alow, wgu, b_gate, gn,
                                        job=_job_gather([w_ffn_in[0].astype(BF16)], [True]))
    y_sgu = _sgu_fwd(proj, ln_g, ln_b, w_sp, b_sp_t)
    zg, zs, merged, mix, x1 = _merge_fwd(xt, proj, y_gla, y_sgu, w_bg, w_bs, w_o, norm_post_mix)
    h, f, dgu, dy, dx1, loss, d_gpf, d_gpo = _ffn_fwd_bwd(x1, tgt, w_fi, w_fo, norm_pre_ffn, norm_post_ffn)

    dw_fo, _ = _tn_matmul(f, dy, "dw_ffn_out")
    dw_fi, (mine_fo, q_fo) = _tn_matmul(h, dgu, "dw_ffn_in", job=_job_split_cores([[(_halves_view(dw_fo), 0)]]))
    c_fo = _presum(mine_fo, q_fo, "presum_ffn_out")
    (dmix, dzg, dzs, dp_mrg, dyg, dys, d_gpm), (s_fo, mine_fi, q_fi) = _merge_bwd(
        dx1, mix, proj, zg, zs, w_bg, w_bs, w_o, norm_post_mix,
        job=_join(_job_scatter([c_fo]), _job_split_cores([[(dw_fi, 0)]])))
    c_fi = _presum(mine_fi, q_fi, "presum_ffn_in")
    dw_o, _ = _tn_matmul(merged, dmix, "dw_out")
    dw_bg, _ = _tn_matmul(y_gla, dzg, "dw_branch_gla")
    dw_bs, _ = _tn_matmul(y_sgu, dzs, "dw_branch_sgu")
    (dp_sgu, d_wsp, d_bsp_t, d_lng, d_lnb), (s_fi, mine_o, q_o, mine_bg, q_bg, mine_bs, q_bs) = _sgu_bwd(
        proj, dys, ln_g, ln_b, w_sp, b_sp_t,
        job=_join(_job_scatter([c_fi]), _job_split_cores([[(_halves_view(dw_o), 0)], [(_halves_view(dw_bg), 0)],
                                                          [(_halves_view(dw_bs), 0)]])))
    c_o, c_bg, c_bs = _presum(mine_o, q_o, "presum_out"), _presum(mine_bg, q_bg, "presum_branch_gla"), _presum(mine_bs, q_bs, "presum_branch_sgu")
    h_fo = _sum_slots(s_fo, "sum_ffn_out")
    (dp_gla, dal, d_gn, d_bg, d_wgu), (s_o, s_bg, s_bs, g_fo2) = _gla_bwd(
        proj, alow, wgu, b_gate, gn, states, dyg, job=_join(_job_scatter([c_o, c_bg, c_bs]), _job_join_cores([h_fo])))
    h_fi, h_o, h_bg, h_bs = (_sum_slots(s_fi, "sum_ffn_in"), _sum_slots(s_o, "sum_out"), _sum_slots(s_bg, "sum_branch_gla"),
                             _sum_slots(s_bs, "sum_branch_sgu"))
    dw_a, (g_fi2, g_o2, g_bg2, g_bs2) = _tn_matmul(a, dp_gla, "dw_in_gla", job=_job_join_cores([h_fi, h_o, h_bg, h_bs]))
    dw_b, _ = _tn_matmul(a, dp_sgu, "dw_in_sgu")
    dw_c, _ = _tn_matmul(a, dp_mrg, "dw_in_merge")
    dw_d, _ = _tn_matmul(a, dal, "dw_in_gate")

    full = lambda g2: g2.reshape(2 * g2.shape[1], g2.shape[2])
    grads, deltas, new_m, new_v = {}, {}, {}, {}

    def update(name, w, m, v, g2, job=None):
        g = full(g2)
        (d, m2, v2), jres = _adamw(w[0], m[0], v[0], g, "adamw_" + name, job=job)
        grads[name], deltas[name], new_m[name], new_v[name] = g[None], d[None], m2[None], v2[None]
        return jres

    update("w_ffn_out", w_ffn_out, m_w_ffn_out, v_w_ffn_out, g_fo2)
    mine_in, q_in = update("w_ffn_in", w_ffn_in, m_w_ffn_in, v_w_ffn_in, g_fi2, job=_job_split_cores(
        [[(dw_a, 0), (dw_b, W_GLA), (dw_c, W_GLA + W_SGU), (dw_d, N_MAIN)]]))
    update("w_out", w_out, m_w_out, v_w_out, g_o2)
    update("w_branch_gla", w_branch_gla, m_w_branch_gla, v_w_branch_gla, g_bg2)
    update("w_branch_sgu", w_branch_sgu, m_w_branch_sgu, v_w_branch_sgu, g_bs2)
    c_in = _presum_w_in(mine_in, q_in)
    (grad_x, d_g1), (s_in,) = _inproj_bwd(xt, dx1, norm_pre_mix, w_all, (dp_gla, dp_sgu, dp_mrg, dal),
                                          job=_job_scatter([c_in]))
    h_in = _sum_slots(s_in, "sum_w_in")
    (g_in2,) = _run_job(_job_join_cores([h_in]), "join_w_in")
    update("w_in", w_in, m_w_in, v_w_in, g_in2)

    pack = jnp.concatenate([
        d_wsp.reshape(512, LANES), d_wgu[:GLA_RANK].reshape(64, LANES),
        d_g1.reshape(8, LANES), d_gpm.reshape(8, LANES), d_gpf.reshape(8, LANES), d_gpo.reshape(8, LANES),
        _pad_rows(d_bg.reshape(4, LANES)), _pad_rows(d_bsp_t[:, :SGU_GROUPS].T),
        d_gn.reshape(8, LANES), d_lng.reshape(8, LANES), d_lnb.reshape(8, LANES), _pad_rows(loss)], axis=0)
    tot = _allreduce_small(pack)
    loss_out = tot[648, 0]
    shard64 = lambda r0: _pad_rows(lax.dynamic_slice(tot[r0:r0 + 8].reshape(4, 256), (0, chip * 64), (4, 64)))
    g_small = jnp.concatenate([
        tot[0:512], lax.dynamic_slice(tot[512:576].reshape(16, 512), (0, chip * LANES), (16, LANES)),
        tot[576:608], tot[608:624], shard64(624), shard64(632), shard64(640)], axis=0)

    def small_pack(w_sp_, wg, n1, n2, n3, n4, bg, bs, gl, lg, lb):
        return jnp.concatenate([
            w_sp_[0].reshape(512, LANES), wg[0], n1.reshape(8, LANES), n2.reshape(8, LANES), n3.reshape(8, LANES),
            n4.reshape(8, LANES), _pad_rows(bg.reshape(4, LANES)), _pad_rows(bs[0]),
            _pad_rows(gl[0]), _pad_rows(lg[0]), _pad_rows(lb[0])], axis=0)

    sp_w = small_pack(w_spatial, w_gate_up, norm_pre_mix, norm_post_mix, norm_pre_ffn, norm_post_ffn, b_gate,
                      b_spatial, gla_norm, sgu_ln_g, sgu_ln_b)
    sp_m = small_pack(m_w_spatial, m_w_gate_up, m_norm_pre_mix, m_norm_post_mix, m_norm_pre_ffn, m_norm_post_ffn,
                      m_b_gate, m_b_spatial, m_gla_norm, m_sgu_ln_g, m_sgu_ln_b)
    sp_v = small_pack(v_w_spatial, v_w_gate_up, v_norm_pre_mix, v_norm_post_mix, v_norm_pre_ffn, v_norm_post_ffn,
                      v_b_gate, v_b_spatial, v_gla_norm, v_sgu_ln_g, v_sgu_ln_b)
    (sd, sm, sv), _ = _adamw(sp_w, sp_m, sp_v, g_small, "adamw_small")

    def unpack(p):
        return dict(
            w_spatial=p[0:512].reshape(1, 4, 128, 128), w_gate_up=p[512:528].reshape(1, 16, 128),
            norm_pre_mix=p[528:536].reshape(1, 1024), norm_post_mix=p[536:544].reshape(1, 1024),
            norm_pre_ffn=p[544:552].reshape(1, 1024), norm_post_ffn=p[552:560].reshape(1, 1024),
            b_gate=p[560:564].reshape(1, 512), b_spatial=p[568:572].reshape(1, 4, 128),
            gla_norm=p[576:580, :64].reshape(1, 4, 64), sgu_ln_g=p[584:588, :64].reshape(1, 4, 64),
            sgu_ln_b=p[592:596, :64].reshape(1, 4, 64))

    for store, p in ((grads, g_small), (deltas, sd), (new_m, sm), (new_v, sv)):
        store.update(unpack(p))

    order = ["norm_pre_mix", "w_in", "w_gate_up", "b_gate", "gla_norm", "sgu_ln_g", "sgu_ln_b", "w_spatial", "b_spatial",
             "w_branch_gla", "w_branch_sgu", "w_out", "norm_post_mix", "norm_pre_ffn", "w_ffn_in", "w_ffn_out",
             "norm_post_ffn"]
    out = [loss_out, grad_x[None]]
    for store in (grads, deltas, new_m, new_v):
        out.extend(store[n] for n in order)
    return tuple(out)
```

```python
import jax
import jax.numpy as jnp
from jax import lax
from jax.experimental import pallas as pl
from jax.experimental.pallas import tpu as pltpu

F32 = jnp.float32
BF16 = jnp.bfloat16

D_MODEL = 1024
GLA_HEADS = 4
GLA_DK = 128
GLA_DV = 256
GLA_QK = GLA_HEADS * GLA_DK
GLA_V = GLA_HEADS * GLA_DV
GLA_RANK = 16
GLA_TAU = 16.0
CHUNK = 64
SGU_GROUPS = 4
SGU_BLOCK = 128
SGU_DG = 256
D_FF = 2816
EPS = 1e-6
LANES = 128

OFF_Q, OFF_K, OFF_V, OFF_R, OFF_SU, OFF_SV, OFF_GG, OFF_GS, OFF_AL = 0, 512, 1024, 2048, 3072, 4096, 5120, 6144, 7168
W_GLA, W_SGU, W_MRG = 3072, 2048, 2048
N_MAIN = 7168
N_ALL = N_MAIN + LANES
_IN_SPLITS = (GLA_QK, GLA_QK, GLA_V, GLA_V, GLA_RANK, 1024, 1024, 1024, 1024)
_IN_STARTS = tuple(sum(_IN_SPLITS[:i]) for i in range(len(_IN_SPLITS) + 1))
_IN_DST = (OFF_Q, OFF_K, OFF_V, OFF_R, OFF_AL, OFF_SU, OFF_SV, OFF_GG, OFF_GS)
D_IN = _IN_STARTS[-1]

ADAM_LR = 0.001
ADAM_B1 = 0.9
ADAM_B2 = 0.999
ADAM_EPS = 1e-08
ADAM_WD = 0.01
ADAM_STEP = 10

VMEM_LIMIT_BYTES = 56 * 1024 * 1024
N_CHIPS = 4
N_PEER = N_CHIPS - 1
N_DEV = 8
MESH = pl.DeviceIdType.MESH

_NN = (((1,), (0,)), ((), ()))
_NT = (((1,), (1,)), ((), ()))
_TN = (((0,), (0,)), ((), ()))


def _dot(a, b, dims=_NN):
    return lax.dot_general(a, b, dims, preferred_element_type=F32)


def _split(x):
    hi = x.astype(BF16)
    lo = (x - hi.astype(F32)).astype(BF16)
    return hi, lo


def _dot_f32(a, b, dims=_NN):
    ah, al = _split(a)
    bh, bl = _split(b)
    return _dot(ah, bh, dims) + (_dot(al, bh, dims) + _dot(ah, bl, dims))


def _dot_exact_lhs(m, x):
    xh, xl = _split(x)
    return _dot(m, xh) + _dot(m, xl)


def _sigmoid(x):
    return 1.0 / (1.0 + jnp.exp(-x))


def _log_sigmoid(x):
    return jnp.minimum(x, 0.0) - jnp.log(1.0 + jnp.exp(-jnp.abs(x)))


_GELU_C = 0.7978845608028654
_GELU_A = 0.044715


def _gelu_and_grad(x):
    x2 = x * x
    t = jnp.tanh(_GELU_C * (x + _GELU_A * x * x2))
    g = 0.5 * x * (1.0 + t)
    dg = 0.5 * (1.0 + t) + 0.5 * x * (1.0 - t * t) * (_GELU_C * (1.0 + 3.0 * _GELU_A * x2))
    return g, dg


def _gelu(x):
    t = jnp.tanh(_GELU_C * (x + _GELU_A * x * x * x))
    return 0.5 * x * (1.0 + t)


def _rms_stats(x):
    return lax.rsqrt(jnp.mean(x * x, axis=-1, keepdims=True) + EPS)


def _rms_bwd(dout, y, r, g):
    yhat = y * r
    dn = dout * g
    dy = r * (dn - yhat * jnp.mean(dn * yhat, axis=-1, keepdims=True))
    return dy, dout * yhat


def _whole():
    return pl.BlockSpec(memory_space=pltpu.VMEM)


def _row_tile(T, want):
    t = min(T, want)
    assert T % t == 0
    return t


def _chunk_masks(tT, upper):
    row = lax.broadcasted_iota(jnp.int32, (tT, tT), 0)
    col = lax.broadcasted_iota(jnp.int32, (tT, tT), 1)
    same = lax.shift_right_logical(row, 6) == lax.shift_right_logical(col, 6)
    tri = (col > row) if upper else (col < row)
    return jnp.where(same & tri, 1.0, 0.0).astype(BF16)


class _Job:
    def __init__(self, ins, out_shapes, scratch, start, finish):
        self.ins, self.out_shapes, self.scratch, self.start, self.finish = list(ins), list(out_shapes), list(scratch), start, finish


def _join(*jobs):
    def split(refs, counts):
        out, at = [], 0
        for n in counts:
            out.append(refs[at:at + n])
            at += n
        return out

    ni, no, ns = [len(j.ins) for j in jobs], [len(j.out_shapes) for j in jobs], [len(j.scratch) for j in jobs]

    def start(ins, outs, scr):
        for j, a, b, c in zip(jobs, split(ins, ni), split(outs, no), split(scr, ns)):
            j.start(a, b, c)

    def finish(ins, outs, scr):
        for j, a, b, c in zip(jobs, split(ins, ni), split(outs, no), split(scr, ns)):
            j.finish(a, b, c)

    return _Job(sum((j.ins for j in jobs), []), sum((j.out_shapes for j in jobs), []),
                sum((j.scratch for j in jobs), []), start, finish)


def _mesh_pos():
    return lax.axis_index("x"), lax.axis_index("y"), lax.axis_index("c")


def _peer_chips(xi, yi):
    return [(1 - xi, yi), (xi, 1 - yi), (1 - xi, 1 - yi)]


def _half(ci, rows):
    return pl.ds(pl.multiple_of(ci * rows, 8), rows)


def _sds(shape, dtype):
    return jax.ShapeDtypeStruct(tuple(shape), dtype)


def _job_gather(arrs, by_cols):
    n = len(arrs)

    def dst(o, k, chip, rows):
        if by_cols[k]:
            c = arrs[k].shape[1]
            return o.at[rows, pl.ds(pl.multiple_of(chip * c, LANES), c)]
        return o.at[chip, rows]

    def copies(ins, outs, scr, want):
        ici_send, ici_recv, d2d_send, d2d_recv = scr
        xi, yi, ci = _mesh_pos()
        me = 2 * xi + yi
        res = []
        for k in range(n):
            r = arrs[k].shape[0]
            mine, other = _half(ci, r // 2), _half(1 - ci, r // 2)
            for j, (px, py) in enumerate(_peer_chips(xi, yi)):
                s = k * N_PEER + j
                pc = 2 * px + py
                ici = dict(send_sem=ici_send.at[s], recv_sem=ici_recv.at[s], device_id=(px, py, ci), device_id_type=MESH)
                d2d = dict(send_sem=d2d_send.at[s], recv_sem=d2d_recv.at[s], device_id=(xi, yi, 1 - ci),
                           device_id_type=MESH)
                made = {}
                if "send" in want:
                    made["send"] = pltpu.make_async_remote_copy(
                        src_ref=ins[k].at[mine], dst_ref=dst(outs[k], k, me, mine), **ici)
                if "arrive" in want:
                    made["arrive"] = pltpu.make_async_remote_copy(
                        src_ref=ins[k].at[mine], dst_ref=dst(outs[k], k, pc, mine), **ici)
                if "forward" in want:
                    made["forward"] = pltpu.make_async_remote_copy(
                        src_ref=dst(outs[k], k, pc, mine), dst_ref=dst(outs[k], k, pc, mine), **d2d)
                if "handed" in want:
                    made["handed"] = pltpu.make_async_remote_copy(
                        src_ref=dst(outs[k], k, pc, other), dst_ref=dst(outs[k], k, pc, other), **d2d)
                res.append(made)
        return res

    def start(ins, outs, scr):
        for cp in copies(ins, outs, scr, ("send",)):
            cp["send"].start()

    def finish(ins, outs, scr):
        for cp in copies(ins, outs, scr, ("arrive", "forward")):
            cp["arrive"].wait_recv()
            cp["forward"].start()
        for cp in copies(ins, outs, scr, ("handed", "send", "forward")):
            cp["handed"].wait_recv()
            cp["send"].wait_send()
            cp["forward"].wait_send()

    shapes = [_sds((a.shape[0], N_CHIPS * a.shape[1]) if bc else (N_CHIPS,) + a.shape, a.dtype)
              for a, bc in zip(arrs, by_cols)]
    dma = pltpu.SemaphoreType.DMA
    return _Job(arrs, shapes, [dma((n * N_PEER,))] * 4, start, finish)


def _job_scatter(parts):
    n = len(parts)

    def copies(ins, outs, scr):
        send_sems, recv_sems = scr
        xi, yi, ci = _mesh_pos()
        res = []
        for k in range(n):
            for j, (px, py) in enumerate(_peer_chips(xi, yi)):
                s = k * N_PEER + j
                res.append(pltpu.make_async_remote_copy(
                    src_ref=ins[k].at[2 * px + py], dst_ref=outs[k].at[j], send_sem=send_sems.at[s],
                    recv_sem=recv_sems.at[s], device_id=(px, py, ci), device_id_type=MESH))
        return res

    def start(ins, outs, scr):
        for cp in copies(ins, outs, scr):
            cp.start()

    def finish(ins, outs, scr):
        for cp in copies(ins, outs, scr):
            cp.wait_recv()
            cp.wait_send()

    dma = pltpu.SemaphoreType.DMA
    return _Job(parts, [_sds((N_PEER,) + p.shape[1:], p.dtype) for p in parts], [dma((n * N_PEER,))] * 2, start, finish)


def _job_to_other_core(groups):
    pieces = [(g, a, off) for g, group in enumerate(groups) for a, off in group]
    n = len(pieces)

    def geometry(group):
        a0, off0 = group[0]
        if off0 is None:
            return a0.shape
        if a0.ndim == 4:
            return (N_CHIPS, a0.shape[2], a0.shape[3])
        return (a0.shape[0] // 2, sum(a.shape[1] for a, _ in group))

    def copies(ins, outs, scr):
        send_sems, recv_sems = scr
        xi, yi, ci = _mesh_pos()
        res = []
        for p, (g, a, off) in enumerate(pieces):
            if off is None:
                give, land = ins[p], outs[g]
            elif a.ndim == 4:
                give, land = ins[p].at[pl.ds(0, N_CHIPS), 1 - ci], outs[g]
            else:
                hr, w = a.shape[0] // 2, a.shape[1]
                give, land = ins[p].at[_half(1 - ci, hr)], outs[g].at[pl.ds(0, hr), pl.ds(off, w)]
            res.append(pltpu.make_async_remote_copy(
                src_ref=give, dst_ref=land, send_sem=send_sems.at[p], recv_sem=recv_sems.at[p],
                device_id=(xi, yi, 1 - ci), device_id_type=MESH))
        return res

    def start(ins, outs, scr):
        for cp in copies(ins, outs, scr):
            cp.start()

    def finish(ins, outs, scr):
        for cp in copies(ins, outs, scr):
            cp.wait_recv()
            cp.wait_send()

    dma = pltpu.SemaphoreType.DMA
    return _Job([a for _, a, _ in pieces], [_sds(geometry(group), group[0][0].dtype) for group in groups],
                [dma((n,))] * 2, start, finish)


def _call(body, *, name, grid, in_specs, out_specs, out_shape, args, scratch_shapes=(), parallel=False, job=None):
    n_in, n_out, n_scr = len(in_specs), len(out_specs), len(scratch_shapes)
    if job is None:
        sem = ("parallel" if parallel else "arbitrary",) * len(grid)
        res = pl.pallas_call(
            body, name=name, grid=grid, in_specs=in_specs, out_specs=out_specs, out_shape=out_shape,
            scratch_shapes=list(scratch_shapes),
            compiler_params=pltpu.CompilerParams(dimension_semantics=sem, vmem_limit_bytes=VMEM_LIMIT_BYTES))(*args)
        return list(res), []
    n_ji, n_jo = len(job.ins), len(job.out_shapes)

    def carried(*refs):
        ins, refs = refs[:n_in], refs[n_in:]
        j_ins, refs = refs[:n_ji], refs[n_ji:]
        outs, refs = refs[:n_out], refs[n_out:]
        j_outs, refs = refs[:n_jo], refs[n_jo:]
        scr, j_scr = refs[:n_scr], refs[n_scr:]
        ids = [pl.program_id(d) for d in range(len(grid))]
        first = ids[0] == 0
        last = ids[0] == grid[0] - 1
        for d in range(1, len(grid)):
            first = first & (ids[d] == 0)
            last = last & (ids[d] == grid[d] - 1)

        @pl.when(first)
        def _():
            job.start(j_ins, j_outs, j_scr)

        body(*ins, *outs, *scr)

        @pl.when(last)
        def _():
            job.finish(j_ins, j_outs, j_scr)

    hbm = pl.BlockSpec(memory_space=pl.ANY)
    res = pl.pallas_call(
        carried, name=name, grid=grid, in_specs=list(in_specs) + [hbm] * n_ji, out_specs=list(out_specs) + [hbm] * n_jo,
        out_shape=list(out_shape) + job.out_shapes, scratch_shapes=list(scratch_shapes) + job.scratch,
        compiler_params=pltpu.CompilerParams(dimension_semantics=("arbitrary",) * len(grid),
                                             vmem_limit_bytes=VMEM_LIMIT_BYTES))(*args, *job.ins)
    return list(res[:n_out]), list(res[n_out:])


def _run_job(job, name):
    n_i, n_o = len(job.ins), len(job.out_shapes)

    def body(*refs):
        ins, outs, scr = refs[:n_i], refs[n_i:n_i + n_o], refs[n_i + n_o:]
        job.start(ins, outs, scr)
        job.finish(ins, outs, scr)

    hbm = pl.BlockSpec(memory_space=pl.ANY)
    return list(pl.pallas_call(body, name=name, in_specs=[hbm] * n_i, out_specs=[hbm] * n_o, out_shape=job.out_shapes,
                               scratch_shapes=job.scratch)(*job.ins))


def _allreduce_small(pack):
    rows = pack.shape[0]

    def body(pack_ref, out_ref, slots, send_sems, recv_sems):
        xi, yi, ci = _mesh_pos()
        me = 4 * xi + 2 * yi + ci
        slots[me] = pack_ref[...]

        def copy(r, slot):
            peer = (jnp.bitwise_xor(xi, (r >> 2) & 1), jnp.bitwise_xor(yi, (r >> 1) & 1), jnp.bitwise_xor(ci, r & 1))
            return pltpu.make_async_remote_copy(
                src_ref=pack_ref, dst_ref=slots.at[slot(peer)], send_sem=send_sems.at[r - 1],
                recv_sem=recv_sems.at[r - 1], device_id=peer, device_id_type=MESH)

        sends = [copy(r, lambda peer: me) for r in range(1, N_DEV)]
        for cp in sends:
            cp.start()
        for r in range(1, N_DEV):
            copy(r, lambda peer: 4 * peer[0] + 2 * peer[1] + peer[2]).wait_recv()
        for cp in sends:
            cp.wait_send()
        acc = slots[0]
        for d in range(1, N_DEV):
            acc = acc + slots[d]
        out_ref[...] = acc

    return pl.pallas_call(
        body, name="allreduce_small", in_specs=[_whole()], out_specs=_whole(),
        out_shape=_sds((rows, LANES), F32),
        scratch_shapes=[pltpu.VMEM((N_DEV, rows, LANES), F32), pltpu.SemaphoreType.DMA((N_DEV - 1,)),
                        pltpu.SemaphoreType.DMA((N_DEV - 1,))],
        compiler_params=pltpu.CompilerParams(vmem_limit_bytes=VMEM_LIMIT_BYTES),
    )(pack)


def _w_in_pieces():
    blk = D_IN // N_CHIPS
    pieces = []
    for s in range(len(_IN_SPLITS)):
        lo_s, hi_s = _IN_STARTS[s], _IN_STARTS[s + 1]
        for j in range(N_CHIPS):
            lo, hi = max(lo_s, j * blk), min(hi_s, (j + 1) * blk)
            if lo < hi:
                pieces.append((j, lo - j * blk, _IN_DST[s] + lo - lo_s, hi - lo))
    return pieces


def _relayout_w_in(gathered):
    _, rows, blk = gathered.shape
    tr = 256

    def body(g_ref, o_ref):
        o_ref[:, OFF_AL:N_ALL] = jnp.zeros((tr, LANES), BF16)
        for j, src, dst, w in _w_in_pieces():
            o_ref[:, dst:dst + w] = g_ref[j, :, src:src + w]

    res, _ = _call(body, name="relayout_w_in", grid=(rows // tr,), parallel=True,
                   in_specs=[pl.BlockSpec((N_CHIPS, tr, blk), lambda i: (0, i, 0))],
                   out_specs=[pl.BlockSpec((tr, N_ALL), lambda i: (i, 0))],
                   out_shape=[_sds((rows, N_ALL), BF16)], args=(gathered,))
    return res[0]


def _update_row_tile(rows):
    for t in range(min(rows, 256), 7, -8):
        if rows % t == 0:
            return t
    return rows


def _my_half(first_ref, second_ref):
    return jnp.where(lax.axis_index("c") == 0, first_ref[...], second_ref[...])


def _presum_w_in(dws, theirs):
    hr = theirs.shape[0]
    blk = D_IN // N_CHIPS
    tr = 128
    nh = hr // tr

    def body(*refs):
        dw_refs, (q_ref, o_ref, s_scr) = refs[:2 * len(dws)], refs[2 * len(dws):]
        for p, (a, off) in enumerate(dws):
            w = a.shape[1]
            s_scr[:, off:off + w] = (_my_half(dw_refs[2 * p], dw_refs[2 * p + 1]) + q_ref[:, off:off + w]).astype(BF16)
        for j, src, dst, w in _w_in_pieces():
            o_ref[j, :, src:src + w] = s_scr[:, dst:dst + w]

    in_specs, args = [], []
    for a, _ in dws:
        w = a.shape[1]
        in_specs += [pl.BlockSpec((tr, w), lambda i: (i, 0)), pl.BlockSpec((tr, w), lambda i: (i + nh, 0))]
        args += [a, a]
    res, _ = _call(body, name="presum_w_in", grid=(nh,), parallel=True,
                   in_specs=in_specs + [pl.BlockSpec((tr, N_ALL), lambda i: (i, 0))],
                   out_specs=[pl.BlockSpec((N_CHIPS, tr, blk), lambda i: (0, i, 0))],
                   out_shape=[_sds((N_CHIPS, hr, blk), BF16)], scratch_shapes=[pltpu.VMEM((tr, N_ALL), BF16)],
                   args=(*args, theirs))
    return res[0]


def _presum(dw, theirs, name):
    if dw.ndim == 4:
        _, _, hr, c = dw.shape
        tr = _update_row_tile(hr)
        first = pl.BlockSpec((1, 1, tr, c), lambda j, i: (j, 0, i, 0))
        second = pl.BlockSpec((1, 1, tr, c), lambda j, i: (j, 1, i, 0))
        other = pl.BlockSpec((1, tr, c), lambda j, i: (j, i, 0))
    else:
        hr, c = dw.shape[0] // 2, dw.shape[1] // N_CHIPS
        tr = _update_row_tile(hr)
        nh = hr // tr
        first = pl.BlockSpec((tr, c), lambda j, i: (i, j))
        second = pl.BlockSpec((tr, c), lambda j, i: (i + nh, j))
        other = pl.BlockSpec((tr, c), lambda j, i: (i, j))

    def body(a_ref, b_ref, q_ref, o_ref):
        mine = _my_half(a_ref, b_ref).reshape(tr, c)
        o_ref[...] = (mine + q_ref[...].reshape(tr, c)).astype(BF16).reshape(o_ref.shape)

    res, _ = _call(body, name=name, grid=(N_CHIPS, hr // tr), parallel=True, in_specs=[first, second, other],
                   out_specs=[pl.BlockSpec((1, tr, c), lambda j, i: (j, i, 0))],
                   out_shape=[_sds((N_CHIPS, hr, c), BF16)], args=(dw, dw, theirs))
    return res[0]


def _sum_slots(own, slots, name):
    rows, cols = own.shape
    tr = _update_row_tile(rows)

    def body(own_ref, s_ref, o_ref):
        acc = own_ref[...].astype(F32)
        for j in range(N_PEER):
            acc = acc + s_ref[j].astype(F32)
        o_ref[...] = acc

    res, _ = _call(body, name=name, grid=(rows // tr,), parallel=True,
                   in_specs=[pl.BlockSpec((tr, cols), lambda i: (i, 0)), pl.BlockSpec((N_PEER, tr, cols), lambda i: (0, i, 0))],
                   out_specs=[pl.BlockSpec((tr, cols), lambda i: (i, 0))], out_shape=[_sds((rows, cols), F32)],
                   args=(own, slots))
    return res[0]


def _adamw(w, m, v, g_mine, g_theirs, name, job=None):
    rows, cols = w.shape
    halves = g_theirs is not None
    tr = _update_row_tile(rows // 2 if halves else rows)
    nh = (rows // 2) // tr if halves else rows // tr
    bc1 = 1.0 - ADAM_B1 ** ADAM_STEP
    bc2 = 1.0 - ADAM_B2 ** ADAM_STEP

    def body(w_ref, m_ref, v_ref, *rest):
        g_refs, (g_out, d_out, m_out, v_out) = rest[:-4], rest[-4:]
        if halves:
            mine_here = (pl.program_id(0) // nh) == lax.axis_index("c")
            g = jnp.where(mine_here, g_refs[0][...], g_refs[1][...])
        else:
            g = g_refs[0][...]
        m2 = ADAM_B1 * m_ref[...] + (1.0 - ADAM_B1) * g
        v2 = ADAM_B2 * v_ref[...] + (1.0 - ADAM_B2) * (g * g)
        g_out[...] = g
        m_out[...] = m2
        v_out[...] = v2
        d_out[...] = -ADAM_LR * ((m2 / bc1) / (jnp.sqrt(v2 / bc2) + ADAM_EPS) + ADAM_WD * w_ref[...])

    spec = pl.BlockSpec((tr, cols), lambda i: (i, 0))
    g_spec = pl.BlockSpec((tr, cols), lambda i: (i % nh, 0))
    g_args = (g_mine, g_theirs) if halves else (g_mine,)
    return _call(body, name=name, grid=(rows // tr,), parallel=True, in_specs=[spec] * 3 + [g_spec] * len(g_args),
                 out_specs=[spec] * 4, out_shape=[_sds((rows, cols), F32)] * 4, args=(w, m, v, *g_args), job=job)


def _inproj_fwd(x, g1, w_all, job=None):
    T = x.shape[0]
    tT = _row_tile(T, 512)

    def body(x_ref, g_ref, w_ref, a_ref, proj_ref, alow_ref):
        xv = x_ref[...]
        a = (xv * _rms_stats(xv) * g_ref[...]).astype(BF16)
        a_ref[...] = a
        for j in range(N_MAIN // 1024):
            cols = slice(j * 1024, (j + 1) * 1024)
            proj_ref[:, cols] = _dot(a, w_ref[:, cols]).astype(BF16)
        alow_ref[...] = _dot(a, w_ref[:, N_MAIN:N_ALL])

    row = lambda w: pl.BlockSpec((tT, w), lambda i: (i, 0))
    return _call(
        body, name="inproj_fwd", grid=(T // tT,), parallel=True,
        in_specs=[row(D_MODEL), pl.BlockSpec((1, D_MODEL), lambda i: (0, 0)), _whole()],
        out_specs=[row(D_MODEL), row(N_MAIN), row(LANES)],
        out_shape=[_sds((T, D_MODEL), BF16), _sds((T, N_MAIN), BF16), _sds((T, LANES), F32)],
        args=(x, g1, w_all), job=job)


def _gla_decay_terms(al_ref, wgu_ref, bg_ref, tT):
    logit = _dot_f32(al_ref[...], wgu_ref[...]) + bg_ref[...]
    la = _log_sigmoid(logit) * (1.0 / GLA_TAU)
    delta = _dot_exact_lhs(_chunk_masks(tT, upper=True), la)
    return logit, la, delta


def _gla_fwd(proj, alow, wgu, b_gate, gn, job=None):
    T = proj.shape[0]
    tT = _row_tile(T, 512)
    nc = tT // CHUNK

    def body(q_ref, k_ref, v_ref, r_ref, al_ref, wgu_ref, bg_ref, gn_ref, y_ref, st_ref, s_scr):
        @pl.when(pl.program_id(0) == 0)
        def _():
            s_scr[...] = jnp.zeros_like(s_scr)

        _, la, delta = _gla_decay_terms(al_ref, wgu_ref, bg_ref, tT)
        kdec = (k_ref[...].astype(F32) * jnp.exp(delta)).astype(BF16)
        for c in range(nc):
            rows = slice(c * CHUNK, (c + 1) * CHUNK)
            first = slice(c * CHUNK, c * CHUNK + 1)
            dec = jnp.exp(la[first, :] + delta[first, :])
            for h in range(GLA_HEADS):
                kc = slice(h * GLA_DK, (h + 1) * GLA_DK)
                vc = slice(h * GLA_DV, (h + 1) * GLA_DV)
                upd_t = _dot(v_ref[rows, vc], kdec[rows, kc], _TN)
                s_t = s_scr[h] * dec[:, kc] + upd_t
                s_scr[h] = s_t
                st_ref[c, h] = s_t
                qs = (q_ref[rows, kc].astype(F32) * (GLA_DK ** -0.5)).astype(BF16)
                o = _dot(qs, s_t.astype(BF16), _NT)
                on = o * _rms_stats(o) * gn_ref[:, vc]
                rr = r_ref[rows, vc].astype(F32)
                y_ref[rows, vc] = (on * (rr * _sigmoid(rr))).astype(BF16)

    blk = lambda w, j: pl.BlockSpec((tT, w), lambda i: (i, j))
    return _call(
        body, name="gla_fwd", grid=(T // tT,),
        in_specs=[blk(512, 0), blk(512, 1), blk(1024, 1), blk(1024, 2), blk(LANES, 0), _whole(), _whole(), _whole()],
        out_specs=[pl.BlockSpec((tT, GLA_V), lambda i: (i, 0)),
                   pl.BlockSpec((nc, GLA_HEADS, GLA_DV, GLA_DK), lambda i: (i, 0, 0, 0))],
        out_shape=[_sds((T, GLA_V), BF16), _sds((T // CHUNK, GLA_HEADS, GLA_DV, GLA_DK), F32)],
        scratch_shapes=[pltpu.VMEM((GLA_HEADS, GLA_DV, GLA_DK), F32)],
        args=(proj, proj, proj, proj, alow, wgu, b_gate, gn), job=job)


def _sgu_mask():
    i = lax.broadcasted_iota(jnp.int32, (SGU_BLOCK, SGU_BLOCK), 0)
    j = lax.broadcasted_iota(jnp.int32, (SGU_BLOCK, SGU_BLOCK), 1)
    return lax.shift_right_logical(j, 6) <= lax.shift_right_logical(i, 6)


def _sgu_fwd(proj, ln_g, ln_b, w_sp, b_sp_t):
    T = proj.shape[0]
    tT = _row_tile(T, 512)
    nb = tT // SGU_BLOCK

    def body(su_ref, sv_ref, lg_ref, lb_ref, w_ref, b_ref, y_ref):
        mask = _sgu_mask()
        for g in range(SGU_GROUPS):
            gc = slice(g * SGU_DG, (g + 1) * SGU_DG)
            wm = jnp.where(mask, w_ref[g], 0.0).astype(BF16)
            vf = _gelu(sv_ref[:, gc].astype(F32))
            mu = jnp.mean(vf, axis=-1, keepdims=True)
            vc = vf - mu
            rstd = lax.rsqrt(jnp.mean(vc * vc, axis=-1, keepdims=True) + EPS)
            vn = (vc * rstd * lg_ref[:, gc] + lb_ref[:, gc]).astype(BF16)
            u = _gelu(su_ref[:, gc].astype(F32))
            for b in range(nb):
                rows = slice(b * SGU_BLOCK, (b + 1) * SGU_BLOCK)
                mixed = _dot(wm, vn[rows, :]) + b_ref[:, g:g + 1]
                y_ref[rows, gc] = (u[rows, :] * mixed).astype(BF16)

    blk = lambda j: pl.BlockSpec((tT, 1024), lambda i: (i, j))
    res, _ = _call(body, name="sgu_fwd", grid=(T // tT,), parallel=True,
                   in_specs=[blk(3), blk(4), _whole(), _whole(), _whole(), _whole()],
                   out_specs=[pl.BlockSpec((tT, 1024), lambda i: (i, 0))], out_shape=[_sds((T, 1024), BF16)],
                   args=(proj, proj, ln_g, ln_b, w_sp, b_sp_t))
    return res[0]


def _merge_fwd(x, proj, y_gla, y_sgu, w_bg, w_bs, w_o, g_pm):
    T = x.shape[0]
    tT = _row_tile(T, 512)

    def body(x_ref, gg_ref, gs_ref, yg_ref, ys_ref, wbg_ref, wbs_ref, wo_ref, g_ref,
             zg_ref, zs_ref, mg_ref, mix_ref, x1_ref):
        zg = _dot(yg_ref[...], wbg_ref[...])
        zs = _dot(ys_ref[...], wbs_ref[...])
        zg_ref[...] = zg.astype(BF16)
        zs_ref[...] = zs.astype(BF16)
        merged = (_sigmoid(gg_ref[...].astype(F32)) * zg + _sigmoid(gs_ref[...].astype(F32)) * zs).astype(BF16)
        mg_ref[...] = merged
        mix = _dot(merged, wo_ref[...])
        mix_ref[...] = mix
        x1_ref[...] = x_ref[...] + mix * _rms_stats(mix) * g_ref[...]

    row = pl.BlockSpec((tT, D_MODEL), lambda i: (i, 0))
    blk = lambda j: pl.BlockSpec((tT, 1024), lambda i: (i, j))
    sds = lambda dt: _sds((T, D_MODEL), dt)
    res, _ = _call(body, name="merge_fwd", grid=(T // tT,), parallel=True,
                   in_specs=[row, blk(5), blk(6), row, row, _whole(), _whole(), _whole(),
                             pl.BlockSpec((1, D_MODEL), lambda i: (0, 0))],
                   out_specs=[row] * 5, out_shape=[sds(BF16), sds(BF16), sds(BF16), sds(F32), sds(F32)],
                   args=(x, proj, proj, y_gla, y_sgu, w_bg, w_bs, w_o, g_pm))
    return res


def _ffn_fwd_bwd(x1, tgt, w_fi, w_fo, g_pf, g_po):
    T = x1.shape[0]
    tT = _row_tile(T, 256)
    half = D_FF // 2

    def body(x1_ref, t_ref, wfi_ref, wfo_ref, gpf_ref, gpo_ref,
             h_ref, f_ref, dgu_ref, dy_ref, dx1_ref, loss_ref, dgpf_ref, dgpo_ref, gu_scr):
        @pl.when(pl.program_id(0) == 0)
        def _():
            loss_ref[...] = jnp.zeros_like(loss_ref)
            dgpf_ref[...] = jnp.zeros_like(dgpf_ref)
            dgpo_ref[...] = jnp.zeros_like(dgpo_ref)

        x1v = x1_ref[...]
        r2 = _rms_stats(x1v)
        h = (x1v * r2 * gpf_ref[...]).astype(BF16)
        h_ref[...] = h
        y = jnp.zeros((tT, D_MODEL), F32)
        for j in range(2):
            gc = slice(j * half, (j + 1) * half)
            uc = slice(D_FF + j * half, D_FF + (j + 1) * half)
            gate = _dot(h, wfi_ref[:, gc])
            up = _dot(h, wfi_ref[:, uc])
            gu_scr[:, gc] = gate
            gu_scr[:, uc] = up
            f = (gate * _sigmoid(gate) * up).astype(BF16)
            f_ref[:, gc] = f
            y = y + _dot(f, wfo_ref[gc, :])
        r3 = _rms_stats(y)
        x2 = x1v + y * r3 * gpo_ref[...]
        err = x2 - t_ref[...]
        loss_ref[...] += jnp.sum(err * err) * (0.5 / D_MODEL)
        dx2 = err * (1.0 / D_MODEL)
        dy, dg = _rms_bwd(dx2, y, r3, gpo_ref[...])
        dgpo_ref[...] += jnp.sum(dg, axis=0, keepdims=True)
        dyb = dy.astype(BF16)
        dy_ref[...] = dyb
        dh = jnp.zeros((tT, D_MODEL), F32)
        for j in range(2):
            gc = slice(j * half, (j + 1) * half)
            uc = slice(D_FF + j * half, D_FF + (j + 1) * half)
            df = _dot(dyb, wfo_ref[gc, :], _NT)
            gate = gu_scr[:, gc]
            up = gu_scr[:, uc]
            sg = _sigmoid(gate)
            dgate = (df * up * (sg * (1.0 + gate * (1.0 - sg)))).astype(BF16)
            dup = (df * (gate * sg)).astype(BF16)
            dgu_ref[:, gc] = dgate
            dgu_ref[:, uc] = dup
            dh = dh + _dot(dgate, wfi_ref[:, gc], _NT) + _dot(dup, wfi_ref[:, uc], _NT)
        dx1n, dg2 = _rms_bwd(dh, x1v, r2, gpf_ref[...])
        dgpf_ref[...] += jnp.sum(dg2, axis=0, keepdims=True)
        dx1_ref[...] = dx2 + dx1n

    row = lambda w: pl.BlockSpec((tT, w), lambda i: (i, 0))
    vec = pl.BlockSpec((1, D_MODEL), lambda i: (0, 0))
    res, _ = _call(
        body, name="ffn_fwd_bwd", grid=(T // tT,),
        in_specs=[row(D_MODEL), row(D_MODEL), _whole(), _whole(), vec, vec],
        out_specs=[row(D_MODEL), row(D_FF), row(2 * D_FF), row(D_MODEL), row(D_MODEL),
                   pl.BlockSpec((1, LANES), lambda i: (0, 0)), vec, vec],
        out_shape=[_sds((T, D_MODEL), BF16), _sds((T, D_FF), BF16), _sds((T, 2 * D_FF), BF16), _sds((T, D_MODEL), BF16),
                   _sds((T, D_MODEL), F32), _sds((1, LANES), F32), _sds((1, D_MODEL), F32), _sds((1, D_MODEL), F32)],
        scratch_shapes=[pltpu.VMEM((tT, 2 * D_FF), F32)], args=(x1, tgt, w_fi, w_fo, g_pf, g_po))
    return res


def _merge_bwd(dx1, mix, proj, zg, zs, w_bg, w_bs, w_o, g_pm, job=None):
    T = dx1.shape[0]
    tT = _row_tile(T, 512)

    def body(dx1_ref, mix_ref, gg_ref, gs_ref, zg_ref, zs_ref, wbg_ref, wbs_ref, wo_ref, g_ref,
             dmix_ref, dzg_ref, dzs_ref, dgate_ref, dyg_ref, dys_ref, dgpm_ref):
        @pl.when(pl.program_id(0) == 0)
        def _():
            dgpm_ref[...] = jnp.zeros_like(dgpm_ref)

        mix = mix_ref[...]
        dmix, dg = _rms_bwd(dx1_ref[...], mix, _rms_stats(mix), g_ref[...])
        dgpm_ref[...] += jnp.sum(dg, axis=0, keepdims=True)
        dmb = dmix.astype(BF16)
        dmix_ref[...] = dmb
        dmerged = _dot(dmb, wo_ref[...], _NT)
        for k, (gate_ref, z_ref, w_ref, dz_ref, dy_ref) in enumerate((
                (gg_ref, zg_ref, wbg_ref, dzg_ref, dyg_ref), (gs_ref, zs_ref, wbs_ref, dzs_ref, dys_ref))):
            sg = _sigmoid(gate_ref[...].astype(F32))
            dz = (dmerged * sg).astype(BF16)
            dz_ref[...] = dz
            dgate_ref[:, k * 1024:(k + 1) * 1024] = (dmerged * z_ref[...].astype(F32) * (sg * (1.0 - sg))).astype(BF16)
            dy_ref[...] = _dot(dz, w_ref[...], _NT).astype(BF16)

    row = pl.BlockSpec((tT, D_MODEL), lambda i: (i, 0))
    blk = lambda j: pl.BlockSpec((tT, 1024), lambda i: (i, j))
    vec = pl.BlockSpec((1, D_MODEL), lambda i: (0, 0))
    sds = _sds((T, D_MODEL), BF16)
    return _call(
        body, name="merge_bwd", grid=(T // tT,),
        in_specs=[row, row, blk(5), blk(6), row, row, _whole(), _whole(), _whole(), vec],
        out_specs=[row, row, row, pl.BlockSpec((tT, W_MRG), lambda i: (i, 0)), row, row, vec],
        out_shape=[sds, sds, sds, _sds((T, W_MRG), BF16), sds, sds, _sds((1, D_MODEL), F32)],
        args=(dx1, mix, proj, proj, zg, zs, w_bg, w_bs, w_o, g_pm), job=job)


def _sgu_bwd(proj, dy_sgu, ln_g, ln_b, w_sp, b_sp_t, job=None):
    T = proj.shape[0]
    tT = _row_tile(T, 512)
    nb = tT // SGU_BLOCK

    def body(su_ref, sv_ref, dy_ref, lg_ref, lb_ref, w_ref, b_ref, dp_ref, dw_ref, dbt_ref, dlg_ref, dlb_ref):
        @pl.when(pl.program_id(0) == 0)
        def _():
            dw_ref[...] = jnp.zeros_like(dw_ref)
            dbt_ref[...] = jnp.zeros_like(dbt_ref)
            dlg_ref[...] = jnp.zeros_like(dlg_ref)
            dlb_ref[...] = jnp.zeros_like(dlb_ref)

        mask = _sgu_mask()
        lane = lax.broadcasted_iota(jnp.int32, (SGU_BLOCK, LANES), 1)
        for g in range(SGU_GROUPS):
            gc = slice(g * SGU_DG, (g + 1) * SGU_DG)
            gc_v = slice(1024 + g * SGU_DG, 1024 + (g + 1) * SGU_DG)
            wm = jnp.where(mask, w_ref[g], 0.0).astype(BF16)
            vf, dvf_dsv = _gelu_and_grad(sv_ref[:, gc].astype(F32))
            mu = jnp.mean(vf, axis=-1, keepdims=True)
            vc = vf - mu
            rstd = lax.rsqrt(jnp.mean(vc * vc, axis=-1, keepdims=True) + EPS)
            vhat = vc * rstd
            vn = (vhat * lg_ref[:, gc] + lb_ref[:, gc]).astype(BF16)
            u, du_dsu = _gelu_and_grad(su_ref[:, gc].astype(F32))
            dy = dy_ref[:, gc].astype(F32)
            dmixed = (dy * u).astype(BF16)
            dvn_parts = []
            dw_acc = jnp.zeros((SGU_BLOCK, SGU_BLOCK), F32)
            db_acc = jnp.zeros((SGU_BLOCK, 1), F32)
            for b in range(nb):
                rows = slice(b * SGU_BLOCK, (b + 1) * SGU_BLOCK)
                mixed = _dot(wm, vn[rows, :]) + b_ref[:, g:g + 1]
                dp_ref[rows, gc] = (dy[rows, :] * mixed * du_dsu[rows, :]).astype(BF16)
                dvn_parts.append(_dot(wm, dmixed[rows, :], _TN))
                dw_acc = dw_acc + _dot(dmixed[rows, :], vn[rows, :], _NT)
                db_acc = db_acc + jnp.sum(dmixed[rows, :].astype(F32), axis=-1, keepdims=True)
            dw_ref[g] += jnp.where(mask, dw_acc, 0.0)
            dbt_ref[...] += jnp.where(lane == g, db_acc, 0.0)
            dvn = jnp.concatenate(dvn_parts, axis=0)
            dlg_ref[:, gc] += jnp.sum(dvn * vhat, axis=0, keepdims=True)
            dlb_ref[:, gc] += jnp.sum(dvn, axis=0, keepdims=True)
            dvh = dvn * lg_ref[:, gc]
            dvf = rstd * (dvh - jnp.mean(dvh, axis=-1, keepdims=True)
                          - vhat * jnp.mean(dvh * vhat, axis=-1, keepdims=True))
            dp_ref[:, gc_v] = (dvf * dvf_dsv).astype(BF16)

    blk = lambda j: pl.BlockSpec((tT, 1024), lambda i: (i, j))
    row = lambda w: pl.BlockSpec((tT, w), lambda i: (i, 0))
    vec = pl.BlockSpec((1, 1024), lambda i: (0, 0))
    return _call(
        body, name="sgu_bwd", grid=(T // tT,),
        in_specs=[blk(3), blk(4), row(1024), _whole(), _whole(), _whole(), _whole()],
        out_specs=[row(W_SGU), pl.BlockSpec((SGU_GROUPS, SGU_BLOCK, SGU_BLOCK), lambda i: (0, 0, 0)),
                   pl.BlockSpec((SGU_BLOCK, LANES), lambda i: (0, 0)), vec, vec],
        out_shape=[_sds((T, W_SGU), BF16), _sds((SGU_GROUPS, SGU_BLOCK, SGU_BLOCK), F32), _sds((SGU_BLOCK, LANES), F32),
                   _sds((1, 1024), F32), _sds((1, 1024), F32)],
        args=(proj, proj, dy_sgu, ln_g, ln_b, w_sp, b_sp_t), job=job)


def _gla_bwd(proj, alow, wgu, b_gate, gn, states, dy_gla, job=None):
    T = proj.shape[0]
    tT = _row_tile(T, 512)
    nc = tT // CHUNK
    nt = T // tT

    def body(q_ref, k_ref, v_ref, r_ref, al_ref, wgu_ref, bg_ref, gn_ref, st_ref, sp_ref, dy_ref,
             dp_ref, dal_ref, dgn_ref, dbg_ref, dwgu_ref, g_scr, dd_scr, dt_scr):
        step = pl.program_id(0)

        @pl.when(step == 0)
        def _():
            g_scr[...] = jnp.zeros_like(g_scr)
            dgn_ref[...] = jnp.zeros_like(dgn_ref)
            dbg_ref[...] = jnp.zeros_like(dbg_ref)
            dwgu_ref[...] = jnp.zeros_like(dwgu_ref)

        has_prev = jnp.where(step == nt - 1, 0.0, 1.0)
        logit, la, delta = _gla_decay_terms(al_ref, wgu_ref, bg_ref, tT)
        e = jnp.exp(delta)
        kdec_f = k_ref[...].astype(F32) * e
        kdec = kdec_f.astype(BF16)
        for c in reversed(range(nc)):
            rows = slice(c * CHUNK, (c + 1) * CHUNK)
            first = slice(c * CHUNK, c * CHUNK + 1)
            dec = jnp.exp(la[first, :] + delta[first, :])
            for h in range(GLA_HEADS):
                kc = slice(h * GLA_DK, (h + 1) * GLA_DK)
                vc = slice(h * GLA_DV, (h + 1) * GLA_DV)
                dq_c = slice(OFF_Q + h * GLA_DK, OFF_Q + (h + 1) * GLA_DK)
                dk_c = slice(OFF_K + h * GLA_DK, OFF_K + (h + 1) * GLA_DK)
                dv_c = slice(OFF_V + h * GLA_DV, OFF_V + (h + 1) * GLA_DV)
                dr_c = slice(OFF_R + h * GLA_DV, OFF_R + (h + 1) * GLA_DV)
                s_t = st_ref[c, h]
                s_prev = st_ref[c - 1, h] if c > 0 else sp_ref[0, h] * has_prev
                qs = (q_ref[rows, kc].astype(F32) * (GLA_DK ** -0.5)).astype(BF16)
                s_b = s_t.astype(BF16)
                o = _dot(qs, s_b, _NT)
                rstd = _rms_stats(o)
                ohat = o * rstd
                gnh = gn_ref[:, vc]
                dy = dy_ref[rows, vc].astype(F32)
                rr = r_ref[rows, vc].astype(F32)
                sg = _sigmoid(rr)
                don = dy * (rr * sg)
                dp_ref[rows, dr_c] = (dy * (ohat * gnh) * (sg * (1.0 + rr * (1.0 - sg)))).astype(BF16)
                dgn_ref[:, vc] += jnp.sum(don * ohat, axis=0, keepdims=True)
                dn = don * gnh
                do = (rstd * (dn - ohat * jnp.mean(dn * ohat, axis=-1, keepdims=True))).astype(BF16)
                dp_ref[rows, dq_c] = (_dot(do, s_b) * (GLA_DK ** -0.5)).astype(BF16)
                g_t = _dot(do, qs, _TN) + g_scr[h]
                g_b = g_t.astype(BF16)
                dp_ref[rows, dv_c] = _dot(kdec[rows, kc], g_b, _NT).astype(BF16)
                dkdec = _dot(v_ref[rows, vc], g_b)
                ddec = jnp.sum(g_t * s_prev, axis=0, keepdims=True)
                dp_ref[rows, dk_c] = (dkdec * e[rows, kc]).astype(BF16)
                dd_scr[rows, kc] = dkdec * kdec_f[rows, kc]
                dt_scr[rows, kc] = jnp.broadcast_to(ddec * dec[:, kc], (CHUNK, GLA_DK))
                g_scr[h] = g_t * dec[:, kc]
        dla = _dot_exact_lhs(_chunk_masks(tT, upper=False), dd_scr[...]) + dt_scr[...]
        dlogit = dla * (1.0 / GLA_TAU) * _sigmoid(-logit)
        dbg_ref[...] += jnp.sum(dlogit, axis=0, keepdims=True)
        dwgu_ref[...] += _dot_f32(al_ref[...], dlogit, _TN)
        dal_ref[...] = _dot_f32(dlogit, wgu_ref[...], _NT).astype(BF16)

    rev = lambda i: nt - 1 - i
    blk = lambda w, j: pl.BlockSpec((tT, w), lambda i: (rev(i), j))
    st_blk = pl.BlockSpec((nc, GLA_HEADS, GLA_DV, GLA_DK), lambda i: (rev(i), 0, 0, 0))
    sp_blk = pl.BlockSpec((1, GLA_HEADS, GLA_DV, GLA_DK), lambda i: (jnp.maximum(rev(i) * nc - 1, 0), 0, 0, 0))
    return _call(
        body, name="gla_bwd", grid=(nt,),
        in_specs=[blk(512, 0), blk(512, 1), blk(1024, 1), blk(1024, 2), blk(LANES, 0), _whole(), _whole(), _whole(),
                  st_blk, sp_blk, blk(GLA_V, 0)],
        out_specs=[blk(W_GLA, 0), blk(LANES, 0), pl.BlockSpec((1, GLA_V), lambda i: (0, 0)),
                   pl.BlockSpec((1, GLA_QK), lambda i: (0, 0)), pl.BlockSpec((LANES, GLA_QK), lambda i: (0, 0))],
        out_shape=[_sds((T, W_GLA), BF16), _sds((T, LANES), BF16), _sds((1, GLA_V), F32), _sds((1, GLA_QK), F32),
                   _sds((LANES, GLA_QK), F32)],
        scratch_shapes=[pltpu.VMEM((GLA_HEADS, GLA_DV, GLA_DK), F32), pltpu.VMEM((tT, GLA_QK), F32),
                        pltpu.VMEM((tT, GLA_QK), F32)],
        args=(proj, proj, proj, proj, alow, wgu, b_gate, gn, states, states, dy_gla), job=job)


def _inproj_bwd(x, dx1, g1, w_all, dparts, job=None):
    T = x.shape[0]
    tT = _row_tile(T, 512)
    offs = (0, W_GLA, W_GLA + W_SGU, N_MAIN)

    def body(x_ref, dx1_ref, g_ref, w_ref, *rest):
        part_refs, (dx_ref, dg_ref) = rest[:len(offs)], rest[len(offs):]

        @pl.when(pl.program_id(0) == 0)
        def _():
            dg_ref[...] = jnp.zeros_like(dg_ref)

        da = jnp.zeros((tT, D_MODEL), F32)
        for off, p_ref in zip(offs, part_refs):
            da = da + _dot(p_ref[...], w_ref[:, off:off + p_ref.shape[1]], _NT)
        xv = x_ref[...]
        dx, dg = _rms_bwd(da, xv, _rms_stats(xv), g_ref[...])
        dg_ref[...] += jnp.sum(dg, axis=0, keepdims=True)
        dx_ref[...] = dx1_ref[...] + dx

    row = lambda w: pl.BlockSpec((tT, w), lambda i: (i, 0))
    vec = pl.BlockSpec((1, D_MODEL), lambda i: (0, 0))
    return _call(
        body, name="inproj_bwd", grid=(T // tT,),
        in_specs=[row(D_MODEL), row(D_MODEL), vec, _whole()] + [row(p.shape[1]) for p in dparts],
        out_specs=[row(D_MODEL), vec], out_shape=[_sds((T, D_MODEL), F32), _sds((1, D_MODEL), F32)],
        args=(x, dx1, g1, w_all, *dparts), job=job)


def _tn_matmul(a, b, name, job=None):
    T, M = a.shape
    N = b.shape[1]
    tk = _row_tile(T, 1024)
    tm = M if M <= 1024 else 1408
    tn = N if N <= 1024 else (1024 if N % 1024 == 0 else 1408)
    assert M % tm == 0 and N % tn == 0

    def body(a_ref, b_ref, o_ref):
        @pl.when(pl.program_id(2) == 0)
        def _():
            o_ref[...] = jnp.zeros_like(o_ref)

        o_ref[...] += _dot(a_ref[...], b_ref[...], _TN)

    res, jres = _call(
        body, name=name, grid=(M // tm, N // tn, T // tk),
        in_specs=[pl.BlockSpec((tk, tm), lambda i, j, k: (k, i)), pl.BlockSpec((tk, tn), lambda i, j, k: (k, j))],
        out_specs=[pl.BlockSpec((tm, tn), lambda i, j, k: (i, j))], out_shape=[_sds((M, N), F32)], args=(a, b), job=job)
    return res[0], jres


def _pad_rows(a, rows=8):
    return jnp.pad(a, ((0, rows - a.shape[0]), (0, LANES - a.shape[1])))


def _halves_view(dw):
    r = dw.shape[0] // N_CHIPS
    return dw.reshape(N_CHIPS, 2, r // 2, dw.shape[1])


def kernel(x, norm_pre_mix, w_in, w_gate_up, b_gate, gla_norm, sgu_ln_g, sgu_ln_b, w_spatial, b_spatial, w_branch_gla, w_branch_sgu, w_out, norm_post_mix, norm_pre_ffn, w_ffn_in, w_ffn_out, norm_post_ffn, loss_target, m_norm_pre_mix, m_w_in, m_w_gate_up, m_b_gate, m_gla_norm, m_sgu_ln_g, m_sgu_ln_b, m_w_spatial, m_b_spatial, m_w_branch_gla, m_w_branch_sgu, m_w_out, m_norm_post_mix, m_norm_pre_ffn, m_w_ffn_in, m_w_ffn_out, m_norm_post_ffn, v_norm_pre_mix, v_w_in, v_w_gate_up, v_b_gate, v_gla_norm, v_sgu_ln_g, v_sgu_ln_b, v_w_spatial, v_b_spatial, v_w_branch_gla, v_w_branch_sgu, v_w_out, v_norm_post_mix, v_norm_pre_ffn, v_w_ffn_in, v_w_ffn_out, v_norm_post_ffn):
    chip = 2 * lax.axis_index("x") + lax.axis_index("y")
    xt, tgt = x[0], loss_target[0]

    tiny = jnp.concatenate([w_gate_up[0], _pad_rows(gla_norm[0]), _pad_rows(sgu_ln_g[0]), _pad_rows(sgu_ln_b[0]),
                            jnp.zeros((8, LANES), F32)], axis=0)
    def with_own(gathered, own):
        if gathered.ndim == 3:
            return lax.dynamic_update_slice(gathered, own[None], (chip, 0, 0))
        return lax.dynamic_update_slice(gathered, own, (0, chip * own.shape[1]))

    w_in_b = w_in[0].astype(BF16)
    g_in, g_tiny = _run_job(_job_gather([w_in_b, tiny], [False, False]), "gather_w_in")
    g_tiny = with_own(g_tiny, tiny)
    w_all = _relayout_w_in(with_own(g_in, w_in_b))
    cols = lambda a: a.transpose(1, 0, 2).reshape(a.shape[1], N_CHIPS * a.shape[2])
    wgu = jnp.pad(cols(g_tiny[:, 0:16]), ((0, LANES - GLA_RANK), (0, 0)))
    gn = cols(g_tiny[:, 16:20, :64]).reshape(1, GLA_V)
    ln_g = cols(g_tiny[:, 24:28, :64]).reshape(1, 1024)
    ln_b = cols(g_tiny[:, 32:36, :64]).reshape(1, 1024)
    b_sp_t = jnp.pad(b_spatial[0].T, ((0, 0), (0, LANES - SGU_GROUPS)))
    w_sp = w_spatial[0]

    own_rows = [w_branch_gla[0].astype(BF16), w_branch_sgu[0].astype(BF16), w_out[0].astype(BF16), w_ffn_out[0].astype(BF16)]
    (a, proj, alow), g_rows = _inproj_fwd(xt, norm_pre_mix, w_all, job=_job_gather(own_rows, [False] * 4))
    rows = lambda g: g.reshape(N_CHIPS * g.shape[1], g.shape[2])
    w_bg, w_bs, w_o, w_fo = [rows(with_own(g, own)) for g, own in zip(g_rows, own_rows)]
    w_fi_b = w_ffn_in[0].astype(BF16)
    (y_gla, states), (g_fi,) = _gla_fwd(proj, alow, wgu, b_gate, gn, job=_job_gather([w_fi_b], [True]))
    w_fi = with_own(g_fi, w_fi_b)
    y_sgu = _sgu_fwd(proj, ln_g, ln_b, w_sp, b_sp_t)
    zg, zs, merged, mix, x1 = _merge_fwd(xt, proj, y_gla, y_sgu, w_bg, w_bs, w_o, norm_post_mix)
    h, f, dgu, dy, dx1, loss, d_gpf, d_gpo = _ffn_fwd_bwd(x1, tgt, w_fi, w_fo, norm_pre_ffn, norm_post_ffn)

    own_part = lambda c: lax.dynamic_index_in_dim(c, chip, 0, keepdims=False)
    whole = lambda hs: [[(h_, None)] for h_ in hs]
    dw_fo, _ = _tn_matmul(f, dy, "dw_ffn_out")
    dw_fo4 = _halves_view(dw_fo)
    dw_fi, (q_fo,) = _tn_matmul(h, dgu, "dw_ffn_in", job=_job_to_other_core([[(dw_fo4, 0)]]))
    c_fo = _presum(dw_fo4, q_fo, "presum_ffn_out")
    (dmix, dzg, dzs, dp_mrg, dyg, dys, d_gpm), (s_fo, q_fi) = _merge_bwd(
        dx1, mix, proj, zg, zs, w_bg, w_bs, w_o, norm_post_mix,
        job=_join(_job_scatter([c_fo]), _job_to_other_core([[(dw_fi, 0)]])))
    c_fi = _presum(dw_fi, q_fi, "presum_ffn_in")
    dw_o4 = _halves_view(_tn_matmul(merged, dmix, "dw_out")[0])
    dw_bg4 = _halves_view(_tn_matmul(y_gla, dzg, "dw_branch_gla")[0])
    dw_bs4 = _halves_view(_tn_matmul(y_sgu, dzs, "dw_branch_sgu")[0])
    (dp_sgu, d_wsp, d_bsp_t, d_lng, d_lnb), (s_fi, q_o, q_bg, q_bs) = _sgu_bwd(
        proj, dys, ln_g, ln_b, w_sp, b_sp_t,
        job=_join(_job_scatter([c_fi]), _job_to_other_core([[(dw_o4, 0)], [(dw_bg4, 0)], [(dw_bs4, 0)]])))
    c_o, c_bg, c_bs = (_presum(dw_o4, q_o, "presum_out"), _presum(dw_bg4, q_bg, "presum_branch_gla"),
                       _presum(dw_bs4, q_bs, "presum_branch_sgu"))
    h_fo = _sum_slots(own_part(c_fo), s_fo, "sum_ffn_out")
    (dp_gla, dal, d_gn, d_bg, d_wgu), (s_o, s_bg, s_bs, t_fo) = _gla_bwd(
        proj, alow, wgu, b_gate, gn, states, dyg,
        job=_join(_job_scatter([c_o, c_bg, c_bs]), _job_to_other_core(whole([h_fo]))))
    h_fi, h_o, h_bg, h_bs = (_sum_slots(own_part(c_fi), s_fi, "sum_ffn_in"), _sum_slots(own_part(c_o), s_o, "sum_out"),
                             _sum_slots(own_part(c_bg), s_bg, "sum_branch_gla"),
                             _sum_slots(own_part(c_bs), s_bs, "sum_branch_sgu"))
    dw_a, (t_fi, t_o, t_bg, t_bs) = _tn_matmul(a, dp_gla, "dw_in_gla",
                                               job=_job_to_other_core(whole([h_fi, h_o, h_bg, h_bs])))
    dw_b, _ = _tn_matmul(a, dp_sgu, "dw_in_sgu")
    dw_c, _ = _tn_matmul(a, dp_mrg, "dw_in_merge")
    dw_d, _ = _tn_matmul(a, dal, "dw_in_gate")

    grads, deltas, new_m, new_v = {}, {}, {}, {}

    def update(name, w, m, v, g_mine, g_theirs, job=None):
        (g, d, m2, v2), jres = _adamw(w[0], m[0], v[0], g_mine, g_theirs, "adamw_" + name, job=job)
        grads[name], deltas[name], new_m[name], new_v[name] = g[None], d[None], m2[None], v2[None]
        return jres

    dw_in = [(dw_a, 0), (dw_b, W_GLA), (dw_c, W_GLA + W_SGU), (dw_d, N_MAIN)]
    update("w_ffn_out", w_ffn_out, m_w_ffn_out, v_w_ffn_out, h_fo, t_fo)
    (q_in,) = update("w_ffn_in", w_ffn_in, m_w_ffn_in, v_w_ffn_in, h_fi, t_fi, job=_job_to_other_core([dw_in]))
    update("w_out", w_out, m_w_out, v_w_out, h_o, t_o)
    update("w_branch_gla", w_branch_gla, m_w_branch_gla, v_w_branch_gla, h_bg, t_bg)
    update("w_branch_sgu", w_branch_sgu, m_w_branch_sgu, v_w_branch_sgu, h_bs, t_bs)
    c_in = _presum_w_in(dw_in, q_in)
    (grad_x, d_g1), (s_in,) = _inproj_bwd(xt, dx1, norm_pre_mix, w_all, (dp_gla, dp_sgu, dp_mrg, dal),
                                          job=_job_scatter([c_in]))
    h_in = _sum_slots(own_part(c_in), s_in, "sum_w_in")
    (t_in,) = _run_job(_job_to_other_core(whole([h_in])), "swap_w_in")
    update("w_in", w_in, m_w_in, v_w_in, h_in, t_in)

    pack = jnp.concatenate([
        d_wsp.reshape(512, LANES), d_wgu[:GLA_RANK].reshape(64, LANES),
        d_g1.reshape(8, LANES), d_gpm.reshape(8, LANES), d_gpf.reshape(8, LANES), d_gpo.reshape(8, LANES),
        _pad_rows(d_bg.reshape(4, LANES)), _pad_rows(d_bsp_t[:, :SGU_GROUPS].T),
        d_gn.reshape(8, LANES), d_lng.reshape(8, LANES), d_lnb.reshape(8, LANES), _pad_rows(loss)], axis=0)
    tot = _allreduce_small(pack)
    loss_out = tot[648, 0]
    shard64 = lambda r0: _pad_rows(lax.dynamic_slice(tot[r0:r0 + 8].reshape(4, 256), (0, chip * 64), (4, 64)))
    g_small = jnp.concatenate([
        tot[0:512], lax.dynamic_slice(tot[512:576].reshape(16, 512), (0, chip * LANES), (16, LANES)),
        tot[576:608], tot[608:624], shard64(624), shard64(632), shard64(640)], axis=0)

    def small_pack(w_sp_, wg, n1, n2, n3, n4, bg, bs, gl, lg, lb):
        return jnp.concatenate([
            w_sp_[0].reshape(512, LANES), wg[0], n1.reshape(8, LANES), n2.reshape(8, LANES), n3.reshape(8, LANES),
            n4.reshape(8, LANES), _pad_rows(bg.reshape(4, LANES)), _pad_rows(bs[0]),
            _pad_rows(gl[0]), _pad_rows(lg[0]), _pad_rows(lb[0])], axis=0)

    sp_w = small_pack(w_spatial, w_gate_up, norm_pre_mix, norm_post_mix, norm_pre_ffn, norm_post_ffn, b_gate,
                      b_spatial, gla_norm, sgu_ln_g, sgu_ln_b)
    sp_m = small_pack(m_w_spatial, m_w_gate_up, m_norm_pre_mix, m_norm_post_mix, m_norm_pre_ffn, m_norm_post_ffn,
                      m_b_gate, m_b_spatial, m_gla_norm, m_sgu_ln_g, m_sgu_ln_b)
    sp_v = small_pack(v_w_spatial, v_w_gate_up, v_norm_pre_mix, v_norm_post_mix, v_norm_pre_ffn, v_norm_post_ffn,
                      v_b_gate, v_b_spatial, v_gla_norm, v_sgu_ln_g, v_sgu_ln_b)
    (_, sd, sm, sv), _ = _adamw(sp_w, sp_m, sp_v, g_small, None, "adamw_small")

    def unpack(p):
        return dict(
            w_spatial=p[0:512].reshape(1, 4, 128, 128), w_gate_up=p[512:528].reshape(1, 16, 128),
            norm_pre_mix=p[528:536].reshape(1, 1024), norm_post_mix=p[536:544].reshape(1, 1024),
            norm_pre_ffn=p[544:552].reshape(1, 1024), norm_post_ffn=p[552:560].reshape(1, 1024),
            b_gate=p[560:564].reshape(1, 512), b_spatial=p[568:572].reshape(1, 4, 128),
            gla_norm=p[576:580, :64].reshape(1, 4, 64), sgu_ln_g=p[584:588, :64].reshape(1, 4, 64),
            sgu_ln_b=p[592:596, :64].reshape(1, 4, 64))

    for store, p in ((grads, g_small), (deltas, sd), (new_m, sm), (new_v, sv)):
        store.update(unpack(p))

    order = ["norm_pre_mix", "w_in", "w_gate_up", "b_gate", "gla_norm", "sgu_ln_g", "sgu_ln_b", "w_spatial", "b_spatial",
             "w_branch_gla", "w_branch_sgu", "w_out", "norm_post_mix", "norm_pre_ffn", "w_ffn_in", "w_ffn_out",
             "norm_post_ffn"]
    out = [loss_out, grad_x[None]]
    for store in (grads, deltas, new_m, new_v):
        out.extend(store[n] for n in order)
    return tuple(out)
```

```python
import jax
import jax.numpy as jnp
from jax import lax
from jax.experimental import pallas as pl
from jax.experimental.pallas import tpu as pltpu

F32 = jnp.float32
BF16 = jnp.bfloat16

D_MODEL = 1024
GLA_HEADS = 4
GLA_DK = 128
GLA_DV = 256
GLA_QK = GLA_HEADS * GLA_DK
GLA_V = GLA_HEADS * GLA_DV
GLA_RANK = 16
GLA_TAU = 16.0
CHUNK = 64
SGU_GROUPS = 4
SGU_BLOCK = 128
SGU_DG = 256
D_FF = 2816
EPS = 1e-6
LANES = 128

OFF_Q, OFF_K, OFF_V, OFF_R, OFF_SU, OFF_SV, OFF_GG, OFF_GS, OFF_AL = 0, 512, 1024, 2048, 3072, 4096, 5120, 6144, 7168
W_GLA, W_SGU, W_MRG = 3072, 2048, 2048
N_MAIN = 7168
N_ALL = N_MAIN + LANES
_IN_SPLITS = (GLA_QK, GLA_QK, GLA_V, GLA_V, GLA_RANK, 1024, 1024, 1024, 1024)
_IN_STARTS = tuple(sum(_IN_SPLITS[:i]) for i in range(len(_IN_SPLITS) + 1))
_IN_DST = (OFF_Q, OFF_K, OFF_V, OFF_R, OFF_AL, OFF_SU, OFF_SV, OFF_GG, OFF_GS)
D_IN = _IN_STARTS[-1]

ADAM_LR = 0.001
ADAM_B1 = 0.9
ADAM_B2 = 0.999
ADAM_EPS = 1e-08
ADAM_WD = 0.01
ADAM_STEP = 10

VMEM_LIMIT_BYTES = 56 * 1024 * 1024
N_CHIPS = 4
N_PEER = N_CHIPS - 1
N_DEV = 8
MESH = pl.DeviceIdType.MESH

_NN = (((1,), (0,)), ((), ()))
_NT = (((1,), (1,)), ((), ()))
_TN = (((0,), (0,)), ((), ()))


def _dot(a, b, dims=_NN):
    return lax.dot_general(a, b, dims, preferred_element_type=F32)


def _split(x):
    hi = x.astype(BF16)
    lo = (x - hi.astype(F32)).astype(BF16)
    return hi, lo


def _dot_f32(a, b, dims=_NN):
    ah, al = _split(a)
    bh, bl = _split(b)
    return _dot(ah, bh, dims) + (_dot(al, bh, dims) + _dot(ah, bl, dims))


def _dot_exact_lhs(m, x):
    xh, xl = _split(x)
    return _dot(m, xh) + _dot(m, xl)


def _sigmoid(x):
    return 1.0 / (1.0 + jnp.exp(-x))


def _log_sigmoid(x):
    return jnp.minimum(x, 0.0) - jnp.log(1.0 + jnp.exp(-jnp.abs(x)))


_GELU_C = 0.7978845608028654
_GELU_A = 0.044715


def _gelu_and_grad(x):
    x2 = x * x
    t = jnp.tanh(_GELU_C * (x + _GELU_A * x * x2))
    g = 0.5 * x * (1.0 + t)
    dg = 0.5 * (1.0 + t) + 0.5 * x * (1.0 - t * t) * (_GELU_C * (1.0 + 3.0 * _GELU_A * x2))
    return g, dg


def _gelu(x):
    t = jnp.tanh(_GELU_C * (x + _GELU_A * x * x * x))
    return 0.5 * x * (1.0 + t)


def _rms_stats(x):
    return lax.rsqrt(jnp.mean(x * x, axis=-1, keepdims=True) + EPS)


def _rms_bwd(dout, y, r, g):
    yhat = y * r
    dn = dout * g
    dy = r * (dn - yhat * jnp.mean(dn * yhat, axis=-1, keepdims=True))
    return dy, dout * yhat


def _whole():
    return pl.BlockSpec(memory_space=pltpu.VMEM)


def _row_tile(T, want):
    t = min(T, want)
    assert T % t == 0
    return t


def _chunk_masks(tT, upper):
    row = lax.broadcasted_iota(jnp.int32, (tT, tT), 0)
    col = lax.broadcasted_iota(jnp.int32, (tT, tT), 1)
    same = lax.shift_right_logical(row, 6) == lax.shift_right_logical(col, 6)
    tri = (col > row) if upper else (col < row)
    return jnp.where(same & tri, 1.0, 0.0).astype(BF16)


class _Job:
    def __init__(self, ins, out_shapes, scratch, start, finish):
        self.ins, self.out_shapes, self.scratch, self.start, self.finish = list(ins), list(out_shapes), list(scratch), start, finish


def _join(*jobs):
    def split(refs, counts):
        out, at = [], 0
        for n in counts:
            out.append(refs[at:at + n])
            at += n
        return out

    ni, no, ns = [len(j.ins) for j in jobs], [len(j.out_shapes) for j in jobs], [len(j.scratch) for j in jobs]

    def start(ins, outs, scr):
        for j, a, b, c in zip(jobs, split(ins, ni), split(outs, no), split(scr, ns)):
            j.start(a, b, c)

    def finish(ins, outs, scr):
        for j, a, b, c in zip(jobs, split(ins, ni), split(outs, no), split(scr, ns)):
            j.finish(a, b, c)

    return _Job(sum((j.ins for j in jobs), []), sum((j.out_shapes for j in jobs), []),
                sum((j.scratch for j in jobs), []), start, finish)


def _mesh_pos():
    return lax.axis_index("x"), lax.axis_index("y"), lax.axis_index("c")


def _peer_chips(xi, yi):
    return [(1 - xi, yi), (xi, 1 - yi), (1 - xi, 1 - yi)]


def _half(ci, rows):
    return pl.ds(pl.multiple_of(ci * rows, 8), rows)


def _sds(shape, dtype):
    return jax.ShapeDtypeStruct(tuple(shape), dtype)


def _job_gather(arrs, by_cols):
    n = len(arrs)

    def dst(o, k, chip, rows):
        if by_cols[k]:
            c = arrs[k].shape[1]
            return o.at[rows, pl.ds(pl.multiple_of(chip * c, LANES), c)]
        return o.at[chip, rows]

    def copies(ins, outs, scr, want):
        ici_send, ici_recv, d2d_send, d2d_recv = scr
        xi, yi, ci = _mesh_pos()
        me = 2 * xi + yi
        res = []
        for k in range(n):
            r = arrs[k].shape[0]
            mine, other = _half(ci, r // 2), _half(1 - ci, r // 2)
            for j, (px, py) in enumerate(_peer_chips(xi, yi)):
                s = k * N_PEER + j
                pc = 2 * px + py
                ici = dict(send_sem=ici_send.at[s], recv_sem=ici_recv.at[s], device_id=(px, py, ci), device_id_type=MESH)
                d2d = dict(send_sem=d2d_send.at[s], recv_sem=d2d_recv.at[s], device_id=(xi, yi, 1 - ci),
                           device_id_type=MESH)
                made = {}
                if "send" in want:
                    made["send"] = pltpu.make_async_remote_copy(
                        src_ref=ins[k].at[mine], dst_ref=dst(outs[k], k, me, mine), **ici)
                if "arrive" in want:
                    made["arrive"] = pltpu.make_async_remote_copy(
                        src_ref=ins[k].at[mine], dst_ref=dst(outs[k], k, pc, mine), **ici)
                if "forward" in want:
                    made["forward"] = pltpu.make_async_remote_copy(
                        src_ref=dst(outs[k], k, pc, mine), dst_ref=dst(outs[k], k, pc, mine), **d2d)
                if "handed" in want:
                    made["handed"] = pltpu.make_async_remote_copy(
                        src_ref=dst(outs[k], k, pc, other), dst_ref=dst(outs[k], k, pc, other), **d2d)
                res.append(made)
        return res

    def start(ins, outs, scr):
        for cp in copies(ins, outs, scr, ("send",)):
            cp["send"].start()

    def finish(ins, outs, scr):
        for cp in copies(ins, outs, scr, ("arrive", "forward")):
            cp["arrive"].wait_recv()
            cp["forward"].start()
        for cp in copies(ins, outs, scr, ("handed", "send", "forward")):
            cp["handed"].wait_recv()
            cp["send"].wait_send()
            cp["forward"].wait_send()

    shapes = [_sds((a.shape[0], N_CHIPS * a.shape[1]) if bc else (N_CHIPS,) + a.shape, a.dtype)
              for a, bc in zip(arrs, by_cols)]
    dma = pltpu.SemaphoreType.DMA
    return _Job(arrs, shapes, [dma((n * N_PEER,))] * 4, start, finish)


def _job_scatter(parts):
    n = len(parts)

    def copies(ins, outs, scr):
        send_sems, recv_sems = scr
        xi, yi, ci = _mesh_pos()
        res = []
        for k in range(n):
            for j, (px, py) in enumerate(_peer_chips(xi, yi)):
                s = k * N_PEER + j
                res.append(pltpu.make_async_remote_copy(
                    src_ref=ins[k].at[2 * px + py], dst_ref=outs[k].at[j], send_sem=send_sems.at[s],
                    recv_sem=recv_sems.at[s], device_id=(px, py, ci), device_id_type=MESH))
        return res

    def start(ins, outs, scr):
        for cp in copies(ins, outs, scr):
            cp.start()

    def finish(ins, outs, scr):
        for cp in copies(ins, outs, scr):
            cp.wait_recv()
            cp.wait_send()

    dma = pltpu.SemaphoreType.DMA
    return _Job(parts, [_sds((N_PEER,) + p.shape[1:], p.dtype) for p in parts], [dma((n * N_PEER,))] * 2, start, finish)


def _job_to_other_core(groups):
    pieces = [(g, a, off) for g, group in enumerate(groups) for a, off in group]
    n = len(pieces)

    def geometry(group):
        a0, off0 = group[0]
        if off0 is None:
            return a0.shape
        if a0.ndim == 4:
            return (N_CHIPS, a0.shape[2], a0.shape[3])
        return (a0.shape[0] // 2, sum(a.shape[1] for a, _ in group))

    def copies(ins, outs, scr):
        send_sems, recv_sems = scr
        xi, yi, ci = _mesh_pos()
        res = []
        for p, (g, a, off) in enumerate(pieces):
            if off is None:
                give, land = ins[p], outs[g]
            elif a.ndim == 4:
                give, land = ins[p].at[pl.ds(0, N_CHIPS), 1 - ci], outs[g]
            else:
                hr, w = a.shape[0] // 2, a.shape[1]
                give, land = ins[p].at[_half(1 - ci, hr)], outs[g].at[pl.ds(0, hr), pl.ds(off, w)]
            res.append(pltpu.make_async_remote_copy(
                src_ref=give, dst_ref=land, send_sem=send_sems.at[p], recv_sem=recv_sems.at[p],
                device_id=(xi, yi, 1 - ci), device_id_type=MESH))
        return res

    def start(ins, outs, scr):
        for cp in copies(ins, outs, scr):
            cp.start()

    def finish(ins, outs, scr):
        for cp in copies(ins, outs, scr):
            cp.wait_recv()
            cp.wait_send()

    dma = pltpu.SemaphoreType.DMA
    return _Job([a for _, a, _ in pieces], [_sds(geometry(group), group[0][0].dtype) for group in groups],
                [dma((n,))] * 2, start, finish)


def _call(body, *, name, grid, in_specs, out_specs, out_shape, args, scratch_shapes=(), parallel=False, job=None):
    n_in, n_out, n_scr = len(in_specs), len(out_specs), len(scratch_shapes)
    if job is None:
        sem = ("parallel" if parallel else "arbitrary",) * len(grid)
        res = pl.pallas_call(
            body, name=name, grid=grid, in_specs=in_specs, out_specs=out_specs, out_shape=out_shape,
            scratch_shapes=list(scratch_shapes),
            compiler_params=pltpu.CompilerParams(dimension_semantics=sem, vmem_limit_bytes=VMEM_LIMIT_BYTES))(*args)
        return list(res), []
    n_ji, n_jo = len(job.ins), len(job.out_shapes)

    def carried(*refs):
        ins, refs = refs[:n_in], refs[n_in:]
        j_ins, refs = refs[:n_ji], refs[n_ji:]
        outs, refs = refs[:n_out], refs[n_out:]
        j_outs, refs = refs[:n_jo], refs[n_jo:]
        scr, j_scr = refs[:n_scr], refs[n_scr:]
        ids = [pl.program_id(d) for d in range(len(grid))]
        first = ids[0] == 0
        last = ids[0] == grid[0] - 1
        for d in range(1, len(grid)):
            first = first & (ids[d] == 0)
            last = last & (ids[d] == grid[d] - 1)

        @pl.when(first)
        def _():
            job.start(j_ins, j_outs, j_scr)

        body(*ins, *outs, *scr)

        @pl.when(last)
        def _():
            job.finish(j_ins, j_outs, j_scr)

    hbm = pl.BlockSpec(memory_space=pl.ANY)
    res = pl.pallas_call(
        carried, name=name, grid=grid, in_specs=list(in_specs) + [hbm] * n_ji, out_specs=list(out_specs) + [hbm] * n_jo,
        out_shape=list(out_shape) + job.out_shapes, scratch_shapes=list(scratch_shapes) + job.scratch,
        compiler_params=pltpu.CompilerParams(dimension_semantics=("arbitrary",) * len(grid),
                                             vmem_limit_bytes=VMEM_LIMIT_BYTES))(*args, *job.ins)
    return list(res[:n_out]), list(res[n_out:])


def _run_job(job, name):
    n_i, n_o = len(job.ins), len(job.out_shapes)

    def body(*refs):
        ins, outs, scr = refs[:n_i], refs[n_i:n_i + n_o], refs[n_i + n_o:]
        job.start(ins, outs, scr)
        job.finish(ins, outs, scr)

    hbm = pl.BlockSpec(memory_space=pl.ANY)
    return list(pl.pallas_call(body, name=name, in_specs=[hbm] * n_i, out_specs=[hbm] * n_o, out_shape=job.out_shapes,
                               scratch_shapes=job.scratch)(*job.ins))


def _adam_values(w, m, v, g):
    m2 = ADAM_B1 * m + (1.0 - ADAM_B1) * g
    v2 = ADAM_B2 * v + (1.0 - ADAM_B2) * (g * g)
    delta = -ADAM_LR * ((m2 / (1.0 - ADAM_B1 ** ADAM_STEP)) / (jnp.sqrt(v2 / (1.0 - ADAM_B2 ** ADAM_STEP)) + ADAM_EPS)
                        + ADAM_WD * w)
    return delta, m2, v2


_P_WSP, _P_WGU, _P_NORM, _P_BG, _P_BSP, _P_HEAD, _P_LOSS, _P_ROWS = 0, 512, 576, 608, 616, 624, 720, 728


def _small_update(dgrads, ws, ms, vs):
    n = len(ws)

    def body(*refs):
        dwsp, dwgu, dg1, dgpm, dgpf, dgpo, dbg, dbspt, dgn, dlng, dlnb, loss_in = refs[:12]
        w_refs, m_refs, v_refs = refs[12:12 + n], refs[12 + n:12 + 2 * n], refs[12 + 2 * n:12 + 3 * n]
        loss_out = refs[12 + 3 * n]
        outs = refs[13 + 3 * n:13 + 7 * n]
        pack, slots, tot, send_sems, recv_sems = refs[13 + 7 * n:]
        xi, yi, ci = _mesh_pos()
        me = 4 * xi + 2 * yi + ci
        chip = 2 * xi + yi

        pack[...] = jnp.zeros_like(pack)
        for g in range(SGU_GROUPS):
            pack[_P_WSP + g * SGU_BLOCK:_P_WSP + (g + 1) * SGU_BLOCK] = dwsp[g]
        for j in range(N_CHIPS):
            pack[_P_WGU + GLA_RANK * j:_P_WGU + GLA_RANK * (j + 1)] = dwgu[0:GLA_RANK, LANES * j:LANES * (j + 1)]
        for k, r in enumerate((dg1, dgpm, dgpf, dgpo)):
            for q in range(8):
                pack[_P_NORM + 8 * k + q:_P_NORM + 8 * k + q + 1] = r[:, LANES * q:LANES * (q + 1)]
        for q in range(4):
            pack[_P_BG + q:_P_BG + q + 1] = dbg[:, LANES * q:LANES * (q + 1)]
        pack[_P_BSP:_P_BSP + SGU_GROUPS] = jnp.transpose(dbspt[...])[0:SGU_GROUPS]
        for k, r in enumerate((dgn, dlng, dlnb)):
            for j in range(N_CHIPS):
                for hh in range(4):
                    row = _P_HEAD + 32 * k + 8 * j + hh
                    pack[row:row + 1, 0:64] = r[:, 256 * hh + 64 * j:256 * hh + 64 * (j + 1)]
        pack[_P_LOSS:_P_LOSS + 1] = loss_in[...]

        slots[me] = pack[...]

        def copy(r, slot):
            peer = (jnp.bitwise_xor(xi, (r >> 2) & 1), jnp.bitwise_xor(yi, (r >> 1) & 1), jnp.bitwise_xor(ci, r & 1))
            return pltpu.make_async_remote_copy(
                src_ref=pack, dst_ref=slots.at[slot(peer)], send_sem=send_sems.at[r - 1],
                recv_sem=recv_sems.at[r - 1], device_id=peer, device_id_type=MESH)

        sends = [copy(r, lambda peer: me) for r in range(1, N_DEV)]
        for cp in sends:
            cp.start()
        for r in range(1, N_DEV):
            copy(r, lambda peer: 4 * peer[0] + 2 * peer[1] + peer[2]).wait_recv()
        for cp in sends:
            cp.wait_send()
        acc = slots[0]
        for d in range(1, N_DEV):
            acc = acc + slots[d]
        tot[...] = acc
        loss_out[...] = tot[_P_LOSS:_P_LOSS + 1, 0:1]

        def step(k, g, pick, put):
            d, m2, v2 = _adam_values(pick(w_refs[k]), pick(m_refs[k]), pick(v_refs[k]), g)
            for o, val in zip((outs[k], outs[n + k], outs[2 * n + k], outs[3 * n + k]), (g, d, m2, v2)):
                put(o, val)

        def whole(ref):
            return ref[0]

        def put_whole(ref, val):
            ref[0] = val

        for g in range(SGU_GROUPS):
            def pick_g(ref, g=g):
                return ref[0, g]

            def put_g(ref, val, g=g):
                ref[0, g] = val

            step(0, tot[_P_WSP + g * SGU_BLOCK:_P_WSP + (g + 1) * SGU_BLOCK], pick_g, put_g)
        step(1, tot[pl.ds(pl.multiple_of(_P_WGU + GLA_RANK * chip, GLA_RANK), GLA_RANK), :], whole, put_whole)
        for k, (base, chunks) in enumerate(((_P_NORM, 8), (_P_NORM + 8, 8), (_P_NORM + 16, 8), (_P_NORM + 24, 8), (_P_BG, 4))):
            for q in range(chunks):
                def pick_q(ref, q=q):
                    return ref[:, LANES * q:LANES * (q + 1)]

                def put_q(ref, val, q=q):
                    ref[:, LANES * q:LANES * (q + 1)] = val

                step(2 + k, tot[base + q:base + q + 1], pick_q, put_q)
        step(7, tot[_P_BSP:_P_BSP + SGU_GROUPS], whole, put_whole)
        for k in range(3):
            mine = tot[pl.ds(pl.multiple_of(_P_HEAD + 32 * k + 8 * chip, 8), 8), :]
            step(8 + k, mine[0:4, 0:64], whole, put_whole)

    shapes = [_sds(w.shape, F32) for w in ws]
    res = pl.pallas_call(
        body, name="small_update", in_specs=[_whole()] * (12 + 3 * n), out_specs=[_whole()] * (1 + 4 * n),
        out_shape=[_sds((1, 1), F32)] + shapes * 4,
        scratch_shapes=[pltpu.VMEM((_P_ROWS, LANES), F32), pltpu.VMEM((N_DEV, _P_ROWS, LANES), F32),
                        pltpu.VMEM((_P_ROWS, LANES), F32), pltpu.SemaphoreType.DMA((N_DEV - 1,)),
                        pltpu.SemaphoreType.DMA((N_DEV - 1,))],
        compiler_params=pltpu.CompilerParams(vmem_limit_bytes=VMEM_LIMIT_BYTES),
    )(*dgrads, *ws, *ms, *vs)
    return res[0].reshape(()), [list(res[1 + i * n:1 + (i + 1) * n]) for i in range(4)]


def _w_in_pieces():
    blk = D_IN // N_CHIPS
    pieces = []
    for s in range(len(_IN_SPLITS)):
        lo_s, hi_s = _IN_STARTS[s], _IN_STARTS[s + 1]
        for j in range(N_CHIPS):
            lo, hi = max(lo_s, j * blk), min(hi_s, (j + 1) * blk)
            if lo < hi:
                pieces.append((j, lo - j * blk, _IN_DST[s] + lo - lo_s, hi - lo))
    return pieces


def _relayout_w_in(gathered):
    _, rows, blk = gathered.shape
    tr = 256

    def body(g_ref, o_ref):
        o_ref[:, OFF_AL:N_ALL] = jnp.zeros((tr, LANES), BF16)
        for j, src, dst, w in _w_in_pieces():
            o_ref[:, dst:dst + w] = g_ref[j, :, src:src + w]

    res, _ = _call(body, name="relayout_w_in", grid=(rows // tr,), parallel=True,
                   in_specs=[pl.BlockSpec((N_CHIPS, tr, blk), lambda i: (0, i, 0))],
                   out_specs=[pl.BlockSpec((tr, N_ALL), lambda i: (i, 0))],
                   out_shape=[_sds((rows, N_ALL), BF16)], args=(gathered,))
    return res[0]


def _update_row_tile(rows):
    for t in range(min(rows, 256), 7, -8):
        if rows % t == 0:
            return t
    return rows


def _my_half(first_ref, second_ref):
    return jnp.where(lax.axis_index("c") == 0, first_ref[...], second_ref[...])


def _presum_w_in(dws, theirs):
    hr = theirs.shape[0]
    blk = D_IN // N_CHIPS
    tr = 128
    nh = hr // tr

    def body(*refs):
        dw_refs, (q_ref, o_ref, s_scr) = refs[:2 * len(dws)], refs[2 * len(dws):]
        for p, (a, off) in enumerate(dws):
            w = a.shape[1]
            s_scr[:, off:off + w] = (_my_half(dw_refs[2 * p], dw_refs[2 * p + 1]) + q_ref[:, off:off + w]).astype(BF16)
        for j, src, dst, w in _w_in_pieces():
            o_ref[j, :, src:src + w] = s_scr[:, dst:dst + w]

    in_specs, args = [], []
    for a, _ in dws:
        w = a.shape[1]
        in_specs += [pl.BlockSpec((tr, w), lambda i: (i, 0)), pl.BlockSpec((tr, w), lambda i: (i + nh, 0))]
        args += [a, a]
    res, _ = _call(body, name="presum_w_in", grid=(nh,), parallel=True,
                   in_specs=in_specs + [pl.BlockSpec((tr, N_ALL), lambda i: (i, 0))],
                   out_specs=[pl.BlockSpec((N_CHIPS, tr, blk), lambda i: (0, i, 0))],
                   out_shape=[_sds((N_CHIPS, hr, blk), BF16)], scratch_shapes=[pltpu.VMEM((tr, N_ALL), BF16)],
                   args=(*args, theirs))
    return res[0]


def _presum(dw, theirs, name):
    if dw.ndim == 4:
        _, _, hr, c = dw.shape
        tr = _update_row_tile(hr)
        first = pl.BlockSpec((1, 1, tr, c), lambda j, i: (j, 0, i, 0))
        second = pl.BlockSpec((1, 1, tr, c), lambda j, i: (j, 1, i, 0))
        other = pl.BlockSpec((1, tr, c), lambda j, i: (j, i, 0))
    else:
        hr, c = dw.shape[0] // 2, dw.shape[1] // N_CHIPS
        tr = _update_row_tile(hr)
        nh = hr // tr
        first = pl.BlockSpec((tr, c), lambda j, i: (i, j))
        second = pl.BlockSpec((tr, c), lambda j, i: (i + nh, j))
        other = pl.BlockSpec((tr, c), lambda j, i: (i, j))

    def body(a_ref, b_ref, q_ref, o_ref):
        mine = _my_half(a_ref, b_ref).reshape(tr, c)
        o_ref[...] = (mine + q_ref[...].reshape(tr, c)).astype(BF16).reshape(o_ref.shape)

    res, _ = _call(body, name=name, grid=(N_CHIPS, hr // tr), parallel=True, in_specs=[first, second, other],
                   out_specs=[pl.BlockSpec((1, tr, c), lambda j, i: (j, i, 0))],
                   out_shape=[_sds((N_CHIPS, hr, c), BF16)], args=(dw, dw, theirs))
    return res[0]


def _sum_slots(own, slots, name):
    rows, cols = own.shape
    tr = _update_row_tile(rows)

    def body(own_ref, s_ref, o_ref):
        acc = own_ref[...].astype(F32)
        for j in range(N_PEER):
            acc = acc + s_ref[j].astype(F32)
        o_ref[...] = acc

    res, _ = _call(body, name=name, grid=(rows // tr,), parallel=True,
                   in_specs=[pl.BlockSpec((tr, cols), lambda i: (i, 0)), pl.BlockSpec((N_PEER, tr, cols), lambda i: (0, i, 0))],
                   out_specs=[pl.BlockSpec((tr, cols), lambda i: (i, 0))], out_shape=[_sds((rows, cols), F32)],
                   args=(own, slots))
    return res[0]


def _adamw(w, m, v, g_mine, g_theirs, name, job=None):
    rows, cols = w.shape
    halves = g_theirs is not None
    tr = _update_row_tile(rows // 2 if halves else rows)
    nh = (rows // 2) // tr if halves else rows // tr
    bc1 = 1.0 - ADAM_B1 ** ADAM_STEP
    bc2 = 1.0 - ADAM_B2 ** ADAM_STEP

    def body(w_ref, m_ref, v_ref, *rest):
        g_refs, (g_out, d_out, m_out, v_out) = rest[:-4], rest[-4:]
        if halves:
            mine_here = (pl.program_id(0) // nh) == lax.axis_index("c")
            g = jnp.where(mine_here, g_refs[0][...], g_refs[1][...])
        else:
            g = g_refs[0][...]
        m2 = ADAM_B1 * m_ref[...] + (1.0 - ADAM_B1) * g
        v2 = ADAM_B2 * v_ref[...] + (1.0 - ADAM_B2) * (g * g)
        g_out[...] = g
        m_out[...] = m2
        v_out[...] = v2
        d_out[...] = -ADAM_LR * ((m2 / bc1) / (jnp.sqrt(v2 / bc2) + ADAM_EPS) + ADAM_WD * w_ref[...])

    spec = pl.BlockSpec((tr, cols), lambda i: (i, 0))
    g_spec = pl.BlockSpec((tr, cols), lambda i: (i % nh, 0))
    g_args = (g_mine, g_theirs) if halves else (g_mine,)
    return _call(body, name=name, grid=(rows // tr,), parallel=True, in_specs=[spec] * 3 + [g_spec] * len(g_args),
                 out_specs=[spec] * 4, out_shape=[_sds((rows, cols), F32)] * 4, args=(w, m, v, *g_args), job=job)


def _inproj_fwd(x, g1, w_all, job=None):
    T = x.shape[0]
    tT = _row_tile(T, 512)

    def body(x_ref, g_ref, w_ref, a_ref, proj_ref, alow_ref):
        xv = x_ref[...]
        a = (xv * _rms_stats(xv) * g_ref[...]).astype(BF16)
        a_ref[...] = a
        for j in range(N_MAIN // 1024):
            cols = slice(j * 1024, (j + 1) * 1024)
            proj_ref[:, cols] = _dot(a, w_ref[:, cols]).astype(BF16)
        alow_ref[...] = _dot(a, w_ref[:, N_MAIN:N_ALL])

    row = lambda w: pl.BlockSpec((tT, w), lambda i: (i, 0))
    return _call(
        body, name="inproj_fwd", grid=(T // tT,), parallel=True,
        in_specs=[row(D_MODEL), pl.BlockSpec((1, D_MODEL), lambda i: (0, 0)), _whole()],
        out_specs=[row(D_MODEL), row(N_MAIN), row(LANES)],
        out_shape=[_sds((T, D_MODEL), BF16), _sds((T, N_MAIN), BF16), _sds((T, LANES), F32)],
        args=(x, g1, w_all), job=job)


def _gla_decay_terms(al_ref, wgu_ref, bg_ref, tT):
    logit = _dot_f32(al_ref[...], wgu_ref[...]) + bg_ref[...]
    la = _log_sigmoid(logit) * (1.0 / GLA_TAU)
    delta = _dot_exact_lhs(_chunk_masks(tT, upper=True), la)
    return logit, la, delta


def _gla_fwd(proj, alow, wgu, b_gate, gn, job=None):
    T = proj.shape[0]
    tT = _row_tile(T, 512)
    nc = tT // CHUNK

    def body(q_ref, k_ref, v_ref, r_ref, al_ref, wgu_ref, bg_ref, gn_ref, y_ref, st_ref, s_scr):
        @pl.when(pl.program_id(0) == 0)
        def _():
            s_scr[...] = jnp.zeros_like(s_scr)

        _, la, delta = _gla_decay_terms(al_ref, wgu_ref, bg_ref, tT)
        kdec = (k_ref[...].astype(F32) * jnp.exp(delta)).astype(BF16)
        for c in range(nc):
            rows = slice(c * CHUNK, (c + 1) * CHUNK)
            first = slice(c * CHUNK, c * CHUNK + 1)
            dec = jnp.exp(la[first, :] + delta[first, :])
            for h in range(GLA_HEADS):
                kc = slice(h * GLA_DK, (h + 1) * GLA_DK)
                vc = slice(h * GLA_DV, (h + 1) * GLA_DV)
                upd_t = _dot(v_ref[rows, vc], kdec[rows, kc], _TN)
                s_t = s_scr[h] * dec[:, kc] + upd_t
                s_scr[h] = s_t
                st_ref[c, h] = s_t
                qs = (q_ref[rows, kc].astype(F32) * (GLA_DK ** -0.5)).astype(BF16)
                o = _dot(qs, s_t.astype(BF16), _NT)
                on = o * _rms_stats(o) * gn_ref[:, vc]
                rr = r_ref[rows, vc].astype(F32)
                y_ref[rows, vc] = (on * (rr * _sigmoid(rr))).astype(BF16)

    blk = lambda w, j: pl.BlockSpec((tT, w), lambda i: (i, j))
    return _call(
        body, name="gla_fwd", grid=(T // tT,),
        in_specs=[blk(512, 0), blk(512, 1), blk(1024, 1), blk(1024, 2), blk(LANES, 0), _whole(), _whole(), _whole()],
        out_specs=[pl.BlockSpec((tT, GLA_V), lambda i: (i, 0)),
                   pl.BlockSpec((nc, GLA_HEADS, GLA_DV, GLA_DK), lambda i: (i, 0, 0, 0))],
        out_shape=[_sds((T, GLA_V), BF16), _sds((T // CHUNK, GLA_HEADS, GLA_DV, GLA_DK), F32)],
        scratch_shapes=[pltpu.VMEM((GLA_HEADS, GLA_DV, GLA_DK), F32)],
        args=(proj, proj, proj, proj, alow, wgu, b_gate, gn), job=job)


def _sgu_mask():
    i = lax.broadcasted_iota(jnp.int32, (SGU_BLOCK, SGU_BLOCK), 0)
    j = lax.broadcasted_iota(jnp.int32, (SGU_BLOCK, SGU_BLOCK), 1)
    return lax.shift_right_logical(j, 6) <= lax.shift_right_logical(i, 6)


def _sgu_fwd(proj, ln_g, ln_b, w_sp, b_sp_t):
    T = proj.shape[0]
    tT = _row_tile(T, 512)
    nb = tT // SGU_BLOCK

    def body(su_ref, sv_ref, lg_ref, lb_ref, w_ref, b_ref, y_ref):
        mask = _sgu_mask()
        for g in range(SGU_GROUPS):
            gc = slice(g * SGU_DG, (g + 1) * SGU_DG)
            wm = jnp.where(mask, w_ref[g], 0.0).astype(BF16)
            vf = _gelu(sv_ref[:, gc].astype(F32))
            mu = jnp.mean(vf, axis=-1, keepdims=True)
            vc = vf - mu
            rstd = lax.rsqrt(jnp.mean(vc * vc, axis=-1, keepdims=True) + EPS)
            vn = (vc * rstd * lg_ref[:, gc] + lb_ref[:, gc]).astype(BF16)
            u = _gelu(su_ref[:, gc].astype(F32))
            for b in range(nb):
                rows = slice(b * SGU_BLOCK, (b + 1) * SGU_BLOCK)
                mixed = _dot(wm, vn[rows, :]) + b_ref[:, g:g + 1]
                y_ref[rows, gc] = (u[rows, :] * mixed).astype(BF16)

    blk = lambda j: pl.BlockSpec((tT, 1024), lambda i: (i, j))
    res, _ = _call(body, name="sgu_fwd", grid=(T // tT,), parallel=True,
                   in_specs=[blk(3), blk(4), _whole(), _whole(), _whole(), _whole()],
                   out_specs=[pl.BlockSpec((tT, 1024), lambda i: (i, 0))], out_shape=[_sds((T, 1024), BF16)],
                   args=(proj, proj, ln_g, ln_b, w_sp, b_sp_t))
    return res[0]


def _merge_fwd(x, proj, y_gla, y_sgu, w_bg, w_bs, w_o, g_pm):
    T = x.shape[0]
    tT = _row_tile(T, 512)

    def body(x_ref, gg_ref, gs_ref, yg_ref, ys_ref, wbg_ref, wbs_ref, wo_ref, g_ref,
             zg_ref, zs_ref, mg_ref, mix_ref, x1_ref):
        zg = _dot(yg_ref[...], wbg_ref[...])
        zs = _dot(ys_ref[...], wbs_ref[...])
        zg_ref[...] = zg.astype(BF16)
        zs_ref[...] = zs.astype(BF16)
        merged = (_sigmoid(gg_ref[...].astype(F32)) * zg + _sigmoid(gs_ref[...].astype(F32)) * zs).astype(BF16)
        mg_ref[...] = merged
        mix = _dot(merged, wo_ref[...])
        mix_ref[...] = mix
        x1_ref[...] = x_ref[...] + mix * _rms_stats(mix) * g_ref[...]

    row = pl.BlockSpec((tT, D_MODEL), lambda i: (i, 0))
    blk = lambda j: pl.BlockSpec((tT, 1024), lambda i: (i, j))
    sds = lambda dt: _sds((T, D_MODEL), dt)
    res, _ = _call(body, name="merge_fwd", grid=(T // tT,), parallel=True,
                   in_specs=[row, blk(5), blk(6), row, row, _whole(), _whole(), _whole(),
                             pl.BlockSpec((1, D_MODEL), lambda i: (0, 0))],
                   out_specs=[row] * 5, out_shape=[sds(BF16), sds(BF16), sds(BF16), sds(F32), sds(F32)],
                   args=(x, proj, proj, y_gla, y_sgu, w_bg, w_bs, w_o, g_pm))
    return res


def _ffn_fwd_bwd(x1, tgt, w_fi, w_fo, g_pf, g_po):
    T = x1.shape[0]
    tT = _row_tile(T, 256)
    half = D_FF // 2

    def body(x1_ref, t_ref, wfi_ref, wfo_ref, gpf_ref, gpo_ref,
             h_ref, f_ref, dgu_ref, dy_ref, dx1_ref, loss_ref, dgpf_ref, dgpo_ref, gu_scr):
        @pl.when(pl.program_id(0) == 0)
        def _():
            loss_ref[...] = jnp.zeros_like(loss_ref)
            dgpf_ref[...] = jnp.zeros_like(dgpf_ref)
            dgpo_ref[...] = jnp.zeros_like(dgpo_ref)

        x1v = x1_ref[...]
        r2 = _rms_stats(x1v)
        h = (x1v * r2 * gpf_ref[...]).astype(BF16)
        h_ref[...] = h
        y = jnp.zeros((tT, D_MODEL), F32)
        for j in range(2):
            gc = slice(j * half, (j + 1) * half)
            uc = slice(D_FF + j * half, D_FF + (j + 1) * half)
            gate = _dot(h, wfi_ref[:, gc])
            up = _dot(h, wfi_ref[:, uc])
            gu_scr[:, gc] = gate
            gu_scr[:, uc] = up
            f = (gate * _sigmoid(gate) * up).astype(BF16)
            f_ref[:, gc] = f
            y = y + _dot(f, wfo_ref[gc, :])
        r3 = _rms_stats(y)
        x2 = x1v + y * r3 * gpo_ref[...]
        err = x2 - t_ref[...]
        loss_ref[...] += jnp.sum(err * err) * (0.5 / D_MODEL)
        dx2 = err * (1.0 / D_MODEL)
        dy, dg = _rms_bwd(dx2, y, r3, gpo_ref[...])
        dgpo_ref[...] += jnp.sum(dg, axis=0, keepdims=True)
        dyb = dy.astype(BF16)
        dy_ref[...] = dyb
        dh = jnp.zeros((tT, D_MODEL), F32)
        for j in range(2):
            gc = slice(j * half, (j + 1) * half)
            uc = slice(D_FF + j * half, D_FF + (j + 1) * half)
            df = _dot(dyb, wfo_ref[gc, :], _NT)
            gate = gu_scr[:, gc]
            up = gu_scr[:, uc]
            sg = _sigmoid(gate)
            dgate = (df * up * (sg * (1.0 + gate * (1.0 - sg)))).astype(BF16)
            dup = (df * (gate * sg)).astype(BF16)
            dgu_ref[:, gc] = dgate
            dgu_ref[:, uc] = dup
            dh = dh + _dot(dgate, wfi_ref[:, gc], _NT) + _dot(dup, wfi_ref[:, uc], _NT)
        dx1n, dg2 = _rms_bwd(dh, x1v, r2, gpf_ref[...])
        dgpf_ref[...] += jnp.sum(dg2, axis=0, keepdims=True)
        dx1_ref[...] = dx2 + dx1n

    row = lambda w: pl.BlockSpec((tT, w), lambda i: (i, 0))
    vec = pl.BlockSpec((1, D_MODEL), lambda i: (0, 0))
    res, _ = _call(
        body, name="ffn_fwd_bwd", grid=(T // tT,),
        in_specs=[row(D_MODEL), row(D_MODEL), _whole(), _whole(), vec, vec],
        out_specs=[row(D_MODEL), row(D_FF), row(2 * D_FF), row(D_MODEL), row(D_MODEL),
                   pl.BlockSpec((1, LANES), lambda i: (0, 0)), vec, vec],
        out_shape=[_sds((T, D_MODEL), BF16), _sds((T, D_FF), BF16), _sds((T, 2 * D_FF), BF16), _sds((T, D_MODEL), BF16),
                   _sds((T, D_MODEL), F32), _sds((1, LANES), F32), _sds((1, D_MODEL), F32), _sds((1, D_MODEL), F32)],
        scratch_shapes=[pltpu.VMEM((tT, 2 * D_FF), F32)], args=(x1, tgt, w_fi, w_fo, g_pf, g_po))
    return res


def _merge_bwd(dx1, mix, proj, zg, zs, w_bg, w_bs, w_o, g_pm, job=None):
    T = dx1.shape[0]
    tT = _row_tile(T, 512)

    def body(dx1_ref, mix_ref, gg_ref, gs_ref, zg_ref, zs_ref, wbg_ref, wbs_ref, wo_ref, g_ref,
             dmix_ref, dzg_ref, dzs_ref, dgate_ref, dyg_ref, dys_ref, dgpm_ref):
        @pl.when(pl.program_id(0) == 0)
        def _():
            dgpm_ref[...] = jnp.zeros_like(dgpm_ref)

        mix = mix_ref[...]
        dmix, dg = _rms_bwd(dx1_ref[...], mix, _rms_stats(mix), g_ref[...])
        dgpm_ref[...] += jnp.sum(dg, axis=0, keepdims=True)
        dmb = dmix.astype(BF16)
        dmix_ref[...] = dmb
        dmerged = _dot(dmb, wo_ref[...], _NT)
        for k, (gate_ref, z_ref, w_ref, dz_ref, dy_ref) in enumerate((
                (gg_ref, zg_ref, wbg_ref, dzg_ref, dyg_ref), (gs_ref, zs_ref, wbs_ref, dzs_ref, dys_ref))):
            sg = _sigmoid(gate_ref[...].astype(F32))
            dz = (dmerged * sg).astype(BF16)
            dz_ref[...] = dz
            dgate_ref[:, k * 1024:(k + 1) * 1024] = (dmerged * z_ref[...].astype(F32) * (sg * (1.0 - sg))).astype(BF16)
            dy_ref[...] = _dot(dz, w_ref[...], _NT).astype(BF16)

    row = pl.BlockSpec((tT, D_MODEL), lambda i: (i, 0))
    blk = lambda j: pl.BlockSpec((tT, 1024), lambda i: (i, j))
    vec = pl.BlockSpec((1, D_MODEL), lambda i: (0, 0))
    sds = _sds((T, D_MODEL), BF16)
    return _call(
        body, name="merge_bwd", grid=(T // tT,),
        in_specs=[row, row, blk(5), blk(6), row, row, _whole(), _whole(), _whole(), vec],
        out_specs=[row, row, row, pl.BlockSpec((tT, W_MRG), lambda i: (i, 0)), row, row, vec],
        out_shape=[sds, sds, sds, _sds((T, W_MRG), BF16), sds, sds, _sds((1, D_MODEL), F32)],
        args=(dx1, mix, proj, proj, zg, zs, w_bg, w_bs, w_o, g_pm), job=job)


def _sgu_bwd(proj, dy_sgu, ln_g, ln_b, w_sp, b_sp_t, job=None):
    T = proj.shape[0]
    tT = _row_tile(T, 512)
    nb = tT // SGU_BLOCK

    def body(su_ref, sv_ref, dy_ref, lg_ref, lb_ref, w_ref, b_ref, dp_ref, dw_ref, dbt_ref, dlg_ref, dlb_ref):
        @pl.when(pl.program_id(0) == 0)
        def _():
            dw_ref[...] = jnp.zeros_like(dw_ref)
            dbt_ref[...] = jnp.zeros_like(dbt_ref)
            dlg_ref[...] = jnp.zeros_like(dlg_ref)
            dlb_ref[...] = jnp.zeros_like(dlb_ref)

        mask = _sgu_mask()
        lane = lax.broadcasted_iota(jnp.int32, (SGU_BLOCK, LANES), 1)
        for g in range(SGU_GROUPS):
            gc = slice(g * SGU_DG, (g + 1) * SGU_DG)
            gc_v = slice(1024 + g * SGU_DG, 1024 + (g + 1) * SGU_DG)
            wm = jnp.where(mask, w_ref[g], 0.0).astype(BF16)
            vf, dvf_dsv = _gelu_and_grad(sv_ref[:, gc].astype(F32))
            mu = jnp.mean(vf, axis=-1, keepdims=True)
            vc = vf - mu
            rstd = lax.rsqrt(jnp.mean(vc * vc, axis=-1, keepdims=True) + EPS)
            vhat = vc * rstd
            vn = (vhat * lg_ref[:, gc] + lb_ref[:, gc]).astype(BF16)
            u, du_dsu = _gelu_and_grad(su_ref[:, gc].astype(F32))
            dy = dy_ref[:, gc].astype(F32)
            dmixed = (dy * u).astype(BF16)
            dvn_parts = []
            dw_acc = jnp.zeros((SGU_BLOCK, SGU_BLOCK), F32)
            db_acc = jnp.zeros((SGU_BLOCK, 1), F32)
            for b in range(nb):
                rows = slice(b * SGU_BLOCK, (b + 1) * SGU_BLOCK)
                mixed = _dot(wm, vn[rows, :]) + b_ref[:, g:g + 1]
                dp_ref[rows, gc] = (dy[rows, :] * mixed * du_dsu[rows, :]).astype(BF16)
                dvn_parts.append(_dot(wm, dmixed[rows, :], _TN))
                dw_acc = dw_acc + _dot(dmixed[rows, :], vn[rows, :], _NT)
                db_acc = db_acc + jnp.sum(dmixed[rows, :].astype(F32), axis=-1, keepdims=True)
            dw_ref[g] += jnp.where(mask, dw_acc, 0.0)
            dbt_ref[...] += jnp.where(lane == g, db_acc, 0.0)
            dvn = jnp.concatenate(dvn_parts, axis=0)
            dlg_ref[:, gc] += jnp.sum(dvn * vhat, axis=0, keepdims=True)
            dlb_ref[:, gc] += jnp.sum(dvn, axis=0, keepdims=True)
            dvh = dvn * lg_ref[:, gc]
            dvf = rstd * (dvh - jnp.mean(dvh, axis=-1, keepdims=True)
                          - vhat * jnp.mean(dvh * vhat, axis=-1, keepdims=True))
            dp_ref[:, gc_v] = (dvf * dvf_dsv).astype(BF16)

    blk = lambda j: pl.BlockSpec((tT, 1024), lambda i: (i, j))
    row = lambda w: pl.BlockSpec((tT, w), lambda i: (i, 0))
    vec = pl.BlockSpec((1, 1024), lambda i: (0, 0))
    return _call(
        body, name="sgu_bwd", grid=(T // tT,),
        in_specs=[blk(3), blk(4), row(1024), _whole(), _whole(), _whole(), _whole()],
        out_specs=[row(W_SGU), pl.BlockSpec((SGU_GROUPS, SGU_BLOCK, SGU_BLOCK), lambda i: (0, 0, 0)),
                   pl.BlockSpec((SGU_BLOCK, LANES), lambda i: (0, 0)), vec, vec],
        out_shape=[_sds((T, W_SGU), BF16), _sds((SGU_GROUPS, SGU_BLOCK, SGU_BLOCK), F32), _sds((SGU_BLOCK, LANES), F32),
                   _sds((1, 1024), F32), _sds((1, 1024), F32)],
        args=(proj, proj, dy_sgu, ln_g, ln_b, w_sp, b_sp_t), job=job)


def _gla_bwd(proj, alow, wgu, b_gate, gn, states, dy_gla, job=None):
    T = proj.shape[0]
    tT = _row_tile(T, 512)
    nc = tT // CHUNK
    nt = T // tT

    def body(q_ref, k_ref, v_ref, r_ref, al_ref, wgu_ref, bg_ref, gn_ref, st_ref, sp_ref, dy_ref,
             dp_ref, dal_ref, dgn_ref, dbg_ref, dwgu_ref, g_scr, dd_scr, dt_scr):
        step = pl.program_id(0)

        @pl.when(step == 0)
        def _():
            g_scr[...] = jnp.zeros_like(g_scr)
            dgn_ref[...] = jnp.zeros_like(dgn_ref)
            dbg_ref[...] = jnp.zeros_like(dbg_ref)
            dwgu_ref[...] = jnp.zeros_like(dwgu_ref)

        has_prev = jnp.where(step == nt - 1, 0.0, 1.0)
        logit, la, delta = _gla_decay_terms(al_ref, wgu_ref, bg_ref, tT)
        e = jnp.exp(delta)
        kdec_f = k_ref[...].astype(F32) * e
        kdec = kdec_f.astype(BF16)
        for c in reversed(range(nc)):
            rows = slice(c * CHUNK, (c + 1) * CHUNK)
            first = slice(c * CHUNK, c * CHUNK + 1)
            dec = jnp.exp(la[first, :] + delta[first, :])
            for h in range(GLA_HEADS):
                kc = slice(h * GLA_DK, (h + 1) * GLA_DK)
                vc = slice(h * GLA_DV, (h + 1) * GLA_DV)
                dq_c = slice(OFF_Q + h * GLA_DK, OFF_Q + (h + 1) * GLA_DK)
                dk_c = slice(OFF_K + h * GLA_DK, OFF_K + (h + 1) * GLA_DK)
                dv_c = slice(OFF_V + h * GLA_DV, OFF_V + (h + 1) * GLA_DV)
                dr_c = slice(OFF_R + h * GLA_DV, OFF_R + (h + 1) * GLA_DV)
                s_t = st_ref[c, h]
                s_prev = st_ref[c - 1, h] if c > 0 else sp_ref[0, h] * has_prev
                qs = (q_ref[rows, kc].astype(F32) * (GLA_DK ** -0.5)).astype(BF16)
                s_b = s_t.astype(BF16)
                o = _dot(qs, s_b, _NT)
                rstd = _rms_stats(o)
                ohat = o * rstd
                gnh = gn_ref[:, vc]
                dy = dy_ref[rows, vc].astype(F32)
                rr = r_ref[rows, vc].astype(F32)
                sg = _sigmoid(rr)
                don = dy * (rr * sg)
                dp_ref[rows, dr_c] = (dy * (ohat * gnh) * (sg * (1.0 + rr * (1.0 - sg)))).astype(BF16)
                dgn_ref[:, vc] += jnp.sum(don * ohat, axis=0, keepdims=True)
                dn = don * gnh
                do = (rstd * (dn - ohat * jnp.mean(dn * ohat, axis=-1, keepdims=True))).astype(BF16)
                dp_ref[rows, dq_c] = (_dot(do, s_b) * (GLA_DK ** -0.5)).astype(BF16)
                g_t = _dot(do, qs, _TN) + g_scr[h]
                g_b = g_t.astype(BF16)
                dp_ref[rows, dv_c] = _dot(kdec[rows, kc], g_b, _NT).astype(BF16)
                dkdec = _dot(v_ref[rows, vc], g_b)
                ddec = jnp.sum(g_t * s_prev, axis=0, keepdims=True)
                dp_ref[rows, dk_c] = (dkdec * e[rows, kc]).astype(BF16)
                dd_scr[rows, kc] = dkdec * kdec_f[rows, kc]
                dt_scr[rows, kc] = jnp.broadcast_to(ddec * dec[:, kc], (CHUNK, GLA_DK))
                g_scr[h] = g_t * dec[:, kc]
        dla = _dot_exact_lhs(_chunk_masks(tT, upper=False), dd_scr[...]) + dt_scr[...]
        dlogit = dla * (1.0 / GLA_TAU) * _sigmoid(-logit)
        dbg_ref[...] += jnp.sum(dlogit, axis=0, keepdims=True)
        dwgu_ref[...] += _dot_f32(al_ref[...], dlogit, _TN)
        dal_ref[...] = _dot_f32(dlogit, wgu_ref[...], _NT).astype(BF16)

    rev = lambda i: nt - 1 - i
    blk = lambda w, j: pl.BlockSpec((tT, w), lambda i: (rev(i), j))
    st_blk = pl.BlockSpec((nc, GLA_HEADS, GLA_DV, GLA_DK), lambda i: (rev(i), 0, 0, 0))
    sp_blk = pl.BlockSpec((1, GLA_HEADS, GLA_DV, GLA_DK), lambda i: (jnp.maximum(rev(i) * nc - 1, 0), 0, 0, 0))
    return _call(
        body, name="gla_bwd", grid=(nt,),
        in_specs=[blk(512, 0), blk(512, 1), blk(1024, 1), blk(1024, 2), blk(LANES, 0), _whole(), _whole(), _whole(),
                  st_blk, sp_blk, blk(GLA_V, 0)],
        out_specs=[blk(W_GLA, 0), blk(LANES, 0), pl.BlockSpec((1, GLA_V), lambda i: (0, 0)),
                   pl.BlockSpec((1, GLA_QK), lambda i: (0, 0)), pl.BlockSpec((LANES, GLA_QK), lambda i: (0, 0))],
        out_shape=[_sds((T, W_GLA), BF16), _sds((T, LANES), BF16), _sds((1, GLA_V), F32), _sds((1, GLA_QK), F32),
                   _sds((LANES, GLA_QK), F32)],
        scratch_shapes=[pltpu.VMEM((GLA_HEADS, GLA_DV, GLA_DK), F32), pltpu.VMEM((tT, GLA_QK), F32),
                        pltpu.VMEM((tT, GLA_QK), F32)],
        args=(proj, proj, proj, proj, alow, wgu, b_gate, gn, states, states, dy_gla), job=job)


def _inproj_bwd(x, dx1, g1, w_all, dparts, job=None):
    T = x.shape[0]
    tT = _row_tile(T, 512)
    offs = (0, W_GLA, W_GLA + W_SGU, N_MAIN)

    def body(x_ref, dx1_ref, g_ref, w_ref, *rest):
        part_refs, (dx_ref, dg_ref) = rest[:len(offs)], rest[len(offs):]

        @pl.when(pl.program_id(0) == 0)
        def _():
            dg_ref[...] = jnp.zeros_like(dg_ref)

        da = jnp.zeros((tT, D_MODEL), F32)
        for off, p_ref in zip(offs, part_refs):
            da = da + _dot(p_ref[...], w_ref[:, off:off + p_ref.shape[1]], _NT)
        xv = x_ref[...]
        dx, dg = _rms_bwd(da, xv, _rms_stats(xv), g_ref[...])
        dg_ref[...] += jnp.sum(dg, axis=0, keepdims=True)
        dx_ref[...] = dx1_ref[...] + dx

    row = lambda w: pl.BlockSpec((tT, w), lambda i: (i, 0))
    vec = pl.BlockSpec((1, D_MODEL), lambda i: (0, 0))
    return _call(
        body, name="inproj_bwd", grid=(T // tT,),
        in_specs=[row(D_MODEL), row(D_MODEL), vec, _whole()] + [row(p.shape[1]) for p in dparts],
        out_specs=[row(D_MODEL), vec], out_shape=[_sds((T, D_MODEL), F32), _sds((1, D_MODEL), F32)],
        args=(x, dx1, g1, w_all, *dparts), job=job)


def _tn_matmul(a, b, name, job=None):
    T, M = a.shape
    N = b.shape[1]
    tk = _row_tile(T, 2048)
    tm = M if M <= 1024 else 1408
    tn = N if N <= 1024 else (1024 if N % 1024 == 0 else 1408)
    assert M % tm == 0 and N % tn == 0

    def body(a_ref, b_ref, o_ref):
        @pl.when(pl.program_id(2) == 0)
        def _():
            o_ref[...] = jnp.zeros_like(o_ref)

        o_ref[...] += _dot(a_ref[...], b_ref[...], _TN)

    res, jres = _call(
        body, name=name, grid=(M // tm, N // tn, T // tk),
        in_specs=[pl.BlockSpec((tk, tm), lambda i, j, k: (k, i)), pl.BlockSpec((tk, tn), lambda i, j, k: (k, j))],
        out_specs=[pl.BlockSpec((tm, tn), lambda i, j, k: (i, j))], out_shape=[_sds((M, N), F32)], args=(a, b), job=job)
    return res[0], jres


def _pad_rows(a, rows=8):
    return jnp.pad(a, ((0, rows - a.shape[0]), (0, LANES - a.shape[1])))


def _halves_view(dw):
    r = dw.shape[0] // N_CHIPS
    return dw.reshape(N_CHIPS, 2, r // 2, dw.shape[1])


def kernel(x, norm_pre_mix, w_in, w_gate_up, b_gate, gla_norm, sgu_ln_g, sgu_ln_b, w_spatial, b_spatial, w_branch_gla, w_branch_sgu, w_out, norm_post_mix, norm_pre_ffn, w_ffn_in, w_ffn_out, norm_post_ffn, loss_target, m_norm_pre_mix, m_w_in, m_w_gate_up, m_b_gate, m_gla_norm, m_sgu_ln_g, m_sgu_ln_b, m_w_spatial, m_b_spatial, m_w_branch_gla, m_w_branch_sgu, m_w_out, m_norm_post_mix, m_norm_pre_ffn, m_w_ffn_in, m_w_ffn_out, m_norm_post_ffn, v_norm_pre_mix, v_w_in, v_w_gate_up, v_b_gate, v_gla_norm, v_sgu_ln_g, v_sgu_ln_b, v_w_spatial, v_b_spatial, v_w_branch_gla, v_w_branch_sgu, v_w_out, v_norm_post_mix, v_norm_pre_ffn, v_w_ffn_in, v_w_ffn_out, v_norm_post_ffn):
    chip = 2 * lax.axis_index("x") + lax.axis_index("y")
    xt, tgt = x[0], loss_target[0]

    tiny = jnp.concatenate([w_gate_up[0], _pad_rows(gla_norm[0]), _pad_rows(sgu_ln_g[0]), _pad_rows(sgu_ln_b[0]),
                            jnp.zeros((8, LANES), F32)], axis=0)
    def with_own(gathered, own):
        if gathered.ndim == 3:
            return lax.dynamic_update_slice(gathered, own[None], (chip, 0, 0))
        return lax.dynamic_update_slice(gathered, own, (0, chip * own.shape[1]))

    w_in_b = w_in[0].astype(BF16)
    g_in, g_tiny = _run_job(_job_gather([w_in_b, tiny], [False, False]), "gather_w_in")
    g_tiny = with_own(g_tiny, tiny)
    w_all = _relayout_w_in(with_own(g_in, w_in_b))
    cols = lambda a: a.transpose(1, 0, 2).reshape(a.shape[1], N_CHIPS * a.shape[2])
    wgu = jnp.pad(cols(g_tiny[:, 0:16]), ((0, LANES - GLA_RANK), (0, 0)))
    gn = cols(g_tiny[:, 16:20, :64]).reshape(1, GLA_V)
    ln_g = cols(g_tiny[:, 24:28, :64]).reshape(1, 1024)
    ln_b = cols(g_tiny[:, 32:36, :64]).reshape(1, 1024)
    b_sp_t = jnp.pad(b_spatial[0].T, ((0, 0), (0, LANES - SGU_GROUPS)))
    w_sp = w_spatial[0]

    own_rows = [w_branch_gla[0].astype(BF16), w_branch_sgu[0].astype(BF16), w_out[0].astype(BF16), w_ffn_out[0].astype(BF16)]
    (a, proj, alow), g_rows = _inproj_fwd(xt, norm_pre_mix, w_all, job=_job_gather(own_rows, [False] * 4))
    rows = lambda g: g.reshape(N_CHIPS * g.shape[1], g.shape[2])
    w_bg, w_bs, w_o, w_fo = [rows(with_own(g, own)) for g, own in zip(g_rows, own_rows)]
    w_fi_b = w_ffn_in[0].astype(BF16)
    (y_gla, states), (g_fi,) = _gla_fwd(proj, alow, wgu, b_gate, gn, job=_job_gather([w_fi_b], [True]))
    w_fi = with_own(g_fi, w_fi_b)
    y_sgu = _sgu_fwd(proj, ln_g, ln_b, w_sp, b_sp_t)
    zg, zs, merged, mix, x1 = _merge_fwd(xt, proj, y_gla, y_sgu, w_bg, w_bs, w_o, norm_post_mix)
    h, f, dgu, dy, dx1, loss, d_gpf, d_gpo = _ffn_fwd_bwd(x1, tgt, w_fi, w_fo, norm_pre_ffn, norm_post_ffn)

    own_part = lambda c: lax.dynamic_index_in_dim(c, chip, 0, keepdims=False)
    whole = lambda hs: [[(h_, None)] for h_ in hs]
    dw_fo, _ = _tn_matmul(f, dy, "dw_ffn_out")
    dw_fo4 = _halves_view(dw_fo)
    dw_fi, (q_fo,) = _tn_matmul(h, dgu, "dw_ffn_in", job=_job_to_other_core([[(dw_fo4, 0)]]))
    c_fo = _presum(dw_fo4, q_fo, "presum_ffn_out")
    (dmix, dzg, dzs, dp_mrg, dyg, dys, d_gpm), (s_fo, q_fi) = _merge_bwd(
        dx1, mix, proj, zg, zs, w_bg, w_bs, w_o, norm_post_mix,
        job=_join(_job_scatter([c_fo]), _job_to_other_core([[(dw_fi, 0)]])))
    c_fi = _presum(dw_fi, q_fi, "presum_ffn_in")
    dw_o4 = _halves_view(_tn_matmul(merged, dmix, "dw_out")[0])
    dw_bg4 = _halves_view(_tn_matmul(y_gla, dzg, "dw_branch_gla")[0])
    dw_bs4 = _halves_view(_tn_matmul(y_sgu, dzs, "dw_branch_sgu")[0])
    (dp_sgu, d_wsp, d_bsp_t, d_lng, d_lnb), (s_fi, q_o, q_bg, q_bs) = _sgu_bwd(
        proj, dys, ln_g, ln_b, w_sp, b_sp_t,
        job=_join(_job_scatter([c_fi]), _job_to_other_core([[(dw_o4, 0)], [(dw_bg4, 0)], [(dw_bs4, 0)]])))
    c_o, c_bg, c_bs = (_presum(dw_o4, q_o, "presum_out"), _presum(dw_bg4, q_bg, "presum_branch_gla"),
                       _presum(dw_bs4, q_bs, "presum_branch_sgu"))
    h_fo = _sum_slots(own_part(c_fo), s_fo, "sum_ffn_out")
    (dp_gla, dal, d_gn, d_bg, d_wgu), (s_o, s_bg, s_bs, t_fo) = _gla_bwd(
        proj, alow, wgu, b_gate, gn, states, dyg,
        job=_join(_job_scatter([c_o, c_bg, c_bs]), _job_to_other_core(whole([h_fo]))))
    h_fi, h_o, h_bg, h_bs = (_sum_slots(own_part(c_fi), s_fi, "sum_ffn_in"), _sum_slots(own_part(c_o), s_o, "sum_out"),
                             _sum_slots(own_part(c_bg), s_bg, "sum_branch_gla"),
                             _sum_slots(own_part(c_bs), s_bs, "sum_branch_sgu"))
    dw_a, (t_fi, t_o, t_bg, t_bs) = _tn_matmul(a, dp_gla, "dw_in_gla",
                                               job=_job_to_other_core(whole([h_fi, h_o, h_bg, h_bs])))
    dw_b, _ = _tn_matmul(a, dp_sgu, "dw_in_sgu")
    dw_c, _ = _tn_matmul(a, dp_mrg, "dw_in_merge")
    dw_d, _ = _tn_matmul(a, dal, "dw_in_gate")

    grads, deltas, new_m, new_v = {}, {}, {}, {}

    def update(name, w, m, v, g_mine, g_theirs, job=None):
        (g, d, m2, v2), jres = _adamw(w[0], m[0], v[0], g_mine, g_theirs, "adamw_" + name, job=job)
        grads[name], deltas[name], new_m[name], new_v[name] = g[None], d[None], m2[None], v2[None]
        return jres

    dw_in = [(dw_a, 0), (dw_b, W_GLA), (dw_c, W_GLA + W_SGU), (dw_d, N_MAIN)]
    update("w_ffn_out", w_ffn_out, m_w_ffn_out, v_w_ffn_out, h_fo, t_fo)
    (q_in,) = update("w_ffn_in", w_ffn_in, m_w_ffn_in, v_w_ffn_in, h_fi, t_fi, job=_job_to_other_core([dw_in]))
    update("w_out", w_out, m_w_out, v_w_out, h_o, t_o)
    update("w_branch_gla", w_branch_gla, m_w_branch_gla, v_w_branch_gla, h_bg, t_bg)
    update("w_branch_sgu", w_branch_sgu, m_w_branch_sgu, v_w_branch_sgu, h_bs, t_bs)
    c_in = _presum_w_in(dw_in, q_in)
    (grad_x, d_g1), (s_in,) = _inproj_bwd(xt, dx1, norm_pre_mix, w_all, (dp_gla, dp_sgu, dp_mrg, dal),
                                          job=_job_scatter([c_in]))
    h_in = _sum_slots(own_part(c_in), s_in, "sum_w_in")
    (t_in,) = _run_job(_job_to_other_core(whole([h_in])), "swap_w_in")
    update("w_in", w_in, m_w_in, v_w_in, h_in, t_in)

    small_names = ["w_spatial", "w_gate_up", "norm_pre_mix", "norm_post_mix", "norm_pre_ffn", "norm_post_ffn", "b_gate",
                   "b_spatial", "gla_norm", "sgu_ln_g", "sgu_ln_b"]
    loss_out, small = _small_update(
        [d_wsp, d_wgu, d_g1, d_gpm, d_gpf, d_gpo, d_bg, d_bsp_t, d_gn, d_lng, d_lnb, loss],
        [w_spatial, w_gate_up, norm_pre_mix, norm_post_mix, norm_pre_ffn, norm_post_ffn, b_gate, b_spatial, gla_norm,
         sgu_ln_g, sgu_ln_b],
        [m_w_spatial, m_w_gate_up, m_norm_pre_mix, m_norm_post_mix, m_norm_pre_ffn, m_norm_post_ffn, m_b_gate,
         m_b_spatial, m_gla_norm, m_sgu_ln_g, m_sgu_ln_b],
        [v_w_spatial, v_w_gate_up, v_norm_pre_mix, v_norm_post_mix, v_norm_pre_ffn, v_norm_post_ffn, v_b_gate,
         v_b_spatial, v_gla_norm, v_sgu_ln_g, v_sgu_ln_b])
    for store, vals in zip((grads, deltas, new_m, new_v), small):
        store.update(zip(small_names, vals))

    order = ["norm_pre_mix", "w_in", "w_gate_up", "b_gate", "gla_norm", "sgu_ln_g", "sgu_ln_b", "w_spatial", "b_spatial",
             "w_branch_gla", "w_branch_sgu", "w_out", "norm_post_mix", "norm_pre_ffn", "w_ffn_in", "w_ffn_out",
             "norm_post_ffn"]
    out = [loss_out, grad_x[None]]
    for store in (grads, deltas, new_m, new_v):
        out.extend(store[n] for n in order)
    return tuple(out)
```

```python
import jax
import jax.numpy as jnp
from jax import lax
from jax.experimental import pallas as pl
from jax.experimental.pallas import tpu as pltpu

F32 = jnp.float32
BF16 = jnp.bfloat16

D_MODEL = 1024
GLA_HEADS = 4
GLA_DK = 128
GLA_DV = 256
GLA_QK = GLA_HEADS * GLA_DK
GLA_V = GLA_HEADS * GLA_DV
GLA_RANK = 16
GLA_TAU = 16.0
CHUNK = 64
SGU_GROUPS = 4
SGU_BLOCK = 128
SGU_DG = 256
D_FF = 2816
EPS = 1e-6
LANES = 128

OFF_Q, OFF_K, OFF_V, OFF_R, OFF_SU, OFF_SV, OFF_GG, OFF_GS, OFF_AL = 0, 512, 1024, 2048, 3072, 4096, 5120, 6144, 7168
W_GLA, W_SGU, W_MRG = 3072, 2048, 2048
N_MAIN = 7168
N_ALL = N_MAIN + LANES
_IN_SPLITS = (GLA_QK, GLA_QK, GLA_V, GLA_V, GLA_RANK, 1024, 1024, 1024, 1024)
_IN_STARTS = tuple(sum(_IN_SPLITS[:i]) for i in range(len(_IN_SPLITS) + 1))
_IN_DST = (OFF_Q, OFF_K, OFF_V, OFF_R, OFF_AL, OFF_SU, OFF_SV, OFF_GG, OFF_GS)
D_IN = _IN_STARTS[-1]

ADAM_LR = 0.001
ADAM_B1 = 0.9
ADAM_B2 = 0.999
ADAM_EPS = 1e-08
ADAM_WD = 0.01
ADAM_STEP = 10

VMEM_LIMIT_BYTES = 56 * 1024 * 1024
N_CHIPS = 4
N_PEER = N_CHIPS - 1
N_DEV = 8
MESH = pl.DeviceIdType.MESH

_NN = (((1,), (0,)), ((), ()))
_NT = (((1,), (1,)), ((), ()))
_TN = (((0,), (0,)), ((), ()))


def _dot(a, b, dims=_NN):
    return lax.dot_general(a, b, dims, preferred_element_type=F32)


def _split(x):
    hi = x.astype(BF16)
    lo = (x - hi.astype(F32)).astype(BF16)
    return hi, lo


def _dot_f32(a, b, dims=_NN):
    ah, al = _split(a)
    bh, bl = _split(b)
    return _dot(ah, bh, dims) + (_dot(al, bh, dims) + _dot(ah, bl, dims))


def _dot_exact_lhs(m, x):
    xh, xl = _split(x)
    return _dot(m, xh) + _dot(m, xl)


def _sigmoid(x):
    return 1.0 / (1.0 + jnp.exp(-x))


def _log_sigmoid(x):
    return jnp.minimum(x, 0.0) - jnp.log(1.0 + jnp.exp(-jnp.abs(x)))


_GELU_C = 0.7978845608028654
_GELU_A = 0.044715


def _gelu_and_grad(x):
    x2 = x * x
    t = jnp.tanh(_GELU_C * (x + _GELU_A * x * x2))
    g = 0.5 * x * (1.0 + t)
    dg = 0.5 * (1.0 + t) + 0.5 * x * (1.0 - t * t) * (_GELU_C * (1.0 + 3.0 * _GELU_A * x2))
    return g, dg


def _gelu(x):
    t = jnp.tanh(_GELU_C * (x + _GELU_A * x * x * x))
    return 0.5 * x * (1.0 + t)


def _rms_stats(x):
    return lax.rsqrt(jnp.mean(x * x, axis=-1, keepdims=True) + EPS)


def _rms_bwd(dout, y, r, g):
    yhat = y * r
    dn = dout * g
    dy = r * (dn - yhat * jnp.mean(dn * yhat, axis=-1, keepdims=True))
    return dy, dout * yhat


def _whole():
    return pl.BlockSpec(memory_space=pltpu.VMEM)


def _row_tile(T, want):
    t = min(T, want)
    assert T % t == 0
    return t


def _chunk_masks(tT, upper):
    row = lax.broadcasted_iota(jnp.int32, (tT, tT), 0)
    col = lax.broadcasted_iota(jnp.int32, (tT, tT), 1)
    same = lax.shift_right_logical(row, 6) == lax.shift_right_logical(col, 6)
    tri = (col > row) if upper else (col < row)
    return jnp.where(same & tri, 1.0, 0.0).astype(BF16)


class _Job:
    def __init__(self, ins, out_shapes, scratch, start, finish):
        self.ins, self.out_shapes, self.scratch, self.start, self.finish = list(ins), list(out_shapes), list(scratch), start, finish


def _join(*jobs):
    def split(refs, counts):
        out, at = [], 0
        for n in counts:
            out.append(refs[at:at + n])
            at += n
        return out

    ni, no, ns = [len(j.ins) for j in jobs], [len(j.out_shapes) for j in jobs], [len(j.scratch) for j in jobs]

    def start(ins, outs, scr):
        for j, a, b, c in zip(jobs, split(ins, ni), split(outs, no), split(scr, ns)):
            j.start(a, b, c)

    def finish(ins, outs, scr):
        for j, a, b, c in zip(jobs, split(ins, ni), split(outs, no), split(scr, ns)):
            j.finish(a, b, c)

    return _Job(sum((j.ins for j in jobs), []), sum((j.out_shapes for j in jobs), []),
                sum((j.scratch for j in jobs), []), start, finish)


def _mesh_pos():
    return lax.axis_index("x"), lax.axis_index("y"), lax.axis_index("c")


def _peer_chips(xi, yi):
    return [(1 - xi, yi), (xi, 1 - yi), (1 - xi, 1 - yi)]


def _half(ci, rows):
    return pl.ds(pl.multiple_of(ci * rows, 8), rows)


def _sds(shape, dtype):
    return jax.ShapeDtypeStruct(tuple(shape), dtype)


def _job_gather(arrs, by_cols):
    n = len(arrs)

    def dst(o, k, chip, rows):
        if by_cols[k]:
            c = arrs[k].shape[1]
            return o.at[rows, pl.ds(pl.multiple_of(chip * c, LANES), c)]
        return o.at[chip, rows]

    def copies(ins, outs, scr, want):
        ici_send, ici_recv, d2d_send, d2d_recv = scr
        xi, yi, ci = _mesh_pos()
        me = 2 * xi + yi
        res = []
        for k in range(n):
            r = arrs[k].shape[0]
            mine, other = _half(ci, r // 2), _half(1 - ci, r // 2)
            for j, (px, py) in enumerate(_peer_chips(xi, yi)):
                s = k * N_PEER + j
                pc = 2 * px + py
                ici = dict(send_sem=ici_send.at[s], recv_sem=ici_recv.at[s], device_id=(px, py, ci), device_id_type=MESH)
                d2d = dict(send_sem=d2d_send.at[s], recv_sem=d2d_recv.at[s], device_id=(xi, yi, 1 - ci),
                           device_id_type=MESH)
                made = {}
                if "send" in want:
                    made["send"] = pltpu.make_async_remote_copy(
                        src_ref=ins[k].at[mine], dst_ref=dst(outs[k], k, me, mine), **ici)
                if "arrive" in want:
                    made["arrive"] = pltpu.make_async_remote_copy(
                        src_ref=ins[k].at[mine], dst_ref=dst(outs[k], k, pc, mine), **ici)
                if "forward" in want:
                    made["forward"] = pltpu.make_async_remote_copy(
                        src_ref=dst(outs[k], k, pc, mine), dst_ref=dst(outs[k], k, pc, mine), **d2d)
                if "handed" in want:
                    made["handed"] = pltpu.make_async_remote_copy(
                        src_ref=dst(outs[k], k, pc, other), dst_ref=dst(outs[k], k, pc, other), **d2d)
                res.append(made)
        return res

    def start(ins, outs, scr):
        for cp in copies(ins, outs, scr, ("send",)):
            cp["send"].start()

    def finish(ins, outs, scr):
        for cp in copies(ins, outs, scr, ("arrive", "forward")):
            cp["arrive"].wait_recv()
            cp["forward"].start()
        for cp in copies(ins, outs, scr, ("handed", "send", "forward")):
            cp["handed"].wait_recv()
            cp["send"].wait_send()
            cp["forward"].wait_send()

    shapes = [_sds((a.shape[0], N_CHIPS * a.shape[1]) if bc else (N_CHIPS,) + a.shape, a.dtype)
              for a, bc in zip(arrs, by_cols)]
    dma = pltpu.SemaphoreType.DMA
    return _Job(arrs, shapes, [dma((n * N_PEER,))] * 4, start, finish)


def _job_scatter(parts):
    n = len(parts)

    def copies(ins, outs, scr):
        send_sems, recv_sems = scr
        xi, yi, ci = _mesh_pos()
        res = []
        for k in range(n):
            for j, (px, py) in enumerate(_peer_chips(xi, yi)):
                s = k * N_PEER + j
                res.append(pltpu.make_async_remote_copy(
                    src_ref=ins[k].at[2 * px + py], dst_ref=outs[k].at[j], send_sem=send_sems.at[s],
                    recv_sem=recv_sems.at[s], device_id=(px, py, ci), device_id_type=MESH))
        return res

    def start(ins, outs, scr):
        for cp in copies(ins, outs, scr):
            cp.start()

    def finish(ins, outs, scr):
        for cp in copies(ins, outs, scr):
            cp.wait_recv()
            cp.wait_send()

    dma = pltpu.SemaphoreType.DMA
    return _Job(parts, [_sds((N_PEER,) + p.shape[1:], p.dtype) for p in parts], [dma((n * N_PEER,))] * 2, start, finish)


def _job_to_other_core(groups):
    pieces = [(g, a, off) for g, group in enumerate(groups) for a, off in group]
    n = len(pieces)

    def geometry(group):
        a0, off0 = group[0]
        if off0 is None:
            return a0.shape
        if a0.ndim == 4:
            return (N_CHIPS, a0.shape[2], a0.shape[3])
        return (a0.shape[0] // 2, sum(a.shape[1] for a, _ in group))

    def copies(ins, outs, scr):
        send_sems, recv_sems = scr
        xi, yi, ci = _mesh_pos()
        res = []
        for p, (g, a, off) in enumerate(pieces):
            if off is None:
                give, land = ins[p], outs[g]
            elif a.ndim == 4:
                give, land = ins[p].at[pl.ds(0, N_CHIPS), 1 - ci], outs[g]
            else:
                hr, w = a.shape[0] // 2, a.shape[1]
                give, land = ins[p].at[_half(1 - ci, hr)], outs[g].at[pl.ds(0, hr), pl.ds(off, w)]
            res.append(pltpu.make_async_remote_copy(
                src_ref=give, dst_ref=land, send_sem=send_sems.at[p], recv_sem=recv_sems.at[p],
                device_id=(xi, yi, 1 - ci), device_id_type=MESH))
        return res

    def start(ins, outs, scr):
        for cp in copies(ins, outs, scr):
            cp.start()

    def finish(ins, outs, scr):
        for cp in copies(ins, outs, scr):
            cp.wait_recv()
            cp.wait_send()

    dma = pltpu.SemaphoreType.DMA
    return _Job([a for _, a, _ in pieces], [_sds(geometry(group), group[0][0].dtype) for group in groups],
                [dma((n,))] * 2, start, finish)


def _call(body, *, name, grid, in_specs, out_specs, out_shape, args, scratch_shapes=(), parallel=False, job=None):
    n_in, n_out, n_scr = len(in_specs), len(out_specs), len(scratch_shapes)
    if job is None:
        sem = ("parallel" if parallel else "arbitrary",) * len(grid)
        res = pl.pallas_call(
            body, name=name, grid=grid, in_specs=in_specs, out_specs=out_specs, out_shape=out_shape,
            scratch_shapes=list(scratch_shapes),
            compiler_params=pltpu.CompilerParams(dimension_semantics=sem, vmem_limit_bytes=VMEM_LIMIT_BYTES))(*args)
        return list(res), []
    n_ji, n_jo = len(job.ins), len(job.out_shapes)

    def carried(*refs):
        ins, refs = refs[:n_in], refs[n_in:]
        j_ins, refs = refs[:n_ji], refs[n_ji:]
        outs, refs = refs[:n_out], refs[n_out:]
        j_outs, refs = refs[:n_jo], refs[n_jo:]
        scr, j_scr = refs[:n_scr], refs[n_scr:]
        ids = [pl.program_id(d) for d in range(len(grid))]
        first = ids[0] == 0
        last = ids[0] == grid[0] - 1
        for d in range(1, len(grid)):
            first = first & (ids[d] == 0)
            last = last & (ids[d] == grid[d] - 1)

        @pl.when(first)
        def _():
            job.start(j_ins, j_outs, j_scr)

        body(*ins, *outs, *scr)

        @pl.when(last)
        def _():
            job.finish(j_ins, j_outs, j_scr)

    hbm = pl.BlockSpec(memory_space=pl.ANY)
    res = pl.pallas_call(
        carried, name=name, grid=grid, in_specs=list(in_specs) + [hbm] * n_ji, out_specs=list(out_specs) + [hbm] * n_jo,
        out_shape=list(out_shape) + job.out_shapes, scratch_shapes=list(scratch_shapes) + job.scratch,
        compiler_params=pltpu.CompilerParams(dimension_semantics=("arbitrary",) * len(grid),
                                             vmem_limit_bytes=VMEM_LIMIT_BYTES))(*args, *job.ins)
    return list(res[:n_out]), list(res[n_out:])


def _run_job(job, name):
    n_i, n_o = len(job.ins), len(job.out_shapes)

    def body(*refs):
        ins, outs, scr = refs[:n_i], refs[n_i:n_i + n_o], refs[n_i + n_o:]
        job.start(ins, outs, scr)
        job.finish(ins, outs, scr)

    hbm = pl.BlockSpec(memory_space=pl.ANY)
    return list(pl.pallas_call(body, name=name, in_specs=[hbm] * n_i, out_specs=[hbm] * n_o, out_shape=job.out_shapes,
                               scratch_shapes=job.scratch)(*job.ins))


def _adam_values(w, m, v, g):
    m2 = ADAM_B1 * m + (1.0 - ADAM_B1) * g
    v2 = ADAM_B2 * v + (1.0 - ADAM_B2) * (g * g)
    delta = -ADAM_LR * ((m2 / (1.0 - ADAM_B1 ** ADAM_STEP)) / (jnp.sqrt(v2 / (1.0 - ADAM_B2 ** ADAM_STEP)) + ADAM_EPS)
                        + ADAM_WD * w)
    return delta, m2, v2


_P_WSP, _P_WGU, _P_NORM, _P_BG, _P_BSP, _P_HEAD, _P_LOSS, _P_ROWS = 0, 512, 576, 608, 616, 624, 720, 728


def _small_update(dgrads, ws, ms, vs):
    n = len(ws)

    def body(*refs):
        dwsp, dwgu, dg1, dgpm, dgpf, dgpo, dbg, dbspt, dgn, dlng, dlnb, loss_in = refs[:12]
        w_refs, m_refs, v_refs = refs[12:12 + n], refs[12 + n:12 + 2 * n], refs[12 + 2 * n:12 + 3 * n]
        loss_out = refs[12 + 3 * n]
        outs = refs[13 + 3 * n:13 + 7 * n]
        pack, slots, tot, send_sems, recv_sems = refs[13 + 7 * n:]
        xi, yi, ci = _mesh_pos()
        me = 4 * xi + 2 * yi + ci
        chip = 2 * xi + yi

        pack[...] = jnp.zeros_like(pack)
        for g in range(SGU_GROUPS):
            pack[_P_WSP + g * SGU_BLOCK:_P_WSP + (g + 1) * SGU_BLOCK] = dwsp[g]
        for j in range(N_CHIPS):
            pack[_P_WGU + GLA_RANK * j:_P_WGU + GLA_RANK * (j + 1)] = dwgu[0:GLA_RANK, LANES * j:LANES * (j + 1)]
        for k, r in enumerate((dg1, dgpm, dgpf, dgpo)):
            for q in range(8):
                pack[_P_NORM + 8 * k + q:_P_NORM + 8 * k + q + 1] = r[:, LANES * q:LANES * (q + 1)]
        for q in range(4):
            pack[_P_BG + q:_P_BG + q + 1] = dbg[:, LANES * q:LANES * (q + 1)]
        pack[_P_BSP:_P_BSP + SGU_GROUPS] = jnp.transpose(dbspt[...])[0:SGU_GROUPS]
        for k, r in enumerate((dgn, dlng, dlnb)):
            for j in range(N_CHIPS):
                for hh in range(4):
                    row = _P_HEAD + 32 * k + 8 * j + hh
                    pack[row:row + 1, 0:64] = r[:, 256 * hh + 64 * j:256 * hh + 64 * (j + 1)]
        pack[_P_LOSS:_P_LOSS + 1] = loss_in[...]

        slots[me] = pack[...]

        def copy(r, slot):
            peer = (jnp.bitwise_xor(xi, (r >> 2) & 1), jnp.bitwise_xor(yi, (r >> 1) & 1), jnp.bitwise_xor(ci, r & 1))
            return pltpu.make_async_remote_copy(
                src_ref=pack, dst_ref=slots.at[slot(peer)], send_sem=send_sems.at[r - 1],
                recv_sem=recv_sems.at[r - 1], device_id=peer, device_id_type=MESH)

        sends = [copy(r, lambda peer: me) for r in range(1, N_DEV)]
        for cp in sends:
            cp.start()
        for r in range(1, N_DEV):
            copy(r, lambda peer: 4 * peer[0] + 2 * peer[1] + peer[2]).wait_recv()
        for cp in sends:
            cp.wait_send()
        acc = slots[0]
        for d in range(1, N_DEV):
            acc = acc + slots[d]
        tot[...] = acc
        loss_out[...] = tot[_P_LOSS:_P_LOSS + 1, 0:1]

        def step(k, g, pick, put):
            d, m2, v2 = _adam_values(pick(w_refs[k]), pick(m_refs[k]), pick(v_refs[k]), g)
            for o, val in zip((outs[k], outs[n + k], outs[2 * n + k], outs[3 * n + k]), (g, d, m2, v2)):
                put(o, val)

        def whole(ref):
            return ref[0]

        def put_whole(ref, val):
            ref[0] = val

        for g in range(SGU_GROUPS):
            def pick_g(ref, g=g):
                return ref[0, g]

            def put_g(ref, val, g=g):
                ref[0, g] = val

            step(0, tot[_P_WSP + g * SGU_BLOCK:_P_WSP + (g + 1) * SGU_BLOCK], pick_g, put_g)
        step(1, tot[pl.ds(pl.multiple_of(_P_WGU + GLA_RANK * chip, GLA_RANK), GLA_RANK), :], whole, put_whole)
        for k, (base, chunks) in enumerate(((_P_NORM, 8), (_P_NORM + 8, 8), (_P_NORM + 16, 8), (_P_NORM + 24, 8), (_P_BG, 4))):
            for q in range(chunks):
                def pick_q(ref, q=q):
                    return ref[:, LANES * q:LANES * (q + 1)]

                def put_q(ref, val, q=q):
                    ref[:, LANES * q:LANES * (q + 1)] = val

                step(2 + k, tot[base + q:base + q + 1], pick_q, put_q)
        step(7, tot[_P_BSP:_P_BSP + SGU_GROUPS], whole, put_whole)
        for k in range(3):
            mine = tot[pl.ds(pl.multiple_of(_P_HEAD + 32 * k + 8 * chip, 8), 8), :]
            step(8 + k, mine[0:4, 0:64], whole, put_whole)

    shapes = [_sds(w.shape, F32) for w in ws]
    res = pl.pallas_call(
        body, name="small_update", in_specs=[_whole()] * (12 + 3 * n), out_specs=[_whole()] * (1 + 4 * n),
        out_shape=[_sds((1, 1), F32)] + shapes * 4,
        scratch_shapes=[pltpu.VMEM((_P_ROWS, LANES), F32), pltpu.VMEM((N_DEV, _P_ROWS, LANES), F32),
                        pltpu.VMEM((_P_ROWS, LANES), F32), pltpu.SemaphoreType.DMA((N_DEV - 1,)),
                        pltpu.SemaphoreType.DMA((N_DEV - 1,))],
        compiler_params=pltpu.CompilerParams(vmem_limit_bytes=VMEM_LIMIT_BYTES),
    )(*dgrads, *ws, *ms, *vs)
    return res[0].reshape(()), [list(res[1 + i * n:1 + (i + 1) * n]) for i in range(4)]


def _w_in_pieces():
    blk = D_IN // N_CHIPS
    pieces = []
    for s in range(len(_IN_SPLITS)):
        lo_s, hi_s = _IN_STARTS[s], _IN_STARTS[s + 1]
        for j in range(N_CHIPS):
            lo, hi = max(lo_s, j * blk), min(hi_s, (j + 1) * blk)
            if lo < hi:
                pieces.append((j, lo - j * blk, _IN_DST[s] + lo - lo_s, hi - lo))
    return pieces


def _relayout_w_in(gathered):
    _, rows, blk = gathered.shape
    tr = 256

    def body(g_ref, o_ref):
        o_ref[:, OFF_AL:N_ALL] = jnp.zeros((tr, LANES), BF16)
        for j, src, dst, w in _w_in_pieces():
            o_ref[:, dst:dst + w] = g_ref[j, :, src:src + w]

    res, _ = _call(body, name="relayout_w_in", grid=(rows // tr,), parallel=True,
                   in_specs=[pl.BlockSpec((N_CHIPS, tr, blk), lambda i: (0, i, 0))],
                   out_specs=[pl.BlockSpec((tr, N_ALL), lambda i: (i, 0))],
                   out_shape=[_sds((rows, N_ALL), BF16)], args=(gathered,))
    return res[0]


def _update_row_tile(rows):
    for t in range(min(rows, 256), 7, -8):
        if rows % t == 0:
            return t
    return rows


def _my_half(first_ref, second_ref):
    return jnp.where(lax.axis_index("c") == 0, first_ref[...], second_ref[...])


def _presum_w_in(dws, theirs):
    hr = theirs.shape[0]
    blk = D_IN // N_CHIPS
    tr = 128
    nh = hr // tr

    def body(*refs):
        dw_refs, (q_ref, o_ref, s_scr) = refs[:2 * len(dws)], refs[2 * len(dws):]
        for p, (a, off) in enumerate(dws):
            w = a.shape[1]
            s_scr[:, off:off + w] = (_my_half(dw_refs[2 * p], dw_refs[2 * p + 1]) + q_ref[:, off:off + w]).astype(BF16)
        for j, src, dst, w in _w_in_pieces():
            o_ref[j, :, src:src + w] = s_scr[:, dst:dst + w]

    in_specs, args = [], []
    for a, _ in dws:
        w = a.shape[1]
        in_specs += [pl.BlockSpec((tr, w), lambda i: (i, 0)), pl.BlockSpec((tr, w), lambda i: (i + nh, 0))]
        args += [a, a]
    res, _ = _call(body, name="presum_w_in", grid=(nh,), parallel=True,
                   in_specs=in_specs + [pl.BlockSpec((tr, N_ALL), lambda i: (i, 0))],
                   out_specs=[pl.BlockSpec((N_CHIPS, tr, blk), lambda i: (0, i, 0))],
                   out_shape=[_sds((N_CHIPS, hr, blk), BF16)], scratch_shapes=[pltpu.VMEM((tr, N_ALL), BF16)],
                   args=(*args, theirs))
    return res[0]


def _presum(dw, theirs, name):
    if dw.ndim == 4:
        _, _, hr, c = dw.shape
        tr = _update_row_tile(hr)
        first = pl.BlockSpec((1, 1, tr, c), lambda j, i: (j, 0, i, 0))
        second = pl.BlockSpec((1, 1, tr, c), lambda j, i: (j, 1, i, 0))
        other = pl.BlockSpec((1, tr, c), lambda j, i: (j, i, 0))
    else:
        hr, c = dw.shape[0] // 2, dw.shape[1] // N_CHIPS
        tr = _update_row_tile(hr)
        nh = hr // tr
        first = pl.BlockSpec((tr, c), lambda j, i: (i, j))
        second = pl.BlockSpec((tr, c), lambda j, i: (i + nh, j))
        other = pl.BlockSpec((tr, c), lambda j, i: (i, j))

    def body(a_ref, b_ref, q_ref, o_ref):
        mine = _my_half(a_ref, b_ref).reshape(tr, c)
        o_ref[...] = (mine + q_ref[...].reshape(tr, c)).astype(BF16).reshape(o_ref.shape)

    res, _ = _call(body, name=name, grid=(N_CHIPS, hr // tr), parallel=True, in_specs=[first, second, other],
                   out_specs=[pl.BlockSpec((1, tr, c), lambda j, i: (j, i, 0))],
                   out_shape=[_sds((N_CHIPS, hr, c), BF16)], args=(dw, dw, theirs))
    return res[0]


def _sum_slots(own, slots, name):
    rows, cols = own.shape
    tr = _update_row_tile(rows)

    def body(own_ref, s_ref, o_ref):
        acc = own_ref[...].astype(F32)
        for j in range(N_PEER):
            acc = acc + s_ref[j].astype(F32)
        o_ref[...] = acc

    res, _ = _call(body, name=name, grid=(rows // tr,), parallel=True,
                   in_specs=[pl.BlockSpec((tr, cols), lambda i: (i, 0)), pl.BlockSpec((N_PEER, tr, cols), lambda i: (0, i, 0))],
                   out_specs=[pl.BlockSpec((tr, cols), lambda i: (i, 0))], out_shape=[_sds((rows, cols), F32)],
                   args=(own, slots))
    return res[0]


def _adamw(w, m, v, g_mine, g_theirs, name, job=None):
    rows, cols = w.shape
    halves = g_theirs is not None
    tr = _update_row_tile(rows // 2 if halves else rows)
    nh = (rows // 2) // tr if halves else rows // tr
    bc1 = 1.0 - ADAM_B1 ** ADAM_STEP
    bc2 = 1.0 - ADAM_B2 ** ADAM_STEP

    def body(w_ref, m_ref, v_ref, *rest):
        g_refs, (g_out, d_out, m_out, v_out) = rest[:-4], rest[-4:]
        if halves:
            mine_here = (pl.program_id(0) // nh) == lax.axis_index("c")
            g = jnp.where(mine_here, g_refs[0][...], g_refs[1][...])
        else:
            g = g_refs[0][...]
        m2 = ADAM_B1 * m_ref[...] + (1.0 - ADAM_B1) * g
        v2 = ADAM_B2 * v_ref[...] + (1.0 - ADAM_B2) * (g * g)
        g_out[...] = g
        m_out[...] = m2
        v_out[...] = v2
        d_out[...] = -ADAM_LR * ((m2 / bc1) / (jnp.sqrt(v2 / bc2) + ADAM_EPS) + ADAM_WD * w_ref[...])

    spec = pl.BlockSpec((tr, cols), lambda i: (i, 0))
    g_spec = pl.BlockSpec((tr, cols), lambda i: (i % nh, 0))
    g_args = (g_mine, g_theirs) if halves else (g_mine,)
    return _call(body, name=name, grid=(rows // tr,), parallel=True, in_specs=[spec] * 3 + [g_spec] * len(g_args),
                 out_specs=[spec] * 4, out_shape=[_sds((rows, cols), F32)] * 4, args=(w, m, v, *g_args), job=job)


def _inproj_fwd(x, g1, w_all, job=None):
    T = x.shape[0]
    tT = _row_tile(T, 512)

    def body(x_ref, g_ref, w_ref, a_ref, proj_ref, alow_ref):
        xv = x_ref[...]
        a = (xv * _rms_stats(xv) * g_ref[...]).astype(BF16)
        a_ref[...] = a
        for j in range(N_MAIN // 1024):
            cols = slice(j * 1024, (j + 1) * 1024)
            proj_ref[:, cols] = _dot(a, w_ref[:, cols]).astype(BF16)
        alow_ref[...] = _dot(a, w_ref[:, N_MAIN:N_ALL])

    row = lambda w: pl.BlockSpec((tT, w), lambda i: (i, 0))
    return _call(
        body, name="inproj_fwd", grid=(T // tT,), parallel=True,
        in_specs=[row(D_MODEL), pl.BlockSpec((1, D_MODEL), lambda i: (0, 0)), _whole()],
        out_specs=[row(D_MODEL), row(N_MAIN), row(LANES)],
        out_shape=[_sds((T, D_MODEL), BF16), _sds((T, N_MAIN), BF16), _sds((T, LANES), F32)],
        args=(x, g1, w_all), job=job)


def _gla_decay_terms(al_ref, wgu_ref, bg_ref, tT):
    logit = _dot_f32(al_ref[...], wgu_ref[...]) + bg_ref[...]
    la = _log_sigmoid(logit) * (1.0 / GLA_TAU)
    delta = _dot_exact_lhs(_chunk_masks(tT, upper=True), la)
    return logit, la, delta


def _gla_fwd(proj, alow, wgu, b_gate, gn, job=None):
    T = proj.shape[0]
    tT = _row_tile(T, 512)
    nc = tT // CHUNK

    def body(q_ref, k_ref, v_ref, r_ref, al_ref, wgu_ref, bg_ref, gn_ref, y_ref, st_ref, s_scr):
        @pl.when(pl.program_id(0) == 0)
        def _():
            s_scr[...] = jnp.zeros_like(s_scr)

        _, la, delta = _gla_decay_terms(al_ref, wgu_ref, bg_ref, tT)
        kdec = (k_ref[...].astype(F32) * jnp.exp(delta)).astype(BF16)
        heads = range(GLA_HEADS)
        kcs = [slice(h * GLA_DK, (h + 1) * GLA_DK) for h in heads]
        vcs = [slice(h * GLA_DV, (h + 1) * GLA_DV) for h in heads]
        state = [s_scr[h] for h in heads]
        for c in range(nc):
            rows = slice(c * CHUNK, (c + 1) * CHUNK)
            first = slice(c * CHUNK, c * CHUNK + 1)
            dec = jnp.exp(la[first, :] + delta[first, :])
            upd_t = [_dot(v_ref[rows, vcs[h]], kdec[rows, kcs[h]], _TN) for h in heads]
            qs = [(q_ref[rows, kcs[h]].astype(F32) * (GLA_DK ** -0.5)).astype(BF16) for h in heads]
            for h in heads:
                state[h] = state[h] * dec[:, kcs[h]] + upd_t[h]
                st_ref[c, h] = state[h]
            o = [_dot(qs[h], state[h].astype(BF16), _NT) for h in heads]
            for h in heads:
                on = o[h] * _rms_stats(o[h]) * gn_ref[:, vcs[h]]
                rr = r_ref[rows, vcs[h]].astype(F32)
                y_ref[rows, vcs[h]] = (on * (rr * _sigmoid(rr))).astype(BF16)
        for h in heads:
            s_scr[h] = state[h]

    blk = lambda w, j: pl.BlockSpec((tT, w), lambda i: (i, j))
    return _call(
        body, name="gla_fwd", grid=(T // tT,),
        in_specs=[blk(512, 0), blk(512, 1), blk(1024, 1), blk(1024, 2), blk(LANES, 0), _whole(), _whole(), _whole()],
        out_specs=[pl.BlockSpec((tT, GLA_V), lambda i: (i, 0)),
                   pl.BlockSpec((nc, GLA_HEADS, GLA_DV, GLA_DK), lambda i: (i, 0, 0, 0))],
        out_shape=[_sds((T, GLA_V), BF16), _sds((T // CHUNK, GLA_HEADS, GLA_DV, GLA_DK), F32)],
        scratch_shapes=[pltpu.VMEM((GLA_HEADS, GLA_DV, GLA_DK), F32)],
        args=(proj, proj, proj, proj, alow, wgu, b_gate, gn), job=job)


def _sgu_mask():
    i = lax.broadcasted_iota(jnp.int32, (SGU_BLOCK, SGU_BLOCK), 0)
    j = lax.broadcasted_iota(jnp.int32, (SGU_BLOCK, SGU_BLOCK), 1)
    return lax.shift_right_logical(j, 6) <= lax.shift_right_logical(i, 6)


def _sgu_fwd(proj, ln_g, ln_b, w_sp, b_sp_t):
    T = proj.shape[0]
    tT = _row_tile(T, 512)
    nb = tT // SGU_BLOCK

    def body(su_ref, sv_ref, lg_ref, lb_ref, w_ref, b_ref, y_ref):
        mask = _sgu_mask()
        for g in range(SGU_GROUPS):
            gc = slice(g * SGU_DG, (g + 1) * SGU_DG)
            wm = jnp.where(mask, w_ref[g], 0.0).astype(BF16)
            vf = _gelu(sv_ref[:, gc].astype(F32))
            mu = jnp.mean(vf, axis=-1, keepdims=True)
            vc = vf - mu
            rstd = lax.rsqrt(jnp.mean(vc * vc, axis=-1, keepdims=True) + EPS)
            vn = (vc * rstd * lg_ref[:, gc] + lb_ref[:, gc]).astype(BF16)
            u = _gelu(su_ref[:, gc].astype(F32))
            for b in range(nb):
                rows = slice(b * SGU_BLOCK, (b + 1) * SGU_BLOCK)
                mixed = _dot(wm, vn[rows, :]) + b_ref[:, g:g + 1]
                y_ref[rows, gc] = (u[rows, :] * mixed).astype(BF16)

    blk = lambda j: pl.BlockSpec((tT, 1024), lambda i: (i, j))
    res, _ = _call(body, name="sgu_fwd", grid=(T // tT,), parallel=True,
                   in_specs=[blk(3), blk(4), _whole(), _whole(), _whole(), _whole()],
                   out_specs=[pl.BlockSpec((tT, 1024), lambda i: (i, 0))], out_shape=[_sds((T, 1024), BF16)],
                   args=(proj, proj, ln_g, ln_b, w_sp, b_sp_t))
    return res[0]


def _merge_fwd(x, proj, y_gla, y_sgu, w_bg, w_bs, w_o, g_pm):
    T = x.shape[0]
    tT = _row_tile(T, 512)

    def body(x_ref, gg_ref, gs_ref, yg_ref, ys_ref, wbg_ref, wbs_ref, wo_ref, g_ref,
             zg_ref, zs_ref, mg_ref, mix_ref, x1_ref):
        zg = _dot(yg_ref[...], wbg_ref[...])
        zs = _dot(ys_ref[...], wbs_ref[...])
        zg_ref[...] = zg.astype(BF16)
        zs_ref[...] = zs.astype(BF16)
        merged = (_sigmoid(gg_ref[...].astype(F32)) * zg + _sigmoid(gs_ref[...].astype(F32)) * zs).astype(BF16)
        mg_ref[...] = merged
        mix = _dot(merged, wo_ref[...])
        mix_ref[...] = mix
        x1_ref[...] = x_ref[...] + mix * _rms_stats(mix) * g_ref[...]

    row = pl.BlockSpec((tT, D_MODEL), lambda i: (i, 0))
    blk = lambda j: pl.BlockSpec((tT, 1024), lambda i: (i, j))
    sds = lambda dt: _sds((T, D_MODEL), dt)
    res, _ = _call(body, name="merge_fwd", grid=(T // tT,), parallel=True,
                   in_specs=[row, blk(5), blk(6), row, row, _whole(), _whole(), _whole(),
                             pl.BlockSpec((1, D_MODEL), lambda i: (0, 0))],
                   out_specs=[row] * 5, out_shape=[sds(BF16), sds(BF16), sds(BF16), sds(F32), sds(F32)],
                   args=(x, proj, proj, y_gla, y_sgu, w_bg, w_bs, w_o, g_pm))
    return res


def _ffn_fwd_bwd(x1, tgt, w_fi, w_fo, g_pf, g_po):
    T = x1.shape[0]
    tT = _row_tile(T, 256)
    half = D_FF // 2

    def body(x1_ref, t_ref, wfi_ref, wfo_ref, gpf_ref, gpo_ref,
             h_ref, f_ref, dgu_ref, dy_ref, dx1_ref, loss_ref, dgpf_ref, dgpo_ref, gu_scr):
        @pl.when(pl.program_id(0) == 0)
        def _():
            loss_ref[...] = jnp.zeros_like(loss_ref)
            dgpf_ref[...] = jnp.zeros_like(dgpf_ref)
            dgpo_ref[...] = jnp.zeros_like(dgpo_ref)

        x1v = x1_ref[...]
        r2 = _rms_stats(x1v)
        h = (x1v * r2 * gpf_ref[...]).astype(BF16)
        h_ref[...] = h
        y = jnp.zeros((tT, D_MODEL), F32)
        for j in range(2):
            gc = slice(j * half, (j + 1) * half)
            uc = slice(D_FF + j * half, D_FF + (j + 1) * half)
            gate = _dot(h, wfi_ref[:, gc])
            up = _dot(h, wfi_ref[:, uc])
            gu_scr[:, gc] = gate
            gu_scr[:, uc] = up
            f = (gate * _sigmoid(gate) * up).astype(BF16)
            f_ref[:, gc] = f
            y = y + _dot(f, wfo_ref[gc, :])
        r3 = _rms_stats(y)
        x2 = x1v + y * r3 * gpo_ref[...]
        err = x2 - t_ref[...]
        loss_ref[...] += jnp.sum(err * err) * (0.5 / D_MODEL)
        dx2 = err * (1.0 / D_MODEL)
        dy, dg = _rms_bwd(dx2, y, r3, gpo_ref[...])
        dgpo_ref[...] += jnp.sum(dg, axis=0, keepdims=True)
        dyb = dy.astype(BF16)
        dy_ref[...] = dyb
        dh = jnp.zeros((tT, D_MODEL), F32)
        for j in range(2):
            gc = slice(j * half, (j + 1) * half)
            uc = slice(D_FF + j * half, D_FF + (j + 1) * half)
            df = _dot(dyb, wfo_ref[gc, :], _NT)
            gate = gu_scr[:, gc]
            up = gu_scr[:, uc]
            sg = _sigmoid(gate)
            dgate = (df * up * (sg * (1.0 + gate * (1.0 - sg)))).astype(BF16)
            dup = (df * (gate * sg)).astype(BF16)
            dgu_ref[:, gc] = dgate
            dgu_ref[:, uc] = dup
            dh = dh + _dot(dgate, wfi_ref[:, gc], _NT) + _dot(dup, wfi_ref[:, uc], _NT)
        dx1n, dg2 = _rms_bwd(dh, x1v, r2, gpf_ref[...])
        dgpf_ref[...] += jnp.sum(dg2, axis=0, keepdims=True)
        dx1_ref[...] = dx2 + dx1n

    row = lambda w: pl.BlockSpec((tT, w), lambda i: (i, 0))
    vec = pl.BlockSpec((1, D_MODEL), lambda i: (0, 0))
    res, _ = _call(
        body, name="ffn_fwd_bwd", grid=(T // tT,),
        in_specs=[row(D_MODEL), row(D_MODEL), _whole(), _whole(), vec, vec],
        out_specs=[row(D_MODEL), row(D_FF), row(2 * D_FF), row(D_MODEL), row(D_MODEL),
                   pl.BlockSpec((1, LANES), lambda i: (0, 0)), vec, vec],
        out_shape=[_sds((T, D_MODEL), BF16), _sds((T, D_FF), BF16), _sds((T, 2 * D_FF), BF16), _sds((T, D_MODEL), BF16),
                   _sds((T, D_MODEL), F32), _sds((1, LANES), F32), _sds((1, D_MODEL), F32), _sds((1, D_MODEL), F32)],
        scratch_shapes=[pltpu.VMEM((tT, 2 * D_FF), F32)], args=(x1, tgt, w_fi, w_fo, g_pf, g_po))
    return res


def _merge_bwd(dx1, mix, proj, zg, zs, w_bg, w_bs, w_o, g_pm, job=None):
    T = dx1.shape[0]
    tT = _row_tile(T, 512)

    def body(dx1_ref, mix_ref, gg_ref, gs_ref, zg_ref, zs_ref, wbg_ref, wbs_ref, wo_ref, g_ref,
             dmix_ref, dzg_ref, dzs_ref, dgate_ref, dyg_ref, dys_ref, dgpm_ref):
        @pl.when(pl.program_id(0) == 0)
        def _():
            dgpm_ref[...] = jnp.zeros_like(dgpm_ref)

        mix = mix_ref[...]
        dmix, dg = _rms_bwd(dx1_ref[...], mix, _rms_stats(mix), g_ref[...])
        dgpm_ref[...] += jnp.sum(dg, axis=0, keepdims=True)
        dmb = dmix.astype(BF16)
        dmix_ref[...] = dmb
        dmerged = _dot(dmb, wo_ref[...], _NT)
        for k, (gate_ref, z_ref, w_ref, dz_ref, dy_ref) in enumerate((
                (gg_ref, zg_ref, wbg_ref, dzg_ref, dyg_ref), (gs_ref, zs_ref, wbs_ref, dzs_ref, dys_ref))):
            sg = _sigmoid(gate_ref[...].astype(F32))
            dz = (dmerged * sg).astype(BF16)
            dz_ref[...] = dz
            dgate_ref[:, k * 1024:(k + 1) * 1024] = (dmerged * z_ref[...].astype(F32) * (sg * (1.0 - sg))).astype(BF16)
            dy_ref[...] = _dot(dz, w_ref[...], _NT).astype(BF16)

    row = pl.BlockSpec((tT, D_MODEL), lambda i: (i, 0))
    blk = lambda j: pl.BlockSpec((tT, 1024), lambda i: (i, j))
    vec = pl.BlockSpec((1, D_MODEL), lambda i: (0, 0))
    sds = _sds((T, D_MODEL), BF16)
    return _call(
        body, name="merge_bwd", grid=(T // tT,),
        in_specs=[row, row, blk(5), blk(6), row, row, _whole(), _whole(), _whole(), vec],
        out_specs=[row, row, row, pl.BlockSpec((tT, W_MRG), lambda i: (i, 0)), row, row, vec],
        out_shape=[sds, sds, sds, _sds((T, W_MRG), BF16), sds, sds, _sds((1, D_MODEL), F32)],
        args=(dx1, mix, proj, proj, zg, zs, w_bg, w_bs, w_o, g_pm), job=job)


def _sgu_bwd(proj, dy_sgu, ln_g, ln_b, w_sp, b_sp_t, job=None):
    T = proj.shape[0]
    tT = _row_tile(T, 512)
    nb = tT // SGU_BLOCK

    def body(su_ref, sv_ref, dy_ref, lg_ref, lb_ref, w_ref, b_ref, dp_ref, dw_ref, dbt_ref, dlg_ref, dlb_ref):
        @pl.when(pl.program_id(0) == 0)
        def _():
            dw_ref[...] = jnp.zeros_like(dw_ref)
            dbt_ref[...] = jnp.zeros_like(dbt_ref)
            dlg_ref[...] = jnp.zeros_like(dlg_ref)
            dlb_ref[...] = jnp.zeros_like(dlb_ref)

        mask = _sgu_mask()
        lane = lax.broadcasted_iota(jnp.int32, (SGU_BLOCK, LANES), 1)
        for g in range(SGU_GROUPS):
            gc = slice(g * SGU_DG, (g + 1) * SGU_DG)
            gc_v = slice(1024 + g * SGU_DG, 1024 + (g + 1) * SGU_DG)
            wm = jnp.where(mask, w_ref[g], 0.0).astype(BF16)
            vf, dvf_dsv = _gelu_and_grad(sv_ref[:, gc].astype(F32))
            mu = jnp.mean(vf, axis=-1, keepdims=True)
            vc = vf - mu
            rstd = lax.rsqrt(jnp.mean(vc * vc, axis=-1, keepdims=True) + EPS)
            vhat = vc * rstd
            vn = (vhat * lg_ref[:, gc] + lb_ref[:, gc]).astype(BF16)
            u, du_dsu = _gelu_and_grad(su_ref[:, gc].astype(F32))
            dy = dy_ref[:, gc].astype(F32)
            dmixed = (dy * u).astype(BF16)
            dvn_parts = []
            dw_acc = jnp.zeros((SGU_BLOCK, SGU_BLOCK), F32)
            db_acc = jnp.zeros((SGU_BLOCK, 1), F32)
            for b in range(nb):
                rows = slice(b * SGU_BLOCK, (b + 1) * SGU_BLOCK)
                mixed = _dot(wm, vn[rows, :]) + b_ref[:, g:g + 1]
                dp_ref[rows, gc] = (dy[rows, :] * mixed * du_dsu[rows, :]).astype(BF16)
                dvn_parts.append(_dot(wm, dmixed[rows, :], _TN))
                dw_acc = dw_acc + _dot(dmixed[rows, :], vn[rows, :], _NT)
                db_acc = db_acc + jnp.sum(dmixed[rows, :].astype(F32), axis=-1, keepdims=True)
            dw_ref[g] += jnp.where(mask, dw_acc, 0.0)
            dbt_ref[...] += jnp.where(lane == g, db_acc, 0.0)
            dvn = jnp.concatenate(dvn_parts, axis=0)
            dlg_ref[:, gc] += jnp.sum(dvn * vhat, axis=0, keepdims=True)
            dlb_ref[:, gc] += jnp.sum(dvn, axis=0, keepdims=True)
            dvh = dvn * lg_ref[:, gc]
            dvf = rstd * (dvh - jnp.mean(dvh, axis=-1, keepdims=True)
                          - vhat * jnp.mean(dvh * vhat, axis=-1, keepdims=True))
            dp_ref[:, gc_v] = (dvf * dvf_dsv).astype(BF16)

    blk = lambda j: pl.BlockSpec((tT, 1024), lambda i: (i, j))
    row = lambda w: pl.BlockSpec((tT, w), lambda i: (i, 0))
    vec = pl.BlockSpec((1, 1024), lambda i: (0, 0))
    return _call(
        body, name="sgu_bwd", grid=(T // tT,),
        in_specs=[blk(3), blk(4), row(1024), _whole(), _whole(), _whole(), _whole()],
        out_specs=[row(W_SGU), pl.BlockSpec((SGU_GROUPS, SGU_BLOCK, SGU_BLOCK), lambda i: (0, 0, 0)),
                   pl.BlockSpec((SGU_BLOCK, LANES), lambda i: (0, 0)), vec, vec],
        out_shape=[_sds((T, W_SGU), BF16), _sds((SGU_GROUPS, SGU_BLOCK, SGU_BLOCK), F32), _sds((SGU_BLOCK, LANES), F32),
                   _sds((1, 1024), F32), _sds((1, 1024), F32)],
        args=(proj, proj, dy_sgu, ln_g, ln_b, w_sp, b_sp_t), job=job)


def _gla_bwd(proj, alow, wgu, b_gate, gn, states, dy_gla, job=None):
    T = proj.shape[0]
    tT = _row_tile(T, 512)
    nc = tT // CHUNK
    nt = T // tT

    def body(q_ref, k_ref, v_ref, r_ref, al_ref, wgu_ref, bg_ref, gn_ref, st_ref, sp_ref, dy_ref,
             dp_ref, dal_ref, dgn_ref, dbg_ref, dwgu_ref, g_scr, dd_scr, dt_scr):
        step = pl.program_id(0)

        @pl.when(step == 0)
        def _():
            g_scr[...] = jnp.zeros_like(g_scr)
            dgn_ref[...] = jnp.zeros_like(dgn_ref)
            dbg_ref[...] = jnp.zeros_like(dbg_ref)
            dwgu_ref[...] = jnp.zeros_like(dwgu_ref)

        has_prev = jnp.where(step == nt - 1, 0.0, 1.0)
        logit, la, delta = _gla_decay_terms(al_ref, wgu_ref, bg_ref, tT)
        e = jnp.exp(delta)
        kdec_f = k_ref[...].astype(F32) * e
        kdec = kdec_f.astype(BF16)
        heads = range(GLA_HEADS)
        kcs = [slice(h * GLA_DK, (h + 1) * GLA_DK) for h in heads]
        vcs = [slice(h * GLA_DV, (h + 1) * GLA_DV) for h in heads]
        carry = [g_scr[h] for h in heads]
        dgn_acc = [jnp.zeros((1, GLA_DV), F32) for _ in heads]
        for c in reversed(range(nc)):
            rows = slice(c * CHUNK, (c + 1) * CHUNK)
            first = slice(c * CHUNK, c * CHUNK + 1)
            dec = jnp.exp(la[first, :] + delta[first, :])
            s_b = [st_ref[c, h].astype(BF16) for h in heads]
            qs = [(q_ref[rows, kcs[h]].astype(F32) * (GLA_DK ** -0.5)).astype(BF16) for h in heads]
            o = [_dot(qs[h], s_b[h], _NT) for h in heads]
            do = []
            for h in heads:
                rstd = _rms_stats(o[h])
                ohat = o[h] * rstd
                gnh = gn_ref[:, vcs[h]]
                dy = dy_ref[rows, vcs[h]].astype(F32)
                rr = r_ref[rows, vcs[h]].astype(F32)
                sg = _sigmoid(rr)
                don = dy * (rr * sg)
                dp_ref[rows, OFF_R + h * GLA_DV:OFF_R + (h + 1) * GLA_DV] = (
                    dy * (ohat * gnh) * (sg * (1.0 + rr * (1.0 - sg)))).astype(BF16)
                dgn_acc[h] = dgn_acc[h] + jnp.sum(don * ohat, axis=0, keepdims=True)
                dn = don * gnh
                do.append((rstd * (dn - ohat * jnp.mean(dn * ohat, axis=-1, keepdims=True))).astype(BF16))
            dq = [_dot(do[h], s_b[h]) for h in heads]
            g_t = [_dot(do[h], qs[h], _TN) + carry[h] for h in heads]
            g_b = [g_t[h].astype(BF16) for h in heads]
            dv = [_dot(kdec[rows, kcs[h]], g_b[h], _NT) for h in heads]
            dkdec = [_dot(v_ref[rows, vcs[h]], g_b[h]) for h in heads]
            for h in heads:
                s_prev = st_ref[c - 1, h] if c > 0 else sp_ref[0, h] * has_prev
                ddec = jnp.sum(g_t[h] * s_prev, axis=0, keepdims=True)
                carry[h] = g_t[h] * dec[:, kcs[h]]
                dp_ref[rows, OFF_Q + h * GLA_DK:OFF_Q + (h + 1) * GLA_DK] = (dq[h] * (GLA_DK ** -0.5)).astype(BF16)
                dp_ref[rows, OFF_V + h * GLA_DV:OFF_V + (h + 1) * GLA_DV] = dv[h].astype(BF16)
                dp_ref[rows, OFF_K + h * GLA_DK:OFF_K + (h + 1) * GLA_DK] = (dkdec[h] * e[rows, kcs[h]]).astype(BF16)
                dd_scr[rows, kcs[h]] = dkdec[h] * kdec_f[rows, kcs[h]]
                dt_scr[rows, kcs[h]] = jnp.broadcast_to(ddec * dec[:, kcs[h]], (CHUNK, GLA_DK))
        for h in heads:
            g_scr[h] = carry[h]
            dgn_ref[:, vcs[h]] += dgn_acc[h]
        dla = _dot_exact_lhs(_chunk_masks(tT, upper=False), dd_scr[...]) + dt_scr[...]
        dlogit = dla * (1.0 / GLA_TAU) * _sigmoid(-logit)
        dbg_ref[...] += jnp.sum(dlogit, axis=0, keepdims=True)
        dwgu_ref[...] += _dot_f32(al_ref[...], dlogit, _TN)
        dal_ref[...] = _dot_f32(dlogit, wgu_ref[...], _NT).astype(BF16)

    rev = lambda i: nt - 1 - i
    blk = lambda w, j: pl.BlockSpec((tT, w), lambda i: (rev(i), j))
    st_blk = pl.BlockSpec((nc, GLA_HEADS, GLA_DV, GLA_DK), lambda i: (rev(i), 0, 0, 0))
    sp_blk = pl.BlockSpec((1, GLA_HEADS, GLA_DV, GLA_DK), lambda i: (jnp.maximum(rev(i) * nc - 1, 0), 0, 0, 0))
    return _call(
        body, name="gla_bwd", grid=(nt,),
        in_specs=[blk(512, 0), blk(512, 1), blk(1024, 1), blk(1024, 2), blk(LANES, 0), _whole(), _whole(), _whole(),
                  st_blk, sp_blk, blk(GLA_V, 0)],
        out_specs=[blk(W_GLA, 0), blk(LANES, 0), pl.BlockSpec((1, GLA_V), lambda i: (0, 0)),
                   pl.BlockSpec((1, GLA_QK), lambda i: (0, 0)), pl.BlockSpec((LANES, GLA_QK), lambda i: (0, 0))],
        out_shape=[_sds((T, W_GLA), BF16), _sds((T, LANES), BF16), _sds((1, GLA_V), F32), _sds((1, GLA_QK), F32),
                   _sds((LANES, GLA_QK), F32)],
        scratch_shapes=[pltpu.VMEM((GLA_HEADS, GLA_DV, GLA_DK), F32), pltpu.VMEM((tT, GLA_QK), F32),
                        pltpu.VMEM((tT, GLA_QK), F32)],
        args=(proj, proj, proj, proj, alow, wgu, b_gate, gn, states, states, dy_gla), job=job)


def _inproj_bwd(x, dx1, g1, w_all, dparts, job=None):
    T = x.shape[0]
    tT = _row_tile(T, 512)
    offs = (0, W_GLA, W_GLA + W_SGU, N_MAIN)

    def body(x_ref, dx1_ref, g_ref, w_ref, *rest):
        part_refs, (dx_ref, dg_ref) = rest[:len(offs)], rest[len(offs):]

        @pl.when(pl.program_id(0) == 0)
        def _():
            dg_ref[...] = jnp.zeros_like(dg_ref)

        da = jnp.zeros((tT, D_MODEL), F32)
        for off, p_ref in zip(offs, part_refs):
            da = da + _dot(p_ref[...], w_ref[:, off:off + p_ref.shape[1]], _NT)
        xv = x_ref[...]
        dx, dg = _rms_bwd(da, xv, _rms_stats(xv), g_ref[...])
        dg_ref[...] += jnp.sum(dg, axis=0, keepdims=True)
        dx_ref[...] = dx1_ref[...] + dx

    row = lambda w: pl.BlockSpec((tT, w), lambda i: (i, 0))
    vec = pl.BlockSpec((1, D_MODEL), lambda i: (0, 0))
    return _call(
        body, name="inproj_bwd", grid=(T // tT,),
        in_specs=[row(D_MODEL), row(D_MODEL), vec, _whole()] + [row(p.shape[1]) for p in dparts],
        out_specs=[row(D_MODEL), vec], out_shape=[_sds((T, D_MODEL), F32), _sds((1, D_MODEL), F32)],
        args=(x, dx1, g1, w_all, *dparts), job=job)


def _tn_matmul(a, b, name, job=None):
    T, M = a.shape
    N = b.shape[1]
    tk = _row_tile(T, 2048)
    tm = M if M <= 1024 else 1408
    tn = N if N <= 1024 else (1024 if N % 1024 == 0 else 1408)
    assert M % tm == 0 and N % tn == 0

    def body(a_ref, b_ref, o_ref):
        @pl.when(pl.program_id(2) == 0)
        def _():
            o_ref[...] = jnp.zeros_like(o_ref)

        o_ref[...] += _dot(a_ref[...], b_ref[...], _TN)

    res, jres = _call(
        body, name=name, grid=(M // tm, N // tn, T // tk),
        in_specs=[pl.BlockSpec((tk, tm), lambda i, j, k: (k, i)), pl.BlockSpec((tk, tn), lambda i, j, k: (k, j))],
        out_specs=[pl.BlockSpec((tm, tn), lambda i, j, k: (i, j))], out_shape=[_sds((M, N), F32)], args=(a, b), job=job)
    return res[0], jres


def _pad_rows(a, rows=8):
    return jnp.pad(a, ((0, rows - a.shape[0]), (0, LANES - a.shape[1])))


def _halves_view(dw):
    r = dw.shape[0] // N_CHIPS
    return dw.reshape(N_CHIPS, 2, r // 2, dw.shape[1])


def kernel(x, norm_pre_mix, w_in, w_gate_up, b_gate, gla_norm, sgu_ln_g, sgu_ln_b, w_spatial, b_spatial, w_branch_gla, w_branch_sgu, w_out, norm_post_mix, norm_pre_ffn, w_ffn_in, w_ffn_out, norm_post_ffn, loss_target, m_norm_pre_mix, m_w_in, m_w_gate_up, m_b_gate, m_gla_norm, m_sgu_ln_g, m_sgu_ln_b, m_w_spatial, m_b_spatial, m_w_branch_gla, m_w_branch_sgu, m_w_out, m_norm_post_mix, m_norm_pre_ffn, m_w_ffn_in, m_w_ffn_out, m_norm_post_ffn, v_norm_pre_mix, v_w_in, v_w_gate_up, v_b_gate, v_gla_norm, v_sgu_ln_g, v_sgu_ln_b, v_w_spatial, v_b_spatial, v_w_branch_gla, v_w_branch_sgu, v_w_out, v_norm_post_mix, v_norm_pre_ffn, v_w_ffn_in, v_w_ffn_out, v_norm_post_ffn):
    chip = 2 * lax.axis_index("x") + lax.axis_index("y")
    xt, tgt = x[0], loss_target[0]

    tiny = jnp.concatenate([w_gate_up[0], _pad_rows(gla_norm[0]), _pad_rows(sgu_ln_g[0]), _pad_rows(sgu_ln_b[0]),
                            jnp.zeros((8, LANES), F32)], axis=0)
    def with_own(gathered, own):
        if gathered.ndim == 3:
            return lax.dynamic_update_slice(gathered, own[None], (chip, 0, 0))
        return lax.dynamic_update_slice(gathered, own, (0, chip * own.shape[1]))

    w_in_b = w_in[0].astype(BF16)
    g_in, g_tiny = _run_job(_job_gather([w_in_b, tiny], [False, False]), "gather_w_in")
    g_tiny = with_own(g_tiny, tiny)
    w_all = _relayout_w_in(with_own(g_in, w_in_b))
    cols = lambda a: a.transpose(1, 0, 2).reshape(a.shape[1], N_CHIPS * a.shape[2])
    wgu = jnp.pad(cols(g_tiny[:, 0:16]), ((0, LANES - GLA_RANK), (0, 0)))
    gn = cols(g_tiny[:, 16:20, :64]).reshape(1, GLA_V)
    ln_g = cols(g_tiny[:, 24:28, :64]).reshape(1, 1024)
    ln_b = cols(g_tiny[:, 32:36, :64]).reshape(1, 1024)
    b_sp_t = jnp.pad(b_spatial[0].T, ((0, 0), (0, LANES - SGU_GROUPS)))
    w_sp = w_spatial[0]

    own_rows = [w_branch_gla[0].astype(BF16), w_branch_sgu[0].astype(BF16), w_out[0].astype(BF16), w_ffn_out[0].astype(BF16)]
    (a, proj, alow), g_rows = _inproj_fwd(xt, norm_pre_mix, w_all, job=_job_gather(own_rows, [False] * 4))
    rows = lambda g: g.reshape(N_CHIPS * g.shape[1], g.shape[2])
    w_bg, w_bs, w_o, w_fo = [rows(with_own(g, own)) for g, own in zip(g_rows, own_rows)]
    w_fi_b = w_ffn_in[0].astype(BF16)
    (y_gla, states), (g_fi,) = _gla_fwd(proj, alow, wgu, b_gate, gn, job=_job_gather([w_fi_b], [True]))
    w_fi = with_own(g_fi, w_fi_b)
    y_sgu = _sgu_fwd(proj, ln_g, ln_b, w_sp, b_sp_t)
    zg, zs, merged, mix, x1 = _merge_fwd(xt, proj, y_gla, y_sgu, w_bg, w_bs, w_o, norm_post_mix)
    h, f, dgu, dy, dx1, loss, d_gpf, d_gpo = _ffn_fwd_bwd(x1, tgt, w_fi, w_fo, norm_pre_ffn, norm_post_ffn)

    own_part = lambda c: lax.dynamic_index_in_dim(c, chip, 0, keepdims=False)
    whole = lambda hs: [[(h_, None)] for h_ in hs]
    dw_fo, _ = _tn_matmul(f, dy, "dw_ffn_out")
    dw_fo4 = _halves_view(dw_fo)
    dw_fi, (q_fo,) = _tn_matmul(h, dgu, "dw_ffn_in", job=_job_to_other_core([[(dw_fo4, 0)]]))
    c_fo = _presum(dw_fo4, q_fo, "presum_ffn_out")
    (dmix, dzg, dzs, dp_mrg, dyg, dys, d_gpm), (s_fo, q_fi) = _merge_bwd(
        dx1, mix, proj, zg, zs, w_bg, w_bs, w_o, norm_post_mix,
        job=_join(_job_scatter([c_fo]), _job_to_other_core([[(dw_fi, 0)]])))
    c_fi = _presum(dw_fi, q_fi, "presum_ffn_in")
    dw_o4 = _halves_view(_tn_matmul(merged, dmix, "dw_out")[0])
    dw_bg4 = _halves_view(_tn_matmul(y_gla, dzg, "dw_branch_gla")[0])
    dw_bs4 = _halves_view(_tn_matmul(y_sgu, dzs, "dw_branch_sgu")[0])
    (dp_sgu, d_wsp, d_bsp_t, d_lng, d_lnb), (s_fi, q_o, q_bg, q_bs) = _sgu_bwd(
        proj, dys, ln_g, ln_b, w_sp, b_sp_t,
        job=_join(_job_scatter([c_fi]), _job_to_other_core([[(dw_o4, 0)], [(dw_bg4, 0)], [(dw_bs4, 0)]])))
    c_o, c_bg, c_bs = (_presum(dw_o4, q_o, "presum_out"), _presum(dw_bg4, q_bg, "presum_branch_gla"),
                       _presum(dw_bs4, q_bs, "presum_branch_sgu"))
    h_fo = _sum_slots(own_part(c_fo), s_fo, "sum_ffn_out")
    (dp_gla, dal, d_gn, d_bg, d_wgu), (s_o, s_bg, s_bs, t_fo) = _gla_bwd(
        proj, alow, wgu, b_gate, gn, states, dyg,
        job=_join(_job_scatter([c_o, c_bg, c_bs]), _job_to_other_core(whole([h_fo]))))
    h_fi, h_o, h_bg, h_bs = (_sum_slots(own_part(c_fi), s_fi, "sum_ffn_in"), _sum_slots(own_part(c_o), s_o, "sum_out"),
                             _sum_slots(own_part(c_bg), s_bg, "sum_branch_gla"),
                             _sum_slots(own_part(c_bs), s_bs, "sum_branch_sgu"))
    dw_a, (t_fi, t_o, t_bg, t_bs) = _tn_matmul(a, dp_gla, "dw_in_gla",
                                               job=_job_to_other_core(whole([h_fi, h_o, h_bg, h_bs])))
    dw_b, _ = _tn_matmul(a, dp_sgu, "dw_in_sgu")
    dw_c, _ = _tn_matmul(a, dp_mrg, "dw_in_merge")
    dw_d, _ = _tn_matmul(a, dal, "dw_in_gate")

    grads, deltas, new_m, new_v = {}, {}, {}, {}

    def update(name, w, m, v, g_mine, g_theirs, job=None):
        (g, d, m2, v2), jres = _adamw(w[0], m[0], v[0], g_mine, g_theirs, "adamw_" + name, job=job)
        grads[name], deltas[name], new_m[name], new_v[name] = g[None], d[None], m2[None], v2[None]
        return jres

    dw_in = [(dw_a, 0), (dw_b, W_GLA), (dw_c, W_GLA + W_SGU), (dw_d, N_MAIN)]
    update("w_ffn_out", w_ffn_out, m_w_ffn_out, v_w_ffn_out, h_fo, t_fo)
    (q_in,) = update("w_ffn_in", w_ffn_in, m_w_ffn_in, v_w_ffn_in, h_fi, t_fi, job=_job_to_other_core([dw_in]))
    update("w_out", w_out, m_w_out, v_w_out, h_o, t_o)
    update("w_branch_gla", w_branch_gla, m_w_branch_gla, v_w_branch_gla, h_bg, t_bg)
    update("w_branch_sgu", w_branch_sgu, m_w_branch_sgu, v_w_branch_sgu, h_bs, t_bs)
    c_in = _presum_w_in(dw_in, q_in)
    (grad_x, d_g1), (s_in,) = _inproj_bwd(xt, dx1, norm_pre_mix, w_all, (dp_gla, dp_sgu, dp_mrg, dal),
                                          job=_job_scatter([c_in]))
    h_in = _sum_slots(own_part(c_in), s_in, "sum_w_in")
    (t_in,) = _run_job(_job_to_other_core(whole([h_in])), "swap_w_in")
    update("w_in", w_in, m_w_in, v_w_in, h_in, t_in)

    small_names = ["w_spatial", "w_gate_up", "norm_pre_mix", "norm_post_mix", "norm_pre_ffn", "norm_post_ffn", "b_gate",
                   "b_spatial", "gla_norm", "sgu_ln_g", "sgu_ln_b"]
    loss_out, small = _small_update(
        [d_wsp, d_wgu, d_g1, d_gpm, d_gpf, d_gpo, d_bg, d_bsp_t, d_gn, d_lng, d_lnb, loss],
        [w_spatial, w_gate_up, norm_pre_mix, norm_post_mix, norm_pre_ffn, norm_post_ffn, b_gate, b_spatial, gla_norm,
         sgu_ln_g, sgu_ln_b],
        [m_w_spatial, m_w_gate_up, m_norm_pre_mix, m_norm_post_mix, m_norm_pre_ffn, m_norm_post_ffn, m_b_gate,
         m_b_spatial, m_gla_norm, m_sgu_ln_g, m_sgu_ln_b],
        [v_w_spatial, v_w_gate_up, v_norm_pre_mix, v_norm_post_mix, v_norm_pre_ffn, v_norm_post_ffn, v_b_gate,
         v_b_spatial, v_gla_norm, v_sgu_ln_g, v_sgu_ln_b])
    for store, vals in zip((grads, deltas, new_m, new_v), small):
        store.update(zip(small_names, vals))

    order = ["norm_pre_mix", "w_in", "w_gate_up", "b_gate", "gla_norm", "sgu_ln_g", "sgu_ln_b", "w_spatial", "b_spatial",
             "w_branch_gla", "w_branch_sgu", "w_out", "norm_post_mix", "norm_pre_ffn", "w_ffn_in", "w_ffn_out",
             "norm_post_ffn"]
    out = [loss_out, grad_x[None]]
    for store in (grads, deltas, new_m, new_v):
        out.extend(store[n] for n in order)
    return tuple(out)
```

```python
import jax
import jax.numpy as jnp
from jax import lax
from jax.experimental import pallas as pl
from jax.experimental.pallas import tpu as pltpu

F32 = jnp.float32
BF16 = jnp.bfloat16

D_MODEL = 1024
GLA_HEADS = 4
GLA_DK = 128
GLA_DV = 256
GLA_QK = GLA_HEADS * GLA_DK
GLA_V = GLA_HEADS * GLA_DV
GLA_RANK = 16
GLA_TAU = 16.0
CHUNK = 64
SGU_GROUPS = 4
SGU_BLOCK = 128
SGU_DG = 256
D_FF = 2816
EPS = 1e-6
LANES = 128

OFF_Q, OFF_K, OFF_V, OFF_R, OFF_SU, OFF_SV, OFF_GG, OFF_GS, OFF_AL = 0, 512, 1024, 2048, 3072, 4096, 5120, 6144, 7168
W_GLA, W_SGU, W_MRG = 3072, 2048, 2048
N_MAIN = 7168
N_ALL = N_MAIN + LANES
_IN_SPLITS = (GLA_QK, GLA_QK, GLA_V, GLA_V, GLA_RANK, 1024, 1024, 1024, 1024)
_IN_STARTS = tuple(sum(_IN_SPLITS[:i]) for i in range(len(_IN_SPLITS) + 1))
_IN_DST = (OFF_Q, OFF_K, OFF_V, OFF_R, OFF_AL, OFF_SU, OFF_SV, OFF_GG, OFF_GS)
D_IN = _IN_STARTS[-1]

ADAM_LR = 0.001
ADAM_B1 = 0.9
ADAM_B2 = 0.999
ADAM_EPS = 1e-08
ADAM_WD = 0.01
ADAM_STEP = 10

VMEM_LIMIT_BYTES = 56 * 1024 * 1024
N_CHIPS = 4
N_PEER = N_CHIPS - 1
N_DEV = 8
MESH = pl.DeviceIdType.MESH

_NN = (((1,), (0,)), ((), ()))
_NT = (((1,), (1,)), ((), ()))
_TN = (((0,), (0,)), ((), ()))


def _dot(a, b, dims=_NN):
    return lax.dot_general(a, b, dims, preferred_element_type=F32)


def _split(x):
    hi = x.astype(BF16)
    lo = (x - hi.astype(F32)).astype(BF16)
    return hi, lo


def _dot_f32(a, b, dims=_NN):
    ah, al = _split(a)
    bh, bl = _split(b)
    return _dot(ah, bh, dims) + (_dot(al, bh, dims) + _dot(ah, bl, dims))


def _dot_exact_lhs(m, x):
    xh, xl = _split(x)
    return _dot(m, xh) + _dot(m, xl)


def _sigmoid(x):
    return 1.0 / (1.0 + jnp.exp(-x))


def _log_sigmoid(x):
    return jnp.minimum(x, 0.0) - jnp.log(1.0 + jnp.exp(-jnp.abs(x)))


_GELU_C = 0.7978845608028654
_GELU_A = 0.044715


def _gelu_and_grad(x):
    x2 = x * x
    t = jnp.tanh(_GELU_C * (x + _GELU_A * x * x2))
    g = 0.5 * x * (1.0 + t)
    dg = 0.5 * (1.0 + t) + 0.5 * x * (1.0 - t * t) * (_GELU_C * (1.0 + 3.0 * _GELU_A * x2))
    return g, dg


def _gelu(x):
    t = jnp.tanh(_GELU_C * (x + _GELU_A * x * x * x))
    return 0.5 * x * (1.0 + t)


def _rms_stats(x):
    return lax.rsqrt(jnp.mean(x * x, axis=-1, keepdims=True) + EPS)


def _rms_bwd(dout, y, r, g):
    yhat = y * r
    dn = dout * g
    dy = r * (dn - yhat * jnp.mean(dn * yhat, axis=-1, keepdims=True))
    return dy, dout * yhat


def _whole():
    return pl.BlockSpec(memory_space=pltpu.VMEM)


def _row_tile(T, want):
    t = min(T, want)
    assert T % t == 0
    return t


def _chunk_masks(tT, upper):
    row = lax.broadcasted_iota(jnp.int32, (tT, tT), 0)
    col = lax.broadcasted_iota(jnp.int32, (tT, tT), 1)
    same = lax.shift_right_logical(row, 6) == lax.shift_right_logical(col, 6)
    tri = (col > row) if upper else (col < row)
    return jnp.where(same & tri, 1.0, 0.0).astype(BF16)


class _Job:
    def __init__(self, ins, out_shapes, scratch, start, finish):
        self.ins, self.out_shapes, self.scratch, self.start, self.finish = list(ins), list(out_shapes), list(scratch), start, finish


def _join(*jobs):
    def split(refs, counts):
        out, at = [], 0
        for n in counts:
            out.append(refs[at:at + n])
            at += n
        return out

    ni, no, ns = [len(j.ins) for j in jobs], [len(j.out_shapes) for j in jobs], [len(j.scratch) for j in jobs]

    def start(ins, outs, scr):
        for j, a, b, c in zip(jobs, split(ins, ni), split(outs, no), split(scr, ns)):
            j.start(a, b, c)

    def finish(ins, outs, scr):
        for j, a, b, c in zip(jobs, split(ins, ni), split(outs, no), split(scr, ns)):
            j.finish(a, b, c)

    return _Job(sum((j.ins for j in jobs), []), sum((j.out_shapes for j in jobs), []),
                sum((j.scratch for j in jobs), []), start, finish)


def _mesh_pos():
    return lax.axis_index("x"), lax.axis_index("y"), lax.axis_index("c")


def _peer_chips(xi, yi):
    return [(1 - xi, yi), (xi, 1 - yi), (1 - xi, 1 - yi)]


def _half(ci, rows):
    return pl.ds(pl.multiple_of(ci * rows, 8), rows)


def _sds(shape, dtype):
    return jax.ShapeDtypeStruct(tuple(shape), dtype)


def _job_gather(arrs, by_cols):
    n = len(arrs)

    def dst(o, k, chip, rows):
        if by_cols[k]:
            c = arrs[k].shape[1]
            return o.at[rows, pl.ds(pl.multiple_of(chip * c, LANES), c)]
        return o.at[chip, rows]

    def copies(ins, outs, scr, want):
        ici_send, ici_recv, d2d_send, d2d_recv = scr
        xi, yi, ci = _mesh_pos()
        me = 2 * xi + yi
        res = []
        for k in range(n):
            r = arrs[k].shape[0]
            mine, other = _half(ci, r // 2), _half(1 - ci, r // 2)
            for j, (px, py) in enumerate(_peer_chips(xi, yi)):
                s = k * N_PEER + j
                pc = 2 * px + py
                ici = dict(send_sem=ici_send.at[s], recv_sem=ici_recv.at[s], device_id=(px, py, ci), device_id_type=MESH)
                d2d = dict(send_sem=d2d_send.at[s], recv_sem=d2d_recv.at[s], device_id=(xi, yi, 1 - ci),
                           device_id_type=MESH)
                made = {}
                if "send" in want:
                    made["send"] = pltpu.make_async_remote_copy(
                        src_ref=ins[k].at[mine], dst_ref=dst(outs[k], k, me, mine), **ici)
                if "arrive" in want:
                    made["arrive"] = pltpu.make_async_remote_copy(
                        src_ref=ins[k].at[mine], dst_ref=dst(outs[k], k, pc, mine), **ici)
                if "forward" in want:
                    made["forward"] = pltpu.make_async_remote_copy(
                        src_ref=dst(outs[k], k, pc, mine), dst_ref=dst(outs[k], k, pc, mine), **d2d)
                if "handed" in want:
                    made["handed"] = pltpu.make_async_remote_copy(
                        src_ref=dst(outs[k], k, pc, other), dst_ref=dst(outs[k], k, pc, other), **d2d)
                res.append(made)
        return res

    def start(ins, outs, scr):
        for cp in copies(ins, outs, scr, ("send",)):
            cp["send"].start()

    def finish(ins, outs, scr):
        for cp in copies(ins, outs, scr, ("arrive", "forward")):
            cp["arrive"].wait_recv()
            cp["forward"].start()
        for cp in copies(ins, outs, scr, ("handed", "send", "forward")):
            cp["handed"].wait_recv()
            cp["send"].wait_send()
            cp["forward"].wait_send()

    shapes = [_sds((a.shape[0], N_CHIPS * a.shape[1]) if bc else (N_CHIPS,) + a.shape, a.dtype)
              for a, bc in zip(arrs, by_cols)]
    dma = pltpu.SemaphoreType.DMA
    return _Job(arrs, shapes, [dma((n * N_PEER,))] * 4, start, finish)


def _job_scatter(parts):
    n = len(parts)

    def copies(ins, outs, scr):
        send_sems, recv_sems = scr
        xi, yi, ci = _mesh_pos()
        res = []
        for k in range(n):
            for j, (px, py) in enumerate(_peer_chips(xi, yi)):
                s = k * N_PEER + j
                res.append(pltpu.make_async_remote_copy(
                    src_ref=ins[k].at[2 * px + py], dst_ref=outs[k].at[j], send_sem=send_sems.at[s],
                    recv_sem=recv_sems.at[s], device_id=(px, py, ci), device_id_type=MESH))
        return res

    def start(ins, outs, scr):
        for cp in copies(ins, outs, scr):
            cp.start()

    def finish(ins, outs, scr):
        for cp in copies(ins, outs, scr):
            cp.wait_recv()
            cp.wait_send()

    dma = pltpu.SemaphoreType.DMA
    return _Job(parts, [_sds((N_PEER,) + p.shape[1:], p.dtype) for p in parts], [dma((n * N_PEER,))] * 2, start, finish)


def _job_to_other_core(groups):
    pieces = [(g, a, off) for g, group in enumerate(groups) for a, off in group]
    n = len(pieces)

    def geometry(group):
        a0, off0 = group[0]
        if off0 is None:
            return a0.shape
        if a0.ndim == 4:
            return (N_CHIPS, a0.shape[2], a0.shape[3])
        return (a0.shape[0] // 2, sum(a.shape[1] for a, _ in group))

    def copies(ins, outs, scr):
        send_sems, recv_sems = scr
        xi, yi, ci = _mesh_pos()
        res = []
        for p, (g, a, off) in enumerate(pieces):
            if off is None:
                give, land = ins[p], outs[g]
            elif a.ndim == 4:
                give, land = ins[p].at[pl.ds(0, N_CHIPS), 1 - ci], outs[g]
            else:
                hr, w = a.shape[0] // 2, a.shape[1]
                give, land = ins[p].at[_half(1 - ci, hr)], outs[g].at[pl.ds(0, hr), pl.ds(off, w)]
            res.append(pltpu.make_async_remote_copy(
                src_ref=give, dst_ref=land, send_sem=send_sems.at[p], recv_sem=recv_sems.at[p],
                device_id=(xi, yi, 1 - ci), device_id_type=MESH))
        return res

    def start(ins, outs, scr):
        for cp in copies(ins, outs, scr):
            cp.start()

    def finish(ins, outs, scr):
        for cp in copies(ins, outs, scr):
            cp.wait_recv()
            cp.wait_send()

    dma = pltpu.SemaphoreType.DMA
    return _Job([a for _, a, _ in pieces], [_sds(geometry(group), group[0][0].dtype) for group in groups],
                [dma((n,))] * 2, start, finish)


def _call(body, *, name, grid, in_specs, out_specs, out_shape, args, scratch_shapes=(), parallel=False, job=None):
    n_in, n_out, n_scr = len(in_specs), len(out_specs), len(scratch_shapes)
    if job is None:
        sem = ("parallel" if parallel else "arbitrary",) * len(grid)
        res = pl.pallas_call(
            body, name=name, grid=grid, in_specs=in_specs, out_specs=out_specs, out_shape=out_shape,
            scratch_shapes=list(scratch_shapes),
            compiler_params=pltpu.CompilerParams(dimension_semantics=sem, vmem_limit_bytes=VMEM_LIMIT_BYTES))(*args)
        return list(res), []
    n_ji, n_jo = len(job.ins), len(job.out_shapes)

    def carried(*refs):
        ins, refs = refs[:n_in], refs[n_in:]
        j_ins, refs = refs[:n_ji], refs[n_ji:]
        outs, refs = refs[:n_out], refs[n_out:]
        j_outs, refs = refs[:n_jo], refs[n_jo:]
        scr, j_scr = refs[:n_scr], refs[n_scr:]
        ids = [pl.program_id(d) for d in range(len(grid))]
        first = ids[0] == 0
        last = ids[0] == grid[0] - 1
        for d in range(1, len(grid)):
            first = first & (ids[d] == 0)
            last = last & (ids[d] == grid[d] - 1)

        @pl.when(first)
        def _():
            job.start(j_ins, j_outs, j_scr)

        body(*ins, *outs, *scr)

        @pl.when(last)
        def _():
            job.finish(j_ins, j_outs, j_scr)

    hbm = pl.BlockSpec(memory_space=pl.ANY)
    res = pl.pallas_call(
        carried, name=name, grid=grid, in_specs=list(in_specs) + [hbm] * n_ji, out_specs=list(out_specs) + [hbm] * n_jo,
        out_shape=list(out_shape) + job.out_shapes, scratch_shapes=list(scratch_shapes) + job.scratch,
        compiler_params=pltpu.CompilerParams(dimension_semantics=("arbitrary",) * len(grid),
                                             vmem_limit_bytes=VMEM_LIMIT_BYTES))(*args, *job.ins)
    return list(res[:n_out]), list(res[n_out:])


def _run_job(job, name):
    n_i, n_o = len(job.ins), len(job.out_shapes)

    def body(*refs):
        ins, outs, scr = refs[:n_i], refs[n_i:n_i + n_o], refs[n_i + n_o:]
        job.start(ins, outs, scr)
        job.finish(ins, outs, scr)

    hbm = pl.BlockSpec(memory_space=pl.ANY)
    return list(pl.pallas_call(body, name=name, in_specs=[hbm] * n_i, out_specs=[hbm] * n_o, out_shape=job.out_shapes,
                               scratch_shapes=job.scratch)(*job.ins))


def _adam_values(w, m, v, g):
    m2 = ADAM_B1 * m + (1.0 - ADAM_B1) * g
    v2 = ADAM_B2 * v + (1.0 - ADAM_B2) * (g * g)
    delta = -ADAM_LR * ((m2 / (1.0 - ADAM_B1 ** ADAM_STEP)) / (jnp.sqrt(v2 / (1.0 - ADAM_B2 ** ADAM_STEP)) + ADAM_EPS)
                        + ADAM_WD * w)
    return delta, m2, v2


_P_WSP, _P_WGU, _P_NORM, _P_BG, _P_BSP, _P_HEAD, _P_LOSS, _P_ROWS = 0, 512, 576, 608, 616, 624, 720, 728


def _small_update(dgrads, ws, ms, vs):
    n = len(ws)

    def body(*refs):
        dwsp, dwgu, dg1, dgpm, dgpf, dgpo, dbg, dbspt, dgn, dlng, dlnb, loss_in = refs[:12]
        w_refs, m_refs, v_refs = refs[12:12 + n], refs[12 + n:12 + 2 * n], refs[12 + 2 * n:12 + 3 * n]
        loss_out = refs[12 + 3 * n]
        outs = refs[13 + 3 * n:13 + 7 * n]
        pack, slots, tot, send_sems, recv_sems = refs[13 + 7 * n:]
        xi, yi, ci = _mesh_pos()
        me = 4 * xi + 2 * yi + ci
        chip = 2 * xi + yi

        pack[...] = jnp.zeros_like(pack)
        for g in range(SGU_GROUPS):
            pack[_P_WSP + g * SGU_BLOCK:_P_WSP + (g + 1) * SGU_BLOCK] = dwsp[g]
        for j in range(N_CHIPS):
            pack[_P_WGU + GLA_RANK * j:_P_WGU + GLA_RANK * (j + 1)] = dwgu[0:GLA_RANK, LANES * j:LANES * (j + 1)]
        for k, r in enumerate((dg1, dgpm, dgpf, dgpo)):
            for q in range(8):
                pack[_P_NORM + 8 * k + q:_P_NORM + 8 * k + q + 1] = r[:, LANES * q:LANES * (q + 1)]
        for q in range(4):
            pack[_P_BG + q:_P_BG + q + 1] = dbg[:, LANES * q:LANES * (q + 1)]
        pack[_P_BSP:_P_BSP + SGU_GROUPS] = jnp.transpose(dbspt[...])[0:SGU_GROUPS]
        for k, r in enumerate((dgn, dlng, dlnb)):
            for j in range(N_CHIPS):
                for hh in range(4):
                    row = _P_HEAD + 32 * k + 8 * j + hh
                    pack[row:row + 1, 0:64] = r[:, 256 * hh + 64 * j:256 * hh + 64 * (j + 1)]
        pack[_P_LOSS:_P_LOSS + 1] = loss_in[...]

        slots[me] = pack[...]

        def copy(r, slot):
            peer = (jnp.bitwise_xor(xi, (r >> 2) & 1), jnp.bitwise_xor(yi, (r >> 1) & 1), jnp.bitwise_xor(ci, r & 1))
            return pltpu.make_async_remote_copy(
                src_ref=pack, dst_ref=slots.at[slot(peer)], send_sem=send_sems.at[r - 1],
                recv_sem=recv_sems.at[r - 1], device_id=peer, device_id_type=MESH)

        sends = [copy(r, lambda peer: me) for r in range(1, N_DEV)]
        for cp in sends:
            cp.start()
        for r in range(1, N_DEV):
            copy(r, lambda peer: 4 * peer[0] + 2 * peer[1] + peer[2]).wait_recv()
        for cp in sends:
            cp.wait_send()
        acc = slots[0]
        for d in range(1, N_DEV):
            acc = acc + slots[d]
        tot[...] = acc
        loss_out[...] = tot[_P_LOSS:_P_LOSS + 1, 0:1]

        def step(k, g, pick, put):
            d, m2, v2 = _adam_values(pick(w_refs[k]), pick(m_refs[k]), pick(v_refs[k]), g)
            for o, val in zip((outs[k], outs[n + k], outs[2 * n + k], outs[3 * n + k]), (g, d, m2, v2)):
                put(o, val)

        def whole(ref):
            return ref[0]

        def put_whole(ref, val):
            ref[0] = val

        for g in range(SGU_GROUPS):
            def pick_g(ref, g=g):
                return ref[0, g]

            def put_g(ref, val, g=g):
                ref[0, g] = val

            step(0, tot[_P_WSP + g * SGU_BLOCK:_P_WSP + (g + 1) * SGU_BLOCK], pick_g, put_g)
        step(1, tot[pl.ds(pl.multiple_of(_P_WGU + GLA_RANK * chip, GLA_RANK), GLA_RANK), :], whole, put_whole)
        for k, (base, chunks) in enumerate(((_P_NORM, 8), (_P_NORM + 8, 8), (_P_NORM + 16, 8), (_P_NORM + 24, 8), (_P_BG, 4))):
            for q in range(chunks):
                def pick_q(ref, q=q):
                    return ref[:, LANES * q:LANES * (q + 1)]

                def put_q(ref, val, q=q):
                    ref[:, LANES * q:LANES * (q + 1)] = val

                step(2 + k, tot[base + q:base + q + 1], pick_q, put_q)
        step(7, tot[_P_BSP:_P_BSP + SGU_GROUPS], whole, put_whole)
        for k in range(3):
            mine = tot[pl.ds(pl.multiple_of(_P_HEAD + 32 * k + 8 * chip, 8), 8), :]
            step(8 + k, mine[0:4, 0:64], whole, put_whole)

    shapes = [_sds(w.shape, F32) for w in ws]
    res = pl.pallas_call(
        body, name="small_update", in_specs=[_whole()] * (12 + 3 * n), out_specs=[_whole()] * (1 + 4 * n),
        out_shape=[_sds((1, 1), F32)] + shapes * 4,
        scratch_shapes=[pltpu.VMEM((_P_ROWS, LANES), F32), pltpu.VMEM((N_DEV, _P_ROWS, LANES), F32),
                        pltpu.VMEM((_P_ROWS, LANES), F32), pltpu.SemaphoreType.DMA((N_DEV - 1,)),
                        pltpu.SemaphoreType.DMA((N_DEV - 1,))],
        compiler_params=pltpu.CompilerParams(vmem_limit_bytes=VMEM_LIMIT_BYTES),
    )(*dgrads, *ws, *ms, *vs)
    return res[0].reshape(()), [list(res[1 + i * n:1 + (i + 1) * n]) for i in range(4)]


def _w_in_pieces():
    blk = D_IN // N_CHIPS
    pieces = []
    for s in range(len(_IN_SPLITS)):
        lo_s, hi_s = _IN_STARTS[s], _IN_STARTS[s + 1]
        for j in range(N_CHIPS):
            lo, hi = max(lo_s, j * blk), min(hi_s, (j + 1) * blk)
            if lo < hi:
                pieces.append((j, lo - j * blk, _IN_DST[s] + lo - lo_s, hi - lo))
    return pieces


def _relayout_w_in(gathered):
    _, rows, blk = gathered.shape
    tr = 256

    def body(g_ref, o_ref):
        o_ref[:, OFF_AL:N_ALL] = jnp.zeros((tr, LANES), BF16)
        for j, src, dst, w in _w_in_pieces():
            o_ref[:, dst:dst + w] = g_ref[j, :, src:src + w]

    res, _ = _call(body, name="relayout_w_in", grid=(rows // tr,), parallel=True,
                   in_specs=[pl.BlockSpec((N_CHIPS, tr, blk), lambda i: (0, i, 0))],
                   out_specs=[pl.BlockSpec((tr, N_ALL), lambda i: (i, 0))],
                   out_shape=[_sds((rows, N_ALL), BF16)], args=(gathered,))
    return res[0]


def _update_row_tile(rows):
    for t in range(min(rows, 256), 7, -8):
        if rows % t == 0:
            return t
    return rows


def _my_half(first_ref, second_ref):
    return jnp.where(lax.axis_index("c") == 0, first_ref[...], second_ref[...])


def _presum_w_in(dws, theirs):
    hr = theirs.shape[0]
    blk = D_IN // N_CHIPS
    tr = 128
    nh = hr // tr

    def body(*refs):
        dw_refs, (q_ref, o_ref, s_scr) = refs[:2 * len(dws)], refs[2 * len(dws):]
        for p, (a, off) in enumerate(dws):
            w = a.shape[1]
            s_scr[:, off:off + w] = (_my_half(dw_refs[2 * p], dw_refs[2 * p + 1]) + q_ref[:, off:off + w]).astype(BF16)
        for j, src, dst, w in _w_in_pieces():
            o_ref[j, :, src:src + w] = s_scr[:, dst:dst + w]

    in_specs, args = [], []
    for a, _ in dws:
        w = a.shape[1]
        in_specs += [pl.BlockSpec((tr, w), lambda i: (i, 0)), pl.BlockSpec((tr, w), lambda i: (i + nh, 0))]
        args += [a, a]
    res, _ = _call(body, name="presum_w_in", grid=(nh,), parallel=True,
                   in_specs=in_specs + [pl.BlockSpec((tr, N_ALL), lambda i: (i, 0))],
                   out_specs=[pl.BlockSpec((N_CHIPS, tr, blk), lambda i: (0, i, 0))],
                   out_shape=[_sds((N_CHIPS, hr, blk), BF16)], scratch_shapes=[pltpu.VMEM((tr, N_ALL), BF16)],
                   args=(*args, theirs))
    return res[0]


def _presum(dw, theirs, name):
    if dw.ndim == 4:
        _, _, hr, c = dw.shape
        tr = _update_row_tile(hr)
        first = pl.BlockSpec((1, 1, tr, c), lambda j, i: (j, 0, i, 0))
        second = pl.BlockSpec((1, 1, tr, c), lambda j, i: (j, 1, i, 0))
        other = pl.BlockSpec((1, tr, c), lambda j, i: (j, i, 0))
    else:
        hr, c = dw.shape[0] // 2, dw.shape[1] // N_CHIPS
        tr = _update_row_tile(hr)
        nh = hr // tr
        first = pl.BlockSpec((tr, c), lambda j, i: (i, j))
        second = pl.BlockSpec((tr, c), lambda j, i: (i + nh, j))
        other = pl.BlockSpec((tr, c), lambda j, i: (i, j))

    def body(a_ref, b_ref, q_ref, o_ref):
        mine = _my_half(a_ref, b_ref).reshape(tr, c)
        o_ref[...] = (mine + q_ref[...].reshape(tr, c)).astype(BF16).reshape(o_ref.shape)

    res, _ = _call(body, name=name, grid=(N_CHIPS, hr // tr), parallel=True, in_specs=[first, second, other],
                   out_specs=[pl.BlockSpec((1, tr, c), lambda j, i: (j, i, 0))],
                   out_shape=[_sds((N_CHIPS, hr, c), BF16)], args=(dw, dw, theirs))
    return res[0]


def _sum_slots(own, slots, name):
    rows, cols = own.shape
    tr = _update_row_tile(rows)

    def body(own_ref, s_ref, o_ref):
        acc = own_ref[...].astype(F32)
        for j in range(N_PEER):
            acc = acc + s_ref[j].astype(F32)
        o_ref[...] = acc

    res, _ = _call(body, name=name, grid=(rows // tr,), parallel=True,
                   in_specs=[pl.BlockSpec((tr, cols), lambda i: (i, 0)), pl.BlockSpec((N_PEER, tr, cols), lambda i: (0, i, 0))],
                   out_specs=[pl.BlockSpec((tr, cols), lambda i: (i, 0))], out_shape=[_sds((rows, cols), F32)],
                   args=(own, slots))
    return res[0]


def _adamw(w, m, v, g_mine, g_theirs, name, job=None):
    rows, cols = w.shape
    halves = g_theirs is not None
    tr = _update_row_tile(rows // 2 if halves else rows)
    nh = (rows // 2) // tr if halves else rows // tr
    bc1 = 1.0 - ADAM_B1 ** ADAM_STEP
    bc2 = 1.0 - ADAM_B2 ** ADAM_STEP

    def body(w_ref, m_ref, v_ref, *rest):
        g_refs, (g_out, d_out, m_out, v_out) = rest[:-4], rest[-4:]
        if halves:
            mine_here = (pl.program_id(0) // nh) == lax.axis_index("c")
            g = jnp.where(mine_here, g_refs[0][...], g_refs[1][...])
        else:
            g = g_refs[0][...]
        m2 = ADAM_B1 * m_ref[...] + (1.0 - ADAM_B1) * g
        v2 = ADAM_B2 * v_ref[...] + (1.0 - ADAM_B2) * (g * g)
        g_out[...] = g
        m_out[...] = m2
        v_out[...] = v2
        d_out[...] = -ADAM_LR * ((m2 / bc1) / (jnp.sqrt(v2 / bc2) + ADAM_EPS) + ADAM_WD * w_ref[...])

    spec = pl.BlockSpec((tr, cols), lambda i: (i, 0))
    g_spec = pl.BlockSpec((tr, cols), lambda i: (i % nh, 0))
    g_args = (g_mine, g_theirs) if halves else (g_mine,)
    return _call(body, name=name, grid=(rows // tr,), parallel=True, in_specs=[spec] * 3 + [g_spec] * len(g_args),
                 out_specs=[spec] * 4, out_shape=[_sds((rows, cols), F32)] * 4, args=(w, m, v, *g_args), job=job)


def _inproj_fwd(x, g1, w_all, job=None):
    T = x.shape[0]
    tT = _row_tile(T, 512)

    def body(x_ref, g_ref, w_ref, a_ref, proj_ref, alow_ref):
        xv = x_ref[...]
        a = (xv * _rms_stats(xv) * g_ref[...]).astype(BF16)
        a_ref[...] = a
        for j in range(N_MAIN // 1024):
            cols = slice(j * 1024, (j + 1) * 1024)
            proj_ref[:, cols] = _dot(a, w_ref[:, cols]).astype(BF16)
        alow_ref[...] = _dot(a, w_ref[:, N_MAIN:N_ALL])

    row = lambda w: pl.BlockSpec((tT, w), lambda i: (i, 0))
    return _call(
        body, name="inproj_fwd", grid=(T // tT,), parallel=True,
        in_specs=[row(D_MODEL), pl.BlockSpec((1, D_MODEL), lambda i: (0, 0)), _whole()],
        out_specs=[row(D_MODEL), row(N_MAIN), row(LANES)],
        out_shape=[_sds((T, D_MODEL), BF16), _sds((T, N_MAIN), BF16), _sds((T, LANES), F32)],
        args=(x, g1, w_all), job=job)


def _gla_decay_terms(al_ref, wgu_ref, bg_ref, tT):
    logit = _dot_f32(al_ref[...], wgu_ref[...]) + bg_ref[...]
    la = _log_sigmoid(logit) * (1.0 / GLA_TAU)
    delta = _dot_exact_lhs(_chunk_masks(tT, upper=True), la)
    return logit, la, delta


def _gla_fwd(proj, alow, wgu, b_gate, gn, job=None):
    T = proj.shape[0]
    tT = _row_tile(T, 512)
    nc = tT // CHUNK

    def body(q_ref, k_ref, v_ref, r_ref, al_ref, wgu_ref, bg_ref, gn_ref, y_ref, st_ref, s_scr):
        @pl.when(pl.program_id(0) == 0)
        def _():
            s_scr[...] = jnp.zeros_like(s_scr)

        _, la, delta = _gla_decay_terms(al_ref, wgu_ref, bg_ref, tT)
        kdec = (k_ref[...].astype(F32) * jnp.exp(delta)).astype(BF16)
        heads = range(GLA_HEADS)
        kcs = [slice(h * GLA_DK, (h + 1) * GLA_DK) for h in heads]
        vcs = [slice(h * GLA_DV, (h + 1) * GLA_DV) for h in heads]
        state = [s_scr[h] for h in heads]
        for c in range(nc):
            rows = slice(c * CHUNK, (c + 1) * CHUNK)
            first = slice(c * CHUNK, c * CHUNK + 1)
            dec = jnp.exp(la[first, :] + delta[first, :])
            upd_t = [_dot(v_ref[rows, vcs[h]], kdec[rows, kcs[h]], _TN) for h in heads]
            qs = [(q_ref[rows, kcs[h]].astype(F32) * (GLA_DK ** -0.5)).astype(BF16) for h in heads]
            for h in heads:
                state[h] = state[h] * dec[:, kcs[h]] + upd_t[h]
                st_ref[c, h] = state[h]
            o = [_dot(qs[h], state[h].astype(BF16), _NT) for h in heads]
            for h in heads:
                on = o[h] * _rms_stats(o[h]) * gn_ref[:, vcs[h]]
                rr = r_ref[rows, vcs[h]].astype(F32)
                y_ref[rows, vcs[h]] = (on * (rr * _sigmoid(rr))).astype(BF16)
        for h in heads:
            s_scr[h] = state[h]

    blk = lambda w, j: pl.BlockSpec((tT, w), lambda i: (i, j))
    return _call(
        body, name="gla_fwd", grid=(T // tT,),
        in_specs=[blk(512, 0), blk(512, 1), blk(1024, 1), blk(1024, 2), blk(LANES, 0), _whole(), _whole(), _whole()],
        out_specs=[pl.BlockSpec((tT, GLA_V), lambda i: (i, 0)),
                   pl.BlockSpec((nc, GLA_HEADS, GLA_DV, GLA_DK), lambda i: (i, 0, 0, 0))],
        out_shape=[_sds((T, GLA_V), BF16), _sds((T // CHUNK, GLA_HEADS, GLA_DV, GLA_DK), F32)],
        scratch_shapes=[pltpu.VMEM((GLA_HEADS, GLA_DV, GLA_DK), F32)],
        args=(proj, proj, proj, proj, alow, wgu, b_gate, gn), job=job)


def _sgu_mask():
    i = lax.broadcasted_iota(jnp.int32, (SGU_BLOCK, SGU_BLOCK), 0)
    j = lax.broadcasted_iota(jnp.int32, (SGU_BLOCK, SGU_BLOCK), 1)
    return lax.shift_right_logical(j, 6) <= lax.shift_right_logical(i, 6)


def _sgu_fwd(proj, ln_g, ln_b, w_sp, b_sp_t):
    T = proj.shape[0]
    tT = _row_tile(T, 512)
    nb = tT // SGU_BLOCK

    def body(su_ref, sv_ref, lg_ref, lb_ref, w_ref, b_ref, y_ref):
        mask = _sgu_mask()
        for g in range(SGU_GROUPS):
            gc = slice(g * SGU_DG, (g + 1) * SGU_DG)
            wm = jnp.where(mask, w_ref[g], 0.0).astype(BF16)
            vf = _gelu(sv_ref[:, gc].astype(F32))
            mu = jnp.mean(vf, axis=-1, keepdims=True)
            vc = vf - mu
            rstd = lax.rsqrt(jnp.mean(vc * vc, axis=-1, keepdims=True) + EPS)
            vn = (vc * rstd * lg_ref[:, gc] + lb_ref[:, gc]).astype(BF16)
            u = _gelu(su_ref[:, gc].astype(F32))
            for b in range(nb):
                rows = slice(b * SGU_BLOCK, (b + 1) * SGU_BLOCK)
                mixed = _dot(wm, vn[rows, :]) + b_ref[:, g:g + 1]
                y_ref[rows, gc] = (u[rows, :] * mixed).astype(BF16)

    blk = lambda j: pl.BlockSpec((tT, 1024), lambda i: (i, j))
    res, _ = _call(body, name="sgu_fwd", grid=(T // tT,), parallel=True,
                   in_specs=[blk(3), blk(4), _whole(), _whole(), _whole(), _whole()],
                   out_specs=[pl.BlockSpec((tT, 1024), lambda i: (i, 0))], out_shape=[_sds((T, 1024), BF16)],
                   args=(proj, proj, ln_g, ln_b, w_sp, b_sp_t))
    return res[0]


def _merge_fwd(x, proj, y_gla, y_sgu, w_bg, w_bs, w_o, g_pm, job=None):
    T = x.shape[0]
    tT = _row_tile(T, 512)

    def body(x_ref, gg_ref, gs_ref, yg_ref, ys_ref, wbg_ref, wbs_ref, wo_ref, g_ref,
             zg_ref, zs_ref, mg_ref, mix_ref, x1_ref):
        zg = _dot(yg_ref[...], wbg_ref[...])
        zs = _dot(ys_ref[...], wbs_ref[...])
        zg_ref[...] = zg.astype(BF16)
        zs_ref[...] = zs.astype(BF16)
        merged = (_sigmoid(gg_ref[...].astype(F32)) * zg + _sigmoid(gs_ref[...].astype(F32)) * zs).astype(BF16)
        mg_ref[...] = merged
        mix = _dot(merged, wo_ref[...])
        mix_ref[...] = mix
        x1_ref[...] = x_ref[...] + mix * _rms_stats(mix) * g_ref[...]

    row = pl.BlockSpec((tT, D_MODEL), lambda i: (i, 0))
    blk = lambda j: pl.BlockSpec((tT, 1024), lambda i: (i, j))
    sds = lambda dt: _sds((T, D_MODEL), dt)
    return _call(body, name="merge_fwd", grid=(T // tT,), parallel=True,
                 in_specs=[row, blk(5), blk(6), row, row, _whole(), _whole(), _whole(),
                           pl.BlockSpec((1, D_MODEL), lambda i: (0, 0))],
                 out_specs=[row] * 5, out_shape=[sds(BF16), sds(BF16), sds(BF16), sds(F32), sds(F32)],
                 args=(x, proj, proj, y_gla, y_sgu, w_bg, w_bs, w_o, g_pm), job=job)


def _ffn_fwd_bwd(x1, tgt, w_fi_top, w_fi_bot, w_fo, g_pf, g_po):
    T = x1.shape[0]
    tT = _row_tile(T, 256)
    half = D_FF // 2
    kh = D_MODEL // 2

    def body(x1_ref, t_ref, top_ref, bot_ref, wfo_ref, gpf_ref, gpo_ref,
             h_ref, f_ref, dgu_ref, dy_ref, dx1_ref, loss_ref, dgpf_ref, dgpo_ref, gu_scr):
        @pl.when(pl.program_id(0) == 0)
        def _():
            loss_ref[...] = jnp.zeros_like(loss_ref)
            dgpf_ref[...] = jnp.zeros_like(dgpf_ref)
            dgpo_ref[...] = jnp.zeros_like(dgpo_ref)

        x1v = x1_ref[...]
        r2 = _rms_stats(x1v)
        h = (x1v * r2 * gpf_ref[...]).astype(BF16)
        h_ref[...] = h
        y = jnp.zeros((tT, D_MODEL), F32)
        for j in range(2):
            gc = slice(j * half, (j + 1) * half)
            uc = slice(D_FF + j * half, D_FF + (j + 1) * half)
            gate = _dot(h[:, :kh], top_ref[j]) + _dot(h[:, kh:], bot_ref[j])
            up = _dot(h[:, :kh], top_ref[2 + j]) + _dot(h[:, kh:], bot_ref[2 + j])
            gu_scr[:, gc] = gate
            gu_scr[:, uc] = up
            f = (gate * _sigmoid(gate) * up).astype(BF16)
            f_ref[:, gc] = f
            y = y + _dot(f, wfo_ref[gc, :])
        r3 = _rms_stats(y)
        x2 = x1v + y * r3 * gpo_ref[...]
        err = x2 - t_ref[...]
        loss_ref[...] += jnp.sum(err * err) * (0.5 / D_MODEL)
        dx2 = err * (1.0 / D_MODEL)
        dy, dg = _rms_bwd(dx2, y, r3, gpo_ref[...])
        dgpo_ref[...] += jnp.sum(dg, axis=0, keepdims=True)
        dyb = dy.astype(BF16)
        dy_ref[...] = dyb
        dh_top = jnp.zeros((tT, kh), F32)
        dh_bot = jnp.zeros((tT, kh), F32)
        for j in range(2):
            gc = slice(j * half, (j + 1) * half)
            uc = slice(D_FF + j * half, D_FF + (j + 1) * half)
            df = _dot(dyb, wfo_ref[gc, :], _NT)
            gate = gu_scr[:, gc]
            up = gu_scr[:, uc]
            sg = _sigmoid(gate)
            dgate = (df * up * (sg * (1.0 + gate * (1.0 - sg)))).astype(BF16)
            dup = (df * (gate * sg)).astype(BF16)
            dgu_ref[:, gc] = dgate
            dgu_ref[:, uc] = dup
            dh_top = dh_top + _dot(dgate, top_ref[j], _NT) + _dot(dup, top_ref[2 + j], _NT)
            dh_bot = dh_bot + _dot(dgate, bot_ref[j], _NT) + _dot(dup, bot_ref[2 + j], _NT)
        dh = jnp.concatenate([dh_top, dh_bot], axis=1)
        dx1n, dg2 = _rms_bwd(dh, x1v, r2, gpf_ref[...])
        dgpf_ref[...] += jnp.sum(dg2, axis=0, keepdims=True)
        dx1_ref[...] = dx2 + dx1n

    row = lambda w: pl.BlockSpec((tT, w), lambda i: (i, 0))
    vec = pl.BlockSpec((1, D_MODEL), lambda i: (0, 0))
    res, _ = _call(
        body, name="ffn_fwd_bwd", grid=(T // tT,),
        in_specs=[row(D_MODEL), row(D_MODEL), _whole(), _whole(), _whole(), vec, vec],
        out_specs=[row(D_MODEL), row(D_FF), row(2 * D_FF), row(D_MODEL), row(D_MODEL),
                   pl.BlockSpec((1, LANES), lambda i: (0, 0)), vec, vec],
        out_shape=[_sds((T, D_MODEL), BF16), _sds((T, D_FF), BF16), _sds((T, 2 * D_FF), BF16), _sds((T, D_MODEL), BF16),
                   _sds((T, D_MODEL), F32), _sds((1, LANES), F32), _sds((1, D_MODEL), F32), _sds((1, D_MODEL), F32)],
        scratch_shapes=[pltpu.VMEM((tT, 2 * D_FF), F32)], args=(x1, tgt, w_fi_top, w_fi_bot, w_fo, g_pf, g_po))
    return res


def _merge_bwd(dx1, mix, proj, zg, zs, w_bg, w_bs, w_o, g_pm, job=None):
    T = dx1.shape[0]
    tT = _row_tile(T, 512)

    def body(dx1_ref, mix_ref, gg_ref, gs_ref, zg_ref, zs_ref, wbg_ref, wbs_ref, wo_ref, g_ref,
             dmix_ref, dzg_ref, dzs_ref, dgate_ref, dyg_ref, dys_ref, dgpm_ref):
        @pl.when(pl.program_id(0) == 0)
        def _():
            dgpm_ref[...] = jnp.zeros_like(dgpm_ref)

        mix = mix_ref[...]
        dmix, dg = _rms_bwd(dx1_ref[...], mix, _rms_stats(mix), g_ref[...])
        dgpm_ref[...] += jnp.sum(dg, axis=0, keepdims=True)
        dmb = dmix.astype(BF16)
        dmix_ref[...] = dmb
        dmerged = _dot(dmb, wo_ref[...], _NT)
        for k, (gate_ref, z_ref, w_ref, dz_ref, dy_ref) in enumerate((
                (gg_ref, zg_ref, wbg_ref, dzg_ref, dyg_ref), (gs_ref, zs_ref, wbs_ref, dzs_ref, dys_ref))):
            sg = _sigmoid(gate_ref[...].astype(F32))
            dz = (dmerged * sg).astype(BF16)
            dz_ref[...] = dz
            dgate_ref[:, k * 1024:(k + 1) * 1024] = (dmerged * z_ref[...].astype(F32) * (sg * (1.0 - sg))).astype(BF16)
            dy_ref[...] = _dot(dz, w_ref[...], _NT).astype(BF16)

    row = pl.BlockSpec((tT, D_MODEL), lambda i: (i, 0))
    blk = lambda j: pl.BlockSpec((tT, 1024), lambda i: (i, j))
    vec = pl.BlockSpec((1, D_MODEL), lambda i: (0, 0))
    sds = _sds((T, D_MODEL), BF16)
    return _call(
        body, name="merge_bwd", grid=(T // tT,),
        in_specs=[row, row, blk(5), blk(6), row, row, _whole(), _whole(), _whole(), vec],
        out_specs=[row, row, row, pl.BlockSpec((tT, W_MRG), lambda i: (i, 0)), row, row, vec],
        out_shape=[sds, sds, sds, _sds((T, W_MRG), BF16), sds, sds, _sds((1, D_MODEL), F32)],
        args=(dx1, mix, proj, proj, zg, zs, w_bg, w_bs, w_o, g_pm), job=job)


def _sgu_bwd(proj, dy_sgu, ln_g, ln_b, w_sp, b_sp_t, job=None):
    T = proj.shape[0]
    tT = _row_tile(T, 512)
    nb = tT // SGU_BLOCK

    def body(su_ref, sv_ref, dy_ref, lg_ref, lb_ref, w_ref, b_ref, dp_ref, dw_ref, dbt_ref, dlg_ref, dlb_ref):
        @pl.when(pl.program_id(0) == 0)
        def _():
            dw_ref[...] = jnp.zeros_like(dw_ref)
            dbt_ref[...] = jnp.zeros_like(dbt_ref)
            dlg_ref[...] = jnp.zeros_like(dlg_ref)
            dlb_ref[...] = jnp.zeros_like(dlb_ref)

        mask = _sgu_mask()
        lane = lax.broadcasted_iota(jnp.int32, (SGU_BLOCK, LANES), 1)
        for g in range(SGU_GROUPS):
            gc = slice(g * SGU_DG, (g + 1) * SGU_DG)
            gc_v = slice(1024 + g * SGU_DG, 1024 + (g + 1) * SGU_DG)
            wm = jnp.where(mask, w_ref[g], 0.0).astype(BF16)
            vf, dvf_dsv = _gelu_and_grad(sv_ref[:, gc].astype(F32))
            mu = jnp.mean(vf, axis=-1, keepdims=True)
            vc = vf - mu
            rstd = lax.rsqrt(jnp.mean(vc * vc, axis=-1, keepdims=True) + EPS)
            vhat = vc * rstd
            vn = (vhat * lg_ref[:, gc] + lb_ref[:, gc]).astype(BF16)
            u, du_dsu = _gelu_and_grad(su_ref[:, gc].astype(F32))
            dy = dy_ref[:, gc].astype(F32)
            dmixed = (dy * u).astype(BF16)
            dvn_parts = []
            dw_acc = jnp.zeros((SGU_BLOCK, SGU_BLOCK), F32)
            db_acc = jnp.zeros((SGU_BLOCK, 1), F32)
            for b in range(nb):
                rows = slice(b * SGU_BLOCK, (b + 1) * SGU_BLOCK)
                mixed = _dot(wm, vn[rows, :]) + b_ref[:, g:g + 1]
                dp_ref[rows, gc] = (dy[rows, :] * mixed * du_dsu[rows, :]).astype(BF16)
                dvn_parts.append(_dot(wm, dmixed[rows, :], _TN))
                dw_acc = dw_acc + _dot(dmixed[rows, :], vn[rows, :], _NT)
                db_acc = db_acc + jnp.sum(dmixed[rows, :].astype(F32), axis=-1, keepdims=True)
            dw_ref[g] += jnp.where(mask, dw_acc, 0.0)
            dbt_ref[...] += jnp.where(lane == g, db_acc, 0.0)
            dvn = jnp.concatenate(dvn_parts, axis=0)
            dlg_ref[:, gc] += jnp.sum(dvn * vhat, axis=0, keepdims=True)
            dlb_ref[:, gc] += jnp.sum(dvn, axis=0, keepdims=True)
            dvh = dvn * lg_ref[:, gc]
            dvf = rstd * (dvh - jnp.mean(dvh, axis=-1, keepdims=True)
                          - vhat * jnp.mean(dvh * vhat, axis=-1, keepdims=True))
            dp_ref[:, gc_v] = (dvf * dvf_dsv).astype(BF16)

    blk = lambda j: pl.BlockSpec((tT, 1024), lambda i: (i, j))
    row = lambda w: pl.BlockSpec((tT, w), lambda i: (i, 0))
    vec = pl.BlockSpec((1, 1024), lambda i: (0, 0))
    return _call(
        body, name="sgu_bwd", grid=(T // tT,),
        in_specs=[blk(3), blk(4), row(1024), _whole(), _whole(), _whole(), _whole()],
        out_specs=[row(W_SGU), pl.BlockSpec((SGU_GROUPS, SGU_BLOCK, SGU_BLOCK), lambda i: (0, 0, 0)),
                   pl.BlockSpec((SGU_BLOCK, LANES), lambda i: (0, 0)), vec, vec],
        out_shape=[_sds((T, W_SGU), BF16), _sds((SGU_GROUPS, SGU_BLOCK, SGU_BLOCK), F32), _sds((SGU_BLOCK, LANES), F32),
                   _sds((1, 1024), F32), _sds((1, 1024), F32)],
        args=(proj, proj, dy_sgu, ln_g, ln_b, w_sp, b_sp_t), job=job)


def _gla_bwd(proj, alow, wgu, b_gate, gn, states, dy_gla, job=None):
    T = proj.shape[0]
    tT = _row_tile(T, 512)
    nc = tT // CHUNK
    nt = T // tT

    def body(q_ref, k_ref, v_ref, r_ref, al_ref, wgu_ref, bg_ref, gn_ref, st_ref, sp_ref, dy_ref,
             dp_ref, dal_ref, dgn_ref, dbg_ref, dwgu_ref, g_scr, dd_scr, dt_scr):
        step = pl.program_id(0)

        @pl.when(step == 0)
        def _():
            g_scr[...] = jnp.zeros_like(g_scr)
            dgn_ref[...] = jnp.zeros_like(dgn_ref)
            dbg_ref[...] = jnp.zeros_like(dbg_ref)
            dwgu_ref[...] = jnp.zeros_like(dwgu_ref)

        has_prev = jnp.where(step == nt - 1, 0.0, 1.0)
        logit, la, delta = _gla_decay_terms(al_ref, wgu_ref, bg_ref, tT)
        e = jnp.exp(delta)
        kdec_f = k_ref[...].astype(F32) * e
        kdec = kdec_f.astype(BF16)
        heads = range(GLA_HEADS)
        kcs = [slice(h * GLA_DK, (h + 1) * GLA_DK) for h in heads]
        vcs = [slice(h * GLA_DV, (h + 1) * GLA_DV) for h in heads]
        carry = [g_scr[h] for h in heads]
        dgn_acc = [jnp.zeros((1, GLA_DV), F32) for _ in heads]
        for c in reversed(range(nc)):
            rows = slice(c * CHUNK, (c + 1) * CHUNK)
            first = slice(c * CHUNK, c * CHUNK + 1)
            dec = jnp.exp(la[first, :] + delta[first, :])
            s_b = [st_ref[c, h].astype(BF16) for h in heads]
            qs = [(q_ref[rows, kcs[h]].astype(F32) * (GLA_DK ** -0.5)).astype(BF16) for h in heads]
            o = [_dot(qs[h], s_b[h], _NT) for h in heads]
            do = []
            for h in heads:
                rstd = _rms_stats(o[h])
                ohat = o[h] * rstd
                gnh = gn_ref[:, vcs[h]]
                dy = dy_ref[rows, vcs[h]].astype(F32)
                rr = r_ref[rows, vcs[h]].astype(F32)
                sg = _sigmoid(rr)
                don = dy * (rr * sg)
                dp_ref[rows, OFF_R + h * GLA_DV:OFF_R + (h + 1) * GLA_DV] = (
                    dy * (ohat * gnh) * (sg * (1.0 + rr * (1.0 - sg)))).astype(BF16)
                dgn_acc[h] = dgn_acc[h] + jnp.sum(don * ohat, axis=0, keepdims=True)
                dn = don * gnh
                do.append((rstd * (dn - ohat * jnp.mean(dn * ohat, axis=-1, keepdims=True))).astype(BF16))
            dq = [_dot(do[h], s_b[h]) for h in heads]
            g_t = [_dot(do[h], qs[h], _TN) + carry[h] for h in heads]
            g_b = [g_t[h].astype(BF16) for h in heads]
            dv = [_dot(kdec[rows, kcs[h]], g_b[h], _NT) for h in heads]
            dkdec = [_dot(v_ref[rows, vcs[h]], g_b[h]) for h in heads]
            for h in heads:
                s_prev = st_ref[c - 1, h] if c > 0 else sp_ref[0, h] * has_prev
                ddec = jnp.sum(g_t[h] * s_prev, axis=0, keepdims=True)
                carry[h] = g_t[h] * dec[:, kcs[h]]
                dp_ref[rows, OFF_Q + h * GLA_DK:OFF_Q + (h + 1) * GLA_DK] = (dq[h] * (GLA_DK ** -0.5)).astype(BF16)
                dp_ref[rows, OFF_V + h * GLA_DV:OFF_V + (h + 1) * GLA_DV] = dv[h].astype(BF16)
                dp_ref[rows, OFF_K + h * GLA_DK:OFF_K + (h + 1) * GLA_DK] = (dkdec[h] * e[rows, kcs[h]]).astype(BF16)
                dd_scr[rows, kcs[h]] = dkdec[h] * kdec_f[rows, kcs[h]]
                dt_scr[rows, kcs[h]] = jnp.broadcast_to(ddec * dec[:, kcs[h]], (CHUNK, GLA_DK))
        for h in heads:
            g_scr[h] = carry[h]
            dgn_ref[:, vcs[h]] += dgn_acc[h]
        dla = _dot_exact_lhs(_chunk_masks(tT, upper=False), dd_scr[...]) + dt_scr[...]
        dlogit = dla * (1.0 / GLA_TAU) * _sigmoid(-logit)
        dbg_ref[...] += jnp.sum(dlogit, axis=0, keepdims=True)
        dwgu_ref[...] += _dot_f32(al_ref[...], dlogit, _TN)
        dal_ref[...] = _dot_f32(dlogit, wgu_ref[...], _NT).astype(BF16)

    rev = lambda i: nt - 1 - i
    blk = lambda w, j: pl.BlockSpec((tT, w), lambda i: (rev(i), j))
    st_blk = pl.BlockSpec((nc, GLA_HEADS, GLA_DV, GLA_DK), lambda i: (rev(i), 0, 0, 0))
    sp_blk = pl.BlockSpec((1, GLA_HEADS, GLA_DV, GLA_DK), lambda i: (jnp.maximum(rev(i) * nc - 1, 0), 0, 0, 0))
    return _call(
        body, name="gla_bwd", grid=(nt,),
        in_specs=[blk(512, 0), blk(512, 1), blk(1024, 1), blk(1024, 2), blk(LANES, 0), _whole(), _whole(), _whole(),
                  st_blk, sp_blk, blk(GLA_V, 0)],
        out_specs=[blk(W_GLA, 0), blk(LANES, 0), pl.BlockSpec((1, GLA_V), lambda i: (0, 0)),
                   pl.BlockSpec((1, GLA_QK), lambda i: (0, 0)), pl.BlockSpec((LANES, GLA_QK), lambda i: (0, 0))],
        out_shape=[_sds((T, W_GLA), BF16), _sds((T, LANES), BF16), _sds((1, GLA_V), F32), _sds((1, GLA_QK), F32),
                   _sds((LANES, GLA_QK), F32)],
        scratch_shapes=[pltpu.VMEM((GLA_HEADS, GLA_DV, GLA_DK), F32), pltpu.VMEM((tT, GLA_QK), F32),
                        pltpu.VMEM((tT, GLA_QK), F32)],
        args=(proj, proj, proj, proj, alow, wgu, b_gate, gn, states, states, dy_gla), job=job)


def _inproj_bwd(x, dx1, g1, w_all, dparts, job=None):
    T = x.shape[0]
    tT = _row_tile(T, 512)
    offs = (0, W_GLA, W_GLA + W_SGU, N_MAIN)

    def body(x_ref, dx1_ref, g_ref, w_ref, *rest):
        part_refs, (dx_ref, dg_ref) = rest[:len(offs)], rest[len(offs):]

        @pl.when(pl.program_id(0) == 0)
        def _():
            dg_ref[...] = jnp.zeros_like(dg_ref)

        da = jnp.zeros((tT, D_MODEL), F32)
        for off, p_ref in zip(offs, part_refs):
            da = da + _dot(p_ref[...], w_ref[:, off:off + p_ref.shape[1]], _NT)
        xv = x_ref[...]
        dx, dg = _rms_bwd(da, xv, _rms_stats(xv), g_ref[...])
        dg_ref[...] += jnp.sum(dg, axis=0, keepdims=True)
        dx_ref[...] = dx1_ref[...] + dx

    row = lambda w: pl.BlockSpec((tT, w), lambda i: (i, 0))
    vec = pl.BlockSpec((1, D_MODEL), lambda i: (0, 0))
    return _call(
        body, name="inproj_bwd", grid=(T // tT,),
        in_specs=[row(D_MODEL), row(D_MODEL), vec, _whole()] + [row(p.shape[1]) for p in dparts],
        out_specs=[row(D_MODEL), vec], out_shape=[_sds((T, D_MODEL), F32), _sds((1, D_MODEL), F32)],
        args=(x, dx1, g1, w_all, *dparts), job=job)


def _tn_matmul(a, b, name, job=None):
    T, M = a.shape
    N = b.shape[1]
    tk = _row_tile(T, 2048)
    tm = M if M <= 1024 else 1408
    tn = N if N <= 1024 else (1024 if N % 1024 == 0 else 1408)
    assert M % tm == 0 and N % tn == 0

    def body(a_ref, b_ref, o_ref):
        @pl.when(pl.program_id(2) == 0)
        def _():
            o_ref[...] = jnp.zeros_like(o_ref)

        o_ref[...] += _dot(a_ref[...], b_ref[...], _TN)

    res, jres = _call(
        body, name=name, grid=(M // tm, N // tn, T // tk),
        in_specs=[pl.BlockSpec((tk, tm), lambda i, j, k: (k, i)), pl.BlockSpec((tk, tn), lambda i, j, k: (k, j))],
        out_specs=[pl.BlockSpec((tm, tn), lambda i, j, k: (i, j))], out_shape=[_sds((M, N), F32)], args=(a, b), job=job)
    return res[0], jres


def _pad_rows(a, rows=8):
    return jnp.pad(a, ((0, rows - a.shape[0]), (0, LANES - a.shape[1])))


def _halves_view(dw):
    r = dw.shape[0] // N_CHIPS
    return dw.reshape(N_CHIPS, 2, r // 2, dw.shape[1])


def kernel(x, norm_pre_mix, w_in, w_gate_up, b_gate, gla_norm, sgu_ln_g, sgu_ln_b, w_spatial, b_spatial, w_branch_gla, w_branch_sgu, w_out, norm_post_mix, norm_pre_ffn, w_ffn_in, w_ffn_out, norm_post_ffn, loss_target, m_norm_pre_mix, m_w_in, m_w_gate_up, m_b_gate, m_gla_norm, m_sgu_ln_g, m_sgu_ln_b, m_w_spatial, m_b_spatial, m_w_branch_gla, m_w_branch_sgu, m_w_out, m_norm_post_mix, m_norm_pre_ffn, m_w_ffn_in, m_w_ffn_out, m_norm_post_ffn, v_norm_pre_mix, v_w_in, v_w_gate_up, v_b_gate, v_gla_norm, v_sgu_ln_g, v_sgu_ln_b, v_w_spatial, v_b_spatial, v_w_branch_gla, v_w_branch_sgu, v_w_out, v_norm_post_mix, v_norm_pre_ffn, v_w_ffn_in, v_w_ffn_out, v_norm_post_ffn):
    chip = 2 * lax.axis_index("x") + lax.axis_index("y")
    xt, tgt = x[0], loss_target[0]

    tiny = jnp.concatenate([w_gate_up[0], _pad_rows(gla_norm[0]), _pad_rows(sgu_ln_g[0]), _pad_rows(sgu_ln_b[0]),
                            jnp.zeros((8, LANES), F32)], axis=0)
    def with_own(gathered, own):
        if gathered.ndim == 3:
            return lax.dynamic_update_slice(gathered, own[None], (chip, 0, 0))
        return lax.dynamic_update_slice(gathered, own, (0, chip * own.shape[1]))

    w_in_b = w_in[0].astype(BF16)
    g_in, g_tiny = _run_job(_job_gather([w_in_b, tiny], [False, False]), "gather_w_in")
    g_tiny = with_own(g_tiny, tiny)
    w_all = _relayout_w_in(with_own(g_in, w_in_b))
    cols = lambda a: a.transpose(1, 0, 2).reshape(a.shape[1], N_CHIPS * a.shape[2])
    wgu = jnp.pad(cols(g_tiny[:, 0:16]), ((0, LANES - GLA_RANK), (0, 0)))
    gn = cols(g_tiny[:, 16:20, :64]).reshape(1, GLA_V)
    ln_g = cols(g_tiny[:, 24:28, :64]).reshape(1, 1024)
    ln_b = cols(g_tiny[:, 32:36, :64]).reshape(1, 1024)
    b_sp_t = jnp.pad(b_spatial[0].T, ((0, 0), (0, LANES - SGU_GROUPS)))
    w_sp = w_spatial[0]

    own_rows = [w_branch_gla[0].astype(BF16), w_branch_sgu[0].astype(BF16), w_out[0].astype(BF16), w_ffn_out[0].astype(BF16)]
    (a, proj, alow), g_rows = _inproj_fwd(xt, norm_pre_mix, w_all, job=_job_gather(own_rows, [False] * 4))
    rows = lambda g: g.reshape(N_CHIPS * g.shape[1], g.shape[2])
    w_bg, w_bs, w_o, w_fo = [rows(with_own(g, own)) for g, own in zip(g_rows, own_rows)]
    w_fi_b = w_ffn_in[0].astype(BF16)
    fi_top, fi_bot = w_fi_b[:D_MODEL // 2], w_fi_b[D_MODEL // 2:]
    (y_gla, states), (g_top,) = _gla_fwd(proj, alow, wgu, b_gate, gn, job=_job_gather([fi_top], [False]))
    y_sgu = _sgu_fwd(proj, ln_g, ln_b, w_sp, b_sp_t)
    (zg, zs, merged, mix, x1), (g_bot,) = _merge_fwd(xt, proj, y_gla, y_sgu, w_bg, w_bs, w_o, norm_post_mix,
                                                     job=_job_gather([fi_bot], [False]))
    h, f, dgu, dy, dx1, loss, d_gpf, d_gpo = _ffn_fwd_bwd(x1, tgt, with_own(g_top, fi_top), with_own(g_bot, fi_bot),
                                                          w_fo, norm_pre_ffn, norm_post_ffn)

    own_part = lambda c: lax.dynamic_index_in_dim(c, chip, 0, keepdims=False)
    whole = lambda hs: [[(h_, None)] for h_ in hs]
    dw_fo, _ = _tn_matmul(f, dy, "dw_ffn_out")
    dw_fo4 = _halves_view(dw_fo)
    dw_fi, (q_fo,) = _tn_matmul(h, dgu, "dw_ffn_in", job=_job_to_other_core([[(dw_fo4, 0)]]))
    c_fo = _presum(dw_fo4, q_fo, "presum_ffn_out")
    (dmix, dzg, dzs, dp_mrg, dyg, dys, d_gpm), (s_fo, q_fi) = _merge_bwd(
        dx1, mix, proj, zg, zs, w_bg, w_bs, w_o, norm_post_mix,
        job=_join(_job_scatter([c_fo]), _job_to_other_core([[(dw_fi, 0)]])))
    c_fi = _presum(dw_fi, q_fi, "presum_ffn_in")
    dw_o4 = _halves_view(_tn_matmul(merged, dmix, "dw_out")[0])
    dw_bg4 = _halves_view(_tn_matmul(y_gla, dzg, "dw_branch_gla")[0])
    dw_bs4 = _halves_view(_tn_matmul(y_sgu, dzs, "dw_branch_sgu")[0])
    (dp_sgu, d_wsp, d_bsp_t, d_lng, d_lnb), (s_fi, q_o, q_bg, q_bs) = _sgu_bwd(
        proj, dys, ln_g, ln_b, w_sp, b_sp_t,
        job=_join(_job_scatter([c_fi]), _job_to_other_core([[(dw_o4, 0)], [(dw_bg4, 0)], [(dw_bs4, 0)]])))
    c_o, c_bg, c_bs = (_presum(dw_o4, q_o, "presum_out"), _presum(dw_bg4, q_bg, "presum_branch_gla"),
                       _presum(dw_bs4, q_bs, "presum_branch_sgu"))
    h_fo = _sum_slots(own_part(c_fo), s_fo, "sum_ffn_out")
    (dp_gla, dal, d_gn, d_bg, d_wgu), (s_o, s_bg, s_bs, t_fo) = _gla_bwd(
        proj, alow, wgu, b_gate, gn, states, dyg,
        job=_join(_job_scatter([c_o, c_bg, c_bs]), _job_to_other_core(whole([h_fo]))))
    h_fi, h_o, h_bg, h_bs = (_sum_slots(own_part(c_fi), s_fi, "sum_ffn_in"), _sum_slots(own_part(c_o), s_o, "sum_out"),
                             _sum_slots(own_part(c_bg), s_bg, "sum_branch_gla"),
                             _sum_slots(own_part(c_bs), s_bs, "sum_branch_sgu"))
    dw_a, (t_fi, t_o, t_bg, t_bs) = _tn_matmul(a, dp_gla, "dw_in_gla",
                                               job=_job_to_other_core(whole([h_fi, h_o, h_bg, h_bs])))
    dw_b, _ = _tn_matmul(a, dp_sgu, "dw_in_sgu")
    dw_c, _ = _tn_matmul(a, dp_mrg, "dw_in_merge")
    dw_d, _ = _tn_matmul(a, dal, "dw_in_gate")

    grads, deltas, new_m, new_v = {}, {}, {}, {}

    def update(name, w, m, v, g_mine, g_theirs, job=None):
        (g, d, m2, v2), jres = _adamw(w[0], m[0], v[0], g_mine, g_theirs, "adamw_" + name, job=job)
        grads[name], deltas[name], new_m[name], new_v[name] = g[None], d[None], m2[None], v2[None]
        return jres

    dw_in = [(dw_a, 0), (dw_b, W_GLA), (dw_c, W_GLA + W_SGU), (dw_d, N_MAIN)]
    update("w_ffn_out", w_ffn_out, m_w_ffn_out, v_w_ffn_out, h_fo, t_fo)
    (q_in,) = update("w_ffn_in", w_ffn_in, m_w_ffn_in, v_w_ffn_in, h_fi, t_fi, job=_job_to_other_core([dw_in]))
    update("w_out", w_out, m_w_out, v_w_out, h_o, t_o)
    update("w_branch_gla", w_branch_gla, m_w_branch_gla, v_w_branch_gla, h_bg, t_bg)
    update("w_branch_sgu", w_branch_sgu, m_w_branch_sgu, v_w_branch_sgu, h_bs, t_bs)
    c_in = _presum_w_in(dw_in, q_in)
    (grad_x, d_g1), (s_in,) = _inproj_bwd(xt, dx1, norm_pre_mix, w_all, (dp_gla, dp_sgu, dp_mrg, dal),
                                          job=_job_scatter([c_in]))
    h_in = _sum_slots(own_part(c_in), s_in, "sum_w_in")
    (t_in,) = _run_job(_job_to_other_core(whole([h_in])), "swap_w_in")
    update("w_in", w_in, m_w_in, v_w_in, h_in, t_in)

    small_names = ["w_spatial", "w_gate_up", "norm_pre_mix", "norm_post_mix", "norm_pre_ffn", "norm_post_ffn", "b_gate",
                   "b_spatial", "gla_norm", "sgu_ln_g", "sgu_ln_b"]
    loss_out, small = _small_update(
        [d_wsp, d_wgu, d_g1, d_gpm, d_gpf, d_gpo, d_bg, d_bsp_t, d_gn, d_lng, d_lnb, loss],
        [w_spatial, w_gate_up, norm_pre_mix, norm_post_mix, norm_pre_ffn, norm_post_ffn, b_gate, b_spatial, gla_norm,
         sgu_ln_g, sgu_ln_b],
        [m_w_spatial, m_w_gate_up, m_norm_pre_mix, m_norm_post_mix, m_norm_pre_ffn, m_norm_post_ffn, m_b_gate,
         m_b_spatial, m_gla_norm, m_sgu_ln_g, m_sgu_ln_b],
        [v_w_spatial, v_w_gate_up, v_norm_pre_mix, v_norm_post_mix, v_norm_pre_ffn, v_norm_post_ffn, v_b_gate,
         v_b_spatial, v_gla_norm, v_sgu_ln_g, v_sgu_ln_b])
    for store, vals in zip((grads, deltas, new_m, new_v), small):
        store.update(zip(small_names, vals))

    order = ["norm_pre_mix", "w_in", "w_gate_up", "b_gate", "gla_norm", "sgu_ln_g", "sgu_ln_b", "w_spatial", "b_spatial",
             "w_branch_gla", "w_branch_sgu", "w_out", "norm_post_mix", "norm_pre_ffn", "w_ffn_in", "w_ffn_out",
             "norm_post_ffn"]
    out = [loss_out, grad_x[None]]
    for store in (grads, deltas, new_m, new_v):
        out.extend(store[n] for n in order)
    return tuple(out)
```

```python
import jax
import jax.numpy as jnp
from jax import lax
from jax.experimental import pallas as pl
from jax.experimental.pallas import tpu as pltpu

F32 = jnp.float32
BF16 = jnp.bfloat16

D_MODEL = 1024
GLA_HEADS = 4
GLA_DK = 128
GLA_DV = 256
GLA_QK = GLA_HEADS * GLA_DK
GLA_V = GLA_HEADS * GLA_DV
GLA_RANK = 16
GLA_TAU = 16.0
CHUNK = 64
SGU_GROUPS = 4
SGU_BLOCK = 128
SGU_DG = 256
D_FF = 2816
EPS = 1e-6
LANES = 128

OFF_Q, OFF_K, OFF_V, OFF_R, OFF_SU, OFF_SV, OFF_GG, OFF_GS, OFF_AL = 0, 512, 1024, 2048, 3072, 4096, 5120, 6144, 7168
W_GLA, W_SGU, W_MRG = 3072, 2048, 2048
N_MAIN = 7168
N_ALL = N_MAIN + LANES
_IN_SPLITS = (GLA_QK, GLA_QK, GLA_V, GLA_V, GLA_RANK, 1024, 1024, 1024, 1024)
_IN_STARTS = tuple(sum(_IN_SPLITS[:i]) for i in range(len(_IN_SPLITS) + 1))
_IN_DST = (OFF_Q, OFF_K, OFF_V, OFF_R, OFF_AL, OFF_SU, OFF_SV, OFF_GG, OFF_GS)
D_IN = _IN_STARTS[-1]

ADAM_LR = 0.001
ADAM_B1 = 0.9
ADAM_B2 = 0.999
ADAM_EPS = 1e-08
ADAM_WD = 0.01
ADAM_STEP = 10

VMEM_LIMIT_BYTES = 56 * 1024 * 1024
N_CHIPS = 4
N_PEER = N_CHIPS - 1
N_DEV = 8
MESH = pl.DeviceIdType.MESH

_NN = (((1,), (0,)), ((), ()))
_NT = (((1,), (1,)), ((), ()))
_TN = (((0,), (0,)), ((), ()))


def _dot(a, b, dims=_NN):
    return lax.dot_general(a, b, dims, preferred_element_type=F32)


def _split(x):
    hi = x.astype(BF16)
    lo = (x - hi.astype(F32)).astype(BF16)
    return hi, lo


def _dot_f32(a, b, dims=_NN):
    ah, al = _split(a)
    bh, bl = _split(b)
    return _dot(ah, bh, dims) + (_dot(al, bh, dims) + _dot(ah, bl, dims))


def _dot_exact_lhs(m, x):
    xh, xl = _split(x)
    return _dot(m, xh) + _dot(m, xl)


def _sigmoid(x):
    return 1.0 / (1.0 + jnp.exp(-x))


def _log_sigmoid(x):
    return jnp.minimum(x, 0.0) - jnp.log(1.0 + jnp.exp(-jnp.abs(x)))


_GELU_C = 0.7978845608028654
_GELU_A = 0.044715


def _gelu_and_grad(x):
    x2 = x * x
    t = jnp.tanh(_GELU_C * (x + _GELU_A * x * x2))
    g = 0.5 * x * (1.0 + t)
    dg = 0.5 * (1.0 + t) + 0.5 * x * (1.0 - t * t) * (_GELU_C * (1.0 + 3.0 * _GELU_A * x2))
    return g, dg


def _gelu(x):
    t = jnp.tanh(_GELU_C * (x + _GELU_A * x * x * x))
    return 0.5 * x * (1.0 + t)


def _rms_stats(x):
    return lax.rsqrt(jnp.mean(x * x, axis=-1, keepdims=True) + EPS)


def _rms_bwd(dout, y, r, g):
    yhat = y * r
    dn = dout * g
    dy = r * (dn - yhat * jnp.mean(dn * yhat, axis=-1, keepdims=True))
    return dy, dout * yhat


def _whole():
    return pl.BlockSpec(memory_space=pltpu.VMEM)


def _row_tile(T, want):
    t = min(T, want)
    assert T % t == 0
    return t


def _chunk_masks(tT, upper):
    row = lax.broadcasted_iota(jnp.int32, (tT, tT), 0)
    col = lax.broadcasted_iota(jnp.int32, (tT, tT), 1)
    same = lax.shift_right_logical(row, 6) == lax.shift_right_logical(col, 6)
    tri = (col > row) if upper else (col < row)
    return jnp.where(same & tri, 1.0, 0.0).astype(BF16)


class _Job:
    def __init__(self, ins, out_shapes, scratch, start, finish):
        self.ins, self.out_shapes, self.scratch, self.start, self.finish = list(ins), list(out_shapes), list(scratch), start, finish


def _join(*jobs):
    def split(refs, counts):
        out, at = [], 0
        for n in counts:
            out.append(refs[at:at + n])
            at += n
        return out

    ni, no, ns = [len(j.ins) for j in jobs], [len(j.out_shapes) for j in jobs], [len(j.scratch) for j in jobs]

    def start(ins, outs, scr):
        for j, a, b, c in zip(jobs, split(ins, ni), split(outs, no), split(scr, ns)):
            j.start(a, b, c)

    def finish(ins, outs, scr):
        for j, a, b, c in zip(jobs, split(ins, ni), split(outs, no), split(scr, ns)):
            j.finish(a, b, c)

    return _Job(sum((j.ins for j in jobs), []), sum((j.out_shapes for j in jobs), []),
                sum((j.scratch for j in jobs), []), start, finish)


def _mesh_pos():
    return lax.axis_index("x"), lax.axis_index("y"), lax.axis_index("c")


def _peer_chips(xi, yi):
    return [(1 - xi, yi), (xi, 1 - yi), (1 - xi, 1 - yi)]


def _half(ci, rows):
    return pl.ds(pl.multiple_of(ci * rows, 8), rows)


def _sds(shape, dtype):
    return jax.ShapeDtypeStruct(tuple(shape), dtype)


def _job_gather(arrs, by_cols):
    n = len(arrs)

    def dst(o, k, chip, rows):
        if by_cols[k]:
            c = arrs[k].shape[1]
            return o.at[rows, pl.ds(pl.multiple_of(chip * c, LANES), c)]
        return o.at[chip, rows]

    def copies(ins, outs, scr, want):
        ici_send, ici_recv, d2d_send, d2d_recv = scr
        xi, yi, ci = _mesh_pos()
        me = 2 * xi + yi
        res = []
        for k in range(n):
            r = arrs[k].shape[0]
            mine, other = _half(ci, r // 2), _half(1 - ci, r // 2)
            for j, (px, py) in enumerate(_peer_chips(xi, yi)):
                s = k * N_PEER + j
                pc = 2 * px + py
                ici = dict(send_sem=ici_send.at[s], recv_sem=ici_recv.at[s], device_id=(px, py, ci), device_id_type=MESH)
                d2d = dict(send_sem=d2d_send.at[s], recv_sem=d2d_recv.at[s], device_id=(xi, yi, 1 - ci),
                           device_id_type=MESH)
                made = {}
                if "send" in want:
                    made["send"] = pltpu.make_async_remote_copy(
                        src_ref=ins[k].at[mine], dst_ref=dst(outs[k], k, me, mine), **ici)
                if "arrive" in want:
                    made["arrive"] = pltpu.make_async_remote_copy(
                        src_ref=ins[k].at[mine], dst_ref=dst(outs[k], k, pc, mine), **ici)
                if "forward" in want:
                    made["forward"] = pltpu.make_async_remote_copy(
                        src_ref=dst(outs[k], k, pc, mine), dst_ref=dst(outs[k], k, pc, mine), **d2d)
                if "handed" in want:
                    made["handed"] = pltpu.make_async_remote_copy(
                        src_ref=dst(outs[k], k, pc, other), dst_ref=dst(outs[k], k, pc, other), **d2d)
                res.append(made)
        return res

    def start(ins, outs, scr):
        for cp in copies(ins, outs, scr, ("send",)):
            cp["send"].start()

    def finish(ins, outs, scr):
        for cp in copies(ins, outs, scr, ("arrive", "forward")):
            cp["arrive"].wait_recv()
            cp["forward"].start()
        for cp in copies(ins, outs, scr, ("handed", "send", "forward")):
            cp["handed"].wait_recv()
            cp["send"].wait_send()
            cp["forward"].wait_send()

    shapes = [_sds((a.shape[0], N_CHIPS * a.shape[1]) if bc else (N_CHIPS,) + a.shape, a.dtype)
              for a, bc in zip(arrs, by_cols)]
    dma = pltpu.SemaphoreType.DMA
    return _Job(arrs, shapes, [dma((n * N_PEER,))] * 4, start, finish)


def _job_scatter(parts):
    n = len(parts)

    def copies(ins, outs, scr):
        send_sems, recv_sems = scr
        xi, yi, ci = _mesh_pos()
        res = []
        for k in range(n):
            for j, (px, py) in enumerate(_peer_chips(xi, yi)):
                s = k * N_PEER + j
                res.append(pltpu.make_async_remote_copy(
                    src_ref=ins[k].at[2 * px + py], dst_ref=outs[k].at[j], send_sem=send_sems.at[s],
                    recv_sem=recv_sems.at[s], device_id=(px, py, ci), device_id_type=MESH))
        return res

    def start(ins, outs, scr):
        for cp in copies(ins, outs, scr):
            cp.start()

    def finish(ins, outs, scr):
        for cp in copies(ins, outs, scr):
            cp.wait_recv()
            cp.wait_send()

    dma = pltpu.SemaphoreType.DMA
    return _Job(parts, [_sds((N_PEER,) + p.shape[1:], p.dtype) for p in parts], [dma((n * N_PEER,))] * 2, start, finish)


def _job_to_other_core(groups):
    pieces = [(g, a, off) for g, group in enumerate(groups) for a, off in group]
    n = len(pieces)

    def geometry(group):
        a0, off0 = group[0]
        if off0 is None:
            return a0.shape
        if a0.ndim == 4:
            return (N_CHIPS, a0.shape[2], a0.shape[3])
        return (a0.shape[0] // 2, sum(a.shape[1] for a, _ in group))

    def copies(ins, outs, scr):
        send_sems, recv_sems = scr
        xi, yi, ci = _mesh_pos()
        res = []
        for p, (g, a, off) in enumerate(pieces):
            if off is None:
                give, land = ins[p], outs[g]
            elif a.ndim == 4:
                give, land = ins[p].at[pl.ds(0, N_CHIPS), 1 - ci], outs[g]
            else:
                hr, w = a.shape[0] // 2, a.shape[1]
                give, land = ins[p].at[_half(1 - ci, hr)], outs[g].at[pl.ds(0, hr), pl.ds(off, w)]
            res.append(pltpu.make_async_remote_copy(
                src_ref=give, dst_ref=land, send_sem=send_sems.at[p], recv_sem=recv_sems.at[p],
                device_id=(xi, yi, 1 - ci), device_id_type=MESH))
        return res

    def start(ins, outs, scr):
        for cp in copies(ins, outs, scr):
            cp.start()

    def finish(ins, outs, scr):
        for cp in copies(ins, outs, scr):
            cp.wait_recv()
            cp.wait_send()

    dma = pltpu.SemaphoreType.DMA
    return _Job([a for _, a, _ in pieces], [_sds(geometry(group), group[0][0].dtype) for group in groups],
                [dma((n,))] * 2, start, finish)


def _call(body, *, name, grid, in_specs, out_specs, out_shape, args, scratch_shapes=(), parallel=False, job=None):
    n_in, n_out, n_scr = len(in_specs), len(out_specs), len(scratch_shapes)

    def params(sem):
        return pltpu.CompilerParams(dimension_semantics=sem, vmem_limit_bytes=VMEM_LIMIT_BYTES)

    if job is None:
        sem = ("parallel" if parallel else "arbitrary",) * len(grid)
        res = pl.pallas_call(
            body, name=name, grid=grid, in_specs=in_specs, out_specs=out_specs, out_shape=out_shape,
            scratch_shapes=list(scratch_shapes), compiler_params=params(sem))(*args)
        return list(res), []
    n_ji, n_jo = len(job.ins), len(job.out_shapes)

    def carried(*refs):
        ins, refs = refs[:n_in], refs[n_in:]
        j_ins, refs = refs[:n_ji], refs[n_ji:]
        outs, refs = refs[:n_out], refs[n_out:]
        j_outs, refs = refs[:n_jo], refs[n_jo:]
        scr, j_scr = refs[:n_scr], refs[n_scr:]
        ids = [pl.program_id(d) for d in range(len(grid))]
        first = ids[0] == 0
        last = ids[0] == grid[0] - 1
        for d in range(1, len(grid)):
            first = first & (ids[d] == 0)
            last = last & (ids[d] == grid[d] - 1)

        @pl.when(first)
        def _():
            job.start(j_ins, j_outs, j_scr)

        body(*ins, *outs, *scr)

        @pl.when(last)
        def _():
            job.finish(j_ins, j_outs, j_scr)

    hbm = pl.BlockSpec(memory_space=pl.ANY)
    res = pl.pallas_call(
        carried, name=name, grid=grid, in_specs=list(in_specs) + [hbm] * n_ji, out_specs=list(out_specs) + [hbm] * n_jo,
        out_shape=list(out_shape) + job.out_shapes, scratch_shapes=list(scratch_shapes) + job.scratch,
        compiler_params=params(("arbitrary",) * len(grid)))(*args, *job.ins)
    return list(res[:n_out]), list(res[n_out:])


def _run_job(job, name):
    n_i, n_o = len(job.ins), len(job.out_shapes)

    def body(*refs):
        ins, outs, scr = refs[:n_i], refs[n_i:n_i + n_o], refs[n_i + n_o:]
        job.start(ins, outs, scr)
        job.finish(ins, outs, scr)

    hbm = pl.BlockSpec(memory_space=pl.ANY)
    return list(pl.pallas_call(body, name=name, in_specs=[hbm] * n_i, out_specs=[hbm] * n_o, out_shape=job.out_shapes,
                               scratch_shapes=job.scratch)(*job.ins))


def _adam_values(w, m, v, g):
    m2 = ADAM_B1 * m + (1.0 - ADAM_B1) * g
    v2 = ADAM_B2 * v + (1.0 - ADAM_B2) * (g * g)
    delta = -ADAM_LR * ((m2 / (1.0 - ADAM_B1 ** ADAM_STEP)) / (jnp.sqrt(v2 / (1.0 - ADAM_B2 ** ADAM_STEP)) + ADAM_EPS)
                        + ADAM_WD * w)
    return delta, m2, v2


_P_WSP, _P_WGU, _P_NORM, _P_BG, _P_BSP, _P_HEAD, _P_LOSS, _P_ROWS = 0, 512, 576, 608, 616, 624, 720, 728


def _small_update(dgrads, ws, ms, vs):
    n = len(ws)

    def body(*refs):
        dwsp, dwgu, dg1, dgpm, dgpf, dgpo, dbg, dbspt, dgn, dlng, dlnb, loss_in = refs[:12]
        w_refs, m_refs, v_refs = refs[12:12 + n], refs[12 + n:12 + 2 * n], refs[12 + 2 * n:12 + 3 * n]
        loss_out = refs[12 + 3 * n]
        outs = refs[13 + 3 * n:13 + 7 * n]
        pack, slots, tot, send_sems, recv_sems = refs[13 + 7 * n:]
        xi, yi, ci = _mesh_pos()
        chip = 2 * xi + yi

        pack[...] = jnp.zeros_like(pack)
        for g in range(SGU_GROUPS):
            pack[_P_WSP + g * SGU_BLOCK:_P_WSP + (g + 1) * SGU_BLOCK] = dwsp[g]
        for j in range(N_CHIPS):
            pack[_P_WGU + GLA_RANK * j:_P_WGU + GLA_RANK * (j + 1)] = dwgu[0:GLA_RANK, LANES * j:LANES * (j + 1)]
        for k, r in enumerate((dg1, dgpm, dgpf, dgpo)):
            for q in range(8):
                pack[_P_NORM + 8 * k + q:_P_NORM + 8 * k + q + 1] = r[:, LANES * q:LANES * (q + 1)]
        for q in range(4):
            pack[_P_BG + q:_P_BG + q + 1] = dbg[:, LANES * q:LANES * (q + 1)]
        pack[_P_BSP:_P_BSP + SGU_GROUPS] = jnp.transpose(dbspt[...])[0:SGU_GROUPS]
        for k, r in enumerate((dgn, dlng, dlnb)):
            for j in range(N_CHIPS):
                for hh in range(4):
                    row = _P_HEAD + 32 * k + 8 * j + hh
                    pack[row:row + 1, 0:64] = r[:, 256 * hh + 64 * j:256 * hh + 64 * (j + 1)]
        pack[_P_LOSS:_P_LOSS + 1] = loss_in[...]

        to_sibling = pltpu.make_async_remote_copy(
            src_ref=pack, dst_ref=tot, send_sem=send_sems.at[N_PEER], recv_sem=recv_sems.at[N_PEER],
            device_id=(xi, yi, 1 - ci), device_id_type=MESH)
        to_sibling.start()
        to_sibling.wait_recv()
        to_sibling.wait_send()
        pack[...] = pack[...] + tot[...]
        slots[chip] = pack[...]

        def copy(j, slot):
            px, py = _peer_chips(xi, yi)[j]
            return pltpu.make_async_remote_copy(
                src_ref=pack, dst_ref=slots.at[slot(2 * px + py)], send_sem=send_sems.at[j], recv_sem=recv_sems.at[j],
                device_id=(px, py, ci), device_id_type=MESH)

        sends = [copy(j, lambda peer_chip: chip) for j in range(N_PEER)]
        for cp in sends:
            cp.start()
        for j in range(N_PEER):
            copy(j, lambda peer_chip: peer_chip).wait_recv()
        for cp in sends:
            cp.wait_send()
        acc = slots[0]
        for d in range(1, N_CHIPS):
            acc = acc + slots[d]
        tot[...] = acc
        loss_out[...] = tot[_P_LOSS:_P_LOSS + 1, 0:1]

        def step(k, g, pick, put):
            d, m2, v2 = _adam_values(pick(w_refs[k]), pick(m_refs[k]), pick(v_refs[k]), g)
            for o, val in zip((outs[k], outs[n + k], outs[2 * n + k], outs[3 * n + k]), (g, d, m2, v2)):
                put(o, val)

        def whole(ref):
            return ref[0]

        def put_whole(ref, val):
            ref[0] = val

        for g in range(SGU_GROUPS):
            def pick_g(ref, g=g):
                return ref[0, g]

            def put_g(ref, val, g=g):
                ref[0, g] = val

            step(0, tot[_P_WSP + g * SGU_BLOCK:_P_WSP + (g + 1) * SGU_BLOCK], pick_g, put_g)
        step(1, tot[pl.ds(pl.multiple_of(_P_WGU + GLA_RANK * chip, GLA_RANK), GLA_RANK), :], whole, put_whole)
        for k, (base, chunks) in enumerate(((_P_NORM, 8), (_P_NORM + 8, 8), (_P_NORM + 16, 8), (_P_NORM + 24, 8), (_P_BG, 4))):
            for q in range(chunks):
                def pick_q(ref, q=q):
                    return ref[:, LANES * q:LANES * (q + 1)]

                def put_q(ref, val, q=q):
                    ref[:, LANES * q:LANES * (q + 1)] = val

                step(2 + k, tot[base + q:base + q + 1], pick_q, put_q)
        step(7, tot[_P_BSP:_P_BSP + SGU_GROUPS], whole, put_whole)
        for k in range(3):
            mine = tot[pl.ds(pl.multiple_of(_P_HEAD + 32 * k + 8 * chip, 8), 8), :]
            step(8 + k, mine[0:4, 0:64], whole, put_whole)

    shapes = [_sds(w.shape, F32) for w in ws]
    res = pl.pallas_call(
        body, name="small_update", in_specs=[_whole()] * (12 + 3 * n), out_specs=[_whole()] * (1 + 4 * n),
        out_shape=[_sds((1, 1), F32)] + shapes * 4,
        scratch_shapes=[pltpu.VMEM((_P_ROWS, LANES), F32), pltpu.VMEM((N_CHIPS, _P_ROWS, LANES), F32),
                        pltpu.VMEM((_P_ROWS, LANES), F32), pltpu.SemaphoreType.DMA((N_PEER + 1,)),
                        pltpu.SemaphoreType.DMA((N_PEER + 1,))],
        compiler_params=pltpu.CompilerParams(vmem_limit_bytes=VMEM_LIMIT_BYTES),
    )(*dgrads, *ws, *ms, *vs)
    return res[0].reshape(()), [list(res[1 + i * n:1 + (i + 1) * n]) for i in range(4)]


def _w_in_pieces():
    blk = D_IN // N_CHIPS
    pieces = []
    for s in range(len(_IN_SPLITS)):
        lo_s, hi_s = _IN_STARTS[s], _IN_STARTS[s + 1]
        for j in range(N_CHIPS):
            lo, hi = max(lo_s, j * blk), min(hi_s, (j + 1) * blk)
            if lo < hi:
                pieces.append((j, lo - j * blk, _IN_DST[s] + lo - lo_s, hi - lo))
    return pieces


def _relayout_w_in(gathered):
    _, rows, blk = gathered.shape
    tr = 256

    def body(g_ref, o_ref):
        o_ref[:, OFF_AL:N_ALL] = jnp.zeros((tr, LANES), BF16)
        for j, src, dst, w in _w_in_pieces():
            o_ref[:, dst:dst + w] = g_ref[j, :, src:src + w]

    res, _ = _call(body, name="relayout_w_in", grid=(rows // tr,), parallel=True,
                   in_specs=[pl.BlockSpec((N_CHIPS, tr, blk), lambda i: (0, i, 0))],
                   out_specs=[pl.BlockSpec((tr, N_ALL), lambda i: (i, 0))],
                   out_shape=[_sds((rows, N_ALL), BF16)], args=(gathered,))
    return res[0]


def _update_row_tile(rows):
    for t in range(min(rows, 256), 7, -8):
        if rows % t == 0:
            return t
    return rows


def _my_half(first_ref, second_ref):
    return jnp.where(lax.axis_index("c") == 0, first_ref[...], second_ref[...])


def _presum_w_in(dws, theirs, row0, rows, name):
    hr = theirs[0].shape[0]
    blk = D_IN // N_CHIPS
    tr = 128
    nh, t0 = hr // tr, row0 // tr
    n = len(dws)

    def body(*refs):
        dw_refs, q_refs, (o_ref, s_scr) = refs[:2 * n], refs[2 * n:3 * n], refs[3 * n:]
        for p, (a, off) in enumerate(dws):
            w = a.shape[1]
            s_scr[:, off:off + w] = (_my_half(dw_refs[2 * p], dw_refs[2 * p + 1]) + q_refs[p][...]).astype(BF16)
        for j, src, dst, w in _w_in_pieces():
            o_ref[j, :, src:src + w] = s_scr[:, dst:dst + w]

    in_specs, args = [], []
    for a, _ in dws:
        w = a.shape[1]
        in_specs += [pl.BlockSpec((tr, w), lambda i: (i + t0, 0)), pl.BlockSpec((tr, w), lambda i: (i + t0 + nh, 0))]
        args += [a, a]
    in_specs += [pl.BlockSpec((tr, q.shape[1]), lambda i: (i + t0, 0)) for q in theirs]
    res, _ = _call(body, name=name, grid=(rows // tr,), parallel=True, in_specs=in_specs,
                   out_specs=[pl.BlockSpec((N_CHIPS, tr, blk), lambda i: (0, i, 0))],
                   out_shape=[_sds((N_CHIPS, rows, blk), BF16)], scratch_shapes=[pltpu.VMEM((tr, N_ALL), BF16)],
                   args=(*args, *theirs))
    return res[0]


def _presum(dw, theirs, name):
    if dw.ndim == 4:
        _, _, hr, c = dw.shape
        tr = _update_row_tile(hr)
        first = pl.BlockSpec((1, 1, tr, c), lambda j, i: (j, 0, i, 0))
        second = pl.BlockSpec((1, 1, tr, c), lambda j, i: (j, 1, i, 0))
        other = pl.BlockSpec((1, tr, c), lambda j, i: (j, i, 0))
    else:
        hr, c = dw.shape[0] // 2, dw.shape[1] // N_CHIPS
        tr = _update_row_tile(hr)
        nh = hr // tr
        first = pl.BlockSpec((tr, c), lambda j, i: (i, j))
        second = pl.BlockSpec((tr, c), lambda j, i: (i + nh, j))
        other = pl.BlockSpec((tr, c), lambda j, i: (i, j))

    def body(a_ref, b_ref, q_ref, o_ref):
        mine = _my_half(a_ref, b_ref).reshape(tr, c)
        o_ref[...] = (mine + q_ref[...].reshape(tr, c)).astype(BF16).reshape(o_ref.shape)

    res, _ = _call(body, name=name, grid=(N_CHIPS, hr // tr), parallel=True, in_specs=[first, second, other],
                   out_specs=[pl.BlockSpec((1, tr, c), lambda j, i: (j, i, 0))],
                   out_shape=[_sds((N_CHIPS, hr, c), BF16)], args=(dw, dw, theirs))
    return res[0]


def _sum_slots(own, slots, name):
    rows, cols = own.shape
    tr = _update_row_tile(rows)

    def body(own_ref, s_ref, o_ref):
        acc = own_ref[...].astype(F32)
        for j in range(N_PEER):
            acc = acc + s_ref[j].astype(F32)
        o_ref[...] = acc

    res, _ = _call(body, name=name, grid=(rows // tr,), parallel=True,
                   in_specs=[pl.BlockSpec((tr, cols), lambda i: (i, 0)), pl.BlockSpec((N_PEER, tr, cols), lambda i: (0, i, 0))],
                   out_specs=[pl.BlockSpec((tr, cols), lambda i: (i, 0))], out_shape=[_sds((rows, cols), F32)],
                   args=(own, slots))
    return res[0]


def _adamw(w, m, v, g_mine, g_theirs, name, job=None):
    rows, cols = w.shape
    part_rows = [p.shape[0] for p in g_mine]
    assert sum(part_rows) == rows // 2 and [p.shape[0] for p in g_theirs] == part_rows
    tr = _update_row_tile(min(part_rows))
    assert all(r % tr == 0 for r in part_rows)
    nh = (rows // 2) // tr
    starts = [sum(part_rows[:k]) // tr for k in range(len(part_rows))]
    n_parts = len(part_rows)

    def body(w_ref, m_ref, v_ref, *rest):
        g_refs, (g_out, d_out, m_out, v_out) = rest[:-4], rest[-4:]
        step = pl.program_id(0)
        mine_here = (step // nh) == lax.axis_index("c")
        q = step % nh
        g = None
        for k in reversed(range(n_parts)):
            val = jnp.where(mine_here, g_refs[k][...], g_refs[n_parts + k][...])
            g = val if g is None else jnp.where(q < starts[k + 1], val, g)
        d, m2, v2 = _adam_values(w_ref[...], m_ref[...], v_ref[...], g)
        g_out[...] = g
        m_out[...] = m2
        v_out[...] = v2
        d_out[...] = d

    def g_spec(k):
        last = part_rows[k] // tr - 1
        return pl.BlockSpec((tr, cols), lambda i: (jnp.clip(i % nh - starts[k], 0, last), 0))

    spec = pl.BlockSpec((tr, cols), lambda i: (i, 0))
    return _call(body, name=name, grid=(rows // tr,), parallel=True,
                 in_specs=[spec] * 3 + [g_spec(k) for k in range(n_parts)] * 2, out_specs=[spec] * 4,
                 out_shape=[_sds((rows, cols), F32)] * 4, args=(w, m, v, *g_mine, *g_theirs), job=job)


def _inproj_fwd(x, g1, w_all, job=None):
    T = x.shape[0]
    tT = _row_tile(T, 512)

    def body(x_ref, g_ref, w_ref, a_ref, proj_ref, alow_ref):
        xv = x_ref[...]
        a = (xv * _rms_stats(xv) * g_ref[...]).astype(BF16)
        a_ref[...] = a
        for j in range(N_MAIN // 1024):
            cols = slice(j * 1024, (j + 1) * 1024)
            proj_ref[:, cols] = _dot(a, w_ref[:, cols]).astype(BF16)
        alow_ref[...] = _dot(a, w_ref[:, N_MAIN:N_ALL])

    row = lambda w: pl.BlockSpec((tT, w), lambda i: (i, 0))
    return _call(
        body, name="inproj_fwd", grid=(T // tT,), parallel=True,
        in_specs=[row(D_MODEL), pl.BlockSpec((1, D_MODEL), lambda i: (0, 0)), _whole()],
        out_specs=[row(D_MODEL), row(N_MAIN), row(LANES)],
        out_shape=[_sds((T, D_MODEL), BF16), _sds((T, N_MAIN), BF16), _sds((T, LANES), F32)],
        args=(x, g1, w_all), job=job)


def _gla_decay_terms(al_ref, wgu_ref, bg_ref, tT):
    logit = _dot_f32(al_ref[...], wgu_ref[...]) + bg_ref[...]
    la = _log_sigmoid(logit) * (1.0 / GLA_TAU)
    delta = _dot_exact_lhs(_chunk_masks(tT, upper=True), la)
    return logit, la, delta


def _gla_fwd(proj, alow, wgu, b_gate, gn, job=None):
    T = proj.shape[0]
    tT = _row_tile(T, 512)
    nc = tT // CHUNK

    def body(q_ref, k_ref, v_ref, r_ref, al_ref, wgu_ref, bg_ref, gn_ref, y_ref, st_ref, s_scr):
        @pl.when(pl.program_id(0) == 0)
        def _():
            s_scr[...] = jnp.zeros_like(s_scr)

        _, la, delta = _gla_decay_terms(al_ref, wgu_ref, bg_ref, tT)
        kdec = (k_ref[...].astype(F32) * jnp.exp(delta)).astype(BF16)
        heads = range(GLA_HEADS)
        kcs = [slice(h * GLA_DK, (h + 1) * GLA_DK) for h in heads]
        vcs = [slice(h * GLA_DV, (h + 1) * GLA_DV) for h in heads]
        state = [s_scr[h] for h in heads]
        for c in range(nc):
            rows = slice(c * CHUNK, (c + 1) * CHUNK)
            first = slice(c * CHUNK, c * CHUNK + 1)
            dec = jnp.exp(la[first, :] + delta[first, :])
            upd_t = [_dot(v_ref[rows, vcs[h]], kdec[rows, kcs[h]], _TN) for h in heads]
            qs = [(q_ref[rows, kcs[h]].astype(F32) * (GLA_DK ** -0.5)).astype(BF16) for h in heads]
            for h in heads:
                state[h] = state[h] * dec[:, kcs[h]] + upd_t[h]
                st_ref[c, h] = state[h]
            o = [_dot(qs[h], state[h].astype(BF16), _NT) for h in heads]
            for h in heads:
                on = o[h] * _rms_stats(o[h]) * gn_ref[:, vcs[h]]
                rr = r_ref[rows, vcs[h]].astype(F32)
                y_ref[rows, vcs[h]] = (on * (rr * _sigmoid(rr))).astype(BF16)
        for h in heads:
            s_scr[h] = state[h]

    blk = lambda w, j: pl.BlockSpec((tT, w), lambda i: (i, j))
    return _call(
        body, name="gla_fwd", grid=(T // tT,),
        in_specs=[blk(512, 0), blk(512, 1), blk(1024, 1), blk(1024, 2), blk(LANES, 0), _whole(), _whole(), _whole()],
        out_specs=[pl.BlockSpec((tT, GLA_V), lambda i: (i, 0)),
                   pl.BlockSpec((nc, GLA_HEADS, GLA_DV, GLA_DK), lambda i: (i, 0, 0, 0))],
        out_shape=[_sds((T, GLA_V), BF16), _sds((T // CHUNK, GLA_HEADS, GLA_DV, GLA_DK), F32)],
        scratch_shapes=[pltpu.VMEM((GLA_HEADS, GLA_DV, GLA_DK), F32)],
        args=(proj, proj, proj, proj, alow, wgu, b_gate, gn), job=job)


def _sgu_mask():
    i = lax.broadcasted_iota(jnp.int32, (SGU_BLOCK, SGU_BLOCK), 0)
    j = lax.broadcasted_iota(jnp.int32, (SGU_BLOCK, SGU_BLOCK), 1)
    return lax.shift_right_logical(j, 6) <= lax.shift_right_logical(i, 6)


def _sgu_fwd(proj, ln_g, ln_b, w_sp, b_sp_t):
    T = proj.shape[0]
    tT = _row_tile(T, 512)
    nb = tT // SGU_BLOCK

    def body(su_ref, sv_ref, lg_ref, lb_ref, w_ref, b_ref, y_ref):
        mask = _sgu_mask()
        for g in range(SGU_GROUPS):
            gc = slice(g * SGU_DG, (g + 1) * SGU_DG)
            wm = jnp.where(mask, w_ref[g], 0.0).astype(BF16)
            vf = _gelu(sv_ref[:, gc].astype(F32))
            mu = jnp.mean(vf, axis=-1, keepdims=True)
            vc = vf - mu
            rstd = lax.rsqrt(jnp.mean(vc * vc, axis=-1, keepdims=True) + EPS)
            vn = (vc * rstd * lg_ref[:, gc] + lb_ref[:, gc]).astype(BF16)
            u = _gelu(su_ref[:, gc].astype(F32))
            for b in range(nb):
                rows = slice(b * SGU_BLOCK, (b + 1) * SGU_BLOCK)
                mixed = _dot(wm, vn[rows, :]) + b_ref[:, g:g + 1]
                y_ref[rows, gc] = (u[rows, :] * mixed).astype(BF16)

    blk = lambda j: pl.BlockSpec((tT, 1024), lambda i: (i, j))
    res, _ = _call(body, name="sgu_fwd", grid=(T // tT,), parallel=True,
                   in_specs=[blk(3), blk(4), _whole(), _whole(), _whole(), _whole()],
                   out_specs=[pl.BlockSpec((tT, 1024), lambda i: (i, 0))], out_shape=[_sds((T, 1024), BF16)],
                   args=(proj, proj, ln_g, ln_b, w_sp, b_sp_t))
    return res[0]


def _merge_fwd(x, proj, y_gla, y_sgu, w_bg, w_bs, w_o, g_pm, job=None):
    T = x.shape[0]
    tT = _row_tile(T, 512)

    def body(x_ref, gg_ref, gs_ref, yg_ref, ys_ref, wbg_ref, wbs_ref, wo_ref, g_ref,
             zg_ref, zs_ref, mg_ref, mix_ref, x1_ref):
        zg = _dot(yg_ref[...], wbg_ref[...])
        zs = _dot(ys_ref[...], wbs_ref[...])
        zg_ref[...] = zg.astype(BF16)
        zs_ref[...] = zs.astype(BF16)
        merged = (_sigmoid(gg_ref[...].astype(F32)) * zg + _sigmoid(gs_ref[...].astype(F32)) * zs).astype(BF16)
        mg_ref[...] = merged
        mix = _dot(merged, wo_ref[...])
        mix_ref[...] = mix
        x1_ref[...] = x_ref[...] + mix * _rms_stats(mix) * g_ref[...]

    row = pl.BlockSpec((tT, D_MODEL), lambda i: (i, 0))
    blk = lambda j: pl.BlockSpec((tT, 1024), lambda i: (i, j))
    sds = lambda dt: _sds((T, D_MODEL), dt)
    return _call(body, name="merge_fwd", grid=(T // tT,), parallel=True,
                 in_specs=[row, blk(5), blk(6), row, row, _whole(), _whole(), _whole(),
                           pl.BlockSpec((1, D_MODEL), lambda i: (0, 0))],
                 out_specs=[row] * 5, out_shape=[sds(BF16), sds(BF16), sds(BF16), sds(F32), sds(F32)],
                 args=(x, proj, proj, y_gla, y_sgu, w_bg, w_bs, w_o, g_pm), job=job)


def _ffn_fwd_bwd(x1, tgt, w_fi_top, w_fi_bot, w_fo, g_pf, g_po):
    T = x1.shape[0]
    tT = _row_tile(T, 256)
    half = D_FF // 2
    kh = D_MODEL // 2

    def body(x1_ref, t_ref, top_ref, bot_ref, wfo_ref, gpf_ref, gpo_ref,
             h_ref, f_ref, dgu_ref, dy_ref, dx1_ref, loss_ref, dgpf_ref, dgpo_ref, gu_scr):
        @pl.when(pl.program_id(0) == 0)
        def _():
            loss_ref[...] = jnp.zeros_like(loss_ref)
            dgpf_ref[...] = jnp.zeros_like(dgpf_ref)
            dgpo_ref[...] = jnp.zeros_like(dgpo_ref)

        x1v = x1_ref[...]
        r2 = _rms_stats(x1v)
        h = (x1v * r2 * gpf_ref[...]).astype(BF16)
        h_ref[...] = h
        y = jnp.zeros((tT, D_MODEL), F32)
        for j in range(2):
            gc = slice(j * half, (j + 1) * half)
            uc = slice(D_FF + j * half, D_FF + (j + 1) * half)
            gate = _dot(h[:, :kh], top_ref[j]) + _dot(h[:, kh:], bot_ref[j])
            up = _dot(h[:, :kh], top_ref[2 + j]) + _dot(h[:, kh:], bot_ref[2 + j])
            gu_scr[:, gc] = gate
            gu_scr[:, uc] = up
            f = (gate * _sigmoid(gate) * up).astype(BF16)
            f_ref[:, gc] = f
            y = y + _dot(f, wfo_ref[gc, :])
        r3 = _rms_stats(y)
        x2 = x1v + y * r3 * gpo_ref[...]
        err = x2 - t_ref[...]
        loss_ref[...] += jnp.sum(err * err) * (0.5 / D_MODEL)
        dx2 = err * (1.0 / D_MODEL)
        dy, dg = _rms_bwd(dx2, y, r3, gpo_ref[...])
        dgpo_ref[...] += jnp.sum(dg, axis=0, keepdims=True)
        dyb = dy.astype(BF16)
        dy_ref[...] = dyb
        dh_top = jnp.zeros((tT, kh), F32)
        dh_bot = jnp.zeros((tT, kh), F32)
        for j in range(2):
            gc = slice(j * half, (j + 1) * half)
            uc = slice(D_FF + j * half, D_FF + (j + 1) * half)
            df = _dot(dyb, wfo_ref[gc, :], _NT)
            gate = gu_scr[:, gc]
            up = gu_scr[:, uc]
            sg = _sigmoid(gate)
            dgate = (df * up * (sg * (1.0 + gate * (1.0 - sg)))).astype(BF16)
            dup = (df * (gate * sg)).astype(BF16)
            dgu_ref[:, gc] = dgate
            dgu_ref[:, uc] = dup
            dh_top = dh_top + _dot(dgate, top_ref[j], _NT) + _dot(dup, top_ref[2 + j], _NT)
            dh_bot = dh_bot + _dot(dgate, bot_ref[j], _NT) + _dot(dup, bot_ref[2 + j], _NT)
        dh = jnp.concatenate([dh_top, dh_bot], axis=1)
        dx1n, dg2 = _rms_bwd(dh, x1v, r2, gpf_ref[...])
        dgpf_ref[...] += jnp.sum(dg2, axis=0, keepdims=True)
        dx1_ref[...] = dx2 + dx1n

    row = lambda w: pl.BlockSpec((tT, w), lambda i: (i, 0))
    vec = pl.BlockSpec((1, D_MODEL), lambda i: (0, 0))
    res, _ = _call(
        body, name="ffn_fwd_bwd", grid=(T // tT,),
        in_specs=[row(D_MODEL), row(D_MODEL), _whole(), _whole(), _whole(), vec, vec],
        out_specs=[row(D_MODEL), row(D_FF), row(2 * D_FF), row(D_MODEL), row(D_MODEL),
                   pl.BlockSpec((1, LANES), lambda i: (0, 0)), vec, vec],
        out_shape=[_sds((T, D_MODEL), BF16), _sds((T, D_FF), BF16), _sds((T, 2 * D_FF), BF16), _sds((T, D_MODEL), BF16),
                   _sds((T, D_MODEL), F32), _sds((1, LANES), F32), _sds((1, D_MODEL), F32), _sds((1, D_MODEL), F32)],
        scratch_shapes=[pltpu.VMEM((tT, 2 * D_FF), F32)], args=(x1, tgt, w_fi_top, w_fi_bot, w_fo, g_pf, g_po))
    return res


def _merge_bwd(dx1, mix, proj, zg, zs, w_bg, w_bs, w_o, g_pm, job=None):
    T = dx1.shape[0]
    tT = _row_tile(T, 512)

    def body(dx1_ref, mix_ref, gg_ref, gs_ref, zg_ref, zs_ref, wbg_ref, wbs_ref, wo_ref, g_ref,
             dmix_ref, dzg_ref, dzs_ref, dgate_ref, dyg_ref, dys_ref, dgpm_ref):
        @pl.when(pl.program_id(0) == 0)
        def _():
            dgpm_ref[...] = jnp.zeros_like(dgpm_ref)

        mix = mix_ref[...]
        dmix, dg = _rms_bwd(dx1_ref[...], mix, _rms_stats(mix), g_ref[...])
        dgpm_ref[...] += jnp.sum(dg, axis=0, keepdims=True)
        dmb = dmix.astype(BF16)
        dmix_ref[...] = dmb
        dmerged = _dot(dmb, wo_ref[...], _NT)
        for k, (gate_ref, z_ref, w_ref, dz_ref, dy_ref) in enumerate((
                (gg_ref, zg_ref, wbg_ref, dzg_ref, dyg_ref), (gs_ref, zs_ref, wbs_ref, dzs_ref, dys_ref))):
            sg = _sigmoid(gate_ref[...].astype(F32))
            dz = (dmerged * sg).astype(BF16)
            dz_ref[...] = dz
            dgate_ref[:, k * 1024:(k + 1) * 1024] = (dmerged * z_ref[...].astype(F32) * (sg * (1.0 - sg))).astype(BF16)
            dy_ref[...] = _dot(dz, w_ref[...], _NT).astype(BF16)

    row = pl.BlockSpec((tT, D_MODEL), lambda i: (i, 0))
    blk = lambda j: pl.BlockSpec((tT, 1024), lambda i: (i, j))
    vec = pl.BlockSpec((1, D_MODEL), lambda i: (0, 0))
    sds = _sds((T, D_MODEL), BF16)
    return _call(
        body, name="merge_bwd", grid=(T // tT,),
        in_specs=[row, row, blk(5), blk(6), row, row, _whole(), _whole(), _whole(), vec],
        out_specs=[row, row, row, pl.BlockSpec((tT, W_MRG), lambda i: (i, 0)), row, row, vec],
        out_shape=[sds, sds, sds, _sds((T, W_MRG), BF16), sds, sds, _sds((1, D_MODEL), F32)],
        args=(dx1, mix, proj, proj, zg, zs, w_bg, w_bs, w_o, g_pm), job=job)


def _sgu_bwd(proj, dy_sgu, ln_g, ln_b, w_sp, b_sp_t, job=None):
    T = proj.shape[0]
    tT = _row_tile(T, 512)
    nb = tT // SGU_BLOCK

    def body(su_ref, sv_ref, dy_ref, lg_ref, lb_ref, w_ref, b_ref, dp_ref, dw_ref, dbt_ref, dlg_ref, dlb_ref):
        @pl.when(pl.program_id(0) == 0)
        def _():
            dw_ref[...] = jnp.zeros_like(dw_ref)
            dbt_ref[...] = jnp.zeros_like(dbt_ref)
            dlg_ref[...] = jnp.zeros_like(dlg_ref)
            dlb_ref[...] = jnp.zeros_like(dlb_ref)

        mask = _sgu_mask()
        lane = lax.broadcasted_iota(jnp.int32, (SGU_BLOCK, LANES), 1)
        for g in range(SGU_GROUPS):
            gc = slice(g * SGU_DG, (g + 1) * SGU_DG)
            gc_v = slice(1024 + g * SGU_DG, 1024 + (g + 1) * SGU_DG)
            wm = jnp.where(mask, w_ref[g], 0.0).astype(BF16)
            vf, dvf_dsv = _gelu_and_grad(sv_ref[:, gc].astype(F32))
            mu = jnp.mean(vf, axis=-1, keepdims=True)
            vc = vf - mu
            rstd = lax.rsqrt(jnp.mean(vc * vc, axis=-1, keepdims=True) + EPS)
            vhat = vc * rstd
            vn = (vhat * lg_ref[:, gc] + lb_ref[:, gc]).astype(BF16)
            u, du_dsu = _gelu_and_grad(su_ref[:, gc].astype(F32))
            dy = dy_ref[:, gc].astype(F32)
            dmixed = (dy * u).astype(BF16)
            dvn_parts = []
            dw_acc = jnp.zeros((SGU_BLOCK, SGU_BLOCK), F32)
            db_acc = jnp.zeros((SGU_BLOCK, 1), F32)
            for b in range(nb):
                rows = slice(b * SGU_BLOCK, (b + 1) * SGU_BLOCK)
                mixed = _dot(wm, vn[rows, :]) + b_ref[:, g:g + 1]
                dp_ref[rows, gc] = (dy[rows, :] * mixed * du_dsu[rows, :]).astype(BF16)
                dvn_parts.append(_dot(wm, dmixed[rows, :], _TN))
                dw_acc = dw_acc + _dot(dmixed[rows, :], vn[rows, :], _NT)
                db_acc = db_acc + jnp.sum(dmixed[rows, :].astype(F32), axis=-1, keepdims=True)
            dw_ref[g] += jnp.where(mask, dw_acc, 0.0)
            dbt_ref[...] += jnp.where(lane == g, db_acc, 0.0)
            dvn = jnp.concatenate(dvn_parts, axis=0)
            dlg_ref[:, gc] += jnp.sum(dvn * vhat, axis=0, keepdims=True)
            dlb_ref[:, gc] += jnp.sum(dvn, axis=0, keepdims=True)
            dvh = dvn * lg_ref[:, gc]
            dvf = rstd * (dvh - jnp.mean(dvh, axis=-1, keepdims=True)
                          - vhat * jnp.mean(dvh * vhat, axis=-1, keepdims=True))
            dp_ref[:, gc_v] = (dvf * dvf_dsv).astype(BF16)

    blk = lambda j: pl.BlockSpec((tT, 1024), lambda i: (i, j))
    row = lambda w: pl.BlockSpec((tT, w), lambda i: (i, 0))
    vec = pl.BlockSpec((1, 1024), lambda i: (0, 0))
    return _call(
        body, name="sgu_bwd", grid=(T // tT,),
        in_specs=[blk(3), blk(4), row(1024), _whole(), _whole(), _whole(), _whole()],
        out_specs=[row(W_SGU), pl.BlockSpec((SGU_GROUPS, SGU_BLOCK, SGU_BLOCK), lambda i: (0, 0, 0)),
                   pl.BlockSpec((SGU_BLOCK, LANES), lambda i: (0, 0)), vec, vec],
        out_shape=[_sds((T, W_SGU), BF16), _sds((SGU_GROUPS, SGU_BLOCK, SGU_BLOCK), F32), _sds((SGU_BLOCK, LANES), F32),
                   _sds((1, 1024), F32), _sds((1, 1024), F32)],
        args=(proj, proj, dy_sgu, ln_g, ln_b, w_sp, b_sp_t), job=job)


def _gla_bwd(proj, alow, wgu, b_gate, gn, states, dy_gla, job=None):
    T = proj.shape[0]
    tT = _row_tile(T, 512)
    nc = tT // CHUNK
    nt = T // tT

    def body(q_ref, k_ref, v_ref, r_ref, al_ref, wgu_ref, bg_ref, gn_ref, st_ref, sp_ref, dy_ref,
             dp_ref, dal_ref, dgn_ref, dbg_ref, dwgu_ref, g_scr, dd_scr, dt_scr):
        step = pl.program_id(0)

        @pl.when(step == 0)
        def _():
            g_scr[...] = jnp.zeros_like(g_scr)
            dgn_ref[...] = jnp.zeros_like(dgn_ref)
            dbg_ref[...] = jnp.zeros_like(dbg_ref)
            dwgu_ref[...] = jnp.zeros_like(dwgu_ref)

        has_prev = jnp.where(step == nt - 1, 0.0, 1.0)
        logit, la, delta = _gla_decay_terms(al_ref, wgu_ref, bg_ref, tT)
        e = jnp.exp(delta)
        kdec_f = k_ref[...].astype(F32) * e
        kdec = kdec_f.astype(BF16)
        heads = range(GLA_HEADS)
        kcs = [slice(h * GLA_DK, (h + 1) * GLA_DK) for h in heads]
        vcs = [slice(h * GLA_DV, (h + 1) * GLA_DV) for h in heads]
        carry = [g_scr[h] for h in heads]
        dgn_acc = [jnp.zeros((1, GLA_DV), F32) for _ in heads]
        for c in reversed(range(nc)):
            rows = slice(c * CHUNK, (c + 1) * CHUNK)
            first = slice(c * CHUNK, c * CHUNK + 1)
            dec = jnp.exp(la[first, :] + delta[first, :])
            s_b = [st_ref[c, h].astype(BF16) for h in heads]
            qs = [(q_ref[rows, kcs[h]].astype(F32) * (GLA_DK ** -0.5)).astype(BF16) for h in heads]
            o = [_dot(qs[h], s_b[h], _NT) for h in heads]
            do = []
            for h in heads:
                rstd = _rms_stats(o[h])
                ohat = o[h] * rstd
                gnh = gn_ref[:, vcs[h]]
                dy = dy_ref[rows, vcs[h]].astype(F32)
                rr = r_ref[rows, vcs[h]].astype(F32)
                sg = _sigmoid(rr)
                don = dy * (rr * sg)
                dp_ref[rows, OFF_R + h * GLA_DV:OFF_R + (h + 1) * GLA_DV] = (
                    dy * (ohat * gnh) * (sg * (1.0 + rr * (1.0 - sg)))).astype(BF16)
                dgn_acc[h] = dgn_acc[h] + jnp.sum(don * ohat, axis=0, keepdims=True)
                dn = don * gnh
                do.append((rstd * (dn - ohat * jnp.mean(dn * ohat, axis=-1, keepdims=True))).astype(BF16))
            dq = [_dot(do[h], s_b[h]) for h in heads]
            g_t = [_dot(do[h], qs[h], _TN) + carry[h] for h in heads]
            g_b = [g_t[h].astype(BF16) for h in heads]
            dv = [_dot(kdec[rows, kcs[h]], g_b[h], _NT) for h in heads]
            dkdec = [_dot(v_ref[rows, vcs[h]], g_b[h]) for h in heads]
            for h in heads:
                s_prev = st_ref[c - 1, h] if c > 0 else sp_ref[0, h] * has_prev
                ddec = jnp.sum(g_t[h] * s_prev, axis=0, keepdims=True)
                carry[h] = g_t[h] * dec[:, kcs[h]]
                dp_ref[rows, OFF_Q + h * GLA_DK:OFF_Q + (h + 1) * GLA_DK] = (dq[h] * (GLA_DK ** -0.5)).astype(BF16)
                dp_ref[rows, OFF_V + h * GLA_DV:OFF_V + (h + 1) * GLA_DV] = dv[h].astype(BF16)
                dp_ref[rows, OFF_K + h * GLA_DK:OFF_K + (h + 1) * GLA_DK] = (dkdec[h] * e[rows, kcs[h]]).astype(BF16)
                dd_scr[rows, kcs[h]] = dkdec[h] * kdec_f[rows, kcs[h]]
                dt_scr[rows, kcs[h]] = jnp.broadcast_to(ddec * dec[:, kcs[h]], (CHUNK, GLA_DK))
        for h in heads:
            g_scr[h] = carry[h]
            dgn_ref[:, vcs[h]] += dgn_acc[h]
        dla = _dot_exact_lhs(_chunk_masks(tT, upper=False), dd_scr[...]) + dt_scr[...]
        dlogit = dla * (1.0 / GLA_TAU) * _sigmoid(-logit)
        dbg_ref[...] += jnp.sum(dlogit, axis=0, keepdims=True)
        dwgu_ref[...] += _dot_f32(al_ref[...], dlogit, _TN)
        dal_ref[...] = _dot_f32(dlogit, wgu_ref[...], _NT).astype(BF16)

    rev = lambda i: nt - 1 - i
    blk = lambda w, j: pl.BlockSpec((tT, w), lambda i: (rev(i), j))
    st_blk = pl.BlockSpec((nc, GLA_HEADS, GLA_DV, GLA_DK), lambda i: (rev(i), 0, 0, 0))
    sp_blk = pl.BlockSpec((1, GLA_HEADS, GLA_DV, GLA_DK), lambda i: (jnp.maximum(rev(i) * nc - 1, 0), 0, 0, 0))
    return _call(
        body, name="gla_bwd", grid=(nt,),
        in_specs=[blk(512, 0), blk(512, 1), blk(1024, 1), blk(1024, 2), blk(LANES, 0), _whole(), _whole(), _whole(),
                  st_blk, sp_blk, blk(GLA_V, 0)],
        out_specs=[blk(W_GLA, 0), blk(LANES, 0), pl.BlockSpec((1, GLA_V), lambda i: (0, 0)),
                   pl.BlockSpec((1, GLA_QK), lambda i: (0, 0)), pl.BlockSpec((LANES, GLA_QK), lambda i: (0, 0))],
        out_shape=[_sds((T, W_GLA), BF16), _sds((T, LANES), BF16), _sds((1, GLA_V), F32), _sds((1, GLA_QK), F32),
                   _sds((LANES, GLA_QK), F32)],
        scratch_shapes=[pltpu.VMEM((GLA_HEADS, GLA_DV, GLA_DK), F32), pltpu.VMEM((tT, GLA_QK), F32),
                        pltpu.VMEM((tT, GLA_QK), F32)],
        args=(proj, proj, proj, proj, alow, wgu, b_gate, gn, states, states, dy_gla), job=job)


def _inproj_bwd(x, dx1, g1, w_all, dparts, job=None):
    T = x.shape[0]
    tT = _row_tile(T, 512)
    offs = (0, W_GLA, W_GLA + W_SGU, N_MAIN)

    def body(x_ref, dx1_ref, g_ref, w_ref, *rest):
        part_refs, (dx_ref, dg_ref) = rest[:len(offs)], rest[len(offs):]

        @pl.when(pl.program_id(0) == 0)
        def _():
            dg_ref[...] = jnp.zeros_like(dg_ref)

        da = jnp.zeros((tT, D_MODEL), F32)
        for off, p_ref in zip(offs, part_refs):
            da = da + _dot(p_ref[...], w_ref[:, off:off + p_ref.shape[1]], _NT)
        xv = x_ref[...]
        dx, dg = _rms_bwd(da, xv, _rms_stats(xv), g_ref[...])
        dg_ref[...] += jnp.sum(dg, axis=0, keepdims=True)
        dx_ref[...] = dx1_ref[...] + dx

    row = lambda w: pl.BlockSpec((tT, w), lambda i: (i, 0))
    vec = pl.BlockSpec((1, D_MODEL), lambda i: (0, 0))
    return _call(
        body, name="inproj_bwd", grid=(T // tT,),
        in_specs=[row(D_MODEL), row(D_MODEL), vec, _whole()] + [row(p.shape[1]) for p in dparts],
        out_specs=[row(D_MODEL), vec], out_shape=[_sds((T, D_MODEL), F32), _sds((1, D_MODEL), F32)],
        args=(x, dx1, g1, w_all, *dparts), job=job)


def _tn_matmul(a, b, name, job=None):
    T, M = a.shape
    N = b.shape[1]
    tk = _row_tile(T, 1024)
    tm = M if M <= 1024 else 1408
    tn = N if N <= 3072 else N // 2
    assert M % tm == 0 and N % tn == 0

    def body(a_ref, b_ref, o_ref):
        @pl.when(pl.program_id(2) == 0)
        def _():
            o_ref[...] = _dot(a_ref[...], b_ref[...], _TN)

        @pl.when(pl.program_id(2) > 0)
        def _():
            o_ref[...] += _dot(a_ref[...], b_ref[...], _TN)

    res, jres = _call(
        body, name=name, grid=(M // tm, N // tn, T // tk),
        in_specs=[pl.BlockSpec((tk, tm), lambda i, j, k: (k, i)), pl.BlockSpec((tk, tn), lambda i, j, k: (k, j))],
        out_specs=[pl.BlockSpec((tm, tn), lambda i, j, k: (i, j))], out_shape=[_sds((M, N), F32)], args=(a, b), job=job)
    return res[0], jres


def _pad_rows(a, rows=8):
    return jnp.pad(a, ((0, rows - a.shape[0]), (0, LANES - a.shape[1])))


def _halves_view(dw):
    r = dw.shape[0] // N_CHIPS
    return dw.reshape(N_CHIPS, 2, r // 2, dw.shape[1])


def kernel(x, norm_pre_mix, w_in, w_gate_up, b_gate, gla_norm, sgu_ln_g, sgu_ln_b, w_spatial, b_spatial, w_branch_gla, w_branch_sgu, w_out, norm_post_mix, norm_pre_ffn, w_ffn_in, w_ffn_out, norm_post_ffn, loss_target, m_norm_pre_mix, m_w_in, m_w_gate_up, m_b_gate, m_gla_norm, m_sgu_ln_g, m_sgu_ln_b, m_w_spatial, m_b_spatial, m_w_branch_gla, m_w_branch_sgu, m_w_out, m_norm_post_mix, m_norm_pre_ffn, m_w_ffn_in, m_w_ffn_out, m_norm_post_ffn, v_norm_pre_mix, v_w_in, v_w_gate_up, v_b_gate, v_gla_norm, v_sgu_ln_g, v_sgu_ln_b, v_w_spatial, v_b_spatial, v_w_branch_gla, v_w_branch_sgu, v_w_out, v_norm_post_mix, v_norm_pre_ffn, v_w_ffn_in, v_w_ffn_out, v_norm_post_ffn):
    chip = 2 * lax.axis_index("x") + lax.axis_index("y")
    xt, tgt = x[0], loss_target[0]

    tiny = jnp.concatenate([w_gate_up[0], _pad_rows(gla_norm[0]), _pad_rows(sgu_ln_g[0]), _pad_rows(sgu_ln_b[0]),
                            jnp.zeros((8, LANES), F32)], axis=0)
    def with_own(gathered, own):
        if gathered.ndim == 3:
            return lax.dynamic_update_slice(gathered, own[None], (chip, 0, 0))
        return lax.dynamic_update_slice(gathered, own, (0, chip * own.shape[1]))

    w_in_b = w_in[0].astype(BF16)
    g_in, g_tiny = _run_job(_job_gather([w_in_b, tiny], [False, False]), "gather_w_in")
    g_tiny = with_own(g_tiny, tiny)
    w_all = _relayout_w_in(with_own(g_in, w_in_b))
    cols = lambda a: a.transpose(1, 0, 2).reshape(a.shape[1], N_CHIPS * a.shape[2])
    wgu = jnp.pad(cols(g_tiny[:, 0:16]), ((0, LANES - GLA_RANK), (0, 0)))
    gn = cols(g_tiny[:, 16:20, :64]).reshape(1, GLA_V)
    ln_g = cols(g_tiny[:, 24:28, :64]).reshape(1, 1024)
    ln_b = cols(g_tiny[:, 32:36, :64]).reshape(1, 1024)
    b_sp_t = jnp.pad(b_spatial[0].T, ((0, 0), (0, LANES - SGU_GROUPS)))
    w_sp = w_spatial[0]

    own_rows = [w_branch_gla[0].astype(BF16), w_branch_sgu[0].astype(BF16), w_out[0].astype(BF16), w_ffn_out[0].astype(BF16)]
    (a, proj, alow), g_rows = _inproj_fwd(xt, norm_pre_mix, w_all, job=_job_gather(own_rows, [False] * 4))
    rows = lambda g: g.reshape(N_CHIPS * g.shape[1], g.shape[2])
    w_bg, w_bs, w_o, w_fo = [rows(with_own(g, own)) for g, own in zip(g_rows, own_rows)]
    w_fi_b = w_ffn_in[0].astype(BF16)
    fi_top, fi_bot = w_fi_b[:D_MODEL // 2], w_fi_b[D_MODEL // 2:]
    (y_gla, states), (g_top,) = _gla_fwd(proj, alow, wgu, b_gate, gn, job=_job_gather([fi_top], [False]))
    y_sgu = _sgu_fwd(proj, ln_g, ln_b, w_sp, b_sp_t)
    (zg, zs, merged, mix, x1), (g_bot,) = _merge_fwd(xt, proj, y_gla, y_sgu, w_bg, w_bs, w_o, norm_post_mix,
                                                     job=_job_gather([fi_bot], [False]))
    h, f, dgu, dy, dx1, loss, d_gpf, d_gpo = _ffn_fwd_bwd(x1, tgt, with_own(g_top, fi_top), with_own(g_bot, fi_bot),
                                                          w_fo, norm_pre_ffn, norm_post_ffn)

    own_part = lambda c: lax.dynamic_index_in_dim(c, chip, 0, keepdims=False)
    whole = lambda hs: [[(h_, None)] for h_ in hs]
    dw_fo, _ = _tn_matmul(f, dy, "dw_ffn_out")
    dw_fo4 = _halves_view(dw_fo)
    dw_fi, (q_fo,) = _tn_matmul(h, dgu, "dw_ffn_in", job=_job_to_other_core([[(dw_fo4, 0)]]))
    c_fo = _presum(dw_fo4, q_fo, "presum_ffn_out")
    (dmix, dzg, dzs, dp_mrg, dyg, dys, d_gpm), (s_fo, q_fi) = _merge_bwd(
        dx1, mix, proj, zg, zs, w_bg, w_bs, w_o, norm_post_mix,
        job=_join(_job_scatter([c_fo]), _job_to_other_core([[(dw_fi, 0)]])))
    c_fi = _presum(dw_fi, q_fi, "presum_ffn_in")
    dw_o4 = _halves_view(_tn_matmul(merged, dmix, "dw_out")[0])
    dw_bg4 = _halves_view(_tn_matmul(y_gla, dzg, "dw_branch_gla")[0])
    dw_bs4 = _halves_view(_tn_matmul(y_sgu, dzs, "dw_branch_sgu")[0])
    (dp_sgu, d_wsp, d_bsp_t, d_lng, d_lnb), (s_fi, q_o, q_bg, q_bs) = _sgu_bwd(
        proj, dys, ln_g, ln_b, w_sp, b_sp_t,
        job=_join(_job_scatter([c_fi]), _job_to_other_core([[(dw_o4, 0)], [(dw_bg4, 0)], [(dw_bs4, 0)]])))
    c_o, c_bg, c_bs = (_presum(dw_o4, q_o, "presum_out"), _presum(dw_bg4, q_bg, "presum_branch_gla"),
                       _presum(dw_bs4, q_bs, "presum_branch_sgu"))
    h_fo = _sum_slots(own_part(c_fo), s_fo, "sum_ffn_out")
    (dp_gla, dal, d_gn, d_bg, d_wgu), (s_o, s_bg, s_bs, t_fo) = _gla_bwd(
        proj, alow, wgu, b_gate, gn, states, dyg,
        job=_join(_job_scatter([c_o, c_bg, c_bs]), _job_to_other_core(whole([h_fo]))))
    h_fi, h_o, h_bg, h_bs = (_sum_slots(own_part(c_fi), s_fi, "sum_ffn_in"), _sum_slots(own_part(c_o), s_o, "sum_out"),
                             _sum_slots(own_part(c_bg), s_bg, "sum_branch_gla"),
                             _sum_slots(own_part(c_bs), s_bs, "sum_branch_sgu"))
    dw_a, (t_fi, t_o, t_bg, t_bs) = _tn_matmul(a, dp_gla, "dw_in_gla",
                                               job=_job_to_other_core(whole([h_fi, h_o, h_bg, h_bs])))
    dw_b, (q_a,) = _tn_matmul(a, dp_sgu, "dw_in_sgu", job=_job_to_other_core([[(dw_a, 0)]]))
    dw_c, (q_b,) = _tn_matmul(a, dp_mrg, "dw_in_merge", job=_job_to_other_core([[(dw_b, 0)]]))
    dw_d, _ = _tn_matmul(a, dal, "dw_in_gate")

    grads, deltas, new_m, new_v = {}, {}, {}, {}

    def update(name, w, m, v, g_mine, g_theirs, job=None):
        (g, d, m2, v2), jres = _adamw(w[0], m[0], v[0], g_mine, g_theirs, "adamw_" + name, job=job)
        grads[name], deltas[name], new_m[name], new_v[name] = g[None], d[None], m2[None], v2[None]
        return jres

    dw_in = [(dw_a, 0), (dw_b, W_GLA), (dw_c, W_GLA + W_SGU), (dw_d, N_MAIN)]
    q_c, q_d = update("w_ffn_out", w_ffn_out, m_w_ffn_out, v_w_ffn_out, [h_fo], [t_fo],
                      job=_job_to_other_core([[(dw_c, 0)], [(dw_d, 0)]]))
    q_in = [q_a, q_b, q_c, q_d]
    hr_in = D_MODEL // 2
    c_in_a = _presum_w_in(dw_in, q_in, 0, hr_in // 4, "presum_w_in_a")
    c_in_b = _presum_w_in(dw_in, q_in, hr_in // 4, 3 * hr_in // 4, "presum_w_in_b")
    (s_in_a,) = update("w_ffn_in", w_ffn_in, m_w_ffn_in, v_w_ffn_in, [h_fi], [t_fi], job=_job_scatter([c_in_a]))
    update("w_out", w_out, m_w_out, v_w_out, [h_o], [t_o])
    update("w_branch_gla", w_branch_gla, m_w_branch_gla, v_w_branch_gla, [h_bg], [t_bg])
    update("w_branch_sgu", w_branch_sgu, m_w_branch_sgu, v_w_branch_sgu, [h_bs], [t_bs])
    (grad_x, d_g1), (s_in_b,) = _inproj_bwd(xt, dx1, norm_pre_mix, w_all, (dp_gla, dp_sgu, dp_mrg, dal),
                                            job=_job_scatter([c_in_b]))
    h_in = [_sum_slots(own_part(c_in_a), s_in_a, "sum_w_in_a"), _sum_slots(own_part(c_in_b), s_in_b, "sum_w_in_b")]
    t_in = _run_job(_job_to_other_core(whole(h_in)), "swap_w_in")
    update("w_in", w_in, m_w_in, v_w_in, h_in, t_in)

    small_names = ["w_spatial", "w_gate_up", "norm_pre_mix", "norm_post_mix", "norm_pre_ffn", "norm_post_ffn", "b_gate",
                   "b_spatial", "gla_norm", "sgu_ln_g", "sgu_ln_b"]
    loss_out, small = _small_update(
        [d_wsp, d_wgu, d_g1, d_gpm, d_gpf, d_gpo, d_bg, d_bsp_t, d_gn, d_lng, d_lnb, loss],
        [w_spatial, w_gate_up, norm_pre_mix, norm_post_mix, norm_pre_ffn, norm_post_ffn, b_gate, b_spatial, gla_norm,
         sgu_ln_g, sgu_ln_b],
        [m_w_spatial, m_w_gate_up, m_norm_pre_mix, m_norm_post_mix, m_norm_pre_ffn, m_norm_post_ffn, m_b_gate,
         m_b_spatial, m_gla_norm, m_sgu_ln_g, m_sgu_ln_b],
        [v_w_spatial, v_w_gate_up, v_norm_pre_mix, v_norm_post_mix, v_norm_pre_ffn, v_norm_post_ffn, v_b_gate,
         v_b_spatial, v_gla_norm, v_sgu_ln_g, v_sgu_ln_b])
    for store, vals in zip((grads, deltas, new_m, new_v), small):
        store.update(zip(small_names, vals))

    order = ["norm_pre_mix", "w_in", "w_gate_up", "b_gate", "gla_norm", "sgu_ln_g", "sgu_ln_b", "w_spatial", "b_spatial",
             "w_branch_gla", "w_branch_sgu", "w_out", "norm_post_mix", "norm_pre_ffn", "w_ffn_in", "w_ffn_out",
             "norm_post_ffn"]
    out = [loss_out, grad_x[None]]
    for store in (grads, deltas, new_m, new_v):
        out.extend(store[n] for n in order)
    return tuple(out)
```

```python
import jax
import jax.numpy as jnp
from jax import lax
from jax.experimental import pallas as pl
from jax.experimental.pallas import tpu as pltpu

F32 = jnp.float32
BF16 = jnp.bfloat16

D_MODEL = 1024
GLA_HEADS = 4
GLA_DK = 128
GLA_DV = 256
GLA_QK = GLA_HEADS * GLA_DK
GLA_V = GLA_HEADS * GLA_DV
GLA_RANK = 16
GLA_TAU = 16.0
CHUNK = 64
SGU_GROUPS = 4
SGU_BLOCK = 128
SGU_DG = 256
D_FF = 2816
EPS = 1e-6
LANES = 128

OFF_Q, OFF_K, OFF_V, OFF_R, OFF_SU, OFF_SV, OFF_GG, OFF_GS, OFF_AL = 0, 512, 1024, 2048, 3072, 4096, 5120, 6144, 7168
W_GLA, W_SGU, W_MRG = 3072, 2048, 2048
N_MAIN = 7168
N_ALL = N_MAIN + LANES
_IN_SPLITS = (GLA_QK, GLA_QK, GLA_V, GLA_V, GLA_RANK, 1024, 1024, 1024, 1024)
_IN_STARTS = tuple(sum(_IN_SPLITS[:i]) for i in range(len(_IN_SPLITS) + 1))
_IN_DST = (OFF_Q, OFF_K, OFF_V, OFF_R, OFF_AL, OFF_SU, OFF_SV, OFF_GG, OFF_GS)
D_IN = _IN_STARTS[-1]

ADAM_LR = 0.001
ADAM_B1 = 0.9
ADAM_B2 = 0.999
ADAM_EPS = 1e-08
ADAM_WD = 0.01
ADAM_STEP = 10

VMEM_LIMIT_BYTES = 56 * 1024 * 1024
N_CHIPS = 4
N_PEER = N_CHIPS - 1
N_DEV = 8
MESH = pl.DeviceIdType.MESH

_NN = (((1,), (0,)), ((), ()))
_NT = (((1,), (1,)), ((), ()))
_TN = (((0,), (0,)), ((), ()))


def _dot(a, b, dims=_NN):
    return lax.dot_general(a, b, dims, preferred_element_type=F32)


def _split(x):
    hi = x.astype(BF16)
    lo = (x - hi.astype(F32)).astype(BF16)
    return hi, lo


def _dot_f32(a, b, dims=_NN):
    ah, al = _split(a)
    bh, bl = _split(b)
    return _dot(ah, bh, dims) + (_dot(al, bh, dims) + _dot(ah, bl, dims))


def _dot_exact_lhs(m, x):
    xh, xl = _split(x)
    return _dot(m, xh) + _dot(m, xl)


def _sigmoid(x):
    return 1.0 / (1.0 + jnp.exp(-x))


def _log_sigmoid(x):
    return jnp.minimum(x, 0.0) - jnp.log(1.0 + jnp.exp(-jnp.abs(x)))


_GELU_C = 0.7978845608028654
_GELU_A = 0.044715


def _gelu_and_grad(x):
    x2 = x * x
    t = jnp.tanh(_GELU_C * (x + _GELU_A * x * x2))
    g = 0.5 * x * (1.0 + t)
    dg = 0.5 * (1.0 + t) + 0.5 * x * (1.0 - t * t) * (_GELU_C * (1.0 + 3.0 * _GELU_A * x2))
    return g, dg


def _gelu(x):
    t = jnp.tanh(_GELU_C * (x + _GELU_A * x * x * x))
    return 0.5 * x * (1.0 + t)


def _rms_stats(x):
    return lax.rsqrt(jnp.mean(x * x, axis=-1, keepdims=True) + EPS)


def _rms_bwd(dout, y, r, g):
    yhat = y * r
    dn = dout * g
    dy = r * (dn - yhat * jnp.mean(dn * yhat, axis=-1, keepdims=True))
    return dy, dout * yhat


def _whole():
    return pl.BlockSpec(memory_space=pltpu.VMEM)


def _row_tile(T, want):
    t = min(T, want)
    assert T % t == 0
    return t


def _chunk_masks(tT, upper):
    row = lax.broadcasted_iota(jnp.int32, (tT, tT), 0)
    col = lax.broadcasted_iota(jnp.int32, (tT, tT), 1)
    same = lax.shift_right_logical(row, 6) == lax.shift_right_logical(col, 6)
    tri = (col > row) if upper else (col < row)
    return jnp.where(same & tri, 1.0, 0.0).astype(BF16)


class _Job:
    def __init__(self, ins, out_shapes, scratch, start, finish):
        self.ins, self.out_shapes, self.scratch, self.start, self.finish = list(ins), list(out_shapes), list(scratch), start, finish


def _join(*jobs):
    def split(refs, counts):
        out, at = [], 0
        for n in counts:
            out.append(refs[at:at + n])
            at += n
        return out

    ni, no, ns = [len(j.ins) for j in jobs], [len(j.out_shapes) for j in jobs], [len(j.scratch) for j in jobs]

    def start(ins, outs, scr):
        for j, a, b, c in zip(jobs, split(ins, ni), split(outs, no), split(scr, ns)):
            j.start(a, b, c)

    def finish(ins, outs, scr):
        for j, a, b, c in zip(jobs, split(ins, ni), split(outs, no), split(scr, ns)):
            j.finish(a, b, c)

    return _Job(sum((j.ins for j in jobs), []), sum((j.out_shapes for j in jobs), []),
                sum((j.scratch for j in jobs), []), start, finish)


def _mesh_pos():
    return lax.axis_index("x"), lax.axis_index("y"), lax.axis_index("c")


def _peer_chips(xi, yi):
    return [(1 - xi, yi), (xi, 1 - yi), (1 - xi, 1 - yi)]


def _half(ci, rows):
    return pl.ds(pl.multiple_of(ci * rows, 8), rows)


def _sds(shape, dtype):
    return jax.ShapeDtypeStruct(tuple(shape), dtype)


def _job_gather(arrs, by_cols):
    n = len(arrs)

    def dst(o, k, chip, rows):
        if by_cols[k]:
            c = arrs[k].shape[1]
            return o.at[rows, pl.ds(pl.multiple_of(chip * c, LANES), c)]
        return o.at[chip, rows]

    def copies(ins, outs, scr, want):
        ici_send, ici_recv, d2d_send, d2d_recv = scr
        xi, yi, ci = _mesh_pos()
        me = 2 * xi + yi
        res = []
        for k in range(n):
            r = arrs[k].shape[0]
            mine, other = _half(ci, r // 2), _half(1 - ci, r // 2)
            for j, (px, py) in enumerate(_peer_chips(xi, yi)):
                s = k * N_PEER + j
                pc = 2 * px + py
                ici = dict(send_sem=ici_send.at[s], recv_sem=ici_recv.at[s], device_id=(px, py, ci), device_id_type=MESH)
                d2d = dict(send_sem=d2d_send.at[s], recv_sem=d2d_recv.at[s], device_id=(xi, yi, 1 - ci),
                           device_id_type=MESH)
                made = {}
                if "send" in want:
                    made["send"] = pltpu.make_async_remote_copy(
                        src_ref=ins[k].at[mine], dst_ref=dst(outs[k], k, me, mine), **ici)
                if "arrive" in want:
                    made["arrive"] = pltpu.make_async_remote_copy(
                        src_ref=ins[k].at[mine], dst_ref=dst(outs[k], k, pc, mine), **ici)
                if "forward" in want:
                    made["forward"] = pltpu.make_async_remote_copy(
                        src_ref=dst(outs[k], k, pc, mine), dst_ref=dst(outs[k], k, pc, mine), **d2d)
                if "handed" in want:
                    made["handed"] = pltpu.make_async_remote_copy(
                        src_ref=dst(outs[k], k, pc, other), dst_ref=dst(outs[k], k, pc, other), **d2d)
                res.append(made)
        return res

    def start(ins, outs, scr):
        for cp in copies(ins, outs, scr, ("send",)):
            cp["send"].start()

    def finish(ins, outs, scr):
        for cp in copies(ins, outs, scr, ("arrive", "forward")):
            cp["arrive"].wait_recv()
            cp["forward"].start()
        for cp in copies(ins, outs, scr, ("handed", "send", "forward")):
            cp["handed"].wait_recv()
            cp["send"].wait_send()
            cp["forward"].wait_send()

    shapes = [_sds((a.shape[0], N_CHIPS * a.shape[1]) if bc else (N_CHIPS,) + a.shape, a.dtype)
              for a, bc in zip(arrs, by_cols)]
    dma = pltpu.SemaphoreType.DMA
    return _Job(arrs, shapes, [dma((n * N_PEER,))] * 4, start, finish)


def _job_scatter(parts):
    n = len(parts)

    def copies(ins, outs, scr):
        send_sems, recv_sems = scr
        xi, yi, ci = _mesh_pos()
        res = []
        for k in range(n):
            for j, (px, py) in enumerate(_peer_chips(xi, yi)):
                s = k * N_PEER + j
                res.append(pltpu.make_async_remote_copy(
                    src_ref=ins[k].at[2 * px + py], dst_ref=outs[k].at[j], send_sem=send_sems.at[s],
                    recv_sem=recv_sems.at[s], device_id=(px, py, ci), device_id_type=MESH))
        return res

    def start(ins, outs, scr):
        for cp in copies(ins, outs, scr):
            cp.start()

    def finish(ins, outs, scr):
        for cp in copies(ins, outs, scr):
            cp.wait_recv()
            cp.wait_send()

    dma = pltpu.SemaphoreType.DMA
    return _Job(parts, [_sds((N_PEER,) + p.shape[1:], p.dtype) for p in parts], [dma((n * N_PEER,))] * 2, start, finish)


def _job_to_other_core(groups):
    pieces = [(g, a, off) for g, group in enumerate(groups) for a, off in group]
    n = len(pieces)

    def geometry(group):
        a0, off0 = group[0]
        if off0 is None:
            return a0.shape
        if a0.ndim == 4:
            return (N_CHIPS, a0.shape[2], a0.shape[3])
        return (a0.shape[0] // 2, sum(a.shape[1] for a, _ in group))

    def copies(ins, outs, scr):
        send_sems, recv_sems = scr
        xi, yi, ci = _mesh_pos()
        res = []
        for p, (g, a, off) in enumerate(pieces):
            if off is None:
                give, land = ins[p], outs[g]
            elif a.ndim == 4:
                give, land = ins[p].at[pl.ds(0, N_CHIPS), 1 - ci], outs[g]
            else:
                hr, w = a.shape[0] // 2, a.shape[1]
                give, land = ins[p].at[_half(1 - ci, hr)], outs[g].at[pl.ds(0, hr), pl.ds(off, w)]
            res.append(pltpu.make_async_remote_copy(
                src_ref=give, dst_ref=land, send_sem=send_sems.at[p], recv_sem=recv_sems.at[p],
                device_id=(xi, yi, 1 - ci), device_id_type=MESH))
        return res

    def start(ins, outs, scr):
        for cp in copies(ins, outs, scr):
            cp.start()

    def finish(ins, outs, scr):
        for cp in copies(ins, outs, scr):
            cp.wait_recv()
            cp.wait_send()

    dma = pltpu.SemaphoreType.DMA
    return _Job([a for _, a, _ in pieces], [_sds(geometry(group), group[0][0].dtype) for group in groups],
                [dma((n,))] * 2, start, finish)


def _call(body, *, name, grid, in_specs, out_specs, out_shape, args, scratch_shapes=(), parallel=False, job=None):
    n_in, n_out, n_scr = len(in_specs), len(out_specs), len(scratch_shapes)

    def params(sem):
        return pltpu.CompilerParams(dimension_semantics=sem, vmem_limit_bytes=VMEM_LIMIT_BYTES)

    if job is None:
        sem = ("parallel" if parallel else "arbitrary",) * len(grid)
        res = pl.pallas_call(
            body, name=name, grid=grid, in_specs=in_specs, out_specs=out_specs, out_shape=out_shape,
            scratch_shapes=list(scratch_shapes), compiler_params=params(sem))(*args)
        return list(res), []
    n_ji, n_jo = len(job.ins), len(job.out_shapes)

    def carried(*refs):
        ins, refs = refs[:n_in], refs[n_in:]
        j_ins, refs = refs[:n_ji], refs[n_ji:]
        outs, refs = refs[:n_out], refs[n_out:]
        j_outs, refs = refs[:n_jo], refs[n_jo:]
        scr, j_scr = refs[:n_scr], refs[n_scr:]
        ids = [pl.program_id(d) for d in range(len(grid))]
        first = ids[0] == 0
        last = ids[0] == grid[0] - 1
        for d in range(1, len(grid)):
            first = first & (ids[d] == 0)
            last = last & (ids[d] == grid[d] - 1)

        @pl.when(first)
        def _():
            job.start(j_ins, j_outs, j_scr)

        body(*ins, *outs, *scr)

        @pl.when(last)
        def _():
            job.finish(j_ins, j_outs, j_scr)

    hbm = pl.BlockSpec(memory_space=pl.ANY)
    res = pl.pallas_call(
        carried, name=name, grid=grid, in_specs=list(in_specs) + [hbm] * n_ji, out_specs=list(out_specs) + [hbm] * n_jo,
        out_shape=list(out_shape) + job.out_shapes, scratch_shapes=list(scratch_shapes) + job.scratch,
        compiler_params=params(("arbitrary",) * len(grid)))(*args, *job.ins)
    return list(res[:n_out]), list(res[n_out:])


def _run_job(job, name):
    n_i, n_o = len(job.ins), len(job.out_shapes)

    def body(*refs):
        ins, outs, scr = refs[:n_i], refs[n_i:n_i + n_o], refs[n_i + n_o:]
        job.start(ins, outs, scr)
        job.finish(ins, outs, scr)

    hbm = pl.BlockSpec(memory_space=pl.ANY)
    return list(pl.pallas_call(body, name=name, in_specs=[hbm] * n_i, out_specs=[hbm] * n_o, out_shape=job.out_shapes,
                               scratch_shapes=job.scratch)(*job.ins))


def _adam_values(w, m, v, g):
    m2 = ADAM_B1 * m + (1.0 - ADAM_B1) * g
    v2 = ADAM_B2 * v + (1.0 - ADAM_B2) * (g * g)
    delta = -ADAM_LR * ((m2 / (1.0 - ADAM_B1 ** ADAM_STEP)) / (jnp.sqrt(v2 / (1.0 - ADAM_B2 ** ADAM_STEP)) + ADAM_EPS)
                        + ADAM_WD * w)
    return delta, m2, v2


_P_WSP, _P_WGU, _P_NORM, _P_BG, _P_BSP, _P_HEAD, _P_LOSS, _P_ROWS = 0, 512, 576, 608, 616, 624, 720, 728


def _small_update(dgrads, ws, ms, vs):
    n = len(ws)

    def body(*refs):
        dwsp, dwgu, dg1, dgpm, dgpf, dgpo, dbg, dbspt, dgn, dlng, dlnb, loss_in = refs[:12]
        w_refs, m_refs, v_refs = refs[12:12 + n], refs[12 + n:12 + 2 * n], refs[12 + 2 * n:12 + 3 * n]
        loss_out = refs[12 + 3 * n]
        outs = refs[13 + 3 * n:13 + 7 * n]
        pack, slots, tot, send_sems, recv_sems = refs[13 + 7 * n:]
        xi, yi, ci = _mesh_pos()
        chip = 2 * xi + yi

        pack[...] = jnp.zeros_like(pack)
        for g in range(SGU_GROUPS):
            pack[_P_WSP + g * SGU_BLOCK:_P_WSP + (g + 1) * SGU_BLOCK] = dwsp[g]
        for j in range(N_CHIPS):
            pack[_P_WGU + GLA_RANK * j:_P_WGU + GLA_RANK * (j + 1)] = dwgu[0:GLA_RANK, LANES * j:LANES * (j + 1)]
        for k, r in enumerate((dg1, dgpm, dgpf, dgpo)):
            for q in range(8):
                pack[_P_NORM + 8 * k + q:_P_NORM + 8 * k + q + 1] = r[:, LANES * q:LANES * (q + 1)]
        for q in range(4):
            pack[_P_BG + q:_P_BG + q + 1] = dbg[:, LANES * q:LANES * (q + 1)]
        pack[_P_BSP:_P_BSP + SGU_GROUPS] = jnp.transpose(dbspt[...])[0:SGU_GROUPS]
        for k, r in enumerate((dgn, dlng, dlnb)):
            for j in range(N_CHIPS):
                for hh in range(4):
                    row = _P_HEAD + 32 * k + 8 * j + hh
                    pack[row:row + 1, 0:64] = r[:, 256 * hh + 64 * j:256 * hh + 64 * (j + 1)]
        pack[_P_LOSS:_P_LOSS + 1] = loss_in[...]

        to_sibling = pltpu.make_async_remote_copy(
            src_ref=pack, dst_ref=tot, send_sem=send_sems.at[N_PEER], recv_sem=recv_sems.at[N_PEER],
            device_id=(xi, yi, 1 - ci), device_id_type=MESH)
        to_sibling.start()
        to_sibling.wait_recv()
        to_sibling.wait_send()
        pack[...] = pack[...] + tot[...]
        slots[chip] = pack[...]

        def copy(j, slot):
            px, py = _peer_chips(xi, yi)[j]
            return pltpu.make_async_remote_copy(
                src_ref=pack, dst_ref=slots.at[slot(2 * px + py)], send_sem=send_sems.at[j], recv_sem=recv_sems.at[j],
                device_id=(px, py, ci), device_id_type=MESH)

        sends = [copy(j, lambda peer_chip: chip) for j in range(N_PEER)]
        for cp in sends:
            cp.start()
        for j in range(N_PEER):
            copy(j, lambda peer_chip: peer_chip).wait_recv()
        for cp in sends:
            cp.wait_send()
        acc = slots[0]
        for d in range(1, N_CHIPS):
            acc = acc + slots[d]
        tot[...] = acc
        loss_out[...] = tot[_P_LOSS:_P_LOSS + 1, 0:1]

        def step(k, g, pick, put):
            d, m2, v2 = _adam_values(pick(w_refs[k]), pick(m_refs[k]), pick(v_refs[k]), g)
            for o, val in zip((outs[k], outs[n + k], outs[2 * n + k], outs[3 * n + k]), (g, d, m2, v2)):
                put(o, val)

        def whole(ref):
            return ref[0]

        def put_whole(ref, val):
            ref[0] = val

        for g in range(SGU_GROUPS):
            def pick_g(ref, g=g):
                return ref[0, g]

            def put_g(ref, val, g=g):
                ref[0, g] = val

            step(0, tot[_P_WSP + g * SGU_BLOCK:_P_WSP + (g + 1) * SGU_BLOCK], pick_g, put_g)
        step(1, tot[pl.ds(pl.multiple_of(_P_WGU + GLA_RANK * chip, GLA_RANK), GLA_RANK), :], whole, put_whole)
        for k, (base, chunks) in enumerate(((_P_NORM, 8), (_P_NORM + 8, 8), (_P_NORM + 16, 8), (_P_NORM + 24, 8), (_P_BG, 4))):
            for q in range(chunks):
                def pick_q(ref, q=q):
                    return ref[:, LANES * q:LANES * (q + 1)]

                def put_q(ref, val, q=q):
                    ref[:, LANES * q:LANES * (q + 1)] = val

                step(2 + k, tot[base + q:base + q + 1], pick_q, put_q)
        step(7, tot[_P_BSP:_P_BSP + SGU_GROUPS], whole, put_whole)
        for k in range(3):
            mine = tot[pl.ds(pl.multiple_of(_P_HEAD + 32 * k + 8 * chip, 8), 8), :]
            step(8 + k, mine[0:4, 0:64], whole, put_whole)

    shapes = [_sds(w.shape, F32) for w in ws]
    res = pl.pallas_call(
        body, name="small_update", in_specs=[_whole()] * (12 + 3 * n), out_specs=[_whole()] * (1 + 4 * n),
        out_shape=[_sds((1, 1), F32)] + shapes * 4,
        scratch_shapes=[pltpu.VMEM((_P_ROWS, LANES), F32), pltpu.VMEM((N_CHIPS, _P_ROWS, LANES), F32),
                        pltpu.VMEM((_P_ROWS, LANES), F32), pltpu.SemaphoreType.DMA((N_PEER + 1,)),
                        pltpu.SemaphoreType.DMA((N_PEER + 1,))],
        compiler_params=pltpu.CompilerParams(vmem_limit_bytes=VMEM_LIMIT_BYTES),
    )(*dgrads, *ws, *ms, *vs)
    return res[0].reshape(()), [list(res[1 + i * n:1 + (i + 1) * n]) for i in range(4)]


def _w_in_pieces():
    blk = D_IN // N_CHIPS
    pieces = []
    for s in range(len(_IN_SPLITS)):
        lo_s, hi_s = _IN_STARTS[s], _IN_STARTS[s + 1]
        for j in range(N_CHIPS):
            lo, hi = max(lo_s, j * blk), min(hi_s, (j + 1) * blk)
            if lo < hi:
                pieces.append((j, lo - j * blk, _IN_DST[s] + lo - lo_s, hi - lo))
    return pieces


def _relayout_w_in(gathered):
    _, rows, blk = gathered.shape
    tr = 256

    def body(g_ref, o_ref):
        o_ref[:, OFF_AL:N_ALL] = jnp.zeros((tr, LANES), BF16)
        for j, src, dst, w in _w_in_pieces():
            o_ref[:, dst:dst + w] = g_ref[j, :, src:src + w]

    res, _ = _call(body, name="relayout_w_in", grid=(rows // tr,), parallel=True,
                   in_specs=[pl.BlockSpec((N_CHIPS, tr, blk), lambda i: (0, i, 0))],
                   out_specs=[pl.BlockSpec((tr, N_ALL), lambda i: (i, 0))],
                   out_shape=[_sds((rows, N_ALL), BF16)], args=(gathered,))
    return res[0]


def _update_row_tile(rows):
    for t in range(min(rows, 256), 7, -8):
        if rows % t == 0:
            return t
    return rows


def _my_half(first_ref, second_ref):
    return jnp.where(lax.axis_index("c") == 0, first_ref[...], second_ref[...])


def _presum_w_in(dws, theirs, row0, rows, name):
    hr = theirs[0].shape[0]
    blk = D_IN // N_CHIPS
    tr = 128
    nh, t0 = hr // tr, row0 // tr
    n = len(dws)

    def body(*refs):
        dw_refs, q_refs, (o_ref, s_scr) = refs[:2 * n], refs[2 * n:3 * n], refs[3 * n:]
        for p, (a, off) in enumerate(dws):
            w = a.shape[1]
            s_scr[:, off:off + w] = (_my_half(dw_refs[2 * p], dw_refs[2 * p + 1]) + q_refs[p][...]).astype(BF16)
        for j, src, dst, w in _w_in_pieces():
            o_ref[j, :, src:src + w] = s_scr[:, dst:dst + w]

    in_specs, args = [], []
    for a, _ in dws:
        w = a.shape[1]
        in_specs += [pl.BlockSpec((tr, w), lambda i: (i + t0, 0)), pl.BlockSpec((tr, w), lambda i: (i + t0 + nh, 0))]
        args += [a, a]
    in_specs += [pl.BlockSpec((tr, q.shape[1]), lambda i: (i + t0, 0)) for q in theirs]
    res, _ = _call(body, name=name, grid=(rows // tr,), parallel=True, in_specs=in_specs,
                   out_specs=[pl.BlockSpec((N_CHIPS, tr, blk), lambda i: (0, i, 0))],
                   out_shape=[_sds((N_CHIPS, rows, blk), BF16)], scratch_shapes=[pltpu.VMEM((tr, N_ALL), BF16)],
                   args=(*args, *theirs))
    return res[0]


def _presum(dw, theirs, name):
    if dw.ndim == 4:
        _, _, hr, c = dw.shape
        tr = _update_row_tile(hr)
        first = pl.BlockSpec((1, 1, tr, c), lambda j, i: (j, 0, i, 0))
        second = pl.BlockSpec((1, 1, tr, c), lambda j, i: (j, 1, i, 0))
        other = pl.BlockSpec((1, tr, c), lambda j, i: (j, i, 0))
    else:
        hr, c = dw.shape[0] // 2, dw.shape[1] // N_CHIPS
        tr = _update_row_tile(hr)
        nh = hr // tr
        first = pl.BlockSpec((tr, c), lambda j, i: (i, j))
        second = pl.BlockSpec((tr, c), lambda j, i: (i + nh, j))
        other = pl.BlockSpec((tr, c), lambda j, i: (i, j))

    def body(a_ref, b_ref, q_ref, o_ref):
        mine = _my_half(a_ref, b_ref).reshape(tr, c)
        o_ref[...] = (mine + q_ref[...].reshape(tr, c)).astype(BF16).reshape(o_ref.shape)

    res, _ = _call(body, name=name, grid=(N_CHIPS, hr // tr), parallel=True, in_specs=[first, second, other],
                   out_specs=[pl.BlockSpec((1, tr, c), lambda j, i: (j, i, 0))],
                   out_shape=[_sds((N_CHIPS, hr, c), BF16)], args=(dw, dw, theirs))
    return res[0]


def _sum_slots(own, slots, name):
    rows, cols = own.shape
    tr = _update_row_tile(rows)

    def body(own_ref, s_ref, o_ref):
        acc = own_ref[...].astype(F32)
        for j in range(N_PEER):
            acc = acc + s_ref[j].astype(F32)
        o_ref[...] = acc

    res, _ = _call(body, name=name, grid=(rows // tr,), parallel=True,
                   in_specs=[pl.BlockSpec((tr, cols), lambda i: (i, 0)), pl.BlockSpec((N_PEER, tr, cols), lambda i: (0, i, 0))],
                   out_specs=[pl.BlockSpec((tr, cols), lambda i: (i, 0))], out_shape=[_sds((rows, cols), F32)],
                   args=(own, slots))
    return res[0]


def _adamw(w, m, v, g_mine, g_theirs, name, job=None):
    rows, cols = w.shape
    part_rows = [p.shape[0] for p in g_mine]
    assert sum(part_rows) == rows // 2 and [p.shape[0] for p in g_theirs] == part_rows
    tr = _update_row_tile(min(part_rows))
    assert all(r % tr == 0 for r in part_rows)
    nh = (rows // 2) // tr
    starts = [sum(part_rows[:k]) // tr for k in range(len(part_rows))]
    n_parts = len(part_rows)

    def body(w_ref, m_ref, v_ref, *rest):
        g_refs, (g_out, d_out, m_out, v_out) = rest[:-4], rest[-4:]
        step = pl.program_id(0)
        mine_here = (step // nh) == lax.axis_index("c")
        q = step % nh
        g = None
        for k in reversed(range(n_parts)):
            val = jnp.where(mine_here, g_refs[k][...], g_refs[n_parts + k][...])
            g = val if g is None else jnp.where(q < starts[k + 1], val, g)
        d, m2, v2 = _adam_values(w_ref[...], m_ref[...], v_ref[...], g)
        g_out[...] = g
        m_out[...] = m2
        v_out[...] = v2
        d_out[...] = d

    def g_spec(k):
        last = part_rows[k] // tr - 1
        return pl.BlockSpec((tr, cols), lambda i: (jnp.clip(i % nh - starts[k], 0, last), 0))

    spec = pl.BlockSpec((tr, cols), lambda i: (i, 0))
    return _call(body, name=name, grid=(rows // tr,), parallel=True,
                 in_specs=[spec] * 3 + [g_spec(k) for k in range(n_parts)] * 2, out_specs=[spec] * 4,
                 out_shape=[_sds((rows, cols), F32)] * 4, args=(w, m, v, *g_mine, *g_theirs), job=job)


def _inproj_fwd(x, g1, w_all, job=None):
    T = x.shape[0]
    tT = _row_tile(T, 512)

    def body(x_ref, g_ref, w_ref, a_ref, proj_ref, alow_ref):
        xv = x_ref[...]
        a = (xv * _rms_stats(xv) * g_ref[...]).astype(BF16)
        a_ref[...] = a
        for j in range(N_MAIN // 1024):
            cols = slice(j * 1024, (j + 1) * 1024)
            proj_ref[:, cols] = _dot(a, w_ref[:, cols]).astype(BF16)
        alow_ref[...] = _dot(a, w_ref[:, N_MAIN:N_ALL])

    row = lambda w: pl.BlockSpec((tT, w), lambda i: (i, 0))
    return _call(
        body, name="inproj_fwd", grid=(T // tT,), parallel=True,
        in_specs=[row(D_MODEL), pl.BlockSpec((1, D_MODEL), lambda i: (0, 0)), _whole()],
        out_specs=[row(D_MODEL), row(N_MAIN), row(LANES)],
        out_shape=[_sds((T, D_MODEL), BF16), _sds((T, N_MAIN), BF16), _sds((T, LANES), F32)],
        args=(x, g1, w_all), job=job)


def _gla_decay_terms(al_ref, wgu_ref, bg_ref, tT):
    logit = _dot_f32(al_ref[...], wgu_ref[...]) + bg_ref[...]
    la = _log_sigmoid(logit) * (1.0 / GLA_TAU)
    delta = _dot_exact_lhs(_chunk_masks(tT, upper=True), la)
    return logit, la, delta


def _gla_fwd(proj, alow, wgu, b_gate, gn, job=None):
    T = proj.shape[0]
    tT = _row_tile(T, 512)
    nc = tT // CHUNK

    def body(q_ref, k_ref, v_ref, r_ref, al_ref, wgu_ref, bg_ref, gn_ref, y_ref, st_ref, s_scr):
        @pl.when(pl.program_id(0) == 0)
        def _():
            s_scr[...] = jnp.zeros_like(s_scr)

        _, la, delta = _gla_decay_terms(al_ref, wgu_ref, bg_ref, tT)
        kdec = (k_ref[...].astype(F32) * jnp.exp(delta)).astype(BF16)
        heads = range(GLA_HEADS)
        kcs = [slice(h * GLA_DK, (h + 1) * GLA_DK) for h in heads]
        vcs = [slice(h * GLA_DV, (h + 1) * GLA_DV) for h in heads]
        state = [s_scr[h] for h in heads]
        for c in range(nc):
            rows = slice(c * CHUNK, (c + 1) * CHUNK)
            first = slice(c * CHUNK, c * CHUNK + 1)
            dec = jnp.exp(la[first, :] + delta[first, :])
            upd_t = [_dot(v_ref[rows, vcs[h]], kdec[rows, kcs[h]], _TN) for h in heads]
            qs = [(q_ref[rows, kcs[h]].astype(F32) * (GLA_DK ** -0.5)).astype(BF16) for h in heads]
            for h in heads:
                state[h] = state[h] * dec[:, kcs[h]] + upd_t[h]
                st_ref[c, h] = state[h]
            o = [_dot(qs[h], state[h].astype(BF16), _NT) for h in heads]
            for h in heads:
                on = o[h] * _rms_stats(o[h]) * gn_ref[:, vcs[h]]
                rr = r_ref[rows, vcs[h]].astype(F32)
                y_ref[rows, vcs[h]] = (on * (rr * _sigmoid(rr))).astype(BF16)
        for h in heads:
            s_scr[h] = state[h]

    blk = lambda w, j: pl.BlockSpec((tT, w), lambda i: (i, j))
    return _call(
        body, name="gla_fwd", grid=(T // tT,),
        in_specs=[blk(512, 0), blk(512, 1), blk(1024, 1), blk(1024, 2), blk(LANES, 0), _whole(), _whole(), _whole()],
        out_specs=[pl.BlockSpec((tT, GLA_V), lambda i: (i, 0)),
                   pl.BlockSpec((nc, GLA_HEADS, GLA_DV, GLA_DK), lambda i: (i, 0, 0, 0))],
        out_shape=[_sds((T, GLA_V), BF16), _sds((T // CHUNK, GLA_HEADS, GLA_DV, GLA_DK), F32)],
        scratch_shapes=[pltpu.VMEM((GLA_HEADS, GLA_DV, GLA_DK), F32)],
        args=(proj, proj, proj, proj, alow, wgu, b_gate, gn), job=job)


def _sgu_mask():
    i = lax.broadcasted_iota(jnp.int32, (SGU_BLOCK, SGU_BLOCK), 0)
    j = lax.broadcasted_iota(jnp.int32, (SGU_BLOCK, SGU_BLOCK), 1)
    return lax.shift_right_logical(j, 6) <= lax.shift_right_logical(i, 6)


def _sgu_fwd(proj, ln_g, ln_b, w_sp, b_sp_t):
    T = proj.shape[0]
    tT = _row_tile(T, 512)
    nb = tT // SGU_BLOCK

    def body(su_ref, sv_ref, lg_ref, lb_ref, w_ref, b_ref, y_ref):
        mask = _sgu_mask()
        for g in range(SGU_GROUPS):
            gc = slice(g * SGU_DG, (g + 1) * SGU_DG)
            wm = jnp.where(mask, w_ref[g], 0.0).astype(BF16)
            vf = _gelu(sv_ref[:, gc].astype(F32))
            mu = jnp.mean(vf, axis=-1, keepdims=True)
            vc = vf - mu
            rstd = lax.rsqrt(jnp.mean(vc * vc, axis=-1, keepdims=True) + EPS)
            vn = (vc * rstd * lg_ref[:, gc] + lb_ref[:, gc]).astype(BF16)
            u = _gelu(su_ref[:, gc].astype(F32))
            for b in range(nb):
                rows = slice(b * SGU_BLOCK, (b + 1) * SGU_BLOCK)
                mixed = _dot(wm, vn[rows, :]) + b_ref[:, g:g + 1]
                y_ref[rows, gc] = (u[rows, :] * mixed).astype(BF16)

    blk = lambda j: pl.BlockSpec((tT, 1024), lambda i: (i, j))
    res, _ = _call(body, name="sgu_fwd", grid=(T // tT,), parallel=True,
                   in_specs=[blk(3), blk(4), _whole(), _whole(), _whole(), _whole()],
                   out_specs=[pl.BlockSpec((tT, 1024), lambda i: (i, 0))], out_shape=[_sds((T, 1024), BF16)],
                   args=(proj, proj, ln_g, ln_b, w_sp, b_sp_t))
    return res[0]


def _merge_fwd(x, proj, y_gla, y_sgu, w_bg, w_bs, w_o, g_pm, job=None):
    T = x.shape[0]
    tT = _row_tile(T, 512)

    def body(x_ref, gg_ref, gs_ref, yg_ref, ys_ref, wbg_ref, wbs_ref, wo_ref, g_ref,
             zg_ref, zs_ref, mg_ref, mix_ref, x1_ref):
        zg = _dot(yg_ref[...], wbg_ref[...])
        zs = _dot(ys_ref[...], wbs_ref[...])
        zg_ref[...] = zg.astype(BF16)
        zs_ref[...] = zs.astype(BF16)
        merged = (_sigmoid(gg_ref[...].astype(F32)) * zg + _sigmoid(gs_ref[...].astype(F32)) * zs).astype(BF16)
        mg_ref[...] = merged
        mix = _dot(merged, wo_ref[...])
        mix_ref[...] = mix
        x1_ref[...] = x_ref[...] + mix * _rms_stats(mix) * g_ref[...]

    row = pl.BlockSpec((tT, D_MODEL), lambda i: (i, 0))
    blk = lambda j: pl.BlockSpec((tT, 1024), lambda i: (i, j))
    sds = lambda dt: _sds((T, D_MODEL), dt)
    return _call(body, name="merge_fwd", grid=(T // tT,), parallel=True,
                 in_specs=[row, blk(5), blk(6), row, row, _whole(), _whole(), _whole(),
                           pl.BlockSpec((1, D_MODEL), lambda i: (0, 0))],
                 out_specs=[row] * 5, out_shape=[sds(BF16), sds(BF16), sds(BF16), sds(F32), sds(F32)],
                 args=(x, proj, proj, y_gla, y_sgu, w_bg, w_bs, w_o, g_pm), job=job)


def _ffn_fwd_bwd(x1, tgt, w_fi_top, w_fi_bot, w_fo, g_pf, g_po):
    T = x1.shape[0]
    tT = _row_tile(T, 256)
    half = D_FF // 2
    kh = D_MODEL // 2

    def body(x1_ref, t_ref, top_ref, bot_ref, wfo_ref, gpf_ref, gpo_ref,
             h_ref, f_ref, dgu_ref, dy_ref, dx1_ref, loss_ref, dgpf_ref, dgpo_ref, gu_scr):
        @pl.when(pl.program_id(0) == 0)
        def _():
            loss_ref[...] = jnp.zeros_like(loss_ref)
            dgpf_ref[...] = jnp.zeros_like(dgpf_ref)
            dgpo_ref[...] = jnp.zeros_like(dgpo_ref)

        x1v = x1_ref[...]
        r2 = _rms_stats(x1v)
        h = (x1v * r2 * gpf_ref[...]).astype(BF16)
        h_ref[...] = h
        y = jnp.zeros((tT, D_MODEL), F32)
        for j in range(2):
            gc = slice(j * half, (j + 1) * half)
            uc = slice(D_FF + j * half, D_FF + (j + 1) * half)
            gate = _dot(h[:, :kh], top_ref[j]) + _dot(h[:, kh:], bot_ref[j])
            up = _dot(h[:, :kh], top_ref[2 + j]) + _dot(h[:, kh:], bot_ref[2 + j])
            gu_scr[:, gc] = gate
            gu_scr[:, uc] = up
            f = (gate * _sigmoid(gate) * up).astype(BF16)
            f_ref[:, gc] = f
            y = y + _dot(f, wfo_ref[gc, :])
        r3 = _rms_stats(y)
        x2 = x1v + y * r3 * gpo_ref[...]
        err = x2 - t_ref[...]
        loss_ref[...] += jnp.sum(err * err) * (0.5 / D_MODEL)
        dx2 = err * (1.0 / D_MODEL)
        dy, dg = _rms_bwd(dx2, y, r3, gpo_ref[...])
        dgpo_ref[...] += jnp.sum(dg, axis=0, keepdims=True)
        dyb = dy.astype(BF16)
        dy_ref[...] = dyb
        dh_top = jnp.zeros((tT, kh), F32)
        dh_bot = jnp.zeros((tT, kh), F32)
        for j in range(2):
            gc = slice(j * half, (j + 1) * half)
            uc = slice(D_FF + j * half, D_FF + (j + 1) * half)
            df = _dot(dyb, wfo_ref[gc, :], _NT)
            gate = gu_scr[:, gc]
            up = gu_scr[:, uc]
            sg = _sigmoid(gate)
            dgate = (df * up * (sg * (1.0 + gate * (1.0 - sg)))).astype(BF16)
            dup = (df * (gate * sg)).astype(BF16)
            dgu_ref[:, gc] = dgate
            dgu_ref[:, uc] = dup
            dh_top = dh_top + _dot(dgate, top_ref[j], _NT) + _dot(dup, top_ref[2 + j], _NT)
            dh_bot = dh_bot + _dot(dgate, bot_ref[j], _NT) + _dot(dup, bot_ref[2 + j], _NT)
        dh = jnp.concatenate([dh_top, dh_bot], axis=1)
        dx1n, dg2 = _rms_bwd(dh, x1v, r2, gpf_ref[...])
        dgpf_ref[...] += jnp.sum(dg2, axis=0, keepdims=True)
        dx1_ref[...] = dx2 + dx1n

    row = lambda w: pl.BlockSpec((tT, w), lambda i: (i, 0))
    vec = pl.BlockSpec((1, D_MODEL), lambda i: (0, 0))
    res, _ = _call(
        body, name="ffn_fwd_bwd", grid=(T // tT,),
        in_specs=[row(D_MODEL), row(D_MODEL), _whole(), _whole(), _whole(), vec, vec],
        out_specs=[row(D_MODEL), row(D_FF), row(2 * D_FF), row(D_MODEL), row(D_MODEL),
                   pl.BlockSpec((1, LANES), lambda i: (0, 0)), vec, vec],
        out_shape=[_sds((T, D_MODEL), BF16), _sds((T, D_FF), BF16), _sds((T, 2 * D_FF), BF16), _sds((T, D_MODEL), BF16),
                   _sds((T, D_MODEL), F32), _sds((1, LANES), F32), _sds((1, D_MODEL), F32), _sds((1, D_MODEL), F32)],
        scratch_shapes=[pltpu.VMEM((tT, 2 * D_FF), F32)], args=(x1, tgt, w_fi_top, w_fi_bot, w_fo, g_pf, g_po))
    return res


def _merge_bwd(dx1, mix, proj, zg, zs, w_bg, w_bs, w_o, g_pm, job=None):
    T = dx1.shape[0]
    tT = _row_tile(T, 512)

    def body(dx1_ref, mix_ref, gg_ref, gs_ref, zg_ref, zs_ref, wbg_ref, wbs_ref, wo_ref, g_ref,
             dmix_ref, dzg_ref, dzs_ref, dgate_ref, dyg_ref, dys_ref, dgpm_ref):
        @pl.when(pl.program_id(0) == 0)
        def _():
            dgpm_ref[...] = jnp.zeros_like(dgpm_ref)

        mix = mix_ref[...]
        dmix, dg = _rms_bwd(dx1_ref[...], mix, _rms_stats(mix), g_ref[...])
        dgpm_ref[...] += jnp.sum(dg, axis=0, keepdims=True)
        dmb = dmix.astype(BF16)
        dmix_ref[...] = dmb
        dmerged = _dot(dmb, wo_ref[...], _NT)
        for k, (gate_ref, z_ref, w_ref, dz_ref, dy_ref) in enumerate((
                (gg_ref, zg_ref, wbg_ref, dzg_ref, dyg_ref), (gs_ref, zs_ref, wbs_ref, dzs_ref, dys_ref))):
            sg = _sigmoid(gate_ref[...].astype(F32))
            dz = (dmerged * sg).astype(BF16)
            dz_ref[...] = dz
            dgate_ref[:, k * 1024:(k + 1) * 1024] = (dmerged * z_ref[...].astype(F32) * (sg * (1.0 - sg))).astype(BF16)
            dy_ref[...] = _dot(dz, w_ref[...], _NT).astype(BF16)

    row = pl.BlockSpec((tT, D_MODEL), lambda i: (i, 0))
    blk = lambda j: pl.BlockSpec((tT, 1024), lambda i: (i, j))
    vec = pl.BlockSpec((1, D_MODEL), lambda i: (0, 0))
    sds = _sds((T, D_MODEL), BF16)
    return _call(
        body, name="merge_bwd", grid=(T // tT,),
        in_specs=[row, row, blk(5), blk(6), row, row, _whole(), _whole(), _whole(), vec],
        out_specs=[row, row, row, pl.BlockSpec((tT, W_MRG), lambda i: (i, 0)), row, row, vec],
        out_shape=[sds, sds, sds, _sds((T, W_MRG), BF16), sds, sds, _sds((1, D_MODEL), F32)],
        args=(dx1, mix, proj, proj, zg, zs, w_bg, w_bs, w_o, g_pm), job=job)


def _sgu_bwd(proj, dy_sgu, ln_g, ln_b, w_sp, b_sp_t, job=None):
    T = proj.shape[0]
    tT = _row_tile(T, 512)
    nb = tT // SGU_BLOCK

    def body(su_ref, sv_ref, dy_ref, lg_ref, lb_ref, w_ref, b_ref, dp_ref, dw_ref, dbt_ref, dlg_ref, dlb_ref):
        @pl.when(pl.program_id(0) == 0)
        def _():
            dw_ref[...] = jnp.zeros_like(dw_ref)
            dbt_ref[...] = jnp.zeros_like(dbt_ref)
            dlg_ref[...] = jnp.zeros_like(dlg_ref)
            dlb_ref[...] = jnp.zeros_like(dlb_ref)

        mask = _sgu_mask()
        lane = lax.broadcasted_iota(jnp.int32, (SGU_BLOCK, LANES), 1)
        for g in range(SGU_GROUPS):
            gc = slice(g * SGU_DG, (g + 1) * SGU_DG)
            gc_v = slice(1024 + g * SGU_DG, 1024 + (g + 1) * SGU_DG)
            wm = jnp.where(mask, w_ref[g], 0.0).astype(BF16)
            vf, dvf_dsv = _gelu_and_grad(sv_ref[:, gc].astype(F32))
            mu = jnp.mean(vf, axis=-1, keepdims=True)
            vc = vf - mu
            rstd = lax.rsqrt(jnp.mean(vc * vc, axis=-1, keepdims=True) + EPS)
            vhat = vc * rstd
            vn = (vhat * lg_ref[:, gc] + lb_ref[:, gc]).astype(BF16)
            u, du_dsu = _gelu_and_grad(su_ref[:, gc].astype(F32))
            dy = dy_ref[:, gc].astype(F32)
            dmixed = (dy * u).astype(BF16)
            dvn_parts = []
            dw_acc = jnp.zeros((SGU_BLOCK, SGU_BLOCK), F32)
            db_acc = jnp.zeros((SGU_BLOCK, 1), F32)
            for b in range(nb):
                rows = slice(b * SGU_BLOCK, (b + 1) * SGU_BLOCK)
                mixed = _dot(wm, vn[rows, :]) + b_ref[:, g:g + 1]
                dp_ref[rows, gc] = (dy[rows, :] * mixed * du_dsu[rows, :]).astype(BF16)
                dvn_parts.append(_dot(wm, dmixed[rows, :], _TN))
                dw_acc = dw_acc + _dot(dmixed[rows, :], vn[rows, :], _NT)
                db_acc = db_acc + jnp.sum(dmixed[rows, :].astype(F32), axis=-1, keepdims=True)
            dw_ref[g] += jnp.where(mask, dw_acc, 0.0)
            dbt_ref[...] += jnp.where(lane == g, db_acc, 0.0)
            dvn = jnp.concatenate(dvn_parts, axis=0)
            dlg_ref[:, gc] += jnp.sum(dvn * vhat, axis=0, keepdims=True)
            dlb_ref[:, gc] += jnp.sum(dvn, axis=0, keepdims=True)
            dvh = dvn * lg_ref[:, gc]
            dvf = rstd * (dvh - jnp.mean(dvh, axis=-1, keepdims=True)
                          - vhat * jnp.mean(dvh * vhat, axis=-1, keepdims=True))
            dp_ref[:, gc_v] = (dvf * dvf_dsv).astype(BF16)

    blk = lambda j: pl.BlockSpec((tT, 1024), lambda i: (i, j))
    row = lambda w: pl.BlockSpec((tT, w), lambda i: (i, 0))
    vec = pl.BlockSpec((1, 1024), lambda i: (0, 0))
    return _call(
        body, name="sgu_bwd", grid=(T // tT,),
        in_specs=[blk(3), blk(4), row(1024), _whole(), _whole(), _whole(), _whole()],
        out_specs=[row(W_SGU), pl.BlockSpec((SGU_GROUPS, SGU_BLOCK, SGU_BLOCK), lambda i: (0, 0, 0)),
                   pl.BlockSpec((SGU_BLOCK, LANES), lambda i: (0, 0)), vec, vec],
        out_shape=[_sds((T, W_SGU), BF16), _sds((SGU_GROUPS, SGU_BLOCK, SGU_BLOCK), F32), _sds((SGU_BLOCK, LANES), F32),
                   _sds((1, 1024), F32), _sds((1, 1024), F32)],
        args=(proj, proj, dy_sgu, ln_g, ln_b, w_sp, b_sp_t), job=job)


def _gla_bwd(proj, alow, wgu, b_gate, gn, states, dy_gla, job=None):
    T = proj.shape[0]
    tT = _row_tile(T, 512)
    nc = tT // CHUNK
    nt = T // tT

    def body(q_ref, k_ref, v_ref, r_ref, al_ref, wgu_ref, bg_ref, gn_ref, st_ref, sp_ref, dy_ref,
             dp_ref, dal_ref, dgn_ref, dbg_ref, dwgu_ref, g_scr, dd_scr, dt_scr):
        step = pl.program_id(0)

        @pl.when(step == 0)
        def _():
            g_scr[...] = jnp.zeros_like(g_scr)
            dgn_ref[...] = jnp.zeros_like(dgn_ref)
            dbg_ref[...] = jnp.zeros_like(dbg_ref)
            dwgu_ref[...] = jnp.zeros_like(dwgu_ref)

        has_prev = jnp.where(step == nt - 1, 0.0, 1.0)
        logit, la, delta = _gla_decay_terms(al_ref, wgu_ref, bg_ref, tT)
        e = jnp.exp(delta)
        kdec_f = k_ref[...].astype(F32) * e
        kdec = kdec_f.astype(BF16)
        heads = range(GLA_HEADS)
        kcs = [slice(h * GLA_DK, (h + 1) * GLA_DK) for h in heads]
        vcs = [slice(h * GLA_DV, (h + 1) * GLA_DV) for h in heads]
        carry = [g_scr[h] for h in heads]
        dgn_acc = [jnp.zeros((1, GLA_DV), F32) for _ in heads]
        for c in reversed(range(nc)):
            rows = slice(c * CHUNK, (c + 1) * CHUNK)
            first = slice(c * CHUNK, c * CHUNK + 1)
            dec = jnp.exp(la[first, :] + delta[first, :])
            s_b = [st_ref[c, h].astype(BF16) for h in heads]
            qs = [(q_ref[rows, kcs[h]].astype(F32) * (GLA_DK ** -0.5)).astype(BF16) for h in heads]
            o = [_dot(qs[h], s_b[h], _NT) for h in heads]
            do = []
            for h in heads:
                rstd = _rms_stats(o[h])
                ohat = o[h] * rstd
                gnh = gn_ref[:, vcs[h]]
                dy = dy_ref[rows, vcs[h]].astype(F32)
                rr = r_ref[rows, vcs[h]].astype(F32)
                sg = _sigmoid(rr)
                don = dy * (rr * sg)
                dp_ref[rows, OFF_R + h * GLA_DV:OFF_R + (h + 1) * GLA_DV] = (
                    dy * (ohat * gnh) * (sg * (1.0 + rr * (1.0 - sg)))).astype(BF16)
                dgn_acc[h] = dgn_acc[h] + jnp.sum(don * ohat, axis=0, keepdims=True)
                dn = don * gnh
                do.append((rstd * (dn - ohat * jnp.mean(dn * ohat, axis=-1, keepdims=True))).astype(BF16))
            dq = [_dot(do[h], s_b[h]) for h in heads]
            g_t = [_dot(do[h], qs[h], _TN) + carry[h] for h in heads]
            g_b = [g_t[h].astype(BF16) for h in heads]
            dv = [_dot(kdec[rows, kcs[h]], g_b[h], _NT) for h in heads]
            dkdec = [_dot(v_ref[rows, vcs[h]], g_b[h]) for h in heads]
            for h in heads:
                s_prev = st_ref[c - 1, h] if c > 0 else sp_ref[0, h] * has_prev
                ddec = jnp.sum(g_t[h] * s_prev, axis=0, keepdims=True)
                carry[h] = g_t[h] * dec[:, kcs[h]]
                dp_ref[rows, OFF_Q + h * GLA_DK:OFF_Q + (h + 1) * GLA_DK] = (dq[h] * (GLA_DK ** -0.5)).astype(BF16)
                dp_ref[rows, OFF_V + h * GLA_DV:OFF_V + (h + 1) * GLA_DV] = dv[h].astype(BF16)
                dp_ref[rows, OFF_K + h * GLA_DK:OFF_K + (h + 1) * GLA_DK] = (dkdec[h] * e[rows, kcs[h]]).astype(BF16)
                dd_scr[rows, kcs[h]] = dkdec[h] * kdec_f[rows, kcs[h]]
                dt_scr[rows, kcs[h]] = jnp.broadcast_to(ddec * dec[:, kcs[h]], (CHUNK, GLA_DK))
        for h in heads:
            g_scr[h] = carry[h]
            dgn_ref[:, vcs[h]] += dgn_acc[h]
        dla = _dot_exact_lhs(_chunk_masks(tT, upper=False), dd_scr[...]) + dt_scr[...]
        dlogit = dla * (1.0 / GLA_TAU) * _sigmoid(-logit)
        dbg_ref[...] += jnp.sum(dlogit, axis=0, keepdims=True)
        dwgu_ref[...] += _dot_f32(al_ref[...], dlogit, _TN)
        dal_ref[...] = _dot_f32(dlogit, wgu_ref[...], _NT).astype(BF16)

    rev = lambda i: nt - 1 - i
    blk = lambda w, j: pl.BlockSpec((tT, w), lambda i: (rev(i), j))
    st_blk = pl.BlockSpec((nc, GLA_HEADS, GLA_DV, GLA_DK), lambda i: (rev(i), 0, 0, 0))
    sp_blk = pl.BlockSpec((1, GLA_HEADS, GLA_DV, GLA_DK), lambda i: (jnp.maximum(rev(i) * nc - 1, 0), 0, 0, 0))
    return _call(
        body, name="gla_bwd", grid=(nt,),
        in_specs=[blk(512, 0), blk(512, 1), blk(1024, 1), blk(1024, 2), blk(LANES, 0), _whole(), _whole(), _whole(),
                  st_blk, sp_blk, blk(GLA_V, 0)],
        out_specs=[blk(W_GLA, 0), blk(LANES, 0), pl.BlockSpec((1, GLA_V), lambda i: (0, 0)),
                   pl.BlockSpec((1, GLA_QK), lambda i: (0, 0)), pl.BlockSpec((LANES, GLA_QK), lambda i: (0, 0))],
        out_shape=[_sds((T, W_GLA), BF16), _sds((T, LANES), BF16), _sds((1, GLA_V), F32), _sds((1, GLA_QK), F32),
                   _sds((LANES, GLA_QK), F32)],
        scratch_shapes=[pltpu.VMEM((GLA_HEADS, GLA_DV, GLA_DK), F32), pltpu.VMEM((tT, GLA_QK), F32),
                        pltpu.VMEM((tT, GLA_QK), F32)],
        args=(proj, proj, proj, proj, alow, wgu, b_gate, gn, states, states, dy_gla), job=job)


def _inproj_bwd(x, dx1, g1, w_all, dparts, job=None):
    T = x.shape[0]
    tT = _row_tile(T, 512)
    offs = (0, W_GLA, W_GLA + W_SGU, N_MAIN)

    def body(x_ref, dx1_ref, g_ref, w_ref, *rest):
        part_refs, (dx_ref, dg_ref) = rest[:len(offs)], rest[len(offs):]

        @pl.when(pl.program_id(0) == 0)
        def _():
            dg_ref[...] = jnp.zeros_like(dg_ref)

        da = jnp.zeros((tT, D_MODEL), F32)
        for off, p_ref in zip(offs, part_refs):
            da = da + _dot(p_ref[...], w_ref[:, off:off + p_ref.shape[1]], _NT)
        xv = x_ref[...]
        dx, dg = _rms_bwd(da, xv, _rms_stats(xv), g_ref[...])
        dg_ref[...] += jnp.sum(dg, axis=0, keepdims=True)
        dx_ref[...] = dx1_ref[...] + dx

    row = lambda w: pl.BlockSpec((tT, w), lambda i: (i, 0))
    vec = pl.BlockSpec((1, D_MODEL), lambda i: (0, 0))
    return _call(
        body, name="inproj_bwd", grid=(T // tT,),
        in_specs=[row(D_MODEL), row(D_MODEL), vec, _whole()] + [row(p.shape[1]) for p in dparts],
        out_specs=[row(D_MODEL), vec], out_shape=[_sds((T, D_MODEL), F32), _sds((1, D_MODEL), F32)],
        args=(x, dx1, g1, w_all, *dparts), job=job)


def _tn_matmul(a, b, name, job=None):
    T, M = a.shape
    N = b.shape[1]
    tk = _row_tile(T, 1024)
    tm = M if M <= 1024 else 1408
    tn = N if N <= 3072 else N // 2
    assert M % tm == 0 and N % tn == 0

    def body(a_ref, b_ref, o_ref):
        @pl.when(pl.program_id(2) == 0)
        def _():
            o_ref[...] = _dot(a_ref[...], b_ref[...], _TN)

        @pl.when(pl.program_id(2) > 0)
        def _():
            o_ref[...] += _dot(a_ref[...], b_ref[...], _TN)

    res, jres = _call(
        body, name=name, grid=(M // tm, N // tn, T // tk),
        in_specs=[pl.BlockSpec((tk, tm), lambda i, j, k: (k, i)), pl.BlockSpec((tk, tn), lambda i, j, k: (k, j))],
        out_specs=[pl.BlockSpec((tm, tn), lambda i, j, k: (i, j))], out_shape=[_sds((M, N), F32)], args=(a, b), job=job)
    return res[0], jres


def _pad_rows(a, rows=8):
    return jnp.pad(a, ((0, rows - a.shape[0]), (0, LANES - a.shape[1])))


def _halves_view(dw):
    r = dw.shape[0] // N_CHIPS
    return dw.reshape(N_CHIPS, 2, r // 2, dw.shape[1])


def kernel(x, norm_pre_mix, w_in, w_gate_up, b_gate, gla_norm, sgu_ln_g, sgu_ln_b, w_spatial, b_spatial, w_branch_gla, w_branch_sgu, w_out, norm_post_mix, norm_pre_ffn, w_ffn_in, w_ffn_out, norm_post_ffn, loss_target, m_norm_pre_mix, m_w_in, m_w_gate_up, m_b_gate, m_gla_norm, m_sgu_ln_g, m_sgu_ln_b, m_w_spatial, m_b_spatial, m_w_branch_gla, m_w_branch_sgu, m_w_out, m_norm_post_mix, m_norm_pre_ffn, m_w_ffn_in, m_w_ffn_out, m_norm_post_ffn, v_norm_pre_mix, v_w_in, v_w_gate_up, v_b_gate, v_gla_norm, v_sgu_ln_g, v_sgu_ln_b, v_w_spatial, v_b_spatial, v_w_branch_gla, v_w_branch_sgu, v_w_out, v_norm_post_mix, v_norm_pre_ffn, v_w_ffn_in, v_w_ffn_out, v_norm_post_ffn):
    chip = 2 * lax.axis_index("x") + lax.axis_index("y")
    xt, tgt = x[0], loss_target[0]

    tiny = jnp.concatenate([w_gate_up[0], _pad_rows(gla_norm[0]), _pad_rows(sgu_ln_g[0]), _pad_rows(sgu_ln_b[0]),
                            jnp.zeros((8, LANES), F32)], axis=0)
    def with_own(gathered, own):
        if gathered.ndim == 3:
            return lax.dynamic_update_slice(gathered, own[None], (chip, 0, 0))
        return lax.dynamic_update_slice(gathered, own, (0, chip * own.shape[1]))

    w_in_b = w_in[0].astype(BF16)
    g_in, g_tiny = _run_job(_job_gather([w_in_b, tiny], [False, False]), "gather_w_in")
    g_tiny = with_own(g_tiny, tiny)
    w_all = _relayout_w_in(with_own(g_in, w_in_b))
    cols = lambda a: a.transpose(1, 0, 2).reshape(a.shape[1], N_CHIPS * a.shape[2])
    wgu = jnp.pad(cols(g_tiny[:, 0:16]), ((0, LANES - GLA_RANK), (0, 0)))
    gn = cols(g_tiny[:, 16:20, :64]).reshape(1, GLA_V)
    ln_g = cols(g_tiny[:, 24:28, :64]).reshape(1, 1024)
    ln_b = cols(g_tiny[:, 32:36, :64]).reshape(1, 1024)
    b_sp_t = jnp.pad(b_spatial[0].T, ((0, 0), (0, LANES - SGU_GROUPS)))
    w_sp = w_spatial[0]

    own_rows = [w_branch_gla[0].astype(BF16), w_branch_sgu[0].astype(BF16), w_out[0].astype(BF16), w_ffn_out[0].astype(BF16)]
    (a, proj, alow), g_rows = _inproj_fwd(xt, norm_pre_mix, w_all, job=_job_gather(own_rows, [False] * 4))
    rows = lambda g: g.reshape(N_CHIPS * g.shape[1], g.shape[2])
    w_bg, w_bs, w_o, w_fo = [rows(with_own(g, own)) for g, own in zip(g_rows, own_rows)]
    w_fi_b = w_ffn_in[0].astype(BF16)
    fi_top, fi_bot = w_fi_b[:D_MODEL // 2], w_fi_b[D_MODEL // 2:]
    (y_gla, states), (g_top,) = _gla_fwd(proj, alow, wgu, b_gate, gn, job=_job_gather([fi_top], [False]))
    y_sgu = _sgu_fwd(proj, ln_g, ln_b, w_sp, b_sp_t)
    (zg, zs, merged, mix, x1), (g_bot,) = _merge_fwd(xt, proj, y_gla, y_sgu, w_bg, w_bs, w_o, norm_post_mix,
                                                     job=_job_gather([fi_bot], [False]))
    h, f, dgu, dy, dx1, loss, d_gpf, d_gpo = _ffn_fwd_bwd(x1, tgt, with_own(g_top, fi_top), with_own(g_bot, fi_bot),
                                                          w_fo, norm_pre_ffn, norm_post_ffn)

    own_part = lambda c: lax.dynamic_index_in_dim(c, chip, 0, keepdims=False)
    whole = lambda hs: [[(h_, None)] for h_ in hs]
    dw_fo, _ = _tn_matmul(f, dy, "dw_ffn_out")
    dw_fo4 = _halves_view(dw_fo)
    dw_fi, (q_fo,) = _tn_matmul(h, dgu, "dw_ffn_in", job=_job_to_other_core([[(dw_fo4, 0)]]))
    c_fo = _presum(dw_fo4, q_fo, "presum_ffn_out")
    (dmix, dzg, dzs, dp_mrg, dyg, dys, d_gpm), (s_fo, q_fi) = _merge_bwd(
        dx1, mix, proj, zg, zs, w_bg, w_bs, w_o, norm_post_mix,
        job=_join(_job_scatter([c_fo]), _job_to_other_core([[(dw_fi, 0)]])))
    c_fi = _presum(dw_fi, q_fi, "presum_ffn_in")
    dw_c, _ = _tn_matmul(a, dp_mrg, "dw_in_merge")
    dw_o4 = _halves_view(_tn_matmul(merged, dmix, "dw_out")[0])
    dw_bg4 = _halves_view(_tn_matmul(y_gla, dzg, "dw_branch_gla")[0])
    dw_bs4 = _halves_view(_tn_matmul(y_sgu, dzs, "dw_branch_sgu")[0])
    (dp_sgu, d_wsp, d_bsp_t, d_lng, d_lnb), (q_o, q_bg, q_bs, q_c) = _sgu_bwd(
        proj, dys, ln_g, ln_b, w_sp, b_sp_t,
        job=_job_to_other_core([[(dw_o4, 0)], [(dw_bg4, 0)], [(dw_bs4, 0)], [(dw_c, 0)]]))
    c_o, c_bg, c_bs = (_presum(dw_o4, q_o, "presum_out"), _presum(dw_bg4, q_bg, "presum_branch_gla"),
                       _presum(dw_bs4, q_bs, "presum_branch_sgu"))
    h_fo = _sum_slots(own_part(c_fo), s_fo, "sum_ffn_out")
    dw_b, _ = _tn_matmul(a, dp_sgu, "dw_in_sgu")
    (dp_gla, dal, d_gn, d_bg, d_wgu), (s_fi, t_fo, q_b) = _gla_bwd(
        proj, alow, wgu, b_gate, gn, states, dyg,
        job=_join(_job_scatter([c_fi]), _job_to_other_core(whole([h_fo]) + [[(dw_b, 0)]])))
    h_fi = _sum_slots(own_part(c_fi), s_fi, "sum_ffn_in")
    dw_d, _ = _tn_matmul(a, dal, "dw_in_gate")
    dw_a, (s_o, s_bg, s_bs, t_fi, q_d) = _tn_matmul(
        a, dp_gla, "dw_in_gla",
        job=_join(_job_scatter([c_o, c_bg, c_bs]), _job_to_other_core(whole([h_fi]) + [[(dw_d, 0)]])))
    h_o, h_bg, h_bs = (_sum_slots(own_part(c_o), s_o, "sum_out"), _sum_slots(own_part(c_bg), s_bg, "sum_branch_gla"),
                       _sum_slots(own_part(c_bs), s_bs, "sum_branch_sgu"))

    grads, deltas, new_m, new_v = {}, {}, {}, {}

    def update(name, w, m, v, g_mine, g_theirs, job=None):
        (g, d, m2, v2), jres = _adamw(w[0], m[0], v[0], g_mine, g_theirs, "adamw_" + name, job=job)
        grads[name], deltas[name], new_m[name], new_v[name] = g[None], d[None], m2[None], v2[None]
        return jres

    dw_in = [(dw_a, 0), (dw_b, W_GLA), (dw_c, W_GLA + W_SGU), (dw_d, N_MAIN)]
    q_a, t_o, t_bg, t_bs = update("w_ffn_out", w_ffn_out, m_w_ffn_out, v_w_ffn_out, [h_fo], [t_fo],
                                  job=_job_to_other_core([[(dw_a, 0)]] + whole([h_o, h_bg, h_bs])))
    q_in = [q_a, q_b, q_c, q_d]
    hr_in = D_MODEL // 2
    c_in_a = _presum_w_in(dw_in, q_in, 0, hr_in // 4, "presum_w_in_a")
    c_in_b = _presum_w_in(dw_in, q_in, hr_in // 4, 3 * hr_in // 4, "presum_w_in_b")
    (s_in_a,) = update("w_ffn_in", w_ffn_in, m_w_ffn_in, v_w_ffn_in, [h_fi], [t_fi], job=_job_scatter([c_in_a]))
    update("w_out", w_out, m_w_out, v_w_out, [h_o], [t_o])
    update("w_branch_gla", w_branch_gla, m_w_branch_gla, v_w_branch_gla, [h_bg], [t_bg])
    update("w_branch_sgu", w_branch_sgu, m_w_branch_sgu, v_w_branch_sgu, [h_bs], [t_bs])
    (grad_x, d_g1), (s_in_b,) = _inproj_bwd(xt, dx1, norm_pre_mix, w_all, (dp_gla, dp_sgu, dp_mrg, dal),
                                            job=_job_scatter([c_in_b]))
    h_in = [_sum_slots(own_part(c_in_a), s_in_a, "sum_w_in_a"), _sum_slots(own_part(c_in_b), s_in_b, "sum_w_in_b")]
    t_in = _run_job(_job_to_other_core(whole(h_in)), "swap_w_in")
    update("w_in", w_in, m_w_in, v_w_in, h_in, t_in)

    small_names = ["w_spatial", "w_gate_up", "norm_pre_mix", "norm_post_mix", "norm_pre_ffn", "norm_post_ffn", "b_gate",
                   "b_spatial", "gla_norm", "sgu_ln_g", "sgu_ln_b"]
    loss_out, small = _small_update(
        [d_wsp, d_wgu, d_g1, d_gpm, d_gpf, d_gpo, d_bg, d_bsp_t, d_gn, d_lng, d_lnb, loss],
        [w_spatial, w_gate_up, norm_pre_mix, norm_post_mix, norm_pre_ffn, norm_post_ffn, b_gate, b_spatial, gla_norm,
         sgu_ln_g, sgu_ln_b],
        [m_w_spatial, m_w_gate_up, m_norm_pre_mix, m_norm_post_mix, m_norm_pre_ffn, m_norm_post_ffn, m_b_gate,
         m_b_spatial, m_gla_norm, m_sgu_ln_g, m_sgu_ln_b],
        [v_w_spatial, v_w_gate_up, v_norm_pre_mix, v_norm_post_mix, v_norm_pre_ffn, v_norm_post_ffn, v_b_gate,
         v_b_spatial, v_gla_norm, v_sgu_ln_g, v_sgu_ln_b])
    for store, vals in zip((grads, deltas, new_m, new_v), small):
        store.update(zip(small_names, vals))

    order = ["norm_pre_mix", "w_in", "w_gate_up", "b_gate", "gla_norm", "sgu_ln_g", "sgu_ln_b", "w_spatial", "b_spatial",
             "w_branch_gla", "w_branch_sgu", "w_out", "norm_post_mix", "norm_pre_ffn", "w_ffn_in", "w_ffn_out",
             "norm_post_ffn"]
    out = [loss_out, grad_x[None]]
    for store in (grads, deltas, new_m, new_v):
        out.extend(store[n] for n in order)
    return tuple(out)
```

```python
import jax
import jax.numpy as jnp
from jax import lax
from jax.experimental import pallas as pl
from jax.experimental.pallas import tpu as pltpu

F32 = jnp.float32
BF16 = jnp.bfloat16

D_MODEL = 1024
GLA_HEADS = 4
GLA_DK = 128
GLA_DV = 256
GLA_QK = GLA_HEADS * GLA_DK
GLA_V = GLA_HEADS * GLA_DV
GLA_RANK = 16
GLA_TAU = 16.0
CHUNK = 64
SGU_GROUPS = 4
SGU_BLOCK = 128
SGU_DG = 256
D_FF = 2816
EPS = 1e-6
LANES = 128

OFF_Q, OFF_K, OFF_V, OFF_R, OFF_SU, OFF_SV, OFF_GG, OFF_GS, OFF_AL = 0, 512, 1024, 2048, 3072, 4096, 5120, 6144, 7168
W_GLA, W_SGU, W_MRG = 3072, 2048, 2048
N_MAIN = 7168
N_ALL = N_MAIN + LANES
_IN_SPLITS = (GLA_QK, GLA_QK, GLA_V, GLA_V, GLA_RANK, 1024, 1024, 1024, 1024)
_IN_STARTS = tuple(sum(_IN_SPLITS[:i]) for i in range(len(_IN_SPLITS) + 1))
_IN_DST = (OFF_Q, OFF_K, OFF_V, OFF_R, OFF_AL, OFF_SU, OFF_SV, OFF_GG, OFF_GS)
D_IN = _IN_STARTS[-1]

ADAM_LR = 0.001
ADAM_B1 = 0.9
ADAM_B2 = 0.999
ADAM_EPS = 1e-08
ADAM_WD = 0.01
ADAM_STEP = 10

VMEM_LIMIT_BYTES = 56 * 1024 * 1024
N_CHIPS = 4
N_PEER = N_CHIPS - 1
N_DEV = 8
MESH = pl.DeviceIdType.MESH

_NN = (((1,), (0,)), ((), ()))
_NT = (((1,), (1,)), ((), ()))
_TN = (((0,), (0,)), ((), ()))


def _dot(a, b, dims=_NN):
    return lax.dot_general(a, b, dims, preferred_element_type=F32)


def _split(x):
    hi = x.astype(BF16)
    lo = (x - hi.astype(F32)).astype(BF16)
    return hi, lo


def _dot_f32(a, b, dims=_NN):
    ah, al = _split(a)
    bh, bl = _split(b)
    return _dot(ah, bh, dims) + (_dot(al, bh, dims) + _dot(ah, bl, dims))


def _dot_exact_lhs(m, x):
    xh, xl = _split(x)
    return _dot(m, xh) + _dot(m, xl)


def _sigmoid(x):
    return 1.0 / (1.0 + jnp.exp(-x))


def _log_sigmoid(x):
    return jnp.minimum(x, 0.0) - jnp.log(1.0 + jnp.exp(-jnp.abs(x)))


_GELU_C = 0.7978845608028654
_GELU_A = 0.044715


def _gelu_and_grad(x):
    x2 = x * x
    t = jnp.tanh(_GELU_C * (x + _GELU_A * x * x2))
    g = 0.5 * x * (1.0 + t)
    dg = 0.5 * (1.0 + t) + 0.5 * x * (1.0 - t * t) * (_GELU_C * (1.0 + 3.0 * _GELU_A * x2))
    return g, dg


def _gelu(x):
    t = jnp.tanh(_GELU_C * (x + _GELU_A * x * x * x))
    return 0.5 * x * (1.0 + t)


def _rms_stats(x):
    return lax.rsqrt(jnp.mean(x * x, axis=-1, keepdims=True) + EPS)


def _rms_bwd(dout, y, r, g):
    yhat = y * r
    dn = dout * g
    dy = r * (dn - yhat * jnp.mean(dn * yhat, axis=-1, keepdims=True))
    return dy, dout * yhat


def _whole():
    return pl.BlockSpec(memory_space=pltpu.VMEM)


def _row_tile(T, want):
    t = min(T, want)
    assert T % t == 0
    return t


def _chunk_masks(tT, upper):
    row = lax.broadcasted_iota(jnp.int32, (tT, tT), 0)
    col = lax.broadcasted_iota(jnp.int32, (tT, tT), 1)
    same = lax.shift_right_logical(row, 6) == lax.shift_right_logical(col, 6)
    tri = (col > row) if upper else (col < row)
    return jnp.where(same & tri, 1.0, 0.0).astype(BF16)


class _Job:
    def __init__(self, ins, out_shapes, scratch, start, finish):
        self.ins, self.out_shapes, self.scratch, self.start, self.finish = list(ins), list(out_shapes), list(scratch), start, finish


def _join(*jobs):
    def split(refs, counts):
        out, at = [], 0
        for n in counts:
            out.append(refs[at:at + n])
            at += n
        return out

    ni, no, ns = [len(j.ins) for j in jobs], [len(j.out_shapes) for j in jobs], [len(j.scratch) for j in jobs]

    def start(ins, outs, scr):
        for j, a, b, c in zip(jobs, split(ins, ni), split(outs, no), split(scr, ns)):
            j.start(a, b, c)

    def finish(ins, outs, scr):
        for j, a, b, c in zip(jobs, split(ins, ni), split(outs, no), split(scr, ns)):
            j.finish(a, b, c)

    return _Job(sum((j.ins for j in jobs), []), sum((j.out_shapes for j in jobs), []),
                sum((j.scratch for j in jobs), []), start, finish)


def _mesh_pos():
    return lax.axis_index("x"), lax.axis_index("y"), lax.axis_index("c")


def _peer_chips(xi, yi):
    return [(1 - xi, yi), (xi, 1 - yi), (1 - xi, 1 - yi)]


def _half(ci, rows):
    return pl.ds(pl.multiple_of(ci * rows, 8), rows)


def _sds(shape, dtype):
    return jax.ShapeDtypeStruct(tuple(shape), dtype)


def _job_gather(arrs, by_cols):
    n = len(arrs)

    def dst(o, k, chip, rows):
        if by_cols[k]:
            c = arrs[k].shape[1]
            return o.at[rows, pl.ds(pl.multiple_of(chip * c, LANES), c)]
        return o.at[chip, rows]

    def copies(ins, outs, scr, want):
        ici_send, ici_recv, d2d_send, d2d_recv = scr
        xi, yi, ci = _mesh_pos()
        me = 2 * xi + yi
        res = []
        for k in range(n):
            r = arrs[k].shape[0]
            mine, other = _half(ci, r // 2), _half(1 - ci, r // 2)
            for j, (px, py) in enumerate(_peer_chips(xi, yi)):
                s = k * N_PEER + j
                pc = 2 * px + py
                ici = dict(send_sem=ici_send.at[s], recv_sem=ici_recv.at[s], device_id=(px, py, ci), device_id_type=MESH)
                d2d = dict(send_sem=d2d_send.at[s], recv_sem=d2d_recv.at[s], device_id=(xi, yi, 1 - ci),
                           device_id_type=MESH)
                made = {}
                if "send" in want:
                    made["send"] = pltpu.make_async_remote_copy(
                        src_ref=ins[k].at[mine], dst_ref=dst(outs[k], k, me, mine), **ici)
                if "arrive" in want:
                    made["arrive"] = pltpu.make_async_remote_copy(
                        src_ref=ins[k].at[mine], dst_ref=dst(outs[k], k, pc, mine), **ici)
                if "forward" in want:
                    made["forward"] = pltpu.make_async_remote_copy(
                        src_ref=dst(outs[k], k, pc, mine), dst_ref=dst(outs[k], k, pc, mine), **d2d)
                if "handed" in want:
                    made["handed"] = pltpu.make_async_remote_copy(
                        src_ref=dst(outs[k], k, pc, other), dst_ref=dst(outs[k], k, pc, other), **d2d)
                res.append(made)
        return res

    def start(ins, outs, scr):
        for cp in copies(ins, outs, scr, ("send",)):
            cp["send"].start()

    def finish(ins, outs, scr):
        for cp in copies(ins, outs, scr, ("arrive", "forward")):
            cp["arrive"].wait_recv()
            cp["forward"].start()
        for cp in copies(ins, outs, scr, ("handed", "send", "forward")):
            cp["handed"].wait_recv()
            cp["send"].wait_send()
            cp["forward"].wait_send()

    shapes = [_sds((a.shape[0], N_CHIPS * a.shape[1]) if bc else (N_CHIPS,) + a.shape, a.dtype)
              for a, bc in zip(arrs, by_cols)]
    dma = pltpu.SemaphoreType.DMA
    return _Job(arrs, shapes, [dma((n * N_PEER,))] * 4, start, finish)


def _job_scatter(parts):
    n = len(parts)

    def copies(ins, outs, scr):
        send_sems, recv_sems = scr
        xi, yi, ci = _mesh_pos()
        res = []
        for k in range(n):
            for j, (px, py) in enumerate(_peer_chips(xi, yi)):
                s = k * N_PEER + j
                res.append(pltpu.make_async_remote_copy(
                    src_ref=ins[k].at[2 * px + py], dst_ref=outs[k].at[j], send_sem=send_sems.at[s],
                    recv_sem=recv_sems.at[s], device_id=(px, py, ci), device_id_type=MESH))
        return res

    def start(ins, outs, scr):
        for cp in copies(ins, outs, scr):
            cp.start()

    def finish(ins, outs, scr):
        for cp in copies(ins, outs, scr):
            cp.wait_recv()
            cp.wait_send()

    dma = pltpu.SemaphoreType.DMA
    return _Job(parts, [_sds((N_PEER,) + p.shape[1:], p.dtype) for p in parts], [dma((n * N_PEER,))] * 2, start, finish)


def _job_to_other_core(groups):
    pieces = [(g, a, off) for g, group in enumerate(groups) for a, off in group]
    n = len(pieces)

    def geometry(group):
        a0, off0 = group[0]
        if off0 is None:
            return a0.shape
        if a0.ndim == 4:
            return (N_CHIPS, a0.shape[2], a0.shape[3])
        return (a0.shape[0] // 2, sum(a.shape[1] for a, _ in group))

    def copies(ins, outs, scr):
        send_sems, recv_sems = scr
        xi, yi, ci = _mesh_pos()
        res = []
        for p, (g, a, off) in enumerate(pieces):
            if off is None:
                give, land = ins[p], outs[g]
            elif a.ndim == 4:
                give, land = ins[p].at[pl.ds(0, N_CHIPS), 1 - ci], outs[g]
            else:
                hr, w = a.shape[0] // 2, a.shape[1]
                give, land = ins[p].at[_half(1 - ci, hr)], outs[g].at[pl.ds(0, hr), pl.ds(off, w)]
            res.append(pltpu.make_async_remote_copy(
                src_ref=give, dst_ref=land, send_sem=send_sems.at[p], recv_sem=recv_sems.at[p],
                device_id=(xi, yi, 1 - ci), device_id_type=MESH))
        return res

    def start(ins, outs, scr):
        for cp in copies(ins, outs, scr):
            cp.start()

    def finish(ins, outs, scr):
        for cp in copies(ins, outs, scr):
            cp.wait_recv()
            cp.wait_send()

    dma = pltpu.SemaphoreType.DMA
    return _Job([a for _, a, _ in pieces], [_sds(geometry(group), group[0][0].dtype) for group in groups],
                [dma((n,))] * 2, start, finish)


def _call(body, *, name, grid, in_specs, out_specs, out_shape, args, scratch_shapes=(), parallel=False, job=None):
    n_in, n_out, n_scr = len(in_specs), len(out_specs), len(scratch_shapes)

    def params(sem):
        return pltpu.CompilerParams(dimension_semantics=sem, vmem_limit_bytes=VMEM_LIMIT_BYTES)

    if job is None:
        sem = ("parallel" if parallel else "arbitrary",) * len(grid)
        res = pl.pallas_call(
            body, name=name, grid=grid, in_specs=in_specs, out_specs=out_specs, out_shape=out_shape,
            scratch_shapes=list(scratch_shapes), compiler_params=params(sem))(*args)
        return list(res), []
    n_ji, n_jo = len(job.ins), len(job.out_shapes)

    def carried(*refs):
        ins, refs = refs[:n_in], refs[n_in:]
        j_ins, refs = refs[:n_ji], refs[n_ji:]
        outs, refs = refs[:n_out], refs[n_out:]
        j_outs, refs = refs[:n_jo], refs[n_jo:]
        scr, j_scr = refs[:n_scr], refs[n_scr:]
        ids = [pl.program_id(d) for d in range(len(grid))]
        first = ids[0] == 0
        last = ids[0] == grid[0] - 1
        for d in range(1, len(grid)):
            first = first & (ids[d] == 0)
            last = last & (ids[d] == grid[d] - 1)

        @pl.when(first)
        def _():
            job.start(j_ins, j_outs, j_scr)

        body(*ins, *outs, *scr)

        @pl.when(last)
        def _():
            job.finish(j_ins, j_outs, j_scr)

    hbm = pl.BlockSpec(memory_space=pl.ANY)
    res = pl.pallas_call(
        carried, name=name, grid=grid, in_specs=list(in_specs) + [hbm] * n_ji, out_specs=list(out_specs) + [hbm] * n_jo,
        out_shape=list(out_shape) + job.out_shapes, scratch_shapes=list(scratch_shapes) + job.scratch,
        compiler_params=params(("arbitrary",) * len(grid)))(*args, *job.ins)
    return list(res[:n_out]), list(res[n_out:])


def _run_job(job, name):
    n_i, n_o = len(job.ins), len(job.out_shapes)

    def body(*refs):
        ins, outs, scr = refs[:n_i], refs[n_i:n_i + n_o], refs[n_i + n_o:]
        job.start(ins, outs, scr)
        job.finish(ins, outs, scr)

    hbm = pl.BlockSpec(memory_space=pl.ANY)
    return list(pl.pallas_call(body, name=name, in_specs=[hbm] * n_i, out_specs=[hbm] * n_o, out_shape=job.out_shapes,
                               scratch_shapes=job.scratch)(*job.ins))


def _adam_values(w, m, v, g):
    m2 = ADAM_B1 * m + (1.0 - ADAM_B1) * g
    v2 = ADAM_B2 * v + (1.0 - ADAM_B2) * (g * g)
    delta = -ADAM_LR * ((m2 / (1.0 - ADAM_B1 ** ADAM_STEP)) / (jnp.sqrt(v2 / (1.0 - ADAM_B2 ** ADAM_STEP)) + ADAM_EPS)
                        + ADAM_WD * w)
    return delta, m2, v2


_P_WSP, _P_WGU, _P_NORM, _P_BG, _P_BSP, _P_HEAD, _P_LOSS, _P_ROWS = 0, 512, 576, 608, 616, 624, 720, 728


def _small_sum(dgrads):
    def body(dwsp, dwgu, dg1, dgpm, dgpf, dgpo, dbg, dbspt, dgn, dlng, dlnb, loss_in, tot, pack, slots, send_sems,
             recv_sems):
        xi, yi, ci = _mesh_pos()
        chip = 2 * xi + yi

        pack[...] = jnp.zeros_like(pack)
        for g in range(SGU_GROUPS):
            pack[_P_WSP + g * SGU_BLOCK:_P_WSP + (g + 1) * SGU_BLOCK] = dwsp[g]
        for j in range(N_CHIPS):
            pack[_P_WGU + GLA_RANK * j:_P_WGU + GLA_RANK * (j + 1)] = dwgu[0:GLA_RANK, LANES * j:LANES * (j + 1)]
        for k, r in enumerate((dg1, dgpm, dgpf, dgpo)):
            for q in range(8):
                pack[_P_NORM + 8 * k + q:_P_NORM + 8 * k + q + 1] = r[:, LANES * q:LANES * (q + 1)]
        for q in range(4):
            pack[_P_BG + q:_P_BG + q + 1] = dbg[:, LANES * q:LANES * (q + 1)]
        pack[_P_BSP:_P_BSP + SGU_GROUPS] = jnp.transpose(dbspt[...])[0:SGU_GROUPS]
        for k, r in enumerate((dgn, dlng, dlnb)):
            for j in range(N_CHIPS):
                for hh in range(4):
                    row = _P_HEAD + 32 * k + 8 * j + hh
                    pack[row:row + 1, 0:64] = r[:, 256 * hh + 64 * j:256 * hh + 64 * (j + 1)]
        pack[_P_LOSS:_P_LOSS + 1] = loss_in[...]

        to_sibling = pltpu.make_async_remote_copy(
            src_ref=pack, dst_ref=tot, send_sem=send_sems.at[N_PEER], recv_sem=recv_sems.at[N_PEER],
            device_id=(xi, yi, 1 - ci), device_id_type=MESH)
        to_sibling.start()
        to_sibling.wait_recv()
        to_sibling.wait_send()
        pack[...] = pack[...] + tot[...]
        slots[chip] = pack[...]

        def copy(j, slot):
            px, py = _peer_chips(xi, yi)[j]
            return pltpu.make_async_remote_copy(
                src_ref=pack, dst_ref=slots.at[slot(2 * px + py)], send_sem=send_sems.at[j], recv_sem=recv_sems.at[j],
                device_id=(px, py, ci), device_id_type=MESH)

        sends = [copy(j, lambda peer_chip: chip) for j in range(N_PEER)]
        for cp in sends:
            cp.start()
        for j in range(N_PEER):
            copy(j, lambda peer_chip: peer_chip).wait_recv()
        for cp in sends:
            cp.wait_send()
        acc = slots[0]
        for d in range(1, N_CHIPS):
            acc = acc + slots[d]
        tot[...] = acc

    return pl.pallas_call(
        body, name="small_sum", in_specs=[_whole()] * 12, out_specs=_whole(), out_shape=_sds((_P_ROWS, LANES), F32),
        scratch_shapes=[pltpu.VMEM((_P_ROWS, LANES), F32), pltpu.VMEM((N_CHIPS, _P_ROWS, LANES), F32),
                        pltpu.SemaphoreType.DMA((N_PEER + 1,)), pltpu.SemaphoreType.DMA((N_PEER + 1,))],
        compiler_params=pltpu.CompilerParams(vmem_limit_bytes=VMEM_LIMIT_BYTES),
    )(*dgrads)


def _small_adamw(tot, ws, ms, vs):
    n = len(ws)

    def body(*refs):
        tot = refs[0]
        w_refs, m_refs, v_refs = refs[1:1 + n], refs[1 + n:1 + 2 * n], refs[1 + 2 * n:1 + 3 * n]
        loss_out = refs[1 + 3 * n]
        outs = refs[2 + 3 * n:]
        chip = 2 * lax.axis_index("x") + lax.axis_index("y")
        loss_out[...] = tot[_P_LOSS:_P_LOSS + 1, 0:1]

        def step(k, g, pick, put):
            d, m2, v2 = _adam_values(pick(w_refs[k]), pick(m_refs[k]), pick(v_refs[k]), g)
            for o, val in zip((outs[k], outs[n + k], outs[2 * n + k], outs[3 * n + k]), (g, d, m2, v2)):
                put(o, val)

        def whole(ref):
            return ref[0]

        def put_whole(ref, val):
            ref[0] = val

        for g in range(SGU_GROUPS):
            def pick_g(ref, g=g):
                return ref[0, g]

            def put_g(ref, val, g=g):
                ref[0, g] = val

            step(0, tot[_P_WSP + g * SGU_BLOCK:_P_WSP + (g + 1) * SGU_BLOCK], pick_g, put_g)
        step(1, tot[pl.ds(pl.multiple_of(_P_WGU + GLA_RANK * chip, GLA_RANK), GLA_RANK), :], whole, put_whole)
        for k, (base, chunks) in enumerate(((_P_NORM, 8), (_P_NORM + 8, 8), (_P_NORM + 16, 8), (_P_NORM + 24, 8), (_P_BG, 4))):
            for q in range(chunks):
                def pick_q(ref, q=q):
                    return ref[:, LANES * q:LANES * (q + 1)]

                def put_q(ref, val, q=q):
                    ref[:, LANES * q:LANES * (q + 1)] = val

                step(2 + k, tot[base + q:base + q + 1], pick_q, put_q)
        step(7, tot[_P_BSP:_P_BSP + SGU_GROUPS], whole, put_whole)
        for k in range(3):
            mine = tot[pl.ds(pl.multiple_of(_P_HEAD + 32 * k + 8 * chip, 8), 8), :]
            step(8 + k, mine[0:4, 0:64], whole, put_whole)

    shapes = [_sds(w.shape, F32) for w in ws]
    res = pl.pallas_call(
        body, name="small_adamw", in_specs=[_whole()] * (1 + 3 * n), out_specs=[_whole()] * (1 + 4 * n),
        out_shape=[_sds((1, 1), F32)] + shapes * 4,
        compiler_params=pltpu.CompilerParams(vmem_limit_bytes=VMEM_LIMIT_BYTES),
    )(tot, *ws, *ms, *vs)
    return res[0].reshape(()), [list(res[1 + i * n:1 + (i + 1) * n]) for i in range(4)]


def _w_in_pieces():
    blk = D_IN // N_CHIPS
    pieces = []
    for s in range(len(_IN_SPLITS)):
        lo_s, hi_s = _IN_STARTS[s], _IN_STARTS[s + 1]
        for j in range(N_CHIPS):
            lo, hi = max(lo_s, j * blk), min(hi_s, (j + 1) * blk)
            if lo < hi:
                pieces.append((j, lo - j * blk, _IN_DST[s] + lo - lo_s, hi - lo))
    return pieces


def _relayout_w_in(gathered):
    _, rows, blk = gathered.shape
    tr = 256

    def body(g_ref, o_ref):
        o_ref[:, OFF_AL:N_ALL] = jnp.zeros((tr, LANES), BF16)
        for j, src, dst, w in _w_in_pieces():
            o_ref[:, dst:dst + w] = g_ref[j, :, src:src + w]

    res, _ = _call(body, name="relayout_w_in", grid=(rows // tr,), parallel=True,
                   in_specs=[pl.BlockSpec((N_CHIPS, tr, blk), lambda i: (0, i, 0))],
                   out_specs=[pl.BlockSpec((tr, N_ALL), lambda i: (i, 0))],
                   out_shape=[_sds((rows, N_ALL), BF16)], args=(gathered,))
    return res[0]


def _update_row_tile(rows):
    for t in range(min(rows, 256), 7, -8):
        if rows % t == 0:
            return t
    return rows


def _my_half(first_ref, second_ref):
    return jnp.where(lax.axis_index("c") == 0, first_ref[...], second_ref[...])


def _presum_w_in(dws, theirs, row0, rows, name, job=None):
    hr = theirs[0].shape[0]
    blk = D_IN // N_CHIPS
    tr = 64
    assert row0 % tr == 0 and rows % tr == 0
    nh, t0 = hr // tr, row0 // tr
    n = len(dws)

    def body(*refs):
        dw_refs, q_refs, (o_ref, s_scr) = refs[:2 * n], refs[2 * n:3 * n], refs[3 * n:]
        for p, (a, off) in enumerate(dws):
            w = a.shape[1]
            s_scr[:, off:off + w] = (_my_half(dw_refs[2 * p], dw_refs[2 * p + 1]) + q_refs[p][...]).astype(BF16)
        for j, src, dst, w in _w_in_pieces():
            o_ref[j, :, src:src + w] = s_scr[:, dst:dst + w]

    in_specs, args = [], []
    for a, _ in dws:
        w = a.shape[1]
        in_specs += [pl.BlockSpec((tr, w), lambda i: (i + t0, 0)), pl.BlockSpec((tr, w), lambda i: (i + t0 + nh, 0))]
        args += [a, a]
    in_specs += [pl.BlockSpec((tr, q.shape[1]), lambda i: (i + t0, 0)) for q in theirs]
    res, jres = _call(body, name=name, grid=(rows // tr,), parallel=True, in_specs=in_specs,
                      out_specs=[pl.BlockSpec((N_CHIPS, tr, blk), lambda i: (0, i, 0))],
                      out_shape=[_sds((N_CHIPS, rows, blk), BF16)], scratch_shapes=[pltpu.VMEM((tr, N_ALL), BF16)],
                      args=(*args, *theirs), job=job)
    return res[0], jres


def _presum(dw, theirs, name):
    if dw.ndim == 4:
        _, _, hr, c = dw.shape
        tr = _update_row_tile(hr)
        first = pl.BlockSpec((1, 1, tr, c), lambda j, i: (j, 0, i, 0))
        second = pl.BlockSpec((1, 1, tr, c), lambda j, i: (j, 1, i, 0))
        other = pl.BlockSpec((1, tr, c), lambda j, i: (j, i, 0))
    else:
        hr, c = dw.shape[0] // 2, dw.shape[1] // N_CHIPS
        tr = _update_row_tile(hr)
        nh = hr // tr
        first = pl.BlockSpec((tr, c), lambda j, i: (i, j))
        second = pl.BlockSpec((tr, c), lambda j, i: (i + nh, j))
        other = pl.BlockSpec((tr, c), lambda j, i: (i, j))

    def body(a_ref, b_ref, q_ref, o_ref):
        mine = _my_half(a_ref, b_ref).reshape(tr, c)
        o_ref[...] = (mine + q_ref[...].reshape(tr, c)).astype(BF16).reshape(o_ref.shape)

    res, _ = _call(body, name=name, grid=(N_CHIPS, hr // tr), parallel=True, in_specs=[first, second, other],
                   out_specs=[pl.BlockSpec((1, tr, c), lambda j, i: (j, i, 0))],
                   out_shape=[_sds((N_CHIPS, hr, c), BF16)], args=(dw, dw, theirs))
    return res[0]


def _sum_slots(own, slots, name):
    rows, cols = own.shape
    tr = _update_row_tile(rows)

    def body(own_ref, s_ref, o_ref):
        acc = own_ref[...].astype(F32)
        for j in range(N_PEER):
            acc = acc + s_ref[j].astype(F32)
        o_ref[...] = acc

    res, _ = _call(body, name=name, grid=(rows // tr,), parallel=True,
                   in_specs=[pl.BlockSpec((tr, cols), lambda i: (i, 0)), pl.BlockSpec((N_PEER, tr, cols), lambda i: (0, i, 0))],
                   out_specs=[pl.BlockSpec((tr, cols), lambda i: (i, 0))], out_shape=[_sds((rows, cols), F32)],
                   args=(own, slots))
    return res[0]


def _adamw(w, m, v, g_mine, g_theirs, name, job=None):
    rows, cols = w.shape
    part_rows = [p.shape[0] for p in g_mine]
    assert sum(part_rows) == rows // 2 and [p.shape[0] for p in g_theirs] == part_rows
    tr = _update_row_tile(min(part_rows))
    assert all(r % tr == 0 for r in part_rows)
    nh = (rows // 2) // tr
    starts = [sum(part_rows[:k]) // tr for k in range(len(part_rows))]
    n_parts = len(part_rows)

    def body(w_ref, m_ref, v_ref, *rest):
        g_refs, (g_out, d_out, m_out, v_out) = rest[:-4], rest[-4:]
        step = pl.program_id(0)
        mine_here = (step // nh) == lax.axis_index("c")
        q = step % nh
        g = None
        for k in reversed(range(n_parts)):
            val = jnp.where(mine_here, g_refs[k][...], g_refs[n_parts + k][...])
            g = val if g is None else jnp.where(q < starts[k + 1], val, g)
        d, m2, v2 = _adam_values(w_ref[...], m_ref[...], v_ref[...], g)
        g_out[...] = g
        m_out[...] = m2
        v_out[...] = v2
        d_out[...] = d

    def g_spec(k):
        last = part_rows[k] // tr - 1
        return pl.BlockSpec((tr, cols), lambda i: (jnp.clip(i % nh - starts[k], 0, last), 0))

    spec = pl.BlockSpec((tr, cols), lambda i: (i, 0))
    return _call(body, name=name, grid=(rows // tr,), parallel=True,
                 in_specs=[spec] * 3 + [g_spec(k) for k in range(n_parts)] * 2, out_specs=[spec] * 4,
                 out_shape=[_sds((rows, cols), F32)] * 4, args=(w, m, v, *g_mine, *g_theirs), job=job)


def _inproj_fwd(x, g1, w_all, job=None):
    T = x.shape[0]
    tT = _row_tile(T, 512)

    def body(x_ref, g_ref, w_ref, a_ref, proj_ref, alow_ref):
        xv = x_ref[...]
        a = (xv * _rms_stats(xv) * g_ref[...]).astype(BF16)
        a_ref[...] = a
        for j in range(N_MAIN // 1024):
            cols = slice(j * 1024, (j + 1) * 1024)
            proj_ref[:, cols] = _dot(a, w_ref[:, cols]).astype(BF16)
        alow_ref[...] = _dot(a, w_ref[:, N_MAIN:N_ALL])

    row = lambda w: pl.BlockSpec((tT, w), lambda i: (i, 0))
    return _call(
        body, name="inproj_fwd", grid=(T // tT,), parallel=True,
        in_specs=[row(D_MODEL), pl.BlockSpec((1, D_MODEL), lambda i: (0, 0)), _whole()],
        out_specs=[row(D_MODEL), row(N_MAIN), row(LANES)],
        out_shape=[_sds((T, D_MODEL), BF16), _sds((T, N_MAIN), BF16), _sds((T, LANES), F32)],
        args=(x, g1, w_all), job=job)


def _gla_decay_terms(al_ref, wgu_ref, bg_ref, tT):
    logit = _dot_f32(al_ref[...], wgu_ref[...]) + bg_ref[...]
    la = _log_sigmoid(logit) * (1.0 / GLA_TAU)
    delta = _dot_exact_lhs(_chunk_masks(tT, upper=True), la)
    return logit, la, delta


def _gla_fwd(proj, alow, wgu, b_gate, gn, job=None):
    T = proj.shape[0]
    tT = _row_tile(T, 512)
    nc = tT // CHUNK

    def body(q_ref, k_ref, v_ref, r_ref, al_ref, wgu_ref, bg_ref, gn_ref, y_ref, st_ref, s_scr):
        @pl.when(pl.program_id(0) == 0)
        def _():
            s_scr[...] = jnp.zeros_like(s_scr)

        _, la, delta = _gla_decay_terms(al_ref, wgu_ref, bg_ref, tT)
        kdec = (k_ref[...].astype(F32) * jnp.exp(delta)).astype(BF16)
        heads = range(GLA_HEADS)
        kcs = [slice(h * GLA_DK, (h + 1) * GLA_DK) for h in heads]
        vcs = [slice(h * GLA_DV, (h + 1) * GLA_DV) for h in heads]
        state = [s_scr[h] for h in heads]
        for c in range(nc):
            rows = slice(c * CHUNK, (c + 1) * CHUNK)
            first = slice(c * CHUNK, c * CHUNK + 1)
            dec = jnp.exp(la[first, :] + delta[first, :])
            upd_t = [_dot(v_ref[rows, vcs[h]], kdec[rows, kcs[h]], _TN) for h in heads]
            qs = [(q_ref[rows, kcs[h]].astype(F32) * (GLA_DK ** -0.5)).astype(BF16) for h in heads]
            for h in heads:
                state[h] = state[h] * dec[:, kcs[h]] + upd_t[h]
                st_ref[c, h] = state[h]
            o = [_dot(qs[h], state[h].astype(BF16), _NT) for h in heads]
            for h in heads:
                on = o[h] * _rms_stats(o[h]) * gn_ref[:, vcs[h]]
                rr = r_ref[rows, vcs[h]].astype(F32)
                y_ref[rows, vcs[h]] = (on * (rr * _sigmoid(rr))).astype(BF16)
        for h in heads:
            s_scr[h] = state[h]

    blk = lambda w, j: pl.BlockSpec((tT, w), lambda i: (i, j))
    return _call(
        body, name="gla_fwd", grid=(T // tT,),
        in_specs=[blk(512, 0), blk(512, 1), blk(1024, 1), blk(1024, 2), blk(LANES, 0), _whole(), _whole(), _whole()],
        out_specs=[pl.BlockSpec((tT, GLA_V), lambda i: (i, 0)),
                   pl.BlockSpec((nc, GLA_HEADS, GLA_DV, GLA_DK), lambda i: (i, 0, 0, 0))],
        out_shape=[_sds((T, GLA_V), BF16), _sds((T // CHUNK, GLA_HEADS, GLA_DV, GLA_DK), F32)],
        scratch_shapes=[pltpu.VMEM((GLA_HEADS, GLA_DV, GLA_DK), F32)],
        args=(proj, proj, proj, proj, alow, wgu, b_gate, gn), job=job)


def _sgu_mask():
    i = lax.broadcasted_iota(jnp.int32, (SGU_BLOCK, SGU_BLOCK), 0)
    j = lax.broadcasted_iota(jnp.int32, (SGU_BLOCK, SGU_BLOCK), 1)
    return lax.shift_right_logical(j, 6) <= lax.shift_right_logical(i, 6)


def _sgu_fwd(proj, ln_g, ln_b, w_sp, b_sp_t):
    T = proj.shape[0]
    tT = _row_tile(T, 512)
    nb = tT // SGU_BLOCK

    def body(su_ref, sv_ref, lg_ref, lb_ref, w_ref, b_ref, y_ref):
        mask = _sgu_mask()
        for g in range(SGU_GROUPS):
            gc = slice(g * SGU_DG, (g + 1) * SGU_DG)
            wm = jnp.where(mask, w_ref[g], 0.0).astype(BF16)
            vf = _gelu(sv_ref[:, gc].astype(F32))
            mu = jnp.mean(vf, axis=-1, keepdims=True)
            vc = vf - mu
            rstd = lax.rsqrt(jnp.mean(vc * vc, axis=-1, keepdims=True) + EPS)
            vn = (vc * rstd * lg_ref[:, gc] + lb_ref[:, gc]).astype(BF16)
            u = _gelu(su_ref[:, gc].astype(F32))
            for b in range(nb):
                rows = slice(b * SGU_BLOCK, (b + 1) * SGU_BLOCK)
                mixed = _dot(wm, vn[rows, :]) + b_ref[:, g:g + 1]
                y_ref[rows, gc] = (u[rows, :] * mixed).astype(BF16)

    blk = lambda j: pl.BlockSpec((tT, 1024), lambda i: (i, j))
    res, _ = _call(body, name="sgu_fwd", grid=(T // tT,), parallel=True,
                   in_specs=[blk(3), blk(4), _whole(), _whole(), _whole(), _whole()],
                   out_specs=[pl.BlockSpec((tT, 1024), lambda i: (i, 0))], out_shape=[_sds((T, 1024), BF16)],
                   args=(proj, proj, ln_g, ln_b, w_sp, b_sp_t))
    return res[0]


def _merge_fwd(x, proj, y_gla, y_sgu, w_bg, w_bs, w_o, g_pm, job=None):
    T = x.shape[0]
    tT = _row_tile(T, 512)

    def body(x_ref, gg_ref, gs_ref, yg_ref, ys_ref, wbg_ref, wbs_ref, wo_ref, g_ref,
             zg_ref, zs_ref, mg_ref, mix_ref, x1_ref):
        zg = _dot(yg_ref[...], wbg_ref[...])
        zs = _dot(ys_ref[...], wbs_ref[...])
        zg_ref[...] = zg.astype(BF16)
        zs_ref[...] = zs.astype(BF16)
        merged = (_sigmoid(gg_ref[...].astype(F32)) * zg + _sigmoid(gs_ref[...].astype(F32)) * zs).astype(BF16)
        mg_ref[...] = merged
        mix = _dot(merged, wo_ref[...])
        mix_ref[...] = mix
        x1_ref[...] = x_ref[...] + mix * _rms_stats(mix) * g_ref[...]

    row = pl.BlockSpec((tT, D_MODEL), lambda i: (i, 0))
    blk = lambda j: pl.BlockSpec((tT, 1024), lambda i: (i, j))
    sds = lambda dt: _sds((T, D_MODEL), dt)
    return _call(body, name="merge_fwd", grid=(T // tT,), parallel=True,
                 in_specs=[row, blk(5), blk(6), row, row, _whole(), _whole(), _whole(),
                           pl.BlockSpec((1, D_MODEL), lambda i: (0, 0))],
                 out_specs=[row] * 5, out_shape=[sds(BF16), sds(BF16), sds(BF16), sds(F32), sds(F32)],
                 args=(x, proj, proj, y_gla, y_sgu, w_bg, w_bs, w_o, g_pm), job=job)


def _ffn_fwd_bwd(x1, tgt, w_fi_top, w_fi_bot, w_fo, g_pf, g_po):
    T = x1.shape[0]
    tT = _row_tile(T, 256)
    half = D_FF // 2
    kh = D_MODEL // 2

    def body(x1_ref, t_ref, top_ref, bot_ref, wfo_ref, gpf_ref, gpo_ref,
             h_ref, f_ref, dgu_ref, dy_ref, dx1_ref, loss_ref, dgpf_ref, dgpo_ref, gu_scr):
        @pl.when(pl.program_id(0) == 0)
        def _():
            loss_ref[...] = jnp.zeros_like(loss_ref)
            dgpf_ref[...] = jnp.zeros_like(dgpf_ref)
            dgpo_ref[...] = jnp.zeros_like(dgpo_ref)

        x1v = x1_ref[...]
        r2 = _rms_stats(x1v)
        h = (x1v * r2 * gpf_ref[...]).astype(BF16)
        h_ref[...] = h
        y = jnp.zeros((tT, D_MODEL), F32)
        for j in range(2):
            gc = slice(j * half, (j + 1) * half)
            uc = slice(D_FF + j * half, D_FF + (j + 1) * half)
            gate = _dot(h[:, :kh], top_ref[j]) + _dot(h[:, kh:], bot_ref[j])
            up = _dot(h[:, :kh], top_ref[2 + j]) + _dot(h[:, kh:], bot_ref[2 + j])
            gu_scr[:, gc] = gate
            gu_scr[:, uc] = up
            f = (gate * _sigmoid(gate) * up).astype(BF16)
            f_ref[:, gc] = f
            y = y + _dot(f, wfo_ref[gc, :])
        r3 = _rms_stats(y)
        x2 = x1v + y * r3 * gpo_ref[...]
        err = x2 - t_ref[...]
        loss_ref[...] += jnp.sum(err * err) * (0.5 / D_MODEL)
        dx2 = err * (1.0 / D_MODEL)
        dy, dg = _rms_bwd(dx2, y, r3, gpo_ref[...])
        dgpo_ref[...] += jnp.sum(dg, axis=0, keepdims=True)
        dyb = dy.astype(BF16)
        dy_ref[...] = dyb
        dh_top = jnp.zeros((tT, kh), F32)
        dh_bot = jnp.zeros((tT, kh), F32)
        for j in range(2):
            gc = slice(j * half, (j + 1) * half)
            uc = slice(D_FF + j * half, D_FF + (j + 1) * half)
            df = _dot(dyb, wfo_ref[gc, :], _NT)
            gate = gu_scr[:, gc]
            up = gu_scr[:, uc]
            sg = _sigmoid(gate)
            dgate = (df * up * (sg * (1.0 + gate * (1.0 - sg)))).astype(BF16)
            dup = (df * (gate * sg)).astype(BF16)
            dgu_ref[:, gc] = dgate
            dgu_ref[:, uc] = dup
            dh_top = dh_top + _dot(dgate, top_ref[j], _NT) + _dot(dup, top_ref[2 + j], _NT)
            dh_bot = dh_bot + _dot(dgate, bot_ref[j], _NT) + _dot(dup, bot_ref[2 + j], _NT)
        dh = jnp.concatenate([dh_top, dh_bot], axis=1)
        dx1n, dg2 = _rms_bwd(dh, x1v, r2, gpf_ref[...])
        dgpf_ref[...] += jnp.sum(dg2, axis=0, keepdims=True)
        dx1_ref[...] = dx2 + dx1n

    row = lambda w: pl.BlockSpec((tT, w), lambda i: (i, 0))
    vec = pl.BlockSpec((1, D_MODEL), lambda i: (0, 0))
    res, _ = _call(
        body, name="ffn_fwd_bwd", grid=(T // tT,),
        in_specs=[row(D_MODEL), row(D_MODEL), _whole(), _whole(), _whole(), vec, vec],
        out_specs=[row(D_MODEL), row(D_FF), row(2 * D_FF), row(D_MODEL), row(D_MODEL),
                   pl.BlockSpec((1, LANES), lambda i: (0, 0)), vec, vec],
        out_shape=[_sds((T, D_MODEL), BF16), _sds((T, D_FF), BF16), _sds((T, 2 * D_FF), BF16), _sds((T, D_MODEL), BF16),
                   _sds((T, D_MODEL), F32), _sds((1, LANES), F32), _sds((1, D_MODEL), F32), _sds((1, D_MODEL), F32)],
        scratch_shapes=[pltpu.VMEM((tT, 2 * D_FF), F32)], args=(x1, tgt, w_fi_top, w_fi_bot, w_fo, g_pf, g_po))
    return res


def _merge_bwd(dx1, mix, proj, zg, zs, w_bg, w_bs, w_o, g_pm, job=None):
    T = dx1.shape[0]
    tT = _row_tile(T, 512)

    def body(dx1_ref, mix_ref, gg_ref, gs_ref, zg_ref, zs_ref, wbg_ref, wbs_ref, wo_ref, g_ref,
             dmix_ref, dzg_ref, dzs_ref, dgate_ref, dyg_ref, dys_ref, dgpm_ref):
        @pl.when(pl.program_id(0) == 0)
        def _():
            dgpm_ref[...] = jnp.zeros_like(dgpm_ref)

        mix = mix_ref[...]
        dmix, dg = _rms_bwd(dx1_ref[...], mix, _rms_stats(mix), g_ref[...])
        dgpm_ref[...] += jnp.sum(dg, axis=0, keepdims=True)
        dmb = dmix.astype(BF16)
        dmix_ref[...] = dmb
        dmerged = _dot(dmb, wo_ref[...], _NT)
        for k, (gate_ref, z_ref, w_ref, dz_ref, dy_ref) in enumerate((
                (gg_ref, zg_ref, wbg_ref, dzg_ref, dyg_ref), (gs_ref, zs_ref, wbs_ref, dzs_ref, dys_ref))):
            sg = _sigmoid(gate_ref[...].astype(F32))
            dz = (dmerged * sg).astype(BF16)
            dz_ref[...] = dz
            dgate_ref[:, k * 1024:(k + 1) * 1024] = (dmerged * z_ref[...].astype(F32) * (sg * (1.0 - sg))).astype(BF16)
            dy_ref[...] = _dot(dz, w_ref[...], _NT).astype(BF16)

    row = pl.BlockSpec((tT, D_MODEL), lambda i: (i, 0))
    blk = lambda j: pl.BlockSpec((tT, 1024), lambda i: (i, j))
    vec = pl.BlockSpec((1, D_MODEL), lambda i: (0, 0))
    sds = _sds((T, D_MODEL), BF16)
    return _call(
        body, name="merge_bwd", grid=(T // tT,),
        in_specs=[row, row, blk(5), blk(6), row, row, _whole(), _whole(), _whole(), vec],
        out_specs=[row, row, row, pl.BlockSpec((tT, W_MRG), lambda i: (i, 0)), row, row, vec],
        out_shape=[sds, sds, sds, _sds((T, W_MRG), BF16), sds, sds, _sds((1, D_MODEL), F32)],
        args=(dx1, mix, proj, proj, zg, zs, w_bg, w_bs, w_o, g_pm), job=job)


def _sgu_bwd(proj, dy_sgu, ln_g, ln_b, w_sp, b_sp_t, job=None):
    T = proj.shape[0]
    tT = _row_tile(T, 512)
    nb = tT // SGU_BLOCK

    def body(su_ref, sv_ref, dy_ref, lg_ref, lb_ref, w_ref, b_ref, dp_ref, dw_ref, dbt_ref, dlg_ref, dlb_ref):
        @pl.when(pl.program_id(0) == 0)
        def _():
            dw_ref[...] = jnp.zeros_like(dw_ref)
            dbt_ref[...] = jnp.zeros_like(dbt_ref)
            dlg_ref[...] = jnp.zeros_like(dlg_ref)
            dlb_ref[...] = jnp.zeros_like(dlb_ref)

        mask = _sgu_mask()
        lane = lax.broadcasted_iota(jnp.int32, (SGU_BLOCK, LANES), 1)
        for g in range(SGU_GROUPS):
            gc = slice(g * SGU_DG, (g + 1) * SGU_DG)
            gc_v = slice(1024 + g * SGU_DG, 1024 + (g + 1) * SGU_DG)
            wm = jnp.where(mask, w_ref[g], 0.0).astype(BF16)
            vf, dvf_dsv = _gelu_and_grad(sv_ref[:, gc].astype(F32))
            mu = jnp.mean(vf, axis=-1, keepdims=True)
            vc = vf - mu
            rstd = lax.rsqrt(jnp.mean(vc * vc, axis=-1, keepdims=True) + EPS)
            vhat = vc * rstd
            vn = (vhat * lg_ref[:, gc] + lb_ref[:, gc]).astype(BF16)
            u, du_dsu = _gelu_and_grad(su_ref[:, gc].astype(F32))
            dy = dy_ref[:, gc].astype(F32)
            dmixed = (dy * u).astype(BF16)
            dvn_parts = []
            dw_acc = jnp.zeros((SGU_BLOCK, SGU_BLOCK), F32)
            db_acc = jnp.zeros((SGU_BLOCK, 1), F32)
            for b in range(nb):
                rows = slice(b * SGU_BLOCK, (b + 1) * SGU_BLOCK)
                mixed = _dot(wm, vn[rows, :]) + b_ref[:, g:g + 1]
                dp_ref[rows, gc] = (dy[rows, :] * mixed * du_dsu[rows, :]).astype(BF16)
                dvn_parts.append(_dot(wm, dmixed[rows, :], _TN))
                dw_acc = dw_acc + _dot(dmixed[rows, :], vn[rows, :], _NT)
                db_acc = db_acc + jnp.sum(dmixed[rows, :].astype(F32), axis=-1, keepdims=True)
            dw_ref[g] += jnp.where(mask, dw_acc, 0.0)
            dbt_ref[...] += jnp.where(lane == g, db_acc, 0.0)
            dvn = jnp.concatenate(dvn_parts, axis=0)
            dlg_ref[:, gc] += jnp.sum(dvn * vhat, axis=0, keepdims=True)
            dlb_ref[:, gc] += jnp.sum(dvn, axis=0, keepdims=True)
            dvh = dvn * lg_ref[:, gc]
            dvf = rstd * (dvh - jnp.mean(dvh, axis=-1, keepdims=True)
                          - vhat * jnp.mean(dvh * vhat, axis=-1, keepdims=True))
            dp_ref[:, gc_v] = (dvf * dvf_dsv).astype(BF16)

    blk = lambda j: pl.BlockSpec((tT, 1024), lambda i: (i, j))
    row = lambda w: pl.BlockSpec((tT, w), lambda i: (i, 0))
    vec = pl.BlockSpec((1, 1024), lambda i: (0, 0))
    return _call(
        body, name="sgu_bwd", grid=(T // tT,),
        in_specs=[blk(3), blk(4), row(1024), _whole(), _whole(), _whole(), _whole()],
        out_specs=[row(W_SGU), pl.BlockSpec((SGU_GROUPS, SGU_BLOCK, SGU_BLOCK), lambda i: (0, 0, 0)),
                   pl.BlockSpec((SGU_BLOCK, LANES), lambda i: (0, 0)), vec, vec],
        out_shape=[_sds((T, W_SGU), BF16), _sds((SGU_GROUPS, SGU_BLOCK, SGU_BLOCK), F32), _sds((SGU_BLOCK, LANES), F32),
                   _sds((1, 1024), F32), _sds((1, 1024), F32)],
        args=(proj, proj, dy_sgu, ln_g, ln_b, w_sp, b_sp_t), job=job)


def _gla_bwd(proj, alow, wgu, b_gate, gn, states, dy_gla, job=None):
    T = proj.shape[0]
    tT = _row_tile(T, 512)
    nc = tT // CHUNK
    nt = T // tT

    def body(q_ref, k_ref, v_ref, r_ref, al_ref, wgu_ref, bg_ref, gn_ref, st_ref, sp_ref, dy_ref,
             dp_ref, dal_ref, dgn_ref, dbg_ref, dwgu_ref, g_scr, dd_scr, dt_scr):
        step = pl.program_id(0)

        @pl.when(step == 0)
        def _():
            g_scr[...] = jnp.zeros_like(g_scr)
            dgn_ref[...] = jnp.zeros_like(dgn_ref)
            dbg_ref[...] = jnp.zeros_like(dbg_ref)
            dwgu_ref[...] = jnp.zeros_like(dwgu_ref)

        has_prev = jnp.where(step == nt - 1, 0.0, 1.0)
        logit, la, delta = _gla_decay_terms(al_ref, wgu_ref, bg_ref, tT)
        e = jnp.exp(delta)
        kdec_f = k_ref[...].astype(F32) * e
        kdec = kdec_f.astype(BF16)
        heads = range(GLA_HEADS)
        kcs = [slice(h * GLA_DK, (h + 1) * GLA_DK) for h in heads]
        vcs = [slice(h * GLA_DV, (h + 1) * GLA_DV) for h in heads]
        carry = [g_scr[h] for h in heads]
        dgn_acc = [jnp.zeros((1, GLA_DV), F32) for _ in heads]
        for c in reversed(range(nc)):
            rows = slice(c * CHUNK, (c + 1) * CHUNK)
            first = slice(c * CHUNK, c * CHUNK + 1)
            dec = jnp.exp(la[first, :] + delta[first, :])
            s_b = [st_ref[c, h].astype(BF16) for h in heads]
            qs = [(q_ref[rows, kcs[h]].astype(F32) * (GLA_DK ** -0.5)).astype(BF16) for h in heads]
            o = [_dot(qs[h], s_b[h], _NT) for h in heads]
            do = []
            for h in heads:
                rstd = _rms_stats(o[h])
                ohat = o[h] * rstd
                gnh = gn_ref[:, vcs[h]]
                dy = dy_ref[rows, vcs[h]].astype(F32)
                rr = r_ref[rows, vcs[h]].astype(F32)
                sg = _sigmoid(rr)
                don = dy * (rr * sg)
                dp_ref[rows, OFF_R + h * GLA_DV:OFF_R + (h + 1) * GLA_DV] = (
                    dy * (ohat * gnh) * (sg * (1.0 + rr * (1.0 - sg)))).astype(BF16)
                dgn_acc[h] = dgn_acc[h] + jnp.sum(don * ohat, axis=0, keepdims=True)
                dn = don * gnh
                do.append((rstd * (dn - ohat * jnp.mean(dn * ohat, axis=-1, keepdims=True))).astype(BF16))
            dq = [_dot(do[h], s_b[h]) for h in heads]
            g_t = [_dot(do[h], qs[h], _TN) + carry[h] for h in heads]
            g_b = [g_t[h].astype(BF16) for h in heads]
            dv = [_dot(kdec[rows, kcs[h]], g_b[h], _NT) for h in heads]
            dkdec = [_dot(v_ref[rows, vcs[h]], g_b[h]) for h in heads]
            for h in heads:
                s_prev = st_ref[c - 1, h] if c > 0 else sp_ref[0, h] * has_prev
                ddec = jnp.sum(g_t[h] * s_prev, axis=0, keepdims=True)
                carry[h] = g_t[h] * dec[:, kcs[h]]
                dp_ref[rows, OFF_Q + h * GLA_DK:OFF_Q + (h + 1) * GLA_DK] = (dq[h] * (GLA_DK ** -0.5)).astype(BF16)
                dp_ref[rows, OFF_V + h * GLA_DV:OFF_V + (h + 1) * GLA_DV] = dv[h].astype(BF16)
                dp_ref[rows, OFF_K + h * GLA_DK:OFF_K + (h + 1) * GLA_DK] = (dkdec[h] * e[rows, kcs[h]]).astype(BF16)
                dd_scr[rows, kcs[h]] = dkdec[h] * kdec_f[rows, kcs[h]]
                dt_scr[rows, kcs[h]] = jnp.broadcast_to(ddec * dec[:, kcs[h]], (CHUNK, GLA_DK))
        for h in heads:
            g_scr[h] = carry[h]
            dgn_ref[:, vcs[h]] += dgn_acc[h]
        dla = _dot_exact_lhs(_chunk_masks(tT, upper=False), dd_scr[...]) + dt_scr[...]
        dlogit = dla * (1.0 / GLA_TAU) * _sigmoid(-logit)
        dbg_ref[...] += jnp.sum(dlogit, axis=0, keepdims=True)
        dwgu_ref[...] += _dot_f32(al_ref[...], dlogit, _TN)
        dal_ref[...] = _dot_f32(dlogit, wgu_ref[...], _NT).astype(BF16)

    rev = lambda i: nt - 1 - i
    blk = lambda w, j: pl.BlockSpec((tT, w), lambda i: (rev(i), j))
    st_blk = pl.BlockSpec((nc, GLA_HEADS, GLA_DV, GLA_DK), lambda i: (rev(i), 0, 0, 0))
    sp_blk = pl.BlockSpec((1, GLA_HEADS, GLA_DV, GLA_DK), lambda i: (jnp.maximum(rev(i) * nc - 1, 0), 0, 0, 0))
    return _call(
        body, name="gla_bwd", grid=(nt,),
        in_specs=[blk(512, 0), blk(512, 1), blk(1024, 1), blk(1024, 2), blk(LANES, 0), _whole(), _whole(), _whole(),
                  st_blk, sp_blk, blk(GLA_V, 0)],
        out_specs=[blk(W_GLA, 0), blk(LANES, 0), pl.BlockSpec((1, GLA_V), lambda i: (0, 0)),
                   pl.BlockSpec((1, GLA_QK), lambda i: (0, 0)), pl.BlockSpec((LANES, GLA_QK), lambda i: (0, 0))],
        out_shape=[_sds((T, W_GLA), BF16), _sds((T, LANES), BF16), _sds((1, GLA_V), F32), _sds((1, GLA_QK), F32),
                   _sds((LANES, GLA_QK), F32)],
        scratch_shapes=[pltpu.VMEM((GLA_HEADS, GLA_DV, GLA_DK), F32), pltpu.VMEM((tT, GLA_QK), F32),
                        pltpu.VMEM((tT, GLA_QK), F32)],
        args=(proj, proj, proj, proj, alow, wgu, b_gate, gn, states, states, dy_gla), job=job)


def _inproj_bwd(x, dx1, g1, w_all, dparts, job=None):
    T = x.shape[0]
    tT = _row_tile(T, 512)
    offs = (0, W_GLA, W_GLA + W_SGU, N_MAIN)

    def body(x_ref, dx1_ref, g_ref, w_ref, *rest):
        part_refs, (dx_ref, dg_ref) = rest[:len(offs)], rest[len(offs):]

        @pl.when(pl.program_id(0) == 0)
        def _():
            dg_ref[...] = jnp.zeros_like(dg_ref)

        da = jnp.zeros((tT, D_MODEL), F32)
        for off, p_ref in zip(offs, part_refs):
            da = da + _dot(p_ref[...], w_ref[:, off:off + p_ref.shape[1]], _NT)
        xv = x_ref[...]
        dx, dg = _rms_bwd(da, xv, _rms_stats(xv), g_ref[...])
        dg_ref[...] += jnp.sum(dg, axis=0, keepdims=True)
        dx_ref[...] = dx1_ref[...] + dx

    row = lambda w: pl.BlockSpec((tT, w), lambda i: (i, 0))
    vec = pl.BlockSpec((1, D_MODEL), lambda i: (0, 0))
    return _call(
        body, name="inproj_bwd", grid=(T // tT,),
        in_specs=[row(D_MODEL), row(D_MODEL), vec, _whole()] + [row(p.shape[1]) for p in dparts],
        out_specs=[row(D_MODEL), vec], out_shape=[_sds((T, D_MODEL), F32), _sds((1, D_MODEL), F32)],
        args=(x, dx1, g1, w_all, *dparts), job=job)


def _tn_matmul(a, b, name, job=None):
    T, M = a.shape
    N = b.shape[1]
    tk = _row_tile(T, 1024)
    tm = M if M <= 1024 else 1408
    tn = N if N <= 3072 else N // 2
    assert M % tm == 0 and N % tn == 0

    def body(a_ref, b_ref, o_ref):
        @pl.when(pl.program_id(2) == 0)
        def _():
            o_ref[...] = _dot(a_ref[...], b_ref[...], _TN)

        @pl.when(pl.program_id(2) > 0)
        def _():
            o_ref[...] += _dot(a_ref[...], b_ref[...], _TN)

    res, jres = _call(
        body, name=name, grid=(M // tm, N // tn, T // tk),
        in_specs=[pl.BlockSpec((tk, tm), lambda i, j, k: (k, i)), pl.BlockSpec((tk, tn), lambda i, j, k: (k, j))],
        out_specs=[pl.BlockSpec((tm, tn), lambda i, j, k: (i, j))], out_shape=[_sds((M, N), F32)], args=(a, b), job=job)
    return res[0], jres


def _pad_rows(a, rows=8):
    return jnp.pad(a, ((0, rows - a.shape[0]), (0, LANES - a.shape[1])))


def _halves_view(dw):
    r = dw.shape[0] // N_CHIPS
    return dw.reshape(N_CHIPS, 2, r // 2, dw.shape[1])


def kernel(x, norm_pre_mix, w_in, w_gate_up, b_gate, gla_norm, sgu_ln_g, sgu_ln_b, w_spatial, b_spatial, w_branch_gla, w_branch_sgu, w_out, norm_post_mix, norm_pre_ffn, w_ffn_in, w_ffn_out, norm_post_ffn, loss_target, m_norm_pre_mix, m_w_in, m_w_gate_up, m_b_gate, m_gla_norm, m_sgu_ln_g, m_sgu_ln_b, m_w_spatial, m_b_spatial, m_w_branch_gla, m_w_branch_sgu, m_w_out, m_norm_post_mix, m_norm_pre_ffn, m_w_ffn_in, m_w_ffn_out, m_norm_post_ffn, v_norm_pre_mix, v_w_in, v_w_gate_up, v_b_gate, v_gla_norm, v_sgu_ln_g, v_sgu_ln_b, v_w_spatial, v_b_spatial, v_w_branch_gla, v_w_branch_sgu, v_w_out, v_norm_post_mix, v_norm_pre_ffn, v_w_ffn_in, v_w_ffn_out, v_norm_post_ffn):
    chip = 2 * lax.axis_index("x") + lax.axis_index("y")
    xt, tgt = x[0], loss_target[0]

    tiny = jnp.concatenate([w_gate_up[0], _pad_rows(gla_norm[0]), _pad_rows(sgu_ln_g[0]), _pad_rows(sgu_ln_b[0]),
                            jnp.zeros((8, LANES), F32)], axis=0)
    def with_own(gathered, own):
        if gathered.ndim == 3:
            return lax.dynamic_update_slice(gathered, own[None], (chip, 0, 0))
        return lax.dynamic_update_slice(gathered, own, (0, chip * own.shape[1]))

    w_in_b = w_in[0].astype(BF16)
    g_in, g_tiny = _run_job(_job_gather([w_in_b, tiny], [False, False]), "gather_w_in")
    g_tiny = with_own(g_tiny, tiny)
    w_all = _relayout_w_in(with_own(g_in, w_in_b))
    cols = lambda a: a.transpose(1, 0, 2).reshape(a.shape[1], N_CHIPS * a.shape[2])
    wgu = jnp.pad(cols(g_tiny[:, 0:16]), ((0, LANES - GLA_RANK), (0, 0)))
    gn = cols(g_tiny[:, 16:20, :64]).reshape(1, GLA_V)
    ln_g = cols(g_tiny[:, 24:28, :64]).reshape(1, 1024)
    ln_b = cols(g_tiny[:, 32:36, :64]).reshape(1, 1024)
    b_sp_t = jnp.pad(b_spatial[0].T, ((0, 0), (0, LANES - SGU_GROUPS)))
    w_sp = w_spatial[0]

    own_rows = [w_branch_gla[0].astype(BF16), w_branch_sgu[0].astype(BF16), w_out[0].astype(BF16), w_ffn_out[0].astype(BF16)]
    (a, proj, alow), g_rows = _inproj_fwd(xt, norm_pre_mix, w_all, job=_job_gather(own_rows, [False] * 4))
    rows = lambda g: g.reshape(N_CHIPS * g.shape[1], g.shape[2])
    w_bg, w_bs, w_o, w_fo = [rows(with_own(g, own)) for g, own in zip(g_rows, own_rows)]
    w_fi_b = w_ffn_in[0].astype(BF16)
    fi_top, fi_bot = w_fi_b[:D_MODEL // 2], w_fi_b[D_MODEL // 2:]
    (y_gla, states), (g_top,) = _gla_fwd(proj, alow, wgu, b_gate, gn, job=_job_gather([fi_top], [False]))
    y_sgu = _sgu_fwd(proj, ln_g, ln_b, w_sp, b_sp_t)
    (zg, zs, merged, mix, x1), (g_bot,) = _merge_fwd(xt, proj, y_gla, y_sgu, w_bg, w_bs, w_o, norm_post_mix,
                                                     job=_job_gather([fi_bot], [False]))
    h, f, dgu, dy, dx1, loss, d_gpf, d_gpo = _ffn_fwd_bwd(x1, tgt, with_own(g_top, fi_top), with_own(g_bot, fi_bot),
                                                          w_fo, norm_pre_ffn, norm_post_ffn)

    own_part = lambda c: lax.dynamic_index_in_dim(c, chip, 0, keepdims=False)
    whole = lambda hs: [[(h_, None)] for h_ in hs]
    dw_fo, _ = _tn_matmul(f, dy, "dw_ffn_out")
    dw_fo4 = _halves_view(dw_fo)
    dw_fi, (q_fo,) = _tn_matmul(h, dgu, "dw_ffn_in", job=_job_to_other_core([[(dw_fo4, 0)]]))
    c_fo = _presum(dw_fo4, q_fo, "presum_ffn_out")
    (dmix, dzg, dzs, dp_mrg, dyg, dys, d_gpm), (s_fo, q_fi) = _merge_bwd(
        dx1, mix, proj, zg, zs, w_bg, w_bs, w_o, norm_post_mix,
        job=_join(_job_scatter([c_fo]), _job_to_other_core([[(dw_fi, 0)]])))
    c_fi = _presum(dw_fi, q_fi, "presum_ffn_in")
    dw_c, _ = _tn_matmul(a, dp_mrg, "dw_in_merge")
    dw_o4 = _halves_view(_tn_matmul(merged, dmix, "dw_out")[0])
    dw_bg4 = _halves_view(_tn_matmul(y_gla, dzg, "dw_branch_gla")[0])
    dw_bs4 = _halves_view(_tn_matmul(y_sgu, dzs, "dw_branch_sgu")[0])
    (dp_sgu, d_wsp, d_bsp_t, d_lng, d_lnb), (q_o, q_bg, q_bs, q_c) = _sgu_bwd(
        proj, dys, ln_g, ln_b, w_sp, b_sp_t,
        job=_job_to_other_core([[(dw_o4, 0)], [(dw_bg4, 0)], [(dw_bs4, 0)], [(dw_c, 0)]]))
    c_o, c_bg, c_bs = (_presum(dw_o4, q_o, "presum_out"), _presum(dw_bg4, q_bg, "presum_branch_gla"),
                       _presum(dw_bs4, q_bs, "presum_branch_sgu"))
    h_fo = _sum_slots(own_part(c_fo), s_fo, "sum_ffn_out")
    dw_b, _ = _tn_matmul(a, dp_sgu, "dw_in_sgu")
    (dp_gla, dal, d_gn, d_bg, d_wgu), (s_fi, t_fo, q_b) = _gla_bwd(
        proj, alow, wgu, b_gate, gn, states, dyg,
        job=_join(_job_scatter([c_fi]), _job_to_other_core(whole([h_fo]) + [[(dw_b, 0)]])))
    h_fi = _sum_slots(own_part(c_fi), s_fi, "sum_ffn_in")
    dw_d, _ = _tn_matmul(a, dal, "dw_in_gate")
    dw_a, (s_o, s_bg, s_bs, t_fi, q_d) = _tn_matmul(
        a, dp_gla, "dw_in_gla",
        job=_join(_job_scatter([c_o, c_bg, c_bs]), _job_to_other_core(whole([h_fi]) + [[(dw_d, 0)]])))
    h_o, h_bg, h_bs = (_sum_slots(own_part(c_o), s_o, "sum_out"), _sum_slots(own_part(c_bg), s_bg, "sum_branch_gla"),
                       _sum_slots(own_part(c_bs), s_bs, "sum_branch_sgu"))

    grads, deltas, new_m, new_v = {}, {}, {}, {}

    def update(name, w, m, v, g_mine, g_theirs, job=None):
        (g, d, m2, v2), jres = _adamw(w[0], m[0], v[0], g_mine, g_theirs, "adamw_" + name, job=job)
        grads[name], deltas[name], new_m[name], new_v[name] = g[None], d[None], m2[None], v2[None]
        return jres

    dw_in = [(dw_a, 0), (dw_b, W_GLA), (dw_c, W_GLA + W_SGU), (dw_d, N_MAIN)]
    q_a, t_o, t_bg, t_bs = update("w_ffn_out", w_ffn_out, m_w_ffn_out, v_w_ffn_out, [h_fo], [t_fo],
                                  job=_job_to_other_core([[(dw_a, 0)]] + whole([h_o, h_bg, h_bs])))
    q_in = [q_a, q_b, q_c, q_d]
    hr_in = D_MODEL // 2
    c_in_a, _ = _presum_w_in(dw_in, q_in, 0, hr_in // 8, "presum_w_in_a")
    c_in_b, (s_in_a,) = _presum_w_in(dw_in, q_in, hr_in // 8, 7 * hr_in // 8, "presum_w_in_b",
                                     job=_job_scatter([c_in_a]))
    update("w_ffn_in", w_ffn_in, m_w_ffn_in, v_w_ffn_in, [h_fi], [t_fi])
    update("w_out", w_out, m_w_out, v_w_out, [h_o], [t_o])
    update("w_branch_gla", w_branch_gla, m_w_branch_gla, v_w_branch_gla, [h_bg], [t_bg])
    update("w_branch_sgu", w_branch_sgu, m_w_branch_sgu, v_w_branch_sgu, [h_bs], [t_bs])
    (grad_x, d_g1), (s_in_b,) = _inproj_bwd(xt, dx1, norm_pre_mix, w_all, (dp_gla, dp_sgu, dp_mrg, dal),
                                            job=_job_scatter([c_in_b]))
    h_in = [_sum_slots(own_part(c_in_a), s_in_a, "sum_w_in_a"), _sum_slots(own_part(c_in_b), s_in_b, "sum_w_in_b")]
    t_in = _run_job(_job_to_other_core(whole(h_in)), "swap_w_in")
    update("w_in", w_in, m_w_in, v_w_in, h_in, t_in)

    small_names = ["w_spatial", "w_gate_up", "norm_pre_mix", "norm_post_mix", "norm_pre_ffn", "norm_post_ffn", "b_gate",
                   "b_spatial", "gla_norm", "sgu_ln_g", "sgu_ln_b"]
    loss_out, small = _small_adamw(
        _small_sum([d_wsp, d_wgu, d_g1, d_gpm, d_gpf, d_gpo, d_bg, d_bsp_t, d_gn, d_lng, d_lnb, loss]),
        [w_spatial, w_gate_up, norm_pre_mix, norm_post_mix, norm_pre_ffn, norm_post_ffn, b_gate, b_spatial, gla_norm,
         sgu_ln_g, sgu_ln_b],
        [m_w_spatial, m_w_gate_up, m_norm_pre_mix, m_norm_post_mix, m_norm_pre_ffn, m_norm_post_ffn, m_b_gate,
         m_b_spatial, m_gla_norm, m_sgu_ln_g, m_sgu_ln_b],
        [v_w_spatial, v_w_gate_up, v_norm_pre_mix, v_norm_post_mix, v_norm_pre_ffn, v_norm_post_ffn, v_b_gate,
         v_b_spatial, v_gla_norm, v_sgu_ln_g, v_sgu_ln_b])
    for store, vals in zip((grads, deltas, new_m, new_v), small):
        store.update(zip(small_names, vals))

    order = ["norm_pre_mix", "w_in", "w_gate_up", "b_gate", "gla_norm", "sgu_ln_g", "sgu_ln_b", "w_spatial", "b_spatial",
             "w_branch_gla", "w_branch_sgu", "w_out", "norm_post_mix", "norm_pre_ffn", "w_ffn_in", "w_ffn_out",
             "norm_post_ffn"]
    out = [loss_out, grad_x[None]]
    for store in (grads, deltas, new_m, new_v):
        out.extend(store[n] for n in order)
    return tuple(out)
```

```python
import jax
import jax.numpy as jnp
from jax import lax
from jax.experimental import pallas as pl
from jax.experimental.pallas import tpu as pltpu

F32 = jnp.float32
BF16 = jnp.bfloat16

D_MODEL = 1024
GLA_HEADS = 4
GLA_DK = 128
GLA_DV = 256
GLA_QK = GLA_HEADS * GLA_DK
GLA_V = GLA_HEADS * GLA_DV
GLA_RANK = 16
GLA_TAU = 16.0
CHUNK = 64
SGU_GROUPS = 4
SGU_BLOCK = 128
SGU_DG = 256
D_FF = 2816
EPS = 1e-6
LANES = 128

OFF_Q, OFF_K, OFF_V, OFF_R, OFF_SU, OFF_SV, OFF_GG, OFF_GS, OFF_AL = 0, 512, 1024, 2048, 3072, 4096, 5120, 6144, 7168
W_GLA, W_SGU, W_MRG = 3072, 2048, 2048
N_MAIN = 7168
N_ALL = N_MAIN + LANES
_IN_SPLITS = (GLA_QK, GLA_QK, GLA_V, GLA_V, GLA_RANK, 1024, 1024, 1024, 1024)
_IN_STARTS = tuple(sum(_IN_SPLITS[:i]) for i in range(len(_IN_SPLITS) + 1))
_IN_DST = (OFF_Q, OFF_K, OFF_V, OFF_R, OFF_AL, OFF_SU, OFF_SV, OFF_GG, OFF_GS)
D_IN = _IN_STARTS[-1]

ADAM_LR = 0.001
ADAM_B1 = 0.9
ADAM_B2 = 0.999
ADAM_EPS = 1e-08
ADAM_WD = 0.01
ADAM_STEP = 10

VMEM_LIMIT_BYTES = 56 * 1024 * 1024
N_CHIPS = 4
N_PEER = N_CHIPS - 1
N_DEV = 8
MESH = pl.DeviceIdType.MESH

_NN = (((1,), (0,)), ((), ()))
_NT = (((1,), (1,)), ((), ()))
_TN = (((0,), (0,)), ((), ()))


def _dot(a, b, dims=_NN):
    return lax.dot_general(a, b, dims, preferred_element_type=F32)


def _split(x):
    hi = x.astype(BF16)
    lo = (x - hi.astype(F32)).astype(BF16)
    return hi, lo


def _dot_f32(a, b, dims=_NN):
    ah, al = _split(a)
    bh, bl = _split(b)
    return _dot(ah, bh, dims) + (_dot(al, bh, dims) + _dot(ah, bl, dims))


def _dot_exact_lhs(m, x):
    xh, xl = _split(x)
    return _dot(m, xh) + _dot(m, xl)


def _sigmoid(x):
    return 1.0 / (1.0 + jnp.exp(-x))


def _log_sigmoid(x):
    return jnp.minimum(x, 0.0) - jnp.log(1.0 + jnp.exp(-jnp.abs(x)))


_GELU_C = 0.7978845608028654
_GELU_A = 0.044715


def _gelu_and_grad(x):
    x2 = x * x
    t = jnp.tanh(_GELU_C * (x + _GELU_A * x * x2))
    g = 0.5 * x * (1.0 + t)
    dg = 0.5 * (1.0 + t) + 0.5 * x * (1.0 - t * t) * (_GELU_C * (1.0 + 3.0 * _GELU_A * x2))
    return g, dg


def _gelu(x):
    t = jnp.tanh(_GELU_C * (x + _GELU_A * x * x * x))
    return 0.5 * x * (1.0 + t)


def _rms_stats(x):
    return lax.rsqrt(jnp.mean(x * x, axis=-1, keepdims=True) + EPS)


def _rms_bwd(dout, y, r, g):
    yhat = y * r
    dn = dout * g
    dy = r * (dn - yhat * jnp.mean(dn * yhat, axis=-1, keepdims=True))
    return dy, dout * yhat


def _whole():
    return pl.BlockSpec(memory_space=pltpu.VMEM)


def _row_tile(T, want):
    t = min(T, want)
    assert T % t == 0
    return t


def _chunk_masks(tT, upper):
    row = lax.broadcasted_iota(jnp.int32, (tT, tT), 0)
    col = lax.broadcasted_iota(jnp.int32, (tT, tT), 1)
    same = lax.shift_right_logical(row, 6) == lax.shift_right_logical(col, 6)
    tri = (col > row) if upper else (col < row)
    return jnp.where(same & tri, 1.0, 0.0).astype(BF16)


class _Job:
    def __init__(self, ins, out_shapes, scratch, start, finish):
        self.ins, self.out_shapes, self.scratch, self.start, self.finish = list(ins), list(out_shapes), list(scratch), start, finish


def _join(*jobs):
    def split(refs, counts):
        out, at = [], 0
        for n in counts:
            out.append(refs[at:at + n])
            at += n
        return out

    ni, no, ns = [len(j.ins) for j in jobs], [len(j.out_shapes) for j in jobs], [len(j.scratch) for j in jobs]

    def start(ins, outs, scr):
        for j, a, b, c in zip(jobs, split(ins, ni), split(outs, no), split(scr, ns)):
            j.start(a, b, c)

    def finish(ins, outs, scr):
        for j, a, b, c in zip(jobs, split(ins, ni), split(outs, no), split(scr, ns)):
            j.finish(a, b, c)

    return _Job(sum((j.ins for j in jobs), []), sum((j.out_shapes for j in jobs), []),
                sum((j.scratch for j in jobs), []), start, finish)


def _mesh_pos():
    return lax.axis_index("x"), lax.axis_index("y"), lax.axis_index("c")


def _peer_chips(xi, yi):
    return [(1 - xi, yi), (xi, 1 - yi), (1 - xi, 1 - yi)]


def _half(ci, rows):
    return pl.ds(pl.multiple_of(ci * rows, 8), rows)


def _sds(shape, dtype):
    return jax.ShapeDtypeStruct(tuple(shape), dtype)


def _job_gather(arrs, by_cols):
    n = len(arrs)

    def dst(o, k, chip, rows):
        if by_cols[k]:
            c = arrs[k].shape[1]
            return o.at[rows, pl.ds(pl.multiple_of(chip * c, LANES), c)]
        return o.at[chip, rows]

    def copies(ins, outs, scr, want):
        ici_send, ici_recv, d2d_send, d2d_recv = scr
        xi, yi, ci = _mesh_pos()
        me = 2 * xi + yi
        res = []
        for k in range(n):
            r = arrs[k].shape[0]
            mine, other = _half(ci, r // 2), _half(1 - ci, r // 2)
            for j, (px, py) in enumerate(_peer_chips(xi, yi)):
                s = k * N_PEER + j
                pc = 2 * px + py
                ici = dict(send_sem=ici_send.at[s], recv_sem=ici_recv.at[s], device_id=(px, py, ci), device_id_type=MESH)
                d2d = dict(send_sem=d2d_send.at[s], recv_sem=d2d_recv.at[s], device_id=(xi, yi, 1 - ci),
                           device_id_type=MESH)
                made = {}
                if "send" in want:
                    made["send"] = pltpu.make_async_remote_copy(
                        src_ref=ins[k].at[mine], dst_ref=dst(outs[k], k, me, mine), **ici)
                if "arrive" in want:
                    made["arrive"] = pltpu.make_async_remote_copy(
                        src_ref=ins[k].at[mine], dst_ref=dst(outs[k], k, pc, mine), **ici)
                if "forward" in want:
                    made["forward"] = pltpu.make_async_remote_copy(
                        src_ref=dst(outs[k], k, pc, mine), dst_ref=dst(outs[k], k, pc, mine), **d2d)
                if "handed" in want:
                    made["handed"] = pltpu.make_async_remote_copy(
                        src_ref=dst(outs[k], k, pc, other), dst_ref=dst(outs[k], k, pc, other), **d2d)
                res.append(made)
        return res

    def start(ins, outs, scr):
        for cp in copies(ins, outs, scr, ("send",)):
            cp["send"].start()

    def finish(ins, outs, scr):
        for cp in copies(ins, outs, scr, ("arrive", "forward")):
            cp["arrive"].wait_recv()
            cp["forward"].start()
        for cp in copies(ins, outs, scr, ("handed", "send", "forward")):
            cp["handed"].wait_recv()
            cp["send"].wait_send()
            cp["forward"].wait_send()

    shapes = [_sds((a.shape[0], N_CHIPS * a.shape[1]) if bc else (N_CHIPS,) + a.shape, a.dtype)
              for a, bc in zip(arrs, by_cols)]
    dma = pltpu.SemaphoreType.DMA
    return _Job(arrs, shapes, [dma((n * N_PEER,))] * 4, start, finish)


def _job_scatter(parts):
    n = len(parts)

    def copies(ins, outs, scr):
        send_sems, recv_sems = scr
        xi, yi, ci = _mesh_pos()
        res = []
        for k in range(n):
            for j, (px, py) in enumerate(_peer_chips(xi, yi)):
                s = k * N_PEER + j
                res.append(pltpu.make_async_remote_copy(
                    src_ref=ins[k].at[2 * px + py], dst_ref=outs[k].at[j], send_sem=send_sems.at[s],
                    recv_sem=recv_sems.at[s], device_id=(px, py, ci), device_id_type=MESH))
        return res

    def start(ins, outs, scr):
        for cp in copies(ins, outs, scr):
            cp.start()

    def finish(ins, outs, scr):
        for cp in copies(ins, outs, scr):
            cp.wait_recv()
            cp.wait_send()

    dma = pltpu.SemaphoreType.DMA
    return _Job(parts, [_sds((N_PEER,) + p.shape[1:], p.dtype) for p in parts], [dma((n * N_PEER,))] * 2, start, finish)


def _job_to_other_core(groups):
    pieces = [(g, a, off) for g, group in enumerate(groups) for a, off in group]
    n = len(pieces)

    def geometry(group):
        a0, off0 = group[0]
        if off0 is None:
            return a0.shape
        if a0.ndim == 4:
            return (N_CHIPS, a0.shape[2], a0.shape[3])
        return (a0.shape[0] // 2, sum(a.shape[1] for a, _ in group))

    def copies(ins, outs, scr):
        send_sems, recv_sems = scr
        xi, yi, ci = _mesh_pos()
        res = []
        for p, (g, a, off) in enumerate(pieces):
            if off is None:
                give, land = ins[p], outs[g]
            elif a.ndim == 4:
                give, land = ins[p].at[pl.ds(0, N_CHIPS), 1 - ci], outs[g]
            else:
                hr, w = a.shape[0] // 2, a.shape[1]
                give, land = ins[p].at[_half(1 - ci, hr)], outs[g].at[pl.ds(0, hr), pl.ds(off, w)]
            res.append(pltpu.make_async_remote_copy(
                src_ref=give, dst_ref=land, send_sem=send_sems.at[p], recv_sem=recv_sems.at[p],
                device_id=(xi, yi, 1 - ci), device_id_type=MESH))
        return res

    def start(ins, outs, scr):
        for cp in copies(ins, outs, scr):
            cp.start()

    def finish(ins, outs, scr):
        for cp in copies(ins, outs, scr):
            cp.wait_recv()
            cp.wait_send()

    dma = pltpu.SemaphoreType.DMA
    return _Job([a for _, a, _ in pieces], [_sds(geometry(group), group[0][0].dtype) for group in groups],
                [dma((n,))] * 2, start, finish)


def _call(body, *, name, grid, in_specs, out_specs, out_shape, args, scratch_shapes=(), parallel=False, job=None):
    n_in, n_out, n_scr = len(in_specs), len(out_specs), len(scratch_shapes)

    def params(sem):
        return pltpu.CompilerParams(dimension_semantics=sem, vmem_limit_bytes=VMEM_LIMIT_BYTES)

    if job is None:
        sem = ("parallel" if parallel else "arbitrary",) * len(grid)
        res = pl.pallas_call(
            body, name=name, grid=grid, in_specs=in_specs, out_specs=out_specs, out_shape=out_shape,
            scratch_shapes=list(scratch_shapes), compiler_params=params(sem))(*args)
        return list(res), []
    n_ji, n_jo = len(job.ins), len(job.out_shapes)

    def carried(*refs):
        ins, refs = refs[:n_in], refs[n_in:]
        j_ins, refs = refs[:n_ji], refs[n_ji:]
        outs, refs = refs[:n_out], refs[n_out:]
        j_outs, refs = refs[:n_jo], refs[n_jo:]
        scr, j_scr = refs[:n_scr], refs[n_scr:]
        ids = [pl.program_id(d) for d in range(len(grid))]
        first = ids[0] == 0
        last = ids[0] == grid[0] - 1
        for d in range(1, len(grid)):
            first = first & (ids[d] == 0)
            last = last & (ids[d] == grid[d] - 1)

        @pl.when(first)
        def _():
            job.start(j_ins, j_outs, j_scr)

        body(*ins, *outs, *scr)

        @pl.when(last)
        def _():
            job.finish(j_ins, j_outs, j_scr)

    hbm = pl.BlockSpec(memory_space=pl.ANY)
    res = pl.pallas_call(
        carried, name=name, grid=grid, in_specs=list(in_specs) + [hbm] * n_ji, out_specs=list(out_specs) + [hbm] * n_jo,
        out_shape=list(out_shape) + job.out_shapes, scratch_shapes=list(scratch_shapes) + job.scratch,
        compiler_params=params(("arbitrary",) * len(grid)))(*args, *job.ins)
    return list(res[:n_out]), list(res[n_out:])


def _run_job(job, name):
    n_i, n_o = len(job.ins), len(job.out_shapes)

    def body(*refs):
        ins, outs, scr = refs[:n_i], refs[n_i:n_i + n_o], refs[n_i + n_o:]
        job.start(ins, outs, scr)
        job.finish(ins, outs, scr)

    hbm = pl.BlockSpec(memory_space=pl.ANY)
    return list(pl.pallas_call(body, name=name, in_specs=[hbm] * n_i, out_specs=[hbm] * n_o, out_shape=job.out_shapes,
                               scratch_shapes=job.scratch)(*job.ins))


def _adam_values(w, m, v, g):
    m2 = ADAM_B1 * m + (1.0 - ADAM_B1) * g
    v2 = ADAM_B2 * v + (1.0 - ADAM_B2) * (g * g)
    delta = -ADAM_LR * ((m2 / (1.0 - ADAM_B1 ** ADAM_STEP)) / (jnp.sqrt(v2 / (1.0 - ADAM_B2 ** ADAM_STEP)) + ADAM_EPS)
                        + ADAM_WD * w)
    return delta, m2, v2


_P_WSP, _P_WGU, _P_NORM, _P_BG, _P_BSP, _P_HEAD, _P_LOSS, _P_ROWS = 0, 512, 576, 608, 616, 624, 720, 728


def _small_sum(dgrads):
    def body(dwsp, dwgu, dg1, dgpm, dgpf, dgpo, dbg, dbspt, dgn, dlng, dlnb, loss_in, tot, pack, slots, send_sems,
             recv_sems):
        xi, yi, ci = _mesh_pos()
        chip = 2 * xi + yi

        pack[...] = jnp.zeros_like(pack)
        for g in range(SGU_GROUPS):
            pack[_P_WSP + g * SGU_BLOCK:_P_WSP + (g + 1) * SGU_BLOCK] = dwsp[g]
        for j in range(N_CHIPS):
            pack[_P_WGU + GLA_RANK * j:_P_WGU + GLA_RANK * (j + 1)] = dwgu[0:GLA_RANK, LANES * j:LANES * (j + 1)]
        for k, r in enumerate((dg1, dgpm, dgpf, dgpo)):
            for q in range(8):
                pack[_P_NORM + 8 * k + q:_P_NORM + 8 * k + q + 1] = r[:, LANES * q:LANES * (q + 1)]
        for q in range(4):
            pack[_P_BG + q:_P_BG + q + 1] = dbg[:, LANES * q:LANES * (q + 1)]
        pack[_P_BSP:_P_BSP + SGU_GROUPS] = jnp.transpose(dbspt[...])[0:SGU_GROUPS]
        for k, r in enumerate((dgn, dlng, dlnb)):
            for j in range(N_CHIPS):
                for hh in range(4):
                    row = _P_HEAD + 32 * k + 8 * j + hh
                    pack[row:row + 1, 0:64] = r[:, 256 * hh + 64 * j:256 * hh + 64 * (j + 1)]
        pack[_P_LOSS:_P_LOSS + 1] = loss_in[...]

        to_sibling = pltpu.make_async_remote_copy(
            src_ref=pack, dst_ref=tot, send_sem=send_sems.at[N_PEER], recv_sem=recv_sems.at[N_PEER],
            device_id=(xi, yi, 1 - ci), device_id_type=MESH)
        to_sibling.start()
        to_sibling.wait_recv()
        to_sibling.wait_send()
        pack[...] = pack[...] + tot[...]
        slots[chip] = pack[...]

        def copy(j, slot):
            px, py = _peer_chips(xi, yi)[j]
            return pltpu.make_async_remote_copy(
                src_ref=pack, dst_ref=slots.at[slot(2 * px + py)], send_sem=send_sems.at[j], recv_sem=recv_sems.at[j],
                device_id=(px, py, ci), device_id_type=MESH)

        sends = [copy(j, lambda peer_chip: chip) for j in range(N_PEER)]
        for cp in sends:
            cp.start()
        for j in range(N_PEER):
            copy(j, lambda peer_chip: peer_chip).wait_recv()
        for cp in sends:
            cp.wait_send()
        acc = slots[0]
        for d in range(1, N_CHIPS):
            acc = acc + slots[d]
        tot[...] = acc

    return pl.pallas_call(
        body, name="small_sum", in_specs=[_whole()] * 12, out_specs=_whole(), out_shape=_sds((_P_ROWS, LANES), F32),
        scratch_shapes=[pltpu.VMEM((_P_ROWS, LANES), F32), pltpu.VMEM((N_CHIPS, _P_ROWS, LANES), F32),
                        pltpu.SemaphoreType.DMA((N_PEER + 1,)), pltpu.SemaphoreType.DMA((N_PEER + 1,))],
        compiler_params=pltpu.CompilerParams(vmem_limit_bytes=VMEM_LIMIT_BYTES),
    )(*dgrads)


def _small_adamw(tot, ws, ms, vs):
    n = len(ws)

    def body(*refs):
        tot = refs[0]
        w_refs, m_refs, v_refs = refs[1:1 + n], refs[1 + n:1 + 2 * n], refs[1 + 2 * n:1 + 3 * n]
        loss_out = refs[1 + 3 * n]
        outs = refs[2 + 3 * n:]
        chip = 2 * lax.axis_index("x") + lax.axis_index("y")
        loss_out[...] = tot[_P_LOSS:_P_LOSS + 1, 0:1]

        def step(k, g, pick, put):
            d, m2, v2 = _adam_values(pick(w_refs[k]), pick(m_refs[k]), pick(v_refs[k]), g)
            for o, val in zip((outs[k], outs[n + k], outs[2 * n + k], outs[3 * n + k]), (g, d, m2, v2)):
                put(o, val)

        def whole(ref):
            return ref[0]

        def put_whole(ref, val):
            ref[0] = val

        for g in range(SGU_GROUPS):
            def pick_g(ref, g=g):
                return ref[0, g]

            def put_g(ref, val, g=g):
                ref[0, g] = val

            step(0, tot[_P_WSP + g * SGU_BLOCK:_P_WSP + (g + 1) * SGU_BLOCK], pick_g, put_g)
        step(1, tot[pl.ds(pl.multiple_of(_P_WGU + GLA_RANK * chip, GLA_RANK), GLA_RANK), :], whole, put_whole)
        for k, (base, chunks) in enumerate(((_P_NORM, 8), (_P_NORM + 8, 8), (_P_NORM + 16, 8), (_P_NORM + 24, 8), (_P_BG, 4))):
            for q in range(chunks):
                def pick_q(ref, q=q):
                    return ref[:, LANES * q:LANES * (q + 1)]

                def put_q(ref, val, q=q):
                    ref[:, LANES * q:LANES * (q + 1)] = val

                step(2 + k, tot[base + q:base + q + 1], pick_q, put_q)
        step(7, tot[_P_BSP:_P_BSP + SGU_GROUPS], whole, put_whole)
        for k in range(3):
            mine = tot[pl.ds(pl.multiple_of(_P_HEAD + 32 * k + 8 * chip, 8), 8), :]
            step(8 + k, mine[0:4, 0:64], whole, put_whole)

    shapes = [_sds(w.shape, F32) for w in ws]
    res = pl.pallas_call(
        body, name="small_adamw", in_specs=[_whole()] * (1 + 3 * n), out_specs=[_whole()] * (1 + 4 * n),
        out_shape=[_sds((1, 1), F32)] + shapes * 4,
        compiler_params=pltpu.CompilerParams(vmem_limit_bytes=VMEM_LIMIT_BYTES),
    )(tot, *ws, *ms, *vs)
    return res[0].reshape(()), [list(res[1 + i * n:1 + (i + 1) * n]) for i in range(4)]


def _w_in_pieces():
    blk = D_IN // N_CHIPS
    pieces = []
    for s in range(len(_IN_SPLITS)):
        lo_s, hi_s = _IN_STARTS[s], _IN_STARTS[s + 1]
        for j in range(N_CHIPS):
            lo, hi = max(lo_s, j * blk), min(hi_s, (j + 1) * blk)
            if lo < hi:
                pieces.append((j, lo - j * blk, _IN_DST[s] + lo - lo_s, hi - lo))
    return pieces


def _relayout_w_in(gathered):
    _, rows, blk = gathered.shape
    tr = 256

    def body(g_ref, o_ref):
        o_ref[:, OFF_AL:N_ALL] = jnp.zeros((tr, LANES), BF16)
        for j, src, dst, w in _w_in_pieces():
            o_ref[:, dst:dst + w] = g_ref[j, :, src:src + w]

    res, _ = _call(body, name="relayout_w_in", grid=(rows // tr,), parallel=True,
                   in_specs=[pl.BlockSpec((N_CHIPS, tr, blk), lambda i: (0, i, 0))],
                   out_specs=[pl.BlockSpec((tr, N_ALL), lambda i: (i, 0))],
                   out_shape=[_sds((rows, N_ALL), BF16)], args=(gathered,))
    return res[0]


def _update_row_tile(rows):
    for t in range(min(rows, 256), 7, -8):
        if rows % t == 0:
            return t
    return rows


def _my_half(first_ref, second_ref):
    return jnp.where(lax.axis_index("c") == 0, first_ref[...], second_ref[...])


def _presum_w_in(dws, theirs, row0, rows, name, job=None):
    hr = theirs[0].shape[0]
    blk = D_IN // N_CHIPS
    tr = 64
    assert row0 % tr == 0 and rows % tr == 0
    nh, t0 = hr // tr, row0 // tr
    n = len(dws)

    def body(*refs):
        dw_refs, q_refs, (o_ref, s_scr) = refs[:2 * n], refs[2 * n:3 * n], refs[3 * n:]
        for p, (a, off) in enumerate(dws):
            w = a.shape[1]
            s_scr[:, off:off + w] = (_my_half(dw_refs[2 * p], dw_refs[2 * p + 1]) + q_refs[p][...]).astype(BF16)
        for j, src, dst, w in _w_in_pieces():
            o_ref[j, :, src:src + w] = s_scr[:, dst:dst + w]

    in_specs, args = [], []
    for a, _ in dws:
        w = a.shape[1]
        in_specs += [pl.BlockSpec((tr, w), lambda i: (i + t0, 0)), pl.BlockSpec((tr, w), lambda i: (i + t0 + nh, 0))]
        args += [a, a]
    in_specs += [pl.BlockSpec((tr, q.shape[1]), lambda i: (i + t0, 0)) for q in theirs]
    res, jres = _call(body, name=name, grid=(rows // tr,), parallel=True, in_specs=in_specs,
                      out_specs=[pl.BlockSpec((N_CHIPS, tr, blk), lambda i: (0, i, 0))],
                      out_shape=[_sds((N_CHIPS, rows, blk), BF16)], scratch_shapes=[pltpu.VMEM((tr, N_ALL), BF16)],
                      args=(*args, *theirs), job=job)
    return res[0], jres


def _presum(dw, theirs, name):
    if dw.ndim == 4:
        _, _, hr, c = dw.shape
        tr = _update_row_tile(hr)
        first = pl.BlockSpec((1, 1, tr, c), lambda j, i: (j, 0, i, 0))
        second = pl.BlockSpec((1, 1, tr, c), lambda j, i: (j, 1, i, 0))
        other = pl.BlockSpec((1, tr, c), lambda j, i: (j, i, 0))
    else:
        hr, c = dw.shape[0] // 2, dw.shape[1] // N_CHIPS
        tr = _update_row_tile(hr)
        nh = hr // tr
        first = pl.BlockSpec((tr, c), lambda j, i: (i, j))
        second = pl.BlockSpec((tr, c), lambda j, i: (i + nh, j))
        other = pl.BlockSpec((tr, c), lambda j, i: (i, j))

    def body(a_ref, b_ref, q_ref, o_ref):
        mine = _my_half(a_ref, b_ref).reshape(tr, c)
        o_ref[...] = (mine + q_ref[...].reshape(tr, c)).astype(BF16).reshape(o_ref.shape)

    res, _ = _call(body, name=name, grid=(N_CHIPS, hr // tr), parallel=True, in_specs=[first, second, other],
                   out_specs=[pl.BlockSpec((1, tr, c), lambda j, i: (j, i, 0))],
                   out_shape=[_sds((N_CHIPS, hr, c), BF16)], args=(dw, dw, theirs))
    return res[0]


def _sum_slots(own, slots, name):
    rows, cols = own.shape
    tr = _update_row_tile(rows)

    def body(own_ref, s_ref, o_ref):
        acc = own_ref[...].astype(F32)
        for j in range(N_PEER):
            acc = acc + s_ref[j].astype(F32)
        o_ref[...] = acc

    res, _ = _call(body, name=name, grid=(rows // tr,), parallel=True,
                   in_specs=[pl.BlockSpec((tr, cols), lambda i: (i, 0)), pl.BlockSpec((N_PEER, tr, cols), lambda i: (0, i, 0))],
                   out_specs=[pl.BlockSpec((tr, cols), lambda i: (i, 0))], out_shape=[_sds((rows, cols), F32)],
                   args=(own, slots))
    return res[0]


def _adamw(w, m, v, g_mine, g_theirs, name, job=None):
    rows, cols = w.shape
    part_rows = [p.shape[0] for p in g_mine]
    assert sum(part_rows) == rows // 2 and [p.shape[0] for p in g_theirs] == part_rows
    tr = _update_row_tile(min(part_rows))
    assert all(r % tr == 0 for r in part_rows)
    nh = (rows // 2) // tr
    starts = [sum(part_rows[:k]) // tr for k in range(len(part_rows))]
    n_parts = len(part_rows)

    def body(w_ref, m_ref, v_ref, *rest):
        g_refs, (g_out, d_out, m_out, v_out) = rest[:-4], rest[-4:]
        step = pl.program_id(0)
        mine_here = (step // nh) == lax.axis_index("c")
        q = step % nh
        g = None
        for k in reversed(range(n_parts)):
            val = jnp.where(mine_here, g_refs[k][...], g_refs[n_parts + k][...])
            g = val if g is None else jnp.where(q < starts[k + 1], val, g)
        d, m2, v2 = _adam_values(w_ref[...], m_ref[...], v_ref[...], g)
        g_out[...] = g
        m_out[...] = m2
        v_out[...] = v2
        d_out[...] = d

    def g_spec(k):
        last = part_rows[k] // tr - 1
        return pl.BlockSpec((tr, cols), lambda i: (jnp.clip(i % nh - starts[k], 0, last), 0))

    spec = pl.BlockSpec((tr, cols), lambda i: (i, 0))
    return _call(body, name=name, grid=(rows // tr,), parallel=True,
                 in_specs=[spec] * 3 + [g_spec(k) for k in range(n_parts)] * 2, out_specs=[spec] * 4,
                 out_shape=[_sds((rows, cols), F32)] * 4, args=(w, m, v, *g_mine, *g_theirs), job=job)


def _transposed_cast(wt):
    cols, rows = wt.shape

    def body(x_ref, o_ref):
        o_ref[...] = jnp.transpose(x_ref[...]).astype(BF16)

    res, _ = _call(body, name="transpose_w_in", grid=(pl.cdiv(cols, LANES),), parallel=True,
                   in_specs=[pl.BlockSpec((LANES, rows), lambda j: (j, 0))],
                   out_specs=[pl.BlockSpec((rows, LANES), lambda j: (0, j))], out_shape=[_sds((rows, cols), BF16)],
                   args=(wt,))
    return res[0]


def _adamw_transposed(wt, mt, vt, g_mine, g_theirs, name):
    cols, rows = wt.shape
    n_parts = len(g_mine)

    def body(w_ref, m_ref, v_ref, *rest):
        g_refs, (g_out, d_out, m_out, v_out) = rest[:-4], rest[-4:]
        mine = jnp.concatenate([r[...] for r in g_refs[:n_parts]], axis=0)
        theirs = jnp.concatenate([r[...] for r in g_refs[n_parts:]], axis=0)
        first = lax.axis_index("c") == 0
        g = jnp.transpose(jnp.concatenate([jnp.where(first, mine, theirs), jnp.where(first, theirs, mine)], axis=0))
        d, m2, v2 = _adam_values(w_ref[...], m_ref[...], v_ref[...], g)
        g_out[...] = g
        m_out[...] = m2
        v_out[...] = v2
        d_out[...] = d

    spec = pl.BlockSpec((LANES, rows), lambda j: (j, 0))
    g_specs = [pl.BlockSpec((p.shape[0], LANES), lambda j: (0, j)) for p in g_mine] * 2
    res, _ = _call(body, name=name, grid=(pl.cdiv(cols, LANES),), parallel=True, in_specs=[spec] * 3 + g_specs,
                   out_specs=[spec] * 4, out_shape=[_sds((cols, rows), F32)] * 4, args=(wt, mt, vt, *g_mine, *g_theirs))
    return res


def _inproj_fwd(x, g1, w_all, job=None):
    T = x.shape[0]
    tT = _row_tile(T, 512)

    def body(x_ref, g_ref, w_ref, a_ref, proj_ref, alow_ref):
        xv = x_ref[...]
        a = (xv * _rms_stats(xv) * g_ref[...]).astype(BF16)
        a_ref[...] = a
        for j in range(N_MAIN // 1024):
            cols = slice(j * 1024, (j + 1) * 1024)
            proj_ref[:, cols] = _dot(a, w_ref[:, cols]).astype(BF16)
        alow_ref[...] = _dot(a, w_ref[:, N_MAIN:N_ALL])

    row = lambda w: pl.BlockSpec((tT, w), lambda i: (i, 0))
    return _call(
        body, name="inproj_fwd", grid=(T // tT,), parallel=True,
        in_specs=[row(D_MODEL), pl.BlockSpec((1, D_MODEL), lambda i: (0, 0)), _whole()],
        out_specs=[row(D_MODEL), row(N_MAIN), row(LANES)],
        out_shape=[_sds((T, D_MODEL), BF16), _sds((T, N_MAIN), BF16), _sds((T, LANES), F32)],
        args=(x, g1, w_all), job=job)


def _gla_decay_terms(al_ref, wgu_ref, bg_ref, tT):
    logit = _dot_f32(al_ref[...], wgu_ref[...]) + bg_ref[...]
    la = _log_sigmoid(logit) * (1.0 / GLA_TAU)
    delta = _dot_exact_lhs(_chunk_masks(tT, upper=True), la)
    return logit, la, delta


def _gla_fwd(proj, alow, wgu, b_gate, gn, job=None):
    T = proj.shape[0]
    tT = _row_tile(T, 512)
    nc = tT // CHUNK

    def body(q_ref, k_ref, v_ref, r_ref, al_ref, wgu_ref, bg_ref, gn_ref, y_ref, st_ref, s_scr):
        @pl.when(pl.program_id(0) == 0)
        def _():
            s_scr[...] = jnp.zeros_like(s_scr)

        _, la, delta = _gla_decay_terms(al_ref, wgu_ref, bg_ref, tT)
        kdec = (k_ref[...].astype(F32) * jnp.exp(delta)).astype(BF16)
        heads = range(GLA_HEADS)
        kcs = [slice(h * GLA_DK, (h + 1) * GLA_DK) for h in heads]
        vcs = [slice(h * GLA_DV, (h + 1) * GLA_DV) for h in heads]
        state = [s_scr[h] for h in heads]
        for c in range(nc):
            rows = slice(c * CHUNK, (c + 1) * CHUNK)
            first = slice(c * CHUNK, c * CHUNK + 1)
            dec = jnp.exp(la[first, :] + delta[first, :])
            upd_t = [_dot(v_ref[rows, vcs[h]], kdec[rows, kcs[h]], _TN) for h in heads]
            qs = [(q_ref[rows, kcs[h]].astype(F32) * (GLA_DK ** -0.5)).astype(BF16) for h in heads]
            for h in heads:
                state[h] = state[h] * dec[:, kcs[h]] + upd_t[h]
                st_ref[c, h] = state[h]
            o = [_dot(qs[h], state[h].astype(BF16), _NT) for h in heads]
            for h in heads:
                on = o[h] * _rms_stats(o[h]) * gn_ref[:, vcs[h]]
                rr = r_ref[rows, vcs[h]].astype(F32)
                y_ref[rows, vcs[h]] = (on * (rr * _sigmoid(rr))).astype(BF16)
        for h in heads:
            s_scr[h] = state[h]

    blk = lambda w, j: pl.BlockSpec((tT, w), lambda i: (i, j))
    return _call(
        body, name="gla_fwd", grid=(T // tT,),
        in_specs=[blk(512, 0), blk(512, 1), blk(1024, 1), blk(1024, 2), blk(LANES, 0), _whole(), _whole(), _whole()],
        out_specs=[pl.BlockSpec((tT, GLA_V), lambda i: (i, 0)),
                   pl.BlockSpec((nc, GLA_HEADS, GLA_DV, GLA_DK), lambda i: (i, 0, 0, 0))],
        out_shape=[_sds((T, GLA_V), BF16), _sds((T // CHUNK, GLA_HEADS, GLA_DV, GLA_DK), F32)],
        scratch_shapes=[pltpu.VMEM((GLA_HEADS, GLA_DV, GLA_DK), F32)],
        args=(proj, proj, proj, proj, alow, wgu, b_gate, gn), job=job)


def _sgu_mask():
    i = lax.broadcasted_iota(jnp.int32, (SGU_BLOCK, SGU_BLOCK), 0)
    j = lax.broadcasted_iota(jnp.int32, (SGU_BLOCK, SGU_BLOCK), 1)
    return lax.shift_right_logical(j, 6) <= lax.shift_right_logical(i, 6)


def _sgu_fwd(proj, ln_g, ln_b, w_sp, b_sp_t):
    T = proj.shape[0]
    tT = _row_tile(T, 512)
    nb = tT // SGU_BLOCK

    def body(su_ref, sv_ref, lg_ref, lb_ref, w_ref, b_ref, y_ref):
        mask = _sgu_mask()
        for g in range(SGU_GROUPS):
            gc = slice(g * SGU_DG, (g + 1) * SGU_DG)
            wm = jnp.where(mask, w_ref[g], 0.0).astype(BF16)
            vf = _gelu(sv_ref[:, gc].astype(F32))
            mu = jnp.mean(vf, axis=-1, keepdims=True)
            vc = vf - mu
            rstd = lax.rsqrt(jnp.mean(vc * vc, axis=-1, keepdims=True) + EPS)
            vn = (vc * rstd * lg_ref[:, gc] + lb_ref[:, gc]).astype(BF16)
            u = _gelu(su_ref[:, gc].astype(F32))
            for b in range(nb):
                rows = slice(b * SGU_BLOCK, (b + 1) * SGU_BLOCK)
                mixed = _dot(wm, vn[rows, :]) + b_ref[:, g:g + 1]
                y_ref[rows, gc] = (u[rows, :] * mixed).astype(BF16)

    blk = lambda j: pl.BlockSpec((tT, 1024), lambda i: (i, j))
    res, _ = _call(body, name="sgu_fwd", grid=(T // tT,), parallel=True,
                   in_specs=[blk(3), blk(4), _whole(), _whole(), _whole(), _whole()],
                   out_specs=[pl.BlockSpec((tT, 1024), lambda i: (i, 0))], out_shape=[_sds((T, 1024), BF16)],
                   args=(proj, proj, ln_g, ln_b, w_sp, b_sp_t))
    return res[0]


def _merge_fwd(x, proj, y_gla, y_sgu, w_bg, w_bs, w_o, g_pm, job=None):
    T = x.shape[0]
    tT = _row_tile(T, 512)

    def body(x_ref, gg_ref, gs_ref, yg_ref, ys_ref, wbg_ref, wbs_ref, wo_ref, g_ref,
             zg_ref, zs_ref, mg_ref, mix_ref, x1_ref):
        zg = _dot(yg_ref[...], wbg_ref[...])
        zs = _dot(ys_ref[...], wbs_ref[...])
        zg_ref[...] = zg.astype(BF16)
        zs_ref[...] = zs.astype(BF16)
        merged = (_sigmoid(gg_ref[...].astype(F32)) * zg + _sigmoid(gs_ref[...].astype(F32)) * zs).astype(BF16)
        mg_ref[...] = merged
        mix = _dot(merged, wo_ref[...])
        mix_ref[...] = mix
        x1_ref[...] = x_ref[...] + mix * _rms_stats(mix) * g_ref[...]

    row = pl.BlockSpec((tT, D_MODEL), lambda i: (i, 0))
    blk = lambda j: pl.BlockSpec((tT, 1024), lambda i: (i, j))
    sds = lambda dt: _sds((T, D_MODEL), dt)
    return _call(body, name="merge_fwd", grid=(T // tT,), parallel=True,
                 in_specs=[row, blk(5), blk(6), row, row, _whole(), _whole(), _whole(),
                           pl.BlockSpec((1, D_MODEL), lambda i: (0, 0))],
                 out_specs=[row] * 5, out_shape=[sds(BF16), sds(BF16), sds(BF16), sds(F32), sds(F32)],
                 args=(x, proj, proj, y_gla, y_sgu, w_bg, w_bs, w_o, g_pm), job=job)


def _ffn_fwd_bwd(x1, tgt, w_fi_top, w_fi_bot, w_fo, g_pf, g_po):
    T = x1.shape[0]
    tT = _row_tile(T, 256)
    half = D_FF // 2
    kh = D_MODEL // 2

    def body(x1_ref, t_ref, top_ref, bot_ref, wfo_ref, gpf_ref, gpo_ref,
             h_ref, f_ref, dgu_ref, dy_ref, dx1_ref, loss_ref, dgpf_ref, dgpo_ref, gu_scr):
        @pl.when(pl.program_id(0) == 0)
        def _():
            loss_ref[...] = jnp.zeros_like(loss_ref)
            dgpf_ref[...] = jnp.zeros_like(dgpf_ref)
            dgpo_ref[...] = jnp.zeros_like(dgpo_ref)

        x1v = x1_ref[...]
        r2 = _rms_stats(x1v)
        h = (x1v * r2 * gpf_ref[...]).astype(BF16)
        h_ref[...] = h
        y = jnp.zeros((tT, D_MODEL), F32)
        for j in range(2):
            gc = slice(j * half, (j + 1) * half)
            uc = slice(D_FF + j * half, D_FF + (j + 1) * half)
            gate = _dot(h[:, :kh], top_ref[j]) + _dot(h[:, kh:], bot_ref[j])
            up = _dot(h[:, :kh], top_ref[2 + j]) + _dot(h[:, kh:], bot_ref[2 + j])
            gu_scr[:, gc] = gate
            gu_scr[:, uc] = up
            f = (gate * _sigmoid(gate) * up).astype(BF16)
            f_ref[:, gc] = f
            y = y + _dot(f, wfo_ref[gc, :])
        r3 = _rms_stats(y)
        x2 = x1v + y * r3 * gpo_ref[...]
        err = x2 - t_ref[...]
        loss_ref[...] += jnp.sum(err * err) * (0.5 / D_MODEL)
        dx2 = err * (1.0 / D_MODEL)
        dy, dg = _rms_bwd(dx2, y, r3, gpo_ref[...])
        dgpo_ref[...] += jnp.sum(dg, axis=0, keepdims=True)
        dyb = dy.astype(BF16)
        dy_ref[...] = dyb
        dh_top = jnp.zeros((tT, kh), F32)
        dh_bot = jnp.zeros((tT, kh), F32)
        for j in range(2):
            gc = slice(j * half, (j + 1) * half)
            uc = slice(D_FF + j * half, D_FF + (j + 1) * half)
            df = _dot(dyb, wfo_ref[gc, :], _NT)
            gate = gu_scr[:, gc]
            up = gu_scr[:, uc]
            sg = _sigmoid(gate)
            dgate = (df * up * (sg * (1.0 + gate * (1.0 - sg)))).astype(BF16)
            dup = (df * (gate * sg)).astype(BF16)
            dgu_ref[:, gc] = dgate
            dgu_ref[:, uc] = dup
            dh_top = dh_top + _dot(dgate, top_ref[j], _NT) + _dot(dup, top_ref[2 + j], _NT)
            dh_bot = dh_bot + _dot(dgate, bot_ref[j], _NT) + _dot(dup, bot_ref[2 + j], _NT)
        dh = jnp.concatenate([dh_top, dh_bot], axis=1)
        dx1n, dg2 = _rms_bwd(dh, x1v, r2, gpf_ref[...])
        dgpf_ref[...] += jnp.sum(dg2, axis=0, keepdims=True)
        dx1_ref[...] = dx2 + dx1n

    row = lambda w: pl.BlockSpec((tT, w), lambda i: (i, 0))
    vec = pl.BlockSpec((1, D_MODEL), lambda i: (0, 0))
    res, _ = _call(
        body, name="ffn_fwd_bwd", grid=(T // tT,),
        in_specs=[row(D_MODEL), row(D_MODEL), _whole(), _whole(), _whole(), vec, vec],
        out_specs=[row(D_MODEL), row(D_FF), row(2 * D_FF), row(D_MODEL), row(D_MODEL),
                   pl.BlockSpec((1, LANES), lambda i: (0, 0)), vec, vec],
        out_shape=[_sds((T, D_MODEL), BF16), _sds((T, D_FF), BF16), _sds((T, 2 * D_FF), BF16), _sds((T, D_MODEL), BF16),
                   _sds((T, D_MODEL), F32), _sds((1, LANES), F32), _sds((1, D_MODEL), F32), _sds((1, D_MODEL), F32)],
        scratch_shapes=[pltpu.VMEM((tT, 2 * D_FF), F32)], args=(x1, tgt, w_fi_top, w_fi_bot, w_fo, g_pf, g_po))
    return res


def _merge_bwd(dx1, mix, proj, zg, zs, w_bg, w_bs, w_o, g_pm, job=None):
    T = dx1.shape[0]
    tT = _row_tile(T, 512)

    def body(dx1_ref, mix_ref, gg_ref, gs_ref, zg_ref, zs_ref, wbg_ref, wbs_ref, wo_ref, g_ref,
             dmix_ref, dzg_ref, dzs_ref, dgate_ref, dyg_ref, dys_ref, dgpm_ref):
        @pl.when(pl.program_id(0) == 0)
        def _():
            dgpm_ref[...] = jnp.zeros_like(dgpm_ref)

        mix = mix_ref[...]
        dmix, dg = _rms_bwd(dx1_ref[...], mix, _rms_stats(mix), g_ref[...])
        dgpm_ref[...] += jnp.sum(dg, axis=0, keepdims=True)
        dmb = dmix.astype(BF16)
        dmix_ref[...] = dmb
        dmerged = _dot(dmb, wo_ref[...], _NT)
        for k, (gate_ref, z_ref, w_ref, dz_ref, dy_ref) in enumerate((
                (gg_ref, zg_ref, wbg_ref, dzg_ref, dyg_ref), (gs_ref, zs_ref, wbs_ref, dzs_ref, dys_ref))):
            sg = _sigmoid(gate_ref[...].astype(F32))
            dz = (dmerged * sg).astype(BF16)
            dz_ref[...] = dz
            dgate_ref[:, k * 1024:(k + 1) * 1024] = (dmerged * z_ref[...].astype(F32) * (sg * (1.0 - sg))).astype(BF16)
            dy_ref[...] = _dot(dz, w_ref[...], _NT).astype(BF16)

    row = pl.BlockSpec((tT, D_MODEL), lambda i: (i, 0))
    blk = lambda j: pl.BlockSpec((tT, 1024), lambda i: (i, j))
    vec = pl.BlockSpec((1, D_MODEL), lambda i: (0, 0))
    sds = _sds((T, D_MODEL), BF16)
    return _call(
        body, name="merge_bwd", grid=(T // tT,),
        in_specs=[row, row, blk(5), blk(6), row, row, _whole(), _whole(), _whole(), vec],
        out_specs=[row, row, row, pl.BlockSpec((tT, W_MRG), lambda i: (i, 0)), row, row, vec],
        out_shape=[sds, sds, sds, _sds((T, W_MRG), BF16), sds, sds, _sds((1, D_MODEL), F32)],
        args=(dx1, mix, proj, proj, zg, zs, w_bg, w_bs, w_o, g_pm), job=job)


def _sgu_bwd(proj, dy_sgu, ln_g, ln_b, w_sp, b_sp_t, job=None):
    T = proj.shape[0]
    tT = _row_tile(T, 512)
    nb = tT // SGU_BLOCK

    def body(su_ref, sv_ref, dy_ref, lg_ref, lb_ref, w_ref, b_ref, dp_ref, dw_ref, dbt_ref, dlg_ref, dlb_ref):
        @pl.when(pl.program_id(0) == 0)
        def _():
            dw_ref[...] = jnp.zeros_like(dw_ref)
            dbt_ref[...] = jnp.zeros_like(dbt_ref)
            dlg_ref[...] = jnp.zeros_like(dlg_ref)
            dlb_ref[...] = jnp.zeros_like(dlb_ref)

        mask = _sgu_mask()
        lane = lax.broadcasted_iota(jnp.int32, (SGU_BLOCK, LANES), 1)
        for g in range(SGU_GROUPS):
            gc = slice(g * SGU_DG, (g + 1) * SGU_DG)
            gc_v = slice(1024 + g * SGU_DG, 1024 + (g + 1) * SGU_DG)
            wm = jnp.where(mask, w_ref[g], 0.0).astype(BF16)
            vf, dvf_dsv = _gelu_and_grad(sv_ref[:, gc].astype(F32))
            mu = jnp.mean(vf, axis=-1, keepdims=True)
            vc = vf - mu
            rstd = lax.rsqrt(jnp.mean(vc * vc, axis=-1, keepdims=True) + EPS)
            vhat = vc * rstd
            vn = (vhat * lg_ref[:, gc] + lb_ref[:, gc]).astype(BF16)
            u, du_dsu = _gelu_and_grad(su_ref[:, gc].astype(F32))
            dy = dy_ref[:, gc].astype(F32)
            dmixed = (dy * u).astype(BF16)
            dvn_parts = []
            dw_acc = jnp.zeros((SGU_BLOCK, SGU_BLOCK), F32)
            db_acc = jnp.zeros((SGU_BLOCK, 1), F32)
            for b in range(nb):
                rows = slice(b * SGU_BLOCK, (b + 1) * SGU_BLOCK)
                mixed = _dot(wm, vn[rows, :]) + b_ref[:, g:g + 1]
                dp_ref[rows, gc] = (dy[rows, :] * mixed * du_dsu[rows, :]).astype(BF16)
                dvn_parts.append(_dot(wm, dmixed[rows, :], _TN))
                dw_acc = dw_acc + _dot(dmixed[rows, :], vn[rows, :], _NT)
                db_acc = db_acc + jnp.sum(dmixed[rows, :].astype(F32), axis=-1, keepdims=True)
            dw_ref[g] += jnp.where(mask, dw_acc, 0.0)
            dbt_ref[...] += jnp.where(lane == g, db_acc, 0.0)
            dvn = jnp.concatenate(dvn_parts, axis=0)
            dlg_ref[:, gc] += jnp.sum(dvn * vhat, axis=0, keepdims=True)
            dlb_ref[:, gc] += jnp.sum(dvn, axis=0, keepdims=True)
            dvh = dvn * lg_ref[:, gc]
            dvf = rstd * (dvh - jnp.mean(dvh, axis=-1, keepdims=True)
                          - vhat * jnp.mean(dvh * vhat, axis=-1, keepdims=True))
            dp_ref[:, gc_v] = (dvf * dvf_dsv).astype(BF16)

    blk = lambda j: pl.BlockSpec((tT, 1024), lambda i: (i, j))
    row = lambda w: pl.BlockSpec((tT, w), lambda i: (i, 0))
    vec = pl.BlockSpec((1, 1024), lambda i: (0, 0))
    return _call(
        body, name="sgu_bwd", grid=(T // tT,),
        in_specs=[blk(3), blk(4), row(1024), _whole(), _whole(), _whole(), _whole()],
        out_specs=[row(W_SGU), pl.BlockSpec((SGU_GROUPS, SGU_BLOCK, SGU_BLOCK), lambda i: (0, 0, 0)),
                   pl.BlockSpec((SGU_BLOCK, LANES), lambda i: (0, 0)), vec, vec],
        out_shape=[_sds((T, W_SGU), BF16), _sds((SGU_GROUPS, SGU_BLOCK, SGU_BLOCK), F32), _sds((SGU_BLOCK, LANES), F32),
                   _sds((1, 1024), F32), _sds((1, 1024), F32)],
        args=(proj, proj, dy_sgu, ln_g, ln_b, w_sp, b_sp_t), job=job)


def _gla_bwd(proj, alow, wgu, b_gate, gn, states, dy_gla, job=None):
    T = proj.shape[0]
    tT = _row_tile(T, 512)
    nc = tT // CHUNK
    nt = T // tT

    def body(q_ref, k_ref, v_ref, r_ref, al_ref, wgu_ref, bg_ref, gn_ref, st_ref, sp_ref, dy_ref,
             dp_ref, dal_ref, dgn_ref, dbg_ref, dwgu_ref, g_scr, dd_scr, dt_scr):
        step = pl.program_id(0)

        @pl.when(step == 0)
        def _():
            g_scr[...] = jnp.zeros_like(g_scr)
            dgn_ref[...] = jnp.zeros_like(dgn_ref)
            dbg_ref[...] = jnp.zeros_like(dbg_ref)
            dwgu_ref[...] = jnp.zeros_like(dwgu_ref)

        has_prev = jnp.where(step == nt - 1, 0.0, 1.0)
        logit, la, delta = _gla_decay_terms(al_ref, wgu_ref, bg_ref, tT)
        e = jnp.exp(delta)
        kdec_f = k_ref[...].astype(F32) * e
        kdec = kdec_f.astype(BF16)
        heads = range(GLA_HEADS)
        kcs = [slice(h * GLA_DK, (h + 1) * GLA_DK) for h in heads]
        vcs = [slice(h * GLA_DV, (h + 1) * GLA_DV) for h in heads]
        carry = [g_scr[h] for h in heads]
        dgn_acc = [jnp.zeros((1, GLA_DV), F32) for _ in heads]
        for c in reversed(range(nc)):
            rows = slice(c * CHUNK, (c + 1) * CHUNK)
            first = slice(c * CHUNK, c * CHUNK + 1)
            dec = jnp.exp(la[first, :] + delta[first, :])
            s_b = [st_ref[c, h].astype(BF16) for h in heads]
            qs = [(q_ref[rows, kcs[h]].astype(F32) * (GLA_DK ** -0.5)).astype(BF16) for h in heads]
            o = [_dot(qs[h], s_b[h], _NT) for h in heads]
            do = []
            for h in heads:
                rstd = _rms_stats(o[h])
                ohat = o[h] * rstd
                gnh = gn_ref[:, vcs[h]]
                dy = dy_ref[rows, vcs[h]].astype(F32)
                rr = r_ref[rows, vcs[h]].astype(F32)
                sg = _sigmoid(rr)
                don = dy * (rr * sg)
                dp_ref[rows, OFF_R + h * GLA_DV:OFF_R + (h + 1) * GLA_DV] = (
                    dy * (ohat * gnh) * (sg * (1.0 + rr * (1.0 - sg)))).astype(BF16)
                dgn_acc[h] = dgn_acc[h] + jnp.sum(don * ohat, axis=0, keepdims=True)
                dn = don * gnh
                do.append((rstd * (dn - ohat * jnp.mean(dn * ohat, axis=-1, keepdims=True))).astype(BF16))
            dq = [_dot(do[h], s_b[h]) for h in heads]
            g_t = [_dot(do[h], qs[h], _TN) + carry[h] for h in heads]
            g_b = [g_t[h].astype(BF16) for h in heads]
            dv = [_dot(kdec[rows, kcs[h]], g_b[h], _NT) for h in heads]
            dkdec = [_dot(v_ref[rows, vcs[h]], g_b[h]) for h in heads]
            for h in heads:
                s_prev = st_ref[c - 1, h] if c > 0 else sp_ref[0, h] * has_prev
                ddec = jnp.sum(g_t[h] * s_prev, axis=0, keepdims=True)
                carry[h] = g_t[h] * dec[:, kcs[h]]
                dp_ref[rows, OFF_Q + h * GLA_DK:OFF_Q + (h + 1) * GLA_DK] = (dq[h] * (GLA_DK ** -0.5)).astype(BF16)
                dp_ref[rows, OFF_V + h * GLA_DV:OFF_V + (h + 1) * GLA_DV] = dv[h].astype(BF16)
                dp_ref[rows, OFF_K + h * GLA_DK:OFF_K + (h + 1) * GLA_DK] = (dkdec[h] * e[rows, kcs[h]]).astype(BF16)
                dd_scr[rows, kcs[h]] = dkdec[h] * kdec_f[rows, kcs[h]]
                dt_scr[rows, kcs[h]] = jnp.broadcast_to(ddec * dec[:, kcs[h]], (CHUNK, GLA_DK))
        for h in heads:
            g_scr[h] = carry[h]
            dgn_ref[:, vcs[h]] += dgn_acc[h]
        dla = _dot_exact_lhs(_chunk_masks(tT, upper=False), dd_scr[...]) + dt_scr[...]
        dlogit = dla * (1.0 / GLA_TAU) * _sigmoid(-logit)
        dbg_ref[...] += jnp.sum(dlogit, axis=0, keepdims=True)
        dwgu_ref[...] += _dot_f32(al_ref[...], dlogit, _TN)
        dal_ref[...] = _dot_f32(dlogit, wgu_ref[...], _NT).astype(BF16)

    rev = lambda i: nt - 1 - i
    blk = lambda w, j: pl.BlockSpec((tT, w), lambda i: (rev(i), j))
    st_blk = pl.BlockSpec((nc, GLA_HEADS, GLA_DV, GLA_DK), lambda i: (rev(i), 0, 0, 0))
    sp_blk = pl.BlockSpec((1, GLA_HEADS, GLA_DV, GLA_DK), lambda i: (jnp.maximum(rev(i) * nc - 1, 0), 0, 0, 0))
    return _call(
        body, name="gla_bwd", grid=(nt,),
        in_specs=[blk(512, 0), blk(512, 1), blk(1024, 1), blk(1024, 2), blk(LANES, 0), _whole(), _whole(), _whole(),
                  st_blk, sp_blk, blk(GLA_V, 0)],
        out_specs=[blk(W_GLA, 0), blk(LANES, 0), pl.BlockSpec((1, GLA_V), lambda i: (0, 0)),
                   pl.BlockSpec((1, GLA_QK), lambda i: (0, 0)), pl.BlockSpec((LANES, GLA_QK), lambda i: (0, 0))],
        out_shape=[_sds((T, W_GLA), BF16), _sds((T, LANES), BF16), _sds((1, GLA_V), F32), _sds((1, GLA_QK), F32),
                   _sds((LANES, GLA_QK), F32)],
        scratch_shapes=[pltpu.VMEM((GLA_HEADS, GLA_DV, GLA_DK), F32), pltpu.VMEM((tT, GLA_QK), F32),
                        pltpu.VMEM((tT, GLA_QK), F32)],
        args=(proj, proj, proj, proj, alow, wgu, b_gate, gn, states, states, dy_gla), job=job)


def _inproj_bwd(x, dx1, g1, w_all, dparts, job=None):
    T = x.shape[0]
    tT = _row_tile(T, 512)
    offs = (0, W_GLA, W_GLA + W_SGU, N_MAIN)

    def body(x_ref, dx1_ref, g_ref, w_ref, *rest):
        part_refs, (dx_ref, dg_ref) = rest[:len(offs)], rest[len(offs):]

        @pl.when(pl.program_id(0) == 0)
        def _():
            dg_ref[...] = jnp.zeros_like(dg_ref)

        da = jnp.zeros((tT, D_MODEL), F32)
        for off, p_ref in zip(offs, part_refs):
            da = da + _dot(p_ref[...], w_ref[:, off:off + p_ref.shape[1]], _NT)
        xv = x_ref[...]
        dx, dg = _rms_bwd(da, xv, _rms_stats(xv), g_ref[...])
        dg_ref[...] += jnp.sum(dg, axis=0, keepdims=True)
        dx_ref[...] = dx1_ref[...] + dx

    row = lambda w: pl.BlockSpec((tT, w), lambda i: (i, 0))
    vec = pl.BlockSpec((1, D_MODEL), lambda i: (0, 0))
    return _call(
        body, name="inproj_bwd", grid=(T // tT,),
        in_specs=[row(D_MODEL), row(D_MODEL), vec, _whole()] + [row(p.shape[1]) for p in dparts],
        out_specs=[row(D_MODEL), vec], out_shape=[_sds((T, D_MODEL), F32), _sds((1, D_MODEL), F32)],
        args=(x, dx1, g1, w_all, *dparts), job=job)


def _tn_matmul(a, b, name, job=None):
    T, M = a.shape
    N = b.shape[1]
    tk = _row_tile(T, 1024)
    tm = M if M <= 1024 else 1408
    tn = N if N <= 3072 else N // 2
    assert M % tm == 0 and N % tn == 0

    def body(a_ref, b_ref, o_ref):
        @pl.when(pl.program_id(2) == 0)
        def _():
            o_ref[...] = _dot(a_ref[...], b_ref[...], _TN)

        @pl.when(pl.program_id(2) > 0)
        def _():
            o_ref[...] += _dot(a_ref[...], b_ref[...], _TN)

    res, jres = _call(
        body, name=name, grid=(M // tm, N // tn, T // tk),
        in_specs=[pl.BlockSpec((tk, tm), lambda i, j, k: (k, i)), pl.BlockSpec((tk, tn), lambda i, j, k: (k, j))],
        out_specs=[pl.BlockSpec((tm, tn), lambda i, j, k: (i, j))], out_shape=[_sds((M, N), F32)], args=(a, b), job=job)
    return res[0], jres


def _pad_rows(a, rows=8):
    return jnp.pad(a, ((0, rows - a.shape[0]), (0, LANES - a.shape[1])))


def _halves_view(dw):
    r = dw.shape[0] // N_CHIPS
    return dw.reshape(N_CHIPS, 2, r // 2, dw.shape[1])


def kernel(x, norm_pre_mix, w_in, w_gate_up, b_gate, gla_norm, sgu_ln_g, sgu_ln_b, w_spatial, b_spatial, w_branch_gla, w_branch_sgu, w_out, norm_post_mix, norm_pre_ffn, w_ffn_in, w_ffn_out, norm_post_ffn, loss_target, m_norm_pre_mix, m_w_in, m_w_gate_up, m_b_gate, m_gla_norm, m_sgu_ln_g, m_sgu_ln_b, m_w_spatial, m_b_spatial, m_w_branch_gla, m_w_branch_sgu, m_w_out, m_norm_post_mix, m_norm_pre_ffn, m_w_ffn_in, m_w_ffn_out, m_norm_post_ffn, v_norm_pre_mix, v_w_in, v_w_gate_up, v_b_gate, v_gla_norm, v_sgu_ln_g, v_sgu_ln_b, v_w_spatial, v_b_spatial, v_w_branch_gla, v_w_branch_sgu, v_w_out, v_norm_post_mix, v_norm_pre_ffn, v_w_ffn_in, v_w_ffn_out, v_norm_post_ffn):
    chip = 2 * lax.axis_index("x") + lax.axis_index("y")
    xt, tgt = x[0], loss_target[0]

    tiny = jnp.concatenate([w_gate_up[0], _pad_rows(gla_norm[0]), _pad_rows(sgu_ln_g[0]), _pad_rows(sgu_ln_b[0]),
                            jnp.zeros((8, LANES), F32)], axis=0)
    def with_own(gathered, own):
        if gathered.ndim == 3:
            return lax.dynamic_update_slice(gathered, own[None], (chip, 0, 0))
        return lax.dynamic_update_slice(gathered, own, (0, chip * own.shape[1]))

    w_in_t, m_in_t, v_in_t = w_in[0].T, m_w_in[0].T, v_w_in[0].T
    w_in_b = _transposed_cast(w_in_t)
    g_in, g_tiny = _run_job(_job_gather([w_in_b, tiny], [False, False]), "gather_w_in")
    g_tiny = with_own(g_tiny, tiny)
    w_all = _relayout_w_in(with_own(g_in, w_in_b))
    cols = lambda a: a.transpose(1, 0, 2).reshape(a.shape[1], N_CHIPS * a.shape[2])
    wgu = jnp.pad(cols(g_tiny[:, 0:16]), ((0, LANES - GLA_RANK), (0, 0)))
    gn = cols(g_tiny[:, 16:20, :64]).reshape(1, GLA_V)
    ln_g = cols(g_tiny[:, 24:28, :64]).reshape(1, 1024)
    ln_b = cols(g_tiny[:, 32:36, :64]).reshape(1, 1024)
    b_sp_t = jnp.pad(b_spatial[0].T, ((0, 0), (0, LANES - SGU_GROUPS)))
    w_sp = w_spatial[0]

    own_rows = [w_branch_gla[0].astype(BF16), w_branch_sgu[0].astype(BF16), w_out[0].astype(BF16), w_ffn_out[0].astype(BF16)]
    (a, proj, alow), g_rows = _inproj_fwd(xt, norm_pre_mix, w_all, job=_job_gather(own_rows, [False] * 4))
    rows = lambda g: g.reshape(N_CHIPS * g.shape[1], g.shape[2])
    w_bg, w_bs, w_o, w_fo = [rows(with_own(g, own)) for g, own in zip(g_rows, own_rows)]
    w_fi_b = w_ffn_in[0].astype(BF16)
    fi_top, fi_bot = w_fi_b[:D_MODEL // 2], w_fi_b[D_MODEL // 2:]
    (y_gla, states), (g_top,) = _gla_fwd(proj, alow, wgu, b_gate, gn, job=_job_gather([fi_top], [False]))
    y_sgu = _sgu_fwd(proj, ln_g, ln_b, w_sp, b_sp_t)
    (zg, zs, merged, mix, x1), (g_bot,) = _merge_fwd(xt, proj, y_gla, y_sgu, w_bg, w_bs, w_o, norm_post_mix,
                                                     job=_job_gather([fi_bot], [False]))
    h, f, dgu, dy, dx1, loss, d_gpf, d_gpo = _ffn_fwd_bwd(x1, tgt, with_own(g_top, fi_top), with_own(g_bot, fi_bot),
                                                          w_fo, norm_pre_ffn, norm_post_ffn)

    own_part = lambda c: lax.dynamic_index_in_dim(c, chip, 0, keepdims=False)
    whole = lambda hs: [[(h_, None)] for h_ in hs]
    dw_fo, _ = _tn_matmul(f, dy, "dw_ffn_out")
    dw_fo4 = _halves_view(dw_fo)
    dw_fi, (q_fo,) = _tn_matmul(h, dgu, "dw_ffn_in", job=_job_to_other_core([[(dw_fo4, 0)]]))
    c_fo = _presum(dw_fo4, q_fo, "presum_ffn_out")
    (dmix, dzg, dzs, dp_mrg, dyg, dys, d_gpm), (s_fo, q_fi) = _merge_bwd(
        dx1, mix, proj, zg, zs, w_bg, w_bs, w_o, norm_post_mix,
        job=_join(_job_scatter([c_fo]), _job_to_other_core([[(dw_fi, 0)]])))
    c_fi = _presum(dw_fi, q_fi, "presum_ffn_in")
    dw_c, _ = _tn_matmul(a, dp_mrg, "dw_in_merge")
    dw_o4 = _halves_view(_tn_matmul(merged, dmix, "dw_out")[0])
    dw_bg4 = _halves_view(_tn_matmul(y_gla, dzg, "dw_branch_gla")[0])
    dw_bs4 = _halves_view(_tn_matmul(y_sgu, dzs, "dw_branch_sgu")[0])
    (dp_sgu, d_wsp, d_bsp_t, d_lng, d_lnb), (q_o, q_bg, q_bs, q_c) = _sgu_bwd(
        proj, dys, ln_g, ln_b, w_sp, b_sp_t,
        job=_job_to_other_core([[(dw_o4, 0)], [(dw_bg4, 0)], [(dw_bs4, 0)], [(dw_c, 0)]]))
    c_o, c_bg, c_bs = (_presum(dw_o4, q_o, "presum_out"), _presum(dw_bg4, q_bg, "presum_branch_gla"),
                       _presum(dw_bs4, q_bs, "presum_branch_sgu"))
    h_fo = _sum_slots(own_part(c_fo), s_fo, "sum_ffn_out")
    dw_b, _ = _tn_matmul(a, dp_sgu, "dw_in_sgu")
    (dp_gla, dal, d_gn, d_bg, d_wgu), (s_fi, t_fo, q_b) = _gla_bwd(
        proj, alow, wgu, b_gate, gn, states, dyg,
        job=_join(_job_scatter([c_fi]), _job_to_other_core(whole([h_fo]) + [[(dw_b, 0)]])))
    h_fi = _sum_slots(own_part(c_fi), s_fi, "sum_ffn_in")
    dw_d, _ = _tn_matmul(a, dal, "dw_in_gate")
    dw_a, (s_o, s_bg, s_bs, t_fi, q_d) = _tn_matmul(
        a, dp_gla, "dw_in_gla",
        job=_join(_job_scatter([c_o, c_bg, c_bs]), _job_to_other_core(whole([h_fi]) + [[(dw_d, 0)]])))
    h_o, h_bg, h_bs = (_sum_slots(own_part(c_o), s_o, "sum_out"), _sum_slots(own_part(c_bg), s_bg, "sum_branch_gla"),
                       _sum_slots(own_part(c_bs), s_bs, "sum_branch_sgu"))

    grads, deltas, new_m, new_v = {}, {}, {}, {}

    def update(name, w, m, v, g_mine, g_theirs, job=None):
        (g, d, m2, v2), jres = _adamw(w[0], m[0], v[0], g_mine, g_theirs, "adamw_" + name, job=job)
        grads[name], deltas[name], new_m[name], new_v[name] = g[None], d[None], m2[None], v2[None]
        return jres

    dw_in = [(dw_a, 0), (dw_b, W_GLA), (dw_c, W_GLA + W_SGU), (dw_d, N_MAIN)]
    q_a, t_o, t_bg, t_bs = update("w_ffn_out", w_ffn_out, m_w_ffn_out, v_w_ffn_out, [h_fo], [t_fo],
                                  job=_job_to_other_core([[(dw_a, 0)]] + whole([h_o, h_bg, h_bs])))
    q_in = [q_a, q_b, q_c, q_d]
    hr_in = D_MODEL // 2
    c_in_a, _ = _presum_w_in(dw_in, q_in, 0, hr_in // 8, "presum_w_in_a")
    c_in_b, (s_in_a,) = _presum_w_in(dw_in, q_in, hr_in // 8, 7 * hr_in // 8, "presum_w_in_b",
                                     job=_job_scatter([c_in_a]))
    update("w_ffn_in", w_ffn_in, m_w_ffn_in, v_w_ffn_in, [h_fi], [t_fi])
    update("w_out", w_out, m_w_out, v_w_out, [h_o], [t_o])
    update("w_branch_gla", w_branch_gla, m_w_branch_gla, v_w_branch_gla, [h_bg], [t_bg])
    update("w_branch_sgu", w_branch_sgu, m_w_branch_sgu, v_w_branch_sgu, [h_bs], [t_bs])
    (grad_x, d_g1), (s_in_b,) = _inproj_bwd(xt, dx1, norm_pre_mix, w_all, (dp_gla, dp_sgu, dp_mrg, dal),
                                            job=_job_scatter([c_in_b]))
    h_in = [_sum_slots(own_part(c_in_a), s_in_a, "sum_w_in_a"), _sum_slots(own_part(c_in_b), s_in_b, "sum_w_in_b")]
    t_in = _run_job(_job_to_other_core(whole(h_in)), "swap_w_in")
    for store, val in zip((grads, deltas, new_m, new_v),
                          _adamw_transposed(w_in_t, m_in_t, v_in_t, h_in, t_in, "adamw_w_in")):
        store["w_in"] = val.T[None]

    small_names = ["w_spatial", "w_gate_up", "norm_pre_mix", "norm_post_mix", "norm_pre_ffn", "norm_post_ffn", "b_gate",
                   "b_spatial", "gla_norm", "sgu_ln_g", "sgu_ln_b"]
    loss_out, small = _small_adamw(
        _small_sum([d_wsp, d_wgu, d_g1, d_gpm, d_gpf, d_gpo, d_bg, d_bsp_t, d_gn, d_lng, d_lnb, loss]),
        [w_spatial, w_gate_up, norm_pre_mix, norm_post_mix, norm_pre_ffn, norm_post_ffn, b_gate, b_spatial, gla_norm,
         sgu_ln_g, sgu_ln_b],
        [m_w_spatial, m_w_gate_up, m_norm_pre_mix, m_norm_post_mix, m_norm_pre_ffn, m_norm_post_ffn, m_b_gate,
         m_b_spatial, m_gla_norm, m_sgu_ln_g, m_sgu_ln_b],
        [v_w_spatial, v_w_gate_up, v_norm_pre_mix, v_norm_post_mix, v_norm_pre_ffn, v_norm_post_ffn, v_b_gate,
         v_b_spatial, v_gla_norm, v_sgu_ln_g, v_sgu_ln_b])
    for store, vals in zip((grads, deltas, new_m, new_v), small):
        store.update(zip(small_names, vals))

    order = ["norm_pre_mix", "w_in", "w_gate_up", "b_gate", "gla_norm", "sgu_ln_g", "sgu_ln_b", "w_spatial", "b_spatial",
             "w_branch_gla", "w_branch_sgu", "w_out", "norm_post_mix", "norm_pre_ffn", "w_ffn_in", "w_ffn_out",
             "norm_post_ffn"]
    out = [loss_out, grad_x[None]]
    for store in (grads, deltas, new_m, new_v):
        out.extend(store[n] for n in order)
    return tuple(out)
```

```python
import jax
import jax.numpy as jnp
from jax import lax
from jax.experimental import pallas as pl
from jax.experimental.pallas import tpu as pltpu

F32 = jnp.float32
BF16 = jnp.bfloat16

D_MODEL = 1024
GLA_HEADS = 4
GLA_DK = 128
GLA_DV = 256
GLA_QK = GLA_HEADS * GLA_DK
GLA_V = GLA_HEADS * GLA_DV
GLA_RANK = 16
GLA_TAU = 16.0
CHUNK = 64
SGU_GROUPS = 4
SGU_BLOCK = 128
SGU_DG = 256
D_FF = 2816
EPS = 1e-6
LANES = 128

OFF_Q, OFF_K, OFF_V, OFF_R, OFF_SU, OFF_SV, OFF_GG, OFF_GS, OFF_AL = 0, 512, 1024, 2048, 3072, 4096, 5120, 6144, 7168
W_GLA, W_SGU, W_MRG = 3072, 2048, 2048
N_MAIN = 7168
N_ALL = N_MAIN + LANES
_IN_SPLITS = (GLA_QK, GLA_QK, GLA_V, GLA_V, GLA_RANK, 1024, 1024, 1024, 1024)
_IN_STARTS = tuple(sum(_IN_SPLITS[:i]) for i in range(len(_IN_SPLITS) + 1))
_IN_DST = (OFF_Q, OFF_K, OFF_V, OFF_R, OFF_AL, OFF_SU, OFF_SV, OFF_GG, OFF_GS)
D_IN = _IN_STARTS[-1]

ADAM_LR = 0.001
ADAM_B1 = 0.9
ADAM_B2 = 0.999
ADAM_EPS = 1e-08
ADAM_WD = 0.01
ADAM_STEP = 10

VMEM_LIMIT_BYTES = 56 * 1024 * 1024
N_CHIPS = 4
N_PEER = N_CHIPS - 1
N_DEV = 8
MESH = pl.DeviceIdType.MESH

_NN = (((1,), (0,)), ((), ()))
_NT = (((1,), (1,)), ((), ()))
_TN = (((0,), (0,)), ((), ()))


def _dot(a, b, dims=_NN):
    return lax.dot_general(a, b, dims, preferred_element_type=F32)


def _split(x):
    hi = x.astype(BF16)
    lo = (x - hi.astype(F32)).astype(BF16)
    return hi, lo


def _dot_f32(a, b, dims=_NN):
    ah, al = _split(a)
    bh, bl = _split(b)
    return _dot(ah, bh, dims) + (_dot(al, bh, dims) + _dot(ah, bl, dims))


def _dot_exact_lhs(m, x):
    xh, xl = _split(x)
    return _dot(m, xh) + _dot(m, xl)


def _sigmoid(x):
    return 0.5 * jnp.tanh(0.5 * x) + 0.5


def _log_sigmoid(x):
    return jnp.minimum(x, 0.0) - jnp.log(1.0 + jnp.exp(-jnp.abs(x)))


_GELU_C = 0.7978845608028654
_GELU_A = 0.044715


def _gelu_and_grad(x):
    x2 = x * x
    t = jnp.tanh(_GELU_C * (x + _GELU_A * x * x2))
    g = 0.5 * x * (1.0 + t)
    dg = 0.5 * (1.0 + t) + 0.5 * x * (1.0 - t * t) * (_GELU_C * (1.0 + 3.0 * _GELU_A * x2))
    return g, dg


def _gelu(x):
    t = jnp.tanh(_GELU_C * (x + _GELU_A * x * x * x))
    return 0.5 * x * (1.0 + t)


def _rms_stats(x):
    return lax.rsqrt(jnp.mean(x * x, axis=-1, keepdims=True) + EPS)


def _rms_bwd(dout, y, r, g):
    yhat = y * r
    dn = dout * g
    dy = r * (dn - yhat * jnp.mean(dn * yhat, axis=-1, keepdims=True))
    return dy, dout * yhat


def _whole():
    return pl.BlockSpec(memory_space=pltpu.VMEM)


def _row_tile(T, want):
    t = min(T, want)
    assert T % t == 0
    return t


def _chunk_masks(tT, upper):
    row = lax.broadcasted_iota(jnp.int32, (tT, tT), 0)
    col = lax.broadcasted_iota(jnp.int32, (tT, tT), 1)
    same = (row // CHUNK) == (col // CHUNK)
    tri = (col > row) if upper else (col < row)
    return jnp.where(same & tri, 1.0, 0.0).astype(BF16)


class _Job:
    def __init__(self, ins, out_shapes, scratch, start, finish):
        self.ins, self.out_shapes, self.scratch, self.start, self.finish = list(ins), list(out_shapes), list(scratch), start, finish


def _join(*jobs):
    def split(refs, counts):
        out, at = [], 0
        for n in counts:
            out.append(refs[at:at + n])
            at += n
        return out

    ni, no, ns = [len(j.ins) for j in jobs], [len(j.out_shapes) for j in jobs], [len(j.scratch) for j in jobs]

    def start(ins, outs, scr):
        for j, a, b, c in zip(jobs, split(ins, ni), split(outs, no), split(scr, ns)):
            j.start(a, b, c)

    def finish(ins, outs, scr):
        for j, a, b, c in zip(jobs, split(ins, ni), split(outs, no), split(scr, ns)):
            j.finish(a, b, c)

    return _Job(sum((j.ins for j in jobs), []), sum((j.out_shapes for j in jobs), []),
                sum((j.scratch for j in jobs), []), start, finish)


def _mesh_pos():
    return lax.axis_index("x"), lax.axis_index("y"), lax.axis_index("c")


def _peer_chips(xi, yi):
    return [(1 - xi, yi), (xi, 1 - yi), (1 - xi, 1 - yi)]


def _half(ci, rows):
    return pl.ds(pl.multiple_of(ci * rows, 8), rows)


def _sds(shape, dtype):
    return jax.ShapeDtypeStruct(tuple(shape), dtype)


def _job_gather(arrs, by_cols):
    n = len(arrs)

    def dst(o, k, chip, rows):
        if by_cols[k]:
            c = arrs[k].shape[1]
            return o.at[rows, pl.ds(pl.multiple_of(chip * c, LANES), c)]
        return o.at[chip, rows]

    def copies(ins, outs, scr, want):
        ici_send, ici_recv, d2d_send, d2d_recv = scr
        xi, yi, ci = _mesh_pos()
        me = 2 * xi + yi
        res = []
        for k in range(n):
            r = arrs[k].shape[0]
            mine, other = _half(ci, r // 2), _half(1 - ci, r // 2)
            for j, (px, py) in enumerate(_peer_chips(xi, yi)):
                s = k * N_PEER + j
                pc = 2 * px + py
                ici = dict(send_sem=ici_send.at[s], recv_sem=ici_recv.at[s], device_id=(px, py, ci), device_id_type=MESH)
                d2d = dict(send_sem=d2d_send.at[s], recv_sem=d2d_recv.at[s], device_id=(xi, yi, 1 - ci),
                           device_id_type=MESH)
                made = {}
                if "send" in want:
                    made["send"] = pltpu.make_async_remote_copy(
                        src_ref=ins[k].at[mine], dst_ref=dst(outs[k], k, me, mine), **ici)
                if "arrive" in want:
                    made["arrive"] = pltpu.make_async_remote_copy(
                        src_ref=ins[k].at[mine], dst_ref=dst(outs[k], k, pc, mine), **ici)
                if "forward" in want:
                    made["forward"] = pltpu.make_async_remote_copy(
                        src_ref=dst(outs[k], k, pc, mine), dst_ref=dst(outs[k], k, pc, mine), **d2d)
                if "handed" in want:
                    made["handed"] = pltpu.make_async_remote_copy(
                        src_ref=dst(outs[k], k, pc, other), dst_ref=dst(outs[k], k, pc, other), **d2d)
                res.append(made)
        return res

    def start(ins, outs, scr):
        for cp in copies(ins, outs, scr, ("send",)):
            cp["send"].start()

    def finish(ins, outs, scr):
        for cp in copies(ins, outs, scr, ("arrive", "forward")):
            cp["arrive"].wait_recv()
            cp["forward"].start()
        for cp in copies(ins, outs, scr, ("handed", "send", "forward")):
            cp["handed"].wait_recv()
            cp["send"].wait_send()
            cp["forward"].wait_send()

    shapes = [_sds((a.shape[0], N_CHIPS * a.shape[1]) if bc else (N_CHIPS,) + a.shape, a.dtype)
              for a, bc in zip(arrs, by_cols)]
    dma = pltpu.SemaphoreType.DMA
    return _Job(arrs, shapes, [dma((n * N_PEER,))] * 4, start, finish)


def _job_scatter(parts):
    n = len(parts)

    def copies(ins, outs, scr):
        send_sems, recv_sems = scr
        xi, yi, ci = _mesh_pos()
        res = []
        for k in range(n):
            for j, (px, py) in enumerate(_peer_chips(xi, yi)):
                s = k * N_PEER + j
                res.append(pltpu.make_async_remote_copy(
                    src_ref=ins[k].at[2 * px + py], dst_ref=outs[k].at[j], send_sem=send_sems.at[s],
                    recv_sem=recv_sems.at[s], device_id=(px, py, ci), device_id_type=MESH))
        return res

    def start(ins, outs, scr):
        for cp in copies(ins, outs, scr):
            cp.start()

    def finish(ins, outs, scr):
        for cp in copies(ins, outs, scr):
            cp.wait_recv()
            cp.wait_send()

    dma = pltpu.SemaphoreType.DMA
    return _Job(parts, [_sds((N_PEER,) + p.shape[1:], p.dtype) for p in parts], [dma((n * N_PEER,))] * 2, start, finish)


def _job_to_other_core(groups):
    pieces = [(g, a, off) for g, group in enumerate(groups) for a, off in group]
    n = len(pieces)

    def geometry(group):
        a0, off0 = group[0]
        if off0 is None:
            return a0.shape
        if a0.ndim == 4:
            return (N_CHIPS, a0.shape[2], a0.shape[3])
        return (a0.shape[0] // 2, sum(a.shape[1] for a, _ in group))

    def copies(ins, outs, scr):
        send_sems, recv_sems = scr
        xi, yi, ci = _mesh_pos()
        res = []
        for p, (g, a, off) in enumerate(pieces):
            if off is None:
                give, land = ins[p], outs[g]
            elif a.ndim == 4:
                give, land = ins[p].at[pl.ds(0, N_CHIPS), 1 - ci], outs[g]
            else:
                hr, w = a.shape[0] // 2, a.shape[1]
                give, land = ins[p].at[_half(1 - ci, hr)], outs[g].at[pl.ds(0, hr), pl.ds(off, w)]
            res.append(pltpu.make_async_remote_copy(
                src_ref=give, dst_ref=land, send_sem=send_sems.at[p], recv_sem=recv_sems.at[p],
                device_id=(xi, yi, 1 - ci), device_id_type=MESH))
        return res

    def start(ins, outs, scr):
        for cp in copies(ins, outs, scr):
            cp.start()

    def finish(ins, outs, scr):
        for cp in copies(ins, outs, scr):
            cp.wait_recv()
            cp.wait_send()

    dma = pltpu.SemaphoreType.DMA
    return _Job([a for _, a, _ in pieces], [_sds(geometry(group), group[0][0].dtype) for group in groups],
                [dma((n,))] * 2, start, finish)


def _call(body, *, name, grid, in_specs, out_specs, out_shape, args, scratch_shapes=(), parallel=False, job=None):
    n_in, n_out, n_scr = len(in_specs), len(out_specs), len(scratch_shapes)

    def params(sem):
        return pltpu.CompilerParams(dimension_semantics=sem, vmem_limit_bytes=VMEM_LIMIT_BYTES)

    if job is None:
        sem = ("parallel" if parallel else "arbitrary",) * len(grid)
        res = pl.pallas_call(
            body, name=name, grid=grid, in_specs=in_specs, out_specs=out_specs, out_shape=out_shape,
            scratch_shapes=list(scratch_shapes), compiler_params=params(sem))(*args)
        return list(res), []
    n_ji, n_jo = len(job.ins), len(job.out_shapes)

    def carried(*refs):
        ins, refs = refs[:n_in], refs[n_in:]
        j_ins, refs = refs[:n_ji], refs[n_ji:]
        outs, refs = refs[:n_out], refs[n_out:]
        j_outs, refs = refs[:n_jo], refs[n_jo:]
        scr, j_scr = refs[:n_scr], refs[n_scr:]
        ids = [pl.program_id(d) for d in range(len(grid))]
        first = ids[0] == 0
        last = ids[0] == grid[0] - 1
        for d in range(1, len(grid)):
            first = first & (ids[d] == 0)
            last = last & (ids[d] == grid[d] - 1)

        @pl.when(first)
        def _():
            job.start(j_ins, j_outs, j_scr)

        body(*ins, *outs, *scr)

        @pl.when(last)
        def _():
            job.finish(j_ins, j_outs, j_scr)

    hbm = pl.BlockSpec(memory_space=pl.ANY)
    res = pl.pallas_call(
        carried, name=name, grid=grid, in_specs=list(in_specs) + [hbm] * n_ji, out_specs=list(out_specs) + [hbm] * n_jo,
        out_shape=list(out_shape) + job.out_shapes, scratch_shapes=list(scratch_shapes) + job.scratch,
        compiler_params=params(("arbitrary",) * len(grid)))(*args, *job.ins)
    return list(res[:n_out]), list(res[n_out:])


def _run_job(job, name):
    n_i, n_o = len(job.ins), len(job.out_shapes)

    def body(*refs):
        ins, outs, scr = refs[:n_i], refs[n_i:n_i + n_o], refs[n_i + n_o:]
        job.start(ins, outs, scr)
        job.finish(ins, outs, scr)

    hbm = pl.BlockSpec(memory_space=pl.ANY)
    return list(pl.pallas_call(body, name=name, in_specs=[hbm] * n_i, out_specs=[hbm] * n_o, out_shape=job.out_shapes,
                               scratch_shapes=job.scratch)(*job.ins))


def _adam_values(w, m, v, g):
    m2 = ADAM_B1 * m + (1.0 - ADAM_B1) * g
    v2 = ADAM_B2 * v + (1.0 - ADAM_B2) * (g * g)
    delta = -ADAM_LR * ((m2 / (1.0 - ADAM_B1 ** ADAM_STEP)) / (jnp.sqrt(v2 / (1.0 - ADAM_B2 ** ADAM_STEP)) + ADAM_EPS)
                        + ADAM_WD * w)
    return delta, m2, v2


_P_WSP, _P_WGU, _P_NORM, _P_BG, _P_BSP, _P_HEAD, _P_LOSS, _P_ROWS = 0, 512, 576, 608, 616, 624, 720, 728


def _small_sum(dgrads):
    def body(dwsp, dwgu, dg1, dgpm, dgpf, dgpo, dbg, dbspt, dgn, dlng, dlnb, loss_in, tot, pack, slots, send_sems,
             recv_sems):
        xi, yi, ci = _mesh_pos()
        chip = 2 * xi + yi

        pack[...] = jnp.zeros_like(pack)
        for g in range(SGU_GROUPS):
            pack[_P_WSP + g * SGU_BLOCK:_P_WSP + (g + 1) * SGU_BLOCK] = dwsp[g]
        for j in range(N_CHIPS):
            pack[_P_WGU + GLA_RANK * j:_P_WGU + GLA_RANK * (j + 1)] = dwgu[0:GLA_RANK, LANES * j:LANES * (j + 1)]
        for k, r in enumerate((dg1, dgpm, dgpf, dgpo)):
            for q in range(8):
                pack[_P_NORM + 8 * k + q:_P_NORM + 8 * k + q + 1] = r[:, LANES * q:LANES * (q + 1)]
        for q in range(4):
            pack[_P_BG + q:_P_BG + q + 1] = dbg[:, LANES * q:LANES * (q + 1)]
        pack[_P_BSP:_P_BSP + SGU_GROUPS] = jnp.transpose(dbspt[...])[0:SGU_GROUPS]
        for k, r in enumerate((dgn, dlng, dlnb)):
            for j in range(N_CHIPS):
                for hh in range(4):
                    row = _P_HEAD + 32 * k + 8 * j + hh
                    pack[row:row + 1, 0:64] = r[:, 256 * hh + 64 * j:256 * hh + 64 * (j + 1)]
        pack[_P_LOSS:_P_LOSS + 1] = loss_in[...]

        to_sibling = pltpu.make_async_remote_copy(
            src_ref=pack, dst_ref=tot, send_sem=send_sems.at[N_PEER], recv_sem=recv_sems.at[N_PEER],
            device_id=(xi, yi, 1 - ci), device_id_type=MESH)
        to_sibling.start()
        to_sibling.wait_recv()
        to_sibling.wait_send()
        pack[...] = pack[...] + tot[...]
        slots[chip] = pack[...]

        def copy(j, slot):
            px, py = _peer_chips(xi, yi)[j]
            return pltpu.make_async_remote_copy(
                src_ref=pack, dst_ref=slots.at[slot(2 * px + py)], send_sem=send_sems.at[j], recv_sem=recv_sems.at[j],
                device_id=(px, py, ci), device_id_type=MESH)

        sends = [copy(j, lambda peer_chip: chip) for j in range(N_PEER)]
        for cp in sends:
            cp.start()
        for j in range(N_PEER):
            copy(j, lambda peer_chip: peer_chip).wait_recv()
        for cp in sends:
            cp.wait_send()
        acc = slots[0]
        for d in range(1, N_CHIPS):
            acc = acc + slots[d]
        tot[...] = acc

    return pl.pallas_call(
        body, name="small_sum", in_specs=[_whole()] * 12, out_specs=_whole(), out_shape=_sds((_P_ROWS, LANES), F32),
        scratch_shapes=[pltpu.VMEM((_P_ROWS, LANES), F32), pltpu.VMEM((N_CHIPS, _P_ROWS, LANES), F32),
                        pltpu.SemaphoreType.DMA((N_PEER + 1,)), pltpu.SemaphoreType.DMA((N_PEER + 1,))],
        compiler_params=pltpu.CompilerParams(vmem_limit_bytes=VMEM_LIMIT_BYTES),
    )(*dgrads)


def _small_adamw(tot, ws, ms, vs):
    n = len(ws)

    def body(*refs):
        tot = refs[0]
        w_refs, m_refs, v_refs = refs[1:1 + n], refs[1 + n:1 + 2 * n], refs[1 + 2 * n:1 + 3 * n]
        loss_out = refs[1 + 3 * n]
        outs = refs[2 + 3 * n:]
        chip = 2 * lax.axis_index("x") + lax.axis_index("y")
        loss_out[...] = tot[_P_LOSS:_P_LOSS + 1, 0:1]

        def step(k, g, pick, put):
            d, m2, v2 = _adam_values(pick(w_refs[k]), pick(m_refs[k]), pick(v_refs[k]), g)
            for o, val in zip((outs[k], outs[n + k], outs[2 * n + k], outs[3 * n + k]), (g, d, m2, v2)):
                put(o, val)

        def whole(ref):
            return ref[0]

        def put_whole(ref, val):
            ref[0] = val

        for g in range(SGU_GROUPS):
            def pick_g(ref, g=g):
                return ref[0, g]

            def put_g(ref, val, g=g):
                ref[0, g] = val

            step(0, tot[_P_WSP + g * SGU_BLOCK:_P_WSP + (g + 1) * SGU_BLOCK], pick_g, put_g)
        step(1, tot[pl.ds(pl.multiple_of(_P_WGU + GLA_RANK * chip, GLA_RANK), GLA_RANK), :], whole, put_whole)
        for k, (base, chunks) in enumerate(((_P_NORM, 8), (_P_NORM + 8, 8), (_P_NORM + 16, 8), (_P_NORM + 24, 8), (_P_BG, 4))):
            for q in range(chunks):
                def pick_q(ref, q=q):
                    return ref[:, LANES * q:LANES * (q + 1)]

                def put_q(ref, val, q=q):
                    ref[:, LANES * q:LANES * (q + 1)] = val

                step(2 + k, tot[base + q:base + q + 1], pick_q, put_q)
        step(7, tot[_P_BSP:_P_BSP + SGU_GROUPS], whole, put_whole)
        for k in range(3):
            mine = tot[pl.ds(pl.multiple_of(_P_HEAD + 32 * k + 8 * chip, 8), 8), :]
            step(8 + k, mine[0:4, 0:64], whole, put_whole)

    shapes = [_sds(w.shape, F32) for w in ws]
    res = pl.pallas_call(
        body, name="small_adamw", in_specs=[_whole()] * (1 + 3 * n), out_specs=[_whole()] * (1 + 4 * n),
        out_shape=[_sds((1, 1), F32)] + shapes * 4,
        compiler_params=pltpu.CompilerParams(vmem_limit_bytes=VMEM_LIMIT_BYTES),
    )(tot, *ws, *ms, *vs)
    return res[0].reshape(()), [list(res[1 + i * n:1 + (i + 1) * n]) for i in range(4)]


def _w_in_pieces():
    blk = D_IN // N_CHIPS
    pieces = []
    for s in range(len(_IN_SPLITS)):
        lo_s, hi_s = _IN_STARTS[s], _IN_STARTS[s + 1]
        for j in range(N_CHIPS):
            lo, hi = max(lo_s, j * blk), min(hi_s, (j + 1) * blk)
            if lo < hi:
                pieces.append((j, lo - j * blk, _IN_DST[s] + lo - lo_s, hi - lo))
    return pieces


def _relayout_w_in(gathered):
    _, rows, blk = gathered.shape
    tr = 256

    def body(g_ref, o_ref):
        o_ref[:, OFF_AL:N_ALL] = jnp.zeros((tr, LANES), BF16)
        for j, src, dst, w in _w_in_pieces():
            o_ref[:, dst:dst + w] = g_ref[j, :, src:src + w]

    res, _ = _call(body, name="relayout_w_in", grid=(rows // tr,), parallel=True,
                   in_specs=[pl.BlockSpec((N_CHIPS, tr, blk), lambda i: (0, i, 0))],
                   out_specs=[pl.BlockSpec((tr, N_ALL), lambda i: (i, 0))],
                   out_shape=[_sds((rows, N_ALL), BF16)], args=(gathered,))
    return res[0]


def _update_row_tile(rows):
    for t in range(min(rows, 256), 7, -8):
        if rows % t == 0:
            return t
    return rows


def _my_half(first_ref, second_ref):
    return jnp.where(lax.axis_index("c") == 0, first_ref[...], second_ref[...])


def _presum_w_in(dws, theirs, row0, rows, name, job=None):
    hr = theirs[0].shape[0]
    blk = D_IN // N_CHIPS
    tr = 64
    assert row0 % tr == 0 and rows % tr == 0
    nh, t0 = hr // tr, row0 // tr
    n = len(dws)

    def body(*refs):
        dw_refs, q_refs, (o_ref, s_scr) = refs[:2 * n], refs[2 * n:3 * n], refs[3 * n:]
        for p, (a, off) in enumerate(dws):
            w = a.shape[1]
            s_scr[:, off:off + w] = (_my_half(dw_refs[2 * p], dw_refs[2 * p + 1]) + q_refs[p][...]).astype(BF16)
        for j, src, dst, w in _w_in_pieces():
            o_ref[j, :, src:src + w] = s_scr[:, dst:dst + w]

    in_specs, args = [], []
    for a, _ in dws:
        w = a.shape[1]
        in_specs += [pl.BlockSpec((tr, w), lambda i: (i + t0, 0)), pl.BlockSpec((tr, w), lambda i: (i + t0 + nh, 0))]
        args += [a, a]
    in_specs += [pl.BlockSpec((tr, q.shape[1]), lambda i: (i + t0, 0)) for q in theirs]
    res, jres = _call(body, name=name, grid=(rows // tr,), parallel=True, in_specs=in_specs,
                      out_specs=[pl.BlockSpec((N_CHIPS, tr, blk), lambda i: (0, i, 0))],
                      out_shape=[_sds((N_CHIPS, rows, blk), BF16)], scratch_shapes=[pltpu.VMEM((tr, N_ALL), BF16)],
                      args=(*args, *theirs), job=job)
    return res[0], jres


def _presum(dw, theirs, name):
    if dw.ndim == 4:
        _, _, hr, c = dw.shape
        tr = _update_row_tile(hr)
        first = pl.BlockSpec((1, 1, tr, c), lambda j, i: (j, 0, i, 0))
        second = pl.BlockSpec((1, 1, tr, c), lambda j, i: (j, 1, i, 0))
        other = pl.BlockSpec((1, tr, c), lambda j, i: (j, i, 0))
    else:
        hr, c = dw.shape[0] // 2, dw.shape[1] // N_CHIPS
        tr = _update_row_tile(hr)
        nh = hr // tr
        first = pl.BlockSpec((tr, c), lambda j, i: (i, j))
        second = pl.BlockSpec((tr, c), lambda j, i: (i + nh, j))
        other = pl.BlockSpec((tr, c), lambda j, i: (i, j))

    def body(a_ref, b_ref, q_ref, o_ref):
        mine = _my_half(a_ref, b_ref).reshape(tr, c)
        o_ref[...] = (mine + q_ref[...].reshape(tr, c)).astype(BF16).reshape(o_ref.shape)

    res, _ = _call(body, name=name, grid=(N_CHIPS, hr // tr), parallel=True, in_specs=[first, second, other],
                   out_specs=[pl.BlockSpec((1, tr, c), lambda j, i: (j, i, 0))],
                   out_shape=[_sds((N_CHIPS, hr, c), BF16)], args=(dw, dw, theirs))
    return res[0]


def _sum_slots(own, slots, name):
    rows, cols = own.shape
    tr = _update_row_tile(rows)

    def body(own_ref, s_ref, o_ref):
        acc = own_ref[...].astype(F32)
        for j in range(N_PEER):
            acc = acc + s_ref[j].astype(F32)
        o_ref[...] = acc

    res, _ = _call(body, name=name, grid=(rows // tr,), parallel=True,
                   in_specs=[pl.BlockSpec((tr, cols), lambda i: (i, 0)), pl.BlockSpec((N_PEER, tr, cols), lambda i: (0, i, 0))],
                   out_specs=[pl.BlockSpec((tr, cols), lambda i: (i, 0))], out_shape=[_sds((rows, cols), F32)],
                   args=(own, slots))
    return res[0]


def _adamw(w, m, v, g_mine, g_theirs, name, job=None):
    rows, cols = w.shape
    part_rows = [p.shape[0] for p in g_mine]
    assert sum(part_rows) == rows // 2 and [p.shape[0] for p in g_theirs] == part_rows
    tr = _update_row_tile(min(part_rows))
    assert all(r % tr == 0 for r in part_rows)
    nh = (rows // 2) // tr
    starts = [sum(part_rows[:k]) // tr for k in range(len(part_rows))]
    n_parts = len(part_rows)

    def body(w_ref, m_ref, v_ref, *rest):
        g_refs, (g_out, d_out, m_out, v_out) = rest[:-4], rest[-4:]
        step = pl.program_id(0)
        mine_here = (step // nh) == lax.axis_index("c")
        q = step % nh
        g = None
        for k in reversed(range(n_parts)):
            val = jnp.where(mine_here, g_refs[k][...], g_refs[n_parts + k][...])
            g = val if g is None else jnp.where(q < starts[k + 1], val, g)
        d, m2, v2 = _adam_values(w_ref[...], m_ref[...], v_ref[...], g)
        g_out[...] = g
        m_out[...] = m2
        v_out[...] = v2
        d_out[...] = d

    def g_spec(k):
        last = part_rows[k] // tr - 1
        return pl.BlockSpec((tr, cols), lambda i: (jnp.clip(i % nh - starts[k], 0, last), 0))

    spec = pl.BlockSpec((tr, cols), lambda i: (i, 0))
    return _call(body, name=name, grid=(rows // tr,), parallel=True,
                 in_specs=[spec] * 3 + [g_spec(k) for k in range(n_parts)] * 2, out_specs=[spec] * 4,
                 out_shape=[_sds((rows, cols), F32)] * 4, args=(w, m, v, *g_mine, *g_theirs), job=job)


def _transposed_cast(wt):
    cols, rows = wt.shape

    def body(x_ref, o_ref):
        o_ref[...] = jnp.transpose(x_ref[...]).astype(BF16)

    res, _ = _call(body, name="transpose_w_in", grid=(pl.cdiv(cols, LANES),), parallel=True,
                   in_specs=[pl.BlockSpec((LANES, rows), lambda j: (j, 0))],
                   out_specs=[pl.BlockSpec((rows, LANES), lambda j: (0, j))], out_shape=[_sds((rows, cols), BF16)],
                   args=(wt,))
    return res[0]


def _adamw_transposed(wt, mt, vt, g_mine, g_theirs, name):
    cols, rows = wt.shape
    n_parts = len(g_mine)

    def body(w_ref, m_ref, v_ref, *rest):
        g_refs, (g_out, d_out, m_out, v_out) = rest[:-4], rest[-4:]
        mine = jnp.concatenate([r[...] for r in g_refs[:n_parts]], axis=0)
        theirs = jnp.concatenate([r[...] for r in g_refs[n_parts:]], axis=0)
        first = lax.axis_index("c") == 0
        g = jnp.transpose(jnp.concatenate([jnp.where(first, mine, theirs), jnp.where(first, theirs, mine)], axis=0))
        d, m2, v2 = _adam_values(w_ref[...], m_ref[...], v_ref[...], g)
        g_out[...] = g
        m_out[...] = m2
        v_out[...] = v2
        d_out[...] = d

    spec = pl.BlockSpec((LANES, rows), lambda j: (j, 0))
    g_specs = [pl.BlockSpec((p.shape[0], LANES), lambda j: (0, j)) for p in g_mine] * 2
    res, _ = _call(body, name=name, grid=(pl.cdiv(cols, LANES),), parallel=True, in_specs=[spec] * 3 + g_specs,
                   out_specs=[spec] * 4, out_shape=[_sds((cols, rows), F32)] * 4, args=(wt, mt, vt, *g_mine, *g_theirs))
    return res


def _inproj_fwd(x, g1, w_all, job=None):
    T = x.shape[0]
    tT = _row_tile(T, 512)

    def body(x_ref, g_ref, w_ref, a_ref, proj_ref, alow_ref):
        xv = x_ref[...]
        a = (xv * _rms_stats(xv) * g_ref[...]).astype(BF16)
        a_ref[...] = a
        for j in range(N_MAIN // 1024):
            cols = slice(j * 1024, (j + 1) * 1024)
            proj_ref[:, cols] = _dot(a, w_ref[:, cols]).astype(BF16)
        alow_ref[...] = _dot(a, w_ref[:, N_MAIN:N_ALL])

    row = lambda w: pl.BlockSpec((tT, w), lambda i: (i, 0))
    return _call(
        body, name="inproj_fwd", grid=(T // tT,), parallel=True,
        in_specs=[row(D_MODEL), pl.BlockSpec((1, D_MODEL), lambda i: (0, 0)), _whole()],
        out_specs=[row(D_MODEL), row(N_MAIN), row(LANES)],
        out_shape=[_sds((T, D_MODEL), BF16), _sds((T, N_MAIN), BF16), _sds((T, LANES), F32)],
        args=(x, g1, w_all), job=job)


def _gla_decay_terms(al_ref, wgu_ref, bg_ref, later_ref):
    logit = _dot_f32(al_ref[...], wgu_ref[...]) + bg_ref[...]
    la = _log_sigmoid(logit) * (1.0 / GLA_TAU)
    delta = _dot_exact_lhs(later_ref[...], la)
    return logit, la, delta


def _gla_fwd(proj, alow, wgu, b_gate, gn, job=None):
    T = proj.shape[0]
    tT = _row_tile(T, 512)
    nc = tT // CHUNK

    def body(q_ref, k_ref, v_ref, r_ref, al_ref, wgu_ref, bg_ref, gn_ref, later_ref, y_ref, st_ref, s_scr):
        @pl.when(pl.program_id(0) == 0)
        def _():
            s_scr[...] = jnp.zeros_like(s_scr)

        _, la, delta = _gla_decay_terms(al_ref, wgu_ref, bg_ref, later_ref)
        kdec = (k_ref[...].astype(F32) * jnp.exp(delta)).astype(BF16)
        heads = range(GLA_HEADS)
        kcs = [slice(h * GLA_DK, (h + 1) * GLA_DK) for h in heads]
        vcs = [slice(h * GLA_DV, (h + 1) * GLA_DV) for h in heads]
        state = [s_scr[h] for h in heads]
        for c in range(nc):
            rows = slice(c * CHUNK, (c + 1) * CHUNK)
            first = slice(c * CHUNK, c * CHUNK + 1)
            dec = jnp.exp(la[first, :] + delta[first, :])
            upd_t = [_dot(v_ref[rows, vcs[h]], kdec[rows, kcs[h]], _TN) for h in heads]
            qs = [(q_ref[rows, kcs[h]].astype(F32) * (GLA_DK ** -0.5)).astype(BF16) for h in heads]
            for h in heads:
                state[h] = state[h] * dec[:, kcs[h]] + upd_t[h]
                st_ref[c, h] = state[h]
            o = [_dot(qs[h], state[h].astype(BF16), _NT) for h in heads]
            for h in heads:
                on = o[h] * _rms_stats(o[h]) * gn_ref[:, vcs[h]]
                rr = r_ref[rows, vcs[h]].astype(F32)
                y_ref[rows, vcs[h]] = (on * (rr * _sigmoid(rr))).astype(BF16)
        for h in heads:
            s_scr[h] = state[h]

    blk = lambda w, j: pl.BlockSpec((tT, w), lambda i: (i, j))
    return _call(
        body, name="gla_fwd", grid=(T // tT,),
        in_specs=[blk(512, 0), blk(512, 1), blk(1024, 1), blk(1024, 2), blk(LANES, 0)] + [_whole()] * 4,
        out_specs=[pl.BlockSpec((tT, GLA_V), lambda i: (i, 0)),
                   pl.BlockSpec((nc, GLA_HEADS, GLA_DV, GLA_DK), lambda i: (i, 0, 0, 0))],
        out_shape=[_sds((T, GLA_V), BF16), _sds((T // CHUNK, GLA_HEADS, GLA_DV, GLA_DK), F32)],
        scratch_shapes=[pltpu.VMEM((GLA_HEADS, GLA_DV, GLA_DK), F32)],
        args=(proj, proj, proj, proj, alow, wgu, b_gate, gn, _chunk_masks(tT, upper=True)), job=job)


def _sgu_mask():
    i = lax.broadcasted_iota(jnp.int32, (SGU_BLOCK, SGU_BLOCK), 0)
    j = lax.broadcasted_iota(jnp.int32, (SGU_BLOCK, SGU_BLOCK), 1)
    return lax.shift_right_logical(j, 6) <= lax.shift_right_logical(i, 6)


def _sgu_fwd(proj, ln_g, ln_b, w_sp, b_sp_t):
    T = proj.shape[0]
    tT = _row_tile(T, 512)
    nb = tT // SGU_BLOCK

    def body(su_ref, sv_ref, lg_ref, lb_ref, w_ref, b_ref, y_ref):
        mask = _sgu_mask()
        for g in range(SGU_GROUPS):
            gc = slice(g * SGU_DG, (g + 1) * SGU_DG)
            wm = jnp.where(mask, w_ref[g], 0.0).astype(BF16)
            vf = _gelu(sv_ref[:, gc].astype(F32))
            mu = jnp.mean(vf, axis=-1, keepdims=True)
            vc = vf - mu
            rstd = lax.rsqrt(jnp.mean(vc * vc, axis=-1, keepdims=True) + EPS)
            vn = (vc * rstd * lg_ref[:, gc] + lb_ref[:, gc]).astype(BF16)
            u = _gelu(su_ref[:, gc].astype(F32))
            for b in range(nb):
                rows = slice(b * SGU_BLOCK, (b + 1) * SGU_BLOCK)
                mixed = _dot(wm, vn[rows, :]) + b_ref[:, g:g + 1]
                y_ref[rows, gc] = (u[rows, :] * mixed).astype(BF16)

    blk = lambda j: pl.BlockSpec((tT, 1024), lambda i: (i, j))
    res, _ = _call(body, name="sgu_fwd", grid=(T // tT,), parallel=True,
                   in_specs=[blk(3), blk(4), _whole(), _whole(), _whole(), _whole()],
                   out_specs=[pl.BlockSpec((tT, 1024), lambda i: (i, 0))], out_shape=[_sds((T, 1024), BF16)],
                   args=(proj, proj, ln_g, ln_b, w_sp, b_sp_t))
    return res[0]


def _merge_fwd(x, proj, y_gla, y_sgu, w_bg, w_bs, w_o, g_pm, job=None):
    T = x.shape[0]
    tT = _row_tile(T, 512)

    def body(x_ref, gg_ref, gs_ref, yg_ref, ys_ref, wbg_ref, wbs_ref, wo_ref, g_ref,
             zg_ref, zs_ref, mg_ref, mix_ref, x1_ref):
        zg = _dot(yg_ref[...], wbg_ref[...])
        zs = _dot(ys_ref[...], wbs_ref[...])
        zg_ref[...] = zg.astype(BF16)
        zs_ref[...] = zs.astype(BF16)
        merged = (_sigmoid(gg_ref[...].astype(F32)) * zg + _sigmoid(gs_ref[...].astype(F32)) * zs).astype(BF16)
        mg_ref[...] = merged
        mix = _dot(merged, wo_ref[...])
        mix_ref[...] = mix
        x1_ref[...] = x_ref[...] + mix * _rms_stats(mix) * g_ref[...]

    row = pl.BlockSpec((tT, D_MODEL), lambda i: (i, 0))
    blk = lambda j: pl.BlockSpec((tT, 1024), lambda i: (i, j))
    sds = lambda dt: _sds((T, D_MODEL), dt)
    return _call(body, name="merge_fwd", grid=(T // tT,), parallel=True,
                 in_specs=[row, blk(5), blk(6), row, row, _whole(), _whole(), _whole(),
                           pl.BlockSpec((1, D_MODEL), lambda i: (0, 0))],
                 out_specs=[row] * 5, out_shape=[sds(BF16), sds(BF16), sds(BF16), sds(F32), sds(F32)],
                 args=(x, proj, proj, y_gla, y_sgu, w_bg, w_bs, w_o, g_pm), job=job)


def _ffn_fwd_bwd(x1, tgt, w_fi_top, w_fi_bot, w_fo, g_pf, g_po):
    T = x1.shape[0]
    tT = _row_tile(T, 256)
    half = D_FF // 2
    kh = D_MODEL // 2

    def body(x1_ref, t_ref, top_ref, bot_ref, wfo_ref, gpf_ref, gpo_ref,
             h_ref, f_ref, dgu_ref, dy_ref, dx1_ref, loss_ref, dgpf_ref, dgpo_ref, gu_scr):
        @pl.when(pl.program_id(0) == 0)
        def _():
            loss_ref[...] = jnp.zeros_like(loss_ref)
            dgpf_ref[...] = jnp.zeros_like(dgpf_ref)
            dgpo_ref[...] = jnp.zeros_like(dgpo_ref)

        x1v = x1_ref[...]
        r2 = _rms_stats(x1v)
        h = (x1v * r2 * gpf_ref[...]).astype(BF16)
        h_ref[...] = h
        y = jnp.zeros((tT, D_MODEL), F32)
        for j in range(2):
            gc = slice(j * half, (j + 1) * half)
            uc = slice(D_FF + j * half, D_FF + (j + 1) * half)
            gate = _dot(h[:, :kh], top_ref[j]) + _dot(h[:, kh:], bot_ref[j])
            up = _dot(h[:, :kh], top_ref[2 + j]) + _dot(h[:, kh:], bot_ref[2 + j])
            gu_scr[:, gc] = gate
            gu_scr[:, uc] = up
            f = (gate * _sigmoid(gate) * up).astype(BF16)
            f_ref[:, gc] = f
            y = y + _dot(f, wfo_ref[gc, :])
        r3 = _rms_stats(y)
        x2 = x1v + y * r3 * gpo_ref[...]
        err = x2 - t_ref[...]
        loss_ref[...] += jnp.sum(err * err) * (0.5 / D_MODEL)
        dx2 = err * (1.0 / D_MODEL)
        dy, dg = _rms_bwd(dx2, y, r3, gpo_ref[...])
        dgpo_ref[...] += jnp.sum(dg, axis=0, keepdims=True)
        dyb = dy.astype(BF16)
        dy_ref[...] = dyb
        dh_top = jnp.zeros((tT, kh), F32)
        dh_bot = jnp.zeros((tT, kh), F32)
        for j in range(2):
            gc = slice(j * half, (j + 1) * half)
            uc = slice(D_FF + j * half, D_FF + (j + 1) * half)
            df = _dot(dyb, wfo_ref[gc, :], _NT)
            gate = gu_scr[:, gc]
            up = gu_scr[:, uc]
            sg = _sigmoid(gate)
            dgate = (df * up * (sg * (1.0 + gate * (1.0 - sg)))).astype(BF16)
            dup = (df * (gate * sg)).astype(BF16)
            dgu_ref[:, gc] = dgate
            dgu_ref[:, uc] = dup
            dh_top = dh_top + _dot(dgate, top_ref[j], _NT) + _dot(dup, top_ref[2 + j], _NT)
            dh_bot = dh_bot + _dot(dgate, bot_ref[j], _NT) + _dot(dup, bot_ref[2 + j], _NT)
        dh = jnp.concatenate([dh_top, dh_bot], axis=1)
        dx1n, dg2 = _rms_bwd(dh, x1v, r2, gpf_ref[...])
        dgpf_ref[...] += jnp.sum(dg2, axis=0, keepdims=True)
        dx1_ref[...] = dx2 + dx1n

    row = lambda w: pl.BlockSpec((tT, w), lambda i: (i, 0))
    vec = pl.BlockSpec((1, D_MODEL), lambda i: (0, 0))
    res, _ = _call(
        body, name="ffn_fwd_bwd", grid=(T // tT,),
        in_specs=[row(D_MODEL), row(D_MODEL), _whole(), _whole(), _whole(), vec, vec],
        out_specs=[row(D_MODEL), row(D_FF), row(2 * D_FF), row(D_MODEL), row(D_MODEL),
                   pl.BlockSpec((1, LANES), lambda i: (0, 0)), vec, vec],
        out_shape=[_sds((T, D_MODEL), BF16), _sds((T, D_FF), BF16), _sds((T, 2 * D_FF), BF16), _sds((T, D_MODEL), BF16),
                   _sds((T, D_MODEL), F32), _sds((1, LANES), F32), _sds((1, D_MODEL), F32), _sds((1, D_MODEL), F32)],
        scratch_shapes=[pltpu.VMEM((tT, 2 * D_FF), F32)], args=(x1, tgt, w_fi_top, w_fi_bot, w_fo, g_pf, g_po))
    return res


def _merge_bwd(dx1, mix, proj, zg, zs, w_bg, w_bs, w_o, g_pm, job=None):
    T = dx1.shape[0]
    tT = _row_tile(T, 512)

    def body(dx1_ref, mix_ref, gg_ref, gs_ref, zg_ref, zs_ref, wbg_ref, wbs_ref, wo_ref, g_ref,
             dmix_ref, dzg_ref, dzs_ref, dgate_ref, dyg_ref, dys_ref, dgpm_ref):
        @pl.when(pl.program_id(0) == 0)
        def _():
            dgpm_ref[...] = jnp.zeros_like(dgpm_ref)

        mix = mix_ref[...]
        dmix, dg = _rms_bwd(dx1_ref[...], mix, _rms_stats(mix), g_ref[...])
        dgpm_ref[...] += jnp.sum(dg, axis=0, keepdims=True)
        dmb = dmix.astype(BF16)
        dmix_ref[...] = dmb
        dmerged = _dot(dmb, wo_ref[...], _NT)
        for k, (gate_ref, z_ref, w_ref, dz_ref, dy_ref) in enumerate((
                (gg_ref, zg_ref, wbg_ref, dzg_ref, dyg_ref), (gs_ref, zs_ref, wbs_ref, dzs_ref, dys_ref))):
            sg = _sigmoid(gate_ref[...].astype(F32))
            dz = (dmerged * sg).astype(BF16)
            dz_ref[...] = dz
            dgate_ref[:, k * 1024:(k + 1) * 1024] = (dmerged * z_ref[...].astype(F32) * (sg * (1.0 - sg))).astype(BF16)
            dy_ref[...] = _dot(dz, w_ref[...], _NT).astype(BF16)

    row = pl.BlockSpec((tT, D_MODEL), lambda i: (i, 0))
    blk = lambda j: pl.BlockSpec((tT, 1024), lambda i: (i, j))
    vec = pl.BlockSpec((1, D_MODEL), lambda i: (0, 0))
    sds = _sds((T, D_MODEL), BF16)
    return _call(
        body, name="merge_bwd", grid=(T // tT,),
        in_specs=[row, row, blk(5), blk(6), row, row, _whole(), _whole(), _whole(), vec],
        out_specs=[row, row, row, pl.BlockSpec((tT, W_MRG), lambda i: (i, 0)), row, row, vec],
        out_shape=[sds, sds, sds, _sds((T, W_MRG), BF16), sds, sds, _sds((1, D_MODEL), F32)],
        args=(dx1, mix, proj, proj, zg, zs, w_bg, w_bs, w_o, g_pm), job=job)


def _sgu_bwd(proj, dy_sgu, ln_g, ln_b, w_sp, b_sp_t, job=None):
    T = proj.shape[0]
    tT = _row_tile(T, 512)
    nb = tT // SGU_BLOCK

    def body(su_ref, sv_ref, dy_ref, lg_ref, lb_ref, w_ref, b_ref, dp_ref, dw_ref, dbt_ref, dlg_ref, dlb_ref):
        @pl.when(pl.program_id(0) == 0)
        def _():
            dw_ref[...] = jnp.zeros_like(dw_ref)
            dbt_ref[...] = jnp.zeros_like(dbt_ref)
            dlg_ref[...] = jnp.zeros_like(dlg_ref)
            dlb_ref[...] = jnp.zeros_like(dlb_ref)

        mask = _sgu_mask()
        lane = lax.broadcasted_iota(jnp.int32, (SGU_BLOCK, LANES), 1)
        for g in range(SGU_GROUPS):
            gc = slice(g * SGU_DG, (g + 1) * SGU_DG)
            gc_v = slice(1024 + g * SGU_DG, 1024 + (g + 1) * SGU_DG)
            wm = jnp.where(mask, w_ref[g], 0.0).astype(BF16)
            vf, dvf_dsv = _gelu_and_grad(sv_ref[:, gc].astype(F32))
            mu = jnp.mean(vf, axis=-1, keepdims=True)
            vc = vf - mu
            rstd = lax.rsqrt(jnp.mean(vc * vc, axis=-1, keepdims=True) + EPS)
            vhat = vc * rstd
            vn = (vhat * lg_ref[:, gc] + lb_ref[:, gc]).astype(BF16)
            u, du_dsu = _gelu_and_grad(su_ref[:, gc].astype(F32))
            dy = dy_ref[:, gc].astype(F32)
            dmixed = (dy * u).astype(BF16)
            dvn_parts = []
            dw_acc = jnp.zeros((SGU_BLOCK, SGU_BLOCK), F32)
            db_acc = jnp.zeros((SGU_BLOCK, 1), F32)
            for b in range(nb):
                rows = slice(b * SGU_BLOCK, (b + 1) * SGU_BLOCK)
                mixed = _dot(wm, vn[rows, :]) + b_ref[:, g:g + 1]
                dp_ref[rows, gc] = (dy[rows, :] * mixed * du_dsu[rows, :]).astype(BF16)
                dvn_parts.append(_dot(wm, dmixed[rows, :], _TN))
                dw_acc = dw_acc + _dot(dmixed[rows, :], vn[rows, :], _NT)
                db_acc = db_acc + jnp.sum(dmixed[rows, :].astype(F32), axis=-1, keepdims=True)
            dw_ref[g] += jnp.where(mask, dw_acc, 0.0)
            dbt_ref[...] += jnp.where(lane == g, db_acc, 0.0)
            dvn = jnp.concatenate(dvn_parts, axis=0)
            dlg_ref[:, gc] += jnp.sum(dvn * vhat, axis=0, keepdims=True)
            dlb_ref[:, gc] += jnp.sum(dvn, axis=0, keepdims=True)
            dvh = dvn * lg_ref[:, gc]
            dvf = rstd * (dvh - jnp.mean(dvh, axis=-1, keepdims=True)
                          - vhat * jnp.mean(dvh * vhat, axis=-1, keepdims=True))
            dp_ref[:, gc_v] = (dvf * dvf_dsv).astype(BF16)

    blk = lambda j: pl.BlockSpec((tT, 1024), lambda i: (i, j))
    row = lambda w: pl.BlockSpec((tT, w), lambda i: (i, 0))
    vec = pl.BlockSpec((1, 1024), lambda i: (0, 0))
    return _call(
        body, name="sgu_bwd", grid=(T // tT,),
        in_specs=[blk(3), blk(4), row(1024), _whole(), _whole(), _whole(), _whole()],
        out_specs=[row(W_SGU), pl.BlockSpec((SGU_GROUPS, SGU_BLOCK, SGU_BLOCK), lambda i: (0, 0, 0)),
                   pl.BlockSpec((SGU_BLOCK, LANES), lambda i: (0, 0)), vec, vec],
        out_shape=[_sds((T, W_SGU), BF16), _sds((SGU_GROUPS, SGU_BLOCK, SGU_BLOCK), F32), _sds((SGU_BLOCK, LANES), F32),
                   _sds((1, 1024), F32), _sds((1, 1024), F32)],
        args=(proj, proj, dy_sgu, ln_g, ln_b, w_sp, b_sp_t), job=job)


def _gla_bwd(proj, alow, wgu, b_gate, gn, states, dy_gla, job=None):
    T = proj.shape[0]
    tT = _row_tile(T, 512)
    nc = tT // CHUNK
    nt = T // tT

    def body(q_ref, k_ref, v_ref, r_ref, al_ref, wgu_ref, bg_ref, gn_ref, later_ref, earlier_ref, st_ref, sp_ref, dy_ref,
             dp_ref, dal_ref, dgn_ref, dbg_ref, dwgu_ref, g_scr, dd_scr, dt_scr):
        step = pl.program_id(0)

        @pl.when(step == 0)
        def _():
            g_scr[...] = jnp.zeros_like(g_scr)
            dgn_ref[...] = jnp.zeros_like(dgn_ref)
            dbg_ref[...] = jnp.zeros_like(dbg_ref)
            dwgu_ref[...] = jnp.zeros_like(dwgu_ref)

        has_prev = jnp.where(step == nt - 1, 0.0, 1.0)
        logit, la, delta = _gla_decay_terms(al_ref, wgu_ref, bg_ref, later_ref)
        e = jnp.exp(delta)
        kdec_f = k_ref[...].astype(F32) * e
        kdec = kdec_f.astype(BF16)
        heads = range(GLA_HEADS)
        kcs = [slice(h * GLA_DK, (h + 1) * GLA_DK) for h in heads]
        vcs = [slice(h * GLA_DV, (h + 1) * GLA_DV) for h in heads]
        carry = [g_scr[h] for h in heads]
        dgn_acc = [jnp.zeros((1, GLA_DV), F32) for _ in heads]
        for c in reversed(range(nc)):
            rows = slice(c * CHUNK, (c + 1) * CHUNK)
            first = slice(c * CHUNK, c * CHUNK + 1)
            dec = jnp.exp(la[first, :] + delta[first, :])
            s_b = [st_ref[c, h].astype(BF16) for h in heads]
            qs = [(q_ref[rows, kcs[h]].astype(F32) * (GLA_DK ** -0.5)).astype(BF16) for h in heads]
            o = [_dot(qs[h], s_b[h], _NT) for h in heads]
            do = []
            for h in heads:
                rstd = _rms_stats(o[h])
                ohat = o[h] * rstd
                gnh = gn_ref[:, vcs[h]]
                dy = dy_ref[rows, vcs[h]].astype(F32)
                rr = r_ref[rows, vcs[h]].astype(F32)
                sg = _sigmoid(rr)
                don = dy * (rr * sg)
                dp_ref[rows, OFF_R + h * GLA_DV:OFF_R + (h + 1) * GLA_DV] = (
                    dy * (ohat * gnh) * (sg * (1.0 + rr * (1.0 - sg)))).astype(BF16)
                dgn_acc[h] = dgn_acc[h] + jnp.sum(don * ohat, axis=0, keepdims=True)
                dn = don * gnh
                do.append((rstd * (dn - ohat * jnp.mean(dn * ohat, axis=-1, keepdims=True))).astype(BF16))
            dq = [_dot(do[h], s_b[h]) for h in heads]
            g_t = [_dot(do[h], qs[h], _TN) + carry[h] for h in heads]
            g_b = [g_t[h].astype(BF16) for h in heads]
            dv = [_dot(kdec[rows, kcs[h]], g_b[h], _NT) for h in heads]
            dkdec = [_dot(v_ref[rows, vcs[h]], g_b[h]) for h in heads]
            for h in heads:
                s_prev = st_ref[c - 1, h] if c > 0 else sp_ref[0, h] * has_prev
                ddec = jnp.sum(g_t[h] * s_prev, axis=0, keepdims=True)
                carry[h] = g_t[h] * dec[:, kcs[h]]
                dp_ref[rows, OFF_Q + h * GLA_DK:OFF_Q + (h + 1) * GLA_DK] = (dq[h] * (GLA_DK ** -0.5)).astype(BF16)
                dp_ref[rows, OFF_V + h * GLA_DV:OFF_V + (h + 1) * GLA_DV] = dv[h].astype(BF16)
                dp_ref[rows, OFF_K + h * GLA_DK:OFF_K + (h + 1) * GLA_DK] = (dkdec[h] * e[rows, kcs[h]]).astype(BF16)
                dd_scr[rows, kcs[h]] = dkdec[h] * kdec_f[rows, kcs[h]]
                dt_scr[rows, kcs[h]] = jnp.broadcast_to(ddec * dec[:, kcs[h]], (CHUNK, GLA_DK))
        for h in heads:
            g_scr[h] = carry[h]
            dgn_ref[:, vcs[h]] += dgn_acc[h]
        dla = _dot_exact_lhs(earlier_ref[...], dd_scr[...]) + dt_scr[...]
        dlogit = dla * (1.0 / GLA_TAU) * _sigmoid(-logit)
        dbg_ref[...] += jnp.sum(dlogit, axis=0, keepdims=True)
        dwgu_ref[...] += _dot_f32(al_ref[...], dlogit, _TN)
        dal_ref[...] = _dot_f32(dlogit, wgu_ref[...], _NT).astype(BF16)

    rev = lambda i: nt - 1 - i
    blk = lambda w, j: pl.BlockSpec((tT, w), lambda i: (rev(i), j))
    st_blk = pl.BlockSpec((nc, GLA_HEADS, GLA_DV, GLA_DK), lambda i: (rev(i), 0, 0, 0))
    sp_blk = pl.BlockSpec((1, GLA_HEADS, GLA_DV, GLA_DK), lambda i: (jnp.maximum(rev(i) * nc - 1, 0), 0, 0, 0))
    return _call(
        body, name="gla_bwd", grid=(nt,),
        in_specs=[blk(512, 0), blk(512, 1), blk(1024, 1), blk(1024, 2), blk(LANES, 0)] + [_whole()] * 5
        + [st_blk, sp_blk, blk(GLA_V, 0)],
        out_specs=[blk(W_GLA, 0), blk(LANES, 0), pl.BlockSpec((1, GLA_V), lambda i: (0, 0)),
                   pl.BlockSpec((1, GLA_QK), lambda i: (0, 0)), pl.BlockSpec((LANES, GLA_QK), lambda i: (0, 0))],
        out_shape=[_sds((T, W_GLA), BF16), _sds((T, LANES), BF16), _sds((1, GLA_V), F32), _sds((1, GLA_QK), F32),
                   _sds((LANES, GLA_QK), F32)],
        scratch_shapes=[pltpu.VMEM((GLA_HEADS, GLA_DV, GLA_DK), F32), pltpu.VMEM((tT, GLA_QK), F32),
                        pltpu.VMEM((tT, GLA_QK), F32)],
        args=(proj, proj, proj, proj, alow, wgu, b_gate, gn, _chunk_masks(tT, upper=True), _chunk_masks(tT, upper=False),
              states, states, dy_gla), job=job)


def _inproj_bwd(x, dx1, g1, w_all, dparts, job=None):
    T = x.shape[0]
    tT = _row_tile(T, 512)
    offs = (0, W_GLA, W_GLA + W_SGU, N_MAIN)

    def body(x_ref, dx1_ref, g_ref, w_ref, *rest):
        part_refs, (dx_ref, dg_ref) = rest[:len(offs)], rest[len(offs):]

        @pl.when(pl.program_id(0) == 0)
        def _():
            dg_ref[...] = jnp.zeros_like(dg_ref)

        da = jnp.zeros((tT, D_MODEL), F32)
        for off, p_ref in zip(offs, part_refs):
            da = da + _dot(p_ref[...], w_ref[:, off:off + p_ref.shape[1]], _NT)
        xv = x_ref[...]
        dx, dg = _rms_bwd(da, xv, _rms_stats(xv), g_ref[...])
        dg_ref[...] += jnp.sum(dg, axis=0, keepdims=True)
        dx_ref[...] = dx1_ref[...] + dx

    row = lambda w: pl.BlockSpec((tT, w), lambda i: (i, 0))
    vec = pl.BlockSpec((1, D_MODEL), lambda i: (0, 0))
    return _call(
        body, name="inproj_bwd", grid=(T // tT,),
        in_specs=[row(D_MODEL), row(D_MODEL), vec, _whole()] + [row(p.shape[1]) for p in dparts],
        out_specs=[row(D_MODEL), vec], out_shape=[_sds((T, D_MODEL), F32), _sds((1, D_MODEL), F32)],
        args=(x, dx1, g1, w_all, *dparts), job=job)


def _tn_matmul(a, b, name, job=None):
    T, M = a.shape
    N = b.shape[1]
    tk = _row_tile(T, 1024)
    tm = M if M <= 1024 else 1408
    tn = N if N <= 3072 else N // 2
    assert M % tm == 0 and N % tn == 0

    def body(a_ref, b_ref, o_ref):
        @pl.when(pl.program_id(2) == 0)
        def _():
            o_ref[...] = _dot(a_ref[...], b_ref[...], _TN)

        @pl.when(pl.program_id(2) > 0)
        def _():
            o_ref[...] += _dot(a_ref[...], b_ref[...], _TN)

    res, jres = _call(
        body, name=name, grid=(M // tm, N // tn, T // tk),
        in_specs=[pl.BlockSpec((tk, tm), lambda i, j, k: (k, i)), pl.BlockSpec((tk, tn), lambda i, j, k: (k, j))],
        out_specs=[pl.BlockSpec((tm, tn), lambda i, j, k: (i, j))], out_shape=[_sds((M, N), F32)], args=(a, b), job=job)
    return res[0], jres


def _pad_rows(a, rows=8):
    return jnp.pad(a, ((0, rows - a.shape[0]), (0, LANES - a.shape[1])))


def _halves_view(dw):
    r = dw.shape[0] // N_CHIPS
    return dw.reshape(N_CHIPS, 2, r // 2, dw.shape[1])


def kernel(x, norm_pre_mix, w_in, w_gate_up, b_gate, gla_norm, sgu_ln_g, sgu_ln_b, w_spatial, b_spatial, w_branch_gla, w_branch_sgu, w_out, norm_post_mix, norm_pre_ffn, w_ffn_in, w_ffn_out, norm_post_ffn, loss_target, m_norm_pre_mix, m_w_in, m_w_gate_up, m_b_gate, m_gla_norm, m_sgu_ln_g, m_sgu_ln_b, m_w_spatial, m_b_spatial, m_w_branch_gla, m_w_branch_sgu, m_w_out, m_norm_post_mix, m_norm_pre_ffn, m_w_ffn_in, m_w_ffn_out, m_norm_post_ffn, v_norm_pre_mix, v_w_in, v_w_gate_up, v_b_gate, v_gla_norm, v_sgu_ln_g, v_sgu_ln_b, v_w_spatial, v_b_spatial, v_w_branch_gla, v_w_branch_sgu, v_w_out, v_norm_post_mix, v_norm_pre_ffn, v_w_ffn_in, v_w_ffn_out, v_norm_post_ffn):
    chip = 2 * lax.axis_index("x") + lax.axis_index("y")
    xt, tgt = x[0], loss_target[0]

    tiny = jnp.concatenate([w_gate_up[0], _pad_rows(gla_norm[0]), _pad_rows(sgu_ln_g[0]), _pad_rows(sgu_ln_b[0]),
                            jnp.zeros((8, LANES), F32)], axis=0)
    def with_own(gathered, own):
        if gathered.ndim == 3:
            return lax.dynamic_update_slice(gathered, own[None], (chip, 0, 0))
        return lax.dynamic_update_slice(gathered, own, (0, chip * own.shape[1]))

    w_in_t, m_in_t, v_in_t = w_in[0].T, m_w_in[0].T, v_w_in[0].T
    w_in_b = _transposed_cast(w_in_t)
    g_in, g_tiny = _run_job(_job_gather([w_in_b, tiny], [False, False]), "gather_w_in")
    g_tiny = with_own(g_tiny, tiny)
    w_all = _relayout_w_in(with_own(g_in, w_in_b))
    cols = lambda a: a.transpose(1, 0, 2).reshape(a.shape[1], N_CHIPS * a.shape[2])
    wgu = jnp.pad(cols(g_tiny[:, 0:16]), ((0, LANES - GLA_RANK), (0, 0)))
    gn = cols(g_tiny[:, 16:20, :64]).reshape(1, GLA_V)
    ln_g = cols(g_tiny[:, 24:28, :64]).reshape(1, 1024)
    ln_b = cols(g_tiny[:, 32:36, :64]).reshape(1, 1024)
    b_sp_t = jnp.pad(b_spatial[0].T, ((0, 0), (0, LANES - SGU_GROUPS)))
    w_sp = w_spatial[0]

    own_rows = [w_branch_gla[0].astype(BF16), w_branch_sgu[0].astype(BF16), w_out[0].astype(BF16), w_ffn_out[0].astype(BF16)]
    (a, proj, alow), g_rows = _inproj_fwd(xt, norm_pre_mix, w_all, job=_job_gather(own_rows, [False] * 4))
    rows = lambda g: g.reshape(N_CHIPS * g.shape[1], g.shape[2])
    w_bg, w_bs, w_o, w_fo = [rows(with_own(g, own)) for g, own in zip(g_rows, own_rows)]
    w_fi_b = w_ffn_in[0].astype(BF16)
    fi_top, fi_bot = w_fi_b[:D_MODEL // 2], w_fi_b[D_MODEL // 2:]
    (y_gla, states), (g_top,) = _gla_fwd(proj, alow, wgu, b_gate, gn, job=_job_gather([fi_top], [False]))
    y_sgu = _sgu_fwd(proj, ln_g, ln_b, w_sp, b_sp_t)
    (zg, zs, merged, mix, x1), (g_bot,) = _merge_fwd(xt, proj, y_gla, y_sgu, w_bg, w_bs, w_o, norm_post_mix,
                                                     job=_job_gather([fi_bot], [False]))
    h, f, dgu, dy, dx1, loss, d_gpf, d_gpo = _ffn_fwd_bwd(x1, tgt, with_own(g_top, fi_top), with_own(g_bot, fi_bot),
                                                          w_fo, norm_pre_ffn, norm_post_ffn)

    own_part = lambda c: lax.dynamic_index_in_dim(c, chip, 0, keepdims=False)
    whole = lambda hs: [[(h_, None)] for h_ in hs]
    dw_fo, _ = _tn_matmul(f, dy, "dw_ffn_out")
    dw_fo4 = _halves_view(dw_fo)
    dw_fi, (q_fo,) = _tn_matmul(h, dgu, "dw_ffn_in", job=_job_to_other_core([[(dw_fo4, 0)]]))
    c_fo = _presum(dw_fo4, q_fo, "presum_ffn_out")
    (dmix, dzg, dzs, dp_mrg, dyg, dys, d_gpm), (s_fo, q_fi) = _merge_bwd(
        dx1, mix, proj, zg, zs, w_bg, w_bs, w_o, norm_post_mix,
        job=_join(_job_scatter([c_fo]), _job_to_other_core([[(dw_fi, 0)]])))
    c_fi = _presum(dw_fi, q_fi, "presum_ffn_in")
    dw_c, _ = _tn_matmul(a, dp_mrg, "dw_in_merge")
    dw_o4 = _halves_view(_tn_matmul(merged, dmix, "dw_out")[0])
    dw_bg4 = _halves_view(_tn_matmul(y_gla, dzg, "dw_branch_gla")[0])
    dw_bs4 = _halves_view(_tn_matmul(y_sgu, dzs, "dw_branch_sgu")[0])
    (dp_sgu, d_wsp, d_bsp_t, d_lng, d_lnb), (q_o, q_bg, q_bs, q_c) = _sgu_bwd(
        proj, dys, ln_g, ln_b, w_sp, b_sp_t,
        job=_job_to_other_core([[(dw_o4, 0)], [(dw_bg4, 0)], [(dw_bs4, 0)], [(dw_c, 0)]]))
    c_o, c_bg, c_bs = (_presum(dw_o4, q_o, "presum_out"), _presum(dw_bg4, q_bg, "presum_branch_gla"),
                       _presum(dw_bs4, q_bs, "presum_branch_sgu"))
    h_fo = _sum_slots(own_part(c_fo), s_fo, "sum_ffn_out")
    dw_b, _ = _tn_matmul(a, dp_sgu, "dw_in_sgu")
    (dp_gla, dal, d_gn, d_bg, d_wgu), (s_fi, t_fo, q_b) = _gla_bwd(
        proj, alow, wgu, b_gate, gn, states, dyg,
        job=_join(_job_scatter([c_fi]), _job_to_other_core(whole([h_fo]) + [[(dw_b, 0)]])))
    h_fi = _sum_slots(own_part(c_fi), s_fi, "sum_ffn_in")
    dw_d, _ = _tn_matmul(a, dal, "dw_in_gate")
    dw_a, (s_o, s_bg, s_bs, t_fi, q_d) = _tn_matmul(
        a, dp_gla, "dw_in_gla",
        job=_join(_job_scatter([c_o, c_bg, c_bs]), _job_to_other_core(whole([h_fi]) + [[(dw_d, 0)]])))
    h_o, h_bg, h_bs = (_sum_slots(own_part(c_o), s_o, "sum_out"), _sum_slots(own_part(c_bg), s_bg, "sum_branch_gla"),
                       _sum_slots(own_part(c_bs), s_bs, "sum_branch_sgu"))

    grads, deltas, new_m, new_v = {}, {}, {}, {}

    def update(name, w, m, v, g_mine, g_theirs, job=None):
        (g, d, m2, v2), jres = _adamw(w[0], m[0], v[0], g_mine, g_theirs, "adamw_" + name, job=job)
        grads[name], deltas[name], new_m[name], new_v[name] = g[None], d[None], m2[None], v2[None]
        return jres

    dw_in = [(dw_a, 0), (dw_b, W_GLA), (dw_c, W_GLA + W_SGU), (dw_d, N_MAIN)]
    q_a, t_o, t_bg, t_bs = update("w_ffn_out", w_ffn_out, m_w_ffn_out, v_w_ffn_out, [h_fo], [t_fo],
                                  job=_job_to_other_core([[(dw_a, 0)]] + whole([h_o, h_bg, h_bs])))
    q_in = [q_a, q_b, q_c, q_d]
    hr_in = D_MODEL // 2
    c_in_a, _ = _presum_w_in(dw_in, q_in, 0, hr_in // 8, "presum_w_in_a")
    c_in_b, (s_in_a,) = _presum_w_in(dw_in, q_in, hr_in // 8, 7 * hr_in // 8, "presum_w_in_b",
                                     job=_job_scatter([c_in_a]))
    update("w_ffn_in", w_ffn_in, m_w_ffn_in, v_w_ffn_in, [h_fi], [t_fi])
    update("w_out", w_out, m_w_out, v_w_out, [h_o], [t_o])
    update("w_branch_gla", w_branch_gla, m_w_branch_gla, v_w_branch_gla, [h_bg], [t_bg])
    update("w_branch_sgu", w_branch_sgu, m_w_branch_sgu, v_w_branch_sgu, [h_bs], [t_bs])
    (grad_x, d_g1), (s_in_b,) = _inproj_bwd(xt, dx1, norm_pre_mix, w_all, (dp_gla, dp_sgu, dp_mrg, dal),
                                            job=_job_scatter([c_in_b]))
    h_in = [_sum_slots(own_part(c_in_a), s_in_a, "sum_w_in_a"), _sum_slots(own_part(c_in_b), s_in_b, "sum_w_in_b")]
    t_in = _run_job(_job_to_other_core(whole(h_in)), "swap_w_in")
    for store, val in zip((grads, deltas, new_m, new_v),
                          _adamw_transposed(w_in_t, m_in_t, v_in_t, h_in, t_in, "adamw_w_in")):
        store["w_in"] = val.T[None]

    small_names = ["w_spatial", "w_gate_up", "norm_pre_mix", "norm_post_mix", "norm_pre_ffn", "norm_post_ffn", "b_gate",
                   "b_spatial", "gla_norm", "sgu_ln_g", "sgu_ln_b"]
    loss_out, small = _small_adamw(
        _small_sum([d_wsp, d_wgu, d_g1, d_gpm, d_gpf, d_gpo, d_bg, d_bsp_t, d_gn, d_lng, d_lnb, loss]),
        [w_spatial, w_gate_up, norm_pre_mix, norm_post_mix, norm_pre_ffn, norm_post_ffn, b_gate, b_spatial, gla_norm,
         sgu_ln_g, sgu_ln_b],
        [m_w_spatial, m_w_gate_up, m_norm_pre_mix, m_norm_post_mix, m_norm_pre_ffn, m_norm_post_ffn, m_b_gate,
         m_b_spatial, m_gla_norm, m_sgu_ln_g, m_sgu_ln_b],
        [v_w_spatial, v_w_gate_up, v_norm_pre_mix, v_norm_post_mix, v_norm_pre_ffn, v_norm_post_ffn, v_b_gate,
         v_b_spatial, v_gla_norm, v_sgu_ln_g, v_sgu_ln_b])
    for store, vals in zip((grads, deltas, new_m, new_v), small):
        store.update(zip(small_names, vals))

    order = ["norm_pre_mix", "w_in", "w_gate_up", "b_gate", "gla_norm", "sgu_ln_g", "sgu_ln_b", "w_spatial", "b_spatial",
             "w_branch_gla", "w_branch_sgu", "w_out", "norm_post_mix", "norm_pre_ffn", "w_ffn_in", "w_ffn_out",
             "norm_post_ffn"]
    out = [loss_out, grad_x[None]]
    for store in (grads, deltas, new_m, new_v):
        out.extend(store[n] for n in order)
    return tuple(out)
```

```python
import jax
import jax.numpy as jnp
from jax import lax
from jax.experimental import pallas as pl
from jax.experimental.pallas import tpu as pltpu

F32 = jnp.float32
BF16 = jnp.bfloat16

D_MODEL = 1024
GLA_HEADS = 4
GLA_DK = 128
GLA_DV = 256
GLA_QK = GLA_HEADS * GLA_DK
GLA_V = GLA_HEADS * GLA_DV
GLA_RANK = 16
GLA_TAU = 16.0
CHUNK = 64
SGU_GROUPS = 4
SGU_BLOCK = 128
SGU_DG = 256
D_FF = 2816
EPS = 1e-6
LANES = 128

OFF_Q, OFF_K, OFF_V, OFF_R, OFF_SU, OFF_SV, OFF_GG, OFF_GS, OFF_AL = 0, 512, 1024, 2048, 3072, 4096, 5120, 6144, 7168
W_GLA, W_SGU, W_MRG = 3072, 2048, 2048
N_MAIN = 7168
N_ALL = N_MAIN + LANES
_IN_SPLITS = (GLA_QK, GLA_QK, GLA_V, GLA_V, GLA_RANK, 1024, 1024, 1024, 1024)
_IN_STARTS = tuple(sum(_IN_SPLITS[:i]) for i in range(len(_IN_SPLITS) + 1))
_IN_DST = (OFF_Q, OFF_K, OFF_V, OFF_R, OFF_AL, OFF_SU, OFF_SV, OFF_GG, OFF_GS)
D_IN = _IN_STARTS[-1]

ADAM_LR = 0.001
ADAM_B1 = 0.9
ADAM_B2 = 0.999
ADAM_EPS = 1e-08
ADAM_WD = 0.01
ADAM_STEP = 10

VMEM_LIMIT_BYTES = 56 * 1024 * 1024
N_CHIPS = 4
N_PEER = N_CHIPS - 1
N_DEV = 8
MESH = pl.DeviceIdType.MESH

_NN = (((1,), (0,)), ((), ()))
_NT = (((1,), (1,)), ((), ()))
_TN = (((0,), (0,)), ((), ()))


def _dot(a, b, dims=_NN):
    return lax.dot_general(a, b, dims, preferred_element_type=F32)


def _split(x):
    hi = x.astype(BF16)
    lo = (x - hi.astype(F32)).astype(BF16)
    return hi, lo


def _dot_f32(a, b, dims=_NN):
    ah, al = _split(a)
    bh, bl = _split(b)
    return _dot(ah, bh, dims) + (_dot(al, bh, dims) + _dot(ah, bl, dims))


def _dot_exact_lhs(m, x):
    xh, xl = _split(x)
    return _dot(m, xh) + _dot(m, xl)


def _sigmoid(x):
    return 0.5 * jnp.tanh(0.5 * x) + 0.5


def _log_sigmoid(x):
    return jnp.minimum(x, 0.0) - jnp.log(1.0 + jnp.exp(-jnp.abs(x)))


_GELU_C = 0.7978845608028654
_GELU_A = 0.044715


def _gelu_and_grad(x):
    x2 = x * x
    t = jnp.tanh(_GELU_C * (x + _GELU_A * x * x2))
    g = 0.5 * x * (1.0 + t)
    dg = 0.5 * (1.0 + t) + 0.5 * x * (1.0 - t * t) * (_GELU_C * (1.0 + 3.0 * _GELU_A * x2))
    return g, dg


def _gelu(x):
    t = jnp.tanh(_GELU_C * (x + _GELU_A * x * x * x))
    return 0.5 * x * (1.0 + t)


def _rms_stats(x):
    return lax.rsqrt(jnp.mean(x * x, axis=-1, keepdims=True) + EPS)


def _rms_bwd(dout, y, r, g):
    yhat = y * r
    dn = dout * g
    dy = r * (dn - yhat * jnp.mean(dn * yhat, axis=-1, keepdims=True))
    return dy, dout * yhat


def _whole():
    return pl.BlockSpec(memory_space=pltpu.VMEM)


def _row_tile(T, want):
    t = min(T, want)
    assert T % t == 0
    return t


def _chunk_masks(tT, upper):
    row = lax.broadcasted_iota(jnp.int32, (tT, tT), 0)
    col = lax.broadcasted_iota(jnp.int32, (tT, tT), 1)
    same = (row // CHUNK) == (col // CHUNK)
    tri = (col > row) if upper else (col < row)
    return jnp.where(same & tri, 1.0, 0.0).astype(BF16)


class _Job:
    def __init__(self, ins, out_shapes, scratch, start, finish, mid=None):
        self.ins, self.out_shapes, self.scratch = list(ins), list(out_shapes), list(scratch)
        self.start, self.finish, self.mid = start, finish, mid


def _join(*jobs):
    def split(refs, counts):
        out, at = [], 0
        for n in counts:
            out.append(refs[at:at + n])
            at += n
        return out

    ni, no, ns = [len(j.ins) for j in jobs], [len(j.out_shapes) for j in jobs], [len(j.scratch) for j in jobs]

    def start(ins, outs, scr):
        for j, a, b, c in zip(jobs, split(ins, ni), split(outs, no), split(scr, ns)):
            j.start(a, b, c)

    def finish(ins, outs, scr):
        for j, a, b, c in zip(jobs, split(ins, ni), split(outs, no), split(scr, ns)):
            j.finish(a, b, c)

    def mid(ins, outs, scr):
        for j, a, b, c in zip(jobs, split(ins, ni), split(outs, no), split(scr, ns)):
            if j.mid is not None:
                j.mid(a, b, c)

    return _Job(sum((j.ins for j in jobs), []), sum((j.out_shapes for j in jobs), []),
                sum((j.scratch for j in jobs), []), start, finish, mid if any(j.mid for j in jobs) else None)


def _mesh_pos():
    return lax.axis_index("x"), lax.axis_index("y"), lax.axis_index("c")


def _peer_chips(xi, yi):
    return [(1 - xi, yi), (xi, 1 - yi), (1 - xi, 1 - yi)]


def _half(ci, rows):
    return pl.ds(pl.multiple_of(ci * rows, 8), rows)


def _sds(shape, dtype):
    return jax.ShapeDtypeStruct(tuple(shape), dtype)


def _job_gather(arrs):
    n = len(arrs)
    kinds = 12
    Y0, Y1, X1, X0, ON_X, ON_Y, D2D = 0, 1, 2, 3, 4, 5, 6

    def copies(ins, outs, scr):
        send_sems, recv_sems = scr
        xi, yi, ci = _mesh_pos()
        me, cx, cy, cd = 2 * xi + yi, 2 * (1 - xi) + yi, 2 * xi + (1 - yi), 2 * (1 - xi) + (1 - yi)
        to_x, to_y, to_core = (1 - xi, yi, ci), (xi, 1 - yi, ci), (xi, yi, 1 - ci)
        table = []
        for k in range(n):
            qr = arrs[k].shape[0] // 4

            def rows(core, q):
                return pl.ds(pl.multiple_of((2 * core + q) * qr, 8), qr)

            def cp(kind, src, dst, to):
                s = k * kinds + kind
                return pltpu.make_async_remote_copy(src_ref=src, dst_ref=dst, send_sem=send_sems.at[s],
                                                    recv_sem=recv_sems.at[s], device_id=to, device_id_type=MESH)

            def slab(chip, core, q):
                return outs[k].at[chip, rows(core, q)]

            t = {}
            for kind, q, to, frm in ((Y0, 0, to_y, cy), (Y1, 1, to_y, cy), (X1, 1, to_x, cx), (X0, 0, to_x, cx)):
                mine = ins[k].at[rows(ci, q)]
                t[kind] = (cp(kind, mine, slab(me, ci, q), to), cp(kind, mine, slab(frm, ci, q), to))
            t[ON_X] = (cp(ON_X, slab(cy, ci, 0), slab(cy, ci, 0), to_x), cp(ON_X, slab(cy, ci, 0), slab(cd, ci, 0), to_x))
            t[ON_Y] = (cp(ON_Y, slab(cx, ci, 1), slab(cx, ci, 1), to_y), cp(ON_Y, slab(cx, ci, 1), slab(cd, ci, 1), to_y))
            for i, (chip, q) in enumerate(((cy, 0), (cy, 1), (cx, 1), (cx, 0), (cd, 0), (cd, 1))):
                t[D2D + i] = (cp(D2D + i, slab(chip, ci, q), slab(chip, ci, q), to_core),
                              cp(D2D + i, slab(chip, ci, q), slab(chip, 1 - ci, q), to_core))
            table.append(t)
        return table

    def start(ins, outs, scr):
        table = copies(ins, outs, scr)
        for kind in (Y0, X1, Y1, X0):
            for t in table:
                t[kind][0].start()

    def arrived(table, kind, then):
        for t in table:
            t[kind][1].wait_recv()
            for nxt in then:
                t[nxt][0].start()

    def mid(ins, outs, scr):
        table = copies(ins, outs, scr)
        arrived(table, Y0, (ON_X, D2D + 0))
        arrived(table, X1, (ON_Y, D2D + 2))

    def finish(ins, outs, scr):
        table = copies(ins, outs, scr)
        arrived(table, Y1, (D2D + 1,))
        arrived(table, X0, (D2D + 3,))
        arrived(table, ON_X, (D2D + 4,))
        arrived(table, ON_Y, (D2D + 5,))
        for t in table:
            for i in range(6):
                t[D2D + i][1].wait_recv()
            for kind in range(kinds):
                t[kind][0].wait_send()

    dma = pltpu.SemaphoreType.DMA
    return _Job(arrs, [_sds((N_CHIPS,) + a.shape, a.dtype) for a in arrs], [dma((n * kinds,))] * 2, start, finish, mid)


def _job_scatter(parts):
    n = len(parts)

    def copies(ins, outs, scr):
        send_sems, recv_sems = scr
        xi, yi, ci = _mesh_pos()
        res = []
        for k in range(n):
            for j, (px, py) in enumerate(_peer_chips(xi, yi)):
                s = k * N_PEER + j
                res.append(pltpu.make_async_remote_copy(
                    src_ref=ins[k].at[2 * px + py], dst_ref=outs[k].at[j], send_sem=send_sems.at[s],
                    recv_sem=recv_sems.at[s], device_id=(px, py, ci), device_id_type=MESH))
        return res

    def start(ins, outs, scr):
        for cp in copies(ins, outs, scr):
            cp.start()

    def finish(ins, outs, scr):
        for cp in copies(ins, outs, scr):
            cp.wait_recv()
            cp.wait_send()

    dma = pltpu.SemaphoreType.DMA
    return _Job(parts, [_sds((N_PEER,) + p.shape[1:], p.dtype) for p in parts], [dma((n * N_PEER,))] * 2, start, finish)


def _job_to_other_core(groups):
    pieces = [(g, a, off) for g, group in enumerate(groups) for a, off in group]
    n = len(pieces)

    def geometry(group):
        a0, off0 = group[0]
        if off0 is None:
            return a0.shape
        if a0.ndim == 4:
            return (N_CHIPS, a0.shape[2], a0.shape[3])
        return (a0.shape[0] // 2, sum(a.shape[1] for a, _ in group))

    def copies(ins, outs, scr):
        send_sems, recv_sems = scr
        xi, yi, ci = _mesh_pos()
        res = []
        for p, (g, a, off) in enumerate(pieces):
            if off is None:
                give, land = ins[p], outs[g]
            elif a.ndim == 4:
                give, land = ins[p].at[pl.ds(0, N_CHIPS), 1 - ci], outs[g]
            else:
                hr, w = a.shape[0] // 2, a.shape[1]
                give, land = ins[p].at[_half(1 - ci, hr)], outs[g].at[pl.ds(0, hr), pl.ds(off, w)]
            res.append(pltpu.make_async_remote_copy(
                src_ref=give, dst_ref=land, send_sem=send_sems.at[p], recv_sem=recv_sems.at[p],
                device_id=(xi, yi, 1 - ci), device_id_type=MESH))
        return res

    def start(ins, outs, scr):
        for cp in copies(ins, outs, scr):
            cp.start()

    def finish(ins, outs, scr):
        for cp in copies(ins, outs, scr):
            cp.wait_recv()
            cp.wait_send()

    dma = pltpu.SemaphoreType.DMA
    return _Job([a for _, a, _ in pieces], [_sds(geometry(group), group[0][0].dtype) for group in groups],
                [dma((n,))] * 2, start, finish)


def _call(body, *, name, grid, in_specs, out_specs, out_shape, args, scratch_shapes=(), parallel=False, job=None):
    n_in, n_out, n_scr = len(in_specs), len(out_specs), len(scratch_shapes)

    def params(sem):
        return pltpu.CompilerParams(dimension_semantics=sem, vmem_limit_bytes=VMEM_LIMIT_BYTES)

    if job is None:
        sem = ("parallel" if parallel else "arbitrary",) * len(grid)
        res = pl.pallas_call(
            body, name=name, grid=grid, in_specs=in_specs, out_specs=out_specs, out_shape=out_shape,
            scratch_shapes=list(scratch_shapes), compiler_params=params(sem))(*args)
        return list(res), []
    n_ji, n_jo = len(job.ins), len(job.out_shapes)

    def carried(*refs):
        ins, refs = refs[:n_in], refs[n_in:]
        j_ins, refs = refs[:n_ji], refs[n_ji:]
        outs, refs = refs[:n_out], refs[n_out:]
        j_outs, refs = refs[:n_jo], refs[n_jo:]
        scr, j_scr = refs[:n_scr], refs[n_scr:]
        ids = [pl.program_id(d) for d in range(len(grid))]
        first = ids[0] == 0
        last = ids[0] == grid[0] - 1
        for d in range(1, len(grid)):
            first = first & (ids[d] == 0)
            last = last & (ids[d] == grid[d] - 1)

        @pl.when(first)
        def _():
            job.start(j_ins, j_outs, j_scr)

        if job.mid is not None and grid[0] >= 4:
            half_way = ids[0] == grid[0] // 2
            for d in range(1, len(grid)):
                half_way = half_way & (ids[d] == 0)

            @pl.when(half_way)
            def _():
                job.mid(j_ins, j_outs, j_scr)

        body(*ins, *outs, *scr)

        @pl.when(last)
        def _():
            if job.mid is not None and grid[0] < 4:
                job.mid(j_ins, j_outs, j_scr)
            job.finish(j_ins, j_outs, j_scr)

    hbm = pl.BlockSpec(memory_space=pl.ANY)
    res = pl.pallas_call(
        carried, name=name, grid=grid, in_specs=list(in_specs) + [hbm] * n_ji, out_specs=list(out_specs) + [hbm] * n_jo,
        out_shape=list(out_shape) + job.out_shapes, scratch_shapes=list(scratch_shapes) + job.scratch,
        compiler_params=params(("arbitrary",) * len(grid)))(*args, *job.ins)
    return list(res[:n_out]), list(res[n_out:])


def _run_job(job, name):
    n_i, n_o = len(job.ins), len(job.out_shapes)

    def body(*refs):
        ins, outs, scr = refs[:n_i], refs[n_i:n_i + n_o], refs[n_i + n_o:]
        job.start(ins, outs, scr)
        if job.mid is not None:
            job.mid(ins, outs, scr)
        job.finish(ins, outs, scr)

    hbm = pl.BlockSpec(memory_space=pl.ANY)
    return list(pl.pallas_call(body, name=name, in_specs=[hbm] * n_i, out_specs=[hbm] * n_o, out_shape=job.out_shapes,
                               scratch_shapes=job.scratch)(*job.ins))


def _adam_values(w, m, v, g):
    m2 = ADAM_B1 * m + (1.0 - ADAM_B1) * g
    v2 = ADAM_B2 * v + (1.0 - ADAM_B2) * (g * g)
    delta = -ADAM_LR * ((m2 / (1.0 - ADAM_B1 ** ADAM_STEP)) / (jnp.sqrt(v2 / (1.0 - ADAM_B2 ** ADAM_STEP)) + ADAM_EPS)
                        + ADAM_WD * w)
    return delta, m2, v2


_P_WSP, _P_WGU, _P_NORM, _P_BG, _P_BSP, _P_HEAD, _P_LOSS, _P_ROWS = 0, 512, 576, 608, 616, 624, 720, 728


def _small_sum(dgrads):
    def body(dwsp, dwgu, dg1, dgpm, dgpf, dgpo, dbg, dbspt, dgn, dlng, dlnb, loss_in, tot, pack, slots, send_sems,
             recv_sems):
        xi, yi, ci = _mesh_pos()
        chip = 2 * xi + yi

        pack[...] = jnp.zeros_like(pack)
        for g in range(SGU_GROUPS):
            pack[_P_WSP + g * SGU_BLOCK:_P_WSP + (g + 1) * SGU_BLOCK] = dwsp[g]
        for j in range(N_CHIPS):
            pack[_P_WGU + GLA_RANK * j:_P_WGU + GLA_RANK * (j + 1)] = dwgu[0:GLA_RANK, LANES * j:LANES * (j + 1)]
        for k, r in enumerate((dg1, dgpm, dgpf, dgpo)):
            for q in range(8):
                pack[_P_NORM + 8 * k + q:_P_NORM + 8 * k + q + 1] = r[:, LANES * q:LANES * (q + 1)]
        for q in range(4):
            pack[_P_BG + q:_P_BG + q + 1] = dbg[:, LANES * q:LANES * (q + 1)]
        pack[_P_BSP:_P_BSP + SGU_GROUPS] = jnp.transpose(dbspt[...])[0:SGU_GROUPS]
        for k, r in enumerate((dgn, dlng, dlnb)):
            for j in range(N_CHIPS):
                for hh in range(4):
                    row = _P_HEAD + 32 * k + 8 * j + hh
                    pack[row:row + 1, 0:64] = r[:, 256 * hh + 64 * j:256 * hh + 64 * (j + 1)]
        pack[_P_LOSS:_P_LOSS + 1] = loss_in[...]

        to_sibling = pltpu.make_async_remote_copy(
            src_ref=pack, dst_ref=tot, send_sem=send_sems.at[N_PEER], recv_sem=recv_sems.at[N_PEER],
            device_id=(xi, yi, 1 - ci), device_id_type=MESH)
        to_sibling.start()
        to_sibling.wait_recv()
        to_sibling.wait_send()
        pack[...] = pack[...] + tot[...]
        slots[chip] = pack[...]

        def copy(j, slot):
            px, py = _peer_chips(xi, yi)[j]
            return pltpu.make_async_remote_copy(
                src_ref=pack, dst_ref=slots.at[slot(2 * px + py)], send_sem=send_sems.at[j], recv_sem=recv_sems.at[j],
                device_id=(px, py, ci), device_id_type=MESH)

        sends = [copy(j, lambda peer_chip: chip) for j in range(N_PEER)]
        for cp in sends:
            cp.start()
        for j in range(N_PEER):
            copy(j, lambda peer_chip: peer_chip).wait_recv()
        for cp in sends:
            cp.wait_send()
        acc = slots[0]
        for d in range(1, N_CHIPS):
            acc = acc + slots[d]
        tot[...] = acc

    return pl.pallas_call(
        body, name="small_sum", in_specs=[_whole()] * 12, out_specs=_whole(), out_shape=_sds((_P_ROWS, LANES), F32),
        scratch_shapes=[pltpu.VMEM((_P_ROWS, LANES), F32), pltpu.VMEM((N_CHIPS, _P_ROWS, LANES), F32),
                        pltpu.SemaphoreType.DMA((N_PEER + 1,)), pltpu.SemaphoreType.DMA((N_PEER + 1,))],
        compiler_params=pltpu.CompilerParams(vmem_limit_bytes=VMEM_LIMIT_BYTES),
    )(*dgrads)


def _small_adamw(tot, ws, ms, vs):
    n = len(ws)

    def body(*refs):
        tot = refs[0]
        w_refs, m_refs, v_refs = refs[1:1 + n], refs[1 + n:1 + 2 * n], refs[1 + 2 * n:1 + 3 * n]
        loss_out = refs[1 + 3 * n]
        outs = refs[2 + 3 * n:]
        chip = 2 * lax.axis_index("x") + lax.axis_index("y")
        loss_out[...] = tot[_P_LOSS:_P_LOSS + 1, 0:1]

        def step(k, g, pick, put):
            d, m2, v2 = _adam_values(pick(w_refs[k]), pick(m_refs[k]), pick(v_refs[k]), g)
            for o, val in zip((outs[k], outs[n + k], outs[2 * n + k], outs[3 * n + k]), (g, d, m2, v2)):
                put(o, val)

        def whole(ref):
            return ref[0]

        def put_whole(ref, val):
            ref[0] = val

        for g in range(SGU_GROUPS):
            def pick_g(ref, g=g):
                return ref[0, g]

            def put_g(ref, val, g=g):
                ref[0, g] = val

            step(0, tot[_P_WSP + g * SGU_BLOCK:_P_WSP + (g + 1) * SGU_BLOCK], pick_g, put_g)
        step(1, tot[pl.ds(pl.multiple_of(_P_WGU + GLA_RANK * chip, GLA_RANK), GLA_RANK), :], whole, put_whole)
        for k, (base, chunks) in enumerate(((_P_NORM, 8), (_P_NORM + 8, 8), (_P_NORM + 16, 8), (_P_NORM + 24, 8), (_P_BG, 4))):
            for q in range(chunks):
                def pick_q(ref, q=q):
                    return ref[:, LANES * q:LANES * (q + 1)]

                def put_q(ref, val, q=q):
                    ref[:, LANES * q:LANES * (q + 1)] = val

                step(2 + k, tot[base + q:base + q + 1], pick_q, put_q)
        step(7, tot[_P_BSP:_P_BSP + SGU_GROUPS], whole, put_whole)
        for k in range(3):
            mine = tot[pl.ds(pl.multiple_of(_P_HEAD + 32 * k + 8 * chip, 8), 8), :]
            step(8 + k, mine[0:4, 0:64], whole, put_whole)

    shapes = [_sds(w.shape, F32) for w in ws]
    res = pl.pallas_call(
        body, name="small_adamw", in_specs=[_whole()] * (1 + 3 * n), out_specs=[_whole()] * (1 + 4 * n),
        out_shape=[_sds((1, 1), F32)] + shapes * 4,
        compiler_params=pltpu.CompilerParams(vmem_limit_bytes=VMEM_LIMIT_BYTES),
    )(tot, *ws, *ms, *vs)
    return res[0].reshape(()), [list(res[1 + i * n:1 + (i + 1) * n]) for i in range(4)]


def _w_in_pieces():
    blk = D_IN // N_CHIPS
    pieces = []
    for s in range(len(_IN_SPLITS)):
        lo_s, hi_s = _IN_STARTS[s], _IN_STARTS[s + 1]
        for j in range(N_CHIPS):
            lo, hi = max(lo_s, j * blk), min(hi_s, (j + 1) * blk)
            if lo < hi:
                pieces.append((j, lo - j * blk, _IN_DST[s] + lo - lo_s, hi - lo))
    return pieces


def _relayout_w_in(gathered):
    _, rows, blk = gathered.shape
    tr = 256

    def body(g_ref, o_ref):
        o_ref[:, OFF_AL:N_ALL] = jnp.zeros((tr, LANES), BF16)
        for j, src, dst, w in _w_in_pieces():
            o_ref[:, dst:dst + w] = g_ref[j, :, src:src + w]

    res, _ = _call(body, name="relayout_w_in", grid=(rows // tr,), parallel=True,
                   in_specs=[pl.BlockSpec((N_CHIPS, tr, blk), lambda i: (0, i, 0))],
                   out_specs=[pl.BlockSpec((tr, N_ALL), lambda i: (i, 0))],
                   out_shape=[_sds((rows, N_ALL), BF16)], args=(gathered,))
    return res[0]


def _update_row_tile(rows):
    for t in range(min(rows, 256), 7, -8):
        if rows % t == 0:
            return t
    return rows


def _my_half(first_ref, second_ref):
    return jnp.where(lax.axis_index("c") == 0, first_ref[...], second_ref[...])


def _presum_w_in(dws, theirs, row0, rows, name, job=None):
    hr = theirs[0].shape[0]
    blk = D_IN // N_CHIPS
    tr = 64
    assert row0 % tr == 0 and rows % tr == 0
    nh, t0 = hr // tr, row0 // tr
    n = len(dws)

    def body(*refs):
        dw_refs, q_refs, (o_ref, s_scr) = refs[:2 * n], refs[2 * n:3 * n], refs[3 * n:]
        for p, (a, off) in enumerate(dws):
            w = a.shape[1]
            s_scr[:, off:off + w] = (_my_half(dw_refs[2 * p], dw_refs[2 * p + 1]) + q_refs[p][...]).astype(BF16)
        for j, src, dst, w in _w_in_pieces():
            o_ref[j, :, src:src + w] = s_scr[:, dst:dst + w]

    in_specs, args = [], []
    for a, _ in dws:
        w = a.shape[1]
        in_specs += [pl.BlockSpec((tr, w), lambda i: (i + t0, 0)), pl.BlockSpec((tr, w), lambda i: (i + t0 + nh, 0))]
        args += [a, a]
    in_specs += [pl.BlockSpec((tr, q.shape[1]), lambda i: (i + t0, 0)) for q in theirs]
    res, jres = _call(body, name=name, grid=(rows // tr,), parallel=True, in_specs=in_specs,
                      out_specs=[pl.BlockSpec((N_CHIPS, tr, blk), lambda i: (0, i, 0))],
                      out_shape=[_sds((N_CHIPS, rows, blk), BF16)], scratch_shapes=[pltpu.VMEM((tr, N_ALL), BF16)],
                      args=(*args, *theirs), job=job)
    return res[0], jres


def _presum(dw, theirs, name):
    if dw.ndim == 4:
        _, _, hr, c = dw.shape
        tr = _update_row_tile(hr)
        first = pl.BlockSpec((1, 1, tr, c), lambda j, i: (j, 0, i, 0))
        second = pl.BlockSpec((1, 1, tr, c), lambda j, i: (j, 1, i, 0))
        other = pl.BlockSpec((1, tr, c), lambda j, i: (j, i, 0))
    else:
        hr, c = dw.shape[0] // 2, dw.shape[1] // N_CHIPS
        tr = _update_row_tile(hr)
        nh = hr // tr
        first = pl.BlockSpec((tr, c), lambda j, i: (i, j))
        second = pl.BlockSpec((tr, c), lambda j, i: (i + nh, j))
        other = pl.BlockSpec((tr, c), lambda j, i: (i, j))

    def body(a_ref, b_ref, q_ref, o_ref):
        mine = _my_half(a_ref, b_ref).reshape(tr, c)
        o_ref[...] = (mine + q_ref[...].reshape(tr, c)).astype(BF16).reshape(o_ref.shape)

    res, _ = _call(body, name=name, grid=(N_CHIPS, hr // tr), parallel=True, in_specs=[first, second, other],
                   out_specs=[pl.BlockSpec((1, tr, c), lambda j, i: (j, i, 0))],
                   out_shape=[_sds((N_CHIPS, hr, c), BF16)], args=(dw, dw, theirs))
    return res[0]


def _sum_slots(own, slots, name):
    rows, cols = own.shape
    tr = _update_row_tile(rows)

    def body(own_ref, s_ref, o_ref):
        acc = own_ref[...].astype(F32)
        for j in range(N_PEER):
            acc = acc + s_ref[j].astype(F32)
        o_ref[...] = acc

    res, _ = _call(body, name=name, grid=(rows // tr,), parallel=True,
                   in_specs=[pl.BlockSpec((tr, cols), lambda i: (i, 0)), pl.BlockSpec((N_PEER, tr, cols), lambda i: (0, i, 0))],
                   out_specs=[pl.BlockSpec((tr, cols), lambda i: (i, 0))], out_shape=[_sds((rows, cols), F32)],
                   args=(own, slots))
    return res[0]


def _adamw(w, m, v, g_mine, g_theirs, name, job=None):
    rows, cols = w.shape
    part_rows = [p.shape[0] for p in g_mine]
    assert sum(part_rows) == rows // 2 and [p.shape[0] for p in g_theirs] == part_rows
    tr = _update_row_tile(min(part_rows))
    assert all(r % tr == 0 for r in part_rows)
    nh = (rows // 2) // tr
    starts = [sum(part_rows[:k]) // tr for k in range(len(part_rows))]
    n_parts = len(part_rows)

    def body(w_ref, m_ref, v_ref, *rest):
        g_refs, (g_out, d_out, m_out, v_out) = rest[:-4], rest[-4:]
        step = pl.program_id(0)
        mine_here = (step // nh) == lax.axis_index("c")
        q = step % nh
        g = None
        for k in reversed(range(n_parts)):
            val = jnp.where(mine_here, g_refs[k][...], g_refs[n_parts + k][...])
            g = val if g is None else jnp.where(q < starts[k + 1], val, g)
        d, m2, v2 = _adam_values(w_ref[...], m_ref[...], v_ref[...], g)
        g_out[...] = g
        m_out[...] = m2
        v_out[...] = v2
        d_out[...] = d

    def g_spec(k):
        last = part_rows[k] // tr - 1
        return pl.BlockSpec((tr, cols), lambda i: (jnp.clip(i % nh - starts[k], 0, last), 0))

    spec = pl.BlockSpec((tr, cols), lambda i: (i, 0))
    return _call(body, name=name, grid=(rows // tr,), parallel=True,
                 in_specs=[spec] * 3 + [g_spec(k) for k in range(n_parts)] * 2, out_specs=[spec] * 4,
                 out_shape=[_sds((rows, cols), F32)] * 4, args=(w, m, v, *g_mine, *g_theirs), job=job)


def _transposed_cast(wt):
    cols, rows = wt.shape

    def body(x_ref, o_ref):
        o_ref[...] = jnp.transpose(x_ref[...]).astype(BF16)

    res, _ = _call(body, name="transpose_w_in", grid=(pl.cdiv(cols, LANES),), parallel=True,
                   in_specs=[pl.BlockSpec((LANES, rows), lambda j: (j, 0))],
                   out_specs=[pl.BlockSpec((rows, LANES), lambda j: (0, j))], out_shape=[_sds((rows, cols), BF16)],
                   args=(wt,))
    return res[0]


def _adamw_transposed(wt, mt, vt, g_mine, g_theirs, name):
    cols, rows = wt.shape
    n_parts = len(g_mine)

    def body(w_ref, m_ref, v_ref, *rest):
        g_refs, (g_out, d_out, m_out, v_out) = rest[:-4], rest[-4:]
        mine = jnp.concatenate([r[...] for r in g_refs[:n_parts]], axis=0)
        theirs = jnp.concatenate([r[...] for r in g_refs[n_parts:]], axis=0)
        first = lax.axis_index("c") == 0
        g = jnp.transpose(jnp.concatenate([jnp.where(first, mine, theirs), jnp.where(first, theirs, mine)], axis=0))
        d, m2, v2 = _adam_values(w_ref[...], m_ref[...], v_ref[...], g)
        g_out[...] = g
        m_out[...] = m2
        v_out[...] = v2
        d_out[...] = d

    spec = pl.BlockSpec((LANES, rows), lambda j: (j, 0))
    g_specs = [pl.BlockSpec((p.shape[0], LANES), lambda j: (0, j)) for p in g_mine] * 2
    res, _ = _call(body, name=name, grid=(pl.cdiv(cols, LANES),), parallel=True, in_specs=[spec] * 3 + g_specs,
                   out_specs=[spec] * 4, out_shape=[_sds((cols, rows), F32)] * 4, args=(wt, mt, vt, *g_mine, *g_theirs))
    return res


def _inproj_fwd(x, g1, w_all, job=None):
    T = x.shape[0]
    tT = _row_tile(T, 512)

    def body(x_ref, g_ref, w_ref, a_ref, proj_ref, alow_ref):
        xv = x_ref[...]
        a = (xv * _rms_stats(xv) * g_ref[...]).astype(BF16)
        a_ref[...] = a
        for j in range(N_MAIN // 1024):
            cols = slice(j * 1024, (j + 1) * 1024)
            proj_ref[:, cols] = _dot(a, w_ref[:, cols]).astype(BF16)
        alow_ref[...] = _dot(a, w_ref[:, N_MAIN:N_ALL])

    row = lambda w: pl.BlockSpec((tT, w), lambda i: (i, 0))
    return _call(
        body, name="inproj_fwd", grid=(T // tT,), parallel=True,
        in_specs=[row(D_MODEL), pl.BlockSpec((1, D_MODEL), lambda i: (0, 0)), _whole()],
        out_specs=[row(D_MODEL), row(N_MAIN), row(LANES)],
        out_shape=[_sds((T, D_MODEL), BF16), _sds((T, N_MAIN), BF16), _sds((T, LANES), F32)],
        args=(x, g1, w_all), job=job)


def _gla_decay_terms(al_ref, wgu_ref, bg_ref, later_ref):
    logit = _dot_f32(al_ref[...], wgu_ref[...]) + bg_ref[...]
    la = _log_sigmoid(logit) * (1.0 / GLA_TAU)
    delta = _dot_exact_lhs(later_ref[...], la)
    return logit, la, delta


def _gla_fwd(proj, alow, wgu, b_gate, gn, job=None):
    T = proj.shape[0]
    tT = _row_tile(T, 512)
    nc = tT // CHUNK

    def body(q_ref, k_ref, v_ref, r_ref, al_ref, wgu_ref, bg_ref, gn_ref, later_ref, y_ref, st_ref, s_scr):
        @pl.when(pl.program_id(0) == 0)
        def _():
            s_scr[...] = jnp.zeros_like(s_scr)

        _, la, delta = _gla_decay_terms(al_ref, wgu_ref, bg_ref, later_ref)
        kdec = (k_ref[...].astype(F32) * jnp.exp(delta)).astype(BF16)
        heads = range(GLA_HEADS)
        kcs = [slice(h * GLA_DK, (h + 1) * GLA_DK) for h in heads]
        vcs = [slice(h * GLA_DV, (h + 1) * GLA_DV) for h in heads]
        state = [s_scr[h] for h in heads]
        for c in range(nc):
            rows = slice(c * CHUNK, (c + 1) * CHUNK)
            first = slice(c * CHUNK, c * CHUNK + 1)
            dec = jnp.exp(la[first, :] + delta[first, :])
            upd_t = [_dot(v_ref[rows, vcs[h]], kdec[rows, kcs[h]], _TN) for h in heads]
            qs = [(q_ref[rows, kcs[h]].astype(F32) * (GLA_DK ** -0.5)).astype(BF16) for h in heads]
            for h in heads:
                state[h] = state[h] * dec[:, kcs[h]] + upd_t[h]
                st_ref[c, h] = state[h]
            o = [_dot(qs[h], state[h].astype(BF16), _NT) for h in heads]
            for h in heads:
                on = o[h] * _rms_stats(o[h]) * gn_ref[:, vcs[h]]
                rr = r_ref[rows, vcs[h]].astype(F32)
                y_ref[rows, vcs[h]] = (on * (rr * _sigmoid(rr))).astype(BF16)
        for h in heads:
            s_scr[h] = state[h]

    blk = lambda w, j: pl.BlockSpec((tT, w), lambda i: (i, j))
    return _call(
        body, name="gla_fwd", grid=(T // tT,),
        in_specs=[blk(512, 0), blk(512, 1), blk(1024, 1), blk(1024, 2), blk(LANES, 0)] + [_whole()] * 4,
        out_specs=[pl.BlockSpec((tT, GLA_V), lambda i: (i, 0)),
                   pl.BlockSpec((nc, GLA_HEADS, GLA_DV, GLA_DK), lambda i: (i, 0, 0, 0))],
        out_shape=[_sds((T, GLA_V), BF16), _sds((T // CHUNK, GLA_HEADS, GLA_DV, GLA_DK), F32)],
        scratch_shapes=[pltpu.VMEM((GLA_HEADS, GLA_DV, GLA_DK), F32)],
        args=(proj, proj, proj, proj, alow, wgu, b_gate, gn, _chunk_masks(tT, upper=True)), job=job)


def _sgu_mask():
    i = lax.broadcasted_iota(jnp.int32, (SGU_BLOCK, SGU_BLOCK), 0)
    j = lax.broadcasted_iota(jnp.int32, (SGU_BLOCK, SGU_BLOCK), 1)
    return lax.shift_right_logical(j, 6) <= lax.shift_right_logical(i, 6)


def _sgu_fwd(proj, ln_g, ln_b, w_sp, b_sp_t):
    T = proj.shape[0]
    tT = _row_tile(T, 512)
    nb = tT // SGU_BLOCK

    def body(su_ref, sv_ref, lg_ref, lb_ref, w_ref, b_ref, y_ref):
        mask = _sgu_mask()
        for g in range(SGU_GROUPS):
            gc = slice(g * SGU_DG, (g + 1) * SGU_DG)
            wm = jnp.where(mask, w_ref[g], 0.0).astype(BF16)
            vf = _gelu(sv_ref[:, gc].astype(F32))
            mu = jnp.mean(vf, axis=-1, keepdims=True)
            vc = vf - mu
            rstd = lax.rsqrt(jnp.mean(vc * vc, axis=-1, keepdims=True) + EPS)
            vn = (vc * rstd * lg_ref[:, gc] + lb_ref[:, gc]).astype(BF16)
            u = _gelu(su_ref[:, gc].astype(F32))
            for b in range(nb):
                rows = slice(b * SGU_BLOCK, (b + 1) * SGU_BLOCK)
                mixed = _dot(wm, vn[rows, :]) + b_ref[:, g:g + 1]
                y_ref[rows, gc] = (u[rows, :] * mixed).astype(BF16)

    blk = lambda j: pl.BlockSpec((tT, 1024), lambda i: (i, j))
    res, _ = _call(body, name="sgu_fwd", grid=(T // tT,), parallel=True,
                   in_specs=[blk(3), blk(4), _whole(), _whole(), _whole(), _whole()],
                   out_specs=[pl.BlockSpec((tT, 1024), lambda i: (i, 0))], out_shape=[_sds((T, 1024), BF16)],
                   args=(proj, proj, ln_g, ln_b, w_sp, b_sp_t))
    return res[0]


def _merge_fwd(x, proj, y_gla, y_sgu, w_bg, w_bs, w_o, g_pm, job=None):
    T = x.shape[0]
    tT = _row_tile(T, 512)

    def body(x_ref, gg_ref, gs_ref, yg_ref, ys_ref, wbg_ref, wbs_ref, wo_ref, g_ref,
             zg_ref, zs_ref, mg_ref, mix_ref, x1_ref):
        zg = _dot(yg_ref[...], wbg_ref[...])
        zs = _dot(ys_ref[...], wbs_ref[...])
        zg_ref[...] = zg.astype(BF16)
        zs_ref[...] = zs.astype(BF16)
        merged = (_sigmoid(gg_ref[...].astype(F32)) * zg + _sigmoid(gs_ref[...].astype(F32)) * zs).astype(BF16)
        mg_ref[...] = merged
        mix = _dot(merged, wo_ref[...])
        mix_ref[...] = mix
        x1_ref[...] = x_ref[...] + mix * _rms_stats(mix) * g_ref[...]

    row = pl.BlockSpec((tT, D_MODEL), lambda i: (i, 0))
    blk = lambda j: pl.BlockSpec((tT, 1024), lambda i: (i, j))
    sds = lambda dt: _sds((T, D_MODEL), dt)
    return _call(body, name="merge_fwd", grid=(T // tT,), parallel=True,
                 in_specs=[row, blk(5), blk(6), row, row, _whole(), _whole(), _whole(),
                           pl.BlockSpec((1, D_MODEL), lambda i: (0, 0))],
                 out_specs=[row] * 5, out_shape=[sds(BF16), sds(BF16), sds(BF16), sds(F32), sds(F32)],
                 args=(x, proj, proj, y_gla, y_sgu, w_bg, w_bs, w_o, g_pm), job=job)


def _ffn_fwd_bwd(x1, tgt, w_fi_top, w_fi_bot, w_fo, g_pf, g_po):
    T = x1.shape[0]
    tT = _row_tile(T, 256)
    half = D_FF // 2
    kh = D_MODEL // 2

    def body(x1_ref, t_ref, top_ref, bot_ref, wfo_ref, gpf_ref, gpo_ref,
             h_ref, f_ref, dgu_ref, dy_ref, dx1_ref, loss_ref, dgpf_ref, dgpo_ref, gu_scr):
        @pl.when(pl.program_id(0) == 0)
        def _():
            loss_ref[...] = jnp.zeros_like(loss_ref)
            dgpf_ref[...] = jnp.zeros_like(dgpf_ref)
            dgpo_ref[...] = jnp.zeros_like(dgpo_ref)

        x1v = x1_ref[...]
        r2 = _rms_stats(x1v)
        h = (x1v * r2 * gpf_ref[...]).astype(BF16)
        h_ref[...] = h
        y = jnp.zeros((tT, D_MODEL), F32)
        for j in range(2):
            gc = slice(j * half, (j + 1) * half)
            uc = slice(D_FF + j * half, D_FF + (j + 1) * half)
            gate = _dot(h[:, :kh], top_ref[j]) + _dot(h[:, kh:], bot_ref[j])
            up = _dot(h[:, :kh], top_ref[2 + j]) + _dot(h[:, kh:], bot_ref[2 + j])
            gu_scr[:, gc] = gate
            gu_scr[:, uc] = up
            f = (gate * _sigmoid(gate) * up).astype(BF16)
            f_ref[:, gc] = f
            y = y + _dot(f, wfo_ref[gc, :])
        r3 = _rms_stats(y)
        x2 = x1v + y * r3 * gpo_ref[...]
        err = x2 - t_ref[...]
        loss_ref[...] += jnp.sum(err * err) * (0.5 / D_MODEL)
        dx2 = err * (1.0 / D_MODEL)
        dy, dg = _rms_bwd(dx2, y, r3, gpo_ref[...])
        dgpo_ref[...] += jnp.sum(dg, axis=0, keepdims=True)
        dyb = dy.astype(BF16)
        dy_ref[...] = dyb
        dh_top = jnp.zeros((tT, kh), F32)
        dh_bot = jnp.zeros((tT, kh), F32)
        for j in range(2):
            gc = slice(j * half, (j + 1) * half)
            uc = slice(D_FF + j * half, D_FF + (j + 1) * half)
            df = _dot(dyb, wfo_ref[gc, :], _NT)
            gate = gu_scr[:, gc]
            up = gu_scr[:, uc]
            sg = _sigmoid(gate)
            dgate = (df * up * (sg * (1.0 + gate * (1.0 - sg)))).astype(BF16)
            dup = (df * (gate * sg)).astype(BF16)
            dgu_ref[:, gc] = dgate
            dgu_ref[:, uc] = dup
            dh_top = dh_top + _dot(dgate, top_ref[j], _NT) + _dot(dup, top_ref[2 + j], _NT)
            dh_bot = dh_bot + _dot(dgate, bot_ref[j], _NT) + _dot(dup, bot_ref[2 + j], _NT)
        dh = jnp.concatenate([dh_top, dh_bot], axis=1)
        dx1n, dg2 = _rms_bwd(dh, x1v, r2, gpf_ref[...])
        dgpf_ref[...] += jnp.sum(dg2, axis=0, keepdims=True)
        dx1_ref[...] = dx2 + dx1n

    row = lambda w: pl.BlockSpec((tT, w), lambda i: (i, 0))
    vec = pl.BlockSpec((1, D_MODEL), lambda i: (0, 0))
    res, _ = _call(
        body, name="ffn_fwd_bwd", grid=(T // tT,),
        in_specs=[row(D_MODEL), row(D_MODEL), _whole(), _whole(), _whole(), vec, vec],
        out_specs=[row(D_MODEL), row(D_FF), row(2 * D_FF), row(D_MODEL), row(D_MODEL),
                   pl.BlockSpec((1, LANES), lambda i: (0, 0)), vec, vec],
        out_shape=[_sds((T, D_MODEL), BF16), _sds((T, D_FF), BF16), _sds((T, 2 * D_FF), BF16), _sds((T, D_MODEL), BF16),
                   _sds((T, D_MODEL), F32), _sds((1, LANES), F32), _sds((1, D_MODEL), F32), _sds((1, D_MODEL), F32)],
        scratch_shapes=[pltpu.VMEM((tT, 2 * D_FF), F32)], args=(x1, tgt, w_fi_top, w_fi_bot, w_fo, g_pf, g_po))
    return res


def _merge_bwd(dx1, mix, proj, zg, zs, w_bg, w_bs, w_o, g_pm, job=None):
    T = dx1.shape[0]
    tT = _row_tile(T, 512)

    def body(dx1_ref, mix_ref, gg_ref, gs_ref, zg_ref, zs_ref, wbg_ref, wbs_ref, wo_ref, g_ref,
             dmix_ref, dzg_ref, dzs_ref, dgate_ref, dyg_ref, dys_ref, dgpm_ref):
        @pl.when(pl.program_id(0) == 0)
        def _():
            dgpm_ref[...] = jnp.zeros_like(dgpm_ref)

        mix = mix_ref[...]
        dmix, dg = _rms_bwd(dx1_ref[...], mix, _rms_stats(mix), g_ref[...])
        dgpm_ref[...] += jnp.sum(dg, axis=0, keepdims=True)
        dmb = dmix.astype(BF16)
        dmix_ref[...] = dmb
        dmerged = _dot(dmb, wo_ref[...], _NT)
        for k, (gate_ref, z_ref, w_ref, dz_ref, dy_ref) in enumerate((
                (gg_ref, zg_ref, wbg_ref, dzg_ref, dyg_ref), (gs_ref, zs_ref, wbs_ref, dzs_ref, dys_ref))):
            sg = _sigmoid(gate_ref[...].astype(F32))
            dz = (dmerged * sg).astype(BF16)
            dz_ref[...] = dz
            dgate_ref[:, k * 1024:(k + 1) * 1024] = (dmerged * z_ref[...].astype(F32) * (sg * (1.0 - sg))).astype(BF16)
            dy_ref[...] = _dot(dz, w_ref[...], _NT).astype(BF16)

    row = pl.BlockSpec((tT, D_MODEL), lambda i: (i, 0))
    blk = lambda j: pl.BlockSpec((tT, 1024), lambda i: (i, j))
    vec = pl.BlockSpec((1, D_MODEL), lambda i: (0, 0))
    sds = _sds((T, D_MODEL), BF16)
    return _call(
        body, name="merge_bwd", grid=(T // tT,),
        in_specs=[row, row, blk(5), blk(6), row, row, _whole(), _whole(), _whole(), vec],
        out_specs=[row, row, row, pl.BlockSpec((tT, W_MRG), lambda i: (i, 0)), row, row, vec],
        out_shape=[sds, sds, sds, _sds((T, W_MRG), BF16), sds, sds, _sds((1, D_MODEL), F32)],
        args=(dx1, mix, proj, proj, zg, zs, w_bg, w_bs, w_o, g_pm), job=job)


def _sgu_bwd(proj, dy_sgu, ln_g, ln_b, w_sp, b_sp_t, job=None):
    T = proj.shape[0]
    tT = _row_tile(T, 512)
    nb = tT // SGU_BLOCK

    def body(su_ref, sv_ref, dy_ref, lg_ref, lb_ref, w_ref, b_ref, dp_ref, dw_ref, dbt_ref, dlg_ref, dlb_ref):
        @pl.when(pl.program_id(0) == 0)
        def _():
            dw_ref[...] = jnp.zeros_like(dw_ref)
            dbt_ref[...] = jnp.zeros_like(dbt_ref)
            dlg_ref[...] = jnp.zeros_like(dlg_ref)
            dlb_ref[...] = jnp.zeros_like(dlb_ref)

        mask = _sgu_mask()
        lane = lax.broadcasted_iota(jnp.int32, (SGU_BLOCK, LANES), 1)
        for g in range(SGU_GROUPS):
            gc = slice(g * SGU_DG, (g + 1) * SGU_DG)
            gc_v = slice(1024 + g * SGU_DG, 1024 + (g + 1) * SGU_DG)
            wm = jnp.where(mask, w_ref[g], 0.0).astype(BF16)
            vf, dvf_dsv = _gelu_and_grad(sv_ref[:, gc].astype(F32))
            mu = jnp.mean(vf, axis=-1, keepdims=True)
            vc = vf - mu
            rstd = lax.rsqrt(jnp.mean(vc * vc, axis=-1, keepdims=True) + EPS)
            vhat = vc * rstd
            vn = (vhat * lg_ref[:, gc] + lb_ref[:, gc]).astype(BF16)
            u, du_dsu = _gelu_and_grad(su_ref[:, gc].astype(F32))
            dy = dy_ref[:, gc].astype(F32)
            dmixed = (dy * u).astype(BF16)
            dvn_parts = []
            dw_acc = jnp.zeros((SGU_BLOCK, SGU_BLOCK), F32)
            db_acc = jnp.zeros((SGU_BLOCK, 1), F32)
            for b in range(nb):
                rows = slice(b * SGU_BLOCK, (b + 1) * SGU_BLOCK)
                mixed = _dot(wm, vn[rows, :]) + b_ref[:, g:g + 1]
                dp_ref[rows, gc] = (dy[rows, :] * mixed * du_dsu[rows, :]).astype(BF16)
                dvn_parts.append(_dot(wm, dmixed[rows, :], _TN))
                dw_acc = dw_acc + _dot(dmixed[rows, :], vn[rows, :], _NT)
                db_acc = db_acc + jnp.sum(dmixed[rows, :].astype(F32), axis=-1, keepdims=True)
            dw_ref[g] += jnp.where(mask, dw_acc, 0.0)
            dbt_ref[...] += jnp.where(lane == g, db_acc, 0.0)
            dvn = jnp.concatenate(dvn_parts, axis=0)
            dlg_ref[:, gc] += jnp.sum(dvn * vhat, axis=0, keepdims=True)
            dlb_ref[:, gc] += jnp.sum(dvn, axis=0, keepdims=True)
            dvh = dvn * lg_ref[:, gc]
            dvf = rstd * (dvh - jnp.mean(dvh, axis=-1, keepdims=True)
                          - vhat * jnp.mean(dvh * vhat, axis=-1, keepdims=True))
            dp_ref[:, gc_v] = (dvf * dvf_dsv).astype(BF16)

    blk = lambda j: pl.BlockSpec((tT, 1024), lambda i: (i, j))
    row = lambda w: pl.BlockSpec((tT, w), lambda i: (i, 0))
    vec = pl.BlockSpec((1, 1024), lambda i: (0, 0))
    return _call(
        body, name="sgu_bwd", grid=(T // tT,),
        in_specs=[blk(3), blk(4), row(1024), _whole(), _whole(), _whole(), _whole()],
        out_specs=[row(W_SGU), pl.BlockSpec((SGU_GROUPS, SGU_BLOCK, SGU_BLOCK), lambda i: (0, 0, 0)),
                   pl.BlockSpec((SGU_BLOCK, LANES), lambda i: (0, 0)), vec, vec],
        out_shape=[_sds((T, W_SGU), BF16), _sds((SGU_GROUPS, SGU_BLOCK, SGU_BLOCK), F32), _sds((SGU_BLOCK, LANES), F32),
                   _sds((1, 1024), F32), _sds((1, 1024), F32)],
        args=(proj, proj, dy_sgu, ln_g, ln_b, w_sp, b_sp_t), job=job)


def _gla_bwd(proj, alow, wgu, b_gate, gn, states, dy_gla, job=None):
    T = proj.shape[0]
    tT = _row_tile(T, 512)
    nc = tT // CHUNK
    nt = T // tT

    def body(q_ref, k_ref, v_ref, r_ref, al_ref, wgu_ref, bg_ref, gn_ref, later_ref, earlier_ref, st_ref, sp_ref, dy_ref,
             dp_ref, dal_ref, dgn_ref, dbg_ref, dwgu_ref, g_scr, dd_scr, dt_scr):
        step = pl.program_id(0)

        @pl.when(step == 0)
        def _():
            g_scr[...] = jnp.zeros_like(g_scr)
            dgn_ref[...] = jnp.zeros_like(dgn_ref)
            dbg_ref[...] = jnp.zeros_like(dbg_ref)
            dwgu_ref[...] = jnp.zeros_like(dwgu_ref)

        has_prev = jnp.where(step == nt - 1, 0.0, 1.0)
        logit, la, delta = _gla_decay_terms(al_ref, wgu_ref, bg_ref, later_ref)
        e = jnp.exp(delta)
        kdec_f = k_ref[...].astype(F32) * e
        kdec = kdec_f.astype(BF16)
        heads = range(GLA_HEADS)
        kcs = [slice(h * GLA_DK, (h + 1) * GLA_DK) for h in heads]
        vcs = [slice(h * GLA_DV, (h + 1) * GLA_DV) for h in heads]
        carry = [g_scr[h] for h in heads]
        dgn_acc = [jnp.zeros((1, GLA_DV), F32) for _ in heads]
        for c in reversed(range(nc)):
            rows = slice(c * CHUNK, (c + 1) * CHUNK)
            first = slice(c * CHUNK, c * CHUNK + 1)
            dec = jnp.exp(la[first, :] + delta[first, :])
            s_b = [st_ref[c, h].astype(BF16) for h in heads]
            qs = [(q_ref[rows, kcs[h]].astype(F32) * (GLA_DK ** -0.5)).astype(BF16) for h in heads]
            o = [_dot(qs[h], s_b[h], _NT) for h in heads]
            do = []
            for h in heads:
                rstd = _rms_stats(o[h])
                ohat = o[h] * rstd
                gnh = gn_ref[:, vcs[h]]
                dy = dy_ref[rows, vcs[h]].astype(F32)
                rr = r_ref[rows, vcs[h]].astype(F32)
                sg = _sigmoid(rr)
                don = dy * (rr * sg)
                dp_ref[rows, OFF_R + h * GLA_DV:OFF_R + (h + 1) * GLA_DV] = (
                    dy * (ohat * gnh) * (sg * (1.0 + rr * (1.0 - sg)))).astype(BF16)
                dgn_acc[h] = dgn_acc[h] + jnp.sum(don * ohat, axis=0, keepdims=True)
                dn = don * gnh
                do.append((rstd * (dn - ohat * jnp.mean(dn * ohat, axis=-1, keepdims=True))).astype(BF16))
            dq = [_dot(do[h], s_b[h]) for h in heads]
            g_t = [_dot(do[h], qs[h], _TN) + carry[h] for h in heads]
            g_b = [g_t[h].astype(BF16) for h in heads]
            dv = [_dot(kdec[rows, kcs[h]], g_b[h], _NT) for h in heads]
            dkdec = [_dot(v_ref[rows, vcs[h]], g_b[h]) for h in heads]
            for h in heads:
                s_prev = st_ref[c - 1, h] if c > 0 else sp_ref[0, h] * has_prev
                ddec = jnp.sum(g_t[h] * s_prev, axis=0, keepdims=True)
                carry[h] = g_t[h] * dec[:, kcs[h]]
                dp_ref[rows, OFF_Q + h * GLA_DK:OFF_Q + (h + 1) * GLA_DK] = (dq[h] * (GLA_DK ** -0.5)).astype(BF16)
                dp_ref[rows, OFF_V + h * GLA_DV:OFF_V + (h + 1) * GLA_DV] = dv[h].astype(BF16)
                dp_ref[rows, OFF_K + h * GLA_DK:OFF_K + (h + 1) * GLA_DK] = (dkdec[h] * e[rows, kcs[h]]).astype(BF16)
                dd_scr[rows, kcs[h]] = dkdec[h] * kdec_f[rows, kcs[h]]
                dt_scr[rows, kcs[h]] = jnp.broadcast_to(ddec * dec[:, kcs[h]], (CHUNK, GLA_DK))
        for h in heads:
            g_scr[h] = carry[h]
            dgn_ref[:, vcs[h]] += dgn_acc[h]
        dla = _dot_exact_lhs(earlier_ref[...], dd_scr[...]) + dt_scr[...]
        dlogit = dla * (1.0 / GLA_TAU) * _sigmoid(-logit)
        dbg_ref[...] += jnp.sum(dlogit, axis=0, keepdims=True)
        dwgu_ref[...] += _dot_f32(al_ref[...], dlogit, _TN)
        dal_ref[...] = _dot_f32(dlogit, wgu_ref[...], _NT).astype(BF16)

    rev = lambda i: nt - 1 - i
    blk = lambda w, j: pl.BlockSpec((tT, w), lambda i: (rev(i), j))
    st_blk = pl.BlockSpec((nc, GLA_HEADS, GLA_DV, GLA_DK), lambda i: (rev(i), 0, 0, 0))
    sp_blk = pl.BlockSpec((1, GLA_HEADS, GLA_DV, GLA_DK), lambda i: (jnp.maximum(rev(i) * nc - 1, 0), 0, 0, 0))
    return _call(
        body, name="gla_bwd", grid=(nt,),
        in_specs=[blk(512, 0), blk(512, 1), blk(1024, 1), blk(1024, 2), blk(LANES, 0)] + [_whole()] * 5
        + [st_blk, sp_blk, blk(GLA_V, 0)],
        out_specs=[blk(W_GLA, 0), blk(LANES, 0), pl.BlockSpec((1, GLA_V), lambda i: (0, 0)),
                   pl.BlockSpec((1, GLA_QK), lambda i: (0, 0)), pl.BlockSpec((LANES, GLA_QK), lambda i: (0, 0))],
        out_shape=[_sds((T, W_GLA), BF16), _sds((T, LANES), BF16), _sds((1, GLA_V), F32), _sds((1, GLA_QK), F32),
                   _sds((LANES, GLA_QK), F32)],
        scratch_shapes=[pltpu.VMEM((GLA_HEADS, GLA_DV, GLA_DK), F32), pltpu.VMEM((tT, GLA_QK), F32),
                        pltpu.VMEM((tT, GLA_QK), F32)],
        args=(proj, proj, proj, proj, alow, wgu, b_gate, gn, _chunk_masks(tT, upper=True), _chunk_masks(tT, upper=False),
              states, states, dy_gla), job=job)


def _inproj_bwd(x, dx1, g1, w_all, dparts, job=None):
    T = x.shape[0]
    tT = _row_tile(T, 512)
    offs = (0, W_GLA, W_GLA + W_SGU, N_MAIN)

    def body(x_ref, dx1_ref, g_ref, w_ref, *rest):
        part_refs, (dx_ref, dg_ref) = rest[:len(offs)], rest[len(offs):]

        @pl.when(pl.program_id(0) == 0)
        def _():
            dg_ref[...] = jnp.zeros_like(dg_ref)

        da = jnp.zeros((tT, D_MODEL), F32)
        for off, p_ref in zip(offs, part_refs):
            da = da + _dot(p_ref[...], w_ref[:, off:off + p_ref.shape[1]], _NT)
        xv = x_ref[...]
        dx, dg = _rms_bwd(da, xv, _rms_stats(xv), g_ref[...])
        dg_ref[...] += jnp.sum(dg, axis=0, keepdims=True)
        dx_ref[...] = dx1_ref[...] + dx

    row = lambda w: pl.BlockSpec((tT, w), lambda i: (i, 0))
    vec = pl.BlockSpec((1, D_MODEL), lambda i: (0, 0))
    return _call(
        body, name="inproj_bwd", grid=(T // tT,),
        in_specs=[row(D_MODEL), row(D_MODEL), vec, _whole()] + [row(p.shape[1]) for p in dparts],
        out_specs=[row(D_MODEL), vec], out_shape=[_sds((T, D_MODEL), F32), _sds((1, D_MODEL), F32)],
        args=(x, dx1, g1, w_all, *dparts), job=job)


def _tn_matmul(a, b, name, job=None):
    T, M = a.shape
    N = b.shape[1]
    tk = _row_tile(T, 1024)
    tm = M if M <= 1024 else 1408
    tn = N if N <= 3072 else N // 2
    assert M % tm == 0 and N % tn == 0

    def body(a_ref, b_ref, o_ref):
        @pl.when(pl.program_id(2) == 0)
        def _():
            o_ref[...] = _dot(a_ref[...], b_ref[...], _TN)

        @pl.when(pl.program_id(2) > 0)
        def _():
            o_ref[...] += _dot(a_ref[...], b_ref[...], _TN)

    res, jres = _call(
        body, name=name, grid=(M // tm, N // tn, T // tk),
        in_specs=[pl.BlockSpec((tk, tm), lambda i, j, k: (k, i)), pl.BlockSpec((tk, tn), lambda i, j, k: (k, j))],
        out_specs=[pl.BlockSpec((tm, tn), lambda i, j, k: (i, j))], out_shape=[_sds((M, N), F32)], args=(a, b), job=job)
    return res[0], jres


def _pad_rows(a, rows=8):
    return jnp.pad(a, ((0, rows - a.shape[0]), (0, LANES - a.shape[1])))


def _halves_view(dw):
    r = dw.shape[0] // N_CHIPS
    return dw.reshape(N_CHIPS, 2, r // 2, dw.shape[1])


def kernel(x, norm_pre_mix, w_in, w_gate_up, b_gate, gla_norm, sgu_ln_g, sgu_ln_b, w_spatial, b_spatial, w_branch_gla, w_branch_sgu, w_out, norm_post_mix, norm_pre_ffn, w_ffn_in, w_ffn_out, norm_post_ffn, loss_target, m_norm_pre_mix, m_w_in, m_w_gate_up, m_b_gate, m_gla_norm, m_sgu_ln_g, m_sgu_ln_b, m_w_spatial, m_b_spatial, m_w_branch_gla, m_w_branch_sgu, m_w_out, m_norm_post_mix, m_norm_pre_ffn, m_w_ffn_in, m_w_ffn_out, m_norm_post_ffn, v_norm_pre_mix, v_w_in, v_w_gate_up, v_b_gate, v_gla_norm, v_sgu_ln_g, v_sgu_ln_b, v_w_spatial, v_b_spatial, v_w_branch_gla, v_w_branch_sgu, v_w_out, v_norm_post_mix, v_norm_pre_ffn, v_w_ffn_in, v_w_ffn_out, v_norm_post_ffn):
    chip = 2 * lax.axis_index("x") + lax.axis_index("y")
    xt, tgt = x[0], loss_target[0]

    tiny = jnp.concatenate([w_gate_up[0], _pad_rows(gla_norm[0]), _pad_rows(sgu_ln_g[0]), _pad_rows(sgu_ln_b[0]),
                            jnp.zeros((24, LANES), F32)], axis=0)

    def with_own(gathered, own):
        return lax.dynamic_update_slice(gathered, own[None], (chip, 0, 0))

    w_in_t, m_in_t, v_in_t = w_in[0].T, m_w_in[0].T, v_w_in[0].T
    w_in_b = _transposed_cast(w_in_t)
    g_in, g_tiny = _run_job(_job_gather([w_in_b, tiny]), "gather_w_in")
    g_tiny = with_own(g_tiny, tiny)
    w_all = _relayout_w_in(with_own(g_in, w_in_b))
    cols = lambda a: a.transpose(1, 0, 2).reshape(a.shape[1], N_CHIPS * a.shape[2])
    wgu = jnp.pad(cols(g_tiny[:, 0:16]), ((0, LANES - GLA_RANK), (0, 0)))
    gn = cols(g_tiny[:, 16:20, :64]).reshape(1, GLA_V)
    ln_g = cols(g_tiny[:, 24:28, :64]).reshape(1, 1024)
    ln_b = cols(g_tiny[:, 32:36, :64]).reshape(1, 1024)
    b_sp_t = jnp.pad(b_spatial[0].T, ((0, 0), (0, LANES - SGU_GROUPS)))
    w_sp = w_spatial[0]

    own_rows = [w_branch_gla[0].astype(BF16), w_branch_sgu[0].astype(BF16), w_out[0].astype(BF16), w_ffn_out[0].astype(BF16)]
    (a, proj, alow), g_rows = _inproj_fwd(xt, norm_pre_mix, w_all, job=_job_gather(own_rows))
    rows = lambda g: g.reshape(N_CHIPS * g.shape[1], g.shape[2])
    w_bg, w_bs, w_o, w_fo = [rows(with_own(g, own)) for g, own in zip(g_rows, own_rows)]
    w_fi_b = w_ffn_in[0].astype(BF16)
    fi_top, fi_bot = w_fi_b[:D_MODEL // 2], w_fi_b[D_MODEL // 2:]
    (y_gla, states), (g_top,) = _gla_fwd(proj, alow, wgu, b_gate, gn, job=_job_gather([fi_top]))
    y_sgu = _sgu_fwd(proj, ln_g, ln_b, w_sp, b_sp_t)
    (zg, zs, merged, mix, x1), (g_bot,) = _merge_fwd(xt, proj, y_gla, y_sgu, w_bg, w_bs, w_o, norm_post_mix,
                                                     job=_job_gather([fi_bot]))
    h, f, dgu, dy, dx1, loss, d_gpf, d_gpo = _ffn_fwd_bwd(x1, tgt, with_own(g_top, fi_top), with_own(g_bot, fi_bot),
                                                          w_fo, norm_pre_ffn, norm_post_ffn)

    own_part = lambda c: lax.dynamic_index_in_dim(c, chip, 0, keepdims=False)
    whole = lambda hs: [[(h_, None)] for h_ in hs]
    dw_fo, _ = _tn_matmul(f, dy, "dw_ffn_out")
    dw_fo4 = _halves_view(dw_fo)
    dw_fi, (q_fo,) = _tn_matmul(h, dgu, "dw_ffn_in", job=_job_to_other_core([[(dw_fo4, 0)]]))
    c_fo = _presum(dw_fo4, q_fo, "presum_ffn_out")
    (dmix, dzg, dzs, dp_mrg, dyg, dys, d_gpm), (s_fo, q_fi) = _merge_bwd(
        dx1, mix, proj, zg, zs, w_bg, w_bs, w_o, norm_post_mix,
        job=_join(_job_scatter([c_fo]), _job_to_other_core([[(dw_fi, 0)]])))
    c_fi = _presum(dw_fi, q_fi, "presum_ffn_in")
    dw_c, _ = _tn_matmul(a, dp_mrg, "dw_in_merge")
    dw_o4 = _halves_view(_tn_matmul(merged, dmix, "dw_out")[0])
    dw_bg4 = _halves_view(_tn_matmul(y_gla, dzg, "dw_branch_gla")[0])
    dw_bs4 = _halves_view(_tn_matmul(y_sgu, dzs, "dw_branch_sgu")[0])
    (dp_sgu, d_wsp, d_bsp_t, d_lng, d_lnb), (q_o, q_bg, q_bs, q_c) = _sgu_bwd(
        proj, dys, ln_g, ln_b, w_sp, b_sp_t,
        job=_job_to_other_core([[(dw_o4, 0)], [(dw_bg4, 0)], [(dw_bs4, 0)], [(dw_c, 0)]]))
    c_o, c_bg, c_bs = (_presum(dw_o4, q_o, "presum_out"), _presum(dw_bg4, q_bg, "presum_branch_gla"),
                       _presum(dw_bs4, q_bs, "presum_branch_sgu"))
    h_fo = _sum_slots(own_part(c_fo), s_fo, "sum_ffn_out")
    dw_b, _ = _tn_matmul(a, dp_sgu, "dw_in_sgu")
    (dp_gla, dal, d_gn, d_bg, d_wgu), (s_fi, t_fo, q_b) = _gla_bwd(
        proj, alow, wgu, b_gate, gn, states, dyg,
        job=_join(_job_scatter([c_fi]), _job_to_other_core(whole([h_fo]) + [[(dw_b, 0)]])))
    h_fi = _sum_slots(own_part(c_fi), s_fi, "sum_ffn_in")
    dw_d, _ = _tn_matmul(a, dal, "dw_in_gate")
    dw_a, (s_o, s_bg, s_bs, t_fi, q_d) = _tn_matmul(
        a, dp_gla, "dw_in_gla",
        job=_join(_job_scatter([c_o, c_bg, c_bs]), _job_to_other_core(whole([h_fi]) + [[(dw_d, 0)]])))
    h_o, h_bg, h_bs = (_sum_slots(own_part(c_o), s_o, "sum_out"), _sum_slots(own_part(c_bg), s_bg, "sum_branch_gla"),
                       _sum_slots(own_part(c_bs), s_bs, "sum_branch_sgu"))

    grads, deltas, new_m, new_v = {}, {}, {}, {}

    def update(name, w, m, v, g_mine, g_theirs, job=None):
        (g, d, m2, v2), jres = _adamw(w[0], m[0], v[0], g_mine, g_theirs, "adamw_" + name, job=job)
        grads[name], deltas[name], new_m[name], new_v[name] = g[None], d[None], m2[None], v2[None]
        return jres

    dw_in = [(dw_a, 0), (dw_b, W_GLA), (dw_c, W_GLA + W_SGU), (dw_d, N_MAIN)]
    q_a, t_o, t_bg, t_bs = update("w_ffn_out", w_ffn_out, m_w_ffn_out, v_w_ffn_out, [h_fo], [t_fo],
                                  job=_job_to_other_core([[(dw_a, 0)]] + whole([h_o, h_bg, h_bs])))
    q_in = [q_a, q_b, q_c, q_d]
    hr_in = D_MODEL // 2
    c_in_a, _ = _presum_w_in(dw_in, q_in, 0, hr_in // 8, "presum_w_in_a")
    c_in_b, (s_in_a,) = _presum_w_in(dw_in, q_in, hr_in // 8, 7 * hr_in // 8, "presum_w_in_b",
                                     job=_job_scatter([c_in_a]))
    update("w_ffn_in", w_ffn_in, m_w_ffn_in, v_w_ffn_in, [h_fi], [t_fi])
    update("w_out", w_out, m_w_out, v_w_out, [h_o], [t_o])
    update("w_branch_gla", w_branch_gla, m_w_branch_gla, v_w_branch_gla, [h_bg], [t_bg])
    update("w_branch_sgu", w_branch_sgu, m_w_branch_sgu, v_w_branch_sgu, [h_bs], [t_bs])
    (grad_x, d_g1), (s_in_b,) = _inproj_bwd(xt, dx1, norm_pre_mix, w_all, (dp_gla, dp_sgu, dp_mrg, dal),
                                            job=_job_scatter([c_in_b]))
    h_in = [_sum_slots(own_part(c_in_a), s_in_a, "sum_w_in_a"), _sum_slots(own_part(c_in_b), s_in_b, "sum_w_in_b")]
    t_in = _run_job(_job_to_other_core(whole(h_in)), "swap_w_in")
    for store, val in zip((grads, deltas, new_m, new_v),
                          _adamw_transposed(w_in_t, m_in_t, v_in_t, h_in, t_in, "adamw_w_in")):
        store["w_in"] = val.T[None]

    small_names = ["w_spatial", "w_gate_up", "norm_pre_mix", "norm_post_mix", "norm_pre_ffn", "norm_post_ffn", "b_gate",
                   "b_spatial", "gla_norm", "sgu_ln_g", "sgu_ln_b"]
    loss_out, small = _small_adamw(
        _small_sum([d_wsp, d_wgu, d_g1, d_gpm, d_gpf, d_gpo, d_bg, d_bsp_t, d_gn, d_lng, d_lnb, loss]),
        [w_spatial, w_gate_up, norm_pre_mix, norm_post_mix, norm_pre_ffn, norm_post_ffn, b_gate, b_spatial, gla_norm,
         sgu_ln_g, sgu_ln_b],
        [m_w_spatial, m_w_gate_up, m_norm_pre_mix, m_norm_post_mix, m_norm_pre_ffn, m_norm_post_ffn, m_b_gate,
         m_b_spatial, m_gla_norm, m_sgu_ln_g, m_sgu_ln_b],
        [v_w_spatial, v_w_gate_up, v_norm_pre_mix, v_norm_post_mix, v_norm_pre_ffn, v_norm_post_ffn, v_b_gate,
         v_b_spatial, v_gla_norm, v_sgu_ln_g, v_sgu_ln_b])
    for store, vals in zip((grads, deltas, new_m, new_v), small):
        store.update(zip(small_names, vals))

    order = ["norm_pre_mix", "w_in", "w_gate_up", "b_gate", "gla_norm", "sgu_ln_g", "sgu_ln_b", "w_spatial", "b_spatial",
             "w_branch_gla", "w_branch_sgu", "w_out", "norm_post_mix", "norm_pre_ffn", "w_ffn_in", "w_ffn_out",
             "norm_post_ffn"]
    out = [loss_out, grad_x[None]]
    for store in (grads, deltas, new_m, new_v):
        out.extend(store[n] for n in order)
    return tuple(out)
```

```python
import jax
import jax.numpy as jnp
from jax import lax
from jax.experimental import pallas as pl
from jax.experimental.pallas import tpu as pltpu

F32 = jnp.float32
BF16 = jnp.bfloat16

D_MODEL = 1024
GLA_HEADS = 4
GLA_DK = 128
GLA_DV = 256
GLA_QK = GLA_HEADS * GLA_DK
GLA_V = GLA_HEADS * GLA_DV
GLA_RANK = 16
GLA_TAU = 16.0
CHUNK = 64
SGU_GROUPS = 4
SGU_BLOCK = 128
SGU_DG = 256
D_FF = 2816
EPS = 1e-6
LANES = 128

OFF_Q, OFF_K, OFF_V, OFF_R, OFF_SU, OFF_SV, OFF_GG, OFF_GS, OFF_AL = 0, 512, 1024, 2048, 3072, 4096, 5120, 6144, 7168
W_GLA, W_SGU, W_MRG = 3072, 2048, 2048
N_MAIN = 7168
N_ALL = N_MAIN + LANES
_IN_SPLITS = (GLA_QK, GLA_QK, GLA_V, GLA_V, GLA_RANK, 1024, 1024, 1024, 1024)
_IN_STARTS = tuple(sum(_IN_SPLITS[:i]) for i in range(len(_IN_SPLITS) + 1))
_IN_DST = (OFF_Q, OFF_K, OFF_V, OFF_R, OFF_AL, OFF_SU, OFF_SV, OFF_GG, OFF_GS)
D_IN = _IN_STARTS[-1]

ADAM_LR = 0.001
ADAM_B1 = 0.9
ADAM_B2 = 0.999
ADAM_EPS = 1e-08
ADAM_WD = 0.01
ADAM_STEP = 10

VMEM_LIMIT_BYTES = 56 * 1024 * 1024
N_CHIPS = 4
N_PEER = N_CHIPS - 1
N_DEV = 8
MESH = pl.DeviceIdType.MESH

_NN = (((1,), (0,)), ((), ()))
_NT = (((1,), (1,)), ((), ()))
_TN = (((0,), (0,)), ((), ()))


def _dot(a, b, dims=_NN):
    return lax.dot_general(a, b, dims, preferred_element_type=F32)


def _split(x):
    hi = x.astype(BF16)
    lo = (x - hi.astype(F32)).astype(BF16)
    return hi, lo


def _dot_bf16(a, b, dims=_NN):
    return _dot(a.astype(BF16), b.astype(BF16), dims)


def _dot_exact_lhs(m, x):
    xh, xl = _split(x)
    return _dot(m, xh) + _dot(m, xl)


def _sigmoid(x):
    return 0.5 * jnp.tanh(0.5 * x) + 0.5


def _log_sigmoid(x):
    return jnp.minimum(x, 0.0) - jnp.log(1.0 + jnp.exp(-jnp.abs(x)))


_GELU_C = 0.7978845608028654
_GELU_A = 0.044715


def _gelu_and_grad(x):
    x2 = x * x
    t = jnp.tanh(_GELU_C * (x + _GELU_A * x * x2))
    g = 0.5 * x * (1.0 + t)
    dg = 0.5 * (1.0 + t) + 0.5 * x * (1.0 - t * t) * (_GELU_C * (1.0 + 3.0 * _GELU_A * x2))
    return g, dg


def _gelu(x):
    t = jnp.tanh(_GELU_C * (x + _GELU_A * x * x * x))
    return 0.5 * x * (1.0 + t)


def _rms_stats(x):
    return lax.rsqrt(jnp.mean(x * x, axis=-1, keepdims=True) + EPS)


def _rms_bwd(dout, y, r, g):
    yhat = y * r
    dn = dout * g
    dy = r * (dn - yhat * jnp.mean(dn * yhat, axis=-1, keepdims=True))
    return dy, dout * yhat


def _whole():
    return pl.BlockSpec(memory_space=pltpu.VMEM)


def _row_tile(T, want):
    t = min(T, want)
    assert T % t == 0
    return t


def _chunk_masks(tT, upper):
    row = lax.broadcasted_iota(jnp.int32, (tT, tT), 0)
    col = lax.broadcasted_iota(jnp.int32, (tT, tT), 1)
    same = (row // CHUNK) == (col // CHUNK)
    tri = (col > row) if upper else (col < row)
    return jnp.where(same & tri, 1.0, 0.0).astype(BF16)


class _Job:
    def __init__(self, ins, out_shapes, scratch, start, finish, mid=None):
        self.ins, self.out_shapes, self.scratch = list(ins), list(out_shapes), list(scratch)
        self.start, self.finish, self.mid = start, finish, mid


def _join(*jobs):
    def split(refs, counts):
        out, at = [], 0
        for n in counts:
            out.append(refs[at:at + n])
            at += n
        return out

    ni, no, ns = [len(j.ins) for j in jobs], [len(j.out_shapes) for j in jobs], [len(j.scratch) for j in jobs]

    def start(ins, outs, scr):
        for j, a, b, c in zip(jobs, split(ins, ni), split(outs, no), split(scr, ns)):
            j.start(a, b, c)

    def finish(ins, outs, scr):
        for j, a, b, c in zip(jobs, split(ins, ni), split(outs, no), split(scr, ns)):
            j.finish(a, b, c)

    def mid(ins, outs, scr):
        for j, a, b, c in zip(jobs, split(ins, ni), split(outs, no), split(scr, ns)):
            if j.mid is not None:
                j.mid(a, b, c)

    return _Job(sum((j.ins for j in jobs), []), sum((j.out_shapes for j in jobs), []),
                sum((j.scratch for j in jobs), []), start, finish, mid if any(j.mid for j in jobs) else None)


def _mesh_pos():
    return lax.axis_index("x"), lax.axis_index("y"), lax.axis_index("c")


def _peer_chips(xi, yi):
    return [(1 - xi, yi), (xi, 1 - yi), (1 - xi, 1 - yi)]


def _half(ci, rows):
    return pl.ds(pl.multiple_of(ci * rows, 8), rows)


def _sds(shape, dtype):
    return jax.ShapeDtypeStruct(tuple(shape), dtype)


def _job_gather(arrs):
    n = len(arrs)
    kinds = 12
    Y0, Y1, X1, X0, ON_X, ON_Y, D2D = 0, 1, 2, 3, 4, 5, 6

    def copies(ins, outs, scr):
        send_sems, recv_sems = scr
        xi, yi, ci = _mesh_pos()
        me, cx, cy, cd = 2 * xi + yi, 2 * (1 - xi) + yi, 2 * xi + (1 - yi), 2 * (1 - xi) + (1 - yi)
        to_x, to_y, to_core = (1 - xi, yi, ci), (xi, 1 - yi, ci), (xi, yi, 1 - ci)
        table = []
        for k in range(n):
            qr = arrs[k].shape[0] // 4

            def rows(core, q):
                return pl.ds(pl.multiple_of((2 * core + q) * qr, 8), qr)

            def cp(kind, src, dst, to):
                s = k * kinds + kind
                return pltpu.make_async_remote_copy(src_ref=src, dst_ref=dst, send_sem=send_sems.at[s],
                                                    recv_sem=recv_sems.at[s], device_id=to, device_id_type=MESH)

            def slab(chip, core, q):
                return outs[k].at[chip, rows(core, q)]

            t = {}
            for kind, q, to, frm in ((Y0, 0, to_y, cy), (Y1, 1, to_y, cy), (X1, 1, to_x, cx), (X0, 0, to_x, cx)):
                mine = ins[k].at[rows(ci, q)]
                t[kind] = (cp(kind, mine, slab(me, ci, q), to), cp(kind, mine, slab(frm, ci, q), to))
            t[ON_X] = (cp(ON_X, slab(cy, ci, 0), slab(cy, ci, 0), to_x), cp(ON_X, slab(cy, ci, 0), slab(cd, ci, 0), to_x))
            t[ON_Y] = (cp(ON_Y, slab(cx, ci, 1), slab(cx, ci, 1), to_y), cp(ON_Y, slab(cx, ci, 1), slab(cd, ci, 1), to_y))
            for i, (chip, q) in enumerate(((cy, 0), (cy, 1), (cx, 1), (cx, 0), (cd, 0), (cd, 1))):
                t[D2D + i] = (cp(D2D + i, slab(chip, ci, q), slab(chip, ci, q), to_core),
                              cp(D2D + i, slab(chip, ci, q), slab(chip, 1 - ci, q), to_core))
            table.append(t)
        return table

    def start(ins, outs, scr):
        table = copies(ins, outs, scr)
        for kind in (Y0, X1, Y1, X0):
            for t in table:
                t[kind][0].start()

    def arrived(table, kind, then):
        for t in table:
            t[kind][1].wait_recv()
            for nxt in then:
                t[nxt][0].start()

    def mid(ins, outs, scr):
        table = copies(ins, outs, scr)
        arrived(table, Y0, (ON_X, D2D + 0))
        arrived(table, X1, (ON_Y, D2D + 2))

    def finish(ins, outs, scr):
        table = copies(ins, outs, scr)
        arrived(table, Y1, (D2D + 1,))
        arrived(table, X0, (D2D + 3,))
        arrived(table, ON_X, (D2D + 4,))
        arrived(table, ON_Y, (D2D + 5,))
        for t in table:
            for i in range(6):
                t[D2D + i][1].wait_recv()
            for kind in range(kinds):
                t[kind][0].wait_send()

    dma = pltpu.SemaphoreType.DMA
    return _Job(arrs, [_sds((N_CHIPS,) + a.shape, a.dtype) for a in arrs], [dma((n * kinds,))] * 2, start, finish, mid)


def _job_scatter(parts):
    n = len(parts)

    def copies(ins, outs, scr):
        send_sems, recv_sems = scr
        xi, yi, ci = _mesh_pos()
        res = []
        for k in range(n):
            for j, (px, py) in enumerate(_peer_chips(xi, yi)):
                s = k * N_PEER + j
                res.append(pltpu.make_async_remote_copy(
                    src_ref=ins[k].at[2 * px + py], dst_ref=outs[k].at[j], send_sem=send_sems.at[s],
                    recv_sem=recv_sems.at[s], device_id=(px, py, ci), device_id_type=MESH))
        return res

    def start(ins, outs, scr):
        for cp in copies(ins, outs, scr):
            cp.start()

    def finish(ins, outs, scr):
        for cp in copies(ins, outs, scr):
            cp.wait_recv()
            cp.wait_send()

    dma = pltpu.SemaphoreType.DMA
    return _Job(parts, [_sds((N_PEER,) + p.shape[1:], p.dtype) for p in parts], [dma((n * N_PEER,))] * 2, start, finish)


def _job_to_other_core(groups):
    pieces = [(g, a, off) for g, group in enumerate(groups) for a, off in group]
    n = len(pieces)

    def geometry(group):
        a0, off0 = group[0]
        if off0 is None:
            return a0.shape
        if a0.ndim == 4:
            return (N_CHIPS, a0.shape[2], a0.shape[3])
        return (a0.shape[0] // 2, sum(a.shape[1] for a, _ in group))

    def copies(ins, outs, scr):
        send_sems, recv_sems = scr
        xi, yi, ci = _mesh_pos()
        res = []
        for p, (g, a, off) in enumerate(pieces):
            if off is None:
                give, land = ins[p], outs[g]
            elif a.ndim == 4:
                give, land = ins[p].at[pl.ds(0, N_CHIPS), 1 - ci], outs[g]
            else:
                hr, w = a.shape[0] // 2, a.shape[1]
                give, land = ins[p].at[_half(1 - ci, hr)], outs[g].at[pl.ds(0, hr), pl.ds(off, w)]
            res.append(pltpu.make_async_remote_copy(
                src_ref=give, dst_ref=land, send_sem=send_sems.at[p], recv_sem=recv_sems.at[p],
                device_id=(xi, yi, 1 - ci), device_id_type=MESH))
        return res

    def start(ins, outs, scr):
        for cp in copies(ins, outs, scr):
            cp.start()

    def finish(ins, outs, scr):
        for cp in copies(ins, outs, scr):
            cp.wait_recv()
            cp.wait_send()

    dma = pltpu.SemaphoreType.DMA
    return _Job([a for _, a, _ in pieces], [_sds(geometry(group), group[0][0].dtype) for group in groups],
                [dma((n,))] * 2, start, finish)


def _call(body, *, name, grid, in_specs, out_specs, out_shape, args, scratch_shapes=(), parallel=False, job=None):
    n_in, n_out, n_scr = len(in_specs), len(out_specs), len(scratch_shapes)

    def params(sem):
        return pltpu.CompilerParams(dimension_semantics=sem, vmem_limit_bytes=VMEM_LIMIT_BYTES)

    if job is None:
        sem = ("parallel" if parallel else "arbitrary",) * len(grid)
        res = pl.pallas_call(
            body, name=name, grid=grid, in_specs=in_specs, out_specs=out_specs, out_shape=out_shape,
            scratch_shapes=list(scratch_shapes), compiler_params=params(sem))(*args)
        return list(res), []
    n_ji, n_jo = len(job.ins), len(job.out_shapes)

    def carried(*refs):
        ins, refs = refs[:n_in], refs[n_in:]
        j_ins, refs = refs[:n_ji], refs[n_ji:]
        outs, refs = refs[:n_out], refs[n_out:]
        j_outs, refs = refs[:n_jo], refs[n_jo:]
        scr, j_scr = refs[:n_scr], refs[n_scr:]
        ids = [pl.program_id(d) for d in range(len(grid))]
        first = ids[0] == 0
        last = ids[0] == grid[0] - 1
        for d in range(1, len(grid)):
            first = first & (ids[d] == 0)
            last = last & (ids[d] == grid[d] - 1)

        @pl.when(first)
        def _():
            job.start(j_ins, j_outs, j_scr)

        if job.mid is not None and grid[0] >= 4:
            half_way = ids[0] == grid[0] // 2
            for d in range(1, len(grid)):
                half_way = half_way & (ids[d] == 0)

            @pl.when(half_way)
            def _():
                job.mid(j_ins, j_outs, j_scr)

        body(*ins, *outs, *scr)

        @pl.when(last)
        def _():
            if job.mid is not None and grid[0] < 4:
                job.mid(j_ins, j_outs, j_scr)
            job.finish(j_ins, j_outs, j_scr)

    hbm = pl.BlockSpec(memory_space=pl.ANY)
    res = pl.pallas_call(
        carried, name=name, grid=grid, in_specs=list(in_specs) + [hbm] * n_ji, out_specs=list(out_specs) + [hbm] * n_jo,
        out_shape=list(out_shape) + job.out_shapes, scratch_shapes=list(scratch_shapes) + job.scratch,
        compiler_params=params(("arbitrary",) * len(grid)))(*args, *job.ins)
    return list(res[:n_out]), list(res[n_out:])


def _run_job(job, name):
    n_i, n_o = len(job.ins), len(job.out_shapes)

    def body(*refs):
        ins, outs, scr = refs[:n_i], refs[n_i:n_i + n_o], refs[n_i + n_o:]
        job.start(ins, outs, scr)
        if job.mid is not None:
            job.mid(ins, outs, scr)
        job.finish(ins, outs, scr)

    hbm = pl.BlockSpec(memory_space=pl.ANY)
    return list(pl.pallas_call(body, name=name, in_specs=[hbm] * n_i, out_specs=[hbm] * n_o, out_shape=job.out_shapes,
                               scratch_shapes=job.scratch)(*job.ins))


def _adam_values(w, m, v, g):
    m2 = ADAM_B1 * m + (1.0 - ADAM_B1) * g
    v2 = ADAM_B2 * v + (1.0 - ADAM_B2) * (g * g)
    delta = -ADAM_LR * ((m2 / (1.0 - ADAM_B1 ** ADAM_STEP)) / (jnp.sqrt(v2 / (1.0 - ADAM_B2 ** ADAM_STEP)) + ADAM_EPS)
                        + ADAM_WD * w)
    return delta, m2, v2


_P_WSP, _P_WGU, _P_NORM, _P_BG, _P_BSP, _P_HEAD, _P_LOSS, _P_ROWS = 0, 512, 576, 608, 616, 624, 720, 728


def _small_sum(dgrads):
    def body(dwsp, dwgu, dg1, dgpm, dgpf, dgpo, dbg, dbspt, dgn, dlng, dlnb, loss_in, tot, pack, slots, send_sems,
             recv_sems):
        xi, yi, ci = _mesh_pos()
        chip = 2 * xi + yi

        pack[...] = jnp.zeros_like(pack)
        for g in range(SGU_GROUPS):
            pack[_P_WSP + g * SGU_BLOCK:_P_WSP + (g + 1) * SGU_BLOCK] = dwsp[g]
        for j in range(N_CHIPS):
            pack[_P_WGU + GLA_RANK * j:_P_WGU + GLA_RANK * (j + 1)] = dwgu[0:GLA_RANK, LANES * j:LANES * (j + 1)]
        for k, r in enumerate((dg1, dgpm, dgpf, dgpo)):
            for q in range(8):
                pack[_P_NORM + 8 * k + q:_P_NORM + 8 * k + q + 1] = r[:, LANES * q:LANES * (q + 1)]
        for q in range(4):
            pack[_P_BG + q:_P_BG + q + 1] = dbg[:, LANES * q:LANES * (q + 1)]
        pack[_P_BSP:_P_BSP + SGU_GROUPS] = jnp.transpose(dbspt[...])[0:SGU_GROUPS]
        for k, r in enumerate((dgn, dlng, dlnb)):
            for j in range(N_CHIPS):
                for hh in range(4):
                    row = _P_HEAD + 32 * k + 8 * j + hh
                    pack[row:row + 1, 0:64] = r[:, 256 * hh + 64 * j:256 * hh + 64 * (j + 1)]
        pack[_P_LOSS:_P_LOSS + 1] = loss_in[...]

        to_sibling = pltpu.make_async_remote_copy(
            src_ref=pack, dst_ref=tot, send_sem=send_sems.at[N_PEER], recv_sem=recv_sems.at[N_PEER],
            device_id=(xi, yi, 1 - ci), device_id_type=MESH)
        to_sibling.start()
        to_sibling.wait_recv()
        to_sibling.wait_send()
        pack[...] = pack[...] + tot[...]
        slots[chip] = pack[...]

        def copy(j, slot):
            px, py = _peer_chips(xi, yi)[j]
            return pltpu.make_async_remote_copy(
                src_ref=pack, dst_ref=slots.at[slot(2 * px + py)], send_sem=send_sems.at[j], recv_sem=recv_sems.at[j],
                device_id=(px, py, ci), device_id_type=MESH)

        sends = [copy(j, lambda peer_chip: chip) for j in range(N_PEER)]
        for cp in sends:
            cp.start()
        for j in range(N_PEER):
            copy(j, lambda peer_chip: peer_chip).wait_recv()
        for cp in sends:
            cp.wait_send()
        acc = slots[0]
        for d in range(1, N_CHIPS):
            acc = acc + slots[d]
        tot[...] = acc

    return pl.pallas_call(
        body, name="small_sum", in_specs=[_whole()] * 12, out_specs=_whole(), out_shape=_sds((_P_ROWS, LANES), F32),
        scratch_shapes=[pltpu.VMEM((_P_ROWS, LANES), F32), pltpu.VMEM((N_CHIPS, _P_ROWS, LANES), F32),
                        pltpu.SemaphoreType.DMA((N_PEER + 1,)), pltpu.SemaphoreType.DMA((N_PEER + 1,))],
        compiler_params=pltpu.CompilerParams(vmem_limit_bytes=VMEM_LIMIT_BYTES),
    )(*dgrads)


def _small_adamw(tot, ws, ms, vs):
    n = len(ws)

    def body(*refs):
        tot = refs[0]
        w_refs, m_refs, v_refs = refs[1:1 + n], refs[1 + n:1 + 2 * n], refs[1 + 2 * n:1 + 3 * n]
        loss_out = refs[1 + 3 * n]
        outs = refs[2 + 3 * n:]
        chip = 2 * lax.axis_index("x") + lax.axis_index("y")
        loss_out[...] = tot[_P_LOSS:_P_LOSS + 1, 0:1]

        def step(k, g, pick, put):
            d, m2, v2 = _adam_values(pick(w_refs[k]), pick(m_refs[k]), pick(v_refs[k]), g)
            for o, val in zip((outs[k], outs[n + k], outs[2 * n + k], outs[3 * n + k]), (g, d, m2, v2)):
                put(o, val)

        def whole(ref):
            return ref[0]

        def put_whole(ref, val):
            ref[0] = val

        for g in range(SGU_GROUPS):
            def pick_g(ref, g=g):
                return ref[0, g]

            def put_g(ref, val, g=g):
                ref[0, g] = val

            step(0, tot[_P_WSP + g * SGU_BLOCK:_P_WSP + (g + 1) * SGU_BLOCK], pick_g, put_g)
        step(1, tot[pl.ds(pl.multiple_of(_P_WGU + GLA_RANK * chip, GLA_RANK), GLA_RANK), :], whole, put_whole)
        for k, (base, chunks) in enumerate(((_P_NORM, 8), (_P_NORM + 8, 8), (_P_NORM + 16, 8), (_P_NORM + 24, 8), (_P_BG, 4))):
            for q in range(chunks):
                def pick_q(ref, q=q):
                    return ref[:, LANES * q:LANES * (q + 1)]

                def put_q(ref, val, q=q):
                    ref[:, LANES * q:LANES * (q + 1)] = val

                step(2 + k, tot[base + q:base + q + 1], pick_q, put_q)
        step(7, tot[_P_BSP:_P_BSP + SGU_GROUPS], whole, put_whole)
        for k in range(3):
            mine = tot[pl.ds(pl.multiple_of(_P_HEAD + 32 * k + 8 * chip, 8), 8), :]
            step(8 + k, mine[0:4, 0:64], whole, put_whole)

    shapes = [_sds(w.shape, F32) for w in ws]
    res = pl.pallas_call(
        body, name="small_adamw", in_specs=[_whole()] * (1 + 3 * n), out_specs=[_whole()] * (1 + 4 * n),
        out_shape=[_sds((1, 1), F32)] + shapes * 4,
        compiler_params=pltpu.CompilerParams(vmem_limit_bytes=VMEM_LIMIT_BYTES),
    )(tot, *ws, *ms, *vs)
    return res[0].reshape(()), [list(res[1 + i * n:1 + (i + 1) * n]) for i in range(4)]


def _w_in_pieces():
    blk = D_IN // N_CHIPS
    pieces = []
    for s in range(len(_IN_SPLITS)):
        lo_s, hi_s = _IN_STARTS[s], _IN_STARTS[s + 1]
        for j in range(N_CHIPS):
            lo, hi = max(lo_s, j * blk), min(hi_s, (j + 1) * blk)
            if lo < hi:
                pieces.append((j, lo - j * blk, _IN_DST[s] + lo - lo_s, hi - lo))
    return pieces


def _relayout_w_in(gathered):
    _, rows, blk = gathered.shape
    tr = 256

    def body(g_ref, o_ref):
        o_ref[:, OFF_AL:N_ALL] = jnp.zeros((tr, LANES), BF16)
        for j, src, dst, w in _w_in_pieces():
            o_ref[:, dst:dst + w] = g_ref[j, :, src:src + w]

    res, _ = _call(body, name="relayout_w_in", grid=(rows // tr,), parallel=True,
                   in_specs=[pl.BlockSpec((N_CHIPS, tr, blk), lambda i: (0, i, 0))],
                   out_specs=[pl.BlockSpec((tr, N_ALL), lambda i: (i, 0))],
                   out_shape=[_sds((rows, N_ALL), BF16)], args=(gathered,))
    return res[0]


def _update_row_tile(rows):
    for t in range(min(rows, 256), 7, -8):
        if rows % t == 0:
            return t
    return rows


def _my_half(first_ref, second_ref):
    return jnp.where(lax.axis_index("c") == 0, first_ref[...], second_ref[...])


def _presum_w_in(dws, theirs, row0, rows, name, job=None):
    hr = theirs[0].shape[0]
    blk = D_IN // N_CHIPS
    tr = 64
    assert row0 % tr == 0 and rows % tr == 0
    nh, t0 = hr // tr, row0 // tr
    n = len(dws)

    def body(*refs):
        dw_refs, q_refs, (o_ref, s_scr) = refs[:2 * n], refs[2 * n:3 * n], refs[3 * n:]
        for p, (a, off) in enumerate(dws):
            w = a.shape[1]
            s_scr[:, off:off + w] = (_my_half(dw_refs[2 * p], dw_refs[2 * p + 1]) + q_refs[p][...]).astype(BF16)
        for j, src, dst, w in _w_in_pieces():
            o_ref[j, :, src:src + w] = s_scr[:, dst:dst + w]

    in_specs, args = [], []
    for a, _ in dws:
        w = a.shape[1]
        in_specs += [pl.BlockSpec((tr, w), lambda i: (i + t0, 0)), pl.BlockSpec((tr, w), lambda i: (i + t0 + nh, 0))]
        args += [a, a]
    in_specs += [pl.BlockSpec((tr, q.shape[1]), lambda i: (i + t0, 0)) for q in theirs]
    res, jres = _call(body, name=name, grid=(rows // tr,), parallel=True, in_specs=in_specs,
                      out_specs=[pl.BlockSpec((N_CHIPS, tr, blk), lambda i: (0, i, 0))],
                      out_shape=[_sds((N_CHIPS, rows, blk), BF16)], scratch_shapes=[pltpu.VMEM((tr, N_ALL), BF16)],
                      args=(*args, *theirs), job=job)
    return res[0], jres


def _presum(dw, theirs, name):
    if dw.ndim == 4:
        _, _, hr, c = dw.shape
        tr = _update_row_tile(hr)
        first = pl.BlockSpec((1, 1, tr, c), lambda j, i: (j, 0, i, 0))
        second = pl.BlockSpec((1, 1, tr, c), lambda j, i: (j, 1, i, 0))
        other = pl.BlockSpec((1, tr, c), lambda j, i: (j, i, 0))
    else:
        hr, c = dw.shape[0] // 2, dw.shape[1] // N_CHIPS
        tr = _update_row_tile(hr)
        nh = hr // tr
        first = pl.BlockSpec((tr, c), lambda j, i: (i, j))
        second = pl.BlockSpec((tr, c), lambda j, i: (i + nh, j))
        other = pl.BlockSpec((tr, c), lambda j, i: (i, j))

    def body(a_ref, b_ref, q_ref, o_ref):
        mine = _my_half(a_ref, b_ref).reshape(tr, c)
        o_ref[...] = (mine + q_ref[...].reshape(tr, c)).astype(BF16).reshape(o_ref.shape)

    res, _ = _call(body, name=name, grid=(N_CHIPS, hr // tr), parallel=True, in_specs=[first, second, other],
                   out_specs=[pl.BlockSpec((1, tr, c), lambda j, i: (j, i, 0))],
                   out_shape=[_sds((N_CHIPS, hr, c), BF16)], args=(dw, dw, theirs))
    return res[0]


def _sum_slots(own, slots, name):
    rows, cols = own.shape
    tr = _update_row_tile(rows)

    def body(own_ref, s_ref, o_ref):
        acc = own_ref[...].astype(F32)
        for j in range(N_PEER):
            acc = acc + s_ref[j].astype(F32)
        o_ref[...] = acc

    res, _ = _call(body, name=name, grid=(rows // tr,), parallel=True,
                   in_specs=[pl.BlockSpec((tr, cols), lambda i: (i, 0)), pl.BlockSpec((N_PEER, tr, cols), lambda i: (0, i, 0))],
                   out_specs=[pl.BlockSpec((tr, cols), lambda i: (i, 0))], out_shape=[_sds((rows, cols), F32)],
                   args=(own, slots))
    return res[0]


def _adamw(w, m, v, g_mine, g_theirs, name, job=None):
    rows, cols = w.shape
    part_rows = [p.shape[0] for p in g_mine]
    assert sum(part_rows) == rows // 2 and [p.shape[0] for p in g_theirs] == part_rows
    tr = _update_row_tile(min(part_rows))
    assert all(r % tr == 0 for r in part_rows)
    nh = (rows // 2) // tr
    starts = [sum(part_rows[:k]) // tr for k in range(len(part_rows))]
    n_parts = len(part_rows)

    def body(w_ref, m_ref, v_ref, *rest):
        g_refs, (g_out, d_out, m_out, v_out) = rest[:-4], rest[-4:]
        step = pl.program_id(0)
        mine_here = (step // nh) == lax.axis_index("c")
        q = step % nh
        g = None
        for k in reversed(range(n_parts)):
            val = jnp.where(mine_here, g_refs[k][...], g_refs[n_parts + k][...])
            g = val if g is None else jnp.where(q < starts[k + 1], val, g)
        d, m2, v2 = _adam_values(w_ref[...], m_ref[...], v_ref[...], g)
        g_out[...] = g
        m_out[...] = m2
        v_out[...] = v2
        d_out[...] = d

    def g_spec(k):
        last = part_rows[k] // tr - 1
        return pl.BlockSpec((tr, cols), lambda i: (jnp.clip(i % nh - starts[k], 0, last), 0))

    spec = pl.BlockSpec((tr, cols), lambda i: (i, 0))
    return _call(body, name=name, grid=(rows // tr,), parallel=True,
                 in_specs=[spec] * 3 + [g_spec(k) for k in range(n_parts)] * 2, out_specs=[spec] * 4,
                 out_shape=[_sds((rows, cols), F32)] * 4, args=(w, m, v, *g_mine, *g_theirs), job=job)


def _transposed_cast(wt):
    cols, rows = wt.shape

    def body(x_ref, o_ref):
        o_ref[...] = jnp.transpose(x_ref[...]).astype(BF16)

    res, _ = _call(body, name="transpose_w_in", grid=(pl.cdiv(cols, LANES),), parallel=True,
                   in_specs=[pl.BlockSpec((LANES, rows), lambda j: (j, 0))],
                   out_specs=[pl.BlockSpec((rows, LANES), lambda j: (0, j))], out_shape=[_sds((rows, cols), BF16)],
                   args=(wt,))
    return res[0]


def _adamw_transposed(wt, mt, vt, g_mine, g_theirs, name):
    cols, rows = wt.shape
    n_parts = len(g_mine)

    def body(w_ref, m_ref, v_ref, *rest):
        g_refs, (g_out, d_out, m_out, v_out) = rest[:-4], rest[-4:]
        mine = jnp.concatenate([r[...] for r in g_refs[:n_parts]], axis=0)
        theirs = jnp.concatenate([r[...] for r in g_refs[n_parts:]], axis=0)
        first = lax.axis_index("c") == 0
        g = jnp.transpose(jnp.concatenate([jnp.where(first, mine, theirs), jnp.where(first, theirs, mine)], axis=0))
        d, m2, v2 = _adam_values(w_ref[...], m_ref[...], v_ref[...], g)
        g_out[...] = g
        m_out[...] = m2
        v_out[...] = v2
        d_out[...] = d

    spec = pl.BlockSpec((LANES, rows), lambda j: (j, 0))
    g_specs = [pl.BlockSpec((p.shape[0], LANES), lambda j: (0, j)) for p in g_mine] * 2
    res, _ = _call(body, name=name, grid=(pl.cdiv(cols, LANES),), parallel=True, in_specs=[spec] * 3 + g_specs,
                   out_specs=[spec] * 4, out_shape=[_sds((cols, rows), F32)] * 4, args=(wt, mt, vt, *g_mine, *g_theirs))
    return res


def _inproj_fwd(x, g1, w_all, job=None):
    T = x.shape[0]
    tT = _row_tile(T, 512)

    def body(x_ref, g_ref, w_ref, a_ref, proj_ref, alow_ref):
        xv = x_ref[...]
        a = (xv * _rms_stats(xv) * g_ref[...]).astype(BF16)
        a_ref[...] = a
        for j in range(N_MAIN // 1024):
            cols = slice(j * 1024, (j + 1) * 1024)
            proj_ref[:, cols] = _dot(a, w_ref[:, cols]).astype(BF16)
        alow_ref[...] = _dot(a, w_ref[:, N_MAIN:N_ALL])

    row = lambda w: pl.BlockSpec((tT, w), lambda i: (i, 0))
    return _call(
        body, name="inproj_fwd", grid=(T // tT,), parallel=True,
        in_specs=[row(D_MODEL), pl.BlockSpec((1, D_MODEL), lambda i: (0, 0)), _whole()],
        out_specs=[row(D_MODEL), row(N_MAIN), row(LANES)],
        out_shape=[_sds((T, D_MODEL), BF16), _sds((T, N_MAIN), BF16), _sds((T, LANES), F32)],
        args=(x, g1, w_all), job=job)


def _gla_decay_terms(al_ref, wgu_ref, bg_ref, later_ref):
    logit = _dot_bf16(al_ref[...], wgu_ref[...]) + bg_ref[...]
    la = _log_sigmoid(logit) * (1.0 / GLA_TAU)
    delta = _dot_exact_lhs(later_ref[...], la)
    return logit, la, delta


def _gla_fwd(proj, alow, wgu, b_gate, gn, job=None):
    T = proj.shape[0]
    tT = _row_tile(T, 512)
    nc = tT // CHUNK

    def body(q_ref, k_ref, v_ref, r_ref, al_ref, wgu_ref, bg_ref, gn_ref, later_ref, y_ref, st_ref, s_scr):
        @pl.when(pl.program_id(0) == 0)
        def _():
            s_scr[...] = jnp.zeros_like(s_scr)

        _, la, delta = _gla_decay_terms(al_ref, wgu_ref, bg_ref, later_ref)
        kdec = (k_ref[...].astype(F32) * jnp.exp(delta)).astype(BF16)
        heads = range(GLA_HEADS)
        kcs = [slice(h * GLA_DK, (h + 1) * GLA_DK) for h in heads]
        vcs = [slice(h * GLA_DV, (h + 1) * GLA_DV) for h in heads]
        state = [s_scr[h] for h in heads]
        for c in range(nc):
            rows = slice(c * CHUNK, (c + 1) * CHUNK)
            first = slice(c * CHUNK, c * CHUNK + 1)
            dec = jnp.exp(la[first, :] + delta[first, :])
            upd_t = [_dot(v_ref[rows, vcs[h]], kdec[rows, kcs[h]], _TN) for h in heads]
            qs = [(q_ref[rows, kcs[h]].astype(F32) * (GLA_DK ** -0.5)).astype(BF16) for h in heads]
            for h in heads:
                state[h] = state[h] * dec[:, kcs[h]] + upd_t[h]
                st_ref[c, h] = state[h]
            o = [_dot(qs[h], state[h].astype(BF16), _NT) for h in heads]
            for h in heads:
                on = o[h] * _rms_stats(o[h]) * gn_ref[:, vcs[h]]
                rr = r_ref[rows, vcs[h]].astype(F32)
                y_ref[rows, vcs[h]] = (on * (rr * _sigmoid(rr))).astype(BF16)
        for h in heads:
            s_scr[h] = state[h]

    blk = lambda w, j: pl.BlockSpec((tT, w), lambda i: (i, j))
    return _call(
        body, name="gla_fwd", grid=(T // tT,),
        in_specs=[blk(512, 0), blk(512, 1), blk(1024, 1), blk(1024, 2), blk(LANES, 0)] + [_whole()] * 4,
        out_specs=[pl.BlockSpec((tT, GLA_V), lambda i: (i, 0)),
                   pl.BlockSpec((nc, GLA_HEADS, GLA_DV, GLA_DK), lambda i: (i, 0, 0, 0))],
        out_shape=[_sds((T, GLA_V), BF16), _sds((T // CHUNK, GLA_HEADS, GLA_DV, GLA_DK), F32)],
        scratch_shapes=[pltpu.VMEM((GLA_HEADS, GLA_DV, GLA_DK), F32)],
        args=(proj, proj, proj, proj, alow, wgu, b_gate, gn, _chunk_masks(tT, upper=True)), job=job)


def _sgu_mask():
    i = lax.broadcasted_iota(jnp.int32, (SGU_BLOCK, SGU_BLOCK), 0)
    j = lax.broadcasted_iota(jnp.int32, (SGU_BLOCK, SGU_BLOCK), 1)
    return lax.shift_right_logical(j, 6) <= lax.shift_right_logical(i, 6)


def _sgu_fwd(proj, ln_g, ln_b, w_sp, b_sp_t):
    T = proj.shape[0]
    tT = _row_tile(T, 512)
    nb = tT // SGU_BLOCK

    def body(su_ref, sv_ref, lg_ref, lb_ref, w_ref, b_ref, y_ref):
        mask = _sgu_mask()
        for g in range(SGU_GROUPS):
            gc = slice(g * SGU_DG, (g + 1) * SGU_DG)
            wm = jnp.where(mask, w_ref[g], 0.0).astype(BF16)
            vf = _gelu(sv_ref[:, gc].astype(F32))
            mu = jnp.mean(vf, axis=-1, keepdims=True)
            vc = vf - mu
            rstd = lax.rsqrt(jnp.mean(vc * vc, axis=-1, keepdims=True) + EPS)
            vn = (vc * rstd * lg_ref[:, gc] + lb_ref[:, gc]).astype(BF16)
            u = _gelu(su_ref[:, gc].astype(F32))
            for b in range(nb):
                rows = slice(b * SGU_BLOCK, (b + 1) * SGU_BLOCK)
                mixed = _dot(wm, vn[rows, :]) + b_ref[:, g:g + 1]
                y_ref[rows, gc] = (u[rows, :] * mixed).astype(BF16)

    blk = lambda j: pl.BlockSpec((tT, 1024), lambda i: (i, j))
    res, _ = _call(body, name="sgu_fwd", grid=(T // tT,), parallel=True,
                   in_specs=[blk(3), blk(4), _whole(), _whole(), _whole(), _whole()],
                   out_specs=[pl.BlockSpec((tT, 1024), lambda i: (i, 0))], out_shape=[_sds((T, 1024), BF16)],
                   args=(proj, proj, ln_g, ln_b, w_sp, b_sp_t))
    return res[0]


def _merge_fwd(x, proj, y_gla, y_sgu, w_bg, w_bs, w_o, g_pm, job=None):
    T = x.shape[0]
    tT = _row_tile(T, 512)

    def body(x_ref, gg_ref, gs_ref, yg_ref, ys_ref, wbg_ref, wbs_ref, wo_ref, g_ref,
             zg_ref, zs_ref, mg_ref, mix_ref, x1_ref):
        zg = _dot(yg_ref[...], wbg_ref[...])
        zs = _dot(ys_ref[...], wbs_ref[...])
        zg_ref[...] = zg.astype(BF16)
        zs_ref[...] = zs.astype(BF16)
        merged = (_sigmoid(gg_ref[...].astype(F32)) * zg + _sigmoid(gs_ref[...].astype(F32)) * zs).astype(BF16)
        mg_ref[...] = merged
        mix = _dot(merged, wo_ref[...])
        mix_ref[...] = mix.astype(BF16)
        x1_ref[...] = x_ref[...] + mix * _rms_stats(mix) * g_ref[...]

    row = pl.BlockSpec((tT, D_MODEL), lambda i: (i, 0))
    blk = lambda j: pl.BlockSpec((tT, 1024), lambda i: (i, j))
    sds = lambda dt: _sds((T, D_MODEL), dt)
    return _call(body, name="merge_fwd", grid=(T // tT,), parallel=True,
                 in_specs=[row, blk(5), blk(6), row, row, _whole(), _whole(), _whole(),
                           pl.BlockSpec((1, D_MODEL), lambda i: (0, 0))],
                 out_specs=[row] * 5, out_shape=[sds(BF16), sds(BF16), sds(BF16), sds(BF16), sds(F32)],
                 args=(x, proj, proj, y_gla, y_sgu, w_bg, w_bs, w_o, g_pm), job=job)


def _ffn_fwd_bwd(x1, tgt, w_fi_top, w_fi_bot, w_fo, g_pf, g_po):
    T = x1.shape[0]
    tT = _row_tile(T, 256)
    half = D_FF // 2
    kh = D_MODEL // 2

    def body(x1_ref, t_ref, top_ref, bot_ref, wfo_ref, gpf_ref, gpo_ref,
             h_ref, f_ref, dgu_ref, dy_ref, dx1_ref, loss_ref, dgpf_ref, dgpo_ref, gu_scr):
        @pl.when(pl.program_id(0) == 0)
        def _():
            loss_ref[...] = jnp.zeros_like(loss_ref)
            dgpf_ref[...] = jnp.zeros_like(dgpf_ref)
            dgpo_ref[...] = jnp.zeros_like(dgpo_ref)

        x1v = x1_ref[...]
        r2 = _rms_stats(x1v)
        h = (x1v * r2 * gpf_ref[...]).astype(BF16)
        h_ref[...] = h
        y = jnp.zeros((tT, D_MODEL), F32)
        for j in range(2):
            gc = slice(j * half, (j + 1) * half)
            uc = slice(D_FF + j * half, D_FF + (j + 1) * half)
            gate = _dot(h[:, :kh], top_ref[j]) + _dot(h[:, kh:], bot_ref[j])
            up = _dot(h[:, :kh], top_ref[2 + j]) + _dot(h[:, kh:], bot_ref[2 + j])
            gu_scr[:, gc] = gate
            gu_scr[:, uc] = up
            f = (gate * _sigmoid(gate) * up).astype(BF16)
            f_ref[:, gc] = f
            y = y + _dot(f, wfo_ref[gc, :])
        r3 = _rms_stats(y)
        x2 = x1v + y * r3 * gpo_ref[...]
        err = x2 - t_ref[...]
        loss_ref[...] += jnp.sum(err * err) * (0.5 / D_MODEL)
        dx2 = err * (1.0 / D_MODEL)
        dy, dg = _rms_bwd(dx2, y, r3, gpo_ref[...])
        dgpo_ref[...] += jnp.sum(dg, axis=0, keepdims=True)
        dyb = dy.astype(BF16)
        dy_ref[...] = dyb
        dh_top = jnp.zeros((tT, kh), F32)
        dh_bot = jnp.zeros((tT, kh), F32)
        for j in range(2):
            gc = slice(j * half, (j + 1) * half)
            uc = slice(D_FF + j * half, D_FF + (j + 1) * half)
            df = _dot(dyb, wfo_ref[gc, :], _NT)
            gate = gu_scr[:, gc]
            up = gu_scr[:, uc]
            sg = _sigmoid(gate)
            dgate = (df * up * (sg * (1.0 + gate * (1.0 - sg)))).astype(BF16)
            dup = (df * (gate * sg)).astype(BF16)
            dgu_ref[:, gc] = dgate
            dgu_ref[:, uc] = dup
            dh_top = dh_top + _dot(dgate, top_ref[j], _NT) + _dot(dup, top_ref[2 + j], _NT)
            dh_bot = dh_bot + _dot(dgate, bot_ref[j], _NT) + _dot(dup, bot_ref[2 + j], _NT)
        dh = jnp.concatenate([dh_top, dh_bot], axis=1)
        dx1n, dg2 = _rms_bwd(dh, x1v, r2, gpf_ref[...])
        dgpf_ref[...] += jnp.sum(dg2, axis=0, keepdims=True)
        dx1_ref[...] = dx2 + dx1n

    row = lambda w: pl.BlockSpec((tT, w), lambda i: (i, 0))
    vec = pl.BlockSpec((1, D_MODEL), lambda i: (0, 0))
    res, _ = _call(
        body, name="ffn_fwd_bwd", grid=(T // tT,),
        in_specs=[row(D_MODEL), row(D_MODEL), _whole(), _whole(), _whole(), vec, vec],
        out_specs=[row(D_MODEL), row(D_FF), row(2 * D_FF), row(D_MODEL), row(D_MODEL),
                   pl.BlockSpec((1, LANES), lambda i: (0, 0)), vec, vec],
        out_shape=[_sds((T, D_MODEL), BF16), _sds((T, D_FF), BF16), _sds((T, 2 * D_FF), BF16), _sds((T, D_MODEL), BF16),
                   _sds((T, D_MODEL), F32), _sds((1, LANES), F32), _sds((1, D_MODEL), F32), _sds((1, D_MODEL), F32)],
        scratch_shapes=[pltpu.VMEM((tT, 2 * D_FF), F32)], args=(x1, tgt, w_fi_top, w_fi_bot, w_fo, g_pf, g_po))
    return res


def _merge_bwd(dx1, mix, proj, zg, zs, w_bg, w_bs, w_o, g_pm, job=None):
    T = dx1.shape[0]
    tT = _row_tile(T, 512)

    def body(dx1_ref, mix_ref, gg_ref, gs_ref, zg_ref, zs_ref, wbg_ref, wbs_ref, wo_ref, g_ref,
             dmix_ref, dzg_ref, dzs_ref, dgate_ref, dyg_ref, dys_ref, dgpm_ref):
        @pl.when(pl.program_id(0) == 0)
        def _():
            dgpm_ref[...] = jnp.zeros_like(dgpm_ref)

        mix = mix_ref[...].astype(F32)
        dmix, dg = _rms_bwd(dx1_ref[...], mix, _rms_stats(mix), g_ref[...])
        dgpm_ref[...] += jnp.sum(dg, axis=0, keepdims=True)
        dmb = dmix.astype(BF16)
        dmix_ref[...] = dmb
        dmerged = _dot(dmb, wo_ref[...], _NT)
        for k, (gate_ref, z_ref, w_ref, dz_ref, dy_ref) in enumerate((
                (gg_ref, zg_ref, wbg_ref, dzg_ref, dyg_ref), (gs_ref, zs_ref, wbs_ref, dzs_ref, dys_ref))):
            sg = _sigmoid(gate_ref[...].astype(F32))
            dz = (dmerged * sg).astype(BF16)
            dz_ref[...] = dz
            dgate_ref[:, k * 1024:(k + 1) * 1024] = (dmerged * z_ref[...].astype(F32) * (sg * (1.0 - sg))).astype(BF16)
            dy_ref[...] = _dot(dz, w_ref[...], _NT).astype(BF16)

    row = pl.BlockSpec((tT, D_MODEL), lambda i: (i, 0))
    blk = lambda j: pl.BlockSpec((tT, 1024), lambda i: (i, j))
    vec = pl.BlockSpec((1, D_MODEL), lambda i: (0, 0))
    sds = _sds((T, D_MODEL), BF16)
    return _call(
        body, name="merge_bwd", grid=(T // tT,),
        in_specs=[row, row, blk(5), blk(6), row, row, _whole(), _whole(), _whole(), vec],
        out_specs=[row, row, row, pl.BlockSpec((tT, W_MRG), lambda i: (i, 0)), row, row, vec],
        out_shape=[sds, sds, sds, _sds((T, W_MRG), BF16), sds, sds, _sds((1, D_MODEL), F32)],
        args=(dx1, mix, proj, proj, zg, zs, w_bg, w_bs, w_o, g_pm), job=job)


def _sgu_bwd(proj, dy_sgu, ln_g, ln_b, w_sp, b_sp_t, job=None):
    T = proj.shape[0]
    tT = _row_tile(T, 512)
    nb = tT // SGU_BLOCK

    def body(su_ref, sv_ref, dy_ref, lg_ref, lb_ref, w_ref, b_ref, dp_ref, dw_ref, dbt_ref, dlg_ref, dlb_ref):
        @pl.when(pl.program_id(0) == 0)
        def _():
            dw_ref[...] = jnp.zeros_like(dw_ref)
            dbt_ref[...] = jnp.zeros_like(dbt_ref)
            dlg_ref[...] = jnp.zeros_like(dlg_ref)
            dlb_ref[...] = jnp.zeros_like(dlb_ref)

        mask = _sgu_mask()
        lane = lax.broadcasted_iota(jnp.int32, (SGU_BLOCK, LANES), 1)
        for g in range(SGU_GROUPS):
            gc = slice(g * SGU_DG, (g + 1) * SGU_DG)
            gc_v = slice(1024 + g * SGU_DG, 1024 + (g + 1) * SGU_DG)
            wm = jnp.where(mask, w_ref[g], 0.0).astype(BF16)
            vf, dvf_dsv = _gelu_and_grad(sv_ref[:, gc].astype(F32))
            mu = jnp.mean(vf, axis=-1, keepdims=True)
            vc = vf - mu
            rstd = lax.rsqrt(jnp.mean(vc * vc, axis=-1, keepdims=True) + EPS)
            vhat = vc * rstd
            vn = (vhat * lg_ref[:, gc] + lb_ref[:, gc]).astype(BF16)
            u, du_dsu = _gelu_and_grad(su_ref[:, gc].astype(F32))
            dy = dy_ref[:, gc].astype(F32)
            dmixed = (dy * u).astype(BF16)
            dvn_parts = []
            dw_acc = jnp.zeros((SGU_BLOCK, SGU_BLOCK), F32)
            db_acc = jnp.zeros((SGU_BLOCK, 1), F32)
            for b in range(nb):
                rows = slice(b * SGU_BLOCK, (b + 1) * SGU_BLOCK)
                mixed = _dot(wm, vn[rows, :]) + b_ref[:, g:g + 1]
                dp_ref[rows, gc] = (dy[rows, :] * mixed * du_dsu[rows, :]).astype(BF16)
                dvn_parts.append(_dot(wm, dmixed[rows, :], _TN))
                dw_acc = dw_acc + _dot(dmixed[rows, :], vn[rows, :], _NT)
                db_acc = db_acc + jnp.sum(dmixed[rows, :].astype(F32), axis=-1, keepdims=True)
            dw_ref[g] += jnp.where(mask, dw_acc, 0.0)
            dbt_ref[...] += jnp.where(lane == g, db_acc, 0.0)
            dvn = jnp.concatenate(dvn_parts, axis=0)
            dlg_ref[:, gc] += jnp.sum(dvn * vhat, axis=0, keepdims=True)
            dlb_ref[:, gc] += jnp.sum(dvn, axis=0, keepdims=True)
            dvh = dvn * lg_ref[:, gc]
            dvf = rstd * (dvh - jnp.mean(dvh, axis=-1, keepdims=True)
                          - vhat * jnp.mean(dvh * vhat, axis=-1, keepdims=True))
            dp_ref[:, gc_v] = (dvf * dvf_dsv).astype(BF16)

    blk = lambda j: pl.BlockSpec((tT, 1024), lambda i: (i, j))
    row = lambda w: pl.BlockSpec((tT, w), lambda i: (i, 0))
    vec = pl.BlockSpec((1, 1024), lambda i: (0, 0))
    return _call(
        body, name="sgu_bwd", grid=(T // tT,),
        in_specs=[blk(3), blk(4), row(1024), _whole(), _whole(), _whole(), _whole()],
        out_specs=[row(W_SGU), pl.BlockSpec((SGU_GROUPS, SGU_BLOCK, SGU_BLOCK), lambda i: (0, 0, 0)),
                   pl.BlockSpec((SGU_BLOCK, LANES), lambda i: (0, 0)), vec, vec],
        out_shape=[_sds((T, W_SGU), BF16), _sds((SGU_GROUPS, SGU_BLOCK, SGU_BLOCK), F32), _sds((SGU_BLOCK, LANES), F32),
                   _sds((1, 1024), F32), _sds((1, 1024), F32)],
        args=(proj, proj, dy_sgu, ln_g, ln_b, w_sp, b_sp_t), job=job)


def _gla_bwd(proj, alow, wgu, b_gate, gn, states, dy_gla, job=None):
    T = proj.shape[0]
    tT = _row_tile(T, 512)
    nc = tT // CHUNK
    nt = T // tT

    def body(q_ref, k_ref, v_ref, r_ref, al_ref, wgu_ref, bg_ref, gn_ref, later_ref, earlier_ref, st_ref, sp_ref, dy_ref,
             dp_ref, dal_ref, dgn_ref, dbg_ref, dwgu_ref, g_scr, dd_scr, dt_scr):
        step = pl.program_id(0)

        @pl.when(step == 0)
        def _():
            g_scr[...] = jnp.zeros_like(g_scr)
            dgn_ref[...] = jnp.zeros_like(dgn_ref)
            dbg_ref[...] = jnp.zeros_like(dbg_ref)
            dwgu_ref[...] = jnp.zeros_like(dwgu_ref)

        has_prev = jnp.where(step == nt - 1, 0.0, 1.0)
        logit, la, delta = _gla_decay_terms(al_ref, wgu_ref, bg_ref, later_ref)
        e = jnp.exp(delta)
        kdec_f = k_ref[...].astype(F32) * e
        kdec = kdec_f.astype(BF16)
        heads = range(GLA_HEADS)
        kcs = [slice(h * GLA_DK, (h + 1) * GLA_DK) for h in heads]
        vcs = [slice(h * GLA_DV, (h + 1) * GLA_DV) for h in heads]
        carry = [g_scr[h] for h in heads]
        dgn_acc = [jnp.zeros((1, GLA_DV), F32) for _ in heads]
        for c in reversed(range(nc)):
            rows = slice(c * CHUNK, (c + 1) * CHUNK)
            first = slice(c * CHUNK, c * CHUNK + 1)
            dec = jnp.exp(la[first, :] + delta[first, :])
            s_b = [st_ref[c, h].astype(BF16) for h in heads]
            qs = [(q_ref[rows, kcs[h]].astype(F32) * (GLA_DK ** -0.5)).astype(BF16) for h in heads]
            o = [_dot(qs[h], s_b[h], _NT) for h in heads]
            do = []
            for h in heads:
                rstd = _rms_stats(o[h])
                ohat = o[h] * rstd
                gnh = gn_ref[:, vcs[h]]
                dy = dy_ref[rows, vcs[h]].astype(F32)
                rr = r_ref[rows, vcs[h]].astype(F32)
                sg = _sigmoid(rr)
                don = dy * (rr * sg)
                dp_ref[rows, OFF_R + h * GLA_DV:OFF_R + (h + 1) * GLA_DV] = (
                    dy * (ohat * gnh) * (sg * (1.0 + rr * (1.0 - sg)))).astype(BF16)
                dgn_acc[h] = dgn_acc[h] + jnp.sum(don * ohat, axis=0, keepdims=True)
                dn = don * gnh
                do.append((rstd * (dn - ohat * jnp.mean(dn * ohat, axis=-1, keepdims=True))).astype(BF16))
            dq = [_dot(do[h], s_b[h]) for h in heads]
            g_t = [_dot(do[h], qs[h], _TN) + carry[h] for h in heads]
            g_b = [g_t[h].astype(BF16) for h in heads]
            dv = [_dot(kdec[rows, kcs[h]], g_b[h], _NT) for h in heads]
            dkdec = [_dot(v_ref[rows, vcs[h]], g_b[h]) for h in heads]
            for h in heads:
                s_prev = st_ref[c - 1, h] if c > 0 else sp_ref[0, h] * has_prev
                ddec = jnp.sum(g_t[h] * s_prev, axis=0, keepdims=True)
                carry[h] = g_t[h] * dec[:, kcs[h]]
                dp_ref[rows, OFF_Q + h * GLA_DK:OFF_Q + (h + 1) * GLA_DK] = (dq[h] * (GLA_DK ** -0.5)).astype(BF16)
                dp_ref[rows, OFF_V + h * GLA_DV:OFF_V + (h + 1) * GLA_DV] = dv[h].astype(BF16)
                dp_ref[rows, OFF_K + h * GLA_DK:OFF_K + (h + 1) * GLA_DK] = (dkdec[h] * e[rows, kcs[h]]).astype(BF16)
                dd_scr[rows, kcs[h]] = dkdec[h] * kdec_f[rows, kcs[h]]
                dt_scr[rows, kcs[h]] = jnp.broadcast_to(ddec * dec[:, kcs[h]], (CHUNK, GLA_DK))
        for h in heads:
            g_scr[h] = carry[h]
            dgn_ref[:, vcs[h]] += dgn_acc[h]
        dla = _dot_exact_lhs(earlier_ref[...], dd_scr[...]) + dt_scr[...]
        dlogit = dla * (1.0 / GLA_TAU) * _sigmoid(-logit)
        dbg_ref[...] += jnp.sum(dlogit, axis=0, keepdims=True)
        dwgu_ref[...] += _dot_bf16(al_ref[...], dlogit, _TN)
        dal_ref[...] = _dot_bf16(dlogit, wgu_ref[...], _NT).astype(BF16)

    rev = lambda i: nt - 1 - i
    blk = lambda w, j: pl.BlockSpec((tT, w), lambda i: (rev(i), j))
    st_blk = pl.BlockSpec((nc, GLA_HEADS, GLA_DV, GLA_DK), lambda i: (rev(i), 0, 0, 0))
    sp_blk = pl.BlockSpec((1, GLA_HEADS, GLA_DV, GLA_DK), lambda i: (jnp.maximum(rev(i) * nc - 1, 0), 0, 0, 0))
    return _call(
        body, name="gla_bwd", grid=(nt,),
        in_specs=[blk(512, 0), blk(512, 1), blk(1024, 1), blk(1024, 2), blk(LANES, 0)] + [_whole()] * 5
        + [st_blk, sp_blk, blk(GLA_V, 0)],
        out_specs=[blk(W_GLA, 0), blk(LANES, 0), pl.BlockSpec((1, GLA_V), lambda i: (0, 0)),
                   pl.BlockSpec((1, GLA_QK), lambda i: (0, 0)), pl.BlockSpec((LANES, GLA_QK), lambda i: (0, 0))],
        out_shape=[_sds((T, W_GLA), BF16), _sds((T, LANES), BF16), _sds((1, GLA_V), F32), _sds((1, GLA_QK), F32),
                   _sds((LANES, GLA_QK), F32)],
        scratch_shapes=[pltpu.VMEM((GLA_HEADS, GLA_DV, GLA_DK), F32), pltpu.VMEM((tT, GLA_QK), F32),
                        pltpu.VMEM((tT, GLA_QK), F32)],
        args=(proj, proj, proj, proj, alow, wgu, b_gate, gn, _chunk_masks(tT, upper=True), _chunk_masks(tT, upper=False),
              states, states, dy_gla), job=job)


def _inproj_bwd(x, dx1, g1, w_all, dparts, job=None):
    T = x.shape[0]
    tT = _row_tile(T, 512)
    offs = (0, W_GLA, W_GLA + W_SGU, N_MAIN)

    def body(x_ref, dx1_ref, g_ref, w_ref, *rest):
        part_refs, (dx_ref, dg_ref) = rest[:len(offs)], rest[len(offs):]

        @pl.when(pl.program_id(0) == 0)
        def _():
            dg_ref[...] = jnp.zeros_like(dg_ref)

        da = jnp.zeros((tT, D_MODEL), F32)
        for off, p_ref in zip(offs, part_refs):
            da = da + _dot(p_ref[...], w_ref[:, off:off + p_ref.shape[1]], _NT)
        xv = x_ref[...]
        dx, dg = _rms_bwd(da, xv, _rms_stats(xv), g_ref[...])
        dg_ref[...] += jnp.sum(dg, axis=0, keepdims=True)
        dx_ref[...] = dx1_ref[...] + dx

    row = lambda w: pl.BlockSpec((tT, w), lambda i: (i, 0))
    vec = pl.BlockSpec((1, D_MODEL), lambda i: (0, 0))
    return _call(
        body, name="inproj_bwd", grid=(T // tT,),
        in_specs=[row(D_MODEL), row(D_MODEL), vec, _whole()] + [row(p.shape[1]) for p in dparts],
        out_specs=[row(D_MODEL), vec], out_shape=[_sds((T, D_MODEL), F32), _sds((1, D_MODEL), F32)],
        args=(x, dx1, g1, w_all, *dparts), job=job)


def _tn_matmul(a, b, name, job=None):
    T, M = a.shape
    N = b.shape[1]
    tk = _row_tile(T, 1024)
    tm = M if M <= 1024 else 1408
    tn = N if N <= 3072 else N // 2
    assert M % tm == 0 and N % tn == 0

    def body(a_ref, b_ref, o_ref):
        @pl.when(pl.program_id(2) == 0)
        def _():
            o_ref[...] = _dot(a_ref[...], b_ref[...], _TN)

        @pl.when(pl.program_id(2) > 0)
        def _():
            o_ref[...] += _dot(a_ref[...], b_ref[...], _TN)

    res, jres = _call(
        body, name=name, grid=(M // tm, N // tn, T // tk),
        in_specs=[pl.BlockSpec((tk, tm), lambda i, j, k: (k, i)), pl.BlockSpec((tk, tn), lambda i, j, k: (k, j))],
        out_specs=[pl.BlockSpec((tm, tn), lambda i, j, k: (i, j))], out_shape=[_sds((M, N), F32)], args=(a, b), job=job)
    return res[0], jres


def _pad_rows(a, rows=8):
    return jnp.pad(a, ((0, rows - a.shape[0]), (0, LANES - a.shape[1])))


def _halves_view(dw):
    r = dw.shape[0] // N_CHIPS
    return dw.reshape(N_CHIPS, 2, r // 2, dw.shape[1])


def kernel(x, norm_pre_mix, w_in, w_gate_up, b_gate, gla_norm, sgu_ln_g, sgu_ln_b, w_spatial, b_spatial, w_branch_gla, w_branch_sgu, w_out, norm_post_mix, norm_pre_ffn, w_ffn_in, w_ffn_out, norm_post_ffn, loss_target, m_norm_pre_mix, m_w_in, m_w_gate_up, m_b_gate, m_gla_norm, m_sgu_ln_g, m_sgu_ln_b, m_w_spatial, m_b_spatial, m_w_branch_gla, m_w_branch_sgu, m_w_out, m_norm_post_mix, m_norm_pre_ffn, m_w_ffn_in, m_w_ffn_out, m_norm_post_ffn, v_norm_pre_mix, v_w_in, v_w_gate_up, v_b_gate, v_gla_norm, v_sgu_ln_g, v_sgu_ln_b, v_w_spatial, v_b_spatial, v_w_branch_gla, v_w_branch_sgu, v_w_out, v_norm_post_mix, v_norm_pre_ffn, v_w_ffn_in, v_w_ffn_out, v_norm_post_ffn):
    chip = 2 * lax.axis_index("x") + lax.axis_index("y")
    xt, tgt = x[0], loss_target[0]

    tiny = jnp.concatenate([w_gate_up[0], _pad_rows(gla_norm[0]), _pad_rows(sgu_ln_g[0]), _pad_rows(sgu_ln_b[0]),
                            jnp.zeros((24, LANES), F32)], axis=0)

    def with_own(gathered, own):
        return lax.dynamic_update_slice(gathered, own[None], (chip, 0, 0))

    w_in_t, m_in_t, v_in_t = w_in[0].T, m_w_in[0].T, v_w_in[0].T
    w_in_b = _transposed_cast(w_in_t)
    g_in, g_tiny = _run_job(_job_gather([w_in_b, tiny]), "gather_w_in")
    g_tiny = with_own(g_tiny, tiny)
    w_all = _relayout_w_in(with_own(g_in, w_in_b))
    cols = lambda a: a.transpose(1, 0, 2).reshape(a.shape[1], N_CHIPS * a.shape[2])
    wgu = jnp.pad(cols(g_tiny[:, 0:16]), ((0, LANES - GLA_RANK), (0, 0)))
    gn = cols(g_tiny[:, 16:20, :64]).reshape(1, GLA_V)
    ln_g = cols(g_tiny[:, 24:28, :64]).reshape(1, 1024)
    ln_b = cols(g_tiny[:, 32:36, :64]).reshape(1, 1024)
    b_sp_t = jnp.pad(b_spatial[0].T, ((0, 0), (0, LANES - SGU_GROUPS)))
    w_sp = w_spatial[0]

    own_rows = [w_branch_gla[0].astype(BF16), w_branch_sgu[0].astype(BF16), w_out[0].astype(BF16), w_ffn_out[0].astype(BF16)]
    (a, proj, alow), g_rows = _inproj_fwd(xt, norm_pre_mix, w_all, job=_job_gather(own_rows))
    rows = lambda g: g.reshape(N_CHIPS * g.shape[1], g.shape[2])
    w_bg, w_bs, w_o, w_fo = [rows(with_own(g, own)) for g, own in zip(g_rows, own_rows)]
    w_fi_b = w_ffn_in[0].astype(BF16)
    fi_top, fi_bot = w_fi_b[:D_MODEL // 2], w_fi_b[D_MODEL // 2:]
    (y_gla, states), (g_top,) = _gla_fwd(proj, alow, wgu, b_gate, gn, job=_job_gather([fi_top]))
    y_sgu = _sgu_fwd(proj, ln_g, ln_b, w_sp, b_sp_t)
    (zg, zs, merged, mix, x1), (g_bot,) = _merge_fwd(xt, proj, y_gla, y_sgu, w_bg, w_bs, w_o, norm_post_mix,
                                                     job=_job_gather([fi_bot]))
    h, f, dgu, dy, dx1, loss, d_gpf, d_gpo = _ffn_fwd_bwd(x1, tgt, with_own(g_top, fi_top), with_own(g_bot, fi_bot),
                                                          w_fo, norm_pre_ffn, norm_post_ffn)

    own_part = lambda c: lax.dynamic_index_in_dim(c, chip, 0, keepdims=False)
    whole = lambda hs: [[(h_, None)] for h_ in hs]
    dw_fo, _ = _tn_matmul(f, dy, "dw_ffn_out")
    dw_fo4 = _halves_view(dw_fo)
    dw_fi, (q_fo,) = _tn_matmul(h, dgu, "dw_ffn_in", job=_job_to_other_core([[(dw_fo4, 0)]]))
    c_fo = _presum(dw_fo4, q_fo, "presum_ffn_out")
    (dmix, dzg, dzs, dp_mrg, dyg, dys, d_gpm), (s_fo, q_fi) = _merge_bwd(
        dx1, mix, proj, zg, zs, w_bg, w_bs, w_o, norm_post_mix,
        job=_join(_job_scatter([c_fo]), _job_to_other_core([[(dw_fi, 0)]])))
    c_fi = _presum(dw_fi, q_fi, "presum_ffn_in")
    dw_c, _ = _tn_matmul(a, dp_mrg, "dw_in_merge")
    dw_o4 = _halves_view(_tn_matmul(merged, dmix, "dw_out")[0])
    dw_bg4 = _halves_view(_tn_matmul(y_gla, dzg, "dw_branch_gla")[0])
    dw_bs4 = _halves_view(_tn_matmul(y_sgu, dzs, "dw_branch_sgu")[0])
    (dp_sgu, d_wsp, d_bsp_t, d_lng, d_lnb), (q_o, q_bg, q_bs, q_c) = _sgu_bwd(
        proj, dys, ln_g, ln_b, w_sp, b_sp_t,
        job=_job_to_other_core([[(dw_o4, 0)], [(dw_bg4, 0)], [(dw_bs4, 0)], [(dw_c, 0)]]))
    c_o, c_bg, c_bs = (_presum(dw_o4, q_o, "presum_out"), _presum(dw_bg4, q_bg, "presum_branch_gla"),
                       _presum(dw_bs4, q_bs, "presum_branch_sgu"))
    h_fo = _sum_slots(own_part(c_fo), s_fo, "sum_ffn_out")
    dw_b, _ = _tn_matmul(a, dp_sgu, "dw_in_sgu")
    (dp_gla, dal, d_gn, d_bg, d_wgu), (s_fi, t_fo, q_b) = _gla_bwd(
        proj, alow, wgu, b_gate, gn, states, dyg,
        job=_join(_job_scatter([c_fi]), _job_to_other_core(whole([h_fo]) + [[(dw_b, 0)]])))
    h_fi = _sum_slots(own_part(c_fi), s_fi, "sum_ffn_in")
    dw_d, _ = _tn_matmul(a, dal, "dw_in_gate")
    dw_a, (s_o, s_bg, s_bs, t_fi, q_d) = _tn_matmul(
        a, dp_gla, "dw_in_gla",
        job=_join(_job_scatter([c_o, c_bg, c_bs]), _job_to_other_core(whole([h_fi]) + [[(dw_d, 0)]])))
    h_o, h_bg, h_bs = (_sum_slots(own_part(c_o), s_o, "sum_out"), _sum_slots(own_part(c_bg), s_bg, "sum_branch_gla"),
                       _sum_slots(own_part(c_bs), s_bs, "sum_branch_sgu"))

    grads, deltas, new_m, new_v = {}, {}, {}, {}

    def update(name, w, m, v, g_mine, g_theirs, job=None):
        (g, d, m2, v2), jres = _adamw(w[0], m[0], v[0], g_mine, g_theirs, "adamw_" + name, job=job)
        grads[name], deltas[name], new_m[name], new_v[name] = g[None], d[None], m2[None], v2[None]
        return jres

    dw_in = [(dw_a, 0), (dw_b, W_GLA), (dw_c, W_GLA + W_SGU), (dw_d, N_MAIN)]
    q_a, t_o, t_bg, t_bs = update("w_ffn_out", w_ffn_out, m_w_ffn_out, v_w_ffn_out, [h_fo], [t_fo],
                                  job=_job_to_other_core([[(dw_a, 0)]] + whole([h_o, h_bg, h_bs])))
    q_in = [q_a, q_b, q_c, q_d]
    hr_in = D_MODEL // 2
    c_in_a, _ = _presum_w_in(dw_in, q_in, 0, hr_in // 8, "presum_w_in_a")
    c_in_b, (s_in_a,) = _presum_w_in(dw_in, q_in, hr_in // 8, 7 * hr_in // 8, "presum_w_in_b",
                                     job=_job_scatter([c_in_a]))
    update("w_ffn_in", w_ffn_in, m_w_ffn_in, v_w_ffn_in, [h_fi], [t_fi])
    update("w_out", w_out, m_w_out, v_w_out, [h_o], [t_o])
    update("w_branch_gla", w_branch_gla, m_w_branch_gla, v_w_branch_gla, [h_bg], [t_bg])
    update("w_branch_sgu", w_branch_sgu, m_w_branch_sgu, v_w_branch_sgu, [h_bs], [t_bs])
    (grad_x, d_g1), (s_in_b,) = _inproj_bwd(xt, dx1, norm_pre_mix, w_all, (dp_gla, dp_sgu, dp_mrg, dal),
                                            job=_job_scatter([c_in_b]))
    h_in = [_sum_slots(own_part(c_in_a), s_in_a, "sum_w_in_a"), _sum_slots(own_part(c_in_b), s_in_b, "sum_w_in_b")]
    t_in = _run_job(_job_to_other_core(whole(h_in)), "swap_w_in")
    for store, val in zip((grads, deltas, new_m, new_v),
                          _adamw_transposed(w_in_t, m_in_t, v_in_t, h_in, t_in, "adamw_w_in")):
        store["w_in"] = val.T[None]

    small_names = ["w_spatial", "w_gate_up", "norm_pre_mix", "norm_post_mix", "norm_pre_ffn", "norm_post_ffn", "b_gate",
                   "b_spatial", "gla_norm", "sgu_ln_g", "sgu_ln_b"]
    loss_out, small = _small_adamw(
        _small_sum([d_wsp, d_wgu, d_g1, d_gpm, d_gpf, d_gpo, d_bg, d_bsp_t, d_gn, d_lng, d_lnb, loss]),
        [w_spatial, w_gate_up, norm_pre_mix, norm_post_mix, norm_pre_ffn, norm_post_ffn, b_gate, b_spatial, gla_norm,
         sgu_ln_g, sgu_ln_b],
        [m_w_spatial, m_w_gate_up, m_norm_pre_mix, m_norm_post_mix, m_norm_pre_ffn, m_norm_post_ffn, m_b_gate,
         m_b_spatial, m_gla_norm, m_sgu_ln_g, m_sgu_ln_b],
        [v_w_spatial, v_w_gate_up, v_norm_pre_mix, v_norm_post_mix, v_norm_pre_ffn, v_norm_post_ffn, v_b_gate,
         v_b_spatial, v_gla_norm, v_sgu_ln_g, v_sgu_ln_b])
    for store, vals in zip((grads, deltas, new_m, new_v), small):
        store.update(zip(small_names, vals))

    order = ["norm_pre_mix", "w_in", "w_gate_up", "b_gate", "gla_norm", "sgu_ln_g", "sgu_ln_b", "w_spatial", "b_spatial",
             "w_branch_gla", "w_branch_sgu", "w_out", "norm_post_mix", "norm_pre_ffn", "w_ffn_in", "w_ffn_out",
             "norm_post_ffn"]
    out = [loss_out, grad_x[None]]
    for store in (grads, deltas, new_m, new_v):
        out.extend(store[n] for n in order)
    return tuple(out)
```

```python
import jax
import jax.numpy as jnp
from jax import lax
from jax.experimental import pallas as pl
from jax.experimental.pallas import tpu as pltpu

F32 = jnp.float32
BF16 = jnp.bfloat16

D_MODEL = 1024
GLA_HEADS = 4
GLA_DK = 128
GLA_DV = 256
GLA_QK = GLA_HEADS * GLA_DK
GLA_V = GLA_HEADS * GLA_DV
GLA_RANK = 16
GLA_TAU = 16.0
CHUNK = 64
SGU_GROUPS = 4
SGU_BLOCK = 128
SGU_DG = 256
D_FF = 2816
EPS = 1e-6
LANES = 128

OFF_Q, OFF_K, OFF_V, OFF_R, OFF_SU, OFF_SV, OFF_GG, OFF_GS, OFF_AL = 0, 512, 1024, 2048, 3072, 4096, 5120, 6144, 7168
W_GLA, W_SGU, W_MRG = 3072, 2048, 2048
N_MAIN = 7168
N_ALL = N_MAIN + LANES
_IN_SPLITS = (GLA_QK, GLA_QK, GLA_V, GLA_V, GLA_RANK, 1024, 1024, 1024, 1024)
_IN_STARTS = tuple(sum(_IN_SPLITS[:i]) for i in range(len(_IN_SPLITS) + 1))
_IN_DST = (OFF_Q, OFF_K, OFF_V, OFF_R, OFF_AL, OFF_SU, OFF_SV, OFF_GG, OFF_GS)
D_IN = _IN_STARTS[-1]

ADAM_LR = 0.001
ADAM_B1 = 0.9
ADAM_B2 = 0.999
ADAM_EPS = 1e-08
ADAM_WD = 0.01
ADAM_STEP = 10

VMEM_LIMIT_BYTES = 56 * 1024 * 1024
N_CHIPS = 4
N_PEER = N_CHIPS - 1
N_DEV = 8
MESH = pl.DeviceIdType.MESH

_NN = (((1,), (0,)), ((), ()))
_NT = (((1,), (1,)), ((), ()))
_TN = (((0,), (0,)), ((), ()))


def _dot(a, b, dims=_NN):
    return lax.dot_general(a, b, dims, preferred_element_type=F32)


def _split(x):
    hi = x.astype(BF16)
    lo = (x - hi.astype(F32)).astype(BF16)
    return hi, lo


def _dot_bf16(a, b, dims=_NN):
    return _dot(a.astype(BF16), b.astype(BF16), dims)


def _dot_exact_lhs(m, x):
    xh, xl = _split(x)
    return _dot(m, xh) + _dot(m, xl)


def _sigmoid(x):
    return 0.5 * jnp.tanh(0.5 * x) + 0.5


def _log_sigmoid(x):
    return jnp.minimum(x, 0.0) - jnp.log(1.0 + jnp.exp(-jnp.abs(x)))


_GELU_C = 0.7978845608028654
_GELU_A = 0.044715


def _gelu_and_grad(x):
    x2 = x * x
    t = jnp.tanh(_GELU_C * (x + _GELU_A * x * x2))
    g = 0.5 * x * (1.0 + t)
    dg = 0.5 * (1.0 + t) + 0.5 * x * (1.0 - t * t) * (_GELU_C * (1.0 + 3.0 * _GELU_A * x2))
    return g, dg


def _gelu(x):
    t = jnp.tanh(_GELU_C * (x + _GELU_A * x * x * x))
    return 0.5 * x * (1.0 + t)


def _rms_stats(x):
    return lax.rsqrt(jnp.mean(x * x, axis=-1, keepdims=True) + EPS)


def _rms_bwd(dout, y, r, g):
    yhat = y * r
    dn = dout * g
    dy = r * (dn - yhat * jnp.mean(dn * yhat, axis=-1, keepdims=True))
    return dy, dout * yhat


def _whole():
    return pl.BlockSpec(memory_space=pltpu.VMEM)


def _row_tile(T, want):
    t = min(T, want)
    assert T % t == 0
    return t


def _chunk_masks(tT, upper):
    row = lax.broadcasted_iota(jnp.int32, (tT, tT), 0)
    col = lax.broadcasted_iota(jnp.int32, (tT, tT), 1)
    same = (row // CHUNK) == (col // CHUNK)
    tri = (col > row) if upper else (col < row)
    return jnp.where(same & tri, 1.0, 0.0).astype(BF16)


class _Job:
    def __init__(self, ins, out_shapes, scratch, start, finish, mid=None):
        self.ins, self.out_shapes, self.scratch = list(ins), list(out_shapes), list(scratch)
        self.start, self.finish, self.mid = start, finish, mid


def _join(*jobs):
    def split(refs, counts):
        out, at = [], 0
        for n in counts:
            out.append(refs[at:at + n])
            at += n
        return out

    ni, no, ns = [len(j.ins) for j in jobs], [len(j.out_shapes) for j in jobs], [len(j.scratch) for j in jobs]

    def start(ins, outs, scr):
        for j, a, b, c in zip(jobs, split(ins, ni), split(outs, no), split(scr, ns)):
            j.start(a, b, c)

    def finish(ins, outs, scr):
        for j, a, b, c in zip(jobs, split(ins, ni), split(outs, no), split(scr, ns)):
            j.finish(a, b, c)

    def mid(ins, outs, scr):
        for j, a, b, c in zip(jobs, split(ins, ni), split(outs, no), split(scr, ns)):
            if j.mid is not None:
                j.mid(a, b, c)

    return _Job(sum((j.ins for j in jobs), []), sum((j.out_shapes for j in jobs), []),
                sum((j.scratch for j in jobs), []), start, finish, mid if any(j.mid for j in jobs) else None)


def _mesh_pos():
    return lax.axis_index("x"), lax.axis_index("y"), lax.axis_index("c")


def _peer_chips(xi, yi):
    return [(1 - xi, yi), (xi, 1 - yi), (1 - xi, 1 - yi)]


def _half(ci, rows):
    return pl.ds(pl.multiple_of(ci * rows, 8), rows)


def _sds(shape, dtype):
    return jax.ShapeDtypeStruct(tuple(shape), dtype)


def _job_gather(arrs):
    n = len(arrs)
    kinds = 12
    Y0, Y1, X1, X0, ON_X, ON_Y, D2D = 0, 1, 2, 3, 4, 5, 6

    def copies(ins, outs, scr):
        send_sems, recv_sems = scr
        xi, yi, ci = _mesh_pos()
        me, cx, cy, cd = 2 * xi + yi, 2 * (1 - xi) + yi, 2 * xi + (1 - yi), 2 * (1 - xi) + (1 - yi)
        to_x, to_y, to_core = (1 - xi, yi, ci), (xi, 1 - yi, ci), (xi, yi, 1 - ci)
        table = []
        for k in range(n):
            qr = arrs[k].shape[0] // 4

            def rows(core, q):
                return pl.ds(pl.multiple_of((2 * core + q) * qr, 8), qr)

            def cp(kind, src, dst, to):
                s = k * kinds + kind
                return pltpu.make_async_remote_copy(src_ref=src, dst_ref=dst, send_sem=send_sems.at[s],
                                                    recv_sem=recv_sems.at[s], device_id=to, device_id_type=MESH)

            def slab(chip, core, q):
                return outs[k].at[chip, rows(core, q)]

            t = {}
            for kind, q, to, frm in ((Y0, 0, to_y, cy), (Y1, 1, to_y, cy), (X1, 1, to_x, cx), (X0, 0, to_x, cx)):
                mine = ins[k].at[rows(ci, q)]
                t[kind] = (cp(kind, mine, slab(me, ci, q), to), cp(kind, mine, slab(frm, ci, q), to))
            t[ON_X] = (cp(ON_X, slab(cy, ci, 0), slab(cy, ci, 0), to_x), cp(ON_X, slab(cy, ci, 0), slab(cd, ci, 0), to_x))
            t[ON_Y] = (cp(ON_Y, slab(cx, ci, 1), slab(cx, ci, 1), to_y), cp(ON_Y, slab(cx, ci, 1), slab(cd, ci, 1), to_y))
            for i, (chip, q) in enumerate(((cy, 0), (cy, 1), (cx, 1), (cx, 0), (cd, 0), (cd, 1))):
                t[D2D + i] = (cp(D2D + i, slab(chip, ci, q), slab(chip, ci, q), to_core),
                              cp(D2D + i, slab(chip, ci, q), slab(chip, 1 - ci, q), to_core))
            table.append(t)
        return table

    def start(ins, outs, scr):
        table = copies(ins, outs, scr)
        for kind in (Y0, X1, Y1, X0):
            for t in table:
                t[kind][0].start()

    def arrived(table, kind, then):
        for t in table:
            t[kind][1].wait_recv()
            for nxt in then:
                t[nxt][0].start()

    def mid(ins, outs, scr):
        table = copies(ins, outs, scr)
        arrived(table, Y0, (ON_X, D2D + 0))
        arrived(table, X1, (ON_Y, D2D + 2))

    def finish(ins, outs, scr):
        table = copies(ins, outs, scr)
        arrived(table, Y1, (D2D + 1,))
        arrived(table, X0, (D2D + 3,))
        arrived(table, ON_X, (D2D + 4,))
        arrived(table, ON_Y, (D2D + 5,))
        for t in table:
            for i in range(6):
                t[D2D + i][1].wait_recv()
            for kind in range(kinds):
                t[kind][0].wait_send()

    dma = pltpu.SemaphoreType.DMA
    return _Job(arrs, [_sds((N_CHIPS,) + a.shape, a.dtype) for a in arrs], [dma((n * kinds,))] * 2, start, finish, mid)


def _job_scatter(parts):
    n = len(parts)

    def copies(ins, outs, scr):
        send_sems, recv_sems = scr
        xi, yi, ci = _mesh_pos()
        res = []
        for k in range(n):
            for j, (px, py) in enumerate(_peer_chips(xi, yi)):
                s = k * N_PEER + j
                res.append(pltpu.make_async_remote_copy(
                    src_ref=ins[k].at[2 * px + py], dst_ref=outs[k].at[j], send_sem=send_sems.at[s],
                    recv_sem=recv_sems.at[s], device_id=(px, py, ci), device_id_type=MESH))
        return res

    def start(ins, outs, scr):
        for cp in copies(ins, outs, scr):
            cp.start()

    def finish(ins, outs, scr):
        for cp in copies(ins, outs, scr):
            cp.wait_recv()
            cp.wait_send()

    dma = pltpu.SemaphoreType.DMA
    return _Job(parts, [_sds((N_PEER,) + p.shape[1:], p.dtype) for p in parts], [dma((n * N_PEER,))] * 2, start, finish)


def _job_to_other_core(groups):
    pieces = [(g, a, off) for g, group in enumerate(groups) for a, off in group]
    n = len(pieces)

    def geometry(group):
        a0, off0 = group[0]
        if off0 is None:
            return a0.shape
        if a0.ndim == 4:
            return (N_CHIPS, a0.shape[2], a0.shape[3])
        return (a0.shape[0] // 2, sum(a.shape[1] for a, _ in group))

    def copies(ins, outs, scr):
        send_sems, recv_sems = scr
        xi, yi, ci = _mesh_pos()
        res = []
        for p, (g, a, off) in enumerate(pieces):
            if off is None:
                give, land = ins[p], outs[g]
            elif a.ndim == 4:
                give, land = ins[p].at[pl.ds(0, N_CHIPS), 1 - ci], outs[g]
            else:
                hr, w = a.shape[0] // 2, a.shape[1]
                give, land = ins[p].at[_half(1 - ci, hr)], outs[g].at[pl.ds(0, hr), pl.ds(off, w)]
            res.append(pltpu.make_async_remote_copy(
                src_ref=give, dst_ref=land, send_sem=send_sems.at[p], recv_sem=recv_sems.at[p],
                device_id=(xi, yi, 1 - ci), device_id_type=MESH))
        return res

    def start(ins, outs, scr):
        for cp in copies(ins, outs, scr):
            cp.start()

    def finish(ins, outs, scr):
        for cp in copies(ins, outs, scr):
            cp.wait_recv()
            cp.wait_send()

    dma = pltpu.SemaphoreType.DMA
    return _Job([a for _, a, _ in pieces], [_sds(geometry(group), group[0][0].dtype) for group in groups],
                [dma((n,))] * 2, start, finish)


def _call(body, *, name, grid, in_specs, out_specs, out_shape, args, scratch_shapes=(), parallel=False, job=None,
          by_core=False):
    n_in, n_out, n_scr = len(in_specs), len(out_specs), len(scratch_shapes)
    hbm = pl.BlockSpec(memory_space=pl.ANY)
    n_ji, n_jo = (len(job.ins), len(job.out_shapes)) if job is not None else (0, 0)
    lead = 1 if by_core else 0

    def kernel_fn(*refs):
        core, refs = refs[:lead], refs[lead:]
        ins, refs = refs[:n_in], refs[n_in:]
        j_ins, refs = refs[:n_ji], refs[n_ji:]
        outs, refs = refs[:n_out], refs[n_out:]
        j_outs, refs = refs[:n_jo], refs[n_jo:]
        scr, j_scr = refs[:n_scr], refs[n_scr:]
        if job is None:
            body(*core, *ins, *outs, *scr)
            return
        ids = [pl.program_id(d) for d in range(len(grid))]
        first = ids[0] == 0
        last = ids[0] == grid[0] - 1
        for d in range(1, len(grid)):
            first = first & (ids[d] == 0)
            last = last & (ids[d] == grid[d] - 1)

        @pl.when(first)
        def _():
            job.start(j_ins, j_outs, j_scr)

        if job.mid is not None and grid[0] >= 4:
            half_way = ids[0] == grid[0] // 2
            for d in range(1, len(grid)):
                half_way = half_way & (ids[d] == 0)

            @pl.when(half_way)
            def _():
                job.mid(j_ins, j_outs, j_scr)

        body(*core, *ins, *outs, *scr)

        @pl.when(last)
        def _():
            if job.mid is not None and grid[0] < 4:
                job.mid(j_ins, j_outs, j_scr)
            job.finish(j_ins, j_outs, j_scr)

    sem = ("parallel" if parallel and job is None else "arbitrary",) * len(grid)
    all_in = list(in_specs) + [hbm] * n_ji
    all_out = list(out_specs) + [hbm] * n_jo
    all_scratch = list(scratch_shapes) + (job.scratch if job is not None else [])
    all_shapes = list(out_shape) + (job.out_shapes if job is not None else [])
    all_args = list(args) + (job.ins if job is not None else [])
    params = pltpu.CompilerParams(dimension_semantics=sem, vmem_limit_bytes=VMEM_LIMIT_BYTES)
    if by_core:
        spec = pltpu.PrefetchScalarGridSpec(num_scalar_prefetch=1, grid=grid, in_specs=all_in, out_specs=all_out,
                                            scratch_shapes=all_scratch)
        core = lax.axis_index("c").astype(jnp.int32).reshape(1)
        res = pl.pallas_call(kernel_fn, name=name, grid_spec=spec, out_shape=all_shapes, compiler_params=params)(
            core, *all_args)
    else:
        res = pl.pallas_call(kernel_fn, name=name, grid=grid, in_specs=all_in, out_specs=all_out, out_shape=all_shapes,
                             scratch_shapes=all_scratch, compiler_params=params)(*all_args)
    return list(res[:n_out]), list(res[n_out:])


def _run_job(job, name):
    n_i, n_o = len(job.ins), len(job.out_shapes)

    def body(*refs):
        ins, outs, scr = refs[:n_i], refs[n_i:n_i + n_o], refs[n_i + n_o:]
        job.start(ins, outs, scr)
        if job.mid is not None:
            job.mid(ins, outs, scr)
        job.finish(ins, outs, scr)

    hbm = pl.BlockSpec(memory_space=pl.ANY)
    return list(pl.pallas_call(body, name=name, in_specs=[hbm] * n_i, out_specs=[hbm] * n_o, out_shape=job.out_shapes,
                               scratch_shapes=job.scratch)(*job.ins))


def _adam_values(w, m, v, g):
    m2 = ADAM_B1 * m + (1.0 - ADAM_B1) * g
    v2 = ADAM_B2 * v + (1.0 - ADAM_B2) * (g * g)
    delta = -ADAM_LR * ((m2 / (1.0 - ADAM_B1 ** ADAM_STEP)) / (jnp.sqrt(v2 / (1.0 - ADAM_B2 ** ADAM_STEP)) + ADAM_EPS)
                        + ADAM_WD * w)
    return delta, m2, v2


_P_WSP, _P_WGU, _P_NORM, _P_BG, _P_BSP, _P_HEAD, _P_LOSS, _P_ROWS = 0, 512, 576, 608, 616, 624, 720, 728


def _small_sum(dgrads):
    def body(dwsp, dwgu, dg1, dgpm, dgpf, dgpo, dbg, dbspt, dgn, dlng, dlnb, loss_in, tot, pack, slots, send_sems,
             recv_sems):
        xi, yi, ci = _mesh_pos()
        chip = 2 * xi + yi

        pack[...] = jnp.zeros_like(pack)
        for g in range(SGU_GROUPS):
            pack[_P_WSP + g * SGU_BLOCK:_P_WSP + (g + 1) * SGU_BLOCK] = dwsp[g]
        for j in range(N_CHIPS):
            pack[_P_WGU + GLA_RANK * j:_P_WGU + GLA_RANK * (j + 1)] = dwgu[0:GLA_RANK, LANES * j:LANES * (j + 1)]
        for k, r in enumerate((dg1, dgpm, dgpf, dgpo)):
            for q in range(8):
                pack[_P_NORM + 8 * k + q:_P_NORM + 8 * k + q + 1] = r[:, LANES * q:LANES * (q + 1)]
        for q in range(4):
            pack[_P_BG + q:_P_BG + q + 1] = dbg[:, LANES * q:LANES * (q + 1)]
        pack[_P_BSP:_P_BSP + SGU_GROUPS] = jnp.transpose(dbspt[...])[0:SGU_GROUPS]
        for k, r in enumerate((dgn, dlng, dlnb)):
            for j in range(N_CHIPS):
                for hh in range(4):
                    row = _P_HEAD + 32 * k + 8 * j + hh
                    pack[row:row + 1, 0:64] = r[:, 256 * hh + 64 * j:256 * hh + 64 * (j + 1)]
        pack[_P_LOSS:_P_LOSS + 1] = loss_in[...]

        to_sibling = pltpu.make_async_remote_copy(
            src_ref=pack, dst_ref=tot, send_sem=send_sems.at[N_PEER], recv_sem=recv_sems.at[N_PEER],
            device_id=(xi, yi, 1 - ci), device_id_type=MESH)
        to_sibling.start()
        to_sibling.wait_recv()
        to_sibling.wait_send()
        pack[...] = pack[...] + tot[...]
        slots[chip] = pack[...]

        def copy(j, slot):
            px, py = _peer_chips(xi, yi)[j]
            return pltpu.make_async_remote_copy(
                src_ref=pack, dst_ref=slots.at[slot(2 * px + py)], send_sem=send_sems.at[j], recv_sem=recv_sems.at[j],
                device_id=(px, py, ci), device_id_type=MESH)

        sends = [copy(j, lambda peer_chip: chip) for j in range(N_PEER)]
        for cp in sends:
            cp.start()
        for j in range(N_PEER):
            copy(j, lambda peer_chip: peer_chip).wait_recv()
        for cp in sends:
            cp.wait_send()
        acc = slots[0]
        for d in range(1, N_CHIPS):
            acc = acc + slots[d]
        tot[...] = acc

    return pl.pallas_call(
        body, name="small_sum", in_specs=[_whole()] * 12, out_specs=_whole(), out_shape=_sds((_P_ROWS, LANES), F32),
        scratch_shapes=[pltpu.VMEM((_P_ROWS, LANES), F32), pltpu.VMEM((N_CHIPS, _P_ROWS, LANES), F32),
                        pltpu.SemaphoreType.DMA((N_PEER + 1,)), pltpu.SemaphoreType.DMA((N_PEER + 1,))],
        compiler_params=pltpu.CompilerParams(vmem_limit_bytes=VMEM_LIMIT_BYTES),
    )(*dgrads)


def _small_adamw(tot, ws, ms, vs):
    n = len(ws)

    def body(*refs):
        tot = refs[0]
        w_refs, m_refs, v_refs = refs[1:1 + n], refs[1 + n:1 + 2 * n], refs[1 + 2 * n:1 + 3 * n]
        loss_out = refs[1 + 3 * n]
        outs = refs[2 + 3 * n:]
        chip = 2 * lax.axis_index("x") + lax.axis_index("y")
        loss_out[...] = tot[_P_LOSS:_P_LOSS + 1, 0:1]

        def step(k, g, pick, put):
            d, m2, v2 = _adam_values(pick(w_refs[k]), pick(m_refs[k]), pick(v_refs[k]), g)
            for o, val in zip((outs[k], outs[n + k], outs[2 * n + k], outs[3 * n + k]), (g, d, m2, v2)):
                put(o, val)

        def whole(ref):
            return ref[0]

        def put_whole(ref, val):
            ref[0] = val

        for g in range(SGU_GROUPS):
            def pick_g(ref, g=g):
                return ref[0, g]

            def put_g(ref, val, g=g):
                ref[0, g] = val

            step(0, tot[_P_WSP + g * SGU_BLOCK:_P_WSP + (g + 1) * SGU_BLOCK], pick_g, put_g)
        step(1, tot[pl.ds(pl.multiple_of(_P_WGU + GLA_RANK * chip, GLA_RANK), GLA_RANK), :], whole, put_whole)
        for k, (base, chunks) in enumerate(((_P_NORM, 8), (_P_NORM + 8, 8), (_P_NORM + 16, 8), (_P_NORM + 24, 8), (_P_BG, 4))):
            for q in range(chunks):
                def pick_q(ref, q=q):
                    return ref[:, LANES * q:LANES * (q + 1)]

                def put_q(ref, val, q=q):
                    ref[:, LANES * q:LANES * (q + 1)] = val

                step(2 + k, tot[base + q:base + q + 1], pick_q, put_q)
        step(7, tot[_P_BSP:_P_BSP + SGU_GROUPS], whole, put_whole)
        for k in range(3):
            mine = tot[pl.ds(pl.multiple_of(_P_HEAD + 32 * k + 8 * chip, 8), 8), :]
            step(8 + k, mine[0:4, 0:64], whole, put_whole)

    shapes = [_sds(w.shape, F32) for w in ws]
    res = pl.pallas_call(
        body, name="small_adamw", in_specs=[_whole()] * (1 + 3 * n), out_specs=[_whole()] * (1 + 4 * n),
        out_shape=[_sds((1, 1), F32)] + shapes * 4,
        compiler_params=pltpu.CompilerParams(vmem_limit_bytes=VMEM_LIMIT_BYTES),
    )(tot, *ws, *ms, *vs)
    return res[0].reshape(()), [list(res[1 + i * n:1 + (i + 1) * n]) for i in range(4)]


def _w_in_pieces():
    blk = D_IN // N_CHIPS
    pieces = []
    for s in range(len(_IN_SPLITS)):
        lo_s, hi_s = _IN_STARTS[s], _IN_STARTS[s + 1]
        for j in range(N_CHIPS):
            lo, hi = max(lo_s, j * blk), min(hi_s, (j + 1) * blk)
            if lo < hi:
                pieces.append((j, lo - j * blk, _IN_DST[s] + lo - lo_s, hi - lo))
    return pieces


def _relayout_w_in(gathered):
    _, rows, blk = gathered.shape
    tr = 256

    def body(g_ref, o_ref):
        o_ref[:, OFF_AL:N_ALL] = jnp.zeros((tr, LANES), BF16)
        for j, src, dst, w in _w_in_pieces():
            o_ref[:, dst:dst + w] = g_ref[j, :, src:src + w]

    res, _ = _call(body, name="relayout_w_in", grid=(rows // tr,), parallel=True,
                   in_specs=[pl.BlockSpec((N_CHIPS, tr, blk), lambda i: (0, i, 0))],
                   out_specs=[pl.BlockSpec((tr, N_ALL), lambda i: (i, 0))],
                   out_shape=[_sds((rows, N_ALL), BF16)], args=(gathered,))
    return res[0]


def _update_row_tile(rows):
    for t in range(min(rows, 256), 7, -8):
        if rows % t == 0:
            return t
    return rows


def _presum_w_in(dws, theirs, row0, rows, name, job=None):
    hr = theirs[0].shape[0]
    blk = D_IN // N_CHIPS
    tr = 64
    assert row0 % tr == 0 and rows % tr == 0
    nh, t0 = hr // tr, row0 // tr
    n = len(dws)

    def body(core_ref, *refs):
        dw_refs, q_refs, (o_ref, s_scr) = refs[:n], refs[n:2 * n], refs[2 * n:]
        for p, (a, off) in enumerate(dws):
            w = a.shape[1]
            s_scr[:, off:off + w] = (dw_refs[p][...] + q_refs[p][...]).astype(BF16)
        for j, src, dst, w in _w_in_pieces():
            o_ref[j, :, src:src + w] = s_scr[:, dst:dst + w]

    in_specs = [pl.BlockSpec((tr, a.shape[1]), lambda i, core: (i + t0 + core[0] * nh, 0)) for a, _ in dws]
    in_specs += [pl.BlockSpec((tr, q.shape[1]), lambda i, core: (i + t0, 0)) for q in theirs]
    res, jres = _call(body, name=name, grid=(rows // tr,), parallel=True, in_specs=in_specs,
                      out_specs=[pl.BlockSpec((N_CHIPS, tr, blk), lambda i, core: (0, i, 0))],
                      out_shape=[_sds((N_CHIPS, rows, blk), BF16)], scratch_shapes=[pltpu.VMEM((tr, N_ALL), BF16)],
                      args=(*[a for a, _ in dws], *theirs), job=job, by_core=True)
    return res[0], jres


def _presum(dw, theirs, name):
    if dw.ndim == 4:
        _, _, hr, c = dw.shape
        tr = _update_row_tile(hr)
        mine = pl.BlockSpec((1, 1, tr, c), lambda j, i, core: (j, core[0], i, 0))
        other = pl.BlockSpec((1, tr, c), lambda j, i, core: (j, i, 0))
    else:
        hr, c = dw.shape[0] // 2, dw.shape[1] // N_CHIPS
        tr = _update_row_tile(hr)
        nh = hr // tr
        mine = pl.BlockSpec((tr, c), lambda j, i, core: (i + core[0] * nh, j))
        other = pl.BlockSpec((tr, c), lambda j, i, core: (i, j))

    def body(core_ref, a_ref, q_ref, o_ref):
        o_ref[...] = (a_ref[...].reshape(tr, c) + q_ref[...].reshape(tr, c)).astype(BF16).reshape(o_ref.shape)

    res, _ = _call(body, name=name, grid=(N_CHIPS, hr // tr), parallel=True, in_specs=[mine, other],
                   out_specs=[pl.BlockSpec((1, tr, c), lambda j, i, core: (j, i, 0))],
                   out_shape=[_sds((N_CHIPS, hr, c), BF16)], args=(dw, theirs), by_core=True)
    return res[0]


def _sum_slots(own, slots, name):
    rows, cols = own.shape
    tr = _update_row_tile(rows)

    def body(own_ref, s_ref, o_ref):
        acc = own_ref[...].astype(F32)
        for j in range(N_PEER):
            acc = acc + s_ref[j].astype(F32)
        o_ref[...] = acc

    res, _ = _call(body, name=name, grid=(rows // tr,), parallel=True,
                   in_specs=[pl.BlockSpec((tr, cols), lambda i: (i, 0)), pl.BlockSpec((N_PEER, tr, cols), lambda i: (0, i, 0))],
                   out_specs=[pl.BlockSpec((tr, cols), lambda i: (i, 0))], out_shape=[_sds((rows, cols), F32)],
                   args=(own, slots))
    return res[0]


def _adamw(w, m, v, g_mine, g_theirs, name, job=None):
    rows, cols = w.shape
    part_rows = [p.shape[0] for p in g_mine]
    assert sum(part_rows) == rows // 2 and [p.shape[0] for p in g_theirs] == part_rows
    tr = _update_row_tile(min(part_rows))
    assert all(r % tr == 0 for r in part_rows)
    nh = (rows // 2) // tr
    starts = [sum(part_rows[:k]) // tr for k in range(len(part_rows))]
    n_parts = len(part_rows)

    def body(core_ref, w_ref, m_ref, v_ref, *rest):
        g_refs, (g_out, d_out, m_out, v_out) = rest[:-4], rest[-4:]
        step = pl.program_id(0)
        mine_here = (step // nh) == core_ref[0]
        q = step % nh
        g = None
        for k in reversed(range(n_parts)):
            val = jnp.where(mine_here, g_refs[k][...], g_refs[n_parts + k][...])
            g = val if g is None else jnp.where(q < starts[k + 1], val, g)
        d, m2, v2 = _adam_values(w_ref[...], m_ref[...], v_ref[...], g)
        g_out[...] = g
        m_out[...] = m2
        v_out[...] = v2
        d_out[...] = d

    def g_spec(k, mine):
        last = part_rows[k] // tr - 1

        def index(i, core):
            half = core[0] if mine else 1 - core[0]
            here = jnp.clip(i % nh - starts[k], 0, last)
            return (jnp.where(i // nh == half, here, jnp.where(i // nh > half, last, 0)), 0)

        return pl.BlockSpec((tr, cols), index)

    spec = pl.BlockSpec((tr, cols), lambda i, core: (i, 0))
    g_specs = [g_spec(k, True) for k in range(n_parts)] + [g_spec(k, False) for k in range(n_parts)]
    return _call(body, name=name, grid=(rows // tr,), parallel=True, in_specs=[spec] * 3 + g_specs,
                 out_specs=[spec] * 4, out_shape=[_sds((rows, cols), F32)] * 4, args=(w, m, v, *g_mine, *g_theirs),
                 job=job, by_core=True)


def _transposed_cast(wt):
    cols, rows = wt.shape

    def body(x_ref, o_ref):
        o_ref[...] = jnp.transpose(x_ref[...]).astype(BF16)

    res, _ = _call(body, name="transpose_w_in", grid=(pl.cdiv(cols, LANES),), parallel=True,
                   in_specs=[pl.BlockSpec((LANES, rows), lambda j: (j, 0))],
                   out_specs=[pl.BlockSpec((rows, LANES), lambda j: (0, j))], out_shape=[_sds((rows, cols), BF16)],
                   args=(wt,))
    return res[0]


def _adamw_transposed(wt, mt, vt, g_mine, g_theirs, name):
    cols, rows = wt.shape
    n_parts = len(g_mine)

    def body(w_ref, m_ref, v_ref, *rest):
        g_refs, (g_out, d_out, m_out, v_out) = rest[:-4], rest[-4:]
        mine = jnp.concatenate([r[...] for r in g_refs[:n_parts]], axis=0)
        theirs = jnp.concatenate([r[...] for r in g_refs[n_parts:]], axis=0)
        first = lax.axis_index("c") == 0
        g = jnp.transpose(jnp.concatenate([jnp.where(first, mine, theirs), jnp.where(first, theirs, mine)], axis=0))
        d, m2, v2 = _adam_values(w_ref[...], m_ref[...], v_ref[...], g)
        g_out[...] = g
        m_out[...] = m2
        v_out[...] = v2
        d_out[...] = d

    spec = pl.BlockSpec((LANES, rows), lambda j: (j, 0))
    g_specs = [pl.BlockSpec((p.shape[0], LANES), lambda j: (0, j)) for p in g_mine] * 2
    res, _ = _call(body, name=name, grid=(pl.cdiv(cols, LANES),), parallel=True, in_specs=[spec] * 3 + g_specs,
                   out_specs=[spec] * 4, out_shape=[_sds((cols, rows), F32)] * 4, args=(wt, mt, vt, *g_mine, *g_theirs))
    return res


def _inproj_fwd(x, g1, w_all, job=None):
    T = x.shape[0]
    tT = _row_tile(T, 512)

    def body(x_ref, g_ref, w_ref, a_ref, proj_ref, alow_ref):
        xv = x_ref[...]
        a = (xv * _rms_stats(xv) * g_ref[...]).astype(BF16)
        a_ref[...] = a
        for j in range(N_MAIN // 1024):
            cols = slice(j * 1024, (j + 1) * 1024)
            proj_ref[:, cols] = _dot(a, w_ref[:, cols]).astype(BF16)
        alow_ref[...] = _dot(a, w_ref[:, N_MAIN:N_ALL])

    row = lambda w: pl.BlockSpec((tT, w), lambda i: (i, 0))
    return _call(
        body, name="inproj_fwd", grid=(T // tT,), parallel=True,
        in_specs=[row(D_MODEL), pl.BlockSpec((1, D_MODEL), lambda i: (0, 0)), _whole()],
        out_specs=[row(D_MODEL), row(N_MAIN), row(LANES)],
        out_shape=[_sds((T, D_MODEL), BF16), _sds((T, N_MAIN), BF16), _sds((T, LANES), F32)],
        args=(x, g1, w_all), job=job)


def _gla_decay_terms(al_ref, wgu_ref, bg_ref, later_ref):
    logit = _dot_bf16(al_ref[...], wgu_ref[...]) + bg_ref[...]
    la = _log_sigmoid(logit) * (1.0 / GLA_TAU)
    delta = _dot_exact_lhs(later_ref[...], la)
    return logit, la, delta


def _gla_fwd(proj, alow, wgu, b_gate, gn, job=None):
    T = proj.shape[0]
    tT = _row_tile(T, 512)
    nc = tT // CHUNK

    def body(q_ref, k_ref, v_ref, r_ref, al_ref, wgu_ref, bg_ref, gn_ref, later_ref, y_ref, st_ref, s_scr):
        @pl.when(pl.program_id(0) == 0)
        def _():
            s_scr[...] = jnp.zeros_like(s_scr)

        _, la, delta = _gla_decay_terms(al_ref, wgu_ref, bg_ref, later_ref)
        kdec = (k_ref[...].astype(F32) * jnp.exp(delta)).astype(BF16)
        heads = range(GLA_HEADS)
        kcs = [slice(h * GLA_DK, (h + 1) * GLA_DK) for h in heads]
        vcs = [slice(h * GLA_DV, (h + 1) * GLA_DV) for h in heads]
        state = [s_scr[h] for h in heads]
        for c in range(nc):
            rows = slice(c * CHUNK, (c + 1) * CHUNK)
            first = slice(c * CHUNK, c * CHUNK + 1)
            dec = jnp.exp(la[first, :] + delta[first, :])
            upd_t = [_dot(v_ref[rows, vcs[h]], kdec[rows, kcs[h]], _TN) for h in heads]
            qs = [(q_ref[rows, kcs[h]].astype(F32) * (GLA_DK ** -0.5)).astype(BF16) for h in heads]
            for h in heads:
                state[h] = state[h] * dec[:, kcs[h]] + upd_t[h]
                st_ref[c, h] = state[h]
            o = [_dot(qs[h], state[h].astype(BF16), _NT) for h in heads]
            for h in heads:
                on = o[h] * _rms_stats(o[h]) * gn_ref[:, vcs[h]]
                rr = r_ref[rows, vcs[h]].astype(F32)
                y_ref[rows, vcs[h]] = (on * (rr * _sigmoid(rr))).astype(BF16)
        for h in heads:
            s_scr[h] = state[h]

    blk = lambda w, j: pl.BlockSpec((tT, w), lambda i: (i, j))
    return _call(
        body, name="gla_fwd", grid=(T // tT,),
        in_specs=[blk(512, 0), blk(512, 1), blk(1024, 1), blk(1024, 2), blk(LANES, 0)] + [_whole()] * 4,
        out_specs=[pl.BlockSpec((tT, GLA_V), lambda i: (i, 0)),
                   pl.BlockSpec((nc, GLA_HEADS, GLA_DV, GLA_DK), lambda i: (i, 0, 0, 0))],
        out_shape=[_sds((T, GLA_V), BF16), _sds((T // CHUNK, GLA_HEADS, GLA_DV, GLA_DK), F32)],
        scratch_shapes=[pltpu.VMEM((GLA_HEADS, GLA_DV, GLA_DK), F32)],
        args=(proj, proj, proj, proj, alow, wgu, b_gate, gn, _chunk_masks(tT, upper=True)), job=job)


def _sgu_mask():
    i = lax.broadcasted_iota(jnp.int32, (SGU_BLOCK, SGU_BLOCK), 0)
    j = lax.broadcasted_iota(jnp.int32, (SGU_BLOCK, SGU_BLOCK), 1)
    return lax.shift_right_logical(j, 6) <= lax.shift_right_logical(i, 6)


def _sgu_fwd(proj, ln_g, ln_b, w_sp, b_sp_t):
    T = proj.shape[0]
    tT = _row_tile(T, 512)
    nb = tT // SGU_BLOCK

    def body(su_ref, sv_ref, lg_ref, lb_ref, w_ref, b_ref, y_ref):
        mask = _sgu_mask()
        for g in range(SGU_GROUPS):
            gc = slice(g * SGU_DG, (g + 1) * SGU_DG)
            wm = jnp.where(mask, w_ref[g], 0.0).astype(BF16)
            vf = _gelu(sv_ref[:, gc].astype(F32))
            mu = jnp.mean(vf, axis=-1, keepdims=True)
            vc = vf - mu
            rstd = lax.rsqrt(jnp.mean(vc * vc, axis=-1, keepdims=True) + EPS)
            vn = (vc * rstd * lg_ref[:, gc] + lb_ref[:, gc]).astype(BF16)
            u = _gelu(su_ref[:, gc].astype(F32))
            for b in range(nb):
                rows = slice(b * SGU_BLOCK, (b + 1) * SGU_BLOCK)
                mixed = _dot(wm, vn[rows, :]) + b_ref[:, g:g + 1]
                y_ref[rows, gc] = (u[rows, :] * mixed).astype(BF16)

    blk = lambda j: pl.BlockSpec((tT, 1024), lambda i: (i, j))
    res, _ = _call(body, name="sgu_fwd", grid=(T // tT,), parallel=True,
                   in_specs=[blk(3), blk(4), _whole(), _whole(), _whole(), _whole()],
                   out_specs=[pl.BlockSpec((tT, 1024), lambda i: (i, 0))], out_shape=[_sds((T, 1024), BF16)],
                   args=(proj, proj, ln_g, ln_b, w_sp, b_sp_t))
    return res[0]


def _merge_fwd(x, proj, y_gla, y_sgu, w_bg, w_bs, w_o, g_pm, job=None):
    T = x.shape[0]
    tT = _row_tile(T, 512)

    def body(x_ref, gg_ref, gs_ref, yg_ref, ys_ref, wbg_ref, wbs_ref, wo_ref, g_ref,
             zg_ref, zs_ref, mg_ref, mix_ref, x1_ref):
        zg = _dot(yg_ref[...], wbg_ref[...])
        zs = _dot(ys_ref[...], wbs_ref[...])
        zg_ref[...] = zg.astype(BF16)
        zs_ref[...] = zs.astype(BF16)
        merged = (_sigmoid(gg_ref[...].astype(F32)) * zg + _sigmoid(gs_ref[...].astype(F32)) * zs).astype(BF16)
        mg_ref[...] = merged
        mix = _dot(merged, wo_ref[...])
        mix_ref[...] = mix.astype(BF16)
        x1_ref[...] = x_ref[...] + mix * _rms_stats(mix) * g_ref[...]

    row = pl.BlockSpec((tT, D_MODEL), lambda i: (i, 0))
    blk = lambda j: pl.BlockSpec((tT, 1024), lambda i: (i, j))
    sds = lambda dt: _sds((T, D_MODEL), dt)
    return _call(body, name="merge_fwd", grid=(T // tT,), parallel=True,
                 in_specs=[row, blk(5), blk(6), row, row, _whole(), _whole(), _whole(),
                           pl.BlockSpec((1, D_MODEL), lambda i: (0, 0))],
                 out_specs=[row] * 5, out_shape=[sds(BF16), sds(BF16), sds(BF16), sds(BF16), sds(F32)],
                 args=(x, proj, proj, y_gla, y_sgu, w_bg, w_bs, w_o, g_pm), job=job)


def _ffn_fwd_bwd(x1, tgt, w_fi_top, w_fi_bot, w_fo, g_pf, g_po):
    T = x1.shape[0]
    tT = _row_tile(T, 256)
    half = D_FF // 2
    kh = D_MODEL // 2

    def body(x1_ref, t_ref, top_ref, bot_ref, wfo_ref, gpf_ref, gpo_ref,
             h_ref, f_ref, dgu_ref, dy_ref, dx1_ref, loss_ref, dgpf_ref, dgpo_ref, gu_scr):
        @pl.when(pl.program_id(0) == 0)
        def _():
            loss_ref[...] = jnp.zeros_like(loss_ref)
            dgpf_ref[...] = jnp.zeros_like(dgpf_ref)
            dgpo_ref[...] = jnp.zeros_like(dgpo_ref)

        x1v = x1_ref[...]
        r2 = _rms_stats(x1v)
        h = (x1v * r2 * gpf_ref[...]).astype(BF16)
        h_ref[...] = h
        y = jnp.zeros((tT, D_MODEL), F32)
        for j in range(2):
            gc = slice(j * half, (j + 1) * half)
            uc = slice(D_FF + j * half, D_FF + (j + 1) * half)
            gate = _dot(h[:, :kh], top_ref[j]) + _dot(h[:, kh:], bot_ref[j])
            up = _dot(h[:, :kh], top_ref[2 + j]) + _dot(h[:, kh:], bot_ref[2 + j])
            gu_scr[:, gc] = gate
            gu_scr[:, uc] = up
            f = (gate * _sigmoid(gate) * up).astype(BF16)
            f_ref[:, gc] = f
            y = y + _dot(f, wfo_ref[gc, :])
        r3 = _rms_stats(y)
        x2 = x1v + y * r3 * gpo_ref[...]
        err = x2 - t_ref[...]
        loss_ref[...] += jnp.sum(err * err) * (0.5 / D_MODEL)
        dx2 = err * (1.0 / D_MODEL)
        dy, dg = _rms_bwd(dx2, y, r3, gpo_ref[...])
        dgpo_ref[...] += jnp.sum(dg, axis=0, keepdims=True)
        dyb = dy.astype(BF16)
        dy_ref[...] = dyb
        dh_top = jnp.zeros((tT, kh), F32)
        dh_bot = jnp.zeros((tT, kh), F32)
        for j in range(2):
            gc = slice(j * half, (j + 1) * half)
            uc = slice(D_FF + j * half, D_FF + (j + 1) * half)
            df = _dot(dyb, wfo_ref[gc, :], _NT)
            gate = gu_scr[:, gc]
            up = gu_scr[:, uc]
            sg = _sigmoid(gate)
            dgate = (df * up * (sg * (1.0 + gate * (1.0 - sg)))).astype(BF16)
            dup = (df * (gate * sg)).astype(BF16)
            dgu_ref[:, gc] = dgate
            dgu_ref[:, uc] = dup
            dh_top = dh_top + _dot(dgate, top_ref[j], _NT) + _dot(dup, top_ref[2 + j], _NT)
            dh_bot = dh_bot + _dot(dgate, bot_ref[j], _NT) + _dot(dup, bot_ref[2 + j], _NT)
        dh = jnp.concatenate([dh_top, dh_bot], axis=1)
        dx1n, dg2 = _rms_bwd(dh, x1v, r2, gpf_ref[...])
        dgpf_ref[...] += jnp.sum(dg2, axis=0, keepdims=True)
        dx1_ref[...] = dx2 + dx1n

    row = lambda w: pl.BlockSpec((tT, w), lambda i: (i, 0))
    vec = pl.BlockSpec((1, D_MODEL), lambda i: (0, 0))
    res, _ = _call(
        body, name="ffn_fwd_bwd", grid=(T // tT,),
        in_specs=[row(D_MODEL), row(D_MODEL), _whole(), _whole(), _whole(), vec, vec],
        out_specs=[row(D_MODEL), row(D_FF), row(2 * D_FF), row(D_MODEL), row(D_MODEL),
                   pl.BlockSpec((1, LANES), lambda i: (0, 0)), vec, vec],
        out_shape=[_sds((T, D_MODEL), BF16), _sds((T, D_FF), BF16), _sds((T, 2 * D_FF), BF16), _sds((T, D_MODEL), BF16),
                   _sds((T, D_MODEL), F32), _sds((1, LANES), F32), _sds((1, D_MODEL), F32), _sds((1, D_MODEL), F32)],
        scratch_shapes=[pltpu.VMEM((tT, 2 * D_FF), F32)], args=(x1, tgt, w_fi_top, w_fi_bot, w_fo, g_pf, g_po))
    return res


def _merge_bwd(dx1, mix, proj, zg, zs, w_bg, w_bs, w_o, g_pm, job=None):
    T = dx1.shape[0]
    tT = _row_tile(T, 512)

    def body(dx1_ref, mix_ref, gg_ref, gs_ref, zg_ref, zs_ref, wbg_ref, wbs_ref, wo_ref, g_ref,
             dmix_ref, dzg_ref, dzs_ref, dgate_ref, dyg_ref, dys_ref, dgpm_ref):
        @pl.when(pl.program_id(0) == 0)
        def _():
            dgpm_ref[...] = jnp.zeros_like(dgpm_ref)

        mix = mix_ref[...].astype(F32)
        dmix, dg = _rms_bwd(dx1_ref[...], mix, _rms_stats(mix), g_ref[...])
        dgpm_ref[...] += jnp.sum(dg, axis=0, keepdims=True)
        dmb = dmix.astype(BF16)
        dmix_ref[...] = dmb
        dmerged = _dot(dmb, wo_ref[...], _NT)
        for k, (gate_ref, z_ref, w_ref, dz_ref, dy_ref) in enumerate((
                (gg_ref, zg_ref, wbg_ref, dzg_ref, dyg_ref), (gs_ref, zs_ref, wbs_ref, dzs_ref, dys_ref))):
            sg = _sigmoid(gate_ref[...].astype(F32))
            dz = (dmerged * sg).astype(BF16)
            dz_ref[...] = dz
            dgate_ref[:, k * 1024:(k + 1) * 1024] = (dmerged * z_ref[...].astype(F32) * (sg * (1.0 - sg))).astype(BF16)
            dy_ref[...] = _dot(dz, w_ref[...], _NT).astype(BF16)

    row = pl.BlockSpec((tT, D_MODEL), lambda i: (i, 0))
    blk = lambda j: pl.BlockSpec((tT, 1024), lambda i: (i, j))
    vec = pl.BlockSpec((1, D_MODEL), lambda i: (0, 0))
    sds = _sds((T, D_MODEL), BF16)
    return _call(
        body, name="merge_bwd", grid=(T // tT,),
        in_specs=[row, row, blk(5), blk(6), row, row, _whole(), _whole(), _whole(), vec],
        out_specs=[row, row, row, pl.BlockSpec((tT, W_MRG), lambda i: (i, 0)), row, row, vec],
        out_shape=[sds, sds, sds, _sds((T, W_MRG), BF16), sds, sds, _sds((1, D_MODEL), F32)],
        args=(dx1, mix, proj, proj, zg, zs, w_bg, w_bs, w_o, g_pm), job=job)


def _sgu_bwd(proj, dy_sgu, ln_g, ln_b, w_sp, b_sp_t, job=None):
    T = proj.shape[0]
    tT = _row_tile(T, 512)
    nb = tT // SGU_BLOCK

    def body(su_ref, sv_ref, dy_ref, lg_ref, lb_ref, w_ref, b_ref, dp_ref, dw_ref, dbt_ref, dlg_ref, dlb_ref):
        @pl.when(pl.program_id(0) == 0)
        def _():
            dw_ref[...] = jnp.zeros_like(dw_ref)
            dbt_ref[...] = jnp.zeros_like(dbt_ref)
            dlg_ref[...] = jnp.zeros_like(dlg_ref)
            dlb_ref[...] = jnp.zeros_like(dlb_ref)

        mask = _sgu_mask()
        lane = lax.broadcasted_iota(jnp.int32, (SGU_BLOCK, LANES), 1)
        for g in range(SGU_GROUPS):
            gc = slice(g * SGU_DG, (g + 1) * SGU_DG)
            gc_v = slice(1024 + g * SGU_DG, 1024 + (g + 1) * SGU_DG)
            wm = jnp.where(mask, w_ref[g], 0.0).astype(BF16)
            vf, dvf_dsv = _gelu_and_grad(sv_ref[:, gc].astype(F32))
            mu = jnp.mean(vf, axis=-1, keepdims=True)
            vc = vf - mu
            rstd = lax.rsqrt(jnp.mean(vc * vc, axis=-1, keepdims=True) + EPS)
            vhat = vc * rstd
            vn = (vhat * lg_ref[:, gc] + lb_ref[:, gc]).astype(BF16)
            u, du_dsu = _gelu_and_grad(su_ref[:, gc].astype(F32))
            dy = dy_ref[:, gc].astype(F32)
            dmixed = (dy * u).astype(BF16)
            dvn_parts = []
            dw_acc = jnp.zeros((SGU_BLOCK, SGU_BLOCK), F32)
            db_acc = jnp.zeros((SGU_BLOCK, 1), F32)
            for b in range(nb):
                rows = slice(b * SGU_BLOCK, (b + 1) * SGU_BLOCK)
                mixed = _dot(wm, vn[rows, :]) + b_ref[:, g:g + 1]
                dp_ref[rows, gc] = (dy[rows, :] * mixed * du_dsu[rows, :]).astype(BF16)
                dvn_parts.append(_dot(wm, dmixed[rows, :], _TN))
                dw_acc = dw_acc + _dot(dmixed[rows, :], vn[rows, :], _NT)
                db_acc = db_acc + jnp.sum(dmixed[rows, :].astype(F32), axis=-1, keepdims=True)
            dw_ref[g] += jnp.where(mask, dw_acc, 0.0)
            dbt_ref[...] += jnp.where(lane == g, db_acc, 0.0)
            dvn = jnp.concatenate(dvn_parts, axis=0)
            dlg_ref[:, gc] += jnp.sum(dvn * vhat, axis=0, keepdims=True)
            dlb_ref[:, gc] += jnp.sum(dvn, axis=0, keepdims=True)
            dvh = dvn * lg_ref[:, gc]
            dvf = rstd * (dvh - jnp.mean(dvh, axis=-1, keepdims=True)
                          - vhat * jnp.mean(dvh * vhat, axis=-1, keepdims=True))
            dp_ref[:, gc_v] = (dvf * dvf_dsv).astype(BF16)

    blk = lambda j: pl.BlockSpec((tT, 1024), lambda i: (i, j))
    row = lambda w: pl.BlockSpec((tT, w), lambda i: (i, 0))
    vec = pl.BlockSpec((1, 1024), lambda i: (0, 0))
    return _call(
        body, name="sgu_bwd", grid=(T // tT,),
        in_specs=[blk(3), blk(4), row(1024), _whole(), _whole(), _whole(), _whole()],
        out_specs=[row(W_SGU), pl.BlockSpec((SGU_GROUPS, SGU_BLOCK, SGU_BLOCK), lambda i: (0, 0, 0)),
                   pl.BlockSpec((SGU_BLOCK, LANES), lambda i: (0, 0)), vec, vec],
        out_shape=[_sds((T, W_SGU), BF16), _sds((SGU_GROUPS, SGU_BLOCK, SGU_BLOCK), F32), _sds((SGU_BLOCK, LANES), F32),
                   _sds((1, 1024), F32), _sds((1, 1024), F32)],
        args=(proj, proj, dy_sgu, ln_g, ln_b, w_sp, b_sp_t), job=job)


def _gla_bwd(proj, alow, wgu, b_gate, gn, states, dy_gla, job=None):
    T = proj.shape[0]
    tT = _row_tile(T, 512)
    nc = tT // CHUNK
    nt = T // tT

    def body(q_ref, k_ref, v_ref, r_ref, al_ref, wgu_ref, bg_ref, gn_ref, later_ref, earlier_ref, st_ref, sp_ref, dy_ref,
             dp_ref, dal_ref, dgn_ref, dbg_ref, dwgu_ref, g_scr, dd_scr, dt_scr):
        step = pl.program_id(0)

        @pl.when(step == 0)
        def _():
            g_scr[...] = jnp.zeros_like(g_scr)
            dgn_ref[...] = jnp.zeros_like(dgn_ref)
            dbg_ref[...] = jnp.zeros_like(dbg_ref)
            dwgu_ref[...] = jnp.zeros_like(dwgu_ref)

        has_prev = jnp.where(step == nt - 1, 0.0, 1.0)
        logit, la, delta = _gla_decay_terms(al_ref, wgu_ref, bg_ref, later_ref)
        e = jnp.exp(delta)
        kdec_f = k_ref[...].astype(F32) * e
        kdec = kdec_f.astype(BF16)
        heads = range(GLA_HEADS)
        kcs = [slice(h * GLA_DK, (h + 1) * GLA_DK) for h in heads]
        vcs = [slice(h * GLA_DV, (h + 1) * GLA_DV) for h in heads]
        carry = [g_scr[h] for h in heads]
        dgn_acc = [jnp.zeros((1, GLA_DV), F32) for _ in heads]
        for c in reversed(range(nc)):
            rows = slice(c * CHUNK, (c + 1) * CHUNK)
            first = slice(c * CHUNK, c * CHUNK + 1)
            dec = jnp.exp(la[first, :] + delta[first, :])
            s_b = [st_ref[c, h].astype(BF16) for h in heads]
            qs = [(q_ref[rows, kcs[h]].astype(F32) * (GLA_DK ** -0.5)).astype(BF16) for h in heads]
            o = [_dot(qs[h], s_b[h], _NT) for h in heads]
            do = []
            for h in heads:
                rstd = _rms_stats(o[h])
                ohat = o[h] * rstd
                gnh = gn_ref[:, vcs[h]]
                dy = dy_ref[rows, vcs[h]].astype(F32)
                rr = r_ref[rows, vcs[h]].astype(F32)
                sg = _sigmoid(rr)
                don = dy * (rr * sg)
                dp_ref[rows, OFF_R + h * GLA_DV:OFF_R + (h + 1) * GLA_DV] = (
                    dy * (ohat * gnh) * (sg * (1.0 + rr * (1.0 - sg)))).astype(BF16)
                dgn_acc[h] = dgn_acc[h] + jnp.sum(don * ohat, axis=0, keepdims=True)
                dn = don * gnh
                do.append((rstd * (dn - ohat * jnp.mean(dn * ohat, axis=-1, keepdims=True))).astype(BF16))
            dq = [_dot(do[h], s_b[h]) for h in heads]
            g_t = [_dot(do[h], qs[h], _TN) + carry[h] for h in heads]
            g_b = [g_t[h].astype(BF16) for h in heads]
            dv = [_dot(kdec[rows, kcs[h]], g_b[h], _NT) for h in heads]
            dkdec = [_dot(v_ref[rows, vcs[h]], g_b[h]) for h in heads]
            for h in heads:
                s_prev = st_ref[c - 1, h] if c > 0 else sp_ref[0, h] * has_prev
                ddec = jnp.sum(g_t[h] * s_prev, axis=0, keepdims=True)
                carry[h] = g_t[h] * dec[:, kcs[h]]
                dp_ref[rows, OFF_Q + h * GLA_DK:OFF_Q + (h + 1) * GLA_DK] = (dq[h] * (GLA_DK ** -0.5)).astype(BF16)
                dp_ref[rows, OFF_V + h * GLA_DV:OFF_V + (h + 1) * GLA_DV] = dv[h].astype(BF16)
                dp_ref[rows, OFF_K + h * GLA_DK:OFF_K + (h + 1) * GLA_DK] = (dkdec[h] * e[rows, kcs[h]]).astype(BF16)
                dd_scr[rows, kcs[h]] = dkdec[h] * kdec_f[rows, kcs[h]]
                dt_scr[rows, kcs[h]] = jnp.broadcast_to(ddec * dec[:, kcs[h]], (CHUNK, GLA_DK))
        for h in heads:
            g_scr[h] = carry[h]
            dgn_ref[:, vcs[h]] += dgn_acc[h]
        dla = _dot_exact_lhs(earlier_ref[...], dd_scr[...]) + dt_scr[...]
        dlogit = dla * (1.0 / GLA_TAU) * _sigmoid(-logit)
        dbg_ref[...] += jnp.sum(dlogit, axis=0, keepdims=True)
        dwgu_ref[...] += _dot_bf16(al_ref[...], dlogit, _TN)
        dal_ref[...] = _dot_bf16(dlogit, wgu_ref[...], _NT).astype(BF16)

    rev = lambda i: nt - 1 - i
    blk = lambda w, j: pl.BlockSpec((tT, w), lambda i: (rev(i), j))
    st_blk = pl.BlockSpec((nc, GLA_HEADS, GLA_DV, GLA_DK), lambda i: (rev(i), 0, 0, 0))
    sp_blk = pl.BlockSpec((1, GLA_HEADS, GLA_DV, GLA_DK), lambda i: (jnp.maximum(rev(i) * nc - 1, 0), 0, 0, 0))
    return _call(
        body, name="gla_bwd", grid=(nt,),
        in_specs=[blk(512, 0), blk(512, 1), blk(1024, 1), blk(1024, 2), blk(LANES, 0)] + [_whole()] * 5
        + [st_blk, sp_blk, blk(GLA_V, 0)],
        out_specs=[blk(W_GLA, 0), blk(LANES, 0), pl.BlockSpec((1, GLA_V), lambda i: (0, 0)),
                   pl.BlockSpec((1, GLA_QK), lambda i: (0, 0)), pl.BlockSpec((LANES, GLA_QK), lambda i: (0, 0))],
        out_shape=[_sds((T, W_GLA), BF16), _sds((T, LANES), BF16), _sds((1, GLA_V), F32), _sds((1, GLA_QK), F32),
                   _sds((LANES, GLA_QK), F32)],
        scratch_shapes=[pltpu.VMEM((GLA_HEADS, GLA_DV, GLA_DK), F32), pltpu.VMEM((tT, GLA_QK), F32),
                        pltpu.VMEM((tT, GLA_QK), F32)],
        args=(proj, proj, proj, proj, alow, wgu, b_gate, gn, _chunk_masks(tT, upper=True), _chunk_masks(tT, upper=False),
              states, states, dy_gla), job=job)


def _inproj_bwd(x, dx1, g1, w_all, dparts, job=None):
    T = x.shape[0]
    tT = _row_tile(T, 512)
    offs = (0, W_GLA, W_GLA + W_SGU, N_MAIN)

    def body(x_ref, dx1_ref, g_ref, w_ref, *rest):
        part_refs, (dx_ref, dg_ref) = rest[:len(offs)], rest[len(offs):]

        @pl.when(pl.program_id(0) == 0)
        def _():
            dg_ref[...] = jnp.zeros_like(dg_ref)

        da = jnp.zeros((tT, D_MODEL), F32)
        for off, p_ref in zip(offs, part_refs):
            da = da + _dot(p_ref[...], w_ref[:, off:off + p_ref.shape[1]], _NT)
        xv = x_ref[...]
        dx, dg = _rms_bwd(da, xv, _rms_stats(xv), g_ref[...])
        dg_ref[...] += jnp.sum(dg, axis=0, keepdims=True)
        dx_ref[...] = dx1_ref[...] + dx

    row = lambda w: pl.BlockSpec((tT, w), lambda i: (i, 0))
    vec = pl.BlockSpec((1, D_MODEL), lambda i: (0, 0))
    return _call(
        body, name="inproj_bwd", grid=(T // tT,),
        in_specs=[row(D_MODEL), row(D_MODEL), vec, _whole()] + [row(p.shape[1]) for p in dparts],
        out_specs=[row(D_MODEL), vec], out_shape=[_sds((T, D_MODEL), F32), _sds((1, D_MODEL), F32)],
        args=(x, dx1, g1, w_all, *dparts), job=job)


def _tn_matmul(a, b, name, job=None):
    T, M = a.shape
    N = b.shape[1]
    tk = _row_tile(T, 1024)
    tm = M if M <= 1024 else 1408
    tn = N if N <= 3072 else N // 2
    assert M % tm == 0 and N % tn == 0

    def body(a_ref, b_ref, o_ref):
        @pl.when(pl.program_id(2) == 0)
        def _():
            o_ref[...] = _dot(a_ref[...], b_ref[...], _TN)

        @pl.when(pl.program_id(2) > 0)
        def _():
            o_ref[...] += _dot(a_ref[...], b_ref[...], _TN)

    res, jres = _call(
        body, name=name, grid=(M // tm, N // tn, T // tk),
        in_specs=[pl.BlockSpec((tk, tm), lambda i, j, k: (k, i)), pl.BlockSpec((tk, tn), lambda i, j, k: (k, j))],
        out_specs=[pl.BlockSpec((tm, tn), lambda i, j, k: (i, j))], out_shape=[_sds((M, N), F32)], args=(a, b), job=job)
    return res[0], jres


def _pad_rows(a, rows=8):
    return jnp.pad(a, ((0, rows - a.shape[0]), (0, LANES - a.shape[1])))


def _halves_view(dw):
    r = dw.shape[0] // N_CHIPS
    return dw.reshape(N_CHIPS, 2, r // 2, dw.shape[1])


def kernel(x, norm_pre_mix, w_in, w_gate_up, b_gate, gla_norm, sgu_ln_g, sgu_ln_b, w_spatial, b_spatial, w_branch_gla, w_branch_sgu, w_out, norm_post_mix, norm_pre_ffn, w_ffn_in, w_ffn_out, norm_post_ffn, loss_target, m_norm_pre_mix, m_w_in, m_w_gate_up, m_b_gate, m_gla_norm, m_sgu_ln_g, m_sgu_ln_b, m_w_spatial, m_b_spatial, m_w_branch_gla, m_w_branch_sgu, m_w_out, m_norm_post_mix, m_norm_pre_ffn, m_w_ffn_in, m_w_ffn_out, m_norm_post_ffn, v_norm_pre_mix, v_w_in, v_w_gate_up, v_b_gate, v_gla_norm, v_sgu_ln_g, v_sgu_ln_b, v_w_spatial, v_b_spatial, v_w_branch_gla, v_w_branch_sgu, v_w_out, v_norm_post_mix, v_norm_pre_ffn, v_w_ffn_in, v_w_ffn_out, v_norm_post_ffn):
    chip = 2 * lax.axis_index("x") + lax.axis_index("y")
    xt, tgt = x[0], loss_target[0]

    tiny = jnp.concatenate([w_gate_up[0], _pad_rows(gla_norm[0]), _pad_rows(sgu_ln_g[0]), _pad_rows(sgu_ln_b[0]),
                            jnp.zeros((24, LANES), F32)], axis=0)

    def with_own(gathered, own):
        return lax.dynamic_update_slice(gathered, own[None], (chip, 0, 0))

    w_in_t, m_in_t, v_in_t = w_in[0].T, m_w_in[0].T, v_w_in[0].T
    w_in_b = _transposed_cast(w_in_t)
    g_in, g_tiny = _run_job(_job_gather([w_in_b, tiny]), "gather_w_in")
    g_tiny = with_own(g_tiny, tiny)
    w_all = _relayout_w_in(with_own(g_in, w_in_b))
    cols = lambda a: a.transpose(1, 0, 2).reshape(a.shape[1], N_CHIPS * a.shape[2])
    wgu = jnp.pad(cols(g_tiny[:, 0:16]), ((0, LANES - GLA_RANK), (0, 0)))
    gn = cols(g_tiny[:, 16:20, :64]).reshape(1, GLA_V)
    ln_g = cols(g_tiny[:, 24:28, :64]).reshape(1, 1024)
    ln_b = cols(g_tiny[:, 32:36, :64]).reshape(1, 1024)
    b_sp_t = jnp.pad(b_spatial[0].T, ((0, 0), (0, LANES - SGU_GROUPS)))
    w_sp = w_spatial[0]

    own_rows = [w_branch_gla[0].astype(BF16), w_branch_sgu[0].astype(BF16), w_out[0].astype(BF16), w_ffn_out[0].astype(BF16)]
    (a, proj, alow), g_rows = _inproj_fwd(xt, norm_pre_mix, w_all, job=_job_gather(own_rows))
    rows = lambda g: g.reshape(N_CHIPS * g.shape[1], g.shape[2])
    w_bg, w_bs, w_o, w_fo = [rows(with_own(g, own)) for g, own in zip(g_rows, own_rows)]
    w_fi_b = w_ffn_in[0].astype(BF16)
    fi_top, fi_bot = w_fi_b[:D_MODEL // 2], w_fi_b[D_MODEL // 2:]
    (y_gla, states), (g_top,) = _gla_fwd(proj, alow, wgu, b_gate, gn, job=_job_gather([fi_top]))
    y_sgu = _sgu_fwd(proj, ln_g, ln_b, w_sp, b_sp_t)
    (zg, zs, merged, mix, x1), (g_bot,) = _merge_fwd(xt, proj, y_gla, y_sgu, w_bg, w_bs, w_o, norm_post_mix,
                                                     job=_job_gather([fi_bot]))
    h, f, dgu, dy, dx1, loss, d_gpf, d_gpo = _ffn_fwd_bwd(x1, tgt, with_own(g_top, fi_top), with_own(g_bot, fi_bot),
                                                          w_fo, norm_pre_ffn, norm_post_ffn)

    own_part = lambda c: lax.dynamic_index_in_dim(c, chip, 0, keepdims=False)
    whole = lambda hs: [[(h_, None)] for h_ in hs]
    dw_fo, _ = _tn_matmul(f, dy, "dw_ffn_out")
    dw_fo4 = _halves_view(dw_fo)
    dw_fi, (q_fo,) = _tn_matmul(h, dgu, "dw_ffn_in", job=_job_to_other_core([[(dw_fo4, 0)]]))
    c_fo = _presum(dw_fo4, q_fo, "presum_ffn_out")
    (dmix, dzg, dzs, dp_mrg, dyg, dys, d_gpm), (s_fo, q_fi) = _merge_bwd(
        dx1, mix, proj, zg, zs, w_bg, w_bs, w_o, norm_post_mix,
        job=_join(_job_scatter([c_fo]), _job_to_other_core([[(dw_fi, 0)]])))
    c_fi = _presum(dw_fi, q_fi, "presum_ffn_in")
    dw_c, _ = _tn_matmul(a, dp_mrg, "dw_in_merge")
    dw_o4 = _halves_view(_tn_matmul(merged, dmix, "dw_out")[0])
    dw_bg4 = _halves_view(_tn_matmul(y_gla, dzg, "dw_branch_gla")[0])
    dw_bs4 = _halves_view(_tn_matmul(y_sgu, dzs, "dw_branch_sgu")[0])
    (dp_sgu, d_wsp, d_bsp_t, d_lng, d_lnb), (q_o, q_bg, q_bs, q_c) = _sgu_bwd(
        proj, dys, ln_g, ln_b, w_sp, b_sp_t,
        job=_job_to_other_core([[(dw_o4, 0)], [(dw_bg4, 0)], [(dw_bs4, 0)], [(dw_c, 0)]]))
    c_o, c_bg, c_bs = (_presum(dw_o4, q_o, "presum_out"), _presum(dw_bg4, q_bg, "presum_branch_gla"),
                       _presum(dw_bs4, q_bs, "presum_branch_sgu"))
    h_fo = _sum_slots(own_part(c_fo), s_fo, "sum_ffn_out")
    dw_b, _ = _tn_matmul(a, dp_sgu, "dw_in_sgu")
    (dp_gla, dal, d_gn, d_bg, d_wgu), (s_fi, t_fo, q_b) = _gla_bwd(
        proj, alow, wgu, b_gate, gn, states, dyg,
        job=_join(_job_scatter([c_fi]), _job_to_other_core(whole([h_fo]) + [[(dw_b, 0)]])))
    h_fi = _sum_slots(own_part(c_fi), s_fi, "sum_ffn_in")
    dw_d, _ = _tn_matmul(a, dal, "dw_in_gate")
    dw_a, (s_o, s_bg, s_bs, t_fi, q_d) = _tn_matmul(
        a, dp_gla, "dw_in_gla",
        job=_join(_job_scatter([c_o, c_bg, c_bs]), _job_to_other_core(whole([h_fi]) + [[(dw_d, 0)]])))
    h_o, h_bg, h_bs = (_sum_slots(own_part(c_o), s_o, "sum_out"), _sum_slots(own_part(c_bg), s_bg, "sum_branch_gla"),
                       _sum_slots(own_part(c_bs), s_bs, "sum_branch_sgu"))

    grads, deltas, new_m, new_v = {}, {}, {}, {}

    def update(name, w, m, v, g_mine, g_theirs, job=None):
        (g, d, m2, v2), jres = _adamw(w[0], m[0], v[0], g_mine, g_theirs, "adamw_" + name, job=job)
        grads[name], deltas[name], new_m[name], new_v[name] = g[None], d[None], m2[None], v2[None]
        return jres

    dw_in = [(dw_a, 0), (dw_b, W_GLA), (dw_c, W_GLA + W_SGU), (dw_d, N_MAIN)]
    q_a, t_o, t_bg, t_bs = update("w_ffn_out", w_ffn_out, m_w_ffn_out, v_w_ffn_out, [h_fo], [t_fo],
                                  job=_job_to_other_core([[(dw_a, 0)]] + whole([h_o, h_bg, h_bs])))
    q_in = [q_a, q_b, q_c, q_d]
    hr_in = D_MODEL // 2
    c_in_a, _ = _presum_w_in(dw_in, q_in, 0, hr_in // 8, "presum_w_in_a")
    c_in_b, (s_in_a,) = _presum_w_in(dw_in, q_in, hr_in // 8, 7 * hr_in // 8, "presum_w_in_b",
                                     job=_job_scatter([c_in_a]))
    update("w_ffn_in", w_ffn_in, m_w_ffn_in, v_w_ffn_in, [h_fi], [t_fi])
    update("w_out", w_out, m_w_out, v_w_out, [h_o], [t_o])
    update("w_branch_gla", w_branch_gla, m_w_branch_gla, v_w_branch_gla, [h_bg], [t_bg])
    update("w_branch_sgu", w_branch_sgu, m_w_branch_sgu, v_w_branch_sgu, [h_bs], [t_bs])
    (grad_x, d_g1), (s_in_b,) = _inproj_bwd(xt, dx1, norm_pre_mix, w_all, (dp_gla, dp_sgu, dp_mrg, dal),
                                            job=_job_scatter([c_in_b]))
    h_in = [_sum_slots(own_part(c_in_a), s_in_a, "sum_w_in_a"), _sum_slots(own_part(c_in_b), s_in_b, "sum_w_in_b")]
    t_in = _run_job(_job_to_other_core(whole(h_in)), "swap_w_in")
    for store, val in zip((grads, deltas, new_m, new_v),
                          _adamw_transposed(w_in_t, m_in_t, v_in_t, h_in, t_in, "adamw_w_in")):
        store["w_in"] = val.T[None]

    small_names = ["w_spatial", "w_gate_up", "norm_pre_mix", "norm_post_mix", "norm_pre_ffn", "norm_post_ffn", "b_gate",
                   "b_spatial", "gla_norm", "sgu_ln_g", "sgu_ln_b"]
    loss_out, small = _small_adamw(
        _small_sum([d_wsp, d_wgu, d_g1, d_gpm, d_gpf, d_gpo, d_bg, d_bsp_t, d_gn, d_lng, d_lnb, loss]),
        [w_spatial, w_gate_up, norm_pre_mix, norm_post_mix, norm_pre_ffn, norm_post_ffn, b_gate, b_spatial, gla_norm,
         sgu_ln_g, sgu_ln_b],
        [m_w_spatial, m_w_gate_up, m_norm_pre_mix, m_norm_post_mix, m_norm_pre_ffn, m_norm_post_ffn, m_b_gate,
         m_b_spatial, m_gla_norm, m_sgu_ln_g, m_sgu_ln_b],
        [v_w_spatial, v_w_gate_up, v_norm_pre_mix, v_norm_post_mix, v_norm_pre_ffn, v_norm_post_ffn, v_b_gate,
         v_b_spatial, v_gla_norm, v_sgu_ln_g, v_sgu_ln_b])
    for store, vals in zip((grads, deltas, new_m, new_v), small):
        store.update(zip(small_names, vals))

    order = ["norm_pre_mix", "w_in", "w_gate_up", "b_gate", "gla_norm", "sgu_ln_g", "sgu_ln_b", "w_spatial", "b_spatial",
             "w_branch_gla", "w_branch_sgu", "w_out", "norm_post_mix", "norm_pre_ffn", "w_ffn_in", "w_ffn_out",
             "norm_post_ffn"]
    out = [loss_out, grad_x[None]]
    for store in (grads, deltas, new_m, new_v):
        out.extend(store[n] for n in order)
    return tuple(out)
```

```python
import jax
import jax.numpy as jnp
from jax import lax
from jax.experimental import pallas as pl
from jax.experimental.pallas import tpu as pltpu

F32 = jnp.float32
BF16 = jnp.bfloat16

D_MODEL = 1024
GLA_HEADS = 4
GLA_DK = 128
GLA_DV = 256
GLA_QK = GLA_HEADS * GLA_DK
GLA_V = GLA_HEADS * GLA_DV
GLA_RANK = 16
GLA_TAU = 16.0
CHUNK = 64
SGU_GROUPS = 4
SGU_BLOCK = 128
SGU_DG = 256
D_FF = 2816
EPS = 1e-6
LANES = 128

OFF_Q, OFF_K, OFF_V, OFF_R, OFF_SU, OFF_SV, OFF_GG, OFF_GS, OFF_AL = 0, 512, 1024, 2048, 3072, 4096, 5120, 6144, 7168
W_GLA, W_SGU, W_MRG = 3072, 2048, 2048
N_MAIN = 7168
N_ALL = N_MAIN + LANES
_IN_SPLITS = (GLA_QK, GLA_QK, GLA_V, GLA_V, GLA_RANK, 1024, 1024, 1024, 1024)
_IN_STARTS = tuple(sum(_IN_SPLITS[:i]) for i in range(len(_IN_SPLITS) + 1))
_IN_DST = (OFF_Q, OFF_K, OFF_V, OFF_R, OFF_AL, OFF_SU, OFF_SV, OFF_GG, OFF_GS)
D_IN = _IN_STARTS[-1]

ADAM_LR = 0.001
ADAM_B1 = 0.9
ADAM_B2 = 0.999
ADAM_EPS = 1e-08
ADAM_WD = 0.01
ADAM_STEP = 10

VMEM_LIMIT_BYTES = 56 * 1024 * 1024
N_CHIPS = 4
N_PEER = N_CHIPS - 1
N_DEV = 8
MESH = pl.DeviceIdType.MESH

_NN = (((1,), (0,)), ((), ()))
_NT = (((1,), (1,)), ((), ()))
_TN = (((0,), (0,)), ((), ()))


def _dot(a, b, dims=_NN):
    return lax.dot_general(a, b, dims, preferred_element_type=F32)


def _split(x):
    hi = x.astype(BF16)
    lo = (x - hi.astype(F32)).astype(BF16)
    return hi, lo


def _dot_bf16(a, b, dims=_NN):
    return _dot(a.astype(BF16), b.astype(BF16), dims)


def _dot_exact_lhs(m, x):
    xh, xl = _split(x)
    return _dot(m, xh) + _dot(m, xl)


def _sigmoid(x):
    return 0.5 * jnp.tanh(0.5 * x) + 0.5


def _log_sigmoid(x):
    return jnp.minimum(x, 0.0) - jnp.log(1.0 + jnp.exp(-jnp.abs(x)))


_GELU_C = 0.7978845608028654
_GELU_A = 0.044715


def _gelu_and_grad(x):
    x2 = x * x
    t = jnp.tanh(_GELU_C * (x + _GELU_A * x * x2))
    g = 0.5 * x * (1.0 + t)
    dg = 0.5 * (1.0 + t) + 0.5 * x * (1.0 - t * t) * (_GELU_C * (1.0 + 3.0 * _GELU_A * x2))
    return g, dg


def _gelu(x):
    t = jnp.tanh(_GELU_C * (x + _GELU_A * x * x * x))
    return 0.5 * x * (1.0 + t)


def _rms_stats(x):
    return lax.rsqrt(jnp.mean(x * x, axis=-1, keepdims=True) + EPS)


def _rms_bwd(dout, y, r, g):
    yhat = y * r
    dn = dout * g
    dy = r * (dn - yhat * jnp.mean(dn * yhat, axis=-1, keepdims=True))
    return dy, dout * yhat


def _whole():
    return pl.BlockSpec(memory_space=pltpu.VMEM)


def _row_tile(T, want):
    t = min(T, want)
    assert T % t == 0
    return t


def _chunk_masks(tT, upper):
    row = lax.broadcasted_iota(jnp.int32, (tT, tT), 0)
    col = lax.broadcasted_iota(jnp.int32, (tT, tT), 1)
    same = (row // CHUNK) == (col // CHUNK)
    tri = (col > row) if upper else (col < row)
    return jnp.where(same & tri, 1.0, 0.0).astype(BF16)


class _Job:
    def __init__(self, ins, out_shapes, scratch, start, finish, mid=None):
        self.ins, self.out_shapes, self.scratch = list(ins), list(out_shapes), list(scratch)
        self.start, self.finish, self.mid = start, finish, mid


def _join(*jobs):
    def split(refs, counts):
        out, at = [], 0
        for n in counts:
            out.append(refs[at:at + n])
            at += n
        return out

    ni, no, ns = [len(j.ins) for j in jobs], [len(j.out_shapes) for j in jobs], [len(j.scratch) for j in jobs]

    def start(ins, outs, scr):
        for j, a, b, c in zip(jobs, split(ins, ni), split(outs, no), split(scr, ns)):
            j.start(a, b, c)

    def finish(ins, outs, scr):
        for j, a, b, c in zip(jobs, split(ins, ni), split(outs, no), split(scr, ns)):
            j.finish(a, b, c)

    def mid(ins, outs, scr):
        for j, a, b, c in zip(jobs, split(ins, ni), split(outs, no), split(scr, ns)):
            if j.mid is not None:
                j.mid(a, b, c)

    return _Job(sum((j.ins for j in jobs), []), sum((j.out_shapes for j in jobs), []),
                sum((j.scratch for j in jobs), []), start, finish, mid if any(j.mid for j in jobs) else None)


def _mesh_pos():
    return lax.axis_index("x"), lax.axis_index("y"), lax.axis_index("c")


def _peer_chips(xi, yi):
    return [(1 - xi, yi), (xi, 1 - yi), (1 - xi, 1 - yi)]


def _half(ci, rows):
    return pl.ds(pl.multiple_of(ci * rows, 8), rows)


def _sds(shape, dtype):
    return jax.ShapeDtypeStruct(tuple(shape), dtype)


def _job_gather(arrs):
    n = len(arrs)
    kinds = 12
    Y0, Y1, X1, X0, ON_X, ON_Y, D2D = 0, 1, 2, 3, 4, 5, 6

    def copies(ins, outs, scr):
        send_sems, recv_sems = scr
        xi, yi, ci = _mesh_pos()
        me, cx, cy, cd = 2 * xi + yi, 2 * (1 - xi) + yi, 2 * xi + (1 - yi), 2 * (1 - xi) + (1 - yi)
        to_x, to_y, to_core = (1 - xi, yi, ci), (xi, 1 - yi, ci), (xi, yi, 1 - ci)
        table = []
        for k in range(n):
            qr = arrs[k].shape[0] // 4

            def rows(core, q):
                return pl.ds(pl.multiple_of((2 * core + q) * qr, 8), qr)

            def cp(kind, src, dst, to):
                s = k * kinds + kind
                return pltpu.make_async_remote_copy(src_ref=src, dst_ref=dst, send_sem=send_sems.at[s],
                                                    recv_sem=recv_sems.at[s], device_id=to, device_id_type=MESH)

            def slab(chip, core, q):
                return outs[k].at[chip, rows(core, q)]

            t = {}
            for kind, q, to, frm in ((Y0, 0, to_y, cy), (Y1, 1, to_y, cy), (X1, 1, to_x, cx), (X0, 0, to_x, cx)):
                mine = ins[k].at[rows(ci, q)]
                t[kind] = (cp(kind, mine, slab(me, ci, q), to), cp(kind, mine, slab(frm, ci, q), to))
            t[ON_X] = (cp(ON_X, slab(cy, ci, 0), slab(cy, ci, 0), to_x), cp(ON_X, slab(cy, ci, 0), slab(cd, ci, 0), to_x))
            t[ON_Y] = (cp(ON_Y, slab(cx, ci, 1), slab(cx, ci, 1), to_y), cp(ON_Y, slab(cx, ci, 1), slab(cd, ci, 1), to_y))
            for i, (chip, q) in enumerate(((cy, 0), (cy, 1), (cx, 1), (cx, 0), (cd, 0), (cd, 1))):
                t[D2D + i] = (cp(D2D + i, slab(chip, ci, q), slab(chip, ci, q), to_core),
                              cp(D2D + i, slab(chip, ci, q), slab(chip, 1 - ci, q), to_core))
            table.append(t)
        return table

    def start(ins, outs, scr):
        table = copies(ins, outs, scr)
        for kind in (Y0, X1, Y1, X0):
            for t in table:
                t[kind][0].start()

    def arrived(table, kind, then):
        for t in table:
            t[kind][1].wait_recv()
            for nxt in then:
                t[nxt][0].start()

    def mid(ins, outs, scr):
        table = copies(ins, outs, scr)
        arrived(table, Y0, (ON_X, D2D + 0))
        arrived(table, X1, (ON_Y, D2D + 2))

    def finish(ins, outs, scr):
        table = copies(ins, outs, scr)
        arrived(table, Y1, (D2D + 1,))
        arrived(table, X0, (D2D + 3,))
        arrived(table, ON_X, (D2D + 4,))
        arrived(table, ON_Y, (D2D + 5,))
        for t in table:
            for i in range(6):
                t[D2D + i][1].wait_recv()
            for kind in range(kinds):
                t[kind][0].wait_send()

    dma = pltpu.SemaphoreType.DMA
    return _Job(arrs, [_sds((N_CHIPS,) + a.shape, a.dtype) for a in arrs], [dma((n * kinds,))] * 2, start, finish, mid)


def _job_scatter(parts):
    n = len(parts)

    def copies(ins, outs, scr):
        send_sems, recv_sems = scr
        xi, yi, ci = _mesh_pos()
        res = []
        for k in range(n):
            for j, (px, py) in enumerate(_peer_chips(xi, yi)):
                s = k * N_PEER + j
                res.append(pltpu.make_async_remote_copy(
                    src_ref=ins[k].at[2 * px + py], dst_ref=outs[k].at[j], send_sem=send_sems.at[s],
                    recv_sem=recv_sems.at[s], device_id=(px, py, ci), device_id_type=MESH))
        return res

    def start(ins, outs, scr):
        for cp in copies(ins, outs, scr):
            cp.start()

    def finish(ins, outs, scr):
        for cp in copies(ins, outs, scr):
            cp.wait_recv()
            cp.wait_send()

    dma = pltpu.SemaphoreType.DMA
    return _Job(parts, [_sds((N_PEER,) + p.shape[1:], p.dtype) for p in parts], [dma((n * N_PEER,))] * 2, start, finish)


def _job_to_other_core(groups):
    pieces = [(g, a, off) for g, group in enumerate(groups) for a, off in group]
    n = len(pieces)

    def geometry(group):
        a0, off0 = group[0]
        if off0 is None:
            return a0.shape
        if a0.ndim == 4:
            return (N_CHIPS, a0.shape[2], a0.shape[3])
        return (a0.shape[0] // 2, sum(a.shape[1] for a, _ in group))

    def copies(ins, outs, scr):
        send_sems, recv_sems = scr
        xi, yi, ci = _mesh_pos()
        res = []
        for p, (g, a, off) in enumerate(pieces):
            if off is None:
                give, land = ins[p], outs[g]
            elif a.ndim == 4:
                give, land = ins[p].at[pl.ds(0, N_CHIPS), 1 - ci], outs[g]
            else:
                hr, w = a.shape[0] // 2, a.shape[1]
                give, land = ins[p].at[_half(1 - ci, hr)], outs[g].at[pl.ds(0, hr), pl.ds(off, w)]
            res.append(pltpu.make_async_remote_copy(
                src_ref=give, dst_ref=land, send_sem=send_sems.at[p], recv_sem=recv_sems.at[p],
                device_id=(xi, yi, 1 - ci), device_id_type=MESH))
        return res

    def start(ins, outs, scr):
        for cp in copies(ins, outs, scr):
            cp.start()

    def finish(ins, outs, scr):
        for cp in copies(ins, outs, scr):
            cp.wait_recv()
            cp.wait_send()

    dma = pltpu.SemaphoreType.DMA
    return _Job([a for _, a, _ in pieces], [_sds(geometry(group), group[0][0].dtype) for group in groups],
                [dma((n,))] * 2, start, finish)


def _call(body, *, name, grid, in_specs, out_specs, out_shape, args, scratch_shapes=(), parallel=False, job=None,
          by_core=False):
    n_in, n_out, n_scr = len(in_specs), len(out_specs), len(scratch_shapes)
    hbm = pl.BlockSpec(memory_space=pl.ANY)
    n_ji, n_jo = (len(job.ins), len(job.out_shapes)) if job is not None else (0, 0)
    lead = 1 if by_core else 0

    def kernel_fn(*refs):
        core, refs = refs[:lead], refs[lead:]
        ins, refs = refs[:n_in], refs[n_in:]
        j_ins, refs = refs[:n_ji], refs[n_ji:]
        outs, refs = refs[:n_out], refs[n_out:]
        j_outs, refs = refs[:n_jo], refs[n_jo:]
        scr, j_scr = refs[:n_scr], refs[n_scr:]
        if job is None:
            body(*core, *ins, *outs, *scr)
            return
        ids = [pl.program_id(d) for d in range(len(grid))]
        first = ids[0] == 0
        last = ids[0] == grid[0] - 1
        for d in range(1, len(grid)):
            first = first & (ids[d] == 0)
            last = last & (ids[d] == grid[d] - 1)

        @pl.when(first)
        def _():
            job.start(j_ins, j_outs, j_scr)

        if job.mid is not None and grid[0] >= 4:
            half_way = ids[0] == grid[0] // 2
            for d in range(1, len(grid)):
                half_way = half_way & (ids[d] == 0)

            @pl.when(half_way)
            def _():
                job.mid(j_ins, j_outs, j_scr)

        body(*core, *ins, *outs, *scr)

        @pl.when(last)
        def _():
            if job.mid is not None and grid[0] < 4:
                job.mid(j_ins, j_outs, j_scr)
            job.finish(j_ins, j_outs, j_scr)

    sem = ("parallel" if parallel and job is None else "arbitrary",) * len(grid)
    all_in = list(in_specs) + [hbm] * n_ji
    all_out = list(out_specs) + [hbm] * n_jo
    all_scratch = list(scratch_shapes) + (job.scratch if job is not None else [])
    all_shapes = list(out_shape) + (job.out_shapes if job is not None else [])
    all_args = list(args) + (job.ins if job is not None else [])
    params = pltpu.CompilerParams(dimension_semantics=sem, vmem_limit_bytes=VMEM_LIMIT_BYTES)
    if by_core:
        spec = pltpu.PrefetchScalarGridSpec(num_scalar_prefetch=1, grid=grid, in_specs=all_in, out_specs=all_out,
                                            scratch_shapes=all_scratch)
        core = lax.axis_index("c").astype(jnp.int32).reshape(1)
        res = pl.pallas_call(kernel_fn, name=name, grid_spec=spec, out_shape=all_shapes, compiler_params=params)(
            core, *all_args)
    else:
        res = pl.pallas_call(kernel_fn, name=name, grid=grid, in_specs=all_in, out_specs=all_out, out_shape=all_shapes,
                             scratch_shapes=all_scratch, compiler_params=params)(*all_args)
    return list(res[:n_out]), list(res[n_out:])


def _run_job(job, name):
    n_i, n_o = len(job.ins), len(job.out_shapes)

    def body(*refs):
        ins, outs, scr = refs[:n_i], refs[n_i:n_i + n_o], refs[n_i + n_o:]
        job.start(ins, outs, scr)
        if job.mid is not None:
            job.mid(ins, outs, scr)
        job.finish(ins, outs, scr)

    hbm = pl.BlockSpec(memory_space=pl.ANY)
    return list(pl.pallas_call(body, name=name, in_specs=[hbm] * n_i, out_specs=[hbm] * n_o, out_shape=job.out_shapes,
                               scratch_shapes=job.scratch)(*job.ins))


def _adam_values(w, m, v, g):
    m2 = ADAM_B1 * m + (1.0 - ADAM_B1) * g
    v2 = ADAM_B2 * v + (1.0 - ADAM_B2) * (g * g)
    delta = -ADAM_LR * ((m2 / (1.0 - ADAM_B1 ** ADAM_STEP)) / (jnp.sqrt(v2 / (1.0 - ADAM_B2 ** ADAM_STEP)) + ADAM_EPS)
                        + ADAM_WD * w)
    return delta, m2, v2


_P_WSP, _P_WGU, _P_NORM, _P_BG, _P_BSP, _P_HEAD, _P_LOSS, _P_ROWS = 0, 512, 576, 608, 616, 624, 720, 728


def _small_sum(dgrads):
    def body(dwsp, dwgu, dg1, dgpm, dgpf, dgpo, dbg, dbspt, dgn, dlng, dlnb, loss_in, tot, pack, slots, send_sems,
             recv_sems):
        xi, yi, ci = _mesh_pos()
        chip = 2 * xi + yi

        pack[...] = jnp.zeros_like(pack)
        for g in range(SGU_GROUPS):
            pack[_P_WSP + g * SGU_BLOCK:_P_WSP + (g + 1) * SGU_BLOCK] = dwsp[g]
        for j in range(N_CHIPS):
            pack[_P_WGU + GLA_RANK * j:_P_WGU + GLA_RANK * (j + 1)] = dwgu[0:GLA_RANK, LANES * j:LANES * (j + 1)]
        for k, r in enumerate((dg1, dgpm, dgpf, dgpo)):
            for q in range(8):
                pack[_P_NORM + 8 * k + q:_P_NORM + 8 * k + q + 1] = r[:, LANES * q:LANES * (q + 1)]
        for q in range(4):
            pack[_P_BG + q:_P_BG + q + 1] = dbg[:, LANES * q:LANES * (q + 1)]
        pack[_P_BSP:_P_BSP + SGU_GROUPS] = jnp.transpose(dbspt[...])[0:SGU_GROUPS]
        for k, r in enumerate((dgn, dlng, dlnb)):
            for j in range(N_CHIPS):
                for hh in range(4):
                    row = _P_HEAD + 32 * k + 8 * j + hh
                    pack[row:row + 1, 0:64] = r[:, 256 * hh + 64 * j:256 * hh + 64 * (j + 1)]
        pack[_P_LOSS:_P_LOSS + 1] = loss_in[...]

        to_sibling = pltpu.make_async_remote_copy(
            src_ref=pack, dst_ref=tot, send_sem=send_sems.at[N_PEER], recv_sem=recv_sems.at[N_PEER],
            device_id=(xi, yi, 1 - ci), device_id_type=MESH)
        to_sibling.start()
        to_sibling.wait_recv()
        to_sibling.wait_send()
        pack[...] = pack[...] + tot[...]
        slots[chip] = pack[...]

        def copy(j, slot):
            px, py = _peer_chips(xi, yi)[j]
            return pltpu.make_async_remote_copy(
                src_ref=pack, dst_ref=slots.at[slot(2 * px + py)], send_sem=send_sems.at[j], recv_sem=recv_sems.at[j],
                device_id=(px, py, ci), device_id_type=MESH)

        sends = [copy(j, lambda peer_chip: chip) for j in range(N_PEER)]
        for cp in sends:
            cp.start()
        for j in range(N_PEER):
            copy(j, lambda peer_chip: peer_chip).wait_recv()
        for cp in sends:
            cp.wait_send()
        acc = slots[0]
        for d in range(1, N_CHIPS):
            acc = acc + slots[d]
        tot[...] = acc

    return pl.pallas_call(
        body, name="small_sum", in_specs=[_whole()] * 12, out_specs=_whole(), out_shape=_sds((_P_ROWS, LANES), F32),
        scratch_shapes=[pltpu.VMEM((_P_ROWS, LANES), F32), pltpu.VMEM((N_CHIPS, _P_ROWS, LANES), F32),
                        pltpu.SemaphoreType.DMA((N_PEER + 1,)), pltpu.SemaphoreType.DMA((N_PEER + 1,))],
        compiler_params=pltpu.CompilerParams(vmem_limit_bytes=VMEM_LIMIT_BYTES),
    )(*dgrads)


def _small_adamw(tot, ws, ms, vs):
    n = len(ws)

    def body(*refs):
        tot = refs[0]
        w_refs, m_refs, v_refs = refs[1:1 + n], refs[1 + n:1 + 2 * n], refs[1 + 2 * n:1 + 3 * n]
        loss_out = refs[1 + 3 * n]
        outs = refs[2 + 3 * n:]
        chip = 2 * lax.axis_index("x") + lax.axis_index("y")
        loss_out[...] = tot[_P_LOSS:_P_LOSS + 1, 0:1]

        def step(k, g, pick, put):
            d, m2, v2 = _adam_values(pick(w_refs[k]), pick(m_refs[k]), pick(v_refs[k]), g)
            for o, val in zip((outs[k], outs[n + k], outs[2 * n + k], outs[3 * n + k]), (g, d, m2, v2)):
                put(o, val)

        def whole(ref):
            return ref[0]

        def put_whole(ref, val):
            ref[0] = val

        for g in range(SGU_GROUPS):
            def pick_g(ref, g=g):
                return ref[0, g]

            def put_g(ref, val, g=g):
                ref[0, g] = val

            step(0, tot[_P_WSP + g * SGU_BLOCK:_P_WSP + (g + 1) * SGU_BLOCK], pick_g, put_g)
        step(1, tot[pl.ds(pl.multiple_of(_P_WGU + GLA_RANK * chip, GLA_RANK), GLA_RANK), :], whole, put_whole)
        for k, (base, chunks) in enumerate(((_P_NORM, 8), (_P_NORM + 8, 8), (_P_NORM + 16, 8), (_P_NORM + 24, 8), (_P_BG, 4))):
            for q in range(chunks):
                def pick_q(ref, q=q):
                    return ref[:, LANES * q:LANES * (q + 1)]

                def put_q(ref, val, q=q):
                    ref[:, LANES * q:LANES * (q + 1)] = val

                step(2 + k, tot[base + q:base + q + 1], pick_q, put_q)
        step(7, tot[_P_BSP:_P_BSP + SGU_GROUPS], whole, put_whole)
        for k in range(3):
            mine = tot[pl.ds(pl.multiple_of(_P_HEAD + 32 * k + 8 * chip, 8), 8), :]
            step(8 + k, mine[0:4, 0:64], whole, put_whole)

    shapes = [_sds(w.shape, F32) for w in ws]
    res = pl.pallas_call(
        body, name="small_adamw", in_specs=[_whole()] * (1 + 3 * n), out_specs=[_whole()] * (1 + 4 * n),
        out_shape=[_sds((1, 1), F32)] + shapes * 4,
        compiler_params=pltpu.CompilerParams(vmem_limit_bytes=VMEM_LIMIT_BYTES),
    )(tot, *ws, *ms, *vs)
    return res[0].reshape(()), [list(res[1 + i * n:1 + (i + 1) * n]) for i in range(4)]


def _w_in_pieces():
    blk = D_IN // N_CHIPS
    pieces = []
    for s in range(len(_IN_SPLITS)):
        lo_s, hi_s = _IN_STARTS[s], _IN_STARTS[s + 1]
        for j in range(N_CHIPS):
            lo, hi = max(lo_s, j * blk), min(hi_s, (j + 1) * blk)
            if lo < hi:
                pieces.append((j, lo - j * blk, _IN_DST[s] + lo - lo_s, hi - lo))
    return pieces


def _relayout_w_in(gathered):
    _, rows, blk = gathered.shape
    tr = 256

    def body(g_ref, o_ref):
        o_ref[:, OFF_AL:N_ALL] = jnp.zeros((tr, LANES), BF16)
        for j, src, dst, w in _w_in_pieces():
            o_ref[:, dst:dst + w] = g_ref[j, :, src:src + w]

    res, _ = _call(body, name="relayout_w_in", grid=(rows // tr,), parallel=True,
                   in_specs=[pl.BlockSpec((N_CHIPS, tr, blk), lambda i: (0, i, 0))],
                   out_specs=[pl.BlockSpec((tr, N_ALL), lambda i: (i, 0))],
                   out_shape=[_sds((rows, N_ALL), BF16)], args=(gathered,))
    return res[0]


def _update_row_tile(rows):
    for t in range(min(rows, 256), 7, -8):
        if rows % t == 0:
            return t
    return rows


def _presum_w_in(dws, theirs, row0, rows, name, job=None):
    hr = theirs[0].shape[0]
    blk = D_IN // N_CHIPS
    tr = 64
    assert row0 % tr == 0 and rows % tr == 0
    nh, t0 = hr // tr, row0 // tr
    n = len(dws)

    def body(core_ref, *refs):
        dw_refs, q_refs, (o_ref, s_scr) = refs[:n], refs[n:2 * n], refs[2 * n:]
        for p, (a, off) in enumerate(dws):
            w = a.shape[1]
            s_scr[:, off:off + w] = (dw_refs[p][...] + q_refs[p][...]).astype(BF16)
        for j, src, dst, w in _w_in_pieces():
            o_ref[j, :, src:src + w] = s_scr[:, dst:dst + w]

    in_specs = [pl.BlockSpec((tr, a.shape[1]), lambda i, core: (i + t0 + core[0] * nh, 0)) for a, _ in dws]
    in_specs += [pl.BlockSpec((tr, q.shape[1]), lambda i, core: (i + t0, 0)) for q in theirs]
    res, jres = _call(body, name=name, grid=(rows // tr,), parallel=True, in_specs=in_specs,
                      out_specs=[pl.BlockSpec((N_CHIPS, tr, blk), lambda i, core: (0, i, 0))],
                      out_shape=[_sds((N_CHIPS, rows, blk), BF16)], scratch_shapes=[pltpu.VMEM((tr, N_ALL), BF16)],
                      args=(*[a for a, _ in dws], *theirs), job=job, by_core=True)
    return res[0], jres


def _presum(dw, theirs, name):
    if dw.ndim == 4:
        _, _, hr, c = dw.shape
        tr = _update_row_tile(hr)
        mine = pl.BlockSpec((1, 1, tr, c), lambda j, i, core: (j, core[0], i, 0))
        other = pl.BlockSpec((1, tr, c), lambda j, i, core: (j, i, 0))
    else:
        hr, c = dw.shape[0] // 2, dw.shape[1] // N_CHIPS
        tr = _update_row_tile(hr)
        nh = hr // tr
        mine = pl.BlockSpec((tr, c), lambda j, i, core: (i + core[0] * nh, j))
        other = pl.BlockSpec((tr, c), lambda j, i, core: (i, j))

    def body(core_ref, a_ref, q_ref, o_ref):
        o_ref[...] = (a_ref[...].reshape(tr, c) + q_ref[...].reshape(tr, c)).astype(BF16).reshape(o_ref.shape)

    res, _ = _call(body, name=name, grid=(N_CHIPS, hr // tr), parallel=True, in_specs=[mine, other],
                   out_specs=[pl.BlockSpec((1, tr, c), lambda j, i, core: (j, i, 0))],
                   out_shape=[_sds((N_CHIPS, hr, c), BF16)], args=(dw, theirs), by_core=True)
    return res[0]


def _sum_slots(own, slots, name):
    rows, cols = own.shape
    tr = _update_row_tile(rows)

    def body(own_ref, s_ref, o_ref):
        acc = own_ref[...].astype(F32)
        for j in range(N_PEER):
            acc = acc + s_ref[j].astype(F32)
        o_ref[...] = acc

    res, _ = _call(body, name=name, grid=(rows // tr,), parallel=True,
                   in_specs=[pl.BlockSpec((tr, cols), lambda i: (i, 0)), pl.BlockSpec((N_PEER, tr, cols), lambda i: (0, i, 0))],
                   out_specs=[pl.BlockSpec((tr, cols), lambda i: (i, 0))], out_shape=[_sds((rows, cols), F32)],
                   args=(own, slots))
    return res[0]


def _adamw(w, m, v, g_mine, g_theirs, name, job=None):
    rows, cols = w.shape
    part_rows = [p.shape[0] for p in g_mine]
    assert sum(part_rows) == rows // 2 and [p.shape[0] for p in g_theirs] == part_rows
    tr = _update_row_tile(min(part_rows))
    assert all(r % tr == 0 for r in part_rows)
    nh = (rows // 2) // tr
    starts = [sum(part_rows[:k]) // tr for k in range(len(part_rows))]
    n_parts = len(part_rows)

    def body(core_ref, w_ref, m_ref, v_ref, *rest):
        g_refs, (g_out, d_out, m_out, v_out) = rest[:-4], rest[-4:]
        step = pl.program_id(0)
        mine_here = (step // nh) == core_ref[0]
        q = step % nh
        g = None
        for k in reversed(range(n_parts)):
            val = jnp.where(mine_here, g_refs[k][...], g_refs[n_parts + k][...])
            g = val if g is None else jnp.where(q < starts[k + 1], val, g)
        d, m2, v2 = _adam_values(w_ref[...], m_ref[...], v_ref[...], g)
        g_out[...] = g
        m_out[...] = m2
        v_out[...] = v2
        d_out[...] = d

    def g_spec(k, mine):
        last = part_rows[k] // tr - 1

        def index(i, core):
            half = core[0] if mine else 1 - core[0]
            here = jnp.clip(i % nh - starts[k], 0, last)
            return (jnp.where(i // nh == half, here, jnp.where(i // nh > half, last, 0)), 0)

        return pl.BlockSpec((tr, cols), index)

    spec = pl.BlockSpec((tr, cols), lambda i, core: (i, 0))
    g_specs = [g_spec(k, True) for k in range(n_parts)] + [g_spec(k, False) for k in range(n_parts)]
    return _call(body, name=name, grid=(rows // tr,), parallel=True, in_specs=[spec] * 3 + g_specs,
                 out_specs=[spec] * 4, out_shape=[_sds((rows, cols), F32)] * 4, args=(w, m, v, *g_mine, *g_theirs),
                 job=job, by_core=True)


def _transposed_cast(wt):
    cols, rows = wt.shape

    def body(x_ref, o_ref):
        o_ref[...] = jnp.transpose(x_ref[...]).astype(BF16)

    res, _ = _call(body, name="transpose_w_in", grid=(pl.cdiv(cols, LANES),), parallel=True,
                   in_specs=[pl.BlockSpec((LANES, rows), lambda j: (j, 0))],
                   out_specs=[pl.BlockSpec((rows, LANES), lambda j: (0, j))], out_shape=[_sds((rows, cols), BF16)],
                   args=(wt,))
    return res[0]


def _adamw_transposed(wt, mt, vt, g_mine, g_theirs, name):
    cols, rows = wt.shape
    n_parts = len(g_mine)

    def body(w_ref, m_ref, v_ref, *rest):
        g_refs, (g_out, d_out, m_out, v_out) = rest[:-4], rest[-4:]
        mine = jnp.concatenate([r[...] for r in g_refs[:n_parts]], axis=0)
        theirs = jnp.concatenate([r[...] for r in g_refs[n_parts:]], axis=0)
        first = lax.axis_index("c") == 0
        g = jnp.transpose(jnp.concatenate([jnp.where(first, mine, theirs), jnp.where(first, theirs, mine)], axis=0))
        d, m2, v2 = _adam_values(w_ref[...], m_ref[...], v_ref[...], g)
        g_out[...] = g
        m_out[...] = m2
        v_out[...] = v2
        d_out[...] = d

    spec = pl.BlockSpec((LANES, rows), lambda j: (j, 0))
    g_specs = [pl.BlockSpec((p.shape[0], LANES), lambda j: (0, j)) for p in g_mine] * 2
    res, _ = _call(body, name=name, grid=(pl.cdiv(cols, LANES),), parallel=True, in_specs=[spec] * 3 + g_specs,
                   out_specs=[spec] * 4, out_shape=[_sds((cols, rows), F32)] * 4, args=(wt, mt, vt, *g_mine, *g_theirs))
    return res


def _inproj_fwd(x, g1, w_all, job=None):
    T = x.shape[0]
    tT = _row_tile(T, 512)

    def body(x_ref, g_ref, w_ref, a_ref, proj_ref, alow_ref):
        xv = x_ref[...]
        a = (xv * _rms_stats(xv) * g_ref[...]).astype(BF16)
        a_ref[...] = a
        for j in range(N_MAIN // 1024):
            cols = slice(j * 1024, (j + 1) * 1024)
            proj_ref[:, cols] = _dot(a, w_ref[:, cols]).astype(BF16)
        alow_ref[...] = _dot(a, w_ref[:, N_MAIN:N_ALL])

    row = lambda w: pl.BlockSpec((tT, w), lambda i: (i, 0))
    return _call(
        body, name="inproj_fwd", grid=(T // tT,), parallel=True,
        in_specs=[row(D_MODEL), pl.BlockSpec((1, D_MODEL), lambda i: (0, 0)), _whole()],
        out_specs=[row(D_MODEL), row(N_MAIN), row(LANES)],
        out_shape=[_sds((T, D_MODEL), BF16), _sds((T, N_MAIN), BF16), _sds((T, LANES), F32)],
        args=(x, g1, w_all), job=job)


def _gla_decay_terms(al_ref, wgu_ref, bg_ref, later_ref):
    logit = _dot_bf16(al_ref[...], wgu_ref[...]) + bg_ref[...]
    la = _log_sigmoid(logit) * (1.0 / GLA_TAU)
    delta = _dot_exact_lhs(later_ref[...], la)
    return logit, la, delta


def _gla_fwd(proj, alow, wgu, b_gate, gn, job=None):
    T = proj.shape[0]
    tT = _row_tile(T, 512)
    nc = tT // CHUNK

    def body(q_ref, k_ref, v_ref, r_ref, al_ref, wgu_ref, bg_ref, gn_ref, later_ref, y_ref, st_ref, s_scr):
        @pl.when(pl.program_id(0) == 0)
        def _():
            s_scr[...] = jnp.zeros_like(s_scr)

        _, la, delta = _gla_decay_terms(al_ref, wgu_ref, bg_ref, later_ref)
        kdec = (k_ref[...].astype(F32) * jnp.exp(delta)).astype(BF16)
        heads = range(GLA_HEADS)
        kcs = [slice(h * GLA_DK, (h + 1) * GLA_DK) for h in heads]
        vcs = [slice(h * GLA_DV, (h + 1) * GLA_DV) for h in heads]
        state = [s_scr[h] for h in heads]
        for c in range(nc):
            rows = slice(c * CHUNK, (c + 1) * CHUNK)
            first = slice(c * CHUNK, c * CHUNK + 1)
            dec = jnp.exp(la[first, :] + delta[first, :])
            upd_t = [_dot(v_ref[rows, vcs[h]], kdec[rows, kcs[h]], _TN) for h in heads]
            qs = [(q_ref[rows, kcs[h]].astype(F32) * (GLA_DK ** -0.5)).astype(BF16) for h in heads]
            for h in heads:
                state[h] = state[h] * dec[:, kcs[h]] + upd_t[h]
                st_ref[c, h] = state[h]
            o = [_dot(qs[h], state[h].astype(BF16), _NT) for h in heads]
            for h in heads:
                on = o[h] * _rms_stats(o[h]) * gn_ref[:, vcs[h]]
                rr = r_ref[rows, vcs[h]].astype(F32)
                y_ref[rows, vcs[h]] = (on * (rr * _sigmoid(rr))).astype(BF16)
        for h in heads:
            s_scr[h] = state[h]

    blk = lambda w, j: pl.BlockSpec((tT, w), lambda i: (i, j))
    return _call(
        body, name="gla_fwd", grid=(T // tT,),
        in_specs=[blk(512, 0), blk(512, 1), blk(1024, 1), blk(1024, 2), blk(LANES, 0)] + [_whole()] * 4,
        out_specs=[pl.BlockSpec((tT, GLA_V), lambda i: (i, 0)),
                   pl.BlockSpec((nc, GLA_HEADS, GLA_DV, GLA_DK), lambda i: (i, 0, 0, 0))],
        out_shape=[_sds((T, GLA_V), BF16), _sds((T // CHUNK, GLA_HEADS, GLA_DV, GLA_DK), F32)],
        scratch_shapes=[pltpu.VMEM((GLA_HEADS, GLA_DV, GLA_DK), F32)],
        args=(proj, proj, proj, proj, alow, wgu, b_gate, gn, _chunk_masks(tT, upper=True)), job=job)


def _sgu_mask():
    i = lax.broadcasted_iota(jnp.int32, (SGU_BLOCK, SGU_BLOCK), 0)
    j = lax.broadcasted_iota(jnp.int32, (SGU_BLOCK, SGU_BLOCK), 1)
    return lax.shift_right_logical(j, 6) <= lax.shift_right_logical(i, 6)


def _sgu_fwd(proj, ln_g, ln_b, w_sp, b_sp_t):
    T = proj.shape[0]
    tT = _row_tile(T, 512)
    nb = tT // SGU_BLOCK

    def body(su_ref, sv_ref, lg_ref, lb_ref, w_ref, b_ref, y_ref):
        mask = _sgu_mask()
        for g in range(SGU_GROUPS):
            gc = slice(g * SGU_DG, (g + 1) * SGU_DG)
            wm = jnp.where(mask, w_ref[g], 0.0).astype(BF16)
            vf = _gelu(sv_ref[:, gc].astype(F32))
            mu = jnp.mean(vf, axis=-1, keepdims=True)
            vc = vf - mu
            rstd = lax.rsqrt(jnp.mean(vc * vc, axis=-1, keepdims=True) + EPS)
            vn = (vc * rstd * lg_ref[:, gc] + lb_ref[:, gc]).astype(BF16)
            u = _gelu(su_ref[:, gc].astype(F32))
            for b in range(nb):
                rows = slice(b * SGU_BLOCK, (b + 1) * SGU_BLOCK)
                mixed = _dot(wm, vn[rows, :]) + b_ref[:, g:g + 1]
                y_ref[rows, gc] = (u[rows, :] * mixed).astype(BF16)

    blk = lambda j: pl.BlockSpec((tT, 1024), lambda i: (i, j))
    res, _ = _call(body, name="sgu_fwd", grid=(T // tT,), parallel=True,
                   in_specs=[blk(3), blk(4), _whole(), _whole(), _whole(), _whole()],
                   out_specs=[pl.BlockSpec((tT, 1024), lambda i: (i, 0))], out_shape=[_sds((T, 1024), BF16)],
                   args=(proj, proj, ln_g, ln_b, w_sp, b_sp_t))
    return res[0]


def _merge_fwd(x, proj, y_gla, y_sgu, w_bg, w_bs, w_o, g_pm, job=None):
    T = x.shape[0]
    tT = _row_tile(T, 512)

    def body(x_ref, gg_ref, gs_ref, yg_ref, ys_ref, wbg_ref, wbs_ref, wo_ref, g_ref,
             zg_ref, zs_ref, mg_ref, mix_ref, x1_ref):
        zg = _dot(yg_ref[...], wbg_ref[...])
        zs = _dot(ys_ref[...], wbs_ref[...])
        zg_ref[...] = zg.astype(BF16)
        zs_ref[...] = zs.astype(BF16)
        merged = (_sigmoid(gg_ref[...].astype(F32)) * zg + _sigmoid(gs_ref[...].astype(F32)) * zs).astype(BF16)
        mg_ref[...] = merged
        mix = _dot(merged, wo_ref[...])
        mix_ref[...] = mix.astype(BF16)
        x1_ref[...] = x_ref[...] + mix * _rms_stats(mix) * g_ref[...]

    row = pl.BlockSpec((tT, D_MODEL), lambda i: (i, 0))
    blk = lambda j: pl.BlockSpec((tT, 1024), lambda i: (i, j))
    sds = lambda dt: _sds((T, D_MODEL), dt)
    return _call(body, name="merge_fwd", grid=(T // tT,), parallel=True,
                 in_specs=[row, blk(5), blk(6), row, row, _whole(), _whole(), _whole(),
                           pl.BlockSpec((1, D_MODEL), lambda i: (0, 0))],
                 out_specs=[row] * 5, out_shape=[sds(BF16), sds(BF16), sds(BF16), sds(BF16), sds(F32)],
                 args=(x, proj, proj, y_gla, y_sgu, w_bg, w_bs, w_o, g_pm), job=job)


def _ffn_fwd_bwd(x1, tgt, w_fi_top, w_fi_bot, w_fo, g_pf, g_po):
    T = x1.shape[0]
    tT = _row_tile(T, 256)
    half = D_FF // 2
    kh = D_MODEL // 2

    def body(x1_ref, t_ref, top_ref, bot_ref, wfo_ref, gpf_ref, gpo_ref,
             h_ref, f_ref, dgu_ref, dy_ref, dx1_ref, loss_ref, dgpf_ref, dgpo_ref, gu_scr):
        @pl.when(pl.program_id(0) == 0)
        def _():
            loss_ref[...] = jnp.zeros_like(loss_ref)
            dgpf_ref[...] = jnp.zeros_like(dgpf_ref)
            dgpo_ref[...] = jnp.zeros_like(dgpo_ref)

        main = (half // 256) * 256
        pieces = (0, 1, None)

        def w_in_cols(ref, first_slab, p):
            if p is not None:
                return ref[first_slab + p, :, :main]
            return jnp.concatenate([ref[first_slab, :, main:], ref[first_slab + 1, :, main:]], axis=1)

        def w_out_rows(p):
            if p is not None:
                return wfo_ref[p * half:p * half + main, :]
            return jnp.concatenate([wfo_ref[main:half, :], wfo_ref[half + main:2 * half, :]], axis=0)

        def put(ref, base, p, val):
            if p is not None:
                ref[:, base + p * half:base + p * half + main] = val
            else:
                ref[:, base + main:base + half] = val[:, :half - main]
                ref[:, base + half + main:base + 2 * half] = val[:, half - main:]

        def get(ref, base, p):
            if p is not None:
                return ref[:, base + p * half:base + p * half + main]
            return jnp.concatenate([ref[:, base + main:base + half], ref[:, base + half + main:base + 2 * half]], axis=1)

        x1v = x1_ref[...]
        r2 = _rms_stats(x1v)
        h = (x1v * r2 * gpf_ref[...]).astype(BF16)
        h_ref[...] = h
        y = jnp.zeros((tT, D_MODEL), F32)
        for p in pieces:
            gate = _dot(h[:, :kh], w_in_cols(top_ref, 0, p)) + _dot(h[:, kh:], w_in_cols(bot_ref, 0, p))
            up = _dot(h[:, :kh], w_in_cols(top_ref, 2, p)) + _dot(h[:, kh:], w_in_cols(bot_ref, 2, p))
            put(gu_scr, 0, p, gate)
            put(gu_scr, D_FF, p, up)
            f = (gate * _sigmoid(gate) * up).astype(BF16)
            put(f_ref, 0, p, f)
            y = y + _dot(f, w_out_rows(p))
        r3 = _rms_stats(y)
        x2 = x1v + y * r3 * gpo_ref[...]
        err = x2 - t_ref[...]
        loss_ref[...] += jnp.sum(err * err) * (0.5 / D_MODEL)
        dx2 = err * (1.0 / D_MODEL)
        dy, dg = _rms_bwd(dx2, y, r3, gpo_ref[...])
        dgpo_ref[...] += jnp.sum(dg, axis=0, keepdims=True)
        dyb = dy.astype(BF16)
        dy_ref[...] = dyb
        dh_top = jnp.zeros((tT, kh), F32)
        dh_bot = jnp.zeros((tT, kh), F32)
        for p in pieces:
            df = _dot(dyb, w_out_rows(p), _NT)
            gate = get(gu_scr, 0, p)
            up = get(gu_scr, D_FF, p)
            sg = _sigmoid(gate)
            dgate = (df * up * (sg * (1.0 + gate * (1.0 - sg)))).astype(BF16)
            dup = (df * (gate * sg)).astype(BF16)
            put(dgu_ref, 0, p, dgate)
            put(dgu_ref, D_FF, p, dup)
            dh_top = dh_top + _dot(dgate, w_in_cols(top_ref, 0, p), _NT) + _dot(dup, w_in_cols(top_ref, 2, p), _NT)
            dh_bot = dh_bot + _dot(dgate, w_in_cols(bot_ref, 0, p), _NT) + _dot(dup, w_in_cols(bot_ref, 2, p), _NT)
        dh = jnp.concatenate([dh_top, dh_bot], axis=1)
        dx1n, dg2 = _rms_bwd(dh, x1v, r2, gpf_ref[...])
        dgpf_ref[...] += jnp.sum(dg2, axis=0, keepdims=True)
        dx1_ref[...] = dx2 + dx1n

    row = lambda w: pl.BlockSpec((tT, w), lambda i: (i, 0))
    vec = pl.BlockSpec((1, D_MODEL), lambda i: (0, 0))
    res, _ = _call(
        body, name="ffn_fwd_bwd", grid=(T // tT,),
        in_specs=[row(D_MODEL), row(D_MODEL), _whole(), _whole(), _whole(), vec, vec],
        out_specs=[row(D_MODEL), row(D_FF), row(2 * D_FF), row(D_MODEL), row(D_MODEL),
                   pl.BlockSpec((1, LANES), lambda i: (0, 0)), vec, vec],
        out_shape=[_sds((T, D_MODEL), BF16), _sds((T, D_FF), BF16), _sds((T, 2 * D_FF), BF16), _sds((T, D_MODEL), BF16),
                   _sds((T, D_MODEL), F32), _sds((1, LANES), F32), _sds((1, D_MODEL), F32), _sds((1, D_MODEL), F32)],
        scratch_shapes=[pltpu.VMEM((tT, 2 * D_FF), F32)], args=(x1, tgt, w_fi_top, w_fi_bot, w_fo, g_pf, g_po))
    return res


def _merge_bwd(dx1, mix, proj, zg, zs, w_bg, w_bs, w_o, g_pm, job=None):
    T = dx1.shape[0]
    tT = _row_tile(T, 512)

    def body(dx1_ref, mix_ref, gg_ref, gs_ref, zg_ref, zs_ref, wbg_ref, wbs_ref, wo_ref, g_ref,
             dmix_ref, dzg_ref, dzs_ref, dgate_ref, dyg_ref, dys_ref, dgpm_ref):
        @pl.when(pl.program_id(0) == 0)
        def _():
            dgpm_ref[...] = jnp.zeros_like(dgpm_ref)

        mix = mix_ref[...].astype(F32)
        dmix, dg = _rms_bwd(dx1_ref[...], mix, _rms_stats(mix), g_ref[...])
        dgpm_ref[...] += jnp.sum(dg, axis=0, keepdims=True)
        dmb = dmix.astype(BF16)
        dmix_ref[...] = dmb
        dmerged = _dot(dmb, wo_ref[...], _NT)
        for k, (gate_ref, z_ref, w_ref, dz_ref, dy_ref) in enumerate((
                (gg_ref, zg_ref, wbg_ref, dzg_ref, dyg_ref), (gs_ref, zs_ref, wbs_ref, dzs_ref, dys_ref))):
            sg = _sigmoid(gate_ref[...].astype(F32))
            dz = (dmerged * sg).astype(BF16)
            dz_ref[...] = dz
            dgate_ref[:, k * 1024:(k + 1) * 1024] = (dmerged * z_ref[...].astype(F32) * (sg * (1.0 - sg))).astype(BF16)
            dy_ref[...] = _dot(dz, w_ref[...], _NT).astype(BF16)

    row = pl.BlockSpec((tT, D_MODEL), lambda i: (i, 0))
    blk = lambda j: pl.BlockSpec((tT, 1024), lambda i: (i, j))
    vec = pl.BlockSpec((1, D_MODEL), lambda i: (0, 0))
    sds = _sds((T, D_MODEL), BF16)
    return _call(
        body, name="merge_bwd", grid=(T // tT,),
        in_specs=[row, row, blk(5), blk(6), row, row, _whole(), _whole(), _whole(), vec],
        out_specs=[row, row, row, pl.BlockSpec((tT, W_MRG), lambda i: (i, 0)), row, row, vec],
        out_shape=[sds, sds, sds, _sds((T, W_MRG), BF16), sds, sds, _sds((1, D_MODEL), F32)],
        args=(dx1, mix, proj, proj, zg, zs, w_bg, w_bs, w_o, g_pm), job=job)


def _sgu_bwd(proj, dy_sgu, ln_g, ln_b, w_sp, b_sp_t, job=None):
    T = proj.shape[0]
    tT = _row_tile(T, 512)
    nb = tT // SGU_BLOCK

    def body(su_ref, sv_ref, dy_ref, lg_ref, lb_ref, w_ref, b_ref, dp_ref, dw_ref, dbt_ref, dlg_ref, dlb_ref):
        @pl.when(pl.program_id(0) == 0)
        def _():
            dw_ref[...] = jnp.zeros_like(dw_ref)
            dbt_ref[...] = jnp.zeros_like(dbt_ref)
            dlg_ref[...] = jnp.zeros_like(dlg_ref)
            dlb_ref[...] = jnp.zeros_like(dlb_ref)

        mask = _sgu_mask()
        lane = lax.broadcasted_iota(jnp.int32, (SGU_BLOCK, LANES), 1)
        for g in range(SGU_GROUPS):
            gc = slice(g * SGU_DG, (g + 1) * SGU_DG)
            gc_v = slice(1024 + g * SGU_DG, 1024 + (g + 1) * SGU_DG)
            wm = jnp.where(mask, w_ref[g], 0.0).astype(BF16)
            vf, dvf_dsv = _gelu_and_grad(sv_ref[:, gc].astype(F32))
            mu = jnp.mean(vf, axis=-1, keepdims=True)
            vc = vf - mu
            rstd = lax.rsqrt(jnp.mean(vc * vc, axis=-1, keepdims=True) + EPS)
            vhat = vc * rstd
            vn = (vhat * lg_ref[:, gc] + lb_ref[:, gc]).astype(BF16)
            u, du_dsu = _gelu_and_grad(su_ref[:, gc].astype(F32))
            dy = dy_ref[:, gc].astype(F32)
            dmixed = (dy * u).astype(BF16)
            dvn_parts = []
            dw_acc = jnp.zeros((SGU_BLOCK, SGU_BLOCK), F32)
            db_acc = jnp.zeros((SGU_BLOCK, 1), F32)
            for b in range(nb):
                rows = slice(b * SGU_BLOCK, (b + 1) * SGU_BLOCK)
                mixed = _dot(wm, vn[rows, :]) + b_ref[:, g:g + 1]
                dp_ref[rows, gc] = (dy[rows, :] * mixed * du_dsu[rows, :]).astype(BF16)
                dvn_parts.append(_dot(wm, dmixed[rows, :], _TN))
                dw_acc = dw_acc + _dot(dmixed[rows, :], vn[rows, :], _NT)
                db_acc = db_acc + jnp.sum(dmixed[rows, :].astype(F32), axis=-1, keepdims=True)
            dw_ref[g] += jnp.where(mask, dw_acc, 0.0)
            dbt_ref[...] += jnp.where(lane == g, db_acc, 0.0)
            dvn = jnp.concatenate(dvn_parts, axis=0)
            dlg_ref[:, gc] += jnp.sum(dvn * vhat, axis=0, keepdims=True)
            dlb_ref[:, gc] += jnp.sum(dvn, axis=0, keepdims=True)
            dvh = dvn * lg_ref[:, gc]
            dvf = rstd * (dvh - jnp.mean(dvh, axis=-1, keepdims=True)
                          - vhat * jnp.mean(dvh * vhat, axis=-1, keepdims=True))
            dp_ref[:, gc_v] = (dvf * dvf_dsv).astype(BF16)

    blk = lambda j: pl.BlockSpec((tT, 1024), lambda i: (i, j))
    row = lambda w: pl.BlockSpec((tT, w), lambda i: (i, 0))
    vec = pl.BlockSpec((1, 1024), lambda i: (0, 0))
    return _call(
        body, name="sgu_bwd", grid=(T // tT,),
        in_specs=[blk(3), blk(4), row(1024), _whole(), _whole(), _whole(), _whole()],
        out_specs=[row(W_SGU), pl.BlockSpec((SGU_GROUPS, SGU_BLOCK, SGU_BLOCK), lambda i: (0, 0, 0)),
                   pl.BlockSpec((SGU_BLOCK, LANES), lambda i: (0, 0)), vec, vec],
        out_shape=[_sds((T, W_SGU), BF16), _sds((SGU_GROUPS, SGU_BLOCK, SGU_BLOCK), F32), _sds((SGU_BLOCK, LANES), F32),
                   _sds((1, 1024), F32), _sds((1, 1024), F32)],
        args=(proj, proj, dy_sgu, ln_g, ln_b, w_sp, b_sp_t), job=job)


def _gla_bwd(proj, alow, wgu, b_gate, gn, states, dy_gla, job=None):
    T = proj.shape[0]
    tT = _row_tile(T, 512)
    nc = tT // CHUNK
    nt = T // tT

    def body(q_ref, k_ref, v_ref, r_ref, al_ref, wgu_ref, bg_ref, gn_ref, later_ref, earlier_ref, st_ref, sp_ref, dy_ref,
             dp_ref, dal_ref, dgn_ref, dbg_ref, dwgu_ref, g_scr, dd_scr, dt_scr):
        step = pl.program_id(0)

        @pl.when(step == 0)
        def _():
            g_scr[...] = jnp.zeros_like(g_scr)
            dgn_ref[...] = jnp.zeros_like(dgn_ref)
            dbg_ref[...] = jnp.zeros_like(dbg_ref)
            dwgu_ref[...] = jnp.zeros_like(dwgu_ref)

        has_prev = jnp.where(step == nt - 1, 0.0, 1.0)
        logit, la, delta = _gla_decay_terms(al_ref, wgu_ref, bg_ref, later_ref)
        e = jnp.exp(delta)
        kdec_f = k_ref[...].astype(F32) * e
        kdec = kdec_f.astype(BF16)
        heads = range(GLA_HEADS)
        kcs = [slice(h * GLA_DK, (h + 1) * GLA_DK) for h in heads]
        vcs = [slice(h * GLA_DV, (h + 1) * GLA_DV) for h in heads]
        carry = [g_scr[h] for h in heads]
        dgn_acc = [jnp.zeros((1, GLA_DV), F32) for _ in heads]
        for c in reversed(range(nc)):
            rows = slice(c * CHUNK, (c + 1) * CHUNK)
            first = slice(c * CHUNK, c * CHUNK + 1)
            dec = jnp.exp(la[first, :] + delta[first, :])
            s_b = [st_ref[c, h].astype(BF16) for h in heads]
            qs = [(q_ref[rows, kcs[h]].astype(F32) * (GLA_DK ** -0.5)).astype(BF16) for h in heads]
            o = [_dot(qs[h], s_b[h], _NT) for h in heads]
            do = []
            for h in heads:
                rstd = _rms_stats(o[h])
                ohat = o[h] * rstd
                gnh = gn_ref[:, vcs[h]]
                dy = dy_ref[rows, vcs[h]].astype(F32)
                rr = r_ref[rows, vcs[h]].astype(F32)
                sg = _sigmoid(rr)
                don = dy * (rr * sg)
                dp_ref[rows, OFF_R + h * GLA_DV:OFF_R + (h + 1) * GLA_DV] = (
                    dy * (ohat * gnh) * (sg * (1.0 + rr * (1.0 - sg)))).astype(BF16)
                dgn_acc[h] = dgn_acc[h] + jnp.sum(don * ohat, axis=0, keepdims=True)
                dn = don * gnh
                do.append((rstd * (dn - ohat * jnp.mean(dn * ohat, axis=-1, keepdims=True))).astype(BF16))
            dq = [_dot(do[h], s_b[h]) for h in heads]
            g_t = [_dot(do[h], qs[h], _TN) + carry[h] for h in heads]
            g_b = [g_t[h].astype(BF16) for h in heads]
            dv = [_dot(kdec[rows, kcs[h]], g_b[h], _NT) for h in heads]
            dkdec = [_dot(v_ref[rows, vcs[h]], g_b[h]) for h in heads]
            for h in heads:
                s_prev = st_ref[c - 1, h] if c > 0 else sp_ref[0, h] * has_prev
                ddec = jnp.sum(g_t[h] * s_prev, axis=0, keepdims=True)
                carry[h] = g_t[h] * dec[:, kcs[h]]
                dp_ref[rows, OFF_Q + h * GLA_DK:OFF_Q + (h + 1) * GLA_DK] = (dq[h] * (GLA_DK ** -0.5)).astype(BF16)
                dp_ref[rows, OFF_V + h * GLA_DV:OFF_V + (h + 1) * GLA_DV] = dv[h].astype(BF16)
                dp_ref[rows, OFF_K + h * GLA_DK:OFF_K + (h + 1) * GLA_DK] = (dkdec[h] * e[rows, kcs[h]]).astype(BF16)
                dd_scr[rows, kcs[h]] = dkdec[h] * kdec_f[rows, kcs[h]]
                dt_scr[rows, kcs[h]] = jnp.broadcast_to(ddec * dec[:, kcs[h]], (CHUNK, GLA_DK))
        for h in heads:
            g_scr[h] = carry[h]
            dgn_ref[:, vcs[h]] += dgn_acc[h]
        dla = _dot_exact_lhs(earlier_ref[...], dd_scr[...]) + dt_scr[...]
        dlogit = dla * (1.0 / GLA_TAU) * _sigmoid(-logit)
        dbg_ref[...] += jnp.sum(dlogit, axis=0, keepdims=True)
        dwgu_ref[...] += _dot_bf16(al_ref[...], dlogit, _TN)
        dal_ref[...] = _dot_bf16(dlogit, wgu_ref[...], _NT).astype(BF16)

    rev = lambda i: nt - 1 - i
    blk = lambda w, j: pl.BlockSpec((tT, w), lambda i: (rev(i), j))
    st_blk = pl.BlockSpec((nc, GLA_HEADS, GLA_DV, GLA_DK), lambda i: (rev(i), 0, 0, 0))
    sp_blk = pl.BlockSpec((1, GLA_HEADS, GLA_DV, GLA_DK), lambda i: (jnp.maximum(rev(i) * nc - 1, 0), 0, 0, 0))
    return _call(
        body, name="gla_bwd", grid=(nt,),
        in_specs=[blk(512, 0), blk(512, 1), blk(1024, 1), blk(1024, 2), blk(LANES, 0)] + [_whole()] * 5
        + [st_blk, sp_blk, blk(GLA_V, 0)],
        out_specs=[blk(W_GLA, 0), blk(LANES, 0), pl.BlockSpec((1, GLA_V), lambda i: (0, 0)),
                   pl.BlockSpec((1, GLA_QK), lambda i: (0, 0)), pl.BlockSpec((LANES, GLA_QK), lambda i: (0, 0))],
        out_shape=[_sds((T, W_GLA), BF16), _sds((T, LANES), BF16), _sds((1, GLA_V), F32), _sds((1, GLA_QK), F32),
                   _sds((LANES, GLA_QK), F32)],
        scratch_shapes=[pltpu.VMEM((GLA_HEADS, GLA_DV, GLA_DK), F32), pltpu.VMEM((tT, GLA_QK), F32),
                        pltpu.VMEM((tT, GLA_QK), F32)],
        args=(proj, proj, proj, proj, alow, wgu, b_gate, gn, _chunk_masks(tT, upper=True), _chunk_masks(tT, upper=False),
              states, states, dy_gla), job=job)


def _inproj_bwd(x, dx1, g1, w_all, dparts, job=None):
    T = x.shape[0]
    tT = _row_tile(T, 512)
    offs = (0, W_GLA, W_GLA + W_SGU, N_MAIN)

    def body(x_ref, dx1_ref, g_ref, w_ref, *rest):
        part_refs, (dx_ref, dg_ref) = rest[:len(offs)], rest[len(offs):]

        @pl.when(pl.program_id(0) == 0)
        def _():
            dg_ref[...] = jnp.zeros_like(dg_ref)

        da = jnp.zeros((tT, D_MODEL), F32)
        for off, p_ref in zip(offs, part_refs):
            da = da + _dot(p_ref[...], w_ref[:, off:off + p_ref.shape[1]], _NT)
        xv = x_ref[...]
        dx, dg = _rms_bwd(da, xv, _rms_stats(xv), g_ref[...])
        dg_ref[...] += jnp.sum(dg, axis=0, keepdims=True)
        dx_ref[...] = dx1_ref[...] + dx

    row = lambda w: pl.BlockSpec((tT, w), lambda i: (i, 0))
    vec = pl.BlockSpec((1, D_MODEL), lambda i: (0, 0))
    return _call(
        body, name="inproj_bwd", grid=(T // tT,),
        in_specs=[row(D_MODEL), row(D_MODEL), vec, _whole()] + [row(p.shape[1]) for p in dparts],
        out_specs=[row(D_MODEL), vec], out_shape=[_sds((T, D_MODEL), F32), _sds((1, D_MODEL), F32)],
        args=(x, dx1, g1, w_all, *dparts), job=job)


def _tn_matmul(a, b, name, job=None):
    T, M = a.shape
    N = b.shape[1]
    tk = _row_tile(T, 1024)
    tm = M if M <= 1024 else 1408
    tn = N if N <= 3072 else N // 2
    assert M % tm == 0 and N % tn == 0

    def body(a_ref, b_ref, o_ref):
        @pl.when(pl.program_id(2) == 0)
        def _():
            o_ref[...] = _dot(a_ref[...], b_ref[...], _TN)

        @pl.when(pl.program_id(2) > 0)
        def _():
            o_ref[...] += _dot(a_ref[...], b_ref[...], _TN)

    res, jres = _call(
        body, name=name, grid=(M // tm, N // tn, T // tk),
        in_specs=[pl.BlockSpec((tk, tm), lambda i, j, k: (k, i)), pl.BlockSpec((tk, tn), lambda i, j, k: (k, j))],
        out_specs=[pl.BlockSpec((tm, tn), lambda i, j, k: (i, j))], out_shape=[_sds((M, N), F32)], args=(a, b), job=job)
    return res[0], jres


def _pad_rows(a, rows=8):
    return jnp.pad(a, ((0, rows - a.shape[0]), (0, LANES - a.shape[1])))


def _halves_view(dw):
    r = dw.shape[0] // N_CHIPS
    return dw.reshape(N_CHIPS, 2, r // 2, dw.shape[1])


def kernel(x, norm_pre_mix, w_in, w_gate_up, b_gate, gla_norm, sgu_ln_g, sgu_ln_b, w_spatial, b_spatial, w_branch_gla, w_branch_sgu, w_out, norm_post_mix, norm_pre_ffn, w_ffn_in, w_ffn_out, norm_post_ffn, loss_target, m_norm_pre_mix, m_w_in, m_w_gate_up, m_b_gate, m_gla_norm, m_sgu_ln_g, m_sgu_ln_b, m_w_spatial, m_b_spatial, m_w_branch_gla, m_w_branch_sgu, m_w_out, m_norm_post_mix, m_norm_pre_ffn, m_w_ffn_in, m_w_ffn_out, m_norm_post_ffn, v_norm_pre_mix, v_w_in, v_w_gate_up, v_b_gate, v_gla_norm, v_sgu_ln_g, v_sgu_ln_b, v_w_spatial, v_b_spatial, v_w_branch_gla, v_w_branch_sgu, v_w_out, v_norm_post_mix, v_norm_pre_ffn, v_w_ffn_in, v_w_ffn_out, v_norm_post_ffn):
    chip = 2 * lax.axis_index("x") + lax.axis_index("y")
    xt, tgt = x[0], loss_target[0]

    tiny = jnp.concatenate([w_gate_up[0], _pad_rows(gla_norm[0]), _pad_rows(sgu_ln_g[0]), _pad_rows(sgu_ln_b[0]),
                            jnp.zeros((24, LANES), F32)], axis=0)

    def with_own(gathered, own):
        return lax.dynamic_update_slice(gathered, own[None], (chip, 0, 0))

    w_in_t, m_in_t, v_in_t = w_in[0].T, m_w_in[0].T, v_w_in[0].T
    w_in_b = _transposed_cast(w_in_t)
    g_in, g_tiny = _run_job(_job_gather([w_in_b, tiny]), "gather_w_in")
    g_tiny = with_own(g_tiny, tiny)
    w_all = _relayout_w_in(with_own(g_in, w_in_b))
    cols = lambda a: a.transpose(1, 0, 2).reshape(a.shape[1], N_CHIPS * a.shape[2])
    wgu = jnp.pad(cols(g_tiny[:, 0:16]), ((0, LANES - GLA_RANK), (0, 0)))
    gn = cols(g_tiny[:, 16:20, :64]).reshape(1, GLA_V)
    ln_g = cols(g_tiny[:, 24:28, :64]).reshape(1, 1024)
    ln_b = cols(g_tiny[:, 32:36, :64]).reshape(1, 1024)
    b_sp_t = jnp.pad(b_spatial[0].T, ((0, 0), (0, LANES - SGU_GROUPS)))
    w_sp = w_spatial[0]

    own_rows = [w_branch_gla[0].astype(BF16), w_branch_sgu[0].astype(BF16), w_out[0].astype(BF16), w_ffn_out[0].astype(BF16)]
    (a, proj, alow), g_rows = _inproj_fwd(xt, norm_pre_mix, w_all, job=_job_gather(own_rows))
    rows = lambda g: g.reshape(N_CHIPS * g.shape[1], g.shape[2])
    w_bg, w_bs, w_o, w_fo = [rows(with_own(g, own)) for g, own in zip(g_rows, own_rows)]
    w_fi_b = w_ffn_in[0].astype(BF16)
    fi_top, fi_bot = w_fi_b[:D_MODEL // 2], w_fi_b[D_MODEL // 2:]
    (y_gla, states), (g_top,) = _gla_fwd(proj, alow, wgu, b_gate, gn, job=_job_gather([fi_top]))
    y_sgu = _sgu_fwd(proj, ln_g, ln_b, w_sp, b_sp_t)
    (zg, zs, merged, mix, x1), (g_bot,) = _merge_fwd(xt, proj, y_gla, y_sgu, w_bg, w_bs, w_o, norm_post_mix,
                                                     job=_job_gather([fi_bot]))
    h, f, dgu, dy, dx1, loss, d_gpf, d_gpo = _ffn_fwd_bwd(x1, tgt, with_own(g_top, fi_top), with_own(g_bot, fi_bot),
                                                          w_fo, norm_pre_ffn, norm_post_ffn)

    own_part = lambda c: lax.dynamic_index_in_dim(c, chip, 0, keepdims=False)
    whole = lambda hs: [[(h_, None)] for h_ in hs]
    dw_fo, _ = _tn_matmul(f, dy, "dw_ffn_out")
    dw_fo4 = _halves_view(dw_fo)
    dw_fi, (q_fo,) = _tn_matmul(h, dgu, "dw_ffn_in", job=_job_to_other_core([[(dw_fo4, 0)]]))
    c_fo = _presum(dw_fo4, q_fo, "presum_ffn_out")
    (dmix, dzg, dzs, dp_mrg, dyg, dys, d_gpm), (s_fo, q_fi) = _merge_bwd(
        dx1, mix, proj, zg, zs, w_bg, w_bs, w_o, norm_post_mix,
        job=_join(_job_scatter([c_fo]), _job_to_other_core([[(dw_fi, 0)]])))
    c_fi = _presum(dw_fi, q_fi, "presum_ffn_in")
    dw_c, _ = _tn_matmul(a, dp_mrg, "dw_in_merge")
    dw_o4 = _halves_view(_tn_matmul(merged, dmix, "dw_out")[0])
    dw_bg4 = _halves_view(_tn_matmul(y_gla, dzg, "dw_branch_gla")[0])
    dw_bs4 = _halves_view(_tn_matmul(y_sgu, dzs, "dw_branch_sgu")[0])
    (dp_sgu, d_wsp, d_bsp_t, d_lng, d_lnb), (q_o, q_bg, q_bs, q_c) = _sgu_bwd(
        proj, dys, ln_g, ln_b, w_sp, b_sp_t,
        job=_job_to_other_core([[(dw_o4, 0)], [(dw_bg4, 0)], [(dw_bs4, 0)], [(dw_c, 0)]]))
    c_o, c_bg, c_bs = (_presum(dw_o4, q_o, "presum_out"), _presum(dw_bg4, q_bg, "presum_branch_gla"),
                       _presum(dw_bs4, q_bs, "presum_branch_sgu"))
    h_fo = _sum_slots(own_part(c_fo), s_fo, "sum_ffn_out")
    dw_b, _ = _tn_matmul(a, dp_sgu, "dw_in_sgu")
    (dp_gla, dal, d_gn, d_bg, d_wgu), (s_fi, t_fo, q_b) = _gla_bwd(
        proj, alow, wgu, b_gate, gn, states, dyg,
        job=_join(_job_scatter([c_fi]), _job_to_other_core(whole([h_fo]) + [[(dw_b, 0)]])))
    h_fi = _sum_slots(own_part(c_fi), s_fi, "sum_ffn_in")
    dw_d, _ = _tn_matmul(a, dal, "dw_in_gate")
    dw_a, (s_o, s_bg, s_bs, t_fi, q_d) = _tn_matmul(
        a, dp_gla, "dw_in_gla",
        job=_join(_job_scatter([c_o, c_bg, c_bs]), _job_to_other_core(whole([h_fi]) + [[(dw_d, 0)]])))
    h_o, h_bg, h_bs = (_sum_slots(own_part(c_o), s_o, "sum_out"), _sum_slots(own_part(c_bg), s_bg, "sum_branch_gla"),
                       _sum_slots(own_part(c_bs), s_bs, "sum_branch_sgu"))

    grads, deltas, new_m, new_v = {}, {}, {}, {}

    def update(name, w, m, v, g_mine, g_theirs, job=None):
        (g, d, m2, v2), jres = _adamw(w[0], m[0], v[0], g_mine, g_theirs, "adamw_" + name, job=job)
        grads[name], deltas[name], new_m[name], new_v[name] = g[None], d[None], m2[None], v2[None]
        return jres

    dw_in = [(dw_a, 0), (dw_b, W_GLA), (dw_c, W_GLA + W_SGU), (dw_d, N_MAIN)]
    q_a, t_o, t_bg, t_bs = update("w_ffn_out", w_ffn_out, m_w_ffn_out, v_w_ffn_out, [h_fo], [t_fo],
                                  job=_job_to_other_core([[(dw_a, 0)]] + whole([h_o, h_bg, h_bs])))
    q_in = [q_a, q_b, q_c, q_d]
    hr_in = D_MODEL // 2
    c_in_a, _ = _presum_w_in(dw_in, q_in, 0, hr_in // 8, "presum_w_in_a")
    c_in_b, (s_in_a,) = _presum_w_in(dw_in, q_in, hr_in // 8, 7 * hr_in // 8, "presum_w_in_b",
                                     job=_job_scatter([c_in_a]))
    update("w_ffn_in", w_ffn_in, m_w_ffn_in, v_w_ffn_in, [h_fi], [t_fi])
    update("w_out", w_out, m_w_out, v_w_out, [h_o], [t_o])
    update("w_branch_gla", w_branch_gla, m_w_branch_gla, v_w_branch_gla, [h_bg], [t_bg])
    update("w_branch_sgu", w_branch_sgu, m_w_branch_sgu, v_w_branch_sgu, [h_bs], [t_bs])
    (grad_x, d_g1), (s_in_b,) = _inproj_bwd(xt, dx1, norm_pre_mix, w_all, (dp_gla, dp_sgu, dp_mrg, dal),
                                            job=_job_scatter([c_in_b]))
    h_in = [_sum_slots(own_part(c_in_a), s_in_a, "sum_w_in_a"), _sum_slots(own_part(c_in_b), s_in_b, "sum_w_in_b")]
    t_in = _run_job(_job_to_other_core(whole(h_in)), "swap_w_in")
    for store, val in zip((grads, deltas, new_m, new_v),
                          _adamw_transposed(w_in_t, m_in_t, v_in_t, h_in, t_in, "adamw_w_in")):
        store["w_in"] = val.T[None]

    small_names = ["w_spatial", "w_gate_up", "norm_pre_mix", "norm_post_mix", "norm_pre_ffn", "norm_post_ffn", "b_gate",
                   "b_spatial", "gla_norm", "sgu_ln_g", "sgu_ln_b"]
    loss_out, small = _small_adamw(
        _small_sum([d_wsp, d_wgu, d_g1, d_gpm, d_gpf, d_gpo, d_bg, d_bsp_t, d_gn, d_lng, d_lnb, loss]),
        [w_spatial, w_gate_up, norm_pre_mix, norm_post_mix, norm_pre_ffn, norm_post_ffn, b_gate, b_spatial, gla_norm,
         sgu_ln_g, sgu_ln_b],
        [m_w_spatial, m_w_gate_up, m_norm_pre_mix, m_norm_post_mix, m_norm_pre_ffn, m_norm_post_ffn, m_b_gate,
         m_b_spatial, m_gla_norm, m_sgu_ln_g, m_sgu_ln_b],
        [v_w_spatial, v_w_gate_up, v_norm_pre_mix, v_norm_post_mix, v_norm_pre_ffn, v_norm_post_ffn, v_b_gate,
         v_b_spatial, v_gla_norm, v_sgu_ln_g, v_sgu_ln_b])
    for store, vals in zip((grads, deltas, new_m, new_v), small):
        store.update(zip(small_names, vals))

    order = ["norm_pre_mix", "w_in", "w_gate_up", "b_gate", "gla_norm", "sgu_ln_g", "sgu_ln_b", "w_spatial", "b_spatial",
             "w_branch_gla", "w_branch_sgu", "w_out", "norm_post_mix", "norm_pre_ffn", "w_ffn_in", "w_ffn_out",
             "norm_post_ffn"]
    out = [loss_out, grad_x[None]]
    for store in (grads, deltas, new_m, new_v):
        out.extend(store[n] for n in order)
    return tuple(out)
```

```python
import jax
import jax.numpy as jnp
from jax import lax
from jax.experimental import pallas as pl
from jax.experimental.pallas import tpu as pltpu
from jax.experimental.pallas import tpu_sc as plsc

F32 = jnp.float32
BF16 = jnp.bfloat16

D_MODEL = 1024
GLA_HEADS = 4
GLA_DK = 128
GLA_DV = 256
GLA_QK = GLA_HEADS * GLA_DK
GLA_V = GLA_HEADS * GLA_DV
GLA_RANK = 16
GLA_TAU = 16.0
CHUNK = 64
SGU_GROUPS = 4
SGU_BLOCK = 128
SGU_DG = 256
D_FF = 2816
EPS = 1e-6
LANES = 128

OFF_Q, OFF_K, OFF_V, OFF_R, OFF_SU, OFF_SV, OFF_GG, OFF_GS, OFF_AL = 0, 512, 1024, 2048, 3072, 4096, 5120, 6144, 7168
W_GLA, W_SGU, W_MRG = 3072, 2048, 2048
N_MAIN = 7168
N_ALL = N_MAIN + LANES
_IN_SPLITS = (GLA_QK, GLA_QK, GLA_V, GLA_V, GLA_RANK, 1024, 1024, 1024, 1024)
_IN_STARTS = tuple(sum(_IN_SPLITS[:i]) for i in range(len(_IN_SPLITS) + 1))
_IN_DST = (OFF_Q, OFF_K, OFF_V, OFF_R, OFF_AL, OFF_SU, OFF_SV, OFF_GG, OFF_GS)
D_IN = _IN_STARTS[-1]

ADAM_LR = 0.001
ADAM_B1 = 0.9
ADAM_B2 = 0.999
ADAM_EPS = 1e-08
ADAM_WD = 0.01
ADAM_STEP = 10

VMEM_LIMIT_BYTES = 56 * 1024 * 1024
N_CHIPS = 4
N_PEER = N_CHIPS - 1
N_DEV = 8
MESH = pl.DeviceIdType.MESH

_NN = (((1,), (0,)), ((), ()))
_NT = (((1,), (1,)), ((), ()))
_TN = (((0,), (0,)), ((), ()))


def _dot(a, b, dims=_NN):
    return lax.dot_general(a, b, dims, preferred_element_type=F32)


def _split(x):
    hi = x.astype(BF16)
    lo = (x - hi.astype(F32)).astype(BF16)
    return hi, lo


def _dot_bf16(a, b, dims=_NN):
    return _dot(a.astype(BF16), b.astype(BF16), dims)


def _dot_exact_lhs(m, x):
    xh, xl = _split(x)
    return _dot(m, xh) + _dot(m, xl)


def _sigmoid(x):
    return 0.5 * jnp.tanh(0.5 * x) + 0.5


def _log_sigmoid(x):
    return jnp.minimum(x, 0.0) - jnp.log(1.0 + jnp.exp(-jnp.abs(x)))


_GELU_C = 0.7978845608028654
_GELU_A = 0.044715


def _gelu_and_grad(x):
    x2 = x * x
    t = jnp.tanh(_GELU_C * (x + _GELU_A * x * x2))
    g = 0.5 * x * (1.0 + t)
    dg = 0.5 * (1.0 + t) + 0.5 * x * (1.0 - t * t) * (_GELU_C * (1.0 + 3.0 * _GELU_A * x2))
    return g, dg


def _gelu(x):
    t = jnp.tanh(_GELU_C * (x + _GELU_A * x * x * x))
    return 0.5 * x * (1.0 + t)


def _rms_stats(x):
    return lax.rsqrt(jnp.mean(x * x, axis=-1, keepdims=True) + EPS)


def _rms_bwd(dout, y, r, g):
    yhat = y * r
    dn = dout * g
    dy = r * (dn - yhat * jnp.mean(dn * yhat, axis=-1, keepdims=True))
    return dy, dout * yhat


def _whole():
    return pl.BlockSpec(memory_space=pltpu.VMEM)


def _row_tile(T, want):
    t = min(T, want)
    assert T % t == 0
    return t


def _chunk_masks(tT, upper):
    row = lax.broadcasted_iota(jnp.int32, (tT, tT), 0)
    col = lax.broadcasted_iota(jnp.int32, (tT, tT), 1)
    same = (row // CHUNK) == (col // CHUNK)
    tri = (col > row) if upper else (col < row)
    return jnp.where(same & tri, 1.0, 0.0).astype(BF16)


class _Job:
    def __init__(self, ins, out_shapes, scratch, start, finish, mid=None):
        self.ins, self.out_shapes, self.scratch = list(ins), list(out_shapes), list(scratch)
        self.start, self.finish, self.mid = start, finish, mid


def _join(*jobs):
    def split(refs, counts):
        out, at = [], 0
        for n in counts:
            out.append(refs[at:at + n])
            at += n
        return out

    ni, no, ns = [len(j.ins) for j in jobs], [len(j.out_shapes) for j in jobs], [len(j.scratch) for j in jobs]

    def start(ins, outs, scr):
        for j, a, b, c in zip(jobs, split(ins, ni), split(outs, no), split(scr, ns)):
            j.start(a, b, c)

    def finish(ins, outs, scr):
        for j, a, b, c in zip(jobs, split(ins, ni), split(outs, no), split(scr, ns)):
            j.finish(a, b, c)

    def mid(ins, outs, scr):
        for j, a, b, c in zip(jobs, split(ins, ni), split(outs, no), split(scr, ns)):
            if j.mid is not None:
                j.mid(a, b, c)

    return _Job(sum((j.ins for j in jobs), []), sum((j.out_shapes for j in jobs), []),
                sum((j.scratch for j in jobs), []), start, finish, mid if any(j.mid for j in jobs) else None)


def _mesh_pos():
    return lax.axis_index("x"), lax.axis_index("y"), lax.axis_index("c")


def _peer_chips(xi, yi):
    return [(1 - xi, yi), (xi, 1 - yi), (1 - xi, 1 - yi)]


def _half(ci, rows):
    return pl.ds(pl.multiple_of(ci * rows, 8), rows)


def _sds(shape, dtype):
    return jax.ShapeDtypeStruct(tuple(shape), dtype)


def _job_gather(arrs):
    n = len(arrs)
    kinds = 12
    Y0, Y1, X1, X0, ON_X, ON_Y, D2D = 0, 1, 2, 3, 4, 5, 6

    def copies(ins, outs, scr):
        send_sems, recv_sems = scr
        xi, yi, ci = _mesh_pos()
        me, cx, cy, cd = 2 * xi + yi, 2 * (1 - xi) + yi, 2 * xi + (1 - yi), 2 * (1 - xi) + (1 - yi)
        to_x, to_y, to_core = (1 - xi, yi, ci), (xi, 1 - yi, ci), (xi, yi, 1 - ci)
        table = []
        for k in range(n):
            qr = arrs[k].shape[0] // 4

            def rows(core, q):
                return pl.ds(pl.multiple_of((2 * core + q) * qr, 8), qr)

            def cp(kind, src, dst, to):
                s = k * kinds + kind
                return pltpu.make_async_remote_copy(src_ref=src, dst_ref=dst, send_sem=send_sems.at[s],
                                                    recv_sem=recv_sems.at[s], device_id=to, device_id_type=MESH)

            def slab(chip, core, q):
                return outs[k].at[chip, rows(core, q)]

            t = {}
            for kind, q, to, frm in ((Y0, 0, to_y, cy), (Y1, 1, to_y, cy), (X1, 1, to_x, cx), (X0, 0, to_x, cx)):
                mine = ins[k].at[rows(ci, q)]
                t[kind] = (cp(kind, mine, slab(me, ci, q), to), cp(kind, mine, slab(frm, ci, q), to))
            t[ON_X] = (cp(ON_X, slab(cy, ci, 0), slab(cy, ci, 0), to_x), cp(ON_X, slab(cy, ci, 0), slab(cd, ci, 0), to_x))
            t[ON_Y] = (cp(ON_Y, slab(cx, ci, 1), slab(cx, ci, 1), to_y), cp(ON_Y, slab(cx, ci, 1), slab(cd, ci, 1), to_y))
            for i, (chip, q) in enumerate(((cy, 0), (cy, 1), (cx, 1), (cx, 0), (cd, 0), (cd, 1))):
                t[D2D + i] = (cp(D2D + i, slab(chip, ci, q), slab(chip, ci, q), to_core),
                              cp(D2D + i, slab(chip, ci, q), slab(chip, 1 - ci, q), to_core))
            table.append(t)
        return table

    def start(ins, outs, scr):
        table = copies(ins, outs, scr)
        for kind in (Y0, X1, Y1, X0):
            for t in table:
                t[kind][0].start()

    def arrived(table, kind, then):
        for t in table:
            t[kind][1].wait_recv()
            for nxt in then:
                t[nxt][0].start()

    def mid(ins, outs, scr):
        table = copies(ins, outs, scr)
        arrived(table, Y0, (ON_X, D2D + 0))
        arrived(table, X1, (ON_Y, D2D + 2))

    def finish(ins, outs, scr):
        table = copies(ins, outs, scr)
        arrived(table, Y1, (D2D + 1,))
        arrived(table, X0, (D2D + 3,))
        arrived(table, ON_X, (D2D + 4,))
        arrived(table, ON_Y, (D2D + 5,))
        for t in table:
            for i in range(6):
                t[D2D + i][1].wait_recv()
            for kind in range(kinds):
                t[kind][0].wait_send()

    dma = pltpu.SemaphoreType.DMA
    return _Job(arrs, [_sds((N_CHIPS,) + a.shape, a.dtype) for a in arrs], [dma((n * kinds,))] * 2, start, finish, mid)


def _job_scatter(parts):
    n = len(parts)

    def copies(ins, outs, scr):
        send_sems, recv_sems = scr
        xi, yi, ci = _mesh_pos()
        res = []
        for k in range(n):
            for j, (px, py) in enumerate(_peer_chips(xi, yi)):
                s = k * N_PEER + j
                res.append(pltpu.make_async_remote_copy(
                    src_ref=ins[k].at[2 * px + py], dst_ref=outs[k].at[j], send_sem=send_sems.at[s],
                    recv_sem=recv_sems.at[s], device_id=(px, py, ci), device_id_type=MESH))
        return res

    def start(ins, outs, scr):
        for cp in copies(ins, outs, scr):
            cp.start()

    def finish(ins, outs, scr):
        for cp in copies(ins, outs, scr):
            cp.wait_recv()
            cp.wait_send()

    dma = pltpu.SemaphoreType.DMA
    return _Job(parts, [_sds((N_PEER,) + p.shape[1:], p.dtype) for p in parts], [dma((n * N_PEER,))] * 2, start, finish)


def _job_to_other_core(groups):
    pieces = [(g, a, off) for g, group in enumerate(groups) for a, off in group]
    n = len(pieces)

    def geometry(group):
        a0, off0 = group[0]
        if off0 is None:
            return a0.shape
        if a0.ndim == 4:
            return (N_CHIPS, a0.shape[2], a0.shape[3])
        return (a0.shape[0] // 2, sum(a.shape[1] for a, _ in group))

    def copies(ins, outs, scr):
        send_sems, recv_sems = scr
        xi, yi, ci = _mesh_pos()
        res = []
        for p, (g, a, off) in enumerate(pieces):
            if off is None:
                give, land = ins[p], outs[g]
            elif a.ndim == 4:
                give, land = ins[p].at[pl.ds(0, N_CHIPS), 1 - ci], outs[g]
            else:
                hr, w = a.shape[0] // 2, a.shape[1]
                give, land = ins[p].at[_half(1 - ci, hr)], outs[g].at[pl.ds(0, hr), pl.ds(off, w)]
            res.append(pltpu.make_async_remote_copy(
                src_ref=give, dst_ref=land, send_sem=send_sems.at[p], recv_sem=recv_sems.at[p],
                device_id=(xi, yi, 1 - ci), device_id_type=MESH))
        return res

    def start(ins, outs, scr):
        for cp in copies(ins, outs, scr):
            cp.start()

    def finish(ins, outs, scr):
        for cp in copies(ins, outs, scr):
            cp.wait_recv()
            cp.wait_send()

    dma = pltpu.SemaphoreType.DMA
    return _Job([a for _, a, _ in pieces], [_sds(geometry(group), group[0][0].dtype) for group in groups],
                [dma((n,))] * 2, start, finish)


def _call(body, *, name, grid, in_specs, out_specs, out_shape, args, scratch_shapes=(), parallel=False, job=None,
          by_core=False):
    n_in, n_out, n_scr = len(in_specs), len(out_specs), len(scratch_shapes)
    hbm = pl.BlockSpec(memory_space=pl.ANY)
    n_ji, n_jo = (len(job.ins), len(job.out_shapes)) if job is not None else (0, 0)
    lead = 1 if by_core else 0

    def kernel_fn(*refs):
        core, refs = refs[:lead], refs[lead:]
        ins, refs = refs[:n_in], refs[n_in:]
        j_ins, refs = refs[:n_ji], refs[n_ji:]
        outs, refs = refs[:n_out], refs[n_out:]
        j_outs, refs = refs[:n_jo], refs[n_jo:]
        scr, j_scr = refs[:n_scr], refs[n_scr:]
        if job is None:
            body(*core, *ins, *outs, *scr)
            return
        ids = [pl.program_id(d) for d in range(len(grid))]
        first = ids[0] == 0
        last = ids[0] == grid[0] - 1
        for d in range(1, len(grid)):
            first = first & (ids[d] == 0)
            last = last & (ids[d] == grid[d] - 1)

        @pl.when(first)
        def _():
            job.start(j_ins, j_outs, j_scr)

        if job.mid is not None and grid[0] >= 4:
            half_way = ids[0] == grid[0] // 2
            for d in range(1, len(grid)):
                half_way = half_way & (ids[d] == 0)

            @pl.when(half_way)
            def _():
                job.mid(j_ins, j_outs, j_scr)

        body(*core, *ins, *outs, *scr)

        @pl.when(last)
        def _():
            if job.mid is not None and grid[0] < 4:
                job.mid(j_ins, j_outs, j_scr)
            job.finish(j_ins, j_outs, j_scr)

    sem = ("parallel" if parallel and job is None else "arbitrary",) * len(grid)
    all_in = list(in_specs) + [hbm] * n_ji
    all_out = list(out_specs) + [hbm] * n_jo
    all_scratch = list(scratch_shapes) + (job.scratch if job is not None else [])
    all_shapes = list(out_shape) + (job.out_shapes if job is not None else [])
    all_args = list(args) + (job.ins if job is not None else [])
    params = pltpu.CompilerParams(dimension_semantics=sem, vmem_limit_bytes=VMEM_LIMIT_BYTES)
    if by_core:
        spec = pltpu.PrefetchScalarGridSpec(num_scalar_prefetch=1, grid=grid, in_specs=all_in, out_specs=all_out,
                                            scratch_shapes=all_scratch)
        core = lax.axis_index("c").astype(jnp.int32).reshape(1)
        res = pl.pallas_call(kernel_fn, name=name, grid_spec=spec, out_shape=all_shapes, compiler_params=params)(
            core, *all_args)
    else:
        res = pl.pallas_call(kernel_fn, name=name, grid=grid, in_specs=all_in, out_specs=all_out, out_shape=all_shapes,
                             scratch_shapes=all_scratch, compiler_params=params)(*all_args)
    return list(res[:n_out]), list(res[n_out:])


def _run_job(job, name):
    n_i, n_o = len(job.ins), len(job.out_shapes)

    def body(*refs):
        ins, outs, scr = refs[:n_i], refs[n_i:n_i + n_o], refs[n_i + n_o:]
        job.start(ins, outs, scr)
        if job.mid is not None:
            job.mid(ins, outs, scr)
        job.finish(ins, outs, scr)

    hbm = pl.BlockSpec(memory_space=pl.ANY)
    return list(pl.pallas_call(body, name=name, in_specs=[hbm] * n_i, out_specs=[hbm] * n_o, out_shape=job.out_shapes,
                               scratch_shapes=job.scratch)(*job.ins))


def _adam_values(w, m, v, g):
    m2 = ADAM_B1 * m + (1.0 - ADAM_B1) * g
    v2 = ADAM_B2 * v + (1.0 - ADAM_B2) * (g * g)
    delta = -ADAM_LR * ((m2 / (1.0 - ADAM_B1 ** ADAM_STEP)) / (jnp.sqrt(v2 / (1.0 - ADAM_B2 ** ADAM_STEP)) + ADAM_EPS)
                        + ADAM_WD * w)
    return delta, m2, v2


_P_WSP, _P_WGU, _P_NORM, _P_BG, _P_BSP, _P_HEAD, _P_LOSS, _P_ROWS = 0, 512, 576, 608, 616, 624, 720, 728


def _small_sum(dgrads):
    def body(dwsp, dwgu, dg1, dgpm, dgpf, dgpo, dbg, dbspt, dgn, dlng, dlnb, loss_in, tot, pack, slots, send_sems,
             recv_sems):
        xi, yi, ci = _mesh_pos()
        chip = 2 * xi + yi

        pack[...] = jnp.zeros_like(pack)
        for g in range(SGU_GROUPS):
            pack[_P_WSP + g * SGU_BLOCK:_P_WSP + (g + 1) * SGU_BLOCK] = dwsp[g]
        for j in range(N_CHIPS):
            pack[_P_WGU + GLA_RANK * j:_P_WGU + GLA_RANK * (j + 1)] = dwgu[0:GLA_RANK, LANES * j:LANES * (j + 1)]
        for k, r in enumerate((dg1, dgpm, dgpf, dgpo)):
            for q in range(8):
                pack[_P_NORM + 8 * k + q:_P_NORM + 8 * k + q + 1] = r[:, LANES * q:LANES * (q + 1)]
        for q in range(4):
            pack[_P_BG + q:_P_BG + q + 1] = dbg[:, LANES * q:LANES * (q + 1)]
        pack[_P_BSP:_P_BSP + SGU_GROUPS] = jnp.transpose(dbspt[...])[0:SGU_GROUPS]
        for k, r in enumerate((dgn, dlng, dlnb)):
            for j in range(N_CHIPS):
                for hh in range(4):
                    row = _P_HEAD + 32 * k + 8 * j + hh
                    pack[row:row + 1, 0:64] = r[:, 256 * hh + 64 * j:256 * hh + 64 * (j + 1)]
        pack[_P_LOSS:_P_LOSS + 1] = loss_in[...]

        to_sibling = pltpu.make_async_remote_copy(
            src_ref=pack, dst_ref=tot, send_sem=send_sems.at[N_PEER], recv_sem=recv_sems.at[N_PEER],
            device_id=(xi, yi, 1 - ci), device_id_type=MESH)
        to_sibling.start()
        to_sibling.wait_recv()
        to_sibling.wait_send()
        pack[...] = pack[...] + tot[...]
        slots[chip] = pack[...]

        def copy(j, slot):
            px, py = _peer_chips(xi, yi)[j]
            return pltpu.make_async_remote_copy(
                src_ref=pack, dst_ref=slots.at[slot(2 * px + py)], send_sem=send_sems.at[j], recv_sem=recv_sems.at[j],
                device_id=(px, py, ci), device_id_type=MESH)

        sends = [copy(j, lambda peer_chip: chip) for j in range(N_PEER)]
        for cp in sends:
            cp.start()
        for j in range(N_PEER):
            copy(j, lambda peer_chip: peer_chip).wait_recv()
        for cp in sends:
            cp.wait_send()
        acc = slots[0]
        for d in range(1, N_CHIPS):
            acc = acc + slots[d]
        tot[...] = acc

    return pl.pallas_call(
        body, name="small_sum", in_specs=[_whole()] * 12, out_specs=_whole(), out_shape=_sds((_P_ROWS, LANES), F32),
        scratch_shapes=[pltpu.VMEM((_P_ROWS, LANES), F32), pltpu.VMEM((N_CHIPS, _P_ROWS, LANES), F32),
                        pltpu.SemaphoreType.DMA((N_PEER + 1,)), pltpu.SemaphoreType.DMA((N_PEER + 1,))],
        compiler_params=pltpu.CompilerParams(vmem_limit_bytes=VMEM_LIMIT_BYTES),
    )(*dgrads)


def _small_adamw(tot, ws, ms, vs):
    n = len(ws)

    def body(*refs):
        tot = refs[0]
        w_refs, m_refs, v_refs = refs[1:1 + n], refs[1 + n:1 + 2 * n], refs[1 + 2 * n:1 + 3 * n]
        loss_out = refs[1 + 3 * n]
        outs = refs[2 + 3 * n:]
        chip = 2 * lax.axis_index("x") + lax.axis_index("y")
        loss_out[...] = tot[_P_LOSS:_P_LOSS + 1, 0:1]

        def step(k, g, pick, put):
            d, m2, v2 = _adam_values(pick(w_refs[k]), pick(m_refs[k]), pick(v_refs[k]), g)
            for o, val in zip((outs[k], outs[n + k], outs[2 * n + k], outs[3 * n + k]), (g, d, m2, v2)):
                put(o, val)

        def whole(ref):
            return ref[0]

        def put_whole(ref, val):
            ref[0] = val

        for g in range(SGU_GROUPS):
            def pick_g(ref, g=g):
                return ref[0, g]

            def put_g(ref, val, g=g):
                ref[0, g] = val

            step(0, tot[_P_WSP + g * SGU_BLOCK:_P_WSP + (g + 1) * SGU_BLOCK], pick_g, put_g)
        step(1, tot[pl.ds(pl.multiple_of(_P_WGU + GLA_RANK * chip, GLA_RANK), GLA_RANK), :], whole, put_whole)
        for k, (base, chunks) in enumerate(((_P_NORM, 8), (_P_NORM + 8, 8), (_P_NORM + 16, 8), (_P_NORM + 24, 8), (_P_BG, 4))):
            for q in range(chunks):
                def pick_q(ref, q=q):
                    return ref[:, LANES * q:LANES * (q + 1)]

                def put_q(ref, val, q=q):
                    ref[:, LANES * q:LANES * (q + 1)] = val

                step(2 + k, tot[base + q:base + q + 1], pick_q, put_q)
        step(7, tot[_P_BSP:_P_BSP + SGU_GROUPS], whole, put_whole)
        for k in range(3):
            mine = tot[pl.ds(pl.multiple_of(_P_HEAD + 32 * k + 8 * chip, 8), 8), :]
            step(8 + k, mine[0:4, 0:64], whole, put_whole)

    shapes = [_sds(w.shape, F32) for w in ws]
    res = pl.pallas_call(
        body, name="small_adamw", in_specs=[_whole()] * (1 + 3 * n), out_specs=[_whole()] * (1 + 4 * n),
        out_shape=[_sds((1, 1), F32)] + shapes * 4,
        compiler_params=pltpu.CompilerParams(vmem_limit_bytes=VMEM_LIMIT_BYTES),
    )(tot, *ws, *ms, *vs)
    return res[0].reshape(()), [list(res[1 + i * n:1 + (i + 1) * n]) for i in range(4)]


def _w_in_pieces():
    blk = D_IN // N_CHIPS
    pieces = []
    for s in range(len(_IN_SPLITS)):
        lo_s, hi_s = _IN_STARTS[s], _IN_STARTS[s + 1]
        for j in range(N_CHIPS):
            lo, hi = max(lo_s, j * blk), min(hi_s, (j + 1) * blk)
            if lo < hi:
                pieces.append((j, lo - j * blk, _IN_DST[s] + lo - lo_s, hi - lo))
    return pieces


def _relayout_w_in(gathered):
    _, rows, blk = gathered.shape
    tr = 256

    def body(g_ref, o_ref):
        o_ref[:, OFF_AL:N_ALL] = jnp.zeros((tr, LANES), BF16)
        for j, src, dst, w in _w_in_pieces():
            o_ref[:, dst:dst + w] = g_ref[j, :, src:src + w]

    res, _ = _call(body, name="relayout_w_in", grid=(rows // tr,), parallel=True,
                   in_specs=[pl.BlockSpec((N_CHIPS, tr, blk), lambda i: (0, i, 0))],
                   out_specs=[pl.BlockSpec((tr, N_ALL), lambda i: (i, 0))],
                   out_shape=[_sds((rows, N_ALL), BF16)], args=(gathered,))
    return res[0]


def _update_row_tile(rows):
    for t in range(min(rows, 256), 7, -8):
        if rows % t == 0:
            return t
    return rows


def _presum_w_in(dws, theirs, row0, rows, name, job=None):
    hr = theirs[0].shape[0]
    blk = D_IN // N_CHIPS
    tr = 64
    assert row0 % tr == 0 and rows % tr == 0
    nh, t0 = hr // tr, row0 // tr
    n = len(dws)

    def body(core_ref, *refs):
        dw_refs, q_refs, (o_ref, s_scr) = refs[:n], refs[n:2 * n], refs[2 * n:]
        for p, (a, off) in enumerate(dws):
            w = a.shape[1]
            s_scr[:, off:off + w] = (dw_refs[p][...] + q_refs[p][...]).astype(BF16)
        for j, src, dst, w in _w_in_pieces():
            o_ref[j, :, src:src + w] = s_scr[:, dst:dst + w]

    in_specs = [pl.BlockSpec((tr, a.shape[1]), lambda i, core: (i + t0 + core[0] * nh, 0)) for a, _ in dws]
    in_specs += [pl.BlockSpec((tr, q.shape[1]), lambda i, core: (i + t0, 0)) for q in theirs]
    res, jres = _call(body, name=name, grid=(rows // tr,), parallel=True, in_specs=in_specs,
                      out_specs=[pl.BlockSpec((N_CHIPS, tr, blk), lambda i, core: (0, i, 0))],
                      out_shape=[_sds((N_CHIPS, rows, blk), BF16)], scratch_shapes=[pltpu.VMEM((tr, N_ALL), BF16)],
                      args=(*[a for a, _ in dws], *theirs), job=job, by_core=True)
    return res[0], jres


def _presum(dw, theirs, name):
    if dw.ndim == 4:
        _, _, hr, c = dw.shape
        tr = _update_row_tile(hr)
        mine = pl.BlockSpec((1, 1, tr, c), lambda j, i, core: (j, core[0], i, 0))
        other = pl.BlockSpec((1, tr, c), lambda j, i, core: (j, i, 0))
    else:
        hr, c = dw.shape[0] // 2, dw.shape[1] // N_CHIPS
        tr = _update_row_tile(hr)
        nh = hr // tr
        mine = pl.BlockSpec((tr, c), lambda j, i, core: (i + core[0] * nh, j))
        other = pl.BlockSpec((tr, c), lambda j, i, core: (i, j))

    def body(core_ref, a_ref, q_ref, o_ref):
        o_ref[...] = (a_ref[...].reshape(tr, c) + q_ref[...].reshape(tr, c)).astype(BF16).reshape(o_ref.shape)

    res, _ = _call(body, name=name, grid=(N_CHIPS, hr // tr), parallel=True, in_specs=[mine, other],
                   out_specs=[pl.BlockSpec((1, tr, c), lambda j, i, core: (j, i, 0))],
                   out_shape=[_sds((N_CHIPS, hr, c), BF16)], args=(dw, theirs), by_core=True)
    return res[0]


def _sum_slots(own, slots, name):
    rows, cols = own.shape
    tr = _update_row_tile(rows)

    def body(own_ref, s_ref, o_ref):
        acc = own_ref[...].astype(F32)
        for j in range(N_PEER):
            acc = acc + s_ref[j].astype(F32)
        o_ref[...] = acc

    res, _ = _call(body, name=name, grid=(rows // tr,), parallel=True,
                   in_specs=[pl.BlockSpec((tr, cols), lambda i: (i, 0)), pl.BlockSpec((N_PEER, tr, cols), lambda i: (0, i, 0))],
                   out_specs=[pl.BlockSpec((tr, cols), lambda i: (i, 0))], out_shape=[_sds((rows, cols), F32)],
                   args=(own, slots))
    return res[0]


def _adamw(w, m, v, g_mine, g_theirs, name, job=None):
    rows, cols = w.shape
    part_rows = [p.shape[0] for p in g_mine]
    assert sum(part_rows) == rows // 2 and [p.shape[0] for p in g_theirs] == part_rows
    tr = _update_row_tile(min(part_rows))
    assert all(r % tr == 0 for r in part_rows)
    nh = (rows // 2) // tr
    starts = [sum(part_rows[:k]) // tr for k in range(len(part_rows))]
    n_parts = len(part_rows)

    def body(core_ref, w_ref, m_ref, v_ref, *rest):
        g_refs, (g_out, d_out, m_out, v_out) = rest[:-4], rest[-4:]
        step = pl.program_id(0)
        mine_here = (step // nh) == core_ref[0]
        q = step % nh
        g = None
        for k in reversed(range(n_parts)):
            val = jnp.where(mine_here, g_refs[k][...], g_refs[n_parts + k][...])
            g = val if g is None else jnp.where(q < starts[k + 1], val, g)
        d, m2, v2 = _adam_values(w_ref[...], m_ref[...], v_ref[...], g)
        g_out[...] = g
        m_out[...] = m2
        v_out[...] = v2
        d_out[...] = d

    def g_spec(k, mine):
        last = part_rows[k] // tr - 1

        def index(i, core):
            half = core[0] if mine else 1 - core[0]
            here = jnp.clip(i % nh - starts[k], 0, last)
            return (jnp.where(i // nh == half, here, jnp.where(i // nh > half, last, 0)), 0)

        return pl.BlockSpec((tr, cols), index)

    spec = pl.BlockSpec((tr, cols), lambda i, core: (i, 0))
    g_specs = [g_spec(k, True) for k in range(n_parts)] + [g_spec(k, False) for k in range(n_parts)]
    return _call(body, name=name, grid=(rows // tr,), parallel=True, in_specs=[spec] * 3 + g_specs,
                 out_specs=[spec] * 4, out_shape=[_sds((rows, cols), F32)] * 4, args=(w, m, v, *g_mine, *g_theirs),
                 job=job, by_core=True)


SC_LANES = 16
SC_BLOCK = (8, 128)


def _adamw_sparsecore(w, m, v, g_mine, g_theirs, name):
    rows, cols = w.shape
    hr = rows // 2
    assert hr % SC_BLOCK[0] == 0 and cols % SC_BLOCK[1] == 0
    mesh = plsc.VectorSubcoreMesh(core_axis_name="sc_core", subcore_axis_name="sc_tile")
    sds = _sds((rows, cols), F32)

    def block(w_v, m_v, v_v, g_v, g_o, d_o, m_o, v_o):
        @pl.loop(0, SC_BLOCK[0])
        def _(r):
            @pl.loop(0, SC_BLOCK[1], step=SC_LANES)
            def _(c):
                at = (pl.ds(r, 1), pl.ds(c, SC_LANES))
                g = g_v.at[*at][...]
                d, m2, v2 = _adam_values(w_v.at[*at][...], m_v.at[*at][...], v_v.at[*at][...], g)
                g_o.at[*at][...] = g
                d_o.at[*at][...] = d
                m_o.at[*at][...] = m2
                v_o.at[*at][...] = v2

    def kernel_fn(w_ref, m_ref, v_ref, gm_ref, gt_ref, g_out, d_out, m_out, v_out):
        def half(k, g_src):
            off = k * (hr // SC_BLOCK[0])
            here = pl.BlockSpec(block_shape=SC_BLOCK, index_map=lambda i, j: (i + off, j))
            grad = pl.BlockSpec(block_shape=SC_BLOCK, index_map=lambda i, j: (i, j))
            pltpu.emit_pipeline(
                block, grid=(hr // SC_BLOCK[0], cols // SC_BLOCK[1]), in_specs=[here] * 3 + [grad], out_specs=[here] * 4,
                core_axis_name=("sc_core", "sc_tile"), dimension_semantics=(pltpu.PARALLEL, pltpu.PARALLEL),
            )(w_ref, m_ref, v_ref, g_src, g_out, d_out, m_out, v_out)

        core = lax.axis_index("c")

        @pl.when(core == 0)
        def _():
            half(0, gm_ref)
            half(1, gt_ref)

        @pl.when(core == 1)
        def _():
            half(0, gt_ref)
            half(1, gm_ref)

    return pl.kernel(kernel_fn, out_type=[sds] * 4, mesh=mesh, scratch_types=[], name=name)(w, m, v, g_mine, g_theirs)


def _transposed_cast(wt):
    cols, rows = wt.shape

    def body(x_ref, o_ref):
        o_ref[...] = jnp.transpose(x_ref[...]).astype(BF16)

    res, _ = _call(body, name="transpose_w_in", grid=(pl.cdiv(cols, LANES),), parallel=True,
                   in_specs=[pl.BlockSpec((LANES, rows), lambda j: (j, 0))],
                   out_specs=[pl.BlockSpec((rows, LANES), lambda j: (0, j))], out_shape=[_sds((rows, cols), BF16)],
                   args=(wt,))
    return res[0]


def _adamw_transposed(wt, mt, vt, g_mine, g_theirs, name):
    cols, rows = wt.shape
    n_parts = len(g_mine)

    def body(w_ref, m_ref, v_ref, *rest):
        g_refs, (g_out, d_out, m_out, v_out) = rest[:-4], rest[-4:]
        mine = jnp.concatenate([r[...] for r in g_refs[:n_parts]], axis=0)
        theirs = jnp.concatenate([r[...] for r in g_refs[n_parts:]], axis=0)
        first = lax.axis_index("c") == 0
        g = jnp.transpose(jnp.concatenate([jnp.where(first, mine, theirs), jnp.where(first, theirs, mine)], axis=0))
        d, m2, v2 = _adam_values(w_ref[...], m_ref[...], v_ref[...], g)
        g_out[...] = g
        m_out[...] = m2
        v_out[...] = v2
        d_out[...] = d

    spec = pl.BlockSpec((LANES, rows), lambda j: (j, 0))
    g_specs = [pl.BlockSpec((p.shape[0], LANES), lambda j: (0, j)) for p in g_mine] * 2
    res, _ = _call(body, name=name, grid=(pl.cdiv(cols, LANES),), parallel=True, in_specs=[spec] * 3 + g_specs,
                   out_specs=[spec] * 4, out_shape=[_sds((cols, rows), F32)] * 4, args=(wt, mt, vt, *g_mine, *g_theirs))
    return res


def _inproj_fwd(x, g1, w_all, job=None):
    T = x.shape[0]
    tT = _row_tile(T, 512)

    def body(x_ref, g_ref, w_ref, a_ref, proj_ref, alow_ref):
        xv = x_ref[...]
        a = (xv * _rms_stats(xv) * g_ref[...]).astype(BF16)
        a_ref[...] = a
        for j in range(N_MAIN // 1024):
            cols = slice(j * 1024, (j + 1) * 1024)
            proj_ref[:, cols] = _dot(a, w_ref[:, cols]).astype(BF16)
        alow_ref[...] = _dot(a, w_ref[:, N_MAIN:N_ALL])

    row = lambda w: pl.BlockSpec((tT, w), lambda i: (i, 0))
    return _call(
        body, name="inproj_fwd", grid=(T // tT,), parallel=True,
        in_specs=[row(D_MODEL), pl.BlockSpec((1, D_MODEL), lambda i: (0, 0)), _whole()],
        out_specs=[row(D_MODEL), row(N_MAIN), row(LANES)],
        out_shape=[_sds((T, D_MODEL), BF16), _sds((T, N_MAIN), BF16), _sds((T, LANES), F32)],
        args=(x, g1, w_all), job=job)


def _gla_decay_terms(al_ref, wgu_ref, bg_ref, later_ref):
    logit = _dot_bf16(al_ref[...], wgu_ref[...]) + bg_ref[...]
    la = _log_sigmoid(logit) * (1.0 / GLA_TAU)
    delta = _dot_exact_lhs(later_ref[...], la)
    return logit, la, delta


def _gla_fwd(proj, alow, wgu, b_gate, gn, job=None):
    T = proj.shape[0]
    tT = _row_tile(T, 512)
    nc = tT // CHUNK

    def body(q_ref, k_ref, v_ref, r_ref, al_ref, wgu_ref, bg_ref, gn_ref, later_ref, y_ref, st_ref, s_scr):
        @pl.when(pl.program_id(0) == 0)
        def _():
            s_scr[...] = jnp.zeros_like(s_scr)

        _, la, delta = _gla_decay_terms(al_ref, wgu_ref, bg_ref, later_ref)
        kdec = (k_ref[...].astype(F32) * jnp.exp(delta)).astype(BF16)
        heads = range(GLA_HEADS)
        kcs = [slice(h * GLA_DK, (h + 1) * GLA_DK) for h in heads]
        vcs = [slice(h * GLA_DV, (h + 1) * GLA_DV) for h in heads]
        state = [s_scr[h] for h in heads]
        for c in range(nc):
            rows = slice(c * CHUNK, (c + 1) * CHUNK)
            first = slice(c * CHUNK, c * CHUNK + 1)
            dec = jnp.exp(la[first, :] + delta[first, :])
            upd_t = [_dot(v_ref[rows, vcs[h]], kdec[rows, kcs[h]], _TN) for h in heads]
            qs = [(q_ref[rows, kcs[h]].astype(F32) * (GLA_DK ** -0.5)).astype(BF16) for h in heads]
            for h in heads:
                state[h] = state[h] * dec[:, kcs[h]] + upd_t[h]
                st_ref[c, h] = state[h]
            o = [_dot(qs[h], state[h].astype(BF16), _NT) for h in heads]
            for h in heads:
                on = o[h] * _rms_stats(o[h]) * gn_ref[:, vcs[h]]
                rr = r_ref[rows, vcs[h]].astype(F32)
                y_ref[rows, vcs[h]] = (on * (rr * _sigmoid(rr))).astype(BF16)
        for h in heads:
            s_scr[h] = state[h]

    blk = lambda w, j: pl.BlockSpec((tT, w), lambda i: (i, j))
    return _call(
        body, name="gla_fwd", grid=(T // tT,),
        in_specs=[blk(512, 0), blk(512, 1), blk(1024, 1), blk(1024, 2), blk(LANES, 0)] + [_whole()] * 4,
        out_specs=[pl.BlockSpec((tT, GLA_V), lambda i: (i, 0)),
                   pl.BlockSpec((nc, GLA_HEADS, GLA_DV, GLA_DK), lambda i: (i, 0, 0, 0))],
        out_shape=[_sds((T, GLA_V), BF16), _sds((T // CHUNK, GLA_HEADS, GLA_DV, GLA_DK), F32)],
        scratch_shapes=[pltpu.VMEM((GLA_HEADS, GLA_DV, GLA_DK), F32)],
        args=(proj, proj, proj, proj, alow, wgu, b_gate, gn, _chunk_masks(tT, upper=True)), job=job)


def _sgu_mask():
    i = lax.broadcasted_iota(jnp.int32, (SGU_BLOCK, SGU_BLOCK), 0)
    j = lax.broadcasted_iota(jnp.int32, (SGU_BLOCK, SGU_BLOCK), 1)
    return lax.shift_right_logical(j, 6) <= lax.shift_right_logical(i, 6)


def _sgu_fwd(proj, ln_g, ln_b, w_sp, b_sp_t):
    T = proj.shape[0]
    tT = _row_tile(T, 512)
    nb = tT // SGU_BLOCK

    def body(su_ref, sv_ref, lg_ref, lb_ref, w_ref, b_ref, y_ref):
        mask = _sgu_mask()
        for g in range(SGU_GROUPS):
            gc = slice(g * SGU_DG, (g + 1) * SGU_DG)
            wm = jnp.where(mask, w_ref[g], 0.0).astype(BF16)
            vf = _gelu(sv_ref[:, gc].astype(F32))
            mu = jnp.mean(vf, axis=-1, keepdims=True)
            vc = vf - mu
            rstd = lax.rsqrt(jnp.mean(vc * vc, axis=-1, keepdims=True) + EPS)
            vn = (vc * rstd * lg_ref[:, gc] + lb_ref[:, gc]).astype(BF16)
            u = _gelu(su_ref[:, gc].astype(F32))
            for b in range(nb):
                rows = slice(b * SGU_BLOCK, (b + 1) * SGU_BLOCK)
                mixed = _dot(wm, vn[rows, :]) + b_ref[:, g:g + 1]
                y_ref[rows, gc] = (u[rows, :] * mixed).astype(BF16)

    blk = lambda j: pl.BlockSpec((tT, 1024), lambda i: (i, j))
    res, _ = _call(body, name="sgu_fwd", grid=(T // tT,), parallel=True,
                   in_specs=[blk(3), blk(4), _whole(), _whole(), _whole(), _whole()],
                   out_specs=[pl.BlockSpec((tT, 1024), lambda i: (i, 0))], out_shape=[_sds((T, 1024), BF16)],
                   args=(proj, proj, ln_g, ln_b, w_sp, b_sp_t))
    return res[0]


def _merge_fwd(x, proj, y_gla, y_sgu, w_bg, w_bs, w_o, g_pm, job=None):
    T = x.shape[0]
    tT = _row_tile(T, 512)

    def body(x_ref, gg_ref, gs_ref, yg_ref, ys_ref, wbg_ref, wbs_ref, wo_ref, g_ref,
             zg_ref, zs_ref, mg_ref, mix_ref, x1_ref):
        zg = _dot(yg_ref[...], wbg_ref[...])
        zs = _dot(ys_ref[...], wbs_ref[...])
        zg_ref[...] = zg.astype(BF16)
        zs_ref[...] = zs.astype(BF16)
        merged = (_sigmoid(gg_ref[...].astype(F32)) * zg + _sigmoid(gs_ref[...].astype(F32)) * zs).astype(BF16)
        mg_ref[...] = merged
        mix = _dot(merged, wo_ref[...])
        mix_ref[...] = mix.astype(BF16)
        x1_ref[...] = x_ref[...] + mix * _rms_stats(mix) * g_ref[...]

    row = pl.BlockSpec((tT, D_MODEL), lambda i: (i, 0))
    blk = lambda j: pl.BlockSpec((tT, 1024), lambda i: (i, j))
    sds = lambda dt: _sds((T, D_MODEL), dt)
    return _call(body, name="merge_fwd", grid=(T // tT,), parallel=True,
                 in_specs=[row, blk(5), blk(6), row, row, _whole(), _whole(), _whole(),
                           pl.BlockSpec((1, D_MODEL), lambda i: (0, 0))],
                 out_specs=[row] * 5, out_shape=[sds(BF16), sds(BF16), sds(BF16), sds(BF16), sds(F32)],
                 args=(x, proj, proj, y_gla, y_sgu, w_bg, w_bs, w_o, g_pm), job=job)


def _ffn_fwd_bwd(x1, tgt, w_fi_top, w_fi_bot, w_fo, g_pf, g_po):
    T = x1.shape[0]
    tT = _row_tile(T, 256)
    half = D_FF // 2
    kh = D_MODEL // 2

    def body(x1_ref, t_ref, top_ref, bot_ref, wfo_ref, gpf_ref, gpo_ref,
             h_ref, f_ref, dgu_ref, dy_ref, dx1_ref, loss_ref, dgpf_ref, dgpo_ref, gu_scr):
        @pl.when(pl.program_id(0) == 0)
        def _():
            loss_ref[...] = jnp.zeros_like(loss_ref)
            dgpf_ref[...] = jnp.zeros_like(dgpf_ref)
            dgpo_ref[...] = jnp.zeros_like(dgpo_ref)

        main = (half // 256) * 256
        pieces = (0, 1, None)

        def w_in_cols(ref, first_slab, p):
            if p is not None:
                return ref[first_slab + p, :, :main]
            return jnp.concatenate([ref[first_slab, :, main:], ref[first_slab + 1, :, main:]], axis=1)

        def w_out_rows(p):
            if p is not None:
                return wfo_ref[p * half:p * half + main, :]
            return jnp.concatenate([wfo_ref[main:half, :], wfo_ref[half + main:2 * half, :]], axis=0)

        def put(ref, base, p, val):
            if p is not None:
                ref[:, base + p * half:base + p * half + main] = val
            else:
                ref[:, base + main:base + half] = val[:, :half - main]
                ref[:, base + half + main:base + 2 * half] = val[:, half - main:]

        def get(ref, base, p):
            if p is not None:
                return ref[:, base + p * half:base + p * half + main]
            return jnp.concatenate([ref[:, base + main:base + half], ref[:, base + half + main:base + 2 * half]], axis=1)

        x1v = x1_ref[...]
        r2 = _rms_stats(x1v)
        h = (x1v * r2 * gpf_ref[...]).astype(BF16)
        h_ref[...] = h
        y = jnp.zeros((tT, D_MODEL), F32)
        for p in pieces:
            gate = _dot(h[:, :kh], w_in_cols(top_ref, 0, p)) + _dot(h[:, kh:], w_in_cols(bot_ref, 0, p))
            up = _dot(h[:, :kh], w_in_cols(top_ref, 2, p)) + _dot(h[:, kh:], w_in_cols(bot_ref, 2, p))
            put(gu_scr, 0, p, gate)
            put(gu_scr, D_FF, p, up)
            f = (gate * _sigmoid(gate) * up).astype(BF16)
            put(f_ref, 0, p, f)
            y = y + _dot(f, w_out_rows(p))
        r3 = _rms_stats(y)
        x2 = x1v + y * r3 * gpo_ref[...]
        err = x2 - t_ref[...]
        loss_ref[...] += jnp.sum(err * err) * (0.5 / D_MODEL)
        dx2 = err * (1.0 / D_MODEL)
        dy, dg = _rms_bwd(dx2, y, r3, gpo_ref[...])
        dgpo_ref[...] += jnp.sum(dg, axis=0, keepdims=True)
        dyb = dy.astype(BF16)
        dy_ref[...] = dyb
        dh_top = jnp.zeros((tT, kh), F32)
        dh_bot = jnp.zeros((tT, kh), F32)
        for p in pieces:
            df = _dot(dyb, w_out_rows(p), _NT)
            gate = get(gu_scr, 0, p)
            up = get(gu_scr, D_FF, p)
            sg = _sigmoid(gate)
            dgate = (df * up * (sg * (1.0 + gate * (1.0 - sg)))).astype(BF16)
            dup = (df * (gate * sg)).astype(BF16)
            put(dgu_ref, 0, p, dgate)
            put(dgu_ref, D_FF, p, dup)
            dh_top = dh_top + _dot(dgate, w_in_cols(top_ref, 0, p), _NT) + _dot(dup, w_in_cols(top_ref, 2, p), _NT)
            dh_bot = dh_bot + _dot(dgate, w_in_cols(bot_ref, 0, p), _NT) + _dot(dup, w_in_cols(bot_ref, 2, p), _NT)
        dh = jnp.concatenate([dh_top, dh_bot], axis=1)
        dx1n, dg2 = _rms_bwd(dh, x1v, r2, gpf_ref[...])
        dgpf_ref[...] += jnp.sum(dg2, axis=0, keepdims=True)
        dx1_ref[...] = dx2 + dx1n

    row = lambda w: pl.BlockSpec((tT, w), lambda i: (i, 0))
    vec = pl.BlockSpec((1, D_MODEL), lambda i: (0, 0))
    res, _ = _call(
        body, name="ffn_fwd_bwd", grid=(T // tT,),
        in_specs=[row(D_MODEL), row(D_MODEL), _whole(), _whole(), _whole(), vec, vec],
        out_specs=[row(D_MODEL), row(D_FF), row(2 * D_FF), row(D_MODEL), row(D_MODEL),
                   pl.BlockSpec((1, LANES), lambda i: (0, 0)), vec, vec],
        out_shape=[_sds((T, D_MODEL), BF16), _sds((T, D_FF), BF16), _sds((T, 2 * D_FF), BF16), _sds((T, D_MODEL), BF16),
                   _sds((T, D_MODEL), F32), _sds((1, LANES), F32), _sds((1, D_MODEL), F32), _sds((1, D_MODEL), F32)],
        scratch_shapes=[pltpu.VMEM((tT, 2 * D_FF), F32)], args=(x1, tgt, w_fi_top, w_fi_bot, w_fo, g_pf, g_po))
    return res


def _merge_bwd(dx1, mix, proj, zg, zs, w_bg, w_bs, w_o, g_pm, job=None):
    T = dx1.shape[0]
    tT = _row_tile(T, 512)

    def body(dx1_ref, mix_ref, gg_ref, gs_ref, zg_ref, zs_ref, wbg_ref, wbs_ref, wo_ref, g_ref,
             dmix_ref, dzg_ref, dzs_ref, dgate_ref, dyg_ref, dys_ref, dgpm_ref):
        @pl.when(pl.program_id(0) == 0)
        def _():
            dgpm_ref[...] = jnp.zeros_like(dgpm_ref)

        mix = mix_ref[...].astype(F32)
        dmix, dg = _rms_bwd(dx1_ref[...], mix, _rms_stats(mix), g_ref[...])
        dgpm_ref[...] += jnp.sum(dg, axis=0, keepdims=True)
        dmb = dmix.astype(BF16)
        dmix_ref[...] = dmb
        dmerged = _dot(dmb, wo_ref[...], _NT)
        for k, (gate_ref, z_ref, w_ref, dz_ref, dy_ref) in enumerate((
                (gg_ref, zg_ref, wbg_ref, dzg_ref, dyg_ref), (gs_ref, zs_ref, wbs_ref, dzs_ref, dys_ref))):
            sg = _sigmoid(gate_ref[...].astype(F32))
            dz = (dmerged * sg).astype(BF16)
            dz_ref[...] = dz
            dgate_ref[:, k * 1024:(k + 1) * 1024] = (dmerged * z_ref[...].astype(F32) * (sg * (1.0 - sg))).astype(BF16)
            dy_ref[...] = _dot(dz, w_ref[...], _NT).astype(BF16)

    row = pl.BlockSpec((tT, D_MODEL), lambda i: (i, 0))
    blk = lambda j: pl.BlockSpec((tT, 1024), lambda i: (i, j))
    vec = pl.BlockSpec((1, D_MODEL), lambda i: (0, 0))
    sds = _sds((T, D_MODEL), BF16)
    return _call(
        body, name="merge_bwd", grid=(T // tT,),
        in_specs=[row, row, blk(5), blk(6), row, row, _whole(), _whole(), _whole(), vec],
        out_specs=[row, row, row, pl.BlockSpec((tT, W_MRG), lambda i: (i, 0)), row, row, vec],
        out_shape=[sds, sds, sds, _sds((T, W_MRG), BF16), sds, sds, _sds((1, D_MODEL), F32)],
        args=(dx1, mix, proj, proj, zg, zs, w_bg, w_bs, w_o, g_pm), job=job)


def _sgu_bwd(proj, dy_sgu, ln_g, ln_b, w_sp, b_sp_t, job=None):
    T = proj.shape[0]
    tT = _row_tile(T, 512)
    nb = tT // SGU_BLOCK

    def body(su_ref, sv_ref, dy_ref, lg_ref, lb_ref, w_ref, b_ref, dp_ref, dw_ref, dbt_ref, dlg_ref, dlb_ref):
        @pl.when(pl.program_id(0) == 0)
        def _():
            dw_ref[...] = jnp.zeros_like(dw_ref)
            dbt_ref[...] = jnp.zeros_like(dbt_ref)
            dlg_ref[...] = jnp.zeros_like(dlg_ref)
            dlb_ref[...] = jnp.zeros_like(dlb_ref)

        mask = _sgu_mask()
        lane = lax.broadcasted_iota(jnp.int32, (SGU_BLOCK, LANES), 1)
        for g in range(SGU_GROUPS):
            gc = slice(g * SGU_DG, (g + 1) * SGU_DG)
            gc_v = slice(1024 + g * SGU_DG, 1024 + (g + 1) * SGU_DG)
            wm = jnp.where(mask, w_ref[g], 0.0).astype(BF16)
            vf, dvf_dsv = _gelu_and_grad(sv_ref[:, gc].astype(F32))
            mu = jnp.mean(vf, axis=-1, keepdims=True)
            vc = vf - mu
            rstd = lax.rsqrt(jnp.mean(vc * vc, axis=-1, keepdims=True) + EPS)
            vhat = vc * rstd
            vn = (vhat * lg_ref[:, gc] + lb_ref[:, gc]).astype(BF16)
            u, du_dsu = _gelu_and_grad(su_ref[:, gc].astype(F32))
            dy = dy_ref[:, gc].astype(F32)
            dmixed = (dy * u).astype(BF16)
            dvn_parts = []
            dw_acc = jnp.zeros((SGU_BLOCK, SGU_BLOCK), F32)
            db_acc = jnp.zeros((SGU_BLOCK, 1), F32)
            for b in range(nb):
                rows = slice(b * SGU_BLOCK, (b + 1) * SGU_BLOCK)
                mixed = _dot(wm, vn[rows, :]) + b_ref[:, g:g + 1]
                dp_ref[rows, gc] = (dy[rows, :] * mixed * du_dsu[rows, :]).astype(BF16)
                dvn_parts.append(_dot(wm, dmixed[rows, :], _TN))
                dw_acc = dw_acc + _dot(dmixed[rows, :], vn[rows, :], _NT)
                db_acc = db_acc + jnp.sum(dmixed[rows, :].astype(F32), axis=-1, keepdims=True)
            dw_ref[g] += jnp.where(mask, dw_acc, 0.0)
            dbt_ref[...] += jnp.where(lane == g, db_acc, 0.0)
            dvn = jnp.concatenate(dvn_parts, axis=0)
            dlg_ref[:, gc] += jnp.sum(dvn * vhat, axis=0, keepdims=True)
            dlb_ref[:, gc] += jnp.sum(dvn, axis=0, keepdims=True)
            dvh = dvn * lg_ref[:, gc]
            dvf = rstd * (dvh - jnp.mean(dvh, axis=-1, keepdims=True)
                          - vhat * jnp.mean(dvh * vhat, axis=-1, keepdims=True))
            dp_ref[:, gc_v] = (dvf * dvf_dsv).astype(BF16)

    blk = lambda j: pl.BlockSpec((tT, 1024), lambda i: (i, j))
    row = lambda w: pl.BlockSpec((tT, w), lambda i: (i, 0))
    vec = pl.BlockSpec((1, 1024), lambda i: (0, 0))
    return _call(
        body, name="sgu_bwd", grid=(T // tT,),
        in_specs=[blk(3), blk(4), row(1024), _whole(), _whole(), _whole(), _whole()],
        out_specs=[row(W_SGU), pl.BlockSpec((SGU_GROUPS, SGU_BLOCK, SGU_BLOCK), lambda i: (0, 0, 0)),
                   pl.BlockSpec((SGU_BLOCK, LANES), lambda i: (0, 0)), vec, vec],
        out_shape=[_sds((T, W_SGU), BF16), _sds((SGU_GROUPS, SGU_BLOCK, SGU_BLOCK), F32), _sds((SGU_BLOCK, LANES), F32),
                   _sds((1, 1024), F32), _sds((1, 1024), F32)],
        args=(proj, proj, dy_sgu, ln_g, ln_b, w_sp, b_sp_t), job=job)


def _gla_bwd(proj, alow, wgu, b_gate, gn, states, dy_gla, job=None):
    T = proj.shape[0]
    tT = _row_tile(T, 512)
    nc = tT // CHUNK
    nt = T // tT

    def body(q_ref, k_ref, v_ref, r_ref, al_ref, wgu_ref, bg_ref, gn_ref, later_ref, earlier_ref, st_ref, sp_ref, dy_ref,
             dp_ref, dal_ref, dgn_ref, dbg_ref, dwgu_ref, g_scr, dd_scr, dt_scr):
        step = pl.program_id(0)

        @pl.when(step == 0)
        def _():
            g_scr[...] = jnp.zeros_like(g_scr)
            dgn_ref[...] = jnp.zeros_like(dgn_ref)
            dbg_ref[...] = jnp.zeros_like(dbg_ref)
            dwgu_ref[...] = jnp.zeros_like(dwgu_ref)

        has_prev = jnp.where(step == nt - 1, 0.0, 1.0)
        logit, la, delta = _gla_decay_terms(al_ref, wgu_ref, bg_ref, later_ref)
        e = jnp.exp(delta)
        kdec_f = k_ref[...].astype(F32) * e
        kdec = kdec_f.astype(BF16)
        heads = range(GLA_HEADS)
        kcs = [slice(h * GLA_DK, (h + 1) * GLA_DK) for h in heads]
        vcs = [slice(h * GLA_DV, (h + 1) * GLA_DV) for h in heads]
        carry = [g_scr[h] for h in heads]
        dgn_acc = [jnp.zeros((1, GLA_DV), F32) for _ in heads]
        for c in reversed(range(nc)):
            rows = slice(c * CHUNK, (c + 1) * CHUNK)
            first = slice(c * CHUNK, c * CHUNK + 1)
            dec = jnp.exp(la[first, :] + delta[first, :])
            s_b = [st_ref[c, h].astype(BF16) for h in heads]
            qs = [(q_ref[rows, kcs[h]].astype(F32) * (GLA_DK ** -0.5)).astype(BF16) for h in heads]
            o = [_dot(qs[h], s_b[h], _NT) for h in heads]
            do = []
            for h in heads:
                rstd = _rms_stats(o[h])
                ohat = o[h] * rstd
                gnh = gn_ref[:, vcs[h]]
                dy = dy_ref[rows, vcs[h]].astype(F32)
                rr = r_ref[rows, vcs[h]].astype(F32)
                sg = _sigmoid(rr)
                don = dy * (rr * sg)
                dp_ref[rows, OFF_R + h * GLA_DV:OFF_R + (h + 1) * GLA_DV] = (
                    dy * (ohat * gnh) * (sg * (1.0 + rr * (1.0 - sg)))).astype(BF16)
                dgn_acc[h] = dgn_acc[h] + jnp.sum(don * ohat, axis=0, keepdims=True)
                dn = don * gnh
                do.append((rstd * (dn - ohat * jnp.mean(dn * ohat, axis=-1, keepdims=True))).astype(BF16))
            dq = [_dot(do[h], s_b[h]) for h in heads]
            g_t = [_dot(do[h], qs[h], _TN) + carry[h] for h in heads]
            g_b = [g_t[h].astype(BF16) for h in heads]
            dv = [_dot(kdec[rows, kcs[h]], g_b[h], _NT) for h in heads]
            dkdec = [_dot(v_ref[rows, vcs[h]], g_b[h]) for h in heads]
            for h in heads:
                s_prev = st_ref[c - 1, h] if c > 0 else sp_ref[0, h] * has_prev
                ddec = jnp.sum(g_t[h] * s_prev, axis=0, keepdims=True)
                carry[h] = g_t[h] * dec[:, kcs[h]]
                dp_ref[rows, OFF_Q + h * GLA_DK:OFF_Q + (h + 1) * GLA_DK] = (dq[h] * (GLA_DK ** -0.5)).astype(BF16)
                dp_ref[rows, OFF_V + h * GLA_DV:OFF_V + (h + 1) * GLA_DV] = dv[h].astype(BF16)
                dp_ref[rows, OFF_K + h * GLA_DK:OFF_K + (h + 1) * GLA_DK] = (dkdec[h] * e[rows, kcs[h]]).astype(BF16)
                dd_scr[rows, kcs[h]] = dkdec[h] * kdec_f[rows, kcs[h]]
                dt_scr[rows, kcs[h]] = jnp.broadcast_to(ddec * dec[:, kcs[h]], (CHUNK, GLA_DK))
        for h in heads:
            g_scr[h] = carry[h]
            dgn_ref[:, vcs[h]] += dgn_acc[h]
        dla = _dot_exact_lhs(earlier_ref[...], dd_scr[...]) + dt_scr[...]
        dlogit = dla * (1.0 / GLA_TAU) * _sigmoid(-logit)
        dbg_ref[...] += jnp.sum(dlogit, axis=0, keepdims=True)
        dwgu_ref[...] += _dot_bf16(al_ref[...], dlogit, _TN)
        dal_ref[...] = _dot_bf16(dlogit, wgu_ref[...], _NT).astype(BF16)

    rev = lambda i: nt - 1 - i
    blk = lambda w, j: pl.BlockSpec((tT, w), lambda i: (rev(i), j))
    st_blk = pl.BlockSpec((nc, GLA_HEADS, GLA_DV, GLA_DK), lambda i: (rev(i), 0, 0, 0))
    sp_blk = pl.BlockSpec((1, GLA_HEADS, GLA_DV, GLA_DK), lambda i: (jnp.maximum(rev(i) * nc - 1, 0), 0, 0, 0))
    return _call(
        body, name="gla_bwd", grid=(nt,),
        in_specs=[blk(512, 0), blk(512, 1), blk(1024, 1), blk(1024, 2), blk(LANES, 0)] + [_whole()] * 5
        + [st_blk, sp_blk, blk(GLA_V, 0)],
        out_specs=[blk(W_GLA, 0), blk(LANES, 0), pl.BlockSpec((1, GLA_V), lambda i: (0, 0)),
                   pl.BlockSpec((1, GLA_QK), lambda i: (0, 0)), pl.BlockSpec((LANES, GLA_QK), lambda i: (0, 0))],
        out_shape=[_sds((T, W_GLA), BF16), _sds((T, LANES), BF16), _sds((1, GLA_V), F32), _sds((1, GLA_QK), F32),
                   _sds((LANES, GLA_QK), F32)],
        scratch_shapes=[pltpu.VMEM((GLA_HEADS, GLA_DV, GLA_DK), F32), pltpu.VMEM((tT, GLA_QK), F32),
                        pltpu.VMEM((tT, GLA_QK), F32)],
        args=(proj, proj, proj, proj, alow, wgu, b_gate, gn, _chunk_masks(tT, upper=True), _chunk_masks(tT, upper=False),
              states, states, dy_gla), job=job)


def _inproj_bwd(x, dx1, g1, w_all, dparts, job=None):
    T = x.shape[0]
    tT = _row_tile(T, 512)
    offs = (0, W_GLA, W_GLA + W_SGU, N_MAIN)

    def body(x_ref, dx1_ref, g_ref, w_ref, *rest):
        part_refs, (dx_ref, dg_ref) = rest[:len(offs)], rest[len(offs):]

        @pl.when(pl.program_id(0) == 0)
        def _():
            dg_ref[...] = jnp.zeros_like(dg_ref)

        da = jnp.zeros((tT, D_MODEL), F32)
        for off, p_ref in zip(offs, part_refs):
            da = da + _dot(p_ref[...], w_ref[:, off:off + p_ref.shape[1]], _NT)
        xv = x_ref[...]
        dx, dg = _rms_bwd(da, xv, _rms_stats(xv), g_ref[...])
        dg_ref[...] += jnp.sum(dg, axis=0, keepdims=True)
        dx_ref[...] = dx1_ref[...] + dx

    row = lambda w: pl.BlockSpec((tT, w), lambda i: (i, 0))
    vec = pl.BlockSpec((1, D_MODEL), lambda i: (0, 0))
    return _call(
        body, name="inproj_bwd", grid=(T // tT,),
        in_specs=[row(D_MODEL), row(D_MODEL), vec, _whole()] + [row(p.shape[1]) for p in dparts],
        out_specs=[row(D_MODEL), vec], out_shape=[_sds((T, D_MODEL), F32), _sds((1, D_MODEL), F32)],
        args=(x, dx1, g1, w_all, *dparts), job=job)


def _tn_matmul(a, b, name, job=None):
    T, M = a.shape
    N = b.shape[1]
    tk = _row_tile(T, 1024)
    tm = M if M <= 1024 else 1408
    tn = N if N <= 3072 else N // 2
    assert M % tm == 0 and N % tn == 0

    def body(a_ref, b_ref, o_ref):
        @pl.when(pl.program_id(2) == 0)
        def _():
            o_ref[...] = _dot(a_ref[...], b_ref[...], _TN)

        @pl.when(pl.program_id(2) > 0)
        def _():
            o_ref[...] += _dot(a_ref[...], b_ref[...], _TN)

    res, jres = _call(
        body, name=name, grid=(M // tm, N // tn, T // tk),
        in_specs=[pl.BlockSpec((tk, tm), lambda i, j, k: (k, i)), pl.BlockSpec((tk, tn), lambda i, j, k: (k, j))],
        out_specs=[pl.BlockSpec((tm, tn), lambda i, j, k: (i, j))], out_shape=[_sds((M, N), F32)], args=(a, b), job=job)
    return res[0], jres


def _pad_rows(a, rows=8):
    return jnp.pad(a, ((0, rows - a.shape[0]), (0, LANES - a.shape[1])))


def _halves_view(dw):
    r = dw.shape[0] // N_CHIPS
    return dw.reshape(N_CHIPS, 2, r // 2, dw.shape[1])


def kernel(x, norm_pre_mix, w_in, w_gate_up, b_gate, gla_norm, sgu_ln_g, sgu_ln_b, w_spatial, b_spatial, w_branch_gla, w_branch_sgu, w_out, norm_post_mix, norm_pre_ffn, w_ffn_in, w_ffn_out, norm_post_ffn, loss_target, m_norm_pre_mix, m_w_in, m_w_gate_up, m_b_gate, m_gla_norm, m_sgu_ln_g, m_sgu_ln_b, m_w_spatial, m_b_spatial, m_w_branch_gla, m_w_branch_sgu, m_w_out, m_norm_post_mix, m_norm_pre_ffn, m_w_ffn_in, m_w_ffn_out, m_norm_post_ffn, v_norm_pre_mix, v_w_in, v_w_gate_up, v_b_gate, v_gla_norm, v_sgu_ln_g, v_sgu_ln_b, v_w_spatial, v_b_spatial, v_w_branch_gla, v_w_branch_sgu, v_w_out, v_norm_post_mix, v_norm_pre_ffn, v_w_ffn_in, v_w_ffn_out, v_norm_post_ffn):
    chip = 2 * lax.axis_index("x") + lax.axis_index("y")
    xt, tgt = x[0], loss_target[0]

    tiny = jnp.concatenate([w_gate_up[0], _pad_rows(gla_norm[0]), _pad_rows(sgu_ln_g[0]), _pad_rows(sgu_ln_b[0]),
                            jnp.zeros((24, LANES), F32)], axis=0)

    def with_own(gathered, own):
        return lax.dynamic_update_slice(gathered, own[None], (chip, 0, 0))

    w_in_t, m_in_t, v_in_t = w_in[0].T, m_w_in[0].T, v_w_in[0].T
    w_in_b = _transposed_cast(w_in_t)
    g_in, g_tiny = _run_job(_job_gather([w_in_b, tiny]), "gather_w_in")
    g_tiny = with_own(g_tiny, tiny)
    w_all = _relayout_w_in(with_own(g_in, w_in_b))
    cols = lambda a: a.transpose(1, 0, 2).reshape(a.shape[1], N_CHIPS * a.shape[2])
    wgu = jnp.pad(cols(g_tiny[:, 0:16]), ((0, LANES - GLA_RANK), (0, 0)))
    gn = cols(g_tiny[:, 16:20, :64]).reshape(1, GLA_V)
    ln_g = cols(g_tiny[:, 24:28, :64]).reshape(1, 1024)
    ln_b = cols(g_tiny[:, 32:36, :64]).reshape(1, 1024)
    b_sp_t = jnp.pad(b_spatial[0].T, ((0, 0), (0, LANES - SGU_GROUPS)))
    w_sp = w_spatial[0]

    own_rows = [w_branch_gla[0].astype(BF16), w_branch_sgu[0].astype(BF16), w_out[0].astype(BF16), w_ffn_out[0].astype(BF16)]
    (a, proj, alow), g_rows = _inproj_fwd(xt, norm_pre_mix, w_all, job=_job_gather(own_rows))
    rows = lambda g: g.reshape(N_CHIPS * g.shape[1], g.shape[2])
    w_bg, w_bs, w_o, w_fo = [rows(with_own(g, own)) for g, own in zip(g_rows, own_rows)]
    w_fi_b = w_ffn_in[0].astype(BF16)
    fi_top, fi_bot = w_fi_b[:D_MODEL // 2], w_fi_b[D_MODEL // 2:]
    (y_gla, states), (g_top,) = _gla_fwd(proj, alow, wgu, b_gate, gn, job=_job_gather([fi_top]))
    y_sgu = _sgu_fwd(proj, ln_g, ln_b, w_sp, b_sp_t)
    (zg, zs, merged, mix, x1), (g_bot,) = _merge_fwd(xt, proj, y_gla, y_sgu, w_bg, w_bs, w_o, norm_post_mix,
                                                     job=_job_gather([fi_bot]))
    h, f, dgu, dy, dx1, loss, d_gpf, d_gpo = _ffn_fwd_bwd(x1, tgt, with_own(g_top, fi_top), with_own(g_bot, fi_bot),
                                                          w_fo, norm_pre_ffn, norm_post_ffn)

    own_part = lambda c: lax.dynamic_index_in_dim(c, chip, 0, keepdims=False)
    whole = lambda hs: [[(h_, None)] for h_ in hs]
    dw_fo, _ = _tn_matmul(f, dy, "dw_ffn_out")
    dw_fo4 = _halves_view(dw_fo)
    dw_fi, (q_fo,) = _tn_matmul(h, dgu, "dw_ffn_in", job=_job_to_other_core([[(dw_fo4, 0)]]))
    c_fo = _presum(dw_fo4, q_fo, "presum_ffn_out")
    (dmix, dzg, dzs, dp_mrg, dyg, dys, d_gpm), (s_fo, q_fi) = _merge_bwd(
        dx1, mix, proj, zg, zs, w_bg, w_bs, w_o, norm_post_mix,
        job=_join(_job_scatter([c_fo]), _job_to_other_core([[(dw_fi, 0)]])))
    c_fi = _presum(dw_fi, q_fi, "presum_ffn_in")
    dw_c, _ = _tn_matmul(a, dp_mrg, "dw_in_merge")
    dw_o4 = _halves_view(_tn_matmul(merged, dmix, "dw_out")[0])
    dw_bg4 = _halves_view(_tn_matmul(y_gla, dzg, "dw_branch_gla")[0])
    dw_bs4 = _halves_view(_tn_matmul(y_sgu, dzs, "dw_branch_sgu")[0])
    (dp_sgu, d_wsp, d_bsp_t, d_lng, d_lnb), (q_o, q_bg, q_bs, q_c) = _sgu_bwd(
        proj, dys, ln_g, ln_b, w_sp, b_sp_t,
        job=_job_to_other_core([[(dw_o4, 0)], [(dw_bg4, 0)], [(dw_bs4, 0)], [(dw_c, 0)]]))
    c_o, c_bg, c_bs = (_presum(dw_o4, q_o, "presum_out"), _presum(dw_bg4, q_bg, "presum_branch_gla"),
                       _presum(dw_bs4, q_bs, "presum_branch_sgu"))
    h_fo = _sum_slots(own_part(c_fo), s_fo, "sum_ffn_out")
    dw_b, _ = _tn_matmul(a, dp_sgu, "dw_in_sgu")
    (dp_gla, dal, d_gn, d_bg, d_wgu), (s_fi, t_fo, q_b) = _gla_bwd(
        proj, alow, wgu, b_gate, gn, states, dyg,
        job=_join(_job_scatter([c_fi]), _job_to_other_core(whole([h_fo]) + [[(dw_b, 0)]])))
    h_fi = _sum_slots(own_part(c_fi), s_fi, "sum_ffn_in")
    dw_d, _ = _tn_matmul(a, dal, "dw_in_gate")
    dw_a, (s_o, s_bg, s_bs, t_fi, q_d) = _tn_matmul(
        a, dp_gla, "dw_in_gla",
        job=_join(_job_scatter([c_o, c_bg, c_bs]), _job_to_other_core(whole([h_fi]) + [[(dw_d, 0)]])))
    h_o, h_bg, h_bs = (_sum_slots(own_part(c_o), s_o, "sum_out"), _sum_slots(own_part(c_bg), s_bg, "sum_branch_gla"),
                       _sum_slots(own_part(c_bs), s_bs, "sum_branch_sgu"))

    grads, deltas, new_m, new_v = {}, {}, {}, {}

    def update(name, w, m, v, g_mine, g_theirs, job=None):
        (g, d, m2, v2), jres = _adamw(w[0], m[0], v[0], g_mine, g_theirs, "adamw_" + name, job=job)
        grads[name], deltas[name], new_m[name], new_v[name] = g[None], d[None], m2[None], v2[None]
        return jres

    dw_in = [(dw_a, 0), (dw_b, W_GLA), (dw_c, W_GLA + W_SGU), (dw_d, N_MAIN)]
    q_a, t_o, t_bg, t_bs = update("w_ffn_out", w_ffn_out, m_w_ffn_out, v_w_ffn_out, [h_fo], [t_fo],
                                  job=_job_to_other_core([[(dw_a, 0)]] + whole([h_o, h_bg, h_bs])))
    q_in = [q_a, q_b, q_c, q_d]
    hr_in = D_MODEL // 2
    c_in_a, _ = _presum_w_in(dw_in, q_in, 0, hr_in // 8, "presum_w_in_a")
    c_in_b, (s_in_a,) = _presum_w_in(dw_in, q_in, hr_in // 8, 7 * hr_in // 8, "presum_w_in_b",
                                     job=_job_scatter([c_in_a]))
    for store, val in zip((grads, deltas, new_m, new_v),
                          _adamw_sparsecore(w_ffn_in[0], m_w_ffn_in[0], v_w_ffn_in[0], h_fi, t_fi, "adamw_w_ffn_in")):
        store["w_ffn_in"] = val[None]
    update("w_out", w_out, m_w_out, v_w_out, [h_o], [t_o])
    update("w_branch_gla", w_branch_gla, m_w_branch_gla, v_w_branch_gla, [h_bg], [t_bg])
    update("w_branch_sgu", w_branch_sgu, m_w_branch_sgu, v_w_branch_sgu, [h_bs], [t_bs])
    (grad_x, d_g1), (s_in_b,) = _inproj_bwd(xt, dx1, norm_pre_mix, w_all, (dp_gla, dp_sgu, dp_mrg, dal),
                                            job=_job_scatter([c_in_b]))
    h_in = [_sum_slots(own_part(c_in_a), s_in_a, "sum_w_in_a"), _sum_slots(own_part(c_in_b), s_in_b, "sum_w_in_b")]
    t_in = _run_job(_job_to_other_core(whole(h_in)), "swap_w_in")
    for store, val in zip((grads, deltas, new_m, new_v),
                          _adamw_transposed(w_in_t, m_in_t, v_in_t, h_in, t_in, "adamw_w_in")):
        store["w_in"] = val.T[None]

    small_names = ["w_spatial", "w_gate_up", "norm_pre_mix", "norm_post_mix", "norm_pre_ffn", "norm_post_ffn", "b_gate",
                   "b_spatial", "gla_norm", "sgu_ln_g", "sgu_ln_b"]
    loss_out, small = _small_adamw(
        _small_sum([d_wsp, d_wgu, d_g1, d_gpm, d_gpf, d_gpo, d_bg, d_bsp_t, d_gn, d_lng, d_lnb, loss]),
        [w_spatial, w_gate_up, norm_pre_mix, norm_post_mix, norm_pre_ffn, norm_post_ffn, b_gate, b_spatial, gla_norm,
         sgu_ln_g, sgu_ln_b],
        [m_w_spatial, m_w_gate_up, m_norm_pre_mix, m_norm_post_mix, m_norm_pre_ffn, m_norm_post_ffn, m_b_gate,
         m_b_spatial, m_gla_norm, m_sgu_ln_g, m_sgu_ln_b],
        [v_w_spatial, v_w_gate_up, v_norm_pre_mix, v_norm_post_mix, v_norm_pre_ffn, v_norm_post_ffn, v_b_gate,
         v_b_spatial, v_gla_norm, v_sgu_ln_g, v_sgu_ln_b])
    for store, vals in zip((grads, deltas, new_m, new_v), small):
        store.update(zip(small_names, vals))

    order = ["norm_pre_mix", "w_in", "w_gate_up", "b_gate", "gla_norm", "sgu_ln_g", "sgu_ln_b", "w_spatial", "b_spatial",
             "w_branch_gla", "w_branch_sgu", "w_out", "norm_post_mix", "norm_pre_ffn", "w_ffn_in", "w_ffn_out",
             "norm_post_ffn"]
    out = [loss_out, grad_x[None]]
    for store in (grads, deltas, new_m, new_v):
        out.extend(store[n] for n in order)
    return tuple(out)
```

```python
import jax
import jax.numpy as jnp
from jax import lax
from jax.experimental import pallas as pl
from jax.experimental.pallas import tpu as pltpu

F32 = jnp.float32
BF16 = jnp.bfloat16

D_MODEL = 1024
GLA_HEADS = 4
GLA_DK = 128
GLA_DV = 256
GLA_QK = GLA_HEADS * GLA_DK
GLA_V = GLA_HEADS * GLA_DV
GLA_RANK = 16
GLA_TAU = 16.0
CHUNK = 64
SGU_GROUPS = 4
SGU_BLOCK = 128
SGU_DG = 256
D_FF = 2816
EPS = 1e-6
LANES = 128

OFF_Q, OFF_K, OFF_V, OFF_R, OFF_SU, OFF_SV, OFF_GG, OFF_GS, OFF_AL = 0, 512, 1024, 2048, 3072, 4096, 5120, 6144, 7168
W_GLA, W_SGU, W_MRG = 3072, 2048, 2048
N_MAIN = 7168
N_ALL = N_MAIN + LANES
_IN_SPLITS = (GLA_QK, GLA_QK, GLA_V, GLA_V, GLA_RANK, 1024, 1024, 1024, 1024)
_IN_STARTS = tuple(sum(_IN_SPLITS[:i]) for i in range(len(_IN_SPLITS) + 1))
_IN_DST = (OFF_Q, OFF_K, OFF_V, OFF_R, OFF_AL, OFF_SU, OFF_SV, OFF_GG, OFF_GS)
D_IN = _IN_STARTS[-1]

ADAM_LR = 0.001
ADAM_B1 = 0.9
ADAM_B2 = 0.999
ADAM_EPS = 1e-08
ADAM_WD = 0.01
ADAM_STEP = 10

VMEM_LIMIT_BYTES = 56 * 1024 * 1024
N_CHIPS = 4
N_PEER = N_CHIPS - 1
N_DEV = 8
MESH = pl.DeviceIdType.MESH

_NN = (((1,), (0,)), ((), ()))
_NT = (((1,), (1,)), ((), ()))
_TN = (((0,), (0,)), ((), ()))


def _dot(a, b, dims=_NN):
    return lax.dot_general(a, b, dims, preferred_element_type=F32)


def _split(x):
    hi = x.astype(BF16)
    lo = (x - hi.astype(F32)).astype(BF16)
    return hi, lo


def _dot_bf16(a, b, dims=_NN):
    return _dot(a.astype(BF16), b.astype(BF16), dims)


def _dot_exact_lhs(m, x):
    xh, xl = _split(x)
    return _dot(m, xh) + _dot(m, xl)


def _sigmoid(x):
    return 0.5 * jnp.tanh(0.5 * x) + 0.5


def _log_sigmoid(x):
    return jnp.minimum(x, 0.0) - jnp.log(1.0 + jnp.exp(-jnp.abs(x)))


_GELU_C = 0.7978845608028654
_GELU_A = 0.044715


def _gelu_and_grad(x):
    x2 = x * x
    t = jnp.tanh(_GELU_C * (x + _GELU_A * x * x2))
    g = 0.5 * x * (1.0 + t)
    dg = 0.5 * (1.0 + t) + 0.5 * x * (1.0 - t * t) * (_GELU_C * (1.0 + 3.0 * _GELU_A * x2))
    return g, dg


def _gelu(x):
    t = jnp.tanh(_GELU_C * (x + _GELU_A * x * x * x))
    return 0.5 * x * (1.0 + t)


def _rms_stats(x):
    return lax.rsqrt(jnp.mean(x * x, axis=-1, keepdims=True) + EPS)


def _rms_bwd(dout, y, r, g):
    yhat = y * r
    dn = dout * g
    dy = r * (dn - yhat * jnp.mean(dn * yhat, axis=-1, keepdims=True))
    return dy, dout * yhat


def _whole():
    return pl.BlockSpec(memory_space=pltpu.VMEM)


def _row_tile(T, want):
    t = min(T, want)
    assert T % t == 0
    return t


def _chunk_masks(tT, upper):
    row = lax.broadcasted_iota(jnp.int32, (tT, tT), 0)
    col = lax.broadcasted_iota(jnp.int32, (tT, tT), 1)
    same = (row // CHUNK) == (col // CHUNK)
    tri = (col > row) if upper else (col < row)
    return jnp.where(same & tri, 1.0, 0.0).astype(BF16)


class _Job:
    def __init__(self, ins, out_shapes, scratch, start, finish, mid=None):
        self.ins, self.out_shapes, self.scratch = list(ins), list(out_shapes), list(scratch)
        self.start, self.finish, self.mid = start, finish, mid


def _join(*jobs):
    def split(refs, counts):
        out, at = [], 0
        for n in counts:
            out.append(refs[at:at + n])
            at += n
        return out

    ni, no, ns = [len(j.ins) for j in jobs], [len(j.out_shapes) for j in jobs], [len(j.scratch) for j in jobs]

    def start(ins, outs, scr):
        for j, a, b, c in zip(jobs, split(ins, ni), split(outs, no), split(scr, ns)):
            j.start(a, b, c)

    def finish(ins, outs, scr):
        for j, a, b, c in zip(jobs, split(ins, ni), split(outs, no), split(scr, ns)):
            j.finish(a, b, c)

    def mid(ins, outs, scr):
        for j, a, b, c in zip(jobs, split(ins, ni), split(outs, no), split(scr, ns)):
            if j.mid is not None:
                j.mid(a, b, c)

    return _Job(sum((j.ins for j in jobs), []), sum((j.out_shapes for j in jobs), []),
                sum((j.scratch for j in jobs), []), start, finish, mid if any(j.mid for j in jobs) else None)


def _mesh_pos():
    return lax.axis_index("x"), lax.axis_index("y"), lax.axis_index("c")


def _peer_chips(xi, yi):
    return [(1 - xi, yi), (xi, 1 - yi), (1 - xi, 1 - yi)]


def _half(ci, rows):
    return pl.ds(pl.multiple_of(ci * rows, 8), rows)


def _sds(shape, dtype):
    return jax.ShapeDtypeStruct(tuple(shape), dtype)


def _job_gather(arrs):
    n = len(arrs)
    kinds = 12
    Y0, Y1, X1, X0, ON_X, ON_Y, D2D = 0, 1, 2, 3, 4, 5, 6

    def copies(ins, outs, scr):
        send_sems, recv_sems = scr
        xi, yi, ci = _mesh_pos()
        me, cx, cy, cd = 2 * xi + yi, 2 * (1 - xi) + yi, 2 * xi + (1 - yi), 2 * (1 - xi) + (1 - yi)
        to_x, to_y, to_core = (1 - xi, yi, ci), (xi, 1 - yi, ci), (xi, yi, 1 - ci)
        table = []
        for k in range(n):
            qr = arrs[k].shape[0] // 4

            def rows(core, q):
                return pl.ds(pl.multiple_of((2 * core + q) * qr, 8), qr)

            def cp(kind, src, dst, to):
                s = k * kinds + kind
                return pltpu.make_async_remote_copy(src_ref=src, dst_ref=dst, send_sem=send_sems.at[s],
                                                    recv_sem=recv_sems.at[s], device_id=to, device_id_type=MESH)

            def slab(chip, core, q):
                return outs[k].at[chip, rows(core, q)]

            t = {}
            for kind, q, to, frm in ((Y0, 0, to_y, cy), (Y1, 1, to_y, cy), (X1, 1, to_x, cx), (X0, 0, to_x, cx)):
                mine = ins[k].at[rows(ci, q)]
                t[kind] = (cp(kind, mine, slab(me, ci, q), to), cp(kind, mine, slab(frm, ci, q), to))
            t[ON_X] = (cp(ON_X, slab(cy, ci, 0), slab(cy, ci, 0), to_x), cp(ON_X, slab(cy, ci, 0), slab(cd, ci, 0), to_x))
            t[ON_Y] = (cp(ON_Y, slab(cx, ci, 1), slab(cx, ci, 1), to_y), cp(ON_Y, slab(cx, ci, 1), slab(cd, ci, 1), to_y))
            for i, (chip, q) in enumerate(((cy, 0), (cy, 1), (cx, 1), (cx, 0), (cd, 0), (cd, 1))):
                t[D2D + i] = (cp(D2D + i, slab(chip, ci, q), slab(chip, ci, q), to_core),
                              cp(D2D + i, slab(chip, ci, q), slab(chip, 1 - ci, q), to_core))
            table.append(t)
        return table

    def start(ins, outs, scr):
        table = copies(ins, outs, scr)
        for kind in (Y0, X1, Y1, X0):
            for t in table:
                t[kind][0].start()

    def arrived(table, kind, then):
        for t in table:
            t[kind][1].wait_recv()
            for nxt in then:
                t[nxt][0].start()

    def mid(ins, outs, scr):
        table = copies(ins, outs, scr)
        arrived(table, Y0, (ON_X, D2D + 0))
        arrived(table, X1, (ON_Y, D2D + 2))

    def finish(ins, outs, scr):
        table = copies(ins, outs, scr)
        arrived(table, Y1, (D2D + 1,))
        arrived(table, X0, (D2D + 3,))
        arrived(table, ON_X, (D2D + 4,))
        arrived(table, ON_Y, (D2D + 5,))
        for t in table:
            for i in range(6):
                t[D2D + i][1].wait_recv()
            for kind in range(kinds):
                t[kind][0].wait_send()

    dma = pltpu.SemaphoreType.DMA
    return _Job(arrs, [_sds((N_CHIPS,) + a.shape, a.dtype) for a in arrs], [dma((n * kinds,))] * 2, start, finish, mid)


def _job_scatter(parts):
    n = len(parts)

    def copies(ins, outs, scr):
        send_sems, recv_sems = scr
        xi, yi, ci = _mesh_pos()
        res = []
        for k in range(n):
            for j, (px, py) in enumerate(_peer_chips(xi, yi)):
                s = k * N_PEER + j
                res.append(pltpu.make_async_remote_copy(
                    src_ref=ins[k].at[2 * px + py], dst_ref=outs[k].at[j], send_sem=send_sems.at[s],
                    recv_sem=recv_sems.at[s], device_id=(px, py, ci), device_id_type=MESH))
        return res

    def start(ins, outs, scr):
        for cp in copies(ins, outs, scr):
            cp.start()

    def finish(ins, outs, scr):
        for cp in copies(ins, outs, scr):
            cp.wait_recv()
            cp.wait_send()

    dma = pltpu.SemaphoreType.DMA
    return _Job(parts, [_sds((N_PEER,) + p.shape[1:], p.dtype) for p in parts], [dma((n * N_PEER,))] * 2, start, finish)


def _job_to_other_core(groups):
    pieces = [(g, a, off) for g, group in enumerate(groups) for a, off in group]
    n = len(pieces)

    def geometry(group):
        a0, off0 = group[0]
        if off0 is None:
            return a0.shape
        if a0.ndim == 4:
            return (N_CHIPS, a0.shape[2], a0.shape[3])
        return (a0.shape[0] // 2, sum(a.shape[1] for a, _ in group))

    def copies(ins, outs, scr):
        send_sems, recv_sems = scr
        xi, yi, ci = _mesh_pos()
        res = []
        for p, (g, a, off) in enumerate(pieces):
            if off is None:
                give, land = ins[p], outs[g]
            elif a.ndim == 4:
                give, land = ins[p].at[pl.ds(0, N_CHIPS), 1 - ci], outs[g]
            else:
                hr, w = a.shape[0] // 2, a.shape[1]
                give, land = ins[p].at[_half(1 - ci, hr)], outs[g].at[pl.ds(0, hr), pl.ds(off, w)]
            res.append(pltpu.make_async_remote_copy(
                src_ref=give, dst_ref=land, send_sem=send_sems.at[p], recv_sem=recv_sems.at[p],
                device_id=(xi, yi, 1 - ci), device_id_type=MESH))
        return res

    def start(ins, outs, scr):
        for cp in copies(ins, outs, scr):
            cp.start()

    def finish(ins, outs, scr):
        for cp in copies(ins, outs, scr):
            cp.wait_recv()
            cp.wait_send()

    dma = pltpu.SemaphoreType.DMA
    return _Job([a for _, a, _ in pieces], [_sds(geometry(group), group[0][0].dtype) for group in groups],
                [dma((n,))] * 2, start, finish)


def _call(body, *, name, grid, in_specs, out_specs, out_shape, args, scratch_shapes=(), parallel=False, job=None,
          by_core=False):
    n_in, n_out, n_scr = len(in_specs), len(out_specs), len(scratch_shapes)
    hbm = pl.BlockSpec(memory_space=pl.ANY)
    n_ji, n_jo = (len(job.ins), len(job.out_shapes)) if job is not None else (0, 0)
    lead = 1 if by_core else 0

    def kernel_fn(*refs):
        core, refs = refs[:lead], refs[lead:]
        ins, refs = refs[:n_in], refs[n_in:]
        j_ins, refs = refs[:n_ji], refs[n_ji:]
        outs, refs = refs[:n_out], refs[n_out:]
        j_outs, refs = refs[:n_jo], refs[n_jo:]
        scr, j_scr = refs[:n_scr], refs[n_scr:]
        if job is None:
            body(*core, *ins, *outs, *scr)
            return
        ids = [pl.program_id(d) for d in range(len(grid))]
        first = ids[0] == 0
        last = ids[0] == grid[0] - 1
        for d in range(1, len(grid)):
            first = first & (ids[d] == 0)
            last = last & (ids[d] == grid[d] - 1)

        @pl.when(first)
        def _():
            job.start(j_ins, j_outs, j_scr)

        if job.mid is not None and grid[0] >= 4:
            half_way = ids[0] == grid[0] // 2
            for d in range(1, len(grid)):
                half_way = half_way & (ids[d] == 0)

            @pl.when(half_way)
            def _():
                job.mid(j_ins, j_outs, j_scr)

        body(*core, *ins, *outs, *scr)

        @pl.when(last)
        def _():
            if job.mid is not None and grid[0] < 4:
                job.mid(j_ins, j_outs, j_scr)
            job.finish(j_ins, j_outs, j_scr)

    sem = ("parallel" if parallel and job is None else "arbitrary",) * len(grid)
    all_in = list(in_specs) + [hbm] * n_ji
    all_out = list(out_specs) + [hbm] * n_jo
    all_scratch = list(scratch_shapes) + (job.scratch if job is not None else [])
    all_shapes = list(out_shape) + (job.out_shapes if job is not None else [])
    all_args = list(args) + (job.ins if job is not None else [])
    params = pltpu.CompilerParams(dimension_semantics=sem, vmem_limit_bytes=VMEM_LIMIT_BYTES)
    if by_core:
        spec = pltpu.PrefetchScalarGridSpec(num_scalar_prefetch=1, grid=grid, in_specs=all_in, out_specs=all_out,
                                            scratch_shapes=all_scratch)
        core = lax.axis_index("c").astype(jnp.int32).reshape(1)
        res = pl.pallas_call(kernel_fn, name=name, grid_spec=spec, out_shape=all_shapes, compiler_params=params)(
            core, *all_args)
    else:
        res = pl.pallas_call(kernel_fn, name=name, grid=grid, in_specs=all_in, out_specs=all_out, out_shape=all_shapes,
                             scratch_shapes=all_scratch, compiler_params=params)(*all_args)
    return list(res[:n_out]), list(res[n_out:])


def _run_job(job, name):
    n_i, n_o = len(job.ins), len(job.out_shapes)

    def body(*refs):
        ins, outs, scr = refs[:n_i], refs[n_i:n_i + n_o], refs[n_i + n_o:]
        job.start(ins, outs, scr)
        if job.mid is not None:
            job.mid(ins, outs, scr)
        job.finish(ins, outs, scr)

    hbm = pl.BlockSpec(memory_space=pl.ANY)
    return list(pl.pallas_call(body, name=name, in_specs=[hbm] * n_i, out_specs=[hbm] * n_o, out_shape=job.out_shapes,
                               scratch_shapes=job.scratch)(*job.ins))


def _adam_values(w, m, v, g):
    m2 = ADAM_B1 * m + (1.0 - ADAM_B1) * g
    v2 = ADAM_B2 * v + (1.0 - ADAM_B2) * (g * g)
    delta = -ADAM_LR * ((m2 / (1.0 - ADAM_B1 ** ADAM_STEP)) / (jnp.sqrt(v2 / (1.0 - ADAM_B2 ** ADAM_STEP)) + ADAM_EPS)
                        + ADAM_WD * w)
    return delta, m2, v2


_P_WSP, _P_WGU, _P_NORM, _P_BG, _P_BSP, _P_HEAD, _P_LOSS, _P_ROWS = 0, 512, 576, 608, 616, 624, 720, 728


def _small_sum(dgrads):
    def body(dwsp, dwgu, dg1, dgpm, dgpf, dgpo, dbg, dbspt, dgn, dlng, dlnb, loss_in, tot, pack, slots, send_sems,
             recv_sems):
        xi, yi, ci = _mesh_pos()
        chip = 2 * xi + yi

        pack[...] = jnp.zeros_like(pack)
        for g in range(SGU_GROUPS):
            pack[_P_WSP + g * SGU_BLOCK:_P_WSP + (g + 1) * SGU_BLOCK] = dwsp[g]
        for j in range(N_CHIPS):
            pack[_P_WGU + GLA_RANK * j:_P_WGU + GLA_RANK * (j + 1)] = dwgu[0:GLA_RANK, LANES * j:LANES * (j + 1)]
        for k, r in enumerate((dg1, dgpm, dgpf, dgpo)):
            for q in range(8):
                pack[_P_NORM + 8 * k + q:_P_NORM + 8 * k + q + 1] = r[:, LANES * q:LANES * (q + 1)]
        for q in range(4):
            pack[_P_BG + q:_P_BG + q + 1] = dbg[:, LANES * q:LANES * (q + 1)]
        pack[_P_BSP:_P_BSP + SGU_GROUPS] = jnp.transpose(dbspt[...])[0:SGU_GROUPS]
        for k, r in enumerate((dgn, dlng, dlnb)):
            for j in range(N_CHIPS):
                for hh in range(4):
                    row = _P_HEAD + 32 * k + 8 * j + hh
                    pack[row:row + 1, 0:64] = r[:, 256 * hh + 64 * j:256 * hh + 64 * (j + 1)]
        pack[_P_LOSS:_P_LOSS + 1] = loss_in[...]

        to_sibling = pltpu.make_async_remote_copy(
            src_ref=pack, dst_ref=tot, send_sem=send_sems.at[N_PEER], recv_sem=recv_sems.at[N_PEER],
            device_id=(xi, yi, 1 - ci), device_id_type=MESH)
        to_sibling.start()
        to_sibling.wait_recv()
        to_sibling.wait_send()
        pack[...] = pack[...] + tot[...]
        slots[chip] = pack[...]

        def copy(j, slot):
            px, py = _peer_chips(xi, yi)[j]
            return pltpu.make_async_remote_copy(
                src_ref=pack, dst_ref=slots.at[slot(2 * px + py)], send_sem=send_sems.at[j], recv_sem=recv_sems.at[j],
                device_id=(px, py, ci), device_id_type=MESH)

        sends = [copy(j, lambda peer_chip: chip) for j in range(N_PEER)]
        for cp in sends:
            cp.start()
        for j in range(N_PEER):
            copy(j, lambda peer_chip: peer_chip).wait_recv()
        for cp in sends:
            cp.wait_send()
        acc = slots[0]
        for d in range(1, N_CHIPS):
            acc = acc + slots[d]
        tot[...] = acc

    return pl.pallas_call(
        body, name="small_sum", in_specs=[_whole()] * 12, out_specs=_whole(), out_shape=_sds((_P_ROWS, LANES), F32),
        scratch_shapes=[pltpu.VMEM((_P_ROWS, LANES), F32), pltpu.VMEM((N_CHIPS, _P_ROWS, LANES), F32),
                        pltpu.SemaphoreType.DMA((N_PEER + 1,)), pltpu.SemaphoreType.DMA((N_PEER + 1,))],
        compiler_params=pltpu.CompilerParams(vmem_limit_bytes=VMEM_LIMIT_BYTES),
    )(*dgrads)


def _small_adamw(tot, ws, ms, vs):
    n = len(ws)

    def body(*refs):
        tot = refs[0]
        w_refs, m_refs, v_refs = refs[1:1 + n], refs[1 + n:1 + 2 * n], refs[1 + 2 * n:1 + 3 * n]
        loss_out = refs[1 + 3 * n]
        outs = refs[2 + 3 * n:]
        chip = 2 * lax.axis_index("x") + lax.axis_index("y")
        loss_out[...] = tot[_P_LOSS:_P_LOSS + 1, 0:1]

        def step(k, g, pick, put):
            d, m2, v2 = _adam_values(pick(w_refs[k]), pick(m_refs[k]), pick(v_refs[k]), g)
            for o, val in zip((outs[k], outs[n + k], outs[2 * n + k], outs[3 * n + k]), (g, d, m2, v2)):
                put(o, val)

        def whole(ref):
            return ref[0]

        def put_whole(ref, val):
            ref[0] = val

        for g in range(SGU_GROUPS):
            def pick_g(ref, g=g):
                return ref[0, g]

            def put_g(ref, val, g=g):
                ref[0, g] = val

            step(0, tot[_P_WSP + g * SGU_BLOCK:_P_WSP + (g + 1) * SGU_BLOCK], pick_g, put_g)
        step(1, tot[pl.ds(pl.multiple_of(_P_WGU + GLA_RANK * chip, GLA_RANK), GLA_RANK), :], whole, put_whole)
        for k, (base, chunks) in enumerate(((_P_NORM, 8), (_P_NORM + 8, 8), (_P_NORM + 16, 8), (_P_NORM + 24, 8), (_P_BG, 4))):
            for q in range(chunks):
                def pick_q(ref, q=q):
                    return ref[:, LANES * q:LANES * (q + 1)]

                def put_q(ref, val, q=q):
                    ref[:, LANES * q:LANES * (q + 1)] = val

                step(2 + k, tot[base + q:base + q + 1], pick_q, put_q)
        step(7, tot[_P_BSP:_P_BSP + SGU_GROUPS], whole, put_whole)
        for k in range(3):
            mine = tot[pl.ds(pl.multiple_of(_P_HEAD + 32 * k + 8 * chip, 8), 8), :]
            step(8 + k, mine[0:4, 0:64], whole, put_whole)

    shapes = [_sds(w.shape, F32) for w in ws]
    res = pl.pallas_call(
        body, name="small_adamw", in_specs=[_whole()] * (1 + 3 * n), out_specs=[_whole()] * (1 + 4 * n),
        out_shape=[_sds((1, 1), F32)] + shapes * 4,
        compiler_params=pltpu.CompilerParams(vmem_limit_bytes=VMEM_LIMIT_BYTES),
    )(tot, *ws, *ms, *vs)
    return res[0].reshape(()), [list(res[1 + i * n:1 + (i + 1) * n]) for i in range(4)]


def _w_in_pieces():
    blk = D_IN // N_CHIPS
    pieces = []
    for s in range(len(_IN_SPLITS)):
        lo_s, hi_s = _IN_STARTS[s], _IN_STARTS[s + 1]
        for j in range(N_CHIPS):
            lo, hi = max(lo_s, j * blk), min(hi_s, (j + 1) * blk)
            if lo < hi:
                pieces.append((j, lo - j * blk, _IN_DST[s] + lo - lo_s, hi - lo))
    return pieces


def _relayout_w_in(gathered):
    _, rows, blk = gathered.shape
    tr = 256

    def body(g_ref, o_ref):
        o_ref[:, OFF_AL:N_ALL] = jnp.zeros((tr, LANES), BF16)
        for j, src, dst, w in _w_in_pieces():
            o_ref[:, dst:dst + w] = g_ref[j, :, src:src + w]

    res, _ = _call(body, name="relayout_w_in", grid=(rows // tr,), parallel=True,
                   in_specs=[pl.BlockSpec((N_CHIPS, tr, blk), lambda i: (0, i, 0))],
                   out_specs=[pl.BlockSpec((tr, N_ALL), lambda i: (i, 0))],
                   out_shape=[_sds((rows, N_ALL), BF16)], args=(gathered,))
    return res[0]


def _update_row_tile(rows):
    for t in range(min(rows, 256), 7, -8):
        if rows % t == 0:
            return t
    return rows


def _presum_w_in(dws, theirs, row0, rows, name, job=None):
    hr = theirs[0].shape[0]
    blk = D_IN // N_CHIPS
    tr = 64
    assert row0 % tr == 0 and rows % tr == 0
    nh, t0 = hr // tr, row0 // tr
    n = len(dws)

    def body(core_ref, *refs):
        dw_refs, q_refs, (o_ref, s_scr) = refs[:n], refs[n:2 * n], refs[2 * n:]
        for p, (a, off) in enumerate(dws):
            w = a.shape[1]
            s_scr[:, off:off + w] = (dw_refs[p][...] + q_refs[p][...]).astype(BF16)
        for j, src, dst, w in _w_in_pieces():
            o_ref[j, :, src:src + w] = s_scr[:, dst:dst + w]

    in_specs = [pl.BlockSpec((tr, a.shape[1]), lambda i, core: (i + t0 + core[0] * nh, 0)) for a, _ in dws]
    in_specs += [pl.BlockSpec((tr, q.shape[1]), lambda i, core: (i + t0, 0)) for q in theirs]
    res, jres = _call(body, name=name, grid=(rows // tr,), parallel=True, in_specs=in_specs,
                      out_specs=[pl.BlockSpec((N_CHIPS, tr, blk), lambda i, core: (0, i, 0))],
                      out_shape=[_sds((N_CHIPS, rows, blk), BF16)], scratch_shapes=[pltpu.VMEM((tr, N_ALL), BF16)],
                      args=(*[a for a, _ in dws], *theirs), job=job, by_core=True)
    return res[0], jres


def _presum(dw, theirs, name):
    if dw.ndim == 4:
        _, _, hr, c = dw.shape
        tr = _update_row_tile(hr)
        mine = pl.BlockSpec((1, 1, tr, c), lambda j, i, core: (j, core[0], i, 0))
        other = pl.BlockSpec((1, tr, c), lambda j, i, core: (j, i, 0))
    else:
        hr, c = dw.shape[0] // 2, dw.shape[1] // N_CHIPS
        tr = _update_row_tile(hr)
        nh = hr // tr
        mine = pl.BlockSpec((tr, c), lambda j, i, core: (i + core[0] * nh, j))
        other = pl.BlockSpec((tr, c), lambda j, i, core: (i, j))

    def body(core_ref, a_ref, q_ref, o_ref):
        o_ref[...] = (a_ref[...].reshape(tr, c) + q_ref[...].reshape(tr, c)).astype(BF16).reshape(o_ref.shape)

    res, _ = _call(body, name=name, grid=(N_CHIPS, hr // tr), parallel=True, in_specs=[mine, other],
                   out_specs=[pl.BlockSpec((1, tr, c), lambda j, i, core: (j, i, 0))],
                   out_shape=[_sds((N_CHIPS, hr, c), BF16)], args=(dw, theirs), by_core=True)
    return res[0]


def _sum_slots(own, slots, name):
    rows, cols = own.shape
    tr = _update_row_tile(rows)

    def body(own_ref, s_ref, o_ref):
        acc = own_ref[...].astype(F32)
        for j in range(N_PEER):
            acc = acc + s_ref[j].astype(F32)
        o_ref[...] = acc

    res, _ = _call(body, name=name, grid=(rows // tr,), parallel=True,
                   in_specs=[pl.BlockSpec((tr, cols), lambda i: (i, 0)), pl.BlockSpec((N_PEER, tr, cols), lambda i: (0, i, 0))],
                   out_specs=[pl.BlockSpec((tr, cols), lambda i: (i, 0))], out_shape=[_sds((rows, cols), F32)],
                   args=(own, slots))
    return res[0]


def _adamw(w, m, v, g_mine, g_theirs, name, job=None):
    rows, cols = w.shape
    part_rows = [p.shape[0] for p in g_mine]
    assert sum(part_rows) == rows // 2 and [p.shape[0] for p in g_theirs] == part_rows
    tr = _update_row_tile(min(part_rows))
    assert all(r % tr == 0 for r in part_rows)
    nh = (rows // 2) // tr
    starts = [sum(part_rows[:k]) // tr for k in range(len(part_rows))]
    n_parts = len(part_rows)

    def body(core_ref, w_ref, m_ref, v_ref, *rest):
        g_refs, (g_out, d_out, m_out, v_out) = rest[:-4], rest[-4:]
        step = pl.program_id(0)
        mine_here = (step // nh) == core_ref[0]
        q = step % nh
        g = None
        for k in reversed(range(n_parts)):
            val = jnp.where(mine_here, g_refs[k][...], g_refs[n_parts + k][...])
            g = val if g is None else jnp.where(q < starts[k + 1], val, g)
        d, m2, v2 = _adam_values(w_ref[...], m_ref[...], v_ref[...], g)
        g_out[...] = g
        m_out[...] = m2
        v_out[...] = v2
        d_out[...] = d

    def g_spec(k, mine):
        last = part_rows[k] // tr - 1

        def index(i, core):
            half = core[0] if mine else 1 - core[0]
            here = jnp.clip(i % nh - starts[k], 0, last)
            return (jnp.where(i // nh == half, here, jnp.where(i // nh > half, last, 0)), 0)

        return pl.BlockSpec((tr, cols), index)

    spec = pl.BlockSpec((tr, cols), lambda i, core: (i, 0))
    g_specs = [g_spec(k, True) for k in range(n_parts)] + [g_spec(k, False) for k in range(n_parts)]
    return _call(body, name=name, grid=(rows // tr,), parallel=True, in_specs=[spec] * 3 + g_specs,
                 out_specs=[spec] * 4, out_shape=[_sds((rows, cols), F32)] * 4, args=(w, m, v, *g_mine, *g_theirs),
                 job=job, by_core=True)


def _transposed_cast(wt):
    cols, rows = wt.shape

    def body(x_ref, o_ref):
        o_ref[...] = jnp.transpose(x_ref[...]).astype(BF16)

    res, _ = _call(body, name="transpose_w_in", grid=(pl.cdiv(cols, LANES),), parallel=True,
                   in_specs=[pl.BlockSpec((LANES, rows), lambda j: (j, 0))],
                   out_specs=[pl.BlockSpec((rows, LANES), lambda j: (0, j))], out_shape=[_sds((rows, cols), BF16)],
                   args=(wt,))
    return res[0]


def _adamw_transposed(wt, mt, vt, g_mine, g_theirs, name):
    cols, rows = wt.shape
    n_parts = len(g_mine)

    def body(w_ref, m_ref, v_ref, *rest):
        g_refs, (g_out, d_out, m_out, v_out) = rest[:-4], rest[-4:]
        mine = jnp.concatenate([r[...] for r in g_refs[:n_parts]], axis=0)
        theirs = jnp.concatenate([r[...] for r in g_refs[n_parts:]], axis=0)
        first = lax.axis_index("c") == 0
        g = jnp.transpose(jnp.concatenate([jnp.where(first, mine, theirs), jnp.where(first, theirs, mine)], axis=0))
        d, m2, v2 = _adam_values(w_ref[...], m_ref[...], v_ref[...], g)
        g_out[...] = g
        m_out[...] = m2
        v_out[...] = v2
        d_out[...] = d

    spec = pl.BlockSpec((LANES, rows), lambda j: (j, 0))
    g_specs = [pl.BlockSpec((p.shape[0], LANES), lambda j: (0, j)) for p in g_mine] * 2
    res, _ = _call(body, name=name, grid=(pl.cdiv(cols, LANES),), parallel=True, in_specs=[spec] * 3 + g_specs,
                   out_specs=[spec] * 4, out_shape=[_sds((cols, rows), F32)] * 4, args=(wt, mt, vt, *g_mine, *g_theirs))
    return res


def _inproj_fwd(x, g1, w_all, job=None):
    T = x.shape[0]
    tT = _row_tile(T, 512)

    def body(x_ref, g_ref, w_ref, a_ref, proj_ref, alow_ref):
        xv = x_ref[...]
        a = (xv * _rms_stats(xv) * g_ref[...]).astype(BF16)
        a_ref[...] = a
        for j in range(N_MAIN // 1024):
            cols = slice(j * 1024, (j + 1) * 1024)
            proj_ref[:, cols] = _dot(a, w_ref[:, cols]).astype(BF16)
        alow_ref[...] = _dot(a, w_ref[:, N_MAIN:N_ALL])

    row = lambda w: pl.BlockSpec((tT, w), lambda i: (i, 0))
    return _call(
        body, name="inproj_fwd", grid=(T // tT,), parallel=True,
        in_specs=[row(D_MODEL), pl.BlockSpec((1, D_MODEL), lambda i: (0, 0)), _whole()],
        out_specs=[row(D_MODEL), row(N_MAIN), row(LANES)],
        out_shape=[_sds((T, D_MODEL), BF16), _sds((T, N_MAIN), BF16), _sds((T, LANES), F32)],
        args=(x, g1, w_all), job=job)


def _gla_decay_terms(al_ref, wgu_ref, bg_ref, later_ref):
    logit = _dot_bf16(al_ref[...], wgu_ref[...]) + bg_ref[...]
    la = _log_sigmoid(logit) * (1.0 / GLA_TAU)
    delta = _dot_exact_lhs(later_ref[...], la)
    return logit, la, delta


def _gla_fwd(proj, alow, wgu, b_gate, gn, job=None):
    T = proj.shape[0]
    tT = _row_tile(T, 512)
    nc = tT // CHUNK

    def body(q_ref, k_ref, v_ref, r_ref, al_ref, wgu_ref, bg_ref, gn_ref, later_ref, y_ref, st_ref, s_scr):
        @pl.when(pl.program_id(0) == 0)
        def _():
            s_scr[...] = jnp.zeros_like(s_scr)

        _, la, delta = _gla_decay_terms(al_ref, wgu_ref, bg_ref, later_ref)
        kdec = (k_ref[...].astype(F32) * jnp.exp(delta)).astype(BF16)
        heads = range(GLA_HEADS)
        kcs = [slice(h * GLA_DK, (h + 1) * GLA_DK) for h in heads]
        vcs = [slice(h * GLA_DV, (h + 1) * GLA_DV) for h in heads]
        state = [s_scr[h] for h in heads]
        for c in range(nc):
            rows = slice(c * CHUNK, (c + 1) * CHUNK)
            first = slice(c * CHUNK, c * CHUNK + 1)
            dec = jnp.exp(la[first, :] + delta[first, :])
            upd_t = [_dot(v_ref[rows, vcs[h]], kdec[rows, kcs[h]], _TN) for h in heads]
            qs = [(q_ref[rows, kcs[h]].astype(F32) * (GLA_DK ** -0.5)).astype(BF16) for h in heads]
            for h in heads:
                state[h] = state[h] * dec[:, kcs[h]] + upd_t[h]
                st_ref[c, h] = state[h]
            o = [_dot(qs[h], state[h].astype(BF16), _NT) for h in heads]
            for h in heads:
                on = o[h] * _rms_stats(o[h]) * gn_ref[:, vcs[h]]
                rr = r_ref[rows, vcs[h]].astype(F32)
                y_ref[rows, vcs[h]] = (on * (rr * _sigmoid(rr))).astype(BF16)
        for h in heads:
            s_scr[h] = state[h]

    blk = lambda w, j: pl.BlockSpec((tT, w), lambda i: (i, j))
    return _call(
        body, name="gla_fwd", grid=(T // tT,),
        in_specs=[blk(512, 0), blk(512, 1), blk(1024, 1), blk(1024, 2), blk(LANES, 0)] + [_whole()] * 4,
        out_specs=[pl.BlockSpec((tT, GLA_V), lambda i: (i, 0)),
                   pl.BlockSpec((nc, GLA_HEADS, GLA_DV, GLA_DK), lambda i: (i, 0, 0, 0))],
        out_shape=[_sds((T, GLA_V), BF16), _sds((T // CHUNK, GLA_HEADS, GLA_DV, GLA_DK), F32)],
        scratch_shapes=[pltpu.VMEM((GLA_HEADS, GLA_DV, GLA_DK), F32)],
        args=(proj, proj, proj, proj, alow, wgu, b_gate, gn, _chunk_masks(tT, upper=True)), job=job)


def _sgu_mask():
    i = lax.broadcasted_iota(jnp.int32, (SGU_BLOCK, SGU_BLOCK), 0)
    j = lax.broadcasted_iota(jnp.int32, (SGU_BLOCK, SGU_BLOCK), 1)
    return lax.shift_right_logical(j, 6) <= lax.shift_right_logical(i, 6)


def _sgu_merge_fwd(x, proj, y_gla, ln_g, ln_b, w_sp, b_sp_t, w_bg, w_bs, w_o, g_pm, job=None):
    T = x.shape[0]
    tT = _row_tile(T, 512)
    nb = tT // SGU_BLOCK

    def body(x_ref, su_ref, sv_ref, gg_ref, gs_ref, yg_ref, lg_ref, lb_ref, w_ref, b_ref, wbg_ref, wbs_ref, wo_ref,
             g_ref, ys_ref, zg_ref, zs_ref, mg_ref, mix_ref, x1_ref):
        mask = _sgu_mask()
        for g in range(SGU_GROUPS):
            gc = slice(g * SGU_DG, (g + 1) * SGU_DG)
            wm = jnp.where(mask, w_ref[g], 0.0).astype(BF16)
            vf = _gelu(sv_ref[:, gc].astype(F32))
            mu = jnp.mean(vf, axis=-1, keepdims=True)
            vc = vf - mu
            rstd = lax.rsqrt(jnp.mean(vc * vc, axis=-1, keepdims=True) + EPS)
            vn = (vc * rstd * lg_ref[:, gc] + lb_ref[:, gc]).astype(BF16)
            u = _gelu(su_ref[:, gc].astype(F32))
            for b in range(nb):
                rows = slice(b * SGU_BLOCK, (b + 1) * SGU_BLOCK)
                mixed = _dot(wm, vn[rows, :]) + b_ref[:, g:g + 1]
                ys_ref[rows, gc] = (u[rows, :] * mixed).astype(BF16)
        zg = _dot(yg_ref[...], wbg_ref[...])
        zs = _dot(ys_ref[...], wbs_ref[...])
        zg_ref[...] = zg.astype(BF16)
        zs_ref[...] = zs.astype(BF16)
        merged = (_sigmoid(gg_ref[...].astype(F32)) * zg + _sigmoid(gs_ref[...].astype(F32)) * zs).astype(BF16)
        mg_ref[...] = merged
        mix = _dot(merged, wo_ref[...])
        mix_ref[...] = mix.astype(BF16)
        x1_ref[...] = x_ref[...] + mix * _rms_stats(mix) * g_ref[...]

    row = pl.BlockSpec((tT, D_MODEL), lambda i: (i, 0))
    blk = lambda j: pl.BlockSpec((tT, 1024), lambda i: (i, j))
    sds = lambda dt: _sds((T, D_MODEL), dt)
    return _call(body, name="sgu_merge_fwd", grid=(T // tT,), parallel=True,
                 in_specs=[row, blk(3), blk(4), blk(5), blk(6), row] + [_whole()] * 7
                 + [pl.BlockSpec((1, D_MODEL), lambda i: (0, 0))],
                 out_specs=[row] * 6, out_shape=[sds(BF16)] * 5 + [sds(F32)],
                 args=(x, proj, proj, proj, proj, y_gla, ln_g, ln_b, w_sp, b_sp_t, w_bg, w_bs, w_o, g_pm), job=job)


def _ffn_fwd_bwd(x1, tgt, w_fi_top, w_fi_bot, w_fo, g_pf, g_po):
    T = x1.shape[0]
    tT = _row_tile(T, 256)
    half = D_FF // 2
    kh = D_MODEL // 2

    def body(x1_ref, t_ref, top_ref, bot_ref, wfo_ref, gpf_ref, gpo_ref,
             h_ref, f_ref, dgu_ref, dy_ref, dx1_ref, loss_ref, dgpf_ref, dgpo_ref, gu_scr):
        @pl.when(pl.program_id(0) == 0)
        def _():
            loss_ref[...] = jnp.zeros_like(loss_ref)
            dgpf_ref[...] = jnp.zeros_like(dgpf_ref)
            dgpo_ref[...] = jnp.zeros_like(dgpo_ref)

        main = (half // 256) * 256
        pieces = (0, 1, None)

        def w_in_cols(ref, first_slab, p):
            if p is not None:
                return ref[first_slab + p, :, :main]
            return jnp.concatenate([ref[first_slab, :, main:], ref[first_slab + 1, :, main:]], axis=1)

        def w_out_rows(p):
            if p is not None:
                return wfo_ref[p * half:p * half + main, :]
            return jnp.concatenate([wfo_ref[main:half, :], wfo_ref[half + main:2 * half, :]], axis=0)

        def put(ref, base, p, val):
            if p is not None:
                ref[:, base + p * half:base + p * half + main] = val
            else:
                ref[:, base + main:base + half] = val[:, :half - main]
                ref[:, base + half + main:base + 2 * half] = val[:, half - main:]

        def get(ref, base, p):
            if p is not None:
                return ref[:, base + p * half:base + p * half + main]
            return jnp.concatenate([ref[:, base + main:base + half], ref[:, base + half + main:base + 2 * half]], axis=1)

        x1v = x1_ref[...]
        r2 = _rms_stats(x1v)
        h = (x1v * r2 * gpf_ref[...]).astype(BF16)
        h_ref[...] = h
        y = jnp.zeros((tT, D_MODEL), F32)
        for p in pieces:
            gate = _dot(h[:, :kh], w_in_cols(top_ref, 0, p)) + _dot(h[:, kh:], w_in_cols(bot_ref, 0, p))
            up = _dot(h[:, :kh], w_in_cols(top_ref, 2, p)) + _dot(h[:, kh:], w_in_cols(bot_ref, 2, p))
            put(gu_scr, 0, p, gate)
            put(gu_scr, D_FF, p, up)
            f = (gate * _sigmoid(gate) * up).astype(BF16)
            put(f_ref, 0, p, f)
            y = y + _dot(f, w_out_rows(p))
        r3 = _rms_stats(y)
        x2 = x1v + y * r3 * gpo_ref[...]
        err = x2 - t_ref[...]
        loss_ref[...] += jnp.sum(err * err) * (0.5 / D_MODEL)
        dx2 = err * (1.0 / D_MODEL)
        dy, dg = _rms_bwd(dx2, y, r3, gpo_ref[...])
        dgpo_ref[...] += jnp.sum(dg, axis=0, keepdims=True)
        dyb = dy.astype(BF16)
        dy_ref[...] = dyb
        dh_top = jnp.zeros((tT, kh), F32)
        dh_bot = jnp.zeros((tT, kh), F32)
        for p in pieces:
            df = _dot(dyb, w_out_rows(p), _NT)
            gate = get(gu_scr, 0, p)
            up = get(gu_scr, D_FF, p)
            sg = _sigmoid(gate)
            dgate = (df * up * (sg * (1.0 + gate * (1.0 - sg)))).astype(BF16)
            dup = (df * (gate * sg)).astype(BF16)
            put(dgu_ref, 0, p, dgate)
            put(dgu_ref, D_FF, p, dup)
            dh_top = dh_top + _dot(dgate, w_in_cols(top_ref, 0, p), _NT) + _dot(dup, w_in_cols(top_ref, 2, p), _NT)
            dh_bot = dh_bot + _dot(dgate, w_in_cols(bot_ref, 0, p), _NT) + _dot(dup, w_in_cols(bot_ref, 2, p), _NT)
        dh = jnp.concatenate([dh_top, dh_bot], axis=1)
        dx1n, dg2 = _rms_bwd(dh, x1v, r2, gpf_ref[...])
        dgpf_ref[...] += jnp.sum(dg2, axis=0, keepdims=True)
        dx1_ref[...] = dx2 + dx1n

    row = lambda w: pl.BlockSpec((tT, w), lambda i: (i, 0))
    vec = pl.BlockSpec((1, D_MODEL), lambda i: (0, 0))
    res, _ = _call(
        body, name="ffn_fwd_bwd", grid=(T // tT,),
        in_specs=[row(D_MODEL), row(D_MODEL), _whole(), _whole(), _whole(), vec, vec],
        out_specs=[row(D_MODEL), row(D_FF), row(2 * D_FF), row(D_MODEL), row(D_MODEL),
                   pl.BlockSpec((1, LANES), lambda i: (0, 0)), vec, vec],
        out_shape=[_sds((T, D_MODEL), BF16), _sds((T, D_FF), BF16), _sds((T, 2 * D_FF), BF16), _sds((T, D_MODEL), BF16),
                   _sds((T, D_MODEL), F32), _sds((1, LANES), F32), _sds((1, D_MODEL), F32), _sds((1, D_MODEL), F32)],
        scratch_shapes=[pltpu.VMEM((tT, 2 * D_FF), F32)], args=(x1, tgt, w_fi_top, w_fi_bot, w_fo, g_pf, g_po))
    return res


def _merge_bwd(dx1, mix, proj, zg, zs, w_bg, w_bs, w_o, g_pm, job=None):
    T = dx1.shape[0]
    tT = _row_tile(T, 512)

    def body(dx1_ref, mix_ref, gg_ref, gs_ref, zg_ref, zs_ref, wbg_ref, wbs_ref, wo_ref, g_ref,
             dmix_ref, dzg_ref, dzs_ref, dgate_ref, dyg_ref, dys_ref, dgpm_ref):
        @pl.when(pl.program_id(0) == 0)
        def _():
            dgpm_ref[...] = jnp.zeros_like(dgpm_ref)

        mix = mix_ref[...].astype(F32)
        dmix, dg = _rms_bwd(dx1_ref[...], mix, _rms_stats(mix), g_ref[...])
        dgpm_ref[...] += jnp.sum(dg, axis=0, keepdims=True)
        dmb = dmix.astype(BF16)
        dmix_ref[...] = dmb
        dmerged = _dot(dmb, wo_ref[...], _NT)
        for k, (gate_ref, z_ref, w_ref, dz_ref, dy_ref) in enumerate((
                (gg_ref, zg_ref, wbg_ref, dzg_ref, dyg_ref), (gs_ref, zs_ref, wbs_ref, dzs_ref, dys_ref))):
            sg = _sigmoid(gate_ref[...].astype(F32))
            dz = (dmerged * sg).astype(BF16)
            dz_ref[...] = dz
            dgate_ref[:, k * 1024:(k + 1) * 1024] = (dmerged * z_ref[...].astype(F32) * (sg * (1.0 - sg))).astype(BF16)
            dy_ref[...] = _dot(dz, w_ref[...], _NT).astype(BF16)

    row = pl.BlockSpec((tT, D_MODEL), lambda i: (i, 0))
    blk = lambda j: pl.BlockSpec((tT, 1024), lambda i: (i, j))
    vec = pl.BlockSpec((1, D_MODEL), lambda i: (0, 0))
    sds = _sds((T, D_MODEL), BF16)
    return _call(
        body, name="merge_bwd", grid=(T // tT,),
        in_specs=[row, row, blk(5), blk(6), row, row, _whole(), _whole(), _whole(), vec],
        out_specs=[row, row, row, pl.BlockSpec((tT, W_MRG), lambda i: (i, 0)), row, row, vec],
        out_shape=[sds, sds, sds, _sds((T, W_MRG), BF16), sds, sds, _sds((1, D_MODEL), F32)],
        args=(dx1, mix, proj, proj, zg, zs, w_bg, w_bs, w_o, g_pm), job=job)


def _sgu_bwd(proj, dy_sgu, ln_g, ln_b, w_sp, b_sp_t, job=None):
    T = proj.shape[0]
    tT = _row_tile(T, 512)
    nb = tT // SGU_BLOCK

    def body(su_ref, sv_ref, dy_ref, lg_ref, lb_ref, w_ref, b_ref, dp_ref, dw_ref, dbt_ref, dlg_ref, dlb_ref):
        @pl.when(pl.program_id(0) == 0)
        def _():
            dw_ref[...] = jnp.zeros_like(dw_ref)
            dbt_ref[...] = jnp.zeros_like(dbt_ref)
            dlg_ref[...] = jnp.zeros_like(dlg_ref)
            dlb_ref[...] = jnp.zeros_like(dlb_ref)

        mask = _sgu_mask()
        lane = lax.broadcasted_iota(jnp.int32, (SGU_BLOCK, LANES), 1)
        for g in range(SGU_GROUPS):
            gc = slice(g * SGU_DG, (g + 1) * SGU_DG)
            gc_v = slice(1024 + g * SGU_DG, 1024 + (g + 1) * SGU_DG)
            wm = jnp.where(mask, w_ref[g], 0.0).astype(BF16)
            vf, dvf_dsv = _gelu_and_grad(sv_ref[:, gc].astype(F32))
            mu = jnp.mean(vf, axis=-1, keepdims=True)
            vc = vf - mu
            rstd = lax.rsqrt(jnp.mean(vc * vc, axis=-1, keepdims=True) + EPS)
            vhat = vc * rstd
            vn = (vhat * lg_ref[:, gc] + lb_ref[:, gc]).astype(BF16)
            u, du_dsu = _gelu_and_grad(su_ref[:, gc].astype(F32))
            dy = dy_ref[:, gc].astype(F32)
            dmixed = (dy * u).astype(BF16)
            dvn_parts = []
            dw_acc = jnp.zeros((SGU_BLOCK, SGU_BLOCK), F32)
            db_acc = jnp.zeros((SGU_BLOCK, 1), F32)
            for b in range(nb):
                rows = slice(b * SGU_BLOCK, (b + 1) * SGU_BLOCK)
                mixed = _dot(wm, vn[rows, :]) + b_ref[:, g:g + 1]
                dp_ref[rows, gc] = (dy[rows, :] * mixed * du_dsu[rows, :]).astype(BF16)
                dvn_parts.append(_dot(wm, dmixed[rows, :], _TN))
                dw_acc = dw_acc + _dot(dmixed[rows, :], vn[rows, :], _NT)
                db_acc = db_acc + jnp.sum(dmixed[rows, :].astype(F32), axis=-1, keepdims=True)
            dw_ref[g] += jnp.where(mask, dw_acc, 0.0)
            dbt_ref[...] += jnp.where(lane == g, db_acc, 0.0)
            dvn = jnp.concatenate(dvn_parts, axis=0)
            dlg_ref[:, gc] += jnp.sum(dvn * vhat, axis=0, keepdims=True)
            dlb_ref[:, gc] += jnp.sum(dvn, axis=0, keepdims=True)
            dvh = dvn * lg_ref[:, gc]
            dvf = rstd * (dvh - jnp.mean(dvh, axis=-1, keepdims=True)
                          - vhat * jnp.mean(dvh * vhat, axis=-1, keepdims=True))
            dp_ref[:, gc_v] = (dvf * dvf_dsv).astype(BF16)

    blk = lambda j: pl.BlockSpec((tT, 1024), lambda i: (i, j))
    row = lambda w: pl.BlockSpec((tT, w), lambda i: (i, 0))
    vec = pl.BlockSpec((1, 1024), lambda i: (0, 0))
    return _call(
        body, name="sgu_bwd", grid=(T // tT,),
        in_specs=[blk(3), blk(4), row(1024), _whole(), _whole(), _whole(), _whole()],
        out_specs=[row(W_SGU), pl.BlockSpec((SGU_GROUPS, SGU_BLOCK, SGU_BLOCK), lambda i: (0, 0, 0)),
                   pl.BlockSpec((SGU_BLOCK, LANES), lambda i: (0, 0)), vec, vec],
        out_shape=[_sds((T, W_SGU), BF16), _sds((SGU_GROUPS, SGU_BLOCK, SGU_BLOCK), F32), _sds((SGU_BLOCK, LANES), F32),
                   _sds((1, 1024), F32), _sds((1, 1024), F32)],
        args=(proj, proj, dy_sgu, ln_g, ln_b, w_sp, b_sp_t), job=job)


def _gla_bwd(proj, alow, wgu, b_gate, gn, states, dy_gla, job=None):
    T = proj.shape[0]
    tT = _row_tile(T, 512)
    nc = tT // CHUNK
    nt = T // tT

    def body(q_ref, k_ref, v_ref, r_ref, al_ref, wgu_ref, bg_ref, gn_ref, later_ref, earlier_ref, st_ref, sp_ref, dy_ref,
             dp_ref, dal_ref, dgn_ref, dbg_ref, dwgu_ref, g_scr, dd_scr, dt_scr):
        step = pl.program_id(0)

        @pl.when(step == 0)
        def _():
            g_scr[...] = jnp.zeros_like(g_scr)
            dgn_ref[...] = jnp.zeros_like(dgn_ref)
            dbg_ref[...] = jnp.zeros_like(dbg_ref)
            dwgu_ref[...] = jnp.zeros_like(dwgu_ref)

        has_prev = jnp.where(step == nt - 1, 0.0, 1.0)
        logit, la, delta = _gla_decay_terms(al_ref, wgu_ref, bg_ref, later_ref)
        e = jnp.exp(delta)
        kdec_f = k_ref[...].astype(F32) * e
        kdec = kdec_f.astype(BF16)
        heads = range(GLA_HEADS)
        kcs = [slice(h * GLA_DK, (h + 1) * GLA_DK) for h in heads]
        vcs = [slice(h * GLA_DV, (h + 1) * GLA_DV) for h in heads]
        carry = [g_scr[h] for h in heads]
        dgn_acc = [jnp.zeros((1, GLA_DV), F32) for _ in heads]
        for c in reversed(range(nc)):
            rows = slice(c * CHUNK, (c + 1) * CHUNK)
            first = slice(c * CHUNK, c * CHUNK + 1)
            dec = jnp.exp(la[first, :] + delta[first, :])
            s_b = [st_ref[c, h].astype(BF16) for h in heads]
            qs = [(q_ref[rows, kcs[h]].astype(F32) * (GLA_DK ** -0.5)).astype(BF16) for h in heads]
            o = [_dot(qs[h], s_b[h], _NT) for h in heads]
            do = []
            for h in heads:
                rstd = _rms_stats(o[h])
                ohat = o[h] * rstd
                gnh = gn_ref[:, vcs[h]]
                dy = dy_ref[rows, vcs[h]].astype(F32)
                rr = r_ref[rows, vcs[h]].astype(F32)
                sg = _sigmoid(rr)
                don = dy * (rr * sg)
                dp_ref[rows, OFF_R + h * GLA_DV:OFF_R + (h + 1) * GLA_DV] = (
                    dy * (ohat * gnh) * (sg * (1.0 + rr * (1.0 - sg)))).astype(BF16)
                dgn_acc[h] = dgn_acc[h] + jnp.sum(don * ohat, axis=0, keepdims=True)
                dn = don * gnh
                do.append((rstd * (dn - ohat * jnp.mean(dn * ohat, axis=-1, keepdims=True))).astype(BF16))
            dq = [_dot(do[h], s_b[h]) for h in heads]
            g_t = [_dot(do[h], qs[h], _TN) + carry[h] for h in heads]
            g_b = [g_t[h].astype(BF16) for h in heads]
            dv = [_dot(kdec[rows, kcs[h]], g_b[h], _NT) for h in heads]
            dkdec = [_dot(v_ref[rows, vcs[h]], g_b[h]) for h in heads]
            for h in heads:
                s_prev = st_ref[c - 1, h] if c > 0 else sp_ref[0, h] * has_prev
                ddec = jnp.sum(g_t[h] * s_prev, axis=0, keepdims=True)
                carry[h] = g_t[h] * dec[:, kcs[h]]
                dp_ref[rows, OFF_Q + h * GLA_DK:OFF_Q + (h + 1) * GLA_DK] = (dq[h] * (GLA_DK ** -0.5)).astype(BF16)
                dp_ref[rows, OFF_V + h * GLA_DV:OFF_V + (h + 1) * GLA_DV] = dv[h].astype(BF16)
                dp_ref[rows, OFF_K + h * GLA_DK:OFF_K + (h + 1) * GLA_DK] = (dkdec[h] * e[rows, kcs[h]]).astype(BF16)
                dd_scr[rows, kcs[h]] = dkdec[h] * kdec_f[rows, kcs[h]]
                dt_scr[rows, kcs[h]] = jnp.broadcast_to(ddec * dec[:, kcs[h]], (CHUNK, GLA_DK))
        for h in heads:
            g_scr[h] = carry[h]
            dgn_ref[:, vcs[h]] += dgn_acc[h]
        dla = _dot_exact_lhs(earlier_ref[...], dd_scr[...]) + dt_scr[...]
        dlogit = dla * (1.0 / GLA_TAU) * _sigmoid(-logit)
        dbg_ref[...] += jnp.sum(dlogit, axis=0, keepdims=True)
        dwgu_ref[...] += _dot_bf16(al_ref[...], dlogit, _TN)
        dal_ref[...] = _dot_bf16(dlogit, wgu_ref[...], _NT).astype(BF16)

    rev = lambda i: nt - 1 - i
    blk = lambda w, j: pl.BlockSpec((tT, w), lambda i: (rev(i), j))
    st_blk = pl.BlockSpec((nc, GLA_HEADS, GLA_DV, GLA_DK), lambda i: (rev(i), 0, 0, 0))
    sp_blk = pl.BlockSpec((1, GLA_HEADS, GLA_DV, GLA_DK), lambda i: (jnp.maximum(rev(i) * nc - 1, 0), 0, 0, 0))
    return _call(
        body, name="gla_bwd", grid=(nt,),
        in_specs=[blk(512, 0), blk(512, 1), blk(1024, 1), blk(1024, 2), blk(LANES, 0)] + [_whole()] * 5
        + [st_blk, sp_blk, blk(GLA_V, 0)],
        out_specs=[blk(W_GLA, 0), blk(LANES, 0), pl.BlockSpec((1, GLA_V), lambda i: (0, 0)),
                   pl.BlockSpec((1, GLA_QK), lambda i: (0, 0)), pl.BlockSpec((LANES, GLA_QK), lambda i: (0, 0))],
        out_shape=[_sds((T, W_GLA), BF16), _sds((T, LANES), BF16), _sds((1, GLA_V), F32), _sds((1, GLA_QK), F32),
                   _sds((LANES, GLA_QK), F32)],
        scratch_shapes=[pltpu.VMEM((GLA_HEADS, GLA_DV, GLA_DK), F32), pltpu.VMEM((tT, GLA_QK), F32),
                        pltpu.VMEM((tT, GLA_QK), F32)],
        args=(proj, proj, proj, proj, alow, wgu, b_gate, gn, _chunk_masks(tT, upper=True), _chunk_masks(tT, upper=False),
              states, states, dy_gla), job=job)


def _inproj_bwd(x, dx1, g1, w_all, dparts, job=None):
    T = x.shape[0]
    tT = _row_tile(T, 512)
    offs = (0, W_GLA, W_GLA + W_SGU, N_MAIN)

    def body(x_ref, dx1_ref, g_ref, w_ref, *rest):
        part_refs, (dx_ref, dg_ref) = rest[:len(offs)], rest[len(offs):]

        @pl.when(pl.program_id(0) == 0)
        def _():
            dg_ref[...] = jnp.zeros_like(dg_ref)

        da = jnp.zeros((tT, D_MODEL), F32)
        for off, p_ref in zip(offs, part_refs):
            da = da + _dot(p_ref[...], w_ref[:, off:off + p_ref.shape[1]], _NT)
        xv = x_ref[...]
        dx, dg = _rms_bwd(da, xv, _rms_stats(xv), g_ref[...])
        dg_ref[...] += jnp.sum(dg, axis=0, keepdims=True)
        dx_ref[...] = dx1_ref[...] + dx

    row = lambda w: pl.BlockSpec((tT, w), lambda i: (i, 0))
    vec = pl.BlockSpec((1, D_MODEL), lambda i: (0, 0))
    return _call(
        body, name="inproj_bwd", grid=(T // tT,),
        in_specs=[row(D_MODEL), row(D_MODEL), vec, _whole()] + [row(p.shape[1]) for p in dparts],
        out_specs=[row(D_MODEL), vec], out_shape=[_sds((T, D_MODEL), F32), _sds((1, D_MODEL), F32)],
        args=(x, dx1, g1, w_all, *dparts), job=job)


def _tn_matmul(a, b, name, job=None):
    T, M = a.shape
    N = b.shape[1]
    tk = _row_tile(T, 1024)
    tm = M if M <= 1024 else 1408
    tn = N if N <= 3072 else N // 2
    assert M % tm == 0 and N % tn == 0

    def body(a_ref, b_ref, o_ref):
        @pl.when(pl.program_id(2) == 0)
        def _():
            o_ref[...] = _dot(a_ref[...], b_ref[...], _TN)

        @pl.when(pl.program_id(2) > 0)
        def _():
            o_ref[...] += _dot(a_ref[...], b_ref[...], _TN)

    res, jres = _call(
        body, name=name, grid=(M // tm, N // tn, T // tk),
        in_specs=[pl.BlockSpec((tk, tm), lambda i, j, k: (k, i)), pl.BlockSpec((tk, tn), lambda i, j, k: (k, j))],
        out_specs=[pl.BlockSpec((tm, tn), lambda i, j, k: (i, j))], out_shape=[_sds((M, N), F32)], args=(a, b), job=job)
    return res[0], jres


def _pad_rows(a, rows=8):
    return jnp.pad(a, ((0, rows - a.shape[0]), (0, LANES - a.shape[1])))


def _halves_view(dw):
    r = dw.shape[0] // N_CHIPS
    return dw.reshape(N_CHIPS, 2, r // 2, dw.shape[1])


def kernel(x, norm_pre_mix, w_in, w_gate_up, b_gate, gla_norm, sgu_ln_g, sgu_ln_b, w_spatial, b_spatial, w_branch_gla, w_branch_sgu, w_out, norm_post_mix, norm_pre_ffn, w_ffn_in, w_ffn_out, norm_post_ffn, loss_target, m_norm_pre_mix, m_w_in, m_w_gate_up, m_b_gate, m_gla_norm, m_sgu_ln_g, m_sgu_ln_b, m_w_spatial, m_b_spatial, m_w_branch_gla, m_w_branch_sgu, m_w_out, m_norm_post_mix, m_norm_pre_ffn, m_w_ffn_in, m_w_ffn_out, m_norm_post_ffn, v_norm_pre_mix, v_w_in, v_w_gate_up, v_b_gate, v_gla_norm, v_sgu_ln_g, v_sgu_ln_b, v_w_spatial, v_b_spatial, v_w_branch_gla, v_w_branch_sgu, v_w_out, v_norm_post_mix, v_norm_pre_ffn, v_w_ffn_in, v_w_ffn_out, v_norm_post_ffn):
    chip = 2 * lax.axis_index("x") + lax.axis_index("y")
    xt, tgt = x[0], loss_target[0]

    tiny = jnp.concatenate([w_gate_up[0], _pad_rows(gla_norm[0]), _pad_rows(sgu_ln_g[0]), _pad_rows(sgu_ln_b[0]),
                            jnp.zeros((24, LANES), F32)], axis=0)

    def with_own(gathered, own):
        return lax.dynamic_update_slice(gathered, own[None], (chip, 0, 0))

    w_in_t, m_in_t, v_in_t = w_in[0].T, m_w_in[0].T, v_w_in[0].T
    w_in_b = _transposed_cast(w_in_t)
    g_in, g_tiny = _run_job(_job_gather([w_in_b, tiny]), "gather_w_in")
    g_tiny = with_own(g_tiny, tiny)
    w_all = _relayout_w_in(with_own(g_in, w_in_b))
    cols = lambda a: a.transpose(1, 0, 2).reshape(a.shape[1], N_CHIPS * a.shape[2])
    wgu = jnp.pad(cols(g_tiny[:, 0:16]), ((0, LANES - GLA_RANK), (0, 0)))
    gn = cols(g_tiny[:, 16:20, :64]).reshape(1, GLA_V)
    ln_g = cols(g_tiny[:, 24:28, :64]).reshape(1, 1024)
    ln_b = cols(g_tiny[:, 32:36, :64]).reshape(1, 1024)
    b_sp_t = jnp.pad(b_spatial[0].T, ((0, 0), (0, LANES - SGU_GROUPS)))
    w_sp = w_spatial[0]

    own_rows = [w_branch_gla[0].astype(BF16), w_branch_sgu[0].astype(BF16), w_out[0].astype(BF16), w_ffn_out[0].astype(BF16)]
    (a, proj, alow), g_rows = _inproj_fwd(xt, norm_pre_mix, w_all, job=_job_gather(own_rows))
    rows = lambda g: g.reshape(N_CHIPS * g.shape[1], g.shape[2])
    w_bg, w_bs, w_o, w_fo = [rows(with_own(g, own)) for g, own in zip(g_rows, own_rows)]
    w_fi_b = w_ffn_in[0].astype(BF16)
    fi_top, fi_bot = w_fi_b[:D_MODEL // 2], w_fi_b[D_MODEL // 2:]
    (y_gla, states), (g_top,) = _gla_fwd(proj, alow, wgu, b_gate, gn, job=_job_gather([fi_top]))
    (y_sgu, zg, zs, merged, mix, x1), (g_bot,) = _sgu_merge_fwd(
        xt, proj, y_gla, ln_g, ln_b, w_sp, b_sp_t, w_bg, w_bs, w_o, norm_post_mix, job=_job_gather([fi_bot]))
    h, f, dgu, dy, dx1, loss, d_gpf, d_gpo = _ffn_fwd_bwd(x1, tgt, with_own(g_top, fi_top), with_own(g_bot, fi_bot),
                                                          w_fo, norm_pre_ffn, norm_post_ffn)

    own_part = lambda c: lax.dynamic_index_in_dim(c, chip, 0, keepdims=False)
    whole = lambda hs: [[(h_, None)] for h_ in hs]
    dw_fo, _ = _tn_matmul(f, dy, "dw_ffn_out")
    dw_fo4 = _halves_view(dw_fo)
    dw_fi, (q_fo,) = _tn_matmul(h, dgu, "dw_ffn_in", job=_job_to_other_core([[(dw_fo4, 0)]]))
    c_fo = _presum(dw_fo4, q_fo, "presum_ffn_out")
    (dmix, dzg, dzs, dp_mrg, dyg, dys, d_gpm), (s_fo, q_fi) = _merge_bwd(
        dx1, mix, proj, zg, zs, w_bg, w_bs, w_o, norm_post_mix,
        job=_join(_job_scatter([c_fo]), _job_to_other_core([[(dw_fi, 0)]])))
    c_fi = _presum(dw_fi, q_fi, "presum_ffn_in")
    dw_c, _ = _tn_matmul(a, dp_mrg, "dw_in_merge")
    dw_o4 = _halves_view(_tn_matmul(merged, dmix, "dw_out")[0])
    dw_bg4 = _halves_view(_tn_matmul(y_gla, dzg, "dw_branch_gla")[0])
    dw_bs4 = _halves_view(_tn_matmul(y_sgu, dzs, "dw_branch_sgu")[0])
    (dp_sgu, d_wsp, d_bsp_t, d_lng, d_lnb), (q_o, q_bg, q_bs, q_c) = _sgu_bwd(
        proj, dys, ln_g, ln_b, w_sp, b_sp_t,
        job=_job_to_other_core([[(dw_o4, 0)], [(dw_bg4, 0)], [(dw_bs4, 0)], [(dw_c, 0)]]))
    c_o, c_bg, c_bs = (_presum(dw_o4, q_o, "presum_out"), _presum(dw_bg4, q_bg, "presum_branch_gla"),
                       _presum(dw_bs4, q_bs, "presum_branch_sgu"))
    h_fo = _sum_slots(own_part(c_fo), s_fo, "sum_ffn_out")
    dw_b, _ = _tn_matmul(a, dp_sgu, "dw_in_sgu")
    (dp_gla, dal, d_gn, d_bg, d_wgu), (s_fi, t_fo, q_b) = _gla_bwd(
        proj, alow, wgu, b_gate, gn, states, dyg,
        job=_join(_job_scatter([c_fi]), _job_to_other_core(whole([h_fo]) + [[(dw_b, 0)]])))
    h_fi = _sum_slots(own_part(c_fi), s_fi, "sum_ffn_in")
    dw_d, _ = _tn_matmul(a, dal, "dw_in_gate")
    dw_a, (s_o, s_bg, s_bs, t_fi, q_d) = _tn_matmul(
        a, dp_gla, "dw_in_gla",
        job=_join(_job_scatter([c_o, c_bg, c_bs]), _job_to_other_core(whole([h_fi]) + [[(dw_d, 0)]])))
    h_o, h_bg, h_bs = (_sum_slots(own_part(c_o), s_o, "sum_out"), _sum_slots(own_part(c_bg), s_bg, "sum_branch_gla"),
                       _sum_slots(own_part(c_bs), s_bs, "sum_branch_sgu"))

    grads, deltas, new_m, new_v = {}, {}, {}, {}

    def update(name, w, m, v, g_mine, g_theirs, job=None):
        (g, d, m2, v2), jres = _adamw(w[0], m[0], v[0], g_mine, g_theirs, "adamw_" + name, job=job)
        grads[name], deltas[name], new_m[name], new_v[name] = g[None], d[None], m2[None], v2[None]
        return jres

    dw_in = [(dw_a, 0), (dw_b, W_GLA), (dw_c, W_GLA + W_SGU), (dw_d, N_MAIN)]
    q_a, t_o, t_bg, t_bs = update("w_ffn_out", w_ffn_out, m_w_ffn_out, v_w_ffn_out, [h_fo], [t_fo],
                                  job=_job_to_other_core([[(dw_a, 0)]] + whole([h_o, h_bg, h_bs])))
    q_in = [q_a, q_b, q_c, q_d]
    hr_in = D_MODEL // 2
    c_in_a, _ = _presum_w_in(dw_in, q_in, 0, hr_in // 8, "presum_w_in_a")
    c_in_b, (s_in_a,) = _presum_w_in(dw_in, q_in, hr_in // 8, 7 * hr_in // 8, "presum_w_in_b",
                                     job=_job_scatter([c_in_a]))
    update("w_ffn_in", w_ffn_in, m_w_ffn_in, v_w_ffn_in, [h_fi], [t_fi])
    update("w_out", w_out, m_w_out, v_w_out, [h_o], [t_o])
    update("w_branch_gla", w_branch_gla, m_w_branch_gla, v_w_branch_gla, [h_bg], [t_bg])
    update("w_branch_sgu", w_branch_sgu, m_w_branch_sgu, v_w_branch_sgu, [h_bs], [t_bs])
    (grad_x, d_g1), (s_in_b,) = _inproj_bwd(xt, dx1, norm_pre_mix, w_all, (dp_gla, dp_sgu, dp_mrg, dal),
                                            job=_job_scatter([c_in_b]))
    h_in = [_sum_slots(own_part(c_in_a), s_in_a, "sum_w_in_a"), _sum_slots(own_part(c_in_b), s_in_b, "sum_w_in_b")]
    t_in = _run_job(_job_to_other_core(whole(h_in)), "swap_w_in")
    for store, val in zip((grads, deltas, new_m, new_v),
                          _adamw_transposed(w_in_t, m_in_t, v_in_t, h_in, t_in, "adamw_w_in")):
        store["w_in"] = val.T[None]

    small_names = ["w_spatial", "w_gate_up", "norm_pre_mix", "norm_post_mix", "norm_pre_ffn", "norm_post_ffn", "b_gate",
                   "b_spatial", "gla_norm", "sgu_ln_g", "sgu_ln_b"]
    loss_out, small = _small_adamw(
        _small_sum([d_wsp, d_wgu, d_g1, d_gpm, d_gpf, d_gpo, d_bg, d_bsp_t, d_gn, d_lng, d_lnb, loss]),
        [w_spatial, w_gate_up, norm_pre_mix, norm_post_mix, norm_pre_ffn, norm_post_ffn, b_gate, b_spatial, gla_norm,
         sgu_ln_g, sgu_ln_b],
        [m_w_spatial, m_w_gate_up, m_norm_pre_mix, m_norm_post_mix, m_norm_pre_ffn, m_norm_post_ffn, m_b_gate,
         m_b_spatial, m_gla_norm, m_sgu_ln_g, m_sgu_ln_b],
        [v_w_spatial, v_w_gate_up, v_norm_pre_mix, v_norm_post_mix, v_norm_pre_ffn, v_norm_post_ffn, v_b_gate,
         v_b_spatial, v_gla_norm, v_sgu_ln_g, v_sgu_ln_b])
    for store, vals in zip((grads, deltas, new_m, new_v), small):
        store.update(zip(small_names, vals))

    order = ["norm_pre_mix", "w_in", "w_gate_up", "b_gate", "gla_norm", "sgu_ln_g", "sgu_ln_b", "w_spatial", "b_spatial",
             "w_branch_gla", "w_branch_sgu", "w_out", "norm_post_mix", "norm_pre_ffn", "w_ffn_in", "w_ffn_out",
             "norm_post_ffn"]
    out = [loss_out, grad_x[None]]
    for store in (grads, deltas, new_m, new_v):
        out.extend(store[n] for n in order)
    return tuple(out)
```

```python
import jax
import jax.numpy as jnp
from jax import lax
from jax.experimental import pallas as pl
from jax.experimental.pallas import tpu as pltpu

F32 = jnp.float32
BF16 = jnp.bfloat16

D_MODEL = 1024
GLA_HEADS = 4
GLA_DK = 128
GLA_DV = 256
GLA_QK = GLA_HEADS * GLA_DK
GLA_V = GLA_HEADS * GLA_DV
GLA_RANK = 16
GLA_TAU = 16.0
CHUNK = 64
SGU_GROUPS = 4
SGU_BLOCK = 128
SGU_DG = 256
D_FF = 2816
EPS = 1e-6
LANES = 128

OFF_Q, OFF_K, OFF_V, OFF_R, OFF_SU, OFF_SV, OFF_GG, OFF_GS, OFF_AL = 0, 512, 1024, 2048, 3072, 4096, 5120, 6144, 7168
W_GLA, W_SGU, W_MRG = 3072, 2048, 2048
N_MAIN = 7168
N_ALL = N_MAIN + LANES
_IN_SPLITS = (GLA_QK, GLA_QK, GLA_V, GLA_V, GLA_RANK, 1024, 1024, 1024, 1024)
_IN_STARTS = tuple(sum(_IN_SPLITS[:i]) for i in range(len(_IN_SPLITS) + 1))
_IN_DST = (OFF_Q, OFF_K, OFF_V, OFF_R, OFF_AL, OFF_SU, OFF_SV, OFF_GG, OFF_GS)
D_IN = _IN_STARTS[-1]

ADAM_LR = 0.001
ADAM_B1 = 0.9
ADAM_B2 = 0.999
ADAM_EPS = 1e-08
ADAM_WD = 0.01
ADAM_STEP = 10

VMEM_LIMIT_BYTES = 56 * 1024 * 1024
N_CHIPS = 4
N_PEER = N_CHIPS - 1
N_DEV = 8
MESH = pl.DeviceIdType.MESH

_NN = (((1,), (0,)), ((), ()))
_NT = (((1,), (1,)), ((), ()))
_TN = (((0,), (0,)), ((), ()))


def _dot(a, b, dims=_NN):
    return lax.dot_general(a, b, dims, preferred_element_type=F32)


def _split(x):
    hi = x.astype(BF16)
    lo = (x - hi.astype(F32)).astype(BF16)
    return hi, lo


def _dot_bf16(a, b, dims=_NN):
    return _dot(a.astype(BF16), b.astype(BF16), dims)


def _dot_exact_lhs(m, x):
    xh, xl = _split(x)
    return _dot(m, xh) + _dot(m, xl)


def _sigmoid(x):
    return 0.5 * jnp.tanh(0.5 * x) + 0.5


def _log_sigmoid(x):
    return jnp.minimum(x, 0.0) - jnp.log(1.0 + jnp.exp(-jnp.abs(x)))


_GELU_C = 0.7978845608028654
_GELU_A = 0.044715


def _gelu_and_grad(x):
    x2 = x * x
    t = jnp.tanh(_GELU_C * (x + _GELU_A * x * x2))
    g = 0.5 * x * (1.0 + t)
    dg = 0.5 * (1.0 + t) + 0.5 * x * (1.0 - t * t) * (_GELU_C * (1.0 + 3.0 * _GELU_A * x2))
    return g, dg


def _gelu(x):
    t = jnp.tanh(_GELU_C * (x + _GELU_A * x * x * x))
    return 0.5 * x * (1.0 + t)


def _rms_stats(x):
    return lax.rsqrt(jnp.mean(x * x, axis=-1, keepdims=True) + EPS)


def _rms_bwd(dout, y, r, g):
    yhat = y * r
    dn = dout * g
    dy = r * (dn - yhat * jnp.mean(dn * yhat, axis=-1, keepdims=True))
    return dy, dout * yhat


def _whole():
    return pl.BlockSpec(memory_space=pltpu.VMEM)


def _row_tile(T, want):
    t = min(T, want)
    assert T % t == 0
    return t


def _chunk_masks(tT, upper):
    row = lax.broadcasted_iota(jnp.int32, (tT, tT), 0)
    col = lax.broadcasted_iota(jnp.int32, (tT, tT), 1)
    same = (row // CHUNK) == (col // CHUNK)
    tri = (col > row) if upper else (col < row)
    return jnp.where(same & tri, 1.0, 0.0).astype(BF16)


class _Job:
    def __init__(self, ins, out_shapes, scratch, start, finish, mid=None):
        self.ins, self.out_shapes, self.scratch = list(ins), list(out_shapes), list(scratch)
        self.start, self.finish, self.mid = start, finish, mid


def _join(*jobs):
    def split(refs, counts):
        out, at = [], 0
        for n in counts:
            out.append(refs[at:at + n])
            at += n
        return out

    ni, no, ns = [len(j.ins) for j in jobs], [len(j.out_shapes) for j in jobs], [len(j.scratch) for j in jobs]

    def start(ins, outs, scr):
        for j, a, b, c in zip(jobs, split(ins, ni), split(outs, no), split(scr, ns)):
            j.start(a, b, c)

    def finish(ins, outs, scr):
        for j, a, b, c in zip(jobs, split(ins, ni), split(outs, no), split(scr, ns)):
            j.finish(a, b, c)

    def mid(ins, outs, scr):
        for j, a, b, c in zip(jobs, split(ins, ni), split(outs, no), split(scr, ns)):
            if j.mid is not None:
                j.mid(a, b, c)

    return _Job(sum((j.ins for j in jobs), []), sum((j.out_shapes for j in jobs), []),
                sum((j.scratch for j in jobs), []), start, finish, mid if any(j.mid for j in jobs) else None)


def _mesh_pos():
    return lax.axis_index("x"), lax.axis_index("y"), lax.axis_index("c")


def _peer_chips(xi, yi):
    return [(1 - xi, yi), (xi, 1 - yi), (1 - xi, 1 - yi)]


def _half(ci, rows):
    return pl.ds(pl.multiple_of(ci * rows, 8), rows)


def _sds(shape, dtype):
    return jax.ShapeDtypeStruct(tuple(shape), dtype)


def _job_gather(arrs):
    n = len(arrs)
    kinds = 12
    Y0, Y1, X1, X0, ON_X, ON_Y, D2D = 0, 1, 2, 3, 4, 5, 6

    def copies(ins, outs, scr):
        send_sems, recv_sems = scr
        xi, yi, ci = _mesh_pos()
        me, cx, cy, cd = 2 * xi + yi, 2 * (1 - xi) + yi, 2 * xi + (1 - yi), 2 * (1 - xi) + (1 - yi)
        to_x, to_y, to_core = (1 - xi, yi, ci), (xi, 1 - yi, ci), (xi, yi, 1 - ci)
        table = []
        for k in range(n):
            qr = arrs[k].shape[0] // 4

            def rows(core, q):
                return pl.ds(pl.multiple_of((2 * core + q) * qr, 8), qr)

            def cp(kind, src, dst, to):
                s = k * kinds + kind
                return pltpu.make_async_remote_copy(src_ref=src, dst_ref=dst, send_sem=send_sems.at[s],
                                                    recv_sem=recv_sems.at[s], device_id=to, device_id_type=MESH)

            def slab(chip, core, q):
                return outs[k].at[chip, rows(core, q)]

            t = {}
            for kind, q, to, frm in ((Y0, 0, to_y, cy), (Y1, 1, to_y, cy), (X1, 1, to_x, cx), (X0, 0, to_x, cx)):
                mine = ins[k].at[rows(ci, q)]
                t[kind] = (cp(kind, mine, slab(me, ci, q), to), cp(kind, mine, slab(frm, ci, q), to))
            t[ON_X] = (cp(ON_X, slab(cy, ci, 0), slab(cy, ci, 0), to_x), cp(ON_X, slab(cy, ci, 0), slab(cd, ci, 0), to_x))
            t[ON_Y] = (cp(ON_Y, slab(cx, ci, 1), slab(cx, ci, 1), to_y), cp(ON_Y, slab(cx, ci, 1), slab(cd, ci, 1), to_y))
            for i, (chip, q) in enumerate(((cy, 0), (cy, 1), (cx, 1), (cx, 0), (cd, 0), (cd, 1))):
                t[D2D + i] = (cp(D2D + i, slab(chip, ci, q), slab(chip, ci, q), to_core),
                              cp(D2D + i, slab(chip, ci, q), slab(chip, 1 - ci, q), to_core))
            table.append(t)
        return table

    def start(ins, outs, scr):
        table = copies(ins, outs, scr)
        for kind in (Y0, X1, Y1, X0):
            for t in table:
                t[kind][0].start()

    def arrived(table, kind, then):
        for t in table:
            t[kind][1].wait_recv()
            for nxt in then:
                t[nxt][0].start()

    def mid(ins, outs, scr):
        table = copies(ins, outs, scr)
        arrived(table, Y0, (ON_X, D2D + 0))
        arrived(table, X1, (ON_Y, D2D + 2))

    def finish(ins, outs, scr):
        table = copies(ins, outs, scr)
        arrived(table, Y1, (D2D + 1,))
        arrived(table, X0, (D2D + 3,))
        arrived(table, ON_X, (D2D + 4,))
        arrived(table, ON_Y, (D2D + 5,))
        for t in table:
            for i in range(6):
                t[D2D + i][1].wait_recv()
            for kind in range(kinds):
                t[kind][0].wait_send()

    dma = pltpu.SemaphoreType.DMA
    return _Job(arrs, [_sds((N_CHIPS,) + a.shape, a.dtype) for a in arrs], [dma((n * kinds,))] * 2, start, finish, mid)


def _job_scatter(parts):
    n = len(parts)

    def copies(ins, outs, scr):
        send_sems, recv_sems = scr
        xi, yi, ci = _mesh_pos()
        res = []
        for k in range(n):
            for j, (px, py) in enumerate(_peer_chips(xi, yi)):
                s = k * N_PEER + j
                res.append(pltpu.make_async_remote_copy(
                    src_ref=ins[k].at[2 * px + py], dst_ref=outs[k].at[j], send_sem=send_sems.at[s],
                    recv_sem=recv_sems.at[s], device_id=(px, py, ci), device_id_type=MESH))
        return res

    def start(ins, outs, scr):
        for cp in copies(ins, outs, scr):
            cp.start()

    def finish(ins, outs, scr):
        for cp in copies(ins, outs, scr):
            cp.wait_recv()
            cp.wait_send()

    dma = pltpu.SemaphoreType.DMA
    return _Job(parts, [_sds((N_PEER,) + p.shape[1:], p.dtype) for p in parts], [dma((n * N_PEER,))] * 2, start, finish)


def _job_to_other_core(groups):
    pieces = [(g, a, off) for g, group in enumerate(groups) for a, off in group]
    n = len(pieces)

    def geometry(group):
        a0, off0 = group[0]
        if off0 is None:
            return a0.shape
        if a0.ndim == 4:
            return (N_CHIPS, a0.shape[2], a0.shape[3])
        return (a0.shape[0] // 2, sum(a.shape[1] for a, _ in group))

    def copies(ins, outs, scr):
        send_sems, recv_sems = scr
        xi, yi, ci = _mesh_pos()
        res = []
        for p, (g, a, off) in enumerate(pieces):
            if off is None:
                give, land = ins[p], outs[g]
            elif a.ndim == 4:
                give, land = ins[p].at[pl.ds(0, N_CHIPS), 1 - ci], outs[g]
            else:
                hr, w = a.shape[0] // 2, a.shape[1]
                give, land = ins[p].at[_half(1 - ci, hr)], outs[g].at[pl.ds(0, hr), pl.ds(off, w)]
            res.append(pltpu.make_async_remote_copy(
                src_ref=give, dst_ref=land, send_sem=send_sems.at[p], recv_sem=recv_sems.at[p],
                device_id=(xi, yi, 1 - ci), device_id_type=MESH))
        return res

    def start(ins, outs, scr):
        for cp in copies(ins, outs, scr):
            cp.start()

    def finish(ins, outs, scr):
        for cp in copies(ins, outs, scr):
            cp.wait_recv()
            cp.wait_send()

    dma = pltpu.SemaphoreType.DMA
    return _Job([a for _, a, _ in pieces], [_sds(geometry(group), group[0][0].dtype) for group in groups],
                [dma((n,))] * 2, start, finish)


def _call(body, *, name, grid, in_specs, out_specs, out_shape, args, scratch_shapes=(), parallel=False, job=None,
          by_core=False):
    n_in, n_out, n_scr = len(in_specs), len(out_specs), len(scratch_shapes)
    hbm = pl.BlockSpec(memory_space=pl.ANY)
    n_ji, n_jo = (len(job.ins), len(job.out_shapes)) if job is not None else (0, 0)
    lead = 1 if by_core else 0

    def kernel_fn(*refs):
        core, refs = refs[:lead], refs[lead:]
        ins, refs = refs[:n_in], refs[n_in:]
        j_ins, refs = refs[:n_ji], refs[n_ji:]
        outs, refs = refs[:n_out], refs[n_out:]
        j_outs, refs = refs[:n_jo], refs[n_jo:]
        scr, j_scr = refs[:n_scr], refs[n_scr:]
        if job is None:
            body(*core, *ins, *outs, *scr)
            return
        ids = [pl.program_id(d) for d in range(len(grid))]
        first = ids[0] == 0
        last = ids[0] == grid[0] - 1
        for d in range(1, len(grid)):
            first = first & (ids[d] == 0)
            last = last & (ids[d] == grid[d] - 1)

        @pl.when(first)
        def _():
            job.start(j_ins, j_outs, j_scr)

        if job.mid is not None and grid[0] >= 4:
            half_way = ids[0] == grid[0] // 2
            for d in range(1, len(grid)):
                half_way = half_way & (ids[d] == 0)

            @pl.when(half_way)
            def _():
                job.mid(j_ins, j_outs, j_scr)

        body(*core, *ins, *outs, *scr)

        @pl.when(last)
        def _():
            if job.mid is not None and grid[0] < 4:
                job.mid(j_ins, j_outs, j_scr)
            job.finish(j_ins, j_outs, j_scr)

    sem = ("parallel" if parallel and job is None else "arbitrary",) * len(grid)
    all_in = list(in_specs) + [hbm] * n_ji
    all_out = list(out_specs) + [hbm] * n_jo
    all_scratch = list(scratch_shapes) + (job.scratch if job is not None else [])
    all_shapes = list(out_shape) + (job.out_shapes if job is not None else [])
    all_args = list(args) + (job.ins if job is not None else [])
    params = pltpu.CompilerParams(dimension_semantics=sem, vmem_limit_bytes=VMEM_LIMIT_BYTES)
    if by_core:
        spec = pltpu.PrefetchScalarGridSpec(num_scalar_prefetch=1, grid=grid, in_specs=all_in, out_specs=all_out,
                                            scratch_shapes=all_scratch)
        core = lax.axis_index("c").astype(jnp.int32).reshape(1)
        res = pl.pallas_call(kernel_fn, name=name, grid_spec=spec, out_shape=all_shapes, compiler_params=params)(
            core, *all_args)
    else:
        res = pl.pallas_call(kernel_fn, name=name, grid=grid, in_specs=all_in, out_specs=all_out, out_shape=all_shapes,
                             scratch_shapes=all_scratch, compiler_params=params)(*all_args)
    return list(res[:n_out]), list(res[n_out:])


def _run_job(job, name):
    n_i, n_o = len(job.ins), len(job.out_shapes)

    def body(*refs):
        ins, outs, scr = refs[:n_i], refs[n_i:n_i + n_o], refs[n_i + n_o:]
        job.start(ins, outs, scr)
        if job.mid is not None:
            job.mid(ins, outs, scr)
        job.finish(ins, outs, scr)

    hbm = pl.BlockSpec(memory_space=pl.ANY)
    return list(pl.pallas_call(body, name=name, in_specs=[hbm] * n_i, out_specs=[hbm] * n_o, out_shape=job.out_shapes,
                               scratch_shapes=job.scratch)(*job.ins))


def _adam_values(w, m, v, g):
    m2 = ADAM_B1 * m + (1.0 - ADAM_B1) * g
    v2 = ADAM_B2 * v + (1.0 - ADAM_B2) * (g * g)
    delta = -ADAM_LR * ((m2 / (1.0 - ADAM_B1 ** ADAM_STEP)) / (jnp.sqrt(v2 / (1.0 - ADAM_B2 ** ADAM_STEP)) + ADAM_EPS)
                        + ADAM_WD * w)
    return delta, m2, v2


_P_WSP, _P_WGU, _P_NORM, _P_BG, _P_BSP, _P_HEAD, _P_LOSS, _P_ROWS = 0, 512, 576, 608, 616, 624, 720, 728


def _small_sum(dgrads):
    def body(dwsp, dwgu, dg1, dgpm, dgpf, dgpo, dbg, dbspt, dgn, dlng, dlnb, loss_in, tot, pack, slots, send_sems,
             recv_sems):
        xi, yi, ci = _mesh_pos()
        chip = 2 * xi + yi

        pack[...] = jnp.zeros_like(pack)
        for g in range(SGU_GROUPS):
            pack[_P_WSP + g * SGU_BLOCK:_P_WSP + (g + 1) * SGU_BLOCK] = dwsp[g]
        for j in range(N_CHIPS):
            pack[_P_WGU + GLA_RANK * j:_P_WGU + GLA_RANK * (j + 1)] = dwgu[0:GLA_RANK, LANES * j:LANES * (j + 1)]
        for k, r in enumerate((dg1, dgpm, dgpf, dgpo)):
            for q in range(8):
                pack[_P_NORM + 8 * k + q:_P_NORM + 8 * k + q + 1] = r[:, LANES * q:LANES * (q + 1)]
        for q in range(4):
            pack[_P_BG + q:_P_BG + q + 1] = dbg[:, LANES * q:LANES * (q + 1)]
        pack[_P_BSP:_P_BSP + SGU_GROUPS] = jnp.transpose(dbspt[...])[0:SGU_GROUPS]
        for k, r in enumerate((dgn, dlng, dlnb)):
            for j in range(N_CHIPS):
                for hh in range(4):
                    row = _P_HEAD + 32 * k + 8 * j + hh
                    pack[row:row + 1, 0:64] = r[:, 256 * hh + 64 * j:256 * hh + 64 * (j + 1)]
        pack[_P_LOSS:_P_LOSS + 1] = loss_in[...]

        to_sibling = pltpu.make_async_remote_copy(
            src_ref=pack, dst_ref=tot, send_sem=send_sems.at[N_PEER], recv_sem=recv_sems.at[N_PEER],
            device_id=(xi, yi, 1 - ci), device_id_type=MESH)
        to_sibling.start()
        to_sibling.wait_recv()
        to_sibling.wait_send()
        pack[...] = pack[...] + tot[...]
        slots[chip] = pack[...]

        def copy(j, slot):
            px, py = _peer_chips(xi, yi)[j]
            return pltpu.make_async_remote_copy(
                src_ref=pack, dst_ref=slots.at[slot(2 * px + py)], send_sem=send_sems.at[j], recv_sem=recv_sems.at[j],
                device_id=(px, py, ci), device_id_type=MESH)

        sends = [copy(j, lambda peer_chip: chip) for j in range(N_PEER)]
        for cp in sends:
            cp.start()
        for j in range(N_PEER):
            copy(j, lambda peer_chip: peer_chip).wait_recv()
        for cp in sends:
            cp.wait_send()
        acc = slots[0]
        for d in range(1, N_CHIPS):
            acc = acc + slots[d]
        tot[...] = acc

    return pl.pallas_call(
        body, name="small_sum", in_specs=[_whole()] * 12, out_specs=_whole(), out_shape=_sds((_P_ROWS, LANES), F32),
        scratch_shapes=[pltpu.VMEM((_P_ROWS, LANES), F32), pltpu.VMEM((N_CHIPS, _P_ROWS, LANES), F32),
                        pltpu.SemaphoreType.DMA((N_PEER + 1,)), pltpu.SemaphoreType.DMA((N_PEER + 1,))],
        compiler_params=pltpu.CompilerParams(vmem_limit_bytes=VMEM_LIMIT_BYTES),
    )(*dgrads)


def _small_adamw(tot, ws, ms, vs):
    n = len(ws)

    def body(*refs):
        tot = refs[0]
        w_refs, m_refs, v_refs = refs[1:1 + n], refs[1 + n:1 + 2 * n], refs[1 + 2 * n:1 + 3 * n]
        loss_out = refs[1 + 3 * n]
        outs = refs[2 + 3 * n:]
        chip = 2 * lax.axis_index("x") + lax.axis_index("y")
        loss_out[...] = tot[_P_LOSS:_P_LOSS + 1, 0:1]

        def step(k, g, pick, put):
            d, m2, v2 = _adam_values(pick(w_refs[k]), pick(m_refs[k]), pick(v_refs[k]), g)
            for o, val in zip((outs[k], outs[n + k], outs[2 * n + k], outs[3 * n + k]), (g, d, m2, v2)):
                put(o, val)

        def whole(ref):
            return ref[0]

        def put_whole(ref, val):
            ref[0] = val

        for g in range(SGU_GROUPS):
            def pick_g(ref, g=g):
                return ref[0, g]

            def put_g(ref, val, g=g):
                ref[0, g] = val

            step(0, tot[_P_WSP + g * SGU_BLOCK:_P_WSP + (g + 1) * SGU_BLOCK], pick_g, put_g)
        step(1, tot[pl.ds(pl.multiple_of(_P_WGU + GLA_RANK * chip, GLA_RANK), GLA_RANK), :], whole, put_whole)
        for k, (base, chunks) in enumerate(((_P_NORM, 8), (_P_NORM + 8, 8), (_P_NORM + 16, 8), (_P_NORM + 24, 8), (_P_BG, 4))):
            for q in range(chunks):
                def pick_q(ref, q=q):
                    return ref[:, LANES * q:LANES * (q + 1)]

                def put_q(ref, val, q=q):
                    ref[:, LANES * q:LANES * (q + 1)] = val

                step(2 + k, tot[base + q:base + q + 1], pick_q, put_q)
        step(7, tot[_P_BSP:_P_BSP + SGU_GROUPS], whole, put_whole)
        for k in range(3):
            mine = tot[pl.ds(pl.multiple_of(_P_HEAD + 32 * k + 8 * chip, 8), 8), :]
            step(8 + k, mine[0:4, 0:64], whole, put_whole)

    shapes = [_sds(w.shape, F32) for w in ws]
    res = pl.pallas_call(
        body, name="small_adamw", in_specs=[_whole()] * (1 + 3 * n), out_specs=[_whole()] * (1 + 4 * n),
        out_shape=[_sds((1, 1), F32)] + shapes * 4,
        compiler_params=pltpu.CompilerParams(vmem_limit_bytes=VMEM_LIMIT_BYTES),
    )(tot, *ws, *ms, *vs)
    return res[0].reshape(()), [list(res[1 + i * n:1 + (i + 1) * n]) for i in range(4)]


def _w_in_pieces():
    blk = D_IN // N_CHIPS
    pieces = []
    for s in range(len(_IN_SPLITS)):
        lo_s, hi_s = _IN_STARTS[s], _IN_STARTS[s + 1]
        for j in range(N_CHIPS):
            lo, hi = max(lo_s, j * blk), min(hi_s, (j + 1) * blk)
            if lo < hi:
                pieces.append((j, lo - j * blk, _IN_DST[s] + lo - lo_s, hi - lo))
    return pieces


def _relayout_w_in(gathered):
    _, rows, blk = gathered.shape
    tr = 256

    def body(g_ref, o_ref):
        o_ref[:, OFF_AL:N_ALL] = jnp.zeros((tr, LANES), BF16)
        for j, src, dst, w in _w_in_pieces():
            o_ref[:, dst:dst + w] = g_ref[j, :, src:src + w]

    res, _ = _call(body, name="relayout_w_in", grid=(rows // tr,), parallel=True,
                   in_specs=[pl.BlockSpec((N_CHIPS, tr, blk), lambda i: (0, i, 0))],
                   out_specs=[pl.BlockSpec((tr, N_ALL), lambda i: (i, 0))],
                   out_shape=[_sds((rows, N_ALL), BF16)], args=(gathered,))
    return res[0]


def _update_row_tile(rows):
    for t in range(min(rows, 256), 7, -8):
        if rows % t == 0:
            return t
    return rows


def _presum_w_in(dws, theirs, row0, rows, name, job=None):
    hr = theirs[0].shape[0]
    blk = D_IN // N_CHIPS
    tr = 64
    assert row0 % tr == 0 and rows % tr == 0
    nh, t0 = hr // tr, row0 // tr
    n = len(dws)

    def body(core_ref, *refs):
        dw_refs, q_refs, (o_ref, s_scr) = refs[:n], refs[n:2 * n], refs[2 * n:]
        for p, (a, off) in enumerate(dws):
            w = a.shape[1]
            s_scr[:, off:off + w] = (dw_refs[p][...] + q_refs[p][...]).astype(BF16)
        for j, src, dst, w in _w_in_pieces():
            o_ref[j, :, src:src + w] = s_scr[:, dst:dst + w]

    in_specs = [pl.BlockSpec((tr, a.shape[1]), lambda i, core: (i + t0 + core[0] * nh, 0)) for a, _ in dws]
    in_specs += [pl.BlockSpec((tr, q.shape[1]), lambda i, core: (i + t0, 0)) for q in theirs]
    res, jres = _call(body, name=name, grid=(rows // tr,), parallel=True, in_specs=in_specs,
                      out_specs=[pl.BlockSpec((N_CHIPS, tr, blk), lambda i, core: (0, i, 0))],
                      out_shape=[_sds((N_CHIPS, rows, blk), BF16)], scratch_shapes=[pltpu.VMEM((tr, N_ALL), BF16)],
                      args=(*[a for a, _ in dws], *theirs), job=job, by_core=True)
    return res[0], jres


def _presum(dw, theirs, name):
    if dw.ndim == 4:
        _, _, hr, c = dw.shape
        tr = _update_row_tile(hr)
        mine = pl.BlockSpec((1, 1, tr, c), lambda j, i, core: (j, core[0], i, 0))
        other = pl.BlockSpec((1, tr, c), lambda j, i, core: (j, i, 0))
    else:
        hr, c = dw.shape[0] // 2, dw.shape[1] // N_CHIPS
        tr = _update_row_tile(hr)
        nh = hr // tr
        mine = pl.BlockSpec((tr, c), lambda j, i, core: (i + core[0] * nh, j))
        other = pl.BlockSpec((tr, c), lambda j, i, core: (i, j))

    def body(core_ref, a_ref, q_ref, o_ref):
        o_ref[...] = (a_ref[...].reshape(tr, c) + q_ref[...].reshape(tr, c)).astype(BF16).reshape(o_ref.shape)

    res, _ = _call(body, name=name, grid=(N_CHIPS, hr // tr), parallel=True, in_specs=[mine, other],
                   out_specs=[pl.BlockSpec((1, tr, c), lambda j, i, core: (j, i, 0))],
                   out_shape=[_sds((N_CHIPS, hr, c), BF16)], args=(dw, theirs), by_core=True)
    return res[0]


def _sum_slots(own, slots, name):
    rows, cols = own.shape
    tr = _update_row_tile(rows)

    def body(own_ref, s_ref, o_ref):
        acc = own_ref[...].astype(F32)
        for j in range(N_PEER):
            acc = acc + s_ref[j].astype(F32)
        o_ref[...] = acc

    res, _ = _call(body, name=name, grid=(rows // tr,), parallel=True,
                   in_specs=[pl.BlockSpec((tr, cols), lambda i: (i, 0)), pl.BlockSpec((N_PEER, tr, cols), lambda i: (0, i, 0))],
                   out_specs=[pl.BlockSpec((tr, cols), lambda i: (i, 0))], out_shape=[_sds((rows, cols), F32)],
                   args=(own, slots))
    return res[0]


def _adamw(w, m, v, g_mine, g_theirs, name, job=None):
    rows, cols = w.shape
    part_rows = [p.shape[0] for p in g_mine]
    assert sum(part_rows) == rows // 2 and [p.shape[0] for p in g_theirs] == part_rows
    tr = _update_row_tile(min(part_rows))
    assert all(r % tr == 0 for r in part_rows)
    nh = (rows // 2) // tr
    starts = [sum(part_rows[:k]) // tr for k in range(len(part_rows))]
    n_parts = len(part_rows)

    def body(core_ref, w_ref, m_ref, v_ref, *rest):
        g_refs, (g_out, d_out, m_out, v_out) = rest[:-4], rest[-4:]
        step = pl.program_id(0)
        mine_here = (step // nh) == core_ref[0]
        q = step % nh
        g = None
        for k in reversed(range(n_parts)):
            val = jnp.where(mine_here, g_refs[k][...], g_refs[n_parts + k][...])
            g = val if g is None else jnp.where(q < starts[k + 1], val, g)
        d, m2, v2 = _adam_values(w_ref[...], m_ref[...], v_ref[...], g)
        g_out[...] = g
        m_out[...] = m2
        v_out[...] = v2
        d_out[...] = d

    def g_spec(k, mine):
        last = part_rows[k] // tr - 1

        def index(i, core):
            half = core[0] if mine else 1 - core[0]
            here = jnp.clip(i % nh - starts[k], 0, last)
            return (jnp.where(i // nh == half, here, jnp.where(i // nh > half, last, 0)), 0)

        return pl.BlockSpec((tr, cols), index)

    spec = pl.BlockSpec((tr, cols), lambda i, core: (i, 0))
    g_specs = [g_spec(k, True) for k in range(n_parts)] + [g_spec(k, False) for k in range(n_parts)]
    return _call(body, name=name, grid=(rows // tr,), parallel=True, in_specs=[spec] * 3 + g_specs,
                 out_specs=[spec] * 4, out_shape=[_sds((rows, cols), F32)] * 4, args=(w, m, v, *g_mine, *g_theirs),
                 job=job, by_core=True)


def _transposed_cast(wt):
    cols, rows = wt.shape

    def body(x_ref, o_ref):
        o_ref[...] = jnp.transpose(x_ref[...]).astype(BF16)

    res, _ = _call(body, name="transpose_w_in", grid=(pl.cdiv(cols, LANES),), parallel=True,
                   in_specs=[pl.BlockSpec((LANES, rows), lambda j: (j, 0))],
                   out_specs=[pl.BlockSpec((rows, LANES), lambda j: (0, j))], out_shape=[_sds((rows, cols), BF16)],
                   args=(wt,))
    return res[0]


def _adamw_transposed(wt, mt, vt, g_mine, g_theirs, name):
    cols, rows = wt.shape
    n_parts = len(g_mine)

    def body(w_ref, m_ref, v_ref, *rest):
        g_refs, (g_out, d_out, m_out, v_out) = rest[:-4], rest[-4:]
        mine = jnp.concatenate([r[...] for r in g_refs[:n_parts]], axis=0)
        theirs = jnp.concatenate([r[...] for r in g_refs[n_parts:]], axis=0)
        first = lax.axis_index("c") == 0
        g = jnp.transpose(jnp.concatenate([jnp.where(first, mine, theirs), jnp.where(first, theirs, mine)], axis=0))
        d, m2, v2 = _adam_values(w_ref[...], m_ref[...], v_ref[...], g)
        g_out[...] = g
        m_out[...] = m2
        v_out[...] = v2
        d_out[...] = d

    spec = pl.BlockSpec((LANES, rows), lambda j: (j, 0))
    g_specs = [pl.BlockSpec((p.shape[0], LANES), lambda j: (0, j)) for p in g_mine] * 2
    res, _ = _call(body, name=name, grid=(pl.cdiv(cols, LANES),), parallel=True, in_specs=[spec] * 3 + g_specs,
                   out_specs=[spec] * 4, out_shape=[_sds((cols, rows), F32)] * 4, args=(wt, mt, vt, *g_mine, *g_theirs))
    return res


def _inproj_fwd(x, g1, w_all, job=None):
    T = x.shape[0]
    tT = _row_tile(T, 512)

    def body(x_ref, g_ref, w_ref, a_ref, proj_ref, alow_ref):
        xv = x_ref[...]
        a = (xv * _rms_stats(xv) * g_ref[...]).astype(BF16)
        a_ref[...] = a
        for j in range(N_MAIN // 1024):
            cols = slice(j * 1024, (j + 1) * 1024)
            proj_ref[:, cols] = _dot(a, w_ref[:, cols]).astype(BF16)
        alow_ref[...] = _dot(a, w_ref[:, N_MAIN:N_ALL])

    row = lambda w: pl.BlockSpec((tT, w), lambda i: (i, 0))
    return _call(
        body, name="inproj_fwd", grid=(T // tT,), parallel=True,
        in_specs=[row(D_MODEL), pl.BlockSpec((1, D_MODEL), lambda i: (0, 0)), _whole()],
        out_specs=[row(D_MODEL), row(N_MAIN), row(LANES)],
        out_shape=[_sds((T, D_MODEL), BF16), _sds((T, N_MAIN), BF16), _sds((T, LANES), F32)],
        args=(x, g1, w_all), job=job)


def _gla_decay_terms(al_ref, wgu_ref, bg_ref, later_ref):
    logit = _dot_bf16(al_ref[...], wgu_ref[...]) + bg_ref[...]
    la = _log_sigmoid(logit) * (1.0 / GLA_TAU)
    delta = _dot_exact_lhs(later_ref[...], la)
    return logit, la, delta


def _gla_fwd(proj, alow, wgu, b_gate, gn, job=None):
    T = proj.shape[0]
    tT = _row_tile(T, 512)
    nc = tT // CHUNK

    def body(q_ref, k_ref, v_ref, r_ref, al_ref, wgu_ref, bg_ref, gn_ref, later_ref, y_ref, st_ref, s_scr):
        @pl.when(pl.program_id(0) == 0)
        def _():
            s_scr[...] = jnp.zeros_like(s_scr)

        _, la, delta = _gla_decay_terms(al_ref, wgu_ref, bg_ref, later_ref)
        kdec = (k_ref[...].astype(F32) * jnp.exp(delta)).astype(BF16)
        heads = range(GLA_HEADS)
        kcs = [slice(h * GLA_DK, (h + 1) * GLA_DK) for h in heads]
        vcs = [slice(h * GLA_DV, (h + 1) * GLA_DV) for h in heads]
        state = [s_scr[h] for h in heads]
        for c in range(nc):
            rows = slice(c * CHUNK, (c + 1) * CHUNK)
            first = slice(c * CHUNK, c * CHUNK + 1)
            dec = jnp.exp(la[first, :] + delta[first, :])
            upd_t = [_dot(v_ref[rows, vcs[h]], kdec[rows, kcs[h]], _TN) for h in heads]
            qs = [(q_ref[rows, kcs[h]].astype(F32) * (GLA_DK ** -0.5)).astype(BF16) for h in heads]
            for h in heads:
                state[h] = state[h] * dec[:, kcs[h]] + upd_t[h]
                st_ref[c, h] = state[h]
            o = [_dot(qs[h], state[h].astype(BF16), _NT) for h in heads]
            for h in heads:
                on = o[h] * _rms_stats(o[h]) * gn_ref[:, vcs[h]]
                rr = r_ref[rows, vcs[h]].astype(F32)
                y_ref[rows, vcs[h]] = (on * (rr * _sigmoid(rr))).astype(BF16)
        for h in heads:
            s_scr[h] = state[h]

    blk = lambda w, j: pl.BlockSpec((tT, w), lambda i: (i, j))
    return _call(
        body, name="gla_fwd", grid=(T // tT,),
        in_specs=[blk(512, 0), blk(512, 1), blk(1024, 1), blk(1024, 2), blk(LANES, 0)] + [_whole()] * 4,
        out_specs=[pl.BlockSpec((tT, GLA_V), lambda i: (i, 0)),
                   pl.BlockSpec((nc, GLA_HEADS, GLA_DV, GLA_DK), lambda i: (i, 0, 0, 0))],
        out_shape=[_sds((T, GLA_V), BF16), _sds((T // CHUNK, GLA_HEADS, GLA_DV, GLA_DK), F32)],
        scratch_shapes=[pltpu.VMEM((GLA_HEADS, GLA_DV, GLA_DK), F32)],
        args=(proj, proj, proj, proj, alow, wgu, b_gate, gn, _chunk_masks(tT, upper=True)), job=job)


def _sgu_mask():
    i = lax.broadcasted_iota(jnp.int32, (SGU_BLOCK, SGU_BLOCK), 0)
    j = lax.broadcasted_iota(jnp.int32, (SGU_BLOCK, SGU_BLOCK), 1)
    return lax.shift_right_logical(j, 6) <= lax.shift_right_logical(i, 6)


def _sgu_merge_fwd(x, proj, y_gla, ln_g, ln_b, w_sp, b_sp_t, w_bg, w_bs, w_o, g_pm, job=None):
    T = x.shape[0]
    tT = _row_tile(T, 512)
    nb = tT // SGU_BLOCK

    def body(x_ref, su_ref, sv_ref, gg_ref, gs_ref, yg_ref, lg_ref, lb_ref, w_ref, b_ref, wbg_ref, wbs_ref, wo_ref,
             g_ref, ys_ref, zg_ref, zs_ref, mg_ref, mix_ref, x1_ref):
        mask = _sgu_mask()
        for g in range(SGU_GROUPS):
            gc = slice(g * SGU_DG, (g + 1) * SGU_DG)
            wm = jnp.where(mask, w_ref[g], 0.0).astype(BF16)
            vf = _gelu(sv_ref[:, gc].astype(F32))
            mu = jnp.mean(vf, axis=-1, keepdims=True)
            vc = vf - mu
            rstd = lax.rsqrt(jnp.mean(vc * vc, axis=-1, keepdims=True) + EPS)
            vn = (vc * rstd * lg_ref[:, gc] + lb_ref[:, gc]).astype(BF16)
            u = _gelu(su_ref[:, gc].astype(F32))
            for b in range(nb):
                rows = slice(b * SGU_BLOCK, (b + 1) * SGU_BLOCK)
                mixed = _dot(wm, vn[rows, :]) + b_ref[:, g:g + 1]
                ys_ref[rows, gc] = (u[rows, :] * mixed).astype(BF16)
        zg = _dot(yg_ref[...], wbg_ref[...])
        zs = _dot(ys_ref[...], wbs_ref[...])
        zg_ref[...] = zg.astype(BF16)
        zs_ref[...] = zs.astype(BF16)
        merged = (_sigmoid(gg_ref[...].astype(F32)) * zg + _sigmoid(gs_ref[...].astype(F32)) * zs).astype(BF16)
        mg_ref[...] = merged
        mix = _dot(merged, wo_ref[...])
        mix_ref[...] = mix.astype(BF16)
        x1_ref[...] = x_ref[...] + mix * _rms_stats(mix) * g_ref[...]

    row = pl.BlockSpec((tT, D_MODEL), lambda i: (i, 0))
    blk = lambda j: pl.BlockSpec((tT, 1024), lambda i: (i, j))
    sds = lambda dt: _sds((T, D_MODEL), dt)
    return _call(body, name="sgu_merge_fwd", grid=(T // tT,), parallel=True,
                 in_specs=[row, blk(3), blk(4), blk(5), blk(6), row] + [_whole()] * 7
                 + [pl.BlockSpec((1, D_MODEL), lambda i: (0, 0))],
                 out_specs=[row] * 6, out_shape=[sds(BF16)] * 5 + [sds(F32)],
                 args=(x, proj, proj, proj, proj, y_gla, ln_g, ln_b, w_sp, b_sp_t, w_bg, w_bs, w_o, g_pm), job=job)


def _ffn_fwd_bwd(x1, tgt, w_fi_top, w_fi_bot, w_fo, g_pf, g_po):
    T = x1.shape[0]
    tT = _row_tile(T, 256)
    half = D_FF // 2
    kh = D_MODEL // 2

    def body(x1_ref, t_ref, top_ref, bot_ref, wfo_ref, gpf_ref, gpo_ref,
             h_ref, f_ref, dgu_ref, dy_ref, dx1_ref, loss_ref, dgpf_ref, dgpo_ref, gu_scr):
        @pl.when(pl.program_id(0) == 0)
        def _():
            loss_ref[...] = jnp.zeros_like(loss_ref)
            dgpf_ref[...] = jnp.zeros_like(dgpf_ref)
            dgpo_ref[...] = jnp.zeros_like(dgpo_ref)

        main = (half // 256) * 256
        pieces = (0, 1, None)

        def w_in_cols(ref, first_slab, p):
            if p is not None:
                return ref[first_slab + p, :, :main]
            return jnp.concatenate([ref[first_slab, :, main:], ref[first_slab + 1, :, main:]], axis=1)

        def w_out_rows(p):
            if p is not None:
                return wfo_ref[p * half:p * half + main, :]
            return jnp.concatenate([wfo_ref[main:half, :], wfo_ref[half + main:2 * half, :]], axis=0)

        def put(ref, base, p, val):
            if p is not None:
                ref[:, base + p * half:base + p * half + main] = val
            else:
                ref[:, base + main:base + half] = val[:, :half - main]
                ref[:, base + half + main:base + 2 * half] = val[:, half - main:]

        def get(ref, base, p):
            if p is not None:
                return ref[:, base + p * half:base + p * half + main]
            return jnp.concatenate([ref[:, base + main:base + half], ref[:, base + half + main:base + 2 * half]], axis=1)

        x1v = x1_ref[...]
        r2 = _rms_stats(x1v)
        h = (x1v * r2 * gpf_ref[...]).astype(BF16)
        h_ref[...] = h
        y = jnp.zeros((tT, D_MODEL), F32)
        for p in pieces:
            gate = _dot(h[:, :kh], w_in_cols(top_ref, 0, p)) + _dot(h[:, kh:], w_in_cols(bot_ref, 0, p))
            up = _dot(h[:, :kh], w_in_cols(top_ref, 2, p)) + _dot(h[:, kh:], w_in_cols(bot_ref, 2, p))
            put(gu_scr, 0, p, gate)
            put(gu_scr, D_FF, p, up)
            f = (gate * _sigmoid(gate) * up).astype(BF16)
            put(f_ref, 0, p, f)
            y = y + _dot(f, w_out_rows(p))
        r3 = _rms_stats(y)
        x2 = x1v + y * r3 * gpo_ref[...]
        err = x2 - t_ref[...]
        loss_ref[...] += jnp.sum(err * err) * (0.5 / D_MODEL)
        dx2 = err * (1.0 / D_MODEL)
        dy, dg = _rms_bwd(dx2, y, r3, gpo_ref[...])
        dgpo_ref[...] += jnp.sum(dg, axis=0, keepdims=True)
        dyb = dy.astype(BF16)
        dy_ref[...] = dyb
        dh_top = jnp.zeros((tT, kh), F32)
        dh_bot = jnp.zeros((tT, kh), F32)
        for p in pieces:
            df = _dot(dyb, w_out_rows(p), _NT)
            gate = get(gu_scr, 0, p)
            up = get(gu_scr, D_FF, p)
            sg = _sigmoid(gate)
            dgate = (df * up * (sg * (1.0 + gate * (1.0 - sg)))).astype(BF16)
            dup = (df * (gate * sg)).astype(BF16)
            put(dgu_ref, 0, p, dgate)
            put(dgu_ref, D_FF, p, dup)
            dh_top = dh_top + _dot(dgate, w_in_cols(top_ref, 0, p), _NT) + _dot(dup, w_in_cols(top_ref, 2, p), _NT)
            dh_bot = dh_bot + _dot(dgate, w_in_cols(bot_ref, 0, p), _NT) + _dot(dup, w_in_cols(bot_ref, 2, p), _NT)
        dh = jnp.concatenate([dh_top, dh_bot], axis=1)
        dx1n, dg2 = _rms_bwd(dh, x1v, r2, gpf_ref[...])
        dgpf_ref[...] += jnp.sum(dg2, axis=0, keepdims=True)
        dx1_ref[...] = dx2 + dx1n

    row = lambda w: pl.BlockSpec((tT, w), lambda i: (i, 0))
    vec = pl.BlockSpec((1, D_MODEL), lambda i: (0, 0))
    res, _ = _call(
        body, name="ffn_fwd_bwd", grid=(T // tT,),
        in_specs=[row(D_MODEL), row(D_MODEL), _whole(), _whole(), _whole(), vec, vec],
        out_specs=[row(D_MODEL), row(D_FF), row(2 * D_FF), row(D_MODEL), row(D_MODEL),
                   pl.BlockSpec((1, LANES), lambda i: (0, 0)), vec, vec],
        out_shape=[_sds((T, D_MODEL), BF16), _sds((T, D_FF), BF16), _sds((T, 2 * D_FF), BF16), _sds((T, D_MODEL), BF16),
                   _sds((T, D_MODEL), F32), _sds((1, LANES), F32), _sds((1, D_MODEL), F32), _sds((1, D_MODEL), F32)],
        scratch_shapes=[pltpu.VMEM((tT, 2 * D_FF), F32)], args=(x1, tgt, w_fi_top, w_fi_bot, w_fo, g_pf, g_po))
    return res


def _merge_sgu_bwd(dx1, mix, proj, zg, zs, w_bg, w_bs, w_o, g_pm, ln_g, ln_b, w_sp, b_sp_t, job=None):
    T = dx1.shape[0]
    tT = _row_tile(T, 256)
    nb = tT // SGU_BLOCK

    def body(dx1_ref, mix_ref, su_ref, sv_ref, gg_ref, gs_ref, zg_ref, zs_ref, wbg_ref, wbs_ref, wo_ref, g_ref,
             lg_ref, lb_ref, w_ref, b_ref,
             dmix_ref, dzg_ref, dzs_ref, dgate_ref, dyg_ref, dp_ref, dgpm_ref, dw_ref, dbt_ref, dlg_ref, dlb_ref):
        @pl.when(pl.program_id(0) == 0)
        def _():
            for ref in (dgpm_ref, dw_ref, dbt_ref, dlg_ref, dlb_ref):
                ref[...] = jnp.zeros_like(ref)

        mix = mix_ref[...].astype(F32)
        dmix, dg = _rms_bwd(dx1_ref[...], mix, _rms_stats(mix), g_ref[...])
        dgpm_ref[...] += jnp.sum(dg, axis=0, keepdims=True)
        dmb = dmix.astype(BF16)
        dmix_ref[...] = dmb
        dmerged = _dot(dmb, wo_ref[...], _NT)
        dys = None
        for k, (gate_ref, z_ref, w_br_ref, dz_ref) in enumerate(((gg_ref, zg_ref, wbg_ref, dzg_ref),
                                                                 (gs_ref, zs_ref, wbs_ref, dzs_ref))):
            sg = _sigmoid(gate_ref[...].astype(F32))
            dz = (dmerged * sg).astype(BF16)
            dz_ref[...] = dz
            dgate_ref[:, k * 1024:(k + 1) * 1024] = (dmerged * z_ref[...].astype(F32) * (sg * (1.0 - sg))).astype(BF16)
            dy_branch = _dot(dz, w_br_ref[...], _NT)
            if k == 0:
                dyg_ref[...] = dy_branch.astype(BF16)
            else:
                dys = dy_branch

        mask = _sgu_mask()
        lane = lax.broadcasted_iota(jnp.int32, (SGU_BLOCK, LANES), 1)
        for g in range(SGU_GROUPS):
            gc = slice(g * SGU_DG, (g + 1) * SGU_DG)
            gc_v = slice(1024 + g * SGU_DG, 1024 + (g + 1) * SGU_DG)
            wm = jnp.where(mask, w_ref[g], 0.0).astype(BF16)
            vf, dvf_dsv = _gelu_and_grad(sv_ref[:, gc].astype(F32))
            mu = jnp.mean(vf, axis=-1, keepdims=True)
            vc = vf - mu
            rstd = lax.rsqrt(jnp.mean(vc * vc, axis=-1, keepdims=True) + EPS)
            vhat = vc * rstd
            vn = (vhat * lg_ref[:, gc] + lb_ref[:, gc]).astype(BF16)
            u, du_dsu = _gelu_and_grad(su_ref[:, gc].astype(F32))
            dy = dys[:, gc]
            dmixed = (dy * u).astype(BF16)
            dvn_parts = []
            dw_acc = jnp.zeros((SGU_BLOCK, SGU_BLOCK), F32)
            db_acc = jnp.zeros((SGU_BLOCK, 1), F32)
            for b in range(nb):
                rows = slice(b * SGU_BLOCK, (b + 1) * SGU_BLOCK)
                mixed = _dot(wm, vn[rows, :]) + b_ref[:, g:g + 1]
                dp_ref[rows, gc] = (dy[rows, :] * mixed * du_dsu[rows, :]).astype(BF16)
                dvn_parts.append(_dot(wm, dmixed[rows, :], _TN))
                dw_acc = dw_acc + _dot(dmixed[rows, :], vn[rows, :], _NT)
                db_acc = db_acc + jnp.sum(dmixed[rows, :].astype(F32), axis=-1, keepdims=True)
            dw_ref[g] += jnp.where(mask, dw_acc, 0.0)
            dbt_ref[...] += jnp.where(lane == g, db_acc, 0.0)
            dvn = jnp.concatenate(dvn_parts, axis=0)
            dlg_ref[:, gc] += jnp.sum(dvn * vhat, axis=0, keepdims=True)
            dlb_ref[:, gc] += jnp.sum(dvn, axis=0, keepdims=True)
            dvh = dvn * lg_ref[:, gc]
            dvf = rstd * (dvh - jnp.mean(dvh, axis=-1, keepdims=True)
                          - vhat * jnp.mean(dvh * vhat, axis=-1, keepdims=True))
            dp_ref[:, gc_v] = (dvf * dvf_dsv).astype(BF16)

    row = pl.BlockSpec((tT, D_MODEL), lambda i: (i, 0))
    blk = lambda j: pl.BlockSpec((tT, 1024), lambda i: (i, j))
    vec = pl.BlockSpec((1, D_MODEL), lambda i: (0, 0))
    wide = lambda w: pl.BlockSpec((tT, w), lambda i: (i, 0))
    sds = _sds((T, D_MODEL), BF16)
    return _call(
        body, name="merge_sgu_bwd", grid=(T // tT,),
        in_specs=[row, row, blk(3), blk(4), blk(5), blk(6), row, row] + [_whole()] * 3 + [vec] + [_whole()] * 4,
        out_specs=[row, row, row, wide(W_MRG), row, wide(W_SGU), vec,
                   pl.BlockSpec((SGU_GROUPS, SGU_BLOCK, SGU_BLOCK), lambda i: (0, 0, 0)),
                   pl.BlockSpec((SGU_BLOCK, LANES), lambda i: (0, 0)), vec, vec],
        out_shape=[sds, sds, sds, _sds((T, W_MRG), BF16), sds, _sds((T, W_SGU), BF16), _sds((1, D_MODEL), F32),
                   _sds((SGU_GROUPS, SGU_BLOCK, SGU_BLOCK), F32), _sds((SGU_BLOCK, LANES), F32),
                   _sds((1, 1024), F32), _sds((1, 1024), F32)],
        args=(dx1, mix, proj, proj, proj, proj, zg, zs, w_bg, w_bs, w_o, g_pm, ln_g, ln_b, w_sp, b_sp_t), job=job)


def _gla_bwd(proj, alow, wgu, b_gate, gn, states, dy_gla, job=None):
    T = proj.shape[0]
    tT = _row_tile(T, 512)
    nc = tT // CHUNK
    nt = T // tT

    def body(q_ref, k_ref, v_ref, r_ref, al_ref, wgu_ref, bg_ref, gn_ref, later_ref, earlier_ref, st_ref, sp_ref, dy_ref,
             dp_ref, dal_ref, dgn_ref, dbg_ref, dwgu_ref, g_scr, dd_scr, dt_scr):
        step = pl.program_id(0)

        @pl.when(step == 0)
        def _():
            g_scr[...] = jnp.zeros_like(g_scr)
            dgn_ref[...] = jnp.zeros_like(dgn_ref)
            dbg_ref[...] = jnp.zeros_like(dbg_ref)
            dwgu_ref[...] = jnp.zeros_like(dwgu_ref)

        has_prev = jnp.where(step == nt - 1, 0.0, 1.0)
        logit, la, delta = _gla_decay_terms(al_ref, wgu_ref, bg_ref, later_ref)
        e = jnp.exp(delta)
        kdec_f = k_ref[...].astype(F32) * e
        kdec = kdec_f.astype(BF16)
        heads = range(GLA_HEADS)
        kcs = [slice(h * GLA_DK, (h + 1) * GLA_DK) for h in heads]
        vcs = [slice(h * GLA_DV, (h + 1) * GLA_DV) for h in heads]
        carry = [g_scr[h] for h in heads]
        dgn_acc = [jnp.zeros((1, GLA_DV), F32) for _ in heads]
        for c in reversed(range(nc)):
            rows = slice(c * CHUNK, (c + 1) * CHUNK)
            first = slice(c * CHUNK, c * CHUNK + 1)
            dec = jnp.exp(la[first, :] + delta[first, :])
            s_b = [st_ref[c, h].astype(BF16) for h in heads]
            qs = [(q_ref[rows, kcs[h]].astype(F32) * (GLA_DK ** -0.5)).astype(BF16) for h in heads]
            o = [_dot(qs[h], s_b[h], _NT) for h in heads]
            do = []
            for h in heads:
                rstd = _rms_stats(o[h])
                ohat = o[h] * rstd
                gnh = gn_ref[:, vcs[h]]
                dy = dy_ref[rows, vcs[h]].astype(F32)
                rr = r_ref[rows, vcs[h]].astype(F32)
                sg = _sigmoid(rr)
                don = dy * (rr * sg)
                dp_ref[rows, OFF_R + h * GLA_DV:OFF_R + (h + 1) * GLA_DV] = (
                    dy * (ohat * gnh) * (sg * (1.0 + rr * (1.0 - sg)))).astype(BF16)
                dgn_acc[h] = dgn_acc[h] + jnp.sum(don * ohat, axis=0, keepdims=True)
                dn = don * gnh
                do.append((rstd * (dn - ohat * jnp.mean(dn * ohat, axis=-1, keepdims=True))).astype(BF16))
            dq = [_dot(do[h], s_b[h]) for h in heads]
            g_t = [_dot(do[h], qs[h], _TN) + carry[h] for h in heads]
            g_b = [g_t[h].astype(BF16) for h in heads]
            dv = [_dot(kdec[rows, kcs[h]], g_b[h], _NT) for h in heads]
            dkdec = [_dot(v_ref[rows, vcs[h]], g_b[h]) for h in heads]
            for h in heads:
                s_prev = st_ref[c - 1, h] if c > 0 else sp_ref[0, h] * has_prev
                ddec = jnp.sum(g_t[h] * s_prev, axis=0, keepdims=True)
                carry[h] = g_t[h] * dec[:, kcs[h]]
                dp_ref[rows, OFF_Q + h * GLA_DK:OFF_Q + (h + 1) * GLA_DK] = (dq[h] * (GLA_DK ** -0.5)).astype(BF16)
                dp_ref[rows, OFF_V + h * GLA_DV:OFF_V + (h + 1) * GLA_DV] = dv[h].astype(BF16)
                dp_ref[rows, OFF_K + h * GLA_DK:OFF_K + (h + 1) * GLA_DK] = (dkdec[h] * e[rows, kcs[h]]).astype(BF16)
                dd_scr[rows, kcs[h]] = dkdec[h] * kdec_f[rows, kcs[h]]
                dt_scr[rows, kcs[h]] = jnp.broadcast_to(ddec * dec[:, kcs[h]], (CHUNK, GLA_DK))
        for h in heads:
            g_scr[h] = carry[h]
            dgn_ref[:, vcs[h]] += dgn_acc[h]
        dla = _dot_exact_lhs(earlier_ref[...], dd_scr[...]) + dt_scr[...]
        dlogit = dla * (1.0 / GLA_TAU) * _sigmoid(-logit)
        dbg_ref[...] += jnp.sum(dlogit, axis=0, keepdims=True)
        dwgu_ref[...] += _dot_bf16(al_ref[...], dlogit, _TN)
        dal_ref[...] = _dot_bf16(dlogit, wgu_ref[...], _NT).astype(BF16)

    rev = lambda i: nt - 1 - i
    blk = lambda w, j: pl.BlockSpec((tT, w), lambda i: (rev(i), j))
    st_blk = pl.BlockSpec((nc, GLA_HEADS, GLA_DV, GLA_DK), lambda i: (rev(i), 0, 0, 0))
    sp_blk = pl.BlockSpec((1, GLA_HEADS, GLA_DV, GLA_DK), lambda i: (jnp.maximum(rev(i) * nc - 1, 0), 0, 0, 0))
    return _call(
        body, name="gla_bwd", grid=(nt,),
        in_specs=[blk(512, 0), blk(512, 1), blk(1024, 1), blk(1024, 2), blk(LANES, 0)] + [_whole()] * 5
        + [st_blk, sp_blk, blk(GLA_V, 0)],
        out_specs=[blk(W_GLA, 0), blk(LANES, 0), pl.BlockSpec((1, GLA_V), lambda i: (0, 0)),
                   pl.BlockSpec((1, GLA_QK), lambda i: (0, 0)), pl.BlockSpec((LANES, GLA_QK), lambda i: (0, 0))],
        out_shape=[_sds((T, W_GLA), BF16), _sds((T, LANES), BF16), _sds((1, GLA_V), F32), _sds((1, GLA_QK), F32),
                   _sds((LANES, GLA_QK), F32)],
        scratch_shapes=[pltpu.VMEM((GLA_HEADS, GLA_DV, GLA_DK), F32), pltpu.VMEM((tT, GLA_QK), F32),
                        pltpu.VMEM((tT, GLA_QK), F32)],
        args=(proj, proj, proj, proj, alow, wgu, b_gate, gn, _chunk_masks(tT, upper=True), _chunk_masks(tT, upper=False),
              states, states, dy_gla), job=job)


def _inproj_bwd(x, dx1, g1, w_all, dparts, job=None):
    T = x.shape[0]
    tT = _row_tile(T, 512)
    offs = (0, W_GLA, W_GLA + W_SGU, N_MAIN)

    def body(x_ref, dx1_ref, g_ref, w_ref, *rest):
        part_refs, (dx_ref, dg_ref) = rest[:len(offs)], rest[len(offs):]

        @pl.when(pl.program_id(0) == 0)
        def _():
            dg_ref[...] = jnp.zeros_like(dg_ref)

        da = jnp.zeros((tT, D_MODEL), F32)
        for off, p_ref in zip(offs, part_refs):
            da = da + _dot(p_ref[...], w_ref[:, off:off + p_ref.shape[1]], _NT)
        xv = x_ref[...]
        dx, dg = _rms_bwd(da, xv, _rms_stats(xv), g_ref[...])
        dg_ref[...] += jnp.sum(dg, axis=0, keepdims=True)
        dx_ref[...] = dx1_ref[...] + dx

    row = lambda w: pl.BlockSpec((tT, w), lambda i: (i, 0))
    vec = pl.BlockSpec((1, D_MODEL), lambda i: (0, 0))
    return _call(
        body, name="inproj_bwd", grid=(T // tT,),
        in_specs=[row(D_MODEL), row(D_MODEL), vec, _whole()] + [row(p.shape[1]) for p in dparts],
        out_specs=[row(D_MODEL), vec], out_shape=[_sds((T, D_MODEL), F32), _sds((1, D_MODEL), F32)],
        args=(x, dx1, g1, w_all, *dparts), job=job)


def _tn_matmul(a, b, name, job=None):
    T, M = a.shape
    N = b.shape[1]
    tk = _row_tile(T, 1024)
    tm = M if M <= 1024 else 1408
    tn = N if N <= 3072 else N // 2
    assert M % tm == 0 and N % tn == 0

    def body(a_ref, b_ref, o_ref):
        @pl.when(pl.program_id(2) == 0)
        def _():
            o_ref[...] = _dot(a_ref[...], b_ref[...], _TN)

        @pl.when(pl.program_id(2) > 0)
        def _():
            o_ref[...] += _dot(a_ref[...], b_ref[...], _TN)

    res, jres = _call(
        body, name=name, grid=(M // tm, N // tn, T // tk),
        in_specs=[pl.BlockSpec((tk, tm), lambda i, j, k: (k, i)), pl.BlockSpec((tk, tn), lambda i, j, k: (k, j))],
        out_specs=[pl.BlockSpec((tm, tn), lambda i, j, k: (i, j))], out_shape=[_sds((M, N), F32)], args=(a, b), job=job)
    return res[0], jres


def _pad_rows(a, rows=8):
    return jnp.pad(a, ((0, rows - a.shape[0]), (0, LANES - a.shape[1])))


def _halves_view(dw):
    r = dw.shape[0] // N_CHIPS
    return dw.reshape(N_CHIPS, 2, r // 2, dw.shape[1])


def kernel(x, norm_pre_mix, w_in, w_gate_up, b_gate, gla_norm, sgu_ln_g, sgu_ln_b, w_spatial, b_spatial, w_branch_gla, w_branch_sgu, w_out, norm_post_mix, norm_pre_ffn, w_ffn_in, w_ffn_out, norm_post_ffn, loss_target, m_norm_pre_mix, m_w_in, m_w_gate_up, m_b_gate, m_gla_norm, m_sgu_ln_g, m_sgu_ln_b, m_w_spatial, m_b_spatial, m_w_branch_gla, m_w_branch_sgu, m_w_out, m_norm_post_mix, m_norm_pre_ffn, m_w_ffn_in, m_w_ffn_out, m_norm_post_ffn, v_norm_pre_mix, v_w_in, v_w_gate_up, v_b_gate, v_gla_norm, v_sgu_ln_g, v_sgu_ln_b, v_w_spatial, v_b_spatial, v_w_branch_gla, v_w_branch_sgu, v_w_out, v_norm_post_mix, v_norm_pre_ffn, v_w_ffn_in, v_w_ffn_out, v_norm_post_ffn):
    chip = 2 * lax.axis_index("x") + lax.axis_index("y")
    xt, tgt = x[0], loss_target[0]

    tiny = jnp.concatenate([w_gate_up[0], _pad_rows(gla_norm[0]), _pad_rows(sgu_ln_g[0]), _pad_rows(sgu_ln_b[0]),
                            jnp.zeros((24, LANES), F32)], axis=0)

    def with_own(gathered, own):
        return lax.dynamic_update_slice(gathered, own[None], (chip, 0, 0))

    w_in_t, m_in_t, v_in_t = w_in[0].T, m_w_in[0].T, v_w_in[0].T
    w_in_b = _transposed_cast(w_in_t)
    g_in, g_tiny = _run_job(_job_gather([w_in_b, tiny]), "gather_w_in")
    g_tiny = with_own(g_tiny, tiny)
    w_all = _relayout_w_in(with_own(g_in, w_in_b))
    cols = lambda a: a.transpose(1, 0, 2).reshape(a.shape[1], N_CHIPS * a.shape[2])
    wgu = jnp.pad(cols(g_tiny[:, 0:16]), ((0, LANES - GLA_RANK), (0, 0)))
    gn = cols(g_tiny[:, 16:20, :64]).reshape(1, GLA_V)
    ln_g = cols(g_tiny[:, 24:28, :64]).reshape(1, 1024)
    ln_b = cols(g_tiny[:, 32:36, :64]).reshape(1, 1024)
    b_sp_t = jnp.pad(b_spatial[0].T, ((0, 0), (0, LANES - SGU_GROUPS)))
    w_sp = w_spatial[0]

    own_rows = [w_branch_gla[0].astype(BF16), w_branch_sgu[0].astype(BF16), w_out[0].astype(BF16), w_ffn_out[0].astype(BF16)]
    (a, proj, alow), g_rows = _inproj_fwd(xt, norm_pre_mix, w_all, job=_job_gather(own_rows))
    rows = lambda g: g.reshape(N_CHIPS * g.shape[1], g.shape[2])
    w_bg, w_bs, w_o, w_fo = [rows(with_own(g, own)) for g, own in zip(g_rows, own_rows)]
    w_fi_b = w_ffn_in[0].astype(BF16)
    fi_top, fi_bot = w_fi_b[:D_MODEL // 2], w_fi_b[D_MODEL // 2:]
    (y_gla, states), (g_top,) = _gla_fwd(proj, alow, wgu, b_gate, gn, job=_job_gather([fi_top]))
    (y_sgu, zg, zs, merged, mix, x1), (g_bot,) = _sgu_merge_fwd(
        xt, proj, y_gla, ln_g, ln_b, w_sp, b_sp_t, w_bg, w_bs, w_o, norm_post_mix, job=_job_gather([fi_bot]))
    h, f, dgu, dy, dx1, loss, d_gpf, d_gpo = _ffn_fwd_bwd(x1, tgt, with_own(g_top, fi_top), with_own(g_bot, fi_bot),
                                                          w_fo, norm_pre_ffn, norm_post_ffn)

    own_part = lambda c: lax.dynamic_index_in_dim(c, chip, 0, keepdims=False)
    whole = lambda hs: [[(h_, None)] for h_ in hs]
    dw_fo, _ = _tn_matmul(f, dy, "dw_ffn_out")
    dw_fo4 = _halves_view(dw_fo)
    dw_fi, (q_fo,) = _tn_matmul(h, dgu, "dw_ffn_in", job=_job_to_other_core([[(dw_fo4, 0)]]))
    c_fo = _presum(dw_fo4, q_fo, "presum_ffn_out")
    (dmix, dzg, dzs, dp_mrg, dyg, dp_sgu, d_gpm, d_wsp, d_bsp_t, d_lng, d_lnb), (s_fo, q_fi) = _merge_sgu_bwd(
        dx1, mix, proj, zg, zs, w_bg, w_bs, w_o, norm_post_mix, ln_g, ln_b, w_sp, b_sp_t,
        job=_join(_job_scatter([c_fo]), _job_to_other_core([[(dw_fi, 0)]])))
    c_fi = _presum(dw_fi, q_fi, "presum_ffn_in")
    dw_c, _ = _tn_matmul(a, dp_mrg, "dw_in_merge")
    dw_b, _ = _tn_matmul(a, dp_sgu, "dw_in_sgu")
    dw_o4 = _halves_view(_tn_matmul(merged, dmix, "dw_out")[0])
    dw_bg4 = _halves_view(_tn_matmul(y_gla, dzg, "dw_branch_gla")[0])
    dw_bs4 = _halves_view(_tn_matmul(y_sgu, dzs, "dw_branch_sgu")[0])
    h_fo = _sum_slots(own_part(c_fo), s_fo, "sum_ffn_out")
    (dp_gla, dal, d_gn, d_bg, d_wgu), (s_fi, t_fo, q_b, q_c, q_o, q_bg, q_bs) = _gla_bwd(
        proj, alow, wgu, b_gate, gn, states, dyg,
        job=_join(_job_scatter([c_fi]), _job_to_other_core(
            whole([h_fo]) + [[(dw_b, 0)], [(dw_c, 0)], [(dw_o4, 0)], [(dw_bg4, 0)], [(dw_bs4, 0)]])))
    c_o, c_bg, c_bs = (_presum(dw_o4, q_o, "presum_out"), _presum(dw_bg4, q_bg, "presum_branch_gla"),
                       _presum(dw_bs4, q_bs, "presum_branch_sgu"))
    h_fi = _sum_slots(own_part(c_fi), s_fi, "sum_ffn_in")
    dw_d, _ = _tn_matmul(a, dal, "dw_in_gate")
    dw_a, (s_o, s_bg, s_bs, t_fi, q_d) = _tn_matmul(
        a, dp_gla, "dw_in_gla",
        job=_join(_job_scatter([c_o, c_bg, c_bs]), _job_to_other_core(whole([h_fi]) + [[(dw_d, 0)]])))
    h_o, h_bg, h_bs = (_sum_slots(own_part(c_o), s_o, "sum_out"), _sum_slots(own_part(c_bg), s_bg, "sum_branch_gla"),
                       _sum_slots(own_part(c_bs), s_bs, "sum_branch_sgu"))

    grads, deltas, new_m, new_v = {}, {}, {}, {}

    def update(name, w, m, v, g_mine, g_theirs, job=None):
        (g, d, m2, v2), jres = _adamw(w[0], m[0], v[0], g_mine, g_theirs, "adamw_" + name, job=job)
        grads[name], deltas[name], new_m[name], new_v[name] = g[None], d[None], m2[None], v2[None]
        return jres

    dw_in = [(dw_a, 0), (dw_b, W_GLA), (dw_c, W_GLA + W_SGU), (dw_d, N_MAIN)]
    q_a, t_o, t_bg, t_bs = update("w_ffn_out", w_ffn_out, m_w_ffn_out, v_w_ffn_out, [h_fo], [t_fo],
                                  job=_job_to_other_core([[(dw_a, 0)]] + whole([h_o, h_bg, h_bs])))
    q_in = [q_a, q_b, q_c, q_d]
    hr_in = D_MODEL // 2
    c_in_a, _ = _presum_w_in(dw_in, q_in, 0, hr_in // 8, "presum_w_in_a")
    c_in_b, (s_in_a,) = _presum_w_in(dw_in, q_in, hr_in // 8, 7 * hr_in // 8, "presum_w_in_b",
                                     job=_job_scatter([c_in_a]))
    update("w_ffn_in", w_ffn_in, m_w_ffn_in, v_w_ffn_in, [h_fi], [t_fi])
    update("w_out", w_out, m_w_out, v_w_out, [h_o], [t_o])
    update("w_branch_gla", w_branch_gla, m_w_branch_gla, v_w_branch_gla, [h_bg], [t_bg])
    update("w_branch_sgu", w_branch_sgu, m_w_branch_sgu, v_w_branch_sgu, [h_bs], [t_bs])
    (grad_x, d_g1), (s_in_b,) = _inproj_bwd(xt, dx1, norm_pre_mix, w_all, (dp_gla, dp_sgu, dp_mrg, dal),
                                            job=_job_scatter([c_in_b]))
    h_in = [_sum_slots(own_part(c_in_a), s_in_a, "sum_w_in_a"), _sum_slots(own_part(c_in_b), s_in_b, "sum_w_in_b")]
    t_in = _run_job(_job_to_other_core(whole(h_in)), "swap_w_in")
    for store, val in zip((grads, deltas, new_m, new_v),
                          _adamw_transposed(w_in_t, m_in_t, v_in_t, h_in, t_in, "adamw_w_in")):
        store["w_in"] = val.T[None]

    small_names = ["w_spatial", "w_gate_up", "norm_pre_mix", "norm_post_mix", "norm_pre_ffn", "norm_post_ffn", "b_gate",
                   "b_spatial", "gla_norm", "sgu_ln_g", "sgu_ln_b"]
    loss_out, small = _small_adamw(
        _small_sum([d_wsp, d_wgu, d_g1, d_gpm, d_gpf, d_gpo, d_bg, d_bsp_t, d_gn, d_lng, d_lnb, loss]),
        [w_spatial, w_gate_up, norm_pre_mix, norm_post_mix, norm_pre_ffn, norm_post_ffn, b_gate, b_spatial, gla_norm,
         sgu_ln_g, sgu_ln_b],
        [m_w_spatial, m_w_gate_up, m_norm_pre_mix, m_norm_post_mix, m_norm_pre_ffn, m_norm_post_ffn, m_b_gate,
         m_b_spatial, m_gla_norm, m_sgu_ln_g, m_sgu_ln_b],
        [v_w_spatial, v_w_gate_up, v_norm_pre_mix, v_norm_post_mix, v_norm_pre_ffn, v_norm_post_ffn, v_b_gate,
         v_b_spatial, v_gla_norm, v_sgu_ln_g, v_sgu_ln_b])
    for store, vals in zip((grads, deltas, new_m, new_v), small):
        store.update(zip(small_names, vals))

    order = ["norm_pre_mix", "w_in", "w_gate_up", "b_gate", "gla_norm", "sgu_ln_g", "sgu_ln_b", "w_spatial", "b_spatial",
             "w_branch_gla", "w_branch_sgu", "w_out", "norm_post_mix", "norm_pre_ffn", "w_ffn_in", "w_ffn_out",
             "norm_post_ffn"]
    out = [loss_out, grad_x[None]]
    for store in (grads, deltas, new_m, new_v):
        out.extend(store[n] for n in order)
    return tuple(out)
```

```python
import jax
import jax.numpy as jnp
from jax import lax
from jax.experimental import pallas as pl
from jax.experimental.pallas import tpu as pltpu

F32 = jnp.float32
BF16 = jnp.bfloat16

D_MODEL = 1024
GLA_HEADS = 4
GLA_DK = 128
GLA_DV = 256
GLA_QK = GLA_HEADS * GLA_DK
GLA_V = GLA_HEADS * GLA_DV
GLA_RANK = 16
GLA_TAU = 16.0
CHUNK = 64
SGU_GROUPS = 4
SGU_BLOCK = 128
SGU_DG = 256
D_FF = 2816
EPS = 1e-6
LANES = 128

OFF_Q, OFF_K, OFF_V, OFF_R, OFF_SU, OFF_SV, OFF_GG, OFF_GS, OFF_AL = 0, 512, 1024, 2048, 3072, 4096, 5120, 6144, 7168
W_GLA, W_SGU, W_MRG = 3072, 2048, 2048
N_MAIN = 7168
N_ALL = N_MAIN + LANES
_IN_SPLITS = (GLA_QK, GLA_QK, GLA_V, GLA_V, GLA_RANK, 1024, 1024, 1024, 1024)
_IN_STARTS = tuple(sum(_IN_SPLITS[:i]) for i in range(len(_IN_SPLITS) + 1))
_IN_DST = (OFF_Q, OFF_K, OFF_V, OFF_R, OFF_AL, OFF_SU, OFF_SV, OFF_GG, OFF_GS)
D_IN = _IN_STARTS[-1]

ADAM_LR = 0.001
ADAM_B1 = 0.9
ADAM_B2 = 0.999
ADAM_EPS = 1e-08
ADAM_WD = 0.01
ADAM_STEP = 10

VMEM_LIMIT_BYTES = 56 * 1024 * 1024
N_CHIPS = 4
N_PEER = N_CHIPS - 1
N_DEV = 8
MESH = pl.DeviceIdType.MESH

_NN = (((1,), (0,)), ((), ()))
_NT = (((1,), (1,)), ((), ()))
_TN = (((0,), (0,)), ((), ()))


def _dot(a, b, dims=_NN):
    return lax.dot_general(a, b, dims, preferred_element_type=F32)


def _split(x):
    hi = x.astype(BF16)
    lo = (x - hi.astype(F32)).astype(BF16)
    return hi, lo


def _dot_bf16(a, b, dims=_NN):
    return _dot(a.astype(BF16), b.astype(BF16), dims)


def _dot_exact_lhs(m, x):
    xh, xl = _split(x)
    return _dot(m, xh) + _dot(m, xl)


def _sigmoid(x):
    return 0.5 * jnp.tanh(0.5 * x) + 0.5


def _log_sigmoid(x):
    return jnp.minimum(x, 0.0) - jnp.log(1.0 + jnp.exp(-jnp.abs(x)))


_GELU_C = 0.7978845608028654
_GELU_A = 0.044715


def _gelu_and_grad(x):
    x2 = x * x
    t = jnp.tanh(_GELU_C * (x + _GELU_A * x * x2))
    g = 0.5 * x * (1.0 + t)
    dg = 0.5 * (1.0 + t) + 0.5 * x * (1.0 - t * t) * (_GELU_C * (1.0 + 3.0 * _GELU_A * x2))
    return g, dg


def _gelu(x):
    t = jnp.tanh(_GELU_C * (x + _GELU_A * x * x * x))
    return 0.5 * x * (1.0 + t)


def _rms_stats(x):
    return lax.rsqrt(jnp.mean(x * x, axis=-1, keepdims=True) + EPS)


def _rms_bwd(dout, y, r, g):
    yhat = y * r
    dn = dout * g
    dy = r * (dn - yhat * jnp.mean(dn * yhat, axis=-1, keepdims=True))
    return dy, dout * yhat


def _whole():
    return pl.BlockSpec(memory_space=pltpu.VMEM)


def _row_tile(T, want):
    t = min(T, want)
    assert T % t == 0
    return t


def _chunk_masks(tT, upper):
    row = lax.broadcasted_iota(jnp.int32, (tT, tT), 0)
    col = lax.broadcasted_iota(jnp.int32, (tT, tT), 1)
    same = (row // CHUNK) == (col // CHUNK)
    tri = (col > row) if upper else (col < row)
    return jnp.where(same & tri, 1.0, 0.0).astype(BF16)


class _Job:
    def __init__(self, ins, out_shapes, scratch, start, finish, mid=None):
        self.ins, self.out_shapes, self.scratch = list(ins), list(out_shapes), list(scratch)
        self.start, self.finish, self.mid = start, finish, mid


def _join(*jobs):
    def split(refs, counts):
        out, at = [], 0
        for n in counts:
            out.append(refs[at:at + n])
            at += n
        return out

    ni, no, ns = [len(j.ins) for j in jobs], [len(j.out_shapes) for j in jobs], [len(j.scratch) for j in jobs]

    def start(ins, outs, scr):
        for j, a, b, c in zip(jobs, split(ins, ni), split(outs, no), split(scr, ns)):
            j.start(a, b, c)

    def finish(ins, outs, scr):
        for j, a, b, c in zip(jobs, split(ins, ni), split(outs, no), split(scr, ns)):
            j.finish(a, b, c)

    def mid(ins, outs, scr):
        for j, a, b, c in zip(jobs, split(ins, ni), split(outs, no), split(scr, ns)):
            if j.mid is not None:
                j.mid(a, b, c)

    return _Job(sum((j.ins for j in jobs), []), sum((j.out_shapes for j in jobs), []),
                sum((j.scratch for j in jobs), []), start, finish, mid if any(j.mid for j in jobs) else None)


def _mesh_pos():
    return lax.axis_index("x"), lax.axis_index("y"), lax.axis_index("c")


def _peer_chips(xi, yi):
    return [(1 - xi, yi), (xi, 1 - yi), (1 - xi, 1 - yi)]


def _half(ci, rows):
    return pl.ds(pl.multiple_of(ci * rows, 8), rows)


def _sds(shape, dtype):
    return jax.ShapeDtypeStruct(tuple(shape), dtype)


def _job_gather(arrs):
    n = len(arrs)
    kinds = 12
    Y0, Y1, X1, X0, ON_X, ON_Y, D2D = 0, 1, 2, 3, 4, 5, 6

    def copies(ins, outs, scr):
        send_sems, recv_sems = scr
        xi, yi, ci = _mesh_pos()
        me, cx, cy, cd = 2 * xi + yi, 2 * (1 - xi) + yi, 2 * xi + (1 - yi), 2 * (1 - xi) + (1 - yi)
        to_x, to_y, to_core = (1 - xi, yi, ci), (xi, 1 - yi, ci), (xi, yi, 1 - ci)
        table = []
        for k in range(n):
            qr = arrs[k].shape[0] // 4

            def rows(core, q):
                return pl.ds(pl.multiple_of((2 * core + q) * qr, 8), qr)

            def cp(kind, src, dst, to):
                s = k * kinds + kind
                return pltpu.make_async_remote_copy(src_ref=src, dst_ref=dst, send_sem=send_sems.at[s],
                                                    recv_sem=recv_sems.at[s], device_id=to, device_id_type=MESH)

            def slab(chip, core, q):
                return outs[k].at[chip, rows(core, q)]

            t = {}
            for kind, q, to, frm in ((Y0, 0, to_y, cy), (Y1, 1, to_y, cy), (X1, 1, to_x, cx), (X0, 0, to_x, cx)):
                mine = ins[k].at[rows(ci, q)]
                t[kind] = (cp(kind, mine, slab(me, ci, q), to), cp(kind, mine, slab(frm, ci, q), to))
            t[ON_X] = (cp(ON_X, slab(cy, ci, 0), slab(cy, ci, 0), to_x), cp(ON_X, slab(cy, ci, 0), slab(cd, ci, 0), to_x))
            t[ON_Y] = (cp(ON_Y, slab(cx, ci, 1), slab(cx, ci, 1), to_y), cp(ON_Y, slab(cx, ci, 1), slab(cd, ci, 1), to_y))
            for i, (chip, q) in enumerate(((cy, 0), (cy, 1), (cx, 1), (cx, 0), (cd, 0), (cd, 1))):
                t[D2D + i] = (cp(D2D + i, slab(chip, ci, q), slab(chip, ci, q), to_core),
                              cp(D2D + i, slab(chip, ci, q), slab(chip, 1 - ci, q), to_core))
            table.append(t)
        return table

    def start(ins, outs, scr):
        table = copies(ins, outs, scr)
        for kind in (Y0, X1, Y1, X0):
            for t in table:
                t[kind][0].start()

    def arrived(table, kind, then):
        for t in table:
            t[kind][1].wait_recv()
            for nxt in then:
                t[nxt][0].start()

    def mid(ins, outs, scr):
        table = copies(ins, outs, scr)
        arrived(table, Y0, (ON_X, D2D + 0))
        arrived(table, X1, (ON_Y, D2D + 2))

    def finish(ins, outs, scr):
        table = copies(ins, outs, scr)
        arrived(table, Y1, (D2D + 1,))
        arrived(table, X0, (D2D + 3,))
        arrived(table, ON_X, (D2D + 4,))
        arrived(table, ON_Y, (D2D + 5,))
        for t in table:
            for i in range(6):
                t[D2D + i][1].wait_recv()
            for kind in range(kinds):
                t[kind][0].wait_send()

    dma = pltpu.SemaphoreType.DMA
    return _Job(arrs, [_sds((N_CHIPS,) + a.shape, a.dtype) for a in arrs], [dma((n * kinds,))] * 2, start, finish, mid)


def _job_scatter(parts):
    n = len(parts)

    def copies(ins, outs, scr):
        send_sems, recv_sems = scr
        xi, yi, ci = _mesh_pos()
        res = []
        for k in range(n):
            for j, (px, py) in enumerate(_peer_chips(xi, yi)):
                s = k * N_PEER + j
                res.append(pltpu.make_async_remote_copy(
                    src_ref=ins[k].at[2 * px + py], dst_ref=outs[k].at[j], send_sem=send_sems.at[s],
                    recv_sem=recv_sems.at[s], device_id=(px, py, ci), device_id_type=MESH))
        return res

    def start(ins, outs, scr):
        for cp in copies(ins, outs, scr):
            cp.start()

    def finish(ins, outs, scr):
        for cp in copies(ins, outs, scr):
            cp.wait_recv()
            cp.wait_send()

    dma = pltpu.SemaphoreType.DMA
    return _Job(parts, [_sds((N_PEER,) + p.shape[1:], p.dtype) for p in parts], [dma((n * N_PEER,))] * 2, start, finish)


def _job_to_other_core(groups):
    pieces = [(g, a, off) for g, group in enumerate(groups) for a, off in group]
    n = len(pieces)

    def geometry(group):
        a0, off0 = group[0]
        if off0 is None:
            return a0.shape
        if a0.ndim == 4:
            return (N_CHIPS, a0.shape[2], a0.shape[3])
        return (a0.shape[0] // 2, sum(a.shape[1] for a, _ in group))

    def copies(ins, outs, scr):
        send_sems, recv_sems = scr
        xi, yi, ci = _mesh_pos()
        res = []
        for p, (g, a, off) in enumerate(pieces):
            if off is None:
                give, land = ins[p], outs[g]
            elif a.ndim == 4:
                give, land = ins[p].at[pl.ds(0, N_CHIPS), 1 - ci], outs[g]
            else:
                hr, w = a.shape[0] // 2, a.shape[1]
                give, land = ins[p].at[_half(1 - ci, hr)], outs[g].at[pl.ds(0, hr), pl.ds(off, w)]
            res.append(pltpu.make_async_remote_copy(
                src_ref=give, dst_ref=land, send_sem=send_sems.at[p], recv_sem=recv_sems.at[p],
                device_id=(xi, yi, 1 - ci), device_id_type=MESH))
        return res

    def start(ins, outs, scr):
        for cp in copies(ins, outs, scr):
            cp.start()

    def finish(ins, outs, scr):
        for cp in copies(ins, outs, scr):
            cp.wait_recv()
            cp.wait_send()

    dma = pltpu.SemaphoreType.DMA
    return _Job([a for _, a, _ in pieces], [_sds(geometry(group), group[0][0].dtype) for group in groups],
                [dma((n,))] * 2, start, finish)


def _call(body, *, name, grid, in_specs, out_specs, out_shape, args, scratch_shapes=(), parallel=False, job=None,
          by_core=False):
    n_in, n_out, n_scr = len(in_specs), len(out_specs), len(scratch_shapes)
    hbm = pl.BlockSpec(memory_space=pl.ANY)
    n_ji, n_jo = (len(job.ins), len(job.out_shapes)) if job is not None else (0, 0)
    lead = 1 if by_core else 0

    def kernel_fn(*refs):
        core, refs = refs[:lead], refs[lead:]
        ins, refs = refs[:n_in], refs[n_in:]
        j_ins, refs = refs[:n_ji], refs[n_ji:]
        outs, refs = refs[:n_out], refs[n_out:]
        j_outs, refs = refs[:n_jo], refs[n_jo:]
        scr, j_scr = refs[:n_scr], refs[n_scr:]
        if job is None:
            body(*core, *ins, *outs, *scr)
            return
        ids = [pl.program_id(d) for d in range(len(grid))]
        first = ids[0] == 0
        last = ids[0] == grid[0] - 1
        for d in range(1, len(grid)):
            first = first & (ids[d] == 0)
            last = last & (ids[d] == grid[d] - 1)

        @pl.when(first)
        def _():
            job.start(j_ins, j_outs, j_scr)

        if job.mid is not None and grid[0] >= 4:
            half_way = ids[0] == grid[0] // 2
            for d in range(1, len(grid)):
                half_way = half_way & (ids[d] == 0)

            @pl.when(half_way)
            def _():
                job.mid(j_ins, j_outs, j_scr)

        body(*core, *ins, *outs, *scr)

        @pl.when(last)
        def _():
            if job.mid is not None and grid[0] < 4:
                job.mid(j_ins, j_outs, j_scr)
            job.finish(j_ins, j_outs, j_scr)

    sem = ("parallel" if parallel and job is None else "arbitrary",) * len(grid)
    all_in = list(in_specs) + [hbm] * n_ji
    all_out = list(out_specs) + [hbm] * n_jo
    all_scratch = list(scratch_shapes) + (job.scratch if job is not None else [])
    all_shapes = list(out_shape) + (job.out_shapes if job is not None else [])
    all_args = list(args) + (job.ins if job is not None else [])
    params = pltpu.CompilerParams(dimension_semantics=sem, vmem_limit_bytes=VMEM_LIMIT_BYTES)
    if by_core:
        spec = pltpu.PrefetchScalarGridSpec(num_scalar_prefetch=1, grid=grid, in_specs=all_in, out_specs=all_out,
                                            scratch_shapes=all_scratch)
        core = lax.axis_index("c").astype(jnp.int32).reshape(1)
        res = pl.pallas_call(kernel_fn, name=name, grid_spec=spec, out_shape=all_shapes, compiler_params=params)(
            core, *all_args)
    else:
        res = pl.pallas_call(kernel_fn, name=name, grid=grid, in_specs=all_in, out_specs=all_out, out_shape=all_shapes,
                             scratch_shapes=all_scratch, compiler_params=params)(*all_args)
    return list(res[:n_out]), list(res[n_out:])


def _run_job(job, name):
    n_i, n_o = len(job.ins), len(job.out_shapes)

    def body(*refs):
        ins, outs, scr = refs[:n_i], refs[n_i:n_i + n_o], refs[n_i + n_o:]
        job.start(ins, outs, scr)
        if job.mid is not None:
            job.mid(ins, outs, scr)
        job.finish(ins, outs, scr)

    hbm = pl.BlockSpec(memory_space=pl.ANY)
    return list(pl.pallas_call(body, name=name, in_specs=[hbm] * n_i, out_specs=[hbm] * n_o, out_shape=job.out_shapes,
                               scratch_shapes=job.scratch)(*job.ins))


def _adam_values(w, m, v, g):
    m2 = ADAM_B1 * m + (1.0 - ADAM_B1) * g
    v2 = ADAM_B2 * v + (1.0 - ADAM_B2) * (g * g)
    delta = -ADAM_LR * ((m2 / (1.0 - ADAM_B1 ** ADAM_STEP)) / (jnp.sqrt(v2 / (1.0 - ADAM_B2 ** ADAM_STEP)) + ADAM_EPS)
                        + ADAM_WD * w)
    return delta, m2, v2


_P_WSP, _P_WGU, _P_NORM, _P_BG, _P_BSP, _P_HEAD, _P_LOSS, _P_ROWS = 0, 512, 576, 608, 616, 624, 720, 728


def _small_sum(dgrads):
    def body(dwsp, dwgu, dg1, dgpm, dgpf, dgpo, dbg, dbspt, dgn, dlng, dlnb, loss_in, tot, pack, slots, send_sems,
             recv_sems):
        xi, yi, ci = _mesh_pos()
        chip = 2 * xi + yi

        pack[...] = jnp.zeros_like(pack)
        for g in range(SGU_GROUPS):
            pack[_P_WSP + g * SGU_BLOCK:_P_WSP + (g + 1) * SGU_BLOCK] = dwsp[g]
        for j in range(N_CHIPS):
            pack[_P_WGU + GLA_RANK * j:_P_WGU + GLA_RANK * (j + 1)] = dwgu[0:GLA_RANK, LANES * j:LANES * (j + 1)]
        for k, r in enumerate((dg1, dgpm, dgpf, dgpo)):
            for q in range(8):
                pack[_P_NORM + 8 * k + q:_P_NORM + 8 * k + q + 1] = r[:, LANES * q:LANES * (q + 1)]
        for q in range(4):
            pack[_P_BG + q:_P_BG + q + 1] = dbg[:, LANES * q:LANES * (q + 1)]
        pack[_P_BSP:_P_BSP + SGU_GROUPS] = jnp.transpose(dbspt[...])[0:SGU_GROUPS]
        for k, r in enumerate((dgn, dlng, dlnb)):
            for j in range(N_CHIPS):
                for hh in range(4):
                    row = _P_HEAD + 32 * k + 8 * j + hh
                    pack[row:row + 1, 0:64] = r[:, 256 * hh + 64 * j:256 * hh + 64 * (j + 1)]
        pack[_P_LOSS:_P_LOSS + 1] = loss_in[...]

        to_sibling = pltpu.make_async_remote_copy(
            src_ref=pack, dst_ref=tot, send_sem=send_sems.at[N_PEER], recv_sem=recv_sems.at[N_PEER],
            device_id=(xi, yi, 1 - ci), device_id_type=MESH)
        to_sibling.start()
        to_sibling.wait_recv()
        to_sibling.wait_send()
        pack[...] = pack[...] + tot[...]
        slots[chip] = pack[...]

        def copy(j, slot):
            px, py = _peer_chips(xi, yi)[j]
            return pltpu.make_async_remote_copy(
                src_ref=pack, dst_ref=slots.at[slot(2 * px + py)], send_sem=send_sems.at[j], recv_sem=recv_sems.at[j],
                device_id=(px, py, ci), device_id_type=MESH)

        sends = [copy(j, lambda peer_chip: chip) for j in range(N_PEER)]
        for cp in sends:
            cp.start()
        for j in range(N_PEER):
            copy(j, lambda peer_chip: peer_chip).wait_recv()
        for cp in sends:
            cp.wait_send()
        acc = slots[0]
        for d in range(1, N_CHIPS):
            acc = acc + slots[d]
        tot[...] = acc

    return pl.pallas_call(
        body, name="small_sum", in_specs=[_whole()] * 12, out_specs=_whole(), out_shape=_sds((_P_ROWS, LANES), F32),
        scratch_shapes=[pltpu.VMEM((_P_ROWS, LANES), F32), pltpu.VMEM((N_CHIPS, _P_ROWS, LANES), F32),
                        pltpu.SemaphoreType.DMA((N_PEER + 1,)), pltpu.SemaphoreType.DMA((N_PEER + 1,))],
        compiler_params=pltpu.CompilerParams(vmem_limit_bytes=VMEM_LIMIT_BYTES),
    )(*dgrads)


def _small_adamw(tot, ws, ms, vs):
    n = len(ws)

    def body(*refs):
        tot = refs[0]
        w_refs, m_refs, v_refs = refs[1:1 + n], refs[1 + n:1 + 2 * n], refs[1 + 2 * n:1 + 3 * n]
        loss_out = refs[1 + 3 * n]
        outs = refs[2 + 3 * n:]
        chip = 2 * lax.axis_index("x") + lax.axis_index("y")
        loss_out[...] = tot[_P_LOSS:_P_LOSS + 1, 0:1]

        def step(k, g, pick, put):
            d, m2, v2 = _adam_values(pick(w_refs[k]), pick(m_refs[k]), pick(v_refs[k]), g)
            for o, val in zip((outs[k], outs[n + k], outs[2 * n + k], outs[3 * n + k]), (g, d, m2, v2)):
                put(o, val)

        def whole(ref):
            return ref[0]

        def put_whole(ref, val):
            ref[0] = val

        for g in range(SGU_GROUPS):
            def pick_g(ref, g=g):
                return ref[0, g]

            def put_g(ref, val, g=g):
                ref[0, g] = val

            step(0, tot[_P_WSP + g * SGU_BLOCK:_P_WSP + (g + 1) * SGU_BLOCK], pick_g, put_g)
        step(1, tot[pl.ds(pl.multiple_of(_P_WGU + GLA_RANK * chip, GLA_RANK), GLA_RANK), :], whole, put_whole)
        for k, (base, chunks) in enumerate(((_P_NORM, 8), (_P_NORM + 8, 8), (_P_NORM + 16, 8), (_P_NORM + 24, 8), (_P_BG, 4))):
            for q in range(chunks):
                def pick_q(ref, q=q):
                    return ref[:, LANES * q:LANES * (q + 1)]

                def put_q(ref, val, q=q):
                    ref[:, LANES * q:LANES * (q + 1)] = val

                step(2 + k, tot[base + q:base + q + 1], pick_q, put_q)
        step(7, tot[_P_BSP:_P_BSP + SGU_GROUPS], whole, put_whole)
        for k in range(3):
            mine = tot[pl.ds(pl.multiple_of(_P_HEAD + 32 * k + 8 * chip, 8), 8), :]
            step(8 + k, mine[0:4, 0:64], whole, put_whole)

    shapes = [_sds(w.shape, F32) for w in ws]
    res = pl.pallas_call(
        body, name="small_adamw", in_specs=[_whole()] * (1 + 3 * n), out_specs=[_whole()] * (1 + 4 * n),
        out_shape=[_sds((1, 1), F32)] + shapes * 4,
        compiler_params=pltpu.CompilerParams(vmem_limit_bytes=VMEM_LIMIT_BYTES),
    )(tot, *ws, *ms, *vs)
    return res[0].reshape(()), [list(res[1 + i * n:1 + (i + 1) * n]) for i in range(4)]


def _w_in_pieces():
    blk = D_IN // N_CHIPS
    pieces = []
    for s in range(len(_IN_SPLITS)):
        lo_s, hi_s = _IN_STARTS[s], _IN_STARTS[s + 1]
        for j in range(N_CHIPS):
            lo, hi = max(lo_s, j * blk), min(hi_s, (j + 1) * blk)
            if lo < hi:
                pieces.append((j, lo - j * blk, _IN_DST[s] + lo - lo_s, hi - lo))
    return pieces


def _relayout_w_in(gathered):
    _, rows, blk = gathered.shape
    tr = 256

    def body(g_ref, o_ref):
        o_ref[:, OFF_AL:N_ALL] = jnp.zeros((tr, LANES), BF16)
        for j, src, dst, w in _w_in_pieces():
            o_ref[:, dst:dst + w] = g_ref[j, :, src:src + w]

    res, _ = _call(body, name="relayout_w_in", grid=(rows // tr,), parallel=True,
                   in_specs=[pl.BlockSpec((N_CHIPS, tr, blk), lambda i: (0, i, 0))],
                   out_specs=[pl.BlockSpec((tr, N_ALL), lambda i: (i, 0))],
                   out_shape=[_sds((rows, N_ALL), BF16)], args=(gathered,))
    return res[0]


def _update_row_tile(rows):
    for t in range(min(rows, 256), 7, -8):
        if rows % t == 0:
            return t
    return rows


def _presum_w_in(dws, theirs, row0, rows, name, job=None):
    hr = theirs[0].shape[0]
    blk = D_IN // N_CHIPS
    tr = 64
    assert row0 % tr == 0 and rows % tr == 0
    nh, t0 = hr // tr, row0 // tr
    n = len(dws)

    def body(core_ref, *refs):
        dw_refs, q_refs, (o_ref, s_scr) = refs[:n], refs[n:2 * n], refs[2 * n:]
        for p, (a, off) in enumerate(dws):
            w = a.shape[1]
            s_scr[:, off:off + w] = (dw_refs[p][...] + q_refs[p][...]).astype(BF16)
        for j, src, dst, w in _w_in_pieces():
            o_ref[j, :, src:src + w] = s_scr[:, dst:dst + w]

    in_specs = [pl.BlockSpec((tr, a.shape[1]), lambda i, core: (i + t0 + core[0] * nh, 0)) for a, _ in dws]
    in_specs += [pl.BlockSpec((tr, q.shape[1]), lambda i, core: (i + t0, 0)) for q in theirs]
    res, jres = _call(body, name=name, grid=(rows // tr,), parallel=True, in_specs=in_specs,
                      out_specs=[pl.BlockSpec((N_CHIPS, tr, blk), lambda i, core: (0, i, 0))],
                      out_shape=[_sds((N_CHIPS, rows, blk), BF16)], scratch_shapes=[pltpu.VMEM((tr, N_ALL), BF16)],
                      args=(*[a for a, _ in dws], *theirs), job=job, by_core=True)
    return res[0], jres


def _presum(dw, theirs, name):
    if dw.ndim == 4:
        _, _, hr, c = dw.shape
        tr = _update_row_tile(hr)
        mine = pl.BlockSpec((1, 1, tr, c), lambda j, i, core: (j, core[0], i, 0))
        other = pl.BlockSpec((1, tr, c), lambda j, i, core: (j, i, 0))
    else:
        hr, c = dw.shape[0] // 2, dw.shape[1] // N_CHIPS
        tr = _update_row_tile(hr)
        nh = hr // tr
        mine = pl.BlockSpec((tr, c), lambda j, i, core: (i + core[0] * nh, j))
        other = pl.BlockSpec((tr, c), lambda j, i, core: (i, j))

    def body(core_ref, a_ref, q_ref, o_ref):
        o_ref[...] = (a_ref[...].reshape(tr, c) + q_ref[...].reshape(tr, c)).astype(BF16).reshape(o_ref.shape)

    res, _ = _call(body, name=name, grid=(N_CHIPS, hr // tr), parallel=True, in_specs=[mine, other],
                   out_specs=[pl.BlockSpec((1, tr, c), lambda j, i, core: (j, i, 0))],
                   out_shape=[_sds((N_CHIPS, hr, c), BF16)], args=(dw, theirs), by_core=True)
    return res[0]


def _sum_slots(own, slots, name):
    rows, cols = own.shape
    tr = _update_row_tile(rows)

    def body(own_ref, s_ref, o_ref):
        acc = own_ref[...].astype(F32)
        for j in range(N_PEER):
            acc = acc + s_ref[j].astype(F32)
        o_ref[...] = acc

    res, _ = _call(body, name=name, grid=(rows // tr,), parallel=True,
                   in_specs=[pl.BlockSpec((tr, cols), lambda i: (i, 0)), pl.BlockSpec((N_PEER, tr, cols), lambda i: (0, i, 0))],
                   out_specs=[pl.BlockSpec((tr, cols), lambda i: (i, 0))], out_shape=[_sds((rows, cols), F32)],
                   args=(own, slots))
    return res[0]


def _adamw(w, m, v, g_mine, g_theirs, name, job=None):
    rows, cols = w.shape
    part_rows = [p.shape[0] for p in g_mine]
    assert sum(part_rows) == rows // 2 and [p.shape[0] for p in g_theirs] == part_rows
    tr = _update_row_tile(min(part_rows))
    assert all(r % tr == 0 for r in part_rows)
    nh = (rows // 2) // tr
    starts = [sum(part_rows[:k]) // tr for k in range(len(part_rows))]
    n_parts = len(part_rows)

    def body(core_ref, w_ref, m_ref, v_ref, *rest):
        g_refs, (g_out, d_out, m_out, v_out) = rest[:-4], rest[-4:]
        step = pl.program_id(0)
        mine_here = (step // nh) == core_ref[0]
        q = step % nh
        g = None
        for k in reversed(range(n_parts)):
            val = jnp.where(mine_here, g_refs[k][...], g_refs[n_parts + k][...])
            g = val if g is None else jnp.where(q < starts[k + 1], val, g)
        d, m2, v2 = _adam_values(w_ref[...], m_ref[...], v_ref[...], g)
        g_out[...] = g
        m_out[...] = m2
        v_out[...] = v2
        d_out[...] = d

    def g_spec(k, mine):
        last = part_rows[k] // tr - 1

        def index(i, core):
            half = core[0] if mine else 1 - core[0]
            here = jnp.clip(i % nh - starts[k], 0, last)
            return (jnp.where(i // nh == half, here, jnp.where(i // nh > half, last, 0)), 0)

        return pl.BlockSpec((tr, cols), index)

    spec = pl.BlockSpec((tr, cols), lambda i, core: (i, 0))
    g_specs = [g_spec(k, True) for k in range(n_parts)] + [g_spec(k, False) for k in range(n_parts)]
    return _call(body, name=name, grid=(rows // tr,), parallel=True, in_specs=[spec] * 3 + g_specs,
                 out_specs=[spec] * 4, out_shape=[_sds((rows, cols), F32)] * 4, args=(w, m, v, *g_mine, *g_theirs),
                 job=job, by_core=True)


def _transposed_cast(wt):
    cols, rows = wt.shape

    def body(x_ref, o_ref):
        o_ref[...] = jnp.transpose(x_ref[...]).astype(BF16)

    res, _ = _call(body, name="transpose_w_in", grid=(pl.cdiv(cols, LANES),), parallel=True,
                   in_specs=[pl.BlockSpec((LANES, rows), lambda j: (j, 0))],
                   out_specs=[pl.BlockSpec((rows, LANES), lambda j: (0, j))], out_shape=[_sds((rows, cols), BF16)],
                   args=(wt,))
    return res[0]


def _adamw_transposed(wt, mt, vt, g_mine, g_theirs, name):
    cols, rows = wt.shape
    n_parts = len(g_mine)

    def body(w_ref, m_ref, v_ref, *rest):
        g_refs, (g_out, d_out, m_out, v_out) = rest[:-4], rest[-4:]
        mine = jnp.concatenate([r[...] for r in g_refs[:n_parts]], axis=0)
        theirs = jnp.concatenate([r[...] for r in g_refs[n_parts:]], axis=0)
        first = lax.axis_index("c") == 0
        g = jnp.transpose(jnp.concatenate([jnp.where(first, mine, theirs), jnp.where(first, theirs, mine)], axis=0))
        d, m2, v2 = _adam_values(w_ref[...], m_ref[...], v_ref[...], g)
        g_out[...] = g
        m_out[...] = m2
        v_out[...] = v2
        d_out[...] = d

    spec = pl.BlockSpec((LANES, rows), lambda j: (j, 0))
    g_specs = [pl.BlockSpec((p.shape[0], LANES), lambda j: (0, j)) for p in g_mine] * 2
    res, _ = _call(body, name=name, grid=(pl.cdiv(cols, LANES),), parallel=True, in_specs=[spec] * 3 + g_specs,
                   out_specs=[spec] * 4, out_shape=[_sds((cols, rows), F32)] * 4, args=(wt, mt, vt, *g_mine, *g_theirs))
    return res


def _inproj_fwd(x, g1, w_all, job=None):
    T = x.shape[0]
    tT = _row_tile(T, 512)

    def body(x_ref, g_ref, w_ref, a_ref, proj_ref, alow_ref):
        xv = x_ref[...]
        a = (xv * _rms_stats(xv) * g_ref[...]).astype(BF16)
        a_ref[...] = a
        for j in range(N_MAIN // 1024):
            cols = slice(j * 1024, (j + 1) * 1024)
            proj_ref[:, cols] = _dot(a, w_ref[:, cols]).astype(BF16)
        alow_ref[...] = _dot(a, w_ref[:, N_MAIN:N_ALL])

    row = lambda w: pl.BlockSpec((tT, w), lambda i: (i, 0))
    return _call(
        body, name="inproj_fwd", grid=(T // tT,), parallel=True,
        in_specs=[row(D_MODEL), pl.BlockSpec((1, D_MODEL), lambda i: (0, 0)), _whole()],
        out_specs=[row(D_MODEL), row(N_MAIN), row(LANES)],
        out_shape=[_sds((T, D_MODEL), BF16), _sds((T, N_MAIN), BF16), _sds((T, LANES), F32)],
        args=(x, g1, w_all), job=job)


def _gla_decay_terms(al_ref, wgu_ref, bg_ref, later_ref):
    logit = _dot_bf16(al_ref[...], wgu_ref[...]) + bg_ref[...]
    la = _log_sigmoid(logit) * (1.0 / GLA_TAU)
    delta = _dot_exact_lhs(later_ref[...], la)
    return logit, la, delta


def _gla_fwd(proj, alow, wgu, b_gate, gn, job=None):
    T = proj.shape[0]
    tT = _row_tile(T, 512)
    nc = tT // CHUNK

    def body(q_ref, k_ref, v_ref, r_ref, al_ref, wgu_ref, bg_ref, gn_ref, later_ref, y_ref, st_ref, s_scr):
        @pl.when(pl.program_id(0) == 0)
        def _():
            s_scr[...] = jnp.zeros_like(s_scr)

        _, la, delta = _gla_decay_terms(al_ref, wgu_ref, bg_ref, later_ref)
        kdec = (k_ref[...].astype(F32) * jnp.exp(delta)).astype(BF16)
        heads = range(GLA_HEADS)
        kcs = [slice(h * GLA_DK, (h + 1) * GLA_DK) for h in heads]
        vcs = [slice(h * GLA_DV, (h + 1) * GLA_DV) for h in heads]
        state = [s_scr[h] for h in heads]
        for c in range(nc):
            rows = slice(c * CHUNK, (c + 1) * CHUNK)
            first = slice(c * CHUNK, c * CHUNK + 1)
            dec = jnp.exp(la[first, :] + delta[first, :])
            upd_t = [_dot(v_ref[rows, vcs[h]], kdec[rows, kcs[h]], _TN) for h in heads]
            qs = [(q_ref[rows, kcs[h]].astype(F32) * (GLA_DK ** -0.5)).astype(BF16) for h in heads]
            for h in heads:
                state[h] = state[h] * dec[:, kcs[h]] + upd_t[h]
                st_ref[c, h] = state[h]
            o = [_dot(qs[h], state[h].astype(BF16), _NT) for h in heads]
            for h in heads:
                on = o[h] * _rms_stats(o[h]) * gn_ref[:, vcs[h]]
                rr = r_ref[rows, vcs[h]].astype(F32)
                y_ref[rows, vcs[h]] = (on * (rr * _sigmoid(rr))).astype(BF16)
        for h in heads:
            s_scr[h] = state[h]

    blk = lambda w, j: pl.BlockSpec((tT, w), lambda i: (i, j))
    return _call(
        body, name="gla_fwd", grid=(T // tT,),
        in_specs=[blk(512, 0), blk(512, 1), blk(1024, 1), blk(1024, 2), blk(LANES, 0)] + [_whole()] * 4,
        out_specs=[pl.BlockSpec((tT, GLA_V), lambda i: (i, 0)),
                   pl.BlockSpec((nc, GLA_HEADS, GLA_DV, GLA_DK), lambda i: (i, 0, 0, 0))],
        out_shape=[_sds((T, GLA_V), BF16), _sds((T // CHUNK, GLA_HEADS, GLA_DV, GLA_DK), F32)],
        scratch_shapes=[pltpu.VMEM((GLA_HEADS, GLA_DV, GLA_DK), F32)],
        args=(proj, proj, proj, proj, alow, wgu, b_gate, gn, _chunk_masks(tT, upper=True)), job=job)


def _sgu_mask():
    i = lax.broadcasted_iota(jnp.int32, (SGU_BLOCK, SGU_BLOCK), 0)
    j = lax.broadcasted_iota(jnp.int32, (SGU_BLOCK, SGU_BLOCK), 1)
    return lax.shift_right_logical(j, 6) <= lax.shift_right_logical(i, 6)


def _sgu_merge_fwd(x, proj, y_gla, ln_g, ln_b, w_sp, b_sp_t, w_bg, w_bs, w_o, g_pm, job=None):
    T = x.shape[0]
    tT = _row_tile(T, 512)
    nb = tT // SGU_BLOCK

    def body(x_ref, su_ref, sv_ref, gg_ref, gs_ref, yg_ref, lg_ref, lb_ref, w_ref, b_ref, wbg_ref, wbs_ref, wo_ref,
             g_ref, ys_ref, zg_ref, zs_ref, mg_ref, mix_ref, x1_ref):
        mask = _sgu_mask()
        for g in range(SGU_GROUPS):
            gc = slice(g * SGU_DG, (g + 1) * SGU_DG)
            wm = jnp.where(mask, w_ref[g], 0.0).astype(BF16)
            vf = _gelu(sv_ref[:, gc].astype(F32))
            mu = jnp.mean(vf, axis=-1, keepdims=True)
            vc = vf - mu
            rstd = lax.rsqrt(jnp.mean(vc * vc, axis=-1, keepdims=True) + EPS)
            vn = (vc * rstd * lg_ref[:, gc] + lb_ref[:, gc]).astype(BF16)
            u = _gelu(su_ref[:, gc].astype(F32))
            for b in range(nb):
                rows = slice(b * SGU_BLOCK, (b + 1) * SGU_BLOCK)
                mixed = _dot(wm, vn[rows, :]) + b_ref[:, g:g + 1]
                ys_ref[rows, gc] = (u[rows, :] * mixed).astype(BF16)
        zg = _dot(yg_ref[...], wbg_ref[...])
        zs = _dot(ys_ref[...], wbs_ref[...])
        zg_ref[...] = zg.astype(BF16)
        zs_ref[...] = zs.astype(BF16)
        merged = (_sigmoid(gg_ref[...].astype(F32)) * zg + _sigmoid(gs_ref[...].astype(F32)) * zs).astype(BF16)
        mg_ref[...] = merged
        mix = _dot(merged, wo_ref[...])
        mix_ref[...] = mix.astype(BF16)
        x1_ref[...] = x_ref[...] + mix * _rms_stats(mix) * g_ref[...]

    row = pl.BlockSpec((tT, D_MODEL), lambda i: (i, 0))
    blk = lambda j: pl.BlockSpec((tT, 1024), lambda i: (i, j))
    sds = lambda dt: _sds((T, D_MODEL), dt)
    return _call(body, name="sgu_merge_fwd", grid=(T // tT,), parallel=True,
                 in_specs=[row, blk(3), blk(4), blk(5), blk(6), row] + [_whole()] * 7
                 + [pl.BlockSpec((1, D_MODEL), lambda i: (0, 0))],
                 out_specs=[row] * 6, out_shape=[sds(BF16)] * 5 + [sds(F32)],
                 args=(x, proj, proj, proj, proj, y_gla, ln_g, ln_b, w_sp, b_sp_t, w_bg, w_bs, w_o, g_pm), job=job)


def _ffn_fwd_bwd(x1, tgt, w_fi_top, w_fi_bot, w_fo, g_pf, g_po):
    T = x1.shape[0]
    tT = _row_tile(T, 256)
    half = D_FF // 2
    kh = D_MODEL // 2

    def body(x1_ref, t_ref, top_ref, bot_ref, wfo_ref, gpf_ref, gpo_ref,
             h_ref, f_ref, dgu_ref, dy_ref, dx1_ref, loss_ref, dgpf_ref, dgpo_ref, gu_scr):
        @pl.when(pl.program_id(0) == 0)
        def _():
            loss_ref[...] = jnp.zeros_like(loss_ref)
            dgpf_ref[...] = jnp.zeros_like(dgpf_ref)
            dgpo_ref[...] = jnp.zeros_like(dgpo_ref)

        main = (half // 256) * 256
        pieces = (0, 1, None)

        def w_in_cols(ref, first_slab, p):
            if p is not None:
                return ref[first_slab + p, :, :main]
            return jnp.concatenate([ref[first_slab, :, main:], ref[first_slab + 1, :, main:]], axis=1)

        def w_out_rows(p):
            if p is not None:
                return wfo_ref[p * half:p * half + main, :]
            return jnp.concatenate([wfo_ref[main:half, :], wfo_ref[half + main:2 * half, :]], axis=0)

        def put(ref, base, p, val):
            if p is not None:
                ref[:, base + p * half:base + p * half + main] = val
            else:
                ref[:, base + main:base + half] = val[:, :half - main]
                ref[:, base + half + main:base + 2 * half] = val[:, half - main:]

        def get(ref, base, p):
            if p is not None:
                return ref[:, base + p * half:base + p * half + main]
            return jnp.concatenate([ref[:, base + main:base + half], ref[:, base + half + main:base + 2 * half]], axis=1)

        x1v = x1_ref[...]
        r2 = _rms_stats(x1v)
        h = (x1v * r2 * gpf_ref[...]).astype(BF16)
        h_ref[...] = h
        y = jnp.zeros((tT, D_MODEL), F32)
        for p in pieces:
            gate = _dot(h[:, :kh], w_in_cols(top_ref, 0, p)) + _dot(h[:, kh:], w_in_cols(bot_ref, 0, p))
            up = _dot(h[:, :kh], w_in_cols(top_ref, 2, p)) + _dot(h[:, kh:], w_in_cols(bot_ref, 2, p))
            put(gu_scr, 0, p, gate)
            put(gu_scr, D_FF, p, up)
            f = (gate * _sigmoid(gate) * up).astype(BF16)
            put(f_ref, 0, p, f)
            y = y + _dot(f, w_out_rows(p))
        r3 = _rms_stats(y)
        x2 = x1v + y * r3 * gpo_ref[...]
        err = x2 - t_ref[...]
        loss_ref[...] += jnp.sum(err * err) * (0.5 / D_MODEL)
        dx2 = err * (1.0 / D_MODEL)
        dy, dg = _rms_bwd(dx2, y, r3, gpo_ref[...])
        dgpo_ref[...] += jnp.sum(dg, axis=0, keepdims=True)
        dyb = dy.astype(BF16)
        dy_ref[...] = dyb
        dh_top = jnp.zeros((tT, kh), F32)
        dh_bot = jnp.zeros((tT, kh), F32)
        for p in pieces:
            df = _dot(dyb, w_out_rows(p), _NT)
            gate = get(gu_scr, 0, p)
            up = get(gu_scr, D_FF, p)
            sg = _sigmoid(gate)
            dgate = (df * up * (sg * (1.0 + gate * (1.0 - sg)))).astype(BF16)
            dup = (df * (gate * sg)).astype(BF16)
            put(dgu_ref, 0, p, dgate)
            put(dgu_ref, D_FF, p, dup)
            dh_top = dh_top + _dot(dgate, w_in_cols(top_ref, 0, p), _NT) + _dot(dup, w_in_cols(top_ref, 2, p), _NT)
            dh_bot = dh_bot + _dot(dgate, w_in_cols(bot_ref, 0, p), _NT) + _dot(dup, w_in_cols(bot_ref, 2, p), _NT)
        dh = jnp.concatenate([dh_top, dh_bot], axis=1)
        dx1n, dg2 = _rms_bwd(dh, x1v, r2, gpf_ref[...])
        dgpf_ref[...] += jnp.sum(dg2, axis=0, keepdims=True)
        dx1_ref[...] = dx2 + dx1n

    row = lambda w: pl.BlockSpec((tT, w), lambda i: (i, 0))
    vec = pl.BlockSpec((1, D_MODEL), lambda i: (0, 0))
    res, _ = _call(
        body, name="ffn_fwd_bwd", grid=(T // tT,),
        in_specs=[row(D_MODEL), row(D_MODEL), _whole(), _whole(), _whole(), vec, vec],
        out_specs=[row(D_MODEL), row(D_FF), row(2 * D_FF), row(D_MODEL), row(D_MODEL),
                   pl.BlockSpec((1, LANES), lambda i: (0, 0)), vec, vec],
        out_shape=[_sds((T, D_MODEL), BF16), _sds((T, D_FF), BF16), _sds((T, 2 * D_FF), BF16), _sds((T, D_MODEL), BF16),
                   _sds((T, D_MODEL), F32), _sds((1, LANES), F32), _sds((1, D_MODEL), F32), _sds((1, D_MODEL), F32)],
        scratch_shapes=[pltpu.VMEM((tT, 2 * D_FF), F32)], args=(x1, tgt, w_fi_top, w_fi_bot, w_fo, g_pf, g_po))
    return res


def _merge_sgu_bwd(dx1, mix, proj, zg, zs, w_bg, w_bs, w_o, g_pm, ln_g, ln_b, w_sp, b_sp_t, job=None):
    T = dx1.shape[0]
    tT = _row_tile(T, 256)
    nb = tT // SGU_BLOCK

    def body(dx1_ref, mix_ref, su_ref, sv_ref, gg_ref, gs_ref, zg_ref, zs_ref, wbg_ref, wbs_ref, wo_ref, g_ref,
             lg_ref, lb_ref, w_ref, b_ref,
             dmix_ref, dzg_ref, dzs_ref, dgate_ref, dyg_ref, dp_ref, dgpm_ref, dw_ref, dbt_ref, dlg_ref, dlb_ref):
        @pl.when(pl.program_id(0) == 0)
        def _():
            for ref in (dgpm_ref, dw_ref, dbt_ref, dlg_ref, dlb_ref):
                ref[...] = jnp.zeros_like(ref)

        mix = mix_ref[...].astype(F32)
        dmix, dg = _rms_bwd(dx1_ref[...], mix, _rms_stats(mix), g_ref[...])
        dgpm_ref[...] += jnp.sum(dg, axis=0, keepdims=True)
        dmb = dmix.astype(BF16)
        dmix_ref[...] = dmb
        dmerged = _dot(dmb, wo_ref[...], _NT)
        dys = None
        for k, (gate_ref, z_ref, w_br_ref, dz_ref) in enumerate(((gg_ref, zg_ref, wbg_ref, dzg_ref),
                                                                 (gs_ref, zs_ref, wbs_ref, dzs_ref))):
            sg = _sigmoid(gate_ref[...].astype(F32))
            dz = (dmerged * sg).astype(BF16)
            dz_ref[...] = dz
            dgate_ref[:, k * 1024:(k + 1) * 1024] = (dmerged * z_ref[...].astype(F32) * (sg * (1.0 - sg))).astype(BF16)
            dy_branch = _dot(dz, w_br_ref[...], _NT)
            if k == 0:
                dyg_ref[...] = dy_branch.astype(BF16)
            else:
                dys = dy_branch

        mask = _sgu_mask()
        lane = lax.broadcasted_iota(jnp.int32, (SGU_BLOCK, LANES), 1)
        for g in range(SGU_GROUPS):
            gc = slice(g * SGU_DG, (g + 1) * SGU_DG)
            gc_v = slice(1024 + g * SGU_DG, 1024 + (g + 1) * SGU_DG)
            wm = jnp.where(mask, w_ref[g], 0.0).astype(BF16)
            vf, dvf_dsv = _gelu_and_grad(sv_ref[:, gc].astype(F32))
            mu = jnp.mean(vf, axis=-1, keepdims=True)
            vc = vf - mu
            rstd = lax.rsqrt(jnp.mean(vc * vc, axis=-1, keepdims=True) + EPS)
            vhat = vc * rstd
            vn = (vhat * lg_ref[:, gc] + lb_ref[:, gc]).astype(BF16)
            u, du_dsu = _gelu_and_grad(su_ref[:, gc].astype(F32))
            dy = dys[:, gc]
            dmixed = (dy * u).astype(BF16)
            dvn_parts = []
            dw_acc = jnp.zeros((SGU_BLOCK, SGU_BLOCK), F32)
            db_acc = jnp.zeros((SGU_BLOCK, 1), F32)
            for b in range(nb):
                rows = slice(b * SGU_BLOCK, (b + 1) * SGU_BLOCK)
                mixed = _dot(wm, vn[rows, :]) + b_ref[:, g:g + 1]
                dp_ref[rows, gc] = (dy[rows, :] * mixed * du_dsu[rows, :]).astype(BF16)
                dvn_parts.append(_dot(wm, dmixed[rows, :], _TN))
                dw_acc = dw_acc + _dot(dmixed[rows, :], vn[rows, :], _NT)
                db_acc = db_acc + jnp.sum(dmixed[rows, :].astype(F32), axis=-1, keepdims=True)
            dw_ref[g] += jnp.where(mask, dw_acc, 0.0)
            dbt_ref[...] += jnp.where(lane == g, db_acc, 0.0)
            dvn = jnp.concatenate(dvn_parts, axis=0)
            dlg_ref[:, gc] += jnp.sum(dvn * vhat, axis=0, keepdims=True)
            dlb_ref[:, gc] += jnp.sum(dvn, axis=0, keepdims=True)
            dvh = dvn * lg_ref[:, gc]
            dvf = rstd * (dvh - jnp.mean(dvh, axis=-1, keepdims=True)
                          - vhat * jnp.mean(dvh * vhat, axis=-1, keepdims=True))
            dp_ref[:, gc_v] = (dvf * dvf_dsv).astype(BF16)

    row = pl.BlockSpec((tT, D_MODEL), lambda i: (i, 0))
    blk = lambda j: pl.BlockSpec((tT, 1024), lambda i: (i, j))
    vec = pl.BlockSpec((1, D_MODEL), lambda i: (0, 0))
    wide = lambda w: pl.BlockSpec((tT, w), lambda i: (i, 0))
    sds = _sds((T, D_MODEL), BF16)
    return _call(
        body, name="merge_sgu_bwd", grid=(T // tT,),
        in_specs=[row, row, blk(3), blk(4), blk(5), blk(6), row, row] + [_whole()] * 3 + [vec] + [_whole()] * 4,
        out_specs=[row, row, row, wide(W_MRG), row, wide(W_SGU), vec,
                   pl.BlockSpec((SGU_GROUPS, SGU_BLOCK, SGU_BLOCK), lambda i: (0, 0, 0)),
                   pl.BlockSpec((SGU_BLOCK, LANES), lambda i: (0, 0)), vec, vec],
        out_shape=[sds, sds, sds, _sds((T, W_MRG), BF16), sds, _sds((T, W_SGU), BF16), _sds((1, D_MODEL), F32),
                   _sds((SGU_GROUPS, SGU_BLOCK, SGU_BLOCK), F32), _sds((SGU_BLOCK, LANES), F32),
                   _sds((1, 1024), F32), _sds((1, 1024), F32)],
        args=(dx1, mix, proj, proj, proj, proj, zg, zs, w_bg, w_bs, w_o, g_pm, ln_g, ln_b, w_sp, b_sp_t), job=job)


def _gla_bwd(proj, alow, wgu, b_gate, gn, states, dy_gla, job=None):
    T = proj.shape[0]
    tT = _row_tile(T, 512)
    nc = tT // CHUNK
    nt = T // tT

    def body(q_ref, k_ref, v_ref, r_ref, al_ref, wgu_ref, bg_ref, gn_ref, later_ref, earlier_ref, st_ref, sp_ref, dy_ref,
             dp_ref, dal_ref, dgn_ref, dbg_ref, dwgu_ref, g_scr, dd_scr, dt_scr):
        step = pl.program_id(0)

        @pl.when(step == 0)
        def _():
            g_scr[...] = jnp.zeros_like(g_scr)
            dgn_ref[...] = jnp.zeros_like(dgn_ref)
            dbg_ref[...] = jnp.zeros_like(dbg_ref)
            dwgu_ref[...] = jnp.zeros_like(dwgu_ref)

        has_prev = jnp.where(step == nt - 1, 0.0, 1.0)
        logit, la, delta = _gla_decay_terms(al_ref, wgu_ref, bg_ref, later_ref)
        e = jnp.exp(delta)
        kdec_f = k_ref[...].astype(F32) * e
        kdec = kdec_f.astype(BF16)
        heads = range(GLA_HEADS)
        kcs = [slice(h * GLA_DK, (h + 1) * GLA_DK) for h in heads]
        vcs = [slice(h * GLA_DV, (h + 1) * GLA_DV) for h in heads]
        carry = [g_scr[h] for h in heads]
        dgn_acc = [jnp.zeros((1, GLA_DV), F32) for _ in heads]
        for c in reversed(range(nc)):
            rows = slice(c * CHUNK, (c + 1) * CHUNK)
            first = slice(c * CHUNK, c * CHUNK + 1)
            dec = jnp.exp(la[first, :] + delta[first, :])
            s_b = [st_ref[c, h].astype(BF16) for h in heads]
            qs = [(q_ref[rows, kcs[h]].astype(F32) * (GLA_DK ** -0.5)).astype(BF16) for h in heads]
            o = [_dot(qs[h], s_b[h], _NT) for h in heads]
            do = []
            for h in heads:
                rstd = _rms_stats(o[h])
                ohat = o[h] * rstd
                gnh = gn_ref[:, vcs[h]]
                dy = dy_ref[rows, vcs[h]].astype(F32)
                rr = r_ref[rows, vcs[h]].astype(F32)
                sg = _sigmoid(rr)
                don = dy * (rr * sg)
                dp_ref[rows, OFF_R + h * GLA_DV:OFF_R + (h + 1) * GLA_DV] = (
                    dy * (ohat * gnh) * (sg * (1.0 + rr * (1.0 - sg)))).astype(BF16)
                dgn_acc[h] = dgn_acc[h] + jnp.sum(don * ohat, axis=0, keepdims=True)
                dn = don * gnh
                do.append((rstd * (dn - ohat * jnp.mean(dn * ohat, axis=-1, keepdims=True))).astype(BF16))
            dq = [_dot(do[h], s_b[h]) for h in heads]
            g_t = [_dot(do[h], qs[h], _TN) + carry[h] for h in heads]
            g_b = [g_t[h].astype(BF16) for h in heads]
            dv = [_dot(kdec[rows, kcs[h]], g_b[h], _NT) for h in heads]
            dkdec = [_dot(v_ref[rows, vcs[h]], g_b[h]) for h in heads]
            for h in heads:
                s_prev = st_ref[c - 1, h] if c > 0 else sp_ref[0, h] * has_prev
                ddec = jnp.sum(g_t[h] * s_prev, axis=0, keepdims=True)
                carry[h] = g_t[h] * dec[:, kcs[h]]
                dp_ref[rows, OFF_Q + h * GLA_DK:OFF_Q + (h + 1) * GLA_DK] = (dq[h] * (GLA_DK ** -0.5)).astype(BF16)
                dp_ref[rows, OFF_V + h * GLA_DV:OFF_V + (h + 1) * GLA_DV] = dv[h].astype(BF16)
                dp_ref[rows, OFF_K + h * GLA_DK:OFF_K + (h + 1) * GLA_DK] = (dkdec[h] * e[rows, kcs[h]]).astype(BF16)
                dd_scr[rows, kcs[h]] = dkdec[h] * kdec_f[rows, kcs[h]]
                dt_scr[rows, kcs[h]] = jnp.broadcast_to(ddec * dec[:, kcs[h]], (CHUNK, GLA_DK))
        for h in heads:
            g_scr[h] = carry[h]
            dgn_ref[:, vcs[h]] += dgn_acc[h]
        dla = _dot_exact_lhs(earlier_ref[...], dd_scr[...]) + dt_scr[...]
        dlogit = dla * (1.0 / GLA_TAU) * _sigmoid(-logit)
        dbg_ref[...] += jnp.sum(dlogit, axis=0, keepdims=True)
        dwgu_ref[...] += _dot_bf16(al_ref[...], dlogit, _TN)
        dal_ref[...] = _dot_bf16(dlogit, wgu_ref[...], _NT).astype(BF16)

    rev = lambda i: nt - 1 - i
    blk = lambda w, j: pl.BlockSpec((tT, w), lambda i: (rev(i), j))
    st_blk = pl.BlockSpec((nc, GLA_HEADS, GLA_DV, GLA_DK), lambda i: (rev(i), 0, 0, 0))
    sp_blk = pl.BlockSpec((1, GLA_HEADS, GLA_DV, GLA_DK), lambda i: (jnp.maximum(rev(i) * nc - 1, 0), 0, 0, 0))
    return _call(
        body, name="gla_bwd", grid=(nt,),
        in_specs=[blk(512, 0), blk(512, 1), blk(1024, 1), blk(1024, 2), blk(LANES, 0)] + [_whole()] * 5
        + [st_blk, sp_blk, blk(GLA_V, 0)],
        out_specs=[blk(W_GLA, 0), blk(LANES, 0), pl.BlockSpec((1, GLA_V), lambda i: (0, 0)),
                   pl.BlockSpec((1, GLA_QK), lambda i: (0, 0)), pl.BlockSpec((LANES, GLA_QK), lambda i: (0, 0))],
        out_shape=[_sds((T, W_GLA), BF16), _sds((T, LANES), BF16), _sds((1, GLA_V), F32), _sds((1, GLA_QK), F32),
                   _sds((LANES, GLA_QK), F32)],
        scratch_shapes=[pltpu.VMEM((GLA_HEADS, GLA_DV, GLA_DK), F32), pltpu.VMEM((tT, GLA_QK), F32),
                        pltpu.VMEM((tT, GLA_QK), F32)],
        args=(proj, proj, proj, proj, alow, wgu, b_gate, gn, _chunk_masks(tT, upper=True), _chunk_masks(tT, upper=False),
              states, states, dy_gla), job=job)


def _inproj_bwd(x, dx1, g1, w_all, dparts, job=None):
    T = x.shape[0]
    tT = _row_tile(T, 512)
    offs = (0, W_GLA, W_GLA + W_SGU, N_MAIN)

    def body(x_ref, dx1_ref, g_ref, w_ref, *rest):
        part_refs, (dx_ref, dg_ref) = rest[:len(offs)], rest[len(offs):]

        @pl.when(pl.program_id(0) == 0)
        def _():
            dg_ref[...] = jnp.zeros_like(dg_ref)

        da = jnp.zeros((tT, D_MODEL), F32)
        for off, p_ref in zip(offs, part_refs):
            da = da + _dot(p_ref[...], w_ref[:, off:off + p_ref.shape[1]], _NT)
        xv = x_ref[...]
        dx, dg = _rms_bwd(da, xv, _rms_stats(xv), g_ref[...])
        dg_ref[...] += jnp.sum(dg, axis=0, keepdims=True)
        dx_ref[...] = dx1_ref[...] + dx

    row = lambda w: pl.BlockSpec((tT, w), lambda i: (i, 0))
    vec = pl.BlockSpec((1, D_MODEL), lambda i: (0, 0))
    return _call(
        body, name="inproj_bwd", grid=(T // tT,),
        in_specs=[row(D_MODEL), row(D_MODEL), vec, _whole()] + [row(p.shape[1]) for p in dparts],
        out_specs=[row(D_MODEL), vec], out_shape=[_sds((T, D_MODEL), F32), _sds((1, D_MODEL), F32)],
        args=(x, dx1, g1, w_all, *dparts), job=job)


def _tn_matmul(a, b, name, job=None):
    T, M = a.shape
    N = b.shape[1]
    tk = _row_tile(T, 1024)
    tm = M if M <= 1024 else 1408
    tn = N if N <= 3072 else N // 2
    assert M % tm == 0 and N % tn == 0

    def body(a_ref, b_ref, o_ref):
        @pl.when(pl.program_id(2) == 0)
        def _():
            o_ref[...] = _dot(a_ref[...], b_ref[...], _TN)

        @pl.when(pl.program_id(2) > 0)
        def _():
            o_ref[...] += _dot(a_ref[...], b_ref[...], _TN)

    res, jres = _call(
        body, name=name, grid=(M // tm, N // tn, T // tk),
        in_specs=[pl.BlockSpec((tk, tm), lambda i, j, k: (k, i)), pl.BlockSpec((tk, tn), lambda i, j, k: (k, j))],
        out_specs=[pl.BlockSpec((tm, tn), lambda i, j, k: (i, j))], out_shape=[_sds((M, N), F32)], args=(a, b), job=job)
    return res[0], jres


def _pad_rows(a, rows=8):
    return jnp.pad(a, ((0, rows - a.shape[0]), (0, LANES - a.shape[1])))


def _halves_view(dw):
    r = dw.shape[0] // N_CHIPS
    return dw.reshape(N_CHIPS, 2, r // 2, dw.shape[1])


def kernel(x, norm_pre_mix, w_in, w_gate_up, b_gate, gla_norm, sgu_ln_g, sgu_ln_b, w_spatial, b_spatial, w_branch_gla, w_branch_sgu, w_out, norm_post_mix, norm_pre_ffn, w_ffn_in, w_ffn_out, norm_post_ffn, loss_target, m_norm_pre_mix, m_w_in, m_w_gate_up, m_b_gate, m_gla_norm, m_sgu_ln_g, m_sgu_ln_b, m_w_spatial, m_b_spatial, m_w_branch_gla, m_w_branch_sgu, m_w_out, m_norm_post_mix, m_norm_pre_ffn, m_w_ffn_in, m_w_ffn_out, m_norm_post_ffn, v_norm_pre_mix, v_w_in, v_w_gate_up, v_b_gate, v_gla_norm, v_sgu_ln_g, v_sgu_ln_b, v_w_spatial, v_b_spatial, v_w_branch_gla, v_w_branch_sgu, v_w_out, v_norm_post_mix, v_norm_pre_ffn, v_w_ffn_in, v_w_ffn_out, v_norm_post_ffn):
    chip = 2 * lax.axis_index("x") + lax.axis_index("y")
    xt, tgt = x[0], loss_target[0]

    tiny = jnp.concatenate([w_gate_up[0], _pad_rows(gla_norm[0]), _pad_rows(sgu_ln_g[0]), _pad_rows(sgu_ln_b[0]),
                            jnp.zeros((24, LANES), F32)], axis=0)

    def with_own(gathered, own):
        return lax.dynamic_update_slice(gathered, own[None], (chip, 0, 0))

    w_in_t, m_in_t, v_in_t = w_in[0].T, m_w_in[0].T, v_w_in[0].T
    w_in_b = _transposed_cast(w_in_t)
    g_in, g_tiny = _run_job(_job_gather([w_in_b, tiny]), "gather_w_in")
    g_tiny = with_own(g_tiny, tiny)
    w_all = _relayout_w_in(with_own(g_in, w_in_b))
    cols = lambda a: a.transpose(1, 0, 2).reshape(a.shape[1], N_CHIPS * a.shape[2])
    wgu = jnp.pad(cols(g_tiny[:, 0:16]), ((0, LANES - GLA_RANK), (0, 0)))
    gn = cols(g_tiny[:, 16:20, :64]).reshape(1, GLA_V)
    ln_g = cols(g_tiny[:, 24:28, :64]).reshape(1, 1024)
    ln_b = cols(g_tiny[:, 32:36, :64]).reshape(1, 1024)
    b_sp_t = jnp.pad(b_spatial[0].T, ((0, 0), (0, LANES - SGU_GROUPS)))
    w_sp = w_spatial[0]

    own_rows = [w_branch_gla[0].astype(BF16), w_branch_sgu[0].astype(BF16), w_out[0].astype(BF16)]
    (a, proj, alow), g_rows = _inproj_fwd(xt, norm_pre_mix, w_all, job=_job_gather(own_rows))
    rows = lambda g: g.reshape(N_CHIPS * g.shape[1], g.shape[2])
    w_bg, w_bs, w_o = [rows(with_own(g, own)) for g, own in zip(g_rows, own_rows)]
    w_fi_b = w_ffn_in[0].astype(BF16)
    fi_top, fi_bot = w_fi_b[:D_MODEL // 2], w_fi_b[D_MODEL // 2:]
    fo_b = w_ffn_out[0].astype(BF16)
    (y_gla, states), (g_top,) = _gla_fwd(proj, alow, wgu, b_gate, gn, job=_job_gather([fi_top]))
    (y_sgu, zg, zs, merged, mix, x1), (g_bot, g_fo) = _sgu_merge_fwd(
        xt, proj, y_gla, ln_g, ln_b, w_sp, b_sp_t, w_bg, w_bs, w_o, norm_post_mix, job=_job_gather([fi_bot, fo_b]))
    h, f, dgu, dy, dx1, loss, d_gpf, d_gpo = _ffn_fwd_bwd(x1, tgt, with_own(g_top, fi_top), with_own(g_bot, fi_bot),
                                                          rows(with_own(g_fo, fo_b)), norm_pre_ffn, norm_post_ffn)

    own_part = lambda c: lax.dynamic_index_in_dim(c, chip, 0, keepdims=False)
    whole = lambda hs: [[(h_, None)] for h_ in hs]
    dw_fo, _ = _tn_matmul(f, dy, "dw_ffn_out")
    dw_fo4 = _halves_view(dw_fo)
    dw_fi, (q_fo,) = _tn_matmul(h, dgu, "dw_ffn_in", job=_job_to_other_core([[(dw_fo4, 0)]]))
    c_fo = _presum(dw_fo4, q_fo, "presum_ffn_out")
    (dmix, dzg, dzs, dp_mrg, dyg, dp_sgu, d_gpm, d_wsp, d_bsp_t, d_lng, d_lnb), (s_fo, q_fi) = _merge_sgu_bwd(
        dx1, mix, proj, zg, zs, w_bg, w_bs, w_o, norm_post_mix, ln_g, ln_b, w_sp, b_sp_t,
        job=_join(_job_scatter([c_fo]), _job_to_other_core([[(dw_fi, 0)]])))
    c_fi = _presum(dw_fi, q_fi, "presum_ffn_in")
    dw_c, _ = _tn_matmul(a, dp_mrg, "dw_in_merge")
    dw_b, _ = _tn_matmul(a, dp_sgu, "dw_in_sgu")
    dw_o4 = _halves_view(_tn_matmul(merged, dmix, "dw_out")[0])
    dw_bg4 = _halves_view(_tn_matmul(y_gla, dzg, "dw_branch_gla")[0])
    dw_bs4 = _halves_view(_tn_matmul(y_sgu, dzs, "dw_branch_sgu")[0])
    h_fo = _sum_slots(own_part(c_fo), s_fo, "sum_ffn_out")
    (dp_gla, dal, d_gn, d_bg, d_wgu), (s_fi, t_fo, q_b, q_c, q_o, q_bg, q_bs) = _gla_bwd(
        proj, alow, wgu, b_gate, gn, states, dyg,
        job=_join(_job_scatter([c_fi]), _job_to_other_core(
            whole([h_fo]) + [[(dw_b, 0)], [(dw_c, 0)], [(dw_o4, 0)], [(dw_bg4, 0)], [(dw_bs4, 0)]])))
    c_o, c_bg, c_bs = (_presum(dw_o4, q_o, "presum_out"), _presum(dw_bg4, q_bg, "presum_branch_gla"),
                       _presum(dw_bs4, q_bs, "presum_branch_sgu"))
    h_fi = _sum_slots(own_part(c_fi), s_fi, "sum_ffn_in")
    dw_d, _ = _tn_matmul(a, dal, "dw_in_gate")
    dw_a, (s_o, s_bg, s_bs, t_fi, q_d) = _tn_matmul(
        a, dp_gla, "dw_in_gla",
        job=_join(_job_scatter([c_o, c_bg, c_bs]), _job_to_other_core(whole([h_fi]) + [[(dw_d, 0)]])))
    h_o, h_bg, h_bs = (_sum_slots(own_part(c_o), s_o, "sum_out"), _sum_slots(own_part(c_bg), s_bg, "sum_branch_gla"),
                       _sum_slots(own_part(c_bs), s_bs, "sum_branch_sgu"))

    grads, deltas, new_m, new_v = {}, {}, {}, {}

    def update(name, w, m, v, g_mine, g_theirs, job=None):
        (g, d, m2, v2), jres = _adamw(w[0], m[0], v[0], g_mine, g_theirs, "adamw_" + name, job=job)
        grads[name], deltas[name], new_m[name], new_v[name] = g[None], d[None], m2[None], v2[None]
        return jres

    dw_in = [(dw_a, 0), (dw_b, W_GLA), (dw_c, W_GLA + W_SGU), (dw_d, N_MAIN)]
    q_a, t_o, t_bg, t_bs = update("w_ffn_out", w_ffn_out, m_w_ffn_out, v_w_ffn_out, [h_fo], [t_fo],
                                  job=_job_to_other_core([[(dw_a, 0)]] + whole([h_o, h_bg, h_bs])))
    q_in = [q_a, q_b, q_c, q_d]
    hr_in = D_MODEL // 2
    c_in_a, _ = _presum_w_in(dw_in, q_in, 0, hr_in // 8, "presum_w_in_a")
    c_in_b, (s_in_a,) = _presum_w_in(dw_in, q_in, hr_in // 8, 7 * hr_in // 8, "presum_w_in_b",
                                     job=_job_scatter([c_in_a]))
    update("w_ffn_in", w_ffn_in, m_w_ffn_in, v_w_ffn_in, [h_fi], [t_fi])
    update("w_out", w_out, m_w_out, v_w_out, [h_o], [t_o])
    update("w_branch_gla", w_branch_gla, m_w_branch_gla, v_w_branch_gla, [h_bg], [t_bg])
    update("w_branch_sgu", w_branch_sgu, m_w_branch_sgu, v_w_branch_sgu, [h_bs], [t_bs])
    (grad_x, d_g1), (s_in_b,) = _inproj_bwd(xt, dx1, norm_pre_mix, w_all, (dp_gla, dp_sgu, dp_mrg, dal),
                                            job=_job_scatter([c_in_b]))
    h_in = [_sum_slots(own_part(c_in_a), s_in_a, "sum_w_in_a"), _sum_slots(own_part(c_in_b), s_in_b, "sum_w_in_b")]
    t_in = _run_job(_job_to_other_core(whole(h_in)), "swap_w_in")
    for store, val in zip((grads, deltas, new_m, new_v),
                          _adamw_transposed(w_in_t, m_in_t, v_in_t, h_in, t_in, "adamw_w_in")):
        store["w_in"] = val.T[None]

    small_names = ["w_spatial", "w_gate_up", "norm_pre_mix", "norm_post_mix", "norm_pre_ffn", "norm_post_ffn", "b_gate",
                   "b_spatial", "gla_norm", "sgu_ln_g", "sgu_ln_b"]
    loss_out, small = _small_adamw(
        _small_sum([d_wsp, d_wgu, d_g1, d_gpm, d_gpf, d_gpo, d_bg, d_bsp_t, d_gn, d_lng, d_lnb, loss]),
        [w_spatial, w_gate_up, norm_pre_mix, norm_post_mix, norm_pre_ffn, norm_post_ffn, b_gate, b_spatial, gla_norm,
         sgu_ln_g, sgu_ln_b],
        [m_w_spatial, m_w_gate_up, m_norm_pre_mix, m_norm_post_mix, m_norm_pre_ffn, m_norm_post_ffn, m_b_gate,
         m_b_spatial, m_gla_norm, m_sgu_ln_g, m_sgu_ln_b],
        [v_w_spatial, v_w_gate_up, v_norm_pre_mix, v_norm_post_mix, v_norm_pre_ffn, v_norm_post_ffn, v_b_gate,
         v_b_spatial, v_gla_norm, v_sgu_ln_g, v_sgu_ln_b])
    for store, vals in zip((grads, deltas, new_m, new_v), small):
        store.update(zip(small_names, vals))

    order = ["norm_pre_mix", "w_in", "w_gate_up", "b_gate", "gla_norm", "sgu_ln_g", "sgu_ln_b", "w_spatial", "b_spatial",
             "w_branch_gla", "w_branch_sgu", "w_out", "norm_post_mix", "norm_pre_ffn", "w_ffn_in", "w_ffn_out",
             "norm_post_ffn"]
    out = [loss_out, grad_x[None]]
    for store in (grads, deltas, new_m, new_v):
        out.extend(store[n] for n in order)
    return tuple(out)
```

```python
import jax
import jax.numpy as jnp
from jax import lax
from jax.experimental import pallas as pl
from jax.experimental.pallas import tpu as pltpu

F32 = jnp.float32
BF16 = jnp.bfloat16

D_MODEL = 1024
GLA_HEADS = 4
GLA_DK = 128
GLA_DV = 256
GLA_QK = GLA_HEADS * GLA_DK
GLA_V = GLA_HEADS * GLA_DV
GLA_RANK = 16
GLA_TAU = 16.0
CHUNK = 64
SGU_GROUPS = 4
SGU_BLOCK = 128
SGU_DG = 256
D_FF = 2816
EPS = 1e-6
LANES = 128

OFF_Q, OFF_K, OFF_V, OFF_R, OFF_SU, OFF_SV, OFF_GG, OFF_GS, OFF_AL = 0, 512, 1024, 2048, 3072, 4096, 5120, 6144, 7168
W_GLA, W_SGU, W_MRG = 3072, 2048, 2048
N_MAIN = 7168
N_ALL = N_MAIN + LANES
_IN_SPLITS = (GLA_QK, GLA_QK, GLA_V, GLA_V, GLA_RANK, 1024, 1024, 1024, 1024)
_IN_STARTS = tuple(sum(_IN_SPLITS[:i]) for i in range(len(_IN_SPLITS) + 1))
_IN_DST = (OFF_Q, OFF_K, OFF_V, OFF_R, OFF_AL, OFF_SU, OFF_SV, OFF_GG, OFF_GS)
D_IN = _IN_STARTS[-1]

ADAM_LR = 0.001
ADAM_B1 = 0.9
ADAM_B2 = 0.999
ADAM_EPS = 1e-08
ADAM_WD = 0.01
ADAM_STEP = 10

VMEM_LIMIT_BYTES = 56 * 1024 * 1024
N_CHIPS = 4
N_PEER = N_CHIPS - 1
N_DEV = 8
MESH = pl.DeviceIdType.MESH

_NN = (((1,), (0,)), ((), ()))
_NT = (((1,), (1,)), ((), ()))
_TN = (((0,), (0,)), ((), ()))


def _dot(a, b, dims=_NN):
    return lax.dot_general(a, b, dims, preferred_element_type=F32)


def _split(x):
    hi = x.astype(BF16)
    lo = (x - hi.astype(F32)).astype(BF16)
    return hi, lo


def _dot_bf16(a, b, dims=_NN):
    return _dot(a.astype(BF16), b.astype(BF16), dims)


def _dot_exact_lhs(m, x):
    xh, xl = _split(x)
    return _dot(m, xh) + _dot(m, xl)


def _sigmoid(x):
    return 0.5 * jnp.tanh(0.5 * x) + 0.5


def _log_sigmoid(x):
    return jnp.minimum(x, 0.0) - jnp.log(1.0 + jnp.exp(-jnp.abs(x)))


_GELU_C = 0.7978845608028654
_GELU_A = 0.044715


def _gelu_and_grad(x):
    x2 = x * x
    t = jnp.tanh(_GELU_C * (x + _GELU_A * x * x2))
    g = 0.5 * x * (1.0 + t)
    dg = 0.5 * (1.0 + t) + 0.5 * x * (1.0 - t * t) * (_GELU_C * (1.0 + 3.0 * _GELU_A * x2))
    return g, dg


def _gelu(x):
    t = jnp.tanh(_GELU_C * (x + _GELU_A * x * x * x))
    return 0.5 * x * (1.0 + t)


def _rms_stats(x):
    return lax.rsqrt(jnp.mean(x * x, axis=-1, keepdims=True) + EPS)


def _rms_bwd(dout, y, r, g):
    yhat = y * r
    dn = dout * g
    dy = r * (dn - yhat * jnp.mean(dn * yhat, axis=-1, keepdims=True))
    return dy, dout * yhat


def _whole():
    return pl.BlockSpec(memory_space=pltpu.VMEM)


def _row_tile(T, want):
    t = min(T, want)
    assert T % t == 0
    return t


def _chunk_masks(tT, upper):
    row = lax.broadcasted_iota(jnp.int32, (tT, tT), 0)
    col = lax.broadcasted_iota(jnp.int32, (tT, tT), 1)
    same = (row // CHUNK) == (col // CHUNK)
    tri = (col > row) if upper else (col < row)
    return jnp.where(same & tri, 1.0, 0.0).astype(BF16)


class _Job:
    def __init__(self, ins, out_shapes, scratch, start, finish, mid=None):
        self.ins, self.out_shapes, self.scratch = list(ins), list(out_shapes), list(scratch)
        self.start, self.finish, self.mid = start, finish, mid


def _join(*jobs):
    def split(refs, counts):
        out, at = [], 0
        for n in counts:
            out.append(refs[at:at + n])
            at += n
        return out

    ni, no, ns = [len(j.ins) for j in jobs], [len(j.out_shapes) for j in jobs], [len(j.scratch) for j in jobs]

    def start(ins, outs, scr):
        for j, a, b, c in zip(jobs, split(ins, ni), split(outs, no), split(scr, ns)):
            j.start(a, b, c)

    def finish(ins, outs, scr):
        for j, a, b, c in zip(jobs, split(ins, ni), split(outs, no), split(scr, ns)):
            j.finish(a, b, c)

    def mid(ins, outs, scr):
        for j, a, b, c in zip(jobs, split(ins, ni), split(outs, no), split(scr, ns)):
            if j.mid is not None:
                j.mid(a, b, c)

    return _Job(sum((j.ins for j in jobs), []), sum((j.out_shapes for j in jobs), []),
                sum((j.scratch for j in jobs), []), start, finish, mid if any(j.mid for j in jobs) else None)


def _mesh_pos():
    return lax.axis_index("x"), lax.axis_index("y"), lax.axis_index("c")


def _peer_chips(xi, yi):
    return [(1 - xi, yi), (xi, 1 - yi), (1 - xi, 1 - yi)]


def _half(ci, rows):
    return pl.ds(pl.multiple_of(ci * rows, 8), rows)


def _sds(shape, dtype):
    return jax.ShapeDtypeStruct(tuple(shape), dtype)


def _job_gather(arrs):
    n = len(arrs)
    kinds = 12
    Y0, Y1, X1, X0, ON_X, ON_Y, D2D = 0, 1, 2, 3, 4, 5, 6

    def copies(ins, outs, scr):
        send_sems, recv_sems = scr
        xi, yi, ci = _mesh_pos()
        me, cx, cy, cd = 2 * xi + yi, 2 * (1 - xi) + yi, 2 * xi + (1 - yi), 2 * (1 - xi) + (1 - yi)
        to_x, to_y, to_core = (1 - xi, yi, ci), (xi, 1 - yi, ci), (xi, yi, 1 - ci)
        table = []
        for k in range(n):
            qr = arrs[k].shape[0] // 4

            def rows(core, q):
                return pl.ds(pl.multiple_of((2 * core + q) * qr, 8), qr)

            def cp(kind, src, dst, to):
                s = k * kinds + kind
                return pltpu.make_async_remote_copy(src_ref=src, dst_ref=dst, send_sem=send_sems.at[s],
                                                    recv_sem=recv_sems.at[s], device_id=to, device_id_type=MESH)

            def slab(chip, core, q):
                return outs[k].at[chip, rows(core, q)]

            t = {}
            for kind, q, to, frm in ((Y0, 0, to_y, cy), (Y1, 1, to_y, cy), (X1, 1, to_x, cx), (X0, 0, to_x, cx)):
                mine = ins[k].at[rows(ci, q)]
                t[kind] = (cp(kind, mine, slab(me, ci, q), to), cp(kind, mine, slab(frm, ci, q), to))
            t[ON_X] = (cp(ON_X, slab(cy, ci, 0), slab(cy, ci, 0), to_x), cp(ON_X, slab(cy, ci, 0), slab(cd, ci, 0), to_x))
            t[ON_Y] = (cp(ON_Y, slab(cx, ci, 1), slab(cx, ci, 1), to_y), cp(ON_Y, slab(cx, ci, 1), slab(cd, ci, 1), to_y))
            for i, (chip, q) in enumerate(((cy, 0), (cy, 1), (cx, 1), (cx, 0), (cd, 0), (cd, 1))):
                t[D2D + i] = (cp(D2D + i, slab(chip, ci, q), slab(chip, ci, q), to_core),
                              cp(D2D + i, slab(chip, ci, q), slab(chip, 1 - ci, q), to_core))
            table.append(t)
        return table

    def start(ins, outs, scr):
        table = copies(ins, outs, scr)
        for kind in (Y0, X1, Y1, X0):
            for t in table:
                t[kind][0].start()

    def arrived(table, kind, then):
        for t in table:
            t[kind][1].wait_recv()
            for nxt in then:
                t[nxt][0].start()

    def mid(ins, outs, scr):
        table = copies(ins, outs, scr)
        arrived(table, Y0, (ON_X, D2D + 0))
        arrived(table, X1, (ON_Y, D2D + 2))

    def finish(ins, outs, scr):
        table = copies(ins, outs, scr)
        arrived(table, Y1, (D2D + 1,))
        arrived(table, X0, (D2D + 3,))
        arrived(table, ON_X, (D2D + 4,))
        arrived(table, ON_Y, (D2D + 5,))
        for t in table:
            for i in range(6):
                t[D2D + i][1].wait_recv()
            for kind in range(kinds):
                t[kind][0].wait_send()

    dma = pltpu.SemaphoreType.DMA
    return _Job(arrs, [_sds((N_CHIPS,) + a.shape, a.dtype) for a in arrs], [dma((n * kinds,))] * 2, start, finish, mid)


def _job_scatter(parts):
    n = len(parts)

    def copies(ins, outs, scr):
        send_sems, recv_sems = scr
        xi, yi, ci = _mesh_pos()
        res = []
        for k in range(n):
            for j, (px, py) in enumerate(_peer_chips(xi, yi)):
                s = k * N_PEER + j
                res.append(pltpu.make_async_remote_copy(
                    src_ref=ins[k].at[2 * px + py], dst_ref=outs[k].at[j], send_sem=send_sems.at[s],
                    recv_sem=recv_sems.at[s], device_id=(px, py, ci), device_id_type=MESH))
        return res

    def start(ins, outs, scr):
        for cp in copies(ins, outs, scr):
            cp.start()

    def finish(ins, outs, scr):
        for cp in copies(ins, outs, scr):
            cp.wait_recv()
            cp.wait_send()

    dma = pltpu.SemaphoreType.DMA
    return _Job(parts, [_sds((N_PEER,) + p.shape[1:], p.dtype) for p in parts], [dma((n * N_PEER,))] * 2, start, finish)


def _job_to_other_core(groups):
    pieces = [(g, a, off) for g, group in enumerate(groups) for a, off in group]
    n = len(pieces)

    def geometry(group):
        a0, off0 = group[0]
        if off0 is None:
            return a0.shape
        if a0.ndim == 4:
            return (N_CHIPS, a0.shape[2], a0.shape[3])
        return (a0.shape[0] // 2, sum(a.shape[1] for a, _ in group))

    def copies(ins, outs, scr):
        send_sems, recv_sems = scr
        xi, yi, ci = _mesh_pos()
        res = []
        for p, (g, a, off) in enumerate(pieces):
            if off is None:
                give, land = ins[p], outs[g]
            elif a.ndim == 4:
                give, land = ins[p].at[pl.ds(0, N_CHIPS), 1 - ci], outs[g]
            else:
                hr, w = a.shape[0] // 2, a.shape[1]
                give, land = ins[p].at[_half(1 - ci, hr)], outs[g].at[pl.ds(0, hr), pl.ds(off, w)]
            res.append(pltpu.make_async_remote_copy(
                src_ref=give, dst_ref=land, send_sem=send_sems.at[p], recv_sem=recv_sems.at[p],
                device_id=(xi, yi, 1 - ci), device_id_type=MESH))
        return res

    def start(ins, outs, scr):
        for cp in copies(ins, outs, scr):
            cp.start()

    def finish(ins, outs, scr):
        for cp in copies(ins, outs, scr):
            cp.wait_recv()
            cp.wait_send()

    dma = pltpu.SemaphoreType.DMA
    return _Job([a for _, a, _ in pieces], [_sds(geometry(group), group[0][0].dtype) for group in groups],
                [dma((n,))] * 2, start, finish)


def _call(body, *, name, grid, in_specs, out_specs, out_shape, args, scratch_shapes=(), parallel=False, job=None,
          by_core=False):
    n_in, n_out, n_scr = len(in_specs), len(out_specs), len(scratch_shapes)
    hbm = pl.BlockSpec(memory_space=pl.ANY)
    n_ji, n_jo = (len(job.ins), len(job.out_shapes)) if job is not None else (0, 0)
    lead = 1 if by_core else 0

    def kernel_fn(*refs):
        core, refs = refs[:lead], refs[lead:]
        ins, refs = refs[:n_in], refs[n_in:]
        j_ins, refs = refs[:n_ji], refs[n_ji:]
        outs, refs = refs[:n_out], refs[n_out:]
        j_outs, refs = refs[:n_jo], refs[n_jo:]
        scr, j_scr = refs[:n_scr], refs[n_scr:]
        if job is None:
            body(*core, *ins, *outs, *scr)
            return
        ids = [pl.program_id(d) for d in range(len(grid))]
        first = ids[0] == 0
        last = ids[0] == grid[0] - 1
        for d in range(1, len(grid)):
            first = first & (ids[d] == 0)
            last = last & (ids[d] == grid[d] - 1)

        @pl.when(first)
        def _():
            job.start(j_ins, j_outs, j_scr)

        if job.mid is not None and grid[0] >= 4:
            half_way = ids[0] == grid[0] // 2
            for d in range(1, len(grid)):
                half_way = half_way & (ids[d] == 0)

            @pl.when(half_way)
            def _():
                job.mid(j_ins, j_outs, j_scr)

        body(*core, *ins, *outs, *scr)

        @pl.when(last)
        def _():
            if job.mid is not None and grid[0] < 4:
                job.mid(j_ins, j_outs, j_scr)
            job.finish(j_ins, j_outs, j_scr)

    sem = ("parallel" if parallel and job is None else "arbitrary",) * len(grid)
    all_in = list(in_specs) + [hbm] * n_ji
    all_out = list(out_specs) + [hbm] * n_jo
    all_scratch = list(scratch_shapes) + (job.scratch if job is not None else [])
    all_shapes = list(out_shape) + (job.out_shapes if job is not None else [])
    all_args = list(args) + (job.ins if job is not None else [])
    params = pltpu.CompilerParams(dimension_semantics=sem, vmem_limit_bytes=VMEM_LIMIT_BYTES)
    if by_core:
        spec = pltpu.PrefetchScalarGridSpec(num_scalar_prefetch=1, grid=grid, in_specs=all_in, out_specs=all_out,
                                            scratch_shapes=all_scratch)
        core = lax.axis_index("c").astype(jnp.int32).reshape(1)
        res = pl.pallas_call(kernel_fn, name=name, grid_spec=spec, out_shape=all_shapes, compiler_params=params)(
            core, *all_args)
    else:
        res = pl.pallas_call(kernel_fn, name=name, grid=grid, in_specs=all_in, out_specs=all_out, out_shape=all_shapes,
                             scratch_shapes=all_scratch, compiler_params=params)(*all_args)
    return list(res[:n_out]), list(res[n_out:])


def _run_job(job, name):
    n_i, n_o = len(job.ins), len(job.out_shapes)

    def body(*refs):
        ins, outs, scr = refs[:n_i], refs[n_i:n_i + n_o], refs[n_i + n_o:]
        job.start(ins, outs, scr)
        if job.mid is not None:
            job.mid(ins, outs, scr)
        job.finish(ins, outs, scr)

    hbm = pl.BlockSpec(memory_space=pl.ANY)
    return list(pl.pallas_call(body, name=name, in_specs=[hbm] * n_i, out_specs=[hbm] * n_o, out_shape=job.out_shapes,
                               scratch_shapes=job.scratch)(*job.ins))


def _adam_values(w, m, v, g):
    m2 = ADAM_B1 * m + (1.0 - ADAM_B1) * g
    v2 = ADAM_B2 * v + (1.0 - ADAM_B2) * (g * g)
    delta = -ADAM_LR * ((m2 / (1.0 - ADAM_B1 ** ADAM_STEP)) / (jnp.sqrt(v2 / (1.0 - ADAM_B2 ** ADAM_STEP)) + ADAM_EPS)
                        + ADAM_WD * w)
    return delta, m2, v2


_P_WSP, _P_WGU, _P_NORM, _P_BG, _P_BSP, _P_HEAD, _P_LOSS, _P_ROWS = 0, 512, 576, 608, 616, 624, 720, 736


def _small_sum(dgrads, job=None):
    n_ji, n_jo = (len(job.ins), len(job.out_shapes)) if job is not None else (0, 0)
    hr = _P_ROWS // 2

    def body(dwsp, dwgu, dg1, dgpm, dgpf, dgpo, dbg, dbspt, dgn, dlng, dlnb, loss_in, *rest):
        j_ins, rest = rest[:n_ji], rest[n_ji:]
        tot, rest = rest[0], rest[1:]
        j_outs, rest = rest[:n_jo], rest[n_jo:]
        pack, pair, slots, send_sems, recv_sems = rest[:5]
        j_scr = rest[5:]
        if job is not None:
            job.start(j_ins, j_outs, j_scr)
        xi, yi, ci = _mesh_pos()
        chip = 2 * xi + yi

        pack[...] = jnp.zeros_like(pack)
        for g in range(SGU_GROUPS):
            pack[_P_WSP + g * SGU_BLOCK:_P_WSP + (g + 1) * SGU_BLOCK] = dwsp[g]
        for j in range(N_CHIPS):
            pack[_P_WGU + GLA_RANK * j:_P_WGU + GLA_RANK * (j + 1)] = dwgu[0:GLA_RANK, LANES * j:LANES * (j + 1)]
        for k, r in enumerate((dg1, dgpm, dgpf, dgpo)):
            for q in range(8):
                pack[_P_NORM + 8 * k + q:_P_NORM + 8 * k + q + 1] = r[:, LANES * q:LANES * (q + 1)]
        for q in range(4):
            pack[_P_BG + q:_P_BG + q + 1] = dbg[:, LANES * q:LANES * (q + 1)]
        pack[_P_BSP:_P_BSP + SGU_GROUPS] = jnp.transpose(dbspt[...])[0:SGU_GROUPS]
        for k, r in enumerate((dgn, dlng, dlnb)):
            for j in range(N_CHIPS):
                for hh in range(4):
                    row = _P_HEAD + 32 * k + 8 * j + hh
                    pack[row:row + 1, 0:64] = r[:, 256 * hh + 64 * j:256 * hh + 64 * (j + 1)]
        pack[_P_LOSS:_P_LOSS + 1] = loss_in[...]

        sibling = dict(device_id=(xi, yi, 1 - ci), device_id_type=MESH)
        to_sibling = pltpu.make_async_remote_copy(src_ref=pack, dst_ref=pair, send_sem=send_sems.at[N_PEER],
                                                  recv_sem=recv_sems.at[N_PEER], **sibling)
        to_sibling.start()
        to_sibling.wait_recv()
        to_sibling.wait_send()
        pack[...] = pack[...] + pair[...]
        mine = pl.ds(pl.multiple_of(ci * hr, 8), hr)
        theirs = pl.ds(pl.multiple_of((1 - ci) * hr, 8), hr)
        slots[chip] = pack[mine, :]

        def copy(j, slot):
            px, py = _peer_chips(xi, yi)[j]
            return pltpu.make_async_remote_copy(
                src_ref=pack.at[mine], dst_ref=slots.at[slot(2 * px + py)], send_sem=send_sems.at[j],
                recv_sem=recv_sems.at[j], device_id=(px, py, ci), device_id_type=MESH)

        sends = [copy(j, lambda peer_chip: chip) for j in range(N_PEER)]
        for cp in sends:
            cp.start()
        for j in range(N_PEER):
            copy(j, lambda peer_chip: peer_chip).wait_recv()
        for cp in sends:
            cp.wait_send()
        acc = slots[0]
        for d in range(1, N_CHIPS):
            acc = acc + slots[d]
        tot[mine, :] = acc
        half_over = pltpu.make_async_remote_copy(src_ref=tot.at[mine], dst_ref=tot.at[mine], send_sem=send_sems.at[N_PEER + 1],
                                                 recv_sem=recv_sems.at[N_PEER + 1], **sibling)
        half_back = pltpu.make_async_remote_copy(src_ref=tot.at[mine], dst_ref=tot.at[theirs], send_sem=send_sems.at[N_PEER + 1],
                                                 recv_sem=recv_sems.at[N_PEER + 1], **sibling)
        half_over.start()
        half_back.wait_recv()
        half_over.wait_send()
        if job is not None:
            job.finish(j_ins, j_outs, j_scr)

    hbm = pl.BlockSpec(memory_space=pl.ANY)
    res = pl.pallas_call(
        body, name="small_sum", in_specs=[_whole()] * 12 + [hbm] * n_ji, out_specs=[_whole()] + [hbm] * n_jo,
        out_shape=[_sds((_P_ROWS, LANES), F32)] + (job.out_shapes if job is not None else []),
        scratch_shapes=[pltpu.VMEM((_P_ROWS, LANES), F32), pltpu.VMEM((_P_ROWS, LANES), F32),
                        pltpu.VMEM((N_CHIPS, hr, LANES), F32),
                        pltpu.SemaphoreType.DMA((N_PEER + 2,)), pltpu.SemaphoreType.DMA((N_PEER + 2,))]
        + (job.scratch if job is not None else []),
        compiler_params=pltpu.CompilerParams(vmem_limit_bytes=VMEM_LIMIT_BYTES),
    )(*dgrads, *(job.ins if job is not None else []))
    return res[0], list(res[1:])


def _small_adamw(tot, ws, ms, vs):
    n = len(ws)

    def body(*refs):
        tot = refs[0]
        w_refs, m_refs, v_refs = refs[1:1 + n], refs[1 + n:1 + 2 * n], refs[1 + 2 * n:1 + 3 * n]
        loss_out = refs[1 + 3 * n]
        outs = refs[2 + 3 * n:]
        chip = 2 * lax.axis_index("x") + lax.axis_index("y")
        loss_out[...] = tot[_P_LOSS:_P_LOSS + 1, 0:1]

        def step(k, g, pick, put):
            d, m2, v2 = _adam_values(pick(w_refs[k]), pick(m_refs[k]), pick(v_refs[k]), g)
            for o, val in zip((outs[k], outs[n + k], outs[2 * n + k], outs[3 * n + k]), (g, d, m2, v2)):
                put(o, val)

        def whole(ref):
            return ref[0]

        def put_whole(ref, val):
            ref[0] = val

        for g in range(SGU_GROUPS):
            def pick_g(ref, g=g):
                return ref[0, g]

            def put_g(ref, val, g=g):
                ref[0, g] = val

            step(0, tot[_P_WSP + g * SGU_BLOCK:_P_WSP + (g + 1) * SGU_BLOCK], pick_g, put_g)
        step(1, tot[pl.ds(pl.multiple_of(_P_WGU + GLA_RANK * chip, GLA_RANK), GLA_RANK), :], whole, put_whole)
        for k, (base, chunks) in enumerate(((_P_NORM, 8), (_P_NORM + 8, 8), (_P_NORM + 16, 8), (_P_NORM + 24, 8), (_P_BG, 4))):
            for q in range(chunks):
                def pick_q(ref, q=q):
                    return ref[:, LANES * q:LANES * (q + 1)]

                def put_q(ref, val, q=q):
                    ref[:, LANES * q:LANES * (q + 1)] = val

                step(2 + k, tot[base + q:base + q + 1], pick_q, put_q)
        step(7, tot[_P_BSP:_P_BSP + SGU_GROUPS], whole, put_whole)
        for k in range(3):
            mine = tot[pl.ds(pl.multiple_of(_P_HEAD + 32 * k + 8 * chip, 8), 8), :]
            step(8 + k, mine[0:4, 0:64], whole, put_whole)

    shapes = [_sds(w.shape, F32) for w in ws]
    res = pl.pallas_call(
        body, name="small_adamw", in_specs=[_whole()] * (1 + 3 * n), out_specs=[_whole()] * (1 + 4 * n),
        out_shape=[_sds((1, 1), F32)] + shapes * 4,
        compiler_params=pltpu.CompilerParams(vmem_limit_bytes=VMEM_LIMIT_BYTES),
    )(tot, *ws, *ms, *vs)
    return res[0].reshape(()), [list(res[1 + i * n:1 + (i + 1) * n]) for i in range(4)]


def _w_in_pieces():
    blk = D_IN // N_CHIPS
    pieces = []
    for s in range(len(_IN_SPLITS)):
        lo_s, hi_s = _IN_STARTS[s], _IN_STARTS[s + 1]
        for j in range(N_CHIPS):
            lo, hi = max(lo_s, j * blk), min(hi_s, (j + 1) * blk)
            if lo < hi:
                pieces.append((j, lo - j * blk, _IN_DST[s] + lo - lo_s, hi - lo))
    return pieces


def _relayout_w_in(gathered):
    _, rows, blk = gathered.shape
    tr = 256

    def body(g_ref, o_ref):
        o_ref[:, OFF_AL:N_ALL] = jnp.zeros((tr, LANES), BF16)
        for j, src, dst, w in _w_in_pieces():
            o_ref[:, dst:dst + w] = g_ref[j, :, src:src + w]

    res, _ = _call(body, name="relayout_w_in", grid=(rows // tr,), parallel=True,
                   in_specs=[pl.BlockSpec((N_CHIPS, tr, blk), lambda i: (0, i, 0))],
                   out_specs=[pl.BlockSpec((tr, N_ALL), lambda i: (i, 0))],
                   out_shape=[_sds((rows, N_ALL), BF16)], args=(gathered,))
    return res[0]


def _update_row_tile(rows):
    for t in range(min(rows, 256), 7, -8):
        if rows % t == 0:
            return t
    return rows


def _presum_w_in(dws, theirs, row0, rows, name, job=None):
    hr = theirs[0].shape[0]
    blk = D_IN // N_CHIPS
    tr = 64
    assert row0 % tr == 0 and rows % tr == 0
    nh, t0 = hr // tr, row0 // tr
    n = len(dws)

    def body(core_ref, *refs):
        dw_refs, q_refs, (o_ref, s_scr) = refs[:n], refs[n:2 * n], refs[2 * n:]
        for p, (a, off) in enumerate(dws):
            w = a.shape[1]
            s_scr[:, off:off + w] = (dw_refs[p][...] + q_refs[p][...]).astype(BF16)
        for j, src, dst, w in _w_in_pieces():
            o_ref[j, :, src:src + w] = s_scr[:, dst:dst + w]

    in_specs = [pl.BlockSpec((tr, a.shape[1]), lambda i, core: (i + t0 + core[0] * nh, 0)) for a, _ in dws]
    in_specs += [pl.BlockSpec((tr, q.shape[1]), lambda i, core: (i + t0, 0)) for q in theirs]
    res, jres = _call(body, name=name, grid=(rows // tr,), parallel=True, in_specs=in_specs,
                      out_specs=[pl.BlockSpec((N_CHIPS, tr, blk), lambda i, core: (0, i, 0))],
                      out_shape=[_sds((N_CHIPS, rows, blk), BF16)], scratch_shapes=[pltpu.VMEM((tr, N_ALL), BF16)],
                      args=(*[a for a, _ in dws], *theirs), job=job, by_core=True)
    return res[0], jres


def _presum(dw, theirs, name):
    if dw.ndim == 4:
        _, _, hr, c = dw.shape
        tr = _update_row_tile(hr)
        mine = pl.BlockSpec((1, 1, tr, c), lambda j, i, core: (j, core[0], i, 0))
        other = pl.BlockSpec((1, tr, c), lambda j, i, core: (j, i, 0))
    else:
        hr, c = dw.shape[0] // 2, dw.shape[1] // N_CHIPS
        tr = _update_row_tile(hr)
        nh = hr // tr
        mine = pl.BlockSpec((tr, c), lambda j, i, core: (i + core[0] * nh, j))
        other = pl.BlockSpec((tr, c), lambda j, i, core: (i, j))

    def body(core_ref, a_ref, q_ref, o_ref):
        o_ref[...] = (a_ref[...].reshape(tr, c) + q_ref[...].reshape(tr, c)).astype(BF16).reshape(o_ref.shape)

    res, _ = _call(body, name=name, grid=(N_CHIPS, hr // tr), parallel=True, in_specs=[mine, other],
                   out_specs=[pl.BlockSpec((1, tr, c), lambda j, i, core: (j, i, 0))],
                   out_shape=[_sds((N_CHIPS, hr, c), BF16)], args=(dw, theirs), by_core=True)
    return res[0]


def _sum_slots(own, slots, name):
    rows, cols = own.shape
    tr = _update_row_tile(rows)

    def body(own_ref, s_ref, o_ref):
        acc = own_ref[...].astype(F32)
        for j in range(N_PEER):
            acc = acc + s_ref[j].astype(F32)
        o_ref[...] = acc

    res, _ = _call(body, name=name, grid=(rows // tr,), parallel=True,
                   in_specs=[pl.BlockSpec((tr, cols), lambda i: (i, 0)), pl.BlockSpec((N_PEER, tr, cols), lambda i: (0, i, 0))],
                   out_specs=[pl.BlockSpec((tr, cols), lambda i: (i, 0))], out_shape=[_sds((rows, cols), F32)],
                   args=(own, slots))
    return res[0]


def _adamw(w, m, v, g_mine, g_theirs, name, job=None):
    rows, cols = w.shape
    part_rows = [p.shape[0] for p in g_mine]
    assert sum(part_rows) == rows // 2 and [p.shape[0] for p in g_theirs] == part_rows
    tr = _update_row_tile(min(part_rows))
    assert all(r % tr == 0 for r in part_rows)
    nh = (rows // 2) // tr
    starts = [sum(part_rows[:k]) // tr for k in range(len(part_rows))]
    n_parts = len(part_rows)

    def body(core_ref, w_ref, m_ref, v_ref, *rest):
        g_refs, (g_out, d_out, m_out, v_out) = rest[:-4], rest[-4:]
        step = pl.program_id(0)
        mine_here = (step // nh) == core_ref[0]
        q = step % nh
        g = None
        for k in reversed(range(n_parts)):
            val = jnp.where(mine_here, g_refs[k][...], g_refs[n_parts + k][...])
            g = val if g is None else jnp.where(q < starts[k + 1], val, g)
        d, m2, v2 = _adam_values(w_ref[...], m_ref[...], v_ref[...], g)
        g_out[...] = g
        m_out[...] = m2
        v_out[...] = v2
        d_out[...] = d

    def g_spec(k, mine):
        last = part_rows[k] // tr - 1

        def index(i, core):
            half = core[0] if mine else 1 - core[0]
            here = jnp.clip(i % nh - starts[k], 0, last)
            return (jnp.where(i // nh == half, here, jnp.where(i // nh > half, last, 0)), 0)

        return pl.BlockSpec((tr, cols), index)

    spec = pl.BlockSpec((tr, cols), lambda i, core: (i, 0))
    g_specs = [g_spec(k, True) for k in range(n_parts)] + [g_spec(k, False) for k in range(n_parts)]
    return _call(body, name=name, grid=(rows // tr,), parallel=True, in_specs=[spec] * 3 + g_specs,
                 out_specs=[spec] * 4, out_shape=[_sds((rows, cols), F32)] * 4, args=(w, m, v, *g_mine, *g_theirs),
                 job=job, by_core=True)


def _transposed_cast(wt):
    cols, rows = wt.shape

    def body(x_ref, o_ref):
        o_ref[...] = jnp.transpose(x_ref[...]).astype(BF16)

    res, _ = _call(body, name="transpose_w_in", grid=(pl.cdiv(cols, LANES),), parallel=True,
                   in_specs=[pl.BlockSpec((LANES, rows), lambda j: (j, 0))],
                   out_specs=[pl.BlockSpec((rows, LANES), lambda j: (0, j))], out_shape=[_sds((rows, cols), BF16)],
                   args=(wt,))
    return res[0]


def _adamw_transposed(wt, mt, vt, g_mine, g_theirs, name):
    cols, rows = wt.shape
    n_parts = len(g_mine)

    def body(w_ref, m_ref, v_ref, *rest):
        g_refs, (g_out, d_out, m_out, v_out) = rest[:-4], rest[-4:]
        mine = jnp.concatenate([r[...] for r in g_refs[:n_parts]], axis=0)
        theirs = jnp.concatenate([r[...] for r in g_refs[n_parts:]], axis=0)
        first = lax.axis_index("c") == 0
        g = jnp.transpose(jnp.concatenate([jnp.where(first, mine, theirs), jnp.where(first, theirs, mine)], axis=0))
        d, m2, v2 = _adam_values(w_ref[...], m_ref[...], v_ref[...], g)
        g_out[...] = g
        m_out[...] = m2
        v_out[...] = v2
        d_out[...] = d

    spec = pl.BlockSpec((LANES, rows), lambda j: (j, 0))
    g_specs = [pl.BlockSpec((p.shape[0], LANES), lambda j: (0, j)) for p in g_mine] * 2
    res, _ = _call(body, name=name, grid=(pl.cdiv(cols, LANES),), parallel=True, in_specs=[spec] * 3 + g_specs,
                   out_specs=[spec] * 4, out_shape=[_sds((cols, rows), F32)] * 4, args=(wt, mt, vt, *g_mine, *g_theirs))
    return res


def _inproj_fwd(x, g1, w_all, job=None):
    T = x.shape[0]
    tT = _row_tile(T, 512)

    def body(x_ref, g_ref, w_ref, a_ref, proj_ref, alow_ref):
        xv = x_ref[...]
        a = (xv * _rms_stats(xv) * g_ref[...]).astype(BF16)
        a_ref[...] = a
        for j in range(N_MAIN // 1024):
            cols = slice(j * 1024, (j + 1) * 1024)
            proj_ref[:, cols] = _dot(a, w_ref[:, cols]).astype(BF16)
        alow_ref[...] = _dot(a, w_ref[:, N_MAIN:N_ALL])

    row = lambda w: pl.BlockSpec((tT, w), lambda i: (i, 0))
    return _call(
        body, name="inproj_fwd", grid=(T // tT,), parallel=True,
        in_specs=[row(D_MODEL), pl.BlockSpec((1, D_MODEL), lambda i: (0, 0)), _whole()],
        out_specs=[row(D_MODEL), row(N_MAIN), row(LANES)],
        out_shape=[_sds((T, D_MODEL), BF16), _sds((T, N_MAIN), BF16), _sds((T, LANES), F32)],
        args=(x, g1, w_all), job=job)


def _gla_decay_terms(al_ref, wgu_ref, bg_ref, later_ref):
    logit = _dot_bf16(al_ref[...], wgu_ref[...]) + bg_ref[...]
    la = _log_sigmoid(logit) * (1.0 / GLA_TAU)
    delta = _dot_exact_lhs(later_ref[...], la)
    return logit, la, delta


def _gla_fwd(proj, alow, wgu, b_gate, gn, job=None):
    T = proj.shape[0]
    tT = _row_tile(T, 512)
    nc = tT // CHUNK

    def body(q_ref, k_ref, v_ref, r_ref, al_ref, wgu_ref, bg_ref, gn_ref, later_ref, y_ref, st_ref, s_scr):
        @pl.when(pl.program_id(0) == 0)
        def _():
            s_scr[...] = jnp.zeros_like(s_scr)

        _, la, delta = _gla_decay_terms(al_ref, wgu_ref, bg_ref, later_ref)
        kdec = (k_ref[...].astype(F32) * jnp.exp(delta)).astype(BF16)
        heads = range(GLA_HEADS)
        kcs = [slice(h * GLA_DK, (h + 1) * GLA_DK) for h in heads]
        vcs = [slice(h * GLA_DV, (h + 1) * GLA_DV) for h in heads]
        state = [s_scr[h] for h in heads]
        for c in range(nc):
            rows = slice(c * CHUNK, (c + 1) * CHUNK)
            first = slice(c * CHUNK, c * CHUNK + 1)
            dec = jnp.exp(la[first, :] + delta[first, :])
            upd_t = [_dot(v_ref[rows, vcs[h]], kdec[rows, kcs[h]], _TN) for h in heads]
            qs = [(q_ref[rows, kcs[h]].astype(F32) * (GLA_DK ** -0.5)).astype(BF16) for h in heads]
            for h in heads:
                state[h] = state[h] * dec[:, kcs[h]] + upd_t[h]
                st_ref[c, h] = state[h]
            o = [_dot(qs[h], state[h].astype(BF16), _NT) for h in heads]
            for h in heads:
                on = o[h] * _rms_stats(o[h]) * gn_ref[:, vcs[h]]
                rr = r_ref[rows, vcs[h]].astype(F32)
                y_ref[rows, vcs[h]] = (on * (rr * _sigmoid(rr))).astype(BF16)
        for h in heads:
            s_scr[h] = state[h]

    blk = lambda w, j: pl.BlockSpec((tT, w), lambda i: (i, j))
    return _call(
        body, name="gla_fwd", grid=(T // tT,),
        in_specs=[blk(512, 0), blk(512, 1), blk(1024, 1), blk(1024, 2), blk(LANES, 0)] + [_whole()] * 4,
        out_specs=[pl.BlockSpec((tT, GLA_V), lambda i: (i, 0)),
                   pl.BlockSpec((nc, GLA_HEADS, GLA_DV, GLA_DK), lambda i: (i, 0, 0, 0))],
        out_shape=[_sds((T, GLA_V), BF16), _sds((T // CHUNK, GLA_HEADS, GLA_DV, GLA_DK), F32)],
        scratch_shapes=[pltpu.VMEM((GLA_HEADS, GLA_DV, GLA_DK), F32)],
        args=(proj, proj, proj, proj, alow, wgu, b_gate, gn, _chunk_masks(tT, upper=True)), job=job)


def _sgu_mask():
    i = lax.broadcasted_iota(jnp.int32, (SGU_BLOCK, SGU_BLOCK), 0)
    j = lax.broadcasted_iota(jnp.int32, (SGU_BLOCK, SGU_BLOCK), 1)
    return lax.shift_right_logical(j, 6) <= lax.shift_right_logical(i, 6)


def _sgu_merge_fwd(x, proj, y_gla, ln_g, ln_b, w_sp, b_sp_t, w_bg, w_bs, w_o, g_pm, job=None):
    T = x.shape[0]
    tT = _row_tile(T, 512)
    nb = tT // SGU_BLOCK

    def body(x_ref, su_ref, sv_ref, gg_ref, gs_ref, yg_ref, lg_ref, lb_ref, w_ref, b_ref, wbg_ref, wbs_ref, wo_ref,
             g_ref, ys_ref, zg_ref, zs_ref, mg_ref, mix_ref, x1_ref):
        mask = _sgu_mask()
        for g in range(SGU_GROUPS):
            gc = slice(g * SGU_DG, (g + 1) * SGU_DG)
            wm = jnp.where(mask, w_ref[g], 0.0).astype(BF16)
            vf = _gelu(sv_ref[:, gc].astype(F32))
            mu = jnp.mean(vf, axis=-1, keepdims=True)
            vc = vf - mu
            rstd = lax.rsqrt(jnp.mean(vc * vc, axis=-1, keepdims=True) + EPS)
            vn = (vc * rstd * lg_ref[:, gc] + lb_ref[:, gc]).astype(BF16)
            u = _gelu(su_ref[:, gc].astype(F32))
            for b in range(nb):
                rows = slice(b * SGU_BLOCK, (b + 1) * SGU_BLOCK)
                mixed = _dot(wm, vn[rows, :]) + b_ref[:, g:g + 1]
                ys_ref[rows, gc] = (u[rows, :] * mixed).astype(BF16)
        zg = _dot(yg_ref[...], wbg_ref[...])
        zs = _dot(ys_ref[...], wbs_ref[...])
        zg_ref[...] = zg.astype(BF16)
        zs_ref[...] = zs.astype(BF16)
        merged = (_sigmoid(gg_ref[...].astype(F32)) * zg + _sigmoid(gs_ref[...].astype(F32)) * zs).astype(BF16)
        mg_ref[...] = merged
        mix = _dot(merged, wo_ref[...])
        mix_ref[...] = mix.astype(BF16)
        x1_ref[...] = x_ref[...] + mix * _rms_stats(mix) * g_ref[...]

    row = pl.BlockSpec((tT, D_MODEL), lambda i: (i, 0))
    blk = lambda j: pl.BlockSpec((tT, 1024), lambda i: (i, j))
    sds = lambda dt: _sds((T, D_MODEL), dt)
    return _call(body, name="sgu_merge_fwd", grid=(T // tT,), parallel=True,
                 in_specs=[row, blk(3), blk(4), blk(5), blk(6), row] + [_whole()] * 7
                 + [pl.BlockSpec((1, D_MODEL), lambda i: (0, 0))],
                 out_specs=[row] * 6, out_shape=[sds(BF16)] * 5 + [sds(F32)],
                 args=(x, proj, proj, proj, proj, y_gla, ln_g, ln_b, w_sp, b_sp_t, w_bg, w_bs, w_o, g_pm), job=job)


def _ffn_fwd_bwd(x1, tgt, w_fi_top, w_fi_bot, w_fo, g_pf, g_po):
    T = x1.shape[0]
    tT = _row_tile(T, 256)
    half = D_FF // 2
    kh = D_MODEL // 2

    def body(x1_ref, t_ref, top_ref, bot_ref, wfo_ref, gpf_ref, gpo_ref,
             h_ref, f_ref, dgu_ref, dy_ref, dx1_ref, loss_ref, dgpf_ref, dgpo_ref, gu_scr):
        @pl.when(pl.program_id(0) == 0)
        def _():
            loss_ref[...] = jnp.zeros_like(loss_ref)
            dgpf_ref[...] = jnp.zeros_like(dgpf_ref)
            dgpo_ref[...] = jnp.zeros_like(dgpo_ref)

        main = (half // 256) * 256
        pieces = (0, 1, None)

        def w_in_cols(ref, first_slab, p):
            if p is not None:
                return ref[first_slab + p, :, :main]
            return jnp.concatenate([ref[first_slab, :, main:], ref[first_slab + 1, :, main:]], axis=1)

        def w_out_rows(p):
            if p is not None:
                return wfo_ref[p * half:p * half + main, :]
            return jnp.concatenate([wfo_ref[main:half, :], wfo_ref[half + main:2 * half, :]], axis=0)

        def put(ref, base, p, val):
            if p is not None:
                ref[:, base + p * half:base + p * half + main] = val
            else:
                ref[:, base + main:base + half] = val[:, :half - main]
                ref[:, base + half + main:base + 2 * half] = val[:, half - main:]

        def get(ref, base, p):
            if p is not None:
                return ref[:, base + p * half:base + p * half + main]
            return jnp.concatenate([ref[:, base + main:base + half], ref[:, base + half + main:base + 2 * half]], axis=1)

        x1v = x1_ref[...]
        r2 = _rms_stats(x1v)
        h = (x1v * r2 * gpf_ref[...]).astype(BF16)
        h_ref[...] = h
        y = jnp.zeros((tT, D_MODEL), F32)
        for p in pieces:
            gate = _dot(h[:, :kh], w_in_cols(top_ref, 0, p)) + _dot(h[:, kh:], w_in_cols(bot_ref, 0, p))
            up = _dot(h[:, :kh], w_in_cols(top_ref, 2, p)) + _dot(h[:, kh:], w_in_cols(bot_ref, 2, p))
            put(gu_scr, 0, p, gate)
            put(gu_scr, D_FF, p, up)
            f = (gate * _sigmoid(gate) * up).astype(BF16)
            put(f_ref, 0, p, f)
            y = y + _dot(f, w_out_rows(p))
        r3 = _rms_stats(y)
        x2 = x1v + y * r3 * gpo_ref[...]
        err = x2 - t_ref[...]
        loss_ref[...] += jnp.sum(err * err) * (0.5 / D_MODEL)
        dx2 = err * (1.0 / D_MODEL)
        dy, dg = _rms_bwd(dx2, y, r3, gpo_ref[...])
        dgpo_ref[...] += jnp.sum(dg, axis=0, keepdims=True)
        dyb = dy.astype(BF16)
        dy_ref[...] = dyb
        dh_top = jnp.zeros((tT, kh), F32)
        dh_bot = jnp.zeros((tT, kh), F32)
        for p in pieces:
            df = _dot(dyb, w_out_rows(p), _NT)
            gate = get(gu_scr, 0, p)
            up = get(gu_scr, D_FF, p)
            sg = _sigmoid(gate)
            dgate = (df * up * (sg * (1.0 + gate * (1.0 - sg)))).astype(BF16)
            dup = (df * (gate * sg)).astype(BF16)
            put(dgu_ref, 0, p, dgate)
            put(dgu_ref, D_FF, p, dup)
            dh_top = dh_top + _dot(dgate, w_in_cols(top_ref, 0, p), _NT) + _dot(dup, w_in_cols(top_ref, 2, p), _NT)
            dh_bot = dh_bot + _dot(dgate, w_in_cols(bot_ref, 0, p), _NT) + _dot(dup, w_in_cols(bot_ref, 2, p), _NT)
        dh = jnp.concatenate([dh_top, dh_bot], axis=1)
        dx1n, dg2 = _rms_bwd(dh, x1v, r2, gpf_ref[...])
        dgpf_ref[...] += jnp.sum(dg2, axis=0, keepdims=True)
        dx1_ref[...] = dx2 + dx1n

    row = lambda w: pl.BlockSpec((tT, w), lambda i: (i, 0))
    vec = pl.BlockSpec((1, D_MODEL), lambda i: (0, 0))
    res, _ = _call(
        body, name="ffn_fwd_bwd", grid=(T // tT,),
        in_specs=[row(D_MODEL), row(D_MODEL), _whole(), _whole(), _whole(), vec, vec],
        out_specs=[row(D_MODEL), row(D_FF), row(2 * D_FF), row(D_MODEL), row(D_MODEL),
                   pl.BlockSpec((1, LANES), lambda i: (0, 0)), vec, vec],
        out_shape=[_sds((T, D_MODEL), BF16), _sds((T, D_FF), BF16), _sds((T, 2 * D_FF), BF16), _sds((T, D_MODEL), BF16),
                   _sds((T, D_MODEL), F32), _sds((1, LANES), F32), _sds((1, D_MODEL), F32), _sds((1, D_MODEL), F32)],
        scratch_shapes=[pltpu.VMEM((tT, 2 * D_FF), F32)], args=(x1, tgt, w_fi_top, w_fi_bot, w_fo, g_pf, g_po))
    return res


def _merge_sgu_bwd(dx1, mix, proj, zg, zs, w_bg, w_bs, w_o, g_pm, ln_g, ln_b, w_sp, b_sp_t, job=None):
    T = dx1.shape[0]
    tT = _row_tile(T, 256)
    nb = tT // SGU_BLOCK

    def body(dx1_ref, mix_ref, su_ref, sv_ref, gg_ref, gs_ref, zg_ref, zs_ref, wbg_ref, wbs_ref, wo_ref, g_ref,
             lg_ref, lb_ref, w_ref, b_ref,
             dmix_ref, dzg_ref, dzs_ref, dgate_ref, dyg_ref, dp_ref, dgpm_ref, dw_ref, dbt_ref, dlg_ref, dlb_ref):
        @pl.when(pl.program_id(0) == 0)
        def _():
            for ref in (dgpm_ref, dw_ref, dbt_ref, dlg_ref, dlb_ref):
                ref[...] = jnp.zeros_like(ref)

        mix = mix_ref[...].astype(F32)
        dmix, dg = _rms_bwd(dx1_ref[...], mix, _rms_stats(mix), g_ref[...])
        dgpm_ref[...] += jnp.sum(dg, axis=0, keepdims=True)
        dmb = dmix.astype(BF16)
        dmix_ref[...] = dmb
        dmerged = _dot(dmb, wo_ref[...], _NT)
        dys = None
        for k, (gate_ref, z_ref, w_br_ref, dz_ref) in enumerate(((gg_ref, zg_ref, wbg_ref, dzg_ref),
                                                                 (gs_ref, zs_ref, wbs_ref, dzs_ref))):
            sg = _sigmoid(gate_ref[...].astype(F32))
            dz = (dmerged * sg).astype(BF16)
            dz_ref[...] = dz
            dgate_ref[:, k * 1024:(k + 1) * 1024] = (dmerged * z_ref[...].astype(F32) * (sg * (1.0 - sg))).astype(BF16)
            dy_branch = _dot(dz, w_br_ref[...], _NT)
            if k == 0:
                dyg_ref[...] = dy_branch.astype(BF16)
            else:
                dys = dy_branch

        mask = _sgu_mask()
        lane = lax.broadcasted_iota(jnp.int32, (SGU_BLOCK, LANES), 1)
        for g in range(SGU_GROUPS):
            gc = slice(g * SGU_DG, (g + 1) * SGU_DG)
            gc_v = slice(1024 + g * SGU_DG, 1024 + (g + 1) * SGU_DG)
            wm = jnp.where(mask, w_ref[g], 0.0).astype(BF16)
            vf, dvf_dsv = _gelu_and_grad(sv_ref[:, gc].astype(F32))
            mu = jnp.mean(vf, axis=-1, keepdims=True)
            vc = vf - mu
            rstd = lax.rsqrt(jnp.mean(vc * vc, axis=-1, keepdims=True) + EPS)
            vhat = vc * rstd
            vn = (vhat * lg_ref[:, gc] + lb_ref[:, gc]).astype(BF16)
            u, du_dsu = _gelu_and_grad(su_ref[:, gc].astype(F32))
            dy = dys[:, gc]
            dmixed = (dy * u).astype(BF16)
            dvn_parts = []
            dw_acc = jnp.zeros((SGU_BLOCK, SGU_BLOCK), F32)
            db_acc = jnp.zeros((SGU_BLOCK, 1), F32)
            for b in range(nb):
                rows = slice(b * SGU_BLOCK, (b + 1) * SGU_BLOCK)
                mixed = _dot(wm, vn[rows, :]) + b_ref[:, g:g + 1]
                dp_ref[rows, gc] = (dy[rows, :] * mixed * du_dsu[rows, :]).astype(BF16)
                dvn_parts.append(_dot(wm, dmixed[rows, :], _TN))
                dw_acc = dw_acc + _dot(dmixed[rows, :], vn[rows, :], _NT)
                db_acc = db_acc + jnp.sum(dmixed[rows, :].astype(F32), axis=-1, keepdims=True)
            dw_ref[g] += jnp.where(mask, dw_acc, 0.0)
            dbt_ref[...] += jnp.where(lane == g, db_acc, 0.0)
            dvn = jnp.concatenate(dvn_parts, axis=0)
            dlg_ref[:, gc] += jnp.sum(dvn * vhat, axis=0, keepdims=True)
            dlb_ref[:, gc] += jnp.sum(dvn, axis=0, keepdims=True)
            dvh = dvn * lg_ref[:, gc]
            dvf = rstd * (dvh - jnp.mean(dvh, axis=-1, keepdims=True)
                          - vhat * jnp.mean(dvh * vhat, axis=-1, keepdims=True))
            dp_ref[:, gc_v] = (dvf * dvf_dsv).astype(BF16)

    row = pl.BlockSpec((tT, D_MODEL), lambda i: (i, 0))
    blk = lambda j: pl.BlockSpec((tT, 1024), lambda i: (i, j))
    vec = pl.BlockSpec((1, D_MODEL), lambda i: (0, 0))
    wide = lambda w: pl.BlockSpec((tT, w), lambda i: (i, 0))
    sds = _sds((T, D_MODEL), BF16)
    return _call(
        body, name="merge_sgu_bwd", grid=(T // tT,),
        in_specs=[row, row, blk(3), blk(4), blk(5), blk(6), row, row] + [_whole()] * 3 + [vec] + [_whole()] * 4,
        out_specs=[row, row, row, wide(W_MRG), row, wide(W_SGU), vec,
                   pl.BlockSpec((SGU_GROUPS, SGU_BLOCK, SGU_BLOCK), lambda i: (0, 0, 0)),
                   pl.BlockSpec((SGU_BLOCK, LANES), lambda i: (0, 0)), vec, vec],
        out_shape=[sds, sds, sds, _sds((T, W_MRG), BF16), sds, _sds((T, W_SGU), BF16), _sds((1, D_MODEL), F32),
                   _sds((SGU_GROUPS, SGU_BLOCK, SGU_BLOCK), F32), _sds((SGU_BLOCK, LANES), F32),
                   _sds((1, 1024), F32), _sds((1, 1024), F32)],
        args=(dx1, mix, proj, proj, proj, proj, zg, zs, w_bg, w_bs, w_o, g_pm, ln_g, ln_b, w_sp, b_sp_t), job=job)


def _gla_bwd(proj, alow, wgu, b_gate, gn, states, dy_gla, job=None):
    T = proj.shape[0]
    tT = _row_tile(T, 512)
    nc = tT // CHUNK
    nt = T // tT

    def body(q_ref, k_ref, v_ref, r_ref, al_ref, wgu_ref, bg_ref, gn_ref, later_ref, earlier_ref, st_ref, sp_ref, dy_ref,
             dp_ref, dal_ref, dgn_ref, dbg_ref, dwgu_ref, g_scr, dd_scr, dt_scr):
        step = pl.program_id(0)

        @pl.when(step == 0)
        def _():
            g_scr[...] = jnp.zeros_like(g_scr)
            dgn_ref[...] = jnp.zeros_like(dgn_ref)
            dbg_ref[...] = jnp.zeros_like(dbg_ref)
            dwgu_ref[...] = jnp.zeros_like(dwgu_ref)

        has_prev = jnp.where(step == nt - 1, 0.0, 1.0)
        logit, la, delta = _gla_decay_terms(al_ref, wgu_ref, bg_ref, later_ref)
        e = jnp.exp(delta)
        kdec_f = k_ref[...].astype(F32) * e
        kdec = kdec_f.astype(BF16)
        heads = range(GLA_HEADS)
        kcs = [slice(h * GLA_DK, (h + 1) * GLA_DK) for h in heads]
        vcs = [slice(h * GLA_DV, (h + 1) * GLA_DV) for h in heads]
        carry = [g_scr[h] for h in heads]
        dgn_acc = [jnp.zeros((1, GLA_DV), F32) for _ in heads]
        for c in reversed(range(nc)):
            rows = slice(c * CHUNK, (c + 1) * CHUNK)
            first = slice(c * CHUNK, c * CHUNK + 1)
            dec = jnp.exp(la[first, :] + delta[first, :])
            s_b = [st_ref[c, h].astype(BF16) for h in heads]
            qs = [(q_ref[rows, kcs[h]].astype(F32) * (GLA_DK ** -0.5)).astype(BF16) for h in heads]
            o = [_dot(qs[h], s_b[h], _NT) for h in heads]
            do = []
            for h in heads:
                rstd = _rms_stats(o[h])
                ohat = o[h] * rstd
                gnh = gn_ref[:, vcs[h]]
                dy = dy_ref[rows, vcs[h]].astype(F32)
                rr = r_ref[rows, vcs[h]].astype(F32)
                sg = _sigmoid(rr)
                don = dy * (rr * sg)
                dp_ref[rows, OFF_R + h * GLA_DV:OFF_R + (h + 1) * GLA_DV] = (
                    dy * (ohat * gnh) * (sg * (1.0 + rr * (1.0 - sg)))).astype(BF16)
                dgn_acc[h] = dgn_acc[h] + jnp.sum(don * ohat, axis=0, keepdims=True)
                dn = don * gnh
                do.append((rstd * (dn - ohat * jnp.mean(dn * ohat, axis=-1, keepdims=True))).astype(BF16))
            dq = [_dot(do[h], s_b[h]) for h in heads]
            g_t = [_dot(do[h], qs[h], _TN) + carry[h] for h in heads]
            g_b = [g_t[h].astype(BF16) for h in heads]
            dv = [_dot(kdec[rows, kcs[h]], g_b[h], _NT) for h in heads]
            dkdec = [_dot(v_ref[rows, vcs[h]], g_b[h]) for h in heads]
            for h in heads:
                s_prev = st_ref[c - 1, h] if c > 0 else sp_ref[0, h] * has_prev
                ddec = jnp.sum(g_t[h] * s_prev, axis=0, keepdims=True)
                carry[h] = g_t[h] * dec[:, kcs[h]]
                dp_ref[rows, OFF_Q + h * GLA_DK:OFF_Q + (h + 1) * GLA_DK] = (dq[h] * (GLA_DK ** -0.5)).astype(BF16)
                dp_ref[rows, OFF_V + h * GLA_DV:OFF_V + (h + 1) * GLA_DV] = dv[h].astype(BF16)
                dp_ref[rows, OFF_K + h * GLA_DK:OFF_K + (h + 1) * GLA_DK] = (dkdec[h] * e[rows, kcs[h]]).astype(BF16)
                dd_scr[rows, kcs[h]] = dkdec[h] * kdec_f[rows, kcs[h]]
                dt_scr[rows, kcs[h]] = jnp.broadcast_to(ddec * dec[:, kcs[h]], (CHUNK, GLA_DK))
        for h in heads:
            g_scr[h] = carry[h]
            dgn_ref[:, vcs[h]] += dgn_acc[h]
        dla = _dot_exact_lhs(earlier_ref[...], dd_scr[...]) + dt_scr[...]
        dlogit = dla * (1.0 / GLA_TAU) * _sigmoid(-logit)
        dbg_ref[...] += jnp.sum(dlogit, axis=0, keepdims=True)
        dwgu_ref[...] += _dot_bf16(al_ref[...], dlogit, _TN)
        dal_ref[...] = _dot_bf16(dlogit, wgu_ref[...], _NT).astype(BF16)

    rev = lambda i: nt - 1 - i
    blk = lambda w, j: pl.BlockSpec((tT, w), lambda i: (rev(i), j))
    st_blk = pl.BlockSpec((nc, GLA_HEADS, GLA_DV, GLA_DK), lambda i: (rev(i), 0, 0, 0))
    sp_blk = pl.BlockSpec((1, GLA_HEADS, GLA_DV, GLA_DK), lambda i: (jnp.maximum(rev(i) * nc - 1, 0), 0, 0, 0))
    return _call(
        body, name="gla_bwd", grid=(nt,),
        in_specs=[blk(512, 0), blk(512, 1), blk(1024, 1), blk(1024, 2), blk(LANES, 0)] + [_whole()] * 5
        + [st_blk, sp_blk, blk(GLA_V, 0)],
        out_specs=[blk(W_GLA, 0), blk(LANES, 0), pl.BlockSpec((1, GLA_V), lambda i: (0, 0)),
                   pl.BlockSpec((1, GLA_QK), lambda i: (0, 0)), pl.BlockSpec((LANES, GLA_QK), lambda i: (0, 0))],
        out_shape=[_sds((T, W_GLA), BF16), _sds((T, LANES), BF16), _sds((1, GLA_V), F32), _sds((1, GLA_QK), F32),
                   _sds((LANES, GLA_QK), F32)],
        scratch_shapes=[pltpu.VMEM((GLA_HEADS, GLA_DV, GLA_DK), F32), pltpu.VMEM((tT, GLA_QK), F32),
                        pltpu.VMEM((tT, GLA_QK), F32)],
        args=(proj, proj, proj, proj, alow, wgu, b_gate, gn, _chunk_masks(tT, upper=True), _chunk_masks(tT, upper=False),
              states, states, dy_gla), job=job)


def _inproj_bwd(x, dx1, g1, w_all, dparts, job=None):
    T = x.shape[0]
    tT = _row_tile(T, 512)
    offs = (0, W_GLA, W_GLA + W_SGU, N_MAIN)

    def body(x_ref, dx1_ref, g_ref, w_ref, *rest):
        part_refs, (dx_ref, dg_ref) = rest[:len(offs)], rest[len(offs):]

        @pl.when(pl.program_id(0) == 0)
        def _():
            dg_ref[...] = jnp.zeros_like(dg_ref)

        da = jnp.zeros((tT, D_MODEL), F32)
        for off, p_ref in zip(offs, part_refs):
            da = da + _dot(p_ref[...], w_ref[:, off:off + p_ref.shape[1]], _NT)
        xv = x_ref[...]
        dx, dg = _rms_bwd(da, xv, _rms_stats(xv), g_ref[...])
        dg_ref[...] += jnp.sum(dg, axis=0, keepdims=True)
        dx_ref[...] = dx1_ref[...] + dx

    row = lambda w: pl.BlockSpec((tT, w), lambda i: (i, 0))
    vec = pl.BlockSpec((1, D_MODEL), lambda i: (0, 0))
    return _call(
        body, name="inproj_bwd", grid=(T // tT,),
        in_specs=[row(D_MODEL), row(D_MODEL), vec, _whole()] + [row(p.shape[1]) for p in dparts],
        out_specs=[row(D_MODEL), vec], out_shape=[_sds((T, D_MODEL), F32), _sds((1, D_MODEL), F32)],
        args=(x, dx1, g1, w_all, *dparts), job=job)


def _tn_matmul(a, b, name, job=None):
    T, M = a.shape
    N = b.shape[1]
    tk = _row_tile(T, 1024)
    tm = M if M <= 1024 else 1408
    tn = N if N <= 3072 else N // 2
    assert M % tm == 0 and N % tn == 0

    def body(a_ref, b_ref, o_ref):
        @pl.when(pl.program_id(2) == 0)
        def _():
            o_ref[...] = _dot(a_ref[...], b_ref[...], _TN)

        @pl.when(pl.program_id(2) > 0)
        def _():
            o_ref[...] += _dot(a_ref[...], b_ref[...], _TN)

    res, jres = _call(
        body, name=name, grid=(M // tm, N // tn, T // tk),
        in_specs=[pl.BlockSpec((tk, tm), lambda i, j, k: (k, i)), pl.BlockSpec((tk, tn), lambda i, j, k: (k, j))],
        out_specs=[pl.BlockSpec((tm, tn), lambda i, j, k: (i, j))], out_shape=[_sds((M, N), F32)], args=(a, b), job=job)
    return res[0], jres


def _pad_rows(a, rows=8):
    return jnp.pad(a, ((0, rows - a.shape[0]), (0, LANES - a.shape[1])))


def _halves_view(dw):
    r = dw.shape[0] // N_CHIPS
    return dw.reshape(N_CHIPS, 2, r // 2, dw.shape[1])


def kernel(x, norm_pre_mix, w_in, w_gate_up, b_gate, gla_norm, sgu_ln_g, sgu_ln_b, w_spatial, b_spatial, w_branch_gla, w_branch_sgu, w_out, norm_post_mix, norm_pre_ffn, w_ffn_in, w_ffn_out, norm_post_ffn, loss_target, m_norm_pre_mix, m_w_in, m_w_gate_up, m_b_gate, m_gla_norm, m_sgu_ln_g, m_sgu_ln_b, m_w_spatial, m_b_spatial, m_w_branch_gla, m_w_branch_sgu, m_w_out, m_norm_post_mix, m_norm_pre_ffn, m_w_ffn_in, m_w_ffn_out, m_norm_post_ffn, v_norm_pre_mix, v_w_in, v_w_gate_up, v_b_gate, v_gla_norm, v_sgu_ln_g, v_sgu_ln_b, v_w_spatial, v_b_spatial, v_w_branch_gla, v_w_branch_sgu, v_w_out, v_norm_post_mix, v_norm_pre_ffn, v_w_ffn_in, v_w_ffn_out, v_norm_post_ffn):
    chip = 2 * lax.axis_index("x") + lax.axis_index("y")
    xt, tgt = x[0], loss_target[0]

    tiny = jnp.concatenate([w_gate_up[0], _pad_rows(gla_norm[0]), _pad_rows(sgu_ln_g[0]), _pad_rows(sgu_ln_b[0]),
                            jnp.zeros((24, LANES), F32)], axis=0)

    def with_own(gathered, own):
        return lax.dynamic_update_slice(gathered, own[None], (chip, 0, 0))

    w_in_t, m_in_t, v_in_t = w_in[0].T, m_w_in[0].T, v_w_in[0].T
    w_in_b = _transposed_cast(w_in_t)
    g_in, g_tiny = _run_job(_job_gather([w_in_b, tiny]), "gather_w_in")
    g_tiny = with_own(g_tiny, tiny)
    w_all = _relayout_w_in(with_own(g_in, w_in_b))
    cols = lambda a: a.transpose(1, 0, 2).reshape(a.shape[1], N_CHIPS * a.shape[2])
    wgu = jnp.pad(cols(g_tiny[:, 0:16]), ((0, LANES - GLA_RANK), (0, 0)))
    gn = cols(g_tiny[:, 16:20, :64]).reshape(1, GLA_V)
    ln_g = cols(g_tiny[:, 24:28, :64]).reshape(1, 1024)
    ln_b = cols(g_tiny[:, 32:36, :64]).reshape(1, 1024)
    b_sp_t = jnp.pad(b_spatial[0].T, ((0, 0), (0, LANES - SGU_GROUPS)))
    w_sp = w_spatial[0]

    own_rows = [w_branch_gla[0].astype(BF16), w_branch_sgu[0].astype(BF16), w_out[0].astype(BF16), w_ffn_out[0].astype(BF16)]
    (a, proj, alow), g_rows = _inproj_fwd(xt, norm_pre_mix, w_all, job=_job_gather(own_rows))
    rows = lambda g: g.reshape(N_CHIPS * g.shape[1], g.shape[2])
    w_bg, w_bs, w_o, w_fo = [rows(with_own(g, own)) for g, own in zip(g_rows, own_rows)]
    w_fi_b = w_ffn_in[0].astype(BF16)
    fi_top, fi_bot = w_fi_b[:D_MODEL // 2], w_fi_b[D_MODEL // 2:]
    (y_gla, states), (g_top,) = _gla_fwd(proj, alow, wgu, b_gate, gn, job=_job_gather([fi_top]))
    (y_sgu, zg, zs, merged, mix, x1), (g_bot,) = _sgu_merge_fwd(
        xt, proj, y_gla, ln_g, ln_b, w_sp, b_sp_t, w_bg, w_bs, w_o, norm_post_mix, job=_job_gather([fi_bot]))
    h, f, dgu, dy, dx1, loss, d_gpf, d_gpo = _ffn_fwd_bwd(x1, tgt, with_own(g_top, fi_top), with_own(g_bot, fi_bot),
                                                          w_fo, norm_pre_ffn, norm_post_ffn)

    own_part = lambda c: lax.dynamic_index_in_dim(c, chip, 0, keepdims=False)
    whole = lambda hs: [[(h_, None)] for h_ in hs]
    dw_fo, _ = _tn_matmul(f, dy, "dw_ffn_out")
    dw_fo4 = _halves_view(dw_fo)
    dw_fi, (q_fo,) = _tn_matmul(h, dgu, "dw_ffn_in", job=_job_to_other_core([[(dw_fo4, 0)]]))
    c_fo = _presum(dw_fo4, q_fo, "presum_ffn_out")
    (dmix, dzg, dzs, dp_mrg, dyg, dp_sgu, d_gpm, d_wsp, d_bsp_t, d_lng, d_lnb), (s_fo, q_fi) = _merge_sgu_bwd(
        dx1, mix, proj, zg, zs, w_bg, w_bs, w_o, norm_post_mix, ln_g, ln_b, w_sp, b_sp_t,
        job=_join(_job_scatter([c_fo]), _job_to_other_core([[(dw_fi, 0)]])))
    c_fi = _presum(dw_fi, q_fi, "presum_ffn_in")
    dw_c, _ = _tn_matmul(a, dp_mrg, "dw_in_merge")
    dw_b, _ = _tn_matmul(a, dp_sgu, "dw_in_sgu")
    dw_o4 = _halves_view(_tn_matmul(merged, dmix, "dw_out")[0])
    dw_bg4 = _halves_view(_tn_matmul(y_gla, dzg, "dw_branch_gla")[0])
    dw_bs4 = _halves_view(_tn_matmul(y_sgu, dzs, "dw_branch_sgu")[0])
    h_fo = _sum_slots(own_part(c_fo), s_fo, "sum_ffn_out")
    (dp_gla, dal, d_gn, d_bg, d_wgu), (s_fi, t_fo, q_b, q_c, q_o, q_bg, q_bs) = _gla_bwd(
        proj, alow, wgu, b_gate, gn, states, dyg,
        job=_join(_job_scatter([c_fi]), _job_to_other_core(
            whole([h_fo]) + [[(dw_b, 0)], [(dw_c, 0)], [(dw_o4, 0)], [(dw_bg4, 0)], [(dw_bs4, 0)]])))
    c_o, c_bg, c_bs = (_presum(dw_o4, q_o, "presum_out"), _presum(dw_bg4, q_bg, "presum_branch_gla"),
                       _presum(dw_bs4, q_bs, "presum_branch_sgu"))
    h_fi = _sum_slots(own_part(c_fi), s_fi, "sum_ffn_in")
    dw_d, _ = _tn_matmul(a, dal, "dw_in_gate")
    dw_a, (s_o, s_bg, s_bs, t_fi, q_d) = _tn_matmul(
        a, dp_gla, "dw_in_gla",
        job=_join(_job_scatter([c_o, c_bg, c_bs]), _job_to_other_core(whole([h_fi]) + [[(dw_d, 0)]])))
    h_o, h_bg, h_bs = (_sum_slots(own_part(c_o), s_o, "sum_out"), _sum_slots(own_part(c_bg), s_bg, "sum_branch_gla"),
                       _sum_slots(own_part(c_bs), s_bs, "sum_branch_sgu"))

    grads, deltas, new_m, new_v = {}, {}, {}, {}

    def update(name, w, m, v, g_mine, g_theirs, job=None):
        (g, d, m2, v2), jres = _adamw(w[0], m[0], v[0], g_mine, g_theirs, "adamw_" + name, job=job)
        grads[name], deltas[name], new_m[name], new_v[name] = g[None], d[None], m2[None], v2[None]
        return jres

    dw_in = [(dw_a, 0), (dw_b, W_GLA), (dw_c, W_GLA + W_SGU), (dw_d, N_MAIN)]
    q_a, t_o, t_bg, t_bs = update("w_ffn_out", w_ffn_out, m_w_ffn_out, v_w_ffn_out, [h_fo], [t_fo],
                                  job=_job_to_other_core([[(dw_a, 0)]] + whole([h_o, h_bg, h_bs])))
    q_in = [q_a, q_b, q_c, q_d]
    hr_in = D_MODEL // 2
    r_a, r_c = hr_in // 8, 2 * hr_in // 8
    c_in_a, _ = _presum_w_in(dw_in, q_in, 0, r_a, "presum_w_in_a")
    c_in_c, _ = _presum_w_in(dw_in, q_in, r_a, r_c, "presum_w_in_c")
    c_in_b, (s_in_a,) = _presum_w_in(dw_in, q_in, r_a + r_c, hr_in - r_a - r_c, "presum_w_in_b",
                                     job=_job_scatter([c_in_a]))
    update("w_ffn_in", w_ffn_in, m_w_ffn_in, v_w_ffn_in, [h_fi], [t_fi])
    update("w_out", w_out, m_w_out, v_w_out, [h_o], [t_o])
    update("w_branch_gla", w_branch_gla, m_w_branch_gla, v_w_branch_gla, [h_bg], [t_bg])
    update("w_branch_sgu", w_branch_sgu, m_w_branch_sgu, v_w_branch_sgu, [h_bs], [t_bs])
    (grad_x, d_g1), (s_in_b,) = _inproj_bwd(xt, dx1, norm_pre_mix, w_all, (dp_gla, dp_sgu, dp_mrg, dal),
                                            job=_job_scatter([c_in_b]))
    small_tot, (s_in_c,) = _small_sum([d_wsp, d_wgu, d_g1, d_gpm, d_gpf, d_gpo, d_bg, d_bsp_t, d_gn, d_lng, d_lnb, loss],
                                      job=_job_scatter([c_in_c]))
    h_in = [_sum_slots(own_part(c_in_a), s_in_a, "sum_w_in_a"), _sum_slots(own_part(c_in_c), s_in_c, "sum_w_in_c"),
            _sum_slots(own_part(c_in_b), s_in_b, "sum_w_in_b")]
    t_in = _run_job(_job_to_other_core(whole(h_in)), "swap_w_in")
    for store, val in zip((grads, deltas, new_m, new_v),
                          _adamw_transposed(w_in_t, m_in_t, v_in_t, h_in, t_in, "adamw_w_in")):
        store["w_in"] = val.T[None]

    small_names = ["w_spatial", "w_gate_up", "norm_pre_mix", "norm_post_mix", "norm_pre_ffn", "norm_post_ffn", "b_gate",
                   "b_spatial", "gla_norm", "sgu_ln_g", "sgu_ln_b"]
    loss_out, small = _small_adamw(
        small_tot,
        [w_spatial, w_gate_up, norm_pre_mix, norm_post_mix, norm_pre_ffn, norm_post_ffn, b_gate, b_spatial, gla_norm,
         sgu_ln_g, sgu_ln_b],
        [m_w_spatial, m_w_gate_up, m_norm_pre_mix, m_norm_post_mix, m_norm_pre_ffn, m_norm_post_ffn, m_b_gate,
         m_b_spatial, m_gla_norm, m_sgu_ln_g, m_sgu_ln_b],
        [v_w_spatial, v_w_gate_up, v_norm_pre_mix, v_norm_post_mix, v_norm_pre_ffn, v_norm_post_ffn, v_b_gate,
         v_b_spatial, v_gla_norm, v_sgu_ln_g, v_sgu_ln_b])
    for store, vals in zip((grads, deltas, new_m, new_v), small):
        store.update(zip(small_names, vals))

    order = ["norm_pre_mix", "w_in", "w_gate_up", "b_gate", "gla_norm", "sgu_ln_g", "sgu_ln_b", "w_spatial", "b_spatial",
             "w_branch_gla", "w_branch_sgu", "w_out", "norm_post_mix", "norm_pre_ffn", "w_ffn_in", "w_ffn_out",
             "norm_post_ffn"]
    out = [loss_out, grad_x[None]]
    for store in (grads, deltas, new_m, new_v):
        out.extend(store[n] for n in order)
    return tuple(out)
```

```python
import jax
import jax.numpy as jnp
from jax import lax
from jax.experimental import pallas as pl
from jax.experimental.pallas import tpu as pltpu

F32 = jnp.float32
BF16 = jnp.bfloat16

D_MODEL = 1024
GLA_HEADS = 4
GLA_DK = 128
GLA_DV = 256
GLA_QK = GLA_HEADS * GLA_DK
GLA_V = GLA_HEADS * GLA_DV
GLA_RANK = 16
GLA_TAU = 16.0
CHUNK = 64
SGU_GROUPS = 4
SGU_BLOCK = 128
SGU_DG = 256
D_FF = 2816
EPS = 1e-6
LANES = 128

OFF_Q, OFF_K, OFF_V, OFF_R, OFF_SU, OFF_SV, OFF_GG, OFF_GS, OFF_AL = 0, 512, 1024, 2048, 3072, 4096, 5120, 6144, 7168
W_GLA, W_SGU, W_MRG = 3072, 2048, 2048
N_MAIN = 7168
N_ALL = N_MAIN + LANES
_IN_SPLITS = (GLA_QK, GLA_QK, GLA_V, GLA_V, GLA_RANK, 1024, 1024, 1024, 1024)
_IN_STARTS = tuple(sum(_IN_SPLITS[:i]) for i in range(len(_IN_SPLITS) + 1))
_IN_DST = (OFF_Q, OFF_K, OFF_V, OFF_R, OFF_AL, OFF_SU, OFF_SV, OFF_GG, OFF_GS)
D_IN = _IN_STARTS[-1]

ADAM_LR = 0.001
ADAM_B1 = 0.9
ADAM_B2 = 0.999
ADAM_EPS = 1e-08
ADAM_WD = 0.01
ADAM_STEP = 10

VMEM_LIMIT_BYTES = 56 * 1024 * 1024
N_CHIPS = 4
N_PEER = N_CHIPS - 1
N_DEV = 8
MESH = pl.DeviceIdType.MESH

_NN = (((1,), (0,)), ((), ()))
_NT = (((1,), (1,)), ((), ()))
_TN = (((0,), (0,)), ((), ()))


def _dot(a, b, dims=_NN):
    return lax.dot_general(a, b, dims, preferred_element_type=F32)


def _split(x):
    hi = x.astype(BF16)
    lo = (x - hi.astype(F32)).astype(BF16)
    return hi, lo


def _dot_bf16(a, b, dims=_NN):
    return _dot(a.astype(BF16), b.astype(BF16), dims)


def _dot_exact_lhs(m, x):
    xh, xl = _split(x)
    return _dot(m, xh) + _dot(m, xl)


def _sigmoid(x):
    return 0.5 * jnp.tanh(0.5 * x) + 0.5


def _log_sigmoid(x):
    return jnp.minimum(x, 0.0) - jnp.log(1.0 + jnp.exp(-jnp.abs(x)))


_GELU_C = 0.7978845608028654
_GELU_A = 0.044715


def _gelu_and_grad(x):
    x2 = x * x
    t = jnp.tanh(_GELU_C * (x + _GELU_A * x * x2))
    g = 0.5 * x * (1.0 + t)
    dg = 0.5 * (1.0 + t) + 0.5 * x * (1.0 - t * t) * (_GELU_C * (1.0 + 3.0 * _GELU_A * x2))
    return g, dg


def _gelu(x):
    t = jnp.tanh(_GELU_C * (x + _GELU_A * x * x * x))
    return 0.5 * x * (1.0 + t)


def _rms_stats(x):
    return lax.rsqrt(jnp.mean(x * x, axis=-1, keepdims=True) + EPS)


def _rms_bwd(dout, y, r, g):
    yhat = y * r
    dn = dout * g
    dy = r * (dn - yhat * jnp.mean(dn * yhat, axis=-1, keepdims=True))
    return dy, dout * yhat


def _whole():
    return pl.BlockSpec(memory_space=pltpu.VMEM)


def _row_tile(T, want):
    t = min(T, want)
    assert T % t == 0
    return t


def _chunk_masks(tT, upper):
    row = lax.broadcasted_iota(jnp.int32, (tT, tT), 0)
    col = lax.broadcasted_iota(jnp.int32, (tT, tT), 1)
    same = (row // CHUNK) == (col // CHUNK)
    tri = (col > row) if upper else (col < row)
    return jnp.where(same & tri, 1.0, 0.0).astype(BF16)


class _Job:
    def __init__(self, ins, out_shapes, scratch, start, finish, mid=None):
        self.ins, self.out_shapes, self.scratch = list(ins), list(out_shapes), list(scratch)
        self.start, self.finish, self.mid = start, finish, mid


def _join(*jobs):
    def split(refs, counts):
        out, at = [], 0
        for n in counts:
            out.append(refs[at:at + n])
            at += n
        return out

    ni, no, ns = [len(j.ins) for j in jobs], [len(j.out_shapes) for j in jobs], [len(j.scratch) for j in jobs]

    def start(ins, outs, scr):
        for j, a, b, c in zip(jobs, split(ins, ni), split(outs, no), split(scr, ns)):
            j.start(a, b, c)

    def finish(ins, outs, scr):
        for j, a, b, c in zip(jobs, split(ins, ni), split(outs, no), split(scr, ns)):
            j.finish(a, b, c)

    def mid(ins, outs, scr):
        for j, a, b, c in zip(jobs, split(ins, ni), split(outs, no), split(scr, ns)):
            if j.mid is not None:
                j.mid(a, b, c)

    return _Job(sum((j.ins for j in jobs), []), sum((j.out_shapes for j in jobs), []),
                sum((j.scratch for j in jobs), []), start, finish, mid if any(j.mid for j in jobs) else None)


def _mesh_pos():
    return lax.axis_index("x"), lax.axis_index("y"), lax.axis_index("c")


def _peer_chips(xi, yi):
    return [(1 - xi, yi), (xi, 1 - yi), (1 - xi, 1 - yi)]


def _half(ci, rows):
    return pl.ds(pl.multiple_of(ci * rows, 8), rows)


def _sds(shape, dtype):
    return jax.ShapeDtypeStruct(tuple(shape), dtype)


def _job_gather(arrs):
    n = len(arrs)
    kinds = 12
    Y0, Y1, X1, X0, ON_X, ON_Y, D2D = 0, 1, 2, 3, 4, 5, 6

    def copies(ins, outs, scr):
        send_sems, recv_sems = scr
        xi, yi, ci = _mesh_pos()
        me, cx, cy, cd = 2 * xi + yi, 2 * (1 - xi) + yi, 2 * xi + (1 - yi), 2 * (1 - xi) + (1 - yi)
        to_x, to_y, to_core = (1 - xi, yi, ci), (xi, 1 - yi, ci), (xi, yi, 1 - ci)
        table = []
        for k in range(n):
            qr = arrs[k].shape[0] // 4

            def rows(core, q):
                return pl.ds(pl.multiple_of((2 * core + q) * qr, 8), qr)

            def cp(kind, src, dst, to):
                s = k * kinds + kind
                return pltpu.make_async_remote_copy(src_ref=src, dst_ref=dst, send_sem=send_sems.at[s],
                                                    recv_sem=recv_sems.at[s], device_id=to, device_id_type=MESH)

            def slab(chip, core, q):
                return outs[k].at[chip, rows(core, q)]

            t = {}
            for kind, q, to, frm in ((Y0, 0, to_y, cy), (Y1, 1, to_y, cy), (X1, 1, to_x, cx), (X0, 0, to_x, cx)):
                mine = ins[k].at[rows(ci, q)]
                t[kind] = (cp(kind, mine, slab(me, ci, q), to), cp(kind, mine, slab(frm, ci, q), to))
            t[ON_X] = (cp(ON_X, slab(cy, ci, 0), slab(cy, ci, 0), to_x), cp(ON_X, slab(cy, ci, 0), slab(cd, ci, 0), to_x))
            t[ON_Y] = (cp(ON_Y, slab(cx, ci, 1), slab(cx, ci, 1), to_y), cp(ON_Y, slab(cx, ci, 1), slab(cd, ci, 1), to_y))
            for i, (chip, q) in enumerate(((cy, 0), (cy, 1), (cx, 1), (cx, 0), (cd, 0), (cd, 1))):
                t[D2D + i] = (cp(D2D + i, slab(chip, ci, q), slab(chip, ci, q), to_core),
                              cp(D2D + i, slab(chip, ci, q), slab(chip, 1 - ci, q), to_core))
            table.append(t)
        return table

    def start(ins, outs, scr):
        table = copies(ins, outs, scr)
        for kind in (Y0, X1, Y1, X0):
            for t in table:
                t[kind][0].start()

    def arrived(table, kind, then):
        for t in table:
            t[kind][1].wait_recv()
            for nxt in then:
                t[nxt][0].start()

    def mid(ins, outs, scr):
        table = copies(ins, outs, scr)
        arrived(table, Y0, (ON_X, D2D + 0))
        arrived(table, X1, (ON_Y, D2D + 2))

    def finish(ins, outs, scr):
        table = copies(ins, outs, scr)
        arrived(table, Y1, (D2D + 1,))
        arrived(table, X0, (D2D + 3,))
        arrived(table, ON_X, (D2D + 4,))
        arrived(table, ON_Y, (D2D + 5,))
        for t in table:
            for i in range(6):
                t[D2D + i][1].wait_recv()
            for kind in range(kinds):
                t[kind][0].wait_send()

    dma = pltpu.SemaphoreType.DMA
    return _Job(arrs, [_sds((N_CHIPS,) + a.shape, a.dtype) for a in arrs], [dma((n * kinds,))] * 2, start, finish, mid)


def _job_scatter(parts):
    n = len(parts)

    def copies(ins, outs, scr):
        send_sems, recv_sems = scr
        xi, yi, ci = _mesh_pos()
        res = []
        for k in range(n):
            for j, (px, py) in enumerate(_peer_chips(xi, yi)):
                s = k * N_PEER + j
                res.append(pltpu.make_async_remote_copy(
                    src_ref=ins[k].at[2 * px + py], dst_ref=outs[k].at[j], send_sem=send_sems.at[s],
                    recv_sem=recv_sems.at[s], device_id=(px, py, ci), device_id_type=MESH))
        return res

    def start(ins, outs, scr):
        for cp in copies(ins, outs, scr):
            cp.start()

    def finish(ins, outs, scr):
        for cp in copies(ins, outs, scr):
            cp.wait_recv()
            cp.wait_send()

    dma = pltpu.SemaphoreType.DMA
    return _Job(parts, [_sds((N_PEER,) + p.shape[1:], p.dtype) for p in parts], [dma((n * N_PEER,))] * 2, start, finish)


def _job_to_other_core(groups):
    pieces = [(g, a, off) for g, group in enumerate(groups) for a, off in group]
    n = len(pieces)

    def geometry(group):
        a0, off0 = group[0]
        if off0 is None:
            return a0.shape
        if a0.ndim == 4:
            return (N_CHIPS, a0.shape[2], a0.shape[3])
        return (a0.shape[0] // 2, sum(a.shape[1] for a, _ in group))

    def copies(ins, outs, scr):
        send_sems, recv_sems = scr
        xi, yi, ci = _mesh_pos()
        res = []
        for p, (g, a, off) in enumerate(pieces):
            if off is None:
                give, land = ins[p], outs[g]
            elif a.ndim == 4:
                give, land = ins[p].at[pl.ds(0, N_CHIPS), 1 - ci], outs[g]
            else:
                hr, w = a.shape[0] // 2, a.shape[1]
                give, land = ins[p].at[_half(1 - ci, hr)], outs[g].at[pl.ds(0, hr), pl.ds(off, w)]
            res.append(pltpu.make_async_remote_copy(
                src_ref=give, dst_ref=land, send_sem=send_sems.at[p], recv_sem=recv_sems.at[p],
                device_id=(xi, yi, 1 - ci), device_id_type=MESH))
        return res

    def start(ins, outs, scr):
        for cp in copies(ins, outs, scr):
            cp.start()

    def finish(ins, outs, scr):
        for cp in copies(ins, outs, scr):
            cp.wait_recv()
            cp.wait_send()

    dma = pltpu.SemaphoreType.DMA
    return _Job([a for _, a, _ in pieces], [_sds(geometry(group), group[0][0].dtype) for group in groups],
                [dma((n,))] * 2, start, finish)


def _call(body, *, name, grid, in_specs, out_specs, out_shape, args, scratch_shapes=(), parallel=False, job=None,
          by_core=False):
    n_in, n_out, n_scr = len(in_specs), len(out_specs), len(scratch_shapes)
    hbm = pl.BlockSpec(memory_space=pl.ANY)
    n_ji, n_jo = (len(job.ins), len(job.out_shapes)) if job is not None else (0, 0)
    lead = 1 if by_core else 0

    def kernel_fn(*refs):
        core, refs = refs[:lead], refs[lead:]
        ins, refs = refs[:n_in], refs[n_in:]
        j_ins, refs = refs[:n_ji], refs[n_ji:]
        outs, refs = refs[:n_out], refs[n_out:]
        j_outs, refs = refs[:n_jo], refs[n_jo:]
        scr, j_scr = refs[:n_scr], refs[n_scr:]
        if job is None:
            body(*core, *ins, *outs, *scr)
            return
        ids = [pl.program_id(d) for d in range(len(grid))]
        first = ids[0] == 0
        last = ids[0] == grid[0] - 1
        for d in range(1, len(grid)):
            first = first & (ids[d] == 0)
            last = last & (ids[d] == grid[d] - 1)

        @pl.when(first)
        def _():
            job.start(j_ins, j_outs, j_scr)

        if job.mid is not None and grid[0] >= 4:
            half_way = ids[0] == grid[0] // 2
            for d in range(1, len(grid)):
                half_way = half_way & (ids[d] == 0)

            @pl.when(half_way)
            def _():
                job.mid(j_ins, j_outs, j_scr)

        body(*core, *ins, *outs, *scr)

        @pl.when(last)
        def _():
            if job.mid is not None and grid[0] < 4:
                job.mid(j_ins, j_outs, j_scr)
            job.finish(j_ins, j_outs, j_scr)

    sem = ("parallel" if parallel and job is None else "arbitrary",) * len(grid)
    all_in = list(in_specs) + [hbm] * n_ji
    all_out = list(out_specs) + [hbm] * n_jo
    all_scratch = list(scratch_shapes) + (job.scratch if job is not None else [])
    all_shapes = list(out_shape) + (job.out_shapes if job is not None else [])
    all_args = list(args) + (job.ins if job is not None else [])
    params = pltpu.CompilerParams(dimension_semantics=sem, vmem_limit_bytes=VMEM_LIMIT_BYTES)
    if by_core:
        spec = pltpu.PrefetchScalarGridSpec(num_scalar_prefetch=1, grid=grid, in_specs=all_in, out_specs=all_out,
                                            scratch_shapes=all_scratch)
        core = lax.axis_index("c").astype(jnp.int32).reshape(1)
        res = pl.pallas_call(kernel_fn, name=name, grid_spec=spec, out_shape=all_shapes, compiler_params=params)(
            core, *all_args)
    else:
        res = pl.pallas_call(kernel_fn, name=name, grid=grid, in_specs=all_in, out_specs=all_out, out_shape=all_shapes,
                             scratch_shapes=all_scratch, compiler_params=params)(*all_args)
    return list(res[:n_out]), list(res[n_out:])


def _run_job(job, name):
    n_i, n_o = len(job.ins), len(job.out_shapes)

    def body(*refs):
        ins, outs, scr = refs[:n_i], refs[n_i:n_i + n_o], refs[n_i + n_o:]
        job.start(ins, outs, scr)
        if job.mid is not None:
            job.mid(ins, outs, scr)
        job.finish(ins, outs, scr)

    hbm = pl.BlockSpec(memory_space=pl.ANY)
    return list(pl.pallas_call(body, name=name, in_specs=[hbm] * n_i, out_specs=[hbm] * n_o, out_shape=job.out_shapes,
                               scratch_shapes=job.scratch)(*job.ins))


def _adam_values(w, m, v, g):
    m2 = ADAM_B1 * m + (1.0 - ADAM_B1) * g
    v2 = ADAM_B2 * v + (1.0 - ADAM_B2) * (g * g)
    delta = -ADAM_LR * ((m2 / (1.0 - ADAM_B1 ** ADAM_STEP)) / (jnp.sqrt(v2 / (1.0 - ADAM_B2 ** ADAM_STEP)) + ADAM_EPS)
                        + ADAM_WD * w)
    return delta, m2, v2


_P_WSP, _P_WGU, _P_NORM, _P_BG, _P_BSP, _P_HEAD, _P_LOSS, _P_ROWS = 0, 512, 576, 608, 616, 624, 720, 736


def _small_sum(dgrads):
    hr = _P_ROWS // 2

    def body(dwsp, dwgu, dg1, dgpm, dgpf, dgpo, dbg, dbspt, dgn, dlng, dlnb, loss_in, tot, pack, pair, slots, send_sems,
             recv_sems):
        xi, yi, ci = _mesh_pos()
        chip = 2 * xi + yi

        pack[...] = jnp.zeros_like(pack)
        for g in range(SGU_GROUPS):
            pack[_P_WSP + g * SGU_BLOCK:_P_WSP + (g + 1) * SGU_BLOCK] = dwsp[g]
        for j in range(N_CHIPS):
            pack[_P_WGU + GLA_RANK * j:_P_WGU + GLA_RANK * (j + 1)] = dwgu[0:GLA_RANK, LANES * j:LANES * (j + 1)]
        for k, r in enumerate((dg1, dgpm, dgpf, dgpo)):
            for q in range(8):
                pack[_P_NORM + 8 * k + q:_P_NORM + 8 * k + q + 1] = r[:, LANES * q:LANES * (q + 1)]
        for q in range(4):
            pack[_P_BG + q:_P_BG + q + 1] = dbg[:, LANES * q:LANES * (q + 1)]
        pack[_P_BSP:_P_BSP + SGU_GROUPS] = jnp.transpose(dbspt[...])[0:SGU_GROUPS]
        for k, r in enumerate((dgn, dlng, dlnb)):
            for j in range(N_CHIPS):
                for hh in range(4):
                    row = _P_HEAD + 32 * k + 8 * j + hh
                    pack[row:row + 1, 0:64] = r[:, 256 * hh + 64 * j:256 * hh + 64 * (j + 1)]
        pack[_P_LOSS:_P_LOSS + 1] = loss_in[...]

        sibling = dict(device_id=(xi, yi, 1 - ci), device_id_type=MESH)
        to_sibling = pltpu.make_async_remote_copy(src_ref=pack, dst_ref=pair, send_sem=send_sems.at[N_PEER],
                                                  recv_sem=recv_sems.at[N_PEER], **sibling)
        to_sibling.start()
        to_sibling.wait_recv()
        to_sibling.wait_send()
        pack[...] = pack[...] + pair[...]
        mine = pl.ds(pl.multiple_of(ci * hr, 8), hr)
        theirs = pl.ds(pl.multiple_of((1 - ci) * hr, 8), hr)
        slots[chip] = pack[mine, :]

        def copy(j, slot):
            px, py = _peer_chips(xi, yi)[j]
            return pltpu.make_async_remote_copy(
                src_ref=pack.at[mine], dst_ref=slots.at[slot(2 * px + py)], send_sem=send_sems.at[j],
                recv_sem=recv_sems.at[j], device_id=(px, py, ci), device_id_type=MESH)

        sends = [copy(j, lambda peer_chip: chip) for j in range(N_PEER)]
        for cp in sends:
            cp.start()
        for j in range(N_PEER):
            copy(j, lambda peer_chip: peer_chip).wait_recv()
        for cp in sends:
            cp.wait_send()
        acc = slots[0]
        for d in range(1, N_CHIPS):
            acc = acc + slots[d]
        tot[mine, :] = acc
        half_over = pltpu.make_async_remote_copy(src_ref=tot.at[mine], dst_ref=tot.at[mine], send_sem=send_sems.at[N_PEER + 1],
                                                 recv_sem=recv_sems.at[N_PEER + 1], **sibling)
        half_back = pltpu.make_async_remote_copy(src_ref=tot.at[mine], dst_ref=tot.at[theirs], send_sem=send_sems.at[N_PEER + 1],
                                                 recv_sem=recv_sems.at[N_PEER + 1], **sibling)
        half_over.start()
        half_back.wait_recv()
        half_over.wait_send()

    return pl.pallas_call(
        body, name="small_sum", in_specs=[_whole()] * 12, out_specs=_whole(), out_shape=_sds((_P_ROWS, LANES), F32),
        scratch_shapes=[pltpu.VMEM((_P_ROWS, LANES), F32), pltpu.VMEM((_P_ROWS, LANES), F32),
                        pltpu.VMEM((N_CHIPS, hr, LANES), F32),
                        pltpu.SemaphoreType.DMA((N_PEER + 2,)), pltpu.SemaphoreType.DMA((N_PEER + 2,))],
        compiler_params=pltpu.CompilerParams(vmem_limit_bytes=VMEM_LIMIT_BYTES),
    )(*dgrads)


def _small_adamw(tot, ws, ms, vs):
    n = len(ws)

    def body(*refs):
        tot = refs[0]
        w_refs, m_refs, v_refs = refs[1:1 + n], refs[1 + n:1 + 2 * n], refs[1 + 2 * n:1 + 3 * n]
        loss_out = refs[1 + 3 * n]
        outs = refs[2 + 3 * n:]
        chip = 2 * lax.axis_index("x") + lax.axis_index("y")
        loss_out[...] = tot[_P_LOSS:_P_LOSS + 1, 0:1]

        def step(k, g, pick, put):
            d, m2, v2 = _adam_values(pick(w_refs[k]), pick(m_refs[k]), pick(v_refs[k]), g)
            for o, val in zip((outs[k], outs[n + k], outs[2 * n + k], outs[3 * n + k]), (g, d, m2, v2)):
                put(o, val)

        def whole(ref):
            return ref[0]

        def put_whole(ref, val):
            ref[0] = val

        for g in range(SGU_GROUPS):
            def pick_g(ref, g=g):
                return ref[0, g]

            def put_g(ref, val, g=g):
                ref[0, g] = val

            step(0, tot[_P_WSP + g * SGU_BLOCK:_P_WSP + (g + 1) * SGU_BLOCK], pick_g, put_g)
        step(1, tot[pl.ds(pl.multiple_of(_P_WGU + GLA_RANK * chip, GLA_RANK), GLA_RANK), :], whole, put_whole)
        for k, (base, chunks) in enumerate(((_P_NORM, 8), (_P_NORM + 8, 8), (_P_NORM + 16, 8), (_P_NORM + 24, 8), (_P_BG, 4))):
            for q in range(chunks):
                def pick_q(ref, q=q):
                    return ref[:, LANES * q:LANES * (q + 1)]

                def put_q(ref, val, q=q):
                    ref[:, LANES * q:LANES * (q + 1)] = val

                step(2 + k, tot[base + q:base + q + 1], pick_q, put_q)
        step(7, tot[_P_BSP:_P_BSP + SGU_GROUPS], whole, put_whole)
        for k in range(3):
            mine = tot[pl.ds(pl.multiple_of(_P_HEAD + 32 * k + 8 * chip, 8), 8), :]
            step(8 + k, mine[0:4, 0:64], whole, put_whole)

    shapes = [_sds(w.shape, F32) for w in ws]
    res = pl.pallas_call(
        body, name="small_adamw", in_specs=[_whole()] * (1 + 3 * n), out_specs=[_whole()] * (1 + 4 * n),
        out_shape=[_sds((1, 1), F32)] + shapes * 4,
        compiler_params=pltpu.CompilerParams(vmem_limit_bytes=VMEM_LIMIT_BYTES),
    )(tot, *ws, *ms, *vs)
    return res[0].reshape(()), [list(res[1 + i * n:1 + (i + 1) * n]) for i in range(4)]


def _w_in_pieces():
    blk = D_IN // N_CHIPS
    pieces = []
    for s in range(len(_IN_SPLITS)):
        lo_s, hi_s = _IN_STARTS[s], _IN_STARTS[s + 1]
        for j in range(N_CHIPS):
            lo, hi = max(lo_s, j * blk), min(hi_s, (j + 1) * blk)
            if lo < hi:
                pieces.append((j, lo - j * blk, _IN_DST[s] + lo - lo_s, hi - lo))
    return pieces


def _relayout_w_in(gathered):
    _, rows, blk = gathered.shape
    tr = 256

    def body(g_ref, o_ref):
        o_ref[:, OFF_AL:N_ALL] = jnp.zeros((tr, LANES), BF16)
        for j, src, dst, w in _w_in_pieces():
            o_ref[:, dst:dst + w] = g_ref[j, :, src:src + w]

    res, _ = _call(body, name="relayout_w_in", grid=(rows // tr,), parallel=True,
                   in_specs=[pl.BlockSpec((N_CHIPS, tr, blk), lambda i: (0, i, 0))],
                   out_specs=[pl.BlockSpec((tr, N_ALL), lambda i: (i, 0))],
                   out_shape=[_sds((rows, N_ALL), BF16)], args=(gathered,))
    return res[0]


def _update_row_tile(rows):
    for t in range(min(rows, 256), 7, -8):
        if rows % t == 0:
            return t
    return rows


def _presum_w_in(dws, theirs, row0, rows, name, job=None):
    hr = theirs[0].shape[0]
    blk = D_IN // N_CHIPS
    tr = 64
    assert row0 % tr == 0 and rows % tr == 0
    nh, t0 = hr // tr, row0 // tr
    n = len(dws)

    def body(core_ref, *refs):
        dw_refs, q_refs, (o_ref, s_scr) = refs[:n], refs[n:2 * n], refs[2 * n:]
        for p, (a, off) in enumerate(dws):
            w = a.shape[1]
            s_scr[:, off:off + w] = (dw_refs[p][...] + q_refs[p][...]).astype(BF16)
        for j, src, dst, w in _w_in_pieces():
            o_ref[j, :, src:src + w] = s_scr[:, dst:dst + w]

    in_specs = [pl.BlockSpec((tr, a.shape[1]), lambda i, core: (i + t0 + core[0] * nh, 0)) for a, _ in dws]
    in_specs += [pl.BlockSpec((tr, q.shape[1]), lambda i, core: (i + t0, 0)) for q in theirs]
    res, jres = _call(body, name=name, grid=(rows // tr,), parallel=True, in_specs=in_specs,
                      out_specs=[pl.BlockSpec((N_CHIPS, tr, blk), lambda i, core: (0, i, 0))],
                      out_shape=[_sds((N_CHIPS, rows, blk), BF16)], scratch_shapes=[pltpu.VMEM((tr, N_ALL), BF16)],
                      args=(*[a for a, _ in dws], *theirs), job=job, by_core=True)
    return res[0], jres


def _presum(dw, theirs, name):
    if dw.ndim == 4:
        _, _, hr, c = dw.shape
        tr = _update_row_tile(hr)
        mine = pl.BlockSpec((1, 1, tr, c), lambda j, i, core: (j, core[0], i, 0))
        other = pl.BlockSpec((1, tr, c), lambda j, i, core: (j, i, 0))
    else:
        hr, c = dw.shape[0] // 2, dw.shape[1] // N_CHIPS
        tr = _update_row_tile(hr)
        nh = hr // tr
        mine = pl.BlockSpec((tr, c), lambda j, i, core: (i + core[0] * nh, j))
        other = pl.BlockSpec((tr, c), lambda j, i, core: (i, j))

    def body(core_ref, a_ref, q_ref, o_ref):
        o_ref[...] = (a_ref[...].reshape(tr, c) + q_ref[...].reshape(tr, c)).astype(BF16).reshape(o_ref.shape)

    res, _ = _call(body, name=name, grid=(N_CHIPS, hr // tr), parallel=True, in_specs=[mine, other],
                   out_specs=[pl.BlockSpec((1, tr, c), lambda j, i, core: (j, i, 0))],
                   out_shape=[_sds((N_CHIPS, hr, c), BF16)], args=(dw, theirs), by_core=True)
    return res[0]


def _sum_slots(own, slots, name):
    rows, cols = own.shape
    tr = _update_row_tile(rows)

    def body(own_ref, s_ref, o_ref):
        acc = own_ref[...].astype(F32)
        for j in range(N_PEER):
            acc = acc + s_ref[j].astype(F32)
        o_ref[...] = acc

    res, _ = _call(body, name=name, grid=(rows // tr,), parallel=True,
                   in_specs=[pl.BlockSpec((tr, cols), lambda i: (i, 0)), pl.BlockSpec((N_PEER, tr, cols), lambda i: (0, i, 0))],
                   out_specs=[pl.BlockSpec((tr, cols), lambda i: (i, 0))], out_shape=[_sds((rows, cols), F32)],
                   args=(own, slots))
    return res[0]


def _adamw(w, m, v, g_mine, g_theirs, name, job=None):
    rows, cols = w.shape
    part_rows = [p.shape[0] for p in g_mine]
    assert sum(part_rows) == rows // 2 and [p.shape[0] for p in g_theirs] == part_rows
    tr = _update_row_tile(min(part_rows))
    assert all(r % tr == 0 for r in part_rows)
    nh = (rows // 2) // tr
    starts = [sum(part_rows[:k]) // tr for k in range(len(part_rows))]
    n_parts = len(part_rows)

    def body(core_ref, w_ref, m_ref, v_ref, *rest):
        g_refs, (g_out, d_out, m_out, v_out) = rest[:-4], rest[-4:]
        step = pl.program_id(0)
        mine_here = (step // nh) == core_ref[0]
        q = step % nh
        g = None
        for k in reversed(range(n_parts)):
            val = jnp.where(mine_here, g_refs[k][...], g_refs[n_parts + k][...])
            g = val if g is None else jnp.where(q < starts[k + 1], val, g)
        d, m2, v2 = _adam_values(w_ref[...], m_ref[...], v_ref[...], g)
        g_out[...] = g
        m_out[...] = m2
        v_out[...] = v2
        d_out[...] = d

    def g_spec(k, mine):
        last = part_rows[k] // tr - 1

        def index(i, core):
            half = core[0] if mine else 1 - core[0]
            here = jnp.clip(i % nh - starts[k], 0, last)
            return (jnp.where(i // nh == half, here, jnp.where(i // nh > half, last, 0)), 0)

        return pl.BlockSpec((tr, cols), index)

    spec = pl.BlockSpec((tr, cols), lambda i, core: (i, 0))
    g_specs = [g_spec(k, True) for k in range(n_parts)] + [g_spec(k, False) for k in range(n_parts)]
    return _call(body, name=name, grid=(rows // tr,), parallel=True, in_specs=[spec] * 3 + g_specs,
                 out_specs=[spec] * 4, out_shape=[_sds((rows, cols), F32)] * 4, args=(w, m, v, *g_mine, *g_theirs),
                 job=job, by_core=True)


def _transposed_cast(wt):
    cols, rows = wt.shape

    def body(x_ref, o_ref):
        o_ref[...] = jnp.transpose(x_ref[...]).astype(BF16)

    res, _ = _call(body, name="transpose_w_in", grid=(pl.cdiv(cols, LANES),), parallel=True,
                   in_specs=[pl.BlockSpec((LANES, rows), lambda j: (j, 0))],
                   out_specs=[pl.BlockSpec((rows, LANES), lambda j: (0, j))], out_shape=[_sds((rows, cols), BF16)],
                   args=(wt,))
    return res[0]


def _cast_weights(ws, w_fi, job=None):
    steps = 4
    cols = w_fi.shape[1]
    tile = w_fi.shape[0] // (2 * steps)

    def body(*refs):
        for i_ref, o_ref in zip(refs[:len(refs) // 2], refs[len(refs) // 2:]):
            o_ref[...] = i_ref[...].astype(BF16)

    row_specs = [pl.BlockSpec((w.shape[0] // steps, w.shape[1]), lambda i: (i, 0)) for w in ws]
    half_spec = pl.BlockSpec((tile, cols), lambda i: (i, 0))
    return _call(body, name="cast_weights", grid=(steps,), parallel=True,
                 in_specs=row_specs + [pl.BlockSpec((None, tile, cols), lambda i, k=k: (k, i, 0)) for k in range(2)],
                 out_specs=row_specs + [half_spec, half_spec],
                 out_shape=[_sds(w.shape, BF16) for w in ws] + [_sds((steps * tile, cols), BF16)] * 2,
                 args=(*ws, w_fi.reshape(2, steps * tile, cols), w_fi.reshape(2, steps * tile, cols)), job=job)


def _adamw_transposed(wt, mt, vt, g_mine, g_theirs, name):
    cols, rows = wt.shape
    n_parts = len(g_mine)

    def body(w_ref, m_ref, v_ref, *rest):
        g_refs, (g_out, d_out, m_out, v_out) = rest[:-4], rest[-4:]
        mine = jnp.concatenate([r[...] for r in g_refs[:n_parts]], axis=0)
        theirs = jnp.concatenate([r[...] for r in g_refs[n_parts:]], axis=0)
        first = lax.axis_index("c") == 0
        g = jnp.transpose(jnp.concatenate([jnp.where(first, mine, theirs), jnp.where(first, theirs, mine)], axis=0))
        d, m2, v2 = _adam_values(w_ref[...], m_ref[...], v_ref[...], g)
        g_out[...] = g
        m_out[...] = m2
        v_out[...] = v2
        d_out[...] = d

    spec = pl.BlockSpec((LANES, rows), lambda j: (j, 0))
    g_specs = [pl.BlockSpec((p.shape[0], LANES), lambda j: (0, j)) for p in g_mine] * 2
    res, _ = _call(body, name=name, grid=(pl.cdiv(cols, LANES),), parallel=True, in_specs=[spec] * 3 + g_specs,
                   out_specs=[spec] * 4, out_shape=[_sds((cols, rows), F32)] * 4, args=(wt, mt, vt, *g_mine, *g_theirs))
    return res


def _inproj_fwd(x, g1, w_all, job=None):
    T = x.shape[0]
    tT = _row_tile(T, 512)

    def body(x_ref, g_ref, w_ref, a_ref, proj_ref, alow_ref):
        xv = x_ref[...]
        a = (xv * _rms_stats(xv) * g_ref[...]).astype(BF16)
        a_ref[...] = a
        for j in range(N_MAIN // 1024):
            cols = slice(j * 1024, (j + 1) * 1024)
            proj_ref[:, cols] = _dot(a, w_ref[:, cols]).astype(BF16)
        alow_ref[...] = _dot(a, w_ref[:, N_MAIN:N_ALL])

    row = lambda w: pl.BlockSpec((tT, w), lambda i: (i, 0))
    return _call(
        body, name="inproj_fwd", grid=(T // tT,), parallel=True,
        in_specs=[row(D_MODEL), pl.BlockSpec((1, D_MODEL), lambda i: (0, 0)), _whole()],
        out_specs=[row(D_MODEL), row(N_MAIN), row(LANES)],
        out_shape=[_sds((T, D_MODEL), BF16), _sds((T, N_MAIN), BF16), _sds((T, LANES), F32)],
        args=(x, g1, w_all), job=job)


def _gla_decay_terms(al_ref, wgu_ref, bg_ref, later_ref):
    logit = _dot_bf16(al_ref[...], wgu_ref[...]) + bg_ref[...]
    la = _log_sigmoid(logit) * (1.0 / GLA_TAU)
    delta = _dot_exact_lhs(later_ref[...], la)
    return logit, la, delta


def _gla_fwd(proj, alow, wgu, b_gate, gn, job=None):
    T = proj.shape[0]
    tT = _row_tile(T, 512)
    nc = tT // CHUNK

    def body(q_ref, k_ref, v_ref, r_ref, al_ref, wgu_ref, bg_ref, gn_ref, later_ref, y_ref, st_ref, s_scr):
        @pl.when(pl.program_id(0) == 0)
        def _():
            s_scr[...] = jnp.zeros_like(s_scr)

        _, la, delta = _gla_decay_terms(al_ref, wgu_ref, bg_ref, later_ref)
        kdec = (k_ref[...].astype(F32) * jnp.exp(delta)).astype(BF16)
        heads = range(GLA_HEADS)
        kcs = [slice(h * GLA_DK, (h + 1) * GLA_DK) for h in heads]
        vcs = [slice(h * GLA_DV, (h + 1) * GLA_DV) for h in heads]
        state = [s_scr[h] for h in heads]
        for c in range(nc):
            rows = slice(c * CHUNK, (c + 1) * CHUNK)
            first = slice(c * CHUNK, c * CHUNK + 1)
            dec = jnp.exp(la[first, :] + delta[first, :])
            upd_t = [_dot(v_ref[rows, vcs[h]], kdec[rows, kcs[h]], _TN) for h in heads]
            qs = [(q_ref[rows, kcs[h]].astype(F32) * (GLA_DK ** -0.5)).astype(BF16) for h in heads]
            for h in heads:
                state[h] = state[h] * dec[:, kcs[h]] + upd_t[h]
                st_ref[c, h] = state[h]
            o = [_dot(qs[h], state[h].astype(BF16), _NT) for h in heads]
            for h in heads:
                on = o[h] * _rms_stats(o[h]) * gn_ref[:, vcs[h]]
                rr = r_ref[rows, vcs[h]].astype(F32)
                y_ref[rows, vcs[h]] = (on * (rr * _sigmoid(rr))).astype(BF16)
        for h in heads:
            s_scr[h] = state[h]

    blk = lambda w, j: pl.BlockSpec((tT, w), lambda i: (i, j))
    return _call(
        body, name="gla_fwd", grid=(T // tT,),
        in_specs=[blk(512, 0), blk(512, 1), blk(1024, 1), blk(1024, 2), blk(LANES, 0)] + [_whole()] * 4,
        out_specs=[pl.BlockSpec((tT, GLA_V), lambda i: (i, 0)),
                   pl.BlockSpec((nc, GLA_HEADS, GLA_DV, GLA_DK), lambda i: (i, 0, 0, 0))],
        out_shape=[_sds((T, GLA_V), BF16), _sds((T // CHUNK, GLA_HEADS, GLA_DV, GLA_DK), F32)],
        scratch_shapes=[pltpu.VMEM((GLA_HEADS, GLA_DV, GLA_DK), F32)],
        args=(proj, proj, proj, proj, alow, wgu, b_gate, gn, _chunk_masks(tT, upper=True)), job=job)


def _sgu_mask():
    i = lax.broadcasted_iota(jnp.int32, (SGU_BLOCK, SGU_BLOCK), 0)
    j = lax.broadcasted_iota(jnp.int32, (SGU_BLOCK, SGU_BLOCK), 1)
    return lax.shift_right_logical(j, 6) <= lax.shift_right_logical(i, 6)


def _sgu_merge_fwd(x, proj, y_gla, ln_g, ln_b, w_sp, b_sp_t, w_bg, w_bs, w_o, g_pm, job=None):
    T = x.shape[0]
    tT = _row_tile(T, 512)
    nb = tT // SGU_BLOCK

    def body(x_ref, su_ref, sv_ref, gg_ref, gs_ref, yg_ref, lg_ref, lb_ref, w_ref, b_ref, wbg_ref, wbs_ref, wo_ref,
             g_ref, ys_ref, zg_ref, zs_ref, mg_ref, mix_ref, x1_ref):
        mask = _sgu_mask()
        for g in range(SGU_GROUPS):
            gc = slice(g * SGU_DG, (g + 1) * SGU_DG)
            wm = jnp.where(mask, w_ref[g], 0.0).astype(BF16)
            vf = _gelu(sv_ref[:, gc].astype(F32))
            mu = jnp.mean(vf, axis=-1, keepdims=True)
            vc = vf - mu
            rstd = lax.rsqrt(jnp.mean(vc * vc, axis=-1, keepdims=True) + EPS)
            vn = (vc * rstd * lg_ref[:, gc] + lb_ref[:, gc]).astype(BF16)
            u = _gelu(su_ref[:, gc].astype(F32))
            for b in range(nb):
                rows = slice(b * SGU_BLOCK, (b + 1) * SGU_BLOCK)
                mixed = _dot(wm, vn[rows, :]) + b_ref[:, g:g + 1]
                ys_ref[rows, gc] = (u[rows, :] * mixed).astype(BF16)
        zg = _dot(yg_ref[...], wbg_ref[...])
        zs = _dot(ys_ref[...], wbs_ref[...])
        zg_ref[...] = zg.astype(BF16)
        zs_ref[...] = zs.astype(BF16)
        merged = (_sigmoid(gg_ref[...].astype(F32)) * zg + _sigmoid(gs_ref[...].astype(F32)) * zs).astype(BF16)
        mg_ref[...] = merged
        mix = _dot(merged, wo_ref[...])
        mix_ref[...] = mix.astype(BF16)
        x1_ref[...] = x_ref[...] + mix * _rms_stats(mix) * g_ref[...]

    row = pl.BlockSpec((tT, D_MODEL), lambda i: (i, 0))
    blk = lambda j: pl.BlockSpec((tT, 1024), lambda i: (i, j))
    sds = lambda dt: _sds((T, D_MODEL), dt)
    return _call(body, name="sgu_merge_fwd", grid=(T // tT,), parallel=True,
                 in_specs=[row, blk(3), blk(4), blk(5), blk(6), row] + [_whole()] * 7
                 + [pl.BlockSpec((1, D_MODEL), lambda i: (0, 0))],
                 out_specs=[row] * 6, out_shape=[sds(BF16)] * 5 + [sds(F32)],
                 args=(x, proj, proj, proj, proj, y_gla, ln_g, ln_b, w_sp, b_sp_t, w_bg, w_bs, w_o, g_pm), job=job)


def _ffn_fwd_bwd(x1, tgt, w_fi_top, w_fi_bot, w_fo, g_pf, g_po):
    T = x1.shape[0]
    tT = _row_tile(T, 256)
    half = D_FF // 2
    kh = D_MODEL // 2

    def body(x1_ref, t_ref, top_ref, bot_ref, wfo_ref, gpf_ref, gpo_ref,
             h_ref, f_ref, dgu_ref, dy_ref, dx1_ref, loss_ref, dgpf_ref, dgpo_ref, gu_scr):
        @pl.when(pl.program_id(0) == 0)
        def _():
            loss_ref[...] = jnp.zeros_like(loss_ref)
            dgpf_ref[...] = jnp.zeros_like(dgpf_ref)
            dgpo_ref[...] = jnp.zeros_like(dgpo_ref)

        main = (half // 256) * 256
        pieces = (0, 1, None)

        def w_in_cols(ref, first_slab, p):
            if p is not None:
                return ref[first_slab + p, :, :main]
            return jnp.concatenate([ref[first_slab, :, main:], ref[first_slab + 1, :, main:]], axis=1)

        def w_out_rows(p):
            if p is not None:
                return wfo_ref[p * half:p * half + main, :]
            return jnp.concatenate([wfo_ref[main:half, :], wfo_ref[half + main:2 * half, :]], axis=0)

        def put(ref, base, p, val):
            if p is not None:
                ref[:, base + p * half:base + p * half + main] = val
            else:
                ref[:, base + main:base + half] = val[:, :half - main]
                ref[:, base + half + main:base + 2 * half] = val[:, half - main:]

        def get(ref, base, p):
            if p is not None:
                return ref[:, base + p * half:base + p * half + main]
            return jnp.concatenate([ref[:, base + main:base + half], ref[:, base + half + main:base + 2 * half]], axis=1)

        x1v = x1_ref[...]
        r2 = _rms_stats(x1v)
        h = (x1v * r2 * gpf_ref[...]).astype(BF16)
        h_ref[...] = h
        y = jnp.zeros((tT, D_MODEL), F32)
        for p in pieces:
            gate = _dot(h[:, :kh], w_in_cols(top_ref, 0, p)) + _dot(h[:, kh:], w_in_cols(bot_ref, 0, p))
            up = _dot(h[:, :kh], w_in_cols(top_ref, 2, p)) + _dot(h[:, kh:], w_in_cols(bot_ref, 2, p))
            put(gu_scr, 0, p, gate)
            put(gu_scr, D_FF, p, up)
            f = (gate * _sigmoid(gate) * up).astype(BF16)
            put(f_ref, 0, p, f)
            y = y + _dot(f, w_out_rows(p))
        r3 = _rms_stats(y)
        x2 = x1v + y * r3 * gpo_ref[...]
        err = x2 - t_ref[...]
        loss_ref[...] += jnp.sum(err * err) * (0.5 / D_MODEL)
        dx2 = err * (1.0 / D_MODEL)
        dy, dg = _rms_bwd(dx2, y, r3, gpo_ref[...])
        dgpo_ref[...] += jnp.sum(dg, axis=0, keepdims=True)
        dyb = dy.astype(BF16)
        dy_ref[...] = dyb
        dh_top = jnp.zeros((tT, kh), F32)
        dh_bot = jnp.zeros((tT, kh), F32)
        for p in pieces:
            df = _dot(dyb, w_out_rows(p), _NT)
            gate = get(gu_scr, 0, p)
            up = get(gu_scr, D_FF, p)
            sg = _sigmoid(gate)
            dgate = (df * up * (sg * (1.0 + gate * (1.0 - sg)))).astype(BF16)
            dup = (df * (gate * sg)).astype(BF16)
            put(dgu_ref, 0, p, dgate)
            put(dgu_ref, D_FF, p, dup)
            dh_top = dh_top + _dot(dgate, w_in_cols(top_ref, 0, p), _NT) + _dot(dup, w_in_cols(top_ref, 2, p), _NT)
            dh_bot = dh_bot + _dot(dgate, w_in_cols(bot_ref, 0, p), _NT) + _dot(dup, w_in_cols(bot_ref, 2, p), _NT)
        dh = jnp.concatenate([dh_top, dh_bot], axis=1)
        dx1n, dg2 = _rms_bwd(dh, x1v, r2, gpf_ref[...])
        dgpf_ref[...] += jnp.sum(dg2, axis=0, keepdims=True)
        dx1_ref[...] = dx2 + dx1n

    row = lambda w: pl.BlockSpec((tT, w), lambda i: (i, 0))
    vec = pl.BlockSpec((1, D_MODEL), lambda i: (0, 0))
    res, _ = _call(
        body, name="ffn_fwd_bwd", grid=(T // tT,),
        in_specs=[row(D_MODEL), row(D_MODEL), _whole(), _whole(), _whole(), vec, vec],
        out_specs=[row(D_MODEL), row(D_FF), row(2 * D_FF), row(D_MODEL), row(D_MODEL),
                   pl.BlockSpec((1, LANES), lambda i: (0, 0)), vec, vec],
        out_shape=[_sds((T, D_MODEL), BF16), _sds((T, D_FF), BF16), _sds((T, 2 * D_FF), BF16), _sds((T, D_MODEL), BF16),
                   _sds((T, D_MODEL), F32), _sds((1, LANES), F32), _sds((1, D_MODEL), F32), _sds((1, D_MODEL), F32)],
        scratch_shapes=[pltpu.VMEM((tT, 2 * D_FF), F32)], args=(x1, tgt, w_fi_top, w_fi_bot, w_fo, g_pf, g_po))
    return res


def _merge_sgu_bwd(dx1, mix, proj, zg, zs, w_bg, w_bs, w_o, g_pm, ln_g, ln_b, w_sp, b_sp_t, job=None):
    T = dx1.shape[0]
    tT = _row_tile(T, 256)
    nb = tT // SGU_BLOCK

    def body(dx1_ref, mix_ref, su_ref, sv_ref, gg_ref, gs_ref, zg_ref, zs_ref, wbg_ref, wbs_ref, wo_ref, g_ref,
             lg_ref, lb_ref, w_ref, b_ref,
             dmix_ref, dzg_ref, dzs_ref, dgate_ref, dyg_ref, dp_ref, dgpm_ref, dw_ref, dbt_ref, dlg_ref, dlb_ref):
        @pl.when(pl.program_id(0) == 0)
        def _():
            for ref in (dgpm_ref, dw_ref, dbt_ref, dlg_ref, dlb_ref):
                ref[...] = jnp.zeros_like(ref)

        mix = mix_ref[...].astype(F32)
        dmix, dg = _rms_bwd(dx1_ref[...], mix, _rms_stats(mix), g_ref[...])
        dgpm_ref[...] += jnp.sum(dg, axis=0, keepdims=True)
        dmb = dmix.astype(BF16)
        dmix_ref[...] = dmb
        dmerged = _dot(dmb, wo_ref[...], _NT)
        dys = None
        for k, (gate_ref, z_ref, w_br_ref, dz_ref) in enumerate(((gg_ref, zg_ref, wbg_ref, dzg_ref),
                                                                 (gs_ref, zs_ref, wbs_ref, dzs_ref))):
            sg = _sigmoid(gate_ref[...].astype(F32))
            dz = (dmerged * sg).astype(BF16)
            dz_ref[...] = dz
            dgate_ref[:, k * 1024:(k + 1) * 1024] = (dmerged * z_ref[...].astype(F32) * (sg * (1.0 - sg))).astype(BF16)
            dy_branch = _dot(dz, w_br_ref[...], _NT)
            if k == 0:
                dyg_ref[...] = dy_branch.astype(BF16)
            else:
                dys = dy_branch

        mask = _sgu_mask()
        lane = lax.broadcasted_iota(jnp.int32, (SGU_BLOCK, LANES), 1)
        for g in range(SGU_GROUPS):
            gc = slice(g * SGU_DG, (g + 1) * SGU_DG)
            gc_v = slice(1024 + g * SGU_DG, 1024 + (g + 1) * SGU_DG)
            wm = jnp.where(mask, w_ref[g], 0.0).astype(BF16)
            vf, dvf_dsv = _gelu_and_grad(sv_ref[:, gc].astype(F32))
            mu = jnp.mean(vf, axis=-1, keepdims=True)
            vc = vf - mu
            rstd = lax.rsqrt(jnp.mean(vc * vc, axis=-1, keepdims=True) + EPS)
            vhat = vc * rstd
            vn = (vhat * lg_ref[:, gc] + lb_ref[:, gc]).astype(BF16)
            u, du_dsu = _gelu_and_grad(su_ref[:, gc].astype(F32))
            dy = dys[:, gc]
            dmixed = (dy * u).astype(BF16)
            dvn_parts = []
            dw_acc = jnp.zeros((SGU_BLOCK, SGU_BLOCK), F32)
            db_acc = jnp.zeros((SGU_BLOCK, 1), F32)
            for b in range(nb):
                rows = slice(b * SGU_BLOCK, (b + 1) * SGU_BLOCK)
                mixed = _dot(wm, vn[rows, :]) + b_ref[:, g:g + 1]
                dp_ref[rows, gc] = (dy[rows, :] * mixed * du_dsu[rows, :]).astype(BF16)
                dvn_parts.append(_dot(wm, dmixed[rows, :], _TN))
                dw_acc = dw_acc + _dot(dmixed[rows, :], vn[rows, :], _NT)
                db_acc = db_acc + jnp.sum(dmixed[rows, :].astype(F32), axis=-1, keepdims=True)
            dw_ref[g] += jnp.where(mask, dw_acc, 0.0)
            dbt_ref[...] += jnp.where(lane == g, db_acc, 0.0)
            dvn = jnp.concatenate(dvn_parts, axis=0)
            dlg_ref[:, gc] += jnp.sum(dvn * vhat, axis=0, keepdims=True)
            dlb_ref[:, gc] += jnp.sum(dvn, axis=0, keepdims=True)
            dvh = dvn * lg_ref[:, gc]
            dvf = rstd * (dvh - jnp.mean(dvh, axis=-1, keepdims=True)
                          - vhat * jnp.mean(dvh * vhat, axis=-1, keepdims=True))
            dp_ref[:, gc_v] = (dvf * dvf_dsv).astype(BF16)

    row = pl.BlockSpec((tT, D_MODEL), lambda i: (i, 0))
    blk = lambda j: pl.BlockSpec((tT, 1024), lambda i: (i, j))
    vec = pl.BlockSpec((1, D_MODEL), lambda i: (0, 0))
    wide = lambda w: pl.BlockSpec((tT, w), lambda i: (i, 0))
    sds = _sds((T, D_MODEL), BF16)
    return _call(
        body, name="merge_sgu_bwd", grid=(T // tT,),
        in_specs=[row, row, blk(3), blk(4), blk(5), blk(6), row, row] + [_whole()] * 3 + [vec] + [_whole()] * 4,
        out_specs=[row, row, row, wide(W_MRG), row, wide(W_SGU), vec,
                   pl.BlockSpec((SGU_GROUPS, SGU_BLOCK, SGU_BLOCK), lambda i: (0, 0, 0)),
                   pl.BlockSpec((SGU_BLOCK, LANES), lambda i: (0, 0)), vec, vec],
        out_shape=[sds, sds, sds, _sds((T, W_MRG), BF16), sds, _sds((T, W_SGU), BF16), _sds((1, D_MODEL), F32),
                   _sds((SGU_GROUPS, SGU_BLOCK, SGU_BLOCK), F32), _sds((SGU_BLOCK, LANES), F32),
                   _sds((1, 1024), F32), _sds((1, 1024), F32)],
        args=(dx1, mix, proj, proj, proj, proj, zg, zs, w_bg, w_bs, w_o, g_pm, ln_g, ln_b, w_sp, b_sp_t), job=job)


def _gla_bwd(proj, alow, wgu, b_gate, gn, states, dy_gla, job=None):
    T = proj.shape[0]
    tT = _row_tile(T, 512)
    nc = tT // CHUNK
    nt = T // tT

    def body(q_ref, k_ref, v_ref, r_ref, al_ref, wgu_ref, bg_ref, gn_ref, later_ref, earlier_ref, st_ref, sp_ref, dy_ref,
             dp_ref, dal_ref, dgn_ref, dbg_ref, dwgu_ref, g_scr, dd_scr, dt_scr):
        step = pl.program_id(0)

        @pl.when(step == 0)
        def _():
            g_scr[...] = jnp.zeros_like(g_scr)
            dgn_ref[...] = jnp.zeros_like(dgn_ref)
            dbg_ref[...] = jnp.zeros_like(dbg_ref)
            dwgu_ref[...] = jnp.zeros_like(dwgu_ref)

        has_prev = jnp.where(step == nt - 1, 0.0, 1.0)
        logit, la, delta = _gla_decay_terms(al_ref, wgu_ref, bg_ref, later_ref)
        e = jnp.exp(delta)
        kdec_f = k_ref[...].astype(F32) * e
        kdec = kdec_f.astype(BF16)
        heads = range(GLA_HEADS)
        kcs = [slice(h * GLA_DK, (h + 1) * GLA_DK) for h in heads]
        vcs = [slice(h * GLA_DV, (h + 1) * GLA_DV) for h in heads]
        carry = [g_scr[h] for h in heads]
        dgn_acc = [jnp.zeros((1, GLA_DV), F32) for _ in heads]
        for c in reversed(range(nc)):
            rows = slice(c * CHUNK, (c + 1) * CHUNK)
            first = slice(c * CHUNK, c * CHUNK + 1)
            dec = jnp.exp(la[first, :] + delta[first, :])
            s_b = [st_ref[c, h].astype(BF16) for h in heads]
            qs = [(q_ref[rows, kcs[h]].astype(F32) * (GLA_DK ** -0.5)).astype(BF16) for h in heads]
            o = [_dot(qs[h], s_b[h], _NT) for h in heads]
            do = []
            for h in heads:
                rstd = _rms_stats(o[h])
                ohat = o[h] * rstd
                gnh = gn_ref[:, vcs[h]]
                dy = dy_ref[rows, vcs[h]].astype(F32)
                rr = r_ref[rows, vcs[h]].astype(F32)
                sg = _sigmoid(rr)
                don = dy * (rr * sg)
                dp_ref[rows, OFF_R + h * GLA_DV:OFF_R + (h + 1) * GLA_DV] = (
                    dy * (ohat * gnh) * (sg * (1.0 + rr * (1.0 - sg)))).astype(BF16)
                dgn_acc[h] = dgn_acc[h] + jnp.sum(don * ohat, axis=0, keepdims=True)
                dn = don * gnh
                do.append((rstd * (dn - ohat * jnp.mean(dn * ohat, axis=-1, keepdims=True))).astype(BF16))
            dq = [_dot(do[h], s_b[h]) for h in heads]
            g_t = [_dot(do[h], qs[h], _TN) + carry[h] for h in heads]
            g_b = [g_t[h].astype(BF16) for h in heads]
            dv = [_dot(kdec[rows, kcs[h]], g_b[h], _NT) for h in heads]
            dkdec = [_dot(v_ref[rows, vcs[h]], g_b[h]) for h in heads]
            for h in heads:
                s_prev = st_ref[c - 1, h] if c > 0 else sp_ref[0, h] * has_prev
                ddec = jnp.sum(g_t[h] * s_prev, axis=0, keepdims=True)
                carry[h] = g_t[h] * dec[:, kcs[h]]
                dp_ref[rows, OFF_Q + h * GLA_DK:OFF_Q + (h + 1) * GLA_DK] = (dq[h] * (GLA_DK ** -0.5)).astype(BF16)
                dp_ref[rows, OFF_V + h * GLA_DV:OFF_V + (h + 1) * GLA_DV] = dv[h].astype(BF16)
                dp_ref[rows, OFF_K + h * GLA_DK:OFF_K + (h + 1) * GLA_DK] = (dkdec[h] * e[rows, kcs[h]]).astype(BF16)
                dd_scr[rows, kcs[h]] = dkdec[h] * kdec_f[rows, kcs[h]]
                dt_scr[rows, kcs[h]] = jnp.broadcast_to(ddec * dec[:, kcs[h]], (CHUNK, GLA_DK))
        for h in heads:
            g_scr[h] = carry[h]
            dgn_ref[:, vcs[h]] += dgn_acc[h]
        dla = _dot_exact_lhs(earlier_ref[...], dd_scr[...]) + dt_scr[...]
        dlogit = dla * (1.0 / GLA_TAU) * _sigmoid(-logit)
        dbg_ref[...] += jnp.sum(dlogit, axis=0, keepdims=True)
        dwgu_ref[...] += _dot_bf16(al_ref[...], dlogit, _TN)
        dal_ref[...] = _dot_bf16(dlogit, wgu_ref[...], _NT).astype(BF16)

    rev = lambda i: nt - 1 - i
    blk = lambda w, j: pl.BlockSpec((tT, w), lambda i: (rev(i), j))
    st_blk = pl.BlockSpec((nc, GLA_HEADS, GLA_DV, GLA_DK), lambda i: (rev(i), 0, 0, 0))
    sp_blk = pl.BlockSpec((1, GLA_HEADS, GLA_DV, GLA_DK), lambda i: (jnp.maximum(rev(i) * nc - 1, 0), 0, 0, 0))
    return _call(
        body, name="gla_bwd", grid=(nt,),
        in_specs=[blk(512, 0), blk(512, 1), blk(1024, 1), blk(1024, 2), blk(LANES, 0)] + [_whole()] * 5
        + [st_blk, sp_blk, blk(GLA_V, 0)],
        out_specs=[blk(W_GLA, 0), blk(LANES, 0), pl.BlockSpec((1, GLA_V), lambda i: (0, 0)),
                   pl.BlockSpec((1, GLA_QK), lambda i: (0, 0)), pl.BlockSpec((LANES, GLA_QK), lambda i: (0, 0))],
        out_shape=[_sds((T, W_GLA), BF16), _sds((T, LANES), BF16), _sds((1, GLA_V), F32), _sds((1, GLA_QK), F32),
                   _sds((LANES, GLA_QK), F32)],
        scratch_shapes=[pltpu.VMEM((GLA_HEADS, GLA_DV, GLA_DK), F32), pltpu.VMEM((tT, GLA_QK), F32),
                        pltpu.VMEM((tT, GLA_QK), F32)],
        args=(proj, proj, proj, proj, alow, wgu, b_gate, gn, _chunk_masks(tT, upper=True), _chunk_masks(tT, upper=False),
              states, states, dy_gla), job=job)


def _inproj_bwd(x, dx1, g1, w_all, dparts, job=None):
    T = x.shape[0]
    tT = _row_tile(T, 512)
    offs = (0, W_GLA, W_GLA + W_SGU, N_MAIN)

    def body(x_ref, dx1_ref, g_ref, w_ref, *rest):
        part_refs, (dx_ref, dg_ref) = rest[:len(offs)], rest[len(offs):]

        @pl.when(pl.program_id(0) == 0)
        def _():
            dg_ref[...] = jnp.zeros_like(dg_ref)

        da = jnp.zeros((tT, D_MODEL), F32)
        for off, p_ref in zip(offs, part_refs):
            da = da + _dot(p_ref[...], w_ref[:, off:off + p_ref.shape[1]], _NT)
        xv = x_ref[...]
        dx, dg = _rms_bwd(da, xv, _rms_stats(xv), g_ref[...])
        dg_ref[...] += jnp.sum(dg, axis=0, keepdims=True)
        dx_ref[...] = dx1_ref[...] + dx

    row = lambda w: pl.BlockSpec((tT, w), lambda i: (i, 0))
    vec = pl.BlockSpec((1, D_MODEL), lambda i: (0, 0))
    return _call(
        body, name="inproj_bwd", grid=(T // tT,),
        in_specs=[row(D_MODEL), row(D_MODEL), vec, _whole()] + [row(p.shape[1]) for p in dparts],
        out_specs=[row(D_MODEL), vec], out_shape=[_sds((T, D_MODEL), F32), _sds((1, D_MODEL), F32)],
        args=(x, dx1, g1, w_all, *dparts), job=job)


def _tn_matmul(a, b, name, job=None):
    T, M = a.shape
    N = b.shape[1]
    tk = _row_tile(T, 1024)
    tm = M if M <= 1024 else 1408
    tn = N if N <= 3072 else N // 2
    assert M % tm == 0 and N % tn == 0

    def body(a_ref, b_ref, o_ref):
        @pl.when(pl.program_id(2) == 0)
        def _():
            o_ref[...] = _dot(a_ref[...], b_ref[...], _TN)

        @pl.when(pl.program_id(2) > 0)
        def _():
            o_ref[...] += _dot(a_ref[...], b_ref[...], _TN)

    res, jres = _call(
        body, name=name, grid=(M // tm, N // tn, T // tk),
        in_specs=[pl.BlockSpec((tk, tm), lambda i, j, k: (k, i)), pl.BlockSpec((tk, tn), lambda i, j, k: (k, j))],
        out_specs=[pl.BlockSpec((tm, tn), lambda i, j, k: (i, j))], out_shape=[_sds((M, N), F32)], args=(a, b), job=job)
    return res[0], jres


def _pad_rows(a, rows=8):
    return jnp.pad(a, ((0, rows - a.shape[0]), (0, LANES - a.shape[1])))


def _halves_view(dw):
    r = dw.shape[0] // N_CHIPS
    return dw.reshape(N_CHIPS, 2, r // 2, dw.shape[1])


def kernel(x, norm_pre_mix, w_in, w_gate_up, b_gate, gla_norm, sgu_ln_g, sgu_ln_b, w_spatial, b_spatial, w_branch_gla, w_branch_sgu, w_out, norm_post_mix, norm_pre_ffn, w_ffn_in, w_ffn_out, norm_post_ffn, loss_target, m_norm_pre_mix, m_w_in, m_w_gate_up, m_b_gate, m_gla_norm, m_sgu_ln_g, m_sgu_ln_b, m_w_spatial, m_b_spatial, m_w_branch_gla, m_w_branch_sgu, m_w_out, m_norm_post_mix, m_norm_pre_ffn, m_w_ffn_in, m_w_ffn_out, m_norm_post_ffn, v_norm_pre_mix, v_w_in, v_w_gate_up, v_b_gate, v_gla_norm, v_sgu_ln_g, v_sgu_ln_b, v_w_spatial, v_b_spatial, v_w_branch_gla, v_w_branch_sgu, v_w_out, v_norm_post_mix, v_norm_pre_ffn, v_w_ffn_in, v_w_ffn_out, v_norm_post_ffn):
    chip = 2 * lax.axis_index("x") + lax.axis_index("y")
    xt, tgt = x[0], loss_target[0]

    tiny = jnp.concatenate([w_gate_up[0], _pad_rows(gla_norm[0]), _pad_rows(sgu_ln_g[0]), _pad_rows(sgu_ln_b[0]),
                            jnp.zeros((24, LANES), F32)], axis=0)

    def with_own(gathered, own):
        return lax.dynamic_update_slice(gathered, own[None], (chip, 0, 0))

    w_in_t, m_in_t, v_in_t = w_in[0].T, m_w_in[0].T, v_w_in[0].T
    w_in_b = _transposed_cast(w_in_t)
    (*own_rows, fi_top, fi_bot), (g_in, g_tiny) = _cast_weights(
        [w_branch_gla[0], w_branch_sgu[0], w_out[0], w_ffn_out[0]], w_ffn_in[0], job=_job_gather([w_in_b, tiny]))
    g_tiny = with_own(g_tiny, tiny)
    w_all = _relayout_w_in(with_own(g_in, w_in_b))
    cols = lambda a: a.transpose(1, 0, 2).reshape(a.shape[1], N_CHIPS * a.shape[2])
    wgu = jnp.pad(cols(g_tiny[:, 0:16]), ((0, LANES - GLA_RANK), (0, 0)))
    gn = cols(g_tiny[:, 16:20, :64]).reshape(1, GLA_V)
    ln_g = cols(g_tiny[:, 24:28, :64]).reshape(1, 1024)
    ln_b = cols(g_tiny[:, 32:36, :64]).reshape(1, 1024)
    b_sp_t = jnp.pad(b_spatial[0].T, ((0, 0), (0, LANES - SGU_GROUPS)))
    w_sp = w_spatial[0]

    (a, proj, alow), g_rows = _inproj_fwd(xt, norm_pre_mix, w_all, job=_job_gather(own_rows))
    rows = lambda g: g.reshape(N_CHIPS * g.shape[1], g.shape[2])
    w_bg, w_bs, w_o, w_fo = [rows(with_own(g, own)) for g, own in zip(g_rows, own_rows)]
    (y_gla, states), (g_top,) = _gla_fwd(proj, alow, wgu, b_gate, gn, job=_job_gather([fi_top]))
    (y_sgu, zg, zs, merged, mix, x1), (g_bot,) = _sgu_merge_fwd(
        xt, proj, y_gla, ln_g, ln_b, w_sp, b_sp_t, w_bg, w_bs, w_o, norm_post_mix, job=_job_gather([fi_bot]))
    h, f, dgu, dy, dx1, loss, d_gpf, d_gpo = _ffn_fwd_bwd(x1, tgt, with_own(g_top, fi_top), with_own(g_bot, fi_bot),
                                                          w_fo, norm_pre_ffn, norm_post_ffn)

    own_part = lambda c: lax.dynamic_index_in_dim(c, chip, 0, keepdims=False)
    whole = lambda hs: [[(h_, None)] for h_ in hs]
    dw_fo, _ = _tn_matmul(f, dy, "dw_ffn_out")
    dw_fo4 = _halves_view(dw_fo)
    dw_fi, (q_fo,) = _tn_matmul(h, dgu, "dw_ffn_in", job=_job_to_other_core([[(dw_fo4, 0)]]))
    c_fo = _presum(dw_fo4, q_fo, "presum_ffn_out")
    (dmix, dzg, dzs, dp_mrg, dyg, dp_sgu, d_gpm, d_wsp, d_bsp_t, d_lng, d_lnb), (s_fo, q_fi) = _merge_sgu_bwd(
        dx1, mix, proj, zg, zs, w_bg, w_bs, w_o, norm_post_mix, ln_g, ln_b, w_sp, b_sp_t,
        job=_join(_job_scatter([c_fo]), _job_to_other_core([[(dw_fi, 0)]])))
    c_fi = _presum(dw_fi, q_fi, "presum_ffn_in")
    dw_c, _ = _tn_matmul(a, dp_mrg, "dw_in_merge")
    dw_b, _ = _tn_matmul(a, dp_sgu, "dw_in_sgu")
    dw_o4 = _halves_view(_tn_matmul(merged, dmix, "dw_out")[0])
    dw_bg4 = _halves_view(_tn_matmul(y_gla, dzg, "dw_branch_gla")[0])
    dw_bs4 = _halves_view(_tn_matmul(y_sgu, dzs, "dw_branch_sgu")[0])
    h_fo = _sum_slots(own_part(c_fo), s_fo, "sum_ffn_out")
    (dp_gla, dal, d_gn, d_bg, d_wgu), (s_fi, t_fo, q_b, q_c, q_o, q_bg, q_bs) = _gla_bwd(
        proj, alow, wgu, b_gate, gn, states, dyg,
        job=_join(_job_scatter([c_fi]), _job_to_other_core(
            whole([h_fo]) + [[(dw_b, 0)], [(dw_c, 0)], [(dw_o4, 0)], [(dw_bg4, 0)], [(dw_bs4, 0)]])))
    c_o, c_bg, c_bs = (_presum(dw_o4, q_o, "presum_out"), _presum(dw_bg4, q_bg, "presum_branch_gla"),
                       _presum(dw_bs4, q_bs, "presum_branch_sgu"))
    h_fi = _sum_slots(own_part(c_fi), s_fi, "sum_ffn_in")
    dw_d, _ = _tn_matmul(a, dal, "dw_in_gate")
    dw_a, (s_o, s_bg, s_bs, t_fi, q_d) = _tn_matmul(
        a, dp_gla, "dw_in_gla",
        job=_join(_job_scatter([c_o, c_bg, c_bs]), _job_to_other_core(whole([h_fi]) + [[(dw_d, 0)]])))
    h_o, h_bg, h_bs = (_sum_slots(own_part(c_o), s_o, "sum_out"), _sum_slots(own_part(c_bg), s_bg, "sum_branch_gla"),
                       _sum_slots(own_part(c_bs), s_bs, "sum_branch_sgu"))

    grads, deltas, new_m, new_v = {}, {}, {}, {}

    def update(name, w, m, v, g_mine, g_theirs, job=None):
        (g, d, m2, v2), jres = _adamw(w[0], m[0], v[0], g_mine, g_theirs, "adamw_" + name, job=job)
        grads[name], deltas[name], new_m[name], new_v[name] = g[None], d[None], m2[None], v2[None]
        return jres

    dw_in = [(dw_a, 0), (dw_b, W_GLA), (dw_c, W_GLA + W_SGU), (dw_d, N_MAIN)]
    q_a, t_o, t_bg, t_bs = update("w_ffn_out", w_ffn_out, m_w_ffn_out, v_w_ffn_out, [h_fo], [t_fo],
                                  job=_job_to_other_core([[(dw_a, 0)]] + whole([h_o, h_bg, h_bs])))
    q_in = [q_a, q_b, q_c, q_d]
    hr_in = D_MODEL // 2
    c_in_a, _ = _presum_w_in(dw_in, q_in, 0, hr_in // 8, "presum_w_in_a")
    c_in_b, (s_in_a,) = _presum_w_in(dw_in, q_in, hr_in // 8, 7 * hr_in // 8, "presum_w_in_b",
                                     job=_job_scatter([c_in_a]))
    update("w_ffn_in", w_ffn_in, m_w_ffn_in, v_w_ffn_in, [h_fi], [t_fi])
    update("w_out", w_out, m_w_out, v_w_out, [h_o], [t_o])
    update("w_branch_gla", w_branch_gla, m_w_branch_gla, v_w_branch_gla, [h_bg], [t_bg])
    update("w_branch_sgu", w_branch_sgu, m_w_branch_sgu, v_w_branch_sgu, [h_bs], [t_bs])
    (grad_x, d_g1), (s_in_b,) = _inproj_bwd(xt, dx1, norm_pre_mix, w_all, (dp_gla, dp_sgu, dp_mrg, dal),
                                            job=_job_scatter([c_in_b]))
    h_in = [_sum_slots(own_part(c_in_a), s_in_a, "sum_w_in_a"), _sum_slots(own_part(c_in_b), s_in_b, "sum_w_in_b")]
    t_in = _run_job(_job_to_other_core(whole(h_in)), "swap_w_in")
    for store, val in zip((grads, deltas, new_m, new_v),
                          _adamw_transposed(w_in_t, m_in_t, v_in_t, h_in, t_in, "adamw_w_in")):
        store["w_in"] = val.T[None]

    small_names = ["w_spatial", "w_gate_up", "norm_pre_mix", "norm_post_mix", "norm_pre_ffn", "norm_post_ffn", "b_gate",
                   "b_spatial", "gla_norm", "sgu_ln_g", "sgu_ln_b"]
    loss_out, small = _small_adamw(
        _small_sum([d_wsp, d_wgu, d_g1, d_gpm, d_gpf, d_gpo, d_bg, d_bsp_t, d_gn, d_lng, d_lnb, loss]),
        [w_spatial, w_gate_up, norm_pre_mix, norm_post_mix, norm_pre_ffn, norm_post_ffn, b_gate, b_spatial, gla_norm,
         sgu_ln_g, sgu_ln_b],
        [m_w_spatial, m_w_gate_up, m_norm_pre_mix, m_norm_post_mix, m_norm_pre_ffn, m_norm_post_ffn, m_b_gate,
         m_b_spatial, m_gla_norm, m_sgu_ln_g, m_sgu_ln_b],
        [v_w_spatial, v_w_gate_up, v_norm_pre_mix, v_norm_post_mix, v_norm_pre_ffn, v_norm_post_ffn, v_b_gate,
         v_b_spatial, v_gla_norm, v_sgu_ln_g, v_sgu_ln_b])
    for store, vals in zip((grads, deltas, new_m, new_v), small):
        store.update(zip(small_names, vals))

    order = ["norm_pre_mix", "w_in", "w_gate_up", "b_gate", "gla_norm", "sgu_ln_g", "sgu_ln_b", "w_spatial", "b_spatial",
             "w_branch_gla", "w_branch_sgu", "w_out", "norm_post_mix", "norm_pre_ffn", "w_ffn_in", "w_ffn_out",
             "norm_post_ffn"]
    out = [loss_out, grad_x[None]]
    for store in (grads, deltas, new_m, new_v):
        out.extend(store[n] for n in order)
    return tuple(out)
```

```python
import jax
import jax.numpy as jnp
from jax import lax
from jax.experimental import pallas as pl
from jax.experimental.pallas import tpu as pltpu

F32 = jnp.float32
BF16 = jnp.bfloat16

D_MODEL = 1024
GLA_HEADS = 4
GLA_DK = 128
GLA_DV = 256
GLA_QK = GLA_HEADS * GLA_DK
GLA_V = GLA_HEADS * GLA_DV
GLA_RANK = 16
GLA_TAU = 16.0
CHUNK = 64
SGU_GROUPS = 4
SGU_BLOCK = 128
SGU_DG = 256
D_FF = 2816
EPS = 1e-6
LANES = 128

OFF_Q, OFF_K, OFF_V, OFF_R, OFF_SU, OFF_SV, OFF_GG, OFF_GS, OFF_AL = 0, 512, 1024, 2048, 3072, 4096, 5120, 6144, 7168
W_GLA, W_SGU, W_MRG = 3072, 2048, 2048
N_MAIN = 7168
N_ALL = N_MAIN + LANES
_IN_SPLITS = (GLA_QK, GLA_QK, GLA_V, GLA_V, GLA_RANK, 1024, 1024, 1024, 1024)
_IN_STARTS = tuple(sum(_IN_SPLITS[:i]) for i in range(len(_IN_SPLITS) + 1))
_IN_DST = (OFF_Q, OFF_K, OFF_V, OFF_R, OFF_AL, OFF_SU, OFF_SV, OFF_GG, OFF_GS)
D_IN = _IN_STARTS[-1]

ADAM_LR = 0.001
ADAM_B1 = 0.9
ADAM_B2 = 0.999
ADAM_EPS = 1e-08
ADAM_WD = 0.01
ADAM_STEP = 10

VMEM_LIMIT_BYTES = 56 * 1024 * 1024
N_CHIPS = 4
N_PEER = N_CHIPS - 1
N_DEV = 8
MESH = pl.DeviceIdType.MESH

_NN = (((1,), (0,)), ((), ()))
_NT = (((1,), (1,)), ((), ()))
_TN = (((0,), (0,)), ((), ()))


def _dot(a, b, dims=_NN):
    return lax.dot_general(a, b, dims, preferred_element_type=F32)


def _split(x):
    hi = x.astype(BF16)
    lo = (x - hi.astype(F32)).astype(BF16)
    return hi, lo


def _dot_bf16(a, b, dims=_NN):
    return _dot(a.astype(BF16), b.astype(BF16), dims)


def _dot_exact_lhs(m, x):
    xh, xl = _split(x)
    return _dot(m, xh) + _dot(m, xl)


def _sigmoid(x):
    return 0.5 * jnp.tanh(0.5 * x) + 0.5


def _log_sigmoid(x):
    return jnp.minimum(x, 0.0) - jnp.log(1.0 + jnp.exp(-jnp.abs(x)))


_GELU_C = 0.7978845608028654
_GELU_A = 0.044715


def _gelu_and_grad(x):
    x2 = x * x
    t = jnp.tanh(_GELU_C * (x + _GELU_A * x * x2))
    g = 0.5 * x * (1.0 + t)
    dg = 0.5 * (1.0 + t) + 0.5 * x * (1.0 - t * t) * (_GELU_C * (1.0 + 3.0 * _GELU_A * x2))
    return g, dg


def _gelu(x):
    t = jnp.tanh(_GELU_C * (x + _GELU_A * x * x * x))
    return 0.5 * x * (1.0 + t)


def _rms_stats(x):
    return lax.rsqrt(jnp.mean(x * x, axis=-1, keepdims=True) + EPS)


def _rms_bwd(dout, y, r, g):
    yhat = y * r
    dn = dout * g
    dy = r * (dn - yhat * jnp.mean(dn * yhat, axis=-1, keepdims=True))
    return dy, dout * yhat


def _whole():
    return pl.BlockSpec(memory_space=pltpu.VMEM)


def _row_tile(T, want):
    t = min(T, want)
    assert T % t == 0
    return t


def _chunk_masks(tT, upper):
    row = lax.broadcasted_iota(jnp.int32, (tT, tT), 0)
    col = lax.broadcasted_iota(jnp.int32, (tT, tT), 1)
    same = (row // CHUNK) == (col // CHUNK)
    tri = (col > row) if upper else (col < row)
    return jnp.where(same & tri, 1.0, 0.0).astype(BF16)


class _Job:
    def __init__(self, ins, out_shapes, scratch, start, finish, mid=None):
        self.ins, self.out_shapes, self.scratch = list(ins), list(out_shapes), list(scratch)
        self.start, self.finish, self.mid = start, finish, mid


def _join(*jobs):
    def split(refs, counts):
        out, at = [], 0
        for n in counts:
            out.append(refs[at:at + n])
            at += n
        return out

    ni, no, ns = [len(j.ins) for j in jobs], [len(j.out_shapes) for j in jobs], [len(j.scratch) for j in jobs]

    def start(ins, outs, scr):
        for j, a, b, c in zip(jobs, split(ins, ni), split(outs, no), split(scr, ns)):
            j.start(a, b, c)

    def finish(ins, outs, scr):
        for j, a, b, c in zip(jobs, split(ins, ni), split(outs, no), split(scr, ns)):
            j.finish(a, b, c)

    def mid(ins, outs, scr):
        for j, a, b, c in zip(jobs, split(ins, ni), split(outs, no), split(scr, ns)):
            if j.mid is not None:
                j.mid(a, b, c)

    return _Job(sum((j.ins for j in jobs), []), sum((j.out_shapes for j in jobs), []),
                sum((j.scratch for j in jobs), []), start, finish, mid if any(j.mid for j in jobs) else None)


def _mesh_pos():
    return lax.axis_index("x"), lax.axis_index("y"), lax.axis_index("c")


def _peer_chips(xi, yi):
    return [(1 - xi, yi), (xi, 1 - yi), (1 - xi, 1 - yi)]


def _half(ci, rows):
    return pl.ds(pl.multiple_of(ci * rows, 8), rows)


def _sds(shape, dtype):
    return jax.ShapeDtypeStruct(tuple(shape), dtype)


def _job_gather(arrs):
    n = len(arrs)
    kinds = 12
    Y0, Y1, X1, X0, ON_X, ON_Y, D2D = 0, 1, 2, 3, 4, 5, 6

    def copies(ins, outs, scr):
        send_sems, recv_sems = scr
        xi, yi, ci = _mesh_pos()
        me, cx, cy, cd = 2 * xi + yi, 2 * (1 - xi) + yi, 2 * xi + (1 - yi), 2 * (1 - xi) + (1 - yi)
        to_x, to_y, to_core = (1 - xi, yi, ci), (xi, 1 - yi, ci), (xi, yi, 1 - ci)
        table = []
        for k in range(n):
            qr = arrs[k].shape[0] // 4

            def rows(core, q):
                return pl.ds(pl.multiple_of((2 * core + q) * qr, 8), qr)

            def cp(kind, src, dst, to):
                s = k * kinds + kind
                return pltpu.make_async_remote_copy(src_ref=src, dst_ref=dst, send_sem=send_sems.at[s],
                                                    recv_sem=recv_sems.at[s], device_id=to, device_id_type=MESH)

            def slab(chip, core, q):
                return outs[k].at[chip, rows(core, q)]

            t = {}
            for kind, q, to, frm in ((Y0, 0, to_y, cy), (Y1, 1, to_y, cy), (X1, 1, to_x, cx), (X0, 0, to_x, cx)):
                mine = ins[k].at[rows(ci, q)]
                t[kind] = (cp(kind, mine, slab(me, ci, q), to), cp(kind, mine, slab(frm, ci, q), to))
            t[ON_X] = (cp(ON_X, slab(cy, ci, 0), slab(cy, ci, 0), to_x), cp(ON_X, slab(cy, ci, 0), slab(cd, ci, 0), to_x))
            t[ON_Y] = (cp(ON_Y, slab(cx, ci, 1), slab(cx, ci, 1), to_y), cp(ON_Y, slab(cx, ci, 1), slab(cd, ci, 1), to_y))
            for i, (chip, q) in enumerate(((cy, 0), (cy, 1), (cx, 1), (cx, 0), (cd, 0), (cd, 1))):
                t[D2D + i] = (cp(D2D + i, slab(chip, ci, q), slab(chip, ci, q), to_core),
                              cp(D2D + i, slab(chip, ci, q), slab(chip, 1 - ci, q), to_core))
            table.append(t)
        return table

    def start(ins, outs, scr):
        table = copies(ins, outs, scr)
        for kind in (Y0, X1, Y1, X0):
            for t in table:
                t[kind][0].start()

    def arrived(table, kind, then):
        for t in table:
            t[kind][1].wait_recv()
            for nxt in then:
                t[nxt][0].start()

    def mid(ins, outs, scr):
        table = copies(ins, outs, scr)
        arrived(table, Y0, (ON_X, D2D + 0))
        arrived(table, X1, (ON_Y, D2D + 2))

    def finish(ins, outs, scr):
        table = copies(ins, outs, scr)
        arrived(table, Y1, (D2D + 1,))
        arrived(table, X0, (D2D + 3,))
        arrived(table, ON_X, (D2D + 4,))
        arrived(table, ON_Y, (D2D + 5,))
        for t in table:
            for i in range(6):
                t[D2D + i][1].wait_recv()
            for kind in range(kinds):
                t[kind][0].wait_send()

    dma = pltpu.SemaphoreType.DMA
    return _Job(arrs, [_sds((N_CHIPS,) + a.shape, a.dtype) for a in arrs], [dma((n * kinds,))] * 2, start, finish, mid)


def _job_scatter(parts):
    n = len(parts)

    def copies(ins, outs, scr):
        send_sems, recv_sems = scr
        xi, yi, ci = _mesh_pos()
        res = []
        for k in range(n):
            for j, (px, py) in enumerate(_peer_chips(xi, yi)):
                s = k * N_PEER + j
                res.append(pltpu.make_async_remote_copy(
                    src_ref=ins[k].at[2 * px + py], dst_ref=outs[k].at[j], send_sem=send_sems.at[s],
                    recv_sem=recv_sems.at[s], device_id=(px, py, ci), device_id_type=MESH))
        return res

    def start(ins, outs, scr):
        for cp in copies(ins, outs, scr):
            cp.start()

    def finish(ins, outs, scr):
        for cp in copies(ins, outs, scr):
            cp.wait_recv()
            cp.wait_send()

    dma = pltpu.SemaphoreType.DMA
    return _Job(parts, [_sds((N_PEER,) + p.shape[1:], p.dtype) for p in parts], [dma((n * N_PEER,))] * 2, start, finish)


def _job_to_other_core(groups):
    pieces = [(g, a, off) for g, group in enumerate(groups) for a, off in group]
    n = len(pieces)

    def geometry(group):
        a0, off0 = group[0]
        if off0 is None:
            return a0.shape
        if a0.ndim == 4:
            return (N_CHIPS, a0.shape[2], a0.shape[3])
        return (a0.shape[0] // 2, sum(a.shape[1] for a, _ in group))

    def copies(ins, outs, scr):
        send_sems, recv_sems = scr
        xi, yi, ci = _mesh_pos()
        res = []
        for p, (g, a, off) in enumerate(pieces):
            if off is None:
                give, land = ins[p], outs[g]
            elif a.ndim == 4:
                give, land = ins[p].at[pl.ds(0, N_CHIPS), 1 - ci], outs[g]
            else:
                hr, w = a.shape[0] // 2, a.shape[1]
                give, land = ins[p].at[_half(1 - ci, hr)], outs[g].at[pl.ds(0, hr), pl.ds(off, w)]
            res.append(pltpu.make_async_remote_copy(
                src_ref=give, dst_ref=land, send_sem=send_sems.at[p], recv_sem=recv_sems.at[p],
                device_id=(xi, yi, 1 - ci), device_id_type=MESH))
        return res

    def start(ins, outs, scr):
        for cp in copies(ins, outs, scr):
            cp.start()

    def finish(ins, outs, scr):
        for cp in copies(ins, outs, scr):
            cp.wait_recv()
            cp.wait_send()

    dma = pltpu.SemaphoreType.DMA
    return _Job([a for _, a, _ in pieces], [_sds(geometry(group), group[0][0].dtype) for group in groups],
                [dma((n,))] * 2, start, finish)


def _call(body, *, name, grid, in_specs, out_specs, out_shape, args, scratch_shapes=(), parallel=False, job=None,
          by_core=False):
    n_in, n_out, n_scr = len(in_specs), len(out_specs), len(scratch_shapes)
    hbm = pl.BlockSpec(memory_space=pl.ANY)
    n_ji, n_jo = (len(job.ins), len(job.out_shapes)) if job is not None else (0, 0)
    lead = 1 if by_core else 0

    def kernel_fn(*refs):
        core, refs = refs[:lead], refs[lead:]
        ins, refs = refs[:n_in], refs[n_in:]
        j_ins, refs = refs[:n_ji], refs[n_ji:]
        outs, refs = refs[:n_out], refs[n_out:]
        j_outs, refs = refs[:n_jo], refs[n_jo:]
        scr, j_scr = refs[:n_scr], refs[n_scr:]
        if job is None:
            body(*core, *ins, *outs, *scr)
            return
        ids = [pl.program_id(d) for d in range(len(grid))]
        first = ids[0] == 0
        last = ids[0] == grid[0] - 1
        for d in range(1, len(grid)):
            first = first & (ids[d] == 0)
            last = last & (ids[d] == grid[d] - 1)

        @pl.when(first)
        def _():
            job.start(j_ins, j_outs, j_scr)

        if job.mid is not None and grid[0] >= 4:
            half_way = ids[0] == grid[0] // 2
            for d in range(1, len(grid)):
                half_way = half_way & (ids[d] == 0)

            @pl.when(half_way)
            def _():
                job.mid(j_ins, j_outs, j_scr)

        body(*core, *ins, *outs, *scr)

        @pl.when(last)
        def _():
            if job.mid is not None and grid[0] < 4:
                job.mid(j_ins, j_outs, j_scr)
            job.finish(j_ins, j_outs, j_scr)

    sem = ("parallel" if parallel and job is None else "arbitrary",) * len(grid)
    all_in = list(in_specs) + [hbm] * n_ji
    all_out = list(out_specs) + [hbm] * n_jo
    all_scratch = list(scratch_shapes) + (job.scratch if job is not None else [])
    all_shapes = list(out_shape) + (job.out_shapes if job is not None else [])
    all_args = list(args) + (job.ins if job is not None else [])
    params = pltpu.CompilerParams(dimension_semantics=sem, vmem_limit_bytes=VMEM_LIMIT_BYTES)
    if by_core:
        spec = pltpu.PrefetchScalarGridSpec(num_scalar_prefetch=1, grid=grid, in_specs=all_in, out_specs=all_out,
                                            scratch_shapes=all_scratch)
        core = lax.axis_index("c").astype(jnp.int32).reshape(1)
        res = pl.pallas_call(kernel_fn, name=name, grid_spec=spec, out_shape=all_shapes, compiler_params=params)(
            core, *all_args)
    else:
        res = pl.pallas_call(kernel_fn, name=name, grid=grid, in_specs=all_in, out_specs=all_out, out_shape=all_shapes,
                             scratch_shapes=all_scratch, compiler_params=params)(*all_args)
    return list(res[:n_out]), list(res[n_out:])


def _run_job(job, name):
    n_i, n_o = len(job.ins), len(job.out_shapes)

    def body(*refs):
        ins, outs, scr = refs[:n_i], refs[n_i:n_i + n_o], refs[n_i + n_o:]
        job.start(ins, outs, scr)
        if job.mid is not None:
            job.mid(ins, outs, scr)
        job.finish(ins, outs, scr)

    hbm = pl.BlockSpec(memory_space=pl.ANY)
    return list(pl.pallas_call(body, name=name, in_specs=[hbm] * n_i, out_specs=[hbm] * n_o, out_shape=job.out_shapes,
                               scratch_shapes=job.scratch)(*job.ins))


def _adam_values(w, m, v, g):
    m2 = ADAM_B1 * m + (1.0 - ADAM_B1) * g
    v2 = ADAM_B2 * v + (1.0 - ADAM_B2) * (g * g)
    delta = -ADAM_LR * ((m2 / (1.0 - ADAM_B1 ** ADAM_STEP)) / (jnp.sqrt(v2 / (1.0 - ADAM_B2 ** ADAM_STEP)) + ADAM_EPS)
                        + ADAM_WD * w)
    return delta, m2, v2


_P_WSP, _P_WGU, _P_NORM, _P_BG, _P_BSP, _P_HEAD, _P_LOSS, _P_ROWS = 0, 512, 576, 608, 616, 624, 720, 736


def _small_sum(dgrads):
    hr = _P_ROWS // 2

    def body(dwsp, dwgu, dg1, dgpm, dgpf, dgpo, dbg, dbspt, dgn, dlng, dlnb, loss_in, tot, pack, pair, slots, send_sems,
             recv_sems):
        xi, yi, ci = _mesh_pos()
        chip = 2 * xi + yi

        pack[...] = jnp.zeros_like(pack)
        for g in range(SGU_GROUPS):
            pack[_P_WSP + g * SGU_BLOCK:_P_WSP + (g + 1) * SGU_BLOCK] = dwsp[g]
        for j in range(N_CHIPS):
            pack[_P_WGU + GLA_RANK * j:_P_WGU + GLA_RANK * (j + 1)] = dwgu[0:GLA_RANK, LANES * j:LANES * (j + 1)]
        for k, r in enumerate((dg1, dgpm, dgpf, dgpo)):
            for q in range(8):
                pack[_P_NORM + 8 * k + q:_P_NORM + 8 * k + q + 1] = r[:, LANES * q:LANES * (q + 1)]
        for q in range(4):
            pack[_P_BG + q:_P_BG + q + 1] = dbg[:, LANES * q:LANES * (q + 1)]
        pack[_P_BSP:_P_BSP + SGU_GROUPS] = jnp.transpose(dbspt[...])[0:SGU_GROUPS]
        for k, r in enumerate((dgn, dlng, dlnb)):
            for j in range(N_CHIPS):
                for hh in range(4):
                    row = _P_HEAD + 32 * k + 8 * j + hh
                    pack[row:row + 1, 0:64] = r[:, 256 * hh + 64 * j:256 * hh + 64 * (j + 1)]
        pack[_P_LOSS:_P_LOSS + 1] = loss_in[...]

        sibling = dict(device_id=(xi, yi, 1 - ci), device_id_type=MESH)
        to_sibling = pltpu.make_async_remote_copy(src_ref=pack, dst_ref=pair, send_sem=send_sems.at[N_PEER],
                                                  recv_sem=recv_sems.at[N_PEER], **sibling)
        to_sibling.start()
        to_sibling.wait_recv()
        to_sibling.wait_send()
        pack[...] = pack[...] + pair[...]
        mine = pl.ds(pl.multiple_of(ci * hr, 8), hr)
        theirs = pl.ds(pl.multiple_of((1 - ci) * hr, 8), hr)
        slots[chip] = pack[mine, :]

        def copy(j, slot):
            px, py = _peer_chips(xi, yi)[j]
            return pltpu.make_async_remote_copy(
                src_ref=pack.at[mine], dst_ref=slots.at[slot(2 * px + py)], send_sem=send_sems.at[j],
                recv_sem=recv_sems.at[j], device_id=(px, py, ci), device_id_type=MESH)

        sends = [copy(j, lambda peer_chip: chip) for j in range(N_PEER)]
        for cp in sends:
            cp.start()
        for j in range(N_PEER):
            copy(j, lambda peer_chip: peer_chip).wait_recv()
        for cp in sends:
            cp.wait_send()
        acc = slots[0]
        for d in range(1, N_CHIPS):
            acc = acc + slots[d]
        tot[mine, :] = acc
        half_over = pltpu.make_async_remote_copy(src_ref=tot.at[mine], dst_ref=tot.at[mine], send_sem=send_sems.at[N_PEER + 1],
                                                 recv_sem=recv_sems.at[N_PEER + 1], **sibling)
        half_back = pltpu.make_async_remote_copy(src_ref=tot.at[mine], dst_ref=tot.at[theirs], send_sem=send_sems.at[N_PEER + 1],
                                                 recv_sem=recv_sems.at[N_PEER + 1], **sibling)
        half_over.start()
        half_back.wait_recv()
        half_over.wait_send()

    return pl.pallas_call(
        body, name="small_sum", in_specs=[_whole()] * 12, out_specs=_whole(), out_shape=_sds((_P_ROWS, LANES), F32),
        scratch_shapes=[pltpu.VMEM((_P_ROWS, LANES), F32), pltpu.VMEM((_P_ROWS, LANES), F32),
                        pltpu.VMEM((N_CHIPS, hr, LANES), F32),
                        pltpu.SemaphoreType.DMA((N_PEER + 2,)), pltpu.SemaphoreType.DMA((N_PEER + 2,))],
        compiler_params=pltpu.CompilerParams(vmem_limit_bytes=VMEM_LIMIT_BYTES),
    )(*dgrads)


def _small_adamw(tot, ws, ms, vs):
    n = len(ws)

    def body(*refs):
        tot = refs[0]
        w_refs, m_refs, v_refs = refs[1:1 + n], refs[1 + n:1 + 2 * n], refs[1 + 2 * n:1 + 3 * n]
        loss_out = refs[1 + 3 * n]
        outs = refs[2 + 3 * n:]
        chip = 2 * lax.axis_index("x") + lax.axis_index("y")
        loss_out[...] = tot[_P_LOSS:_P_LOSS + 1, 0:1]

        def step(k, g, pick, put):
            d, m2, v2 = _adam_values(pick(w_refs[k]), pick(m_refs[k]), pick(v_refs[k]), g)
            for o, val in zip((outs[k], outs[n + k], outs[2 * n + k], outs[3 * n + k]), (g, d, m2, v2)):
                put(o, val)

        def whole(ref):
            return ref[0]

        def put_whole(ref, val):
            ref[0] = val

        for g in range(SGU_GROUPS):
            def pick_g(ref, g=g):
                return ref[0, g]

            def put_g(ref, val, g=g):
                ref[0, g] = val

            step(0, tot[_P_WSP + g * SGU_BLOCK:_P_WSP + (g + 1) * SGU_BLOCK], pick_g, put_g)
        step(1, tot[pl.ds(pl.multiple_of(_P_WGU + GLA_RANK * chip, GLA_RANK), GLA_RANK), :], whole, put_whole)
        for k, (base, chunks) in enumerate(((_P_NORM, 8), (_P_NORM + 8, 8), (_P_NORM + 16, 8), (_P_NORM + 24, 8), (_P_BG, 4))):
            for q in range(chunks):
                def pick_q(ref, q=q):
                    return ref[:, LANES * q:LANES * (q + 1)]

                def put_q(ref, val, q=q):
                    ref[:, LANES * q:LANES * (q + 1)] = val

                step(2 + k, tot[base + q:base + q + 1], pick_q, put_q)
        step(7, tot[_P_BSP:_P_BSP + SGU_GROUPS], whole, put_whole)
        for k in range(3):
            mine = tot[pl.ds(pl.multiple_of(_P_HEAD + 32 * k + 8 * chip, 8), 8), :]
            step(8 + k, mine[0:4, 0:64], whole, put_whole)

    shapes = [_sds(w.shape, F32) for w in ws]
    res = pl.pallas_call(
        body, name="small_adamw", in_specs=[_whole()] * (1 + 3 * n), out_specs=[_whole()] * (1 + 4 * n),
        out_shape=[_sds((1, 1), F32)] + shapes * 4,
        compiler_params=pltpu.CompilerParams(vmem_limit_bytes=VMEM_LIMIT_BYTES),
    )(tot, *ws, *ms, *vs)
    return res[0].reshape(()), [list(res[1 + i * n:1 + (i + 1) * n]) for i in range(4)]


def _w_in_pieces():
    blk = D_IN // N_CHIPS
    pieces = []
    for s in range(len(_IN_SPLITS)):
        lo_s, hi_s = _IN_STARTS[s], _IN_STARTS[s + 1]
        for j in range(N_CHIPS):
            lo, hi = max(lo_s, j * blk), min(hi_s, (j + 1) * blk)
            if lo < hi:
                pieces.append((j, lo - j * blk, _IN_DST[s] + lo - lo_s, hi - lo))
    return pieces


def _relayout_w_in(gathered):
    _, rows, blk = gathered.shape
    tr = 256

    def body(g_ref, o_ref):
        o_ref[:, OFF_AL:N_ALL] = jnp.zeros((tr, LANES), BF16)
        for j, src, dst, w in _w_in_pieces():
            o_ref[:, dst:dst + w] = g_ref[j, :, src:src + w]

    res, _ = _call(body, name="relayout_w_in", grid=(rows // tr,), parallel=True,
                   in_specs=[pl.BlockSpec((N_CHIPS, tr, blk), lambda i: (0, i, 0))],
                   out_specs=[pl.BlockSpec((tr, N_ALL), lambda i: (i, 0))],
                   out_shape=[_sds((rows, N_ALL), BF16)], args=(gathered,))
    return res[0]


def _update_row_tile(rows):
    for t in range(min(rows, 256), 7, -8):
        if rows % t == 0:
            return t
    return rows


def _presum_w_in(dws, theirs, row0, rows, name, job=None):
    hr = theirs[0].shape[0]
    blk = D_IN // N_CHIPS
    tr = 64
    assert row0 % tr == 0 and rows % tr == 0
    nh, t0 = hr // tr, row0 // tr
    n = len(dws)

    def body(core_ref, *refs):
        dw_refs, q_refs, (o_ref, s_scr) = refs[:n], refs[n:2 * n], refs[2 * n:]
        for p, (a, off) in enumerate(dws):
            w = a.shape[1]
            s_scr[:, off:off + w] = (dw_refs[p][...] + q_refs[p][...]).astype(BF16)
        for j, src, dst, w in _w_in_pieces():
            o_ref[j, :, src:src + w] = s_scr[:, dst:dst + w]

    in_specs = [pl.BlockSpec((tr, a.shape[1]), lambda i, core: (i + t0 + core[0] * nh, 0)) for a, _ in dws]
    in_specs += [pl.BlockSpec((tr, q.shape[1]), lambda i, core: (i + t0, 0)) for q in theirs]
    res, jres = _call(body, name=name, grid=(rows // tr,), parallel=True, in_specs=in_specs,
                      out_specs=[pl.BlockSpec((N_CHIPS, tr, blk), lambda i, core: (0, i, 0))],
                      out_shape=[_sds((N_CHIPS, rows, blk), BF16)], scratch_shapes=[pltpu.VMEM((tr, N_ALL), BF16)],
                      args=(*[a for a, _ in dws], *theirs), job=job, by_core=True)
    return res[0], jres


def _presum(dw, theirs, name):
    if dw.ndim == 4:
        _, _, hr, c = dw.shape
        tr = _update_row_tile(hr)
        mine = pl.BlockSpec((1, 1, tr, c), lambda j, i, core: (j, core[0], i, 0))
        other = pl.BlockSpec((1, tr, c), lambda j, i, core: (j, i, 0))
    else:
        hr, c = dw.shape[0] // 2, dw.shape[1] // N_CHIPS
        tr = _update_row_tile(hr)
        nh = hr // tr
        mine = pl.BlockSpec((tr, c), lambda j, i, core: (i + core[0] * nh, j))
        other = pl.BlockSpec((tr, c), lambda j, i, core: (i, j))

    def body(core_ref, a_ref, q_ref, o_ref):
        o_ref[...] = (a_ref[...].reshape(tr, c) + q_ref[...].reshape(tr, c)).astype(BF16).reshape(o_ref.shape)

    res, _ = _call(body, name=name, grid=(N_CHIPS, hr // tr), parallel=True, in_specs=[mine, other],
                   out_specs=[pl.BlockSpec((1, tr, c), lambda j, i, core: (j, i, 0))],
                   out_shape=[_sds((N_CHIPS, hr, c), BF16)], args=(dw, theirs), by_core=True)
    return res[0]


def _sum_slots(own, slots, name):
    rows, cols = own.shape
    tr = _update_row_tile(rows)

    def body(own_ref, s_ref, o_ref):
        acc = own_ref[...].astype(F32)
        for j in range(N_PEER):
            acc = acc + s_ref[j].astype(F32)
        o_ref[...] = acc

    res, _ = _call(body, name=name, grid=(rows // tr,), parallel=True,
                   in_specs=[pl.BlockSpec((tr, cols), lambda i: (i, 0)), pl.BlockSpec((N_PEER, tr, cols), lambda i: (0, i, 0))],
                   out_specs=[pl.BlockSpec((tr, cols), lambda i: (i, 0))], out_shape=[_sds((rows, cols), F32)],
                   args=(own, slots))
    return res[0]


def _adamw(w, m, v, g_mine, g_theirs, name, job=None):
    rows, cols = w.shape
    part_rows = [p.shape[0] for p in g_mine]
    assert sum(part_rows) == rows // 2 and [p.shape[0] for p in g_theirs] == part_rows
    tr = _update_row_tile(min(part_rows))
    assert all(r % tr == 0 for r in part_rows)
    nh = (rows // 2) // tr
    starts = [sum(part_rows[:k]) // tr for k in range(len(part_rows))]
    n_parts = len(part_rows)

    def body(core_ref, w_ref, m_ref, v_ref, *rest):
        g_refs, (g_out, d_out, m_out, v_out) = rest[:-4], rest[-4:]
        step = pl.program_id(0)
        mine_here = (step // nh) == core_ref[0]
        q = step % nh
        g = None
        for k in reversed(range(n_parts)):
            val = jnp.where(mine_here, g_refs[k][...], g_refs[n_parts + k][...])
            g = val if g is None else jnp.where(q < starts[k + 1], val, g)
        d, m2, v2 = _adam_values(w_ref[...], m_ref[...], v_ref[...], g)
        g_out[...] = g
        m_out[...] = m2
        v_out[...] = v2
        d_out[...] = d

    def g_spec(k, mine):
        last = part_rows[k] // tr - 1

        def index(i, core):
            half = core[0] if mine else 1 - core[0]
            here = jnp.clip(i % nh - starts[k], 0, last)
            return (jnp.where(i // nh == half, here, jnp.where(i // nh > half, last, 0)), 0)

        return pl.BlockSpec((tr, cols), index)

    spec = pl.BlockSpec((tr, cols), lambda i, core: (i, 0))
    g_specs = [g_spec(k, True) for k in range(n_parts)] + [g_spec(k, False) for k in range(n_parts)]
    return _call(body, name=name, grid=(rows // tr,), parallel=True, in_specs=[spec] * 3 + g_specs,
                 out_specs=[spec] * 4, out_shape=[_sds((rows, cols), F32)] * 4, args=(w, m, v, *g_mine, *g_theirs),
                 job=job, by_core=True)


def _transposed_cast(wt):
    cols, rows = wt.shape

    def body(x_ref, o_ref):
        o_ref[...] = jnp.transpose(x_ref[...]).astype(BF16)

    res, _ = _call(body, name="transpose_w_in", grid=(pl.cdiv(cols, LANES),), parallel=True,
                   in_specs=[pl.BlockSpec((LANES, rows), lambda j: (j, 0))],
                   out_specs=[pl.BlockSpec((rows, LANES), lambda j: (0, j))], out_shape=[_sds((rows, cols), BF16)],
                   args=(wt,))
    return res[0]


def _cast_weights(ws, w_fi, job=None):
    steps = 4
    cols = w_fi.shape[1]
    tile = w_fi.shape[0] // (2 * steps)

    def body(*refs):
        for i_ref, o_ref in zip(refs[:len(refs) // 2], refs[len(refs) // 2:]):
            o_ref[...] = i_ref[...].astype(BF16)

    row_specs = [pl.BlockSpec((w.shape[0] // steps, w.shape[1]), lambda i: (i, 0)) for w in ws]
    half_spec = pl.BlockSpec((tile, cols), lambda i: (i, 0))
    return _call(body, name="cast_weights", grid=(steps,), parallel=True,
                 in_specs=row_specs + [pl.BlockSpec((None, tile, cols), lambda i, k=k: (k, i, 0)) for k in range(2)],
                 out_specs=row_specs + [half_spec, half_spec],
                 out_shape=[_sds(w.shape, BF16) for w in ws] + [_sds((steps * tile, cols), BF16)] * 2,
                 args=(*ws, w_fi.reshape(2, steps * tile, cols), w_fi.reshape(2, steps * tile, cols)), job=job)


def _adamw_transposed(wt, mt, vt, g_mine, g_theirs, name):
    cols, rows = wt.shape
    n_parts = len(g_mine)

    def body(w_ref, m_ref, v_ref, *rest):
        g_refs, (g_out, d_out, m_out, v_out) = rest[:-4], rest[-4:]
        mine = jnp.concatenate([r[...] for r in g_refs[:n_parts]], axis=0)
        theirs = jnp.concatenate([r[...] for r in g_refs[n_parts:]], axis=0)
        first = lax.axis_index("c") == 0
        g = jnp.transpose(jnp.concatenate([jnp.where(first, mine, theirs), jnp.where(first, theirs, mine)], axis=0))
        d, m2, v2 = _adam_values(w_ref[...], m_ref[...], v_ref[...], g)
        g_out[...] = g
        m_out[...] = m2
        v_out[...] = v2
        d_out[...] = d

    spec = pl.BlockSpec((LANES, rows), lambda j: (j, 0))
    g_specs = [pl.BlockSpec((p.shape[0], LANES), lambda j: (0, j)) for p in g_mine] * 2
    res, _ = _call(body, name=name, grid=(pl.cdiv(cols, LANES),), parallel=True, in_specs=[spec] * 3 + g_specs,
                   out_specs=[spec] * 4, out_shape=[_sds((cols, rows), F32)] * 4, args=(wt, mt, vt, *g_mine, *g_theirs))
    return res


def _fetches(pairs, sems):
    copies = [pltpu.make_async_copy(src, dst, sems.at[k]) for k, (src, dst) in enumerate(pairs)]
    first = pl.program_id(0) == 0

    @pl.when(first)
    def _():
        for cp in copies:
            cp.start()

    def ready(k):
        @pl.when(first)
        def _():
            copies[k].wait()

    return ready


def _column_fetches(w_hbm, w_scr, sems, chunks):
    return _fetches([(w_hbm.at[:, pl.ds(s, n)], w_scr.at[:, pl.ds(s, n)]) for s, n in chunks], sems)


def _inproj_fwd(x, g1, w_all, job=None):
    T = x.shape[0]
    tT = _row_tile(T, 512)
    chunks = [(j * 1024, 1024) for j in range(N_MAIN // 1024)] + [(N_MAIN, LANES)]

    def body(x_ref, g_ref, w_hbm, a_ref, proj_ref, alow_ref, w_ref, w_sems):
        ready = _column_fetches(w_hbm, w_ref, w_sems, chunks)
        xv = x_ref[...]
        a = (xv * _rms_stats(xv) * g_ref[...]).astype(BF16)
        a_ref[...] = a
        for j in range(N_MAIN // 1024):
            cols = slice(j * 1024, (j + 1) * 1024)
            ready(j)
            proj_ref[:, cols] = _dot(a, w_ref[:, cols]).astype(BF16)
        ready(N_MAIN // 1024)
        alow_ref[...] = _dot(a, w_ref[:, N_MAIN:N_ALL])

    row = lambda w: pl.BlockSpec((tT, w), lambda i: (i, 0))
    return _call(
        body, name="inproj_fwd", grid=(T // tT,),
        in_specs=[row(D_MODEL), pl.BlockSpec((1, D_MODEL), lambda i: (0, 0)), pl.BlockSpec(memory_space=pl.ANY)],
        out_specs=[row(D_MODEL), row(N_MAIN), row(LANES)],
        out_shape=[_sds((T, D_MODEL), BF16), _sds((T, N_MAIN), BF16), _sds((T, LANES), F32)],
        scratch_shapes=[pltpu.VMEM(w_all.shape, BF16), pltpu.SemaphoreType.DMA((len(chunks),))],
        args=(x, g1, w_all), job=job)


def _gla_decay_terms(al_ref, wgu_ref, bg_ref, later_ref):
    logit = _dot_bf16(al_ref[...], wgu_ref[...]) + bg_ref[...]
    la = _log_sigmoid(logit) * (1.0 / GLA_TAU)
    delta = _dot_exact_lhs(later_ref[...], la)
    return logit, la, delta


def _gla_fwd(proj, alow, wgu, b_gate, gn, job=None):
    T = proj.shape[0]
    tT = _row_tile(T, 512)
    nc = tT // CHUNK

    def body(q_ref, k_ref, v_ref, r_ref, al_ref, wgu_ref, bg_ref, gn_ref, later_ref, y_ref, st_ref, s_scr):
        @pl.when(pl.program_id(0) == 0)
        def _():
            s_scr[...] = jnp.zeros_like(s_scr)

        _, la, delta = _gla_decay_terms(al_ref, wgu_ref, bg_ref, later_ref)
        kdec = (k_ref[...].astype(F32) * jnp.exp(delta)).astype(BF16)
        heads = range(GLA_HEADS)
        kcs = [slice(h * GLA_DK, (h + 1) * GLA_DK) for h in heads]
        vcs = [slice(h * GLA_DV, (h + 1) * GLA_DV) for h in heads]
        state = [s_scr[h] for h in heads]
        for c in range(nc):
            rows = slice(c * CHUNK, (c + 1) * CHUNK)
            first = slice(c * CHUNK, c * CHUNK + 1)
            dec = jnp.exp(la[first, :] + delta[first, :])
            upd_t = [_dot(v_ref[rows, vcs[h]], kdec[rows, kcs[h]], _TN) for h in heads]
            qs = [(q_ref[rows, kcs[h]].astype(F32) * (GLA_DK ** -0.5)).astype(BF16) for h in heads]
            for h in heads:
                state[h] = state[h] * dec[:, kcs[h]] + upd_t[h]
                st_ref[c, h] = state[h]
            o = [_dot(qs[h], state[h].astype(BF16), _NT) for h in heads]
            for h in heads:
                on = o[h] * _rms_stats(o[h]) * gn_ref[:, vcs[h]]
                rr = r_ref[rows, vcs[h]].astype(F32)
                y_ref[rows, vcs[h]] = (on * (rr * _sigmoid(rr))).astype(BF16)
        for h in heads:
            s_scr[h] = state[h]

    blk = lambda w, j: pl.BlockSpec((tT, w), lambda i: (i, j))
    return _call(
        body, name="gla_fwd", grid=(T // tT,),
        in_specs=[blk(512, 0), blk(512, 1), blk(1024, 1), blk(1024, 2), blk(LANES, 0)] + [_whole()] * 4,
        out_specs=[pl.BlockSpec((tT, GLA_V), lambda i: (i, 0)),
                   pl.BlockSpec((nc, GLA_HEADS, GLA_DV, GLA_DK), lambda i: (i, 0, 0, 0))],
        out_shape=[_sds((T, GLA_V), BF16), _sds((T // CHUNK, GLA_HEADS, GLA_DV, GLA_DK), F32)],
        scratch_shapes=[pltpu.VMEM((GLA_HEADS, GLA_DV, GLA_DK), F32)],
        args=(proj, proj, proj, proj, alow, wgu, b_gate, gn, _chunk_masks(tT, upper=True)), job=job)


def _sgu_mask():
    i = lax.broadcasted_iota(jnp.int32, (SGU_BLOCK, SGU_BLOCK), 0)
    j = lax.broadcasted_iota(jnp.int32, (SGU_BLOCK, SGU_BLOCK), 1)
    return lax.shift_right_logical(j, 6) <= lax.shift_right_logical(i, 6)


def _sgu_merge_fwd(x, proj, y_gla, ln_g, ln_b, w_sp, b_sp_t, w_bg, w_bs, w_o, g_pm, job=None):
    T = x.shape[0]
    tT = _row_tile(T, 512)
    nb = tT // SGU_BLOCK

    def body(x_ref, su_ref, sv_ref, gg_ref, gs_ref, yg_ref, lg_ref, lb_ref, w_ref, b_ref, wbg_ref, wbs_ref, wo_ref,
             g_ref, ys_ref, zg_ref, zs_ref, mg_ref, mix_ref, x1_ref):
        mask = _sgu_mask()
        for g in range(SGU_GROUPS):
            gc = slice(g * SGU_DG, (g + 1) * SGU_DG)
            wm = jnp.where(mask, w_ref[g], 0.0).astype(BF16)
            vf = _gelu(sv_ref[:, gc].astype(F32))
            mu = jnp.mean(vf, axis=-1, keepdims=True)
            vc = vf - mu
            rstd = lax.rsqrt(jnp.mean(vc * vc, axis=-1, keepdims=True) + EPS)
            vn = (vc * rstd * lg_ref[:, gc] + lb_ref[:, gc]).astype(BF16)
            u = _gelu(su_ref[:, gc].astype(F32))
            for b in range(nb):
                rows = slice(b * SGU_BLOCK, (b + 1) * SGU_BLOCK)
                mixed = _dot(wm, vn[rows, :]) + b_ref[:, g:g + 1]
                ys_ref[rows, gc] = (u[rows, :] * mixed).astype(BF16)
        zg = _dot(yg_ref[...], wbg_ref[...])
        zs = _dot(ys_ref[...], wbs_ref[...])
        zg_ref[...] = zg.astype(BF16)
        zs_ref[...] = zs.astype(BF16)
        merged = (_sigmoid(gg_ref[...].astype(F32)) * zg + _sigmoid(gs_ref[...].astype(F32)) * zs).astype(BF16)
        mg_ref[...] = merged
        mix = _dot(merged, wo_ref[...])
        mix_ref[...] = mix.astype(BF16)
        x1_ref[...] = x_ref[...] + mix * _rms_stats(mix) * g_ref[...]

    row = pl.BlockSpec((tT, D_MODEL), lambda i: (i, 0))
    blk = lambda j: pl.BlockSpec((tT, 1024), lambda i: (i, j))
    sds = lambda dt: _sds((T, D_MODEL), dt)
    return _call(body, name="sgu_merge_fwd", grid=(T // tT,), parallel=True,
                 in_specs=[row, blk(3), blk(4), blk(5), blk(6), row] + [_whole()] * 7
                 + [pl.BlockSpec((1, D_MODEL), lambda i: (0, 0))],
                 out_specs=[row] * 6, out_shape=[sds(BF16)] * 5 + [sds(F32)],
                 args=(x, proj, proj, proj, proj, y_gla, ln_g, ln_b, w_sp, b_sp_t, w_bg, w_bs, w_o, g_pm), job=job)


def _ffn_fwd_bwd(x1, tgt, w_fi_top, w_fi_bot, w_fo, g_pf, g_po):
    T = x1.shape[0]
    tT = _row_tile(T, 256)
    half = D_FF // 2
    kh = D_MODEL // 2

    def body(x1_ref, t_ref, top_hbm, bot_hbm, wfo_hbm, gpf_ref, gpo_ref,
             h_ref, f_ref, dgu_ref, dy_ref, dx1_ref, loss_ref, dgpf_ref, dgpo_ref, gu_scr, top_ref, bot_ref, wfo_ref, w_sems):
        pairs = []
        for p in range(2):
            pairs += [(top_hbm.at[p], top_ref.at[p]), (bot_hbm.at[p], bot_ref.at[p]),
                      (top_hbm.at[2 + p], top_ref.at[2 + p]), (bot_hbm.at[2 + p], bot_ref.at[2 + p]),
                      (wfo_hbm.at[pl.ds(p * half, half)], wfo_ref.at[pl.ds(p * half, half)])]
        ready = _fetches(pairs, w_sems)

        @pl.when(pl.program_id(0) == 0)
        def _():
            loss_ref[...] = jnp.zeros_like(loss_ref)
            dgpf_ref[...] = jnp.zeros_like(dgpf_ref)
            dgpo_ref[...] = jnp.zeros_like(dgpo_ref)

        main = (half // 256) * 256
        pieces = (0, 1, None)

        def w_in_cols(ref, first_slab, p):
            if p is not None:
                return ref[first_slab + p, :, :main]
            return jnp.concatenate([ref[first_slab, :, main:], ref[first_slab + 1, :, main:]], axis=1)

        def w_out_rows(p):
            if p is not None:
                return wfo_ref[p * half:p * half + main, :]
            return jnp.concatenate([wfo_ref[main:half, :], wfo_ref[half + main:2 * half, :]], axis=0)

        def put(ref, base, p, val):
            if p is not None:
                ref[:, base + p * half:base + p * half + main] = val
            else:
                ref[:, base + main:base + half] = val[:, :half - main]
                ref[:, base + half + main:base + 2 * half] = val[:, half - main:]

        def get(ref, base, p):
            if p is not None:
                return ref[:, base + p * half:base + p * half + main]
            return jnp.concatenate([ref[:, base + main:base + half], ref[:, base + half + main:base + 2 * half]], axis=1)

        x1v = x1_ref[...]
        r2 = _rms_stats(x1v)
        h = (x1v * r2 * gpf_ref[...]).astype(BF16)
        h_ref[...] = h
        y = jnp.zeros((tT, D_MODEL), F32)
        for p in pieces:
            arrived = (lambda k: ready(5 * p + k)) if p is not None else (lambda k: None)
            arrived(0), arrived(1)
            gate = _dot(h[:, :kh], w_in_cols(top_ref, 0, p)) + _dot(h[:, kh:], w_in_cols(bot_ref, 0, p))
            arrived(2), arrived(3)
            up = _dot(h[:, :kh], w_in_cols(top_ref, 2, p)) + _dot(h[:, kh:], w_in_cols(bot_ref, 2, p))
            put(gu_scr, 0, p, gate)
            put(gu_scr, D_FF, p, up)
            f = (gate * _sigmoid(gate) * up).astype(BF16)
            put(f_ref, 0, p, f)
            arrived(4)
            y = y + _dot(f, w_out_rows(p))
        r3 = _rms_stats(y)
        x2 = x1v + y * r3 * gpo_ref[...]
        err = x2 - t_ref[...]
        loss_ref[...] += jnp.sum(err * err) * (0.5 / D_MODEL)
        dx2 = err * (1.0 / D_MODEL)
        dy, dg = _rms_bwd(dx2, y, r3, gpo_ref[...])
        dgpo_ref[...] += jnp.sum(dg, axis=0, keepdims=True)
        dyb = dy.astype(BF16)
        dy_ref[...] = dyb
        dh_top = jnp.zeros((tT, kh), F32)
        dh_bot = jnp.zeros((tT, kh), F32)
        for p in pieces:
            df = _dot(dyb, w_out_rows(p), _NT)
            gate = get(gu_scr, 0, p)
            up = get(gu_scr, D_FF, p)
            sg = _sigmoid(gate)
            dgate = (df * up * (sg * (1.0 + gate * (1.0 - sg)))).astype(BF16)
            dup = (df * (gate * sg)).astype(BF16)
            put(dgu_ref, 0, p, dgate)
            put(dgu_ref, D_FF, p, dup)
            dh_top = dh_top + _dot(dgate, w_in_cols(top_ref, 0, p), _NT) + _dot(dup, w_in_cols(top_ref, 2, p), _NT)
            dh_bot = dh_bot + _dot(dgate, w_in_cols(bot_ref, 0, p), _NT) + _dot(dup, w_in_cols(bot_ref, 2, p), _NT)
        dh = jnp.concatenate([dh_top, dh_bot], axis=1)
        dx1n, dg2 = _rms_bwd(dh, x1v, r2, gpf_ref[...])
        dgpf_ref[...] += jnp.sum(dg2, axis=0, keepdims=True)
        dx1_ref[...] = dx2 + dx1n

    row = lambda w: pl.BlockSpec((tT, w), lambda i: (i, 0))
    vec = pl.BlockSpec((1, D_MODEL), lambda i: (0, 0))
    hbm = pl.BlockSpec(memory_space=pl.ANY)
    res, _ = _call(
        body, name="ffn_fwd_bwd", grid=(T // tT,),
        in_specs=[row(D_MODEL), row(D_MODEL), hbm, hbm, hbm, vec, vec],
        out_specs=[row(D_MODEL), row(D_FF), row(2 * D_FF), row(D_MODEL), row(D_MODEL),
                   pl.BlockSpec((1, LANES), lambda i: (0, 0)), vec, vec],
        out_shape=[_sds((T, D_MODEL), BF16), _sds((T, D_FF), BF16), _sds((T, 2 * D_FF), BF16), _sds((T, D_MODEL), BF16),
                   _sds((T, D_MODEL), F32), _sds((1, LANES), F32), _sds((1, D_MODEL), F32), _sds((1, D_MODEL), F32)],
        scratch_shapes=[pltpu.VMEM((tT, 2 * D_FF), F32), pltpu.VMEM(w_fi_top.shape, BF16), pltpu.VMEM(w_fi_bot.shape, BF16),
                        pltpu.VMEM(w_fo.shape, BF16), pltpu.SemaphoreType.DMA((10,))],
        args=(x1, tgt, w_fi_top, w_fi_bot, w_fo, g_pf, g_po))
    return res


def _merge_sgu_bwd(dx1, mix, proj, zg, zs, w_bg, w_bs, w_o, g_pm, ln_g, ln_b, w_sp, b_sp_t, job=None):
    T = dx1.shape[0]
    tT = _row_tile(T, 256)
    nb = tT // SGU_BLOCK

    def body(dx1_ref, mix_ref, su_ref, sv_ref, gg_ref, gs_ref, zg_ref, zs_ref, wbg_ref, wbs_ref, wo_ref, g_ref,
             lg_ref, lb_ref, w_ref, b_ref,
             dmix_ref, dzg_ref, dzs_ref, dgate_ref, dyg_ref, dp_ref, dgpm_ref, dw_ref, dbt_ref, dlg_ref, dlb_ref):
        @pl.when(pl.program_id(0) == 0)
        def _():
            for ref in (dgpm_ref, dw_ref, dbt_ref, dlg_ref, dlb_ref):
                ref[...] = jnp.zeros_like(ref)

        mix = mix_ref[...].astype(F32)
        dmix, dg = _rms_bwd(dx1_ref[...], mix, _rms_stats(mix), g_ref[...])
        dgpm_ref[...] += jnp.sum(dg, axis=0, keepdims=True)
        dmb = dmix.astype(BF16)
        dmix_ref[...] = dmb
        dmerged = _dot(dmb, wo_ref[...], _NT)
        dys = None
        for k, (gate_ref, z_ref, w_br_ref, dz_ref) in enumerate(((gg_ref, zg_ref, wbg_ref, dzg_ref),
                                                                 (gs_ref, zs_ref, wbs_ref, dzs_ref))):
            sg = _sigmoid(gate_ref[...].astype(F32))
            dz = (dmerged * sg).astype(BF16)
            dz_ref[...] = dz
            dgate_ref[:, k * 1024:(k + 1) * 1024] = (dmerged * z_ref[...].astype(F32) * (sg * (1.0 - sg))).astype(BF16)
            dy_branch = _dot(dz, w_br_ref[...], _NT)
            if k == 0:
                dyg_ref[...] = dy_branch.astype(BF16)
            else:
                dys = dy_branch

        mask = _sgu_mask()
        lane = lax.broadcasted_iota(jnp.int32, (SGU_BLOCK, LANES), 1)
        for g in range(SGU_GROUPS):
            gc = slice(g * SGU_DG, (g + 1) * SGU_DG)
            gc_v = slice(1024 + g * SGU_DG, 1024 + (g + 1) * SGU_DG)
            wm = jnp.where(mask, w_ref[g], 0.0).astype(BF16)
            vf, dvf_dsv = _gelu_and_grad(sv_ref[:, gc].astype(F32))
            mu = jnp.mean(vf, axis=-1, keepdims=True)
            vc = vf - mu
            rstd = lax.rsqrt(jnp.mean(vc * vc, axis=-1, keepdims=True) + EPS)
            vhat = vc * rstd
            vn = (vhat * lg_ref[:, gc] + lb_ref[:, gc]).astype(BF16)
            u, du_dsu = _gelu_and_grad(su_ref[:, gc].astype(F32))
            dy = dys[:, gc]
            dmixed = (dy * u).astype(BF16)
            dvn_parts = []
            dw_acc = jnp.zeros((SGU_BLOCK, SGU_BLOCK), F32)
            db_acc = jnp.zeros((SGU_BLOCK, 1), F32)
            for b in range(nb):
                rows = slice(b * SGU_BLOCK, (b + 1) * SGU_BLOCK)
                mixed = _dot(wm, vn[rows, :]) + b_ref[:, g:g + 1]
                dp_ref[rows, gc] = (dy[rows, :] * mixed * du_dsu[rows, :]).astype(BF16)
                dvn_parts.append(_dot(wm, dmixed[rows, :], _TN))
                dw_acc = dw_acc + _dot(dmixed[rows, :], vn[rows, :], _NT)
                db_acc = db_acc + jnp.sum(dmixed[rows, :].astype(F32), axis=-1, keepdims=True)
            dw_ref[g] += jnp.where(mask, dw_acc, 0.0)
            dbt_ref[...] += jnp.where(lane == g, db_acc, 0.0)
            dvn = jnp.concatenate(dvn_parts, axis=0)
            dlg_ref[:, gc] += jnp.sum(dvn * vhat, axis=0, keepdims=True)
            dlb_ref[:, gc] += jnp.sum(dvn, axis=0, keepdims=True)
            dvh = dvn * lg_ref[:, gc]
            dvf = rstd * (dvh - jnp.mean(dvh, axis=-1, keepdims=True)
                          - vhat * jnp.mean(dvh * vhat, axis=-1, keepdims=True))
            dp_ref[:, gc_v] = (dvf * dvf_dsv).astype(BF16)

    row = pl.BlockSpec((tT, D_MODEL), lambda i: (i, 0))
    blk = lambda j: pl.BlockSpec((tT, 1024), lambda i: (i, j))
    vec = pl.BlockSpec((1, D_MODEL), lambda i: (0, 0))
    wide = lambda w: pl.BlockSpec((tT, w), lambda i: (i, 0))
    sds = _sds((T, D_MODEL), BF16)
    return _call(
        body, name="merge_sgu_bwd", grid=(T // tT,),
        in_specs=[row, row, blk(3), blk(4), blk(5), blk(6), row, row] + [_whole()] * 3 + [vec] + [_whole()] * 4,
        out_specs=[row, row, row, wide(W_MRG), row, wide(W_SGU), vec,
                   pl.BlockSpec((SGU_GROUPS, SGU_BLOCK, SGU_BLOCK), lambda i: (0, 0, 0)),
                   pl.BlockSpec((SGU_BLOCK, LANES), lambda i: (0, 0)), vec, vec],
        out_shape=[sds, sds, sds, _sds((T, W_MRG), BF16), sds, _sds((T, W_SGU), BF16), _sds((1, D_MODEL), F32),
                   _sds((SGU_GROUPS, SGU_BLOCK, SGU_BLOCK), F32), _sds((SGU_BLOCK, LANES), F32),
                   _sds((1, 1024), F32), _sds((1, 1024), F32)],
        args=(dx1, mix, proj, proj, proj, proj, zg, zs, w_bg, w_bs, w_o, g_pm, ln_g, ln_b, w_sp, b_sp_t), job=job)


def _gla_bwd(proj, alow, wgu, b_gate, gn, states, dy_gla, job=None):
    T = proj.shape[0]
    tT = _row_tile(T, 512)
    nc = tT // CHUNK
    nt = T // tT

    def body(q_ref, k_ref, v_ref, r_ref, al_ref, wgu_ref, bg_ref, gn_ref, later_ref, earlier_ref, st_ref, sp_ref, dy_ref,
             dp_ref, dal_ref, dgn_ref, dbg_ref, dwgu_ref, g_scr, dd_scr, dt_scr):
        step = pl.program_id(0)

        @pl.when(step == 0)
        def _():
            g_scr[...] = jnp.zeros_like(g_scr)
            dgn_ref[...] = jnp.zeros_like(dgn_ref)
            dbg_ref[...] = jnp.zeros_like(dbg_ref)
            dwgu_ref[...] = jnp.zeros_like(dwgu_ref)

        has_prev = jnp.where(step == nt - 1, 0.0, 1.0)
        logit, la, delta = _gla_decay_terms(al_ref, wgu_ref, bg_ref, later_ref)
        e = jnp.exp(delta)
        kdec_f = k_ref[...].astype(F32) * e
        kdec = kdec_f.astype(BF16)
        heads = range(GLA_HEADS)
        kcs = [slice(h * GLA_DK, (h + 1) * GLA_DK) for h in heads]
        vcs = [slice(h * GLA_DV, (h + 1) * GLA_DV) for h in heads]
        carry = [g_scr[h] for h in heads]
        dgn_acc = [jnp.zeros((1, GLA_DV), F32) for _ in heads]
        for c in reversed(range(nc)):
            rows = slice(c * CHUNK, (c + 1) * CHUNK)
            first = slice(c * CHUNK, c * CHUNK + 1)
            dec = jnp.exp(la[first, :] + delta[first, :])
            s_b = [st_ref[c, h].astype(BF16) for h in heads]
            qs = [(q_ref[rows, kcs[h]].astype(F32) * (GLA_DK ** -0.5)).astype(BF16) for h in heads]
            o = [_dot(qs[h], s_b[h], _NT) for h in heads]
            do = []
            for h in heads:
                rstd = _rms_stats(o[h])
                ohat = o[h] * rstd
                gnh = gn_ref[:, vcs[h]]
                dy = dy_ref[rows, vcs[h]].astype(F32)
                rr = r_ref[rows, vcs[h]].astype(F32)
                sg = _sigmoid(rr)
                don = dy * (rr * sg)
                dp_ref[rows, OFF_R + h * GLA_DV:OFF_R + (h + 1) * GLA_DV] = (
                    dy * (ohat * gnh) * (sg * (1.0 + rr * (1.0 - sg)))).astype(BF16)
                dgn_acc[h] = dgn_acc[h] + jnp.sum(don * ohat, axis=0, keepdims=True)
                dn = don * gnh
                do.append((rstd * (dn - ohat * jnp.mean(dn * ohat, axis=-1, keepdims=True))).astype(BF16))
            dq = [_dot(do[h], s_b[h]) for h in heads]
            g_t = [_dot(do[h], qs[h], _TN) + carry[h] for h in heads]
            g_b = [g_t[h].astype(BF16) for h in heads]
            dv = [_dot(kdec[rows, kcs[h]], g_b[h], _NT) for h in heads]
            dkdec = [_dot(v_ref[rows, vcs[h]], g_b[h]) for h in heads]
            for h in heads:
                s_prev = st_ref[c - 1, h] if c > 0 else sp_ref[0, h] * has_prev
                ddec = jnp.sum(g_t[h] * s_prev, axis=0, keepdims=True)
                carry[h] = g_t[h] * dec[:, kcs[h]]
                dp_ref[rows, OFF_Q + h * GLA_DK:OFF_Q + (h + 1) * GLA_DK] = (dq[h] * (GLA_DK ** -0.5)).astype(BF16)
                dp_ref[rows, OFF_V + h * GLA_DV:OFF_V + (h + 1) * GLA_DV] = dv[h].astype(BF16)
                dp_ref[rows, OFF_K + h * GLA_DK:OFF_K + (h + 1) * GLA_DK] = (dkdec[h] * e[rows, kcs[h]]).astype(BF16)
                dd_scr[rows, kcs[h]] = dkdec[h] * kdec_f[rows, kcs[h]]
                dt_scr[rows, kcs[h]] = jnp.broadcast_to(ddec * dec[:, kcs[h]], (CHUNK, GLA_DK))
        for h in heads:
            g_scr[h] = carry[h]
            dgn_ref[:, vcs[h]] += dgn_acc[h]
        dla = _dot_exact_lhs(earlier_ref[...], dd_scr[...]) + dt_scr[...]
        dlogit = dla * (1.0 / GLA_TAU) * _sigmoid(-logit)
        dbg_ref[...] += jnp.sum(dlogit, axis=0, keepdims=True)
        dwgu_ref[...] += _dot_bf16(al_ref[...], dlogit, _TN)
        dal_ref[...] = _dot_bf16(dlogit, wgu_ref[...], _NT).astype(BF16)

    rev = lambda i: nt - 1 - i
    blk = lambda w, j: pl.BlockSpec((tT, w), lambda i: (rev(i), j))
    st_blk = pl.BlockSpec((nc, GLA_HEADS, GLA_DV, GLA_DK), lambda i: (rev(i), 0, 0, 0))
    sp_blk = pl.BlockSpec((1, GLA_HEADS, GLA_DV, GLA_DK), lambda i: (jnp.maximum(rev(i) * nc - 1, 0), 0, 0, 0))
    return _call(
        body, name="gla_bwd", grid=(nt,),
        in_specs=[blk(512, 0), blk(512, 1), blk(1024, 1), blk(1024, 2), blk(LANES, 0)] + [_whole()] * 5
        + [st_blk, sp_blk, blk(GLA_V, 0)],
        out_specs=[blk(W_GLA, 0), blk(LANES, 0), pl.BlockSpec((1, GLA_V), lambda i: (0, 0)),
                   pl.BlockSpec((1, GLA_QK), lambda i: (0, 0)), pl.BlockSpec((LANES, GLA_QK), lambda i: (0, 0))],
        out_shape=[_sds((T, W_GLA), BF16), _sds((T, LANES), BF16), _sds((1, GLA_V), F32), _sds((1, GLA_QK), F32),
                   _sds((LANES, GLA_QK), F32)],
        scratch_shapes=[pltpu.VMEM((GLA_HEADS, GLA_DV, GLA_DK), F32), pltpu.VMEM((tT, GLA_QK), F32),
                        pltpu.VMEM((tT, GLA_QK), F32)],
        args=(proj, proj, proj, proj, alow, wgu, b_gate, gn, _chunk_masks(tT, upper=True), _chunk_masks(tT, upper=False),
              states, states, dy_gla), job=job)


def _inproj_bwd(x, dx1, g1, w_all, dparts, job=None):
    T = x.shape[0]
    tT = _row_tile(T, 512)
    offs = (0, W_GLA, W_GLA + W_SGU, N_MAIN)

    chunks = [(off, p.shape[1]) for off, p in zip(offs, dparts)]

    def body(x_ref, dx1_ref, g_ref, w_hbm, *rest):
        part_refs, (dx_ref, dg_ref, w_ref, w_sems) = rest[:len(offs)], rest[len(offs):]
        ready = _column_fetches(w_hbm, w_ref, w_sems, chunks)

        @pl.when(pl.program_id(0) == 0)
        def _():
            dg_ref[...] = jnp.zeros_like(dg_ref)

        da = jnp.zeros((tT, D_MODEL), F32)
        for k, (off, p_ref) in enumerate(zip(offs, part_refs)):
            ready(k)
            da = da + _dot(p_ref[...], w_ref[:, off:off + p_ref.shape[1]], _NT)
        xv = x_ref[...]
        dx, dg = _rms_bwd(da, xv, _rms_stats(xv), g_ref[...])
        dg_ref[...] += jnp.sum(dg, axis=0, keepdims=True)
        dx_ref[...] = dx1_ref[...] + dx

    row = lambda w: pl.BlockSpec((tT, w), lambda i: (i, 0))
    vec = pl.BlockSpec((1, D_MODEL), lambda i: (0, 0))
    return _call(
        body, name="inproj_bwd", grid=(T // tT,),
        in_specs=[row(D_MODEL), row(D_MODEL), vec, pl.BlockSpec(memory_space=pl.ANY)] + [row(p.shape[1]) for p in dparts],
        out_specs=[row(D_MODEL), vec], out_shape=[_sds((T, D_MODEL), F32), _sds((1, D_MODEL), F32)],
        scratch_shapes=[pltpu.VMEM(w_all.shape, BF16), pltpu.SemaphoreType.DMA((len(chunks),))],
        args=(x, dx1, g1, w_all, *dparts), job=job)


def _tn_matmul(a, b, name, job=None):
    T, M = a.shape
    N = b.shape[1]
    tk = _row_tile(T, 1024)
    tm = M if M <= 1024 else 1408
    tn = N if N <= 3072 else N // 2
    assert M % tm == 0 and N % tn == 0

    def body(a_ref, b_ref, o_ref):
        @pl.when(pl.program_id(2) == 0)
        def _():
            o_ref[...] = _dot(a_ref[...], b_ref[...], _TN)

        @pl.when(pl.program_id(2) > 0)
        def _():
            o_ref[...] += _dot(a_ref[...], b_ref[...], _TN)

    res, jres = _call(
        body, name=name, grid=(M // tm, N // tn, T // tk),
        in_specs=[pl.BlockSpec((tk, tm), lambda i, j, k: (k, i)), pl.BlockSpec((tk, tn), lambda i, j, k: (k, j))],
        out_specs=[pl.BlockSpec((tm, tn), lambda i, j, k: (i, j))], out_shape=[_sds((M, N), F32)], args=(a, b), job=job)
    return res[0], jres


def _pad_rows(a, rows=8):
    return jnp.pad(a, ((0, rows - a.shape[0]), (0, LANES - a.shape[1])))


def _halves_view(dw):
    r = dw.shape[0] // N_CHIPS
    return dw.reshape(N_CHIPS, 2, r // 2, dw.shape[1])


def kernel(x, norm_pre_mix, w_in, w_gate_up, b_gate, gla_norm, sgu_ln_g, sgu_ln_b, w_spatial, b_spatial, w_branch_gla, w_branch_sgu, w_out, norm_post_mix, norm_pre_ffn, w_ffn_in, w_ffn_out, norm_post_ffn, loss_target, m_norm_pre_mix, m_w_in, m_w_gate_up, m_b_gate, m_gla_norm, m_sgu_ln_g, m_sgu_ln_b, m_w_spatial, m_b_spatial, m_w_branch_gla, m_w_branch_sgu, m_w_out, m_norm_post_mix, m_norm_pre_ffn, m_w_ffn_in, m_w_ffn_out, m_norm_post_ffn, v_norm_pre_mix, v_w_in, v_w_gate_up, v_b_gate, v_gla_norm, v_sgu_ln_g, v_sgu_ln_b, v_w_spatial, v_b_spatial, v_w_branch_gla, v_w_branch_sgu, v_w_out, v_norm_post_mix, v_norm_pre_ffn, v_w_ffn_in, v_w_ffn_out, v_norm_post_ffn):
    chip = 2 * lax.axis_index("x") + lax.axis_index("y")
    xt, tgt = x[0], loss_target[0]

    tiny = jnp.concatenate([w_gate_up[0], _pad_rows(gla_norm[0]), _pad_rows(sgu_ln_g[0]), _pad_rows(sgu_ln_b[0]),
                            jnp.zeros((24, LANES), F32)], axis=0)

    def with_own(gathered, own):
        return lax.dynamic_update_slice(gathered, own[None], (chip, 0, 0))

    w_in_t, m_in_t, v_in_t = w_in[0].T, m_w_in[0].T, v_w_in[0].T
    w_in_b = _transposed_cast(w_in_t)
    (*own_rows, fi_top, fi_bot), (g_in, g_tiny) = _cast_weights(
        [w_branch_gla[0], w_branch_sgu[0], w_out[0], w_ffn_out[0]], w_ffn_in[0], job=_job_gather([w_in_b, tiny]))
    g_tiny = with_own(g_tiny, tiny)
    w_all = _relayout_w_in(with_own(g_in, w_in_b))
    cols = lambda a: a.transpose(1, 0, 2).reshape(a.shape[1], N_CHIPS * a.shape[2])
    wgu = jnp.pad(cols(g_tiny[:, 0:16]), ((0, LANES - GLA_RANK), (0, 0)))
    gn = cols(g_tiny[:, 16:20, :64]).reshape(1, GLA_V)
    ln_g = cols(g_tiny[:, 24:28, :64]).reshape(1, 1024)
    ln_b = cols(g_tiny[:, 32:36, :64]).reshape(1, 1024)
    b_sp_t = jnp.pad(b_spatial[0].T, ((0, 0), (0, LANES - SGU_GROUPS)))
    w_sp = w_spatial[0]

    (a, proj, alow), g_rows = _inproj_fwd(xt, norm_pre_mix, w_all, job=_job_gather(own_rows))
    rows = lambda g: g.reshape(N_CHIPS * g.shape[1], g.shape[2])
    w_bg, w_bs, w_o, w_fo = [rows(with_own(g, own)) for g, own in zip(g_rows, own_rows)]
    (y_gla, states), (g_top,) = _gla_fwd(proj, alow, wgu, b_gate, gn, job=_job_gather([fi_top]))
    (y_sgu, zg, zs, merged, mix, x1), (g_bot,) = _sgu_merge_fwd(
        xt, proj, y_gla, ln_g, ln_b, w_sp, b_sp_t, w_bg, w_bs, w_o, norm_post_mix, job=_job_gather([fi_bot]))
    h, f, dgu, dy, dx1, loss, d_gpf, d_gpo = _ffn_fwd_bwd(x1, tgt, with_own(g_top, fi_top), with_own(g_bot, fi_bot),
                                                          w_fo, norm_pre_ffn, norm_post_ffn)

    own_part = lambda c: lax.dynamic_index_in_dim(c, chip, 0, keepdims=False)
    whole = lambda hs: [[(h_, None)] for h_ in hs]
    dw_fo, _ = _tn_matmul(f, dy, "dw_ffn_out")
    dw_fo4 = _halves_view(dw_fo)
    dw_fi, (q_fo,) = _tn_matmul(h, dgu, "dw_ffn_in", job=_job_to_other_core([[(dw_fo4, 0)]]))
    c_fo = _presum(dw_fo4, q_fo, "presum_ffn_out")
    (dmix, dzg, dzs, dp_mrg, dyg, dp_sgu, d_gpm, d_wsp, d_bsp_t, d_lng, d_lnb), (s_fo, q_fi) = _merge_sgu_bwd(
        dx1, mix, proj, zg, zs, w_bg, w_bs, w_o, norm_post_mix, ln_g, ln_b, w_sp, b_sp_t,
        job=_join(_job_scatter([c_fo]), _job_to_other_core([[(dw_fi, 0)]])))
    c_fi = _presum(dw_fi, q_fi, "presum_ffn_in")
    dw_c, _ = _tn_matmul(a, dp_mrg, "dw_in_merge")
    dw_b, _ = _tn_matmul(a, dp_sgu, "dw_in_sgu")
    dw_o4 = _halves_view(_tn_matmul(merged, dmix, "dw_out")[0])
    dw_bg4 = _halves_view(_tn_matmul(y_gla, dzg, "dw_branch_gla")[0])
    dw_bs4 = _halves_view(_tn_matmul(y_sgu, dzs, "dw_branch_sgu")[0])
    h_fo = _sum_slots(own_part(c_fo), s_fo, "sum_ffn_out")
    (dp_gla, dal, d_gn, d_bg, d_wgu), (s_fi, t_fo, q_b, q_c, q_o, q_bg, q_bs) = _gla_bwd(
        proj, alow, wgu, b_gate, gn, states, dyg,
        job=_join(_job_scatter([c_fi]), _job_to_other_core(
            whole([h_fo]) + [[(dw_b, 0)], [(dw_c, 0)], [(dw_o4, 0)], [(dw_bg4, 0)], [(dw_bs4, 0)]])))
    c_o, c_bg, c_bs = (_presum(dw_o4, q_o, "presum_out"), _presum(dw_bg4, q_bg, "presum_branch_gla"),
                       _presum(dw_bs4, q_bs, "presum_branch_sgu"))
    h_fi = _sum_slots(own_part(c_fi), s_fi, "sum_ffn_in")
    dw_d, _ = _tn_matmul(a, dal, "dw_in_gate")
    dw_a, (s_o, s_bg, s_bs, t_fi, q_d) = _tn_matmul(
        a, dp_gla, "dw_in_gla",
        job=_join(_job_scatter([c_o, c_bg, c_bs]), _job_to_other_core(whole([h_fi]) + [[(dw_d, 0)]])))
    h_o, h_bg, h_bs = (_sum_slots(own_part(c_o), s_o, "sum_out"), _sum_slots(own_part(c_bg), s_bg, "sum_branch_gla"),
                       _sum_slots(own_part(c_bs), s_bs, "sum_branch_sgu"))

    grads, deltas, new_m, new_v = {}, {}, {}, {}

    def update(name, w, m, v, g_mine, g_theirs, job=None):
        (g, d, m2, v2), jres = _adamw(w[0], m[0], v[0], g_mine, g_theirs, "adamw_" + name, job=job)
        grads[name], deltas[name], new_m[name], new_v[name] = g[None], d[None], m2[None], v2[None]
        return jres

    dw_in = [(dw_a, 0), (dw_b, W_GLA), (dw_c, W_GLA + W_SGU), (dw_d, N_MAIN)]
    q_a, t_o, t_bg, t_bs = update("w_ffn_out", w_ffn_out, m_w_ffn_out, v_w_ffn_out, [h_fo], [t_fo],
                                  job=_job_to_other_core([[(dw_a, 0)]] + whole([h_o, h_bg, h_bs])))
    q_in = [q_a, q_b, q_c, q_d]
    hr_in = D_MODEL // 2
    c_in_a, _ = _presum_w_in(dw_in, q_in, 0, hr_in // 8, "presum_w_in_a")
    c_in_b, (s_in_a,) = _presum_w_in(dw_in, q_in, hr_in // 8, 7 * hr_in // 8, "presum_w_in_b",
                                     job=_job_scatter([c_in_a]))
    update("w_ffn_in", w_ffn_in, m_w_ffn_in, v_w_ffn_in, [h_fi], [t_fi])
    update("w_out", w_out, m_w_out, v_w_out, [h_o], [t_o])
    update("w_branch_gla", w_branch_gla, m_w_branch_gla, v_w_branch_gla, [h_bg], [t_bg])
    update("w_branch_sgu", w_branch_sgu, m_w_branch_sgu, v_w_branch_sgu, [h_bs], [t_bs])
    (grad_x, d_g1), (s_in_b,) = _inproj_bwd(xt, dx1, norm_pre_mix, w_all, (dp_gla, dp_sgu, dp_mrg, dal),
                                            job=_job_scatter([c_in_b]))
    h_in = [_sum_slots(own_part(c_in_a), s_in_a, "sum_w_in_a"), _sum_slots(own_part(c_in_b), s_in_b, "sum_w_in_b")]
    t_in = _run_job(_job_to_other_core(whole(h_in)), "swap_w_in")
    for store, val in zip((grads, deltas, new_m, new_v),
                          _adamw_transposed(w_in_t, m_in_t, v_in_t, h_in, t_in, "adamw_w_in")):
        store["w_in"] = val.T[None]

    small_names = ["w_spatial", "w_gate_up", "norm_pre_mix", "norm_post_mix", "norm_pre_ffn", "norm_post_ffn", "b_gate",
                   "b_spatial", "gla_norm", "sgu_ln_g", "sgu_ln_b"]
    loss_out, small = _small_adamw(
        _small_sum([d_wsp, d_wgu, d_g1, d_gpm, d_gpf, d_gpo, d_bg, d_bsp_t, d_gn, d_lng, d_lnb, loss]),
        [w_spatial, w_gate_up, norm_pre_mix, norm_post_mix, norm_pre_ffn, norm_post_ffn, b_gate, b_spatial, gla_norm,
         sgu_ln_g, sgu_ln_b],
        [m_w_spatial, m_w_gate_up, m_norm_pre_mix, m_norm_post_mix, m_norm_pre_ffn, m_norm_post_ffn, m_b_gate,
         m_b_spatial, m_gla_norm, m_sgu_ln_g, m_sgu_ln_b],
        [v_w_spatial, v_w_gate_up, v_norm_pre_mix, v_norm_post_mix, v_norm_pre_ffn, v_norm_post_ffn, v_b_gate,
         v_b_spatial, v_gla_norm, v_sgu_ln_g, v_sgu_ln_b])
    for store, vals in zip((grads, deltas, new_m, new_v), small):
        store.update(zip(small_names, vals))

    order = ["norm_pre_mix", "w_in", "w_gate_up", "b_gate", "gla_norm", "sgu_ln_g", "sgu_ln_b", "w_spatial", "b_spatial",
             "w_branch_gla", "w_branch_sgu", "w_out", "norm_post_mix", "norm_pre_ffn", "w_ffn_in", "w_ffn_out",
             "norm_post_ffn"]
    out = [loss_out, grad_x[None]]
    for store in (grads, deltas, new_m, new_v):
        out.extend(store[n] for n in order)
    return tuple(out)
```

```python
import jax
import jax.numpy as jnp
from jax import lax
from jax.experimental import pallas as pl
from jax.experimental.pallas import tpu as pltpu

F32 = jnp.float32
BF16 = jnp.bfloat16

D_MODEL = 1024
GLA_HEADS = 4
GLA_DK = 128
GLA_DV = 256
GLA_QK = GLA_HEADS * GLA_DK
GLA_V = GLA_HEADS * GLA_DV
GLA_RANK = 16
GLA_TAU = 16.0
CHUNK = 64
SGU_GROUPS = 4
SGU_BLOCK = 128
SGU_DG = 256
D_FF = 2816
EPS = 1e-6
LANES = 128

OFF_Q, OFF_K, OFF_V, OFF_R, OFF_SU, OFF_SV, OFF_GG, OFF_GS, OFF_AL = 0, 512, 1024, 2048, 3072, 4096, 5120, 6144, 7168
W_GLA, W_SGU, W_MRG = 3072, 2048, 2048
N_MAIN = 7168
N_ALL = N_MAIN + LANES
_IN_SPLITS = (GLA_QK, GLA_QK, GLA_V, GLA_V, GLA_RANK, 1024, 1024, 1024, 1024)
_IN_STARTS = tuple(sum(_IN_SPLITS[:i]) for i in range(len(_IN_SPLITS) + 1))
_IN_DST = (OFF_Q, OFF_K, OFF_V, OFF_R, OFF_AL, OFF_SU, OFF_SV, OFF_GG, OFF_GS)
D_IN = _IN_STARTS[-1]

ADAM_LR = 0.001
ADAM_B1 = 0.9
ADAM_B2 = 0.999
ADAM_EPS = 1e-08
ADAM_WD = 0.01
ADAM_STEP = 10

VMEM_LIMIT_BYTES = 56 * 1024 * 1024
N_CHIPS = 4
N_PEER = N_CHIPS - 1
N_DEV = 8
MESH = pl.DeviceIdType.MESH

_NN = (((1,), (0,)), ((), ()))
_NT = (((1,), (1,)), ((), ()))
_TN = (((0,), (0,)), ((), ()))


def _dot(a, b, dims=_NN):
    return lax.dot_general(a, b, dims, preferred_element_type=F32)


def _split(x):
    hi = x.astype(BF16)
    lo = (x - hi.astype(F32)).astype(BF16)
    return hi, lo


def _dot_bf16(a, b, dims=_NN):
    return _dot(a.astype(BF16), b.astype(BF16), dims)


def _dot_exact_lhs(m, x):
    xh, xl = _split(x)
    return _dot(m, xh) + _dot(m, xl)


def _sigmoid(x):
    return 0.5 * jnp.tanh(0.5 * x) + 0.5


def _log_sigmoid(x):
    return jnp.minimum(x, 0.0) - jnp.log(1.0 + jnp.exp(-jnp.abs(x)))


_GELU_C = 0.7978845608028654
_GELU_A = 0.044715


def _gelu_and_grad(x):
    x2 = x * x
    t = jnp.tanh(_GELU_C * (x + _GELU_A * x * x2))
    g = 0.5 * x * (1.0 + t)
    dg = 0.5 * (1.0 + t) + 0.5 * x * (1.0 - t * t) * (_GELU_C * (1.0 + 3.0 * _GELU_A * x2))
    return g, dg


def _gelu(x):
    t = jnp.tanh(_GELU_C * (x + _GELU_A * x * x * x))
    return 0.5 * x * (1.0 + t)


def _rms_stats(x):
    return lax.rsqrt(jnp.mean(x * x, axis=-1, keepdims=True) + EPS)


def _rms_bwd(dout, y, r, g):
    yhat = y * r
    dn = dout * g
    dy = r * (dn - yhat * jnp.mean(dn * yhat, axis=-1, keepdims=True))
    return dy, dout * yhat


def _whole():
    return pl.BlockSpec(memory_space=pltpu.VMEM)


def _row_tile(T, want):
    t = min(T, want)
    assert T % t == 0
    return t


def _chunk_masks(tT, upper):
    row = lax.broadcasted_iota(jnp.int32, (tT, tT), 0)
    col = lax.broadcasted_iota(jnp.int32, (tT, tT), 1)
    same = (row // CHUNK) == (col // CHUNK)
    tri = (col > row) if upper else (col < row)
    return jnp.where(same & tri, 1.0, 0.0).astype(BF16)


class _Job:
    def __init__(self, ins, out_shapes, scratch, start, finish, mid=None):
        self.ins, self.out_shapes, self.scratch = list(ins), list(out_shapes), list(scratch)
        self.start, self.finish, self.mid = start, finish, mid


def _join(*jobs):
    def split(refs, counts):
        out, at = [], 0
        for n in counts:
            out.append(refs[at:at + n])
            at += n
        return out

    ni, no, ns = [len(j.ins) for j in jobs], [len(j.out_shapes) for j in jobs], [len(j.scratch) for j in jobs]

    def start(ins, outs, scr):
        for j, a, b, c in zip(jobs, split(ins, ni), split(outs, no), split(scr, ns)):
            j.start(a, b, c)

    def finish(ins, outs, scr):
        for j, a, b, c in zip(jobs, split(ins, ni), split(outs, no), split(scr, ns)):
            j.finish(a, b, c)

    def mid(ins, outs, scr):
        for j, a, b, c in zip(jobs, split(ins, ni), split(outs, no), split(scr, ns)):
            if j.mid is not None:
                j.mid(a, b, c)

    return _Job(sum((j.ins for j in jobs), []), sum((j.out_shapes for j in jobs), []),
                sum((j.scratch for j in jobs), []), start, finish, mid if any(j.mid for j in jobs) else None)


def _mesh_pos():
    return lax.axis_index("x"), lax.axis_index("y"), lax.axis_index("c")


def _peer_chips(xi, yi):
    return [(1 - xi, yi), (xi, 1 - yi), (1 - xi, 1 - yi)]


def _half(ci, rows):
    return pl.ds(pl.multiple_of(ci * rows, 8), rows)


def _sds(shape, dtype):
    return jax.ShapeDtypeStruct(tuple(shape), dtype)


def _job_gather(arrs):
    n = len(arrs)
    kinds = 12
    Y0, Y1, X1, X0, ON_X, ON_Y, D2D = 0, 1, 2, 3, 4, 5, 6

    def copies(ins, outs, scr):
        send_sems, recv_sems = scr
        xi, yi, ci = _mesh_pos()
        me, cx, cy, cd = 2 * xi + yi, 2 * (1 - xi) + yi, 2 * xi + (1 - yi), 2 * (1 - xi) + (1 - yi)
        to_x, to_y, to_core = (1 - xi, yi, ci), (xi, 1 - yi, ci), (xi, yi, 1 - ci)
        table = []
        for k in range(n):
            qr = arrs[k].shape[0] // 4

            def rows(core, q):
                return pl.ds(pl.multiple_of((2 * core + q) * qr, 8), qr)

            def cp(kind, src, dst, to):
                s = k * kinds + kind
                return pltpu.make_async_remote_copy(src_ref=src, dst_ref=dst, send_sem=send_sems.at[s],
                                                    recv_sem=recv_sems.at[s], device_id=to, device_id_type=MESH)

            def slab(chip, core, q):
                return outs[k].at[chip, rows(core, q)]

            t = {}
            for kind, q, to, frm in ((Y0, 0, to_y, cy), (Y1, 1, to_y, cy), (X1, 1, to_x, cx), (X0, 0, to_x, cx)):
                mine = ins[k].at[rows(ci, q)]
                t[kind] = (cp(kind, mine, slab(me, ci, q), to), cp(kind, mine, slab(frm, ci, q), to))
            t[ON_X] = (cp(ON_X, slab(cy, ci, 0), slab(cy, ci, 0), to_x), cp(ON_X, slab(cy, ci, 0), slab(cd, ci, 0), to_x))
            t[ON_Y] = (cp(ON_Y, slab(cx, ci, 1), slab(cx, ci, 1), to_y), cp(ON_Y, slab(cx, ci, 1), slab(cd, ci, 1), to_y))
            for i, (chip, q) in enumerate(((cy, 0), (cy, 1), (cx, 1), (cx, 0), (cd, 0), (cd, 1))):
                t[D2D + i] = (cp(D2D + i, slab(chip, ci, q), slab(chip, ci, q), to_core),
                              cp(D2D + i, slab(chip, ci, q), slab(chip, 1 - ci, q), to_core))
            table.append(t)
        return table

    def start(ins, outs, scr):
        table = copies(ins, outs, scr)
        for kind in (Y0, X1, Y1, X0):
            for t in table:
                t[kind][0].start()

    def arrived(table, kind, then):
        for t in table:
            t[kind][1].wait_recv()
            for nxt in then:
                t[nxt][0].start()

    def mid(ins, outs, scr):
        table = copies(ins, outs, scr)
        arrived(table, Y0, (ON_X, D2D + 0))
        arrived(table, X1, (ON_Y, D2D + 2))

    def finish(ins, outs, scr):
        table = copies(ins, outs, scr)
        arrived(table, Y1, (D2D + 1,))
        arrived(table, X0, (D2D + 3,))
        arrived(table, ON_X, (D2D + 4,))
        arrived(table, ON_Y, (D2D + 5,))
        for t in table:
            for i in range(6):
                t[D2D + i][1].wait_recv()
            for kind in range(kinds):
                t[kind][0].wait_send()

    dma = pltpu.SemaphoreType.DMA
    return _Job(arrs, [_sds((N_CHIPS,) + a.shape, a.dtype) for a in arrs], [dma((n * kinds,))] * 2, start, finish, mid)


def _job_scatter(parts):
    n = len(parts)

    def copies(ins, outs, scr):
        send_sems, recv_sems = scr
        xi, yi, ci = _mesh_pos()
        res = []
        for k in range(n):
            for j, (px, py) in enumerate(_peer_chips(xi, yi)):
                s = k * N_PEER + j
                res.append(pltpu.make_async_remote_copy(
                    src_ref=ins[k].at[2 * px + py], dst_ref=outs[k].at[j], send_sem=send_sems.at[s],
                    recv_sem=recv_sems.at[s], device_id=(px, py, ci), device_id_type=MESH))
        return res

    def start(ins, outs, scr):
        for cp in copies(ins, outs, scr):
            cp.start()

    def finish(ins, outs, scr):
        for cp in copies(ins, outs, scr):
            cp.wait_recv()
            cp.wait_send()

    dma = pltpu.SemaphoreType.DMA
    return _Job(parts, [_sds((N_PEER,) + p.shape[1:], p.dtype) for p in parts], [dma((n * N_PEER,))] * 2, start, finish)


def _job_to_other_core(groups):
    pieces = [(g, a, off) for g, group in enumerate(groups) for a, off in group]
    n = len(pieces)

    def geometry(group):
        a0, off0 = group[0]
        if off0 is None:
            return a0.shape
        if a0.ndim == 4:
            return (N_CHIPS, a0.shape[2], a0.shape[3])
        return (a0.shape[0] // 2, sum(a.shape[1] for a, _ in group))

    def copies(ins, outs, scr):
        send_sems, recv_sems = scr
        xi, yi, ci = _mesh_pos()
        res = []
        for p, (g, a, off) in enumerate(pieces):
            if off is None:
                give, land = ins[p], outs[g]
            elif a.ndim == 4:
                give, land = ins[p].at[pl.ds(0, N_CHIPS), 1 - ci], outs[g]
            else:
                hr, w = a.shape[0] // 2, a.shape[1]
                give, land = ins[p].at[_half(1 - ci, hr)], outs[g].at[pl.ds(0, hr), pl.ds(off, w)]
            res.append(pltpu.make_async_remote_copy(
                src_ref=give, dst_ref=land, send_sem=send_sems.at[p], recv_sem=recv_sems.at[p],
                device_id=(xi, yi, 1 - ci), device_id_type=MESH))
        return res

    def start(ins, outs, scr):
        for cp in copies(ins, outs, scr):
            cp.start()

    def finish(ins, outs, scr):
        for cp in copies(ins, outs, scr):
            cp.wait_recv()
            cp.wait_send()

    dma = pltpu.SemaphoreType.DMA
    return _Job([a for _, a, _ in pieces], [_sds(geometry(group), group[0][0].dtype) for group in groups],
                [dma((n,))] * 2, start, finish)


def _call(body, *, name, grid, in_specs, out_specs, out_shape, args, scratch_shapes=(), parallel=False, job=None,
          by_core=False):
    n_in, n_out, n_scr = len(in_specs), len(out_specs), len(scratch_shapes)
    hbm = pl.BlockSpec(memory_space=pl.ANY)
    n_ji, n_jo = (len(job.ins), len(job.out_shapes)) if job is not None else (0, 0)
    lead = 1 if by_core else 0

    def kernel_fn(*refs):
        core, refs = refs[:lead], refs[lead:]
        ins, refs = refs[:n_in], refs[n_in:]
        j_ins, refs = refs[:n_ji], refs[n_ji:]
        outs, refs = refs[:n_out], refs[n_out:]
        j_outs, refs = refs[:n_jo], refs[n_jo:]
        scr, j_scr = refs[:n_scr], refs[n_scr:]
        if job is None:
            body(*core, *ins, *outs, *scr)
            return
        ids = [pl.program_id(d) for d in range(len(grid))]
        first = ids[0] == 0
        last = ids[0] == grid[0] - 1
        for d in range(1, len(grid)):
            first = first & (ids[d] == 0)
            last = last & (ids[d] == grid[d] - 1)

        @pl.when(first)
        def _():
            job.start(j_ins, j_outs, j_scr)

        if job.mid is not None and grid[0] >= 4:
            half_way = ids[0] == grid[0] // 2
            for d in range(1, len(grid)):
                half_way = half_way & (ids[d] == 0)

            @pl.when(half_way)
            def _():
                job.mid(j_ins, j_outs, j_scr)

        body(*core, *ins, *outs, *scr)

        @pl.when(last)
        def _():
            if job.mid is not None and grid[0] < 4:
                job.mid(j_ins, j_outs, j_scr)
            job.finish(j_ins, j_outs, j_scr)

    sem = ("parallel" if parallel and job is None else "arbitrary",) * len(grid)
    all_in = list(in_specs) + [hbm] * n_ji
    all_out = list(out_specs) + [hbm] * n_jo
    all_scratch = list(scratch_shapes) + (job.scratch if job is not None else [])
    all_shapes = list(out_shape) + (job.out_shapes if job is not None else [])
    all_args = list(args) + (job.ins if job is not None else [])
    params = pltpu.CompilerParams(dimension_semantics=sem, vmem_limit_bytes=VMEM_LIMIT_BYTES)
    if by_core:
        spec = pltpu.PrefetchScalarGridSpec(num_scalar_prefetch=1, grid=grid, in_specs=all_in, out_specs=all_out,
                                            scratch_shapes=all_scratch)
        core = lax.axis_index("c").astype(jnp.int32).reshape(1)
        res = pl.pallas_call(kernel_fn, name=name, grid_spec=spec, out_shape=all_shapes, compiler_params=params)(
            core, *all_args)
    else:
        res = pl.pallas_call(kernel_fn, name=name, grid=grid, in_specs=all_in, out_specs=all_out, out_shape=all_shapes,
                             scratch_shapes=all_scratch, compiler_params=params)(*all_args)
    return list(res[:n_out]), list(res[n_out:])


def _run_job(job, name):
    n_i, n_o = len(job.ins), len(job.out_shapes)

    def body(*refs):
        ins, outs, scr = refs[:n_i], refs[n_i:n_i + n_o], refs[n_i + n_o:]
        job.start(ins, outs, scr)
        if job.mid is not None:
            job.mid(ins, outs, scr)
        job.finish(ins, outs, scr)

    hbm = pl.BlockSpec(memory_space=pl.ANY)
    return list(pl.pallas_call(body, name=name, in_specs=[hbm] * n_i, out_specs=[hbm] * n_o, out_shape=job.out_shapes,
                               scratch_shapes=job.scratch)(*job.ins))


def _adam_values(w, m, v, g):
    m2 = ADAM_B1 * m + (1.0 - ADAM_B1) * g
    v2 = ADAM_B2 * v + (1.0 - ADAM_B2) * (g * g)
    delta = -ADAM_LR * ((m2 / (1.0 - ADAM_B1 ** ADAM_STEP)) / (jnp.sqrt(v2 / (1.0 - ADAM_B2 ** ADAM_STEP)) + ADAM_EPS)
                        + ADAM_WD * w)
    return delta, m2, v2


_P_WSP, _P_WGU, _P_NORM, _P_BG, _P_BSP, _P_HEAD, _P_LOSS, _P_ROWS = 0, 512, 576, 608, 616, 624, 720, 736


def _small_sum(dgrads):
    hr = _P_ROWS // 2

    def body(dwsp, dwgu, dg1, dgpm, dgpf, dgpo, dbg, dbspt, dgn, dlng, dlnb, loss_in, tot, pack, pair, slots, send_sems,
             recv_sems):
        xi, yi, ci = _mesh_pos()
        chip = 2 * xi + yi

        pack[...] = jnp.zeros_like(pack)
        for g in range(SGU_GROUPS):
            pack[_P_WSP + g * SGU_BLOCK:_P_WSP + (g + 1) * SGU_BLOCK] = dwsp[g]
        for j in range(N_CHIPS):
            pack[_P_WGU + GLA_RANK * j:_P_WGU + GLA_RANK * (j + 1)] = dwgu[0:GLA_RANK, LANES * j:LANES * (j + 1)]
        for k, r in enumerate((dg1, dgpm, dgpf, dgpo)):
            for q in range(8):
                pack[_P_NORM + 8 * k + q:_P_NORM + 8 * k + q + 1] = r[:, LANES * q:LANES * (q + 1)]
        for q in range(4):
            pack[_P_BG + q:_P_BG + q + 1] = dbg[:, LANES * q:LANES * (q + 1)]
        pack[_P_BSP:_P_BSP + SGU_GROUPS] = jnp.transpose(dbspt[...])[0:SGU_GROUPS]
        for k, r in enumerate((dgn, dlng, dlnb)):
            for j in range(N_CHIPS):
                for hh in range(4):
                    row = _P_HEAD + 32 * k + 8 * j + hh
                    pack[row:row + 1, 0:64] = r[:, 256 * hh + 64 * j:256 * hh + 64 * (j + 1)]
        pack[_P_LOSS:_P_LOSS + 1] = loss_in[...]

        sibling = dict(device_id=(xi, yi, 1 - ci), device_id_type=MESH)
        to_sibling = pltpu.make_async_remote_copy(src_ref=pack, dst_ref=pair, send_sem=send_sems.at[N_PEER],
                                                  recv_sem=recv_sems.at[N_PEER], **sibling)
        to_sibling.start()
        to_sibling.wait_recv()
        to_sibling.wait_send()
        pack[...] = pack[...] + pair[...]
        mine = pl.ds(pl.multiple_of(ci * hr, 8), hr)
        theirs = pl.ds(pl.multiple_of((1 - ci) * hr, 8), hr)
        slots[chip] = pack[mine, :]

        def copy(j, slot):
            px, py = _peer_chips(xi, yi)[j]
            return pltpu.make_async_remote_copy(
                src_ref=pack.at[mine], dst_ref=slots.at[slot(2 * px + py)], send_sem=send_sems.at[j],
                recv_sem=recv_sems.at[j], device_id=(px, py, ci), device_id_type=MESH)

        sends = [copy(j, lambda peer_chip: chip) for j in range(N_PEER)]
        for cp in sends:
            cp.start()
        for j in range(N_PEER):
            copy(j, lambda peer_chip: peer_chip).wait_recv()
        for cp in sends:
            cp.wait_send()
        acc = slots[0]
        for d in range(1, N_CHIPS):
            acc = acc + slots[d]
        tot[mine, :] = acc
        half_over = pltpu.make_async_remote_copy(src_ref=tot.at[mine], dst_ref=tot.at[mine], send_sem=send_sems.at[N_PEER + 1],
                                                 recv_sem=recv_sems.at[N_PEER + 1], **sibling)
        half_back = pltpu.make_async_remote_copy(src_ref=tot.at[mine], dst_ref=tot.at[theirs], send_sem=send_sems.at[N_PEER + 1],
                                                 recv_sem=recv_sems.at[N_PEER + 1], **sibling)
        half_over.start()
        half_back.wait_recv()
        half_over.wait_send()

    return pl.pallas_call(
        body, name="small_sum", in_specs=[_whole()] * 12, out_specs=_whole(), out_shape=_sds((_P_ROWS, LANES), F32),
        scratch_shapes=[pltpu.VMEM((_P_ROWS, LANES), F32), pltpu.VMEM((_P_ROWS, LANES), F32),
                        pltpu.VMEM((N_CHIPS, hr, LANES), F32),
                        pltpu.SemaphoreType.DMA((N_PEER + 2,)), pltpu.SemaphoreType.DMA((N_PEER + 2,))],
        compiler_params=pltpu.CompilerParams(vmem_limit_bytes=VMEM_LIMIT_BYTES),
    )(*dgrads)


def _small_adamw(tot, ws, ms, vs):
    n = len(ws)

    def body(*refs):
        tot = refs[0]
        w_refs, m_refs, v_refs = refs[1:1 + n], refs[1 + n:1 + 2 * n], refs[1 + 2 * n:1 + 3 * n]
        loss_out = refs[1 + 3 * n]
        outs = refs[2 + 3 * n:]
        chip = 2 * lax.axis_index("x") + lax.axis_index("y")
        loss_out[...] = tot[_P_LOSS:_P_LOSS + 1, 0:1]

        def step(k, g, pick, put):
            d, m2, v2 = _adam_values(pick(w_refs[k]), pick(m_refs[k]), pick(v_refs[k]), g)
            for o, val in zip((outs[k], outs[n + k], outs[2 * n + k], outs[3 * n + k]), (g, d, m2, v2)):
                put(o, val)

        def whole(ref):
            return ref[0]

        def put_whole(ref, val):
            ref[0] = val

        for g in range(SGU_GROUPS):
            def pick_g(ref, g=g):
                return ref[0, g]

            def put_g(ref, val, g=g):
                ref[0, g] = val

            step(0, tot[_P_WSP + g * SGU_BLOCK:_P_WSP + (g + 1) * SGU_BLOCK], pick_g, put_g)
        step(1, tot[pl.ds(pl.multiple_of(_P_WGU + GLA_RANK * chip, GLA_RANK), GLA_RANK), :], whole, put_whole)
        for k, (base, chunks) in enumerate(((_P_NORM, 8), (_P_NORM + 8, 8), (_P_NORM + 16, 8), (_P_NORM + 24, 8), (_P_BG, 4))):
            for q in range(chunks):
                def pick_q(ref, q=q):
                    return ref[:, LANES * q:LANES * (q + 1)]

                def put_q(ref, val, q=q):
                    ref[:, LANES * q:LANES * (q + 1)] = val

                step(2 + k, tot[base + q:base + q + 1], pick_q, put_q)
        step(7, tot[_P_BSP:_P_BSP + SGU_GROUPS], whole, put_whole)
        for k in range(3):
            mine = tot[pl.ds(pl.multiple_of(_P_HEAD + 32 * k + 8 * chip, 8), 8), :]
            step(8 + k, mine[0:4, 0:64], whole, put_whole)

    shapes = [_sds(w.shape, F32) for w in ws]
    res = pl.pallas_call(
        body, name="small_adamw", in_specs=[_whole()] * (1 + 3 * n), out_specs=[_whole()] * (1 + 4 * n),
        out_shape=[_sds((1, 1), F32)] + shapes * 4,
        compiler_params=pltpu.CompilerParams(vmem_limit_bytes=VMEM_LIMIT_BYTES),
    )(tot, *ws, *ms, *vs)
    return res[0].reshape(()), [list(res[1 + i * n:1 + (i + 1) * n]) for i in range(4)]


def _w_in_pieces():
    blk = D_IN // N_CHIPS
    pieces = []
    for s in range(len(_IN_SPLITS)):
        lo_s, hi_s = _IN_STARTS[s], _IN_STARTS[s + 1]
        for j in range(N_CHIPS):
            lo, hi = max(lo_s, j * blk), min(hi_s, (j + 1) * blk)
            if lo < hi:
                pieces.append((j, lo - j * blk, _IN_DST[s] + lo - lo_s, hi - lo))
    return pieces


def _relayout_w_in(gathered):
    _, rows, blk = gathered.shape
    tr = 256

    def body(g_ref, o_ref):
        o_ref[:, OFF_AL:N_ALL] = jnp.zeros((tr, LANES), BF16)
        for j, src, dst, w in _w_in_pieces():
            o_ref[:, dst:dst + w] = g_ref[j, :, src:src + w]

    res, _ = _call(body, name="relayout_w_in", grid=(rows // tr,), parallel=True,
                   in_specs=[pl.BlockSpec((N_CHIPS, tr, blk), lambda i: (0, i, 0))],
                   out_specs=[pl.BlockSpec((tr, N_ALL), lambda i: (i, 0))],
                   out_shape=[_sds((rows, N_ALL), BF16)], args=(gathered,))
    return res[0]


def _update_row_tile(rows):
    for t in range(min(rows, 256), 7, -8):
        if rows % t == 0:
            return t
    return rows


def _presum_w_in(dws, theirs, row0, rows, name, job=None):
    hr = theirs[0].shape[0]
    blk = D_IN // N_CHIPS
    tr = 64
    assert row0 % tr == 0 and rows % tr == 0
    nh, t0 = hr // tr, row0 // tr
    n = len(dws)

    def body(core_ref, *refs):
        dw_refs, q_refs, (o_ref, s_scr) = refs[:n], refs[n:2 * n], refs[2 * n:]
        for p, (a, off) in enumerate(dws):
            w = a.shape[1]
            s_scr[:, off:off + w] = (dw_refs[p][...] + q_refs[p][...]).astype(BF16)
        for j, src, dst, w in _w_in_pieces():
            o_ref[j, :, src:src + w] = s_scr[:, dst:dst + w]

    in_specs = [pl.BlockSpec((tr, a.shape[1]), lambda i, core: (i + t0 + core[0] * nh, 0)) for a, _ in dws]
    in_specs += [pl.BlockSpec((tr, q.shape[1]), lambda i, core: (i + t0, 0)) for q in theirs]
    res, jres = _call(body, name=name, grid=(rows // tr,), parallel=True, in_specs=in_specs,
                      out_specs=[pl.BlockSpec((N_CHIPS, tr, blk), lambda i, core: (0, i, 0))],
                      out_shape=[_sds((N_CHIPS, rows, blk), BF16)], scratch_shapes=[pltpu.VMEM((tr, N_ALL), BF16)],
                      args=(*[a for a, _ in dws], *theirs), job=job, by_core=True)
    return res[0], jres


def _presum(dw, theirs, name):
    if dw.ndim == 4:
        _, _, hr, c = dw.shape
        tr = _update_row_tile(hr)
        mine = pl.BlockSpec((1, 1, tr, c), lambda j, i, core: (j, core[0], i, 0))
        other = pl.BlockSpec((1, tr, c), lambda j, i, core: (j, i, 0))
    else:
        hr, c = dw.shape[0] // 2, dw.shape[1] // N_CHIPS
        tr = _update_row_tile(hr)
        nh = hr // tr
        mine = pl.BlockSpec((tr, c), lambda j, i, core: (i + core[0] * nh, j))
        other = pl.BlockSpec((tr, c), lambda j, i, core: (i, j))

    def body(core_ref, a_ref, q_ref, o_ref):
        o_ref[...] = (a_ref[...].reshape(tr, c) + q_ref[...].reshape(tr, c)).astype(BF16).reshape(o_ref.shape)

    res, _ = _call(body, name=name, grid=(N_CHIPS, hr // tr), parallel=True, in_specs=[mine, other],
                   out_specs=[pl.BlockSpec((1, tr, c), lambda j, i, core: (j, i, 0))],
                   out_shape=[_sds((N_CHIPS, hr, c), BF16)], args=(dw, theirs), by_core=True)
    return res[0]


def _sum_slots(own, slots, name):
    rows, cols = own.shape
    tr = _update_row_tile(rows)

    def body(own_ref, s_ref, o_ref):
        acc = own_ref[...].astype(F32)
        for j in range(N_PEER):
            acc = acc + s_ref[j].astype(F32)
        o_ref[...] = acc

    res, _ = _call(body, name=name, grid=(rows // tr,), parallel=True,
                   in_specs=[pl.BlockSpec((tr, cols), lambda i: (i, 0)), pl.BlockSpec((N_PEER, tr, cols), lambda i: (0, i, 0))],
                   out_specs=[pl.BlockSpec((tr, cols), lambda i: (i, 0))], out_shape=[_sds((rows, cols), F32)],
                   args=(own, slots))
    return res[0]


def _adamw(w, m, v, g_mine, g_theirs, name, job=None):
    rows, cols = w.shape
    part_rows = [p.shape[0] for p in g_mine]
    assert sum(part_rows) == rows // 2 and [p.shape[0] for p in g_theirs] == part_rows
    tr = _update_row_tile(min(part_rows))
    assert all(r % tr == 0 for r in part_rows)
    nh = (rows // 2) // tr
    starts = [sum(part_rows[:k]) // tr for k in range(len(part_rows))]
    n_parts = len(part_rows)

    def body(core_ref, w_ref, m_ref, v_ref, *rest):
        g_refs, (g_out, d_out, m_out, v_out) = rest[:-4], rest[-4:]
        step = pl.program_id(0)
        mine_here = (step // nh) == core_ref[0]
        q = step % nh
        g = None
        for k in reversed(range(n_parts)):
            val = jnp.where(mine_here, g_refs[k][...], g_refs[n_parts + k][...])
            g = val if g is None else jnp.where(q < starts[k + 1], val, g)
        d, m2, v2 = _adam_values(w_ref[...], m_ref[...], v_ref[...], g)
        g_out[...] = g
        m_out[...] = m2
        v_out[...] = v2
        d_out[...] = d

    def g_spec(k, mine):
        last = part_rows[k] // tr - 1

        def index(i, core):
            half = core[0] if mine else 1 - core[0]
            here = jnp.clip(i % nh - starts[k], 0, last)
            return (jnp.where(i // nh == half, here, jnp.where(i // nh > half, last, 0)), 0)

        return pl.BlockSpec((tr, cols), index)

    spec = pl.BlockSpec((tr, cols), lambda i, core: (i, 0))
    g_specs = [g_spec(k, True) for k in range(n_parts)] + [g_spec(k, False) for k in range(n_parts)]
    return _call(body, name=name, grid=(rows // tr,), parallel=True, in_specs=[spec] * 3 + g_specs,
                 out_specs=[spec] * 4, out_shape=[_sds((rows, cols), F32)] * 4, args=(w, m, v, *g_mine, *g_theirs),
                 job=job, by_core=True)


def _transposed_cast(wt):
    cols, rows = wt.shape

    def body(x_ref, o_ref):
        o_ref[...] = jnp.transpose(x_ref[...]).astype(BF16)

    res, _ = _call(body, name="transpose_w_in", grid=(pl.cdiv(cols, LANES),), parallel=True,
                   in_specs=[pl.BlockSpec((LANES, rows), lambda j: (j, 0))],
                   out_specs=[pl.BlockSpec((rows, LANES), lambda j: (0, j))], out_shape=[_sds((rows, cols), BF16)],
                   args=(wt,))
    return res[0]


def _cast_weights(ws, w_fi, job=None):
    steps = 4
    cols = w_fi.shape[1]
    tile = w_fi.shape[0] // (2 * steps)

    def body(*refs):
        for i_ref, o_ref in zip(refs[:len(refs) // 2], refs[len(refs) // 2:]):
            o_ref[...] = i_ref[...].astype(BF16)

    row_specs = [pl.BlockSpec((w.shape[0] // steps, w.shape[1]), lambda i: (i, 0)) for w in ws]
    half_spec = pl.BlockSpec((tile, cols), lambda i: (i, 0))
    return _call(body, name="cast_weights", grid=(steps,), parallel=True,
                 in_specs=row_specs + [pl.BlockSpec((None, tile, cols), lambda i, k=k: (k, i, 0)) for k in range(2)],
                 out_specs=row_specs + [half_spec, half_spec],
                 out_shape=[_sds(w.shape, BF16) for w in ws] + [_sds((steps * tile, cols), BF16)] * 2,
                 args=(*ws, w_fi.reshape(2, steps * tile, cols), w_fi.reshape(2, steps * tile, cols)), job=job)


def _adamw_transposed(wt, mt, vt, g_mine, g_theirs, name):
    cols, rows = wt.shape
    n_parts = len(g_mine)

    def body(w_ref, m_ref, v_ref, *rest):
        g_refs, (g_out, d_out, m_out, v_out) = rest[:-4], rest[-4:]
        mine = jnp.concatenate([r[...] for r in g_refs[:n_parts]], axis=0)
        theirs = jnp.concatenate([r[...] for r in g_refs[n_parts:]], axis=0)
        first = lax.axis_index("c") == 0
        g = jnp.transpose(jnp.concatenate([jnp.where(first, mine, theirs), jnp.where(first, theirs, mine)], axis=0))
        d, m2, v2 = _adam_values(w_ref[...], m_ref[...], v_ref[...], g)
        g_out[...] = g
        m_out[...] = m2
        v_out[...] = v2
        d_out[...] = d

    spec = pl.BlockSpec((LANES, rows), lambda j: (j, 0))
    g_specs = [pl.BlockSpec((p.shape[0], LANES), lambda j: (0, j)) for p in g_mine] * 2
    res, _ = _call(body, name=name, grid=(pl.cdiv(cols, LANES),), parallel=True, in_specs=[spec] * 3 + g_specs,
                   out_specs=[spec] * 4, out_shape=[_sds((cols, rows), F32)] * 4, args=(wt, mt, vt, *g_mine, *g_theirs))
    return res


def _first_step_apart(compute):
    first = pl.program_id(0) == 0
    pl.when(first)(lambda: compute(True))
    pl.when(jnp.logical_not(first))(lambda: compute(False))


def _fetches(pairs, sems, first):
    if not first:
        return lambda k: None
    copies = [pltpu.make_async_copy(src, dst, sems.at[k]) for k, (src, dst) in enumerate(pairs)]
    for cp in copies:
        cp.start()
    return lambda k: copies[k].wait()


def _column_fetches(w_hbm, w_scr, sems, chunks, first):
    return _fetches([(w_hbm.at[:, pl.ds(s, n)], w_scr.at[:, pl.ds(s, n)]) for s, n in chunks], sems, first)


def _inproj_fwd(x, g1, w_all, job=None):
    T = x.shape[0]
    tT = _row_tile(T, 512)
    chunks = [(j * 1024, 1024) for j in range(N_MAIN // 1024)] + [(N_MAIN, LANES)]

    def body(x_ref, g_ref, w_hbm, a_ref, proj_ref, alow_ref, w_ref, w_sems):
        def compute(first):
            ready = _column_fetches(w_hbm, w_ref, w_sems, chunks, first)
            xv = x_ref[...]
            a = (xv * _rms_stats(xv) * g_ref[...]).astype(BF16)
            a_ref[...] = a
            for j in range(N_MAIN // 1024):
                cols = slice(j * 1024, (j + 1) * 1024)
                ready(j)
                proj_ref[:, cols] = _dot(a, w_ref[:, cols]).astype(BF16)
            ready(N_MAIN // 1024)
            alow_ref[...] = _dot(a, w_ref[:, N_MAIN:N_ALL])

        _first_step_apart(compute)

    row = lambda w: pl.BlockSpec((tT, w), lambda i: (i, 0))
    return _call(
        body, name="inproj_fwd", grid=(T // tT,),
        in_specs=[row(D_MODEL), pl.BlockSpec((1, D_MODEL), lambda i: (0, 0)), pl.BlockSpec(memory_space=pl.ANY)],
        out_specs=[row(D_MODEL), row(N_MAIN), row(LANES)],
        out_shape=[_sds((T, D_MODEL), BF16), _sds((T, N_MAIN), BF16), _sds((T, LANES), F32)],
        scratch_shapes=[pltpu.VMEM(w_all.shape, BF16), pltpu.SemaphoreType.DMA((len(chunks),))],
        args=(x, g1, w_all), job=job)


def _gla_decay_terms(al_ref, wgu_ref, bg_ref, later_ref):
    logit = _dot_bf16(al_ref[...], wgu_ref[...]) + bg_ref[...]
    la = _log_sigmoid(logit) * (1.0 / GLA_TAU)
    delta = _dot_exact_lhs(later_ref[...], la)
    return logit, la, delta


def _gla_fwd(proj, alow, wgu, b_gate, gn, job=None):
    T = proj.shape[0]
    tT = _row_tile(T, 512)
    nc = tT // CHUNK

    def body(q_ref, k_ref, v_ref, r_ref, al_ref, wgu_ref, bg_ref, gn_ref, later_ref, y_ref, st_ref, s_scr):
        @pl.when(pl.program_id(0) == 0)
        def _():
            s_scr[...] = jnp.zeros_like(s_scr)

        _, la, delta = _gla_decay_terms(al_ref, wgu_ref, bg_ref, later_ref)
        kdec = (k_ref[...].astype(F32) * jnp.exp(delta)).astype(BF16)
        heads = range(GLA_HEADS)
        kcs = [slice(h * GLA_DK, (h + 1) * GLA_DK) for h in heads]
        vcs = [slice(h * GLA_DV, (h + 1) * GLA_DV) for h in heads]
        state = [s_scr[h] for h in heads]
        for c in range(nc):
            rows = slice(c * CHUNK, (c + 1) * CHUNK)
            first = slice(c * CHUNK, c * CHUNK + 1)
            dec = jnp.exp(la[first, :] + delta[first, :])
            upd_t = [_dot(v_ref[rows, vcs[h]], kdec[rows, kcs[h]], _TN) for h in heads]
            qs = [(q_ref[rows, kcs[h]].astype(F32) * (GLA_DK ** -0.5)).astype(BF16) for h in heads]
            for h in heads:
                state[h] = state[h] * dec[:, kcs[h]] + upd_t[h]
                st_ref[c, h] = state[h]
            o = [_dot(qs[h], state[h].astype(BF16), _NT) for h in heads]
            for h in heads:
                on = o[h] * _rms_stats(o[h]) * gn_ref[:, vcs[h]]
                rr = r_ref[rows, vcs[h]].astype(F32)
                y_ref[rows, vcs[h]] = (on * (rr * _sigmoid(rr))).astype(BF16)
        for h in heads:
            s_scr[h] = state[h]

    blk = lambda w, j: pl.BlockSpec((tT, w), lambda i: (i, j))
    return _call(
        body, name="gla_fwd", grid=(T // tT,),
        in_specs=[blk(512, 0), blk(512, 1), blk(1024, 1), blk(1024, 2), blk(LANES, 0)] + [_whole()] * 4,
        out_specs=[pl.BlockSpec((tT, GLA_V), lambda i: (i, 0)),
                   pl.BlockSpec((nc, GLA_HEADS, GLA_DV, GLA_DK), lambda i: (i, 0, 0, 0))],
        out_shape=[_sds((T, GLA_V), BF16), _sds((T // CHUNK, GLA_HEADS, GLA_DV, GLA_DK), F32)],
        scratch_shapes=[pltpu.VMEM((GLA_HEADS, GLA_DV, GLA_DK), F32)],
        args=(proj, proj, proj, proj, alow, wgu, b_gate, gn, _chunk_masks(tT, upper=True)), job=job)


def _sgu_mask():
    i = lax.broadcasted_iota(jnp.int32, (SGU_BLOCK, SGU_BLOCK), 0)
    j = lax.broadcasted_iota(jnp.int32, (SGU_BLOCK, SGU_BLOCK), 1)
    return lax.shift_right_logical(j, 6) <= lax.shift_right_logical(i, 6)


def _sgu_merge_fwd(x, proj, y_gla, ln_g, ln_b, w_sp, b_sp_t, w_bg, w_bs, w_o, g_pm, job=None):
    T = x.shape[0]
    tT = _row_tile(T, 512)
    nb = tT // SGU_BLOCK

    def body(x_ref, su_ref, sv_ref, gg_ref, gs_ref, yg_ref, lg_ref, lb_ref, w_ref, b_ref, wbg_ref, wbs_ref, wo_ref,
             g_ref, ys_ref, zg_ref, zs_ref, mg_ref, mix_ref, x1_ref):
        mask = _sgu_mask()
        for g in range(SGU_GROUPS):
            gc = slice(g * SGU_DG, (g + 1) * SGU_DG)
            wm = jnp.where(mask, w_ref[g], 0.0).astype(BF16)
            vf = _gelu(sv_ref[:, gc].astype(F32))
            mu = jnp.mean(vf, axis=-1, keepdims=True)
            vc = vf - mu
            rstd = lax.rsqrt(jnp.mean(vc * vc, axis=-1, keepdims=True) + EPS)
            vn = (vc * rstd * lg_ref[:, gc] + lb_ref[:, gc]).astype(BF16)
            u = _gelu(su_ref[:, gc].astype(F32))
            for b in range(nb):
                rows = slice(b * SGU_BLOCK, (b + 1) * SGU_BLOCK)
                mixed = _dot(wm, vn[rows, :]) + b_ref[:, g:g + 1]
                ys_ref[rows, gc] = (u[rows, :] * mixed).astype(BF16)
        zg = _dot(yg_ref[...], wbg_ref[...])
        zs = _dot(ys_ref[...], wbs_ref[...])
        zg_ref[...] = zg.astype(BF16)
        zs_ref[...] = zs.astype(BF16)
        merged = (_sigmoid(gg_ref[...].astype(F32)) * zg + _sigmoid(gs_ref[...].astype(F32)) * zs).astype(BF16)
        mg_ref[...] = merged
        mix = _dot(merged, wo_ref[...])
        mix_ref[...] = mix.astype(BF16)
        x1_ref[...] = x_ref[...] + mix * _rms_stats(mix) * g_ref[...]

    row = pl.BlockSpec((tT, D_MODEL), lambda i: (i, 0))
    blk = lambda j: pl.BlockSpec((tT, 1024), lambda i: (i, j))
    sds = lambda dt: _sds((T, D_MODEL), dt)
    return _call(body, name="sgu_merge_fwd", grid=(T // tT,), parallel=True,
                 in_specs=[row, blk(3), blk(4), blk(5), blk(6), row] + [_whole()] * 7
                 + [pl.BlockSpec((1, D_MODEL), lambda i: (0, 0))],
                 out_specs=[row] * 6, out_shape=[sds(BF16)] * 5 + [sds(F32)],
                 args=(x, proj, proj, proj, proj, y_gla, ln_g, ln_b, w_sp, b_sp_t, w_bg, w_bs, w_o, g_pm), job=job)


def _ffn_fwd_bwd(x1, tgt, w_fi_top, w_fi_bot, w_fo, g_pf, g_po):
    T = x1.shape[0]
    tT = _row_tile(T, 256)
    half = D_FF // 2
    kh = D_MODEL // 2

    def body(x1_ref, t_ref, top_hbm, bot_hbm, wfo_hbm, gpf_ref, gpo_ref,
             h_ref, f_ref, dgu_ref, dy_ref, dx1_ref, loss_ref, dgpf_ref, dgpo_ref, gu_scr, top_ref, bot_ref, wfo_ref, w_sems):
        pairs = []
        for p in range(2):
            pairs += [(top_hbm.at[p], top_ref.at[p]), (bot_hbm.at[p], bot_ref.at[p]),
                      (top_hbm.at[2 + p], top_ref.at[2 + p]), (bot_hbm.at[2 + p], bot_ref.at[2 + p]),
                      (wfo_hbm.at[pl.ds(p * half, half)], wfo_ref.at[pl.ds(p * half, half)])]

        main = (half // 256) * 256
        pieces = (0, 1, None)

        def w_in_cols(ref, first_slab, p):
            if p is not None:
                return ref[first_slab + p, :, :main]
            return jnp.concatenate([ref[first_slab, :, main:], ref[first_slab + 1, :, main:]], axis=1)

        def w_out_rows(p):
            if p is not None:
                return wfo_ref[p * half:p * half + main, :]
            return jnp.concatenate([wfo_ref[main:half, :], wfo_ref[half + main:2 * half, :]], axis=0)

        def put(ref, base, p, val):
            if p is not None:
                ref[:, base + p * half:base + p * half + main] = val
            else:
                ref[:, base + main:base + half] = val[:, :half - main]
                ref[:, base + half + main:base + 2 * half] = val[:, half - main:]

        def get(ref, base, p):
            if p is not None:
                return ref[:, base + p * half:base + p * half + main]
            return jnp.concatenate([ref[:, base + main:base + half], ref[:, base + half + main:base + 2 * half]], axis=1)

        def compute(first):
            ready = _fetches(pairs, w_sems, first)

            def add_to(ref, val):
                ref[...] = val if first else ref[...] + val

            x1v = x1_ref[...]
            r2 = _rms_stats(x1v)
            h = (x1v * r2 * gpf_ref[...]).astype(BF16)
            h_ref[...] = h
            y = jnp.zeros((tT, D_MODEL), F32)
            for p in pieces:
                arrived = (lambda k: ready(5 * p + k)) if p is not None else (lambda k: None)
                arrived(0), arrived(1)
                gate = _dot(h[:, :kh], w_in_cols(top_ref, 0, p)) + _dot(h[:, kh:], w_in_cols(bot_ref, 0, p))
                arrived(2), arrived(3)
                up = _dot(h[:, :kh], w_in_cols(top_ref, 2, p)) + _dot(h[:, kh:], w_in_cols(bot_ref, 2, p))
                put(gu_scr, 0, p, gate)
                put(gu_scr, D_FF, p, up)
                f = (gate * _sigmoid(gate) * up).astype(BF16)
                put(f_ref, 0, p, f)
                arrived(4)
                y = y + _dot(f, w_out_rows(p))
            r3 = _rms_stats(y)
            x2 = x1v + y * r3 * gpo_ref[...]
            err = x2 - t_ref[...]
            add_to(loss_ref, jnp.full(loss_ref.shape, jnp.sum(err * err) * (0.5 / D_MODEL), F32))
            dx2 = err * (1.0 / D_MODEL)
            dy, dg = _rms_bwd(dx2, y, r3, gpo_ref[...])
            add_to(dgpo_ref, jnp.sum(dg, axis=0, keepdims=True))
            dyb = dy.astype(BF16)
            dy_ref[...] = dyb
            dh_top = jnp.zeros((tT, kh), F32)
            dh_bot = jnp.zeros((tT, kh), F32)
            for p in pieces:
                df = _dot(dyb, w_out_rows(p), _NT)
                gate = get(gu_scr, 0, p)
                up = get(gu_scr, D_FF, p)
                sg = _sigmoid(gate)
                dgate = (df * up * (sg * (1.0 + gate * (1.0 - sg)))).astype(BF16)
                dup = (df * (gate * sg)).astype(BF16)
                put(dgu_ref, 0, p, dgate)
                put(dgu_ref, D_FF, p, dup)
                dh_top = dh_top + _dot(dgate, w_in_cols(top_ref, 0, p), _NT) + _dot(dup, w_in_cols(top_ref, 2, p), _NT)
                dh_bot = dh_bot + _dot(dgate, w_in_cols(bot_ref, 0, p), _NT) + _dot(dup, w_in_cols(bot_ref, 2, p), _NT)
            dh = jnp.concatenate([dh_top, dh_bot], axis=1)
            dx1n, dg2 = _rms_bwd(dh, x1v, r2, gpf_ref[...])
            add_to(dgpf_ref, jnp.sum(dg2, axis=0, keepdims=True))
            dx1_ref[...] = dx2 + dx1n

        _first_step_apart(compute)

    row = lambda w: pl.BlockSpec((tT, w), lambda i: (i, 0))
    vec = pl.BlockSpec((1, D_MODEL), lambda i: (0, 0))
    hbm = pl.BlockSpec(memory_space=pl.ANY)
    res, _ = _call(
        body, name="ffn_fwd_bwd", grid=(T // tT,),
        in_specs=[row(D_MODEL), row(D_MODEL), hbm, hbm, hbm, vec, vec],
        out_specs=[row(D_MODEL), row(D_FF), row(2 * D_FF), row(D_MODEL), row(D_MODEL),
                   pl.BlockSpec((1, LANES), lambda i: (0, 0)), vec, vec],
        out_shape=[_sds((T, D_MODEL), BF16), _sds((T, D_FF), BF16), _sds((T, 2 * D_FF), BF16), _sds((T, D_MODEL), BF16),
                   _sds((T, D_MODEL), F32), _sds((1, LANES), F32), _sds((1, D_MODEL), F32), _sds((1, D_MODEL), F32)],
        scratch_shapes=[pltpu.VMEM((tT, 2 * D_FF), F32), pltpu.VMEM(w_fi_top.shape, BF16), pltpu.VMEM(w_fi_bot.shape, BF16),
                        pltpu.VMEM(w_fo.shape, BF16), pltpu.SemaphoreType.DMA((10,))],
        args=(x1, tgt, w_fi_top, w_fi_bot, w_fo, g_pf, g_po))
    return res


def _merge_sgu_bwd(dx1, mix, proj, zg, zs, w_bg, w_bs, w_o, g_pm, ln_g, ln_b, w_sp, b_sp_t, job=None):
    T = dx1.shape[0]
    tT = _row_tile(T, 256)
    nb = tT // SGU_BLOCK

    def body(dx1_ref, mix_ref, su_ref, sv_ref, gg_ref, gs_ref, zg_ref, zs_ref, wbg_ref, wbs_ref, wo_ref, g_ref,
             lg_ref, lb_ref, w_ref, b_ref,
             dmix_ref, dzg_ref, dzs_ref, dgate_ref, dyg_ref, dp_ref, dgpm_ref, dw_ref, dbt_ref, dlg_ref, dlb_ref):
        @pl.when(pl.program_id(0) == 0)
        def _():
            for ref in (dgpm_ref, dw_ref, dbt_ref, dlg_ref, dlb_ref):
                ref[...] = jnp.zeros_like(ref)

        mix = mix_ref[...].astype(F32)
        dmix, dg = _rms_bwd(dx1_ref[...], mix, _rms_stats(mix), g_ref[...])
        dgpm_ref[...] += jnp.sum(dg, axis=0, keepdims=True)
        dmb = dmix.astype(BF16)
        dmix_ref[...] = dmb
        dmerged = _dot(dmb, wo_ref[...], _NT)
        dys = None
        for k, (gate_ref, z_ref, w_br_ref, dz_ref) in enumerate(((gg_ref, zg_ref, wbg_ref, dzg_ref),
                                                                 (gs_ref, zs_ref, wbs_ref, dzs_ref))):
            sg = _sigmoid(gate_ref[...].astype(F32))
            dz = (dmerged * sg).astype(BF16)
            dz_ref[...] = dz
            dgate_ref[:, k * 1024:(k + 1) * 1024] = (dmerged * z_ref[...].astype(F32) * (sg * (1.0 - sg))).astype(BF16)
            dy_branch = _dot(dz, w_br_ref[...], _NT)
            if k == 0:
                dyg_ref[...] = dy_branch.astype(BF16)
            else:
                dys = dy_branch

        mask = _sgu_mask()
        lane = lax.broadcasted_iota(jnp.int32, (SGU_BLOCK, LANES), 1)
        for g in range(SGU_GROUPS):
            gc = slice(g * SGU_DG, (g + 1) * SGU_DG)
            gc_v = slice(1024 + g * SGU_DG, 1024 + (g + 1) * SGU_DG)
            wm = jnp.where(mask, w_ref[g], 0.0).astype(BF16)
            vf, dvf_dsv = _gelu_and_grad(sv_ref[:, gc].astype(F32))
            mu = jnp.mean(vf, axis=-1, keepdims=True)
            vc = vf - mu
            rstd = lax.rsqrt(jnp.mean(vc * vc, axis=-1, keepdims=True) + EPS)
            vhat = vc * rstd
            vn = (vhat * lg_ref[:, gc] + lb_ref[:, gc]).astype(BF16)
            u, du_dsu = _gelu_and_grad(su_ref[:, gc].astype(F32))
            dy = dys[:, gc]
            dmixed = (dy * u).astype(BF16)
            dvn_parts = []
            dw_acc = jnp.zeros((SGU_BLOCK, SGU_BLOCK), F32)
            db_acc = jnp.zeros((SGU_BLOCK, 1), F32)
            for b in range(nb):
                rows = slice(b * SGU_BLOCK, (b + 1) * SGU_BLOCK)
                mixed = _dot(wm, vn[rows, :]) + b_ref[:, g:g + 1]
                dp_ref[rows, gc] = (dy[rows, :] * mixed * du_dsu[rows, :]).astype(BF16)
                dvn_parts.append(_dot(wm, dmixed[rows, :], _TN))
                dw_acc = dw_acc + _dot(dmixed[rows, :], vn[rows, :], _NT)
                db_acc = db_acc + jnp.sum(dmixed[rows, :].astype(F32), axis=-1, keepdims=True)
            dw_ref[g] += jnp.where(mask, dw_acc, 0.0)
            dbt_ref[...] += jnp.where(lane == g, db_acc, 0.0)
            dvn = jnp.concatenate(dvn_parts, axis=0)
            dlg_ref[:, gc] += jnp.sum(dvn * vhat, axis=0, keepdims=True)
            dlb_ref[:, gc] += jnp.sum(dvn, axis=0, keepdims=True)
            dvh = dvn * lg_ref[:, gc]
            dvf = rstd * (dvh - jnp.mean(dvh, axis=-1, keepdims=True)
                          - vhat * jnp.mean(dvh * vhat, axis=-1, keepdims=True))
            dp_ref[:, gc_v] = (dvf * dvf_dsv).astype(BF16)

    row = pl.BlockSpec((tT, D_MODEL), lambda i: (i, 0))
    blk = lambda j: pl.BlockSpec((tT, 1024), lambda i: (i, j))
    vec = pl.BlockSpec((1, D_MODEL), lambda i: (0, 0))
    wide = lambda w: pl.BlockSpec((tT, w), lambda i: (i, 0))
    sds = _sds((T, D_MODEL), BF16)
    return _call(
        body, name="merge_sgu_bwd", grid=(T // tT,),
        in_specs=[row, row, blk(3), blk(4), blk(5), blk(6), row, row] + [_whole()] * 3 + [vec] + [_whole()] * 4,
        out_specs=[row, row, row, wide(W_MRG), row, wide(W_SGU), vec,
                   pl.BlockSpec((SGU_GROUPS, SGU_BLOCK, SGU_BLOCK), lambda i: (0, 0, 0)),
                   pl.BlockSpec((SGU_BLOCK, LANES), lambda i: (0, 0)), vec, vec],
        out_shape=[sds, sds, sds, _sds((T, W_MRG), BF16), sds, _sds((T, W_SGU), BF16), _sds((1, D_MODEL), F32),
                   _sds((SGU_GROUPS, SGU_BLOCK, SGU_BLOCK), F32), _sds((SGU_BLOCK, LANES), F32),
                   _sds((1, 1024), F32), _sds((1, 1024), F32)],
        args=(dx1, mix, proj, proj, proj, proj, zg, zs, w_bg, w_bs, w_o, g_pm, ln_g, ln_b, w_sp, b_sp_t), job=job)


def _gla_bwd(proj, alow, wgu, b_gate, gn, states, dy_gla, job=None):
    T = proj.shape[0]
    tT = _row_tile(T, 512)
    nc = tT // CHUNK
    nt = T // tT

    def body(q_ref, k_ref, v_ref, r_ref, al_ref, wgu_ref, bg_ref, gn_ref, later_ref, earlier_ref, st_ref, sp_ref, dy_ref,
             dp_ref, dal_ref, dgn_ref, dbg_ref, dwgu_ref, g_scr, dd_scr, dt_scr):
        step = pl.program_id(0)

        @pl.when(step == 0)
        def _():
            g_scr[...] = jnp.zeros_like(g_scr)
            dgn_ref[...] = jnp.zeros_like(dgn_ref)
            dbg_ref[...] = jnp.zeros_like(dbg_ref)
            dwgu_ref[...] = jnp.zeros_like(dwgu_ref)

        has_prev = jnp.where(step == nt - 1, 0.0, 1.0)
        logit, la, delta = _gla_decay_terms(al_ref, wgu_ref, bg_ref, later_ref)
        e = jnp.exp(delta)
        kdec_f = k_ref[...].astype(F32) * e
        kdec = kdec_f.astype(BF16)
        heads = range(GLA_HEADS)
        kcs = [slice(h * GLA_DK, (h + 1) * GLA_DK) for h in heads]
        vcs = [slice(h * GLA_DV, (h + 1) * GLA_DV) for h in heads]
        carry = [g_scr[h] for h in heads]
        dgn_acc = [jnp.zeros((1, GLA_DV), F32) for _ in heads]
        for c in reversed(range(nc)):
            rows = slice(c * CHUNK, (c + 1) * CHUNK)
            first = slice(c * CHUNK, c * CHUNK + 1)
            dec = jnp.exp(la[first, :] + delta[first, :])
            s_b = [st_ref[c, h].astype(BF16) for h in heads]
            qs = [(q_ref[rows, kcs[h]].astype(F32) * (GLA_DK ** -0.5)).astype(BF16) for h in heads]
            o = [_dot(qs[h], s_b[h], _NT) for h in heads]
            do = []
            for h in heads:
                rstd = _rms_stats(o[h])
                ohat = o[h] * rstd
                gnh = gn_ref[:, vcs[h]]
                dy = dy_ref[rows, vcs[h]].astype(F32)
                rr = r_ref[rows, vcs[h]].astype(F32)
                sg = _sigmoid(rr)
                don = dy * (rr * sg)
                dp_ref[rows, OFF_R + h * GLA_DV:OFF_R + (h + 1) * GLA_DV] = (
                    dy * (ohat * gnh) * (sg * (1.0 + rr * (1.0 - sg)))).astype(BF16)
                dgn_acc[h] = dgn_acc[h] + jnp.sum(don * ohat, axis=0, keepdims=True)
                dn = don * gnh
                do.append((rstd * (dn - ohat * jnp.mean(dn * ohat, axis=-1, keepdims=True))).astype(BF16))
            dq = [_dot(do[h], s_b[h]) for h in heads]
            g_t = [_dot(do[h], qs[h], _TN) + carry[h] for h in heads]
            g_b = [g_t[h].astype(BF16) for h in heads]
            dv = [_dot(kdec[rows, kcs[h]], g_b[h], _NT) for h in heads]
            dkdec = [_dot(v_ref[rows, vcs[h]], g_b[h]) for h in heads]
            for h in heads:
                s_prev = st_ref[c - 1, h] if c > 0 else sp_ref[0, h] * has_prev
                ddec = jnp.sum(g_t[h] * s_prev, axis=0, keepdims=True)
                carry[h] = g_t[h] * dec[:, kcs[h]]
                dp_ref[rows, OFF_Q + h * GLA_DK:OFF_Q + (h + 1) * GLA_DK] = (dq[h] * (GLA_DK ** -0.5)).astype(BF16)
                dp_ref[rows, OFF_V + h * GLA_DV:OFF_V + (h + 1) * GLA_DV] = dv[h].astype(BF16)
                dp_ref[rows, OFF_K + h * GLA_DK:OFF_K + (h + 1) * GLA_DK] = (dkdec[h] * e[rows, kcs[h]]).astype(BF16)
                dd_scr[rows, kcs[h]] = dkdec[h] * kdec_f[rows, kcs[h]]
                dt_scr[rows, kcs[h]] = jnp.broadcast_to(ddec * dec[:, kcs[h]], (CHUNK, GLA_DK))
        for h in heads:
            g_scr[h] = carry[h]
            dgn_ref[:, vcs[h]] += dgn_acc[h]
        dla = _dot_exact_lhs(earlier_ref[...], dd_scr[...]) + dt_scr[...]
        dlogit = dla * (1.0 / GLA_TAU) * _sigmoid(-logit)
        dbg_ref[...] += jnp.sum(dlogit, axis=0, keepdims=True)
        dwgu_ref[...] += _dot_bf16(al_ref[...], dlogit, _TN)
        dal_ref[...] = _dot_bf16(dlogit, wgu_ref[...], _NT).astype(BF16)

    rev = lambda i: nt - 1 - i
    blk = lambda w, j: pl.BlockSpec((tT, w), lambda i: (rev(i), j))
    st_blk = pl.BlockSpec((nc, GLA_HEADS, GLA_DV, GLA_DK), lambda i: (rev(i), 0, 0, 0))
    sp_blk = pl.BlockSpec((1, GLA_HEADS, GLA_DV, GLA_DK), lambda i: (jnp.maximum(rev(i) * nc - 1, 0), 0, 0, 0))
    return _call(
        body, name="gla_bwd", grid=(nt,),
        in_specs=[blk(512, 0), blk(512, 1), blk(1024, 1), blk(1024, 2), blk(LANES, 0)] + [_whole()] * 5
        + [st_blk, sp_blk, blk(GLA_V, 0)],
        out_specs=[blk(W_GLA, 0), blk(LANES, 0), pl.BlockSpec((1, GLA_V), lambda i: (0, 0)),
                   pl.BlockSpec((1, GLA_QK), lambda i: (0, 0)), pl.BlockSpec((LANES, GLA_QK), lambda i: (0, 0))],
        out_shape=[_sds((T, W_GLA), BF16), _sds((T, LANES), BF16), _sds((1, GLA_V), F32), _sds((1, GLA_QK), F32),
                   _sds((LANES, GLA_QK), F32)],
        scratch_shapes=[pltpu.VMEM((GLA_HEADS, GLA_DV, GLA_DK), F32), pltpu.VMEM((tT, GLA_QK), F32),
                        pltpu.VMEM((tT, GLA_QK), F32)],
        args=(proj, proj, proj, proj, alow, wgu, b_gate, gn, _chunk_masks(tT, upper=True), _chunk_masks(tT, upper=False),
              states, states, dy_gla), job=job)


def _inproj_bwd(x, dx1, g1, w_all, dparts, job=None):
    T = x.shape[0]
    tT = _row_tile(T, 512)
    offs = (0, W_GLA, W_GLA + W_SGU, N_MAIN)

    chunks = [(off, p.shape[1]) for off, p in zip(offs, dparts)]

    def body(x_ref, dx1_ref, g_ref, w_hbm, *rest):
        part_refs, (dx_ref, dg_ref, w_ref, w_sems) = rest[:len(offs)], rest[len(offs):]

        def compute(first):
            ready = _column_fetches(w_hbm, w_ref, w_sems, chunks, first)
            da = jnp.zeros((tT, D_MODEL), F32)
            for k, (off, p_ref) in enumerate(zip(offs, part_refs)):
                ready(k)
                da = da + _dot(p_ref[...], w_ref[:, off:off + p_ref.shape[1]], _NT)
            xv = x_ref[...]
            dx, dg = _rms_bwd(da, xv, _rms_stats(xv), g_ref[...])
            dg_sum = jnp.sum(dg, axis=0, keepdims=True)
            dg_ref[...] = dg_sum if first else dg_ref[...] + dg_sum
            dx_ref[...] = dx1_ref[...] + dx

        _first_step_apart(compute)

    row = lambda w: pl.BlockSpec((tT, w), lambda i: (i, 0))
    vec = pl.BlockSpec((1, D_MODEL), lambda i: (0, 0))
    return _call(
        body, name="inproj_bwd", grid=(T // tT,),
        in_specs=[row(D_MODEL), row(D_MODEL), vec, pl.BlockSpec(memory_space=pl.ANY)] + [row(p.shape[1]) for p in dparts],
        out_specs=[row(D_MODEL), vec], out_shape=[_sds((T, D_MODEL), F32), _sds((1, D_MODEL), F32)],
        scratch_shapes=[pltpu.VMEM(w_all.shape, BF16), pltpu.SemaphoreType.DMA((len(chunks),))],
        args=(x, dx1, g1, w_all, *dparts), job=job)


def _tn_matmul(a, b, name, job=None):
    T, M = a.shape
    N = b.shape[1]
    tk = _row_tile(T, 1024)
    tm = M if M <= 1024 else 1408
    tn = N if N <= 3072 else N // 2
    assert M % tm == 0 and N % tn == 0

    def body(a_ref, b_ref, o_ref):
        @pl.when(pl.program_id(2) == 0)
        def _():
            o_ref[...] = _dot(a_ref[...], b_ref[...], _TN)

        @pl.when(pl.program_id(2) > 0)
        def _():
            o_ref[...] += _dot(a_ref[...], b_ref[...], _TN)

    res, jres = _call(
        body, name=name, grid=(M // tm, N // tn, T // tk),
        in_specs=[pl.BlockSpec((tk, tm), lambda i, j, k: (k, i)), pl.BlockSpec((tk, tn), lambda i, j, k: (k, j))],
        out_specs=[pl.BlockSpec((tm, tn), lambda i, j, k: (i, j))], out_shape=[_sds((M, N), F32)], args=(a, b), job=job)
    return res[0], jres


def _pad_rows(a, rows=8):
    return jnp.pad(a, ((0, rows - a.shape[0]), (0, LANES - a.shape[1])))


def _halves_view(dw):
    r = dw.shape[0] // N_CHIPS
    return dw.reshape(N_CHIPS, 2, r // 2, dw.shape[1])


def kernel(x, norm_pre_mix, w_in, w_gate_up, b_gate, gla_norm, sgu_ln_g, sgu_ln_b, w_spatial, b_spatial, w_branch_gla, w_branch_sgu, w_out, norm_post_mix, norm_pre_ffn, w_ffn_in, w_ffn_out, norm_post_ffn, loss_target, m_norm_pre_mix, m_w_in, m_w_gate_up, m_b_gate, m_gla_norm, m_sgu_ln_g, m_sgu_ln_b, m_w_spatial, m_b_spatial, m_w_branch_gla, m_w_branch_sgu, m_w_out, m_norm_post_mix, m_norm_pre_ffn, m_w_ffn_in, m_w_ffn_out, m_norm_post_ffn, v_norm_pre_mix, v_w_in, v_w_gate_up, v_b_gate, v_gla_norm, v_sgu_ln_g, v_sgu_ln_b, v_w_spatial, v_b_spatial, v_w_branch_gla, v_w_branch_sgu, v_w_out, v_norm_post_mix, v_norm_pre_ffn, v_w_ffn_in, v_w_ffn_out, v_norm_post_ffn):
    chip = 2 * lax.axis_index("x") + lax.axis_index("y")
    xt, tgt = x[0], loss_target[0]

    tiny = jnp.concatenate([w_gate_up[0], _pad_rows(gla_norm[0]), _pad_rows(sgu_ln_g[0]), _pad_rows(sgu_ln_b[0]),
                            jnp.zeros((24, LANES), F32)], axis=0)

    def with_own(gathered, own):
        return lax.dynamic_update_slice(gathered, own[None], (chip, 0, 0))

    w_in_t, m_in_t, v_in_t = w_in[0].T, m_w_in[0].T, v_w_in[0].T
    w_in_b = _transposed_cast(w_in_t)
    (*own_rows, fi_top, fi_bot), (g_in, g_tiny) = _cast_weights(
        [w_branch_gla[0], w_branch_sgu[0], w_out[0], w_ffn_out[0]], w_ffn_in[0], job=_job_gather([w_in_b, tiny]))
    g_tiny = with_own(g_tiny, tiny)
    w_all = _relayout_w_in(with_own(g_in, w_in_b))
    cols = lambda a: a.transpose(1, 0, 2).reshape(a.shape[1], N_CHIPS * a.shape[2])
    wgu = jnp.pad(cols(g_tiny[:, 0:16]), ((0, LANES - GLA_RANK), (0, 0)))
    gn = cols(g_tiny[:, 16:20, :64]).reshape(1, GLA_V)
    ln_g = cols(g_tiny[:, 24:28, :64]).reshape(1, 1024)
    ln_b = cols(g_tiny[:, 32:36, :64]).reshape(1, 1024)
    b_sp_t = jnp.pad(b_spatial[0].T, ((0, 0), (0, LANES - SGU_GROUPS)))
    w_sp = w_spatial[0]

    (a, proj, alow), g_rows = _inproj_fwd(xt, norm_pre_mix, w_all, job=_job_gather(own_rows))
    rows = lambda g: g.reshape(N_CHIPS * g.shape[1], g.shape[2])
    w_bg, w_bs, w_o, w_fo = [rows(with_own(g, own)) for g, own in zip(g_rows, own_rows)]
    (y_gla, states), (g_top,) = _gla_fwd(proj, alow, wgu, b_gate, gn, job=_job_gather([fi_top]))
    (y_sgu, zg, zs, merged, mix, x1), (g_bot,) = _sgu_merge_fwd(
        xt, proj, y_gla, ln_g, ln_b, w_sp, b_sp_t, w_bg, w_bs, w_o, norm_post_mix, job=_job_gather([fi_bot]))
    h, f, dgu, dy, dx1, loss, d_gpf, d_gpo = _ffn_fwd_bwd(x1, tgt, with_own(g_top, fi_top), with_own(g_bot, fi_bot),
                                                          w_fo, norm_pre_ffn, norm_post_ffn)

    own_part = lambda c: lax.dynamic_index_in_dim(c, chip, 0, keepdims=False)
    whole = lambda hs: [[(h_, None)] for h_ in hs]
    dw_fo, _ = _tn_matmul(f, dy, "dw_ffn_out")
    dw_fo4 = _halves_view(dw_fo)
    dw_fi, (q_fo,) = _tn_matmul(h, dgu, "dw_ffn_in", job=_job_to_other_core([[(dw_fo4, 0)]]))
    c_fo = _presum(dw_fo4, q_fo, "presum_ffn_out")
    (dmix, dzg, dzs, dp_mrg, dyg, dp_sgu, d_gpm, d_wsp, d_bsp_t, d_lng, d_lnb), (s_fo, q_fi) = _merge_sgu_bwd(
        dx1, mix, proj, zg, zs, w_bg, w_bs, w_o, norm_post_mix, ln_g, ln_b, w_sp, b_sp_t,
        job=_join(_job_scatter([c_fo]), _job_to_other_core([[(dw_fi, 0)]])))
    c_fi = _presum(dw_fi, q_fi, "presum_ffn_in")
    dw_c, _ = _tn_matmul(a, dp_mrg, "dw_in_merge")
    dw_b, _ = _tn_matmul(a, dp_sgu, "dw_in_sgu")
    dw_o4 = _halves_view(_tn_matmul(merged, dmix, "dw_out")[0])
    dw_bg4 = _halves_view(_tn_matmul(y_gla, dzg, "dw_branch_gla")[0])
    dw_bs4 = _halves_view(_tn_matmul(y_sgu, dzs, "dw_branch_sgu")[0])
    h_fo = _sum_slots(own_part(c_fo), s_fo, "sum_ffn_out")
    (dp_gla, dal, d_gn, d_bg, d_wgu), (s_fi, t_fo, q_b, q_c, q_o, q_bg, q_bs) = _gla_bwd(
        proj, alow, wgu, b_gate, gn, states, dyg,
        job=_join(_job_scatter([c_fi]), _job_to_other_core(
            whole([h_fo]) + [[(dw_b, 0)], [(dw_c, 0)], [(dw_o4, 0)], [(dw_bg4, 0)], [(dw_bs4, 0)]])))
    c_o, c_bg, c_bs = (_presum(dw_o4, q_o, "presum_out"), _presum(dw_bg4, q_bg, "presum_branch_gla"),
                       _presum(dw_bs4, q_bs, "presum_branch_sgu"))
    h_fi = _sum_slots(own_part(c_fi), s_fi, "sum_ffn_in")
    dw_d, _ = _tn_matmul(a, dal, "dw_in_gate")
    dw_a, (s_o, s_bg, s_bs, t_fi, q_d) = _tn_matmul(
        a, dp_gla, "dw_in_gla",
        job=_join(_job_scatter([c_o, c_bg, c_bs]), _job_to_other_core(whole([h_fi]) + [[(dw_d, 0)]])))
    h_o, h_bg, h_bs = (_sum_slots(own_part(c_o), s_o, "sum_out"), _sum_slots(own_part(c_bg), s_bg, "sum_branch_gla"),
                       _sum_slots(own_part(c_bs), s_bs, "sum_branch_sgu"))

    grads, deltas, new_m, new_v = {}, {}, {}, {}

    def update(name, w, m, v, g_mine, g_theirs, job=None):
        (g, d, m2, v2), jres = _adamw(w[0], m[0], v[0], g_mine, g_theirs, "adamw_" + name, job=job)
        grads[name], deltas[name], new_m[name], new_v[name] = g[None], d[None], m2[None], v2[None]
        return jres

    dw_in = [(dw_a, 0), (dw_b, W_GLA), (dw_c, W_GLA + W_SGU), (dw_d, N_MAIN)]
    q_a, t_o, t_bg, t_bs = update("w_ffn_out", w_ffn_out, m_w_ffn_out, v_w_ffn_out, [h_fo], [t_fo],
                                  job=_job_to_other_core([[(dw_a, 0)]] + whole([h_o, h_bg, h_bs])))
    q_in = [q_a, q_b, q_c, q_d]
    hr_in = D_MODEL // 2
    c_in_a, _ = _presum_w_in(dw_in, q_in, 0, hr_in // 8, "presum_w_in_a")
    c_in_b, (s_in_a,) = _presum_w_in(dw_in, q_in, hr_in // 8, 7 * hr_in // 8, "presum_w_in_b",
                                     job=_job_scatter([c_in_a]))
    update("w_ffn_in", w_ffn_in, m_w_ffn_in, v_w_ffn_in, [h_fi], [t_fi])
    update("w_out", w_out, m_w_out, v_w_out, [h_o], [t_o])
    update("w_branch_gla", w_branch_gla, m_w_branch_gla, v_w_branch_gla, [h_bg], [t_bg])
    update("w_branch_sgu", w_branch_sgu, m_w_branch_sgu, v_w_branch_sgu, [h_bs], [t_bs])
    (grad_x, d_g1), (s_in_b,) = _inproj_bwd(xt, dx1, norm_pre_mix, w_all, (dp_gla, dp_sgu, dp_mrg, dal),
                                            job=_job_scatter([c_in_b]))
    h_in = [_sum_slots(own_part(c_in_a), s_in_a, "sum_w_in_a"), _sum_slots(own_part(c_in_b), s_in_b, "sum_w_in_b")]
    t_in = _run_job(_job_to_other_core(whole(h_in)), "swap_w_in")
    for store, val in zip((grads, deltas, new_m, new_v),
                          _adamw_transposed(w_in_t, m_in_t, v_in_t, h_in, t_in, "adamw_w_in")):
        store["w_in"] = val.T[None]

    small_names = ["w_spatial", "w_gate_up", "norm_pre_mix", "norm_post_mix", "norm_pre_ffn", "norm_post_ffn", "b_gate",
                   "b_spatial", "gla_norm", "sgu_ln_g", "sgu_ln_b"]
    loss_out, small = _small_adamw(
        _small_sum([d_wsp, d_wgu, d_g1, d_gpm, d_gpf, d_gpo, d_bg, d_bsp_t, d_gn, d_lng, d_lnb, loss]),
        [w_spatial, w_gate_up, norm_pre_mix, norm_post_mix, norm_pre_ffn, norm_post_ffn, b_gate, b_spatial, gla_norm,
         sgu_ln_g, sgu_ln_b],
        [m_w_spatial, m_w_gate_up, m_norm_pre_mix, m_norm_post_mix, m_norm_pre_ffn, m_norm_post_ffn, m_b_gate,
         m_b_spatial, m_gla_norm, m_sgu_ln_g, m_sgu_ln_b],
        [v_w_spatial, v_w_gate_up, v_norm_pre_mix, v_norm_post_mix, v_norm_pre_ffn, v_norm_post_ffn, v_b_gate,
         v_b_spatial, v_gla_norm, v_sgu_ln_g, v_sgu_ln_b])
    for store, vals in zip((grads, deltas, new_m, new_v), small):
        store.update(zip(small_names, vals))

    order = ["norm_pre_mix", "w_in", "w_gate_up", "b_gate", "gla_norm", "sgu_ln_g", "sgu_ln_b", "w_spatial", "b_spatial",
             "w_branch_gla", "w_branch_sgu", "w_out", "norm_post_mix", "norm_pre_ffn", "w_ffn_in", "w_ffn_out",
             "norm_post_ffn"]
    out = [loss_out, grad_x[None]]
    for store in (grads, deltas, new_m, new_v):
        out.extend(store[n] for n in order)
    return tuple(out)
```

```python
import jax
import jax.numpy as jnp
from jax import lax
from jax.experimental import pallas as pl
from jax.experimental.pallas import tpu as pltpu

F32 = jnp.float32
BF16 = jnp.bfloat16

D_MODEL = 1024
GLA_HEADS = 4
GLA_DK = 128
GLA_DV = 256
GLA_QK = GLA_HEADS * GLA_DK
GLA_V = GLA_HEADS * GLA_DV
GLA_RANK = 16
GLA_TAU = 16.0
CHUNK = 64
SGU_GROUPS = 4
SGU_BLOCK = 128
SGU_DG = 256
D_FF = 2816
EPS = 1e-6
LANES = 128

OFF_Q, OFF_K, OFF_V, OFF_R, OFF_SU, OFF_SV, OFF_GG, OFF_GS, OFF_AL = 0, 512, 1024, 2048, 3072, 4096, 5120, 6144, 7168
W_GLA, W_SGU, W_MRG = 3072, 2048, 2048
N_MAIN = 7168
N_ALL = N_MAIN + LANES
_IN_SPLITS = (GLA_QK, GLA_QK, GLA_V, GLA_V, GLA_RANK, 1024, 1024, 1024, 1024)
_IN_STARTS = tuple(sum(_IN_SPLITS[:i]) for i in range(len(_IN_SPLITS) + 1))
_IN_DST = (OFF_Q, OFF_K, OFF_V, OFF_R, OFF_AL, OFF_SU, OFF_SV, OFF_GG, OFF_GS)
D_IN = _IN_STARTS[-1]

ADAM_LR = 0.001
ADAM_B1 = 0.9
ADAM_B2 = 0.999
ADAM_EPS = 1e-08
ADAM_WD = 0.01
ADAM_STEP = 10

VMEM_LIMIT_BYTES = 56 * 1024 * 1024
N_CHIPS = 4
N_PEER = N_CHIPS - 1
N_DEV = 8
MESH = pl.DeviceIdType.MESH

_NN = (((1,), (0,)), ((), ()))
_NT = (((1,), (1,)), ((), ()))
_TN = (((0,), (0,)), ((), ()))


def _dot(a, b, dims=_NN):
    return lax.dot_general(a, b, dims, preferred_element_type=F32)


def _split(x):
    hi = x.astype(BF16)
    lo = (x - hi.astype(F32)).astype(BF16)
    return hi, lo


def _dot_bf16(a, b, dims=_NN):
    return _dot(a.astype(BF16), b.astype(BF16), dims)


def _dot_exact_lhs(m, x):
    xh, xl = _split(x)
    return _dot(m, xh) + _dot(m, xl)


def _sigmoid(x):
    return 0.5 * jnp.tanh(0.5 * x) + 0.5


def _log_sigmoid(x):
    return jnp.minimum(x, 0.0) - jnp.log(1.0 + jnp.exp(-jnp.abs(x)))


_GELU_C = 0.7978845608028654
_GELU_A = 0.044715


def _gelu_and_grad(x):
    x2 = x * x
    t = jnp.tanh(_GELU_C * (x + _GELU_A * x * x2))
    g = 0.5 * x * (1.0 + t)
    dg = 0.5 * (1.0 + t) + 0.5 * x * (1.0 - t * t) * (_GELU_C * (1.0 + 3.0 * _GELU_A * x2))
    return g, dg


def _gelu(x):
    t = jnp.tanh(_GELU_C * (x + _GELU_A * x * x * x))
    return 0.5 * x * (1.0 + t)


def _rms_stats(x):
    return lax.rsqrt(jnp.mean(x * x, axis=-1, keepdims=True) + EPS)


def _rms_bwd(dout, y, r, g):
    yhat = y * r
    dn = dout * g
    dy = r * (dn - yhat * jnp.mean(dn * yhat, axis=-1, keepdims=True))
    return dy, dout * yhat


def _whole():
    return pl.BlockSpec(memory_space=pltpu.VMEM)


def _row_tile(T, want):
    t = min(T, want)
    assert T % t == 0
    return t


def _chunk_masks(tT, upper):
    row = lax.broadcasted_iota(jnp.int32, (tT, tT), 0)
    col = lax.broadcasted_iota(jnp.int32, (tT, tT), 1)
    same = (row // CHUNK) == (col // CHUNK)
    tri = (col > row) if upper else (col < row)
    return jnp.where(same & tri, 1.0, 0.0).astype(BF16)


class _Job:
    def __init__(self, ins, out_shapes, scratch, start, finish, mid=None):
        self.ins, self.out_shapes, self.scratch = list(ins), list(out_shapes), list(scratch)
        self.start, self.finish, self.mid = start, finish, mid


def _join(*jobs):
    def split(refs, counts):
        out, at = [], 0
        for n in counts:
            out.append(refs[at:at + n])
            at += n
        return out

    ni, no, ns = [len(j.ins) for j in jobs], [len(j.out_shapes) for j in jobs], [len(j.scratch) for j in jobs]

    def start(ins, outs, scr):
        for j, a, b, c in zip(jobs, split(ins, ni), split(outs, no), split(scr, ns)):
            j.start(a, b, c)

    def finish(ins, outs, scr):
        for j, a, b, c in zip(jobs, split(ins, ni), split(outs, no), split(scr, ns)):
            j.finish(a, b, c)

    def mid(ins, outs, scr):
        for j, a, b, c in zip(jobs, split(ins, ni), split(outs, no), split(scr, ns)):
            if j.mid is not None:
                j.mid(a, b, c)

    return _Job(sum((j.ins for j in jobs), []), sum((j.out_shapes for j in jobs), []),
                sum((j.scratch for j in jobs), []), start, finish, mid if any(j.mid for j in jobs) else None)


def _mesh_pos():
    return lax.axis_index("x"), lax.axis_index("y"), lax.axis_index("c")


def _peer_chips(xi, yi):
    return [(1 - xi, yi), (xi, 1 - yi), (1 - xi, 1 - yi)]


def _half(ci, rows):
    return pl.ds(pl.multiple_of(ci * rows, 8), rows)


def _sds(shape, dtype):
    return jax.ShapeDtypeStruct(tuple(shape), dtype)


def _job_gather(arrs):
    n = len(arrs)
    kinds = 12
    Y0, Y1, X1, X0, ON_X, ON_Y, D2D = 0, 1, 2, 3, 4, 5, 6

    def copies(ins, outs, scr):
        send_sems, recv_sems = scr
        xi, yi, ci = _mesh_pos()
        me, cx, cy, cd = 2 * xi + yi, 2 * (1 - xi) + yi, 2 * xi + (1 - yi), 2 * (1 - xi) + (1 - yi)
        to_x, to_y, to_core = (1 - xi, yi, ci), (xi, 1 - yi, ci), (xi, yi, 1 - ci)
        table = []
        for k in range(n):
            qr = arrs[k].shape[0] // 4

            def rows(core, q):
                return pl.ds(pl.multiple_of((2 * core + q) * qr, 8), qr)

            def cp(kind, src, dst, to):
                s = k * kinds + kind
                return pltpu.make_async_remote_copy(src_ref=src, dst_ref=dst, send_sem=send_sems.at[s],
                                                    recv_sem=recv_sems.at[s], device_id=to, device_id_type=MESH)

            def slab(chip, core, q):
                return outs[k].at[chip, rows(core, q)]

            t = {}
            for kind, q, to, frm in ((Y0, 0, to_y, cy), (Y1, 1, to_y, cy), (X1, 1, to_x, cx), (X0, 0, to_x, cx)):
                mine = ins[k].at[rows(ci, q)]
                t[kind] = (cp(kind, mine, slab(me, ci, q), to), cp(kind, mine, slab(frm, ci, q), to))
            t[ON_X] = (cp(ON_X, slab(cy, ci, 0), slab(cy, ci, 0), to_x), cp(ON_X, slab(cy, ci, 0), slab(cd, ci, 0), to_x))
            t[ON_Y] = (cp(ON_Y, slab(cx, ci, 1), slab(cx, ci, 1), to_y), cp(ON_Y, slab(cx, ci, 1), slab(cd, ci, 1), to_y))
            for i, (chip, q) in enumerate(((cy, 0), (cy, 1), (cx, 1), (cx, 0), (cd, 0), (cd, 1))):
                t[D2D + i] = (cp(D2D + i, slab(chip, ci, q), slab(chip, ci, q), to_core),
                              cp(D2D + i, slab(chip, ci, q), slab(chip, 1 - ci, q), to_core))
            table.append(t)
        return table

    def start(ins, outs, scr):
        table = copies(ins, outs, scr)
        for kind in (Y0, X1, Y1, X0):
            for t in table:
                t[kind][0].start()

    def arrived(table, kind, then):
        for t in table:
            t[kind][1].wait_recv()
            for nxt in then:
                t[nxt][0].start()

    def mid(ins, outs, scr):
        table = copies(ins, outs, scr)
        arrived(table, Y0, (ON_X, D2D + 0))
        arrived(table, X1, (ON_Y, D2D + 2))

    def finish(ins, outs, scr):
        table = copies(ins, outs, scr)
        arrived(table, Y1, (D2D + 1,))
        arrived(table, X0, (D2D + 3,))
        arrived(table, ON_X, (D2D + 4,))
        arrived(table, ON_Y, (D2D + 5,))
        for t in table:
            for i in range(6):
                t[D2D + i][1].wait_recv()
            for kind in range(kinds):
                t[kind][0].wait_send()

    dma = pltpu.SemaphoreType.DMA
    return _Job(arrs, [_sds((N_CHIPS,) + a.shape, a.dtype) for a in arrs], [dma((n * kinds,))] * 2, start, finish, mid)


def _job_scatter(parts):
    n = len(parts)

    def copies(ins, outs, scr):
        send_sems, recv_sems = scr
        xi, yi, ci = _mesh_pos()
        res = []
        for k in range(n):
            for j, (px, py) in enumerate(_peer_chips(xi, yi)):
                s = k * N_PEER + j
                res.append(pltpu.make_async_remote_copy(
                    src_ref=ins[k].at[2 * px + py], dst_ref=outs[k].at[j], send_sem=send_sems.at[s],
                    recv_sem=recv_sems.at[s], device_id=(px, py, ci), device_id_type=MESH))
        return res

    def start(ins, outs, scr):
        for cp in copies(ins, outs, scr):
            cp.start()

    def finish(ins, outs, scr):
        for cp in copies(ins, outs, scr):
            cp.wait_recv()
            cp.wait_send()

    dma = pltpu.SemaphoreType.DMA
    return _Job(parts, [_sds((N_PEER,) + p.shape[1:], p.dtype) for p in parts], [dma((n * N_PEER,))] * 2, start, finish)


def _job_to_other_core(groups):
    pieces = [(g, a, off) for g, group in enumerate(groups) for a, off in group]
    n = len(pieces)

    def geometry(group):
        a0, off0 = group[0]
        if off0 is None:
            return a0.shape
        if a0.ndim == 4:
            return (N_CHIPS, a0.shape[2], a0.shape[3])
        return (a0.shape[0] // 2, sum(a.shape[1] for a, _ in group))

    def copies(ins, outs, scr):
        send_sems, recv_sems = scr
        xi, yi, ci = _mesh_pos()
        res = []
        for p, (g, a, off) in enumerate(pieces):
            if off is None:
                give, land = ins[p], outs[g]
            elif a.ndim == 4:
                give, land = ins[p].at[pl.ds(0, N_CHIPS), 1 - ci], outs[g]
            else:
                hr, w = a.shape[0] // 2, a.shape[1]
                give, land = ins[p].at[_half(1 - ci, hr)], outs[g].at[pl.ds(0, hr), pl.ds(off, w)]
            res.append(pltpu.make_async_remote_copy(
                src_ref=give, dst_ref=land, send_sem=send_sems.at[p], recv_sem=recv_sems.at[p],
                device_id=(xi, yi, 1 - ci), device_id_type=MESH))
        return res

    def start(ins, outs, scr):
        for cp in copies(ins, outs, scr):
            cp.start()

    def finish(ins, outs, scr):
        for cp in copies(ins, outs, scr):
            cp.wait_recv()
            cp.wait_send()

    dma = pltpu.SemaphoreType.DMA
    return _Job([a for _, a, _ in pieces], [_sds(geometry(group), group[0][0].dtype) for group in groups],
                [dma((n,))] * 2, start, finish)


def _call(body, *, name, grid, in_specs, out_specs, out_shape, args, scratch_shapes=(), parallel=False, job=None,
          by_core=False):
    n_in, n_out, n_scr = len(in_specs), len(out_specs), len(scratch_shapes)
    hbm = pl.BlockSpec(memory_space=pl.ANY)
    n_ji, n_jo = (len(job.ins), len(job.out_shapes)) if job is not None else (0, 0)
    lead = 1 if by_core else 0

    def kernel_fn(*refs):
        core, refs = refs[:lead], refs[lead:]
        ins, refs = refs[:n_in], refs[n_in:]
        j_ins, refs = refs[:n_ji], refs[n_ji:]
        outs, refs = refs[:n_out], refs[n_out:]
        j_outs, refs = refs[:n_jo], refs[n_jo:]
        scr, j_scr = refs[:n_scr], refs[n_scr:]
        if job is None:
            body(*core, *ins, *outs, *scr)
            return
        ids = [pl.program_id(d) for d in range(len(grid))]
        first = ids[0] == 0
        last = ids[0] == grid[0] - 1
        for d in range(1, len(grid)):
            first = first & (ids[d] == 0)
            last = last & (ids[d] == grid[d] - 1)

        @pl.when(first)
        def _():
            job.start(j_ins, j_outs, j_scr)

        if job.mid is not None and grid[0] >= 4:
            half_way = ids[0] == grid[0] // 2
            for d in range(1, len(grid)):
                half_way = half_way & (ids[d] == 0)

            @pl.when(half_way)
            def _():
                job.mid(j_ins, j_outs, j_scr)

        body(*core, *ins, *outs, *scr)

        @pl.when(last)
        def _():
            if job.mid is not None and grid[0] < 4:
                job.mid(j_ins, j_outs, j_scr)
            job.finish(j_ins, j_outs, j_scr)

    sem = ("parallel" if parallel and job is None else "arbitrary",) * len(grid)
    all_in = list(in_specs) + [hbm] * n_ji
    all_out = list(out_specs) + [hbm] * n_jo
    all_scratch = list(scratch_shapes) + (job.scratch if job is not None else [])
    all_shapes = list(out_shape) + (job.out_shapes if job is not None else [])
    all_args = list(args) + (job.ins if job is not None else [])
    params = pltpu.CompilerParams(dimension_semantics=sem, vmem_limit_bytes=VMEM_LIMIT_BYTES)
    if by_core:
        spec = pltpu.PrefetchScalarGridSpec(num_scalar_prefetch=1, grid=grid, in_specs=all_in, out_specs=all_out,
                                            scratch_shapes=all_scratch)
        core = lax.axis_index("c").astype(jnp.int32).reshape(1)
        res = pl.pallas_call(kernel_fn, name=name, grid_spec=spec, out_shape=all_shapes, compiler_params=params)(
            core, *all_args)
    else:
        res = pl.pallas_call(kernel_fn, name=name, grid=grid, in_specs=all_in, out_specs=all_out, out_shape=all_shapes,
                             scratch_shapes=all_scratch, compiler_params=params)(*all_args)
    return list(res[:n_out]), list(res[n_out:])


def _run_job(job, name):
    n_i, n_o = len(job.ins), len(job.out_shapes)

    def body(*refs):
        ins, outs, scr = refs[:n_i], refs[n_i:n_i + n_o], refs[n_i + n_o:]
        job.start(ins, outs, scr)
        if job.mid is not None:
            job.mid(ins, outs, scr)
        job.finish(ins, outs, scr)

    hbm = pl.BlockSpec(memory_space=pl.ANY)
    return list(pl.pallas_call(body, name=name, in_specs=[hbm] * n_i, out_specs=[hbm] * n_o, out_shape=job.out_shapes,
                               scratch_shapes=job.scratch)(*job.ins))


def _adam_values(w, m, v, g):
    m2 = ADAM_B1 * m + (1.0 - ADAM_B1) * g
    v2 = ADAM_B2 * v + (1.0 - ADAM_B2) * (g * g)
    delta = -ADAM_LR * ((m2 / (1.0 - ADAM_B1 ** ADAM_STEP)) / (jnp.sqrt(v2 / (1.0 - ADAM_B2 ** ADAM_STEP)) + ADAM_EPS)
                        + ADAM_WD * w)
    return delta, m2, v2


_P_WSP, _P_WGU, _P_NORM, _P_BG, _P_BSP, _P_HEAD, _P_LOSS, _P_ROWS = 0, 512, 576, 608, 616, 624, 720, 736


def _small_sum(dgrads):
    hr = _P_ROWS // 2

    def body(dwsp, dwgu, dg1, dgpm, dgpf, dgpo, dbg, dbspt, dgn, dlng, dlnb, loss_in, tot, pack, pair, slots, send_sems,
             recv_sems):
        xi, yi, ci = _mesh_pos()
        chip = 2 * xi + yi

        pack[...] = jnp.zeros_like(pack)
        for g in range(SGU_GROUPS):
            pack[_P_WSP + g * SGU_BLOCK:_P_WSP + (g + 1) * SGU_BLOCK] = dwsp[g]
        for j in range(N_CHIPS):
            pack[_P_WGU + GLA_RANK * j:_P_WGU + GLA_RANK * (j + 1)] = dwgu[0:GLA_RANK, LANES * j:LANES * (j + 1)]
        for k, r in enumerate((dg1, dgpm, dgpf, dgpo)):
            for q in range(8):
                pack[_P_NORM + 8 * k + q:_P_NORM + 8 * k + q + 1] = r[:, LANES * q:LANES * (q + 1)]
        for q in range(4):
            pack[_P_BG + q:_P_BG + q + 1] = dbg[:, LANES * q:LANES * (q + 1)]
        pack[_P_BSP:_P_BSP + SGU_GROUPS] = jnp.transpose(dbspt[...])[0:SGU_GROUPS]
        for k, r in enumerate((dgn, dlng, dlnb)):
            for j in range(N_CHIPS):
                for hh in range(4):
                    row = _P_HEAD + 32 * k + 8 * j + hh
                    pack[row:row + 1, 0:64] = r[:, 256 * hh + 64 * j:256 * hh + 64 * (j + 1)]
        pack[_P_LOSS:_P_LOSS + 1] = loss_in[...]

        sibling = dict(device_id=(xi, yi, 1 - ci), device_id_type=MESH)
        to_sibling = pltpu.make_async_remote_copy(src_ref=pack, dst_ref=pair, send_sem=send_sems.at[N_PEER],
                                                  recv_sem=recv_sems.at[N_PEER], **sibling)
        to_sibling.start()
        to_sibling.wait_recv()
        to_sibling.wait_send()
        pack[...] = pack[...] + pair[...]
        mine = pl.ds(pl.multiple_of(ci * hr, 8), hr)
        theirs = pl.ds(pl.multiple_of((1 - ci) * hr, 8), hr)
        slots[chip] = pack[mine, :]

        def copy(j, slot):
            px, py = _peer_chips(xi, yi)[j]
            return pltpu.make_async_remote_copy(
                src_ref=pack.at[mine], dst_ref=slots.at[slot(2 * px + py)], send_sem=send_sems.at[j],
                recv_sem=recv_sems.at[j], device_id=(px, py, ci), device_id_type=MESH)

        sends = [copy(j, lambda peer_chip: chip) for j in range(N_PEER)]
        for cp in sends:
            cp.start()
        for j in range(N_PEER):
            copy(j, lambda peer_chip: peer_chip).wait_recv()
        for cp in sends:
            cp.wait_send()
        acc = slots[0]
        for d in range(1, N_CHIPS):
            acc = acc + slots[d]
        tot[mine, :] = acc
        half_over = pltpu.make_async_remote_copy(src_ref=tot.at[mine], dst_ref=tot.at[mine], send_sem=send_sems.at[N_PEER + 1],
                                                 recv_sem=recv_sems.at[N_PEER + 1], **sibling)
        half_back = pltpu.make_async_remote_copy(src_ref=tot.at[mine], dst_ref=tot.at[theirs], send_sem=send_sems.at[N_PEER + 1],
                                                 recv_sem=recv_sems.at[N_PEER + 1], **sibling)
        half_over.start()
        half_back.wait_recv()
        half_over.wait_send()

    return pl.pallas_call(
        body, name="small_sum", in_specs=[_whole()] * 12, out_specs=_whole(), out_shape=_sds((_P_ROWS, LANES), F32),
        scratch_shapes=[pltpu.VMEM((_P_ROWS, LANES), F32), pltpu.VMEM((_P_ROWS, LANES), F32),
                        pltpu.VMEM((N_CHIPS, hr, LANES), F32),
                        pltpu.SemaphoreType.DMA((N_PEER + 2,)), pltpu.SemaphoreType.DMA((N_PEER + 2,))],
        compiler_params=pltpu.CompilerParams(vmem_limit_bytes=VMEM_LIMIT_BYTES),
    )(*dgrads)


def _small_adamw(tot, ws, ms, vs):
    n = len(ws)

    def body(*refs):
        tot = refs[0]
        w_refs, m_refs, v_refs = refs[1:1 + n], refs[1 + n:1 + 2 * n], refs[1 + 2 * n:1 + 3 * n]
        loss_out = refs[1 + 3 * n]
        outs = refs[2 + 3 * n:]
        chip = 2 * lax.axis_index("x") + lax.axis_index("y")
        loss_out[...] = tot[_P_LOSS:_P_LOSS + 1, 0:1]

        def step(k, g, pick, put):
            d, m2, v2 = _adam_values(pick(w_refs[k]), pick(m_refs[k]), pick(v_refs[k]), g)
            for o, val in zip((outs[k], outs[n + k], outs[2 * n + k], outs[3 * n + k]), (g, d, m2, v2)):
                put(o, val)

        def whole(ref):
            return ref[0]

        def put_whole(ref, val):
            ref[0] = val

        for g in range(SGU_GROUPS):
            def pick_g(ref, g=g):
                return ref[0, g]

            def put_g(ref, val, g=g):
                ref[0, g] = val

            step(0, tot[_P_WSP + g * SGU_BLOCK:_P_WSP + (g + 1) * SGU_BLOCK], pick_g, put_g)
        step(1, tot[pl.ds(pl.multiple_of(_P_WGU + GLA_RANK * chip, GLA_RANK), GLA_RANK), :], whole, put_whole)
        for k, (base, chunks) in enumerate(((_P_NORM, 8), (_P_NORM + 8, 8), (_P_NORM + 16, 8), (_P_NORM + 24, 8), (_P_BG, 4))):
            for q in range(chunks):
                def pick_q(ref, q=q):
                    return ref[:, LANES * q:LANES * (q + 1)]

                def put_q(ref, val, q=q):
                    ref[:, LANES * q:LANES * (q + 1)] = val

                step(2 + k, tot[base + q:base + q + 1], pick_q, put_q)
        step(7, tot[_P_BSP:_P_BSP + SGU_GROUPS], whole, put_whole)
        for k in range(3):
            mine = tot[pl.ds(pl.multiple_of(_P_HEAD + 32 * k + 8 * chip, 8), 8), :]
            step(8 + k, mine[0:4, 0:64], whole, put_whole)

    shapes = [_sds(w.shape, F32) for w in ws]
    res = pl.pallas_call(
        body, name="small_adamw", in_specs=[_whole()] * (1 + 3 * n), out_specs=[_whole()] * (1 + 4 * n),
        out_shape=[_sds((1, 1), F32)] + shapes * 4,
        compiler_params=pltpu.CompilerParams(vmem_limit_bytes=VMEM_LIMIT_BYTES),
    )(tot, *ws, *ms, *vs)
    return res[0].reshape(()), [list(res[1 + i * n:1 + (i + 1) * n]) for i in range(4)]


def _w_in_pieces():
    blk = D_IN // N_CHIPS
    pieces = []
    for s in range(len(_IN_SPLITS)):
        lo_s, hi_s = _IN_STARTS[s], _IN_STARTS[s + 1]
        for j in range(N_CHIPS):
            lo, hi = max(lo_s, j * blk), min(hi_s, (j + 1) * blk)
            if lo < hi:
                pieces.append((j, lo - j * blk, _IN_DST[s] + lo - lo_s, hi - lo))
    return pieces


def _relayout_w_in(gathered):
    _, rows, blk = gathered.shape
    tr = 256

    def body(g_ref, o_ref):
        o_ref[:, OFF_AL:N_ALL] = jnp.zeros((tr, LANES), BF16)
        for j, src, dst, w in _w_in_pieces():
            o_ref[:, dst:dst + w] = g_ref[j, :, src:src + w]

    res, _ = _call(body, name="relayout_w_in", grid=(rows // tr,), parallel=True,
                   in_specs=[pl.BlockSpec((N_CHIPS, tr, blk), lambda i: (0, i, 0))],
                   out_specs=[pl.BlockSpec((tr, N_ALL), lambda i: (i, 0))],
                   out_shape=[_sds((rows, N_ALL), BF16)], args=(gathered,))
    return res[0]


def _update_row_tile(rows):
    for t in range(min(rows, 256), 7, -8):
        if rows % t == 0:
            return t
    return rows


def _presum_w_in(dws, theirs, row0, rows, name, job=None):
    hr = theirs[0].shape[0]
    blk = D_IN // N_CHIPS
    tr = 64
    assert row0 % tr == 0 and rows % tr == 0
    nh, t0 = hr // tr, row0 // tr
    n = len(dws)

    def body(core_ref, *refs):
        dw_refs, q_refs, (o_ref, s_scr) = refs[:n], refs[n:2 * n], refs[2 * n:]
        for p, (a, off) in enumerate(dws):
            w = a.shape[1]
            s_scr[:, off:off + w] = (dw_refs[p][...] + q_refs[p][...]).astype(BF16)
        for j, src, dst, w in _w_in_pieces():
            o_ref[j, :, src:src + w] = s_scr[:, dst:dst + w]

    in_specs = [pl.BlockSpec((tr, a.shape[1]), lambda i, core: (i + t0 + core[0] * nh, 0)) for a, _ in dws]
    in_specs += [pl.BlockSpec((tr, q.shape[1]), lambda i, core: (i + t0, 0)) for q in theirs]
    res, jres = _call(body, name=name, grid=(rows // tr,), parallel=True, in_specs=in_specs,
                      out_specs=[pl.BlockSpec((N_CHIPS, tr, blk), lambda i, core: (0, i, 0))],
                      out_shape=[_sds((N_CHIPS, rows, blk), BF16)], scratch_shapes=[pltpu.VMEM((tr, N_ALL), BF16)],
                      args=(*[a for a, _ in dws], *theirs), job=job, by_core=True)
    return res[0], jres


def _presum(dw, theirs, name):
    if dw.ndim == 4:
        _, _, hr, c = dw.shape
        tr = _update_row_tile(hr)
        mine = pl.BlockSpec((1, 1, tr, c), lambda j, i, core: (j, core[0], i, 0))
        other = pl.BlockSpec((1, tr, c), lambda j, i, core: (j, i, 0))
    else:
        hr, c = dw.shape[0] // 2, dw.shape[1] // N_CHIPS
        tr = _update_row_tile(hr)
        nh = hr // tr
        mine = pl.BlockSpec((tr, c), lambda j, i, core: (i + core[0] * nh, j))
        other = pl.BlockSpec((tr, c), lambda j, i, core: (i, j))

    def body(core_ref, a_ref, q_ref, o_ref):
        o_ref[...] = (a_ref[...].reshape(tr, c) + q_ref[...].reshape(tr, c)).astype(BF16).reshape(o_ref.shape)

    res, _ = _call(body, name=name, grid=(N_CHIPS, hr // tr), parallel=True, in_specs=[mine, other],
                   out_specs=[pl.BlockSpec((1, tr, c), lambda j, i, core: (j, i, 0))],
                   out_shape=[_sds((N_CHIPS, hr, c), BF16)], args=(dw, theirs), by_core=True)
    return res[0]


def _sum_slots(own, slots, name):
    rows, cols = own.shape
    tr = _update_row_tile(rows)

    def body(own_ref, s_ref, o_ref):
        acc = own_ref[...].astype(F32)
        for j in range(N_PEER):
            acc = acc + s_ref[j].astype(F32)
        o_ref[...] = acc

    res, _ = _call(body, name=name, grid=(rows // tr,), parallel=True,
                   in_specs=[pl.BlockSpec((tr, cols), lambda i: (i, 0)), pl.BlockSpec((N_PEER, tr, cols), lambda i: (0, i, 0))],
                   out_specs=[pl.BlockSpec((tr, cols), lambda i: (i, 0))], out_shape=[_sds((rows, cols), F32)],
                   args=(own, slots))
    return res[0]


def _adamw(w, m, v, g_mine, g_theirs, name, job=None):
    rows, cols = w.shape
    part_rows = [p.shape[0] for p in g_mine]
    assert sum(part_rows) == rows // 2 and [p.shape[0] for p in g_theirs] == part_rows
    tr = _update_row_tile(min(part_rows))
    assert all(r % tr == 0 for r in part_rows)
    nh = (rows // 2) // tr
    starts = [sum(part_rows[:k]) // tr for k in range(len(part_rows))]
    n_parts = len(part_rows)

    def body(core_ref, w_ref, m_ref, v_ref, *rest):
        g_refs, (g_out, d_out, m_out, v_out) = rest[:-4], rest[-4:]
        step = pl.program_id(0)
        mine_here = (step // nh) == core_ref[0]
        q = step % nh
        g = None
        for k in reversed(range(n_parts)):
            val = jnp.where(mine_here, g_refs[k][...], g_refs[n_parts + k][...])
            g = val if g is None else jnp.where(q < starts[k + 1], val, g)
        d, m2, v2 = _adam_values(w_ref[...], m_ref[...], v_ref[...], g)
        g_out[...] = g
        m_out[...] = m2
        v_out[...] = v2
        d_out[...] = d

    def g_spec(k, mine):
        last = part_rows[k] // tr - 1

        def index(i, core):
            half = core[0] if mine else 1 - core[0]
            here = jnp.clip(i % nh - starts[k], 0, last)
            return (jnp.where(i // nh == half, here, jnp.where(i // nh > half, last, 0)), 0)

        return pl.BlockSpec((tr, cols), index)

    spec = pl.BlockSpec((tr, cols), lambda i, core: (i, 0))
    g_specs = [g_spec(k, True) for k in range(n_parts)] + [g_spec(k, False) for k in range(n_parts)]
    return _call(body, name=name, grid=(rows // tr,), parallel=True, in_specs=[spec] * 3 + g_specs,
                 out_specs=[spec] * 4, out_shape=[_sds((rows, cols), F32)] * 4, args=(w, m, v, *g_mine, *g_theirs),
                 job=job, by_core=True)


def _transposed_cast(wt):
    cols, rows = wt.shape

    def body(x_ref, o_ref):
        o_ref[...] = jnp.transpose(x_ref[...]).astype(BF16)

    res, _ = _call(body, name="transpose_w_in", grid=(pl.cdiv(cols, LANES),), parallel=True,
                   in_specs=[pl.BlockSpec((LANES, rows), lambda j: (j, 0))],
                   out_specs=[pl.BlockSpec((rows, LANES), lambda j: (0, j))], out_shape=[_sds((rows, cols), BF16)],
                   args=(wt,))
    return res[0]


def _cast_weights(ws, w_fi, job=None):
    steps = 4
    cols = w_fi.shape[1]
    tile = w_fi.shape[0] // (2 * steps)

    def body(*refs):
        for i_ref, o_ref in zip(refs[:len(refs) // 2], refs[len(refs) // 2:]):
            o_ref[...] = i_ref[...].astype(BF16)

    row_specs = [pl.BlockSpec((w.shape[0] // steps, w.shape[1]), lambda i: (i, 0)) for w in ws]
    half_spec = pl.BlockSpec((tile, cols), lambda i: (i, 0))
    return _call(body, name="cast_weights", grid=(steps,), parallel=True,
                 in_specs=row_specs + [pl.BlockSpec((None, tile, cols), lambda i, k=k: (k, i, 0)) for k in range(2)],
                 out_specs=row_specs + [half_spec, half_spec],
                 out_shape=[_sds(w.shape, BF16) for w in ws] + [_sds((steps * tile, cols), BF16)] * 2,
                 args=(*ws, w_fi.reshape(2, steps * tile, cols), w_fi.reshape(2, steps * tile, cols)), job=job)


def _adamw_transposed(wt, mt, vt, g_mine, g_theirs, name):
    cols, rows = wt.shape
    n_parts = len(g_mine)

    def body(w_ref, m_ref, v_ref, *rest):
        g_refs, (g_out, d_out, m_out, v_out) = rest[:-4], rest[-4:]
        mine = jnp.concatenate([r[...] for r in g_refs[:n_parts]], axis=0)
        theirs = jnp.concatenate([r[...] for r in g_refs[n_parts:]], axis=0)
        first = lax.axis_index("c") == 0
        g = jnp.transpose(jnp.concatenate([jnp.where(first, mine, theirs), jnp.where(first, theirs, mine)], axis=0))
        d, m2, v2 = _adam_values(w_ref[...], m_ref[...], v_ref[...], g)
        g_out[...] = g
        m_out[...] = m2
        v_out[...] = v2
        d_out[...] = d

    spec = pl.BlockSpec((LANES, rows), lambda j: (j, 0))
    g_specs = [pl.BlockSpec((p.shape[0], LANES), lambda j: (0, j)) for p in g_mine] * 2
    res, _ = _call(body, name=name, grid=(pl.cdiv(cols, LANES),), parallel=True, in_specs=[spec] * 3 + g_specs,
                   out_specs=[spec] * 4, out_shape=[_sds((cols, rows), F32)] * 4, args=(wt, mt, vt, *g_mine, *g_theirs))
    return res


def _inproj_fwd(x, g1, w_all, job=None):
    T = x.shape[0]
    tT = _row_tile(T, 512)

    def body(x_ref, g_ref, w_ref, a_ref, proj_ref, alow_ref):
        xv = x_ref[...]
        a = (xv * _rms_stats(xv) * g_ref[...]).astype(BF16)
        a_ref[...] = a
        for j in range(N_MAIN // 1024):
            cols = slice(j * 1024, (j + 1) * 1024)
            proj_ref[:, cols] = _dot(a, w_ref[:, cols]).astype(BF16)
        alow_ref[...] = _dot(a, w_ref[:, N_MAIN:N_ALL])

    row = lambda w: pl.BlockSpec((tT, w), lambda i: (i, 0))
    return _call(
        body, name="inproj_fwd", grid=(T // tT,), parallel=True,
        in_specs=[row(D_MODEL), pl.BlockSpec((1, D_MODEL), lambda i: (0, 0)), _whole()],
        out_specs=[row(D_MODEL), row(N_MAIN), row(LANES)],
        out_shape=[_sds((T, D_MODEL), BF16), _sds((T, N_MAIN), BF16), _sds((T, LANES), F32)],
        args=(x, g1, w_all), job=job)


def _gla_decay_terms(al_ref, wgu_ref, bg_ref, later_ref):
    logit = _dot_bf16(al_ref[...], wgu_ref[...]) + bg_ref[...]
    la = _log_sigmoid(logit) * (1.0 / GLA_TAU)
    delta = _dot_exact_lhs(later_ref[...], la)
    return logit, la, delta


def _gla_fwd(proj, alow, wgu, b_gate, gn, job=None):
    T = proj.shape[0]
    tT = _row_tile(T, 512)
    nc = tT // CHUNK

    def body(q_ref, k_ref, v_ref, r_ref, al_ref, wgu_ref, bg_ref, gn_ref, later_ref, y_ref, st_ref, s_scr):
        @pl.when(pl.program_id(0) == 0)
        def _():
            s_scr[...] = jnp.zeros_like(s_scr)

        _, la, delta = _gla_decay_terms(al_ref, wgu_ref, bg_ref, later_ref)
        kdec = (k_ref[...].astype(F32) * jnp.exp(delta)).astype(BF16)
        heads = range(GLA_HEADS)
        kcs = [slice(h * GLA_DK, (h + 1) * GLA_DK) for h in heads]
        vcs = [slice(h * GLA_DV, (h + 1) * GLA_DV) for h in heads]
        state = [s_scr[h] for h in heads]
        for c in range(nc):
            rows = slice(c * CHUNK, (c + 1) * CHUNK)
            first = slice(c * CHUNK, c * CHUNK + 1)
            dec = jnp.exp(la[first, :] + delta[first, :])
            upd_t = [_dot(v_ref[rows, vcs[h]], kdec[rows, kcs[h]], _TN) for h in heads]
            qs = [(q_ref[rows, kcs[h]].astype(F32) * (GLA_DK ** -0.5)).astype(BF16) for h in heads]
            for h in heads:
                state[h] = state[h] * dec[:, kcs[h]] + upd_t[h]
                st_ref[c, h] = state[h]
            o = [_dot(qs[h], state[h].astype(BF16), _NT) for h in heads]
            for h in heads:
                on = o[h] * _rms_stats(o[h]) * gn_ref[:, vcs[h]]
                rr = r_ref[rows, vcs[h]].astype(F32)
                y_ref[rows, vcs[h]] = (on * (rr * _sigmoid(rr))).astype(BF16)
        for h in heads:
            s_scr[h] = state[h]

    blk = lambda w, j: pl.BlockSpec((tT, w), lambda i: (i, j))
    return _call(
        body, name="gla_fwd", grid=(T // tT,),
        in_specs=[blk(512, 0), blk(512, 1), blk(1024, 1), blk(1024, 2), blk(LANES, 0)] + [_whole()] * 4,
        out_specs=[pl.BlockSpec((tT, GLA_V), lambda i: (i, 0)),
                   pl.BlockSpec((nc, GLA_HEADS, GLA_DV, GLA_DK), lambda i: (i, 0, 0, 0))],
        out_shape=[_sds((T, GLA_V), BF16), _sds((T // CHUNK, GLA_HEADS, GLA_DV, GLA_DK), F32)],
        scratch_shapes=[pltpu.VMEM((GLA_HEADS, GLA_DV, GLA_DK), F32)],
        args=(proj, proj, proj, proj, alow, wgu, b_gate, gn, _chunk_masks(tT, upper=True)), job=job)


def _sgu_mask():
    i = lax.broadcasted_iota(jnp.int32, (SGU_BLOCK, SGU_BLOCK), 0)
    j = lax.broadcasted_iota(jnp.int32, (SGU_BLOCK, SGU_BLOCK), 1)
    return lax.shift_right_logical(j, 6) <= lax.shift_right_logical(i, 6)


def _sgu_merge_fwd(x, proj, y_gla, ln_g, ln_b, w_sp, b_sp_t, w_bg, w_bs, w_o, g_pm, job=None):
    T = x.shape[0]
    tT = _row_tile(T, 512)
    nb = tT // SGU_BLOCK

    def body(x_ref, su_ref, sv_ref, gg_ref, gs_ref, yg_ref, lg_ref, lb_ref, w_ref, b_ref, wbg_ref, wbs_ref, wo_ref,
             g_ref, ys_ref, zg_ref, zs_ref, mg_ref, mix_ref, x1_ref):
        mask = _sgu_mask()
        for g in range(SGU_GROUPS):
            gc = slice(g * SGU_DG, (g + 1) * SGU_DG)
            wm = jnp.where(mask, w_ref[g], 0.0).astype(BF16)
            vf = _gelu(sv_ref[:, gc].astype(F32))
            mu = jnp.mean(vf, axis=-1, keepdims=True)
            vc = vf - mu
            rstd = lax.rsqrt(jnp.mean(vc * vc, axis=-1, keepdims=True) + EPS)
            vn = (vc * rstd * lg_ref[:, gc] + lb_ref[:, gc]).astype(BF16)
            u = _gelu(su_ref[:, gc].astype(F32))
            for b in range(nb):
                rows = slice(b * SGU_BLOCK, (b + 1) * SGU_BLOCK)
                mixed = _dot(wm, vn[rows, :]) + b_ref[:, g:g + 1]
                ys_ref[rows, gc] = (u[rows, :] * mixed).astype(BF16)
        zg = _dot(yg_ref[...], wbg_ref[...])
        zs = _dot(ys_ref[...], wbs_ref[...])
        zg_ref[...] = zg.astype(BF16)
        zs_ref[...] = zs.astype(BF16)
        merged = (_sigmoid(gg_ref[...].astype(F32)) * zg + _sigmoid(gs_ref[...].astype(F32)) * zs).astype(BF16)
        mg_ref[...] = merged
        mix = _dot(merged, wo_ref[...])
        mix_ref[...] = mix.astype(BF16)
        x1_ref[...] = x_ref[...] + mix * _rms_stats(mix) * g_ref[...]

    row = pl.BlockSpec((tT, D_MODEL), lambda i: (i, 0))
    blk = lambda j: pl.BlockSpec((tT, 1024), lambda i: (i, j))
    sds = lambda dt: _sds((T, D_MODEL), dt)
    return _call(body, name="sgu_merge_fwd", grid=(T // tT,), parallel=True,
                 in_specs=[row, blk(3), blk(4), blk(5), blk(6), row] + [_whole()] * 7
                 + [pl.BlockSpec((1, D_MODEL), lambda i: (0, 0))],
                 out_specs=[row] * 6, out_shape=[sds(BF16)] * 5 + [sds(F32)],
                 args=(x, proj, proj, proj, proj, y_gla, ln_g, ln_b, w_sp, b_sp_t, w_bg, w_bs, w_o, g_pm), job=job)


def _ffn_fwd_bwd(x1, tgt, w_fi_top, w_fi_bot, w_fo, g_pf, g_po):
    T = x1.shape[0]
    tT = _row_tile(T, 256)
    half = D_FF // 2
    kh = D_MODEL // 2

    def body(x1_ref, t_ref, top_ref, bot_ref, wfo_ref, gpf_ref, gpo_ref,
             h_ref, f_ref, dgu_ref, dy_ref, dx1_ref, loss_ref, dgpf_ref, dgpo_ref, gu_scr):
        @pl.when(pl.program_id(0) == 0)
        def _():
            loss_ref[...] = jnp.zeros_like(loss_ref)
            dgpf_ref[...] = jnp.zeros_like(dgpf_ref)
            dgpo_ref[...] = jnp.zeros_like(dgpo_ref)

        main = (half // 256) * 256
        pieces = (0, 1, None)

        def w_in_cols(ref, first_slab, p):
            if p is not None:
                return ref[first_slab + p, :, :main]
            return jnp.concatenate([ref[first_slab, :, main:], ref[first_slab + 1, :, main:]], axis=1)

        def w_out_rows(p):
            if p is not None:
                return wfo_ref[p * half:p * half + main, :]
            return jnp.concatenate([wfo_ref[main:half, :], wfo_ref[half + main:2 * half, :]], axis=0)

        def put(ref, base, p, val):
            if p is not None:
                ref[:, base + p * half:base + p * half + main] = val
            else:
                ref[:, base + main:base + half] = val[:, :half - main]
                ref[:, base + half + main:base + 2 * half] = val[:, half - main:]

        def get(ref, base, p):
            if p is not None:
                return ref[:, base + p * half:base + p * half + main]
            return jnp.concatenate([ref[:, base + main:base + half], ref[:, base + half + main:base + 2 * half]], axis=1)

        x1v = x1_ref[...]
        r2 = _rms_stats(x1v)
        h = (x1v * r2 * gpf_ref[...]).astype(BF16)
        h_ref[...] = h
        y = jnp.zeros((tT, D_MODEL), F32)
        for p in pieces:
            gate = _dot(h[:, :kh], w_in_cols(top_ref, 0, p)) + _dot(h[:, kh:], w_in_cols(bot_ref, 0, p))
            up = _dot(h[:, :kh], w_in_cols(top_ref, 2, p)) + _dot(h[:, kh:], w_in_cols(bot_ref, 2, p))
            put(gu_scr, 0, p, gate)
            put(gu_scr, D_FF, p, up)
            f = (gate * _sigmoid(gate) * up).astype(BF16)
            put(f_ref, 0, p, f)
            y = y + _dot(f, w_out_rows(p))
        r3 = _rms_stats(y)
        x2 = x1v + y * r3 * gpo_ref[...]
        err = x2 - t_ref[...]
        loss_ref[...] += jnp.sum(err * err) * (0.5 / D_MODEL)
        dx2 = err * (1.0 / D_MODEL)
        dy, dg = _rms_bwd(dx2, y, r3, gpo_ref[...])
        dgpo_ref[...] += jnp.sum(dg, axis=0, keepdims=True)
        dyb = dy.astype(BF16)
        dy_ref[...] = dyb
        dh_top = jnp.zeros((tT, kh), F32)
        dh_bot = jnp.zeros((tT, kh), F32)
        for p in pieces:
            df = _dot(dyb, w_out_rows(p), _NT)
            gate = get(gu_scr, 0, p)
            up = get(gu_scr, D_FF, p)
            sg = _sigmoid(gate)
            dgate = (df * up * (sg * (1.0 + gate * (1.0 - sg)))).astype(BF16)
            dup = (df * (gate * sg)).astype(BF16)
            put(dgu_ref, 0, p, dgate)
            put(dgu_ref, D_FF, p, dup)
            dh_top = dh_top + _dot(dgate, w_in_cols(top_ref, 0, p), _NT) + _dot(dup, w_in_cols(top_ref, 2, p), _NT)
            dh_bot = dh_bot + _dot(dgate, w_in_cols(bot_ref, 0, p), _NT) + _dot(dup, w_in_cols(bot_ref, 2, p), _NT)
        dh = jnp.concatenate([dh_top, dh_bot], axis=1)
        dx1n, dg2 = _rms_bwd(dh, x1v, r2, gpf_ref[...])
        dgpf_ref[...] += jnp.sum(dg2, axis=0, keepdims=True)
        dx1_ref[...] = dx2 + dx1n

    row = lambda w: pl.BlockSpec((tT, w), lambda i: (i, 0))
    vec = pl.BlockSpec((1, D_MODEL), lambda i: (0, 0))
    res, _ = _call(
        body, name="ffn_fwd_bwd", grid=(T // tT,),
        in_specs=[row(D_MODEL), row(D_MODEL), _whole(), _whole(), _whole(), vec, vec],
        out_specs=[row(D_MODEL), row(D_FF), row(2 * D_FF), row(D_MODEL), row(D_MODEL),
                   pl.BlockSpec((1, LANES), lambda i: (0, 0)), vec, vec],
        out_shape=[_sds((T, D_MODEL), BF16), _sds((T, D_FF), BF16), _sds((T, 2 * D_FF), BF16), _sds((T, D_MODEL), BF16),
                   _sds((T, D_MODEL), F32), _sds((1, LANES), F32), _sds((1, D_MODEL), F32), _sds((1, D_MODEL), F32)],
        scratch_shapes=[pltpu.VMEM((tT, 2 * D_FF), F32)], args=(x1, tgt, w_fi_top, w_fi_bot, w_fo, g_pf, g_po))
    return res


def _merge_sgu_bwd(dx1, mix, proj, zg, zs, w_bg, w_bs, w_o, g_pm, ln_g, ln_b, w_sp, b_sp_t, job=None):
    T = dx1.shape[0]
    tT = _row_tile(T, 256)
    nb = tT // SGU_BLOCK

    def body(dx1_ref, mix_ref, su_ref, sv_ref, gg_ref, gs_ref, zg_ref, zs_ref, wbg_ref, wbs_ref, wo_ref, g_ref,
             lg_ref, lb_ref, w_ref, b_ref,
             dmix_ref, dzg_ref, dzs_ref, dgate_ref, dyg_ref, dp_ref, dgpm_ref, dw_ref, dbt_ref, dlg_ref, dlb_ref):
        @pl.when(pl.program_id(0) == 0)
        def _():
            for ref in (dgpm_ref, dw_ref, dbt_ref, dlg_ref, dlb_ref):
                ref[...] = jnp.zeros_like(ref)

        mix = mix_ref[...].astype(F32)
        dmix, dg = _rms_bwd(dx1_ref[...], mix, _rms_stats(mix), g_ref[...])
        dgpm_ref[...] += jnp.sum(dg, axis=0, keepdims=True)
        dmb = dmix.astype(BF16)
        dmix_ref[...] = dmb
        dmerged = _dot(dmb, wo_ref[...], _NT)
        dys = None
        for k, (gate_ref, z_ref, w_br_ref, dz_ref) in enumerate(((gg_ref, zg_ref, wbg_ref, dzg_ref),
                                                                 (gs_ref, zs_ref, wbs_ref, dzs_ref))):
            sg = _sigmoid(gate_ref[...].astype(F32))
            dz = (dmerged * sg).astype(BF16)
            dz_ref[...] = dz
            dgate_ref[:, k * 1024:(k + 1) * 1024] = (dmerged * z_ref[...].astype(F32) * (sg * (1.0 - sg))).astype(BF16)
            dy_branch = _dot(dz, w_br_ref[...], _NT)
            if k == 0:
                dyg_ref[...] = dy_branch.astype(BF16)
            else:
                dys = dy_branch

        mask = _sgu_mask()
        lane = lax.broadcasted_iota(jnp.int32, (SGU_BLOCK, LANES), 1)
        for g in range(SGU_GROUPS):
            gc = slice(g * SGU_DG, (g + 1) * SGU_DG)
            gc_v = slice(1024 + g * SGU_DG, 1024 + (g + 1) * SGU_DG)
            wm = jnp.where(mask, w_ref[g], 0.0).astype(BF16)
            vf, dvf_dsv = _gelu_and_grad(sv_ref[:, gc].astype(F32))
            mu = jnp.mean(vf, axis=-1, keepdims=True)
            vc = vf - mu
            rstd = lax.rsqrt(jnp.mean(vc * vc, axis=-1, keepdims=True) + EPS)
            vhat = vc * rstd
            vn = (vhat * lg_ref[:, gc] + lb_ref[:, gc]).astype(BF16)
            u, du_dsu = _gelu_and_grad(su_ref[:, gc].astype(F32))
            dy = dys[:, gc]
            dmixed = (dy * u).astype(BF16)
            dvn_parts = []
            dw_acc = jnp.zeros((SGU_BLOCK, SGU_BLOCK), F32)
            db_acc = jnp.zeros((SGU_BLOCK, 1), F32)
            for b in range(nb):
                rows = slice(b * SGU_BLOCK, (b + 1) * SGU_BLOCK)
                mixed = _dot(wm, vn[rows, :]) + b_ref[:, g:g + 1]
                dp_ref[rows, gc] = (dy[rows, :] * mixed * du_dsu[rows, :]).astype(BF16)
                dvn_parts.append(_dot(wm, dmixed[rows, :], _TN))
                dw_acc = dw_acc + _dot(dmixed[rows, :], vn[rows, :], _NT)
                db_acc = db_acc + jnp.sum(dmixed[rows, :].astype(F32), axis=-1, keepdims=True)
            dw_ref[g] += jnp.where(mask, dw_acc, 0.0)
            dbt_ref[...] += jnp.where(lane == g, db_acc, 0.0)
            dvn = jnp.concatenate(dvn_parts, axis=0)
            dlg_ref[:, gc] += jnp.sum(dvn * vhat, axis=0, keepdims=True)
            dlb_ref[:, gc] += jnp.sum(dvn, axis=0, keepdims=True)
            dvh = dvn * lg_ref[:, gc]
            dvf = rstd * (dvh - jnp.mean(dvh, axis=-1, keepdims=True)
                          - vhat * jnp.mean(dvh * vhat, axis=-1, keepdims=True))
            dp_ref[:, gc_v] = (dvf * dvf_dsv).astype(BF16)

    row = pl.BlockSpec((tT, D_MODEL), lambda i: (i, 0))
    blk = lambda j: pl.BlockSpec((tT, 1024), lambda i: (i, j))
    vec = pl.BlockSpec((1, D_MODEL), lambda i: (0, 0))
    wide = lambda w: pl.BlockSpec((tT, w), lambda i: (i, 0))
    sds = _sds((T, D_MODEL), BF16)
    return _call(
        body, name="merge_sgu_bwd", grid=(T // tT,),
        in_specs=[row, row, blk(3), blk(4), blk(5), blk(6), row, row] + [_whole()] * 3 + [vec] + [_whole()] * 4,
        out_specs=[row, row, row, wide(W_MRG), row, wide(W_SGU), vec,
                   pl.BlockSpec((SGU_GROUPS, SGU_BLOCK, SGU_BLOCK), lambda i: (0, 0, 0)),
                   pl.BlockSpec((SGU_BLOCK, LANES), lambda i: (0, 0)), vec, vec],
        out_shape=[sds, sds, sds, _sds((T, W_MRG), BF16), sds, _sds((T, W_SGU), BF16), _sds((1, D_MODEL), F32),
                   _sds((SGU_GROUPS, SGU_BLOCK, SGU_BLOCK), F32), _sds((SGU_BLOCK, LANES), F32),
                   _sds((1, 1024), F32), _sds((1, 1024), F32)],
        args=(dx1, mix, proj, proj, proj, proj, zg, zs, w_bg, w_bs, w_o, g_pm, ln_g, ln_b, w_sp, b_sp_t), job=job)


def _gla_bwd(proj, alow, wgu, b_gate, gn, states, dy_gla, job=None):
    T = proj.shape[0]
    tT = _row_tile(T, 512)
    nc = tT // CHUNK
    nt = T // tT

    def body(q_ref, k_ref, v_ref, r_ref, al_ref, wgu_ref, bg_ref, gn_ref, later_ref, earlier_ref, st_ref, sp_ref, dy_ref,
             dp_ref, dal_ref, dgn_ref, dbg_ref, dwgu_ref, g_scr, dd_scr, dt_scr):
        step = pl.program_id(0)

        @pl.when(step == 0)
        def _():
            g_scr[...] = jnp.zeros_like(g_scr)
            dgn_ref[...] = jnp.zeros_like(dgn_ref)
            dbg_ref[...] = jnp.zeros_like(dbg_ref)
            dwgu_ref[...] = jnp.zeros_like(dwgu_ref)

        has_prev = jnp.where(step == nt - 1, 0.0, 1.0)
        logit, la, delta = _gla_decay_terms(al_ref, wgu_ref, bg_ref, later_ref)
        e = jnp.exp(delta)
        kdec_f = k_ref[...].astype(F32) * e
        kdec = kdec_f.astype(BF16)
        heads = range(GLA_HEADS)
        kcs = [slice(h * GLA_DK, (h + 1) * GLA_DK) for h in heads]
        vcs = [slice(h * GLA_DV, (h + 1) * GLA_DV) for h in heads]
        carry = [g_scr[h] for h in heads]
        dgn_acc = [jnp.zeros((1, GLA_DV), F32) for _ in heads]
        for c in reversed(range(nc)):
            rows = slice(c * CHUNK, (c + 1) * CHUNK)
            first = slice(c * CHUNK, c * CHUNK + 1)
            dec = jnp.exp(la[first, :] + delta[first, :])
            s_b = [st_ref[c, h].astype(BF16) for h in heads]
            qs = [(q_ref[rows, kcs[h]].astype(F32) * (GLA_DK ** -0.5)).astype(BF16) for h in heads]
            o = [_dot(qs[h], s_b[h], _NT) for h in heads]
            do = []
            for h in heads:
                rstd = _rms_stats(o[h])
                ohat = o[h] * rstd
                gnh = gn_ref[:, vcs[h]]
                dy = dy_ref[rows, vcs[h]].astype(F32)
                rr = r_ref[rows, vcs[h]].astype(F32)
                sg = _sigmoid(rr)
                don = dy * (rr * sg)
                dp_ref[rows, OFF_R + h * GLA_DV:OFF_R + (h + 1) * GLA_DV] = (
                    dy * (ohat * gnh) * (sg * (1.0 + rr * (1.0 - sg)))).astype(BF16)
                dgn_acc[h] = dgn_acc[h] + jnp.sum(don * ohat, axis=0, keepdims=True)
                dn = don * gnh
                do.append((rstd * (dn - ohat * jnp.mean(dn * ohat, axis=-1, keepdims=True))).astype(BF16))
            dq = [_dot(do[h], s_b[h]) for h in heads]
            g_t = [_dot(do[h], qs[h], _TN) + carry[h] for h in heads]
            g_b = [g_t[h].astype(BF16) for h in heads]
            dv = [_dot(kdec[rows, kcs[h]], g_b[h], _NT) for h in heads]
            dkdec = [_dot(v_ref[rows, vcs[h]], g_b[h]) for h in heads]
            for h in heads:
                s_prev = st_ref[c - 1, h] if c > 0 else sp_ref[0, h] * has_prev
                ddec = jnp.sum(g_t[h] * s_prev, axis=0, keepdims=True)
                carry[h] = g_t[h] * dec[:, kcs[h]]
                dp_ref[rows, OFF_Q + h * GLA_DK:OFF_Q + (h + 1) * GLA_DK] = (dq[h] * (GLA_DK ** -0.5)).astype(BF16)
                dp_ref[rows, OFF_V + h * GLA_DV:OFF_V + (h + 1) * GLA_DV] = dv[h].astype(BF16)
                dp_ref[rows, OFF_K + h * GLA_DK:OFF_K + (h + 1) * GLA_DK] = (dkdec[h] * e[rows, kcs[h]]).astype(BF16)
                dd_scr[rows, kcs[h]] = dkdec[h] * kdec_f[rows, kcs[h]]
                dt_scr[rows, kcs[h]] = jnp.broadcast_to(ddec * dec[:, kcs[h]], (CHUNK, GLA_DK))
        for h in heads:
            g_scr[h] = carry[h]
            dgn_ref[:, vcs[h]] += dgn_acc[h]
        dla = _dot_exact_lhs(earlier_ref[...], dd_scr[...]) + dt_scr[...]
        dlogit = dla * (1.0 / GLA_TAU) * _sigmoid(-logit)
        dbg_ref[...] += jnp.sum(dlogit, axis=0, keepdims=True)
        dwgu_ref[...] += _dot_bf16(al_ref[...], dlogit, _TN)
        dal_ref[...] = _dot_bf16(dlogit, wgu_ref[...], _NT).astype(BF16)

    rev = lambda i: nt - 1 - i
    blk = lambda w, j: pl.BlockSpec((tT, w), lambda i: (rev(i), j))
    st_blk = pl.BlockSpec((nc, GLA_HEADS, GLA_DV, GLA_DK), lambda i: (rev(i), 0, 0, 0))
    sp_blk = pl.BlockSpec((1, GLA_HEADS, GLA_DV, GLA_DK), lambda i: (jnp.maximum(rev(i) * nc - 1, 0), 0, 0, 0))
    return _call(
        body, name="gla_bwd", grid=(nt,),
        in_specs=[blk(512, 0), blk(512, 1), blk(1024, 1), blk(1024, 2), blk(LANES, 0)] + [_whole()] * 5
        + [st_blk, sp_blk, blk(GLA_V, 0)],
        out_specs=[blk(W_GLA, 0), blk(LANES, 0), pl.BlockSpec((1, GLA_V), lambda i: (0, 0)),
                   pl.BlockSpec((1, GLA_QK), lambda i: (0, 0)), pl.BlockSpec((LANES, GLA_QK), lambda i: (0, 0))],
        out_shape=[_sds((T, W_GLA), BF16), _sds((T, LANES), BF16), _sds((1, GLA_V), F32), _sds((1, GLA_QK), F32),
                   _sds((LANES, GLA_QK), F32)],
        scratch_shapes=[pltpu.VMEM((GLA_HEADS, GLA_DV, GLA_DK), F32), pltpu.VMEM((tT, GLA_QK), F32),
                        pltpu.VMEM((tT, GLA_QK), F32)],
        args=(proj, proj, proj, proj, alow, wgu, b_gate, gn, _chunk_masks(tT, upper=True), _chunk_masks(tT, upper=False),
              states, states, dy_gla), job=job)


def _inproj_bwd(x, dx1, g1, w_all, dparts, job=None):
    T = x.shape[0]
    tT = _row_tile(T, 512)
    offs = (0, W_GLA, W_GLA + W_SGU, N_MAIN)

    def body(x_ref, dx1_ref, g_ref, w_hbm, *rest):
        part_refs, (dx_ref, dg_ref, w_ref, w_sems) = rest[:len(offs)], rest[len(offs):]

        def compute(first):
            if first:
                copies = [pltpu.make_async_copy(w_hbm.at[:, pl.ds(off, p.shape[1])], w_ref.at[:, pl.ds(off, p.shape[1])],
                                                w_sems.at[k]) for k, (off, p) in enumerate(zip(offs, dparts))]
                for cp in copies:
                    cp.start()
            da = jnp.zeros((tT, D_MODEL), F32)
            for k, (off, p_ref) in enumerate(zip(offs, part_refs)):
                if first:
                    copies[k].wait()
                da = da + _dot(p_ref[...], w_ref[:, off:off + p_ref.shape[1]], _NT)
            xv = x_ref[...]
            dx, dg = _rms_bwd(da, xv, _rms_stats(xv), g_ref[...])
            dg_sum = jnp.sum(dg, axis=0, keepdims=True)
            dg_ref[...] = dg_sum if first else dg_ref[...] + dg_sum
            dx_ref[...] = dx1_ref[...] + dx

        first_step = pl.program_id(0) == 0
        pl.when(first_step)(lambda: compute(True))
        pl.when(jnp.logical_not(first_step))(lambda: compute(False))

    row = lambda w: pl.BlockSpec((tT, w), lambda i: (i, 0))
    vec = pl.BlockSpec((1, D_MODEL), lambda i: (0, 0))
    return _call(
        body, name="inproj_bwd", grid=(T // tT,),
        in_specs=[row(D_MODEL), row(D_MODEL), vec, pl.BlockSpec(memory_space=pl.ANY)] + [row(p.shape[1]) for p in dparts],
        out_specs=[row(D_MODEL), vec], out_shape=[_sds((T, D_MODEL), F32), _sds((1, D_MODEL), F32)],
        scratch_shapes=[pltpu.VMEM(w_all.shape, BF16), pltpu.SemaphoreType.DMA((len(offs),))],
        args=(x, dx1, g1, w_all, *dparts), job=job)


def _tn_matmul(a, b, name, job=None):
    T, M = a.shape
    N = b.shape[1]
    tk = _row_tile(T, 1024)
    tm = M if M <= 1024 else 1408
    tn = N if N <= 3072 else N // 2
    assert M % tm == 0 and N % tn == 0

    def body(a_ref, b_ref, o_ref):
        @pl.when(pl.program_id(2) == 0)
        def _():
            o_ref[...] = _dot(a_ref[...], b_ref[...], _TN)

        @pl.when(pl.program_id(2) > 0)
        def _():
            o_ref[...] += _dot(a_ref[...], b_ref[...], _TN)

    res, jres = _call(
        body, name=name, grid=(M // tm, N // tn, T // tk),
        in_specs=[pl.BlockSpec((tk, tm), lambda i, j, k: (k, i)), pl.BlockSpec((tk, tn), lambda i, j, k: (k, j))],
        out_specs=[pl.BlockSpec((tm, tn), lambda i, j, k: (i, j))], out_shape=[_sds((M, N), F32)], args=(a, b), job=job)
    return res[0], jres


def _pad_rows(a, rows=8):
    return jnp.pad(a, ((0, rows - a.shape[0]), (0, LANES - a.shape[1])))


def _halves_view(dw):
    r = dw.shape[0] // N_CHIPS
    return dw.reshape(N_CHIPS, 2, r // 2, dw.shape[1])


def kernel(x, norm_pre_mix, w_in, w_gate_up, b_gate, gla_norm, sgu_ln_g, sgu_ln_b, w_spatial, b_spatial, w_branch_gla, w_branch_sgu, w_out, norm_post_mix, norm_pre_ffn, w_ffn_in, w_ffn_out, norm_post_ffn, loss_target, m_norm_pre_mix, m_w_in, m_w_gate_up, m_b_gate, m_gla_norm, m_sgu_ln_g, m_sgu_ln_b, m_w_spatial, m_b_spatial, m_w_branch_gla, m_w_branch_sgu, m_w_out, m_norm_post_mix, m_norm_pre_ffn, m_w_ffn_in, m_w_ffn_out, m_norm_post_ffn, v_norm_pre_mix, v_w_in, v_w_gate_up, v_b_gate, v_gla_norm, v_sgu_ln_g, v_sgu_ln_b, v_w_spatial, v_b_spatial, v_w_branch_gla, v_w_branch_sgu, v_w_out, v_norm_post_mix, v_norm_pre_ffn, v_w_ffn_in, v_w_ffn_out, v_norm_post_ffn):
    chip = 2 * lax.axis_index("x") + lax.axis_index("y")
    xt, tgt = x[0], loss_target[0]

    tiny = jnp.concatenate([w_gate_up[0], _pad_rows(gla_norm[0]), _pad_rows(sgu_ln_g[0]), _pad_rows(sgu_ln_b[0]),
                            jnp.zeros((24, LANES), F32)], axis=0)

    def with_own(gathered, own):
        return lax.dynamic_update_slice(gathered, own[None], (chip, 0, 0))

    w_in_t, m_in_t, v_in_t = w_in[0].T, m_w_in[0].T, v_w_in[0].T
    w_in_b = _transposed_cast(w_in_t)
    (*own_rows, fi_top, fi_bot), (g_in, g_tiny) = _cast_weights(
        [w_branch_gla[0], w_branch_sgu[0], w_out[0], w_ffn_out[0]], w_ffn_in[0], job=_job_gather([w_in_b, tiny]))
    g_tiny = with_own(g_tiny, tiny)
    w_all = _relayout_w_in(with_own(g_in, w_in_b))
    cols = lambda a: a.transpose(1, 0, 2).reshape(a.shape[1], N_CHIPS * a.shape[2])
    wgu = jnp.pad(cols(g_tiny[:, 0:16]), ((0, LANES - GLA_RANK), (0, 0)))
    gn = cols(g_tiny[:, 16:20, :64]).reshape(1, GLA_V)
    ln_g = cols(g_tiny[:, 24:28, :64]).reshape(1, 1024)
    ln_b = cols(g_tiny[:, 32:36, :64]).reshape(1, 1024)
    b_sp_t = jnp.pad(b_spatial[0].T, ((0, 0), (0, LANES - SGU_GROUPS)))
    w_sp = w_spatial[0]

    (a, proj, alow), g_rows = _inproj_fwd(xt, norm_pre_mix, w_all, job=_job_gather(own_rows))
    rows = lambda g: g.reshape(N_CHIPS * g.shape[1], g.shape[2])
    w_bg, w_bs, w_o, w_fo = [rows(with_own(g, own)) for g, own in zip(g_rows, own_rows)]
    (y_gla, states), (g_top,) = _gla_fwd(proj, alow, wgu, b_gate, gn, job=_job_gather([fi_top]))
    (y_sgu, zg, zs, merged, mix, x1), (g_bot,) = _sgu_merge_fwd(
        xt, proj, y_gla, ln_g, ln_b, w_sp, b_sp_t, w_bg, w_bs, w_o, norm_post_mix, job=_job_gather([fi_bot]))
    h, f, dgu, dy, dx1, loss, d_gpf, d_gpo = _ffn_fwd_bwd(x1, tgt, with_own(g_top, fi_top), with_own(g_bot, fi_bot),
                                                          w_fo, norm_pre_ffn, norm_post_ffn)

    own_part = lambda c: lax.dynamic_index_in_dim(c, chip, 0, keepdims=False)
    whole = lambda hs: [[(h_, None)] for h_ in hs]
    dw_fo, _ = _tn_matmul(f, dy, "dw_ffn_out")
    dw_fo4 = _halves_view(dw_fo)
    dw_fi, (q_fo,) = _tn_matmul(h, dgu, "dw_ffn_in", job=_job_to_other_core([[(dw_fo4, 0)]]))
    c_fo = _presum(dw_fo4, q_fo, "presum_ffn_out")
    (dmix, dzg, dzs, dp_mrg, dyg, dp_sgu, d_gpm, d_wsp, d_bsp_t, d_lng, d_lnb), (s_fo, q_fi) = _merge_sgu_bwd(
        dx1, mix, proj, zg, zs, w_bg, w_bs, w_o, norm_post_mix, ln_g, ln_b, w_sp, b_sp_t,
        job=_join(_job_scatter([c_fo]), _job_to_other_core([[(dw_fi, 0)]])))
    c_fi = _presum(dw_fi, q_fi, "presum_ffn_in")
    dw_c, _ = _tn_matmul(a, dp_mrg, "dw_in_merge")
    dw_b, _ = _tn_matmul(a, dp_sgu, "dw_in_sgu")
    dw_o4 = _halves_view(_tn_matmul(merged, dmix, "dw_out")[0])
    dw_bg4 = _halves_view(_tn_matmul(y_gla, dzg, "dw_branch_gla")[0])
    dw_bs4 = _halves_view(_tn_matmul(y_sgu, dzs, "dw_branch_sgu")[0])
    h_fo = _sum_slots(own_part(c_fo), s_fo, "sum_ffn_out")
    (dp_gla, dal, d_gn, d_bg, d_wgu), (s_fi, t_fo, q_b, q_c, q_o, q_bg, q_bs) = _gla_bwd(
        proj, alow, wgu, b_gate, gn, states, dyg,
        job=_join(_job_scatter([c_fi]), _job_to_other_core(
            whole([h_fo]) + [[(dw_b, 0)], [(dw_c, 0)], [(dw_o4, 0)], [(dw_bg4, 0)], [(dw_bs4, 0)]])))
    c_o, c_bg, c_bs = (_presum(dw_o4, q_o, "presum_out"), _presum(dw_bg4, q_bg, "presum_branch_gla"),
                       _presum(dw_bs4, q_bs, "presum_branch_sgu"))
    h_fi = _sum_slots(own_part(c_fi), s_fi, "sum_ffn_in")
    dw_d, _ = _tn_matmul(a, dal, "dw_in_gate")
    dw_a, (s_o, s_bg, s_bs, t_fi, q_d) = _tn_matmul(
        a, dp_gla, "dw_in_gla",
        job=_join(_job_scatter([c_o, c_bg, c_bs]), _job_to_other_core(whole([h_fi]) + [[(dw_d, 0)]])))
    h_o, h_bg, h_bs = (_sum_slots(own_part(c_o), s_o, "sum_out"), _sum_slots(own_part(c_bg), s_bg, "sum_branch_gla"),
                       _sum_slots(own_part(c_bs), s_bs, "sum_branch_sgu"))

    grads, deltas, new_m, new_v = {}, {}, {}, {}

    def update(name, w, m, v, g_mine, g_theirs, job=None):
        (g, d, m2, v2), jres = _adamw(w[0], m[0], v[0], g_mine, g_theirs, "adamw_" + name, job=job)
        grads[name], deltas[name], new_m[name], new_v[name] = g[None], d[None], m2[None], v2[None]
        return jres

    dw_in = [(dw_a, 0), (dw_b, W_GLA), (dw_c, W_GLA + W_SGU), (dw_d, N_MAIN)]
    q_a, t_o, t_bg, t_bs = update("w_ffn_out", w_ffn_out, m_w_ffn_out, v_w_ffn_out, [h_fo], [t_fo],
                                  job=_job_to_other_core([[(dw_a, 0)]] + whole([h_o, h_bg, h_bs])))
    q_in = [q_a, q_b, q_c, q_d]
    hr_in = D_MODEL // 2
    c_in_a, _ = _presum_w_in(dw_in, q_in, 0, hr_in // 8, "presum_w_in_a")
    c_in_b, (s_in_a,) = _presum_w_in(dw_in, q_in, hr_in // 8, 7 * hr_in // 8, "presum_w_in_b",
                                     job=_job_scatter([c_in_a]))
    update("w_ffn_in", w_ffn_in, m_w_ffn_in, v_w_ffn_in, [h_fi], [t_fi])
    update("w_out", w_out, m_w_out, v_w_out, [h_o], [t_o])
    update("w_branch_gla", w_branch_gla, m_w_branch_gla, v_w_branch_gla, [h_bg], [t_bg])
    update("w_branch_sgu", w_branch_sgu, m_w_branch_sgu, v_w_branch_sgu, [h_bs], [t_bs])
    (grad_x, d_g1), (s_in_b,) = _inproj_bwd(xt, dx1, norm_pre_mix, w_all, (dp_gla, dp_sgu, dp_mrg, dal),
                                            job=_job_scatter([c_in_b]))
    h_in = [_sum_slots(own_part(c_in_a), s_in_a, "sum_w_in_a"), _sum_slots(own_part(c_in_b), s_in_b, "sum_w_in_b")]
    t_in = _run_job(_job_to_other_core(whole(h_in)), "swap_w_in")
    for store, val in zip((grads, deltas, new_m, new_v),
                          _adamw_transposed(w_in_t, m_in_t, v_in_t, h_in, t_in, "adamw_w_in")):
        store["w_in"] = val.T[None]

    small_names = ["w_spatial", "w_gate_up", "norm_pre_mix", "norm_post_mix", "norm_pre_ffn", "norm_post_ffn", "b_gate",
                   "b_spatial", "gla_norm", "sgu_ln_g", "sgu_ln_b"]
    loss_out, small = _small_adamw(
        _small_sum([d_wsp, d_wgu, d_g1, d_gpm, d_gpf, d_gpo, d_bg, d_bsp_t, d_gn, d_lng, d_lnb, loss]),
        [w_spatial, w_gate_up, norm_pre_mix, norm_post_mix, norm_pre_ffn, norm_post_ffn, b_gate, b_spatial, gla_norm,
         sgu_ln_g, sgu_ln_b],
        [m_w_spatial, m_w_gate_up, m_norm_pre_mix, m_norm_post_mix, m_norm_pre_ffn, m_norm_post_ffn, m_b_gate,
         m_b_spatial, m_gla_norm, m_sgu_ln_g, m_sgu_ln_b],
        [v_w_spatial, v_w_gate_up, v_norm_pre_mix, v_norm_post_mix, v_norm_pre_ffn, v_norm_post_ffn, v_b_gate,
         v_b_spatial, v_gla_norm, v_sgu_ln_g, v_sgu_ln_b])
    for store, vals in zip((grads, deltas, new_m, new_v), small):
        store.update(zip(small_names, vals))

    order = ["norm_pre_mix", "w_in", "w_gate_up", "b_gate", "gla_norm", "sgu_ln_g", "sgu_ln_b", "w_spatial", "b_spatial",
             "w_branch_gla", "w_branch_sgu", "w_out", "norm_post_mix", "norm_pre_ffn", "w_ffn_in", "w_ffn_out",
             "norm_post_ffn"]
    out = [loss_out, grad_x[None]]
    for store in (grads, deltas, new_m, new_v):
        out.extend(store[n] for n in order)
    return tuple(out)
```

```python
import jax
import jax.numpy as jnp
from jax import lax
from jax.experimental import pallas as pl
from jax.experimental.pallas import tpu as pltpu

F32 = jnp.float32
BF16 = jnp.bfloat16

D_MODEL = 1024
GLA_HEADS = 4
GLA_DK = 128
GLA_DV = 256
GLA_QK = GLA_HEADS * GLA_DK
GLA_V = GLA_HEADS * GLA_DV
GLA_RANK = 16
GLA_TAU = 16.0
CHUNK = 64
SGU_GROUPS = 4
SGU_BLOCK = 128
SGU_DG = 256
D_FF = 2816
EPS = 1e-6
LANES = 128

OFF_Q, OFF_K, OFF_V, OFF_R, OFF_SU, OFF_SV, OFF_GG, OFF_GS, OFF_AL = 0, 512, 1024, 2048, 3072, 4096, 5120, 6144, 7168
W_GLA, W_SGU, W_MRG = 3072, 2048, 2048
N_MAIN = 7168
N_ALL = N_MAIN + LANES
_IN_SPLITS = (GLA_QK, GLA_QK, GLA_V, GLA_V, GLA_RANK, 1024, 1024, 1024, 1024)
_IN_STARTS = tuple(sum(_IN_SPLITS[:i]) for i in range(len(_IN_SPLITS) + 1))
_IN_DST = (OFF_Q, OFF_K, OFF_V, OFF_R, OFF_AL, OFF_SU, OFF_SV, OFF_GG, OFF_GS)
D_IN = _IN_STARTS[-1]

ADAM_LR = 0.001
ADAM_B1 = 0.9
ADAM_B2 = 0.999
ADAM_EPS = 1e-08
ADAM_WD = 0.01
ADAM_STEP = 10

VMEM_LIMIT_BYTES = 56 * 1024 * 1024
N_CHIPS = 4
N_PEER = N_CHIPS - 1
N_DEV = 8
MESH = pl.DeviceIdType.MESH

_NN = (((1,), (0,)), ((), ()))
_NT = (((1,), (1,)), ((), ()))
_TN = (((0,), (0,)), ((), ()))


def _dot(a, b, dims=_NN):
    return lax.dot_general(a, b, dims, preferred_element_type=F32)


def _split(x):
    hi = x.astype(BF16)
    lo = (x - hi.astype(F32)).astype(BF16)
    return hi, lo


def _dot_bf16(a, b, dims=_NN):
    return _dot(a.astype(BF16), b.astype(BF16), dims)


def _dot_exact_lhs(m, x):
    xh, xl = _split(x)
    return _dot(m, xh) + _dot(m, xl)


def _sigmoid(x):
    return 0.5 * jnp.tanh(0.5 * x) + 0.5


def _log_sigmoid(x):
    return jnp.minimum(x, 0.0) - jnp.log(1.0 + jnp.exp(-jnp.abs(x)))


_GELU_C = 0.7978845608028654
_GELU_A = 0.044715


def _gelu_and_grad(x):
    x2 = x * x
    t = jnp.tanh(_GELU_C * (x + _GELU_A * x * x2))
    g = 0.5 * x * (1.0 + t)
    dg = 0.5 * (1.0 + t) + 0.5 * x * (1.0 - t * t) * (_GELU_C * (1.0 + 3.0 * _GELU_A * x2))
    return g, dg


def _gelu(x):
    t = jnp.tanh(_GELU_C * (x + _GELU_A * x * x * x))
    return 0.5 * x * (1.0 + t)


def _rms_stats(x):
    return lax.rsqrt(jnp.mean(x * x, axis=-1, keepdims=True) + EPS)


def _rms_bwd(dout, y, r, g):
    yhat = y * r
    dn = dout * g
    dy = r * (dn - yhat * jnp.mean(dn * yhat, axis=-1, keepdims=True))
    return dy, dout * yhat


def _whole():
    return pl.BlockSpec(memory_space=pltpu.VMEM)


def _row_tile(T, want):
    t = min(T, want)
    assert T % t == 0
    return t


def _chunk_masks(tT, upper):
    row = lax.broadcasted_iota(jnp.int32, (tT, tT), 0)
    col = lax.broadcasted_iota(jnp.int32, (tT, tT), 1)
    same = (row // CHUNK) == (col // CHUNK)
    tri = (col > row) if upper else (col < row)
    return jnp.where(same & tri, 1.0, 0.0).astype(BF16)


class _Job:
    def __init__(self, ins, out_shapes, scratch, start, finish, mid=None):
        self.ins, self.out_shapes, self.scratch = list(ins), list(out_shapes), list(scratch)
        self.start, self.finish, self.mid = start, finish, mid


def _join(*jobs):
    def split(refs, counts):
        out, at = [], 0
        for n in counts:
            out.append(refs[at:at + n])
            at += n
        return out

    ni, no, ns = [len(j.ins) for j in jobs], [len(j.out_shapes) for j in jobs], [len(j.scratch) for j in jobs]

    def start(ins, outs, scr):
        for j, a, b, c in zip(jobs, split(ins, ni), split(outs, no), split(scr, ns)):
            j.start(a, b, c)

    def finish(ins, outs, scr):
        for j, a, b, c in zip(jobs, split(ins, ni), split(outs, no), split(scr, ns)):
            j.finish(a, b, c)

    def mid(ins, outs, scr):
        for j, a, b, c in zip(jobs, split(ins, ni), split(outs, no), split(scr, ns)):
            if j.mid is not None:
                j.mid(a, b, c)

    return _Job(sum((j.ins for j in jobs), []), sum((j.out_shapes for j in jobs), []),
                sum((j.scratch for j in jobs), []), start, finish, mid if any(j.mid for j in jobs) else None)


def _mesh_pos():
    return lax.axis_index("x"), lax.axis_index("y"), lax.axis_index("c")


def _peer_chips(xi, yi):
    return [(1 - xi, yi), (xi, 1 - yi), (1 - xi, 1 - yi)]


def _half(ci, rows):
    return pl.ds(pl.multiple_of(ci * rows, 8), rows)


def _sds(shape, dtype):
    return jax.ShapeDtypeStruct(tuple(shape), dtype)


def _job_gather(arrs):
    n = len(arrs)
    kinds = 12
    Y0, Y1, X1, X0, ON_X, ON_Y, D2D = 0, 1, 2, 3, 4, 5, 6

    def copies(ins, outs, scr):
        send_sems, recv_sems = scr
        xi, yi, ci = _mesh_pos()
        me, cx, cy, cd = 2 * xi + yi, 2 * (1 - xi) + yi, 2 * xi + (1 - yi), 2 * (1 - xi) + (1 - yi)
        to_x, to_y, to_core = (1 - xi, yi, ci), (xi, 1 - yi, ci), (xi, yi, 1 - ci)
        table = []
        for k in range(n):
            qr = arrs[k].shape[0] // 4

            def rows(core, q):
                return pl.ds(pl.multiple_of((2 * core + q) * qr, 8), qr)

            def cp(kind, src, dst, to):
                s = k * kinds + kind
                return pltpu.make_async_remote_copy(src_ref=src, dst_ref=dst, send_sem=send_sems.at[s],
                                                    recv_sem=recv_sems.at[s], device_id=to, device_id_type=MESH)

            def slab(chip, core, q):
                return outs[k].at[chip, rows(core, q)]

            t = {}
            for kind, q, to, frm in ((Y0, 0, to_y, cy), (Y1, 1, to_y, cy), (X1, 1, to_x, cx), (X0, 0, to_x, cx)):
                mine = ins[k].at[rows(ci, q)]
                t[kind] = (cp(kind, mine, slab(me, ci, q), to), cp(kind, mine, slab(frm, ci, q), to))
            t[ON_X] = (cp(ON_X, slab(cy, ci, 0), slab(cy, ci, 0), to_x), cp(ON_X, slab(cy, ci, 0), slab(cd, ci, 0), to_x))
            t[ON_Y] = (cp(ON_Y, slab(cx, ci, 1), slab(cx, ci, 1), to_y), cp(ON_Y, slab(cx, ci, 1), slab(cd, ci, 1), to_y))
            for i, (chip, q) in enumerate(((cy, 0), (cy, 1), (cx, 1), (cx, 0), (cd, 0), (cd, 1))):
                t[D2D + i] = (cp(D2D + i, slab(chip, ci, q), slab(chip, ci, q), to_core),
                              cp(D2D + i, slab(chip, ci, q), slab(chip, 1 - ci, q), to_core))
            table.append(t)
        return table

    def start(ins, outs, scr):
        table = copies(ins, outs, scr)
        for kind in (Y0, X1, Y1, X0):
            for t in table:
                t[kind][0].start()

    def arrived(table, kind, then):
        for t in table:
            t[kind][1].wait_recv()
            for nxt in then:
                t[nxt][0].start()

    def mid(ins, outs, scr):
        table = copies(ins, outs, scr)
        arrived(table, Y0, (ON_X, D2D + 0))
        arrived(table, X1, (ON_Y, D2D + 2))

    def finish(ins, outs, scr):
        table = copies(ins, outs, scr)
        arrived(table, Y1, (D2D + 1,))
        arrived(table, X0, (D2D + 3,))
        arrived(table, ON_X, (D2D + 4,))
        arrived(table, ON_Y, (D2D + 5,))
        for t in table:
            for i in range(6):
                t[D2D + i][1].wait_recv()
            for kind in range(kinds):
                t[kind][0].wait_send()

    dma = pltpu.SemaphoreType.DMA
    return _Job(arrs, [_sds((N_CHIPS,) + a.shape, a.dtype) for a in arrs], [dma((n * kinds,))] * 2, start, finish, mid)


def _job_scatter(parts):
    n = len(parts)

    def copies(ins, outs, scr):
        send_sems, recv_sems = scr
        xi, yi, ci = _mesh_pos()
        res = []
        for k in range(n):
            for j, (px, py) in enumerate(_peer_chips(xi, yi)):
                s = k * N_PEER + j
                res.append(pltpu.make_async_remote_copy(
                    src_ref=ins[k].at[2 * px + py], dst_ref=outs[k].at[j], send_sem=send_sems.at[s],
                    recv_sem=recv_sems.at[s], device_id=(px, py, ci), device_id_type=MESH))
        return res

    def start(ins, outs, scr):
        for cp in copies(ins, outs, scr):
            cp.start()

    def finish(ins, outs, scr):
        for cp in copies(ins, outs, scr):
            cp.wait_recv()
            cp.wait_send()

    dma = pltpu.SemaphoreType.DMA
    return _Job(parts, [_sds((N_PEER,) + p.shape[1:], p.dtype) for p in parts], [dma((n * N_PEER,))] * 2, start, finish)


def _job_to_other_core(groups):
    pieces = [(g, a, off) for g, group in enumerate(groups) for a, off in group]
    n = len(pieces)

    def geometry(group):
        a0, off0 = group[0]
        if off0 is None:
            return a0.shape
        if a0.ndim == 4:
            return (N_CHIPS, a0.shape[2], a0.shape[3])
        return (a0.shape[0] // 2, sum(a.shape[1] for a, _ in group))

    def copies(ins, outs, scr):
        send_sems, recv_sems = scr
        xi, yi, ci = _mesh_pos()
        res = []
        for p, (g, a, off) in enumerate(pieces):
            if off is None:
                give, land = ins[p], outs[g]
            elif a.ndim == 4:
                give, land = ins[p].at[pl.ds(0, N_CHIPS), 1 - ci], outs[g]
            else:
                hr, w = a.shape[0] // 2, a.shape[1]
                give, land = ins[p].at[_half(1 - ci, hr)], outs[g].at[pl.ds(0, hr), pl.ds(off, w)]
            res.append(pltpu.make_async_remote_copy(
                src_ref=give, dst_ref=land, send_sem=send_sems.at[p], recv_sem=recv_sems.at[p],
                device_id=(xi, yi, 1 - ci), device_id_type=MESH))
        return res

    def start(ins, outs, scr):
        for cp in copies(ins, outs, scr):
            cp.start()

    def finish(ins, outs, scr):
        for cp in copies(ins, outs, scr):
            cp.wait_recv()
            cp.wait_send()

    dma = pltpu.SemaphoreType.DMA
    return _Job([a for _, a, _ in pieces], [_sds(geometry(group), group[0][0].dtype) for group in groups],
                [dma((n,))] * 2, start, finish)


def _call(body, *, name, grid, in_specs, out_specs, out_shape, args, scratch_shapes=(), parallel=False, job=None,
          by_core=False):
    n_in, n_out, n_scr = len(in_specs), len(out_specs), len(scratch_shapes)
    hbm = pl.BlockSpec(memory_space=pl.ANY)
    n_ji, n_jo = (len(job.ins), len(job.out_shapes)) if job is not None else (0, 0)
    lead = 1 if by_core else 0

    def kernel_fn(*refs):
        core, refs = refs[:lead], refs[lead:]
        ins, refs = refs[:n_in], refs[n_in:]
        j_ins, refs = refs[:n_ji], refs[n_ji:]
        outs, refs = refs[:n_out], refs[n_out:]
        j_outs, refs = refs[:n_jo], refs[n_jo:]
        scr, j_scr = refs[:n_scr], refs[n_scr:]
        if job is None:
            body(*core, *ins, *outs, *scr)
            return
        ids = [pl.program_id(d) for d in range(len(grid))]
        first = ids[0] == 0
        last = ids[0] == grid[0] - 1
        for d in range(1, len(grid)):
            first = first & (ids[d] == 0)
            last = last & (ids[d] == grid[d] - 1)

        @pl.when(first)
        def _():
            job.start(j_ins, j_outs, j_scr)

        if job.mid is not None and grid[0] >= 4:
            half_way = ids[0] == grid[0] // 2
            for d in range(1, len(grid)):
                half_way = half_way & (ids[d] == 0)

            @pl.when(half_way)
            def _():
                job.mid(j_ins, j_outs, j_scr)

        body(*core, *ins, *outs, *scr)

        @pl.when(last)
        def _():
            if job.mid is not None and grid[0] < 4:
                job.mid(j_ins, j_outs, j_scr)
            job.finish(j_ins, j_outs, j_scr)

    sem = ("parallel" if parallel and job is None else "arbitrary",) * len(grid)
    all_in = list(in_specs) + [hbm] * n_ji
    all_out = list(out_specs) + [hbm] * n_jo
    all_scratch = list(scratch_shapes) + (job.scratch if job is not None else [])
    all_shapes = list(out_shape) + (job.out_shapes if job is not None else [])
    all_args = list(args) + (job.ins if job is not None else [])
    params = pltpu.CompilerParams(dimension_semantics=sem, vmem_limit_bytes=VMEM_LIMIT_BYTES)
    if by_core:
        spec = pltpu.PrefetchScalarGridSpec(num_scalar_prefetch=1, grid=grid, in_specs=all_in, out_specs=all_out,
                                            scratch_shapes=all_scratch)
        core = lax.axis_index("c").astype(jnp.int32).reshape(1)
        res = pl.pallas_call(kernel_fn, name=name, grid_spec=spec, out_shape=all_shapes, compiler_params=params)(
            core, *all_args)
    else:
        res = pl.pallas_call(kernel_fn, name=name, grid=grid, in_specs=all_in, out_specs=all_out, out_shape=all_shapes,
                             scratch_shapes=all_scratch, compiler_params=params)(*all_args)
    return list(res[:n_out]), list(res[n_out:])


def _run_job(job, name):
    n_i, n_o = len(job.ins), len(job.out_shapes)

    def body(*refs):
        ins, outs, scr = refs[:n_i], refs[n_i:n_i + n_o], refs[n_i + n_o:]
        job.start(ins, outs, scr)
        if job.mid is not None:
            job.mid(ins, outs, scr)
        job.finish(ins, outs, scr)

    hbm = pl.BlockSpec(memory_space=pl.ANY)
    return list(pl.pallas_call(body, name=name, in_specs=[hbm] * n_i, out_specs=[hbm] * n_o, out_shape=job.out_shapes,
                               scratch_shapes=job.scratch)(*job.ins))


def _adam_values(w, m, v, g):
    m2 = ADAM_B1 * m + (1.0 - ADAM_B1) * g
    v2 = ADAM_B2 * v + (1.0 - ADAM_B2) * (g * g)
    delta = -ADAM_LR * ((m2 / (1.0 - ADAM_B1 ** ADAM_STEP)) / (jnp.sqrt(v2 / (1.0 - ADAM_B2 ** ADAM_STEP)) + ADAM_EPS)
                        + ADAM_WD * w)
    return delta, m2, v2


_P_WSP, _P_WGU, _P_NORM, _P_BG, _P_BSP, _P_HEAD, _P_LOSS, _P_ROWS = 0, 512, 576, 608, 616, 624, 720, 736


def _small_sum(dgrads):
    hr = _P_ROWS // 2

    def body(dwsp, dwgu, dg1, dgpm, dgpf, dgpo, dbg, dbspt, dgn, dlng, dlnb, loss_in, tot, pack, pair, slots, send_sems,
             recv_sems):
        xi, yi, ci = _mesh_pos()
        chip = 2 * xi + yi

        pack[...] = jnp.zeros_like(pack)
        for g in range(SGU_GROUPS):
            pack[_P_WSP + g * SGU_BLOCK:_P_WSP + (g + 1) * SGU_BLOCK] = dwsp[g]
        for j in range(N_CHIPS):
            pack[_P_WGU + GLA_RANK * j:_P_WGU + GLA_RANK * (j + 1)] = dwgu[0:GLA_RANK, LANES * j:LANES * (j + 1)]
        for k, r in enumerate((dg1, dgpm, dgpf, dgpo)):
            for q in range(8):
                pack[_P_NORM + 8 * k + q:_P_NORM + 8 * k + q + 1] = r[:, LANES * q:LANES * (q + 1)]
        for q in range(4):
            pack[_P_BG + q:_P_BG + q + 1] = dbg[:, LANES * q:LANES * (q + 1)]
        pack[_P_BSP:_P_BSP + SGU_GROUPS] = jnp.transpose(dbspt[...])[0:SGU_GROUPS]
        for k, r in enumerate((dgn, dlng, dlnb)):
            for j in range(N_CHIPS):
                for hh in range(4):
                    row = _P_HEAD + 32 * k + 8 * j + hh
                    pack[row:row + 1, 0:64] = r[:, 256 * hh + 64 * j:256 * hh + 64 * (j + 1)]
        pack[_P_LOSS:_P_LOSS + 1] = loss_in[...]

        sibling = dict(device_id=(xi, yi, 1 - ci), device_id_type=MESH)
        to_sibling = pltpu.make_async_remote_copy(src_ref=pack, dst_ref=pair, send_sem=send_sems.at[N_PEER],
                                                  recv_sem=recv_sems.at[N_PEER], **sibling)
        to_sibling.start()
        to_sibling.wait_recv()
        to_sibling.wait_send()
        pack[...] = pack[...] + pair[...]
        mine = pl.ds(pl.multiple_of(ci * hr, 8), hr)
        theirs = pl.ds(pl.multiple_of((1 - ci) * hr, 8), hr)
        slots[chip] = pack[mine, :]

        def copy(j, slot):
            px, py = _peer_chips(xi, yi)[j]
            return pltpu.make_async_remote_copy(
                src_ref=pack.at[mine], dst_ref=slots.at[slot(2 * px + py)], send_sem=send_sems.at[j],
                recv_sem=recv_sems.at[j], device_id=(px, py, ci), device_id_type=MESH)

        sends = [copy(j, lambda peer_chip: chip) for j in range(N_PEER)]
        for cp in sends:
            cp.start()
        for j in range(N_PEER):
            copy(j, lambda peer_chip: peer_chip).wait_recv()
        for cp in sends:
            cp.wait_send()
        acc = slots[0]
        for d in range(1, N_CHIPS):
            acc = acc + slots[d]
        tot[mine, :] = acc
        half_over = pltpu.make_async_remote_copy(src_ref=tot.at[mine], dst_ref=tot.at[mine], send_sem=send_sems.at[N_PEER + 1],
                                                 recv_sem=recv_sems.at[N_PEER + 1], **sibling)
        half_back = pltpu.make_async_remote_copy(src_ref=tot.at[mine], dst_ref=tot.at[theirs], send_sem=send_sems.at[N_PEER + 1],
                                                 recv_sem=recv_sems.at[N_PEER + 1], **sibling)
        half_over.start()
        half_back.wait_recv()
        half_over.wait_send()

    return pl.pallas_call(
        body, name="small_sum", in_specs=[_whole()] * 12, out_specs=_whole(), out_shape=_sds((_P_ROWS, LANES), F32),
        scratch_shapes=[pltpu.VMEM((_P_ROWS, LANES), F32), pltpu.VMEM((_P_ROWS, LANES), F32),
                        pltpu.VMEM((N_CHIPS, hr, LANES), F32),
                        pltpu.SemaphoreType.DMA((N_PEER + 2,)), pltpu.SemaphoreType.DMA((N_PEER + 2,))],
        compiler_params=pltpu.CompilerParams(vmem_limit_bytes=VMEM_LIMIT_BYTES),
    )(*dgrads)


def _small_adamw(tot, ws, ms, vs):
    n = len(ws)

    def body(*refs):
        tot = refs[0]
        w_refs, m_refs, v_refs = refs[1:1 + n], refs[1 + n:1 + 2 * n], refs[1 + 2 * n:1 + 3 * n]
        loss_out = refs[1 + 3 * n]
        outs = refs[2 + 3 * n:]
        chip = 2 * lax.axis_index("x") + lax.axis_index("y")
        loss_out[...] = tot[_P_LOSS:_P_LOSS + 1, 0:1]

        def step(k, g, pick, put):
            d, m2, v2 = _adam_values(pick(w_refs[k]), pick(m_refs[k]), pick(v_refs[k]), g)
            for o, val in zip((outs[k], outs[n + k], outs[2 * n + k], outs[3 * n + k]), (g, d, m2, v2)):
                put(o, val)

        def whole(ref):
            return ref[0]

        def put_whole(ref, val):
            ref[0] = val

        for g in range(SGU_GROUPS):
            def pick_g(ref, g=g):
                return ref[0, g]

            def put_g(ref, val, g=g):
                ref[0, g] = val

            step(0, tot[_P_WSP + g * SGU_BLOCK:_P_WSP + (g + 1) * SGU_BLOCK], pick_g, put_g)
        step(1, tot[pl.ds(pl.multiple_of(_P_WGU + GLA_RANK * chip, GLA_RANK), GLA_RANK), :], whole, put_whole)
        for k, (base, chunks) in enumerate(((_P_NORM, 8), (_P_NORM + 8, 8), (_P_NORM + 16, 8), (_P_NORM + 24, 8), (_P_BG, 4))):
            for q in range(chunks):
                def pick_q(ref, q=q):
                    return ref[:, LANES * q:LANES * (q + 1)]

                def put_q(ref, val, q=q):
                    ref[:, LANES * q:LANES * (q + 1)] = val

                step(2 + k, tot[base + q:base + q + 1], pick_q, put_q)
        step(7, tot[_P_BSP:_P_BSP + SGU_GROUPS], whole, put_whole)
        for k in range(3):
            mine = tot[pl.ds(pl.multiple_of(_P_HEAD + 32 * k + 8 * chip, 8), 8), :]
            step(8 + k, mine[0:4, 0:64], whole, put_whole)

    shapes = [_sds(w.shape, F32) for w in ws]
    res = pl.pallas_call(
        body, name="small_adamw", in_specs=[_whole()] * (1 + 3 * n), out_specs=[_whole()] * (1 + 4 * n),
        out_shape=[_sds((1, 1), F32)] + shapes * 4,
        compiler_params=pltpu.CompilerParams(vmem_limit_bytes=VMEM_LIMIT_BYTES),
    )(tot, *ws, *ms, *vs)
    return res[0].reshape(()), [list(res[1 + i * n:1 + (i + 1) * n]) for i in range(4)]


def _w_in_pieces():
    blk = D_IN // N_CHIPS
    pieces = []
    for s in range(len(_IN_SPLITS)):
        lo_s, hi_s = _IN_STARTS[s], _IN_STARTS[s + 1]
        for j in range(N_CHIPS):
            lo, hi = max(lo_s, j * blk), min(hi_s, (j + 1) * blk)
            if lo < hi:
                pieces.append((j, lo - j * blk, _IN_DST[s] + lo - lo_s, hi - lo))
    return pieces


def _relayout_w_in(gathered):
    _, rows, blk = gathered.shape
    tr = 256

    def body(g_ref, o_ref):
        o_ref[:, OFF_AL:N_ALL] = jnp.zeros((tr, LANES), BF16)
        for j, src, dst, w in _w_in_pieces():
            o_ref[:, dst:dst + w] = g_ref[j, :, src:src + w]

    res, _ = _call(body, name="relayout_w_in", grid=(rows // tr,), parallel=True,
                   in_specs=[pl.BlockSpec((N_CHIPS, tr, blk), lambda i: (0, i, 0))],
                   out_specs=[pl.BlockSpec((tr, N_ALL), lambda i: (i, 0))],
                   out_shape=[_sds((rows, N_ALL), BF16)], args=(gathered,))
    return res[0]


def _update_row_tile(rows):
    for t in range(min(rows, 256), 7, -8):
        if rows % t == 0:
            return t
    return rows


def _presum_w_in(dws, theirs, row0, rows, name, job=None):
    hr = theirs[0].shape[0]
    blk = D_IN // N_CHIPS
    tr = 64
    assert row0 % tr == 0 and rows % tr == 0
    nh, t0 = hr // tr, row0 // tr
    n = len(dws)

    def body(core_ref, *refs):
        dw_refs, q_refs, (o_ref, s_scr) = refs[:n], refs[n:2 * n], refs[2 * n:]
        for p, (a, off) in enumerate(dws):
            w = a.shape[1]
            s_scr[:, off:off + w] = (dw_refs[p][...] + q_refs[p][...]).astype(BF16)
        for j, src, dst, w in _w_in_pieces():
            o_ref[j, :, src:src + w] = s_scr[:, dst:dst + w]

    in_specs = [pl.BlockSpec((tr, a.shape[1]), lambda i, core: (i + t0 + core[0] * nh, 0)) for a, _ in dws]
    in_specs += [pl.BlockSpec((tr, q.shape[1]), lambda i, core: (i + t0, 0)) for q in theirs]
    res, jres = _call(body, name=name, grid=(rows // tr,), parallel=True, in_specs=in_specs,
                      out_specs=[pl.BlockSpec((N_CHIPS, tr, blk), lambda i, core: (0, i, 0))],
                      out_shape=[_sds((N_CHIPS, rows, blk), BF16)], scratch_shapes=[pltpu.VMEM((tr, N_ALL), BF16)],
                      args=(*[a for a, _ in dws], *theirs), job=job, by_core=True)
    return res[0], jres


def _presum(dw, theirs, name):
    if dw.ndim == 4:
        _, _, hr, c = dw.shape
        tr = _update_row_tile(hr)
        mine = pl.BlockSpec((1, 1, tr, c), lambda j, i, core: (j, core[0], i, 0))
        other = pl.BlockSpec((1, tr, c), lambda j, i, core: (j, i, 0))
    else:
        hr, c = dw.shape[0] // 2, dw.shape[1] // N_CHIPS
        tr = _update_row_tile(hr)
        nh = hr // tr
        mine = pl.BlockSpec((tr, c), lambda j, i, core: (i + core[0] * nh, j))
        other = pl.BlockSpec((tr, c), lambda j, i, core: (i, j))

    def body(core_ref, a_ref, q_ref, o_ref):
        o_ref[...] = (a_ref[...].reshape(tr, c) + q_ref[...].reshape(tr, c)).astype(BF16).reshape(o_ref.shape)

    res, _ = _call(body, name=name, grid=(N_CHIPS, hr // tr), parallel=True, in_specs=[mine, other],
                   out_specs=[pl.BlockSpec((1, tr, c), lambda j, i, core: (j, i, 0))],
                   out_shape=[_sds((N_CHIPS, hr, c), BF16)], args=(dw, theirs), by_core=True)
    return res[0]


def _sum_slots(own, slots, name):
    rows, cols = own.shape
    tr = _update_row_tile(rows)

    def body(own_ref, s_ref, o_ref):
        acc = own_ref[...].astype(F32)
        for j in range(N_PEER):
            acc = acc + s_ref[j].astype(F32)
        o_ref[...] = acc

    res, _ = _call(body, name=name, grid=(rows // tr,), parallel=True,
                   in_specs=[pl.BlockSpec((tr, cols), lambda i: (i, 0)), pl.BlockSpec((N_PEER, tr, cols), lambda i: (0, i, 0))],
                   out_specs=[pl.BlockSpec((tr, cols), lambda i: (i, 0))], out_shape=[_sds((rows, cols), F32)],
                   args=(own, slots))
    return res[0]


def _adamw(w, m, v, g_mine, g_theirs, name, job=None):
    rows, cols = w.shape
    part_rows = [p.shape[0] for p in g_mine]
    assert sum(part_rows) == rows // 2 and [p.shape[0] for p in g_theirs] == part_rows
    tr = _update_row_tile(min(part_rows))
    assert all(r % tr == 0 for r in part_rows)
    nh = (rows // 2) // tr
    starts = [sum(part_rows[:k]) // tr for k in range(len(part_rows))]
    n_parts = len(part_rows)

    def body(core_ref, w_ref, m_ref, v_ref, *rest):
        g_refs, (g_out, d_out, m_out, v_out) = rest[:-4], rest[-4:]
        step = pl.program_id(0)
        mine_here = (step // nh) == core_ref[0]
        q = step % nh
        g = None
        for k in reversed(range(n_parts)):
            val = jnp.where(mine_here, g_refs[k][...], g_refs[n_parts + k][...])
            g = val if g is None else jnp.where(q < starts[k + 1], val, g)
        d, m2, v2 = _adam_values(w_ref[...], m_ref[...], v_ref[...], g)
        g_out[...] = g
        m_out[...] = m2
        v_out[...] = v2
        d_out[...] = d

    def g_spec(k, mine):
        last = part_rows[k] // tr - 1

        def index(i, core):
            half = core[0] if mine else 1 - core[0]
            here = jnp.clip(i % nh - starts[k], 0, last)
            return (jnp.where(i // nh == half, here, jnp.where(i // nh > half, last, 0)), 0)

        return pl.BlockSpec((tr, cols), index)

    spec = pl.BlockSpec((tr, cols), lambda i, core: (i, 0))
    g_specs = [g_spec(k, True) for k in range(n_parts)] + [g_spec(k, False) for k in range(n_parts)]
    return _call(body, name=name, grid=(rows // tr,), parallel=True, in_specs=[spec] * 3 + g_specs,
                 out_specs=[spec] * 4, out_shape=[_sds((rows, cols), F32)] * 4, args=(w, m, v, *g_mine, *g_theirs),
                 job=job, by_core=True)


def _transposed_cast(wt):
    cols, rows = wt.shape
    tile = 4 * LANES

    def body(x_ref, o_ref):
        o_ref[...] = jnp.transpose(x_ref[...]).astype(BF16)

    res, _ = _call(body, name="transpose_w_in", grid=(pl.cdiv(cols, tile),), parallel=True,
                   in_specs=[pl.BlockSpec((tile, rows), lambda j: (j, 0))],
                   out_specs=[pl.BlockSpec((rows, tile), lambda j: (0, j))], out_shape=[_sds((rows, cols), BF16)],
                   args=(wt,))
    return res[0]


def _cast_weights(ws, w_fi, job=None):
    steps = 4
    cols = w_fi.shape[1]
    tile = w_fi.shape[0] // (2 * steps)

    def body(*refs):
        for i_ref, o_ref in zip(refs[:len(refs) // 2], refs[len(refs) // 2:]):
            o_ref[...] = i_ref[...].astype(BF16)

    row_specs = [pl.BlockSpec((w.shape[0] // steps, w.shape[1]), lambda i: (i, 0)) for w in ws]
    half_spec = pl.BlockSpec((tile, cols), lambda i: (i, 0))
    return _call(body, name="cast_weights", grid=(steps,), parallel=True,
                 in_specs=row_specs + [pl.BlockSpec((None, tile, cols), lambda i, k=k: (k, i, 0)) for k in range(2)],
                 out_specs=row_specs + [half_spec, half_spec],
                 out_shape=[_sds(w.shape, BF16) for w in ws] + [_sds((steps * tile, cols), BF16)] * 2,
                 args=(*ws, w_fi.reshape(2, steps * tile, cols), w_fi.reshape(2, steps * tile, cols)), job=job)


def _adamw_transposed(wt, mt, vt, g_mine, g_theirs, name):
    cols, rows = wt.shape
    n_parts = len(g_mine)

    def body(w_ref, m_ref, v_ref, *rest):
        g_refs, (g_out, d_out, m_out, v_out) = rest[:-4], rest[-4:]
        mine = jnp.concatenate([r[...] for r in g_refs[:n_parts]], axis=0)
        theirs = jnp.concatenate([r[...] for r in g_refs[n_parts:]], axis=0)
        first = lax.axis_index("c") == 0
        g = jnp.transpose(jnp.concatenate([jnp.where(first, mine, theirs), jnp.where(first, theirs, mine)], axis=0))
        d, m2, v2 = _adam_values(w_ref[...], m_ref[...], v_ref[...], g)
        g_out[...] = g
        m_out[...] = m2
        v_out[...] = v2
        d_out[...] = d

    spec = pl.BlockSpec((LANES, rows), lambda j: (j, 0))
    g_specs = [pl.BlockSpec((p.shape[0], LANES), lambda j: (0, j)) for p in g_mine] * 2
    res, _ = _call(body, name=name, grid=(pl.cdiv(cols, LANES),), parallel=True, in_specs=[spec] * 3 + g_specs,
                   out_specs=[spec] * 4, out_shape=[_sds((cols, rows), F32)] * 4, args=(wt, mt, vt, *g_mine, *g_theirs))
    return res


def _inproj_fwd(x, g1, w_all, job=None):
    T = x.shape[0]
    tT = _row_tile(T, 512)

    def body(x_ref, g_ref, w_ref, a_ref, proj_ref, alow_ref):
        xv = x_ref[...]
        a = (xv * _rms_stats(xv) * g_ref[...]).astype(BF16)
        a_ref[...] = a
        for j in range(N_MAIN // 1024):
            cols = slice(j * 1024, (j + 1) * 1024)
            proj_ref[:, cols] = _dot(a, w_ref[:, cols]).astype(BF16)
        alow_ref[...] = _dot(a, w_ref[:, N_MAIN:N_ALL])

    row = lambda w: pl.BlockSpec((tT, w), lambda i: (i, 0))
    return _call(
        body, name="inproj_fwd", grid=(T // tT,), parallel=True,
        in_specs=[row(D_MODEL), pl.BlockSpec((1, D_MODEL), lambda i: (0, 0)), _whole()],
        out_specs=[row(D_MODEL), row(N_MAIN), row(LANES)],
        out_shape=[_sds((T, D_MODEL), BF16), _sds((T, N_MAIN), BF16), _sds((T, LANES), F32)],
        args=(x, g1, w_all), job=job)


def _gla_decay_terms(al_ref, wgu_ref, bg_ref, later_ref):
    logit = _dot_bf16(al_ref[...], wgu_ref[...]) + bg_ref[...]
    la = _log_sigmoid(logit) * (1.0 / GLA_TAU)
    delta = _dot_exact_lhs(later_ref[...], la)
    return logit, la, delta


def _gla_fwd(proj, alow, wgu, b_gate, gn, job=None):
    T = proj.shape[0]
    tT = _row_tile(T, 512)
    nc = tT // CHUNK

    def body(q_ref, k_ref, v_ref, r_ref, al_ref, wgu_ref, bg_ref, gn_ref, later_ref, y_ref, st_ref, s_scr):
        @pl.when(pl.program_id(0) == 0)
        def _():
            s_scr[...] = jnp.zeros_like(s_scr)

        _, la, delta = _gla_decay_terms(al_ref, wgu_ref, bg_ref, later_ref)
        kdec = (k_ref[...].astype(F32) * jnp.exp(delta)).astype(BF16)
        heads = range(GLA_HEADS)
        kcs = [slice(h * GLA_DK, (h + 1) * GLA_DK) for h in heads]
        vcs = [slice(h * GLA_DV, (h + 1) * GLA_DV) for h in heads]
        state = [s_scr[h] for h in heads]
        for c in range(nc):
            rows = slice(c * CHUNK, (c + 1) * CHUNK)
            first = slice(c * CHUNK, c * CHUNK + 1)
            dec = jnp.exp(la[first, :] + delta[first, :])
            upd_t = [_dot(v_ref[rows, vcs[h]], kdec[rows, kcs[h]], _TN) for h in heads]
            qs = [(q_ref[rows, kcs[h]].astype(F32) * (GLA_DK ** -0.5)).astype(BF16) for h in heads]
            for h in heads:
                state[h] = state[h] * dec[:, kcs[h]] + upd_t[h]
                st_ref[c, h] = state[h]
            o = [_dot(qs[h], state[h].astype(BF16), _NT) for h in heads]
            for h in heads:
                on = o[h] * _rms_stats(o[h]) * gn_ref[:, vcs[h]]
                rr = r_ref[rows, vcs[h]].astype(F32)
                y_ref[rows, vcs[h]] = (on * (rr * _sigmoid(rr))).astype(BF16)
        for h in heads:
            s_scr[h] = state[h]

    blk = lambda w, j: pl.BlockSpec((tT, w), lambda i: (i, j))
    return _call(
        body, name="gla_fwd", grid=(T // tT,),
        in_specs=[blk(512, 0), blk(512, 1), blk(1024, 1), blk(1024, 2), blk(LANES, 0)] + [_whole()] * 4,
        out_specs=[pl.BlockSpec((tT, GLA_V), lambda i: (i, 0)),
                   pl.BlockSpec((nc, GLA_HEADS, GLA_DV, GLA_DK), lambda i: (i, 0, 0, 0))],
        out_shape=[_sds((T, GLA_V), BF16), _sds((T // CHUNK, GLA_HEADS, GLA_DV, GLA_DK), F32)],
        scratch_shapes=[pltpu.VMEM((GLA_HEADS, GLA_DV, GLA_DK), F32)],
        args=(proj, proj, proj, proj, alow, wgu, b_gate, gn, _chunk_masks(tT, upper=True)), job=job)


def _sgu_mask():
    i = lax.broadcasted_iota(jnp.int32, (SGU_BLOCK, SGU_BLOCK), 0)
    j = lax.broadcasted_iota(jnp.int32, (SGU_BLOCK, SGU_BLOCK), 1)
    return lax.shift_right_logical(j, 6) <= lax.shift_right_logical(i, 6)


def _sgu_merge_fwd(x, proj, y_gla, ln_g, ln_b, w_sp, b_sp_t, w_bg, w_bs, w_o, g_pm, job=None):
    T = x.shape[0]
    tT = _row_tile(T, 512)
    nb = tT // SGU_BLOCK

    def body(x_ref, su_ref, sv_ref, gg_ref, gs_ref, yg_ref, lg_ref, lb_ref, w_ref, b_ref, wbg_ref, wbs_ref, wo_ref,
             g_ref, ys_ref, zg_ref, zs_ref, mg_ref, mix_ref, x1_ref):
        mask = _sgu_mask()
        for g in range(SGU_GROUPS):
            gc = slice(g * SGU_DG, (g + 1) * SGU_DG)
            wm = jnp.where(mask, w_ref[g], 0.0).astype(BF16)
            vf = _gelu(sv_ref[:, gc].astype(F32))
            mu = jnp.mean(vf, axis=-1, keepdims=True)
            vc = vf - mu
            rstd = lax.rsqrt(jnp.mean(vc * vc, axis=-1, keepdims=True) + EPS)
            vn = (vc * rstd * lg_ref[:, gc] + lb_ref[:, gc]).astype(BF16)
            u = _gelu(su_ref[:, gc].astype(F32))
            for b in range(nb):
                rows = slice(b * SGU_BLOCK, (b + 1) * SGU_BLOCK)
                mixed = _dot(wm, vn[rows, :]) + b_ref[:, g:g + 1]
                ys_ref[rows, gc] = (u[rows, :] * mixed).astype(BF16)
        zg = _dot(yg_ref[...], wbg_ref[...])
        zs = _dot(ys_ref[...], wbs_ref[...])
        zg_ref[...] = zg.astype(BF16)
        zs_ref[...] = zs.astype(BF16)
        merged = (_sigmoid(gg_ref[...].astype(F32)) * zg + _sigmoid(gs_ref[...].astype(F32)) * zs).astype(BF16)
        mg_ref[...] = merged
        mix = _dot(merged, wo_ref[...])
        mix_ref[...] = mix.astype(BF16)
        x1_ref[...] = x_ref[...] + mix * _rms_stats(mix) * g_ref[...]

    row = pl.BlockSpec((tT, D_MODEL), lambda i: (i, 0))
    blk = lambda j: pl.BlockSpec((tT, 1024), lambda i: (i, j))
    sds = lambda dt: _sds((T, D_MODEL), dt)
    return _call(body, name="sgu_merge_fwd", grid=(T // tT,), parallel=True,
                 in_specs=[row, blk(3), blk(4), blk(5), blk(6), row] + [_whole()] * 7
                 + [pl.BlockSpec((1, D_MODEL), lambda i: (0, 0))],
                 out_specs=[row] * 6, out_shape=[sds(BF16)] * 5 + [sds(F32)],
                 args=(x, proj, proj, proj, proj, y_gla, ln_g, ln_b, w_sp, b_sp_t, w_bg, w_bs, w_o, g_pm), job=job)


def _ffn_fwd_bwd(x1, tgt, w_fi_top, w_fi_bot, w_fo, g_pf, g_po):
    T = x1.shape[0]
    tT = _row_tile(T, 256)
    half = D_FF // 2
    kh = D_MODEL // 2

    def body(x1_ref, t_ref, top_ref, bot_ref, wfo_ref, gpf_ref, gpo_ref,
             h_ref, f_ref, dgu_ref, dy_ref, dx1_ref, loss_ref, dgpf_ref, dgpo_ref, gu_scr):
        @pl.when(pl.program_id(0) == 0)
        def _():
            loss_ref[...] = jnp.zeros_like(loss_ref)
            dgpf_ref[...] = jnp.zeros_like(dgpf_ref)
            dgpo_ref[...] = jnp.zeros_like(dgpo_ref)

        main = (half // 256) * 256
        pieces = (0, 1, None)

        def w_in_cols(ref, first_slab, p):
            if p is not None:
                return ref[first_slab + p, :, :main]
            return jnp.concatenate([ref[first_slab, :, main:], ref[first_slab + 1, :, main:]], axis=1)

        def w_out_rows(p):
            if p is not None:
                return wfo_ref[p * half:p * half + main, :]
            return jnp.concatenate([wfo_ref[main:half, :], wfo_ref[half + main:2 * half, :]], axis=0)

        def put(ref, base, p, val):
            if p is not None:
                ref[:, base + p * half:base + p * half + main] = val
            else:
                ref[:, base + main:base + half] = val[:, :half - main]
                ref[:, base + half + main:base + 2 * half] = val[:, half - main:]

        def get(ref, base, p):
            if p is not None:
                return ref[:, base + p * half:base + p * half + main]
            return jnp.concatenate([ref[:, base + main:base + half], ref[:, base + half + main:base + 2 * half]], axis=1)

        x1v = x1_ref[...]
        r2 = _rms_stats(x1v)
        h = (x1v * r2 * gpf_ref[...]).astype(BF16)
        h_ref[...] = h
        y = jnp.zeros((tT, D_MODEL), F32)
        for p in pieces:
            gate = _dot(h[:, :kh], w_in_cols(top_ref, 0, p)) + _dot(h[:, kh:], w_in_cols(bot_ref, 0, p))
            up = _dot(h[:, :kh], w_in_cols(top_ref, 2, p)) + _dot(h[:, kh:], w_in_cols(bot_ref, 2, p))
            put(gu_scr, 0, p, gate)
            put(gu_scr, D_FF, p, up)
            f = (gate * _sigmoid(gate) * up).astype(BF16)
            put(f_ref, 0, p, f)
            y = y + _dot(f, w_out_rows(p))
        r3 = _rms_stats(y)
        x2 = x1v + y * r3 * gpo_ref[...]
        err = x2 - t_ref[...]
        loss_ref[...] += jnp.sum(err * err) * (0.5 / D_MODEL)
        dx2 = err * (1.0 / D_MODEL)
        dy, dg = _rms_bwd(dx2, y, r3, gpo_ref[...])
        dgpo_ref[...] += jnp.sum(dg, axis=0, keepdims=True)
        dyb = dy.astype(BF16)
        dy_ref[...] = dyb
        dh_top = jnp.zeros((tT, kh), F32)
        dh_bot = jnp.zeros((tT, kh), F32)
        for p in pieces:
            df = _dot(dyb, w_out_rows(p), _NT)
            gate = get(gu_scr, 0, p)
            up = get(gu_scr, D_FF, p)
            sg = _sigmoid(gate)
            dgate = (df * up * (sg * (1.0 + gate * (1.0 - sg)))).astype(BF16)
            dup = (df * (gate * sg)).astype(BF16)
            put(dgu_ref, 0, p, dgate)
            put(dgu_ref, D_FF, p, dup)
            dh_top = dh_top + _dot(dgate, w_in_cols(top_ref, 0, p), _NT) + _dot(dup, w_in_cols(top_ref, 2, p), _NT)
            dh_bot = dh_bot + _dot(dgate, w_in_cols(bot_ref, 0, p), _NT) + _dot(dup, w_in_cols(bot_ref, 2, p), _NT)
        dh = jnp.concatenate([dh_top, dh_bot], axis=1)
        dx1n, dg2 = _rms_bwd(dh, x1v, r2, gpf_ref[...])
        dgpf_ref[...] += jnp.sum(dg2, axis=0, keepdims=True)
        dx1_ref[...] = dx2 + dx1n

    row = lambda w: pl.BlockSpec((tT, w), lambda i: (i, 0))
    vec = pl.BlockSpec((1, D_MODEL), lambda i: (0, 0))
    res, _ = _call(
        body, name="ffn_fwd_bwd", grid=(T // tT,),
        in_specs=[row(D_MODEL), row(D_MODEL), _whole(), _whole(), _whole(), vec, vec],
        out_specs=[row(D_MODEL), row(D_FF), row(2 * D_FF), row(D_MODEL), row(D_MODEL),
                   pl.BlockSpec((1, LANES), lambda i: (0, 0)), vec, vec],
        out_shape=[_sds((T, D_MODEL), BF16), _sds((T, D_FF), BF16), _sds((T, 2 * D_FF), BF16), _sds((T, D_MODEL), BF16),
                   _sds((T, D_MODEL), F32), _sds((1, LANES), F32), _sds((1, D_MODEL), F32), _sds((1, D_MODEL), F32)],
        scratch_shapes=[pltpu.VMEM((tT, 2 * D_FF), F32)], args=(x1, tgt, w_fi_top, w_fi_bot, w_fo, g_pf, g_po))
    return res


def _merge_sgu_bwd(dx1, mix, proj, zg, zs, w_bg, w_bs, w_o, g_pm, ln_g, ln_b, w_sp, b_sp_t, job=None):
    T = dx1.shape[0]
    tT = _row_tile(T, 256)
    nb = tT // SGU_BLOCK

    def body(dx1_ref, mix_ref, su_ref, sv_ref, gg_ref, gs_ref, zg_ref, zs_ref, wbg_ref, wbs_ref, wo_ref, g_ref,
             lg_ref, lb_ref, w_ref, b_ref,
             dmix_ref, dzg_ref, dzs_ref, dgate_ref, dyg_ref, dp_ref, dgpm_ref, dw_ref, dbt_ref, dlg_ref, dlb_ref):
        @pl.when(pl.program_id(0) == 0)
        def _():
            for ref in (dgpm_ref, dw_ref, dbt_ref, dlg_ref, dlb_ref):
                ref[...] = jnp.zeros_like(ref)

        mix = mix_ref[...].astype(F32)
        dmix, dg = _rms_bwd(dx1_ref[...], mix, _rms_stats(mix), g_ref[...])
        dgpm_ref[...] += jnp.sum(dg, axis=0, keepdims=True)
        dmb = dmix.astype(BF16)
        dmix_ref[...] = dmb
        dmerged = _dot(dmb, wo_ref[...], _NT)
        dys = None
        for k, (gate_ref, z_ref, w_br_ref, dz_ref) in enumerate(((gg_ref, zg_ref, wbg_ref, dzg_ref),
                                                                 (gs_ref, zs_ref, wbs_ref, dzs_ref))):
            sg = _sigmoid(gate_ref[...].astype(F32))
            dz = (dmerged * sg).astype(BF16)
            dz_ref[...] = dz
            dgate_ref[:, k * 1024:(k + 1) * 1024] = (dmerged * z_ref[...].astype(F32) * (sg * (1.0 - sg))).astype(BF16)
            dy_branch = _dot(dz, w_br_ref[...], _NT)
            if k == 0:
                dyg_ref[...] = dy_branch.astype(BF16)
            else:
                dys = dy_branch

        mask = _sgu_mask()
        lane = lax.broadcasted_iota(jnp.int32, (SGU_BLOCK, LANES), 1)
        for g in range(SGU_GROUPS):
            gc = slice(g * SGU_DG, (g + 1) * SGU_DG)
            gc_v = slice(1024 + g * SGU_DG, 1024 + (g + 1) * SGU_DG)
            wm = jnp.where(mask, w_ref[g], 0.0).astype(BF16)
            vf, dvf_dsv = _gelu_and_grad(sv_ref[:, gc].astype(F32))
            mu = jnp.mean(vf, axis=-1, keepdims=True)
            vc = vf - mu
            rstd = lax.rsqrt(jnp.mean(vc * vc, axis=-1, keepdims=True) + EPS)
            vhat = vc * rstd
            vn = (vhat * lg_ref[:, gc] + lb_ref[:, gc]).astype(BF16)
            u, du_dsu = _gelu_and_grad(su_ref[:, gc].astype(F32))
            dy = dys[:, gc]
            dmixed = (dy * u).astype(BF16)
            dvn_parts = []
            dw_acc = jnp.zeros((SGU_BLOCK, SGU_BLOCK), F32)
            db_acc = jnp.zeros((SGU_BLOCK, 1), F32)
            for b in range(nb):
                rows = slice(b * SGU_BLOCK, (b + 1) * SGU_BLOCK)
                mixed = _dot(wm, vn[rows, :]) + b_ref[:, g:g + 1]
                dp_ref[rows, gc] = (dy[rows, :] * mixed * du_dsu[rows, :]).astype(BF16)
                dvn_parts.append(_dot(wm, dmixed[rows, :], _TN))
                dw_acc = dw_acc + _dot(dmixed[rows, :], vn[rows, :], _NT)
                db_acc = db_acc + jnp.sum(dmixed[rows, :].astype(F32), axis=-1, keepdims=True)
            dw_ref[g] += jnp.where(mask, dw_acc, 0.0)
            dbt_ref[...] += jnp.where(lane == g, db_acc, 0.0)
            dvn = jnp.concatenate(dvn_parts, axis=0)
            dlg_ref[:, gc] += jnp.sum(dvn * vhat, axis=0, keepdims=True)
            dlb_ref[:, gc] += jnp.sum(dvn, axis=0, keepdims=True)
            dvh = dvn * lg_ref[:, gc]
            dvf = rstd * (dvh - jnp.mean(dvh, axis=-1, keepdims=True)
                          - vhat * jnp.mean(dvh * vhat, axis=-1, keepdims=True))
            dp_ref[:, gc_v] = (dvf * dvf_dsv).astype(BF16)

    row = pl.BlockSpec((tT, D_MODEL), lambda i: (i, 0))
    blk = lambda j: pl.BlockSpec((tT, 1024), lambda i: (i, j))
    vec = pl.BlockSpec((1, D_MODEL), lambda i: (0, 0))
    wide = lambda w: pl.BlockSpec((tT, w), lambda i: (i, 0))
    sds = _sds((T, D_MODEL), BF16)
    return _call(
        body, name="merge_sgu_bwd", grid=(T // tT,),
        in_specs=[row, row, blk(3), blk(4), blk(5), blk(6), row, row] + [_whole()] * 3 + [vec] + [_whole()] * 4,
        out_specs=[row, row, row, wide(W_MRG), row, wide(W_SGU), vec,
                   pl.BlockSpec((SGU_GROUPS, SGU_BLOCK, SGU_BLOCK), lambda i: (0, 0, 0)),
                   pl.BlockSpec((SGU_BLOCK, LANES), lambda i: (0, 0)), vec, vec],
        out_shape=[sds, sds, sds, _sds((T, W_MRG), BF16), sds, _sds((T, W_SGU), BF16), _sds((1, D_MODEL), F32),
                   _sds((SGU_GROUPS, SGU_BLOCK, SGU_BLOCK), F32), _sds((SGU_BLOCK, LANES), F32),
                   _sds((1, 1024), F32), _sds((1, 1024), F32)],
        args=(dx1, mix, proj, proj, proj, proj, zg, zs, w_bg, w_bs, w_o, g_pm, ln_g, ln_b, w_sp, b_sp_t), job=job)


def _gla_bwd(proj, alow, wgu, b_gate, gn, states, dy_gla, job=None):
    T = proj.shape[0]
    tT = _row_tile(T, 512)
    nc = tT // CHUNK
    nt = T // tT

    def body(q_ref, k_ref, v_ref, r_ref, al_ref, wgu_ref, bg_ref, gn_ref, later_ref, earlier_ref, st_ref, sp_ref, dy_ref,
             dp_ref, dal_ref, dgn_ref, dbg_ref, dwgu_ref, g_scr, dd_scr, dt_scr):
        step = pl.program_id(0)

        @pl.when(step == 0)
        def _():
            g_scr[...] = jnp.zeros_like(g_scr)
            dgn_ref[...] = jnp.zeros_like(dgn_ref)
            dbg_ref[...] = jnp.zeros_like(dbg_ref)
            dwgu_ref[...] = jnp.zeros_like(dwgu_ref)

        has_prev = jnp.where(step == nt - 1, 0.0, 1.0)
        logit, la, delta = _gla_decay_terms(al_ref, wgu_ref, bg_ref, later_ref)
        e = jnp.exp(delta)
        kdec_f = k_ref[...].astype(F32) * e
        kdec = kdec_f.astype(BF16)
        heads = range(GLA_HEADS)
        kcs = [slice(h * GLA_DK, (h + 1) * GLA_DK) for h in heads]
        vcs = [slice(h * GLA_DV, (h + 1) * GLA_DV) for h in heads]
        carry = [g_scr[h] for h in heads]
        dgn_acc = [jnp.zeros((1, GLA_DV), F32) for _ in heads]
        for c in reversed(range(nc)):
            rows = slice(c * CHUNK, (c + 1) * CHUNK)
            first = slice(c * CHUNK, c * CHUNK + 1)
            dec = jnp.exp(la[first, :] + delta[first, :])
            s_b = [st_ref[c, h].astype(BF16) for h in heads]
            qs = [(q_ref[rows, kcs[h]].astype(F32) * (GLA_DK ** -0.5)).astype(BF16) for h in heads]
            o = [_dot(qs[h], s_b[h], _NT) for h in heads]
            do = []
            for h in heads:
                rstd = _rms_stats(o[h])
                ohat = o[h] * rstd
                gnh = gn_ref[:, vcs[h]]
                dy = dy_ref[rows, vcs[h]].astype(F32)
                rr = r_ref[rows, vcs[h]].astype(F32)
                sg = _sigmoid(rr)
                don = dy * (rr * sg)
                dp_ref[rows, OFF_R + h * GLA_DV:OFF_R + (h + 1) * GLA_DV] = (
                    dy * (ohat * gnh) * (sg * (1.0 + rr * (1.0 - sg)))).astype(BF16)
                dgn_acc[h] = dgn_acc[h] + jnp.sum(don * ohat, axis=0, keepdims=True)
                dn = don * gnh
                do.append((rstd * (dn - ohat * jnp.mean(dn * ohat, axis=-1, keepdims=True))).astype(BF16))
            dq = [_dot(do[h], s_b[h]) for h in heads]
            g_t = [_dot(do[h], qs[h], _TN) + carry[h] for h in heads]
            g_b = [g_t[h].astype(BF16) for h in heads]
            dv = [_dot(kdec[rows, kcs[h]], g_b[h], _NT) for h in heads]
            dkdec = [_dot(v_ref[rows, vcs[h]], g_b[h]) for h in heads]
            for h in heads:
                s_prev = st_ref[c - 1, h] if c > 0 else sp_ref[0, h] * has_prev
                ddec = jnp.sum(g_t[h] * s_prev, axis=0, keepdims=True)
                carry[h] = g_t[h] * dec[:, kcs[h]]
                dp_ref[rows, OFF_Q + h * GLA_DK:OFF_Q + (h + 1) * GLA_DK] = (dq[h] * (GLA_DK ** -0.5)).astype(BF16)
                dp_ref[rows, OFF_V + h * GLA_DV:OFF_V + (h + 1) * GLA_DV] = dv[h].astype(BF16)
                dp_ref[rows, OFF_K + h * GLA_DK:OFF_K + (h + 1) * GLA_DK] = (dkdec[h] * e[rows, kcs[h]]).astype(BF16)
                dd_scr[rows, kcs[h]] = dkdec[h] * kdec_f[rows, kcs[h]]
                dt_scr[rows, kcs[h]] = jnp.broadcast_to(ddec * dec[:, kcs[h]], (CHUNK, GLA_DK))
        for h in heads:
            g_scr[h] = carry[h]
            dgn_ref[:, vcs[h]] += dgn_acc[h]
        dla = _dot_exact_lhs(earlier_ref[...], dd_scr[...]) + dt_scr[...]
        dlogit = dla * (1.0 / GLA_TAU) * _sigmoid(-logit)
        dbg_ref[...] += jnp.sum(dlogit, axis=0, keepdims=True)
        dwgu_ref[...] += _dot_bf16(al_ref[...], dlogit, _TN)
        dal_ref[...] = _dot_bf16(dlogit, wgu_ref[...], _NT).astype(BF16)

    rev = lambda i: nt - 1 - i
    blk = lambda w, j: pl.BlockSpec((tT, w), lambda i: (rev(i), j))
    st_blk = pl.BlockSpec((nc, GLA_HEADS, GLA_DV, GLA_DK), lambda i: (rev(i), 0, 0, 0))
    sp_blk = pl.BlockSpec((1, GLA_HEADS, GLA_DV, GLA_DK), lambda i: (jnp.maximum(rev(i) * nc - 1, 0), 0, 0, 0))
    return _call(
        body, name="gla_bwd", grid=(nt,),
        in_specs=[blk(512, 0), blk(512, 1), blk(1024, 1), blk(1024, 2), blk(LANES, 0)] + [_whole()] * 5
        + [st_blk, sp_blk, blk(GLA_V, 0)],
        out_specs=[blk(W_GLA, 0), blk(LANES, 0), pl.BlockSpec((1, GLA_V), lambda i: (0, 0)),
                   pl.BlockSpec((1, GLA_QK), lambda i: (0, 0)), pl.BlockSpec((LANES, GLA_QK), lambda i: (0, 0))],
        out_shape=[_sds((T, W_GLA), BF16), _sds((T, LANES), BF16), _sds((1, GLA_V), F32), _sds((1, GLA_QK), F32),
                   _sds((LANES, GLA_QK), F32)],
        scratch_shapes=[pltpu.VMEM((GLA_HEADS, GLA_DV, GLA_DK), F32), pltpu.VMEM((tT, GLA_QK), F32),
                        pltpu.VMEM((tT, GLA_QK), F32)],
        args=(proj, proj, proj, proj, alow, wgu, b_gate, gn, _chunk_masks(tT, upper=True), _chunk_masks(tT, upper=False),
              states, states, dy_gla), job=job)


def _inproj_bwd(x, dx1, g1, w_all, dparts, job=None):
    T = x.shape[0]
    tT = _row_tile(T, 512)
    offs = (0, W_GLA, W_GLA + W_SGU, N_MAIN)

    def body(x_ref, dx1_ref, g_ref, w_hbm, *rest):
        part_refs, (dx_ref, dg_ref, w_ref, w_sems) = rest[:len(offs)], rest[len(offs):]

        def compute(first):
            if first:
                copies = [pltpu.make_async_copy(w_hbm.at[:, pl.ds(off, p.shape[1])], w_ref.at[:, pl.ds(off, p.shape[1])],
                                                w_sems.at[k]) for k, (off, p) in enumerate(zip(offs, dparts))]
                for cp in copies:
                    cp.start()
            da = jnp.zeros((tT, D_MODEL), F32)
            for k, (off, p_ref) in enumerate(zip(offs, part_refs)):
                if first:
                    copies[k].wait()
                da = da + _dot(p_ref[...], w_ref[:, off:off + p_ref.shape[1]], _NT)
            xv = x_ref[...]
            dx, dg = _rms_bwd(da, xv, _rms_stats(xv), g_ref[...])
            dg_sum = jnp.sum(dg, axis=0, keepdims=True)
            dg_ref[...] = dg_sum if first else dg_ref[...] + dg_sum
            dx_ref[...] = dx1_ref[...] + dx

        first_step = pl.program_id(0) == 0
        pl.when(first_step)(lambda: compute(True))
        pl.when(jnp.logical_not(first_step))(lambda: compute(False))

    row = lambda w: pl.BlockSpec((tT, w), lambda i: (i, 0))
    vec = pl.BlockSpec((1, D_MODEL), lambda i: (0, 0))
    return _call(
        body, name="inproj_bwd", grid=(T // tT,),
        in_specs=[row(D_MODEL), row(D_MODEL), vec, pl.BlockSpec(memory_space=pl.ANY)] + [row(p.shape[1]) for p in dparts],
        out_specs=[row(D_MODEL), vec], out_shape=[_sds((T, D_MODEL), F32), _sds((1, D_MODEL), F32)],
        scratch_shapes=[pltpu.VMEM(w_all.shape, BF16), pltpu.SemaphoreType.DMA((len(offs),))],
        args=(x, dx1, g1, w_all, *dparts), job=job)


def _tn_matmul(a, b, name, job=None):
    T, M = a.shape
    N = b.shape[1]
    tk = _row_tile(T, 1024)
    tm = M if M <= 1024 else 1408
    tn = N // 2 if N % 512 == 0 and N > 1024 else N
    assert M % tm == 0 and N % tn == 0

    def body(a_ref, b_ref, o_ref):
        @pl.when(pl.program_id(2) == 0)
        def _():
            o_ref[...] = _dot(a_ref[...], b_ref[...], _TN)

        @pl.when(pl.program_id(2) > 0)
        def _():
            o_ref[...] += _dot(a_ref[...], b_ref[...], _TN)

    res, jres = _call(
        body, name=name, grid=(M // tm, N // tn, T // tk),
        in_specs=[pl.BlockSpec((tk, tm), lambda i, j, k: (k, i)), pl.BlockSpec((tk, tn), lambda i, j, k: (k, j))],
        out_specs=[pl.BlockSpec((tm, tn), lambda i, j, k: (i, j))], out_shape=[_sds((M, N), F32)], args=(a, b), job=job)
    return res[0], jres


def _pad_rows(a, rows=8):
    return jnp.pad(a, ((0, rows - a.shape[0]), (0, LANES - a.shape[1])))


def _halves_view(dw):
    r = dw.shape[0] // N_CHIPS
    return dw.reshape(N_CHIPS, 2, r // 2, dw.shape[1])


def kernel(x, norm_pre_mix, w_in, w_gate_up, b_gate, gla_norm, sgu_ln_g, sgu_ln_b, w_spatial, b_spatial, w_branch_gla, w_branch_sgu, w_out, norm_post_mix, norm_pre_ffn, w_ffn_in, w_ffn_out, norm_post_ffn, loss_target, m_norm_pre_mix, m_w_in, m_w_gate_up, m_b_gate, m_gla_norm, m_sgu_ln_g, m_sgu_ln_b, m_w_spatial, m_b_spatial, m_w_branch_gla, m_w_branch_sgu, m_w_out, m_norm_post_mix, m_norm_pre_ffn, m_w_ffn_in, m_w_ffn_out, m_norm_post_ffn, v_norm_pre_mix, v_w_in, v_w_gate_up, v_b_gate, v_gla_norm, v_sgu_ln_g, v_sgu_ln_b, v_w_spatial, v_b_spatial, v_w_branch_gla, v_w_branch_sgu, v_w_out, v_norm_post_mix, v_norm_pre_ffn, v_w_ffn_in, v_w_ffn_out, v_norm_post_ffn):
    chip = 2 * lax.axis_index("x") + lax.axis_index("y")
    xt, tgt = x[0], loss_target[0]

    tiny = jnp.concatenate([w_gate_up[0], _pad_rows(gla_norm[0]), _pad_rows(sgu_ln_g[0]), _pad_rows(sgu_ln_b[0]),
                            jnp.zeros((24, LANES), F32)], axis=0)

    def with_own(gathered, own):
        return lax.dynamic_update_slice(gathered, own[None], (chip, 0, 0))

    w_in_t, m_in_t, v_in_t = w_in[0].T, m_w_in[0].T, v_w_in[0].T
    w_in_b = _transposed_cast(w_in_t)
    (*own_rows, fi_top, fi_bot), (g_in, g_tiny) = _cast_weights(
        [w_branch_gla[0], w_branch_sgu[0], w_out[0], w_ffn_out[0]], w_ffn_in[0], job=_job_gather([w_in_b, tiny]))
    g_tiny = with_own(g_tiny, tiny)
    w_all = _relayout_w_in(with_own(g_in, w_in_b))
    cols = lambda a: a.transpose(1, 0, 2).reshape(a.shape[1], N_CHIPS * a.shape[2])
    wgu = jnp.pad(cols(g_tiny[:, 0:16]), ((0, LANES - GLA_RANK), (0, 0)))
    gn = cols(g_tiny[:, 16:20, :64]).reshape(1, GLA_V)
    ln_g = cols(g_tiny[:, 24:28, :64]).reshape(1, 1024)
    ln_b = cols(g_tiny[:, 32:36, :64]).reshape(1, 1024)
    b_sp_t = jnp.pad(b_spatial[0].T, ((0, 0), (0, LANES - SGU_GROUPS)))
    w_sp = w_spatial[0]

    (a, proj, alow), g_rows = _inproj_fwd(xt, norm_pre_mix, w_all, job=_job_gather(own_rows))
    rows = lambda g: g.reshape(N_CHIPS * g.shape[1], g.shape[2])
    w_bg, w_bs, w_o, w_fo = [rows(with_own(g, own)) for g, own in zip(g_rows, own_rows)]
    (y_gla, states), (g_top,) = _gla_fwd(proj, alow, wgu, b_gate, gn, job=_job_gather([fi_top]))
    (y_sgu, zg, zs, merged, mix, x1), (g_bot,) = _sgu_merge_fwd(
        xt, proj, y_gla, ln_g, ln_b, w_sp, b_sp_t, w_bg, w_bs, w_o, norm_post_mix, job=_job_gather([fi_bot]))
    h, f, dgu, dy, dx1, loss, d_gpf, d_gpo = _ffn_fwd_bwd(x1, tgt, with_own(g_top, fi_top), with_own(g_bot, fi_bot),
                                                          w_fo, norm_pre_ffn, norm_post_ffn)

    own_part = lambda c: lax.dynamic_index_in_dim(c, chip, 0, keepdims=False)
    whole = lambda hs: [[(h_, None)] for h_ in hs]
    dw_fo, _ = _tn_matmul(f, dy, "dw_ffn_out")
    dw_fo4 = _halves_view(dw_fo)
    dw_fi, (q_fo,) = _tn_matmul(h, dgu, "dw_ffn_in", job=_job_to_other_core([[(dw_fo4, 0)]]))
    c_fo = _presum(dw_fo4, q_fo, "presum_ffn_out")
    (dmix, dzg, dzs, dp_mrg, dyg, dp_sgu, d_gpm, d_wsp, d_bsp_t, d_lng, d_lnb), (s_fo, q_fi) = _merge_sgu_bwd(
        dx1, mix, proj, zg, zs, w_bg, w_bs, w_o, norm_post_mix, ln_g, ln_b, w_sp, b_sp_t,
        job=_join(_job_scatter([c_fo]), _job_to_other_core([[(dw_fi, 0)]])))
    c_fi = _presum(dw_fi, q_fi, "presum_ffn_in")
    dw_c, _ = _tn_matmul(a, dp_mrg, "dw_in_merge")
    dw_b, _ = _tn_matmul(a, dp_sgu, "dw_in_sgu")
    dw_o4 = _halves_view(_tn_matmul(merged, dmix, "dw_out")[0])
    dw_bg4 = _halves_view(_tn_matmul(y_gla, dzg, "dw_branch_gla")[0])
    dw_bs4 = _halves_view(_tn_matmul(y_sgu, dzs, "dw_branch_sgu")[0])
    h_fo = _sum_slots(own_part(c_fo), s_fo, "sum_ffn_out")
    (dp_gla, dal, d_gn, d_bg, d_wgu), (s_fi, t_fo, q_b, q_c, q_o, q_bg, q_bs) = _gla_bwd(
        proj, alow, wgu, b_gate, gn, states, dyg,
        job=_join(_job_scatter([c_fi]), _job_to_other_core(
            whole([h_fo]) + [[(dw_b, 0)], [(dw_c, 0)], [(dw_o4, 0)], [(dw_bg4, 0)], [(dw_bs4, 0)]])))
    c_o, c_bg, c_bs = (_presum(dw_o4, q_o, "presum_out"), _presum(dw_bg4, q_bg, "presum_branch_gla"),
                       _presum(dw_bs4, q_bs, "presum_branch_sgu"))
    h_fi = _sum_slots(own_part(c_fi), s_fi, "sum_ffn_in")
    dw_d, _ = _tn_matmul(a, dal, "dw_in_gate")
    dw_a, (s_o, s_bg, s_bs, t_fi, q_d) = _tn_matmul(
        a, dp_gla, "dw_in_gla",
        job=_join(_job_scatter([c_o, c_bg, c_bs]), _job_to_other_core(whole([h_fi]) + [[(dw_d, 0)]])))
    h_o, h_bg, h_bs = (_sum_slots(own_part(c_o), s_o, "sum_out"), _sum_slots(own_part(c_bg), s_bg, "sum_branch_gla"),
                       _sum_slots(own_part(c_bs), s_bs, "sum_branch_sgu"))

    grads, deltas, new_m, new_v = {}, {}, {}, {}

    def update(name, w, m, v, g_mine, g_theirs, job=None):
        (g, d, m2, v2), jres = _adamw(w[0], m[0], v[0], g_mine, g_theirs, "adamw_" + name, job=job)
        grads[name], deltas[name], new_m[name], new_v[name] = g[None], d[None], m2[None], v2[None]
        return jres

    dw_in = [(dw_a, 0), (dw_b, W_GLA), (dw_c, W_GLA + W_SGU), (dw_d, N_MAIN)]
    q_a, t_o, t_bg, t_bs = update("w_ffn_out", w_ffn_out, m_w_ffn_out, v_w_ffn_out, [h_fo], [t_fo],
                                  job=_job_to_other_core([[(dw_a, 0)]] + whole([h_o, h_bg, h_bs])))
    q_in = [q_a, q_b, q_c, q_d]
    hr_in = D_MODEL // 2
    c_in_a, _ = _presum_w_in(dw_in, q_in, 0, hr_in // 8, "presum_w_in_a")
    c_in_b, (s_in_a,) = _presum_w_in(dw_in, q_in, hr_in // 8, 7 * hr_in // 8, "presum_w_in_b",
                                     job=_job_scatter([c_in_a]))
    update("w_ffn_in", w_ffn_in, m_w_ffn_in, v_w_ffn_in, [h_fi], [t_fi])
    update("w_out", w_out, m_w_out, v_w_out, [h_o], [t_o])
    update("w_branch_gla", w_branch_gla, m_w_branch_gla, v_w_branch_gla, [h_bg], [t_bg])
    update("w_branch_sgu", w_branch_sgu, m_w_branch_sgu, v_w_branch_sgu, [h_bs], [t_bs])
    (grad_x, d_g1), (s_in_b,) = _inproj_bwd(xt, dx1, norm_pre_mix, w_all, (dp_gla, dp_sgu, dp_mrg, dal),
                                            job=_job_scatter([c_in_b]))
    h_in = [_sum_slots(own_part(c_in_a), s_in_a, "sum_w_in_a"), _sum_slots(own_part(c_in_b), s_in_b, "sum_w_in_b")]
    t_in = _run_job(_job_to_other_core(whole(h_in)), "swap_w_in")
    for store, val in zip((grads, deltas, new_m, new_v),
                          _adamw_transposed(w_in_t, m_in_t, v_in_t, h_in, t_in, "adamw_w_in")):
        store["w_in"] = val.T[None]

    small_names = ["w_spatial", "w_gate_up", "norm_pre_mix", "norm_post_mix", "norm_pre_ffn", "norm_post_ffn", "b_gate",
                   "b_spatial", "gla_norm", "sgu_ln_g", "sgu_ln_b"]
    loss_out, small = _small_adamw(
        _small_sum([d_wsp, d_wgu, d_g1, d_gpm, d_gpf, d_gpo, d_bg, d_bsp_t, d_gn, d_lng, d_lnb, loss]),
        [w_spatial, w_gate_up, norm_pre_mix, norm_post_mix, norm_pre_ffn, norm_post_ffn, b_gate, b_spatial, gla_norm,
         sgu_ln_g, sgu_ln_b],
        [m_w_spatial, m_w_gate_up, m_norm_pre_mix, m_norm_post_mix, m_norm_pre_ffn, m_norm_post_ffn, m_b_gate,
         m_b_spatial, m_gla_norm, m_sgu_ln_g, m_sgu_ln_b],
        [v_w_spatial, v_w_gate_up, v_norm_pre_mix, v_norm_post_mix, v_norm_pre_ffn, v_norm_post_ffn, v_b_gate,
         v_b_spatial, v_gla_norm, v_sgu_ln_g, v_sgu_ln_b])
    for store, vals in zip((grads, deltas, new_m, new_v), small):
        store.update(zip(small_names, vals))

    order = ["norm_pre_mix", "w_in", "w_gate_up", "b_gate", "gla_norm", "sgu_ln_g", "sgu_ln_b", "w_spatial", "b_spatial",
             "w_branch_gla", "w_branch_sgu", "w_out", "norm_post_mix", "norm_pre_ffn", "w_ffn_in", "w_ffn_out",
             "norm_post_ffn"]
    out = [loss_out, grad_x[None]]
    for store in (grads, deltas, new_m, new_v):
        out.extend(store[n] for n in order)
    return tuple(out)
```

```python
import jax
import jax.numpy as jnp
from jax import lax
from jax.experimental import pallas as pl
from jax.experimental.pallas import tpu as pltpu

F32 = jnp.float32
BF16 = jnp.bfloat16

D_MODEL = 1024
GLA_HEADS = 4
GLA_DK = 128
GLA_DV = 256
GLA_QK = GLA_HEADS * GLA_DK
GLA_V = GLA_HEADS * GLA_DV
GLA_RANK = 16
GLA_TAU = 16.0
CHUNK = 64
SGU_GROUPS = 4
SGU_BLOCK = 128
SGU_DG = 256
D_FF = 2816
EPS = 1e-6
LANES = 128

OFF_Q, OFF_K, OFF_V, OFF_R, OFF_SU, OFF_SV, OFF_GG, OFF_GS, OFF_AL = 0, 512, 1024, 2048, 3072, 4096, 5120, 6144, 7168
W_GLA, W_SGU, W_MRG = 3072, 2048, 2048
N_MAIN = 7168
N_ALL = N_MAIN + LANES
_IN_SPLITS = (GLA_QK, GLA_QK, GLA_V, GLA_V, GLA_RANK, 1024, 1024, 1024, 1024)
_IN_STARTS = tuple(sum(_IN_SPLITS[:i]) for i in range(len(_IN_SPLITS) + 1))
_IN_DST = (OFF_Q, OFF_K, OFF_V, OFF_R, OFF_AL, OFF_SU, OFF_SV, OFF_GG, OFF_GS)
D_IN = _IN_STARTS[-1]

ADAM_LR = 0.001
ADAM_B1 = 0.9
ADAM_B2 = 0.999
ADAM_EPS = 1e-08
ADAM_WD = 0.01
ADAM_STEP = 10

VMEM_LIMIT_BYTES = 56 * 1024 * 1024
N_CHIPS = 4
N_PEER = N_CHIPS - 1
N_DEV = 8
MESH = pl.DeviceIdType.MESH

_NN = (((1,), (0,)), ((), ()))
_NT = (((1,), (1,)), ((), ()))
_TN = (((0,), (0,)), ((), ()))


def _dot(a, b, dims=_NN):
    return lax.dot_general(a, b, dims, preferred_element_type=F32)


def _split(x):
    hi = x.astype(BF16)
    lo = (x - hi.astype(F32)).astype(BF16)
    return hi, lo


def _dot_bf16(a, b, dims=_NN):
    return _dot(a.astype(BF16), b.astype(BF16), dims)


def _dot_exact_lhs(m, x):
    xh, xl = _split(x)
    return _dot(m, xh) + _dot(m, xl)


def _sigmoid(x):
    return 0.5 * jnp.tanh(0.5 * x) + 0.5


def _log_sigmoid(x):
    return jnp.minimum(x, 0.0) - jnp.log(1.0 + jnp.exp(-jnp.abs(x)))


_GELU_C = 0.7978845608028654
_GELU_A = 0.044715


def _gelu_and_grad(x):
    x2 = x * x
    t = jnp.tanh(_GELU_C * (x + _GELU_A * x * x2))
    g = 0.5 * x * (1.0 + t)
    dg = 0.5 * (1.0 + t) + 0.5 * x * (1.0 - t * t) * (_GELU_C * (1.0 + 3.0 * _GELU_A * x2))
    return g, dg


def _gelu(x):
    t = jnp.tanh(_GELU_C * (x + _GELU_A * x * x * x))
    return 0.5 * x * (1.0 + t)


def _rms_stats(x):
    return lax.rsqrt(jnp.mean(x * x, axis=-1, keepdims=True) + EPS)


def _rms_bwd(dout, y, r, g):
    yhat = y * r
    dn = dout * g
    dy = r * (dn - yhat * jnp.mean(dn * yhat, axis=-1, keepdims=True))
    return dy, dout * yhat


def _whole():
    return pl.BlockSpec(memory_space=pltpu.VMEM)


def _row_tile(T, want):
    t = min(T, want)
    assert T % t == 0
    return t


def _chunk_masks(tT, upper):
    row = lax.broadcasted_iota(jnp.int32, (tT, tT), 0)
    col = lax.broadcasted_iota(jnp.int32, (tT, tT), 1)
    same = (row // CHUNK) == (col // CHUNK)
    tri = (col > row) if upper else (col < row)
    return jnp.where(same & tri, 1.0, 0.0).astype(BF16)


class _Job:
    def __init__(self, ins, out_shapes, scratch, start, finish, mid=None):
        self.ins, self.out_shapes, self.scratch = list(ins), list(out_shapes), list(scratch)
        self.start, self.finish, self.mid = start, finish, mid


def _join(*jobs):
    def split(refs, counts):
        out, at = [], 0
        for n in counts:
            out.append(refs[at:at + n])
            at += n
        return out

    ni, no, ns = [len(j.ins) for j in jobs], [len(j.out_shapes) for j in jobs], [len(j.scratch) for j in jobs]

    def start(ins, outs, scr):
        for j, a, b, c in zip(jobs, split(ins, ni), split(outs, no), split(scr, ns)):
            j.start(a, b, c)

    def finish(ins, outs, scr):
        for j, a, b, c in zip(jobs, split(ins, ni), split(outs, no), split(scr, ns)):
            j.finish(a, b, c)

    def mid(ins, outs, scr):
        for j, a, b, c in zip(jobs, split(ins, ni), split(outs, no), split(scr, ns)):
            if j.mid is not None:
                j.mid(a, b, c)

    return _Job(sum((j.ins for j in jobs), []), sum((j.out_shapes for j in jobs), []),
                sum((j.scratch for j in jobs), []), start, finish, mid if any(j.mid for j in jobs) else None)


def _mesh_pos():
    return lax.axis_index("x"), lax.axis_index("y"), lax.axis_index("c")


def _peer_chips(xi, yi):
    return [(1 - xi, yi), (xi, 1 - yi), (1 - xi, 1 - yi)]


def _half(ci, rows):
    return pl.ds(pl.multiple_of(ci * rows, 8), rows)


def _sds(shape, dtype):
    return jax.ShapeDtypeStruct(tuple(shape), dtype)


def _job_gather(arrs):
    n = len(arrs)
    kinds = 12
    Y0, Y1, X1, X0, ON_X, ON_Y, D2D = 0, 1, 2, 3, 4, 5, 6

    def copies(ins, outs, scr):
        send_sems, recv_sems = scr
        xi, yi, ci = _mesh_pos()
        me, cx, cy, cd = 2 * xi + yi, 2 * (1 - xi) + yi, 2 * xi + (1 - yi), 2 * (1 - xi) + (1 - yi)
        to_x, to_y, to_core = (1 - xi, yi, ci), (xi, 1 - yi, ci), (xi, yi, 1 - ci)
        table = []
        for k in range(n):
            qr = arrs[k].shape[0] // 4

            def rows(core, q):
                return pl.ds(pl.multiple_of((2 * core + q) * qr, 8), qr)

            def cp(kind, src, dst, to):
                s = k * kinds + kind
                return pltpu.make_async_remote_copy(src_ref=src, dst_ref=dst, send_sem=send_sems.at[s],
                                                    recv_sem=recv_sems.at[s], device_id=to, device_id_type=MESH)

            def slab(chip, core, q):
                return outs[k].at[chip, rows(core, q)]

            t = {}
            for kind, q, to, frm in ((Y0, 0, to_y, cy), (Y1, 1, to_y, cy), (X1, 1, to_x, cx), (X0, 0, to_x, cx)):
                mine = ins[k].at[rows(ci, q)]
                t[kind] = (cp(kind, mine, slab(me, ci, q), to), cp(kind, mine, slab(frm, ci, q), to))
            t[ON_X] = (cp(ON_X, slab(cy, ci, 0), slab(cy, ci, 0), to_x), cp(ON_X, slab(cy, ci, 0), slab(cd, ci, 0), to_x))
            t[ON_Y] = (cp(ON_Y, slab(cx, ci, 1), slab(cx, ci, 1), to_y), cp(ON_Y, slab(cx, ci, 1), slab(cd, ci, 1), to_y))
            for i, (chip, q) in enumerate(((cy, 0), (cy, 1), (cx, 1), (cx, 0), (cd, 0), (cd, 1))):
                t[D2D + i] = (cp(D2D + i, slab(chip, ci, q), slab(chip, ci, q), to_core),
                              cp(D2D + i, slab(chip, ci, q), slab(chip, 1 - ci, q), to_core))
            table.append(t)
        return table

    def start(ins, outs, scr):
        table = copies(ins, outs, scr)
        for kind in (Y0, X1, Y1, X0):
            for t in table:
                t[kind][0].start()

    def arrived(table, kind, then):
        for t in table:
            t[kind][1].wait_recv()
            for nxt in then:
                t[nxt][0].start()

    def mid(ins, outs, scr):
        table = copies(ins, outs, scr)
        arrived(table, Y0, (ON_X, D2D + 0))
        arrived(table, X1, (ON_Y, D2D + 2))

    def finish(ins, outs, scr):
        table = copies(ins, outs, scr)
        arrived(table, Y1, (D2D + 1,))
        arrived(table, X0, (D2D + 3,))
        arrived(table, ON_X, (D2D + 4,))
        arrived(table, ON_Y, (D2D + 5,))
        for t in table:
            for i in range(6):
                t[D2D + i][1].wait_recv()
            for kind in range(kinds):
                t[kind][0].wait_send()

    dma = pltpu.SemaphoreType.DMA
    return _Job(arrs, [_sds((N_CHIPS,) + a.shape, a.dtype) for a in arrs], [dma((n * kinds,))] * 2, start, finish, mid)


def _job_scatter(parts):
    n = len(parts)

    def copies(ins, outs, scr):
        send_sems, recv_sems = scr
        xi, yi, ci = _mesh_pos()
        res = []
        for k in range(n):
            for j, (px, py) in enumerate(_peer_chips(xi, yi)):
                s = k * N_PEER + j
                res.append(pltpu.make_async_remote_copy(
                    src_ref=ins[k].at[2 * px + py], dst_ref=outs[k].at[j], send_sem=send_sems.at[s],
                    recv_sem=recv_sems.at[s], device_id=(px, py, ci), device_id_type=MESH))
        return res

    def start(ins, outs, scr):
        for cp in copies(ins, outs, scr):
            cp.start()

    def finish(ins, outs, scr):
        for cp in copies(ins, outs, scr):
            cp.wait_recv()
            cp.wait_send()

    dma = pltpu.SemaphoreType.DMA
    return _Job(parts, [_sds((N_PEER,) + p.shape[1:], p.dtype) for p in parts], [dma((n * N_PEER,))] * 2, start, finish)


def _job_to_other_core(groups):
    pieces = [(g, a, off) for g, group in enumerate(groups) for a, off in group]
    n = len(pieces)

    def geometry(group):
        a0, off0 = group[0]
        if off0 is None:
            return a0.shape
        if a0.ndim == 4:
            return (N_CHIPS, a0.shape[2], a0.shape[3])
        return (a0.shape[0] // 2, sum(a.shape[1] for a, _ in group))

    def copies(ins, outs, scr):
        send_sems, recv_sems = scr
        xi, yi, ci = _mesh_pos()
        res = []
        for p, (g, a, off) in enumerate(pieces):
            if off is None:
                give, land = ins[p], outs[g]
            elif a.ndim == 4:
                give, land = ins[p].at[pl.ds(0, N_CHIPS), 1 - ci], outs[g]
            else:
                hr, w = a.shape[0] // 2, a.shape[1]
                give, land = ins[p].at[_half(1 - ci, hr)], outs[g].at[pl.ds(0, hr), pl.ds(off, w)]
            res.append(pltpu.make_async_remote_copy(
                src_ref=give, dst_ref=land, send_sem=send_sems.at[p], recv_sem=recv_sems.at[p],
                device_id=(xi, yi, 1 - ci), device_id_type=MESH))
        return res

    def start(ins, outs, scr):
        for cp in copies(ins, outs, scr):
            cp.start()

    def finish(ins, outs, scr):
        for cp in copies(ins, outs, scr):
            cp.wait_recv()
            cp.wait_send()

    dma = pltpu.SemaphoreType.DMA
    return _Job([a for _, a, _ in pieces], [_sds(geometry(group), group[0][0].dtype) for group in groups],
                [dma((n,))] * 2, start, finish)


def _call(body, *, name, grid, in_specs, out_specs, out_shape, args, scratch_shapes=(), parallel=False, job=None,
          by_core=False):
    n_in, n_out, n_scr = len(in_specs), len(out_specs), len(scratch_shapes)
    hbm = pl.BlockSpec(memory_space=pl.ANY)
    n_ji, n_jo = (len(job.ins), len(job.out_shapes)) if job is not None else (0, 0)
    lead = 1 if by_core else 0

    def kernel_fn(*refs):
        core, refs = refs[:lead], refs[lead:]
        ins, refs = refs[:n_in], refs[n_in:]
        j_ins, refs = refs[:n_ji], refs[n_ji:]
        outs, refs = refs[:n_out], refs[n_out:]
        j_outs, refs = refs[:n_jo], refs[n_jo:]
        scr, j_scr = refs[:n_scr], refs[n_scr:]
        if job is None:
            body(*core, *ins, *outs, *scr)
            return
        ids = [pl.program_id(d) for d in range(len(grid))]
        first = ids[0] == 0
        last = ids[0] == grid[0] - 1
        for d in range(1, len(grid)):
            first = first & (ids[d] == 0)
            last = last & (ids[d] == grid[d] - 1)

        @pl.when(first)
        def _():
            job.start(j_ins, j_outs, j_scr)

        if job.mid is not None and grid[0] >= 4:
            half_way = ids[0] == grid[0] // 2
            for d in range(1, len(grid)):
                half_way = half_way & (ids[d] == 0)

            @pl.when(half_way)
            def _():
                job.mid(j_ins, j_outs, j_scr)

        body(*core, *ins, *outs, *scr)

        @pl.when(last)
        def _():
            if job.mid is not None and grid[0] < 4:
                job.mid(j_ins, j_outs, j_scr)
            job.finish(j_ins, j_outs, j_scr)

    sem = ("parallel" if parallel and job is None else "arbitrary",) * len(grid)
    all_in = list(in_specs) + [hbm] * n_ji
    all_out = list(out_specs) + [hbm] * n_jo
    all_scratch = list(scratch_shapes) + (job.scratch if job is not None else [])
    all_shapes = list(out_shape) + (job.out_shapes if job is not None else [])
    all_args = list(args) + (job.ins if job is not None else [])
    params = pltpu.CompilerParams(dimension_semantics=sem, vmem_limit_bytes=VMEM_LIMIT_BYTES)
    if by_core:
        spec = pltpu.PrefetchScalarGridSpec(num_scalar_prefetch=1, grid=grid, in_specs=all_in, out_specs=all_out,
                                            scratch_shapes=all_scratch)
        core = lax.axis_index("c").astype(jnp.int32).reshape(1)
        res = pl.pallas_call(kernel_fn, name=name, grid_spec=spec, out_shape=all_shapes, compiler_params=params)(
            core, *all_args)
    else:
        res = pl.pallas_call(kernel_fn, name=name, grid=grid, in_specs=all_in, out_specs=all_out, out_shape=all_shapes,
                             scratch_shapes=all_scratch, compiler_params=params)(*all_args)
    return list(res[:n_out]), list(res[n_out:])


def _run_job(job, name):
    n_i, n_o = len(job.ins), len(job.out_shapes)

    def body(*refs):
        ins, outs, scr = refs[:n_i], refs[n_i:n_i + n_o], refs[n_i + n_o:]
        job.start(ins, outs, scr)
        if job.mid is not None:
            job.mid(ins, outs, scr)
        job.finish(ins, outs, scr)

    hbm = pl.BlockSpec(memory_space=pl.ANY)
    return list(pl.pallas_call(body, name=name, in_specs=[hbm] * n_i, out_specs=[hbm] * n_o, out_shape=job.out_shapes,
                               scratch_shapes=job.scratch)(*job.ins))


def _adam_values(w, m, v, g):
    m2 = ADAM_B1 * m + (1.0 - ADAM_B1) * g
    v2 = ADAM_B2 * v + (1.0 - ADAM_B2) * (g * g)
    delta = -ADAM_LR * ((m2 / (1.0 - ADAM_B1 ** ADAM_STEP)) / (jnp.sqrt(v2 / (1.0 - ADAM_B2 ** ADAM_STEP)) + ADAM_EPS)
                        + ADAM_WD * w)
    return delta, m2, v2


_P_WSP, _P_WGU, _P_NORM, _P_BG, _P_BSP, _P_HEAD, _P_LOSS, _P_ROWS = 0, 512, 576, 608, 616, 624, 720, 736


def _small_sum(dgrads):
    hr = _P_ROWS // 2

    def body(dwsp, dwgu, dg1, dgpm, dgpf, dgpo, dbg, dbspt, dgn, dlng, dlnb, loss_in, tot, pack, pair, slots, send_sems,
             recv_sems):
        xi, yi, ci = _mesh_pos()
        chip = 2 * xi + yi

        pack[...] = jnp.zeros_like(pack)
        for g in range(SGU_GROUPS):
            pack[_P_WSP + g * SGU_BLOCK:_P_WSP + (g + 1) * SGU_BLOCK] = dwsp[g]
        for j in range(N_CHIPS):
            pack[_P_WGU + GLA_RANK * j:_P_WGU + GLA_RANK * (j + 1)] = dwgu[0:GLA_RANK, LANES * j:LANES * (j + 1)]
        for k, r in enumerate((dg1, dgpm, dgpf, dgpo)):
            for q in range(8):
                pack[_P_NORM + 8 * k + q:_P_NORM + 8 * k + q + 1] = r[:, LANES * q:LANES * (q + 1)]
        for q in range(4):
            pack[_P_BG + q:_P_BG + q + 1] = dbg[:, LANES * q:LANES * (q + 1)]
        pack[_P_BSP:_P_BSP + SGU_GROUPS] = jnp.transpose(dbspt[...])[0:SGU_GROUPS]
        for k, r in enumerate((dgn, dlng, dlnb)):
            for j in range(N_CHIPS):
                for hh in range(4):
                    row = _P_HEAD + 32 * k + 8 * j + hh
                    pack[row:row + 1, 0:64] = r[:, 256 * hh + 64 * j:256 * hh + 64 * (j + 1)]
        pack[_P_LOSS:_P_LOSS + 1] = loss_in[...]

        sibling = dict(device_id=(xi, yi, 1 - ci), device_id_type=MESH)
        to_sibling = pltpu.make_async_remote_copy(src_ref=pack, dst_ref=pair, send_sem=send_sems.at[N_PEER],
                                                  recv_sem=recv_sems.at[N_PEER], **sibling)
        to_sibling.start()
        to_sibling.wait_recv()
        to_sibling.wait_send()
        pack[...] = pack[...] + pair[...]
        mine = pl.ds(pl.multiple_of(ci * hr, 8), hr)
        theirs = pl.ds(pl.multiple_of((1 - ci) * hr, 8), hr)
        slots[chip] = pack[mine, :]

        def copy(j, slot):
            px, py = _peer_chips(xi, yi)[j]
            return pltpu.make_async_remote_copy(
                src_ref=pack.at[mine], dst_ref=slots.at[slot(2 * px + py)], send_sem=send_sems.at[j],
                recv_sem=recv_sems.at[j], device_id=(px, py, ci), device_id_type=MESH)

        sends = [copy(j, lambda peer_chip: chip) for j in range(N_PEER)]
        for cp in sends:
            cp.start()
        for j in range(N_PEER):
            copy(j, lambda peer_chip: peer_chip).wait_recv()
        for cp in sends:
            cp.wait_send()
        acc = slots[0]
        for d in range(1, N_CHIPS):
            acc = acc + slots[d]
        tot[mine, :] = acc
        half_over = pltpu.make_async_remote_copy(src_ref=tot.at[mine], dst_ref=tot.at[mine], send_sem=send_sems.at[N_PEER + 1],
                                                 recv_sem=recv_sems.at[N_PEER + 1], **sibling)
        half_back = pltpu.make_async_remote_copy(src_ref=tot.at[mine], dst_ref=tot.at[theirs], send_sem=send_sems.at[N_PEER + 1],
                                                 recv_sem=recv_sems.at[N_PEER + 1], **sibling)
        half_over.start()
        half_back.wait_recv()
        half_over.wait_send()

    return pl.pallas_call(
        body, name="small_sum", in_specs=[_whole()] * 12, out_specs=_whole(), out_shape=_sds((_P_ROWS, LANES), F32),
        scratch_shapes=[pltpu.VMEM((_P_ROWS, LANES), F32), pltpu.VMEM((_P_ROWS, LANES), F32),
                        pltpu.VMEM((N_CHIPS, hr, LANES), F32),
                        pltpu.SemaphoreType.DMA((N_PEER + 2,)), pltpu.SemaphoreType.DMA((N_PEER + 2,))],
        compiler_params=pltpu.CompilerParams(vmem_limit_bytes=VMEM_LIMIT_BYTES),
    )(*dgrads)


def _small_adamw(tot, ws, ms, vs):
    n = len(ws)

    def body(*refs):
        tot = refs[0]
        w_refs, m_refs, v_refs = refs[1:1 + n], refs[1 + n:1 + 2 * n], refs[1 + 2 * n:1 + 3 * n]
        loss_out = refs[1 + 3 * n]
        outs = refs[2 + 3 * n:]
        chip = 2 * lax.axis_index("x") + lax.axis_index("y")
        loss_out[...] = tot[_P_LOSS:_P_LOSS + 1, 0:1]

        def step(k, g, pick, put):
            d, m2, v2 = _adam_values(pick(w_refs[k]), pick(m_refs[k]), pick(v_refs[k]), g)
            for o, val in zip((outs[k], outs[n + k], outs[2 * n + k], outs[3 * n + k]), (g, d, m2, v2)):
                put(o, val)

        def whole(ref):
            return ref[0]

        def put_whole(ref, val):
            ref[0] = val

        for g in range(SGU_GROUPS):
            def pick_g(ref, g=g):
                return ref[0, g]

            def put_g(ref, val, g=g):
                ref[0, g] = val

            step(0, tot[_P_WSP + g * SGU_BLOCK:_P_WSP + (g + 1) * SGU_BLOCK], pick_g, put_g)
        step(1, tot[pl.ds(pl.multiple_of(_P_WGU + GLA_RANK * chip, GLA_RANK), GLA_RANK), :], whole, put_whole)
        for k, (base, chunks) in enumerate(((_P_NORM, 8), (_P_NORM + 8, 8), (_P_NORM + 16, 8), (_P_NORM + 24, 8), (_P_BG, 4))):
            for q in range(chunks):
                def pick_q(ref, q=q):
                    return ref[:, LANES * q:LANES * (q + 1)]

                def put_q(ref, val, q=q):
                    ref[:, LANES * q:LANES * (q + 1)] = val

                step(2 + k, tot[base + q:base + q + 1], pick_q, put_q)
        step(7, tot[_P_BSP:_P_BSP + SGU_GROUPS], whole, put_whole)
        for k in range(3):
            mine = tot[pl.ds(pl.multiple_of(_P_HEAD + 32 * k + 8 * chip, 8), 8), :]
            step(8 + k, mine[0:4, 0:64], whole, put_whole)

    shapes = [_sds(w.shape, F32) for w in ws]
    res = pl.pallas_call(
        body, name="small_adamw", in_specs=[_whole()] * (1 + 3 * n), out_specs=[_whole()] * (1 + 4 * n),
        out_shape=[_sds((1, 1), F32)] + shapes * 4,
        compiler_params=pltpu.CompilerParams(vmem_limit_bytes=VMEM_LIMIT_BYTES),
    )(tot, *ws, *ms, *vs)
    return res[0].reshape(()), [list(res[1 + i * n:1 + (i + 1) * n]) for i in range(4)]


def _w_in_pieces():
    blk = D_IN // N_CHIPS
    pieces = []
    for s in range(len(_IN_SPLITS)):
        lo_s, hi_s = _IN_STARTS[s], _IN_STARTS[s + 1]
        for j in range(N_CHIPS):
            lo, hi = max(lo_s, j * blk), min(hi_s, (j + 1) * blk)
            if lo < hi:
                pieces.append((j, lo - j * blk, _IN_DST[s] + lo - lo_s, hi - lo))
    return pieces


def _relayout_w_in(gathered):
    _, rows, blk = gathered.shape
    tr = 256

    def body(g_ref, o_ref):
        o_ref[:, OFF_AL:N_ALL] = jnp.zeros((tr, LANES), BF16)
        for j, src, dst, w in _w_in_pieces():
            o_ref[:, dst:dst + w] = g_ref[j, :, src:src + w]

    res, _ = _call(body, name="relayout_w_in", grid=(rows // tr,), parallel=True,
                   in_specs=[pl.BlockSpec((N_CHIPS, tr, blk), lambda i: (0, i, 0))],
                   out_specs=[pl.BlockSpec((tr, N_ALL), lambda i: (i, 0))],
                   out_shape=[_sds((rows, N_ALL), BF16)], args=(gathered,))
    return res[0]


def _update_row_tile(rows):
    for t in range(min(rows, 256), 7, -8):
        if rows % t == 0:
            return t
    return rows


def _presum_w_in(dws, theirs, row0, rows, name, job=None):
    hr = theirs[0].shape[0]
    blk = D_IN // N_CHIPS
    tr = 64
    assert row0 % tr == 0 and rows % tr == 0
    nh, t0 = hr // tr, row0 // tr
    n = len(dws)

    def body(core_ref, *refs):
        dw_refs, q_refs, (o_ref, s_scr) = refs[:n], refs[n:2 * n], refs[2 * n:]
        for p, (a, off) in enumerate(dws):
            w = a.shape[1]
            s_scr[:, off:off + w] = (dw_refs[p][...] + q_refs[p][...]).astype(BF16)
        for j, src, dst, w in _w_in_pieces():
            o_ref[j, :, src:src + w] = s_scr[:, dst:dst + w]

    in_specs = [pl.BlockSpec((tr, a.shape[1]), lambda i, core: (i + t0 + core[0] * nh, 0)) for a, _ in dws]
    in_specs += [pl.BlockSpec((tr, q.shape[1]), lambda i, core: (i + t0, 0)) for q in theirs]
    res, jres = _call(body, name=name, grid=(rows // tr,), parallel=True, in_specs=in_specs,
                      out_specs=[pl.BlockSpec((N_CHIPS, tr, blk), lambda i, core: (0, i, 0))],
                      out_shape=[_sds((N_CHIPS, rows, blk), BF16)], scratch_shapes=[pltpu.VMEM((tr, N_ALL), BF16)],
                      args=(*[a for a, _ in dws], *theirs), job=job, by_core=True)
    return res[0], jres


def _presum(dws, theirs, name):
    dw, n = dws[0], len(dws)
    assert all(d.shape == dw.shape for d in dws)
    if dw.ndim == 4:
        _, _, hr, c = dw.shape
        tr = _update_row_tile(hr)
        mine = pl.BlockSpec((1, 1, tr, c), lambda j, i, core: (j, core[0], i, 0))
        other = pl.BlockSpec((1, tr, c), lambda j, i, core: (j, i, 0))
    else:
        hr, c = dw.shape[0] // 2, dw.shape[1] // N_CHIPS
        tr = _update_row_tile(hr)
        nh = hr // tr
        mine = pl.BlockSpec((tr, c), lambda j, i, core: (i + core[0] * nh, j))
        other = pl.BlockSpec((tr, c), lambda j, i, core: (i, j))

    def body(core_ref, *refs):
        for a_ref, q_ref, o_ref in zip(refs[:n], refs[n:2 * n], refs[2 * n:]):
            o_ref[...] = (a_ref[...].reshape(tr, c) + q_ref[...].reshape(tr, c)).astype(BF16).reshape(o_ref.shape)

    res, _ = _call(body, name=name, grid=(N_CHIPS, hr // tr), parallel=True, in_specs=[mine] * n + [other] * n,
                   out_specs=[pl.BlockSpec((1, tr, c), lambda j, i, core: (j, i, 0))] * n,
                   out_shape=[_sds((N_CHIPS, hr, c), BF16)] * n, args=(*dws, *theirs), by_core=True)
    return res


def _sum_slots(sums, slots, name):
    n = len(sums)
    _, rows, cols = sums[0].shape
    assert all(s.shape == sums[0].shape for s in sums)
    tr = _update_row_tile(rows)

    def body(chip_ref, *refs):
        for own_ref, s_ref, o_ref in zip(refs[:n], refs[n:2 * n], refs[2 * n:]):
            acc = own_ref[...].astype(F32)
            for j in range(N_PEER):
                acc = acc + s_ref[j].astype(F32)
            o_ref[...] = acc

    chip = (2 * lax.axis_index("x") + lax.axis_index("y")).astype(jnp.int32).reshape(1)
    spec = pltpu.PrefetchScalarGridSpec(
        num_scalar_prefetch=1, grid=(rows // tr,),
        in_specs=[pl.BlockSpec((None, tr, cols), lambda i, chip: (chip[0], i, 0))] * n
        + [pl.BlockSpec((N_PEER, tr, cols), lambda i, chip: (0, i, 0))] * n,
        out_specs=[pl.BlockSpec((tr, cols), lambda i, chip: (i, 0))] * n)
    return list(pl.pallas_call(
        body, name=name, grid_spec=spec, out_shape=[_sds((rows, cols), F32)] * n,
        compiler_params=pltpu.CompilerParams(dimension_semantics=("parallel",), vmem_limit_bytes=VMEM_LIMIT_BYTES),
    )(chip, *sums, *slots))


def _adamw(w, m, v, g_mine, g_theirs, name, job=None):
    rows, cols = w.shape
    part_rows = [p.shape[0] for p in g_mine]
    assert sum(part_rows) == rows // 2 and [p.shape[0] for p in g_theirs] == part_rows
    tr = _update_row_tile(min(part_rows))
    assert all(r % tr == 0 for r in part_rows)
    nh = (rows // 2) // tr
    starts = [sum(part_rows[:k]) // tr for k in range(len(part_rows))]
    n_parts = len(part_rows)

    def body(core_ref, w_ref, m_ref, v_ref, *rest):
        g_refs, (g_out, d_out, m_out, v_out) = rest[:-4], rest[-4:]
        step = pl.program_id(0)
        mine_here = (step // nh) == core_ref[0]
        q = step % nh
        g = None
        for k in reversed(range(n_parts)):
            val = jnp.where(mine_here, g_refs[k][...], g_refs[n_parts + k][...])
            g = val if g is None else jnp.where(q < starts[k + 1], val, g)
        d, m2, v2 = _adam_values(w_ref[...], m_ref[...], v_ref[...], g)
        g_out[...] = g
        m_out[...] = m2
        v_out[...] = v2
        d_out[...] = d

    def g_spec(k, mine):
        last = part_rows[k] // tr - 1

        def index(i, core):
            half = core[0] if mine else 1 - core[0]
            here = jnp.clip(i % nh - starts[k], 0, last)
            return (jnp.where(i // nh == half, here, jnp.where(i // nh > half, last, 0)), 0)

        return pl.BlockSpec((tr, cols), index)

    spec = pl.BlockSpec((tr, cols), lambda i, core: (i, 0))
    g_specs = [g_spec(k, True) for k in range(n_parts)] + [g_spec(k, False) for k in range(n_parts)]
    return _call(body, name=name, grid=(rows // tr,), parallel=True, in_specs=[spec] * 3 + g_specs,
                 out_specs=[spec] * 4, out_shape=[_sds((rows, cols), F32)] * 4, args=(w, m, v, *g_mine, *g_theirs),
                 job=job, by_core=True)


def _transposed_cast(wt):
    cols, rows = wt.shape
    tile = 4 * LANES

    def body(x_ref, o_ref):
        o_ref[...] = jnp.transpose(x_ref[...]).astype(BF16)

    res, _ = _call(body, name="transpose_w_in", grid=(pl.cdiv(cols, tile),), parallel=True,
                   in_specs=[pl.BlockSpec((tile, rows), lambda j: (j, 0))],
                   out_specs=[pl.BlockSpec((rows, tile), lambda j: (0, j))], out_shape=[_sds((rows, cols), BF16)],
                   args=(wt,))
    return res[0]


def _cast_weights(ws, w_fi, job=None):
    steps = 4
    cols = w_fi.shape[1]
    tile = w_fi.shape[0] // (2 * steps)

    def body(*refs):
        for i_ref, o_ref in zip(refs[:len(refs) // 2], refs[len(refs) // 2:]):
            o_ref[...] = i_ref[...].astype(BF16)

    row_specs = [pl.BlockSpec((w.shape[0] // steps, w.shape[1]), lambda i: (i, 0)) for w in ws]
    half_spec = pl.BlockSpec((tile, cols), lambda i: (i, 0))
    return _call(body, name="cast_weights", grid=(steps,), parallel=True,
                 in_specs=row_specs + [pl.BlockSpec((None, tile, cols), lambda i, k=k: (k, i, 0)) for k in range(2)],
                 out_specs=row_specs + [half_spec, half_spec],
                 out_shape=[_sds(w.shape, BF16) for w in ws] + [_sds((steps * tile, cols), BF16)] * 2,
                 args=(*ws, w_fi.reshape(2, steps * tile, cols), w_fi.reshape(2, steps * tile, cols)), job=job)


def _adamw_transposed(wt, mt, vt, g_mine, g_theirs, name):
    cols, rows = wt.shape
    n_parts = len(g_mine)

    def body(w_ref, m_ref, v_ref, *rest):
        g_refs, (g_out, d_out, m_out, v_out) = rest[:-4], rest[-4:]
        mine = jnp.concatenate([r[...] for r in g_refs[:n_parts]], axis=0)
        theirs = jnp.concatenate([r[...] for r in g_refs[n_parts:]], axis=0)
        first = lax.axis_index("c") == 0
        g = jnp.transpose(jnp.concatenate([jnp.where(first, mine, theirs), jnp.where(first, theirs, mine)], axis=0))
        d, m2, v2 = _adam_values(w_ref[...], m_ref[...], v_ref[...], g)
        g_out[...] = g
        m_out[...] = m2
        v_out[...] = v2
        d_out[...] = d

    spec = pl.BlockSpec((LANES, rows), lambda j: (j, 0))
    g_specs = [pl.BlockSpec((p.shape[0], LANES), lambda j: (0, j)) for p in g_mine] * 2
    res, _ = _call(body, name=name, grid=(pl.cdiv(cols, LANES),), parallel=True, in_specs=[spec] * 3 + g_specs,
                   out_specs=[spec] * 4, out_shape=[_sds((cols, rows), F32)] * 4, args=(wt, mt, vt, *g_mine, *g_theirs))
    return res


def _inproj_fwd(x, g1, w_all, job=None):
    T = x.shape[0]
    tT = _row_tile(T, 512)

    def body(x_ref, g_ref, w_ref, a_ref, proj_ref, alow_ref):
        xv = x_ref[...]
        a = (xv * _rms_stats(xv) * g_ref[...]).astype(BF16)
        a_ref[...] = a
        for j in range(N_MAIN // 1024):
            cols = slice(j * 1024, (j + 1) * 1024)
            proj_ref[:, cols] = _dot(a, w_ref[:, cols]).astype(BF16)
        alow_ref[...] = _dot(a, w_ref[:, N_MAIN:N_ALL])

    row = lambda w: pl.BlockSpec((tT, w), lambda i: (i, 0))
    return _call(
        body, name="inproj_fwd", grid=(T // tT,), parallel=True,
        in_specs=[row(D_MODEL), pl.BlockSpec((1, D_MODEL), lambda i: (0, 0)), _whole()],
        out_specs=[row(D_MODEL), row(N_MAIN), row(LANES)],
        out_shape=[_sds((T, D_MODEL), BF16), _sds((T, N_MAIN), BF16), _sds((T, LANES), F32)],
        args=(x, g1, w_all), job=job)


def _gla_decay_terms(al_ref, wgu_ref, bg_ref, later_ref):
    logit = _dot_bf16(al_ref[...], wgu_ref[...]) + bg_ref[...]
    la = _log_sigmoid(logit) * (1.0 / GLA_TAU)
    delta = _dot_exact_lhs(later_ref[...], la)
    return logit, la, delta


def _gla_fwd(proj, alow, wgu, b_gate, gn, job=None):
    T = proj.shape[0]
    tT = _row_tile(T, 512)
    nc = tT // CHUNK

    def body(q_ref, k_ref, v_ref, r_ref, al_ref, wgu_ref, bg_ref, gn_ref, later_ref, y_ref, st_ref, s_scr):
        @pl.when(pl.program_id(0) == 0)
        def _():
            s_scr[...] = jnp.zeros_like(s_scr)

        _, la, delta = _gla_decay_terms(al_ref, wgu_ref, bg_ref, later_ref)
        kdec = (k_ref[...].astype(F32) * jnp.exp(delta)).astype(BF16)
        heads = range(GLA_HEADS)
        kcs = [slice(h * GLA_DK, (h + 1) * GLA_DK) for h in heads]
        vcs = [slice(h * GLA_DV, (h + 1) * GLA_DV) for h in heads]
        state = [s_scr[h] for h in heads]
        for c in range(nc):
            rows = slice(c * CHUNK, (c + 1) * CHUNK)
            first = slice(c * CHUNK, c * CHUNK + 1)
            dec = jnp.exp(la[first, :] + delta[first, :])
            upd_t = [_dot(v_ref[rows, vcs[h]], kdec[rows, kcs[h]], _TN) for h in heads]
            qs = [(q_ref[rows, kcs[h]].astype(F32) * (GLA_DK ** -0.5)).astype(BF16) for h in heads]
            for h in heads:
                state[h] = state[h] * dec[:, kcs[h]] + upd_t[h]
                st_ref[c, h] = state[h]
            o = [_dot(qs[h], state[h].astype(BF16), _NT) for h in heads]
            for h in heads:
                on = o[h] * _rms_stats(o[h]) * gn_ref[:, vcs[h]]
                rr = r_ref[rows, vcs[h]].astype(F32)
                y_ref[rows, vcs[h]] = (on * (rr * _sigmoid(rr))).astype(BF16)
        for h in heads:
            s_scr[h] = state[h]

    blk = lambda w, j: pl.BlockSpec((tT, w), lambda i: (i, j))
    return _call(
        body, name="gla_fwd", grid=(T // tT,),
        in_specs=[blk(512, 0), blk(512, 1), blk(1024, 1), blk(1024, 2), blk(LANES, 0)] + [_whole()] * 4,
        out_specs=[pl.BlockSpec((tT, GLA_V), lambda i: (i, 0)),
                   pl.BlockSpec((nc, GLA_HEADS, GLA_DV, GLA_DK), lambda i: (i, 0, 0, 0))],
        out_shape=[_sds((T, GLA_V), BF16), _sds((T // CHUNK, GLA_HEADS, GLA_DV, GLA_DK), F32)],
        scratch_shapes=[pltpu.VMEM((GLA_HEADS, GLA_DV, GLA_DK), F32)],
        args=(proj, proj, proj, proj, alow, wgu, b_gate, gn, _chunk_masks(tT, upper=True)), job=job)


def _sgu_mask():
    i = lax.broadcasted_iota(jnp.int32, (SGU_BLOCK, SGU_BLOCK), 0)
    j = lax.broadcasted_iota(jnp.int32, (SGU_BLOCK, SGU_BLOCK), 1)
    return lax.shift_right_logical(j, 6) <= lax.shift_right_logical(i, 6)


def _sgu_merge_fwd(x, proj, y_gla, ln_g, ln_b, w_sp, b_sp_t, w_bg, w_bs, w_o, g_pm, job=None):
    T = x.shape[0]
    tT = _row_tile(T, 512)
    nb = tT // SGU_BLOCK

    def body(x_ref, su_ref, sv_ref, gg_ref, gs_ref, yg_ref, lg_ref, lb_ref, w_ref, b_ref, wbg_ref, wbs_ref, wo_ref,
             g_ref, ys_ref, zg_ref, zs_ref, mg_ref, mix_ref, x1_ref):
        mask = _sgu_mask()
        for g in range(SGU_GROUPS):
            gc = slice(g * SGU_DG, (g + 1) * SGU_DG)
            wm = jnp.where(mask, w_ref[g], 0.0).astype(BF16)
            vf = _gelu(sv_ref[:, gc].astype(F32))
            mu = jnp.mean(vf, axis=-1, keepdims=True)
            vc = vf - mu
            rstd = lax.rsqrt(jnp.mean(vc * vc, axis=-1, keepdims=True) + EPS)
            vn = (vc * rstd * lg_ref[:, gc] + lb_ref[:, gc]).astype(BF16)
            u = _gelu(su_ref[:, gc].astype(F32))
            for b in range(nb):
                rows = slice(b * SGU_BLOCK, (b + 1) * SGU_BLOCK)
                mixed = _dot(wm, vn[rows, :]) + b_ref[:, g:g + 1]
                ys_ref[rows, gc] = (u[rows, :] * mixed).astype(BF16)
        zg = _dot(yg_ref[...], wbg_ref[...])
        zs = _dot(ys_ref[...], wbs_ref[...])
        zg_ref[...] = zg.astype(BF16)
        zs_ref[...] = zs.astype(BF16)
        merged = (_sigmoid(gg_ref[...].astype(F32)) * zg + _sigmoid(gs_ref[...].astype(F32)) * zs).astype(BF16)
        mg_ref[...] = merged
        mix = _dot(merged, wo_ref[...])
        mix_ref[...] = mix.astype(BF16)
        x1_ref[...] = x_ref[...] + mix * _rms_stats(mix) * g_ref[...]

    row = pl.BlockSpec((tT, D_MODEL), lambda i: (i, 0))
    blk = lambda j: pl.BlockSpec((tT, 1024), lambda i: (i, j))
    sds = lambda dt: _sds((T, D_MODEL), dt)
    return _call(body, name="sgu_merge_fwd", grid=(T // tT,), parallel=True,
                 in_specs=[row, blk(3), blk(4), blk(5), blk(6), row] + [_whole()] * 7
                 + [pl.BlockSpec((1, D_MODEL), lambda i: (0, 0))],
                 out_specs=[row] * 6, out_shape=[sds(BF16)] * 5 + [sds(F32)],
                 args=(x, proj, proj, proj, proj, y_gla, ln_g, ln_b, w_sp, b_sp_t, w_bg, w_bs, w_o, g_pm), job=job)


def _ffn_fwd_bwd(x1, tgt, w_fi_top, w_fi_bot, w_fo, g_pf, g_po):
    T = x1.shape[0]
    tT = _row_tile(T, 256)
    half = D_FF // 2
    kh = D_MODEL // 2

    def body(x1_ref, t_ref, top_ref, bot_ref, wfo_ref, gpf_ref, gpo_ref,
             h_ref, f_ref, dgu_ref, dy_ref, dx1_ref, loss_ref, dgpf_ref, dgpo_ref, gu_scr):
        @pl.when(pl.program_id(0) == 0)
        def _():
            loss_ref[...] = jnp.zeros_like(loss_ref)
            dgpf_ref[...] = jnp.zeros_like(dgpf_ref)
            dgpo_ref[...] = jnp.zeros_like(dgpo_ref)

        main = (half // 256) * 256
        pieces = (0, 1, None)

        def w_in_cols(ref, first_slab, p):
            if p is not None:
                return ref[first_slab + p, :, :main]
            return jnp.concatenate([ref[first_slab, :, main:], ref[first_slab + 1, :, main:]], axis=1)

        def w_out_rows(p):
            if p is not None:
                return wfo_ref[p * half:p * half + main, :]
            return jnp.concatenate([wfo_ref[main:half, :], wfo_ref[half + main:2 * half, :]], axis=0)

        def put(ref, base, p, val):
            if p is not None:
                ref[:, base + p * half:base + p * half + main] = val
            else:
                ref[:, base + main:base + half] = val[:, :half - main]
                ref[:, base + half + main:base + 2 * half] = val[:, half - main:]

        def get(ref, base, p):
            if p is not None:
                return ref[:, base + p * half:base + p * half + main]
            return jnp.concatenate([ref[:, base + main:base + half], ref[:, base + half + main:base + 2 * half]], axis=1)

        x1v = x1_ref[...]
        r2 = _rms_stats(x1v)
        h = (x1v * r2 * gpf_ref[...]).astype(BF16)
        h_ref[...] = h
        y = jnp.zeros((tT, D_MODEL), F32)
        for p in pieces:
            gate = _dot(h[:, :kh], w_in_cols(top_ref, 0, p)) + _dot(h[:, kh:], w_in_cols(bot_ref, 0, p))
            up = _dot(h[:, :kh], w_in_cols(top_ref, 2, p)) + _dot(h[:, kh:], w_in_cols(bot_ref, 2, p))
            put(gu_scr, 0, p, gate)
            put(gu_scr, D_FF, p, up)
            f = (gate * _sigmoid(gate) * up).astype(BF16)
            put(f_ref, 0, p, f)
            y = y + _dot(f, w_out_rows(p))
        r3 = _rms_stats(y)
        x2 = x1v + y * r3 * gpo_ref[...]
        err = x2 - t_ref[...]
        loss_ref[...] += jnp.sum(err * err) * (0.5 / D_MODEL)
        dx2 = err * (1.0 / D_MODEL)
        dy, dg = _rms_bwd(dx2, y, r3, gpo_ref[...])
        dgpo_ref[...] += jnp.sum(dg, axis=0, keepdims=True)
        dyb = dy.astype(BF16)
        dy_ref[...] = dyb
        dh_top = jnp.zeros((tT, kh), F32)
        dh_bot = jnp.zeros((tT, kh), F32)
        for p in pieces:
            df = _dot(dyb, w_out_rows(p), _NT)
            gate = get(gu_scr, 0, p)
            up = get(gu_scr, D_FF, p)
            sg = _sigmoid(gate)
            dgate = (df * up * (sg * (1.0 + gate * (1.0 - sg)))).astype(BF16)
            dup = (df * (gate * sg)).astype(BF16)
            put(dgu_ref, 0, p, dgate)
            put(dgu_ref, D_FF, p, dup)
            dh_top = dh_top + _dot(dgate, w_in_cols(top_ref, 0, p), _NT) + _dot(dup, w_in_cols(top_ref, 2, p), _NT)
            dh_bot = dh_bot + _dot(dgate, w_in_cols(bot_ref, 0, p), _NT) + _dot(dup, w_in_cols(bot_ref, 2, p), _NT)
        dh = jnp.concatenate([dh_top, dh_bot], axis=1)
        dx1n, dg2 = _rms_bwd(dh, x1v, r2, gpf_ref[...])
        dgpf_ref[...] += jnp.sum(dg2, axis=0, keepdims=True)
        dx1_ref[...] = dx2 + dx1n

    row = lambda w: pl.BlockSpec((tT, w), lambda i: (i, 0))
    vec = pl.BlockSpec((1, D_MODEL), lambda i: (0, 0))
    res, _ = _call(
        body, name="ffn_fwd_bwd", grid=(T // tT,),
        in_specs=[row(D_MODEL), row(D_MODEL), _whole(), _whole(), _whole(), vec, vec],
        out_specs=[row(D_MODEL), row(D_FF), row(2 * D_FF), row(D_MODEL), row(D_MODEL),
                   pl.BlockSpec((1, LANES), lambda i: (0, 0)), vec, vec],
        out_shape=[_sds((T, D_MODEL), BF16), _sds((T, D_FF), BF16), _sds((T, 2 * D_FF), BF16), _sds((T, D_MODEL), BF16),
                   _sds((T, D_MODEL), F32), _sds((1, LANES), F32), _sds((1, D_MODEL), F32), _sds((1, D_MODEL), F32)],
        scratch_shapes=[pltpu.VMEM((tT, 2 * D_FF), F32)], args=(x1, tgt, w_fi_top, w_fi_bot, w_fo, g_pf, g_po))
    return res


def _merge_sgu_bwd(dx1, mix, proj, zg, zs, w_bg, w_bs, w_o, g_pm, ln_g, ln_b, w_sp, b_sp_t, job=None):
    T = dx1.shape[0]
    tT = _row_tile(T, 256)
    nb = tT // SGU_BLOCK

    def body(dx1_ref, mix_ref, su_ref, sv_ref, gg_ref, gs_ref, zg_ref, zs_ref, wbg_ref, wbs_ref, wo_ref, g_ref,
             lg_ref, lb_ref, w_ref, b_ref,
             dmix_ref, dzg_ref, dzs_ref, dgate_ref, dyg_ref, dp_ref, dgpm_ref, dw_ref, dbt_ref, dlg_ref, dlb_ref):
        @pl.when(pl.program_id(0) == 0)
        def _():
            for ref in (dgpm_ref, dw_ref, dbt_ref, dlg_ref, dlb_ref):
                ref[...] = jnp.zeros_like(ref)

        mix = mix_ref[...].astype(F32)
        dmix, dg = _rms_bwd(dx1_ref[...], mix, _rms_stats(mix), g_ref[...])
        dgpm_ref[...] += jnp.sum(dg, axis=0, keepdims=True)
        dmb = dmix.astype(BF16)
        dmix_ref[...] = dmb
        dmerged = _dot(dmb, wo_ref[...], _NT)
        dys = None
        for k, (gate_ref, z_ref, w_br_ref, dz_ref) in enumerate(((gg_ref, zg_ref, wbg_ref, dzg_ref),
                                                                 (gs_ref, zs_ref, wbs_ref, dzs_ref))):
            sg = _sigmoid(gate_ref[...].astype(F32))
            dz = (dmerged * sg).astype(BF16)
            dz_ref[...] = dz
            dgate_ref[:, k * 1024:(k + 1) * 1024] = (dmerged * z_ref[...].astype(F32) * (sg * (1.0 - sg))).astype(BF16)
            dy_branch = _dot(dz, w_br_ref[...], _NT)
            if k == 0:
                dyg_ref[...] = dy_branch.astype(BF16)
            else:
                dys = dy_branch

        mask = _sgu_mask()
        lane = lax.broadcasted_iota(jnp.int32, (SGU_BLOCK, LANES), 1)
        for g in range(SGU_GROUPS):
            gc = slice(g * SGU_DG, (g + 1) * SGU_DG)
            gc_v = slice(1024 + g * SGU_DG, 1024 + (g + 1) * SGU_DG)
            wm = jnp.where(mask, w_ref[g], 0.0).astype(BF16)
            vf, dvf_dsv = _gelu_and_grad(sv_ref[:, gc].astype(F32))
            mu = jnp.mean(vf, axis=-1, keepdims=True)
            vc = vf - mu
            rstd = lax.rsqrt(jnp.mean(vc * vc, axis=-1, keepdims=True) + EPS)
            vhat = vc * rstd
            vn = (vhat * lg_ref[:, gc] + lb_ref[:, gc]).astype(BF16)
            u, du_dsu = _gelu_and_grad(su_ref[:, gc].astype(F32))
            dy = dys[:, gc]
            dmixed = (dy * u).astype(BF16)
            dvn_parts = []
            dw_acc = jnp.zeros((SGU_BLOCK, SGU_BLOCK), F32)
            db_acc = jnp.zeros((SGU_BLOCK, 1), F32)
            for b in range(nb):
                rows = slice(b * SGU_BLOCK, (b + 1) * SGU_BLOCK)
                mixed = _dot(wm, vn[rows, :]) + b_ref[:, g:g + 1]
                dp_ref[rows, gc] = (dy[rows, :] * mixed * du_dsu[rows, :]).astype(BF16)
                dvn_parts.append(_dot(wm, dmixed[rows, :], _TN))
                dw_acc = dw_acc + _dot(dmixed[rows, :], vn[rows, :], _NT)
                db_acc = db_acc + jnp.sum(dmixed[rows, :].astype(F32), axis=-1, keepdims=True)
            dw_ref[g] += jnp.where(mask, dw_acc, 0.0)
            dbt_ref[...] += jnp.where(lane == g, db_acc, 0.0)
            dvn = jnp.concatenate(dvn_parts, axis=0)
            dlg_ref[:, gc] += jnp.sum(dvn * vhat, axis=0, keepdims=True)
            dlb_ref[:, gc] += jnp.sum(dvn, axis=0, keepdims=True)
            dvh = dvn * lg_ref[:, gc]
            dvf = rstd * (dvh - jnp.mean(dvh, axis=-1, keepdims=True)
                          - vhat * jnp.mean(dvh * vhat, axis=-1, keepdims=True))
            dp_ref[:, gc_v] = (dvf * dvf_dsv).astype(BF16)

    row = pl.BlockSpec((tT, D_MODEL), lambda i: (i, 0))
    blk = lambda j: pl.BlockSpec((tT, 1024), lambda i: (i, j))
    vec = pl.BlockSpec((1, D_MODEL), lambda i: (0, 0))
    wide = lambda w: pl.BlockSpec((tT, w), lambda i: (i, 0))
    sds = _sds((T, D_MODEL), BF16)
    return _call(
        body, name="merge_sgu_bwd", grid=(T // tT,),
        in_specs=[row, row, blk(3), blk(4), blk(5), blk(6), row, row] + [_whole()] * 3 + [vec] + [_whole()] * 4,
        out_specs=[row, row, row, wide(W_MRG), row, wide(W_SGU), vec,
                   pl.BlockSpec((SGU_GROUPS, SGU_BLOCK, SGU_BLOCK), lambda i: (0, 0, 0)),
                   pl.BlockSpec((SGU_BLOCK, LANES), lambda i: (0, 0)), vec, vec],
        out_shape=[sds, sds, sds, _sds((T, W_MRG), BF16), sds, _sds((T, W_SGU), BF16), _sds((1, D_MODEL), F32),
                   _sds((SGU_GROUPS, SGU_BLOCK, SGU_BLOCK), F32), _sds((SGU_BLOCK, LANES), F32),
                   _sds((1, 1024), F32), _sds((1, 1024), F32)],
        args=(dx1, mix, proj, proj, proj, proj, zg, zs, w_bg, w_bs, w_o, g_pm, ln_g, ln_b, w_sp, b_sp_t), job=job)


def _gla_bwd(proj, alow, wgu, b_gate, gn, states, dy_gla, job=None):
    T = proj.shape[0]
    tT = _row_tile(T, 512)
    nc = tT // CHUNK
    nt = T // tT

    def body(q_ref, k_ref, v_ref, r_ref, al_ref, wgu_ref, bg_ref, gn_ref, later_ref, earlier_ref, st_ref, sp_ref, dy_ref,
             dp_ref, dal_ref, dgn_ref, dbg_ref, dwgu_ref, g_scr, dd_scr, dt_scr):
        step = pl.program_id(0)

        @pl.when(step == 0)
        def _():
            g_scr[...] = jnp.zeros_like(g_scr)
            dgn_ref[...] = jnp.zeros_like(dgn_ref)
            dbg_ref[...] = jnp.zeros_like(dbg_ref)
            dwgu_ref[...] = jnp.zeros_like(dwgu_ref)

        has_prev = jnp.where(step == nt - 1, 0.0, 1.0)
        logit, la, delta = _gla_decay_terms(al_ref, wgu_ref, bg_ref, later_ref)
        e = jnp.exp(delta)
        kdec_f = k_ref[...].astype(F32) * e
        kdec = kdec_f.astype(BF16)
        heads = range(GLA_HEADS)
        kcs = [slice(h * GLA_DK, (h + 1) * GLA_DK) for h in heads]
        vcs = [slice(h * GLA_DV, (h + 1) * GLA_DV) for h in heads]
        carry = [g_scr[h] for h in heads]
        dgn_acc = [jnp.zeros((1, GLA_DV), F32) for _ in heads]
        for c in reversed(range(nc)):
            rows = slice(c * CHUNK, (c + 1) * CHUNK)
            first = slice(c * CHUNK, c * CHUNK + 1)
            dec = jnp.exp(la[first, :] + delta[first, :])
            s_b = [st_ref[c, h].astype(BF16) for h in heads]
            qs = [(q_ref[rows, kcs[h]].astype(F32) * (GLA_DK ** -0.5)).astype(BF16) for h in heads]
            o = [_dot(qs[h], s_b[h], _NT) for h in heads]
            do = []
            for h in heads:
                rstd = _rms_stats(o[h])
                ohat = o[h] * rstd
                gnh = gn_ref[:, vcs[h]]
                dy = dy_ref[rows, vcs[h]].astype(F32)
                rr = r_ref[rows, vcs[h]].astype(F32)
                sg = _sigmoid(rr)
                don = dy * (rr * sg)
                dp_ref[rows, OFF_R + h * GLA_DV:OFF_R + (h + 1) * GLA_DV] = (
                    dy * (ohat * gnh) * (sg * (1.0 + rr * (1.0 - sg)))).astype(BF16)
                dgn_acc[h] = dgn_acc[h] + jnp.sum(don * ohat, axis=0, keepdims=True)
                dn = don * gnh
                do.append((rstd * (dn - ohat * jnp.mean(dn * ohat, axis=-1, keepdims=True))).astype(BF16))
            dq = [_dot(do[h], s_b[h]) for h in heads]
            g_t = [_dot(do[h], qs[h], _TN) + carry[h] for h in heads]
            g_b = [g_t[h].astype(BF16) for h in heads]
            dv = [_dot(kdec[rows, kcs[h]], g_b[h], _NT) for h in heads]
            dkdec = [_dot(v_ref[rows, vcs[h]], g_b[h]) for h in heads]
            for h in heads:
                s_prev = st_ref[c - 1, h] if c > 0 else sp_ref[0, h] * has_prev
                ddec = jnp.sum(g_t[h] * s_prev, axis=0, keepdims=True)
                carry[h] = g_t[h] * dec[:, kcs[h]]
                dp_ref[rows, OFF_Q + h * GLA_DK:OFF_Q + (h + 1) * GLA_DK] = (dq[h] * (GLA_DK ** -0.5)).astype(BF16)
                dp_ref[rows, OFF_V + h * GLA_DV:OFF_V + (h + 1) * GLA_DV] = dv[h].astype(BF16)
                dp_ref[rows, OFF_K + h * GLA_DK:OFF_K + (h + 1) * GLA_DK] = (dkdec[h] * e[rows, kcs[h]]).astype(BF16)
                dd_scr[rows, kcs[h]] = dkdec[h] * kdec_f[rows, kcs[h]]
                dt_scr[rows, kcs[h]] = jnp.broadcast_to(ddec * dec[:, kcs[h]], (CHUNK, GLA_DK))
        for h in heads:
            g_scr[h] = carry[h]
            dgn_ref[:, vcs[h]] += dgn_acc[h]
        dla = _dot_exact_lhs(earlier_ref[...], dd_scr[...]) + dt_scr[...]
        dlogit = dla * (1.0 / GLA_TAU) * _sigmoid(-logit)
        dbg_ref[...] += jnp.sum(dlogit, axis=0, keepdims=True)
        dwgu_ref[...] += _dot_bf16(al_ref[...], dlogit, _TN)
        dal_ref[...] = _dot_bf16(dlogit, wgu_ref[...], _NT).astype(BF16)

    rev = lambda i: nt - 1 - i
    blk = lambda w, j: pl.BlockSpec((tT, w), lambda i: (rev(i), j))
    st_blk = pl.BlockSpec((nc, GLA_HEADS, GLA_DV, GLA_DK), lambda i: (rev(i), 0, 0, 0))
    sp_blk = pl.BlockSpec((1, GLA_HEADS, GLA_DV, GLA_DK), lambda i: (jnp.maximum(rev(i) * nc - 1, 0), 0, 0, 0))
    return _call(
        body, name="gla_bwd", grid=(nt,),
        in_specs=[blk(512, 0), blk(512, 1), blk(1024, 1), blk(1024, 2), blk(LANES, 0)] + [_whole()] * 5
        + [st_blk, sp_blk, blk(GLA_V, 0)],
        out_specs=[blk(W_GLA, 0), blk(LANES, 0), pl.BlockSpec((1, GLA_V), lambda i: (0, 0)),
                   pl.BlockSpec((1, GLA_QK), lambda i: (0, 0)), pl.BlockSpec((LANES, GLA_QK), lambda i: (0, 0))],
        out_shape=[_sds((T, W_GLA), BF16), _sds((T, LANES), BF16), _sds((1, GLA_V), F32), _sds((1, GLA_QK), F32),
                   _sds((LANES, GLA_QK), F32)],
        scratch_shapes=[pltpu.VMEM((GLA_HEADS, GLA_DV, GLA_DK), F32), pltpu.VMEM((tT, GLA_QK), F32),
                        pltpu.VMEM((tT, GLA_QK), F32)],
        args=(proj, proj, proj, proj, alow, wgu, b_gate, gn, _chunk_masks(tT, upper=True), _chunk_masks(tT, upper=False),
              states, states, dy_gla), job=job)


def _inproj_bwd(x, dx1, g1, w_all, dparts, job=None):
    T = x.shape[0]
    tT = _row_tile(T, 512)
    offs = (0, W_GLA, W_GLA + W_SGU, N_MAIN)

    def body(x_ref, dx1_ref, g_ref, w_hbm, *rest):
        part_refs, (dx_ref, dg_ref, w_ref, w_sems) = rest[:len(offs)], rest[len(offs):]

        def compute(first):
            if first:
                copies = [pltpu.make_async_copy(w_hbm.at[:, pl.ds(off, p.shape[1])], w_ref.at[:, pl.ds(off, p.shape[1])],
                                                w_sems.at[k]) for k, (off, p) in enumerate(zip(offs, dparts))]
                for cp in copies:
                    cp.start()
            da = jnp.zeros((tT, D_MODEL), F32)
            for k, (off, p_ref) in enumerate(zip(offs, part_refs)):
                if first:
                    copies[k].wait()
                da = da + _dot(p_ref[...], w_ref[:, off:off + p_ref.shape[1]], _NT)
            xv = x_ref[...]
            dx, dg = _rms_bwd(da, xv, _rms_stats(xv), g_ref[...])
            dg_sum = jnp.sum(dg, axis=0, keepdims=True)
            dg_ref[...] = dg_sum if first else dg_ref[...] + dg_sum
            dx_ref[...] = dx1_ref[...] + dx

        first_step = pl.program_id(0) == 0
        pl.when(first_step)(lambda: compute(True))
        pl.when(jnp.logical_not(first_step))(lambda: compute(False))

    row = lambda w: pl.BlockSpec((tT, w), lambda i: (i, 0))
    vec = pl.BlockSpec((1, D_MODEL), lambda i: (0, 0))
    return _call(
        body, name="inproj_bwd", grid=(T // tT,),
        in_specs=[row(D_MODEL), row(D_MODEL), vec, pl.BlockSpec(memory_space=pl.ANY)] + [row(p.shape[1]) for p in dparts],
        out_specs=[row(D_MODEL), vec], out_shape=[_sds((T, D_MODEL), F32), _sds((1, D_MODEL), F32)],
        scratch_shapes=[pltpu.VMEM(w_all.shape, BF16), pltpu.SemaphoreType.DMA((len(offs),))],
        args=(x, dx1, g1, w_all, *dparts), job=job)


def _tn_matmul(a, b, name, job=None):
    T, M = a.shape
    N = b.shape[1]
    tk = _row_tile(T, 1024)
    tm = M if M <= 1024 else 1408
    tn = N // 2 if N > 2048 else N
    assert M % tm == 0 and N % tn == 0

    def body(a_ref, b_ref, o_ref):
        @pl.when(pl.program_id(2) == 0)
        def _():
            o_ref[...] = _dot(a_ref[...], b_ref[...], _TN)

        @pl.when(pl.program_id(2) > 0)
        def _():
            o_ref[...] += _dot(a_ref[...], b_ref[...], _TN)

    res, jres = _call(
        body, name=name, grid=(M // tm, N // tn, T // tk),
        in_specs=[pl.BlockSpec((tk, tm), lambda i, j, k: (k, i)), pl.BlockSpec((tk, tn), lambda i, j, k: (k, j))],
        out_specs=[pl.BlockSpec((tm, tn), lambda i, j, k: (i, j))], out_shape=[_sds((M, N), F32)], args=(a, b), job=job)
    return res[0], jres


def _pad_rows(a, rows=8):
    return jnp.pad(a, ((0, rows - a.shape[0]), (0, LANES - a.shape[1])))


def _halves_view(dw):
    r = dw.shape[0] // N_CHIPS
    return dw.reshape(N_CHIPS, 2, r // 2, dw.shape[1])


def kernel(x, norm_pre_mix, w_in, w_gate_up, b_gate, gla_norm, sgu_ln_g, sgu_ln_b, w_spatial, b_spatial, w_branch_gla, w_branch_sgu, w_out, norm_post_mix, norm_pre_ffn, w_ffn_in, w_ffn_out, norm_post_ffn, loss_target, m_norm_pre_mix, m_w_in, m_w_gate_up, m_b_gate, m_gla_norm, m_sgu_ln_g, m_sgu_ln_b, m_w_spatial, m_b_spatial, m_w_branch_gla, m_w_branch_sgu, m_w_out, m_norm_post_mix, m_norm_pre_ffn, m_w_ffn_in, m_w_ffn_out, m_norm_post_ffn, v_norm_pre_mix, v_w_in, v_w_gate_up, v_b_gate, v_gla_norm, v_sgu_ln_g, v_sgu_ln_b, v_w_spatial, v_b_spatial, v_w_branch_gla, v_w_branch_sgu, v_w_out, v_norm_post_mix, v_norm_pre_ffn, v_w_ffn_in, v_w_ffn_out, v_norm_post_ffn):
    chip = 2 * lax.axis_index("x") + lax.axis_index("y")
    xt, tgt = x[0], loss_target[0]

    tiny = jnp.concatenate([w_gate_up[0], _pad_rows(gla_norm[0]), _pad_rows(sgu_ln_g[0]), _pad_rows(sgu_ln_b[0]),
                            jnp.zeros((24, LANES), F32)], axis=0)

    def with_own(gathered, own):
        return lax.dynamic_update_slice(gathered, own[None], (chip, 0, 0))

    w_in_t, m_in_t, v_in_t = w_in[0].T, m_w_in[0].T, v_w_in[0].T
    w_in_b = _transposed_cast(w_in_t)
    (*own_rows, fi_top, fi_bot), (g_in, g_tiny) = _cast_weights(
        [w_branch_gla[0], w_branch_sgu[0], w_out[0], w_ffn_out[0]], w_ffn_in[0], job=_job_gather([w_in_b, tiny]))
    g_tiny = with_own(g_tiny, tiny)
    w_all = _relayout_w_in(with_own(g_in, w_in_b))
    cols = lambda a: a.transpose(1, 0, 2).reshape(a.shape[1], N_CHIPS * a.shape[2])
    wgu = jnp.pad(cols(g_tiny[:, 0:16]), ((0, LANES - GLA_RANK), (0, 0)))
    gn = cols(g_tiny[:, 16:20, :64]).reshape(1, GLA_V)
    ln_g = cols(g_tiny[:, 24:28, :64]).reshape(1, 1024)
    ln_b = cols(g_tiny[:, 32:36, :64]).reshape(1, 1024)
    b_sp_t = jnp.pad(b_spatial[0].T, ((0, 0), (0, LANES - SGU_GROUPS)))
    w_sp = w_spatial[0]

    (a, proj, alow), g_rows = _inproj_fwd(xt, norm_pre_mix, w_all, job=_job_gather(own_rows))
    rows = lambda g: g.reshape(N_CHIPS * g.shape[1], g.shape[2])
    w_bg, w_bs, w_o, w_fo = [rows(with_own(g, own)) for g, own in zip(g_rows, own_rows)]
    (y_gla, states), (g_top,) = _gla_fwd(proj, alow, wgu, b_gate, gn, job=_job_gather([fi_top]))
    (y_sgu, zg, zs, merged, mix, x1), (g_bot,) = _sgu_merge_fwd(
        xt, proj, y_gla, ln_g, ln_b, w_sp, b_sp_t, w_bg, w_bs, w_o, norm_post_mix, job=_job_gather([fi_bot]))
    h, f, dgu, dy, dx1, loss, d_gpf, d_gpo = _ffn_fwd_bwd(x1, tgt, with_own(g_top, fi_top), with_own(g_bot, fi_bot),
                                                          w_fo, norm_pre_ffn, norm_post_ffn)

    whole = lambda hs: [[(h_, None)] for h_ in hs]
    dw_fo, _ = _tn_matmul(f, dy, "dw_ffn_out")
    dw_fo4 = _halves_view(dw_fo)
    dw_fi, (q_fo,) = _tn_matmul(h, dgu, "dw_ffn_in", job=_job_to_other_core([[(dw_fo4, 0)]]))
    c_fo, = _presum([dw_fo4], [q_fo], "presum_ffn_out")
    (dmix, dzg, dzs, dp_mrg, dyg, dp_sgu, d_gpm, d_wsp, d_bsp_t, d_lng, d_lnb), (s_fo, q_fi) = _merge_sgu_bwd(
        dx1, mix, proj, zg, zs, w_bg, w_bs, w_o, norm_post_mix, ln_g, ln_b, w_sp, b_sp_t,
        job=_join(_job_scatter([c_fo]), _job_to_other_core([[(dw_fi, 0)]])))
    c_fi, = _presum([dw_fi], [q_fi], "presum_ffn_in")
    dw_c, _ = _tn_matmul(a, dp_mrg, "dw_in_merge")
    dw_b, _ = _tn_matmul(a, dp_sgu, "dw_in_sgu")
    dw_o4 = _halves_view(_tn_matmul(merged, dmix, "dw_out")[0])
    dw_bg4 = _halves_view(_tn_matmul(y_gla, dzg, "dw_branch_gla")[0])
    dw_bs4 = _halves_view(_tn_matmul(y_sgu, dzs, "dw_branch_sgu")[0])
    h_fo, = _sum_slots([c_fo], [s_fo], "sum_ffn_out")
    (dp_gla, dal, d_gn, d_bg, d_wgu), (s_fi, t_fo, q_b, q_c, q_o, q_bg, q_bs) = _gla_bwd(
        proj, alow, wgu, b_gate, gn, states, dyg,
        job=_join(_job_scatter([c_fi]), _job_to_other_core(
            whole([h_fo]) + [[(dw_b, 0)], [(dw_c, 0)], [(dw_o4, 0)], [(dw_bg4, 0)], [(dw_bs4, 0)]])))
    c_o, c_bg, c_bs = _presum([dw_o4, dw_bg4, dw_bs4], [q_o, q_bg, q_bs], "presum_out_branches")
    h_fi, = _sum_slots([c_fi], [s_fi], "sum_ffn_in")
    dw_d, _ = _tn_matmul(a, dal, "dw_in_gate")
    dw_a, (s_o, s_bg, s_bs, t_fi, q_d) = _tn_matmul(
        a, dp_gla, "dw_in_gla",
        job=_join(_job_scatter([c_o, c_bg, c_bs]), _job_to_other_core(whole([h_fi]) + [[(dw_d, 0)]])))
    h_o, h_bg, h_bs = _sum_slots([c_o, c_bg, c_bs], [s_o, s_bg, s_bs], "sum_out_branches")

    grads, deltas, new_m, new_v = {}, {}, {}, {}

    def update(name, w, m, v, g_mine, g_theirs, job=None):
        (g, d, m2, v2), jres = _adamw(w[0], m[0], v[0], g_mine, g_theirs, "adamw_" + name, job=job)
        grads[name], deltas[name], new_m[name], new_v[name] = g[None], d[None], m2[None], v2[None]
        return jres

    dw_in = [(dw_a, 0), (dw_b, W_GLA), (dw_c, W_GLA + W_SGU), (dw_d, N_MAIN)]
    q_a, t_o, t_bg, t_bs = update("w_ffn_out", w_ffn_out, m_w_ffn_out, v_w_ffn_out, [h_fo], [t_fo],
                                  job=_job_to_other_core([[(dw_a, 0)]] + whole([h_o, h_bg, h_bs])))
    q_in = [q_a, q_b, q_c, q_d]
    hr_in = D_MODEL // 2
    c_in_a, _ = _presum_w_in(dw_in, q_in, 0, hr_in // 8, "presum_w_in_a")
    c_in_b, (s_in_a,) = _presum_w_in(dw_in, q_in, hr_in // 8, 7 * hr_in // 8, "presum_w_in_b",
                                     job=_job_scatter([c_in_a]))
    update("w_ffn_in", w_ffn_in, m_w_ffn_in, v_w_ffn_in, [h_fi], [t_fi])
    update("w_out", w_out, m_w_out, v_w_out, [h_o], [t_o])
    update("w_branch_gla", w_branch_gla, m_w_branch_gla, v_w_branch_gla, [h_bg], [t_bg])
    update("w_branch_sgu", w_branch_sgu, m_w_branch_sgu, v_w_branch_sgu, [h_bs], [t_bs])
    (grad_x, d_g1), (s_in_b,) = _inproj_bwd(xt, dx1, norm_pre_mix, w_all, (dp_gla, dp_sgu, dp_mrg, dal),
                                            job=_job_scatter([c_in_b]))
    h_in = _sum_slots([c_in_a], [s_in_a], "sum_w_in_a") + _sum_slots([c_in_b], [s_in_b], "sum_w_in_b")
    t_in = _run_job(_job_to_other_core(whole(h_in)), "swap_w_in")
    for store, val in zip((grads, deltas, new_m, new_v),
                          _adamw_transposed(w_in_t, m_in_t, v_in_t, h_in, t_in, "adamw_w_in")):
        store["w_in"] = val.T[None]

    small_names = ["w_spatial", "w_gate_up", "norm_pre_mix", "norm_post_mix", "norm_pre_ffn", "norm_post_ffn", "b_gate",
                   "b_spatial", "gla_norm", "sgu_ln_g", "sgu_ln_b"]
    loss_out, small = _small_adamw(
        _small_sum([d_wsp, d_wgu, d_g1, d_gpm, d_gpf, d_gpo, d_bg, d_bsp_t, d_gn, d_lng, d_lnb, loss]),
        [w_spatial, w_gate_up, norm_pre_mix, norm_post_mix, norm_pre_ffn, norm_post_ffn, b_gate, b_spatial, gla_norm,
         sgu_ln_g, sgu_ln_b],
        [m_w_spatial, m_w_gate_up, m_norm_pre_mix, m_norm_post_mix, m_norm_pre_ffn, m_norm_post_ffn, m_b_gate,
         m_b_spatial, m_gla_norm, m_sgu_ln_g, m_sgu_ln_b],
        [v_w_spatial, v_w_gate_up, v_norm_pre_mix, v_norm_post_mix, v_norm_pre_ffn, v_norm_post_ffn, v_b_gate,
         v_b_spatial, v_gla_norm, v_sgu_ln_g, v_sgu_ln_b])
    for store, vals in zip((grads, deltas, new_m, new_v), small):
        store.update(zip(small_names, vals))

    order = ["norm_pre_mix", "w_in", "w_gate_up", "b_gate", "gla_norm", "sgu_ln_g", "sgu_ln_b", "w_spatial", "b_spatial",
             "w_branch_gla", "w_branch_sgu", "w_out", "norm_post_mix", "norm_pre_ffn", "w_ffn_in", "w_ffn_out",
             "norm_post_ffn"]
    out = [loss_out, grad_x[None]]
    for store in (grads, deltas, new_m, new_v):
        out.extend(store[n] for n in order)
    return tuple(out)
```

```python
import jax
import jax.numpy as jnp
from jax import lax
from jax.experimental import pallas as pl
from jax.experimental.pallas import tpu as pltpu

F32 = jnp.float32
BF16 = jnp.bfloat16

D_MODEL = 1024
GLA_HEADS = 4
GLA_DK = 128
GLA_DV = 256
GLA_QK = GLA_HEADS * GLA_DK
GLA_V = GLA_HEADS * GLA_DV
GLA_RANK = 16
GLA_TAU = 16.0
CHUNK = 64
SGU_GROUPS = 4
SGU_BLOCK = 128
SGU_DG = 256
D_FF = 2816
EPS = 1e-6
LANES = 128

OFF_Q, OFF_K, OFF_V, OFF_R, OFF_SU, OFF_SV, OFF_GG, OFF_GS, OFF_AL = 0, 512, 1024, 2048, 3072, 4096, 5120, 6144, 7168
W_GLA, W_SGU, W_MRG = 3072, 2048, 2048
N_MAIN = 7168
N_ALL = N_MAIN + LANES
_IN_SPLITS = (GLA_QK, GLA_QK, GLA_V, GLA_V, GLA_RANK, 1024, 1024, 1024, 1024)
_IN_STARTS = tuple(sum(_IN_SPLITS[:i]) for i in range(len(_IN_SPLITS) + 1))
_IN_DST = (OFF_Q, OFF_K, OFF_V, OFF_R, OFF_AL, OFF_SU, OFF_SV, OFF_GG, OFF_GS)
D_IN = _IN_STARTS[-1]

ADAM_LR = 0.001
ADAM_B1 = 0.9
ADAM_B2 = 0.999
ADAM_EPS = 1e-08
ADAM_WD = 0.01
ADAM_STEP = 10

VMEM_LIMIT_BYTES = 56 * 1024 * 1024
N_CHIPS = 4
N_PEER = N_CHIPS - 1
N_DEV = 8
MESH = pl.DeviceIdType.MESH

_NN = (((1,), (0,)), ((), ()))
_NT = (((1,), (1,)), ((), ()))
_TN = (((0,), (0,)), ((), ()))


def _dot(a, b, dims=_NN):
    return lax.dot_general(a, b, dims, preferred_element_type=F32)


def _split(x):
    hi = x.astype(BF16)
    lo = (x - hi.astype(F32)).astype(BF16)
    return hi, lo


def _dot_bf16(a, b, dims=_NN):
    return _dot(a.astype(BF16), b.astype(BF16), dims)


def _dot_exact_lhs(m, x):
    xh, xl = _split(x)
    return _dot(m, xh) + _dot(m, xl)


def _sigmoid(x):
    return 0.5 * jnp.tanh(0.5 * x) + 0.5


def _log_sigmoid(x):
    return jnp.minimum(x, 0.0) - jnp.log(1.0 + jnp.exp(-jnp.abs(x)))


_GELU_C = 0.7978845608028654
_GELU_A = 0.044715


def _gelu_and_grad(x):
    x2 = x * x
    t = jnp.tanh(_GELU_C * (x + _GELU_A * x * x2))
    g = 0.5 * x * (1.0 + t)
    dg = 0.5 * (1.0 + t) + 0.5 * x * (1.0 - t * t) * (_GELU_C * (1.0 + 3.0 * _GELU_A * x2))
    return g, dg


def _gelu(x):
    t = jnp.tanh(_GELU_C * (x + _GELU_A * x * x * x))
    return 0.5 * x * (1.0 + t)


def _rms_stats(x):
    return lax.rsqrt(jnp.mean(x * x, axis=-1, keepdims=True) + EPS)


def _rms_bwd(dout, y, r, g):
    yhat = y * r
    dn = dout * g
    dy = r * (dn - yhat * jnp.mean(dn * yhat, axis=-1, keepdims=True))
    return dy, dout * yhat


def _whole():
    return pl.BlockSpec(memory_space=pltpu.VMEM)


def _row_tile(T, want):
    t = min(T, want)
    assert T % t == 0
    return t


def _chunk_masks(tT, upper):
    row = lax.broadcasted_iota(jnp.int32, (tT, tT), 0)
    col = lax.broadcasted_iota(jnp.int32, (tT, tT), 1)
    same = (row // CHUNK) == (col // CHUNK)
    tri = (col > row) if upper else (col < row)
    return jnp.where(same & tri, 1.0, 0.0).astype(BF16)


class _Job:
    def __init__(self, ins, out_shapes, scratch, start, finish, mid=None):
        self.ins, self.out_shapes, self.scratch = list(ins), list(out_shapes), list(scratch)
        self.start, self.finish, self.mid = start, finish, mid


def _join(*jobs):
    def split(refs, counts):
        out, at = [], 0
        for n in counts:
            out.append(refs[at:at + n])
            at += n
        return out

    ni, no, ns = [len(j.ins) for j in jobs], [len(j.out_shapes) for j in jobs], [len(j.scratch) for j in jobs]

    def start(ins, outs, scr):
        for j, a, b, c in zip(jobs, split(ins, ni), split(outs, no), split(scr, ns)):
            j.start(a, b, c)

    def finish(ins, outs, scr):
        for j, a, b, c in zip(jobs, split(ins, ni), split(outs, no), split(scr, ns)):
            j.finish(a, b, c)

    def mid(ins, outs, scr):
        for j, a, b, c in zip(jobs, split(ins, ni), split(outs, no), split(scr, ns)):
            if j.mid is not None:
                j.mid(a, b, c)

    return _Job(sum((j.ins for j in jobs), []), sum((j.out_shapes for j in jobs), []),
                sum((j.scratch for j in jobs), []), start, finish, mid if any(j.mid for j in jobs) else None)


def _mesh_pos():
    return lax.axis_index("x"), lax.axis_index("y"), lax.axis_index("c")


def _peer_chips(xi, yi):
    return [(1 - xi, yi), (xi, 1 - yi), (1 - xi, 1 - yi)]


def _half(ci, rows):
    return pl.ds(pl.multiple_of(ci * rows, 8), rows)


def _sds(shape, dtype):
    return jax.ShapeDtypeStruct(tuple(shape), dtype)


def _job_gather(arrs):
    n = len(arrs)
    kinds = 12
    Y0, Y1, X1, X0, ON_X, ON_Y, D2D = 0, 1, 2, 3, 4, 5, 6

    def copies(ins, outs, scr):
        send_sems, recv_sems = scr
        xi, yi, ci = _mesh_pos()
        me, cx, cy, cd = 2 * xi + yi, 2 * (1 - xi) + yi, 2 * xi + (1 - yi), 2 * (1 - xi) + (1 - yi)
        to_x, to_y, to_core = (1 - xi, yi, ci), (xi, 1 - yi, ci), (xi, yi, 1 - ci)
        table = []
        for k in range(n):
            qr = arrs[k].shape[0] // 4

            def rows(core, q):
                return pl.ds(pl.multiple_of((2 * core + q) * qr, 8), qr)

            def cp(kind, src, dst, to):
                s = k * kinds + kind
                return pltpu.make_async_remote_copy(src_ref=src, dst_ref=dst, send_sem=send_sems.at[s],
                                                    recv_sem=recv_sems.at[s], device_id=to, device_id_type=MESH)

            def slab(chip, core, q):
                return outs[k].at[chip, rows(core, q)]

            t = {}
            for kind, q, to, frm in ((Y0, 0, to_y, cy), (Y1, 1, to_y, cy), (X1, 1, to_x, cx), (X0, 0, to_x, cx)):
                mine = ins[k].at[rows(ci, q)]
                t[kind] = (cp(kind, mine, slab(me, ci, q), to), cp(kind, mine, slab(frm, ci, q), to))
            t[ON_X] = (cp(ON_X, slab(cy, ci, 0), slab(cy, ci, 0), to_x), cp(ON_X, slab(cy, ci, 0), slab(cd, ci, 0), to_x))
            t[ON_Y] = (cp(ON_Y, slab(cx, ci, 1), slab(cx, ci, 1), to_y), cp(ON_Y, slab(cx, ci, 1), slab(cd, ci, 1), to_y))
            for i, (chip, q) in enumerate(((cy, 0), (cy, 1), (cx, 1), (cx, 0), (cd, 0), (cd, 1))):
                t[D2D + i] = (cp(D2D + i, slab(chip, ci, q), slab(chip, ci, q), to_core),
                              cp(D2D + i, slab(chip, ci, q), slab(chip, 1 - ci, q), to_core))
            table.append(t)
        return table

    def start(ins, outs, scr):
        table = copies(ins, outs, scr)
        for kind in (Y0, X1, Y1, X0):
            for t in table:
                t[kind][0].start()

    def arrived(table, kind, then):
        for t in table:
            t[kind][1].wait_recv()
            for nxt in then:
                t[nxt][0].start()

    def mid(ins, outs, scr):
        table = copies(ins, outs, scr)
        arrived(table, Y0, (ON_X, D2D + 0))
        arrived(table, X1, (ON_Y, D2D + 2))

    def finish(ins, outs, scr):
        table = copies(ins, outs, scr)
        arrived(table, Y1, (D2D + 1,))
        arrived(table, X0, (D2D + 3,))
        arrived(table, ON_X, (D2D + 4,))
        arrived(table, ON_Y, (D2D + 5,))
        for t in table:
            for i in range(6):
                t[D2D + i][1].wait_recv()
            for kind in range(kinds):
                t[kind][0].wait_send()

    dma = pltpu.SemaphoreType.DMA
    return _Job(arrs, [_sds((N_CHIPS,) + a.shape, a.dtype) for a in arrs], [dma((n * kinds,))] * 2, start, finish, mid)


def _job_scatter(parts):
    n = len(parts)

    def copies(ins, outs, scr):
        send_sems, recv_sems = scr
        xi, yi, ci = _mesh_pos()
        res = []
        for k in range(n):
            for j, (px, py) in enumerate(_peer_chips(xi, yi)):
                s = k * N_PEER + j
                res.append(pltpu.make_async_remote_copy(
                    src_ref=ins[k].at[2 * px + py], dst_ref=outs[k].at[j], send_sem=send_sems.at[s],
                    recv_sem=recv_sems.at[s], device_id=(px, py, ci), device_id_type=MESH))
        return res

    def start(ins, outs, scr):
        for cp in copies(ins, outs, scr):
            cp.start()

    def finish(ins, outs, scr):
        for cp in copies(ins, outs, scr):
            cp.wait_recv()
            cp.wait_send()

    dma = pltpu.SemaphoreType.DMA
    return _Job(parts, [_sds((N_PEER,) + p.shape[1:], p.dtype) for p in parts], [dma((n * N_PEER,))] * 2, start, finish)


def _job_to_other_core(groups):
    pieces = [(g, a, off) for g, group in enumerate(groups) for a, off in group]
    n = len(pieces)

    def geometry(group):
        a0, off0 = group[0]
        if off0 is None:
            return a0.shape
        if a0.ndim == 4:
            return (N_CHIPS, a0.shape[2], a0.shape[3])
        return (a0.shape[0] // 2, sum(a.shape[1] for a, _ in group))

    def copies(ins, outs, scr):
        send_sems, recv_sems = scr
        xi, yi, ci = _mesh_pos()
        res = []
        for p, (g, a, off) in enumerate(pieces):
            if off is None:
                give, land = ins[p], outs[g]
            elif a.ndim == 4:
                give, land = ins[p].at[pl.ds(0, N_CHIPS), 1 - ci], outs[g]
            else:
                hr, w = a.shape[0] // 2, a.shape[1]
                give, land = ins[p].at[_half(1 - ci, hr)], outs[g].at[pl.ds(0, hr), pl.ds(off, w)]
            res.append(pltpu.make_async_remote_copy(
                src_ref=give, dst_ref=land, send_sem=send_sems.at[p], recv_sem=recv_sems.at[p],
                device_id=(xi, yi, 1 - ci), device_id_type=MESH))
        return res

    def start(ins, outs, scr):
        for cp in copies(ins, outs, scr):
            cp.start()

    def finish(ins, outs, scr):
        for cp in copies(ins, outs, scr):
            cp.wait_recv()
            cp.wait_send()

    dma = pltpu.SemaphoreType.DMA
    return _Job([a for _, a, _ in pieces], [_sds(geometry(group), group[0][0].dtype) for group in groups],
                [dma((n,))] * 2, start, finish)


def _call(body, *, name, grid, in_specs, out_specs, out_shape, args, scratch_shapes=(), parallel=False, job=None,
          by_core=False, by_chip=False):
    n_in, n_out, n_scr = len(in_specs), len(out_specs), len(scratch_shapes)
    hbm = pl.BlockSpec(memory_space=pl.ANY)
    n_ji, n_jo = (len(job.ins), len(job.out_shapes)) if job is not None else (0, 0)
    assert not (by_core and by_chip)
    lead = 1 if by_core or by_chip else 0

    def kernel_fn(*refs):
        core, refs = refs[:lead], refs[lead:]
        ins, refs = refs[:n_in], refs[n_in:]
        j_ins, refs = refs[:n_ji], refs[n_ji:]
        outs, refs = refs[:n_out], refs[n_out:]
        j_outs, refs = refs[:n_jo], refs[n_jo:]
        scr, j_scr = refs[:n_scr], refs[n_scr:]
        if job is None:
            body(*core, *ins, *outs, *scr)
            return
        ids = [pl.program_id(d) for d in range(len(grid))]
        first = ids[0] == 0
        last = ids[0] == grid[0] - 1
        for d in range(1, len(grid)):
            first = first & (ids[d] == 0)
            last = last & (ids[d] == grid[d] - 1)

        @pl.when(first)
        def _():
            job.start(j_ins, j_outs, j_scr)

        if job.mid is not None and grid[0] >= 4:
            half_way = ids[0] == grid[0] // 2
            for d in range(1, len(grid)):
                half_way = half_way & (ids[d] == 0)

            @pl.when(half_way)
            def _():
                job.mid(j_ins, j_outs, j_scr)

        body(*core, *ins, *outs, *scr)

        @pl.when(last)
        def _():
            if job.mid is not None and grid[0] < 4:
                job.mid(j_ins, j_outs, j_scr)
            job.finish(j_ins, j_outs, j_scr)

    sem = ("parallel" if parallel and job is None else "arbitrary",) * len(grid)
    all_in = list(in_specs) + [hbm] * n_ji
    all_out = list(out_specs) + [hbm] * n_jo
    all_scratch = list(scratch_shapes) + (job.scratch if job is not None else [])
    all_shapes = list(out_shape) + (job.out_shapes if job is not None else [])
    all_args = list(args) + (job.ins if job is not None else [])
    params = pltpu.CompilerParams(dimension_semantics=sem, vmem_limit_bytes=VMEM_LIMIT_BYTES)
    if lead:
        spec = pltpu.PrefetchScalarGridSpec(num_scalar_prefetch=1, grid=grid, in_specs=all_in, out_specs=all_out,
                                            scratch_shapes=all_scratch)
        index = lax.axis_index("c") if by_core else 2 * lax.axis_index("x") + lax.axis_index("y")
        res = pl.pallas_call(kernel_fn, name=name, grid_spec=spec, out_shape=all_shapes, compiler_params=params)(
            index.astype(jnp.int32).reshape(1), *all_args)
    else:
        res = pl.pallas_call(kernel_fn, name=name, grid=grid, in_specs=all_in, out_specs=all_out, out_shape=all_shapes,
                             scratch_shapes=all_scratch, compiler_params=params)(*all_args)
    return list(res[:n_out]), list(res[n_out:])


def _run_job(job, name):
    n_i, n_o = len(job.ins), len(job.out_shapes)

    def body(*refs):
        ins, outs, scr = refs[:n_i], refs[n_i:n_i + n_o], refs[n_i + n_o:]
        job.start(ins, outs, scr)
        if job.mid is not None:
            job.mid(ins, outs, scr)
        job.finish(ins, outs, scr)

    hbm = pl.BlockSpec(memory_space=pl.ANY)
    return list(pl.pallas_call(body, name=name, in_specs=[hbm] * n_i, out_specs=[hbm] * n_o, out_shape=job.out_shapes,
                               scratch_shapes=job.scratch)(*job.ins))


def _adam_values(w, m, v, g):
    m2 = ADAM_B1 * m + (1.0 - ADAM_B1) * g
    v2 = ADAM_B2 * v + (1.0 - ADAM_B2) * (g * g)
    delta = -ADAM_LR * ((m2 / (1.0 - ADAM_B1 ** ADAM_STEP)) / (jnp.sqrt(v2 / (1.0 - ADAM_B2 ** ADAM_STEP)) + ADAM_EPS)
                        + ADAM_WD * w)
    return delta, m2, v2


_P_WSP, _P_WGU, _P_NORM, _P_BG, _P_BSP, _P_HEAD, _P_LOSS, _P_ROWS = 0, 512, 576, 608, 616, 624, 720, 736


def _small_sum(dgrads):
    hr = _P_ROWS // 2

    def body(dwsp, dwgu, dg1, dgpm, dgpf, dgpo, dbg, dbspt, dgn, dlng, dlnb, loss_in, tot, pack, pair, slots, send_sems,
             recv_sems):
        xi, yi, ci = _mesh_pos()
        chip = 2 * xi + yi

        pack[...] = jnp.zeros_like(pack)
        for g in range(SGU_GROUPS):
            pack[_P_WSP + g * SGU_BLOCK:_P_WSP + (g + 1) * SGU_BLOCK] = dwsp[g]
        for j in range(N_CHIPS):
            pack[_P_WGU + GLA_RANK * j:_P_WGU + GLA_RANK * (j + 1)] = dwgu[0:GLA_RANK, LANES * j:LANES * (j + 1)]
        for k, r in enumerate((dg1, dgpm, dgpf, dgpo)):
            for q in range(8):
                pack[_P_NORM + 8 * k + q:_P_NORM + 8 * k + q + 1] = r[:, LANES * q:LANES * (q + 1)]
        for q in range(4):
            pack[_P_BG + q:_P_BG + q + 1] = dbg[:, LANES * q:LANES * (q + 1)]
        pack[_P_BSP:_P_BSP + SGU_GROUPS] = jnp.transpose(dbspt[...])[0:SGU_GROUPS]
        for k, r in enumerate((dgn, dlng, dlnb)):
            for j in range(N_CHIPS):
                for hh in range(4):
                    row = _P_HEAD + 32 * k + 8 * j + hh
                    pack[row:row + 1, 0:64] = r[:, 256 * hh + 64 * j:256 * hh + 64 * (j + 1)]
        pack[_P_LOSS:_P_LOSS + 1] = loss_in[...]

        sibling = dict(device_id=(xi, yi, 1 - ci), device_id_type=MESH)
        to_sibling = pltpu.make_async_remote_copy(src_ref=pack, dst_ref=pair, send_sem=send_sems.at[N_PEER],
                                                  recv_sem=recv_sems.at[N_PEER], **sibling)
        to_sibling.start()
        to_sibling.wait_recv()
        to_sibling.wait_send()
        pack[...] = pack[...] + pair[...]
        mine = pl.ds(pl.multiple_of(ci * hr, 8), hr)
        theirs = pl.ds(pl.multiple_of((1 - ci) * hr, 8), hr)
        slots[chip] = pack[mine, :]

        def copy(j, slot):
            px, py = _peer_chips(xi, yi)[j]
            return pltpu.make_async_remote_copy(
                src_ref=pack.at[mine], dst_ref=slots.at[slot(2 * px + py)], send_sem=send_sems.at[j],
                recv_sem=recv_sems.at[j], device_id=(px, py, ci), device_id_type=MESH)

        sends = [copy(j, lambda peer_chip: chip) for j in range(N_PEER)]
        for cp in sends:
            cp.start()
        for j in range(N_PEER):
            copy(j, lambda peer_chip: peer_chip).wait_recv()
        for cp in sends:
            cp.wait_send()
        acc = slots[0]
        for d in range(1, N_CHIPS):
            acc = acc + slots[d]
        tot[mine, :] = acc
        half_over = pltpu.make_async_remote_copy(src_ref=tot.at[mine], dst_ref=tot.at[mine], send_sem=send_sems.at[N_PEER + 1],
                                                 recv_sem=recv_sems.at[N_PEER + 1], **sibling)
        half_back = pltpu.make_async_remote_copy(src_ref=tot.at[mine], dst_ref=tot.at[theirs], send_sem=send_sems.at[N_PEER + 1],
                                                 recv_sem=recv_sems.at[N_PEER + 1], **sibling)
        half_over.start()
        half_back.wait_recv()
        half_over.wait_send()

    return pl.pallas_call(
        body, name="small_sum", in_specs=[_whole()] * 12, out_specs=_whole(), out_shape=_sds((_P_ROWS, LANES), F32),
        scratch_shapes=[pltpu.VMEM((_P_ROWS, LANES), F32), pltpu.VMEM((_P_ROWS, LANES), F32),
                        pltpu.VMEM((N_CHIPS, hr, LANES), F32),
                        pltpu.SemaphoreType.DMA((N_PEER + 2,)), pltpu.SemaphoreType.DMA((N_PEER + 2,))],
        compiler_params=pltpu.CompilerParams(vmem_limit_bytes=VMEM_LIMIT_BYTES),
    )(*dgrads)


def _small_adamw(tot, ws, ms, vs):
    n = len(ws)

    def body(*refs):
        tot = refs[0]
        w_refs, m_refs, v_refs = refs[1:1 + n], refs[1 + n:1 + 2 * n], refs[1 + 2 * n:1 + 3 * n]
        loss_out = refs[1 + 3 * n]
        outs = refs[2 + 3 * n:]
        chip = 2 * lax.axis_index("x") + lax.axis_index("y")
        loss_out[...] = tot[_P_LOSS:_P_LOSS + 1, 0:1]

        def step(k, g, pick, put):
            d, m2, v2 = _adam_values(pick(w_refs[k]), pick(m_refs[k]), pick(v_refs[k]), g)
            for o, val in zip((outs[k], outs[n + k], outs[2 * n + k], outs[3 * n + k]), (g, d, m2, v2)):
                put(o, val)

        def whole(ref):
            return ref[0]

        def put_whole(ref, val):
            ref[0] = val

        for g in range(SGU_GROUPS):
            def pick_g(ref, g=g):
                return ref[0, g]

            def put_g(ref, val, g=g):
                ref[0, g] = val

            step(0, tot[_P_WSP + g * SGU_BLOCK:_P_WSP + (g + 1) * SGU_BLOCK], pick_g, put_g)
        step(1, tot[pl.ds(pl.multiple_of(_P_WGU + GLA_RANK * chip, GLA_RANK), GLA_RANK), :], whole, put_whole)
        for k, (base, chunks) in enumerate(((_P_NORM, 8), (_P_NORM + 8, 8), (_P_NORM + 16, 8), (_P_NORM + 24, 8), (_P_BG, 4))):
            for q in range(chunks):
                def pick_q(ref, q=q):
                    return ref[:, LANES * q:LANES * (q + 1)]

                def put_q(ref, val, q=q):
                    ref[:, LANES * q:LANES * (q + 1)] = val

                step(2 + k, tot[base + q:base + q + 1], pick_q, put_q)
        step(7, tot[_P_BSP:_P_BSP + SGU_GROUPS], whole, put_whole)
        for k in range(3):
            mine = tot[pl.ds(pl.multiple_of(_P_HEAD + 32 * k + 8 * chip, 8), 8), :]
            step(8 + k, mine[0:4, 0:64], whole, put_whole)

    shapes = [_sds(w.shape, F32) for w in ws]
    res = pl.pallas_call(
        body, name="small_adamw", in_specs=[_whole()] * (1 + 3 * n), out_specs=[_whole()] * (1 + 4 * n),
        out_shape=[_sds((1, 1), F32)] + shapes * 4,
        compiler_params=pltpu.CompilerParams(vmem_limit_bytes=VMEM_LIMIT_BYTES),
    )(tot, *ws, *ms, *vs)
    return res[0].reshape(()), [list(res[1 + i * n:1 + (i + 1) * n]) for i in range(4)]


def _w_in_pieces():
    blk = D_IN // N_CHIPS
    pieces = []
    for s in range(len(_IN_SPLITS)):
        lo_s, hi_s = _IN_STARTS[s], _IN_STARTS[s + 1]
        for j in range(N_CHIPS):
            lo, hi = max(lo_s, j * blk), min(hi_s, (j + 1) * blk)
            if lo < hi:
                pieces.append((j, lo - j * blk, _IN_DST[s] + lo - lo_s, hi - lo))
    return pieces


def _relayout_w_in(gathered):
    _, rows, blk = gathered.shape
    tr = 256

    def body(g_ref, o_ref):
        o_ref[:, OFF_AL:N_ALL] = jnp.zeros((tr, LANES), BF16)
        for j, src, dst, w in _w_in_pieces():
            o_ref[:, dst:dst + w] = g_ref[j, :, src:src + w]

    res, _ = _call(body, name="relayout_w_in", grid=(rows // tr,), parallel=True,
                   in_specs=[pl.BlockSpec((N_CHIPS, tr, blk), lambda i: (0, i, 0))],
                   out_specs=[pl.BlockSpec((tr, N_ALL), lambda i: (i, 0))],
                   out_shape=[_sds((rows, N_ALL), BF16)], args=(gathered,))
    return res[0]


def _update_row_tile(rows):
    for t in range(min(rows, 256), 7, -8):
        if rows % t == 0:
            return t
    return rows


def _presum_w_in(dws, theirs, row0, rows, name, job=None):
    hr = theirs[0].shape[0]
    blk = D_IN // N_CHIPS
    tr = 64
    assert row0 % tr == 0 and rows % tr == 0
    nh, t0 = hr // tr, row0 // tr
    n = len(dws)

    def body(core_ref, *refs):
        dw_refs, q_refs, (o_ref, s_scr) = refs[:n], refs[n:2 * n], refs[2 * n:]
        for p, (a, off) in enumerate(dws):
            w = a.shape[1]
            s_scr[:, off:off + w] = (dw_refs[p][...] + q_refs[p][...]).astype(BF16)
        for j, src, dst, w in _w_in_pieces():
            o_ref[j, :, src:src + w] = s_scr[:, dst:dst + w]

    in_specs = [pl.BlockSpec((tr, a.shape[1]), lambda i, core: (i + t0 + core[0] * nh, 0)) for a, _ in dws]
    in_specs += [pl.BlockSpec((tr, q.shape[1]), lambda i, core: (i + t0, 0)) for q in theirs]
    res, jres = _call(body, name=name, grid=(rows // tr,), parallel=True, in_specs=in_specs,
                      out_specs=[pl.BlockSpec((N_CHIPS, tr, blk), lambda i, core: (0, i, 0))],
                      out_shape=[_sds((N_CHIPS, rows, blk), BF16)], scratch_shapes=[pltpu.VMEM((tr, N_ALL), BF16)],
                      args=(*[a for a, _ in dws], *theirs), job=job, by_core=True)
    return res[0], jres


def _presum(dws, theirs, name):
    dw, n = dws[0], len(dws)
    assert all(d.shape == dw.shape for d in dws)
    if dw.ndim == 4:
        _, _, hr, c = dw.shape
        tr = _update_row_tile(hr)
        mine = pl.BlockSpec((1, 1, tr, c), lambda j, i, core: (j, core[0], i, 0))
        other = pl.BlockSpec((1, tr, c), lambda j, i, core: (j, i, 0))
    else:
        hr, c = dw.shape[0] // 2, dw.shape[1] // N_CHIPS
        tr = _update_row_tile(hr)
        nh = hr // tr
        mine = pl.BlockSpec((tr, c), lambda j, i, core: (i + core[0] * nh, j))
        other = pl.BlockSpec((tr, c), lambda j, i, core: (i, j))

    def body(core_ref, *refs):
        for a_ref, q_ref, o_ref in zip(refs[:n], refs[n:2 * n], refs[2 * n:]):
            o_ref[...] = (a_ref[...].reshape(tr, c) + q_ref[...].reshape(tr, c)).astype(BF16).reshape(o_ref.shape)

    res, _ = _call(body, name=name, grid=(N_CHIPS, hr // tr), parallel=True, in_specs=[mine] * n + [other] * n,
                   out_specs=[pl.BlockSpec((1, tr, c), lambda j, i, core: (j, i, 0))] * n,
                   out_shape=[_sds((N_CHIPS, hr, c), BF16)] * n, args=(*dws, *theirs), by_core=True)
    return res


def _sum_slots(sums, slots, name, job=None):
    n = len(sums)
    _, rows, cols = sums[0].shape
    assert all(s.shape == sums[0].shape for s in sums)
    tr = _update_row_tile(rows)

    def body(chip_ref, *refs):
        for own_ref, s_ref, o_ref in zip(refs[:n], refs[n:2 * n], refs[2 * n:]):
            acc = own_ref[...].astype(F32)
            for j in range(N_PEER):
                acc = acc + s_ref[j].astype(F32)
            o_ref[...] = acc

    return _call(body, name=name, grid=(rows // tr,), parallel=True, by_chip=True,
                 in_specs=[pl.BlockSpec((None, tr, cols), lambda i, chip: (chip[0], i, 0))] * n
                 + [pl.BlockSpec((N_PEER, tr, cols), lambda i, chip: (0, i, 0))] * n,
                 out_specs=[pl.BlockSpec((tr, cols), lambda i, chip: (i, 0))] * n,
                 out_shape=[_sds((rows, cols), F32)] * n, args=(*sums, *slots), job=job)


def _adamw(w, m, v, g_mine, g_theirs, name, job=None):
    rows, cols = w.shape
    part_rows = [p.shape[0] for p in g_mine]
    assert sum(part_rows) == rows // 2 and [p.shape[0] for p in g_theirs] == part_rows
    tr = _update_row_tile(min(part_rows))
    assert all(r % tr == 0 for r in part_rows)
    nh = (rows // 2) // tr
    starts = [sum(part_rows[:k]) // tr for k in range(len(part_rows))]
    n_parts = len(part_rows)

    def body(core_ref, w_ref, m_ref, v_ref, *rest):
        g_refs, (g_out, d_out, m_out, v_out) = rest[:-4], rest[-4:]
        step = pl.program_id(0)
        mine_here = (step // nh) == core_ref[0]
        q = step % nh
        g = None
        for k in reversed(range(n_parts)):
            val = jnp.where(mine_here, g_refs[k][...], g_refs[n_parts + k][...])
            g = val if g is None else jnp.where(q < starts[k + 1], val, g)
        d, m2, v2 = _adam_values(w_ref[...], m_ref[...], v_ref[...], g)
        g_out[...] = g
        m_out[...] = m2
        v_out[...] = v2
        d_out[...] = d

    def g_spec(k, mine):
        last = part_rows[k] // tr - 1

        def index(i, core):
            half = core[0] if mine else 1 - core[0]
            here = jnp.clip(i % nh - starts[k], 0, last)
            return (jnp.where(i // nh == half, here, jnp.where(i // nh > half, last, 0)), 0)

        return pl.BlockSpec((tr, cols), index)

    spec = pl.BlockSpec((tr, cols), lambda i, core: (i, 0))
    g_specs = [g_spec(k, True) for k in range(n_parts)] + [g_spec(k, False) for k in range(n_parts)]
    return _call(body, name=name, grid=(rows // tr,), parallel=True, in_specs=[spec] * 3 + g_specs,
                 out_specs=[spec] * 4, out_shape=[_sds((rows, cols), F32)] * 4, args=(w, m, v, *g_mine, *g_theirs),
                 job=job, by_core=True)


def _transposed_cast(wt):
    cols, rows = wt.shape
    tile = 4 * LANES

    def body(x_ref, o_ref):
        o_ref[...] = jnp.transpose(x_ref[...]).astype(BF16)

    res, _ = _call(body, name="transpose_w_in", grid=(pl.cdiv(cols, tile),), parallel=True,
                   in_specs=[pl.BlockSpec((tile, rows), lambda j: (j, 0))],
                   out_specs=[pl.BlockSpec((rows, tile), lambda j: (0, j))], out_shape=[_sds((rows, cols), BF16)],
                   args=(wt,))
    return res[0]


def _cast_weights(ws, w_fi, job=None):
    steps = 4
    cols = w_fi.shape[1]
    tile = w_fi.shape[0] // (2 * steps)

    def body(*refs):
        for i_ref, o_ref in zip(refs[:len(refs) // 2], refs[len(refs) // 2:]):
            o_ref[...] = i_ref[...].astype(BF16)

    row_specs = [pl.BlockSpec((w.shape[0] // steps, w.shape[1]), lambda i: (i, 0)) for w in ws]
    half_spec = pl.BlockSpec((tile, cols), lambda i: (i, 0))
    return _call(body, name="cast_weights", grid=(steps,), parallel=True,
                 in_specs=row_specs + [pl.BlockSpec((None, tile, cols), lambda i, k=k: (k, i, 0)) for k in range(2)],
                 out_specs=row_specs + [half_spec, half_spec],
                 out_shape=[_sds(w.shape, BF16) for w in ws] + [_sds((steps * tile, cols), BF16)] * 2,
                 args=(*ws, w_fi.reshape(2, steps * tile, cols), w_fi.reshape(2, steps * tile, cols)), job=job)


def _adamw_transposed(wt, mt, vt, g_mine, g_theirs, name):
    cols, rows = wt.shape
    n_parts = len(g_mine)

    def body(w_ref, m_ref, v_ref, *rest):
        g_refs, (g_out, d_out, m_out, v_out) = rest[:-4], rest[-4:]
        mine = jnp.concatenate([r[...] for r in g_refs[:n_parts]], axis=0)
        theirs = jnp.concatenate([r[...] for r in g_refs[n_parts:]], axis=0)
        first = lax.axis_index("c") == 0
        g = jnp.transpose(jnp.concatenate([jnp.where(first, mine, theirs), jnp.where(first, theirs, mine)], axis=0))
        d, m2, v2 = _adam_values(w_ref[...], m_ref[...], v_ref[...], g)
        g_out[...] = g
        m_out[...] = m2
        v_out[...] = v2
        d_out[...] = d

    spec = pl.BlockSpec((LANES, rows), lambda j: (j, 0))
    g_specs = [pl.BlockSpec((p.shape[0], LANES), lambda j: (0, j)) for p in g_mine] * 2
    res, _ = _call(body, name=name, grid=(pl.cdiv(cols, LANES),), parallel=True, in_specs=[spec] * 3 + g_specs,
                   out_specs=[spec] * 4, out_shape=[_sds((cols, rows), F32)] * 4, args=(wt, mt, vt, *g_mine, *g_theirs))
    return res


def _inproj_fwd(x, g1, w_all, job=None):
    T = x.shape[0]
    tT = _row_tile(T, 512)

    def body(x_ref, g_ref, w_ref, a_ref, proj_ref, alow_ref):
        xv = x_ref[...]
        a = (xv * _rms_stats(xv) * g_ref[...]).astype(BF16)
        a_ref[...] = a
        for j in range(N_MAIN // 1024):
            cols = slice(j * 1024, (j + 1) * 1024)
            proj_ref[:, cols] = _dot(a, w_ref[:, cols]).astype(BF16)
        alow_ref[...] = _dot(a, w_ref[:, N_MAIN:N_ALL])

    row = lambda w: pl.BlockSpec((tT, w), lambda i: (i, 0))
    return _call(
        body, name="inproj_fwd", grid=(T // tT,), parallel=True,
        in_specs=[row(D_MODEL), pl.BlockSpec((1, D_MODEL), lambda i: (0, 0)), _whole()],
        out_specs=[row(D_MODEL), row(N_MAIN), row(LANES)],
        out_shape=[_sds((T, D_MODEL), BF16), _sds((T, N_MAIN), BF16), _sds((T, LANES), F32)],
        args=(x, g1, w_all), job=job)


def _gla_decay_terms(al_ref, wgu_ref, bg_ref, later_ref):
    logit = _dot_bf16(al_ref[...], wgu_ref[...]) + bg_ref[...]
    la = _log_sigmoid(logit) * (1.0 / GLA_TAU)
    delta = _dot_exact_lhs(later_ref[...], la)
    return logit, la, delta


def _gla_fwd(proj, alow, wgu, b_gate, gn, job=None):
    T = proj.shape[0]
    tT = _row_tile(T, 512)
    nc = tT // CHUNK

    def body(q_ref, k_ref, v_ref, r_ref, al_ref, wgu_ref, bg_ref, gn_ref, later_ref, y_ref, st_ref, s_scr):
        @pl.when(pl.program_id(0) == 0)
        def _():
            s_scr[...] = jnp.zeros_like(s_scr)

        _, la, delta = _gla_decay_terms(al_ref, wgu_ref, bg_ref, later_ref)
        kdec = (k_ref[...].astype(F32) * jnp.exp(delta)).astype(BF16)
        heads = range(GLA_HEADS)
        kcs = [slice(h * GLA_DK, (h + 1) * GLA_DK) for h in heads]
        vcs = [slice(h * GLA_DV, (h + 1) * GLA_DV) for h in heads]
        state = [s_scr[h] for h in heads]
        for c in range(nc):
            rows = slice(c * CHUNK, (c + 1) * CHUNK)
            first = slice(c * CHUNK, c * CHUNK + 1)
            dec = jnp.exp(la[first, :] + delta[first, :])
            upd_t = [_dot(v_ref[rows, vcs[h]], kdec[rows, kcs[h]], _TN) for h in heads]
            qs = [(q_ref[rows, kcs[h]].astype(F32) * (GLA_DK ** -0.5)).astype(BF16) for h in heads]
            for h in heads:
                state[h] = state[h] * dec[:, kcs[h]] + upd_t[h]
                st_ref[c, h] = state[h]
            o = [_dot(qs[h], state[h].astype(BF16), _NT) for h in heads]
            for h in heads:
                on = o[h] * _rms_stats(o[h]) * gn_ref[:, vcs[h]]
                rr = r_ref[rows, vcs[h]].astype(F32)
                y_ref[rows, vcs[h]] = (on * (rr * _sigmoid(rr))).astype(BF16)
        for h in heads:
            s_scr[h] = state[h]

    blk = lambda w, j: pl.BlockSpec((tT, w), lambda i: (i, j))
    return _call(
        body, name="gla_fwd", grid=(T // tT,),
        in_specs=[blk(512, 0), blk(512, 1), blk(1024, 1), blk(1024, 2), blk(LANES, 0)] + [_whole()] * 4,
        out_specs=[pl.BlockSpec((tT, GLA_V), lambda i: (i, 0)),
                   pl.BlockSpec((nc, GLA_HEADS, GLA_DV, GLA_DK), lambda i: (i, 0, 0, 0))],
        out_shape=[_sds((T, GLA_V), BF16), _sds((T // CHUNK, GLA_HEADS, GLA_DV, GLA_DK), F32)],
        scratch_shapes=[pltpu.VMEM((GLA_HEADS, GLA_DV, GLA_DK), F32)],
        args=(proj, proj, proj, proj, alow, wgu, b_gate, gn, _chunk_masks(tT, upper=True)), job=job)


def _sgu_mask():
    i = lax.broadcasted_iota(jnp.int32, (SGU_BLOCK, SGU_BLOCK), 0)
    j = lax.broadcasted_iota(jnp.int32, (SGU_BLOCK, SGU_BLOCK), 1)
    return lax.shift_right_logical(j, 6) <= lax.shift_right_logical(i, 6)


def _sgu_merge_fwd(x, proj, y_gla, ln_g, ln_b, w_sp, b_sp_t, w_bg, w_bs, w_o, g_pm, job=None):
    T = x.shape[0]
    tT = _row_tile(T, 512)
    nb = tT // SGU_BLOCK

    def body(x_ref, su_ref, sv_ref, gg_ref, gs_ref, yg_ref, lg_ref, lb_ref, w_ref, b_ref, wbg_ref, wbs_ref, wo_ref,
             g_ref, ys_ref, zg_ref, zs_ref, mg_ref, mix_ref, x1_ref):
        mask = _sgu_mask()
        for g in range(SGU_GROUPS):
            gc = slice(g * SGU_DG, (g + 1) * SGU_DG)
            wm = jnp.where(mask, w_ref[g], 0.0).astype(BF16)
            vf = _gelu(sv_ref[:, gc].astype(F32))
            mu = jnp.mean(vf, axis=-1, keepdims=True)
            vc = vf - mu
            rstd = lax.rsqrt(jnp.mean(vc * vc, axis=-1, keepdims=True) + EPS)
            vn = (vc * rstd * lg_ref[:, gc] + lb_ref[:, gc]).astype(BF16)
            u = _gelu(su_ref[:, gc].astype(F32))
            for b in range(nb):
                rows = slice(b * SGU_BLOCK, (b + 1) * SGU_BLOCK)
                mixed = _dot(wm, vn[rows, :]) + b_ref[:, g:g + 1]
                ys_ref[rows, gc] = (u[rows, :] * mixed).astype(BF16)
        zg = _dot(yg_ref[...], wbg_ref[...])
        zs = _dot(ys_ref[...], wbs_ref[...])
        zg_ref[...] = zg.astype(BF16)
        zs_ref[...] = zs.astype(BF16)
        merged = (_sigmoid(gg_ref[...].astype(F32)) * zg + _sigmoid(gs_ref[...].astype(F32)) * zs).astype(BF16)
        mg_ref[...] = merged
        mix = _dot(merged, wo_ref[...])
        mix_ref[...] = mix.astype(BF16)
        x1_ref[...] = x_ref[...] + mix * _rms_stats(mix) * g_ref[...]

    row = pl.BlockSpec((tT, D_MODEL), lambda i: (i, 0))
    blk = lambda j: pl.BlockSpec((tT, 1024), lambda i: (i, j))
    sds = lambda dt: _sds((T, D_MODEL), dt)
    return _call(body, name="sgu_merge_fwd", grid=(T // tT,), parallel=True,
                 in_specs=[row, blk(3), blk(4), blk(5), blk(6), row] + [_whole()] * 7
                 + [pl.BlockSpec((1, D_MODEL), lambda i: (0, 0))],
                 out_specs=[row] * 6, out_shape=[sds(BF16)] * 5 + [sds(F32)],
                 args=(x, proj, proj, proj, proj, y_gla, ln_g, ln_b, w_sp, b_sp_t, w_bg, w_bs, w_o, g_pm), job=job)


def _ffn_fwd_bwd(x1, tgt, w_fi_top, w_fi_bot, w_fo, g_pf, g_po):
    T = x1.shape[0]
    tT = _row_tile(T, 256)
    half = D_FF // 2
    kh = D_MODEL // 2

    def body(x1_ref, t_ref, top_ref, bot_ref, wfo_ref, gpf_ref, gpo_ref,
             h_ref, f_ref, dgu_ref, dy_ref, dx1_ref, loss_ref, dgpf_ref, dgpo_ref, gu_scr):
        @pl.when(pl.program_id(0) == 0)
        def _():
            loss_ref[...] = jnp.zeros_like(loss_ref)
            dgpf_ref[...] = jnp.zeros_like(dgpf_ref)
            dgpo_ref[...] = jnp.zeros_like(dgpo_ref)

        main = (half // 256) * 256
        pieces = (0, 1, None)

        def w_in_cols(ref, first_slab, p):
            if p is not None:
                return ref[first_slab + p, :, :main]
            return jnp.concatenate([ref[first_slab, :, main:], ref[first_slab + 1, :, main:]], axis=1)

        def w_out_rows(p):
            if p is not None:
                return wfo_ref[p * half:p * half + main, :]
            return jnp.concatenate([wfo_ref[main:half, :], wfo_ref[half + main:2 * half, :]], axis=0)

        def put(ref, base, p, val):
            if p is not None:
                ref[:, base + p * half:base + p * half + main] = val
            else:
                ref[:, base + main:base + half] = val[:, :half - main]
                ref[:, base + half + main:base + 2 * half] = val[:, half - main:]

        def get(ref, base, p):
            if p is not None:
                return ref[:, base + p * half:base + p * half + main]
            return jnp.concatenate([ref[:, base + main:base + half], ref[:, base + half + main:base + 2 * half]], axis=1)

        x1v = x1_ref[...]
        r2 = _rms_stats(x1v)
        h = (x1v * r2 * gpf_ref[...]).astype(BF16)
        h_ref[...] = h
        y = jnp.zeros((tT, D_MODEL), F32)
        for p in pieces:
            gate = _dot(h[:, :kh], w_in_cols(top_ref, 0, p)) + _dot(h[:, kh:], w_in_cols(bot_ref, 0, p))
            up = _dot(h[:, :kh], w_in_cols(top_ref, 2, p)) + _dot(h[:, kh:], w_in_cols(bot_ref, 2, p))
            put(gu_scr, 0, p, gate)
            put(gu_scr, D_FF, p, up)
            f = (gate * _sigmoid(gate) * up).astype(BF16)
            put(f_ref, 0, p, f)
            y = y + _dot(f, w_out_rows(p))
        r3 = _rms_stats(y)
        x2 = x1v + y * r3 * gpo_ref[...]
        err = x2 - t_ref[...]
        loss_ref[...] += jnp.sum(err * err) * (0.5 / D_MODEL)
        dx2 = err * (1.0 / D_MODEL)
        dy, dg = _rms_bwd(dx2, y, r3, gpo_ref[...])
        dgpo_ref[...] += jnp.sum(dg, axis=0, keepdims=True)
        dyb = dy.astype(BF16)
        dy_ref[...] = dyb
        dh_top = jnp.zeros((tT, kh), F32)
        dh_bot = jnp.zeros((tT, kh), F32)
        for p in pieces:
            df = _dot(dyb, w_out_rows(p), _NT)
            gate = get(gu_scr, 0, p)
            up = get(gu_scr, D_FF, p)
            sg = _sigmoid(gate)
            dgate = (df * up * (sg * (1.0 + gate * (1.0 - sg)))).astype(BF16)
            dup = (df * (gate * sg)).astype(BF16)
            put(dgu_ref, 0, p, dgate)
            put(dgu_ref, D_FF, p, dup)
            dh_top = dh_top + _dot(dgate, w_in_cols(top_ref, 0, p), _NT) + _dot(dup, w_in_cols(top_ref, 2, p), _NT)
            dh_bot = dh_bot + _dot(dgate, w_in_cols(bot_ref, 0, p), _NT) + _dot(dup, w_in_cols(bot_ref, 2, p), _NT)
        dh = jnp.concatenate([dh_top, dh_bot], axis=1)
        dx1n, dg2 = _rms_bwd(dh, x1v, r2, gpf_ref[...])
        dgpf_ref[...] += jnp.sum(dg2, axis=0, keepdims=True)
        dx1_ref[...] = dx2 + dx1n

    row = lambda w: pl.BlockSpec((tT, w), lambda i: (i, 0))
    vec = pl.BlockSpec((1, D_MODEL), lambda i: (0, 0))
    res, _ = _call(
        body, name="ffn_fwd_bwd", grid=(T // tT,),
        in_specs=[row(D_MODEL), row(D_MODEL), _whole(), _whole(), _whole(), vec, vec],
        out_specs=[row(D_MODEL), row(D_FF), row(2 * D_FF), row(D_MODEL), row(D_MODEL),
                   pl.BlockSpec((1, LANES), lambda i: (0, 0)), vec, vec],
        out_shape=[_sds((T, D_MODEL), BF16), _sds((T, D_FF), BF16), _sds((T, 2 * D_FF), BF16), _sds((T, D_MODEL), BF16),
                   _sds((T, D_MODEL), F32), _sds((1, LANES), F32), _sds((1, D_MODEL), F32), _sds((1, D_MODEL), F32)],
        scratch_shapes=[pltpu.VMEM((tT, 2 * D_FF), F32)], args=(x1, tgt, w_fi_top, w_fi_bot, w_fo, g_pf, g_po))
    return res


def _merge_sgu_bwd(dx1, mix, proj, zg, zs, w_bg, w_bs, w_o, g_pm, ln_g, ln_b, w_sp, b_sp_t, job=None):
    T = dx1.shape[0]
    tT = _row_tile(T, 256)
    nb = tT // SGU_BLOCK

    def body(dx1_ref, mix_ref, su_ref, sv_ref, gg_ref, gs_ref, zg_ref, zs_ref, wbg_ref, wbs_ref, wo_ref, g_ref,
             lg_ref, lb_ref, w_ref, b_ref,
             dmix_ref, dzg_ref, dzs_ref, dgate_ref, dyg_ref, dp_ref, dgpm_ref, dw_ref, dbt_ref, dlg_ref, dlb_ref):
        @pl.when(pl.program_id(0) == 0)
        def _():
            for ref in (dgpm_ref, dw_ref, dbt_ref, dlg_ref, dlb_ref):
                ref[...] = jnp.zeros_like(ref)

        mix = mix_ref[...].astype(F32)
        dmix, dg = _rms_bwd(dx1_ref[...], mix, _rms_stats(mix), g_ref[...])
        dgpm_ref[...] += jnp.sum(dg, axis=0, keepdims=True)
        dmb = dmix.astype(BF16)
        dmix_ref[...] = dmb
        dmerged = _dot(dmb, wo_ref[...], _NT)
        dys = None
        for k, (gate_ref, z_ref, w_br_ref, dz_ref) in enumerate(((gg_ref, zg_ref, wbg_ref, dzg_ref),
                                                                 (gs_ref, zs_ref, wbs_ref, dzs_ref))):
            sg = _sigmoid(gate_ref[...].astype(F32))
            dz = (dmerged * sg).astype(BF16)
            dz_ref[...] = dz
            dgate_ref[:, k * 1024:(k + 1) * 1024] = (dmerged * z_ref[...].astype(F32) * (sg * (1.0 - sg))).astype(BF16)
            dy_branch = _dot(dz, w_br_ref[...], _NT)
            if k == 0:
                dyg_ref[...] = dy_branch.astype(BF16)
            else:
                dys = dy_branch

        mask = _sgu_mask()
        lane = lax.broadcasted_iota(jnp.int32, (SGU_BLOCK, LANES), 1)
        for g in range(SGU_GROUPS):
            gc = slice(g * SGU_DG, (g + 1) * SGU_DG)
            gc_v = slice(1024 + g * SGU_DG, 1024 + (g + 1) * SGU_DG)
            wm = jnp.where(mask, w_ref[g], 0.0).astype(BF16)
            vf, dvf_dsv = _gelu_and_grad(sv_ref[:, gc].astype(F32))
            mu = jnp.mean(vf, axis=-1, keepdims=True)
            vc = vf - mu
            rstd = lax.rsqrt(jnp.mean(vc * vc, axis=-1, keepdims=True) + EPS)
            vhat = vc * rstd
            vn = (vhat * lg_ref[:, gc] + lb_ref[:, gc]).astype(BF16)
            u, du_dsu = _gelu_and_grad(su_ref[:, gc].astype(F32))
            dy = dys[:, gc]
            dmixed = (dy * u).astype(BF16)
            dvn_parts = []
            dw_acc = jnp.zeros((SGU_BLOCK, SGU_BLOCK), F32)
            db_acc = jnp.zeros((SGU_BLOCK, 1), F32)
            for b in range(nb):
                rows = slice(b * SGU_BLOCK, (b + 1) * SGU_BLOCK)
                mixed = _dot(wm, vn[rows, :]) + b_ref[:, g:g + 1]
                dp_ref[rows, gc] = (dy[rows, :] * mixed * du_dsu[rows, :]).astype(BF16)
                dvn_parts.append(_dot(wm, dmixed[rows, :], _TN))
                dw_acc = dw_acc + _dot(dmixed[rows, :], vn[rows, :], _NT)
                db_acc = db_acc + jnp.sum(dmixed[rows, :].astype(F32), axis=-1, keepdims=True)
            dw_ref[g] += jnp.where(mask, dw_acc, 0.0)
            dbt_ref[...] += jnp.where(lane == g, db_acc, 0.0)
            dvn = jnp.concatenate(dvn_parts, axis=0)
            dlg_ref[:, gc] += jnp.sum(dvn * vhat, axis=0, keepdims=True)
            dlb_ref[:, gc] += jnp.sum(dvn, axis=0, keepdims=True)
            dvh = dvn * lg_ref[:, gc]
            dvf = rstd * (dvh - jnp.mean(dvh, axis=-1, keepdims=True)
                          - vhat * jnp.mean(dvh * vhat, axis=-1, keepdims=True))
            dp_ref[:, gc_v] = (dvf * dvf_dsv).astype(BF16)

    row = pl.BlockSpec((tT, D_MODEL), lambda i: (i, 0))
    blk = lambda j: pl.BlockSpec((tT, 1024), lambda i: (i, j))
    vec = pl.BlockSpec((1, D_MODEL), lambda i: (0, 0))
    wide = lambda w: pl.BlockSpec((tT, w), lambda i: (i, 0))
    sds = _sds((T, D_MODEL), BF16)
    return _call(
        body, name="merge_sgu_bwd", grid=(T // tT,),
        in_specs=[row, row, blk(3), blk(4), blk(5), blk(6), row, row] + [_whole()] * 3 + [vec] + [_whole()] * 4,
        out_specs=[row, row, row, wide(W_MRG), row, wide(W_SGU), vec,
                   pl.BlockSpec((SGU_GROUPS, SGU_BLOCK, SGU_BLOCK), lambda i: (0, 0, 0)),
                   pl.BlockSpec((SGU_BLOCK, LANES), lambda i: (0, 0)), vec, vec],
        out_shape=[sds, sds, sds, _sds((T, W_MRG), BF16), sds, _sds((T, W_SGU), BF16), _sds((1, D_MODEL), F32),
                   _sds((SGU_GROUPS, SGU_BLOCK, SGU_BLOCK), F32), _sds((SGU_BLOCK, LANES), F32),
                   _sds((1, 1024), F32), _sds((1, 1024), F32)],
        args=(dx1, mix, proj, proj, proj, proj, zg, zs, w_bg, w_bs, w_o, g_pm, ln_g, ln_b, w_sp, b_sp_t), job=job)


def _gla_bwd(proj, alow, wgu, b_gate, gn, states, dy_gla, job=None):
    T = proj.shape[0]
    tT = _row_tile(T, 512)
    nc = tT // CHUNK
    nt = T // tT

    def body(q_ref, k_ref, v_ref, r_ref, al_ref, wgu_ref, bg_ref, gn_ref, later_ref, earlier_ref, st_ref, sp_ref, dy_ref,
             dp_ref, dal_ref, dgn_ref, dbg_ref, dwgu_ref, g_scr, dd_scr, dt_scr):
        step = pl.program_id(0)

        @pl.when(step == 0)
        def _():
            g_scr[...] = jnp.zeros_like(g_scr)
            dgn_ref[...] = jnp.zeros_like(dgn_ref)
            dbg_ref[...] = jnp.zeros_like(dbg_ref)
            dwgu_ref[...] = jnp.zeros_like(dwgu_ref)

        has_prev = jnp.where(step == nt - 1, 0.0, 1.0)
        logit, la, delta = _gla_decay_terms(al_ref, wgu_ref, bg_ref, later_ref)
        e = jnp.exp(delta)
        kdec_f = k_ref[...].astype(F32) * e
        kdec = kdec_f.astype(BF16)
        heads = range(GLA_HEADS)
        kcs = [slice(h * GLA_DK, (h + 1) * GLA_DK) for h in heads]
        vcs = [slice(h * GLA_DV, (h + 1) * GLA_DV) for h in heads]
        carry = [g_scr[h] for h in heads]
        dgn_acc = [jnp.zeros((1, GLA_DV), F32) for _ in heads]
        for c in reversed(range(nc)):
            rows = slice(c * CHUNK, (c + 1) * CHUNK)
            first = slice(c * CHUNK, c * CHUNK + 1)
            dec = jnp.exp(la[first, :] + delta[first, :])
            s_b = [st_ref[c, h].astype(BF16) for h in heads]
            qs = [(q_ref[rows, kcs[h]].astype(F32) * (GLA_DK ** -0.5)).astype(BF16) for h in heads]
            o = [_dot(qs[h], s_b[h], _NT) for h in heads]
            do = []
            for h in heads:
                rstd = _rms_stats(o[h])
                ohat = o[h] * rstd
                gnh = gn_ref[:, vcs[h]]
                dy = dy_ref[rows, vcs[h]].astype(F32)
                rr = r_ref[rows, vcs[h]].astype(F32)
                sg = _sigmoid(rr)
                don = dy * (rr * sg)
                dp_ref[rows, OFF_R + h * GLA_DV:OFF_R + (h + 1) * GLA_DV] = (
                    dy * (ohat * gnh) * (sg * (1.0 + rr * (1.0 - sg)))).astype(BF16)
                dgn_acc[h] = dgn_acc[h] + jnp.sum(don * ohat, axis=0, keepdims=True)
                dn = don * gnh
                do.append((rstd * (dn - ohat * jnp.mean(dn * ohat, axis=-1, keepdims=True))).astype(BF16))
            dq = [_dot(do[h], s_b[h]) for h in heads]
            g_t = [_dot(do[h], qs[h], _TN) + carry[h] for h in heads]
            g_b = [g_t[h].astype(BF16) for h in heads]
            dv = [_dot(kdec[rows, kcs[h]], g_b[h], _NT) for h in heads]
            dkdec = [_dot(v_ref[rows, vcs[h]], g_b[h]) for h in heads]
            for h in heads:
                s_prev = st_ref[c - 1, h] if c > 0 else sp_ref[0, h] * has_prev
                ddec = jnp.sum(g_t[h] * s_prev, axis=0, keepdims=True)
                carry[h] = g_t[h] * dec[:, kcs[h]]
                dp_ref[rows, OFF_Q + h * GLA_DK:OFF_Q + (h + 1) * GLA_DK] = (dq[h] * (GLA_DK ** -0.5)).astype(BF16)
                dp_ref[rows, OFF_V + h * GLA_DV:OFF_V + (h + 1) * GLA_DV] = dv[h].astype(BF16)
                dp_ref[rows, OFF_K + h * GLA_DK:OFF_K + (h + 1) * GLA_DK] = (dkdec[h] * e[rows, kcs[h]]).astype(BF16)
                dd_scr[rows, kcs[h]] = dkdec[h] * kdec_f[rows, kcs[h]]
                dt_scr[rows, kcs[h]] = jnp.broadcast_to(ddec * dec[:, kcs[h]], (CHUNK, GLA_DK))
        for h in heads:
            g_scr[h] = carry[h]
            dgn_ref[:, vcs[h]] += dgn_acc[h]
        dla = _dot_exact_lhs(earlier_ref[...], dd_scr[...]) + dt_scr[...]
        dlogit = dla * (1.0 / GLA_TAU) * _sigmoid(-logit)
        dbg_ref[...] += jnp.sum(dlogit, axis=0, keepdims=True)
        dwgu_ref[...] += _dot_bf16(al_ref[...], dlogit, _TN)
        dal_ref[...] = _dot_bf16(dlogit, wgu_ref[...], _NT).astype(BF16)

    rev = lambda i: nt - 1 - i
    blk = lambda w, j: pl.BlockSpec((tT, w), lambda i: (rev(i), j))
    st_blk = pl.BlockSpec((nc, GLA_HEADS, GLA_DV, GLA_DK), lambda i: (rev(i), 0, 0, 0))
    sp_blk = pl.BlockSpec((1, GLA_HEADS, GLA_DV, GLA_DK), lambda i: (jnp.maximum(rev(i) * nc - 1, 0), 0, 0, 0))
    return _call(
        body, name="gla_bwd", grid=(nt,),
        in_specs=[blk(512, 0), blk(512, 1), blk(1024, 1), blk(1024, 2), blk(LANES, 0)] + [_whole()] * 5
        + [st_blk, sp_blk, blk(GLA_V, 0)],
        out_specs=[blk(W_GLA, 0), blk(LANES, 0), pl.BlockSpec((1, GLA_V), lambda i: (0, 0)),
                   pl.BlockSpec((1, GLA_QK), lambda i: (0, 0)), pl.BlockSpec((LANES, GLA_QK), lambda i: (0, 0))],
        out_shape=[_sds((T, W_GLA), BF16), _sds((T, LANES), BF16), _sds((1, GLA_V), F32), _sds((1, GLA_QK), F32),
                   _sds((LANES, GLA_QK), F32)],
        scratch_shapes=[pltpu.VMEM((GLA_HEADS, GLA_DV, GLA_DK), F32), pltpu.VMEM((tT, GLA_QK), F32),
                        pltpu.VMEM((tT, GLA_QK), F32)],
        args=(proj, proj, proj, proj, alow, wgu, b_gate, gn, _chunk_masks(tT, upper=True), _chunk_masks(tT, upper=False),
              states, states, dy_gla), job=job)


def _inproj_bwd(x, dx1, g1, w_all, dparts, job=None):
    T = x.shape[0]
    tT = _row_tile(T, 512)
    offs = (0, W_GLA, W_GLA + W_SGU, N_MAIN)

    def body(x_ref, dx1_ref, g_ref, w_hbm, *rest):
        part_refs, (dx_ref, dg_ref, w_ref, w_sems) = rest[:len(offs)], rest[len(offs):]

        def compute(first):
            if first:
                copies = [pltpu.make_async_copy(w_hbm.at[:, pl.ds(off, p.shape[1])], w_ref.at[:, pl.ds(off, p.shape[1])],
                                                w_sems.at[k]) for k, (off, p) in enumerate(zip(offs, dparts))]
                for cp in copies:
                    cp.start()
            da = jnp.zeros((tT, D_MODEL), F32)
            for k, (off, p_ref) in enumerate(zip(offs, part_refs)):
                if first:
                    copies[k].wait()
                da = da + _dot(p_ref[...], w_ref[:, off:off + p_ref.shape[1]], _NT)
            xv = x_ref[...]
            dx, dg = _rms_bwd(da, xv, _rms_stats(xv), g_ref[...])
            dg_sum = jnp.sum(dg, axis=0, keepdims=True)
            dg_ref[...] = dg_sum if first else dg_ref[...] + dg_sum
            dx_ref[...] = dx1_ref[...] + dx

        first_step = pl.program_id(0) == 0
        pl.when(first_step)(lambda: compute(True))
        pl.when(jnp.logical_not(first_step))(lambda: compute(False))

    row = lambda w: pl.BlockSpec((tT, w), lambda i: (i, 0))
    vec = pl.BlockSpec((1, D_MODEL), lambda i: (0, 0))
    return _call(
        body, name="inproj_bwd", grid=(T // tT,),
        in_specs=[row(D_MODEL), row(D_MODEL), vec, pl.BlockSpec(memory_space=pl.ANY)] + [row(p.shape[1]) for p in dparts],
        out_specs=[row(D_MODEL), vec], out_shape=[_sds((T, D_MODEL), F32), _sds((1, D_MODEL), F32)],
        scratch_shapes=[pltpu.VMEM(w_all.shape, BF16), pltpu.SemaphoreType.DMA((len(offs),))],
        args=(x, dx1, g1, w_all, *dparts), job=job)


def _tn_matmul(a, b, name, job=None):
    T, M = a.shape
    N = b.shape[1]
    tk = _row_tile(T, 1024)
    tm = M if M <= 1024 else 1408
    tn = N // 2 if N > 2048 else N
    assert M % tm == 0 and N % tn == 0

    def body(a_ref, b_ref, o_ref):
        @pl.when(pl.program_id(2) == 0)
        def _():
            o_ref[...] = _dot(a_ref[...], b_ref[...], _TN)

        @pl.when(pl.program_id(2) > 0)
        def _():
            o_ref[...] += _dot(a_ref[...], b_ref[...], _TN)

    res, jres = _call(
        body, name=name, grid=(M // tm, N // tn, T // tk),
        in_specs=[pl.BlockSpec((tk, tm), lambda i, j, k: (k, i)), pl.BlockSpec((tk, tn), lambda i, j, k: (k, j))],
        out_specs=[pl.BlockSpec((tm, tn), lambda i, j, k: (i, j))], out_shape=[_sds((M, N), F32)], args=(a, b), job=job)
    return res[0], jres


def _pad_rows(a, rows=8):
    return jnp.pad(a, ((0, rows - a.shape[0]), (0, LANES - a.shape[1])))


def _halves_view(dw):
    r = dw.shape[0] // N_CHIPS
    return dw.reshape(N_CHIPS, 2, r // 2, dw.shape[1])


def kernel(x, norm_pre_mix, w_in, w_gate_up, b_gate, gla_norm, sgu_ln_g, sgu_ln_b, w_spatial, b_spatial, w_branch_gla, w_branch_sgu, w_out, norm_post_mix, norm_pre_ffn, w_ffn_in, w_ffn_out, norm_post_ffn, loss_target, m_norm_pre_mix, m_w_in, m_w_gate_up, m_b_gate, m_gla_norm, m_sgu_ln_g, m_sgu_ln_b, m_w_spatial, m_b_spatial, m_w_branch_gla, m_w_branch_sgu, m_w_out, m_norm_post_mix, m_norm_pre_ffn, m_w_ffn_in, m_w_ffn_out, m_norm_post_ffn, v_norm_pre_mix, v_w_in, v_w_gate_up, v_b_gate, v_gla_norm, v_sgu_ln_g, v_sgu_ln_b, v_w_spatial, v_b_spatial, v_w_branch_gla, v_w_branch_sgu, v_w_out, v_norm_post_mix, v_norm_pre_ffn, v_w_ffn_in, v_w_ffn_out, v_norm_post_ffn):
    chip = 2 * lax.axis_index("x") + lax.axis_index("y")
    xt, tgt = x[0], loss_target[0]

    tiny = jnp.concatenate([w_gate_up[0], _pad_rows(gla_norm[0]), _pad_rows(sgu_ln_g[0]), _pad_rows(sgu_ln_b[0]),
                            jnp.zeros((24, LANES), F32)], axis=0)

    def with_own(gathered, own):
        return lax.dynamic_update_slice(gathered, own[None], (chip, 0, 0))

    w_in_t, m_in_t, v_in_t = w_in[0].T, m_w_in[0].T, v_w_in[0].T
    w_in_b = _transposed_cast(w_in_t)
    (*own_rows, fi_top, fi_bot), (g_in, g_tiny) = _cast_weights(
        [w_branch_gla[0], w_branch_sgu[0], w_out[0], w_ffn_out[0]], w_ffn_in[0], job=_job_gather([w_in_b, tiny]))
    g_tiny = with_own(g_tiny, tiny)
    w_all = _relayout_w_in(with_own(g_in, w_in_b))
    cols = lambda a: a.transpose(1, 0, 2).reshape(a.shape[1], N_CHIPS * a.shape[2])
    wgu = jnp.pad(cols(g_tiny[:, 0:16]), ((0, LANES - GLA_RANK), (0, 0)))
    gn = cols(g_tiny[:, 16:20, :64]).reshape(1, GLA_V)
    ln_g = cols(g_tiny[:, 24:28, :64]).reshape(1, 1024)
    ln_b = cols(g_tiny[:, 32:36, :64]).reshape(1, 1024)
    b_sp_t = jnp.pad(b_spatial[0].T, ((0, 0), (0, LANES - SGU_GROUPS)))
    w_sp = w_spatial[0]

    (a, proj, alow), g_rows = _inproj_fwd(xt, norm_pre_mix, w_all, job=_job_gather(own_rows))
    rows = lambda g: g.reshape(N_CHIPS * g.shape[1], g.shape[2])
    w_bg, w_bs, w_o, w_fo = [rows(with_own(g, own)) for g, own in zip(g_rows, own_rows)]
    (y_gla, states), (g_top,) = _gla_fwd(proj, alow, wgu, b_gate, gn, job=_job_gather([fi_top]))
    (y_sgu, zg, zs, merged, mix, x1), (g_bot,) = _sgu_merge_fwd(
        xt, proj, y_gla, ln_g, ln_b, w_sp, b_sp_t, w_bg, w_bs, w_o, norm_post_mix, job=_job_gather([fi_bot]))
    h, f, dgu, dy, dx1, loss, d_gpf, d_gpo = _ffn_fwd_bwd(x1, tgt, with_own(g_top, fi_top), with_own(g_bot, fi_bot),
                                                          w_fo, norm_pre_ffn, norm_post_ffn)

    whole = lambda hs: [[(h_, None)] for h_ in hs]
    dw_fo, _ = _tn_matmul(f, dy, "dw_ffn_out")
    dw_fo4 = _halves_view(dw_fo)
    dw_fi, (q_fo,) = _tn_matmul(h, dgu, "dw_ffn_in", job=_job_to_other_core([[(dw_fo4, 0)]]))
    c_fo, = _presum([dw_fo4], [q_fo], "presum_ffn_out")
    (dmix, dzg, dzs, dp_mrg, dyg, dp_sgu, d_gpm, d_wsp, d_bsp_t, d_lng, d_lnb), (s_fo, q_fi) = _merge_sgu_bwd(
        dx1, mix, proj, zg, zs, w_bg, w_bs, w_o, norm_post_mix, ln_g, ln_b, w_sp, b_sp_t,
        job=_join(_job_scatter([c_fo]), _job_to_other_core([[(dw_fi, 0)]])))
    c_fi, = _presum([dw_fi], [q_fi], "presum_ffn_in")
    dw_c, _ = _tn_matmul(a, dp_mrg, "dw_in_merge")
    dw_b, _ = _tn_matmul(a, dp_sgu, "dw_in_sgu")
    dw_o4 = _halves_view(_tn_matmul(merged, dmix, "dw_out")[0])
    dw_bg4 = _halves_view(_tn_matmul(y_gla, dzg, "dw_branch_gla")[0])
    dw_bs4 = _halves_view(_tn_matmul(y_sgu, dzs, "dw_branch_sgu")[0])
    (h_fo,), _ = _sum_slots([c_fo], [s_fo], "sum_ffn_out")
    (dp_gla, dal, d_gn, d_bg, d_wgu), (s_fi, t_fo, q_b, q_c, q_o, q_bg, q_bs) = _gla_bwd(
        proj, alow, wgu, b_gate, gn, states, dyg,
        job=_join(_job_scatter([c_fi]), _job_to_other_core(
            whole([h_fo]) + [[(dw_b, 0)], [(dw_c, 0)], [(dw_o4, 0)], [(dw_bg4, 0)], [(dw_bs4, 0)]])))
    c_o, c_bg, c_bs = _presum([dw_o4, dw_bg4, dw_bs4], [q_o, q_bg, q_bs], "presum_out_branches")
    (h_fi,), _ = _sum_slots([c_fi], [s_fi], "sum_ffn_in")
    dw_d, _ = _tn_matmul(a, dal, "dw_in_gate")
    dw_a, (s_o, s_bg, s_bs, t_fi, q_d) = _tn_matmul(
        a, dp_gla, "dw_in_gla",
        job=_join(_job_scatter([c_o, c_bg, c_bs]), _job_to_other_core(whole([h_fi]) + [[(dw_d, 0)]])))
    (h_o, h_bg, h_bs), (q_a,) = _sum_slots([c_o, c_bg, c_bs], [s_o, s_bg, s_bs], "sum_out_branches",
                                           job=_job_to_other_core([[(dw_a, 0)]]))

    grads, deltas, new_m, new_v = {}, {}, {}, {}

    def update(name, w, m, v, g_mine, g_theirs, job=None):
        (g, d, m2, v2), jres = _adamw(w[0], m[0], v[0], g_mine, g_theirs, "adamw_" + name, job=job)
        grads[name], deltas[name], new_m[name], new_v[name] = g[None], d[None], m2[None], v2[None]
        return jres

    dw_in = [(dw_a, 0), (dw_b, W_GLA), (dw_c, W_GLA + W_SGU), (dw_d, N_MAIN)]
    update("w_ffn_out", w_ffn_out, m_w_ffn_out, v_w_ffn_out, [h_fo], [t_fo])
    q_in = [q_a, q_b, q_c, q_d]
    hr_in = D_MODEL // 2
    c_in_a, _ = _presum_w_in(dw_in, q_in, 0, hr_in // 8, "presum_w_in_a")
    c_in_b, (s_in_a, t_o, t_bg, t_bs) = _presum_w_in(
        dw_in, q_in, hr_in // 8, 7 * hr_in // 8, "presum_w_in_b",
        job=_join(_job_scatter([c_in_a]), _job_to_other_core(whole([h_o, h_bg, h_bs]))))
    update("w_ffn_in", w_ffn_in, m_w_ffn_in, v_w_ffn_in, [h_fi], [t_fi])
    update("w_out", w_out, m_w_out, v_w_out, [h_o], [t_o])
    update("w_branch_gla", w_branch_gla, m_w_branch_gla, v_w_branch_gla, [h_bg], [t_bg])
    update("w_branch_sgu", w_branch_sgu, m_w_branch_sgu, v_w_branch_sgu, [h_bs], [t_bs])
    (grad_x, d_g1), (s_in_b,) = _inproj_bwd(xt, dx1, norm_pre_mix, w_all, (dp_gla, dp_sgu, dp_mrg, dal),
                                            job=_job_scatter([c_in_b]))
    h_in = _sum_slots([c_in_a], [s_in_a], "sum_w_in_a")[0] + _sum_slots([c_in_b], [s_in_b], "sum_w_in_b")[0]
    t_in = _run_job(_job_to_other_core(whole(h_in)), "swap_w_in")
    for store, val in zip((grads, deltas, new_m, new_v),
                          _adamw_transposed(w_in_t, m_in_t, v_in_t, h_in, t_in, "adamw_w_in")):
        store["w_in"] = val.T[None]

    small_names = ["w_spatial", "w_gate_up", "norm_pre_mix", "norm_post_mix", "norm_pre_ffn", "norm_post_ffn", "b_gate",
                   "b_spatial", "gla_norm", "sgu_ln_g", "sgu_ln_b"]
    loss_out, small = _small_adamw(
        _small_sum([d_wsp, d_wgu, d_g1, d_gpm, d_gpf, d_gpo, d_bg, d_bsp_t, d_gn, d_lng, d_lnb, loss]),
        [w_spatial, w_gate_up, norm_pre_mix, norm_post_mix, norm_pre_ffn, norm_post_ffn, b_gate, b_spatial, gla_norm,
         sgu_ln_g, sgu_ln_b],
        [m_w_spatial, m_w_gate_up, m_norm_pre_mix, m_norm_post_mix, m_norm_pre_ffn, m_norm_post_ffn, m_b_gate,
         m_b_spatial, m_gla_norm, m_sgu_ln_g, m_sgu_ln_b],
        [v_w_spatial, v_w_gate_up, v_norm_pre_mix, v_norm_post_mix, v_norm_pre_ffn, v_norm_post_ffn, v_b_gate,
         v_b_spatial, v_gla_norm, v_sgu_ln_g, v_sgu_ln_b])
    for store, vals in zip((grads, deltas, new_m, new_v), small):
        store.update(zip(small_names, vals))

    order = ["norm_pre_mix", "w_in", "w_gate_up", "b_gate", "gla_norm", "sgu_ln_g", "sgu_ln_b", "w_spatial", "b_spatial",
             "w_branch_gla", "w_branch_sgu", "w_out", "norm_post_mix", "norm_pre_ffn", "w_ffn_in", "w_ffn_out",
             "norm_post_ffn"]
    out = [loss_out, grad_x[None]]
    for store in (grads, deltas, new_m, new_v):
        out.extend(store[n] for n in order)
    return tuple(out)
```

```python
import jax
import jax.numpy as jnp
from jax import lax
from jax.experimental import pallas as pl
from jax.experimental.pallas import tpu as pltpu

F32 = jnp.float32
BF16 = jnp.bfloat16

D_MODEL = 1024
GLA_HEADS = 4
GLA_DK = 128
GLA_DV = 256
GLA_QK = GLA_HEADS * GLA_DK
GLA_V = GLA_HEADS * GLA_DV
GLA_RANK = 16
GLA_TAU = 16.0
CHUNK = 64
SGU_GROUPS = 4
SGU_BLOCK = 128
SGU_DG = 256
D_FF = 2816
EPS = 1e-6
LANES = 128

OFF_Q, OFF_K, OFF_V, OFF_R, OFF_SU, OFF_SV, OFF_GG, OFF_GS, OFF_AL = 0, 512, 1024, 2048, 3072, 4096, 5120, 6144, 7168
W_GLA, W_SGU, W_MRG = 3072, 2048, 2048
N_MAIN = 7168
N_ALL = N_MAIN + LANES
_IN_SPLITS = (GLA_QK, GLA_QK, GLA_V, GLA_V, GLA_RANK, 1024, 1024, 1024, 1024)
_IN_STARTS = tuple(sum(_IN_SPLITS[:i]) for i in range(len(_IN_SPLITS) + 1))
_IN_DST = (OFF_Q, OFF_K, OFF_V, OFF_R, OFF_AL, OFF_SU, OFF_SV, OFF_GG, OFF_GS)
D_IN = _IN_STARTS[-1]

ADAM_LR = 0.001
ADAM_B1 = 0.9
ADAM_B2 = 0.999
ADAM_EPS = 1e-08
ADAM_WD = 0.01
ADAM_STEP = 10

VMEM_LIMIT_BYTES = 56 * 1024 * 1024
N_CHIPS = 4
N_PEER = N_CHIPS - 1
N_DEV = 8
MESH = pl.DeviceIdType.MESH

_NN = (((1,), (0,)), ((), ()))
_NT = (((1,), (1,)), ((), ()))
_TN = (((0,), (0,)), ((), ()))


def _dot(a, b, dims=_NN):
    return lax.dot_general(a, b, dims, preferred_element_type=F32)


def _split(x):
    hi = x.astype(BF16)
    lo = (x - hi.astype(F32)).astype(BF16)
    return hi, lo


def _dot_bf16(a, b, dims=_NN):
    return _dot(a.astype(BF16), b.astype(BF16), dims)


def _dot_exact_lhs(m, x):
    xh, xl = _split(x)
    return _dot(m, xh) + _dot(m, xl)


def _sigmoid(x):
    return 0.5 * jnp.tanh(0.5 * x) + 0.5


def _log_sigmoid(x):
    return jnp.minimum(x, 0.0) - jnp.log(1.0 + jnp.exp(-jnp.abs(x)))


_GELU_C = 0.7978845608028654
_GELU_A = 0.044715


def _gelu_and_grad(x):
    x2 = x * x
    t = jnp.tanh(_GELU_C * (x + _GELU_A * x * x2))
    g = 0.5 * x * (1.0 + t)
    dg = 0.5 * (1.0 + t) + 0.5 * x * (1.0 - t * t) * (_GELU_C * (1.0 + 3.0 * _GELU_A * x2))
    return g, dg


def _gelu(x):
    t = jnp.tanh(_GELU_C * (x + _GELU_A * x * x * x))
    return 0.5 * x * (1.0 + t)


def _rms_stats(x):
    return lax.rsqrt(jnp.mean(x * x, axis=-1, keepdims=True) + EPS)


def _rms_bwd(dout, y, r, g):
    yhat = y * r
    dn = dout * g
    dy = r * (dn - yhat * jnp.mean(dn * yhat, axis=-1, keepdims=True))
    return dy, dout * yhat


def _whole():
    return pl.BlockSpec(memory_space=pltpu.VMEM)


def _row_tile(T, want):
    t = min(T, want)
    assert T % t == 0
    return t


def _chunk_masks(tT, upper):
    row = lax.broadcasted_iota(jnp.int32, (tT, tT), 0)
    col = lax.broadcasted_iota(jnp.int32, (tT, tT), 1)
    same = (row // CHUNK) == (col // CHUNK)
    tri = (col > row) if upper else (col < row)
    return jnp.where(same & tri, 1.0, 0.0).astype(BF16)


class _Job:
    def __init__(self, ins, out_shapes, scratch, start, finish, mid=None):
        self.ins, self.out_shapes, self.scratch = list(ins), list(out_shapes), list(scratch)
        self.start, self.finish, self.mid = start, finish, mid


def _join(*jobs):
    def split(refs, counts):
        out, at = [], 0
        for n in counts:
            out.append(refs[at:at + n])
            at += n
        return out

    ni, no, ns = [len(j.ins) for j in jobs], [len(j.out_shapes) for j in jobs], [len(j.scratch) for j in jobs]

    def start(ins, outs, scr):
        for j, a, b, c in zip(jobs, split(ins, ni), split(outs, no), split(scr, ns)):
            j.start(a, b, c)

    def finish(ins, outs, scr):
        for j, a, b, c in zip(jobs, split(ins, ni), split(outs, no), split(scr, ns)):
            j.finish(a, b, c)

    def mid(ins, outs, scr):
        for j, a, b, c in zip(jobs, split(ins, ni), split(outs, no), split(scr, ns)):
            if j.mid is not None:
                j.mid(a, b, c)

    return _Job(sum((j.ins for j in jobs), []), sum((j.out_shapes for j in jobs), []),
                sum((j.scratch for j in jobs), []), start, finish, mid if any(j.mid for j in jobs) else None)


def _mesh_pos():
    return lax.axis_index("x"), lax.axis_index("y"), lax.axis_index("c")


def _peer_chips(xi, yi):
    return [(1 - xi, yi), (xi, 1 - yi), (1 - xi, 1 - yi)]


def _half(ci, rows):
    return pl.ds(pl.multiple_of(ci * rows, 8), rows)


def _sds(shape, dtype):
    return jax.ShapeDtypeStruct(tuple(shape), dtype)


def _job_gather(arrs):
    n = len(arrs)
    kinds = 12
    Y0, Y1, X1, X0, ON_X, ON_Y, D2D = 0, 1, 2, 3, 4, 5, 6

    def copies(ins, outs, scr):
        send_sems, recv_sems = scr
        xi, yi, ci = _mesh_pos()
        me, cx, cy, cd = 2 * xi + yi, 2 * (1 - xi) + yi, 2 * xi + (1 - yi), 2 * (1 - xi) + (1 - yi)
        to_x, to_y, to_core = (1 - xi, yi, ci), (xi, 1 - yi, ci), (xi, yi, 1 - ci)
        table = []
        for k in range(n):
            qr = arrs[k].shape[0] // 4

            def rows(core, q):
                return pl.ds(pl.multiple_of((2 * core + q) * qr, 8), qr)

            def cp(kind, src, dst, to):
                s = k * kinds + kind
                return pltpu.make_async_remote_copy(src_ref=src, dst_ref=dst, send_sem=send_sems.at[s],
                                                    recv_sem=recv_sems.at[s], device_id=to, device_id_type=MESH)

            def slab(chip, core, q):
                return outs[k].at[chip, rows(core, q)]

            t = {}
            for kind, q, to, frm in ((Y0, 0, to_y, cy), (Y1, 1, to_y, cy), (X1, 1, to_x, cx), (X0, 0, to_x, cx)):
                mine = ins[k].at[rows(ci, q)]
                t[kind] = (cp(kind, mine, slab(me, ci, q), to), cp(kind, mine, slab(frm, ci, q), to))
            t[ON_X] = (cp(ON_X, slab(cy, ci, 0), slab(cy, ci, 0), to_x), cp(ON_X, slab(cy, ci, 0), slab(cd, ci, 0), to_x))
            t[ON_Y] = (cp(ON_Y, slab(cx, ci, 1), slab(cx, ci, 1), to_y), cp(ON_Y, slab(cx, ci, 1), slab(cd, ci, 1), to_y))
            for i, (chip, q) in enumerate(((cy, 0), (cy, 1), (cx, 1), (cx, 0), (cd, 0), (cd, 1))):
                t[D2D + i] = (cp(D2D + i, slab(chip, ci, q), slab(chip, ci, q), to_core),
                              cp(D2D + i, slab(chip, ci, q), slab(chip, 1 - ci, q), to_core))
            table.append(t)
        return table

    def start(ins, outs, scr):
        table = copies(ins, outs, scr)
        for kind in (Y0, X1, Y1, X0):
            for t in table:
                t[kind][0].start()

    def arrived(table, kind, then):
        for t in table:
            t[kind][1].wait_recv()
            for nxt in then:
                t[nxt][0].start()

    def mid(ins, outs, scr):
        table = copies(ins, outs, scr)
        arrived(table, Y0, (ON_X, D2D + 0))
        arrived(table, X1, (ON_Y, D2D + 2))

    def finish(ins, outs, scr):
        table = copies(ins, outs, scr)
        arrived(table, Y1, (D2D + 1,))
        arrived(table, X0, (D2D + 3,))
        arrived(table, ON_X, (D2D + 4,))
        arrived(table, ON_Y, (D2D + 5,))
        for t in table:
            for i in range(6):
                t[D2D + i][1].wait_recv()
            for kind in range(kinds):
                t[kind][0].wait_send()

    dma = pltpu.SemaphoreType.DMA
    return _Job(arrs, [_sds((N_CHIPS,) + a.shape, a.dtype) for a in arrs], [dma((n * kinds,))] * 2, start, finish, mid)


def _job_scatter(parts):
    n = len(parts)

    def copies(ins, outs, scr):
        send_sems, recv_sems = scr
        xi, yi, ci = _mesh_pos()
        res = []
        for k in range(n):
            for j, (px, py) in enumerate(_peer_chips(xi, yi)):
                s = k * N_PEER + j
                res.append(pltpu.make_async_remote_copy(
                    src_ref=ins[k].at[2 * px + py], dst_ref=outs[k].at[j], send_sem=send_sems.at[s],
                    recv_sem=recv_sems.at[s], device_id=(px, py, ci), device_id_type=MESH))
        return res

    def start(ins, outs, scr):
        for cp in copies(ins, outs, scr):
            cp.start()

    def finish(ins, outs, scr):
        for cp in copies(ins, outs, scr):
            cp.wait_recv()
            cp.wait_send()

    dma = pltpu.SemaphoreType.DMA
    return _Job(parts, [_sds((N_PEER,) + p.shape[1:], p.dtype) for p in parts], [dma((n * N_PEER,))] * 2, start, finish)


def _job_to_other_core(groups):
    pieces = [(g, a, off) for g, group in enumerate(groups) for a, off in group]
    n = len(pieces)

    def geometry(group):
        a0, off0 = group[0]
        if off0 is None:
            return a0.shape
        if a0.ndim == 4:
            return (N_CHIPS, a0.shape[2], a0.shape[3])
        return (a0.shape[0] // 2, sum(a.shape[1] for a, _ in group))

    def copies(ins, outs, scr):
        send_sems, recv_sems = scr
        xi, yi, ci = _mesh_pos()
        res = []
        for p, (g, a, off) in enumerate(pieces):
            if off is None:
                give, land = ins[p], outs[g]
            elif a.ndim == 4:
                give, land = ins[p].at[pl.ds(0, N_CHIPS), 1 - ci], outs[g]
            else:
                hr, w = a.shape[0] // 2, a.shape[1]
                give, land = ins[p].at[_half(1 - ci, hr)], outs[g].at[pl.ds(0, hr), pl.ds(off, w)]
            res.append(pltpu.make_async_remote_copy(
                src_ref=give, dst_ref=land, send_sem=send_sems.at[p], recv_sem=recv_sems.at[p],
                device_id=(xi, yi, 1 - ci), device_id_type=MESH))
        return res

    def start(ins, outs, scr):
        for cp in copies(ins, outs, scr):
            cp.start()

    def finish(ins, outs, scr):
        for cp in copies(ins, outs, scr):
            cp.wait_recv()
            cp.wait_send()

    dma = pltpu.SemaphoreType.DMA
    return _Job([a for _, a, _ in pieces], [_sds(geometry(group), group[0][0].dtype) for group in groups],
                [dma((n,))] * 2, start, finish)


def _call(body, *, name, grid, in_specs, out_specs, out_shape, args, scratch_shapes=(), parallel=False, job=None,
          by_core=False):
    n_in, n_out, n_scr = len(in_specs), len(out_specs), len(scratch_shapes)
    hbm = pl.BlockSpec(memory_space=pl.ANY)
    n_ji, n_jo = (len(job.ins), len(job.out_shapes)) if job is not None else (0, 0)
    lead = 1 if by_core else 0

    def kernel_fn(*refs):
        core, refs = refs[:lead], refs[lead:]
        ins, refs = refs[:n_in], refs[n_in:]
        j_ins, refs = refs[:n_ji], refs[n_ji:]
        outs, refs = refs[:n_out], refs[n_out:]
        j_outs, refs = refs[:n_jo], refs[n_jo:]
        scr, j_scr = refs[:n_scr], refs[n_scr:]
        if job is None:
            body(*core, *ins, *outs, *scr)
            return
        ids = [pl.program_id(d) for d in range(len(grid))]
        first = ids[0] == 0
        last = ids[0] == grid[0] - 1
        for d in range(1, len(grid)):
            first = first & (ids[d] == 0)
            last = last & (ids[d] == grid[d] - 1)

        @pl.when(first)
        def _():
            job.start(j_ins, j_outs, j_scr)

        if job.mid is not None and grid[0] >= 4:
            half_way = ids[0] == grid[0] // 2
            for d in range(1, len(grid)):
                half_way = half_way & (ids[d] == 0)

            @pl.when(half_way)
            def _():
                job.mid(j_ins, j_outs, j_scr)

        body(*core, *ins, *outs, *scr)

        @pl.when(last)
        def _():
            if job.mid is not None and grid[0] < 4:
                job.mid(j_ins, j_outs, j_scr)
            job.finish(j_ins, j_outs, j_scr)

    sem = ("parallel" if parallel and job is None else "arbitrary",) * len(grid)
    all_in = list(in_specs) + [hbm] * n_ji
    all_out = list(out_specs) + [hbm] * n_jo
    all_scratch = list(scratch_shapes) + (job.scratch if job is not None else [])
    all_shapes = list(out_shape) + (job.out_shapes if job is not None else [])
    all_args = list(args) + (job.ins if job is not None else [])
    params = pltpu.CompilerParams(dimension_semantics=sem, vmem_limit_bytes=VMEM_LIMIT_BYTES)
    if by_core:
        spec = pltpu.PrefetchScalarGridSpec(num_scalar_prefetch=1, grid=grid, in_specs=all_in, out_specs=all_out,
                                            scratch_shapes=all_scratch)
        core = lax.axis_index("c").astype(jnp.int32).reshape(1)
        res = pl.pallas_call(kernel_fn, name=name, grid_spec=spec, out_shape=all_shapes, compiler_params=params)(
            core, *all_args)
    else:
        res = pl.pallas_call(kernel_fn, name=name, grid=grid, in_specs=all_in, out_specs=all_out, out_shape=all_shapes,
                             scratch_shapes=all_scratch, compiler_params=params)(*all_args)
    return list(res[:n_out]), list(res[n_out:])


def _run_job(job, name):
    n_i, n_o = len(job.ins), len(job.out_shapes)

    def body(*refs):
        ins, outs, scr = refs[:n_i], refs[n_i:n_i + n_o], refs[n_i + n_o:]
        job.start(ins, outs, scr)
        if job.mid is not None:
            job.mid(ins, outs, scr)
        job.finish(ins, outs, scr)

    hbm = pl.BlockSpec(memory_space=pl.ANY)
    return list(pl.pallas_call(body, name=name, in_specs=[hbm] * n_i, out_specs=[hbm] * n_o, out_shape=job.out_shapes,
                               scratch_shapes=job.scratch)(*job.ins))


def _adam_values(w, m, v, g):
    m2 = ADAM_B1 * m + (1.0 - ADAM_B1) * g
    v2 = ADAM_B2 * v + (1.0 - ADAM_B2) * (g * g)
    delta = -ADAM_LR * ((m2 / (1.0 - ADAM_B1 ** ADAM_STEP)) / (jnp.sqrt(v2 / (1.0 - ADAM_B2 ** ADAM_STEP)) + ADAM_EPS)
                        + ADAM_WD * w)
    return delta, m2, v2


_P_WSP, _P_WGU, _P_NORM, _P_BG, _P_BSP, _P_HEAD, _P_LOSS, _P_ROWS = 0, 512, 576, 608, 616, 624, 720, 736


def _small_sum(dgrads):
    hr = _P_ROWS // 2

    def body(dwsp, dwgu, dg1, dgpm, dgpf, dgpo, dbg, dbspt, dgn, dlng, dlnb, loss_in, tot, pack, pair, slots, send_sems,
             recv_sems):
        xi, yi, ci = _mesh_pos()
        chip = 2 * xi + yi

        pack[...] = jnp.zeros_like(pack)
        for g in range(SGU_GROUPS):
            pack[_P_WSP + g * SGU_BLOCK:_P_WSP + (g + 1) * SGU_BLOCK] = dwsp[g]
        for j in range(N_CHIPS):
            pack[_P_WGU + GLA_RANK * j:_P_WGU + GLA_RANK * (j + 1)] = dwgu[0:GLA_RANK, LANES * j:LANES * (j + 1)]
        for k, r in enumerate((dg1, dgpm, dgpf, dgpo)):
            for q in range(8):
                pack[_P_NORM + 8 * k + q:_P_NORM + 8 * k + q + 1] = r[:, LANES * q:LANES * (q + 1)]
        for q in range(4):
            pack[_P_BG + q:_P_BG + q + 1] = dbg[:, LANES * q:LANES * (q + 1)]
        pack[_P_BSP:_P_BSP + SGU_GROUPS] = jnp.transpose(dbspt[...])[0:SGU_GROUPS]
        for k, r in enumerate((dgn, dlng, dlnb)):
            for j in range(N_CHIPS):
                for hh in range(4):
                    row = _P_HEAD + 32 * k + 8 * j + hh
                    pack[row:row + 1, 0:64] = r[:, 256 * hh + 64 * j:256 * hh + 64 * (j + 1)]
        pack[_P_LOSS:_P_LOSS + 1] = loss_in[...]

        sibling = dict(device_id=(xi, yi, 1 - ci), device_id_type=MESH)
        to_sibling = pltpu.make_async_remote_copy(src_ref=pack, dst_ref=pair, send_sem=send_sems.at[N_PEER],
                                                  recv_sem=recv_sems.at[N_PEER], **sibling)
        to_sibling.start()
        to_sibling.wait_recv()
        to_sibling.wait_send()
        pack[...] = pack[...] + pair[...]
        mine = pl.ds(pl.multiple_of(ci * hr, 8), hr)
        theirs = pl.ds(pl.multiple_of((1 - ci) * hr, 8), hr)
        slots[chip] = pack[mine, :]

        def copy(j, slot):
            px, py = _peer_chips(xi, yi)[j]
            return pltpu.make_async_remote_copy(
                src_ref=pack.at[mine], dst_ref=slots.at[slot(2 * px + py)], send_sem=send_sems.at[j],
                recv_sem=recv_sems.at[j], device_id=(px, py, ci), device_id_type=MESH)

        sends = [copy(j, lambda peer_chip: chip) for j in range(N_PEER)]
        for cp in sends:
            cp.start()
        for j in range(N_PEER):
            copy(j, lambda peer_chip: peer_chip).wait_recv()
        for cp in sends:
            cp.wait_send()
        acc = slots[0]
        for d in range(1, N_CHIPS):
            acc = acc + slots[d]
        tot[mine, :] = acc
        half_over = pltpu.make_async_remote_copy(src_ref=tot.at[mine], dst_ref=tot.at[mine], send_sem=send_sems.at[N_PEER + 1],
                                                 recv_sem=recv_sems.at[N_PEER + 1], **sibling)
        half_back = pltpu.make_async_remote_copy(src_ref=tot.at[mine], dst_ref=tot.at[theirs], send_sem=send_sems.at[N_PEER + 1],
                                                 recv_sem=recv_sems.at[N_PEER + 1], **sibling)
        half_over.start()
        half_back.wait_recv()
        half_over.wait_send()

    return pl.pallas_call(
        body, name="small_sum", in_specs=[_whole()] * 12, out_specs=_whole(), out_shape=_sds((_P_ROWS, LANES), F32),
        scratch_shapes=[pltpu.VMEM((_P_ROWS, LANES), F32), pltpu.VMEM((_P_ROWS, LANES), F32),
                        pltpu.VMEM((N_CHIPS, hr, LANES), F32),
                        pltpu.SemaphoreType.DMA((N_PEER + 2,)), pltpu.SemaphoreType.DMA((N_PEER + 2,))],
        compiler_params=pltpu.CompilerParams(vmem_limit_bytes=VMEM_LIMIT_BYTES),
    )(*dgrads)


def _small_adamw(tot, ws, ms, vs):
    n = len(ws)

    def body(*refs):
        tot = refs[0]
        w_refs, m_refs, v_refs = refs[1:1 + n], refs[1 + n:1 + 2 * n], refs[1 + 2 * n:1 + 3 * n]
        loss_out = refs[1 + 3 * n]
        outs = refs[2 + 3 * n:]
        chip = 2 * lax.axis_index("x") + lax.axis_index("y")
        loss_out[...] = tot[_P_LOSS:_P_LOSS + 1, 0:1]

        def step(k, g, pick, put):
            d, m2, v2 = _adam_values(pick(w_refs[k]), pick(m_refs[k]), pick(v_refs[k]), g)
            for o, val in zip((outs[k], outs[n + k], outs[2 * n + k], outs[3 * n + k]), (g, d, m2, v2)):
                put(o, val)

        def whole(ref):
            return ref[0]

        def put_whole(ref, val):
            ref[0] = val

        for g in range(SGU_GROUPS):
            def pick_g(ref, g=g):
                return ref[0, g]

            def put_g(ref, val, g=g):
                ref[0, g] = val

            step(0, tot[_P_WSP + g * SGU_BLOCK:_P_WSP + (g + 1) * SGU_BLOCK], pick_g, put_g)
        step(1, tot[pl.ds(pl.multiple_of(_P_WGU + GLA_RANK * chip, GLA_RANK), GLA_RANK), :], whole, put_whole)
        for k, (base, chunks) in enumerate(((_P_NORM, 8), (_P_NORM + 8, 8), (_P_NORM + 16, 8), (_P_NORM + 24, 8), (_P_BG, 4))):
            for q in range(chunks):
                def pick_q(ref, q=q):
                    return ref[:, LANES * q:LANES * (q + 1)]

                def put_q(ref, val, q=q):
                    ref[:, LANES * q:LANES * (q + 1)] = val

                step(2 + k, tot[base + q:base + q + 1], pick_q, put_q)
        step(7, tot[_P_BSP:_P_BSP + SGU_GROUPS], whole, put_whole)
        for k in range(3):
            mine = tot[pl.ds(pl.multiple_of(_P_HEAD + 32 * k + 8 * chip, 8), 8), :]
            step(8 + k, mine[0:4, 0:64], whole, put_whole)

    shapes = [_sds(w.shape, F32) for w in ws]
    res = pl.pallas_call(
        body, name="small_adamw", in_specs=[_whole()] * (1 + 3 * n), out_specs=[_whole()] * (1 + 4 * n),
        out_shape=[_sds((1, 1), F32)] + shapes * 4,
        compiler_params=pltpu.CompilerParams(vmem_limit_bytes=VMEM_LIMIT_BYTES),
    )(tot, *ws, *ms, *vs)
    return res[0].reshape(()), [list(res[1 + i * n:1 + (i + 1) * n]) for i in range(4)]


def _w_in_pieces():
    blk = D_IN // N_CHIPS
    pieces = []
    for s in range(len(_IN_SPLITS)):
        lo_s, hi_s = _IN_STARTS[s], _IN_STARTS[s + 1]
        for j in range(N_CHIPS):
            lo, hi = max(lo_s, j * blk), min(hi_s, (j + 1) * blk)
            if lo < hi:
                pieces.append((j, lo - j * blk, _IN_DST[s] + lo - lo_s, hi - lo))
    return pieces


def _relayout_w_in(gathered):
    _, rows, blk = gathered.shape
    tr = 256

    def body(g_ref, o_ref):
        o_ref[:, OFF_AL:N_ALL] = jnp.zeros((tr, LANES), BF16)
        for j, src, dst, w in _w_in_pieces():
            o_ref[:, dst:dst + w] = g_ref[j, :, src:src + w]

    res, _ = _call(body, name="relayout_w_in", grid=(rows // tr,), parallel=True,
                   in_specs=[pl.BlockSpec((N_CHIPS, tr, blk), lambda i: (0, i, 0))],
                   out_specs=[pl.BlockSpec((tr, N_ALL), lambda i: (i, 0))],
                   out_shape=[_sds((rows, N_ALL), BF16)], args=(gathered,))
    return res[0]


def _update_row_tile(rows):
    for t in range(min(rows, 256), 7, -8):
        if rows % t == 0:
            return t
    return rows


def _presum_w_in(dws, theirs, row0, rows, name, job=None):
    hr = theirs[0].shape[0]
    blk = D_IN // N_CHIPS
    tr = 64
    assert row0 % tr == 0 and rows % tr == 0
    nh, t0 = hr // tr, row0 // tr
    n = len(dws)

    def body(core_ref, *refs):
        dw_refs, q_refs, (o_ref, s_scr) = refs[:n], refs[n:2 * n], refs[2 * n:]
        for p, (a, off) in enumerate(dws):
            w = a.shape[1]
            s_scr[:, off:off + w] = (dw_refs[p][...] + q_refs[p][...]).astype(BF16)
        for j, src, dst, w in _w_in_pieces():
            o_ref[j, :, src:src + w] = s_scr[:, dst:dst + w]

    in_specs = [pl.BlockSpec((tr, a.shape[1]), lambda i, core: (i + t0 + core[0] * nh, 0)) for a, _ in dws]
    in_specs += [pl.BlockSpec((tr, q.shape[1]), lambda i, core: (i + t0, 0)) for q in theirs]
    res, jres = _call(body, name=name, grid=(rows // tr,), parallel=True, in_specs=in_specs,
                      out_specs=[pl.BlockSpec((N_CHIPS, tr, blk), lambda i, core: (0, i, 0))],
                      out_shape=[_sds((N_CHIPS, rows, blk), BF16)], scratch_shapes=[pltpu.VMEM((tr, N_ALL), BF16)],
                      args=(*[a for a, _ in dws], *theirs), job=job, by_core=True)
    return res[0], jres


def _presum(dws, theirs, name):
    dw, n = dws[0], len(dws)
    assert all(d.shape == dw.shape for d in dws)
    if dw.ndim == 4:
        _, _, hr, c = dw.shape
        tr = _update_row_tile(hr)
        mine = pl.BlockSpec((1, 1, tr, c), lambda j, i, core: (j, core[0], i, 0))
        other = pl.BlockSpec((1, tr, c), lambda j, i, core: (j, i, 0))
    else:
        hr, c = dw.shape[0] // 2, dw.shape[1] // N_CHIPS
        tr = _update_row_tile(hr)
        nh = hr // tr
        mine = pl.BlockSpec((tr, c), lambda j, i, core: (i + core[0] * nh, j))
        other = pl.BlockSpec((tr, c), lambda j, i, core: (i, j))

    def body(core_ref, *refs):
        for a_ref, q_ref, o_ref in zip(refs[:n], refs[n:2 * n], refs[2 * n:]):
            o_ref[...] = (a_ref[...].reshape(tr, c) + q_ref[...].reshape(tr, c)).astype(BF16).reshape(o_ref.shape)

    res, _ = _call(body, name=name, grid=(N_CHIPS, hr // tr), parallel=True, in_specs=[mine] * n + [other] * n,
                   out_specs=[pl.BlockSpec((1, tr, c), lambda j, i, core: (j, i, 0))] * n,
                   out_shape=[_sds((N_CHIPS, hr, c), BF16)] * n, args=(*dws, *theirs), by_core=True)
    return res


def _sum_slots(sums, slots, name):
    n = len(sums)
    _, rows, cols = sums[0].shape
    assert all(s.shape == sums[0].shape for s in sums)
    tr = _update_row_tile(rows)

    def body(chip_ref, *refs):
        for own_ref, s_ref, o_ref in zip(refs[:n], refs[n:2 * n], refs[2 * n:]):
            acc = own_ref[...].astype(F32)
            for j in range(N_PEER):
                acc = acc + s_ref[j].astype(F32)
            o_ref[...] = acc

    chip = (2 * lax.axis_index("x") + lax.axis_index("y")).astype(jnp.int32).reshape(1)
    spec = pltpu.PrefetchScalarGridSpec(
        num_scalar_prefetch=1, grid=(rows // tr,),
        in_specs=[pl.BlockSpec((None, tr, cols), lambda i, chip: (chip[0], i, 0))] * n
        + [pl.BlockSpec((N_PEER, tr, cols), lambda i, chip: (0, i, 0))] * n,
        out_specs=[pl.BlockSpec((tr, cols), lambda i, chip: (i, 0))] * n)
    return list(pl.pallas_call(
        body, name=name, grid_spec=spec, out_shape=[_sds((rows, cols), F32)] * n,
        compiler_params=pltpu.CompilerParams(dimension_semantics=("parallel",), vmem_limit_bytes=VMEM_LIMIT_BYTES),
    )(chip, *sums, *slots))


def _adamw(ws, ms, vs, g_mine, g_theirs, name, job=None):
    n = len(ws)
    rows, cols = ws[0].shape
    part_rows = [p.shape[0] for p in g_mine[0]]
    assert all(w.shape == (rows, cols) for w in ws) and sum(part_rows) == rows // 2
    assert all([p.shape[0] for p in parts] == part_rows for parts in (*g_mine, *g_theirs))
    tr = _update_row_tile(min(part_rows))
    assert all(r % tr == 0 for r in part_rows)
    nh = (rows // 2) // tr
    starts = [sum(part_rows[:k]) // tr for k in range(len(part_rows))]
    n_parts = len(part_rows)

    def body(core_ref, *refs):
        ins, outs = refs[:(3 + 2 * n_parts) * n], refs[(3 + 2 * n_parts) * n:]
        step = pl.program_id(0)
        mine_here = (step // nh) == core_ref[0]
        q = step % nh
        for a in range(n):
            g_refs = ins[3 * n + 2 * n_parts * a:3 * n + 2 * n_parts * (a + 1)]
            g = None
            for k in reversed(range(n_parts)):
                val = jnp.where(mine_here, g_refs[k][...], g_refs[n_parts + k][...])
                g = val if g is None else jnp.where(q < starts[k + 1], val, g)
            d, m2, v2 = _adam_values(ins[a][...], ins[n + a][...], ins[2 * n + a][...], g)
            g_out, d_out, m_out, v_out = outs[4 * a:4 * a + 4]
            g_out[...] = g
            m_out[...] = m2
            v_out[...] = v2
            d_out[...] = d

    def g_spec(k, mine):
        last = part_rows[k] // tr - 1

        def index(i, core):
            half = core[0] if mine else 1 - core[0]
            here = jnp.clip(i % nh - starts[k], 0, last)
            return (jnp.where(i // nh == half, here, jnp.where(i // nh > half, last, 0)), 0)

        return pl.BlockSpec((tr, cols), index)

    spec = pl.BlockSpec((tr, cols), lambda i, core: (i, 0))
    g_specs = [g_spec(k, True) for k in range(n_parts)] + [g_spec(k, False) for k in range(n_parts)]
    g_args = [p for a in range(n) for p in (*g_mine[a], *g_theirs[a])]
    res, jres = _call(body, name=name, grid=(rows // tr,), parallel=True, in_specs=[spec] * (3 * n) + g_specs * n,
                      out_specs=[spec] * (4 * n), out_shape=[_sds((rows, cols), F32)] * (4 * n),
                      args=(*ws, *ms, *vs, *g_args), job=job, by_core=True)
    return [tuple(res[4 * a:4 * a + 4]) for a in range(n)], jres


def _transposed_cast(wt):
    cols, rows = wt.shape
    tile = 4 * LANES

    def body(x_ref, o_ref):
        o_ref[...] = jnp.transpose(x_ref[...]).astype(BF16)

    res, _ = _call(body, name="transpose_w_in", grid=(pl.cdiv(cols, tile),), parallel=True,
                   in_specs=[pl.BlockSpec((tile, rows), lambda j: (j, 0))],
                   out_specs=[pl.BlockSpec((rows, tile), lambda j: (0, j))], out_shape=[_sds((rows, cols), BF16)],
                   args=(wt,))
    return res[0]


def _cast_weights(ws, w_fi, job=None):
    steps = 4
    cols = w_fi.shape[1]
    tile = w_fi.shape[0] // (2 * steps)

    def body(*refs):
        for i_ref, o_ref in zip(refs[:len(refs) // 2], refs[len(refs) // 2:]):
            o_ref[...] = i_ref[...].astype(BF16)

    row_specs = [pl.BlockSpec((w.shape[0] // steps, w.shape[1]), lambda i: (i, 0)) for w in ws]
    half_spec = pl.BlockSpec((tile, cols), lambda i: (i, 0))
    return _call(body, name="cast_weights", grid=(steps,), parallel=True,
                 in_specs=row_specs + [pl.BlockSpec((None, tile, cols), lambda i, k=k: (k, i, 0)) for k in range(2)],
                 out_specs=row_specs + [half_spec, half_spec],
                 out_shape=[_sds(w.shape, BF16) for w in ws] + [_sds((steps * tile, cols), BF16)] * 2,
                 args=(*ws, w_fi.reshape(2, steps * tile, cols), w_fi.reshape(2, steps * tile, cols)), job=job)


def _adamw_transposed(wt, mt, vt, g_mine, g_theirs, name):
    cols, rows = wt.shape
    n_parts = len(g_mine)

    def body(w_ref, m_ref, v_ref, *rest):
        g_refs, (g_out, d_out, m_out, v_out) = rest[:-4], rest[-4:]
        mine = jnp.concatenate([r[...] for r in g_refs[:n_parts]], axis=0)
        theirs = jnp.concatenate([r[...] for r in g_refs[n_parts:]], axis=0)
        first = lax.axis_index("c") == 0
        g = jnp.transpose(jnp.concatenate([jnp.where(first, mine, theirs), jnp.where(first, theirs, mine)], axis=0))
        d, m2, v2 = _adam_values(w_ref[...], m_ref[...], v_ref[...], g)
        g_out[...] = g
        m_out[...] = m2
        v_out[...] = v2
        d_out[...] = d

    spec = pl.BlockSpec((LANES, rows), lambda j: (j, 0))
    g_specs = [pl.BlockSpec((p.shape[0], LANES), lambda j: (0, j)) for p in g_mine] * 2
    res, _ = _call(body, name=name, grid=(pl.cdiv(cols, LANES),), parallel=True, in_specs=[spec] * 3 + g_specs,
                   out_specs=[spec] * 4, out_shape=[_sds((cols, rows), F32)] * 4, args=(wt, mt, vt, *g_mine, *g_theirs))
    return res


def _inproj_fwd(x, g1, w_all, job=None):
    T = x.shape[0]
    tT = _row_tile(T, 512)

    def body(x_ref, g_ref, w_ref, a_ref, proj_ref, alow_ref):
        xv = x_ref[...]
        a = (xv * _rms_stats(xv) * g_ref[...]).astype(BF16)
        a_ref[...] = a
        for j in range(N_MAIN // 1024):
            cols = slice(j * 1024, (j + 1) * 1024)
            proj_ref[:, cols] = _dot(a, w_ref[:, cols]).astype(BF16)
        alow_ref[...] = _dot(a, w_ref[:, N_MAIN:N_ALL])

    row = lambda w: pl.BlockSpec((tT, w), lambda i: (i, 0))
    return _call(
        body, name="inproj_fwd", grid=(T // tT,), parallel=True,
        in_specs=[row(D_MODEL), pl.BlockSpec((1, D_MODEL), lambda i: (0, 0)), _whole()],
        out_specs=[row(D_MODEL), row(N_MAIN), row(LANES)],
        out_shape=[_sds((T, D_MODEL), BF16), _sds((T, N_MAIN), BF16), _sds((T, LANES), F32)],
        args=(x, g1, w_all), job=job)


def _gla_decay_terms(al_ref, wgu_ref, bg_ref, later_ref):
    logit = _dot_bf16(al_ref[...], wgu_ref[...]) + bg_ref[...]
    la = _log_sigmoid(logit) * (1.0 / GLA_TAU)
    delta = _dot_exact_lhs(later_ref[...], la)
    return logit, la, delta


def _gla_fwd(proj, alow, wgu, b_gate, gn, job=None):
    T = proj.shape[0]
    tT = _row_tile(T, 512)
    nc = tT // CHUNK

    def body(q_ref, k_ref, v_ref, r_ref, al_ref, wgu_ref, bg_ref, gn_ref, later_ref, y_ref, st_ref, s_scr):
        @pl.when(pl.program_id(0) == 0)
        def _():
            s_scr[...] = jnp.zeros_like(s_scr)

        _, la, delta = _gla_decay_terms(al_ref, wgu_ref, bg_ref, later_ref)
        kdec = (k_ref[...].astype(F32) * jnp.exp(delta)).astype(BF16)
        heads = range(GLA_HEADS)
        kcs = [slice(h * GLA_DK, (h + 1) * GLA_DK) for h in heads]
        vcs = [slice(h * GLA_DV, (h + 1) * GLA_DV) for h in heads]
        state = [s_scr[h] for h in heads]
        for c in range(nc):
            rows = slice(c * CHUNK, (c + 1) * CHUNK)
            first = slice(c * CHUNK, c * CHUNK + 1)
            dec = jnp.exp(la[first, :] + delta[first, :])
            upd_t = [_dot(v_ref[rows, vcs[h]], kdec[rows, kcs[h]], _TN) for h in heads]
            qs = [(q_ref[rows, kcs[h]].astype(F32) * (GLA_DK ** -0.5)).astype(BF16) for h in heads]
            for h in heads:
                state[h] = state[h] * dec[:, kcs[h]] + upd_t[h]
                st_ref[c, h] = state[h]
            o = [_dot(qs[h], state[h].astype(BF16), _NT) for h in heads]
            for h in heads:
                on = o[h] * _rms_stats(o[h]) * gn_ref[:, vcs[h]]
                rr = r_ref[rows, vcs[h]].astype(F32)
                y_ref[rows, vcs[h]] = (on * (rr * _sigmoid(rr))).astype(BF16)
        for h in heads:
            s_scr[h] = state[h]

    blk = lambda w, j: pl.BlockSpec((tT, w), lambda i: (i, j))
    return _call(
        body, name="gla_fwd", grid=(T // tT,),
        in_specs=[blk(512, 0), blk(512, 1), blk(1024, 1), blk(1024, 2), blk(LANES, 0)] + [_whole()] * 4,
        out_specs=[pl.BlockSpec((tT, GLA_V), lambda i: (i, 0)),
                   pl.BlockSpec((nc, GLA_HEADS, GLA_DV, GLA_DK), lambda i: (i, 0, 0, 0))],
        out_shape=[_sds((T, GLA_V), BF16), _sds((T // CHUNK, GLA_HEADS, GLA_DV, GLA_DK), F32)],
        scratch_shapes=[pltpu.VMEM((GLA_HEADS, GLA_DV, GLA_DK), F32)],
        args=(proj, proj, proj, proj, alow, wgu, b_gate, gn, _chunk_masks(tT, upper=True)), job=job)


def _sgu_mask():
    i = lax.broadcasted_iota(jnp.int32, (SGU_BLOCK, SGU_BLOCK), 0)
    j = lax.broadcasted_iota(jnp.int32, (SGU_BLOCK, SGU_BLOCK), 1)
    return lax.shift_right_logical(j, 6) <= lax.shift_right_logical(i, 6)


def _sgu_merge_fwd(x, proj, y_gla, ln_g, ln_b, w_sp, b_sp_t, w_bg, w_bs, w_o, g_pm, job=None):
    T = x.shape[0]
    tT = _row_tile(T, 512)
    nb = tT // SGU_BLOCK

    def body(x_ref, su_ref, sv_ref, gg_ref, gs_ref, yg_ref, lg_ref, lb_ref, w_ref, b_ref, wbg_ref, wbs_ref, wo_ref,
             g_ref, ys_ref, zg_ref, zs_ref, mg_ref, mix_ref, x1_ref):
        mask = _sgu_mask()
        for g in range(SGU_GROUPS):
            gc = slice(g * SGU_DG, (g + 1) * SGU_DG)
            wm = jnp.where(mask, w_ref[g], 0.0).astype(BF16)
            vf = _gelu(sv_ref[:, gc].astype(F32))
            mu = jnp.mean(vf, axis=-1, keepdims=True)
            vc = vf - mu
            rstd = lax.rsqrt(jnp.mean(vc * vc, axis=-1, keepdims=True) + EPS)
            vn = (vc * rstd * lg_ref[:, gc] + lb_ref[:, gc]).astype(BF16)
            u = _gelu(su_ref[:, gc].astype(F32))
            for b in range(nb):
                rows = slice(b * SGU_BLOCK, (b + 1) * SGU_BLOCK)
                mixed = _dot(wm, vn[rows, :]) + b_ref[:, g:g + 1]
                ys_ref[rows, gc] = (u[rows, :] * mixed).astype(BF16)
        zg = _dot(yg_ref[...], wbg_ref[...])
        zs = _dot(ys_ref[...], wbs_ref[...])
        zg_ref[...] = zg.astype(BF16)
        zs_ref[...] = zs.astype(BF16)
        merged = (_sigmoid(gg_ref[...].astype(F32)) * zg + _sigmoid(gs_ref[...].astype(F32)) * zs).astype(BF16)
        mg_ref[...] = merged
        mix = _dot(merged, wo_ref[...])
        mix_ref[...] = mix.astype(BF16)
        x1_ref[...] = x_ref[...] + mix * _rms_stats(mix) * g_ref[...]

    row = pl.BlockSpec((tT, D_MODEL), lambda i: (i, 0))
    blk = lambda j: pl.BlockSpec((tT, 1024), lambda i: (i, j))
    sds = lambda dt: _sds((T, D_MODEL), dt)
    return _call(body, name="sgu_merge_fwd", grid=(T // tT,), parallel=True,
                 in_specs=[row, blk(3), blk(4), blk(5), blk(6), row] + [_whole()] * 7
                 + [pl.BlockSpec((1, D_MODEL), lambda i: (0, 0))],
                 out_specs=[row] * 6, out_shape=[sds(BF16)] * 5 + [sds(F32)],
                 args=(x, proj, proj, proj, proj, y_gla, ln_g, ln_b, w_sp, b_sp_t, w_bg, w_bs, w_o, g_pm), job=job)


def _ffn_fwd_bwd(x1, tgt, w_fi_top, w_fi_bot, w_fo, g_pf, g_po):
    T = x1.shape[0]
    tT = _row_tile(T, 256)
    half = D_FF // 2
    kh = D_MODEL // 2

    def body(x1_ref, t_ref, top_ref, bot_ref, wfo_ref, gpf_ref, gpo_ref,
             h_ref, f_ref, dgu_ref, dy_ref, dx1_ref, loss_ref, dgpf_ref, dgpo_ref, gu_scr):
        @pl.when(pl.program_id(0) == 0)
        def _():
            loss_ref[...] = jnp.zeros_like(loss_ref)
            dgpf_ref[...] = jnp.zeros_like(dgpf_ref)
            dgpo_ref[...] = jnp.zeros_like(dgpo_ref)

        main = (half // 256) * 256
        pieces = (0, 1, None)

        def w_in_cols(ref, first_slab, p):
            if p is not None:
                return ref[first_slab + p, :, :main]
            return jnp.concatenate([ref[first_slab, :, main:], ref[first_slab + 1, :, main:]], axis=1)

        def w_out_rows(p):
            if p is not None:
                return wfo_ref[p * half:p * half + main, :]
            return jnp.concatenate([wfo_ref[main:half, :], wfo_ref[half + main:2 * half, :]], axis=0)

        def put(ref, base, p, val):
            if p is not None:
                ref[:, base + p * half:base + p * half + main] = val
            else:
                ref[:, base + main:base + half] = val[:, :half - main]
                ref[:, base + half + main:base + 2 * half] = val[:, half - main:]

        def get(ref, base, p):
            if p is not None:
                return ref[:, base + p * half:base + p * half + main]
            return jnp.concatenate([ref[:, base + main:base + half], ref[:, base + half + main:base + 2 * half]], axis=1)

        x1v = x1_ref[...]
        r2 = _rms_stats(x1v)
        h = (x1v * r2 * gpf_ref[...]).astype(BF16)
        h_ref[...] = h
        y = jnp.zeros((tT, D_MODEL), F32)
        for p in pieces:
            gate = _dot(h[:, :kh], w_in_cols(top_ref, 0, p)) + _dot(h[:, kh:], w_in_cols(bot_ref, 0, p))
            up = _dot(h[:, :kh], w_in_cols(top_ref, 2, p)) + _dot(h[:, kh:], w_in_cols(bot_ref, 2, p))
            put(gu_scr, 0, p, gate)
            put(gu_scr, D_FF, p, up)
            f = (gate * _sigmoid(gate) * up).astype(BF16)
            put(f_ref, 0, p, f)
            y = y + _dot(f, w_out_rows(p))
        r3 = _rms_stats(y)
        x2 = x1v + y * r3 * gpo_ref[...]
        err = x2 - t_ref[...]
        loss_ref[...] += jnp.sum(err * err) * (0.5 / D_MODEL)
        dx2 = err * (1.0 / D_MODEL)
        dy, dg = _rms_bwd(dx2, y, r3, gpo_ref[...])
        dgpo_ref[...] += jnp.sum(dg, axis=0, keepdims=True)
        dyb = dy.astype(BF16)
        dy_ref[...] = dyb
        dh_top = jnp.zeros((tT, kh), F32)
        dh_bot = jnp.zeros((tT, kh), F32)
        for p in pieces:
            df = _dot(dyb, w_out_rows(p), _NT)
            gate = get(gu_scr, 0, p)
            up = get(gu_scr, D_FF, p)
            sg = _sigmoid(gate)
            dgate = (df * up * (sg * (1.0 + gate * (1.0 - sg)))).astype(BF16)
            dup = (df * (gate * sg)).astype(BF16)
            put(dgu_ref, 0, p, dgate)
            put(dgu_ref, D_FF, p, dup)
            dh_top = dh_top + _dot(dgate, w_in_cols(top_ref, 0, p), _NT) + _dot(dup, w_in_cols(top_ref, 2, p), _NT)
            dh_bot = dh_bot + _dot(dgate, w_in_cols(bot_ref, 0, p), _NT) + _dot(dup, w_in_cols(bot_ref, 2, p), _NT)
        dh = jnp.concatenate([dh_top, dh_bot], axis=1)
        dx1n, dg2 = _rms_bwd(dh, x1v, r2, gpf_ref[...])
        dgpf_ref[...] += jnp.sum(dg2, axis=0, keepdims=True)
        dx1_ref[...] = dx2 + dx1n

    row = lambda w: pl.BlockSpec((tT, w), lambda i: (i, 0))
    vec = pl.BlockSpec((1, D_MODEL), lambda i: (0, 0))
    res, _ = _call(
        body, name="ffn_fwd_bwd", grid=(T // tT,),
        in_specs=[row(D_MODEL), row(D_MODEL), _whole(), _whole(), _whole(), vec, vec],
        out_specs=[row(D_MODEL), row(D_FF), row(2 * D_FF), row(D_MODEL), row(D_MODEL),
                   pl.BlockSpec((1, LANES), lambda i: (0, 0)), vec, vec],
        out_shape=[_sds((T, D_MODEL), BF16), _sds((T, D_FF), BF16), _sds((T, 2 * D_FF), BF16), _sds((T, D_MODEL), BF16),
                   _sds((T, D_MODEL), F32), _sds((1, LANES), F32), _sds((1, D_MODEL), F32), _sds((1, D_MODEL), F32)],
        scratch_shapes=[pltpu.VMEM((tT, 2 * D_FF), F32)], args=(x1, tgt, w_fi_top, w_fi_bot, w_fo, g_pf, g_po))
    return res


def _merge_sgu_bwd(dx1, mix, proj, zg, zs, w_bg, w_bs, w_o, g_pm, ln_g, ln_b, w_sp, b_sp_t, job=None):
    T = dx1.shape[0]
    tT = _row_tile(T, 256)
    nb = tT // SGU_BLOCK

    def body(dx1_ref, mix_ref, su_ref, sv_ref, gg_ref, gs_ref, zg_ref, zs_ref, wbg_ref, wbs_ref, wo_ref, g_ref,
             lg_ref, lb_ref, w_ref, b_ref,
             dmix_ref, dzg_ref, dzs_ref, dgate_ref, dyg_ref, dp_ref, dgpm_ref, dw_ref, dbt_ref, dlg_ref, dlb_ref):
        @pl.when(pl.program_id(0) == 0)
        def _():
            for ref in (dgpm_ref, dw_ref, dbt_ref, dlg_ref, dlb_ref):
                ref[...] = jnp.zeros_like(ref)

        mix = mix_ref[...].astype(F32)
        dmix, dg = _rms_bwd(dx1_ref[...], mix, _rms_stats(mix), g_ref[...])
        dgpm_ref[...] += jnp.sum(dg, axis=0, keepdims=True)
        dmb = dmix.astype(BF16)
        dmix_ref[...] = dmb
        dmerged = _dot(dmb, wo_ref[...], _NT)
        dys = None
        for k, (gate_ref, z_ref, w_br_ref, dz_ref) in enumerate(((gg_ref, zg_ref, wbg_ref, dzg_ref),
                                                                 (gs_ref, zs_ref, wbs_ref, dzs_ref))):
            sg = _sigmoid(gate_ref[...].astype(F32))
            dz = (dmerged * sg).astype(BF16)
            dz_ref[...] = dz
            dgate_ref[:, k * 1024:(k + 1) * 1024] = (dmerged * z_ref[...].astype(F32) * (sg * (1.0 - sg))).astype(BF16)
            dy_branch = _dot(dz, w_br_ref[...], _NT)
            if k == 0:
                dyg_ref[...] = dy_branch.astype(BF16)
            else:
                dys = dy_branch

        mask = _sgu_mask()
        lane = lax.broadcasted_iota(jnp.int32, (SGU_BLOCK, LANES), 1)
        for g in range(SGU_GROUPS):
            gc = slice(g * SGU_DG, (g + 1) * SGU_DG)
            gc_v = slice(1024 + g * SGU_DG, 1024 + (g + 1) * SGU_DG)
            wm = jnp.where(mask, w_ref[g], 0.0).astype(BF16)
            vf, dvf_dsv = _gelu_and_grad(sv_ref[:, gc].astype(F32))
            mu = jnp.mean(vf, axis=-1, keepdims=True)
            vc = vf - mu
            rstd = lax.rsqrt(jnp.mean(vc * vc, axis=-1, keepdims=True) + EPS)
            vhat = vc * rstd
            vn = (vhat * lg_ref[:, gc] + lb_ref[:, gc]).astype(BF16)
            u, du_dsu = _gelu_and_grad(su_ref[:, gc].astype(F32))
            dy = dys[:, gc]
            dmixed = (dy * u).astype(BF16)
            dvn_parts = []
            dw_acc = jnp.zeros((SGU_BLOCK, SGU_BLOCK), F32)
            db_acc = jnp.zeros((SGU_BLOCK, 1), F32)
            for b in range(nb):
                rows = slice(b * SGU_BLOCK, (b + 1) * SGU_BLOCK)
                mixed = _dot(wm, vn[rows, :]) + b_ref[:, g:g + 1]
                dp_ref[rows, gc] = (dy[rows, :] * mixed * du_dsu[rows, :]).astype(BF16)
                dvn_parts.append(_dot(wm, dmixed[rows, :], _TN))
                dw_acc = dw_acc + _dot(dmixed[rows, :], vn[rows, :], _NT)
                db_acc = db_acc + jnp.sum(dmixed[rows, :].astype(F32), axis=-1, keepdims=True)
            dw_ref[g] += jnp.where(mask, dw_acc, 0.0)
            dbt_ref[...] += jnp.where(lane == g, db_acc, 0.0)
            dvn = jnp.concatenate(dvn_parts, axis=0)
            dlg_ref[:, gc] += jnp.sum(dvn * vhat, axis=0, keepdims=True)
            dlb_ref[:, gc] += jnp.sum(dvn, axis=0, keepdims=True)
            dvh = dvn * lg_ref[:, gc]
            dvf = rstd * (dvh - jnp.mean(dvh, axis=-1, keepdims=True)
                          - vhat * jnp.mean(dvh * vhat, axis=-1, keepdims=True))
            dp_ref[:, gc_v] = (dvf * dvf_dsv).astype(BF16)

    row = pl.BlockSpec((tT, D_MODEL), lambda i: (i, 0))
    blk = lambda j: pl.BlockSpec((tT, 1024), lambda i: (i, j))
    vec = pl.BlockSpec((1, D_MODEL), lambda i: (0, 0))
    wide = lambda w: pl.BlockSpec((tT, w), lambda i: (i, 0))
    sds = _sds((T, D_MODEL), BF16)
    return _call(
        body, name="merge_sgu_bwd", grid=(T // tT,),
        in_specs=[row, row, blk(3), blk(4), blk(5), blk(6), row, row] + [_whole()] * 3 + [vec] + [_whole()] * 4,
        out_specs=[row, row, row, wide(W_MRG), row, wide(W_SGU), vec,
                   pl.BlockSpec((SGU_GROUPS, SGU_BLOCK, SGU_BLOCK), lambda i: (0, 0, 0)),
                   pl.BlockSpec((SGU_BLOCK, LANES), lambda i: (0, 0)), vec, vec],
        out_shape=[sds, sds, sds, _sds((T, W_MRG), BF16), sds, _sds((T, W_SGU), BF16), _sds((1, D_MODEL), F32),
                   _sds((SGU_GROUPS, SGU_BLOCK, SGU_BLOCK), F32), _sds((SGU_BLOCK, LANES), F32),
                   _sds((1, 1024), F32), _sds((1, 1024), F32)],
        args=(dx1, mix, proj, proj, proj, proj, zg, zs, w_bg, w_bs, w_o, g_pm, ln_g, ln_b, w_sp, b_sp_t), job=job)


def _gla_bwd(proj, alow, wgu, b_gate, gn, states, dy_gla, job=None):
    T = proj.shape[0]
    tT = _row_tile(T, 512)
    nc = tT // CHUNK
    nt = T // tT

    def body(q_ref, k_ref, v_ref, r_ref, al_ref, wgu_ref, bg_ref, gn_ref, later_ref, earlier_ref, st_ref, sp_ref, dy_ref,
             dp_ref, dal_ref, dgn_ref, dbg_ref, dwgu_ref, g_scr, dd_scr, dt_scr):
        step = pl.program_id(0)

        @pl.when(step == 0)
        def _():
            g_scr[...] = jnp.zeros_like(g_scr)
            dgn_ref[...] = jnp.zeros_like(dgn_ref)
            dbg_ref[...] = jnp.zeros_like(dbg_ref)
            dwgu_ref[...] = jnp.zeros_like(dwgu_ref)

        has_prev = jnp.where(step == nt - 1, 0.0, 1.0)
        logit, la, delta = _gla_decay_terms(al_ref, wgu_ref, bg_ref, later_ref)
        e = jnp.exp(delta)
        kdec_f = k_ref[...].astype(F32) * e
        kdec = kdec_f.astype(BF16)
        heads = range(GLA_HEADS)
        kcs = [slice(h * GLA_DK, (h + 1) * GLA_DK) for h in heads]
        vcs = [slice(h * GLA_DV, (h + 1) * GLA_DV) for h in heads]
        carry = [g_scr[h] for h in heads]
        dgn_acc = [jnp.zeros((1, GLA_DV), F32) for _ in heads]
        for c in reversed(range(nc)):
            rows = slice(c * CHUNK, (c + 1) * CHUNK)
            first = slice(c * CHUNK, c * CHUNK + 1)
            dec = jnp.exp(la[first, :] + delta[first, :])
            s_b = [st_ref[c, h].astype(BF16) for h in heads]
            qs = [(q_ref[rows, kcs[h]].astype(F32) * (GLA_DK ** -0.5)).astype(BF16) for h in heads]
            o = [_dot(qs[h], s_b[h], _NT) for h in heads]
            do = []
            for h in heads:
                rstd = _rms_stats(o[h])
                ohat = o[h] * rstd
                gnh = gn_ref[:, vcs[h]]
                dy = dy_ref[rows, vcs[h]].astype(F32)
                rr = r_ref[rows, vcs[h]].astype(F32)
                sg = _sigmoid(rr)
                don = dy * (rr * sg)
                dp_ref[rows, OFF_R + h * GLA_DV:OFF_R + (h + 1) * GLA_DV] = (
                    dy * (ohat * gnh) * (sg * (1.0 + rr * (1.0 - sg)))).astype(BF16)
                dgn_acc[h] = dgn_acc[h] + jnp.sum(don * ohat, axis=0, keepdims=True)
                dn = don * gnh
                do.append((rstd * (dn - ohat * jnp.mean(dn * ohat, axis=-1, keepdims=True))).astype(BF16))
            dq = [_dot(do[h], s_b[h]) for h in heads]
            g_t = [_dot(do[h], qs[h], _TN) + carry[h] for h in heads]
            g_b = [g_t[h].astype(BF16) for h in heads]
            dv = [_dot(kdec[rows, kcs[h]], g_b[h], _NT) for h in heads]
            dkdec = [_dot(v_ref[rows, vcs[h]], g_b[h]) for h in heads]
            for h in heads:
                s_prev = st_ref[c - 1, h] if c > 0 else sp_ref[0, h] * has_prev
                ddec = jnp.sum(g_t[h] * s_prev, axis=0, keepdims=True)
                carry[h] = g_t[h] * dec[:, kcs[h]]
                dp_ref[rows, OFF_Q + h * GLA_DK:OFF_Q + (h + 1) * GLA_DK] = (dq[h] * (GLA_DK ** -0.5)).astype(BF16)
                dp_ref[rows, OFF_V + h * GLA_DV:OFF_V + (h + 1) * GLA_DV] = dv[h].astype(BF16)
                dp_ref[rows, OFF_K + h * GLA_DK:OFF_K + (h + 1) * GLA_DK] = (dkdec[h] * e[rows, kcs[h]]).astype(BF16)
                dd_scr[rows, kcs[h]] = dkdec[h] * kdec_f[rows, kcs[h]]
                dt_scr[rows, kcs[h]] = jnp.broadcast_to(ddec * dec[:, kcs[h]], (CHUNK, GLA_DK))
        for h in heads:
            g_scr[h] = carry[h]
            dgn_ref[:, vcs[h]] += dgn_acc[h]
        dla = _dot_exact_lhs(earlier_ref[...], dd_scr[...]) + dt_scr[...]
        dlogit = dla * (1.0 / GLA_TAU) * _sigmoid(-logit)
        dbg_ref[...] += jnp.sum(dlogit, axis=0, keepdims=True)
        dwgu_ref[...] += _dot_bf16(al_ref[...], dlogit, _TN)
        dal_ref[...] = _dot_bf16(dlogit, wgu_ref[...], _NT).astype(BF16)

    rev = lambda i: nt - 1 - i
    blk = lambda w, j: pl.BlockSpec((tT, w), lambda i: (rev(i), j))
    st_blk = pl.BlockSpec((nc, GLA_HEADS, GLA_DV, GLA_DK), lambda i: (rev(i), 0, 0, 0))
    sp_blk = pl.BlockSpec((1, GLA_HEADS, GLA_DV, GLA_DK), lambda i: (jnp.maximum(rev(i) * nc - 1, 0), 0, 0, 0))
    return _call(
        body, name="gla_bwd", grid=(nt,),
        in_specs=[blk(512, 0), blk(512, 1), blk(1024, 1), blk(1024, 2), blk(LANES, 0)] + [_whole()] * 5
        + [st_blk, sp_blk, blk(GLA_V, 0)],
        out_specs=[blk(W_GLA, 0), blk(LANES, 0), pl.BlockSpec((1, GLA_V), lambda i: (0, 0)),
                   pl.BlockSpec((1, GLA_QK), lambda i: (0, 0)), pl.BlockSpec((LANES, GLA_QK), lambda i: (0, 0))],
        out_shape=[_sds((T, W_GLA), BF16), _sds((T, LANES), BF16), _sds((1, GLA_V), F32), _sds((1, GLA_QK), F32),
                   _sds((LANES, GLA_QK), F32)],
        scratch_shapes=[pltpu.VMEM((GLA_HEADS, GLA_DV, GLA_DK), F32), pltpu.VMEM((tT, GLA_QK), F32),
                        pltpu.VMEM((tT, GLA_QK), F32)],
        args=(proj, proj, proj, proj, alow, wgu, b_gate, gn, _chunk_masks(tT, upper=True), _chunk_masks(tT, upper=False),
              states, states, dy_gla), job=job)


def _inproj_bwd(x, dx1, g1, w_all, dparts, job=None):
    T = x.shape[0]
    tT = _row_tile(T, 512)
    offs = (0, W_GLA, W_GLA + W_SGU, N_MAIN)

    def body(x_ref, dx1_ref, g_ref, w_hbm, *rest):
        part_refs, (dx_ref, dg_ref, w_ref, w_sems) = rest[:len(offs)], rest[len(offs):]

        def compute(first):
            if first:
                copies = [pltpu.make_async_copy(w_hbm.at[:, pl.ds(off, p.shape[1])], w_ref.at[:, pl.ds(off, p.shape[1])],
                                                w_sems.at[k]) for k, (off, p) in enumerate(zip(offs, dparts))]
                for cp in copies:
                    cp.start()
            da = jnp.zeros((tT, D_MODEL), F32)
            for k, (off, p_ref) in enumerate(zip(offs, part_refs)):
                if first:
                    copies[k].wait()
                da = da + _dot(p_ref[...], w_ref[:, off:off + p_ref.shape[1]], _NT)
            xv = x_ref[...]
            dx, dg = _rms_bwd(da, xv, _rms_stats(xv), g_ref[...])
            dg_sum = jnp.sum(dg, axis=0, keepdims=True)
            dg_ref[...] = dg_sum if first else dg_ref[...] + dg_sum
            dx_ref[...] = dx1_ref[...] + dx

        first_step = pl.program_id(0) == 0
        pl.when(first_step)(lambda: compute(True))
        pl.when(jnp.logical_not(first_step))(lambda: compute(False))

    row = lambda w: pl.BlockSpec((tT, w), lambda i: (i, 0))
    vec = pl.BlockSpec((1, D_MODEL), lambda i: (0, 0))
    return _call(
        body, name="inproj_bwd", grid=(T // tT,),
        in_specs=[row(D_MODEL), row(D_MODEL), vec, pl.BlockSpec(memory_space=pl.ANY)] + [row(p.shape[1]) for p in dparts],
        out_specs=[row(D_MODEL), vec], out_shape=[_sds((T, D_MODEL), F32), _sds((1, D_MODEL), F32)],
        scratch_shapes=[pltpu.VMEM(w_all.shape, BF16), pltpu.SemaphoreType.DMA((len(offs),))],
        args=(x, dx1, g1, w_all, *dparts), job=job)


def _tn_matmul(a, b, name, job=None):
    T, M = a.shape
    N = b.shape[1]
    tk = _row_tile(T, 1024)
    tm = M if M <= 1024 else 1408
    tn = N // 2 if N > 2048 else N
    assert M % tm == 0 and N % tn == 0

    def body(a_ref, b_ref, o_ref):
        @pl.when(pl.program_id(2) == 0)
        def _():
            o_ref[...] = _dot(a_ref[...], b_ref[...], _TN)

        @pl.when(pl.program_id(2) > 0)
        def _():
            o_ref[...] += _dot(a_ref[...], b_ref[...], _TN)

    res, jres = _call(
        body, name=name, grid=(M // tm, N // tn, T // tk),
        in_specs=[pl.BlockSpec((tk, tm), lambda i, j, k: (k, i)), pl.BlockSpec((tk, tn), lambda i, j, k: (k, j))],
        out_specs=[pl.BlockSpec((tm, tn), lambda i, j, k: (i, j))], out_shape=[_sds((M, N), F32)], args=(a, b), job=job)
    return res[0], jres


def _pad_rows(a, rows=8):
    return jnp.pad(a, ((0, rows - a.shape[0]), (0, LANES - a.shape[1])))


def _halves_view(dw):
    r = dw.shape[0] // N_CHIPS
    return dw.reshape(N_CHIPS, 2, r // 2, dw.shape[1])


def kernel(x, norm_pre_mix, w_in, w_gate_up, b_gate, gla_norm, sgu_ln_g, sgu_ln_b, w_spatial, b_spatial, w_branch_gla, w_branch_sgu, w_out, norm_post_mix, norm_pre_ffn, w_ffn_in, w_ffn_out, norm_post_ffn, loss_target, m_norm_pre_mix, m_w_in, m_w_gate_up, m_b_gate, m_gla_norm, m_sgu_ln_g, m_sgu_ln_b, m_w_spatial, m_b_spatial, m_w_branch_gla, m_w_branch_sgu, m_w_out, m_norm_post_mix, m_norm_pre_ffn, m_w_ffn_in, m_w_ffn_out, m_norm_post_ffn, v_norm_pre_mix, v_w_in, v_w_gate_up, v_b_gate, v_gla_norm, v_sgu_ln_g, v_sgu_ln_b, v_w_spatial, v_b_spatial, v_w_branch_gla, v_w_branch_sgu, v_w_out, v_norm_post_mix, v_norm_pre_ffn, v_w_ffn_in, v_w_ffn_out, v_norm_post_ffn):
    chip = 2 * lax.axis_index("x") + lax.axis_index("y")
    xt, tgt = x[0], loss_target[0]

    tiny = jnp.concatenate([w_gate_up[0], _pad_rows(gla_norm[0]), _pad_rows(sgu_ln_g[0]), _pad_rows(sgu_ln_b[0]),
                            jnp.zeros((24, LANES), F32)], axis=0)

    def with_own(gathered, own):
        return lax.dynamic_update_slice(gathered, own[None], (chip, 0, 0))

    w_in_t, m_in_t, v_in_t = w_in[0].T, m_w_in[0].T, v_w_in[0].T
    w_in_b = _transposed_cast(w_in_t)
    (*own_rows, fi_top, fi_bot), (g_in, g_tiny) = _cast_weights(
        [w_branch_gla[0], w_branch_sgu[0], w_out[0], w_ffn_out[0]], w_ffn_in[0], job=_job_gather([w_in_b, tiny]))
    g_tiny = with_own(g_tiny, tiny)
    w_all = _relayout_w_in(with_own(g_in, w_in_b))
    cols = lambda a: a.transpose(1, 0, 2).reshape(a.shape[1], N_CHIPS * a.shape[2])
    wgu = jnp.pad(cols(g_tiny[:, 0:16]), ((0, LANES - GLA_RANK), (0, 0)))
    gn = cols(g_tiny[:, 16:20, :64]).reshape(1, GLA_V)
    ln_g = cols(g_tiny[:, 24:28, :64]).reshape(1, 1024)
    ln_b = cols(g_tiny[:, 32:36, :64]).reshape(1, 1024)
    b_sp_t = jnp.pad(b_spatial[0].T, ((0, 0), (0, LANES - SGU_GROUPS)))
    w_sp = w_spatial[0]

    (a, proj, alow), g_rows = _inproj_fwd(xt, norm_pre_mix, w_all, job=_job_gather(own_rows))
    rows = lambda g: g.reshape(N_CHIPS * g.shape[1], g.shape[2])
    w_bg, w_bs, w_o, w_fo = [rows(with_own(g, own)) for g, own in zip(g_rows, own_rows)]
    (y_gla, states), (g_top,) = _gla_fwd(proj, alow, wgu, b_gate, gn, job=_job_gather([fi_top]))
    (y_sgu, zg, zs, merged, mix, x1), (g_bot,) = _sgu_merge_fwd(
        xt, proj, y_gla, ln_g, ln_b, w_sp, b_sp_t, w_bg, w_bs, w_o, norm_post_mix, job=_job_gather([fi_bot]))
    h, f, dgu, dy, dx1, loss, d_gpf, d_gpo = _ffn_fwd_bwd(x1, tgt, with_own(g_top, fi_top), with_own(g_bot, fi_bot),
                                                          w_fo, norm_pre_ffn, norm_post_ffn)

    whole = lambda hs: [[(h_, None)] for h_ in hs]
    dw_fo, _ = _tn_matmul(f, dy, "dw_ffn_out")
    dw_fo4 = _halves_view(dw_fo)
    dw_fi, (q_fo,) = _tn_matmul(h, dgu, "dw_ffn_in", job=_job_to_other_core([[(dw_fo4, 0)]]))
    c_fo, = _presum([dw_fo4], [q_fo], "presum_ffn_out")
    (dmix, dzg, dzs, dp_mrg, dyg, dp_sgu, d_gpm, d_wsp, d_bsp_t, d_lng, d_lnb), (s_fo, q_fi) = _merge_sgu_bwd(
        dx1, mix, proj, zg, zs, w_bg, w_bs, w_o, norm_post_mix, ln_g, ln_b, w_sp, b_sp_t,
        job=_join(_job_scatter([c_fo]), _job_to_other_core([[(dw_fi, 0)]])))
    c_fi, = _presum([dw_fi], [q_fi], "presum_ffn_in")
    dw_c, _ = _tn_matmul(a, dp_mrg, "dw_in_merge")
    dw_b, _ = _tn_matmul(a, dp_sgu, "dw_in_sgu")
    dw_o4 = _halves_view(_tn_matmul(merged, dmix, "dw_out")[0])
    dw_bg4 = _halves_view(_tn_matmul(y_gla, dzg, "dw_branch_gla")[0])
    dw_bs4 = _halves_view(_tn_matmul(y_sgu, dzs, "dw_branch_sgu")[0])
    h_fo, = _sum_slots([c_fo], [s_fo], "sum_ffn_out")
    (dp_gla, dal, d_gn, d_bg, d_wgu), (s_fi, t_fo, q_b, q_c, q_o, q_bg, q_bs) = _gla_bwd(
        proj, alow, wgu, b_gate, gn, states, dyg,
        job=_join(_job_scatter([c_fi]), _job_to_other_core(
            whole([h_fo]) + [[(dw_b, 0)], [(dw_c, 0)], [(dw_o4, 0)], [(dw_bg4, 0)], [(dw_bs4, 0)]])))
    c_o, c_bg, c_bs = _presum([dw_o4, dw_bg4, dw_bs4], [q_o, q_bg, q_bs], "presum_out_branches")
    h_fi, = _sum_slots([c_fi], [s_fi], "sum_ffn_in")
    dw_d, _ = _tn_matmul(a, dal, "dw_in_gate")
    dw_a, (s_o, s_bg, s_bs, t_fi, q_d) = _tn_matmul(
        a, dp_gla, "dw_in_gla",
        job=_join(_job_scatter([c_o, c_bg, c_bs]), _job_to_other_core(whole([h_fi]) + [[(dw_d, 0)]])))
    h_o, h_bg, h_bs = _sum_slots([c_o, c_bg, c_bs], [s_o, s_bg, s_bs], "sum_out_branches")

    grads, deltas, new_m, new_v = {}, {}, {}, {}

    def update(call, names, ws, ms, vs, g_mine, g_theirs, job=None):
        res, jres = _adamw([w[0] for w in ws], [m[0] for m in ms], [v[0] for v in vs], g_mine, g_theirs, call, job=job)
        for name, (g, d, m2, v2) in zip(names, res):
            grads[name], deltas[name], new_m[name], new_v[name] = g[None], d[None], m2[None], v2[None]
        return jres

    dw_in = [(dw_a, 0), (dw_b, W_GLA), (dw_c, W_GLA + W_SGU), (dw_d, N_MAIN)]
    q_a, t_o, t_bg, t_bs = update("adamw_w_ffn_out", ["w_ffn_out"], [w_ffn_out], [m_w_ffn_out], [v_w_ffn_out],
                                  [[h_fo]], [[t_fo]],
                                  job=_job_to_other_core([[(dw_a, 0)]] + whole([h_o, h_bg, h_bs])))
    q_in = [q_a, q_b, q_c, q_d]
    hr_in = D_MODEL // 2
    c_in_a, _ = _presum_w_in(dw_in, q_in, 0, hr_in // 8, "presum_w_in_a")
    c_in_b, (s_in_a,) = _presum_w_in(dw_in, q_in, hr_in // 8, 7 * hr_in // 8, "presum_w_in_b",
                                     job=_job_scatter([c_in_a]))
    update("adamw_w_ffn_in", ["w_ffn_in"], [w_ffn_in], [m_w_ffn_in], [v_w_ffn_in], [[h_fi]], [[t_fi]])
    update("adamw_out_branches", ["w_out", "w_branch_gla", "w_branch_sgu"], [w_out, w_branch_gla, w_branch_sgu],
           [m_w_out, m_w_branch_gla, m_w_branch_sgu], [v_w_out, v_w_branch_gla, v_w_branch_sgu],
           [[h_o], [h_bg], [h_bs]], [[t_o], [t_bg], [t_bs]])
    (grad_x, d_g1), (s_in_b,) = _inproj_bwd(xt, dx1, norm_pre_mix, w_all, (dp_gla, dp_sgu, dp_mrg, dal),
                                            job=_job_scatter([c_in_b]))
    h_in = _sum_slots([c_in_a], [s_in_a], "sum_w_in_a") + _sum_slots([c_in_b], [s_in_b], "sum_w_in_b")
    t_in = _run_job(_job_to_other_core(whole(h_in)), "swap_w_in")
    for store, val in zip((grads, deltas, new_m, new_v),
                          _adamw_transposed(w_in_t, m_in_t, v_in_t, h_in, t_in, "adamw_w_in")):
        store["w_in"] = val.T[None]

    small_names = ["w_spatial", "w_gate_up", "norm_pre_mix", "norm_post_mix", "norm_pre_ffn", "norm_post_ffn", "b_gate",
                   "b_spatial", "gla_norm", "sgu_ln_g", "sgu_ln_b"]
    loss_out, small = _small_adamw(
        _small_sum([d_wsp, d_wgu, d_g1, d_gpm, d_gpf, d_gpo, d_bg, d_bsp_t, d_gn, d_lng, d_lnb, loss]),
        [w_spatial, w_gate_up, norm_pre_mix, norm_post_mix, norm_pre_ffn, norm_post_ffn, b_gate, b_spatial, gla_norm,
         sgu_ln_g, sgu_ln_b],
        [m_w_spatial, m_w_gate_up, m_norm_pre_mix, m_norm_post_mix, m_norm_pre_ffn, m_norm_post_ffn, m_b_gate,
         m_b_spatial, m_gla_norm, m_sgu_ln_g, m_sgu_ln_b],
        [v_w_spatial, v_w_gate_up, v_norm_pre_mix, v_norm_post_mix, v_norm_pre_ffn, v_norm_post_ffn, v_b_gate,
         v_b_spatial, v_gla_norm, v_sgu_ln_g, v_sgu_ln_b])
    for store, vals in zip((grads, deltas, new_m, new_v), small):
        store.update(zip(small_names, vals))

    order = ["norm_pre_mix", "w_in", "w_gate_up", "b_gate", "gla_norm", "sgu_ln_g", "sgu_ln_b", "w_spatial", "b_spatial",
             "w_branch_gla", "w_branch_sgu", "w_out", "norm_post_mix", "norm_pre_ffn", "w_ffn_in", "w_ffn_out",
             "norm_post_ffn"]
    out = [loss_out, grad_x[None]]
    for store in (grads, deltas, new_m, new_v):
        out.extend(store[n] for n in order)
    return tuple(out)
```

```python
import jax
import jax.numpy as jnp
from jax import lax
from jax.experimental import pallas as pl
from jax.experimental.pallas import tpu as pltpu

F32 = jnp.float32
BF16 = jnp.bfloat16

D_MODEL = 1024
GLA_HEADS = 4
GLA_DK = 128
GLA_DV = 256
GLA_QK = GLA_HEADS * GLA_DK
GLA_V = GLA_HEADS * GLA_DV
GLA_RANK = 16
GLA_TAU = 16.0
CHUNK = 64
SGU_GROUPS = 4
SGU_BLOCK = 128
SGU_DG = 256
D_FF = 2816
EPS = 1e-6
LANES = 128

OFF_Q, OFF_K, OFF_V, OFF_R, OFF_SU, OFF_SV, OFF_GG, OFF_GS, OFF_AL = 0, 512, 1024, 2048, 3072, 4096, 5120, 6144, 7168
W_GLA, W_SGU, W_MRG = 3072, 2048, 2048
N_MAIN = 7168
N_ALL = N_MAIN + LANES
_IN_SPLITS = (GLA_QK, GLA_QK, GLA_V, GLA_V, GLA_RANK, 1024, 1024, 1024, 1024)
_IN_STARTS = tuple(sum(_IN_SPLITS[:i]) for i in range(len(_IN_SPLITS) + 1))
_IN_DST = (OFF_Q, OFF_K, OFF_V, OFF_R, OFF_AL, OFF_SU, OFF_SV, OFF_GG, OFF_GS)
D_IN = _IN_STARTS[-1]

ADAM_LR = 0.001
ADAM_B1 = 0.9
ADAM_B2 = 0.999
ADAM_EPS = 1e-08
ADAM_WD = 0.01
ADAM_STEP = 10

VMEM_LIMIT_BYTES = 56 * 1024 * 1024
N_CHIPS = 4
N_PEER = N_CHIPS - 1
N_DEV = 8
MESH = pl.DeviceIdType.MESH

_NN = (((1,), (0,)), ((), ()))
_NT = (((1,), (1,)), ((), ()))
_TN = (((0,), (0,)), ((), ()))


def _dot(a, b, dims=_NN):
    return lax.dot_general(a, b, dims, preferred_element_type=F32)


def _split(x):
    hi = x.astype(BF16)
    lo = (x - hi.astype(F32)).astype(BF16)
    return hi, lo


def _dot_bf16(a, b, dims=_NN):
    return _dot(a.astype(BF16), b.astype(BF16), dims)


def _dot_exact_lhs(m, x):
    xh, xl = _split(x)
    return _dot(m, xh) + _dot(m, xl)


def _sigmoid(x):
    return 0.5 * jnp.tanh(0.5 * x) + 0.5


def _log_sigmoid(x):
    return jnp.minimum(x, 0.0) - jnp.log(1.0 + jnp.exp(-jnp.abs(x)))


_GELU_C = 0.7978845608028654
_GELU_A = 0.044715


def _gelu_and_grad(x):
    x2 = x * x
    t = jnp.tanh(_GELU_C * (x + _GELU_A * x * x2))
    g = 0.5 * x * (1.0 + t)
    dg = 0.5 * (1.0 + t) + 0.5 * x * (1.0 - t * t) * (_GELU_C * (1.0 + 3.0 * _GELU_A * x2))
    return g, dg


def _gelu(x):
    t = jnp.tanh(_GELU_C * (x + _GELU_A * x * x * x))
    return 0.5 * x * (1.0 + t)


def _rms_stats(x):
    return lax.rsqrt(jnp.mean(x * x, axis=-1, keepdims=True) + EPS)


def _rms_bwd(dout, y, r, g):
    yhat = y * r
    dn = dout * g
    dy = r * (dn - yhat * jnp.mean(dn * yhat, axis=-1, keepdims=True))
    return dy, dout * yhat


def _whole():
    return pl.BlockSpec(memory_space=pltpu.VMEM)


def _row_tile(T, want):
    t = min(T, want)
    assert T % t == 0
    return t


def _chunk_masks(tT, upper):
    row = lax.broadcasted_iota(jnp.int32, (tT, tT), 0)
    col = lax.broadcasted_iota(jnp.int32, (tT, tT), 1)
    same = (row // CHUNK) == (col // CHUNK)
    tri = (col > row) if upper else (col < row)
    return jnp.where(same & tri, 1.0, 0.0).astype(BF16)


class _Job:
    def __init__(self, ins, out_shapes, scratch, start, finish, mid=None):
        self.ins, self.out_shapes, self.scratch = list(ins), list(out_shapes), list(scratch)
        self.start, self.finish, self.mid = start, finish, mid


def _join(*jobs):
    def split(refs, counts):
        out, at = [], 0
        for n in counts:
            out.append(refs[at:at + n])
            at += n
        return out

    ni, no, ns = [len(j.ins) for j in jobs], [len(j.out_shapes) for j in jobs], [len(j.scratch) for j in jobs]

    def start(ins, outs, scr):
        for j, a, b, c in zip(jobs, split(ins, ni), split(outs, no), split(scr, ns)):
            j.start(a, b, c)

    def finish(ins, outs, scr):
        for j, a, b, c in zip(jobs, split(ins, ni), split(outs, no), split(scr, ns)):
            j.finish(a, b, c)

    def mid(ins, outs, scr):
        for j, a, b, c in zip(jobs, split(ins, ni), split(outs, no), split(scr, ns)):
            if j.mid is not None:
                j.mid(a, b, c)

    return _Job(sum((j.ins for j in jobs), []), sum((j.out_shapes for j in jobs), []),
                sum((j.scratch for j in jobs), []), start, finish, mid if any(j.mid for j in jobs) else None)


def _mesh_pos():
    return lax.axis_index("x"), lax.axis_index("y"), lax.axis_index("c")


def _peer_chips(xi, yi):
    return [(1 - xi, yi), (xi, 1 - yi), (1 - xi, 1 - yi)]


def _half(ci, rows):
    return pl.ds(pl.multiple_of(ci * rows, 8), rows)


def _sds(shape, dtype):
    return jax.ShapeDtypeStruct(tuple(shape), dtype)


def _job_gather(arrs):
    n = len(arrs)
    kinds = 12
    Y0, Y1, X1, X0, ON_X, ON_Y, D2D = 0, 1, 2, 3, 4, 5, 6

    def copies(ins, outs, scr):
        send_sems, recv_sems = scr
        xi, yi, ci = _mesh_pos()
        me, cx, cy, cd = 2 * xi + yi, 2 * (1 - xi) + yi, 2 * xi + (1 - yi), 2 * (1 - xi) + (1 - yi)
        to_x, to_y, to_core = (1 - xi, yi, ci), (xi, 1 - yi, ci), (xi, yi, 1 - ci)
        table = []
        for k in range(n):
            qr = arrs[k].shape[0] // 4

            def rows(core, q):
                return pl.ds(pl.multiple_of((2 * core + q) * qr, 8), qr)

            def cp(kind, src, dst, to):
                s = k * kinds + kind
                return pltpu.make_async_remote_copy(src_ref=src, dst_ref=dst, send_sem=send_sems.at[s],
                                                    recv_sem=recv_sems.at[s], device_id=to, device_id_type=MESH)

            def slab(chip, core, q):
                return outs[k].at[chip, rows(core, q)]

            t = {}
            for kind, q, to, frm in ((Y0, 0, to_y, cy), (Y1, 1, to_y, cy), (X1, 1, to_x, cx), (X0, 0, to_x, cx)):
                mine = ins[k].at[rows(ci, q)]
                t[kind] = (cp(kind, mine, slab(me, ci, q), to), cp(kind, mine, slab(frm, ci, q), to))
            t[ON_X] = (cp(ON_X, slab(cy, ci, 0), slab(cy, ci, 0), to_x), cp(ON_X, slab(cy, ci, 0), slab(cd, ci, 0), to_x))
            t[ON_Y] = (cp(ON_Y, slab(cx, ci, 1), slab(cx, ci, 1), to_y), cp(ON_Y, slab(cx, ci, 1), slab(cd, ci, 1), to_y))
            for i, (chip, q) in enumerate(((cy, 0), (cy, 1), (cx, 1), (cx, 0), (cd, 0), (cd, 1))):
                t[D2D + i] = (cp(D2D + i, slab(chip, ci, q), slab(chip, ci, q), to_core),
                              cp(D2D + i, slab(chip, ci, q), slab(chip, 1 - ci, q), to_core))
            table.append(t)
        return table

    def start(ins, outs, scr):
        table = copies(ins, outs, scr)
        for kind in (Y0, X1, Y1, X0):
            for t in table:
                t[kind][0].start()

    def arrived(table, kind, then):
        for t in table:
            t[kind][1].wait_recv()
            for nxt in then:
                t[nxt][0].start()

    def mid(ins, outs, scr):
        table = copies(ins, outs, scr)
        arrived(table, Y0, (ON_X, D2D + 0))
        arrived(table, X1, (ON_Y, D2D + 2))

    def finish(ins, outs, scr):
        table = copies(ins, outs, scr)
        arrived(table, Y1, (D2D + 1,))
        arrived(table, X0, (D2D + 3,))
        arrived(table, ON_X, (D2D + 4,))
        arrived(table, ON_Y, (D2D + 5,))
        for t in table:
            for i in range(6):
                t[D2D + i][1].wait_recv()
            for kind in range(kinds):
                t[kind][0].wait_send()

    dma = pltpu.SemaphoreType.DMA
    return _Job(arrs, [_sds((N_CHIPS,) + a.shape, a.dtype) for a in arrs], [dma((n * kinds,))] * 2, start, finish, mid)


def _job_scatter(parts):
    n = len(parts)

    def copies(ins, outs, scr):
        send_sems, recv_sems = scr
        xi, yi, ci = _mesh_pos()
        res = []
        for k in range(n):
            for j, (px, py) in enumerate(_peer_chips(xi, yi)):
                s = k * N_PEER + j
                res.append(pltpu.make_async_remote_copy(
                    src_ref=ins[k].at[2 * px + py], dst_ref=outs[k].at[j], send_sem=send_sems.at[s],
                    recv_sem=recv_sems.at[s], device_id=(px, py, ci), device_id_type=MESH))
        return res

    def start(ins, outs, scr):
        for cp in copies(ins, outs, scr):
            cp.start()

    def finish(ins, outs, scr):
        for cp in copies(ins, outs, scr):
            cp.wait_recv()
            cp.wait_send()

    dma = pltpu.SemaphoreType.DMA
    return _Job(parts, [_sds((N_PEER,) + p.shape[1:], p.dtype) for p in parts], [dma((n * N_PEER,))] * 2, start, finish)


def _job_to_other_core(groups):
    pieces = [(g, a, off) for g, group in enumerate(groups) for a, off in group]
    n = len(pieces)

    def geometry(group):
        a0, off0 = group[0]
        if off0 is None:
            return a0.shape
        if a0.ndim == 4:
            return (N_CHIPS, a0.shape[2], a0.shape[3])
        return (a0.shape[0] // 2, sum(a.shape[1] for a, _ in group))

    def copies(ins, outs, scr):
        send_sems, recv_sems = scr
        xi, yi, ci = _mesh_pos()
        res = []
        for p, (g, a, off) in enumerate(pieces):
            if off is None:
                give, land = ins[p], outs[g]
            elif a.ndim == 4:
                give, land = ins[p].at[pl.ds(0, N_CHIPS), 1 - ci], outs[g]
            else:
                hr, w = a.shape[0] // 2, a.shape[1]
                give, land = ins[p].at[_half(1 - ci, hr)], outs[g].at[pl.ds(0, hr), pl.ds(off, w)]
            res.append(pltpu.make_async_remote_copy(
                src_ref=give, dst_ref=land, send_sem=send_sems.at[p], recv_sem=recv_sems.at[p],
                device_id=(xi, yi, 1 - ci), device_id_type=MESH))
        return res

    def start(ins, outs, scr):
        for cp in copies(ins, outs, scr):
            cp.start()

    def finish(ins, outs, scr):
        for cp in copies(ins, outs, scr):
            cp.wait_recv()
            cp.wait_send()

    dma = pltpu.SemaphoreType.DMA
    return _Job([a for _, a, _ in pieces], [_sds(geometry(group), group[0][0].dtype) for group in groups],
                [dma((n,))] * 2, start, finish)


def _call(body, *, name, grid, in_specs, out_specs, out_shape, args, scratch_shapes=(), parallel=False, job=None,
          by_core=False):
    n_in, n_out, n_scr = len(in_specs), len(out_specs), len(scratch_shapes)
    hbm = pl.BlockSpec(memory_space=pl.ANY)
    n_ji, n_jo = (len(job.ins), len(job.out_shapes)) if job is not None else (0, 0)
    lead = 1 if by_core else 0

    def kernel_fn(*refs):
        core, refs = refs[:lead], refs[lead:]
        ins, refs = refs[:n_in], refs[n_in:]
        j_ins, refs = refs[:n_ji], refs[n_ji:]
        outs, refs = refs[:n_out], refs[n_out:]
        j_outs, refs = refs[:n_jo], refs[n_jo:]
        scr, j_scr = refs[:n_scr], refs[n_scr:]
        if job is None:
            body(*core, *ins, *outs, *scr)
            return
        ids = [pl.program_id(d) for d in range(len(grid))]
        first = ids[0] == 0
        last = ids[0] == grid[0] - 1
        for d in range(1, len(grid)):
            first = first & (ids[d] == 0)
            last = last & (ids[d] == grid[d] - 1)

        @pl.when(first)
        def _():
            job.start(j_ins, j_outs, j_scr)

        if job.mid is not None and grid[0] >= 4:
            half_way = ids[0] == grid[0] // 2
            for d in range(1, len(grid)):
                half_way = half_way & (ids[d] == 0)

            @pl.when(half_way)
            def _():
                job.mid(j_ins, j_outs, j_scr)

        body(*core, *ins, *outs, *scr)

        @pl.when(last)
        def _():
            if job.mid is not None and grid[0] < 4:
                job.mid(j_ins, j_outs, j_scr)
            job.finish(j_ins, j_outs, j_scr)

    sem = ("parallel" if parallel and job is None else "arbitrary",) * len(grid)
    all_in = list(in_specs) + [hbm] * n_ji
    all_out = list(out_specs) + [hbm] * n_jo
    all_scratch = list(scratch_shapes) + (job.scratch if job is not None else [])
    all_shapes = list(out_shape) + (job.out_shapes if job is not None else [])
    all_args = list(args) + (job.ins if job is not None else [])
    params = pltpu.CompilerParams(dimension_semantics=sem, vmem_limit_bytes=VMEM_LIMIT_BYTES)
    if by_core:
        spec = pltpu.PrefetchScalarGridSpec(num_scalar_prefetch=1, grid=grid, in_specs=all_in, out_specs=all_out,
                                            scratch_shapes=all_scratch)
        core = lax.axis_index("c").astype(jnp.int32).reshape(1)
        res = pl.pallas_call(kernel_fn, name=name, grid_spec=spec, out_shape=all_shapes, compiler_params=params)(
            core, *all_args)
    else:
        res = pl.pallas_call(kernel_fn, name=name, grid=grid, in_specs=all_in, out_specs=all_out, out_shape=all_shapes,
                             scratch_shapes=all_scratch, compiler_params=params)(*all_args)
    return list(res[:n_out]), list(res[n_out:])


def _run_job(job, name):
    n_i, n_o = len(job.ins), len(job.out_shapes)

    def body(*refs):
        ins, outs, scr = refs[:n_i], refs[n_i:n_i + n_o], refs[n_i + n_o:]
        job.start(ins, outs, scr)
        if job.mid is not None:
            job.mid(ins, outs, scr)
        job.finish(ins, outs, scr)

    hbm = pl.BlockSpec(memory_space=pl.ANY)
    return list(pl.pallas_call(body, name=name, in_specs=[hbm] * n_i, out_specs=[hbm] * n_o, out_shape=job.out_shapes,
                               scratch_shapes=job.scratch)(*job.ins))


def _adam_values(w, m, v, g):
    m2 = ADAM_B1 * m + (1.0 - ADAM_B1) * g
    v2 = ADAM_B2 * v + (1.0 - ADAM_B2) * (g * g)
    delta = -ADAM_LR * ((m2 / (1.0 - ADAM_B1 ** ADAM_STEP)) / (jnp.sqrt(v2 / (1.0 - ADAM_B2 ** ADAM_STEP)) + ADAM_EPS)
                        + ADAM_WD * w)
    return delta, m2, v2


_P_WSP, _P_WGU, _P_NORM, _P_BG, _P_BSP, _P_HEAD, _P_LOSS, _P_ROWS = 0, 512, 576, 608, 616, 624, 720, 736


def _small_sum(dgrads):
    hr = _P_ROWS // 2

    def body(dwsp, dwgu, dg1, dgpm, dgpf, dgpo, dbg, dbspt, dgn, dlng, dlnb, loss_in, tot, pack, pair, slots, send_sems,
             recv_sems):
        xi, yi, ci = _mesh_pos()
        chip = 2 * xi + yi

        pack[...] = jnp.zeros_like(pack)
        for g in range(SGU_GROUPS):
            pack[_P_WSP + g * SGU_BLOCK:_P_WSP + (g + 1) * SGU_BLOCK] = dwsp[g]
        for j in range(N_CHIPS):
            pack[_P_WGU + GLA_RANK * j:_P_WGU + GLA_RANK * (j + 1)] = dwgu[0:GLA_RANK, LANES * j:LANES * (j + 1)]
        for k, r in enumerate((dg1, dgpm, dgpf, dgpo)):
            for q in range(8):
                pack[_P_NORM + 8 * k + q:_P_NORM + 8 * k + q + 1] = r[:, LANES * q:LANES * (q + 1)]
        for q in range(4):
            pack[_P_BG + q:_P_BG + q + 1] = dbg[:, LANES * q:LANES * (q + 1)]
        pack[_P_BSP:_P_BSP + SGU_GROUPS] = jnp.transpose(dbspt[...])[0:SGU_GROUPS]
        for k, r in enumerate((dgn, dlng, dlnb)):
            for j in range(N_CHIPS):
                for hh in range(4):
                    row = _P_HEAD + 32 * k + 8 * j + hh
                    pack[row:row + 1, 0:64] = r[:, 256 * hh + 64 * j:256 * hh + 64 * (j + 1)]
        pack[_P_LOSS:_P_LOSS + 1] = loss_in[...]

        sibling = dict(device_id=(xi, yi, 1 - ci), device_id_type=MESH)
        to_sibling = pltpu.make_async_remote_copy(src_ref=pack, dst_ref=pair, send_sem=send_sems.at[N_PEER],
                                                  recv_sem=recv_sems.at[N_PEER], **sibling)
        to_sibling.start()
        to_sibling.wait_recv()
        to_sibling.wait_send()
        pack[...] = pack[...] + pair[...]
        mine = pl.ds(pl.multiple_of(ci * hr, 8), hr)
        theirs = pl.ds(pl.multiple_of((1 - ci) * hr, 8), hr)
        slots[chip] = pack[mine, :]

        def copy(j, slot):
            px, py = _peer_chips(xi, yi)[j]
            return pltpu.make_async_remote_copy(
                src_ref=pack.at[mine], dst_ref=slots.at[slot(2 * px + py)], send_sem=send_sems.at[j],
                recv_sem=recv_sems.at[j], device_id=(px, py, ci), device_id_type=MESH)

        sends = [copy(j, lambda peer_chip: chip) for j in range(N_PEER)]
        for cp in sends:
            cp.start()
        for j in range(N_PEER):
            copy(j, lambda peer_chip: peer_chip).wait_recv()
        for cp in sends:
            cp.wait_send()
        acc = slots[0]
        for d in range(1, N_CHIPS):
            acc = acc + slots[d]
        tot[mine, :] = acc
        half_over = pltpu.make_async_remote_copy(src_ref=tot.at[mine], dst_ref=tot.at[mine], send_sem=send_sems.at[N_PEER + 1],
                                                 recv_sem=recv_sems.at[N_PEER + 1], **sibling)
        half_back = pltpu.make_async_remote_copy(src_ref=tot.at[mine], dst_ref=tot.at[theirs], send_sem=send_sems.at[N_PEER + 1],
                                                 recv_sem=recv_sems.at[N_PEER + 1], **sibling)
        half_over.start()
        half_back.wait_recv()
        half_over.wait_send()

    return pl.pallas_call(
        body, name="small_sum", in_specs=[_whole()] * 12, out_specs=_whole(), out_shape=_sds((_P_ROWS, LANES), F32),
        scratch_shapes=[pltpu.VMEM((_P_ROWS, LANES), F32), pltpu.VMEM((_P_ROWS, LANES), F32),
                        pltpu.VMEM((N_CHIPS, hr, LANES), F32),
                        pltpu.SemaphoreType.DMA((N_PEER + 2,)), pltpu.SemaphoreType.DMA((N_PEER + 2,))],
        compiler_params=pltpu.CompilerParams(vmem_limit_bytes=VMEM_LIMIT_BYTES),
    )(*dgrads)


def _small_adamw(tot, ws, ms, vs):
    n = len(ws)

    def body(*refs):
        tot = refs[0]
        w_refs, m_refs, v_refs = refs[1:1 + n], refs[1 + n:1 + 2 * n], refs[1 + 2 * n:1 + 3 * n]
        loss_out = refs[1 + 3 * n]
        outs = refs[2 + 3 * n:]
        chip = 2 * lax.axis_index("x") + lax.axis_index("y")
        loss_out[...] = tot[_P_LOSS:_P_LOSS + 1, 0:1]

        def step(k, g, pick, put):
            d, m2, v2 = _adam_values(pick(w_refs[k]), pick(m_refs[k]), pick(v_refs[k]), g)
            for o, val in zip((outs[k], outs[n + k], outs[2 * n + k], outs[3 * n + k]), (g, d, m2, v2)):
                put(o, val)

        def whole(ref):
            return ref[0]

        def put_whole(ref, val):
            ref[0] = val

        for g in range(SGU_GROUPS):
            def pick_g(ref, g=g):
                return ref[0, g]

            def put_g(ref, val, g=g):
                ref[0, g] = val

            step(0, tot[_P_WSP + g * SGU_BLOCK:_P_WSP + (g + 1) * SGU_BLOCK], pick_g, put_g)
        step(1, tot[pl.ds(pl.multiple_of(_P_WGU + GLA_RANK * chip, GLA_RANK), GLA_RANK), :], whole, put_whole)
        for k, (base, chunks) in enumerate(((_P_NORM, 8), (_P_NORM + 8, 8), (_P_NORM + 16, 8), (_P_NORM + 24, 8), (_P_BG, 4))):
            for q in range(chunks):
                def pick_q(ref, q=q):
                    return ref[:, LANES * q:LANES * (q + 1)]

                def put_q(ref, val, q=q):
                    ref[:, LANES * q:LANES * (q + 1)] = val

                step(2 + k, tot[base + q:base + q + 1], pick_q, put_q)
        step(7, tot[_P_BSP:_P_BSP + SGU_GROUPS], whole, put_whole)
        for k in range(3):
            mine = tot[pl.ds(pl.multiple_of(_P_HEAD + 32 * k + 8 * chip, 8), 8), :]
            step(8 + k, mine[0:4, 0:64], whole, put_whole)

    shapes = [_sds(w.shape, F32) for w in ws]
    res = pl.pallas_call(
        body, name="small_adamw", in_specs=[_whole()] * (1 + 3 * n), out_specs=[_whole()] * (1 + 4 * n),
        out_shape=[_sds((1, 1), F32)] + shapes * 4,
        compiler_params=pltpu.CompilerParams(vmem_limit_bytes=VMEM_LIMIT_BYTES),
    )(tot, *ws, *ms, *vs)
    return res[0].reshape(()), [list(res[1 + i * n:1 + (i + 1) * n]) for i in range(4)]


def _w_in_pieces():
    blk = D_IN // N_CHIPS
    pieces = []
    for s in range(len(_IN_SPLITS)):
        lo_s, hi_s = _IN_STARTS[s], _IN_STARTS[s + 1]
        for j in range(N_CHIPS):
            lo, hi = max(lo_s, j * blk), min(hi_s, (j + 1) * blk)
            if lo < hi:
                pieces.append((j, lo - j * blk, _IN_DST[s] + lo - lo_s, hi - lo))
    return pieces


def _relayout_w_in(gathered):
    _, rows, blk = gathered.shape
    tr = 256

    def body(g_ref, o_ref):
        o_ref[:, OFF_AL:N_ALL] = jnp.zeros((tr, LANES), BF16)
        for j, src, dst, w in _w_in_pieces():
            o_ref[:, dst:dst + w] = g_ref[j, :, src:src + w]

    res, _ = _call(body, name="relayout_w_in", grid=(rows // tr,), parallel=True,
                   in_specs=[pl.BlockSpec((N_CHIPS, tr, blk), lambda i: (0, i, 0))],
                   out_specs=[pl.BlockSpec((tr, N_ALL), lambda i: (i, 0))],
                   out_shape=[_sds((rows, N_ALL), BF16)], args=(gathered,))
    return res[0]


def _update_row_tile(rows):
    for t in range(min(rows, 256), 7, -8):
        if rows % t == 0:
            return t
    return rows


def _presum_w_in(dws, theirs, row0, rows, name, job=None):
    hr = theirs[0].shape[0]
    blk = D_IN // N_CHIPS
    tr = 64
    assert row0 % tr == 0 and rows % tr == 0
    nh, t0 = hr // tr, row0 // tr
    n = len(dws)

    def body(core_ref, *refs):
        dw_refs, q_refs, (o_ref, s_scr) = refs[:n], refs[n:2 * n], refs[2 * n:]
        for p, (a, off) in enumerate(dws):
            w = a.shape[1]
            s_scr[:, off:off + w] = (dw_refs[p][...] + q_refs[p][...]).astype(BF16)
        for j, src, dst, w in _w_in_pieces():
            o_ref[j, :, src:src + w] = s_scr[:, dst:dst + w]

    in_specs = [pl.BlockSpec((tr, a.shape[1]), lambda i, core: (i + t0 + core[0] * nh, 0)) for a, _ in dws]
    in_specs += [pl.BlockSpec((tr, q.shape[1]), lambda i, core: (i + t0, 0)) for q in theirs]
    res, jres = _call(body, name=name, grid=(rows // tr,), parallel=True, in_specs=in_specs,
                      out_specs=[pl.BlockSpec((N_CHIPS, tr, blk), lambda i, core: (0, i, 0))],
                      out_shape=[_sds((N_CHIPS, rows, blk), BF16)], scratch_shapes=[pltpu.VMEM((tr, N_ALL), BF16)],
                      args=(*[a for a, _ in dws], *theirs), job=job, by_core=True)
    return res[0], jres


def _presum(dws, theirs, name):
    dw, n = dws[0], len(dws)
    assert all(d.shape == dw.shape for d in dws)
    if dw.ndim == 4:
        _, _, hr, c = dw.shape
        tr = _update_row_tile(hr)
        mine = pl.BlockSpec((1, 1, tr, c), lambda j, i, core: (j, core[0], i, 0))
        other = pl.BlockSpec((1, tr, c), lambda j, i, core: (j, i, 0))
    else:
        hr, c = dw.shape[0] // 2, dw.shape[1] // N_CHIPS
        tr = _update_row_tile(hr)
        nh = hr // tr
        mine = pl.BlockSpec((tr, c), lambda j, i, core: (i + core[0] * nh, j))
        other = pl.BlockSpec((tr, c), lambda j, i, core: (i, j))

    def body(core_ref, *refs):
        for a_ref, q_ref, o_ref in zip(refs[:n], refs[n:2 * n], refs[2 * n:]):
            o_ref[...] = (a_ref[...].reshape(tr, c) + q_ref[...].reshape(tr, c)).astype(BF16).reshape(o_ref.shape)

    res, _ = _call(body, name=name, grid=(N_CHIPS, hr // tr), parallel=True, in_specs=[mine] * n + [other] * n,
                   out_specs=[pl.BlockSpec((1, tr, c), lambda j, i, core: (j, i, 0))] * n,
                   out_shape=[_sds((N_CHIPS, hr, c), BF16)] * n, args=(*dws, *theirs), by_core=True)
    return res


def _sum_slots(sums, slots, name):
    n = len(sums)
    _, rows, cols = sums[0].shape
    assert all(s.shape == sums[0].shape for s in sums)
    tr = _update_row_tile(rows)

    def body(chip_ref, *refs):
        for own_ref, s_ref, o_ref in zip(refs[:n], refs[n:2 * n], refs[2 * n:]):
            acc = own_ref[...].astype(F32)
            for j in range(N_PEER):
                acc = acc + s_ref[j].astype(F32)
            o_ref[...] = acc

    chip = (2 * lax.axis_index("x") + lax.axis_index("y")).astype(jnp.int32).reshape(1)
    spec = pltpu.PrefetchScalarGridSpec(
        num_scalar_prefetch=1, grid=(rows // tr,),
        in_specs=[pl.BlockSpec((None, tr, cols), lambda i, chip: (chip[0], i, 0))] * n
        + [pl.BlockSpec((N_PEER, tr, cols), lambda i, chip: (0, i, 0))] * n,
        out_specs=[pl.BlockSpec((tr, cols), lambda i, chip: (i, 0))] * n)
    return list(pl.pallas_call(
        body, name=name, grid_spec=spec, out_shape=[_sds((rows, cols), F32)] * n,
        compiler_params=pltpu.CompilerParams(dimension_semantics=("parallel",), vmem_limit_bytes=VMEM_LIMIT_BYTES),
    )(chip, *sums, *slots))


def _adamw(ws, ms, vs, g_mine, g_theirs, name, job=None):
    n = len(ws)
    rows, cols = ws[0].shape
    part_rows = [p.shape[0] for p in g_mine[0]]
    assert all(w.shape == (rows, cols) for w in ws) and sum(part_rows) == rows // 2
    assert all([p.shape[0] for p in parts] == part_rows for parts in (*g_mine, *g_theirs))
    tr = _update_row_tile(min(part_rows))
    assert all(r % tr == 0 for r in part_rows)
    nh = (rows // 2) // tr
    starts = [sum(part_rows[:k]) // tr for k in range(len(part_rows))]
    n_parts = len(part_rows)

    def body(core_ref, *refs):
        ins, outs = refs[:(3 + 2 * n_parts) * n], refs[(3 + 2 * n_parts) * n:]
        step = pl.program_id(0)
        mine_here = (step // nh) == core_ref[0]
        q = step % nh
        for a in range(n):
            g_refs = ins[3 * n + 2 * n_parts * a:3 * n + 2 * n_parts * (a + 1)]
            g = None
            for k in reversed(range(n_parts)):
                val = jnp.where(mine_here, g_refs[k][...], g_refs[n_parts + k][...])
                g = val if g is None else jnp.where(q < starts[k + 1], val, g)
            d, m2, v2 = _adam_values(ins[a][...], ins[n + a][...], ins[2 * n + a][...], g)
            g_out, d_out, m_out, v_out = outs[4 * a:4 * a + 4]
            g_out[...] = g
            m_out[...] = m2
            v_out[...] = v2
            d_out[...] = d

    def g_spec(k, mine):
        last = part_rows[k] // tr - 1

        def index(i, core):
            half = core[0] if mine else 1 - core[0]
            here = jnp.clip(i % nh - starts[k], 0, last)
            return (jnp.where(i // nh == half, here, jnp.where(i // nh > half, last, 0)), 0)

        return pl.BlockSpec((tr, cols), index)

    spec = pl.BlockSpec((tr, cols), lambda i, core: (i, 0))
    g_specs = [g_spec(k, True) for k in range(n_parts)] + [g_spec(k, False) for k in range(n_parts)]
    g_args = [p for a in range(n) for p in (*g_mine[a], *g_theirs[a])]
    res, jres = _call(body, name=name, grid=(rows // tr,), parallel=True, in_specs=[spec] * (3 * n) + g_specs * n,
                      out_specs=[spec] * (4 * n), out_shape=[_sds((rows, cols), F32)] * (4 * n),
                      args=(*ws, *ms, *vs, *g_args), job=job, by_core=True)
    return [tuple(res[4 * a:4 * a + 4]) for a in range(n)], jres


def _transposed_cast(wt):
    cols, rows = wt.shape
    tile = 4 * LANES

    def body(x_ref, o_ref):
        o_ref[...] = jnp.transpose(x_ref[...]).astype(BF16)

    res, _ = _call(body, name="transpose_w_in", grid=(pl.cdiv(cols, tile),), parallel=True,
                   in_specs=[pl.BlockSpec((tile, rows), lambda j: (j, 0))],
                   out_specs=[pl.BlockSpec((rows, tile), lambda j: (0, j))], out_shape=[_sds((rows, cols), BF16)],
                   args=(wt,))
    return res[0]


def _cast_weights(ws, w_fi, job=None):
    steps = 4
    cols = w_fi.shape[1]
    tile = w_fi.shape[0] // (2 * steps)

    def body(*refs):
        for i_ref, o_ref in zip(refs[:len(refs) // 2], refs[len(refs) // 2:]):
            o_ref[...] = i_ref[...].astype(BF16)

    row_specs = [pl.BlockSpec((w.shape[0] // steps, w.shape[1]), lambda i: (i, 0)) for w in ws]
    half_spec = pl.BlockSpec((tile, cols), lambda i: (i, 0))
    return _call(body, name="cast_weights", grid=(steps,), parallel=True,
                 in_specs=row_specs + [pl.BlockSpec((None, tile, cols), lambda i, k=k: (k, i, 0)) for k in range(2)],
                 out_specs=row_specs + [half_spec, half_spec],
                 out_shape=[_sds(w.shape, BF16) for w in ws] + [_sds((steps * tile, cols), BF16)] * 2,
                 args=(*ws, w_fi.reshape(2, steps * tile, cols), w_fi.reshape(2, steps * tile, cols)), job=job)


def _adamw_transposed(wt, mt, vt, g_mine, g_theirs, name):
    cols, rows = wt.shape
    n_parts = len(g_mine)

    def body(w_ref, m_ref, v_ref, *rest):
        g_refs, (g_out, d_out, m_out, v_out) = rest[:-4], rest[-4:]
        mine = jnp.concatenate([r[...] for r in g_refs[:n_parts]], axis=0)
        theirs = jnp.concatenate([r[...] for r in g_refs[n_parts:]], axis=0)
        first = lax.axis_index("c") == 0
        g = jnp.transpose(jnp.concatenate([jnp.where(first, mine, theirs), jnp.where(first, theirs, mine)], axis=0))
        d, m2, v2 = _adam_values(w_ref[...], m_ref[...], v_ref[...], g)
        g_out[...] = g
        m_out[...] = m2
        v_out[...] = v2
        d_out[...] = d

    spec = pl.BlockSpec((LANES, rows), lambda j: (j, 0))
    g_specs = [pl.BlockSpec((p.shape[0], LANES), lambda j: (0, j)) for p in g_mine] * 2
    res, _ = _call(body, name=name, grid=(pl.cdiv(cols, LANES),), parallel=True, in_specs=[spec] * 3 + g_specs,
                   out_specs=[spec] * 4, out_shape=[_sds((cols, rows), F32)] * 4, args=(wt, mt, vt, *g_mine, *g_theirs))
    return res


def _inproj_fwd(x, g1, w_all, job=None):
    T = x.shape[0]
    tT = _row_tile(T, 512)

    def body(x_ref, g_ref, w_ref, a_ref, proj_ref, alow_ref):
        xv = x_ref[...]
        a = (xv * _rms_stats(xv) * g_ref[...]).astype(BF16)
        a_ref[...] = a
        for j in range(N_MAIN // 1024):
            cols = slice(j * 1024, (j + 1) * 1024)
            proj_ref[:, cols] = _dot(a, w_ref[:, cols]).astype(BF16)
        alow_ref[...] = _dot(a, w_ref[:, N_MAIN:N_ALL])

    row = lambda w: pl.BlockSpec((tT, w), lambda i: (i, 0))
    return _call(
        body, name="inproj_fwd", grid=(T // tT,), parallel=True,
        in_specs=[row(D_MODEL), pl.BlockSpec((1, D_MODEL), lambda i: (0, 0)), _whole()],
        out_specs=[row(D_MODEL), row(N_MAIN), row(LANES)],
        out_shape=[_sds((T, D_MODEL), BF16), _sds((T, N_MAIN), BF16), _sds((T, LANES), F32)],
        args=(x, g1, w_all), job=job)


def _gla_decay_terms(al_ref, wgu_ref, bg_ref, later_ref):
    logit = _dot_bf16(al_ref[...], wgu_ref[...]) + bg_ref[...]
    la = _log_sigmoid(logit) * (1.0 / GLA_TAU)
    delta = _dot_exact_lhs(later_ref[...], la)
    return logit, la, delta


def _gla_fwd(proj, alow, wgu, b_gate, gn, job=None):
    T = proj.shape[0]
    tT = _row_tile(T, 512)
    nc = tT // CHUNK

    def body(q_ref, k_ref, v_ref, r_ref, al_ref, wgu_ref, bg_ref, gn_ref, later_ref, y_ref, st_ref, s_scr):
        @pl.when(pl.program_id(0) == 0)
        def _():
            s_scr[...] = jnp.zeros_like(s_scr)

        _, la, delta = _gla_decay_terms(al_ref, wgu_ref, bg_ref, later_ref)
        kdec = (k_ref[...].astype(F32) * jnp.exp(delta)).astype(BF16)
        heads = range(GLA_HEADS)
        kcs = [slice(h * GLA_DK, (h + 1) * GLA_DK) for h in heads]
        vcs = [slice(h * GLA_DV, (h + 1) * GLA_DV) for h in heads]
        state = [s_scr[h] for h in heads]
        for c in range(nc):
            rows = slice(c * CHUNK, (c + 1) * CHUNK)
            first = slice(c * CHUNK, c * CHUNK + 1)
            dec = jnp.exp(la[first, :] + delta[first, :])
            upd_t = [_dot(v_ref[rows, vcs[h]], kdec[rows, kcs[h]], _TN) for h in heads]
            qs = [(q_ref[rows, kcs[h]].astype(F32) * (GLA_DK ** -0.5)).astype(BF16) for h in heads]
            for h in heads:
                state[h] = state[h] * dec[:, kcs[h]] + upd_t[h]
                st_ref[c, h] = state[h]
            o = [_dot(qs[h], state[h].astype(BF16), _NT) for h in heads]
            for h in heads:
                on = o[h] * _rms_stats(o[h]) * gn_ref[:, vcs[h]]
                rr = r_ref[rows, vcs[h]].astype(F32)
                y_ref[rows, vcs[h]] = (on * (rr * _sigmoid(rr))).astype(BF16)
        for h in heads:
            s_scr[h] = state[h]

    blk = lambda w, j: pl.BlockSpec((tT, w), lambda i: (i, j))
    return _call(
        body, name="gla_fwd", grid=(T // tT,),
        in_specs=[blk(512, 0), blk(512, 1), blk(1024, 1), blk(1024, 2), blk(LANES, 0)] + [_whole()] * 4,
        out_specs=[pl.BlockSpec((tT, GLA_V), lambda i: (i, 0)),
                   pl.BlockSpec((nc, GLA_HEADS, GLA_DV, GLA_DK), lambda i: (i, 0, 0, 0))],
        out_shape=[_sds((T, GLA_V), BF16), _sds((T // CHUNK, GLA_HEADS, GLA_DV, GLA_DK), F32)],
        scratch_shapes=[pltpu.VMEM((GLA_HEADS, GLA_DV, GLA_DK), F32)],
        args=(proj, proj, proj, proj, alow, wgu, b_gate, gn, _chunk_masks(tT, upper=True)), job=job)


def _sgu_mask():
    i = lax.broadcasted_iota(jnp.int32, (SGU_BLOCK, SGU_BLOCK), 0)
    j = lax.broadcasted_iota(jnp.int32, (SGU_BLOCK, SGU_BLOCK), 1)
    return lax.shift_right_logical(j, 6) <= lax.shift_right_logical(i, 6)


def _sgu_merge_fwd(x, proj, y_gla, ln_g, ln_b, w_sp, b_sp_t, w_bg, w_bs, w_o, g_pm, job=None):
    T = x.shape[0]
    tT = _row_tile(T, 512)
    nb = tT // SGU_BLOCK

    def body(x_ref, su_ref, sv_ref, gg_ref, gs_ref, yg_ref, lg_ref, lb_ref, w_ref, b_ref, wbg_ref, wbs_ref, wo_ref,
             g_ref, ys_ref, zg_ref, zs_ref, mg_ref, mix_ref, x1_ref):
        mask = _sgu_mask()
        for g in range(SGU_GROUPS):
            gc = slice(g * SGU_DG, (g + 1) * SGU_DG)
            wm = jnp.where(mask, w_ref[g], 0.0).astype(BF16)
            vf = _gelu(sv_ref[:, gc].astype(F32))
            mu = jnp.mean(vf, axis=-1, keepdims=True)
            vc = vf - mu
            rstd = lax.rsqrt(jnp.mean(vc * vc, axis=-1, keepdims=True) + EPS)
            vn = (vc * rstd * lg_ref[:, gc] + lb_ref[:, gc]).astype(BF16)
            u = _gelu(su_ref[:, gc].astype(F32))
            for b in range(nb):
                rows = slice(b * SGU_BLOCK, (b + 1) * SGU_BLOCK)
                mixed = _dot(wm, vn[rows, :]) + b_ref[:, g:g + 1]
                ys_ref[rows, gc] = (u[rows, :] * mixed).astype(BF16)
        zg = _dot(yg_ref[...], wbg_ref[...])
        zs = _dot(ys_ref[...], wbs_ref[...])
        zg_ref[...] = zg.astype(BF16)
        zs_ref[...] = zs.astype(BF16)
        merged = (_sigmoid(gg_ref[...].astype(F32)) * zg + _sigmoid(gs_ref[...].astype(F32)) * zs).astype(BF16)
        mg_ref[...] = merged
        mix = _dot(merged, wo_ref[...])
        mix_ref[...] = mix.astype(BF16)
        x1_ref[...] = x_ref[...] + mix * _rms_stats(mix) * g_ref[...]

    row = pl.BlockSpec((tT, D_MODEL), lambda i: (i, 0))
    blk = lambda j: pl.BlockSpec((tT, 1024), lambda i: (i, j))
    sds = lambda dt: _sds((T, D_MODEL), dt)
    return _call(body, name="sgu_merge_fwd", grid=(T // tT,), parallel=True,
                 in_specs=[row, blk(3), blk(4), blk(5), blk(6), row] + [_whole()] * 7
                 + [pl.BlockSpec((1, D_MODEL), lambda i: (0, 0))],
                 out_specs=[row] * 6, out_shape=[sds(BF16)] * 5 + [sds(F32)],
                 args=(x, proj, proj, proj, proj, y_gla, ln_g, ln_b, w_sp, b_sp_t, w_bg, w_bs, w_o, g_pm), job=job)


def _ffn_fwd_bwd(x1, tgt, w_fi_top, w_fi_bot, w_fo, g_pf, g_po):
    T = x1.shape[0]
    tT = _row_tile(T, 256)
    half = D_FF // 2
    kh = D_MODEL // 2

    def body(x1_ref, t_ref, top_ref, bot_ref, wfo_ref, gpf_ref, gpo_ref,
             h_ref, f_ref, dgu_ref, dy_ref, dx1_ref, loss_ref, dgpf_ref, dgpo_ref, gu_scr):
        @pl.when(pl.program_id(0) == 0)
        def _():
            loss_ref[...] = jnp.zeros_like(loss_ref)
            dgpf_ref[...] = jnp.zeros_like(dgpf_ref)
            dgpo_ref[...] = jnp.zeros_like(dgpo_ref)

        main = (half // 256) * 256
        pieces = (0, 1, None)

        def w_in_cols(ref, first_slab, p):
            if p is not None:
                return ref[first_slab + p, :, :main]
            return jnp.concatenate([ref[first_slab, :, main:], ref[first_slab + 1, :, main:]], axis=1)

        def w_out_rows(p):
            if p is not None:
                return wfo_ref[p * half:p * half + main, :]
            return jnp.concatenate([wfo_ref[main:half, :], wfo_ref[half + main:2 * half, :]], axis=0)

        def put(ref, base, p, val):
            if p is not None:
                ref[:, base + p * half:base + p * half + main] = val
            else:
                ref[:, base + main:base + half] = val[:, :half - main]
                ref[:, base + half + main:base + 2 * half] = val[:, half - main:]

        def get(ref, base, p):
            if p is not None:
                return ref[:, base + p * half:base + p * half + main]
            return jnp.concatenate([ref[:, base + main:base + half], ref[:, base + half + main:base + 2 * half]], axis=1)

        x1v = x1_ref[...]
        r2 = _rms_stats(x1v)
        h = (x1v * r2 * gpf_ref[...]).astype(BF16)
        h_ref[...] = h
        y = jnp.zeros((tT, D_MODEL), F32)
        for p in pieces:
            gate = _dot(h[:, :kh], w_in_cols(top_ref, 0, p)) + _dot(h[:, kh:], w_in_cols(bot_ref, 0, p))
            up = _dot(h[:, :kh], w_in_cols(top_ref, 2, p)) + _dot(h[:, kh:], w_in_cols(bot_ref, 2, p))
            put(gu_scr, 0, p, gate)
            put(gu_scr, D_FF, p, up)
            f = (gate * _sigmoid(gate) * up).astype(BF16)
            put(f_ref, 0, p, f)
            y = y + _dot(f, w_out_rows(p))
        r3 = _rms_stats(y)
        x2 = x1v + y * r3 * gpo_ref[...]
        err = x2 - t_ref[...]
        loss_ref[...] += jnp.sum(err * err) * (0.5 / D_MODEL)
        dx2 = err * (1.0 / D_MODEL)
        dy, dg = _rms_bwd(dx2, y, r3, gpo_ref[...])
        dgpo_ref[...] += jnp.sum(dg, axis=0, keepdims=True)
        dyb = dy.astype(BF16)
        dy_ref[...] = dyb
        dh_top = jnp.zeros((tT, kh), F32)
        dh_bot = jnp.zeros((tT, kh), F32)
        for p in pieces:
            df = _dot(dyb, w_out_rows(p), _NT)
            gate = get(gu_scr, 0, p)
            up = get(gu_scr, D_FF, p)
            sg = _sigmoid(gate)
            dgate = (df * up * (sg * (1.0 + gate * (1.0 - sg)))).astype(BF16)
            dup = (df * (gate * sg)).astype(BF16)
            put(dgu_ref, 0, p, dgate)
            put(dgu_ref, D_FF, p, dup)
            dh_top = dh_top + _dot(dgate, w_in_cols(top_ref, 0, p), _NT) + _dot(dup, w_in_cols(top_ref, 2, p), _NT)
            dh_bot = dh_bot + _dot(dgate, w_in_cols(bot_ref, 0, p), _NT) + _dot(dup, w_in_cols(bot_ref, 2, p), _NT)
        dh = jnp.concatenate([dh_top, dh_bot], axis=1)
        dx1n, dg2 = _rms_bwd(dh, x1v, r2, gpf_ref[...])
        dgpf_ref[...] += jnp.sum(dg2, axis=0, keepdims=True)
        dx1_ref[...] = dx2 + dx1n

    row = lambda w: pl.BlockSpec((tT, w), lambda i: (i, 0))
    vec = pl.BlockSpec((1, D_MODEL), lambda i: (0, 0))
    res, _ = _call(
        body, name="ffn_fwd_bwd", grid=(T // tT,),
        in_specs=[row(D_MODEL), row(D_MODEL), _whole(), _whole(), _whole(), vec, vec],
        out_specs=[row(D_MODEL), row(D_FF), row(2 * D_FF), row(D_MODEL), row(D_MODEL),
                   pl.BlockSpec((1, LANES), lambda i: (0, 0)), vec, vec],
        out_shape=[_sds((T, D_MODEL), BF16), _sds((T, D_FF), BF16), _sds((T, 2 * D_FF), BF16), _sds((T, D_MODEL), BF16),
                   _sds((T, D_MODEL), F32), _sds((1, LANES), F32), _sds((1, D_MODEL), F32), _sds((1, D_MODEL), F32)],
        scratch_shapes=[pltpu.VMEM((tT, 2 * D_FF), F32)], args=(x1, tgt, w_fi_top, w_fi_bot, w_fo, g_pf, g_po))
    return res


def _merge_sgu_bwd(dx1, mix, proj, zg, zs, w_bg, w_bs, w_o, g_pm, ln_g, ln_b, w_sp, b_sp_t, job=None):
    T = dx1.shape[0]
    tT = _row_tile(T, 256)
    nb = tT // SGU_BLOCK

    def body(dx1_ref, mix_ref, su_ref, sv_ref, gg_ref, gs_ref, zg_ref, zs_ref, wbg_ref, wbs_ref, wo_ref, g_ref,
             lg_ref, lb_ref, w_ref, b_ref,
             dmix_ref, dzg_ref, dzs_ref, dgate_ref, dyg_ref, dp_ref, dgpm_ref, dw_ref, dbt_ref, dlg_ref, dlb_ref):
        @pl.when(pl.program_id(0) == 0)
        def _():
            for ref in (dgpm_ref, dw_ref, dbt_ref, dlg_ref, dlb_ref):
                ref[...] = jnp.zeros_like(ref)

        mix = mix_ref[...].astype(F32)
        dmix, dg = _rms_bwd(dx1_ref[...], mix, _rms_stats(mix), g_ref[...])
        dgpm_ref[...] += jnp.sum(dg, axis=0, keepdims=True)
        dmb = dmix.astype(BF16)
        dmix_ref[...] = dmb
        dmerged = _dot(dmb, wo_ref[...], _NT)
        dys = None
        for k, (gate_ref, z_ref, w_br_ref, dz_ref) in enumerate(((gg_ref, zg_ref, wbg_ref, dzg_ref),
                                                                 (gs_ref, zs_ref, wbs_ref, dzs_ref))):
            sg = _sigmoid(gate_ref[...].astype(F32))
            dz = (dmerged * sg).astype(BF16)
            dz_ref[...] = dz
            dgate_ref[:, k * 1024:(k + 1) * 1024] = (dmerged * z_ref[...].astype(F32) * (sg * (1.0 - sg))).astype(BF16)
            dy_branch = _dot(dz, w_br_ref[...], _NT)
            if k == 0:
                dyg_ref[...] = dy_branch.astype(BF16)
            else:
                dys = dy_branch

        mask = _sgu_mask()
        lane = lax.broadcasted_iota(jnp.int32, (SGU_BLOCK, LANES), 1)
        for g in range(SGU_GROUPS):
            gc = slice(g * SGU_DG, (g + 1) * SGU_DG)
            gc_v = slice(1024 + g * SGU_DG, 1024 + (g + 1) * SGU_DG)
            wm = jnp.where(mask, w_ref[g], 0.0).astype(BF16)
            vf, dvf_dsv = _gelu_and_grad(sv_ref[:, gc].astype(F32))
            mu = jnp.mean(vf, axis=-1, keepdims=True)
            vc = vf - mu
            rstd = lax.rsqrt(jnp.mean(vc * vc, axis=-1, keepdims=True) + EPS)
            vhat = vc * rstd
            vn = (vhat * lg_ref[:, gc] + lb_ref[:, gc]).astype(BF16)
            u, du_dsu = _gelu_and_grad(su_ref[:, gc].astype(F32))
            dy = dys[:, gc]
            dmixed = (dy * u).astype(BF16)
            dvn_parts = []
            dw_acc = jnp.zeros((SGU_BLOCK, SGU_BLOCK), F32)
            db_acc = jnp.zeros((SGU_BLOCK, 1), F32)
            for b in range(nb):
                rows = slice(b * SGU_BLOCK, (b + 1) * SGU_BLOCK)
                mixed = _dot(wm, vn[rows, :]) + b_ref[:, g:g + 1]
                dp_ref[rows, gc] = (dy[rows, :] * mixed * du_dsu[rows, :]).astype(BF16)
                dvn_parts.append(_dot(wm, dmixed[rows, :], _TN))
                dw_acc = dw_acc + _dot(dmixed[rows, :], vn[rows, :], _NT)
                db_acc = db_acc + jnp.sum(dmixed[rows, :].astype(F32), axis=-1, keepdims=True)
            dw_ref[g] += jnp.where(mask, dw_acc, 0.0)
            dbt_ref[...] += jnp.where(lane == g, db_acc, 0.0)
            dvn = jnp.concatenate(dvn_parts, axis=0)
            dlg_ref[:, gc] += jnp.sum(dvn * vhat, axis=0, keepdims=True)
            dlb_ref[:, gc] += jnp.sum(dvn, axis=0, keepdims=True)
            dvh = dvn * lg_ref[:, gc]
            dvf = rstd * (dvh - jnp.mean(dvh, axis=-1, keepdims=True)
                          - vhat * jnp.mean(dvh * vhat, axis=-1, keepdims=True))
            dp_ref[:, gc_v] = (dvf * dvf_dsv).astype(BF16)

    row = pl.BlockSpec((tT, D_MODEL), lambda i: (i, 0))
    blk = lambda j: pl.BlockSpec((tT, 1024), lambda i: (i, j))
    vec = pl.BlockSpec((1, D_MODEL), lambda i: (0, 0))
    wide = lambda w: pl.BlockSpec((tT, w), lambda i: (i, 0))
    sds = _sds((T, D_MODEL), BF16)
    return _call(
        body, name="merge_sgu_bwd", grid=(T // tT,),
        in_specs=[row, row, blk(3), blk(4), blk(5), blk(6), row, row] + [_whole()] * 3 + [vec] + [_whole()] * 4,
        out_specs=[row, row, row, wide(W_MRG), row, wide(W_SGU), vec,
                   pl.BlockSpec((SGU_GROUPS, SGU_BLOCK, SGU_BLOCK), lambda i: (0, 0, 0)),
                   pl.BlockSpec((SGU_BLOCK, LANES), lambda i: (0, 0)), vec, vec],
        out_shape=[sds, sds, sds, _sds((T, W_MRG), BF16), sds, _sds((T, W_SGU), BF16), _sds((1, D_MODEL), F32),
                   _sds((SGU_GROUPS, SGU_BLOCK, SGU_BLOCK), F32), _sds((SGU_BLOCK, LANES), F32),
                   _sds((1, 1024), F32), _sds((1, 1024), F32)],
        args=(dx1, mix, proj, proj, proj, proj, zg, zs, w_bg, w_bs, w_o, g_pm, ln_g, ln_b, w_sp, b_sp_t), job=job)


def _gla_bwd(proj, alow, wgu, b_gate, gn, states, dy_gla, job=None):
    T = proj.shape[0]
    tT = _row_tile(T, 512)
    nc = tT // CHUNK
    nt = T // tT

    def body(q_ref, k_ref, v_ref, r_ref, al_ref, wgu_ref, bg_ref, gn_ref, later_ref, earlier_ref, st_ref, sp_ref, dy_ref,
             dp_ref, dal_ref, dgn_ref, dbg_ref, dwgu_ref, g_scr, dd_scr, dt_scr):
        step = pl.program_id(0)

        @pl.when(step == 0)
        def _():
            g_scr[...] = jnp.zeros_like(g_scr)
            dgn_ref[...] = jnp.zeros_like(dgn_ref)
            dbg_ref[...] = jnp.zeros_like(dbg_ref)
            dwgu_ref[...] = jnp.zeros_like(dwgu_ref)

        has_prev = jnp.where(step == nt - 1, 0.0, 1.0)
        logit, la, delta = _gla_decay_terms(al_ref, wgu_ref, bg_ref, later_ref)
        e = jnp.exp(delta)
        kdec_f = k_ref[...].astype(F32) * e
        kdec = kdec_f.astype(BF16)
        heads = range(GLA_HEADS)
        kcs = [slice(h * GLA_DK, (h + 1) * GLA_DK) for h in heads]
        vcs = [slice(h * GLA_DV, (h + 1) * GLA_DV) for h in heads]
        carry = [g_scr[h] for h in heads]
        dgn_acc = [jnp.zeros((1, GLA_DV), F32) for _ in heads]
        for c in reversed(range(nc)):
            rows = slice(c * CHUNK, (c + 1) * CHUNK)
            first = slice(c * CHUNK, c * CHUNK + 1)
            dec = jnp.exp(la[first, :] + delta[first, :])
            s_b = [st_ref[c, h].astype(BF16) for h in heads]
            qs = [(q_ref[rows, kcs[h]].astype(F32) * (GLA_DK ** -0.5)).astype(BF16) for h in heads]
            o = [_dot(qs[h], s_b[h], _NT) for h in heads]
            do = []
            for h in heads:
                rstd = _rms_stats(o[h])
                ohat = o[h] * rstd
                gnh = gn_ref[:, vcs[h]]
                dy = dy_ref[rows, vcs[h]].astype(F32)
                rr = r_ref[rows, vcs[h]].astype(F32)
                sg = _sigmoid(rr)
                don = dy * (rr * sg)
                dp_ref[rows, OFF_R + h * GLA_DV:OFF_R + (h + 1) * GLA_DV] = (
                    dy * (ohat * gnh) * (sg * (1.0 + rr * (1.0 - sg)))).astype(BF16)
                dgn_acc[h] = dgn_acc[h] + jnp.sum(don * ohat, axis=0, keepdims=True)
                dn = don * gnh
                do.append((rstd * (dn - ohat * jnp.mean(dn * ohat, axis=-1, keepdims=True))).astype(BF16))
            dq = [_dot(do[h], s_b[h]) for h in heads]
            g_t = [_dot(do[h], qs[h], _TN) + carry[h] for h in heads]
            g_b = [g_t[h].astype(BF16) for h in heads]
            dv = [_dot(kdec[rows, kcs[h]], g_b[h], _NT) for h in heads]
            dkdec = [_dot(v_ref[rows, vcs[h]], g_b[h]) for h in heads]
            for h in heads:
                s_prev = st_ref[c - 1, h] if c > 0 else sp_ref[0, h] * has_prev
                ddec = jnp.sum(g_t[h] * s_prev, axis=0, keepdims=True)
                carry[h] = g_t[h] * dec[:, kcs[h]]
                dp_ref[rows, OFF_Q + h * GLA_DK:OFF_Q + (h + 1) * GLA_DK] = (dq[h] * (GLA_DK ** -0.5)).astype(BF16)
                dp_ref[rows, OFF_V + h * GLA_DV:OFF_V + (h + 1) * GLA_DV] = dv[h].astype(BF16)
                dp_ref[rows, OFF_K + h * GLA_DK:OFF_K + (h + 1) * GLA_DK] = (dkdec[h] * e[rows, kcs[h]]).astype(BF16)
                dd_scr[rows, kcs[h]] = dkdec[h] * kdec_f[rows, kcs[h]]
                dt_scr[rows, kcs[h]] = jnp.broadcast_to(ddec * dec[:, kcs[h]], (CHUNK, GLA_DK))
        for h in heads:
            g_scr[h] = carry[h]
            dgn_ref[:, vcs[h]] += dgn_acc[h]
        dla = _dot_exact_lhs(earlier_ref[...], dd_scr[...]) + dt_scr[...]
        dlogit = dla * (1.0 / GLA_TAU) * _sigmoid(-logit)
        dbg_ref[...] += jnp.sum(dlogit, axis=0, keepdims=True)
        dwgu_ref[...] += _dot_bf16(al_ref[...], dlogit, _TN)
        dal_ref[...] = _dot_bf16(dlogit, wgu_ref[...], _NT).astype(BF16)

    rev = lambda i: nt - 1 - i
    blk = lambda w, j: pl.BlockSpec((tT, w), lambda i: (rev(i), j))
    st_blk = pl.BlockSpec((nc, GLA_HEADS, GLA_DV, GLA_DK), lambda i: (rev(i), 0, 0, 0))
    sp_blk = pl.BlockSpec((1, GLA_HEADS, GLA_DV, GLA_DK), lambda i: (jnp.maximum(rev(i) * nc - 1, 0), 0, 0, 0))
    return _call(
        body, name="gla_bwd", grid=(nt,),
        in_specs=[blk(512, 0), blk(512, 1), blk(1024, 1), blk(1024, 2), blk(LANES, 0)] + [_whole()] * 5
        + [st_blk, sp_blk, blk(GLA_V, 0)],
        out_specs=[blk(W_GLA, 0), blk(LANES, 0), pl.BlockSpec((1, GLA_V), lambda i: (0, 0)),
                   pl.BlockSpec((1, GLA_QK), lambda i: (0, 0)), pl.BlockSpec((LANES, GLA_QK), lambda i: (0, 0))],
        out_shape=[_sds((T, W_GLA), BF16), _sds((T, LANES), BF16), _sds((1, GLA_V), F32), _sds((1, GLA_QK), F32),
                   _sds((LANES, GLA_QK), F32)],
        scratch_shapes=[pltpu.VMEM((GLA_HEADS, GLA_DV, GLA_DK), F32), pltpu.VMEM((tT, GLA_QK), F32),
                        pltpu.VMEM((tT, GLA_QK), F32)],
        args=(proj, proj, proj, proj, alow, wgu, b_gate, gn, _chunk_masks(tT, upper=True), _chunk_masks(tT, upper=False),
              states, states, dy_gla), job=job)


def _inproj_bwd(x, dx1, g1, w_all, dparts, job=None):
    T = x.shape[0]
    tT = _row_tile(T, 512)
    offs = (0, W_GLA, W_GLA + W_SGU, N_MAIN)

    def body(x_ref, dx1_ref, g_ref, w_hbm, *rest):
        part_refs, (dx_ref, dg_ref, w_ref, w_sems) = rest[:len(offs)], rest[len(offs):]

        def compute(first):
            if first:
                copies = [pltpu.make_async_copy(w_hbm.at[:, pl.ds(off, p.shape[1])], w_ref.at[:, pl.ds(off, p.shape[1])],
                                                w_sems.at[k]) for k, (off, p) in enumerate(zip(offs, dparts))]
                for cp in copies:
                    cp.start()
            da = jnp.zeros((tT, D_MODEL), F32)
            for k, (off, p_ref) in enumerate(zip(offs, part_refs)):
                if first:
                    copies[k].wait()
                da = da + _dot(p_ref[...], w_ref[:, off:off + p_ref.shape[1]], _NT)
            xv = x_ref[...]
            dx, dg = _rms_bwd(da, xv, _rms_stats(xv), g_ref[...])
            dg_sum = jnp.sum(dg, axis=0, keepdims=True)
            dg_ref[...] = dg_sum if first else dg_ref[...] + dg_sum
            dx_ref[...] = dx1_ref[...] + dx

        first_step = pl.program_id(0) == 0
        pl.when(first_step)(lambda: compute(True))
        pl.when(jnp.logical_not(first_step))(lambda: compute(False))

    row = lambda w: pl.BlockSpec((tT, w), lambda i: (i, 0))
    vec = pl.BlockSpec((1, D_MODEL), lambda i: (0, 0))
    return _call(
        body, name="inproj_bwd", grid=(T // tT,),
        in_specs=[row(D_MODEL), row(D_MODEL), vec, pl.BlockSpec(memory_space=pl.ANY)] + [row(p.shape[1]) for p in dparts],
        out_specs=[row(D_MODEL), vec], out_shape=[_sds((T, D_MODEL), F32), _sds((1, D_MODEL), F32)],
        scratch_shapes=[pltpu.VMEM(w_all.shape, BF16), pltpu.SemaphoreType.DMA((len(offs),))],
        args=(x, dx1, g1, w_all, *dparts), job=job)


def _tn_matmul(a, b, name, job=None):
    T, M = a.shape
    N = b.shape[1]
    tk = _row_tile(T, 1024)
    tm = M if M <= 1024 else 1408
    tn = N // 2 if N > 2048 else N
    assert M % tm == 0 and N % tn == 0

    def body(a_ref, b_ref, o_ref):
        @pl.when(pl.program_id(2) == 0)
        def _():
            o_ref[...] = _dot(a_ref[...], b_ref[...], _TN)

        @pl.when(pl.program_id(2) > 0)
        def _():
            o_ref[...] += _dot(a_ref[...], b_ref[...], _TN)

    res, jres = _call(
        body, name=name, grid=(M // tm, N // tn, T // tk),
        in_specs=[pl.BlockSpec((tk, tm), lambda i, j, k: (k, i)), pl.BlockSpec((tk, tn), lambda i, j, k: (k, j))],
        out_specs=[pl.BlockSpec((tm, tn), lambda i, j, k: (i, j))], out_shape=[_sds((M, N), F32)], args=(a, b), job=job)
    return res[0], jres


def _pad_rows(a, rows=8):
    return jnp.pad(a, ((0, rows - a.shape[0]), (0, LANES - a.shape[1])))


def _halves_view(dw):
    r = dw.shape[0] // N_CHIPS
    return dw.reshape(N_CHIPS, 2, r // 2, dw.shape[1])


def kernel(x, norm_pre_mix, w_in, w_gate_up, b_gate, gla_norm, sgu_ln_g, sgu_ln_b, w_spatial, b_spatial, w_branch_gla, w_branch_sgu, w_out, norm_post_mix, norm_pre_ffn, w_ffn_in, w_ffn_out, norm_post_ffn, loss_target, m_norm_pre_mix, m_w_in, m_w_gate_up, m_b_gate, m_gla_norm, m_sgu_ln_g, m_sgu_ln_b, m_w_spatial, m_b_spatial, m_w_branch_gla, m_w_branch_sgu, m_w_out, m_norm_post_mix, m_norm_pre_ffn, m_w_ffn_in, m_w_ffn_out, m_norm_post_ffn, v_norm_pre_mix, v_w_in, v_w_gate_up, v_b_gate, v_gla_norm, v_sgu_ln_g, v_sgu_ln_b, v_w_spatial, v_b_spatial, v_w_branch_gla, v_w_branch_sgu, v_w_out, v_norm_post_mix, v_norm_pre_ffn, v_w_ffn_in, v_w_ffn_out, v_norm_post_ffn):
    chip = 2 * lax.axis_index("x") + lax.axis_index("y")
    xt, tgt = x[0], loss_target[0]

    tiny = jnp.concatenate([w_gate_up[0], _pad_rows(gla_norm[0]), _pad_rows(sgu_ln_g[0]), _pad_rows(sgu_ln_b[0]),
                            jnp.zeros((24, LANES), F32)], axis=0)

    def with_own(gathered, own):
        return lax.dynamic_update_slice(gathered, own[None], (chip, 0, 0))

    w_in_t, m_in_t, v_in_t = w_in[0].T, m_w_in[0].T, v_w_in[0].T
    m_in_t, v_in_t = lax.optimization_barrier(m_in_t), lax.optimization_barrier(v_in_t)
    w_in_b = _transposed_cast(w_in_t)
    (*own_rows, fi_top, fi_bot), (g_in, g_tiny) = _cast_weights(
        [w_branch_gla[0], w_branch_sgu[0], w_out[0], w_ffn_out[0]], w_ffn_in[0], job=_job_gather([w_in_b, tiny]))
    g_tiny = with_own(g_tiny, tiny)
    w_all = _relayout_w_in(with_own(g_in, w_in_b))
    cols = lambda a: a.transpose(1, 0, 2).reshape(a.shape[1], N_CHIPS * a.shape[2])
    wgu = jnp.pad(cols(g_tiny[:, 0:16]), ((0, LANES - GLA_RANK), (0, 0)))
    gn = cols(g_tiny[:, 16:20, :64]).reshape(1, GLA_V)
    ln_g = cols(g_tiny[:, 24:28, :64]).reshape(1, 1024)
    ln_b = cols(g_tiny[:, 32:36, :64]).reshape(1, 1024)
    b_sp_t = jnp.pad(b_spatial[0].T, ((0, 0), (0, LANES - SGU_GROUPS)))
    w_sp = w_spatial[0]

    (a, proj, alow), g_rows = _inproj_fwd(xt, norm_pre_mix, w_all, job=_job_gather(own_rows))
    rows = lambda g: g.reshape(N_CHIPS * g.shape[1], g.shape[2])
    w_bg, w_bs, w_o, w_fo = [rows(with_own(g, own)) for g, own in zip(g_rows, own_rows)]
    (y_gla, states), (g_top,) = _gla_fwd(proj, alow, wgu, b_gate, gn, job=_job_gather([fi_top]))
    (y_sgu, zg, zs, merged, mix, x1), (g_bot,) = _sgu_merge_fwd(
        xt, proj, y_gla, ln_g, ln_b, w_sp, b_sp_t, w_bg, w_bs, w_o, norm_post_mix, job=_job_gather([fi_bot]))
    h, f, dgu, dy, dx1, loss, d_gpf, d_gpo = _ffn_fwd_bwd(x1, tgt, with_own(g_top, fi_top), with_own(g_bot, fi_bot),
                                                          w_fo, norm_pre_ffn, norm_post_ffn)

    whole = lambda hs: [[(h_, None)] for h_ in hs]
    dw_fo, _ = _tn_matmul(f, dy, "dw_ffn_out")
    dw_fo4 = _halves_view(dw_fo)
    dw_fi, (q_fo,) = _tn_matmul(h, dgu, "dw_ffn_in", job=_job_to_other_core([[(dw_fo4, 0)]]))
    c_fo, = _presum([dw_fo4], [q_fo], "presum_ffn_out")
    (dmix, dzg, dzs, dp_mrg, dyg, dp_sgu, d_gpm, d_wsp, d_bsp_t, d_lng, d_lnb), (s_fo, q_fi) = _merge_sgu_bwd(
        dx1, mix, proj, zg, zs, w_bg, w_bs, w_o, norm_post_mix, ln_g, ln_b, w_sp, b_sp_t,
        job=_join(_job_scatter([c_fo]), _job_to_other_core([[(dw_fi, 0)]])))
    c_fi, = _presum([dw_fi], [q_fi], "presum_ffn_in")
    dw_c, _ = _tn_matmul(a, dp_mrg, "dw_in_merge")
    dw_b, _ = _tn_matmul(a, dp_sgu, "dw_in_sgu")
    dw_o4 = _halves_view(_tn_matmul(merged, dmix, "dw_out")[0])
    dw_bg4 = _halves_view(_tn_matmul(y_gla, dzg, "dw_branch_gla")[0])
    dw_bs4 = _halves_view(_tn_matmul(y_sgu, dzs, "dw_branch_sgu")[0])
    h_fo, = _sum_slots([c_fo], [s_fo], "sum_ffn_out")
    (dp_gla, dal, d_gn, d_bg, d_wgu), (s_fi, t_fo, q_b, q_c, q_o, q_bg, q_bs) = _gla_bwd(
        proj, alow, wgu, b_gate, gn, states, dyg,
        job=_join(_job_scatter([c_fi]), _job_to_other_core(
            whole([h_fo]) + [[(dw_b, 0)], [(dw_c, 0)], [(dw_o4, 0)], [(dw_bg4, 0)], [(dw_bs4, 0)]])))
    c_o, c_bg, c_bs = _presum([dw_o4, dw_bg4, dw_bs4], [q_o, q_bg, q_bs], "presum_out_branches")
    h_fi, = _sum_slots([c_fi], [s_fi], "sum_ffn_in")
    dw_d, _ = _tn_matmul(a, dal, "dw_in_gate")
    dw_a, (s_o, s_bg, s_bs, t_fi, q_d) = _tn_matmul(
        a, dp_gla, "dw_in_gla",
        job=_join(_job_scatter([c_o, c_bg, c_bs]), _job_to_other_core(whole([h_fi]) + [[(dw_d, 0)]])))
    h_o, h_bg, h_bs = _sum_slots([c_o, c_bg, c_bs], [s_o, s_bg, s_bs], "sum_out_branches")

    grads, deltas, new_m, new_v = {}, {}, {}, {}

    def update(call, names, ws, ms, vs, g_mine, g_theirs, job=None):
        res, jres = _adamw([w[0] for w in ws], [m[0] for m in ms], [v[0] for v in vs], g_mine, g_theirs, call, job=job)
        for name, (g, d, m2, v2) in zip(names, res):
            grads[name], deltas[name], new_m[name], new_v[name] = g[None], d[None], m2[None], v2[None]
        return jres

    dw_in = [(dw_a, 0), (dw_b, W_GLA), (dw_c, W_GLA + W_SGU), (dw_d, N_MAIN)]
    q_a, t_o, t_bg, t_bs = update("adamw_w_ffn_out", ["w_ffn_out"], [w_ffn_out], [m_w_ffn_out], [v_w_ffn_out],
                                  [[h_fo]], [[t_fo]],
                                  job=_job_to_other_core([[(dw_a, 0)]] + whole([h_o, h_bg, h_bs])))
    q_in = [q_a, q_b, q_c, q_d]
    hr_in = D_MODEL // 2
    c_in_a, _ = _presum_w_in(dw_in, q_in, 0, hr_in // 8, "presum_w_in_a")
    c_in_b, (s_in_a,) = _presum_w_in(dw_in, q_in, hr_in // 8, 7 * hr_in // 8, "presum_w_in_b",
                                     job=_job_scatter([c_in_a]))
    update("adamw_w_ffn_in", ["w_ffn_in"], [w_ffn_in], [m_w_ffn_in], [v_w_ffn_in], [[h_fi]], [[t_fi]])
    update("adamw_out_branches", ["w_out", "w_branch_gla", "w_branch_sgu"], [w_out, w_branch_gla, w_branch_sgu],
           [m_w_out, m_w_branch_gla, m_w_branch_sgu], [v_w_out, v_w_branch_gla, v_w_branch_sgu],
           [[h_o], [h_bg], [h_bs]], [[t_o], [t_bg], [t_bs]])
    (grad_x, d_g1), (s_in_b,) = _inproj_bwd(xt, dx1, norm_pre_mix, w_all, (dp_gla, dp_sgu, dp_mrg, dal),
                                            job=_job_scatter([c_in_b]))
    h_in = _sum_slots([c_in_a], [s_in_a], "sum_w_in_a") + _sum_slots([c_in_b], [s_in_b], "sum_w_in_b")
    t_in = _run_job(_job_to_other_core(whole(h_in)), "swap_w_in")
    for store, val in zip((grads, deltas, new_m, new_v),
                          _adamw_transposed(w_in_t, m_in_t, v_in_t, h_in, t_in, "adamw_w_in")):
        store["w_in"] = val.T[None]

    small_names = ["w_spatial", "w_gate_up", "norm_pre_mix", "norm_post_mix", "norm_pre_ffn", "norm_post_ffn", "b_gate",
                   "b_spatial", "gla_norm", "sgu_ln_g", "sgu_ln_b"]
    loss_out, small = _small_adamw(
        _small_sum([d_wsp, d_wgu, d_g1, d_gpm, d_gpf, d_gpo, d_bg, d_bsp_t, d_gn, d_lng, d_lnb, loss]),
        [w_spatial, w_gate_up, norm_pre_mix, norm_post_mix, norm_pre_ffn, norm_post_ffn, b_gate, b_spatial, gla_norm,
         sgu_ln_g, sgu_ln_b],
        [m_w_spatial, m_w_gate_up, m_norm_pre_mix, m_norm_post_mix, m_norm_pre_ffn, m_norm_post_ffn, m_b_gate,
         m_b_spatial, m_gla_norm, m_sgu_ln_g, m_sgu_ln_b],
        [v_w_spatial, v_w_gate_up, v_norm_pre_mix, v_norm_post_mix, v_norm_pre_ffn, v_norm_post_ffn, v_b_gate,
         v_b_spatial, v_gla_norm, v_sgu_ln_g, v_sgu_ln_b])
    for store, vals in zip((grads, deltas, new_m, new_v), small):
        store.update(zip(small_names, vals))

    order = ["norm_pre_mix", "w_in", "w_gate_up", "b_gate", "gla_norm", "sgu_ln_g", "sgu_ln_b", "w_spatial", "b_spatial",
             "w_branch_gla", "w_branch_sgu", "w_out", "norm_post_mix", "norm_pre_ffn", "w_ffn_in", "w_ffn_out",
             "norm_post_ffn"]
    out = [loss_out, grad_x[None]]
    for store in (grads, deltas, new_m, new_v):
        out.extend(store[n] for n in order)
    return tuple(out)
```

```python
import jax
import jax.numpy as jnp
from jax import lax
from jax.experimental import pallas as pl
from jax.experimental.pallas import tpu as pltpu

F32 = jnp.float32
BF16 = jnp.bfloat16

D_MODEL = 1024
GLA_HEADS = 4
GLA_DK = 128
GLA_DV = 256
GLA_QK = GLA_HEADS * GLA_DK
GLA_V = GLA_HEADS * GLA_DV
GLA_RANK = 16
GLA_TAU = 16.0
CHUNK = 64
SGU_GROUPS = 4
SGU_BLOCK = 128
SGU_DG = 256
D_FF = 2816
EPS = 1e-6
LANES = 128

OFF_Q, OFF_K, OFF_V, OFF_R, OFF_SU, OFF_SV, OFF_GG, OFF_GS, OFF_AL = 0, 512, 1024, 2048, 3072, 4096, 5120, 6144, 7168
W_GLA, W_SGU, W_MRG = 3072, 2048, 2048
N_MAIN = 7168
N_ALL = N_MAIN + LANES
_IN_SPLITS = (GLA_QK, GLA_QK, GLA_V, GLA_V, GLA_RANK, 1024, 1024, 1024, 1024)
_IN_STARTS = tuple(sum(_IN_SPLITS[:i]) for i in range(len(_IN_SPLITS) + 1))
_IN_DST = (OFF_Q, OFF_K, OFF_V, OFF_R, OFF_AL, OFF_SU, OFF_SV, OFF_GG, OFF_GS)
D_IN = _IN_STARTS[-1]

ADAM_LR = 0.001
ADAM_B1 = 0.9
ADAM_B2 = 0.999
ADAM_EPS = 1e-08
ADAM_WD = 0.01
ADAM_STEP = 10

VMEM_LIMIT_BYTES = 56 * 1024 * 1024
N_CHIPS = 4
N_PEER = N_CHIPS - 1
N_DEV = 8
MESH = pl.DeviceIdType.MESH

_NN = (((1,), (0,)), ((), ()))
_NT = (((1,), (1,)), ((), ()))
_TN = (((0,), (0,)), ((), ()))


def _dot(a, b, dims=_NN):
    return lax.dot_general(a, b, dims, preferred_element_type=F32)


def _split(x):
    hi = x.astype(BF16)
    lo = (x - hi.astype(F32)).astype(BF16)
    return hi, lo


def _dot_bf16(a, b, dims=_NN):
    return _dot(a.astype(BF16), b.astype(BF16), dims)


def _dot_exact_lhs(m, x):
    xh, xl = _split(x)
    return _dot(m, xh) + _dot(m, xl)


def _sigmoid(x):
    return 0.5 * jnp.tanh(0.5 * x) + 0.5


def _log_sigmoid(x):
    return jnp.minimum(x, 0.0) - jnp.log(1.0 + jnp.exp(-jnp.abs(x)))


_GELU_C = 0.7978845608028654
_GELU_A = 0.044715


def _gelu_and_grad(x):
    x2 = x * x
    t = jnp.tanh(_GELU_C * (x + _GELU_A * x * x2))
    g = 0.5 * x * (1.0 + t)
    dg = 0.5 * (1.0 + t) + 0.5 * x * (1.0 - t * t) * (_GELU_C * (1.0 + 3.0 * _GELU_A * x2))
    return g, dg


def _gelu(x):
    t = jnp.tanh(_GELU_C * (x + _GELU_A * x * x * x))
    return 0.5 * x * (1.0 + t)


def _rms_stats(x):
    return lax.rsqrt(jnp.mean(x * x, axis=-1, keepdims=True) + EPS)


def _rms_bwd(dout, y, r, g):
    yhat = y * r
    dn = dout * g
    dy = r * (dn - yhat * jnp.mean(dn * yhat, axis=-1, keepdims=True))
    return dy, dout * yhat


def _whole():
    return pl.BlockSpec(memory_space=pltpu.VMEM)


def _row_tile(T, want):
    t = min(T, want)
    assert T % t == 0
    return t


def _chunk_masks(tT, upper):
    row = lax.broadcasted_iota(jnp.int32, (tT, tT), 0)
    col = lax.broadcasted_iota(jnp.int32, (tT, tT), 1)
    same = (row // CHUNK) == (col // CHUNK)
    tri = (col > row) if upper else (col < row)
    return jnp.where(same & tri, 1.0, 0.0).astype(BF16)


class _Job:
    def __init__(self, ins, out_shapes, scratch, start, finish, mid=None):
        self.ins, self.out_shapes, self.scratch = list(ins), list(out_shapes), list(scratch)
        self.start, self.finish, self.mid = start, finish, mid


def _join(*jobs):
    def split(refs, counts):
        out, at = [], 0
        for n in counts:
            out.append(refs[at:at + n])
            at += n
        return out

    ni, no, ns = [len(j.ins) for j in jobs], [len(j.out_shapes) for j in jobs], [len(j.scratch) for j in jobs]

    def start(ins, outs, scr):
        for j, a, b, c in zip(jobs, split(ins, ni), split(outs, no), split(scr, ns)):
            j.start(a, b, c)

    def finish(ins, outs, scr):
        for j, a, b, c in zip(jobs, split(ins, ni), split(outs, no), split(scr, ns)):
            j.finish(a, b, c)

    def mid(ins, outs, scr):
        for j, a, b, c in zip(jobs, split(ins, ni), split(outs, no), split(scr, ns)):
            if j.mid is not None:
                j.mid(a, b, c)

    return _Job(sum((j.ins for j in jobs), []), sum((j.out_shapes for j in jobs), []),
                sum((j.scratch for j in jobs), []), start, finish, mid if any(j.mid for j in jobs) else None)


def _mesh_pos():
    return lax.axis_index("x"), lax.axis_index("y"), lax.axis_index("c")


def _peer_chips(xi, yi):
    return [(1 - xi, yi), (xi, 1 - yi), (1 - xi, 1 - yi)]


def _half(ci, rows):
    return pl.ds(pl.multiple_of(ci * rows, 8), rows)


def _sds(shape, dtype):
    return jax.ShapeDtypeStruct(tuple(shape), dtype)


def _job_gather(arrs):
    n = len(arrs)
    kinds = 12
    Y0, Y1, X1, X0, ON_X, ON_Y, D2D = 0, 1, 2, 3, 4, 5, 6

    def copies(ins, outs, scr):
        send_sems, recv_sems = scr
        xi, yi, ci = _mesh_pos()
        me, cx, cy, cd = 2 * xi + yi, 2 * (1 - xi) + yi, 2 * xi + (1 - yi), 2 * (1 - xi) + (1 - yi)
        to_x, to_y, to_core = (1 - xi, yi, ci), (xi, 1 - yi, ci), (xi, yi, 1 - ci)
        table = []
        for k in range(n):
            qr = arrs[k].shape[0] // 4

            def rows(core, q):
                return pl.ds(pl.multiple_of((2 * core + q) * qr, 8), qr)

            def cp(kind, src, dst, to):
                s = k * kinds + kind
                return pltpu.make_async_remote_copy(src_ref=src, dst_ref=dst, send_sem=send_sems.at[s],
                                                    recv_sem=recv_sems.at[s], device_id=to, device_id_type=MESH)

            def slab(chip, core, q):
                return outs[k].at[chip, rows(core, q)]

            t = {}
            for kind, q, to, frm in ((Y0, 0, to_y, cy), (Y1, 1, to_y, cy), (X1, 1, to_x, cx), (X0, 0, to_x, cx)):
                mine = ins[k].at[rows(ci, q)]
                t[kind] = (cp(kind, mine, slab(me, ci, q), to), cp(kind, mine, slab(frm, ci, q), to))
            t[ON_X] = (cp(ON_X, slab(cy, ci, 0), slab(cy, ci, 0), to_x), cp(ON_X, slab(cy, ci, 0), slab(cd, ci, 0), to_x))
            t[ON_Y] = (cp(ON_Y, slab(cx, ci, 1), slab(cx, ci, 1), to_y), cp(ON_Y, slab(cx, ci, 1), slab(cd, ci, 1), to_y))
            for i, (chip, q) in enumerate(((cy, 0), (cy, 1), (cx, 1), (cx, 0), (cd, 0), (cd, 1))):
                t[D2D + i] = (cp(D2D + i, slab(chip, ci, q), slab(chip, ci, q), to_core),
                              cp(D2D + i, slab(chip, ci, q), slab(chip, 1 - ci, q), to_core))
            table.append(t)
        return table

    def start(ins, outs, scr):
        table = copies(ins, outs, scr)
        for kind in (Y0, X1, Y1, X0):
            for t in table:
                t[kind][0].start()

    def arrived(table, kind, then):
        for t in table:
            t[kind][1].wait_recv()
            for nxt in then:
                t[nxt][0].start()

    def mid(ins, outs, scr):
        table = copies(ins, outs, scr)
        arrived(table, Y0, (ON_X, D2D + 0))
        arrived(table, X1, (ON_Y, D2D + 2))

    def finish(ins, outs, scr):
        table = copies(ins, outs, scr)
        arrived(table, Y1, (D2D + 1,))
        arrived(table, X0, (D2D + 3,))
        arrived(table, ON_X, (D2D + 4,))
        arrived(table, ON_Y, (D2D + 5,))
        for t in table:
            for i in range(6):
                t[D2D + i][1].wait_recv()
            for kind in range(kinds):
                t[kind][0].wait_send()

    dma = pltpu.SemaphoreType.DMA
    return _Job(arrs, [_sds((N_CHIPS,) + a.shape, a.dtype) for a in arrs], [dma((n * kinds,))] * 2, start, finish, mid)


def _job_scatter(parts):
    n = len(parts)

    def copies(ins, outs, scr):
        send_sems, recv_sems = scr
        xi, yi, ci = _mesh_pos()
        res = []
        for k in range(n):
            for j, (px, py) in enumerate(_peer_chips(xi, yi)):
                s = k * N_PEER + j
                res.append(pltpu.make_async_remote_copy(
                    src_ref=ins[k].at[2 * px + py], dst_ref=outs[k].at[j], send_sem=send_sems.at[s],
                    recv_sem=recv_sems.at[s], device_id=(px, py, ci), device_id_type=MESH))
        return res

    def start(ins, outs, scr):
        for cp in copies(ins, outs, scr):
            cp.start()

    def finish(ins, outs, scr):
        for cp in copies(ins, outs, scr):
            cp.wait_recv()
            cp.wait_send()

    dma = pltpu.SemaphoreType.DMA
    return _Job(parts, [_sds((N_PEER,) + p.shape[1:], p.dtype) for p in parts], [dma((n * N_PEER,))] * 2, start, finish)


def _job_to_other_core(groups):
    pieces = [(g, a, off) for g, group in enumerate(groups) for a, off in group]
    n = len(pieces)

    def geometry(group):
        a0, off0 = group[0]
        if off0 is None:
            return a0.shape
        if a0.ndim == 4:
            return (N_CHIPS, a0.shape[2], a0.shape[3])
        return (a0.shape[0] // 2, sum(a.shape[1] for a, _ in group))

    def copies(ins, outs, scr):
        send_sems, recv_sems = scr
        xi, yi, ci = _mesh_pos()
        res = []
        for p, (g, a, off) in enumerate(pieces):
            if off is None:
                give, land = ins[p], outs[g]
            elif a.ndim == 4:
                give, land = ins[p].at[pl.ds(0, N_CHIPS), 1 - ci], outs[g]
            else:
                hr, w = a.shape[0] // 2, a.shape[1]
                give, land = ins[p].at[_half(1 - ci, hr)], outs[g].at[pl.ds(0, hr), pl.ds(off, w)]
            res.append(pltpu.make_async_remote_copy(
                src_ref=give, dst_ref=land, send_sem=send_sems.at[p], recv_sem=recv_sems.at[p],
                device_id=(xi, yi, 1 - ci), device_id_type=MESH))
        return res

    def start(ins, outs, scr):
        for cp in copies(ins, outs, scr):
            cp.start()

    def finish(ins, outs, scr):
        for cp in copies(ins, outs, scr):
            cp.wait_recv()
            cp.wait_send()

    dma = pltpu.SemaphoreType.DMA
    return _Job([a for _, a, _ in pieces], [_sds(geometry(group), group[0][0].dtype) for group in groups],
                [dma((n,))] * 2, start, finish)


def _call(body, *, name, grid, in_specs, out_specs, out_shape, args, scratch_shapes=(), parallel=False, job=None,
          by_core=False):
    n_in, n_out, n_scr = len(in_specs), len(out_specs), len(scratch_shapes)
    hbm = pl.BlockSpec(memory_space=pl.ANY)
    n_ji, n_jo = (len(job.ins), len(job.out_shapes)) if job is not None else (0, 0)
    lead = 1 if by_core else 0

    def kernel_fn(*refs):
        core, refs = refs[:lead], refs[lead:]
        ins, refs = refs[:n_in], refs[n_in:]
        j_ins, refs = refs[:n_ji], refs[n_ji:]
        outs, refs = refs[:n_out], refs[n_out:]
        j_outs, refs = refs[:n_jo], refs[n_jo:]
        scr, j_scr = refs[:n_scr], refs[n_scr:]
        if job is None:
            body(*core, *ins, *outs, *scr)
            return
        ids = [pl.program_id(d) for d in range(len(grid))]
        first = ids[0] == 0
        last = ids[0] == grid[0] - 1
        for d in range(1, len(grid)):
            first = first & (ids[d] == 0)
            last = last & (ids[d] == grid[d] - 1)

        @pl.when(first)
        def _():
            job.start(j_ins, j_outs, j_scr)

        if job.mid is not None and grid[0] >= 4:
            half_way = ids[0] == grid[0] // 2
            for d in range(1, len(grid)):
                half_way = half_way & (ids[d] == 0)

            @pl.when(half_way)
            def _():
                job.mid(j_ins, j_outs, j_scr)

        body(*core, *ins, *outs, *scr)

        @pl.when(last)
        def _():
            if job.mid is not None and grid[0] < 4:
                job.mid(j_ins, j_outs, j_scr)
            job.finish(j_ins, j_outs, j_scr)

    sem = ("parallel" if parallel and job is None else "arbitrary",) * len(grid)
    all_in = list(in_specs) + [hbm] * n_ji
    all_out = list(out_specs) + [hbm] * n_jo
    all_scratch = list(scratch_shapes) + (job.scratch if job is not None else [])
    all_shapes = list(out_shape) + (job.out_shapes if job is not None else [])
    all_args = list(args) + (job.ins if job is not None else [])
    params = pltpu.CompilerParams(dimension_semantics=sem, vmem_limit_bytes=VMEM_LIMIT_BYTES)
    if by_core:
        spec = pltpu.PrefetchScalarGridSpec(num_scalar_prefetch=1, grid=grid, in_specs=all_in, out_specs=all_out,
                                            scratch_shapes=all_scratch)
        core = lax.axis_index("c").astype(jnp.int32).reshape(1)
        res = pl.pallas_call(kernel_fn, name=name, grid_spec=spec, out_shape=all_shapes, compiler_params=params)(
            core, *all_args)
    else:
        res = pl.pallas_call(kernel_fn, name=name, grid=grid, in_specs=all_in, out_specs=all_out, out_shape=all_shapes,
                             scratch_shapes=all_scratch, compiler_params=params)(*all_args)
    return list(res[:n_out]), list(res[n_out:])


def _run_job(job, name):
    n_i, n_o = len(job.ins), len(job.out_shapes)

    def body(*refs):
        ins, outs, scr = refs[:n_i], refs[n_i:n_i + n_o], refs[n_i + n_o:]
        job.start(ins, outs, scr)
        if job.mid is not None:
            job.mid(ins, outs, scr)
        job.finish(ins, outs, scr)

    hbm = pl.BlockSpec(memory_space=pl.ANY)
    return list(pl.pallas_call(body, name=name, in_specs=[hbm] * n_i, out_specs=[hbm] * n_o, out_shape=job.out_shapes,
                               scratch_shapes=job.scratch)(*job.ins))


def _adam_values(w, m, v, g):
    m2 = ADAM_B1 * m + (1.0 - ADAM_B1) * g
    v2 = ADAM_B2 * v + (1.0 - ADAM_B2) * (g * g)
    delta = -ADAM_LR * ((m2 / (1.0 - ADAM_B1 ** ADAM_STEP)) / (jnp.sqrt(v2 / (1.0 - ADAM_B2 ** ADAM_STEP)) + ADAM_EPS)
                        + ADAM_WD * w)
    return delta, m2, v2


_P_WSP, _P_WGU, _P_NORM, _P_BG, _P_BSP, _P_HEAD, _P_LOSS, _P_ROWS = 0, 512, 576, 608, 616, 624, 720, 736


def _small_sum(dgrads):
    hr = _P_ROWS // 2

    def body(dwsp, dwgu, dg1, dgpm, dgpf, dgpo, dbg, dbspt, dgn, dlng, dlnb, loss_in, tot, pack, pair, slots, send_sems,
             recv_sems):
        xi, yi, ci = _mesh_pos()
        chip = 2 * xi + yi

        pack[...] = jnp.zeros_like(pack)
        for g in range(SGU_GROUPS):
            pack[_P_WSP + g * SGU_BLOCK:_P_WSP + (g + 1) * SGU_BLOCK] = dwsp[g]
        for j in range(N_CHIPS):
            pack[_P_WGU + GLA_RANK * j:_P_WGU + GLA_RANK * (j + 1)] = dwgu[0:GLA_RANK, LANES * j:LANES * (j + 1)]
        for k, r in enumerate((dg1, dgpm, dgpf, dgpo)):
            for q in range(8):
                pack[_P_NORM + 8 * k + q:_P_NORM + 8 * k + q + 1] = r[:, LANES * q:LANES * (q + 1)]
        for q in range(4):
            pack[_P_BG + q:_P_BG + q + 1] = dbg[:, LANES * q:LANES * (q + 1)]
        pack[_P_BSP:_P_BSP + SGU_GROUPS] = jnp.transpose(dbspt[...])[0:SGU_GROUPS]
        for k, r in enumerate((dgn, dlng, dlnb)):
            for j in range(N_CHIPS):
                for hh in range(4):
                    row = _P_HEAD + 32 * k + 8 * j + hh
                    pack[row:row + 1, 0:64] = r[:, 256 * hh + 64 * j:256 * hh + 64 * (j + 1)]
        pack[_P_LOSS:_P_LOSS + 1] = loss_in[...]

        sibling = dict(device_id=(xi, yi, 1 - ci), device_id_type=MESH)
        to_sibling = pltpu.make_async_remote_copy(src_ref=pack, dst_ref=pair, send_sem=send_sems.at[N_PEER],
                                                  recv_sem=recv_sems.at[N_PEER], **sibling)
        to_sibling.start()
        to_sibling.wait_recv()
        to_sibling.wait_send()
        pack[...] = pack[...] + pair[...]
        mine = pl.ds(pl.multiple_of(ci * hr, 8), hr)
        theirs = pl.ds(pl.multiple_of((1 - ci) * hr, 8), hr)
        slots[chip] = pack[mine, :]

        def copy(j, slot):
            px, py = _peer_chips(xi, yi)[j]
            return pltpu.make_async_remote_copy(
                src_ref=pack.at[mine], dst_ref=slots.at[slot(2 * px + py)], send_sem=send_sems.at[j],
                recv_sem=recv_sems.at[j], device_id=(px, py, ci), device_id_type=MESH)

        sends = [copy(j, lambda peer_chip: chip) for j in range(N_PEER)]
        for cp in sends:
            cp.start()
        for j in range(N_PEER):
            copy(j, lambda peer_chip: peer_chip).wait_recv()
        for cp in sends:
            cp.wait_send()
        acc = slots[0]
        for d in range(1, N_CHIPS):
            acc = acc + slots[d]
        tot[mine, :] = acc
        half_over = pltpu.make_async_remote_copy(src_ref=tot.at[mine], dst_ref=tot.at[mine], send_sem=send_sems.at[N_PEER + 1],
                                                 recv_sem=recv_sems.at[N_PEER + 1], **sibling)
        half_back = pltpu.make_async_remote_copy(src_ref=tot.at[mine], dst_ref=tot.at[theirs], send_sem=send_sems.at[N_PEER + 1],
                                                 recv_sem=recv_sems.at[N_PEER + 1], **sibling)
        half_over.start()
        half_back.wait_recv()
        half_over.wait_send()

    return pl.pallas_call(
        body, name="small_sum", in_specs=[_whole()] * 12, out_specs=_whole(), out_shape=_sds((_P_ROWS, LANES), F32),
        scratch_shapes=[pltpu.VMEM((_P_ROWS, LANES), F32), pltpu.VMEM((_P_ROWS, LANES), F32),
                        pltpu.VMEM((N_CHIPS, hr, LANES), F32),
                        pltpu.SemaphoreType.DMA((N_PEER + 2,)), pltpu.SemaphoreType.DMA((N_PEER + 2,))],
        compiler_params=pltpu.CompilerParams(vmem_limit_bytes=VMEM_LIMIT_BYTES),
    )(*dgrads)


def _small_adamw(tot, ws, ms, vs):
    n = len(ws)

    def body(*refs):
        tot = refs[0]
        w_refs, m_refs, v_refs = refs[1:1 + n], refs[1 + n:1 + 2 * n], refs[1 + 2 * n:1 + 3 * n]
        loss_out = refs[1 + 3 * n]
        outs = refs[2 + 3 * n:]
        chip = 2 * lax.axis_index("x") + lax.axis_index("y")
        loss_out[...] = tot[_P_LOSS:_P_LOSS + 1, 0:1]

        def step(k, g, pick, put):
            d, m2, v2 = _adam_values(pick(w_refs[k]), pick(m_refs[k]), pick(v_refs[k]), g)
            for o, val in zip((outs[k], outs[n + k], outs[2 * n + k], outs[3 * n + k]), (g, d, m2, v2)):
                put(o, val)

        def whole(ref):
            return ref[0]

        def put_whole(ref, val):
            ref[0] = val

        for g in range(SGU_GROUPS):
            def pick_g(ref, g=g):
                return ref[0, g]

            def put_g(ref, val, g=g):
                ref[0, g] = val

            step(0, tot[_P_WSP + g * SGU_BLOCK:_P_WSP + (g + 1) * SGU_BLOCK], pick_g, put_g)
        step(1, tot[pl.ds(pl.multiple_of(_P_WGU + GLA_RANK * chip, GLA_RANK), GLA_RANK), :], whole, put_whole)
        for k, (base, chunks) in enumerate(((_P_NORM, 8), (_P_NORM + 8, 8), (_P_NORM + 16, 8), (_P_NORM + 24, 8), (_P_BG, 4))):
            for q in range(chunks):
                def pick_q(ref, q=q):
                    return ref[:, LANES * q:LANES * (q + 1)]

                def put_q(ref, val, q=q):
                    ref[:, LANES * q:LANES * (q + 1)] = val

                step(2 + k, tot[base + q:base + q + 1], pick_q, put_q)
        step(7, tot[_P_BSP:_P_BSP + SGU_GROUPS], whole, put_whole)
        for k in range(3):
            mine = tot[pl.ds(pl.multiple_of(_P_HEAD + 32 * k + 8 * chip, 8), 8), :]
            step(8 + k, mine[0:4, 0:64], whole, put_whole)

    shapes = [_sds(w.shape, F32) for w in ws]
    res = pl.pallas_call(
        body, name="small_adamw", in_specs=[_whole()] * (1 + 3 * n), out_specs=[_whole()] * (1 + 4 * n),
        out_shape=[_sds((1, 1), F32)] + shapes * 4,
        compiler_params=pltpu.CompilerParams(vmem_limit_bytes=VMEM_LIMIT_BYTES),
    )(tot, *ws, *ms, *vs)
    return res[0].reshape(()), [list(res[1 + i * n:1 + (i + 1) * n]) for i in range(4)]


def _w_in_pieces():
    blk = D_IN // N_CHIPS
    pieces = []
    for s in range(len(_IN_SPLITS)):
        lo_s, hi_s = _IN_STARTS[s], _IN_STARTS[s + 1]
        for j in range(N_CHIPS):
            lo, hi = max(lo_s, j * blk), min(hi_s, (j + 1) * blk)
            if lo < hi:
                pieces.append((j, lo - j * blk, _IN_DST[s] + lo - lo_s, hi - lo))
    return pieces


def _relayout_w_in(gathered):
    _, rows, blk = gathered.shape
    tr = 256

    def body(g_ref, o_ref):
        o_ref[:, OFF_AL:N_ALL] = jnp.zeros((tr, LANES), BF16)
        for j, src, dst, w in _w_in_pieces():
            o_ref[:, dst:dst + w] = g_ref[j, :, src:src + w]

    res, _ = _call(body, name="relayout_w_in", grid=(rows // tr,), parallel=True,
                   in_specs=[pl.BlockSpec((N_CHIPS, tr, blk), lambda i: (0, i, 0))],
                   out_specs=[pl.BlockSpec((tr, N_ALL), lambda i: (i, 0))],
                   out_shape=[_sds((rows, N_ALL), BF16)], args=(gathered,))
    return res[0]


def _update_row_tile(rows):
    for t in range(min(rows, 256), 7, -8):
        if rows % t == 0:
            return t
    return rows


def _presum_w_in(dws, theirs, row0, rows, name, job=None):
    hr = theirs[0].shape[0]
    blk = D_IN // N_CHIPS
    tr = 64
    assert row0 % tr == 0 and rows % tr == 0
    nh, t0 = hr // tr, row0 // tr
    n = len(dws)

    def body(core_ref, *refs):
        dw_refs, q_refs, (o_ref, s_scr) = refs[:n], refs[n:2 * n], refs[2 * n:]
        for p, (a, off) in enumerate(dws):
            w = a.shape[1]
            s_scr[:, off:off + w] = (dw_refs[p][...] + q_refs[p][...]).astype(BF16)
        for j, src, dst, w in _w_in_pieces():
            o_ref[j, :, src:src + w] = s_scr[:, dst:dst + w]

    in_specs = [pl.BlockSpec((tr, a.shape[1]), lambda i, core: (i + t0 + core[0] * nh, 0)) for a, _ in dws]
    in_specs += [pl.BlockSpec((tr, q.shape[1]), lambda i, core: (i + t0, 0)) for q in theirs]
    res, jres = _call(body, name=name, grid=(rows // tr,), parallel=True, in_specs=in_specs,
                      out_specs=[pl.BlockSpec((N_CHIPS, tr, blk), lambda i, core: (0, i, 0))],
                      out_shape=[_sds((N_CHIPS, rows, blk), BF16)], scratch_shapes=[pltpu.VMEM((tr, N_ALL), BF16)],
                      args=(*[a for a, _ in dws], *theirs), job=job, by_core=True)
    return res[0], jres


def _presum(dws, theirs, name):
    dw, n = dws[0], len(dws)
    assert all(d.shape == dw.shape for d in dws)
    if dw.ndim == 4:
        _, _, hr, c = dw.shape
        tr = _update_row_tile(hr)
        mine = pl.BlockSpec((1, 1, tr, c), lambda j, i, core: (j, core[0], i, 0))
        other = pl.BlockSpec((1, tr, c), lambda j, i, core: (j, i, 0))
    else:
        hr, c = dw.shape[0] // 2, dw.shape[1] // N_CHIPS
        tr = _update_row_tile(hr)
        nh = hr // tr
        mine = pl.BlockSpec((tr, c), lambda j, i, core: (i + core[0] * nh, j))
        other = pl.BlockSpec((tr, c), lambda j, i, core: (i, j))

    def body(core_ref, *refs):
        for a_ref, q_ref, o_ref in zip(refs[:n], refs[n:2 * n], refs[2 * n:]):
            o_ref[...] = (a_ref[...].reshape(tr, c) + q_ref[...].reshape(tr, c)).astype(BF16).reshape(o_ref.shape)

    res, _ = _call(body, name=name, grid=(N_CHIPS, hr // tr), parallel=True, in_specs=[mine] * n + [other] * n,
                   out_specs=[pl.BlockSpec((1, tr, c), lambda j, i, core: (j, i, 0))] * n,
                   out_shape=[_sds((N_CHIPS, hr, c), BF16)] * n, args=(*dws, *theirs), by_core=True)
    return res


def _sum_slots(sums, slots, name):
    n = len(sums)
    _, rows, cols = sums[0].shape
    assert all(s.shape == sums[0].shape for s in sums)
    tr = _update_row_tile(rows)

    def body(chip_ref, *refs):
        for own_ref, s_ref, o_ref in zip(refs[:n], refs[n:2 * n], refs[2 * n:]):
            acc = own_ref[...].astype(F32)
            for j in range(N_PEER):
                acc = acc + s_ref[j].astype(F32)
            o_ref[...] = acc

    chip = (2 * lax.axis_index("x") + lax.axis_index("y")).astype(jnp.int32).reshape(1)
    spec = pltpu.PrefetchScalarGridSpec(
        num_scalar_prefetch=1, grid=(rows // tr,),
        in_specs=[pl.BlockSpec((None, tr, cols), lambda i, chip: (chip[0], i, 0))] * n
        + [pl.BlockSpec((N_PEER, tr, cols), lambda i, chip: (0, i, 0))] * n,
        out_specs=[pl.BlockSpec((tr, cols), lambda i, chip: (i, 0))] * n)
    return list(pl.pallas_call(
        body, name=name, grid_spec=spec, out_shape=[_sds((rows, cols), F32)] * n,
        compiler_params=pltpu.CompilerParams(dimension_semantics=("parallel",), vmem_limit_bytes=VMEM_LIMIT_BYTES),
    )(chip, *sums, *slots))


def _adamw(ws, ms, vs, g_mine, g_theirs, name, job=None):
    n = len(ws)
    rows, cols = ws[0].shape
    part_rows = [p.shape[0] for p in g_mine[0]]
    assert all(w.shape == (rows, cols) for w in ws) and sum(part_rows) == rows // 2
    assert all([p.shape[0] for p in parts] == part_rows for parts in (*g_mine, *g_theirs))
    tr = _update_row_tile(min(part_rows))
    assert all(r % tr == 0 for r in part_rows)
    nh = (rows // 2) // tr
    starts = [sum(part_rows[:k]) // tr for k in range(len(part_rows))]
    n_parts = len(part_rows)

    def body(core_ref, *refs):
        ins, outs = refs[:(3 + 2 * n_parts) * n], refs[(3 + 2 * n_parts) * n:]
        step = pl.program_id(0)
        mine_here = (step // nh) == core_ref[0]
        q = step % nh
        for a in range(n):
            g_refs = ins[3 * n + 2 * n_parts * a:3 * n + 2 * n_parts * (a + 1)]
            g = None
            for k in reversed(range(n_parts)):
                val = jnp.where(mine_here, g_refs[k][...], g_refs[n_parts + k][...])
                g = val if g is None else jnp.where(q < starts[k + 1], val, g)
            d, m2, v2 = _adam_values(ins[a][...], ins[n + a][...], ins[2 * n + a][...], g)
            g_out, d_out, m_out, v_out = outs[4 * a:4 * a + 4]
            g_out[...] = g
            m_out[...] = m2
            v_out[...] = v2
            d_out[...] = d

    def g_spec(k, mine):
        last = part_rows[k] // tr - 1

        def index(i, core):
            half = core[0] if mine else 1 - core[0]
            here = jnp.clip(i % nh - starts[k], 0, last)
            return (jnp.where(i // nh == half, here, jnp.where(i // nh > half, last, 0)), 0)

        return pl.BlockSpec((tr, cols), index)

    spec = pl.BlockSpec((tr, cols), lambda i, core: (i, 0))
    g_specs = [g_spec(k, True) for k in range(n_parts)] + [g_spec(k, False) for k in range(n_parts)]
    g_args = [p for a in range(n) for p in (*g_mine[a], *g_theirs[a])]
    res, jres = _call(body, name=name, grid=(rows // tr,), parallel=True, in_specs=[spec] * (3 * n) + g_specs * n,
                      out_specs=[spec] * (4 * n), out_shape=[_sds((rows, cols), F32)] * (4 * n),
                      args=(*ws, *ms, *vs, *g_args), job=job, by_core=True)
    return [tuple(res[4 * a:4 * a + 4]) for a in range(n)], jres


def _transposed_cast(wt):
    cols, rows = wt.shape
    tile = 4 * LANES

    def body(x_ref, o_ref):
        o_ref[...] = jnp.transpose(x_ref[...]).astype(BF16)

    res, _ = _call(body, name="transpose_w_in", grid=(pl.cdiv(cols, tile),), parallel=True,
                   in_specs=[pl.BlockSpec((tile, rows), lambda j: (j, 0))],
                   out_specs=[pl.BlockSpec((rows, tile), lambda j: (0, j))], out_shape=[_sds((rows, cols), BF16)],
                   args=(wt,))
    return res[0]


def _cast_weights(ws, w_fi, job=None):
    steps = 4
    cols = w_fi.shape[1]
    tile = w_fi.shape[0] // (2 * steps)

    def body(*refs):
        for i_ref, o_ref in zip(refs[:len(refs) // 2], refs[len(refs) // 2:]):
            o_ref[...] = i_ref[...].astype(BF16)

    row_specs = [pl.BlockSpec((w.shape[0] // steps, w.shape[1]), lambda i: (i, 0)) for w in ws]
    half_spec = pl.BlockSpec((tile, cols), lambda i: (i, 0))
    return _call(body, name="cast_weights", grid=(steps,), parallel=True,
                 in_specs=row_specs + [pl.BlockSpec((None, tile, cols), lambda i, k=k: (k, i, 0)) for k in range(2)],
                 out_specs=row_specs + [half_spec, half_spec],
                 out_shape=[_sds(w.shape, BF16) for w in ws] + [_sds((steps * tile, cols), BF16)] * 2,
                 args=(*ws, w_fi.reshape(2, steps * tile, cols), w_fi.reshape(2, steps * tile, cols)), job=job)


def _adamw_transposed(wt, mt, vt, g_mine, g_theirs, name):
    cols, rows = wt.shape
    n_parts = len(g_mine)

    def body(w_ref, m_ref, v_ref, *rest):
        g_refs, (g_out, d_out, m_out, v_out) = rest[:-4], rest[-4:]
        mine = jnp.concatenate([r[...] for r in g_refs[:n_parts]], axis=0)
        theirs = jnp.concatenate([r[...] for r in g_refs[n_parts:]], axis=0)
        first = lax.axis_index("c") == 0
        g = jnp.transpose(jnp.concatenate([jnp.where(first, mine, theirs), jnp.where(first, theirs, mine)], axis=0))
        d, m2, v2 = _adam_values(w_ref[...], m_ref[...], v_ref[...], g)
        g_out[...] = g
        m_out[...] = m2
        v_out[...] = v2
        d_out[...] = d

    spec = pl.BlockSpec((LANES, rows), lambda j: (j, 0))
    g_specs = [pl.BlockSpec((p.shape[0], LANES), lambda j: (0, j)) for p in g_mine] * 2
    res, _ = _call(body, name=name, grid=(pl.cdiv(cols, LANES),), parallel=True, in_specs=[spec] * 3 + g_specs,
                   out_specs=[spec] * 4, out_shape=[_sds((cols, rows), F32)] * 4, args=(wt, mt, vt, *g_mine, *g_theirs))
    return res


def _inproj_fwd(x, g1, w_all, job=None):
    T = x.shape[0]
    tT = _row_tile(T, 512)

    def body(x_ref, g_ref, w_ref, a_ref, proj_ref, alow_ref):
        xv = x_ref[...]
        a = (xv * _rms_stats(xv) * g_ref[...]).astype(BF16)
        a_ref[...] = a
        for j in range(N_MAIN // 1024):
            cols = slice(j * 1024, (j + 1) * 1024)
            proj_ref[:, cols] = _dot(a, w_ref[:, cols]).astype(BF16)
        alow_ref[...] = _dot(a, w_ref[:, N_MAIN:N_ALL])

    row = lambda w: pl.BlockSpec((tT, w), lambda i: (i, 0))
    return _call(
        body, name="inproj_fwd", grid=(T // tT,), parallel=True,
        in_specs=[row(D_MODEL), pl.BlockSpec((1, D_MODEL), lambda i: (0, 0)), _whole()],
        out_specs=[row(D_MODEL), row(N_MAIN), row(LANES)],
        out_shape=[_sds((T, D_MODEL), BF16), _sds((T, N_MAIN), BF16), _sds((T, LANES), F32)],
        args=(x, g1, w_all), job=job)


def _gla_decay_terms(al_ref, wgu_ref, bg_ref, later_ref):
    logit = _dot_bf16(al_ref[...], wgu_ref[...]) + bg_ref[...]
    la = _log_sigmoid(logit) * (1.0 / GLA_TAU)
    delta = _dot_exact_lhs(later_ref[...], la)
    return logit, la, delta


def _gla_fwd(proj, alow, wgu, b_gate, gn, job=None):
    T = proj.shape[0]
    tT = _row_tile(T, 512)
    nc = tT // CHUNK

    def body(q_ref, k_ref, v_ref, r_ref, al_ref, wgu_ref, bg_ref, gn_ref, later_ref, y_ref, st_ref, s_scr):
        @pl.when(pl.program_id(0) == 0)
        def _():
            s_scr[...] = jnp.zeros_like(s_scr)

        _, la, delta = _gla_decay_terms(al_ref, wgu_ref, bg_ref, later_ref)
        kdec = (k_ref[...].astype(F32) * jnp.exp(delta)).astype(BF16)
        heads = range(GLA_HEADS)
        kcs = [slice(h * GLA_DK, (h + 1) * GLA_DK) for h in heads]
        vcs = [slice(h * GLA_DV, (h + 1) * GLA_DV) for h in heads]
        state = [s_scr[h] for h in heads]
        for c in range(nc):
            rows = slice(c * CHUNK, (c + 1) * CHUNK)
            first = slice(c * CHUNK, c * CHUNK + 1)
            dec = jnp.exp(la[first, :] + delta[first, :])
            upd_t = [_dot(v_ref[rows, vcs[h]], kdec[rows, kcs[h]], _TN) for h in heads]
            qs = [(q_ref[rows, kcs[h]].astype(F32) * (GLA_DK ** -0.5)).astype(BF16) for h in heads]
            for h in heads:
                state[h] = state[h] * dec[:, kcs[h]] + upd_t[h]
                st_ref[c, h] = state[h]
            o = [_dot(qs[h], state[h].astype(BF16), _NT) for h in heads]
            for h in heads:
                on = o[h] * _rms_stats(o[h]) * gn_ref[:, vcs[h]]
                rr = r_ref[rows, vcs[h]].astype(F32)
                y_ref[rows, vcs[h]] = (on * (rr * _sigmoid(rr))).astype(BF16)
        for h in heads:
            s_scr[h] = state[h]

    blk = lambda w, j: pl.BlockSpec((tT, w), lambda i: (i, j))
    return _call(
        body, name="gla_fwd", grid=(T // tT,),
        in_specs=[blk(512, 0), blk(512, 1), blk(1024, 1), blk(1024, 2), blk(LANES, 0)] + [_whole()] * 4,
        out_specs=[pl.BlockSpec((tT, GLA_V), lambda i: (i, 0)),
                   pl.BlockSpec((nc, GLA_HEADS, GLA_DV, GLA_DK), lambda i: (i, 0, 0, 0))],
        out_shape=[_sds((T, GLA_V), BF16), _sds((T // CHUNK, GLA_HEADS, GLA_DV, GLA_DK), F32)],
        scratch_shapes=[pltpu.VMEM((GLA_HEADS, GLA_DV, GLA_DK), F32)],
        args=(proj, proj, proj, proj, alow, wgu, b_gate, gn, _chunk_masks(tT, upper=True)), job=job)


def _sgu_mask():
    i = lax.broadcasted_iota(jnp.int32, (SGU_BLOCK, SGU_BLOCK), 0)
    j = lax.broadcasted_iota(jnp.int32, (SGU_BLOCK, SGU_BLOCK), 1)
    return lax.shift_right_logical(j, 6) <= lax.shift_right_logical(i, 6)


def _sgu_merge_fwd(x, proj, y_gla, ln_g, ln_b, w_sp, b_sp_t, w_bg, w_bs, w_o, g_pm, job=None):
    T = x.shape[0]
    tT = _row_tile(T, 512)
    nb = tT // SGU_BLOCK

    def body(x_ref, su_ref, sv_ref, gg_ref, gs_ref, yg_ref, lg_ref, lb_ref, w_ref, b_ref, wbg_ref, wbs_ref, wo_ref,
             g_ref, ys_ref, zg_ref, zs_ref, mg_ref, mix_ref, x1_ref):
        mask = _sgu_mask()
        for g in range(SGU_GROUPS):
            gc = slice(g * SGU_DG, (g + 1) * SGU_DG)
            wm = jnp.where(mask, w_ref[g], 0.0).astype(BF16)
            vf = _gelu(sv_ref[:, gc].astype(F32))
            mu = jnp.mean(vf, axis=-1, keepdims=True)
            vc = vf - mu
            rstd = lax.rsqrt(jnp.mean(vc * vc, axis=-1, keepdims=True) + EPS)
            vn = (vc * rstd * lg_ref[:, gc] + lb_ref[:, gc]).astype(BF16)
            u = _gelu(su_ref[:, gc].astype(F32))
            for b in range(nb):
                rows = slice(b * SGU_BLOCK, (b + 1) * SGU_BLOCK)
                mixed = _dot(wm, vn[rows, :]) + b_ref[:, g:g + 1]
                ys_ref[rows, gc] = (u[rows, :] * mixed).astype(BF16)
        zg = _dot(yg_ref[...], wbg_ref[...])
        zs = _dot(ys_ref[...], wbs_ref[...])
        zg_ref[...] = zg.astype(BF16)
        zs_ref[...] = zs.astype(BF16)
        merged = (_sigmoid(gg_ref[...].astype(F32)) * zg + _sigmoid(gs_ref[...].astype(F32)) * zs).astype(BF16)
        mg_ref[...] = merged
        mix = _dot(merged, wo_ref[...])
        mix_ref[...] = mix.astype(BF16)
        x1_ref[...] = x_ref[...] + mix * _rms_stats(mix) * g_ref[...]

    row = pl.BlockSpec((tT, D_MODEL), lambda i: (i, 0))
    blk = lambda j: pl.BlockSpec((tT, 1024), lambda i: (i, j))
    sds = lambda dt: _sds((T, D_MODEL), dt)
    return _call(body, name="sgu_merge_fwd", grid=(T // tT,), parallel=True,
                 in_specs=[row, blk(3), blk(4), blk(5), blk(6), row] + [_whole()] * 7
                 + [pl.BlockSpec((1, D_MODEL), lambda i: (0, 0))],
                 out_specs=[row] * 6, out_shape=[sds(BF16)] * 5 + [sds(F32)],
                 args=(x, proj, proj, proj, proj, y_gla, ln_g, ln_b, w_sp, b_sp_t, w_bg, w_bs, w_o, g_pm), job=job)


def _ffn_fwd_bwd(x1, tgt, w_fi_top, w_fi_bot, w_fo, g_pf, g_po):
    T = x1.shape[0]
    tT = _row_tile(T, 256)
    half = D_FF // 2
    kh = D_MODEL // 2

    def body(x1_ref, t_ref, top_ref, bot_ref, wfo_ref, gpf_ref, gpo_ref,
             h_ref, f_ref, dgu_ref, dy_ref, dx1_ref, loss_ref, dgpf_ref, dgpo_ref, gu_scr):
        @pl.when(pl.program_id(0) == 0)
        def _():
            loss_ref[...] = jnp.zeros_like(loss_ref)
            dgpf_ref[...] = jnp.zeros_like(dgpf_ref)
            dgpo_ref[...] = jnp.zeros_like(dgpo_ref)

        main = (half // 256) * 256
        pieces = (0, 1, None)

        def w_in_cols(ref, first_slab, p):
            if p is not None:
                return ref[first_slab + p, :, :main]
            return jnp.concatenate([ref[first_slab, :, main:], ref[first_slab + 1, :, main:]], axis=1)

        def w_out_rows(p):
            if p is not None:
                return wfo_ref[p * half:p * half + main, :]
            return jnp.concatenate([wfo_ref[main:half, :], wfo_ref[half + main:2 * half, :]], axis=0)

        def put(ref, base, p, val):
            if p is not None:
                ref[:, base + p * half:base + p * half + main] = val
            else:
                ref[:, base + main:base + half] = val[:, :half - main]
                ref[:, base + half + main:base + 2 * half] = val[:, half - main:]

        def get(ref, base, p):
            if p is not None:
                return ref[:, base + p * half:base + p * half + main]
            return jnp.concatenate([ref[:, base + main:base + half], ref[:, base + half + main:base + 2 * half]], axis=1)

        x1v = x1_ref[...]
        r2 = _rms_stats(x1v)
        h = (x1v * r2 * gpf_ref[...]).astype(BF16)
        h_ref[...] = h
        y = jnp.zeros((tT, D_MODEL), F32)
        for p in pieces:
            gate = _dot(h[:, :kh], w_in_cols(top_ref, 0, p)) + _dot(h[:, kh:], w_in_cols(bot_ref, 0, p))
            up = _dot(h[:, :kh], w_in_cols(top_ref, 2, p)) + _dot(h[:, kh:], w_in_cols(bot_ref, 2, p))
            put(gu_scr, 0, p, gate)
            put(gu_scr, D_FF, p, up)
            f = (gate * _sigmoid(gate) * up).astype(BF16)
            put(f_ref, 0, p, f)
            y = y + _dot(f, w_out_rows(p))
        r3 = _rms_stats(y)
        x2 = x1v + y * r3 * gpo_ref[...]
        err = x2 - t_ref[...]
        loss_ref[...] += jnp.sum(err * err) * (0.5 / D_MODEL)
        dx2 = err * (1.0 / D_MODEL)
        dy, dg = _rms_bwd(dx2, y, r3, gpo_ref[...])
        dgpo_ref[...] += jnp.sum(dg, axis=0, keepdims=True)
        dyb = dy.astype(BF16)
        dy_ref[...] = dyb
        dh_top = jnp.zeros((tT, kh), F32)
        dh_bot = jnp.zeros((tT, kh), F32)
        for p in pieces:
            df = _dot(dyb, w_out_rows(p), _NT)
            gate = get(gu_scr, 0, p)
            up = get(gu_scr, D_FF, p)
            sg = _sigmoid(gate)
            dgate = (df * up * (sg * (1.0 + gate * (1.0 - sg)))).astype(BF16)
            dup = (df * (gate * sg)).astype(BF16)
            put(dgu_ref, 0, p, dgate)
            put(dgu_ref, D_FF, p, dup)
            dh_top = dh_top + _dot(dgate, w_in_cols(top_ref, 0, p), _NT) + _dot(dup, w_in_cols(top_ref, 2, p), _NT)
            dh_bot = dh_bot + _dot(dgate, w_in_cols(bot_ref, 0, p), _NT) + _dot(dup, w_in_cols(bot_ref, 2, p), _NT)
        dh = jnp.concatenate([dh_top, dh_bot], axis=1)
        dx1n, dg2 = _rms_bwd(dh, x1v, r2, gpf_ref[...])
        dgpf_ref[...] += jnp.sum(dg2, axis=0, keepdims=True)
        dx1_ref[...] = dx2 + dx1n

    row = lambda w: pl.BlockSpec((tT, w), lambda i: (i, 0))
    vec = pl.BlockSpec((1, D_MODEL), lambda i: (0, 0))
    res, _ = _call(
        body, name="ffn_fwd_bwd", grid=(T // tT,),
        in_specs=[row(D_MODEL), row(D_MODEL), _whole(), _whole(), _whole(), vec, vec],
        out_specs=[row(D_MODEL), row(D_FF), row(2 * D_FF), row(D_MODEL), row(D_MODEL),
                   pl.BlockSpec((1, LANES), lambda i: (0, 0)), vec, vec],
        out_shape=[_sds((T, D_MODEL), BF16), _sds((T, D_FF), BF16), _sds((T, 2 * D_FF), BF16), _sds((T, D_MODEL), BF16),
                   _sds((T, D_MODEL), F32), _sds((1, LANES), F32), _sds((1, D_MODEL), F32), _sds((1, D_MODEL), F32)],
        scratch_shapes=[pltpu.VMEM((tT, 2 * D_FF), F32)], args=(x1, tgt, w_fi_top, w_fi_bot, w_fo, g_pf, g_po))
    return res


def _merge_sgu_bwd(dx1, mix, proj, zg, zs, w_bg, w_bs, w_o, g_pm, ln_g, ln_b, w_sp, b_sp_t, job=None):
    T = dx1.shape[0]
    tT = _row_tile(T, 256)
    nb = tT // SGU_BLOCK

    def body(dx1_ref, mix_ref, su_ref, sv_ref, gg_ref, gs_ref, zg_ref, zs_ref, wbg_ref, wbs_ref, wo_ref, g_ref,
             lg_ref, lb_ref, w_ref, b_ref,
             dmix_ref, dzg_ref, dzs_ref, dgate_ref, dyg_ref, dp_ref, dgpm_ref, dw_ref, dbt_ref, dlg_ref, dlb_ref):
        @pl.when(pl.program_id(0) == 0)
        def _():
            for ref in (dgpm_ref, dw_ref, dbt_ref, dlg_ref, dlb_ref):
                ref[...] = jnp.zeros_like(ref)

        mix = mix_ref[...].astype(F32)
        dmix, dg = _rms_bwd(dx1_ref[...], mix, _rms_stats(mix), g_ref[...])
        dgpm_ref[...] += jnp.sum(dg, axis=0, keepdims=True)
        dmb = dmix.astype(BF16)
        dmix_ref[...] = dmb
        dmerged = _dot(dmb, wo_ref[...], _NT)
        dys = None
        for k, (gate_ref, z_ref, w_br_ref, dz_ref) in enumerate(((gg_ref, zg_ref, wbg_ref, dzg_ref),
                                                                 (gs_ref, zs_ref, wbs_ref, dzs_ref))):
            sg = _sigmoid(gate_ref[...].astype(F32))
            dz = (dmerged * sg).astype(BF16)
            dz_ref[...] = dz
            dgate_ref[:, k * 1024:(k + 1) * 1024] = (dmerged * z_ref[...].astype(F32) * (sg * (1.0 - sg))).astype(BF16)
            dy_branch = _dot(dz, w_br_ref[...], _NT)
            if k == 0:
                dyg_ref[...] = dy_branch.astype(BF16)
            else:
                dys = dy_branch

        mask = _sgu_mask()
        lane = lax.broadcasted_iota(jnp.int32, (SGU_BLOCK, LANES), 1)
        for g in range(SGU_GROUPS):
            gc = slice(g * SGU_DG, (g + 1) * SGU_DG)
            gc_v = slice(1024 + g * SGU_DG, 1024 + (g + 1) * SGU_DG)
            wm = jnp.where(mask, w_ref[g], 0.0).astype(BF16)
            vf, dvf_dsv = _gelu_and_grad(sv_ref[:, gc].astype(F32))
            mu = jnp.mean(vf, axis=-1, keepdims=True)
            vc = vf - mu
            rstd = lax.rsqrt(jnp.mean(vc * vc, axis=-1, keepdims=True) + EPS)
            vhat = vc * rstd
            vn = (vhat * lg_ref[:, gc] + lb_ref[:, gc]).astype(BF16)
            u, du_dsu = _gelu_and_grad(su_ref[:, gc].astype(F32))
            dy = dys[:, gc]
            dmixed = (dy * u).astype(BF16)
            dvn_parts = []
            dw_acc = jnp.zeros((SGU_BLOCK, SGU_BLOCK), F32)
            db_acc = jnp.zeros((SGU_BLOCK, 1), F32)
            for b in range(nb):
                rows = slice(b * SGU_BLOCK, (b + 1) * SGU_BLOCK)
                mixed = _dot(wm, vn[rows, :]) + b_ref[:, g:g + 1]
                dp_ref[rows, gc] = (dy[rows, :] * mixed * du_dsu[rows, :]).astype(BF16)
                dvn_parts.append(_dot(wm, dmixed[rows, :], _TN))
                dw_acc = dw_acc + _dot(dmixed[rows, :], vn[rows, :], _NT)
                db_acc = db_acc + jnp.sum(dmixed[rows, :].astype(F32), axis=-1, keepdims=True)
            dw_ref[g] += jnp.where(mask, dw_acc, 0.0)
            dbt_ref[...] += jnp.where(lane == g, db_acc, 0.0)
            dvn = jnp.concatenate(dvn_parts, axis=0)
            dlg_ref[:, gc] += jnp.sum(dvn * vhat, axis=0, keepdims=True)
            dlb_ref[:, gc] += jnp.sum(dvn, axis=0, keepdims=True)
            dvh = dvn * lg_ref[:, gc]
            dvf = rstd * (dvh - jnp.mean(dvh, axis=-1, keepdims=True)
                          - vhat * jnp.mean(dvh * vhat, axis=-1, keepdims=True))
            dp_ref[:, gc_v] = (dvf * dvf_dsv).astype(BF16)

    row = pl.BlockSpec((tT, D_MODEL), lambda i: (i, 0))
    blk = lambda j: pl.BlockSpec((tT, 1024), lambda i: (i, j))
    vec = pl.BlockSpec((1, D_MODEL), lambda i: (0, 0))
    wide = lambda w: pl.BlockSpec((tT, w), lambda i: (i, 0))
    sds = _sds((T, D_MODEL), BF16)
    return _call(
        body, name="merge_sgu_bwd", grid=(T // tT,),
        in_specs=[row, row, blk(3), blk(4), blk(5), blk(6), row, row] + [_whole()] * 3 + [vec] + [_whole()] * 4,
        out_specs=[row, row, row, wide(W_MRG), row, wide(W_SGU), vec,
                   pl.BlockSpec((SGU_GROUPS, SGU_BLOCK, SGU_BLOCK), lambda i: (0, 0, 0)),
                   pl.BlockSpec((SGU_BLOCK, LANES), lambda i: (0, 0)), vec, vec],
        out_shape=[sds, sds, sds, _sds((T, W_MRG), BF16), sds, _sds((T, W_SGU), BF16), _sds((1, D_MODEL), F32),
                   _sds((SGU_GROUPS, SGU_BLOCK, SGU_BLOCK), F32), _sds((SGU_BLOCK, LANES), F32),
                   _sds((1, 1024), F32), _sds((1, 1024), F32)],
        args=(dx1, mix, proj, proj, proj, proj, zg, zs, w_bg, w_bs, w_o, g_pm, ln_g, ln_b, w_sp, b_sp_t), job=job)


def _gla_bwd(proj, alow, wgu, b_gate, gn, states, dy_gla, job=None):
    T = proj.shape[0]
    tT = _row_tile(T, 512)
    nc = tT // CHUNK
    nt = T // tT

    def body(q_ref, k_ref, v_ref, r_ref, al_ref, wgu_ref, bg_ref, gn_ref, later_ref, earlier_ref, st_ref, sp_ref, dy_ref,
             dp_ref, dal_ref, dgn_ref, dbg_ref, dwgu_ref, g_scr, dd_scr, dt_scr):
        step = pl.program_id(0)

        @pl.when(step == 0)
        def _():
            g_scr[...] = jnp.zeros_like(g_scr)
            dgn_ref[...] = jnp.zeros_like(dgn_ref)
            dbg_ref[...] = jnp.zeros_like(dbg_ref)
            dwgu_ref[...] = jnp.zeros_like(dwgu_ref)

        has_prev = jnp.where(step == nt - 1, 0.0, 1.0)
        logit, la, delta = _gla_decay_terms(al_ref, wgu_ref, bg_ref, later_ref)
        e = jnp.exp(delta)
        kdec_f = k_ref[...].astype(F32) * e
        kdec = kdec_f.astype(BF16)
        heads = range(GLA_HEADS)
        kcs = [slice(h * GLA_DK, (h + 1) * GLA_DK) for h in heads]
        vcs = [slice(h * GLA_DV, (h + 1) * GLA_DV) for h in heads]
        carry = [g_scr[h] for h in heads]
        dgn_acc = [jnp.zeros((1, GLA_DV), F32) for _ in heads]
        for c in reversed(range(nc)):
            rows = slice(c * CHUNK, (c + 1) * CHUNK)
            first = slice(c * CHUNK, c * CHUNK + 1)
            dec = jnp.exp(la[first, :] + delta[first, :])
            s_b = [st_ref[c, h].astype(BF16) for h in heads]
            qs = [(q_ref[rows, kcs[h]].astype(F32) * (GLA_DK ** -0.5)).astype(BF16) for h in heads]
            o = [_dot(qs[h], s_b[h], _NT) for h in heads]
            do = []
            for h in heads:
                rstd = _rms_stats(o[h])
                ohat = o[h] * rstd
                gnh = gn_ref[:, vcs[h]]
                dy = dy_ref[rows, vcs[h]].astype(F32)
                rr = r_ref[rows, vcs[h]].astype(F32)
                sg = _sigmoid(rr)
                don = dy * (rr * sg)
                dp_ref[rows, OFF_R + h * GLA_DV:OFF_R + (h + 1) * GLA_DV] = (
                    dy * (ohat * gnh) * (sg * (1.0 + rr * (1.0 - sg)))).astype(BF16)
                dgn_acc[h] = dgn_acc[h] + jnp.sum(don * ohat, axis=0, keepdims=True)
                dn = don * gnh
                do.append((rstd * (dn - ohat * jnp.mean(dn * ohat, axis=-1, keepdims=True))).astype(BF16))
            dq = [_dot(do[h], s_b[h]) for h in heads]
            g_t = [_dot(do[h], qs[h], _TN) + carry[h] for h in heads]
            g_b = [g_t[h].astype(BF16) for h in heads]
            dv = [_dot(kdec[rows, kcs[h]], g_b[h], _NT) for h in heads]
            dkdec = [_dot(v_ref[rows, vcs[h]], g_b[h]) for h in heads]
            for h in heads:
                s_prev = st_ref[c - 1, h] if c > 0 else sp_ref[0, h] * has_prev
                ddec = jnp.sum(g_t[h] * s_prev, axis=0, keepdims=True)
                carry[h] = g_t[h] * dec[:, kcs[h]]
                dp_ref[rows, OFF_Q + h * GLA_DK:OFF_Q + (h + 1) * GLA_DK] = (dq[h] * (GLA_DK ** -0.5)).astype(BF16)
                dp_ref[rows, OFF_V + h * GLA_DV:OFF_V + (h + 1) * GLA_DV] = dv[h].astype(BF16)
                dp_ref[rows, OFF_K + h * GLA_DK:OFF_K + (h + 1) * GLA_DK] = (dkdec[h] * e[rows, kcs[h]]).astype(BF16)
                dd_scr[rows, kcs[h]] = dkdec[h] * kdec_f[rows, kcs[h]]
                dt_scr[rows, kcs[h]] = jnp.broadcast_to(ddec * dec[:, kcs[h]], (CHUNK, GLA_DK))
        for h in heads:
            g_scr[h] = carry[h]
            dgn_ref[:, vcs[h]] += dgn_acc[h]
        dla = _dot_exact_lhs(earlier_ref[...], dd_scr[...]) + dt_scr[...]
        dlogit = dla * (1.0 / GLA_TAU) * _sigmoid(-logit)
        dbg_ref[...] += jnp.sum(dlogit, axis=0, keepdims=True)
        dwgu_ref[...] += _dot_bf16(al_ref[...], dlogit, _TN)
        dal_ref[...] = _dot_bf16(dlogit, wgu_ref[...], _NT).astype(BF16)

    rev = lambda i: nt - 1 - i
    blk = lambda w, j: pl.BlockSpec((tT, w), lambda i: (rev(i), j))
    st_blk = pl.BlockSpec((nc, GLA_HEADS, GLA_DV, GLA_DK), lambda i: (rev(i), 0, 0, 0))
    sp_blk = pl.BlockSpec((1, GLA_HEADS, GLA_DV, GLA_DK), lambda i: (jnp.maximum(rev(i) * nc - 1, 0), 0, 0, 0))
    return _call(
        body, name="gla_bwd", grid=(nt,),
        in_specs=[blk(512, 0), blk(512, 1), blk(1024, 1), blk(1024, 2), blk(LANES, 0)] + [_whole()] * 5
        + [st_blk, sp_blk, blk(GLA_V, 0)],
        out_specs=[blk(W_GLA, 0), blk(LANES, 0), pl.BlockSpec((1, GLA_V), lambda i: (0, 0)),
                   pl.BlockSpec((1, GLA_QK), lambda i: (0, 0)), pl.BlockSpec((LANES, GLA_QK), lambda i: (0, 0))],
        out_shape=[_sds((T, W_GLA), BF16), _sds((T, LANES), BF16), _sds((1, GLA_V), F32), _sds((1, GLA_QK), F32),
                   _sds((LANES, GLA_QK), F32)],
        scratch_shapes=[pltpu.VMEM((GLA_HEADS, GLA_DV, GLA_DK), F32), pltpu.VMEM((tT, GLA_QK), F32),
                        pltpu.VMEM((tT, GLA_QK), F32)],
        args=(proj, proj, proj, proj, alow, wgu, b_gate, gn, _chunk_masks(tT, upper=True), _chunk_masks(tT, upper=False),
              states, states, dy_gla), job=job)


def _inproj_bwd(x, dx1, g1, w_all, dparts, job=None):
    T = x.shape[0]
    tT = _row_tile(T, 512)
    offs = (0, W_GLA, W_GLA + W_SGU, N_MAIN)

    def body(x_ref, dx1_ref, g_ref, w_hbm, *rest):
        part_refs, (dx_ref, dg_ref, w_ref, w_sems) = rest[:len(offs)], rest[len(offs):]

        def compute(first):
            if first:
                copies = [pltpu.make_async_copy(w_hbm.at[:, pl.ds(off, p.shape[1])], w_ref.at[:, pl.ds(off, p.shape[1])],
                                                w_sems.at[k]) for k, (off, p) in enumerate(zip(offs, dparts))]
                for cp in copies:
                    cp.start()
            da = jnp.zeros((tT, D_MODEL), F32)
            for k, (off, p_ref) in enumerate(zip(offs, part_refs)):
                if first:
                    copies[k].wait()
                da = da + _dot(p_ref[...], w_ref[:, off:off + p_ref.shape[1]], _NT)
            xv = x_ref[...]
            dx, dg = _rms_bwd(da, xv, _rms_stats(xv), g_ref[...])
            dg_sum = jnp.sum(dg, axis=0, keepdims=True)
            dg_ref[...] = dg_sum if first else dg_ref[...] + dg_sum
            dx_ref[...] = dx1_ref[...] + dx

        first_step = pl.program_id(0) == 0
        pl.when(first_step)(lambda: compute(True))
        pl.when(jnp.logical_not(first_step))(lambda: compute(False))

    row = lambda w: pl.BlockSpec((tT, w), lambda i: (i, 0))
    vec = pl.BlockSpec((1, D_MODEL), lambda i: (0, 0))
    return _call(
        body, name="inproj_bwd", grid=(T // tT,),
        in_specs=[row(D_MODEL), row(D_MODEL), vec, pl.BlockSpec(memory_space=pl.ANY)] + [row(p.shape[1]) for p in dparts],
        out_specs=[row(D_MODEL), vec], out_shape=[_sds((T, D_MODEL), F32), _sds((1, D_MODEL), F32)],
        scratch_shapes=[pltpu.VMEM(w_all.shape, BF16), pltpu.SemaphoreType.DMA((len(offs),))],
        args=(x, dx1, g1, w_all, *dparts), job=job)


def _tn_matmul(a, b, name, job=None):
    T, M = a.shape
    N = b.shape[1]
    tk = _row_tile(T, 1024)
    tm = M if M <= 1024 else 1408
    tn = N // 2 if N > 2048 else N
    assert M % tm == 0 and N % tn == 0

    def body(a_ref, b_ref, o_ref):
        @pl.when(pl.program_id(2) == 0)
        def _():
            o_ref[...] = _dot(a_ref[...], b_ref[...], _TN)

        @pl.when(pl.program_id(2) > 0)
        def _():
            o_ref[...] += _dot(a_ref[...], b_ref[...], _TN)

    res, jres = _call(
        body, name=name, grid=(M // tm, N // tn, T // tk),
        in_specs=[pl.BlockSpec((tk, tm), lambda i, j, k: (k, i)), pl.BlockSpec((tk, tn), lambda i, j, k: (k, j))],
        out_specs=[pl.BlockSpec((tm, tn), lambda i, j, k: (i, j))], out_shape=[_sds((M, N), F32)], args=(a, b), job=job)
    return res[0], jres


def _pad_rows(a, rows=8):
    return jnp.pad(a, ((0, rows - a.shape[0]), (0, LANES - a.shape[1])))


def _halves_view(dw):
    r = dw.shape[0] // N_CHIPS
    return dw.reshape(N_CHIPS, 2, r // 2, dw.shape[1])


def kernel(x, norm_pre_mix, w_in, w_gate_up, b_gate, gla_norm, sgu_ln_g, sgu_ln_b, w_spatial, b_spatial, w_branch_gla, w_branch_sgu, w_out, norm_post_mix, norm_pre_ffn, w_ffn_in, w_ffn_out, norm_post_ffn, loss_target, m_norm_pre_mix, m_w_in, m_w_gate_up, m_b_gate, m_gla_norm, m_sgu_ln_g, m_sgu_ln_b, m_w_spatial, m_b_spatial, m_w_branch_gla, m_w_branch_sgu, m_w_out, m_norm_post_mix, m_norm_pre_ffn, m_w_ffn_in, m_w_ffn_out, m_norm_post_ffn, v_norm_pre_mix, v_w_in, v_w_gate_up, v_b_gate, v_gla_norm, v_sgu_ln_g, v_sgu_ln_b, v_w_spatial, v_b_spatial, v_w_branch_gla, v_w_branch_sgu, v_w_out, v_norm_post_mix, v_norm_pre_ffn, v_w_ffn_in, v_w_ffn_out, v_norm_post_ffn):
    chip = 2 * lax.axis_index("x") + lax.axis_index("y")
    xt, tgt = x[0], loss_target[0]

    tiny = jnp.concatenate([w_gate_up[0], _pad_rows(gla_norm[0]), _pad_rows(sgu_ln_g[0]), _pad_rows(sgu_ln_b[0]),
                            jnp.zeros((24, LANES), F32)], axis=0)

    def with_own(gathered, own):
        return lax.dynamic_update_slice(gathered, own[None], (chip, 0, 0))

    w_in_t, m_in_t, v_in_t = w_in[0].T, m_w_in[0].T, v_w_in[0].T
    w_in_b = _transposed_cast(w_in_t)
    (*own_rows, fi_top, fi_bot), (g_in, g_tiny) = _cast_weights(
        [w_branch_gla[0], w_branch_sgu[0], w_out[0], w_ffn_out[0]], w_ffn_in[0], job=_job_gather([w_in_b, tiny]))
    g_tiny = with_own(g_tiny, tiny)
    w_all = _relayout_w_in(with_own(g_in, w_in_b))
    cols = lambda a: a.transpose(1, 0, 2).reshape(a.shape[1], N_CHIPS * a.shape[2])
    wgu = jnp.pad(cols(g_tiny[:, 0:16]), ((0, LANES - GLA_RANK), (0, 0)))
    gn = cols(g_tiny[:, 16:20, :64]).reshape(1, GLA_V)
    ln_g = cols(g_tiny[:, 24:28, :64]).reshape(1, 1024)
    ln_b = cols(g_tiny[:, 32:36, :64]).reshape(1, 1024)
    b_sp_t = jnp.pad(b_spatial[0].T, ((0, 0), (0, LANES - SGU_GROUPS)))
    w_sp = w_spatial[0]

    (a, proj, alow), g_rows = _inproj_fwd(xt, norm_pre_mix, w_all, job=_job_gather(own_rows))
    m_in_t, alow = lax.optimization_barrier((m_in_t, alow))
    v_in_t, a = lax.optimization_barrier((v_in_t, a))
    rows = lambda g: g.reshape(N_CHIPS * g.shape[1], g.shape[2])
    w_bg, w_bs, w_o, w_fo = [rows(with_own(g, own)) for g, own in zip(g_rows, own_rows)]
    (y_gla, states), (g_top,) = _gla_fwd(proj, alow, wgu, b_gate, gn, job=_job_gather([fi_top]))
    (y_sgu, zg, zs, merged, mix, x1), (g_bot,) = _sgu_merge_fwd(
        xt, proj, y_gla, ln_g, ln_b, w_sp, b_sp_t, w_bg, w_bs, w_o, norm_post_mix, job=_job_gather([fi_bot]))
    h, f, dgu, dy, dx1, loss, d_gpf, d_gpo = _ffn_fwd_bwd(x1, tgt, with_own(g_top, fi_top), with_own(g_bot, fi_bot),
                                                          w_fo, norm_pre_ffn, norm_post_ffn)

    whole = lambda hs: [[(h_, None)] for h_ in hs]
    dw_fo, _ = _tn_matmul(f, dy, "dw_ffn_out")
    dw_fo4 = _halves_view(dw_fo)
    dw_fi, (q_fo,) = _tn_matmul(h, dgu, "dw_ffn_in", job=_job_to_other_core([[(dw_fo4, 0)]]))
    c_fo, = _presum([dw_fo4], [q_fo], "presum_ffn_out")
    (dmix, dzg, dzs, dp_mrg, dyg, dp_sgu, d_gpm, d_wsp, d_bsp_t, d_lng, d_lnb), (s_fo, q_fi) = _merge_sgu_bwd(
        dx1, mix, proj, zg, zs, w_bg, w_bs, w_o, norm_post_mix, ln_g, ln_b, w_sp, b_sp_t,
        job=_join(_job_scatter([c_fo]), _job_to_other_core([[(dw_fi, 0)]])))
    c_fi, = _presum([dw_fi], [q_fi], "presum_ffn_in")
    dw_c, _ = _tn_matmul(a, dp_mrg, "dw_in_merge")
    dw_b, _ = _tn_matmul(a, dp_sgu, "dw_in_sgu")
    dw_o4 = _halves_view(_tn_matmul(merged, dmix, "dw_out")[0])
    dw_bg4 = _halves_view(_tn_matmul(y_gla, dzg, "dw_branch_gla")[0])
    dw_bs4 = _halves_view(_tn_matmul(y_sgu, dzs, "dw_branch_sgu")[0])
    h_fo, = _sum_slots([c_fo], [s_fo], "sum_ffn_out")
    (dp_gla, dal, d_gn, d_bg, d_wgu), (s_fi, t_fo, q_b, q_c, q_o, q_bg, q_bs) = _gla_bwd(
        proj, alow, wgu, b_gate, gn, states, dyg,
        job=_join(_job_scatter([c_fi]), _job_to_other_core(
            whole([h_fo]) + [[(dw_b, 0)], [(dw_c, 0)], [(dw_o4, 0)], [(dw_bg4, 0)], [(dw_bs4, 0)]])))
    c_o, c_bg, c_bs = _presum([dw_o4, dw_bg4, dw_bs4], [q_o, q_bg, q_bs], "presum_out_branches")
    h_fi, = _sum_slots([c_fi], [s_fi], "sum_ffn_in")
    dw_d, _ = _tn_matmul(a, dal, "dw_in_gate")
    dw_a, (s_o, s_bg, s_bs, t_fi, q_d) = _tn_matmul(
        a, dp_gla, "dw_in_gla",
        job=_join(_job_scatter([c_o, c_bg, c_bs]), _job_to_other_core(whole([h_fi]) + [[(dw_d, 0)]])))
    h_o, h_bg, h_bs = _sum_slots([c_o, c_bg, c_bs], [s_o, s_bg, s_bs], "sum_out_branches")

    grads, deltas, new_m, new_v = {}, {}, {}, {}

    def update(call, names, ws, ms, vs, g_mine, g_theirs, job=None):
        res, jres = _adamw([w[0] for w in ws], [m[0] for m in ms], [v[0] for v in vs], g_mine, g_theirs, call, job=job)
        for name, (g, d, m2, v2) in zip(names, res):
            grads[name], deltas[name], new_m[name], new_v[name] = g[None], d[None], m2[None], v2[None]
        return jres

    dw_in = [(dw_a, 0), (dw_b, W_GLA), (dw_c, W_GLA + W_SGU), (dw_d, N_MAIN)]
    q_a, t_o, t_bg, t_bs = update("adamw_w_ffn_out", ["w_ffn_out"], [w_ffn_out], [m_w_ffn_out], [v_w_ffn_out],
                                  [[h_fo]], [[t_fo]],
                                  job=_job_to_other_core([[(dw_a, 0)]] + whole([h_o, h_bg, h_bs])))
    q_in = [q_a, q_b, q_c, q_d]
    hr_in = D_MODEL // 2
    c_in_a, _ = _presum_w_in(dw_in, q_in, 0, hr_in // 8, "presum_w_in_a")
    c_in_b, (s_in_a,) = _presum_w_in(dw_in, q_in, hr_in // 8, 7 * hr_in // 8, "presum_w_in_b",
                                     job=_job_scatter([c_in_a]))
    update("adamw_w_ffn_in", ["w_ffn_in"], [w_ffn_in], [m_w_ffn_in], [v_w_ffn_in], [[h_fi]], [[t_fi]])
    update("adamw_out_branches", ["w_out", "w_branch_gla", "w_branch_sgu"], [w_out, w_branch_gla, w_branch_sgu],
           [m_w_out, m_w_branch_gla, m_w_branch_sgu], [v_w_out, v_w_branch_gla, v_w_branch_sgu],
           [[h_o], [h_bg], [h_bs]], [[t_o], [t_bg], [t_bs]])
    (grad_x, d_g1), (s_in_b,) = _inproj_bwd(xt, dx1, norm_pre_mix, w_all, (dp_gla, dp_sgu, dp_mrg, dal),
                                            job=_job_scatter([c_in_b]))
    h_in = _sum_slots([c_in_a], [s_in_a], "sum_w_in_a") + _sum_slots([c_in_b], [s_in_b], "sum_w_in_b")
    t_in = _run_job(_job_to_other_core(whole(h_in)), "swap_w_in")
    for store, val in zip((grads, deltas, new_m, new_v),
                          _adamw_transposed(w_in_t, m_in_t, v_in_t, h_in, t_in, "adamw_w_in")):
        store["w_in"] = val.T[None]

    small_names = ["w_spatial", "w_gate_up", "norm_pre_mix", "norm_post_mix", "norm_pre_ffn", "norm_post_ffn", "b_gate",
                   "b_spatial", "gla_norm", "sgu_ln_g", "sgu_ln_b"]
    loss_out, small = _small_adamw(
        _small_sum([d_wsp, d_wgu, d_g1, d_gpm, d_gpf, d_gpo, d_bg, d_bsp_t, d_gn, d_lng, d_lnb, loss]),
        [w_spatial, w_gate_up, norm_pre_mix, norm_post_mix, norm_pre_ffn, norm_post_ffn, b_gate, b_spatial, gla_norm,
         sgu_ln_g, sgu_ln_b],
        [m_w_spatial, m_w_gate_up, m_norm_pre_mix, m_norm_post_mix, m_norm_pre_ffn, m_norm_post_ffn, m_b_gate,
         m_b_spatial, m_gla_norm, m_sgu_ln_g, m_sgu_ln_b],
        [v_w_spatial, v_w_gate_up, v_norm_pre_mix, v_norm_post_mix, v_norm_pre_ffn, v_norm_post_ffn, v_b_gate,
         v_b_spatial, v_gla_norm, v_sgu_ln_g, v_sgu_ln_b])
    for store, vals in zip((grads, deltas, new_m, new_v), small):
        store.update(zip(small_names, vals))

    order = ["norm_pre_mix", "w_in", "w_gate_up", "b_gate", "gla_norm", "sgu_ln_g", "sgu_ln_b", "w_spatial", "b_spatial",
             "w_branch_gla", "w_branch_sgu", "w_out", "norm_post_mix", "norm_pre_ffn", "w_ffn_in", "w_ffn_out",
             "norm_post_ffn"]
    out = [loss_out, grad_x[None]]
    for store in (grads, deltas, new_m, new_v):
        out.extend(store[n] for n in order)
    return tuple(out)
```

```python
import jax
import jax.numpy as jnp
from jax import lax
from jax.experimental import pallas as pl
from jax.experimental.pallas import tpu as pltpu

F32 = jnp.float32
BF16 = jnp.bfloat16

D_MODEL = 1024
GLA_HEADS = 4
GLA_DK = 128
GLA_DV = 256
GLA_QK = GLA_HEADS * GLA_DK
GLA_V = GLA_HEADS * GLA_DV
GLA_RANK = 16
GLA_TAU = 16.0
CHUNK = 64
SGU_GROUPS = 4
SGU_BLOCK = 128
SGU_DG = 256
D_FF = 2816
EPS = 1e-6
LANES = 128

OFF_Q, OFF_K, OFF_V, OFF_R, OFF_SU, OFF_SV, OFF_GG, OFF_GS, OFF_AL = 0, 512, 1024, 2048, 3072, 4096, 5120, 6144, 7168
W_GLA, W_SGU, W_MRG = 3072, 2048, 2048
N_MAIN = 7168
N_ALL = N_MAIN + LANES
_IN_SPLITS = (GLA_QK, GLA_QK, GLA_V, GLA_V, GLA_RANK, 1024, 1024, 1024, 1024)
_IN_STARTS = tuple(sum(_IN_SPLITS[:i]) for i in range(len(_IN_SPLITS) + 1))
_IN_DST = (OFF_Q, OFF_K, OFF_V, OFF_R, OFF_AL, OFF_SU, OFF_SV, OFF_GG, OFF_GS)
D_IN = _IN_STARTS[-1]

ADAM_LR = 0.001
ADAM_B1 = 0.9
ADAM_B2 = 0.999
ADAM_EPS = 1e-08
ADAM_WD = 0.01
ADAM_STEP = 10

VMEM_LIMIT_BYTES = 56 * 1024 * 1024
N_CHIPS = 4
N_PEER = N_CHIPS - 1
N_DEV = 8
MESH = pl.DeviceIdType.MESH

_NN = (((1,), (0,)), ((), ()))
_NT = (((1,), (1,)), ((), ()))
_TN = (((0,), (0,)), ((), ()))


def _dot(a, b, dims=_NN):
    return lax.dot_general(a, b, dims, preferred_element_type=F32)


def _split(x):
    hi = x.astype(BF16)
    lo = (x - hi.astype(F32)).astype(BF16)
    return hi, lo


def _dot_bf16(a, b, dims=_NN):
    return _dot(a.astype(BF16), b.astype(BF16), dims)


def _dot_exact_lhs(m, x):
    xh, xl = _split(x)
    return _dot(m, xh) + _dot(m, xl)


def _sigmoid(x):
    return 0.5 * jnp.tanh(0.5 * x) + 0.5


def _log_sigmoid(x):
    return jnp.minimum(x, 0.0) - jnp.log(1.0 + jnp.exp(-jnp.abs(x)))


_GELU_C = 0.7978845608028654
_GELU_A = 0.044715


def _gelu_and_grad(x):
    x2 = x * x
    t = jnp.tanh(_GELU_C * (x + _GELU_A * x * x2))
    g = 0.5 * x * (1.0 + t)
    dg = 0.5 * (1.0 + t) + 0.5 * x * (1.0 - t * t) * (_GELU_C * (1.0 + 3.0 * _GELU_A * x2))
    return g, dg


def _gelu(x):
    t = jnp.tanh(_GELU_C * (x + _GELU_A * x * x * x))
    return 0.5 * x * (1.0 + t)


def _rms_stats(x):
    return lax.rsqrt(jnp.mean(x * x, axis=-1, keepdims=True) + EPS)


def _rms_bwd(dout, y, r, g):
    yhat = y * r
    dn = dout * g
    dy = r * (dn - yhat * jnp.mean(dn * yhat, axis=-1, keepdims=True))
    return dy, dout * yhat


def _whole():
    return pl.BlockSpec(memory_space=pltpu.VMEM)


def _row_tile(T, want):
    t = min(T, want)
    assert T % t == 0
    return t


def _chunk_masks(tT, upper):
    row = lax.broadcasted_iota(jnp.int32, (tT, tT), 0)
    col = lax.broadcasted_iota(jnp.int32, (tT, tT), 1)
    same = (row // CHUNK) == (col // CHUNK)
    tri = (col > row) if upper else (col < row)
    return jnp.where(same & tri, 1.0, 0.0).astype(BF16)


class _Job:
    def __init__(self, ins, out_shapes, scratch, start, finish, mid=None):
        self.ins, self.out_shapes, self.scratch = list(ins), list(out_shapes), list(scratch)
        self.start, self.finish, self.mid = start, finish, mid


def _join(*jobs):
    def split(refs, counts):
        out, at = [], 0
        for n in counts:
            out.append(refs[at:at + n])
            at += n
        return out

    ni, no, ns = [len(j.ins) for j in jobs], [len(j.out_shapes) for j in jobs], [len(j.scratch) for j in jobs]

    def start(ins, outs, scr):
        for j, a, b, c in zip(jobs, split(ins, ni), split(outs, no), split(scr, ns)):
            j.start(a, b, c)

    def finish(ins, outs, scr):
        for j, a, b, c in zip(jobs, split(ins, ni), split(outs, no), split(scr, ns)):
            j.finish(a, b, c)

    def mid(ins, outs, scr):
        for j, a, b, c in zip(jobs, split(ins, ni), split(outs, no), split(scr, ns)):
            if j.mid is not None:
                j.mid(a, b, c)

    return _Job(sum((j.ins for j in jobs), []), sum((j.out_shapes for j in jobs), []),
                sum((j.scratch for j in jobs), []), start, finish, mid if any(j.mid for j in jobs) else None)


def _mesh_pos():
    return lax.axis_index("x"), lax.axis_index("y"), lax.axis_index("c")


def _peer_chips(xi, yi):
    return [(1 - xi, yi), (xi, 1 - yi), (1 - xi, 1 - yi)]


def _half(ci, rows):
    return pl.ds(pl.multiple_of(ci * rows, 8), rows)


def _sds(shape, dtype):
    return jax.ShapeDtypeStruct(tuple(shape), dtype)


def _job_gather(arrs):
    n = len(arrs)
    kinds = 12
    Y0, Y1, X1, X0, ON_X, ON_Y, D2D = 0, 1, 2, 3, 4, 5, 6

    def copies(ins, outs, scr):
        send_sems, recv_sems = scr
        xi, yi, ci = _mesh_pos()
        me, cx, cy, cd = 2 * xi + yi, 2 * (1 - xi) + yi, 2 * xi + (1 - yi), 2 * (1 - xi) + (1 - yi)
        to_x, to_y, to_core = (1 - xi, yi, ci), (xi, 1 - yi, ci), (xi, yi, 1 - ci)
        table = []
        for k in range(n):
            qr = arrs[k].shape[0] // 4

            def rows(core, q):
                return pl.ds(pl.multiple_of((2 * core + q) * qr, 8), qr)

            def cp(kind, src, dst, to):
                s = k * kinds + kind
                return pltpu.make_async_remote_copy(src_ref=src, dst_ref=dst, send_sem=send_sems.at[s],
                                                    recv_sem=recv_sems.at[s], device_id=to, device_id_type=MESH)

            def slab(chip, core, q):
                return outs[k].at[chip, rows(core, q)]

            t = {}
            for kind, q, to, frm in ((Y0, 0, to_y, cy), (Y1, 1, to_y, cy), (X1, 1, to_x, cx), (X0, 0, to_x, cx)):
                mine = ins[k].at[rows(ci, q)]
                t[kind] = (cp(kind, mine, slab(me, ci, q), to), cp(kind, mine, slab(frm, ci, q), to))
            t[ON_X] = (cp(ON_X, slab(cy, ci, 0), slab(cy, ci, 0), to_x), cp(ON_X, slab(cy, ci, 0), slab(cd, ci, 0), to_x))
            t[ON_Y] = (cp(ON_Y, slab(cx, ci, 1), slab(cx, ci, 1), to_y), cp(ON_Y, slab(cx, ci, 1), slab(cd, ci, 1), to_y))
            for i, (chip, q) in enumerate(((cy, 0), (cy, 1), (cx, 1), (cx, 0), (cd, 0), (cd, 1))):
                t[D2D + i] = (cp(D2D + i, slab(chip, ci, q), slab(chip, ci, q), to_core),
                              cp(D2D + i, slab(chip, ci, q), slab(chip, 1 - ci, q), to_core))
            table.append(t)
        return table

    def start(ins, outs, scr):
        table = copies(ins, outs, scr)
        for kind in (Y0, X1, Y1, X0):
            for t in table:
                t[kind][0].start()

    def arrived(table, kind, then):
        for t in table:
            t[kind][1].wait_recv()
            for nxt in then:
                t[nxt][0].start()

    def mid(ins, outs, scr):
        table = copies(ins, outs, scr)
        arrived(table, Y0, (ON_X, D2D + 0))
        arrived(table, X1, (ON_Y, D2D + 2))

    def finish(ins, outs, scr):
        table = copies(ins, outs, scr)
        arrived(table, Y1, (D2D + 1,))
        arrived(table, X0, (D2D + 3,))
        arrived(table, ON_X, (D2D + 4,))
        arrived(table, ON_Y, (D2D + 5,))
        for t in table:
            for i in range(6):
                t[D2D + i][1].wait_recv()
            for kind in range(kinds):
                t[kind][0].wait_send()

    dma = pltpu.SemaphoreType.DMA
    return _Job(arrs, [_sds((N_CHIPS,) + a.shape, a.dtype) for a in arrs], [dma((n * kinds,))] * 2, start, finish, mid)


def _job_scatter(parts):
    n = len(parts)

    def copies(ins, outs, scr):
        send_sems, recv_sems = scr
        xi, yi, ci = _mesh_pos()
        res = []
        for k in range(n):
            for j, (px, py) in enumerate(_peer_chips(xi, yi)):
                s = k * N_PEER + j
                res.append(pltpu.make_async_remote_copy(
                    src_ref=ins[k].at[2 * px + py], dst_ref=outs[k].at[j], send_sem=send_sems.at[s],
                    recv_sem=recv_sems.at[s], device_id=(px, py, ci), device_id_type=MESH))
        return res

    def start(ins, outs, scr):
        for cp in copies(ins, outs, scr):
            cp.start()

    def finish(ins, outs, scr):
        for cp in copies(ins, outs, scr):
            cp.wait_recv()
            cp.wait_send()

    dma = pltpu.SemaphoreType.DMA
    return _Job(parts, [_sds((N_PEER,) + p.shape[1:], p.dtype) for p in parts], [dma((n * N_PEER,))] * 2, start, finish)


def _job_to_other_core(groups):
    pieces = [(g, a, off) for g, group in enumerate(groups) for a, off in group]
    n = len(pieces)

    def geometry(group):
        a0, off0 = group[0]
        if off0 is None:
            return a0.shape
        if a0.ndim == 4:
            return (N_CHIPS, a0.shape[2], a0.shape[3])
        return (a0.shape[0] // 2, sum(a.shape[1] for a, _ in group))

    def copies(ins, outs, scr):
        send_sems, recv_sems = scr
        xi, yi, ci = _mesh_pos()
        res = []
        for p, (g, a, off) in enumerate(pieces):
            if off is None:
                give, land = ins[p], outs[g]
            elif a.ndim == 4:
                give, land = ins[p].at[pl.ds(0, N_CHIPS), 1 - ci], outs[g]
            else:
                hr, w = a.shape[0] // 2, a.shape[1]
                give, land = ins[p].at[_half(1 - ci, hr)], outs[g].at[pl.ds(0, hr), pl.ds(off, w)]
            res.append(pltpu.make_async_remote_copy(
                src_ref=give, dst_ref=land, send_sem=send_sems.at[p], recv_sem=recv_sems.at[p],
                device_id=(xi, yi, 1 - ci), device_id_type=MESH))
        return res

    def start(ins, outs, scr):
        for cp in copies(ins, outs, scr):
            cp.start()

    def finish(ins, outs, scr):
        for cp in copies(ins, outs, scr):
            cp.wait_recv()
            cp.wait_send()

    dma = pltpu.SemaphoreType.DMA
    return _Job([a for _, a, _ in pieces], [_sds(geometry(group), group[0][0].dtype) for group in groups],
                [dma((n,))] * 2, start, finish)


def _call(body, *, name, grid, in_specs, out_specs, out_shape, args, scratch_shapes=(), parallel=False, job=None,
          by_core=False):
    n_in, n_out, n_scr = len(in_specs), len(out_specs), len(scratch_shapes)
    hbm = pl.BlockSpec(memory_space=pl.ANY)
    n_ji, n_jo = (len(job.ins), len(job.out_shapes)) if job is not None else (0, 0)
    lead = 1 if by_core else 0

    def kernel_fn(*refs):
        core, refs = refs[:lead], refs[lead:]
        ins, refs = refs[:n_in], refs[n_in:]
        j_ins, refs = refs[:n_ji], refs[n_ji:]
        outs, refs = refs[:n_out], refs[n_out:]
        j_outs, refs = refs[:n_jo], refs[n_jo:]
        scr, j_scr = refs[:n_scr], refs[n_scr:]
        if job is None:
            body(*core, *ins, *outs, *scr)
            return
        ids = [pl.program_id(d) for d in range(len(grid))]
        first = ids[0] == 0
        last = ids[0] == grid[0] - 1
        for d in range(1, len(grid)):
            first = first & (ids[d] == 0)
            last = last & (ids[d] == grid[d] - 1)

        @pl.when(first)
        def _():
            job.start(j_ins, j_outs, j_scr)

        if job.mid is not None and grid[0] >= 4:
            half_way = ids[0] == grid[0] // 2
            for d in range(1, len(grid)):
                half_way = half_way & (ids[d] == 0)

            @pl.when(half_way)
            def _():
                job.mid(j_ins, j_outs, j_scr)

        body(*core, *ins, *outs, *scr)

        @pl.when(last)
        def _():
            if job.mid is not None and grid[0] < 4:
                job.mid(j_ins, j_outs, j_scr)
            job.finish(j_ins, j_outs, j_scr)

    sem = ("parallel" if parallel and job is None else "arbitrary",) * len(grid)
    all_in = list(in_specs) + [hbm] * n_ji
    all_out = list(out_specs) + [hbm] * n_jo
    all_scratch = list(scratch_shapes) + (job.scratch if job is not None else [])
    all_shapes = list(out_shape) + (job.out_shapes if job is not None else [])
    all_args = list(args) + (job.ins if job is not None else [])
    params = pltpu.CompilerParams(dimension_semantics=sem, vmem_limit_bytes=VMEM_LIMIT_BYTES)
    if by_core:
        spec = pltpu.PrefetchScalarGridSpec(num_scalar_prefetch=1, grid=grid, in_specs=all_in, out_specs=all_out,
                                            scratch_shapes=all_scratch)
        core = lax.axis_index("c").astype(jnp.int32).reshape(1)
        res = pl.pallas_call(kernel_fn, name=name, grid_spec=spec, out_shape=all_shapes, compiler_params=params)(
            core, *all_args)
    else:
        res = pl.pallas_call(kernel_fn, name=name, grid=grid, in_specs=all_in, out_specs=all_out, out_shape=all_shapes,
                             scratch_shapes=all_scratch, compiler_params=params)(*all_args)
    return list(res[:n_out]), list(res[n_out:])


def _run_job(job, name):
    n_i, n_o = len(job.ins), len(job.out_shapes)

    def body(*refs):
        ins, outs, scr = refs[:n_i], refs[n_i:n_i + n_o], refs[n_i + n_o:]
        job.start(ins, outs, scr)
        if job.mid is not None:
            job.mid(ins, outs, scr)
        job.finish(ins, outs, scr)

    hbm = pl.BlockSpec(memory_space=pl.ANY)
    return list(pl.pallas_call(body, name=name, in_specs=[hbm] * n_i, out_specs=[hbm] * n_o, out_shape=job.out_shapes,
                               scratch_shapes=job.scratch)(*job.ins))


def _adam_values(w, m, v, g):
    m2 = ADAM_B1 * m + (1.0 - ADAM_B1) * g
    v2 = ADAM_B2 * v + (1.0 - ADAM_B2) * (g * g)
    delta = -ADAM_LR * ((m2 / (1.0 - ADAM_B1 ** ADAM_STEP)) / (jnp.sqrt(v2 / (1.0 - ADAM_B2 ** ADAM_STEP)) + ADAM_EPS)
                        + ADAM_WD * w)
    return delta, m2, v2


_P_WSP, _P_WGU, _P_NORM, _P_BG, _P_BSP, _P_HEAD, _P_LOSS, _P_ROWS = 0, 512, 576, 608, 616, 624, 720, 736


def _small_sum(dgrads):
    hr = _P_ROWS // 2

    def body(dwsp, dwgu, dg1, dgpm, dgpf, dgpo, dbg, dbspt, dgn, dlng, dlnb, loss_in, tot, pack, pair, slots, send_sems,
             recv_sems):
        xi, yi, ci = _mesh_pos()
        chip = 2 * xi + yi

        pack[...] = jnp.zeros_like(pack)
        for g in range(SGU_GROUPS):
            pack[_P_WSP + g * SGU_BLOCK:_P_WSP + (g + 1) * SGU_BLOCK] = dwsp[g]
        for j in range(N_CHIPS):
            pack[_P_WGU + GLA_RANK * j:_P_WGU + GLA_RANK * (j + 1)] = dwgu[0:GLA_RANK, LANES * j:LANES * (j + 1)]
        for k, r in enumerate((dg1, dgpm, dgpf, dgpo)):
            for q in range(8):
                pack[_P_NORM + 8 * k + q:_P_NORM + 8 * k + q + 1] = r[:, LANES * q:LANES * (q + 1)]
        for q in range(4):
            pack[_P_BG + q:_P_BG + q + 1] = dbg[:, LANES * q:LANES * (q + 1)]
        pack[_P_BSP:_P_BSP + SGU_GROUPS] = jnp.transpose(dbspt[...])[0:SGU_GROUPS]
        for k, r in enumerate((dgn, dlng, dlnb)):
            for j in range(N_CHIPS):
                for hh in range(4):
                    row = _P_HEAD + 32 * k + 8 * j + hh
                    pack[row:row + 1, 0:64] = r[:, 256 * hh + 64 * j:256 * hh + 64 * (j + 1)]
        pack[_P_LOSS:_P_LOSS + 1] = loss_in[...]

        sibling = dict(device_id=(xi, yi, 1 - ci), device_id_type=MESH)
        to_sibling = pltpu.make_async_remote_copy(src_ref=pack, dst_ref=pair, send_sem=send_sems.at[N_PEER],
                                                  recv_sem=recv_sems.at[N_PEER], **sibling)
        to_sibling.start()
        to_sibling.wait_recv()
        to_sibling.wait_send()
        pack[...] = pack[...] + pair[...]
        mine = pl.ds(pl.multiple_of(ci * hr, 8), hr)
        theirs = pl.ds(pl.multiple_of((1 - ci) * hr, 8), hr)
        slots[chip] = pack[mine, :]

        def copy(j, slot):
            px, py = _peer_chips(xi, yi)[j]
            return pltpu.make_async_remote_copy(
                src_ref=pack.at[mine], dst_ref=slots.at[slot(2 * px + py)], send_sem=send_sems.at[j],
                recv_sem=recv_sems.at[j], device_id=(px, py, ci), device_id_type=MESH)

        sends = [copy(j, lambda peer_chip: chip) for j in range(N_PEER)]
        for cp in sends:
            cp.start()
        for j in range(N_PEER):
            copy(j, lambda peer_chip: peer_chip).wait_recv()
        for cp in sends:
            cp.wait_send()
        acc = slots[0]
        for d in range(1, N_CHIPS):
            acc = acc + slots[d]
        tot[mine, :] = acc
        half_over = pltpu.make_async_remote_copy(src_ref=tot.at[mine], dst_ref=tot.at[mine], send_sem=send_sems.at[N_PEER + 1],
                                                 recv_sem=recv_sems.at[N_PEER + 1], **sibling)
        half_back = pltpu.make_async_remote_copy(src_ref=tot.at[mine], dst_ref=tot.at[theirs], send_sem=send_sems.at[N_PEER + 1],
                                                 recv_sem=recv_sems.at[N_PEER + 1], **sibling)
        half_over.start()
        half_back.wait_recv()
        half_over.wait_send()

    return pl.pallas_call(
        body, name="small_sum", in_specs=[_whole()] * 12, out_specs=_whole(), out_shape=_sds((_P_ROWS, LANES), F32),
        scratch_shapes=[pltpu.VMEM((_P_ROWS, LANES), F32), pltpu.VMEM((_P_ROWS, LANES), F32),
                        pltpu.VMEM((N_CHIPS, hr, LANES), F32),
                        pltpu.SemaphoreType.DMA((N_PEER + 2,)), pltpu.SemaphoreType.DMA((N_PEER + 2,))],
        compiler_params=pltpu.CompilerParams(vmem_limit_bytes=VMEM_LIMIT_BYTES),
    )(*dgrads)


def _small_adamw(tot, ws, ms, vs):
    n = len(ws)

    def body(*refs):
        tot = refs[0]
        w_refs, m_refs, v_refs = refs[1:1 + n], refs[1 + n:1 + 2 * n], refs[1 + 2 * n:1 + 3 * n]
        loss_out = refs[1 + 3 * n]
        outs = refs[2 + 3 * n:]
        chip = 2 * lax.axis_index("x") + lax.axis_index("y")
        loss_out[...] = tot[_P_LOSS:_P_LOSS + 1, 0:1]

        def step(k, g, pick, put):
            d, m2, v2 = _adam_values(pick(w_refs[k]), pick(m_refs[k]), pick(v_refs[k]), g)
            for o, val in zip((outs[k], outs[n + k], outs[2 * n + k], outs[3 * n + k]), (g, d, m2, v2)):
                put(o, val)

        def whole(ref):
            return ref[0]

        def put_whole(ref, val):
            ref[0] = val

        for g in range(SGU_GROUPS):
            def pick_g(ref, g=g):
                return ref[0, g]

            def put_g(ref, val, g=g):
                ref[0, g] = val

            step(0, tot[_P_WSP + g * SGU_BLOCK:_P_WSP + (g + 1) * SGU_BLOCK], pick_g, put_g)
        step(1, tot[pl.ds(pl.multiple_of(_P_WGU + GLA_RANK * chip, GLA_RANK), GLA_RANK), :], whole, put_whole)
        for k, (base, chunks) in enumerate(((_P_NORM, 8), (_P_NORM + 8, 8), (_P_NORM + 16, 8), (_P_NORM + 24, 8), (_P_BG, 4))):
            for q in range(chunks):
                def pick_q(ref, q=q):
                    return ref[:, LANES * q:LANES * (q + 1)]

                def put_q(ref, val, q=q):
                    ref[:, LANES * q:LANES * (q + 1)] = val

                step(2 + k, tot[base + q:base + q + 1], pick_q, put_q)
        step(7, tot[_P_BSP:_P_BSP + SGU_GROUPS], whole, put_whole)
        for k in range(3):
            mine = tot[pl.ds(pl.multiple_of(_P_HEAD + 32 * k + 8 * chip, 8), 8), :]
            step(8 + k, mine[0:4, 0:64], whole, put_whole)

    shapes = [_sds(w.shape, F32) for w in ws]
    res = pl.pallas_call(
        body, name="small_adamw", in_specs=[_whole()] * (1 + 3 * n), out_specs=[_whole()] * (1 + 4 * n),
        out_shape=[_sds((1, 1), F32)] + shapes * 4,
        compiler_params=pltpu.CompilerParams(vmem_limit_bytes=VMEM_LIMIT_BYTES),
    )(tot, *ws, *ms, *vs)
    return res[0].reshape(()), [list(res[1 + i * n:1 + (i + 1) * n]) for i in range(4)]


def _w_in_pieces():
    blk = D_IN // N_CHIPS
    pieces = []
    for s in range(len(_IN_SPLITS)):
        lo_s, hi_s = _IN_STARTS[s], _IN_STARTS[s + 1]
        for j in range(N_CHIPS):
            lo, hi = max(lo_s, j * blk), min(hi_s, (j + 1) * blk)
            if lo < hi:
                pieces.append((j, lo - j * blk, _IN_DST[s] + lo - lo_s, hi - lo))
    return pieces


def _relayout_w_in(gathered):
    _, rows, blk = gathered.shape
    tr = 256

    def body(g_ref, o_ref):
        o_ref[:, OFF_AL:N_ALL] = jnp.zeros((tr, LANES), BF16)
        for j, src, dst, w in _w_in_pieces():
            o_ref[:, dst:dst + w] = g_ref[j, :, src:src + w]

    res, _ = _call(body, name="relayout_w_in", grid=(rows // tr,), parallel=True,
                   in_specs=[pl.BlockSpec((N_CHIPS, tr, blk), lambda i: (0, i, 0))],
                   out_specs=[pl.BlockSpec((tr, N_ALL), lambda i: (i, 0))],
                   out_shape=[_sds((rows, N_ALL), BF16)], args=(gathered,))
    return res[0]


def _update_row_tile(rows):
    for t in range(min(rows, 256), 7, -8):
        if rows % t == 0:
            return t
    return rows


def _presum_w_in(dws, theirs, row0, rows, name, job=None):
    hr = theirs[0].shape[0]
    blk = D_IN // N_CHIPS
    tr = 64
    assert row0 % tr == 0 and rows % tr == 0
    nh, t0 = hr // tr, row0 // tr
    n = len(dws)

    def body(core_ref, *refs):
        dw_refs, q_refs, (o_ref, s_scr) = refs[:n], refs[n:2 * n], refs[2 * n:]
        for p, (a, off) in enumerate(dws):
            w = a.shape[1]
            s_scr[:, off:off + w] = (dw_refs[p][...] + q_refs[p][...]).astype(BF16)
        for j, src, dst, w in _w_in_pieces():
            o_ref[j, :, src:src + w] = s_scr[:, dst:dst + w]

    in_specs = [pl.BlockSpec((tr, a.shape[1]), lambda i, core: (i + t0 + core[0] * nh, 0)) for a, _ in dws]
    in_specs += [pl.BlockSpec((tr, q.shape[1]), lambda i, core: (i + t0, 0)) for q in theirs]
    res, jres = _call(body, name=name, grid=(rows // tr,), parallel=True, in_specs=in_specs,
                      out_specs=[pl.BlockSpec((N_CHIPS, tr, blk), lambda i, core: (0, i, 0))],
                      out_shape=[_sds((N_CHIPS, rows, blk), BF16)], scratch_shapes=[pltpu.VMEM((tr, N_ALL), BF16)],
                      args=(*[a for a, _ in dws], *theirs), job=job, by_core=True)
    return res[0], jres


def _presum(dws, theirs, name):
    dw, n = dws[0], len(dws)
    assert all(d.shape == dw.shape for d in dws)
    if dw.ndim == 4:
        _, _, hr, c = dw.shape
        tr = _update_row_tile(hr)
        mine = pl.BlockSpec((1, 1, tr, c), lambda j, i, core: (j, core[0], i, 0))
        other = pl.BlockSpec((1, tr, c), lambda j, i, core: (j, i, 0))
    else:
        hr, c = dw.shape[0] // 2, dw.shape[1] // N_CHIPS
        tr = _update_row_tile(hr)
        nh = hr // tr
        mine = pl.BlockSpec((tr, c), lambda j, i, core: (i + core[0] * nh, j))
        other = pl.BlockSpec((tr, c), lambda j, i, core: (i, j))

    def body(core_ref, *refs):
        for a_ref, q_ref, o_ref in zip(refs[:n], refs[n:2 * n], refs[2 * n:]):
            o_ref[...] = (a_ref[...].reshape(tr, c) + q_ref[...].reshape(tr, c)).astype(BF16).reshape(o_ref.shape)

    res, _ = _call(body, name=name, grid=(N_CHIPS, hr // tr), parallel=True, in_specs=[mine] * n + [other] * n,
                   out_specs=[pl.BlockSpec((1, tr, c), lambda j, i, core: (j, i, 0))] * n,
                   out_shape=[_sds((N_CHIPS, hr, c), BF16)] * n, args=(*dws, *theirs), by_core=True)
    return res


def _sum_slots(sums, slots, name):
    n = len(sums)
    _, rows, cols = sums[0].shape
    assert all(s.shape == sums[0].shape for s in sums)
    tr = _update_row_tile(rows)

    def body(chip_ref, *refs):
        for own_ref, s_ref, o_ref in zip(refs[:n], refs[n:2 * n], refs[2 * n:]):
            acc = own_ref[...].astype(F32)
            for j in range(N_PEER):
                acc = acc + s_ref[j].astype(F32)
            o_ref[...] = acc

    chip = (2 * lax.axis_index("x") + lax.axis_index("y")).astype(jnp.int32).reshape(1)
    spec = pltpu.PrefetchScalarGridSpec(
        num_scalar_prefetch=1, grid=(rows // tr,),
        in_specs=[pl.BlockSpec((None, tr, cols), lambda i, chip: (chip[0], i, 0))] * n
        + [pl.BlockSpec((N_PEER, tr, cols), lambda i, chip: (0, i, 0))] * n,
        out_specs=[pl.BlockSpec((tr, cols), lambda i, chip: (i, 0))] * n)
    return list(pl.pallas_call(
        body, name=name, grid_spec=spec, out_shape=[_sds((rows, cols), F32)] * n,
        compiler_params=pltpu.CompilerParams(dimension_semantics=("parallel",), vmem_limit_bytes=VMEM_LIMIT_BYTES),
    )(chip, *sums, *slots))


def _adamw(ws, ms, vs, g_mine, g_theirs, name, job=None):
    n = len(ws)
    rows, cols = ws[0].shape
    part_rows = [p.shape[0] for p in g_mine[0]]
    assert all(w.shape == (rows, cols) for w in ws) and sum(part_rows) == rows // 2
    assert all([p.shape[0] for p in parts] == part_rows for parts in (*g_mine, *g_theirs))
    tr = _update_row_tile(min(part_rows))
    assert all(r % tr == 0 for r in part_rows)
    nh = (rows // 2) // tr
    starts = [sum(part_rows[:k]) // tr for k in range(len(part_rows))]
    n_parts = len(part_rows)

    def body(core_ref, *refs):
        ins, outs = refs[:(3 + 2 * n_parts) * n], refs[(3 + 2 * n_parts) * n:]
        step = pl.program_id(0)
        mine_here = (step // nh) == core_ref[0]
        q = step % nh
        for a in range(n):
            g_refs = ins[3 * n + 2 * n_parts * a:3 * n + 2 * n_parts * (a + 1)]
            g = None
            for k in reversed(range(n_parts)):
                val = jnp.where(mine_here, g_refs[k][...], g_refs[n_parts + k][...])
                g = val if g is None else jnp.where(q < starts[k + 1], val, g)
            d, m2, v2 = _adam_values(ins[a][...], ins[n + a][...], ins[2 * n + a][...], g)
            g_out, d_out, m_out, v_out = outs[4 * a:4 * a + 4]
            g_out[...] = g
            m_out[...] = m2
            v_out[...] = v2
            d_out[...] = d

    def g_spec(k, mine):
        last = part_rows[k] // tr - 1

        def index(i, core):
            half = core[0] if mine else 1 - core[0]
            here = jnp.clip(i % nh - starts[k], 0, last)
            return (jnp.where(i // nh == half, here, jnp.where(i // nh > half, last, 0)), 0)

        return pl.BlockSpec((tr, cols), index)

    spec = pl.BlockSpec((tr, cols), lambda i, core: (i, 0))
    g_specs = [g_spec(k, True) for k in range(n_parts)] + [g_spec(k, False) for k in range(n_parts)]
    g_args = [p for a in range(n) for p in (*g_mine[a], *g_theirs[a])]
    res, jres = _call(body, name=name, grid=(rows // tr,), parallel=True, in_specs=[spec] * (3 * n) + g_specs * n,
                      out_specs=[spec] * (4 * n), out_shape=[_sds((rows, cols), F32)] * (4 * n),
                      args=(*ws, *ms, *vs, *g_args), job=job, by_core=True)
    return [tuple(res[4 * a:4 * a + 4]) for a in range(n)], jres


def _transposed_cast(wt):
    cols, rows = wt.shape
    tile = 4 * LANES

    def body(x_ref, o_ref):
        o_ref[...] = jnp.transpose(x_ref[...]).astype(BF16)

    res, _ = _call(body, name="transpose_w_in", grid=(pl.cdiv(cols, tile),), parallel=True,
                   in_specs=[pl.BlockSpec((tile, rows), lambda j: (j, 0))],
                   out_specs=[pl.BlockSpec((rows, tile), lambda j: (0, j))], out_shape=[_sds((rows, cols), BF16)],
                   args=(wt,))
    return res[0]


def _cast_weights(ws, w_fi, job=None):
    steps = 4
    cols = w_fi.shape[1]
    tile = w_fi.shape[0] // (2 * steps)

    def body(*refs):
        for i_ref, o_ref in zip(refs[:len(refs) // 2], refs[len(refs) // 2:]):
            o_ref[...] = i_ref[...].astype(BF16)

    row_specs = [pl.BlockSpec((w.shape[0] // steps, w.shape[1]), lambda i: (i, 0)) for w in ws]
    half_spec = pl.BlockSpec((tile, cols), lambda i: (i, 0))
    return _call(body, name="cast_weights", grid=(steps,), parallel=True,
                 in_specs=row_specs + [pl.BlockSpec((None, tile, cols), lambda i, k=k: (k, i, 0)) for k in range(2)],
                 out_specs=row_specs + [half_spec, half_spec],
                 out_shape=[_sds(w.shape, BF16) for w in ws] + [_sds((steps * tile, cols), BF16)] * 2,
                 args=(*ws, w_fi.reshape(2, steps * tile, cols), w_fi.reshape(2, steps * tile, cols)), job=job)


def _adamw_transposed(wt, mt, vt, g_mine, g_theirs, name):
    cols, rows = wt.shape
    n_parts = len(g_mine)

    def body(w_ref, m_ref, v_ref, *rest):
        g_refs, (g_out, d_out, m_out, v_out) = rest[:-4], rest[-4:]
        mine = jnp.concatenate([r[...] for r in g_refs[:n_parts]], axis=0)
        theirs = jnp.concatenate([r[...] for r in g_refs[n_parts:]], axis=0)
        first = lax.axis_index("c") == 0
        g = jnp.transpose(jnp.concatenate([jnp.where(first, mine, theirs), jnp.where(first, theirs, mine)], axis=0))
        d, m2, v2 = _adam_values(w_ref[...], m_ref[...], v_ref[...], g)
        g_out[...] = g
        m_out[...] = m2
        v_out[...] = v2
        d_out[...] = d

    spec = pl.BlockSpec((LANES, rows), lambda j: (j, 0))
    g_specs = [pl.BlockSpec((p.shape[0], LANES), lambda j: (0, j)) for p in g_mine] * 2
    res, _ = _call(body, name=name, grid=(pl.cdiv(cols, LANES),), parallel=True, in_specs=[spec] * 3 + g_specs,
                   out_specs=[spec] * 4, out_shape=[_sds((cols, rows), F32)] * 4, args=(wt, mt, vt, *g_mine, *g_theirs))
    return res


def _inproj_fwd(x, g1, w_all, job=None):
    T = x.shape[0]
    tT = _row_tile(T, 512)

    def body(x_ref, g_ref, w_ref, a_ref, proj_ref, alow_ref):
        xv = x_ref[...]
        a = (xv * _rms_stats(xv) * g_ref[...]).astype(BF16)
        a_ref[...] = a
        for j in range(N_MAIN // 1024):
            cols = slice(j * 1024, (j + 1) * 1024)
            proj_ref[:, cols] = _dot(a, w_ref[:, cols]).astype(BF16)
        alow_ref[...] = _dot(a, w_ref[:, N_MAIN:N_ALL])

    row = lambda w: pl.BlockSpec((tT, w), lambda i: (i, 0))
    return _call(
        body, name="inproj_fwd", grid=(T // tT,), parallel=True,
        in_specs=[row(D_MODEL), pl.BlockSpec((1, D_MODEL), lambda i: (0, 0)), _whole()],
        out_specs=[row(D_MODEL), row(N_MAIN), row(LANES)],
        out_shape=[_sds((T, D_MODEL), BF16), _sds((T, N_MAIN), BF16), _sds((T, LANES), F32)],
        args=(x, g1, w_all), job=job)


def _gla_decay_terms(al_ref, wgu_ref, bg_ref, later_ref):
    logit = _dot_bf16(al_ref[...], wgu_ref[...]) + bg_ref[...]
    la = _log_sigmoid(logit) * (1.0 / GLA_TAU)
    delta = _dot_exact_lhs(later_ref[...], la)
    return logit, la, delta


def _gla_fwd(proj, alow, wgu, b_gate, gn, job=None):
    T = proj.shape[0]
    tT = _row_tile(T, 512)
    nc = tT // CHUNK

    def body(q_ref, k_ref, v_ref, r_ref, al_ref, wgu_ref, bg_ref, gn_ref, later_ref, y_ref, st_ref, s_scr):
        @pl.when(pl.program_id(0) == 0)
        def _():
            s_scr[...] = jnp.zeros_like(s_scr)

        _, la, delta = _gla_decay_terms(al_ref, wgu_ref, bg_ref, later_ref)
        kdec = (k_ref[...].astype(F32) * jnp.exp(delta)).astype(BF16)
        heads = range(GLA_HEADS)
        kcs = [slice(h * GLA_DK, (h + 1) * GLA_DK) for h in heads]
        vcs = [slice(h * GLA_DV, (h + 1) * GLA_DV) for h in heads]
        state = [s_scr[h] for h in heads]
        for c in range(nc):
            rows = slice(c * CHUNK, (c + 1) * CHUNK)
            first = slice(c * CHUNK, c * CHUNK + 1)
            dec = jnp.exp(la[first, :] + delta[first, :])
            upd_t = [_dot(v_ref[rows, vcs[h]], kdec[rows, kcs[h]], _TN) for h in heads]
            qs = [(q_ref[rows, kcs[h]].astype(F32) * (GLA_DK ** -0.5)).astype(BF16) for h in heads]
            for h in heads:
                state[h] = state[h] * dec[:, kcs[h]] + upd_t[h]
                st_ref[c, h] = state[h]
            o = [_dot(qs[h], state[h].astype(BF16), _NT) for h in heads]
            for h in heads:
                on = o[h] * _rms_stats(o[h]) * gn_ref[:, vcs[h]]
                rr = r_ref[rows, vcs[h]].astype(F32)
                y_ref[rows, vcs[h]] = (on * (rr * _sigmoid(rr))).astype(BF16)
        for h in heads:
            s_scr[h] = state[h]

    blk = lambda w, j: pl.BlockSpec((tT, w), lambda i: (i, j))
    return _call(
        body, name="gla_fwd", grid=(T // tT,),
        in_specs=[blk(512, 0), blk(512, 1), blk(1024, 1), blk(1024, 2), blk(LANES, 0)] + [_whole()] * 4,
        out_specs=[pl.BlockSpec((tT, GLA_V), lambda i: (i, 0)),
                   pl.BlockSpec((nc, GLA_HEADS, GLA_DV, GLA_DK), lambda i: (i, 0, 0, 0))],
        out_shape=[_sds((T, GLA_V), BF16), _sds((T // CHUNK, GLA_HEADS, GLA_DV, GLA_DK), F32)],
        scratch_shapes=[pltpu.VMEM((GLA_HEADS, GLA_DV, GLA_DK), F32)],
        args=(proj, proj, proj, proj, alow, wgu, b_gate, gn, _chunk_masks(tT, upper=True)), job=job)


def _sgu_mask():
    i = lax.broadcasted_iota(jnp.int32, (SGU_BLOCK, SGU_BLOCK), 0)
    j = lax.broadcasted_iota(jnp.int32, (SGU_BLOCK, SGU_BLOCK), 1)
    return lax.shift_right_logical(j, 6) <= lax.shift_right_logical(i, 6)


def _sgu_merge_fwd(x, proj, y_gla, ln_g, ln_b, w_sp, b_sp_t, w_bg, w_bs, w_o, g_pm, job=None):
    T = x.shape[0]
    tT = _row_tile(T, 512)
    nb = tT // SGU_BLOCK

    def body(x_ref, su_ref, sv_ref, gg_ref, gs_ref, yg_ref, lg_ref, lb_ref, w_ref, b_ref, wbg_ref, wbs_ref, wo_ref,
             g_ref, ys_ref, zg_ref, zs_ref, mg_ref, mix_ref, x1_ref):
        mask = _sgu_mask()
        for g in range(SGU_GROUPS):
            gc = slice(g * SGU_DG, (g + 1) * SGU_DG)
            wm = jnp.where(mask, w_ref[g], 0.0).astype(BF16)
            vf = _gelu(sv_ref[:, gc].astype(F32))
            mu = jnp.mean(vf, axis=-1, keepdims=True)
            vc = vf - mu
            rstd = lax.rsqrt(jnp.mean(vc * vc, axis=-1, keepdims=True) + EPS)
            vn = (vc * rstd * lg_ref[:, gc] + lb_ref[:, gc]).astype(BF16)
            u = _gelu(su_ref[:, gc].astype(F32))
            for b in range(nb):
                rows = slice(b * SGU_BLOCK, (b + 1) * SGU_BLOCK)
                mixed = _dot(wm, vn[rows, :]) + b_ref[:, g:g + 1]
                ys_ref[rows, gc] = (u[rows, :] * mixed).astype(BF16)
        zg = _dot(yg_ref[...], wbg_ref[...])
        zs = _dot(ys_ref[...], wbs_ref[...])
        zg_ref[...] = zg.astype(BF16)
        zs_ref[...] = zs.astype(BF16)
        merged = (_sigmoid(gg_ref[...].astype(F32)) * zg + _sigmoid(gs_ref[...].astype(F32)) * zs).astype(BF16)
        mg_ref[...] = merged
        mix = _dot(merged, wo_ref[...])
        mix_ref[...] = mix.astype(BF16)
        x1_ref[...] = x_ref[...] + mix * _rms_stats(mix) * g_ref[...]

    row = pl.BlockSpec((tT, D_MODEL), lambda i: (i, 0))
    blk = lambda j: pl.BlockSpec((tT, 1024), lambda i: (i, j))
    sds = lambda dt: _sds((T, D_MODEL), dt)
    return _call(body, name="sgu_merge_fwd", grid=(T // tT,), parallel=True,
                 in_specs=[row, blk(3), blk(4), blk(5), blk(6), row] + [_whole()] * 7
                 + [pl.BlockSpec((1, D_MODEL), lambda i: (0, 0))],
                 out_specs=[row] * 6, out_shape=[sds(BF16)] * 5 + [sds(F32)],
                 args=(x, proj, proj, proj, proj, y_gla, ln_g, ln_b, w_sp, b_sp_t, w_bg, w_bs, w_o, g_pm), job=job)


def _ffn_fwd_bwd(x1, tgt, w_fi_top, w_fi_bot, w_fo, g_pf, g_po):
    T = x1.shape[0]
    tT = _row_tile(T, 256)
    half = D_FF // 2
    kh = D_MODEL // 2

    def body(x1_ref, t_ref, top_ref, bot_ref, wfo_ref, gpf_ref, gpo_ref,
             h_ref, f_ref, dgu_ref, dy_ref, dx1_ref, loss_ref, dgpf_ref, dgpo_ref, gu_scr):
        @pl.when(pl.program_id(0) == 0)
        def _():
            loss_ref[...] = jnp.zeros_like(loss_ref)
            dgpf_ref[...] = jnp.zeros_like(dgpf_ref)
            dgpo_ref[...] = jnp.zeros_like(dgpo_ref)

        main = (half // 256) * 256
        pieces = (0, 1, None)

        def w_in_cols(ref, first_slab, p):
            if p is not None:
                return ref[first_slab + p, :, :main]
            return jnp.concatenate([ref[first_slab, :, main:], ref[first_slab + 1, :, main:]], axis=1)

        def w_out_rows(p):
            if p is not None:
                return wfo_ref[p * half:p * half + main, :]
            return jnp.concatenate([wfo_ref[main:half, :], wfo_ref[half + main:2 * half, :]], axis=0)

        def put(ref, base, p, val):
            if p is not None:
                ref[:, base + p * half:base + p * half + main] = val
            else:
                ref[:, base + main:base + half] = val[:, :half - main]
                ref[:, base + half + main:base + 2 * half] = val[:, half - main:]

        def get(ref, base, p):
            if p is not None:
                return ref[:, base + p * half:base + p * half + main]
            return jnp.concatenate([ref[:, base + main:base + half], ref[:, base + half + main:base + 2 * half]], axis=1)

        x1v = x1_ref[...]
        r2 = _rms_stats(x1v)
        h = (x1v * r2 * gpf_ref[...]).astype(BF16)
        h_ref[...] = h
        y = jnp.zeros((tT, D_MODEL), F32)
        for p in pieces:
            gate = _dot(h[:, :kh], w_in_cols(top_ref, 0, p)) + _dot(h[:, kh:], w_in_cols(bot_ref, 0, p))
            up = _dot(h[:, :kh], w_in_cols(top_ref, 2, p)) + _dot(h[:, kh:], w_in_cols(bot_ref, 2, p))
            put(gu_scr, 0, p, gate)
            put(gu_scr, D_FF, p, up)
            f = (gate * _sigmoid(gate) * up).astype(BF16)
            put(f_ref, 0, p, f)
            y = y + _dot(f, w_out_rows(p))
        r3 = _rms_stats(y)
        x2 = x1v + y * r3 * gpo_ref[...]
        err = x2 - t_ref[...]
        loss_ref[...] += jnp.sum(err * err) * (0.5 / D_MODEL)
        dx2 = err * (1.0 / D_MODEL)
        dy, dg = _rms_bwd(dx2, y, r3, gpo_ref[...])
        dgpo_ref[...] += jnp.sum(dg, axis=0, keepdims=True)
        dyb = dy.astype(BF16)
        dy_ref[...] = dyb
        dh_top = jnp.zeros((tT, kh), F32)
        dh_bot = jnp.zeros((tT, kh), F32)
        for p in pieces:
            df = _dot(dyb, w_out_rows(p), _NT)
            gate = get(gu_scr, 0, p)
            up = get(gu_scr, D_FF, p)
            sg = _sigmoid(gate)
            dgate = (df * up * (sg * (1.0 + gate * (1.0 - sg)))).astype(BF16)
            dup = (df * (gate * sg)).astype(BF16)
            put(dgu_ref, 0, p, dgate)
            put(dgu_ref, D_FF, p, dup)
            dh_top = dh_top + _dot(dgate, w_in_cols(top_ref, 0, p), _NT) + _dot(dup, w_in_cols(top_ref, 2, p), _NT)
            dh_bot = dh_bot + _dot(dgate, w_in_cols(bot_ref, 0, p), _NT) + _dot(dup, w_in_cols(bot_ref, 2, p), _NT)
        dh = jnp.concatenate([dh_top, dh_bot], axis=1)
        dx1n, dg2 = _rms_bwd(dh, x1v, r2, gpf_ref[...])
        dgpf_ref[...] += jnp.sum(dg2, axis=0, keepdims=True)
        dx1_ref[...] = dx2 + dx1n

    row = lambda w: pl.BlockSpec((tT, w), lambda i: (i, 0))
    vec = pl.BlockSpec((1, D_MODEL), lambda i: (0, 0))
    res, _ = _call(
        body, name="ffn_fwd_bwd", grid=(T // tT,),
        in_specs=[row(D_MODEL), row(D_MODEL), _whole(), _whole(), _whole(), vec, vec],
        out_specs=[row(D_MODEL), row(D_FF), row(2 * D_FF), row(D_MODEL), row(D_MODEL),
                   pl.BlockSpec((1, LANES), lambda i: (0, 0)), vec, vec],
        out_shape=[_sds((T, D_MODEL), BF16), _sds((T, D_FF), BF16), _sds((T, 2 * D_FF), BF16), _sds((T, D_MODEL), BF16),
                   _sds((T, D_MODEL), F32), _sds((1, LANES), F32), _sds((1, D_MODEL), F32), _sds((1, D_MODEL), F32)],
        scratch_shapes=[pltpu.VMEM((tT, 2 * D_FF), F32)], args=(x1, tgt, w_fi_top, w_fi_bot, w_fo, g_pf, g_po))
    return res


def _merge_sgu_bwd(dx1, mix, proj, zg, zs, w_bg, w_bs, w_o, g_pm, ln_g, ln_b, w_sp, b_sp_t, job=None):
    T = dx1.shape[0]
    tT = _row_tile(T, 256)
    nb = tT // SGU_BLOCK

    def body(dx1_ref, mix_ref, su_ref, sv_ref, gg_ref, gs_ref, zg_ref, zs_ref, wbg_ref, wbs_ref, wo_ref, g_ref,
             lg_ref, lb_ref, w_ref, b_ref,
             dmix_ref, dzg_ref, dzs_ref, dgate_ref, dyg_ref, dp_ref, dgpm_ref, dw_ref, dbt_ref, dlg_ref, dlb_ref):
        @pl.when(pl.program_id(0) == 0)
        def _():
            for ref in (dgpm_ref, dw_ref, dbt_ref, dlg_ref, dlb_ref):
                ref[...] = jnp.zeros_like(ref)

        mix = mix_ref[...].astype(F32)
        dmix, dg = _rms_bwd(dx1_ref[...], mix, _rms_stats(mix), g_ref[...])
        dgpm_ref[...] += jnp.sum(dg, axis=0, keepdims=True)
        dmb = dmix.astype(BF16)
        dmix_ref[...] = dmb
        dmerged = _dot(dmb, wo_ref[...], _NT)
        dys = None
        for k, (gate_ref, z_ref, w_br_ref, dz_ref) in enumerate(((gg_ref, zg_ref, wbg_ref, dzg_ref),
                                                                 (gs_ref, zs_ref, wbs_ref, dzs_ref))):
            sg = _sigmoid(gate_ref[...].astype(F32))
            dz = (dmerged * sg).astype(BF16)
            dz_ref[...] = dz
            dgate_ref[:, k * 1024:(k + 1) * 1024] = (dmerged * z_ref[...].astype(F32) * (sg * (1.0 - sg))).astype(BF16)
            dy_branch = _dot(dz, w_br_ref[...], _NT)
            if k == 0:
                dyg_ref[...] = dy_branch.astype(BF16)
            else:
                dys = dy_branch

        mask = _sgu_mask()
        lane = lax.broadcasted_iota(jnp.int32, (SGU_BLOCK, LANES), 1)
        for g in range(SGU_GROUPS):
            gc = slice(g * SGU_DG, (g + 1) * SGU_DG)
            gc_v = slice(1024 + g * SGU_DG, 1024 + (g + 1) * SGU_DG)
            wm = jnp.where(mask, w_ref[g], 0.0).astype(BF16)
            vf, dvf_dsv = _gelu_and_grad(sv_ref[:, gc].astype(F32))
            mu = jnp.mean(vf, axis=-1, keepdims=True)
            vc = vf - mu
            rstd = lax.rsqrt(jnp.mean(vc * vc, axis=-1, keepdims=True) + EPS)
            vhat = vc * rstd
            vn = (vhat * lg_ref[:, gc] + lb_ref[:, gc]).astype(BF16)
            u, du_dsu = _gelu_and_grad(su_ref[:, gc].astype(F32))
            dy = dys[:, gc]
            dmixed = (dy * u).astype(BF16)
            dvn_parts = []
            dw_acc = jnp.zeros((SGU_BLOCK, SGU_BLOCK), F32)
            db_acc = jnp.zeros((SGU_BLOCK, 1), F32)
            for b in range(nb):
                rows = slice(b * SGU_BLOCK, (b + 1) * SGU_BLOCK)
                mixed = _dot(wm, vn[rows, :]) + b_ref[:, g:g + 1]
                dp_ref[rows, gc] = (dy[rows, :] * mixed * du_dsu[rows, :]).astype(BF16)
                dvn_parts.append(_dot(wm, dmixed[rows, :], _TN))
                dw_acc = dw_acc + _dot(dmixed[rows, :], vn[rows, :], _NT)
                db_acc = db_acc + jnp.sum(dmixed[rows, :].astype(F32), axis=-1, keepdims=True)
            dw_ref[g] += jnp.where(mask, dw_acc, 0.0)
            dbt_ref[...] += jnp.where(lane == g, db_acc, 0.0)
            dvn = jnp.concatenate(dvn_parts, axis=0)
            dlg_ref[:, gc] += jnp.sum(dvn * vhat, axis=0, keepdims=True)
            dlb_ref[:, gc] += jnp.sum(dvn, axis=0, keepdims=True)
            dvh = dvn * lg_ref[:, gc]
            dvf = rstd * (dvh - jnp.mean(dvh, axis=-1, keepdims=True)
                          - vhat * jnp.mean(dvh * vhat, axis=-1, keepdims=True))
            dp_ref[:, gc_v] = (dvf * dvf_dsv).astype(BF16)

    row = pl.BlockSpec((tT, D_MODEL), lambda i: (i, 0))
    blk = lambda j: pl.BlockSpec((tT, 1024), lambda i: (i, j))
    vec = pl.BlockSpec((1, D_MODEL), lambda i: (0, 0))
    wide = lambda w: pl.BlockSpec((tT, w), lambda i: (i, 0))
    sds = _sds((T, D_MODEL), BF16)
    return _call(
        body, name="merge_sgu_bwd", grid=(T // tT,),
        in_specs=[row, row, blk(3), blk(4), blk(5), blk(6), row, row] + [_whole()] * 3 + [vec] + [_whole()] * 4,
        out_specs=[row, row, row, wide(W_MRG), row, wide(W_SGU), vec,
                   pl.BlockSpec((SGU_GROUPS, SGU_BLOCK, SGU_BLOCK), lambda i: (0, 0, 0)),
                   pl.BlockSpec((SGU_BLOCK, LANES), lambda i: (0, 0)), vec, vec],
        out_shape=[sds, sds, sds, _sds((T, W_MRG), BF16), sds, _sds((T, W_SGU), BF16), _sds((1, D_MODEL), F32),
                   _sds((SGU_GROUPS, SGU_BLOCK, SGU_BLOCK), F32), _sds((SGU_BLOCK, LANES), F32),
                   _sds((1, 1024), F32), _sds((1, 1024), F32)],
        args=(dx1, mix, proj, proj, proj, proj, zg, zs, w_bg, w_bs, w_o, g_pm, ln_g, ln_b, w_sp, b_sp_t), job=job)


def _gla_bwd(proj, alow, wgu, b_gate, gn, states, dy_gla, job=None):
    T = proj.shape[0]
    tT = _row_tile(T, 512)
    nc = tT // CHUNK
    nt = T // tT

    def body(q_ref, k_ref, v_ref, r_ref, al_ref, wgu_ref, bg_ref, gn_ref, later_ref, earlier_ref, st_ref, sp_ref, dy_ref,
             dp_ref, dal_ref, dgn_ref, dbg_ref, dwgu_ref, g_scr, dd_scr, dt_scr):
        step = pl.program_id(0)

        @pl.when(step == 0)
        def _():
            g_scr[...] = jnp.zeros_like(g_scr)
            dgn_ref[...] = jnp.zeros_like(dgn_ref)
            dbg_ref[...] = jnp.zeros_like(dbg_ref)
            dwgu_ref[...] = jnp.zeros_like(dwgu_ref)

        has_prev = jnp.where(step == nt - 1, 0.0, 1.0)
        logit, la, delta = _gla_decay_terms(al_ref, wgu_ref, bg_ref, later_ref)
        e = jnp.exp(delta)
        kdec_f = k_ref[...].astype(F32) * e
        kdec = kdec_f.astype(BF16)
        heads = range(GLA_HEADS)
        kcs = [slice(h * GLA_DK, (h + 1) * GLA_DK) for h in heads]
        vcs = [slice(h * GLA_DV, (h + 1) * GLA_DV) for h in heads]
        carry = [g_scr[h] for h in heads]
        dgn_acc = [jnp.zeros((1, GLA_DV), F32) for _ in heads]
        for c in reversed(range(nc)):
            rows = slice(c * CHUNK, (c + 1) * CHUNK)
            first = slice(c * CHUNK, c * CHUNK + 1)
            dec = jnp.exp(la[first, :] + delta[first, :])
            s_b = [st_ref[c, h].astype(BF16) for h in heads]
            qs = [(q_ref[rows, kcs[h]].astype(F32) * (GLA_DK ** -0.5)).astype(BF16) for h in heads]
            o = [_dot(qs[h], s_b[h], _NT) for h in heads]
            do = []
            for h in heads:
                rstd = _rms_stats(o[h])
                ohat = o[h] * rstd
                gnh = gn_ref[:, vcs[h]]
                dy = dy_ref[rows, vcs[h]].astype(F32)
                rr = r_ref[rows, vcs[h]].astype(F32)
                sg = _sigmoid(rr)
                don = dy * (rr * sg)
                dp_ref[rows, OFF_R + h * GLA_DV:OFF_R + (h + 1) * GLA_DV] = (
                    dy * (ohat * gnh) * (sg * (1.0 + rr * (1.0 - sg)))).astype(BF16)
                dgn_acc[h] = dgn_acc[h] + jnp.sum(don * ohat, axis=0, keepdims=True)
                dn = don * gnh
                do.append((rstd * (dn - ohat * jnp.mean(dn * ohat, axis=-1, keepdims=True))).astype(BF16))
            dq = [_dot(do[h], s_b[h]) for h in heads]
            g_t = [_dot(do[h], qs[h], _TN) + carry[h] for h in heads]
            g_b = [g_t[h].astype(BF16) for h in heads]
            dv = [_dot(kdec[rows, kcs[h]], g_b[h], _NT) for h in heads]
            dkdec = [_dot(v_ref[rows, vcs[h]], g_b[h]) for h in heads]
            for h in heads:
                s_prev = st_ref[c - 1, h] if c > 0 else sp_ref[0, h] * has_prev
                ddec = jnp.sum(g_t[h] * s_prev, axis=0, keepdims=True)
                carry[h] = g_t[h] * dec[:, kcs[h]]
                dp_ref[rows, OFF_Q + h * GLA_DK:OFF_Q + (h + 1) * GLA_DK] = (dq[h] * (GLA_DK ** -0.5)).astype(BF16)
                dp_ref[rows, OFF_V + h * GLA_DV:OFF_V + (h + 1) * GLA_DV] = dv[h].astype(BF16)
                dp_ref[rows, OFF_K + h * GLA_DK:OFF_K + (h + 1) * GLA_DK] = (dkdec[h] * e[rows, kcs[h]]).astype(BF16)
                dd_scr[rows, kcs[h]] = dkdec[h] * kdec_f[rows, kcs[h]]
                dt_scr[rows, kcs[h]] = jnp.broadcast_to(ddec * dec[:, kcs[h]], (CHUNK, GLA_DK))
        for h in heads:
            g_scr[h] = carry[h]
            dgn_ref[:, vcs[h]] += dgn_acc[h]
        dla = _dot_exact_lhs(earlier_ref[...], dd_scr[...]) + dt_scr[...]
        dlogit = dla * (1.0 / GLA_TAU) * _sigmoid(-logit)
        dbg_ref[...] += jnp.sum(dlogit, axis=0, keepdims=True)
        dwgu_ref[...] += _dot_bf16(al_ref[...], dlogit, _TN)
        dal_ref[...] = _dot_bf16(dlogit, wgu_ref[...], _NT).astype(BF16)

    rev = lambda i: nt - 1 - i
    blk = lambda w, j: pl.BlockSpec((tT, w), lambda i: (rev(i), j))
    st_blk = pl.BlockSpec((nc, GLA_HEADS, GLA_DV, GLA_DK), lambda i: (rev(i), 0, 0, 0))
    sp_blk = pl.BlockSpec((1, GLA_HEADS, GLA_DV, GLA_DK), lambda i: (jnp.maximum(rev(i) * nc - 1, 0), 0, 0, 0))
    return _call(
        body, name="gla_bwd", grid=(nt,),
        in_specs=[blk(512, 0), blk(512, 1), blk(1024, 1), blk(1024, 2), blk(LANES, 0)] + [_whole()] * 5
        + [st_blk, sp_blk, blk(GLA_V, 0)],
        out_specs=[blk(W_GLA, 0), blk(LANES, 0), pl.BlockSpec((1, GLA_V), lambda i: (0, 0)),
                   pl.BlockSpec((1, GLA_QK), lambda i: (0, 0)), pl.BlockSpec((LANES, GLA_QK), lambda i: (0, 0))],
        out_shape=[_sds((T, W_GLA), BF16), _sds((T, LANES), BF16), _sds((1, GLA_V), F32), _sds((1, GLA_QK), F32),
                   _sds((LANES, GLA_QK), F32)],
        scratch_shapes=[pltpu.VMEM((GLA_HEADS, GLA_DV, GLA_DK), F32), pltpu.VMEM((tT, GLA_QK), F32),
                        pltpu.VMEM((tT, GLA_QK), F32)],
        args=(proj, proj, proj, proj, alow, wgu, b_gate, gn, _chunk_masks(tT, upper=True), _chunk_masks(tT, upper=False),
              states, states, dy_gla), job=job)


def _inproj_bwd(x, dx1, g1, w_all, dparts, job=None):
    T = x.shape[0]
    tT = _row_tile(T, 512)
    offs = (0, W_GLA, W_GLA + W_SGU, N_MAIN)

    def body(x_ref, dx1_ref, g_ref, w_hbm, *rest):
        part_refs, (dx_ref, dg_ref, w_ref, w_sems) = rest[:len(offs)], rest[len(offs):]

        def compute(first):
            if first:
                copies = [pltpu.make_async_copy(w_hbm.at[:, pl.ds(off, p.shape[1])], w_ref.at[:, pl.ds(off, p.shape[1])],
                                                w_sems.at[k]) for k, (off, p) in enumerate(zip(offs, dparts))]
                for cp in copies:
                    cp.start()
            da = jnp.zeros((tT, D_MODEL), F32)
            for k, (off, p_ref) in enumerate(zip(offs, part_refs)):
                if first:
                    copies[k].wait()
                da = da + _dot(p_ref[...], w_ref[:, off:off + p_ref.shape[1]], _NT)
            xv = x_ref[...]
            dx, dg = _rms_bwd(da, xv, _rms_stats(xv), g_ref[...])
            dg_sum = jnp.sum(dg, axis=0, keepdims=True)
            dg_ref[...] = dg_sum if first else dg_ref[...] + dg_sum
            dx_ref[...] = dx1_ref[...] + dx

        first_step = pl.program_id(0) == 0
        pl.when(first_step)(lambda: compute(True))
        pl.when(jnp.logical_not(first_step))(lambda: compute(False))

    row = lambda w: pl.BlockSpec((tT, w), lambda i: (i, 0))
    vec = pl.BlockSpec((1, D_MODEL), lambda i: (0, 0))
    return _call(
        body, name="inproj_bwd", grid=(T // tT,),
        in_specs=[row(D_MODEL), row(D_MODEL), vec, pl.BlockSpec(memory_space=pl.ANY)] + [row(p.shape[1]) for p in dparts],
        out_specs=[row(D_MODEL), vec], out_shape=[_sds((T, D_MODEL), F32), _sds((1, D_MODEL), F32)],
        scratch_shapes=[pltpu.VMEM(w_all.shape, BF16), pltpu.SemaphoreType.DMA((len(offs),))],
        args=(x, dx1, g1, w_all, *dparts), job=job)


def _tn_matmul(a, b, name, job=None):
    T, M = a.shape
    N = b.shape[1]
    tk = _row_tile(T, 1024)
    tm = M if M <= 1024 else 1408
    tn = N // 2 if N > 2048 else N
    assert M % tm == 0 and N % tn == 0

    def body(a_ref, b_ref, o_ref):
        @pl.when(pl.program_id(2) == 0)
        def _():
            o_ref[...] = _dot(a_ref[...], b_ref[...], _TN)

        @pl.when(pl.program_id(2) > 0)
        def _():
            o_ref[...] += _dot(a_ref[...], b_ref[...], _TN)

    res, jres = _call(
        body, name=name, grid=(M // tm, N // tn, T // tk),
        in_specs=[pl.BlockSpec((tk, tm), lambda i, j, k: (k, i)), pl.BlockSpec((tk, tn), lambda i, j, k: (k, j))],
        out_specs=[pl.BlockSpec((tm, tn), lambda i, j, k: (i, j))], out_shape=[_sds((M, N), F32)], args=(a, b), job=job)
    return res[0], jres


def _pad_rows(a, rows=8):
    return jnp.pad(a, ((0, rows - a.shape[0]), (0, LANES - a.shape[1])))


def _halves_view(dw):
    r = dw.shape[0] // N_CHIPS
    return dw.reshape(N_CHIPS, 2, r // 2, dw.shape[1])


def kernel(x, norm_pre_mix, w_in, w_gate_up, b_gate, gla_norm, sgu_ln_g, sgu_ln_b, w_spatial, b_spatial, w_branch_gla, w_branch_sgu, w_out, norm_post_mix, norm_pre_ffn, w_ffn_in, w_ffn_out, norm_post_ffn, loss_target, m_norm_pre_mix, m_w_in, m_w_gate_up, m_b_gate, m_gla_norm, m_sgu_ln_g, m_sgu_ln_b, m_w_spatial, m_b_spatial, m_w_branch_gla, m_w_branch_sgu, m_w_out, m_norm_post_mix, m_norm_pre_ffn, m_w_ffn_in, m_w_ffn_out, m_norm_post_ffn, v_norm_pre_mix, v_w_in, v_w_gate_up, v_b_gate, v_gla_norm, v_sgu_ln_g, v_sgu_ln_b, v_w_spatial, v_b_spatial, v_w_branch_gla, v_w_branch_sgu, v_w_out, v_norm_post_mix, v_norm_pre_ffn, v_w_ffn_in, v_w_ffn_out, v_norm_post_ffn):
    chip = 2 * lax.axis_index("x") + lax.axis_index("y")
    xt, tgt = x[0], loss_target[0]

    tiny = jnp.concatenate([w_gate_up[0], _pad_rows(gla_norm[0]), _pad_rows(sgu_ln_g[0]), _pad_rows(sgu_ln_b[0]),
                            jnp.zeros((24, LANES), F32)], axis=0)

    def with_own(gathered, own):
        return lax.dynamic_update_slice(gathered, own[None], (chip, 0, 0))

    w_in_t = w_in[0].T
    w_in_b = _transposed_cast(w_in_t)
    (*own_rows, fi_top, fi_bot), (g_in, g_tiny) = _cast_weights(
        [w_branch_gla[0], w_branch_sgu[0], w_out[0], w_ffn_out[0]], w_ffn_in[0], job=_job_gather([w_in_b, tiny]))
    g_tiny = with_own(g_tiny, tiny)
    w_all = _relayout_w_in(with_own(g_in, w_in_b))
    cols = lambda a: a.transpose(1, 0, 2).reshape(a.shape[1], N_CHIPS * a.shape[2])
    wgu = jnp.pad(cols(g_tiny[:, 0:16]), ((0, LANES - GLA_RANK), (0, 0)))
    gn = cols(g_tiny[:, 16:20, :64]).reshape(1, GLA_V)
    ln_g = cols(g_tiny[:, 24:28, :64]).reshape(1, 1024)
    ln_b = cols(g_tiny[:, 32:36, :64]).reshape(1, 1024)
    b_sp_t = jnp.pad(b_spatial[0].T, ((0, 0), (0, LANES - SGU_GROUPS)))
    w_sp = w_spatial[0]

    (a, proj, alow), g_rows = _inproj_fwd(xt, norm_pre_mix, w_all, job=_job_gather(own_rows))
    m_late, alow = lax.optimization_barrier((m_w_in, alow))
    v_late, a = lax.optimization_barrier((v_w_in, a))
    m_in_t, v_in_t = m_late[0].T, v_late[0].T
    rows = lambda g: g.reshape(N_CHIPS * g.shape[1], g.shape[2])
    w_bg, w_bs, w_o, w_fo = [rows(with_own(g, own)) for g, own in zip(g_rows, own_rows)]
    (y_gla, states), (g_top,) = _gla_fwd(proj, alow, wgu, b_gate, gn, job=_job_gather([fi_top]))
    (y_sgu, zg, zs, merged, mix, x1), (g_bot,) = _sgu_merge_fwd(
        xt, proj, y_gla, ln_g, ln_b, w_sp, b_sp_t, w_bg, w_bs, w_o, norm_post_mix, job=_job_gather([fi_bot]))
    h, f, dgu, dy, dx1, loss, d_gpf, d_gpo = _ffn_fwd_bwd(x1, tgt, with_own(g_top, fi_top), with_own(g_bot, fi_bot),
                                                          w_fo, norm_pre_ffn, norm_post_ffn)

    whole = lambda hs: [[(h_, None)] for h_ in hs]
    dw_fo, _ = _tn_matmul(f, dy, "dw_ffn_out")
    dw_fo4 = _halves_view(dw_fo)
    dw_fi, (q_fo,) = _tn_matmul(h, dgu, "dw_ffn_in", job=_job_to_other_core([[(dw_fo4, 0)]]))
    c_fo, = _presum([dw_fo4], [q_fo], "presum_ffn_out")
    (dmix, dzg, dzs, dp_mrg, dyg, dp_sgu, d_gpm, d_wsp, d_bsp_t, d_lng, d_lnb), (s_fo, q_fi) = _merge_sgu_bwd(
        dx1, mix, proj, zg, zs, w_bg, w_bs, w_o, norm_post_mix, ln_g, ln_b, w_sp, b_sp_t,
        job=_join(_job_scatter([c_fo]), _job_to_other_core([[(dw_fi, 0)]])))
    c_fi, = _presum([dw_fi], [q_fi], "presum_ffn_in")
    dw_c, _ = _tn_matmul(a, dp_mrg, "dw_in_merge")
    dw_b, _ = _tn_matmul(a, dp_sgu, "dw_in_sgu")
    dw_o4 = _halves_view(_tn_matmul(merged, dmix, "dw_out")[0])
    dw_bg4 = _halves_view(_tn_matmul(y_gla, dzg, "dw_branch_gla")[0])
    dw_bs4 = _halves_view(_tn_matmul(y_sgu, dzs, "dw_branch_sgu")[0])
    h_fo, = _sum_slots([c_fo], [s_fo], "sum_ffn_out")
    (dp_gla, dal, d_gn, d_bg, d_wgu), (s_fi, t_fo, q_b, q_c, q_o, q_bg, q_bs) = _gla_bwd(
        proj, alow, wgu, b_gate, gn, states, dyg,
        job=_join(_job_scatter([c_fi]), _job_to_other_core(
            whole([h_fo]) + [[(dw_b, 0)], [(dw_c, 0)], [(dw_o4, 0)], [(dw_bg4, 0)], [(dw_bs4, 0)]])))
    c_o, c_bg, c_bs = _presum([dw_o4, dw_bg4, dw_bs4], [q_o, q_bg, q_bs], "presum_out_branches")
    h_fi, = _sum_slots([c_fi], [s_fi], "sum_ffn_in")
    dw_d, _ = _tn_matmul(a, dal, "dw_in_gate")
    dw_a, (s_o, s_bg, s_bs, t_fi, q_d) = _tn_matmul(
        a, dp_gla, "dw_in_gla",
        job=_join(_job_scatter([c_o, c_bg, c_bs]), _job_to_other_core(whole([h_fi]) + [[(dw_d, 0)]])))
    h_o, h_bg, h_bs = _sum_slots([c_o, c_bg, c_bs], [s_o, s_bg, s_bs], "sum_out_branches")

    grads, deltas, new_m, new_v = {}, {}, {}, {}

    def update(call, names, ws, ms, vs, g_mine, g_theirs, job=None):
        res, jres = _adamw([w[0] for w in ws], [m[0] for m in ms], [v[0] for v in vs], g_mine, g_theirs, call, job=job)
        for name, (g, d, m2, v2) in zip(names, res):
            grads[name], deltas[name], new_m[name], new_v[name] = g[None], d[None], m2[None], v2[None]
        return jres

    dw_in = [(dw_a, 0), (dw_b, W_GLA), (dw_c, W_GLA + W_SGU), (dw_d, N_MAIN)]
    q_a, t_o, t_bg, t_bs = update("adamw_w_ffn_out", ["w_ffn_out"], [w_ffn_out], [m_w_ffn_out], [v_w_ffn_out],
                                  [[h_fo]], [[t_fo]],
                                  job=_job_to_other_core([[(dw_a, 0)]] + whole([h_o, h_bg, h_bs])))
    q_in = [q_a, q_b, q_c, q_d]
    hr_in = D_MODEL // 2
    c_in_a, _ = _presum_w_in(dw_in, q_in, 0, hr_in // 8, "presum_w_in_a")
    c_in_b, (s_in_a,) = _presum_w_in(dw_in, q_in, hr_in // 8, 7 * hr_in // 8, "presum_w_in_b",
                                     job=_job_scatter([c_in_a]))
    update("adamw_w_ffn_in", ["w_ffn_in"], [w_ffn_in], [m_w_ffn_in], [v_w_ffn_in], [[h_fi]], [[t_fi]])
    update("adamw_out_branches", ["w_out", "w_branch_gla", "w_branch_sgu"], [w_out, w_branch_gla, w_branch_sgu],
           [m_w_out, m_w_branch_gla, m_w_branch_sgu], [v_w_out, v_w_branch_gla, v_w_branch_sgu],
           [[h_o], [h_bg], [h_bs]], [[t_o], [t_bg], [t_bs]])
    (grad_x, d_g1), (s_in_b,) = _inproj_bwd(xt, dx1, norm_pre_mix, w_all, (dp_gla, dp_sgu, dp_mrg, dal),
                                            job=_job_scatter([c_in_b]))
    h_in = _sum_slots([c_in_a], [s_in_a], "sum_w_in_a") + _sum_slots([c_in_b], [s_in_b], "sum_w_in_b")
    t_in = _run_job(_job_to_other_core(whole(h_in)), "swap_w_in")
    for store, val in zip((grads, deltas, new_m, new_v),
                          _adamw_transposed(w_in_t, m_in_t, v_in_t, h_in, t_in, "adamw_w_in")):
        store["w_in"] = val.T[None]

    small_names = ["w_spatial", "w_gate_up", "norm_pre_mix", "norm_post_mix", "norm_pre_ffn", "norm_post_ffn", "b_gate",
                   "b_spatial", "gla_norm", "sgu_ln_g", "sgu_ln_b"]
    loss_out, small = _small_adamw(
        _small_sum([d_wsp, d_wgu, d_g1, d_gpm, d_gpf, d_gpo, d_bg, d_bsp_t, d_gn, d_lng, d_lnb, loss]),
        [w_spatial, w_gate_up, norm_pre_mix, norm_post_mix, norm_pre_ffn, norm_post_ffn, b_gate, b_spatial, gla_norm,
         sgu_ln_g, sgu_ln_b],
        [m_w_spatial, m_w_gate_up, m_norm_pre_mix, m_norm_post_mix, m_norm_pre_ffn, m_norm_post_ffn, m_b_gate,
         m_b_spatial, m_gla_norm, m_sgu_ln_g, m_sgu_ln_b],
        [v_w_spatial, v_w_gate_up, v_norm_pre_mix, v_norm_post_mix, v_norm_pre_ffn, v_norm_post_ffn, v_b_gate,
         v_b_spatial, v_gla_norm, v_sgu_ln_g, v_sgu_ln_b])
    for store, vals in zip((grads, deltas, new_m, new_v), small):
        store.update(zip(small_names, vals))

    order = ["norm_pre_mix", "w_in", "w_gate_up", "b_gate", "gla_norm", "sgu_ln_g", "sgu_ln_b", "w_spatial", "b_spatial",
             "w_branch_gla", "w_branch_sgu", "w_out", "norm_post_mix", "norm_pre_ffn", "w_ffn_in", "w_ffn_out",
             "norm_post_ffn"]
    out = [loss_out, grad_x[None]]
    for store in (grads, deltas, new_m, new_v):
        out.extend(store[n] for n in order)
    return tuple(out)
```

```python
import jax
import jax.numpy as jnp
from jax import lax
from jax.experimental import pallas as pl
from jax.experimental.pallas import tpu as pltpu

F32 = jnp.float32
BF16 = jnp.bfloat16

D_MODEL = 1024
GLA_HEADS = 4
GLA_DK = 128
GLA_DV = 256
GLA_QK = GLA_HEADS * GLA_DK
GLA_V = GLA_HEADS * GLA_DV
GLA_RANK = 16
GLA_TAU = 16.0
CHUNK = 64
SGU_GROUPS = 4
SGU_BLOCK = 128
SGU_DG = 256
D_FF = 2816
EPS = 1e-6
LANES = 128

OFF_Q, OFF_K, OFF_V, OFF_R, OFF_SU, OFF_SV, OFF_GG, OFF_GS, OFF_AL = 0, 512, 1024, 2048, 3072, 4096, 5120, 6144, 7168
W_GLA, W_SGU, W_MRG = 3072, 2048, 2048
N_MAIN = 7168
N_ALL = N_MAIN + LANES
_IN_SPLITS = (GLA_QK, GLA_QK, GLA_V, GLA_V, GLA_RANK, 1024, 1024, 1024, 1024)
_IN_STARTS = tuple(sum(_IN_SPLITS[:i]) for i in range(len(_IN_SPLITS) + 1))
_IN_DST = (OFF_Q, OFF_K, OFF_V, OFF_R, OFF_AL, OFF_SU, OFF_SV, OFF_GG, OFF_GS)
D_IN = _IN_STARTS[-1]

ADAM_LR = 0.001
ADAM_B1 = 0.9
ADAM_B2 = 0.999
ADAM_EPS = 1e-08
ADAM_WD = 0.01
ADAM_STEP = 10

VMEM_LIMIT_BYTES = 56 * 1024 * 1024
N_CHIPS = 4
N_PEER = N_CHIPS - 1
N_DEV = 8
MESH = pl.DeviceIdType.MESH

_NN = (((1,), (0,)), ((), ()))
_NT = (((1,), (1,)), ((), ()))
_TN = (((0,), (0,)), ((), ()))


def _dot(a, b, dims=_NN):
    return lax.dot_general(a, b, dims, preferred_element_type=F32)


def _split(x):
    hi = x.astype(BF16)
    lo = (x - hi.astype(F32)).astype(BF16)
    return hi, lo


def _dot_bf16(a, b, dims=_NN):
    return _dot(a.astype(BF16), b.astype(BF16), dims)


def _dot_exact_lhs(m, x):
    xh, xl = _split(x)
    return _dot(m, xh) + _dot(m, xl)


def _sigmoid(x):
    return 0.5 * jnp.tanh(0.5 * x) + 0.5


def _log_sigmoid(x):
    return jnp.minimum(x, 0.0) - jnp.log(1.0 + jnp.exp(-jnp.abs(x)))


_GELU_C = 0.7978845608028654
_GELU_A = 0.044715


def _gelu_and_grad(x):
    x2 = x * x
    t = jnp.tanh(_GELU_C * (x + _GELU_A * x * x2))
    g = 0.5 * x * (1.0 + t)
    dg = 0.5 * (1.0 + t) + 0.5 * x * (1.0 - t * t) * (_GELU_C * (1.0 + 3.0 * _GELU_A * x2))
    return g, dg


def _gelu(x):
    t = jnp.tanh(_GELU_C * (x + _GELU_A * x * x * x))
    return 0.5 * x * (1.0 + t)


def _rms_stats(x):
    return lax.rsqrt(jnp.mean(x * x, axis=-1, keepdims=True) + EPS)


def _rms_bwd(dout, y, r, g):
    yhat = y * r
    dn = dout * g
    dy = r * (dn - yhat * jnp.mean(dn * yhat, axis=-1, keepdims=True))
    return dy, dout * yhat


def _whole():
    return pl.BlockSpec(memory_space=pltpu.VMEM)


def _row_tile(T, want):
    t = min(T, want)
    assert T % t == 0
    return t


def _chunk_masks(tT, upper):
    row = lax.broadcasted_iota(jnp.int32, (tT, tT), 0)
    col = lax.broadcasted_iota(jnp.int32, (tT, tT), 1)
    same = (row // CHUNK) == (col // CHUNK)
    tri = (col > row) if upper else (col < row)
    return jnp.where(same & tri, 1.0, 0.0).astype(BF16)


class _Job:
    def __init__(self, ins, out_shapes, scratch, start, finish, mid=None):
        self.ins, self.out_shapes, self.scratch = list(ins), list(out_shapes), list(scratch)
        self.start, self.finish, self.mid = start, finish, mid


def _join(*jobs):
    def split(refs, counts):
        out, at = [], 0
        for n in counts:
            out.append(refs[at:at + n])
            at += n
        return out

    ni, no, ns = [len(j.ins) for j in jobs], [len(j.out_shapes) for j in jobs], [len(j.scratch) for j in jobs]

    def start(ins, outs, scr):
        for j, a, b, c in zip(jobs, split(ins, ni), split(outs, no), split(scr, ns)):
            j.start(a, b, c)

    def finish(ins, outs, scr):
        for j, a, b, c in zip(jobs, split(ins, ni), split(outs, no), split(scr, ns)):
            j.finish(a, b, c)

    def mid(ins, outs, scr):
        for j, a, b, c in zip(jobs, split(ins, ni), split(outs, no), split(scr, ns)):
            if j.mid is not None:
                j.mid(a, b, c)

    return _Job(sum((j.ins for j in jobs), []), sum((j.out_shapes for j in jobs), []),
                sum((j.scratch for j in jobs), []), start, finish, mid if any(j.mid for j in jobs) else None)


def _mesh_pos():
    return lax.axis_index("x"), lax.axis_index("y"), lax.axis_index("c")


def _peer_chips(xi, yi):
    return [(1 - xi, yi), (xi, 1 - yi), (1 - xi, 1 - yi)]


def _half(ci, rows):
    return pl.ds(pl.multiple_of(ci * rows, 8), rows)


def _sds(shape, dtype):
    return jax.ShapeDtypeStruct(tuple(shape), dtype)


def _job_gather(arrs):
    n = len(arrs)
    kinds = 12
    Y0, Y1, X1, X0, ON_X, ON_Y, D2D = 0, 1, 2, 3, 4, 5, 6

    def copies(ins, outs, scr):
        send_sems, recv_sems = scr
        xi, yi, ci = _mesh_pos()
        me, cx, cy, cd = 2 * xi + yi, 2 * (1 - xi) + yi, 2 * xi + (1 - yi), 2 * (1 - xi) + (1 - yi)
        to_x, to_y, to_core = (1 - xi, yi, ci), (xi, 1 - yi, ci), (xi, yi, 1 - ci)
        table = []
        for k in range(n):
            qr = arrs[k].shape[0] // 4

            def rows(core, q):
                return pl.ds(pl.multiple_of((2 * core + q) * qr, 8), qr)

            def cp(kind, src, dst, to):
                s = k * kinds + kind
                return pltpu.make_async_remote_copy(src_ref=src, dst_ref=dst, send_sem=send_sems.at[s],
                                                    recv_sem=recv_sems.at[s], device_id=to, device_id_type=MESH)

            def slab(chip, core, q):
                return outs[k].at[chip, rows(core, q)]

            t = {}
            for kind, q, to, frm in ((Y0, 0, to_y, cy), (Y1, 1, to_y, cy), (X1, 1, to_x, cx), (X0, 0, to_x, cx)):
                mine = ins[k].at[rows(ci, q)]
                t[kind] = (cp(kind, mine, slab(me, ci, q), to), cp(kind, mine, slab(frm, ci, q), to))
            t[ON_X] = (cp(ON_X, slab(cy, ci, 0), slab(cy, ci, 0), to_x), cp(ON_X, slab(cy, ci, 0), slab(cd, ci, 0), to_x))
            t[ON_Y] = (cp(ON_Y, slab(cx, ci, 1), slab(cx, ci, 1), to_y), cp(ON_Y, slab(cx, ci, 1), slab(cd, ci, 1), to_y))
            for i, (chip, q) in enumerate(((cy, 0), (cy, 1), (cx, 1), (cx, 0), (cd, 0), (cd, 1))):
                t[D2D + i] = (cp(D2D + i, slab(chip, ci, q), slab(chip, ci, q), to_core),
                              cp(D2D + i, slab(chip, ci, q), slab(chip, 1 - ci, q), to_core))
            table.append(t)
        return table

    def start(ins, outs, scr):
        table = copies(ins, outs, scr)
        for kind in (Y0, X1, Y1, X0):
            for t in table:
                t[kind][0].start()

    def arrived(table, kind, then):
        for t in table:
            t[kind][1].wait_recv()
            for nxt in then:
                t[nxt][0].start()

    def mid(ins, outs, scr):
        table = copies(ins, outs, scr)
        arrived(table, Y0, (ON_X, D2D + 0))
        arrived(table, X1, (ON_Y, D2D + 2))

    def finish(ins, outs, scr):
        table = copies(ins, outs, scr)
        arrived(table, Y1, (D2D + 1,))
        arrived(table, X0, (D2D + 3,))
        arrived(table, ON_X, (D2D + 4,))
        arrived(table, ON_Y, (D2D + 5,))
        for t in table:
            for i in range(6):
                t[D2D + i][1].wait_recv()
            for kind in range(kinds):
                t[kind][0].wait_send()

    dma = pltpu.SemaphoreType.DMA
    return _Job(arrs, [_sds((N_CHIPS,) + a.shape, a.dtype) for a in arrs], [dma((n * kinds,))] * 2, start, finish, mid)


def _job_scatter(parts):
    n = len(parts)

    def copies(ins, outs, scr):
        send_sems, recv_sems = scr
        xi, yi, ci = _mesh_pos()
        res = []
        for k in range(n):
            for j, (px, py) in enumerate(_peer_chips(xi, yi)):
                s = k * N_PEER + j
                res.append(pltpu.make_async_remote_copy(
                    src_ref=ins[k].at[2 * px + py], dst_ref=outs[k].at[j], send_sem=send_sems.at[s],
                    recv_sem=recv_sems.at[s], device_id=(px, py, ci), device_id_type=MESH))
        return res

    def start(ins, outs, scr):
        for cp in copies(ins, outs, scr):
            cp.start()

    def finish(ins, outs, scr):
        for cp in copies(ins, outs, scr):
            cp.wait_recv()
            cp.wait_send()

    dma = pltpu.SemaphoreType.DMA
    return _Job(parts, [_sds((N_PEER,) + p.shape[1:], p.dtype) for p in parts], [dma((n * N_PEER,))] * 2, start, finish)


def _job_to_other_core(groups):
    pieces = [(g, a, off) for g, group in enumerate(groups) for a, off in group]
    n = len(pieces)

    def geometry(group):
        a0, off0 = group[0]
        if off0 is None:
            return a0.shape
        if a0.ndim == 4:
            return (N_CHIPS, a0.shape[2], a0.shape[3])
        return (a0.shape[0] // 2, sum(a.shape[1] for a, _ in group))

    def copies(ins, outs, scr):
        send_sems, recv_sems = scr
        xi, yi, ci = _mesh_pos()
        res = []
        for p, (g, a, off) in enumerate(pieces):
            if off is None:
                give, land = ins[p], outs[g]
            elif a.ndim == 4:
                give, land = ins[p].at[pl.ds(0, N_CHIPS), 1 - ci], outs[g]
            else:
                hr, w = a.shape[0] // 2, a.shape[1]
                give, land = ins[p].at[_half(1 - ci, hr)], outs[g].at[pl.ds(0, hr), pl.ds(off, w)]
            res.append(pltpu.make_async_remote_copy(
                src_ref=give, dst_ref=land, send_sem=send_sems.at[p], recv_sem=recv_sems.at[p],
                device_id=(xi, yi, 1 - ci), device_id_type=MESH))
        return res

    def start(ins, outs, scr):
        for cp in copies(ins, outs, scr):
            cp.start()

    def finish(ins, outs, scr):
        for cp in copies(ins, outs, scr):
            cp.wait_recv()
            cp.wait_send()

    dma = pltpu.SemaphoreType.DMA
    return _Job([a for _, a, _ in pieces], [_sds(geometry(group), group[0][0].dtype) for group in groups],
                [dma((n,))] * 2, start, finish)


def _call(body, *, name, grid, in_specs, out_specs, out_shape, args, scratch_shapes=(), parallel=False, job=None,
          by_core=False):
    n_in, n_out, n_scr = len(in_specs), len(out_specs), len(scratch_shapes)
    hbm = pl.BlockSpec(memory_space=pl.ANY)
    n_ji, n_jo = (len(job.ins), len(job.out_shapes)) if job is not None else (0, 0)
    lead = 1 if by_core else 0

    def kernel_fn(*refs):
        core, refs = refs[:lead], refs[lead:]
        ins, refs = refs[:n_in], refs[n_in:]
        j_ins, refs = refs[:n_ji], refs[n_ji:]
        outs, refs = refs[:n_out], refs[n_out:]
        j_outs, refs = refs[:n_jo], refs[n_jo:]
        scr, j_scr = refs[:n_scr], refs[n_scr:]
        if job is None:
            body(*core, *ins, *outs, *scr)
            return
        ids = [pl.program_id(d) for d in range(len(grid))]
        first = ids[0] == 0
        last = ids[0] == grid[0] - 1
        for d in range(1, len(grid)):
            first = first & (ids[d] == 0)
            last = last & (ids[d] == grid[d] - 1)

        @pl.when(first)
        def _():
            job.start(j_ins, j_outs, j_scr)

        if job.mid is not None and grid[0] >= 4:
            half_way = ids[0] == grid[0] // 2
            for d in range(1, len(grid)):
                half_way = half_way & (ids[d] == 0)

            @pl.when(half_way)
            def _():
                job.mid(j_ins, j_outs, j_scr)

        body(*core, *ins, *outs, *scr)

        @pl.when(last)
        def _():
            if job.mid is not None and grid[0] < 4:
                job.mid(j_ins, j_outs, j_scr)
            job.finish(j_ins, j_outs, j_scr)

    sem = ("parallel" if parallel and job is None else "arbitrary",) * len(grid)
    all_in = list(in_specs) + [hbm] * n_ji
    all_out = list(out_specs) + [hbm] * n_jo
    all_scratch = list(scratch_shapes) + (job.scratch if job is not None else [])
    all_shapes = list(out_shape) + (job.out_shapes if job is not None else [])
    all_args = list(args) + (job.ins if job is not None else [])
    params = pltpu.CompilerParams(dimension_semantics=sem, vmem_limit_bytes=VMEM_LIMIT_BYTES)
    if by_core:
        spec = pltpu.PrefetchScalarGridSpec(num_scalar_prefetch=1, grid=grid, in_specs=all_in, out_specs=all_out,
                                            scratch_shapes=all_scratch)
        core = lax.axis_index("c").astype(jnp.int32).reshape(1)
        res = pl.pallas_call(kernel_fn, name=name, grid_spec=spec, out_shape=all_shapes, compiler_params=params)(
            core, *all_args)
    else:
        res = pl.pallas_call(kernel_fn, name=name, grid=grid, in_specs=all_in, out_specs=all_out, out_shape=all_shapes,
                             scratch_shapes=all_scratch, compiler_params=params)(*all_args)
    return list(res[:n_out]), list(res[n_out:])


def _run_job(job, name):
    n_i, n_o = len(job.ins), len(job.out_shapes)

    def body(*refs):
        ins, outs, scr = refs[:n_i], refs[n_i:n_i + n_o], refs[n_i + n_o:]
        job.start(ins, outs, scr)
        if job.mid is not None:
            job.mid(ins, outs, scr)
        job.finish(ins, outs, scr)

    hbm = pl.BlockSpec(memory_space=pl.ANY)
    return list(pl.pallas_call(body, name=name, in_specs=[hbm] * n_i, out_specs=[hbm] * n_o, out_shape=job.out_shapes,
                               scratch_shapes=job.scratch)(*job.ins))


def _adam_values(w, m, v, g):
    m2 = ADAM_B1 * m + (1.0 - ADAM_B1) * g
    v2 = ADAM_B2 * v + (1.0 - ADAM_B2) * (g * g)
    delta = -ADAM_LR * ((m2 / (1.0 - ADAM_B1 ** ADAM_STEP)) / (jnp.sqrt(v2 / (1.0 - ADAM_B2 ** ADAM_STEP)) + ADAM_EPS)
                        + ADAM_WD * w)
    return delta, m2, v2


_P_WSP, _P_WGU, _P_NORM, _P_BG, _P_BSP, _P_HEAD, _P_LOSS, _P_ROWS = 0, 512, 576, 608, 616, 624, 720, 736


def _small_sum(dgrads):
    hr = _P_ROWS // 2

    def body(dwsp, dwgu, dg1, dgpm, dgpf, dgpo, dbg, dbspt, dgn, dlng, dlnb, loss_in, tot, pack, pair, slots, send_sems,
             recv_sems):
        xi, yi, ci = _mesh_pos()
        chip = 2 * xi + yi

        pack[...] = jnp.zeros_like(pack)
        for g in range(SGU_GROUPS):
            pack[_P_WSP + g * SGU_BLOCK:_P_WSP + (g + 1) * SGU_BLOCK] = dwsp[g]
        for j in range(N_CHIPS):
            pack[_P_WGU + GLA_RANK * j:_P_WGU + GLA_RANK * (j + 1)] = dwgu[0:GLA_RANK, LANES * j:LANES * (j + 1)]
        for k, r in enumerate((dg1, dgpm, dgpf, dgpo)):
            for q in range(8):
                pack[_P_NORM + 8 * k + q:_P_NORM + 8 * k + q + 1] = r[:, LANES * q:LANES * (q + 1)]
        for q in range(4):
            pack[_P_BG + q:_P_BG + q + 1] = dbg[:, LANES * q:LANES * (q + 1)]
        pack[_P_BSP:_P_BSP + SGU_GROUPS] = jnp.transpose(dbspt[...])[0:SGU_GROUPS]
        for k, r in enumerate((dgn, dlng, dlnb)):
            for j in range(N_CHIPS):
                for hh in range(4):
                    row = _P_HEAD + 32 * k + 8 * j + hh
                    pack[row:row + 1, 0:64] = r[:, 256 * hh + 64 * j:256 * hh + 64 * (j + 1)]
        pack[_P_LOSS:_P_LOSS + 1] = loss_in[...]

        sibling = dict(device_id=(xi, yi, 1 - ci), device_id_type=MESH)
        to_sibling = pltpu.make_async_remote_copy(src_ref=pack, dst_ref=pair, send_sem=send_sems.at[N_PEER],
                                                  recv_sem=recv_sems.at[N_PEER], **sibling)
        to_sibling.start()
        to_sibling.wait_recv()
        to_sibling.wait_send()
        pack[...] = pack[...] + pair[...]
        mine = pl.ds(pl.multiple_of(ci * hr, 8), hr)
        theirs = pl.ds(pl.multiple_of((1 - ci) * hr, 8), hr)
        slots[chip] = pack[mine, :]

        def copy(j, slot):
            px, py = _peer_chips(xi, yi)[j]
            return pltpu.make_async_remote_copy(
                src_ref=pack.at[mine], dst_ref=slots.at[slot(2 * px + py)], send_sem=send_sems.at[j],
                recv_sem=recv_sems.at[j], device_id=(px, py, ci), device_id_type=MESH)

        sends = [copy(j, lambda peer_chip: chip) for j in range(N_PEER)]
        for cp in sends:
            cp.start()
        for j in range(N_PEER):
            copy(j, lambda peer_chip: peer_chip).wait_recv()
        for cp in sends:
            cp.wait_send()
        acc = slots[0]
        for d in range(1, N_CHIPS):
            acc = acc + slots[d]
        tot[mine, :] = acc
        half_over = pltpu.make_async_remote_copy(src_ref=tot.at[mine], dst_ref=tot.at[mine], send_sem=send_sems.at[N_PEER + 1],
                                                 recv_sem=recv_sems.at[N_PEER + 1], **sibling)
        half_back = pltpu.make_async_remote_copy(src_ref=tot.at[mine], dst_ref=tot.at[theirs], send_sem=send_sems.at[N_PEER + 1],
                                                 recv_sem=recv_sems.at[N_PEER + 1], **sibling)
        half_over.start()
        half_back.wait_recv()
        half_over.wait_send()

    return pl.pallas_call(
        body, name="small_sum", in_specs=[_whole()] * 12, out_specs=_whole(), out_shape=_sds((_P_ROWS, LANES), F32),
        scratch_shapes=[pltpu.VMEM((_P_ROWS, LANES), F32), pltpu.VMEM((_P_ROWS, LANES), F32),
                        pltpu.VMEM((N_CHIPS, hr, LANES), F32),
                        pltpu.SemaphoreType.DMA((N_PEER + 2,)), pltpu.SemaphoreType.DMA((N_PEER + 2,))],
        compiler_params=pltpu.CompilerParams(vmem_limit_bytes=VMEM_LIMIT_BYTES),
    )(*dgrads)


def _small_adamw(tot, ws, ms, vs):
    n = len(ws)

    def body(*refs):
        tot = refs[0]
        w_refs, m_refs, v_refs = refs[1:1 + n], refs[1 + n:1 + 2 * n], refs[1 + 2 * n:1 + 3 * n]
        loss_out = refs[1 + 3 * n]
        outs = refs[2 + 3 * n:]
        chip = 2 * lax.axis_index("x") + lax.axis_index("y")
        loss_out[...] = tot[_P_LOSS:_P_LOSS + 1, 0:1]

        def step(k, g, pick, put):
            d, m2, v2 = _adam_values(pick(w_refs[k]), pick(m_refs[k]), pick(v_refs[k]), g)
            for o, val in zip((outs[k], outs[n + k], outs[2 * n + k], outs[3 * n + k]), (g, d, m2, v2)):
                put(o, val)

        def whole(ref):
            return ref[0]

        def put_whole(ref, val):
            ref[0] = val

        for g in range(SGU_GROUPS):
            def pick_g(ref, g=g):
                return ref[0, g]

            def put_g(ref, val, g=g):
                ref[0, g] = val

            step(0, tot[_P_WSP + g * SGU_BLOCK:_P_WSP + (g + 1) * SGU_BLOCK], pick_g, put_g)
        step(1, tot[pl.ds(pl.multiple_of(_P_WGU + GLA_RANK * chip, GLA_RANK), GLA_RANK), :], whole, put_whole)
        for k, (base, chunks) in enumerate(((_P_NORM, 8), (_P_NORM + 8, 8), (_P_NORM + 16, 8), (_P_NORM + 24, 8), (_P_BG, 4))):
            for q in range(chunks):
                def pick_q(ref, q=q):
                    return ref[:, LANES * q:LANES * (q + 1)]

                def put_q(ref, val, q=q):
                    ref[:, LANES * q:LANES * (q + 1)] = val

                step(2 + k, tot[base + q:base + q + 1], pick_q, put_q)
        step(7, tot[_P_BSP:_P_BSP + SGU_GROUPS], whole, put_whole)
        for k in range(3):
            mine = tot[pl.ds(pl.multiple_of(_P_HEAD + 32 * k + 8 * chip, 8), 8), :]
            step(8 + k, mine[0:4, 0:64], whole, put_whole)

    shapes = [_sds(w.shape, F32) for w in ws]
    res = pl.pallas_call(
        body, name="small_adamw", in_specs=[_whole()] * (1 + 3 * n), out_specs=[_whole()] * (1 + 4 * n),
        out_shape=[_sds((1, 1), F32)] + shapes * 4,
        compiler_params=pltpu.CompilerParams(vmem_limit_bytes=VMEM_LIMIT_BYTES),
    )(tot, *ws, *ms, *vs)
    return res[0].reshape(()), [list(res[1 + i * n:1 + (i + 1) * n]) for i in range(4)]


def _w_in_pieces():
    blk = D_IN // N_CHIPS
    pieces = []
    for s in range(len(_IN_SPLITS)):
        lo_s, hi_s = _IN_STARTS[s], _IN_STARTS[s + 1]
        for j in range(N_CHIPS):
            lo, hi = max(lo_s, j * blk), min(hi_s, (j + 1) * blk)
            if lo < hi:
                pieces.append((j, lo - j * blk, _IN_DST[s] + lo - lo_s, hi - lo))
    return pieces


def _relayout_w_in(gathered):
    _, rows, blk = gathered.shape
    tr = 256

    def body(g_ref, o_ref):
        o_ref[:, OFF_AL:N_ALL] = jnp.zeros((tr, LANES), BF16)
        for j, src, dst, w in _w_in_pieces():
            o_ref[:, dst:dst + w] = g_ref[j, :, src:src + w]

    res, _ = _call(body, name="relayout_w_in", grid=(rows // tr,), parallel=True,
                   in_specs=[pl.BlockSpec((N_CHIPS, tr, blk), lambda i: (0, i, 0))],
                   out_specs=[pl.BlockSpec((tr, N_ALL), lambda i: (i, 0))],
                   out_shape=[_sds((rows, N_ALL), BF16)], args=(gathered,))
    return res[0]


def _update_row_tile(rows):
    for t in range(min(rows, 256), 7, -8):
        if rows % t == 0:
            return t
    return rows


def _presum_w_in(dws, theirs, row0, rows, name, job=None):
    hr = theirs[0].shape[0]
    blk = D_IN // N_CHIPS
    tr = 64
    assert row0 % tr == 0 and rows % tr == 0
    nh, t0 = hr // tr, row0 // tr
    n = len(dws)

    def body(core_ref, *refs):
        dw_refs, q_refs, (o_ref, s_scr) = refs[:n], refs[n:2 * n], refs[2 * n:]
        for p, (a, off) in enumerate(dws):
            w = a.shape[1]
            s_scr[:, off:off + w] = (dw_refs[p][...] + q_refs[p][...]).astype(BF16)
        for j, src, dst, w in _w_in_pieces():
            o_ref[j, :, src:src + w] = s_scr[:, dst:dst + w]

    in_specs = [pl.BlockSpec((tr, a.shape[1]), lambda i, core: (i + t0 + core[0] * nh, 0)) for a, _ in dws]
    in_specs += [pl.BlockSpec((tr, q.shape[1]), lambda i, core: (i + t0, 0)) for q in theirs]
    res, jres = _call(body, name=name, grid=(rows // tr,), parallel=True, in_specs=in_specs,
                      out_specs=[pl.BlockSpec((N_CHIPS, tr, blk), lambda i, core: (0, i, 0))],
                      out_shape=[_sds((N_CHIPS, rows, blk), BF16)], scratch_shapes=[pltpu.VMEM((tr, N_ALL), BF16)],
                      args=(*[a for a, _ in dws], *theirs), job=job, by_core=True)
    return res[0], jres


def _presum(dws, theirs, name):
    dw, n = dws[0], len(dws)
    assert all(d.shape == dw.shape for d in dws)
    if dw.ndim == 4:
        _, _, hr, c = dw.shape
        tr = _update_row_tile(hr)
        mine = pl.BlockSpec((1, 1, tr, c), lambda j, i, core: (j, core[0], i, 0))
        other = pl.BlockSpec((1, tr, c), lambda j, i, core: (j, i, 0))
    else:
        hr, c = dw.shape[0] // 2, dw.shape[1] // N_CHIPS
        tr = _update_row_tile(hr)
        nh = hr // tr
        mine = pl.BlockSpec((tr, c), lambda j, i, core: (i + core[0] * nh, j))
        other = pl.BlockSpec((tr, c), lambda j, i, core: (i, j))

    def body(core_ref, *refs):
        for a_ref, q_ref, o_ref in zip(refs[:n], refs[n:2 * n], refs[2 * n:]):
            o_ref[...] = (a_ref[...].reshape(tr, c) + q_ref[...].reshape(tr, c)).astype(BF16).reshape(o_ref.shape)

    res, _ = _call(body, name=name, grid=(N_CHIPS, hr // tr), parallel=True, in_specs=[mine] * n + [other] * n,
                   out_specs=[pl.BlockSpec((1, tr, c), lambda j, i, core: (j, i, 0))] * n,
                   out_shape=[_sds((N_CHIPS, hr, c), BF16)] * n, args=(*dws, *theirs), by_core=True)
    return res


def _sum_slots(sums, slots, name):
    n = len(sums)
    _, rows, cols = sums[0].shape
    assert all(s.shape == sums[0].shape for s in sums)
    tr = _update_row_tile(rows)

    def body(chip_ref, *refs):
        for own_ref, s_ref, o_ref in zip(refs[:n], refs[n:2 * n], refs[2 * n:]):
            acc = own_ref[...].astype(F32)
            for j in range(N_PEER):
                acc = acc + s_ref[j].astype(F32)
            o_ref[...] = acc

    chip = (2 * lax.axis_index("x") + lax.axis_index("y")).astype(jnp.int32).reshape(1)
    spec = pltpu.PrefetchScalarGridSpec(
        num_scalar_prefetch=1, grid=(rows // tr,),
        in_specs=[pl.BlockSpec((None, tr, cols), lambda i, chip: (chip[0], i, 0))] * n
        + [pl.BlockSpec((N_PEER, tr, cols), lambda i, chip: (0, i, 0))] * n,
        out_specs=[pl.BlockSpec((tr, cols), lambda i, chip: (i, 0))] * n)
    return list(pl.pallas_call(
        body, name=name, grid_spec=spec, out_shape=[_sds((rows, cols), F32)] * n,
        compiler_params=pltpu.CompilerParams(dimension_semantics=("parallel",), vmem_limit_bytes=VMEM_LIMIT_BYTES),
    )(chip, *sums, *slots))


def _adamw(ws, ms, vs, g_mine, g_theirs, name, job=None):
    n = len(ws)
    rows, cols = ws[0].shape
    part_rows = [p.shape[0] for p in g_mine[0]]
    assert all(w.shape == (rows, cols) for w in ws) and sum(part_rows) == rows // 2
    assert all([p.shape[0] for p in parts] == part_rows for parts in (*g_mine, *g_theirs))
    tr = _update_row_tile(min(part_rows))
    assert all(r % tr == 0 for r in part_rows)
    nh = (rows // 2) // tr
    starts = [sum(part_rows[:k]) // tr for k in range(len(part_rows))]
    n_parts = len(part_rows)

    def body(core_ref, *refs):
        ins, outs = refs[:(3 + 2 * n_parts) * n], refs[(3 + 2 * n_parts) * n:]
        step = pl.program_id(0)
        mine_here = (step // nh) == core_ref[0]
        q = step % nh
        for a in range(n):
            g_refs = ins[3 * n + 2 * n_parts * a:3 * n + 2 * n_parts * (a + 1)]
            g = None
            for k in reversed(range(n_parts)):
                val = jnp.where(mine_here, g_refs[k][...], g_refs[n_parts + k][...])
                g = val if g is None else jnp.where(q < starts[k + 1], val, g)
            d, m2, v2 = _adam_values(ins[a][...], ins[n + a][...], ins[2 * n + a][...], g)
            g_out, d_out, m_out, v_out = outs[4 * a:4 * a + 4]
            g_out[...] = g
            m_out[...] = m2
            v_out[...] = v2
            d_out[...] = d

    def g_spec(k, mine):
        last = part_rows[k] // tr - 1

        def index(i, core):
            half = core[0] if mine else 1 - core[0]
            here = jnp.clip(i % nh - starts[k], 0, last)
            return (jnp.where(i // nh == half, here, jnp.where(i // nh > half, last, 0)), 0)

        return pl.BlockSpec((tr, cols), index)

    spec = pl.BlockSpec((tr, cols), lambda i, core: (i, 0))
    g_specs = [g_spec(k, True) for k in range(n_parts)] + [g_spec(k, False) for k in range(n_parts)]
    g_args = [p for a in range(n) for p in (*g_mine[a], *g_theirs[a])]
    res, jres = _call(body, name=name, grid=(rows // tr,), parallel=True, in_specs=[spec] * (3 * n) + g_specs * n,
                      out_specs=[spec] * (4 * n), out_shape=[_sds((rows, cols), F32)] * (4 * n),
                      args=(*ws, *ms, *vs, *g_args), job=job, by_core=True)
    return [tuple(res[4 * a:4 * a + 4]) for a in range(n)], jres


def _transposed_cast(wt):
    cols, rows = wt.shape
    tile = 4 * LANES

    def body(x_ref, o_ref):
        o_ref[...] = jnp.transpose(x_ref[...]).astype(BF16)

    res, _ = _call(body, name="transpose_w_in", grid=(pl.cdiv(cols, tile),), parallel=True,
                   in_specs=[pl.BlockSpec((tile, rows), lambda j: (j, 0))],
                   out_specs=[pl.BlockSpec((rows, tile), lambda j: (0, j))], out_shape=[_sds((rows, cols), BF16)],
                   args=(wt,))
    return res[0]


def _cast_weights(ws, w_fi, job=None):
    steps = 4
    cols = w_fi.shape[1]
    tile = w_fi.shape[0] // (2 * steps)

    def body(*refs):
        for i_ref, o_ref in zip(refs[:len(refs) // 2], refs[len(refs) // 2:]):
            o_ref[...] = i_ref[...].astype(BF16)

    row_specs = [pl.BlockSpec((w.shape[0] // steps, w.shape[1]), lambda i: (i, 0)) for w in ws]
    half_spec = pl.BlockSpec((tile, cols), lambda i: (i, 0))
    return _call(body, name="cast_weights", grid=(steps,), parallel=True,
                 in_specs=row_specs + [pl.BlockSpec((None, tile, cols), lambda i, k=k: (k, i, 0)) for k in range(2)],
                 out_specs=row_specs + [half_spec, half_spec],
                 out_shape=[_sds(w.shape, BF16) for w in ws] + [_sds((steps * tile, cols), BF16)] * 2,
                 args=(*ws, w_fi.reshape(2, steps * tile, cols), w_fi.reshape(2, steps * tile, cols)), job=job)


def _adamw_transposed(wt, mt, vt, g_mine, g_theirs, name):
    cols, rows = wt.shape
    n_parts = len(g_mine)

    def body(w_ref, m_ref, v_ref, *rest):
        g_refs, (g_out, d_out, m_out, v_out) = rest[:-4], rest[-4:]
        mine = jnp.concatenate([r[...] for r in g_refs[:n_parts]], axis=0)
        theirs = jnp.concatenate([r[...] for r in g_refs[n_parts:]], axis=0)
        first = lax.axis_index("c") == 0
        g = jnp.transpose(jnp.concatenate([jnp.where(first, mine, theirs), jnp.where(first, theirs, mine)], axis=0))
        d, m2, v2 = _adam_values(w_ref[...], m_ref[...], v_ref[...], g)
        g_out[...] = g
        m_out[...] = m2
        v_out[...] = v2
        d_out[...] = d

    spec = pl.BlockSpec((LANES, rows), lambda j: (j, 0))
    g_specs = [pl.BlockSpec((p.shape[0], LANES), lambda j: (0, j)) for p in g_mine] * 2
    res, _ = _call(body, name=name, grid=(pl.cdiv(cols, LANES),), parallel=True, in_specs=[spec] * 3 + g_specs,
                   out_specs=[spec] * 4, out_shape=[_sds((cols, rows), F32)] * 4, args=(wt, mt, vt, *g_mine, *g_theirs))
    return res


def _inproj_fwd(x, g1, w_all, job=None):
    T = x.shape[0]
    tT = _row_tile(T, 512)

    def body(x_ref, g_ref, w_ref, a_ref, proj_ref, alow_ref):
        xv = x_ref[...]
        a = (xv * _rms_stats(xv) * g_ref[...]).astype(BF16)
        a_ref[...] = a
        for j in range(N_MAIN // 1024):
            cols = slice(j * 1024, (j + 1) * 1024)
            proj_ref[:, cols] = _dot(a, w_ref[:, cols]).astype(BF16)
        alow_ref[...] = _dot(a, w_ref[:, N_MAIN:N_ALL])

    row = lambda w: pl.BlockSpec((tT, w), lambda i: (i, 0))
    return _call(
        body, name="inproj_fwd", grid=(T // tT,), parallel=True,
        in_specs=[row(D_MODEL), pl.BlockSpec((1, D_MODEL), lambda i: (0, 0)), _whole()],
        out_specs=[row(D_MODEL), row(N_MAIN), row(LANES)],
        out_shape=[_sds((T, D_MODEL), BF16), _sds((T, N_MAIN), BF16), _sds((T, LANES), F32)],
        args=(x, g1, w_all), job=job)


def _gla_decay_terms(al_ref, wgu_ref, bg_ref, later_ref):
    logit = _dot_bf16(al_ref[...], wgu_ref[...]) + bg_ref[...]
    la = _log_sigmoid(logit) * (1.0 / GLA_TAU)
    delta = _dot_exact_lhs(later_ref[...], la)
    return logit, la, delta


def _gla_fwd(proj, alow, wgu, b_gate, gn, job=None):
    T = proj.shape[0]
    tT = _row_tile(T, 512)
    nc = tT // CHUNK

    def body(q_ref, k_ref, v_ref, r_ref, al_ref, wgu_ref, bg_ref, gn_ref, later_ref, y_ref, st_ref, s_scr):
        @pl.when(pl.program_id(0) == 0)
        def _():
            s_scr[...] = jnp.zeros_like(s_scr)

        _, la, delta = _gla_decay_terms(al_ref, wgu_ref, bg_ref, later_ref)
        kdec = (k_ref[...].astype(F32) * jnp.exp(delta)).astype(BF16)
        heads = range(GLA_HEADS)
        kcs = [slice(h * GLA_DK, (h + 1) * GLA_DK) for h in heads]
        vcs = [slice(h * GLA_DV, (h + 1) * GLA_DV) for h in heads]
        state = [s_scr[h] for h in heads]
        for c in range(nc):
            rows = slice(c * CHUNK, (c + 1) * CHUNK)
            first = slice(c * CHUNK, c * CHUNK + 1)
            dec = jnp.exp(la[first, :] + delta[first, :])
            upd_t = [_dot(v_ref[rows, vcs[h]], kdec[rows, kcs[h]], _TN) for h in heads]
            qs = [(q_ref[rows, kcs[h]].astype(F32) * (GLA_DK ** -0.5)).astype(BF16) for h in heads]
            for h in heads:
                state[h] = state[h] * dec[:, kcs[h]] + upd_t[h]
                st_ref[c, h] = state[h]
            o = [_dot(qs[h], state[h].astype(BF16), _NT) for h in heads]
            for h in heads:
                on = o[h] * _rms_stats(o[h]) * gn_ref[:, vcs[h]]
                rr = r_ref[rows, vcs[h]].astype(F32)
                y_ref[rows, vcs[h]] = (on * (rr * _sigmoid(rr))).astype(BF16)
        for h in heads:
            s_scr[h] = state[h]

    blk = lambda w, j: pl.BlockSpec((tT, w), lambda i: (i, j))
    return _call(
        body, name="gla_fwd", grid=(T // tT,),
        in_specs=[blk(512, 0), blk(512, 1), blk(1024, 1), blk(1024, 2), blk(LANES, 0)] + [_whole()] * 4,
        out_specs=[pl.BlockSpec((tT, GLA_V), lambda i: (i, 0)),
                   pl.BlockSpec((nc, GLA_HEADS, GLA_DV, GLA_DK), lambda i: (i, 0, 0, 0))],
        out_shape=[_sds((T, GLA_V), BF16), _sds((T // CHUNK, GLA_HEADS, GLA_DV, GLA_DK), F32)],
        scratch_shapes=[pltpu.VMEM((GLA_HEADS, GLA_DV, GLA_DK), F32)],
        args=(proj, proj, proj, proj, alow, wgu, b_gate, gn, _chunk_masks(tT, upper=True)), job=job)


def _sgu_mask():
    i = lax.broadcasted_iota(jnp.int32, (SGU_BLOCK, SGU_BLOCK), 0)
    j = lax.broadcasted_iota(jnp.int32, (SGU_BLOCK, SGU_BLOCK), 1)
    return lax.shift_right_logical(j, 6) <= lax.shift_right_logical(i, 6)


def _sgu_merge_fwd(x, proj, y_gla, ln_g, ln_b, w_sp, b_sp_t, w_bg, w_bs, w_o, g_pm, job=None):
    T = x.shape[0]
    tT = _row_tile(T, 512)
    nb = tT // SGU_BLOCK

    def body(x_ref, su_ref, sv_ref, gg_ref, gs_ref, yg_ref, lg_ref, lb_ref, w_ref, b_ref, wbg_ref, wbs_ref, wo_ref,
             g_ref, ys_ref, zg_ref, zs_ref, mg_ref, mix_ref, x1_ref):
        mask = _sgu_mask()
        for g in range(SGU_GROUPS):
            gc = slice(g * SGU_DG, (g + 1) * SGU_DG)
            wm = jnp.where(mask, w_ref[g], 0.0).astype(BF16)
            vf = _gelu(sv_ref[:, gc].astype(F32))
            mu = jnp.mean(vf, axis=-1, keepdims=True)
            vc = vf - mu
            rstd = lax.rsqrt(jnp.mean(vc * vc, axis=-1, keepdims=True) + EPS)
            vn = (vc * rstd * lg_ref[:, gc] + lb_ref[:, gc]).astype(BF16)
            u = _gelu(su_ref[:, gc].astype(F32))
            for b in range(nb):
                rows = slice(b * SGU_BLOCK, (b + 1) * SGU_BLOCK)
                mixed = _dot(wm, vn[rows, :]) + b_ref[:, g:g + 1]
                ys_ref[rows, gc] = (u[rows, :] * mixed).astype(BF16)
        zg = _dot(yg_ref[...], wbg_ref[...])
        zs = _dot(ys_ref[...], wbs_ref[...])
        zg_ref[...] = zg.astype(BF16)
        zs_ref[...] = zs.astype(BF16)
        merged = (_sigmoid(gg_ref[...].astype(F32)) * zg + _sigmoid(gs_ref[...].astype(F32)) * zs).astype(BF16)
        mg_ref[...] = merged
        mix = _dot(merged, wo_ref[...])
        mix_ref[...] = mix.astype(BF16)
        x1_ref[...] = x_ref[...] + mix * _rms_stats(mix) * g_ref[...]

    row = pl.BlockSpec((tT, D_MODEL), lambda i: (i, 0))
    blk = lambda j: pl.BlockSpec((tT, 1024), lambda i: (i, j))
    sds = lambda dt: _sds((T, D_MODEL), dt)
    return _call(body, name="sgu_merge_fwd", grid=(T // tT,), parallel=True,
                 in_specs=[row, blk(3), blk(4), blk(5), blk(6), row] + [_whole()] * 7
                 + [pl.BlockSpec((1, D_MODEL), lambda i: (0, 0))],
                 out_specs=[row] * 6, out_shape=[sds(BF16)] * 5 + [sds(F32)],
                 args=(x, proj, proj, proj, proj, y_gla, ln_g, ln_b, w_sp, b_sp_t, w_bg, w_bs, w_o, g_pm), job=job)


def _ffn_fwd_bwd(x1, tgt, w_fi_top, w_fi_bot, w_fo, g_pf, g_po):
    T = x1.shape[0]
    tT = _row_tile(T, 256)
    half = D_FF // 2
    kh = D_MODEL // 2

    def body(x1_ref, t_ref, top_ref, bot_ref, wfo_ref, gpf_ref, gpo_ref,
             h_ref, f_ref, dgu_ref, dy_ref, dx1_ref, loss_ref, dgpf_ref, dgpo_ref, gu_scr):
        @pl.when(pl.program_id(0) == 0)
        def _():
            loss_ref[...] = jnp.zeros_like(loss_ref)
            dgpf_ref[...] = jnp.zeros_like(dgpf_ref)
            dgpo_ref[...] = jnp.zeros_like(dgpo_ref)

        main = (half // 256) * 256
        pieces = (0, 1, None)

        def w_in_cols(ref, first_slab, p):
            if p is not None:
                return ref[first_slab + p, :, :main]
            return jnp.concatenate([ref[first_slab, :, main:], ref[first_slab + 1, :, main:]], axis=1)

        def w_out_rows(p):
            if p is not None:
                return wfo_ref[p * half:p * half + main, :]
            return jnp.concatenate([wfo_ref[main:half, :], wfo_ref[half + main:2 * half, :]], axis=0)

        def put(ref, base, p, val):
            if p is not None:
                ref[:, base + p * half:base + p * half + main] = val
            else:
                ref[:, base + main:base + half] = val[:, :half - main]
                ref[:, base + half + main:base + 2 * half] = val[:, half - main:]

        def get(ref, base, p):
            if p is not None:
                return ref[:, base + p * half:base + p * half + main]
            return jnp.concatenate([ref[:, base + main:base + half], ref[:, base + half + main:base + 2 * half]], axis=1)

        x1v = x1_ref[...]
        r2 = _rms_stats(x1v)
        h = (x1v * r2 * gpf_ref[...]).astype(BF16)
        h_ref[...] = h
        y = jnp.zeros((tT, D_MODEL), F32)
        for p in pieces:
            gate = _dot(h[:, :kh], w_in_cols(top_ref, 0, p)) + _dot(h[:, kh:], w_in_cols(bot_ref, 0, p))
            up = _dot(h[:, :kh], w_in_cols(top_ref, 2, p)) + _dot(h[:, kh:], w_in_cols(bot_ref, 2, p))
            put(gu_scr, 0, p, gate)
            put(gu_scr, D_FF, p, up)
            f = (gate * _sigmoid(gate) * up).astype(BF16)
            put(f_ref, 0, p, f)
            y = y + _dot(f, w_out_rows(p))
        r3 = _rms_stats(y)
        x2 = x1v + y * r3 * gpo_ref[...]
        err = x2 - t_ref[...]
        loss_ref[...] += jnp.sum(err * err) * (0.5 / D_MODEL)
        dx2 = err * (1.0 / D_MODEL)
        dy, dg = _rms_bwd(dx2, y, r3, gpo_ref[...])
        dgpo_ref[...] += jnp.sum(dg, axis=0, keepdims=True)
        dyb = dy.astype(BF16)
        dy_ref[...] = dyb
        dh_top = jnp.zeros((tT, kh), F32)
        dh_bot = jnp.zeros((tT, kh), F32)
        for p in pieces:
            df = _dot(dyb, w_out_rows(p), _NT)
            gate = get(gu_scr, 0, p)
            up = get(gu_scr, D_FF, p)
            sg = _sigmoid(gate)
            dgate = (df * up * (sg * (1.0 + gate * (1.0 - sg)))).astype(BF16)
            dup = (df * (gate * sg)).astype(BF16)
            put(dgu_ref, 0, p, dgate)
            put(dgu_ref, D_FF, p, dup)
            dh_top = dh_top + _dot(dgate, w_in_cols(top_ref, 0, p), _NT) + _dot(dup, w_in_cols(top_ref, 2, p), _NT)
            dh_bot = dh_bot + _dot(dgate, w_in_cols(bot_ref, 0, p), _NT) + _dot(dup, w_in_cols(bot_ref, 2, p), _NT)
        dh = jnp.concatenate([dh_top, dh_bot], axis=1)
        dx1n, dg2 = _rms_bwd(dh, x1v, r2, gpf_ref[...])
        dgpf_ref[...] += jnp.sum(dg2, axis=0, keepdims=True)
        dx1_ref[...] = dx2 + dx1n

    row = lambda w: pl.BlockSpec((tT, w), lambda i: (i, 0))
    vec = pl.BlockSpec((1, D_MODEL), lambda i: (0, 0))
    res, _ = _call(
        body, name="ffn_fwd_bwd", grid=(T // tT,),
        in_specs=[row(D_MODEL), row(D_MODEL), _whole(), _whole(), _whole(), vec, vec],
        out_specs=[row(D_MODEL), row(D_FF), row(2 * D_FF), row(D_MODEL), row(D_MODEL),
                   pl.BlockSpec((1, LANES), lambda i: (0, 0)), vec, vec],
        out_shape=[_sds((T, D_MODEL), BF16), _sds((T, D_FF), BF16), _sds((T, 2 * D_FF), BF16), _sds((T, D_MODEL), BF16),
                   _sds((T, D_MODEL), F32), _sds((1, LANES), F32), _sds((1, D_MODEL), F32), _sds((1, D_MODEL), F32)],
        scratch_shapes=[pltpu.VMEM((tT, 2 * D_FF), F32)], args=(x1, tgt, w_fi_top, w_fi_bot, w_fo, g_pf, g_po))
    return res


def _merge_sgu_bwd(dx1, mix, proj, zg, zs, w_bg, w_bs, w_o, g_pm, ln_g, ln_b, w_sp, b_sp_t, job=None):
    T = dx1.shape[0]
    tT = _row_tile(T, 256)
    nb = tT // SGU_BLOCK

    def body(dx1_ref, mix_ref, su_ref, sv_ref, gg_ref, gs_ref, zg_ref, zs_ref, wbg_ref, wbs_ref, wo_ref, g_ref,
             lg_ref, lb_ref, w_ref, b_ref,
             dmix_ref, dzg_ref, dzs_ref, dgate_ref, dyg_ref, dp_ref, dgpm_ref, dw_ref, dbt_ref, dlg_ref, dlb_ref):
        @pl.when(pl.program_id(0) == 0)
        def _():
            for ref in (dgpm_ref, dw_ref, dbt_ref, dlg_ref, dlb_ref):
                ref[...] = jnp.zeros_like(ref)

        mix = mix_ref[...].astype(F32)
        dmix, dg = _rms_bwd(dx1_ref[...], mix, _rms_stats(mix), g_ref[...])
        dgpm_ref[...] += jnp.sum(dg, axis=0, keepdims=True)
        dmb = dmix.astype(BF16)
        dmix_ref[...] = dmb
        dmerged = _dot(dmb, wo_ref[...], _NT)
        dys = None
        for k, (gate_ref, z_ref, w_br_ref, dz_ref) in enumerate(((gg_ref, zg_ref, wbg_ref, dzg_ref),
                                                                 (gs_ref, zs_ref, wbs_ref, dzs_ref))):
            sg = _sigmoid(gate_ref[...].astype(F32))
            dz = (dmerged * sg).astype(BF16)
            dz_ref[...] = dz
            dgate_ref[:, k * 1024:(k + 1) * 1024] = (dmerged * z_ref[...].astype(F32) * (sg * (1.0 - sg))).astype(BF16)
            dy_branch = _dot(dz, w_br_ref[...], _NT)
            if k == 0:
                dyg_ref[...] = dy_branch.astype(BF16)
            else:
                dys = dy_branch

        mask = _sgu_mask()
        lane = lax.broadcasted_iota(jnp.int32, (SGU_BLOCK, LANES), 1)
        for g in range(SGU_GROUPS):
            gc = slice(g * SGU_DG, (g + 1) * SGU_DG)
            gc_v = slice(1024 + g * SGU_DG, 1024 + (g + 1) * SGU_DG)
            wm = jnp.where(mask, w_ref[g], 0.0).astype(BF16)
            vf, dvf_dsv = _gelu_and_grad(sv_ref[:, gc].astype(F32))
            mu = jnp.mean(vf, axis=-1, keepdims=True)
            vc = vf - mu
            rstd = lax.rsqrt(jnp.mean(vc * vc, axis=-1, keepdims=True) + EPS)
            vhat = vc * rstd
            vn = (vhat * lg_ref[:, gc] + lb_ref[:, gc]).astype(BF16)
            u, du_dsu = _gelu_and_grad(su_ref[:, gc].astype(F32))
            dy = dys[:, gc]
            dmixed = (dy * u).astype(BF16)
            dvn_parts = []
            dw_acc = jnp.zeros((SGU_BLOCK, SGU_BLOCK), F32)
            db_acc = jnp.zeros((SGU_BLOCK, 1), F32)
            for b in range(nb):
                rows = slice(b * SGU_BLOCK, (b + 1) * SGU_BLOCK)
                mixed = _dot(wm, vn[rows, :]) + b_ref[:, g:g + 1]
                dp_ref[rows, gc] = (dy[rows, :] * mixed * du_dsu[rows, :]).astype(BF16)
                dvn_parts.append(_dot(wm, dmixed[rows, :], _TN))
                dw_acc = dw_acc + _dot(dmixed[rows, :], vn[rows, :], _NT)
                db_acc = db_acc + jnp.sum(dmixed[rows, :].astype(F32), axis=-1, keepdims=True)
            dw_ref[g] += jnp.where(mask, dw_acc, 0.0)
            dbt_ref[...] += jnp.where(lane == g, db_acc, 0.0)
            dvn = jnp.concatenate(dvn_parts, axis=0)
            dlg_ref[:, gc] += jnp.sum(dvn * vhat, axis=0, keepdims=True)
            dlb_ref[:, gc] += jnp.sum(dvn, axis=0, keepdims=True)
            dvh = dvn * lg_ref[:, gc]
            dvf = rstd * (dvh - jnp.mean(dvh, axis=-1, keepdims=True)
                          - vhat * jnp.mean(dvh * vhat, axis=-1, keepdims=True))
            dp_ref[:, gc_v] = (dvf * dvf_dsv).astype(BF16)

    row = pl.BlockSpec((tT, D_MODEL), lambda i: (i, 0))
    blk = lambda j: pl.BlockSpec((tT, 1024), lambda i: (i, j))
    vec = pl.BlockSpec((1, D_MODEL), lambda i: (0, 0))
    wide = lambda w: pl.BlockSpec((tT, w), lambda i: (i, 0))
    sds = _sds((T, D_MODEL), BF16)
    return _call(
        body, name="merge_sgu_bwd", grid=(T // tT,),
        in_specs=[row, row, blk(3), blk(4), blk(5), blk(6), row, row] + [_whole()] * 3 + [vec] + [_whole()] * 4,
        out_specs=[row, row, row, wide(W_MRG), row, wide(W_SGU), vec,
                   pl.BlockSpec((SGU_GROUPS, SGU_BLOCK, SGU_BLOCK), lambda i: (0, 0, 0)),
                   pl.BlockSpec((SGU_BLOCK, LANES), lambda i: (0, 0)), vec, vec],
        out_shape=[sds, sds, sds, _sds((T, W_MRG), BF16), sds, _sds((T, W_SGU), BF16), _sds((1, D_MODEL), F32),
                   _sds((SGU_GROUPS, SGU_BLOCK, SGU_BLOCK), F32), _sds((SGU_BLOCK, LANES), F32),
                   _sds((1, 1024), F32), _sds((1, 1024), F32)],
        args=(dx1, mix, proj, proj, proj, proj, zg, zs, w_bg, w_bs, w_o, g_pm, ln_g, ln_b, w_sp, b_sp_t), job=job)


def _gla_bwd(proj, alow, wgu, b_gate, gn, states, dy_gla, job=None):
    T = proj.shape[0]
    tT = _row_tile(T, 512)
    nc = tT // CHUNK
    nt = T // tT

    def body(q_ref, k_ref, v_ref, r_ref, al_ref, wgu_ref, bg_ref, gn_ref, later_ref, earlier_ref, st_ref, sp_ref, dy_ref,
             dp_ref, dal_ref, dgn_ref, dbg_ref, dwgu_ref, g_scr, dd_scr, dt_scr):
        step = pl.program_id(0)

        @pl.when(step == 0)
        def _():
            g_scr[...] = jnp.zeros_like(g_scr)
            dgn_ref[...] = jnp.zeros_like(dgn_ref)
            dbg_ref[...] = jnp.zeros_like(dbg_ref)
            dwgu_ref[...] = jnp.zeros_like(dwgu_ref)

        has_prev = jnp.where(step == nt - 1, 0.0, 1.0)
        logit, la, delta = _gla_decay_terms(al_ref, wgu_ref, bg_ref, later_ref)
        e = jnp.exp(delta)
        kdec_f = k_ref[...].astype(F32) * e
        kdec = kdec_f.astype(BF16)
        heads = range(GLA_HEADS)
        kcs = [slice(h * GLA_DK, (h + 1) * GLA_DK) for h in heads]
        vcs = [slice(h * GLA_DV, (h + 1) * GLA_DV) for h in heads]
        carry = [g_scr[h] for h in heads]
        dgn_acc = [jnp.zeros((1, GLA_DV), F32) for _ in heads]
        for c in reversed(range(nc)):
            rows = slice(c * CHUNK, (c + 1) * CHUNK)
            first = slice(c * CHUNK, c * CHUNK + 1)
            dec = jnp.exp(la[first, :] + delta[first, :])
            s_b = [st_ref[c, h].astype(BF16) for h in heads]
            qs = [(q_ref[rows, kcs[h]].astype(F32) * (GLA_DK ** -0.5)).astype(BF16) for h in heads]
            o = [_dot(qs[h], s_b[h], _NT) for h in heads]
            do = []
            for h in heads:
                rstd = _rms_stats(o[h])
                ohat = o[h] * rstd
                gnh = gn_ref[:, vcs[h]]
                dy = dy_ref[rows, vcs[h]].astype(F32)
                rr = r_ref[rows, vcs[h]].astype(F32)
                sg = _sigmoid(rr)
                don = dy * (rr * sg)
                dp_ref[rows, OFF_R + h * GLA_DV:OFF_R + (h + 1) * GLA_DV] = (
                    dy * (ohat * gnh) * (sg * (1.0 + rr * (1.0 - sg)))).astype(BF16)
                dgn_acc[h] = dgn_acc[h] + jnp.sum(don * ohat, axis=0, keepdims=True)
                dn = don * gnh
                do.append((rstd * (dn - ohat * jnp.mean(dn * ohat, axis=-1, keepdims=True))).astype(BF16))
            dq = [_dot(do[h], s_b[h]) for h in heads]
            g_t = [_dot(do[h], qs[h], _TN) + carry[h] for h in heads]
            g_b = [g_t[h].astype(BF16) for h in heads]
            dv = [_dot(kdec[rows, kcs[h]], g_b[h], _NT) for h in heads]
            dkdec = [_dot(v_ref[rows, vcs[h]], g_b[h]) for h in heads]
            for h in heads:
                s_prev = st_ref[c - 1, h] if c > 0 else sp_ref[0, h] * has_prev
                ddec = jnp.sum(g_t[h] * s_prev, axis=0, keepdims=True)
                carry[h] = g_t[h] * dec[:, kcs[h]]
                dp_ref[rows, OFF_Q + h * GLA_DK:OFF_Q + (h + 1) * GLA_DK] = (dq[h] * (GLA_DK ** -0.5)).astype(BF16)
                dp_ref[rows, OFF_V + h * GLA_DV:OFF_V + (h + 1) * GLA_DV] = dv[h].astype(BF16)
                dp_ref[rows, OFF_K + h * GLA_DK:OFF_K + (h + 1) * GLA_DK] = (dkdec[h] * e[rows, kcs[h]]).astype(BF16)
                dd_scr[rows, kcs[h]] = dkdec[h] * kdec_f[rows, kcs[h]]
                dt_scr[rows, kcs[h]] = jnp.broadcast_to(ddec * dec[:, kcs[h]], (CHUNK, GLA_DK))
        for h in heads:
            g_scr[h] = carry[h]
            dgn_ref[:, vcs[h]] += dgn_acc[h]
        dla = _dot_exact_lhs(earlier_ref[...], dd_scr[...]) + dt_scr[...]
        dlogit = dla * (1.0 / GLA_TAU) * _sigmoid(-logit)
        dbg_ref[...] += jnp.sum(dlogit, axis=0, keepdims=True)
        dwgu_ref[...] += _dot_bf16(al_ref[...], dlogit, _TN)
        dal_ref[...] = _dot_bf16(dlogit, wgu_ref[...], _NT).astype(BF16)

    rev = lambda i: nt - 1 - i
    blk = lambda w, j: pl.BlockSpec((tT, w), lambda i: (rev(i), j))
    st_blk = pl.BlockSpec((nc, GLA_HEADS, GLA_DV, GLA_DK), lambda i: (rev(i), 0, 0, 0))
    sp_blk = pl.BlockSpec((1, GLA_HEADS, GLA_DV, GLA_DK), lambda i: (jnp.maximum(rev(i) * nc - 1, 0), 0, 0, 0))
    return _call(
        body, name="gla_bwd", grid=(nt,),
        in_specs=[blk(512, 0), blk(512, 1), blk(1024, 1), blk(1024, 2), blk(LANES, 0)] + [_whole()] * 5
        + [st_blk, sp_blk, blk(GLA_V, 0)],
        out_specs=[blk(W_GLA, 0), blk(LANES, 0), pl.BlockSpec((1, GLA_V), lambda i: (0, 0)),
                   pl.BlockSpec((1, GLA_QK), lambda i: (0, 0)), pl.BlockSpec((LANES, GLA_QK), lambda i: (0, 0))],
        out_shape=[_sds((T, W_GLA), BF16), _sds((T, LANES), BF16), _sds((1, GLA_V), F32), _sds((1, GLA_QK), F32),
                   _sds((LANES, GLA_QK), F32)],
        scratch_shapes=[pltpu.VMEM((GLA_HEADS, GLA_DV, GLA_DK), F32), pltpu.VMEM((tT, GLA_QK), F32),
                        pltpu.VMEM((tT, GLA_QK), F32)],
        args=(proj, proj, proj, proj, alow, wgu, b_gate, gn, _chunk_masks(tT, upper=True), _chunk_masks(tT, upper=False),
              states, states, dy_gla), job=job)


def _inproj_bwd(x, dx1, g1, w_all, dparts, job=None):
    T = x.shape[0]
    tT = _row_tile(T, 512)
    offs = (0, W_GLA, W_GLA + W_SGU, N_MAIN)

    def body(x_ref, dx1_ref, g_ref, w_hbm, *rest):
        part_refs, (dx_ref, dg_ref, w_ref, w_sems) = rest[:len(offs)], rest[len(offs):]

        def compute(first):
            if first:
                copies = [pltpu.make_async_copy(w_hbm.at[:, pl.ds(off, p.shape[1])], w_ref.at[:, pl.ds(off, p.shape[1])],
                                                w_sems.at[k]) for k, (off, p) in enumerate(zip(offs, dparts))]
                for cp in copies:
                    cp.start(priority=1)
            da = jnp.zeros((tT, D_MODEL), F32)
            for k, (off, p_ref) in enumerate(zip(offs, part_refs)):
                if first:
                    copies[k].wait()
                da = da + _dot(p_ref[...], w_ref[:, off:off + p_ref.shape[1]], _NT)
            xv = x_ref[...]
            dx, dg = _rms_bwd(da, xv, _rms_stats(xv), g_ref[...])
            dg_sum = jnp.sum(dg, axis=0, keepdims=True)
            dg_ref[...] = dg_sum if first else dg_ref[...] + dg_sum
            dx_ref[...] = dx1_ref[...] + dx

        first_step = pl.program_id(0) == 0
        pl.when(first_step)(lambda: compute(True))
        pl.when(jnp.logical_not(first_step))(lambda: compute(False))

    row = lambda w: pl.BlockSpec((tT, w), lambda i: (i, 0))
    vec = pl.BlockSpec((1, D_MODEL), lambda i: (0, 0))
    return _call(
        body, name="inproj_bwd", grid=(T // tT,),
        in_specs=[row(D_MODEL), row(D_MODEL), vec, pl.BlockSpec(memory_space=pl.ANY)] + [row(p.shape[1]) for p in dparts],
        out_specs=[row(D_MODEL), vec], out_shape=[_sds((T, D_MODEL), F32), _sds((1, D_MODEL), F32)],
        scratch_shapes=[pltpu.VMEM(w_all.shape, BF16), pltpu.SemaphoreType.DMA((len(offs),))],
        args=(x, dx1, g1, w_all, *dparts), job=job)


def _tn_matmul(a, b, name, job=None):
    T, M = a.shape
    N = b.shape[1]
    tk = _row_tile(T, 1024)
    tm = M if M <= 1024 else 1408
    tn = N // 2 if N > 2048 else N
    assert M % tm == 0 and N % tn == 0

    def body(a_ref, b_ref, o_ref):
        @pl.when(pl.program_id(2) == 0)
        def _():
            o_ref[...] = _dot(a_ref[...], b_ref[...], _TN)

        @pl.when(pl.program_id(2) > 0)
        def _():
            o_ref[...] += _dot(a_ref[...], b_ref[...], _TN)

    res, jres = _call(
        body, name=name, grid=(M // tm, N // tn, T // tk),
        in_specs=[pl.BlockSpec((tk, tm), lambda i, j, k: (k, i)), pl.BlockSpec((tk, tn), lambda i, j, k: (k, j))],
        out_specs=[pl.BlockSpec((tm, tn), lambda i, j, k: (i, j))], out_shape=[_sds((M, N), F32)], args=(a, b), job=job)
    return res[0], jres


def _pad_rows(a, rows=8):
    return jnp.pad(a, ((0, rows - a.shape[0]), (0, LANES - a.shape[1])))


def _halves_view(dw):
    r = dw.shape[0] // N_CHIPS
    return dw.reshape(N_CHIPS, 2, r // 2, dw.shape[1])


def kernel(x, norm_pre_mix, w_in, w_gate_up, b_gate, gla_norm, sgu_ln_g, sgu_ln_b, w_spatial, b_spatial, w_branch_gla, w_branch_sgu, w_out, norm_post_mix, norm_pre_ffn, w_ffn_in, w_ffn_out, norm_post_ffn, loss_target, m_norm_pre_mix, m_w_in, m_w_gate_up, m_b_gate, m_gla_norm, m_sgu_ln_g, m_sgu_ln_b, m_w_spatial, m_b_spatial, m_w_branch_gla, m_w_branch_sgu, m_w_out, m_norm_post_mix, m_norm_pre_ffn, m_w_ffn_in, m_w_ffn_out, m_norm_post_ffn, v_norm_pre_mix, v_w_in, v_w_gate_up, v_b_gate, v_gla_norm, v_sgu_ln_g, v_sgu_ln_b, v_w_spatial, v_b_spatial, v_w_branch_gla, v_w_branch_sgu, v_w_out, v_norm_post_mix, v_norm_pre_ffn, v_w_ffn_in, v_w_ffn_out, v_norm_post_ffn):
    chip = 2 * lax.axis_index("x") + lax.axis_index("y")
    xt, tgt = x[0], loss_target[0]

    tiny = jnp.concatenate([w_gate_up[0], _pad_rows(gla_norm[0]), _pad_rows(sgu_ln_g[0]), _pad_rows(sgu_ln_b[0]),
                            jnp.zeros((24, LANES), F32)], axis=0)

    def with_own(gathered, own):
        return lax.dynamic_update_slice(gathered, own[None], (chip, 0, 0))

    w_in_t, m_in_t, v_in_t = w_in[0].T, m_w_in[0].T, v_w_in[0].T
    w_in_b = _transposed_cast(w_in_t)
    (*own_rows, fi_top, fi_bot), (g_in, g_tiny) = _cast_weights(
        [w_branch_gla[0], w_branch_sgu[0], w_out[0], w_ffn_out[0]], w_ffn_in[0], job=_job_gather([w_in_b, tiny]))
    g_tiny = with_own(g_tiny, tiny)
    w_all = _relayout_w_in(with_own(g_in, w_in_b))
    cols = lambda a: a.transpose(1, 0, 2).reshape(a.shape[1], N_CHIPS * a.shape[2])
    wgu = jnp.pad(cols(g_tiny[:, 0:16]), ((0, LANES - GLA_RANK), (0, 0)))
    gn = cols(g_tiny[:, 16:20, :64]).reshape(1, GLA_V)
    ln_g = cols(g_tiny[:, 24:28, :64]).reshape(1, 1024)
    ln_b = cols(g_tiny[:, 32:36, :64]).reshape(1, 1024)
    b_sp_t = jnp.pad(b_spatial[0].T, ((0, 0), (0, LANES - SGU_GROUPS)))
    w_sp = w_spatial[0]

    (a, proj, alow), g_rows = _inproj_fwd(xt, norm_pre_mix, w_all, job=_job_gather(own_rows))
    m_in_t, alow = lax.optimization_barrier((m_in_t, alow))
    v_in_t, a = lax.optimization_barrier((v_in_t, a))
    rows = lambda g: g.reshape(N_CHIPS * g.shape[1], g.shape[2])
    w_bg, w_bs, w_o, w_fo = [rows(with_own(g, own)) for g, own in zip(g_rows, own_rows)]
    (y_gla, states), (g_top,) = _gla_fwd(proj, alow, wgu, b_gate, gn, job=_job_gather([fi_top]))
    (y_sgu, zg, zs, merged, mix, x1), (g_bot,) = _sgu_merge_fwd(
        xt, proj, y_gla, ln_g, ln_b, w_sp, b_sp_t, w_bg, w_bs, w_o, norm_post_mix, job=_job_gather([fi_bot]))
    h, f, dgu, dy, dx1, loss, d_gpf, d_gpo = _ffn_fwd_bwd(x1, tgt, with_own(g_top, fi_top), with_own(g_bot, fi_bot),
                                                          w_fo, norm_pre_ffn, norm_post_ffn)

    whole = lambda hs: [[(h_, None)] for h_ in hs]
    dw_fo, _ = _tn_matmul(f, dy, "dw_ffn_out")
    dw_fo4 = _halves_view(dw_fo)
    dw_fi, (q_fo,) = _tn_matmul(h, dgu, "dw_ffn_in", job=_job_to_other_core([[(dw_fo4, 0)]]))
    c_fo, = _presum([dw_fo4], [q_fo], "presum_ffn_out")
    (dmix, dzg, dzs, dp_mrg, dyg, dp_sgu, d_gpm, d_wsp, d_bsp_t, d_lng, d_lnb), (s_fo, q_fi) = _merge_sgu_bwd(
        dx1, mix, proj, zg, zs, w_bg, w_bs, w_o, norm_post_mix, ln_g, ln_b, w_sp, b_sp_t,
        job=_join(_job_scatter([c_fo]), _job_to_other_core([[(dw_fi, 0)]])))
    c_fi, = _presum([dw_fi], [q_fi], "presum_ffn_in")
    dw_c, _ = _tn_matmul(a, dp_mrg, "dw_in_merge")
    dw_b, _ = _tn_matmul(a, dp_sgu, "dw_in_sgu")
    dw_o4 = _halves_view(_tn_matmul(merged, dmix, "dw_out")[0])
    dw_bg4 = _halves_view(_tn_matmul(y_gla, dzg, "dw_branch_gla")[0])
    dw_bs4 = _halves_view(_tn_matmul(y_sgu, dzs, "dw_branch_sgu")[0])
    h_fo, = _sum_slots([c_fo], [s_fo], "sum_ffn_out")
    (dp_gla, dal, d_gn, d_bg, d_wgu), (s_fi, t_fo, q_b, q_c, q_o, q_bg, q_bs) = _gla_bwd(
        proj, alow, wgu, b_gate, gn, states, dyg,
        job=_join(_job_scatter([c_fi]), _job_to_other_core(
            whole([h_fo]) + [[(dw_b, 0)], [(dw_c, 0)], [(dw_o4, 0)], [(dw_bg4, 0)], [(dw_bs4, 0)]])))
    c_o, c_bg, c_bs = _presum([dw_o4, dw_bg4, dw_bs4], [q_o, q_bg, q_bs], "presum_out_branches")
    h_fi, = _sum_slots([c_fi], [s_fi], "sum_ffn_in")
    dw_d, _ = _tn_matmul(a, dal, "dw_in_gate")
    dw_a, (s_o, s_bg, s_bs, t_fi, q_d) = _tn_matmul(
        a, dp_gla, "dw_in_gla",
        job=_join(_job_scatter([c_o, c_bg, c_bs]), _job_to_other_core(whole([h_fi]) + [[(dw_d, 0)]])))
    h_o, h_bg, h_bs = _sum_slots([c_o, c_bg, c_bs], [s_o, s_bg, s_bs], "sum_out_branches")

    grads, deltas, new_m, new_v = {}, {}, {}, {}

    def update(call, names, ws, ms, vs, g_mine, g_theirs, job=None):
        res, jres = _adamw([w[0] for w in ws], [m[0] for m in ms], [v[0] for v in vs], g_mine, g_theirs, call, job=job)
        for name, (g, d, m2, v2) in zip(names, res):
            grads[name], deltas[name], new_m[name], new_v[name] = g[None], d[None], m2[None], v2[None]
        return jres

    dw_in = [(dw_a, 0), (dw_b, W_GLA), (dw_c, W_GLA + W_SGU), (dw_d, N_MAIN)]
    q_a, t_o, t_bg, t_bs = update("adamw_w_ffn_out", ["w_ffn_out"], [w_ffn_out], [m_w_ffn_out], [v_w_ffn_out],
                                  [[h_fo]], [[t_fo]],
                                  job=_job_to_other_core([[(dw_a, 0)]] + whole([h_o, h_bg, h_bs])))
    q_in = [q_a, q_b, q_c, q_d]
    hr_in = D_MODEL // 2
    c_in_a, _ = _presum_w_in(dw_in, q_in, 0, hr_in // 8, "presum_w_in_a")
    c_in_b, (s_in_a,) = _presum_w_in(dw_in, q_in, hr_in // 8, 7 * hr_in // 8, "presum_w_in_b",
                                     job=_job_scatter([c_in_a]))
    update("adamw_w_ffn_in", ["w_ffn_in"], [w_ffn_in], [m_w_ffn_in], [v_w_ffn_in], [[h_fi]], [[t_fi]])
    update("adamw_out_branches", ["w_out", "w_branch_gla", "w_branch_sgu"], [w_out, w_branch_gla, w_branch_sgu],
           [m_w_out, m_w_branch_gla, m_w_branch_sgu], [v_w_out, v_w_branch_gla, v_w_branch_sgu],
           [[h_o], [h_bg], [h_bs]], [[t_o], [t_bg], [t_bs]])
    (grad_x, d_g1), (s_in_b,) = _inproj_bwd(xt, dx1, norm_pre_mix, w_all, (dp_gla, dp_sgu, dp_mrg, dal),
                                            job=_job_scatter([c_in_b]))
    h_in = _sum_slots([c_in_a], [s_in_a], "sum_w_in_a") + _sum_slots([c_in_b], [s_in_b], "sum_w_in_b")
    t_in = _run_job(_job_to_other_core(whole(h_in)), "swap_w_in")
    for store, val in zip((grads, deltas, new_m, new_v),
                          _adamw_transposed(w_in_t, m_in_t, v_in_t, h_in, t_in, "adamw_w_in")):
        store["w_in"] = val.T[None]

    small_names = ["w_spatial", "w_gate_up", "norm_pre_mix", "norm_post_mix", "norm_pre_ffn", "norm_post_ffn", "b_gate",
                   "b_spatial", "gla_norm", "sgu_ln_g", "sgu_ln_b"]
    loss_out, small = _small_adamw(
        _small_sum([d_wsp, d_wgu, d_g1, d_gpm, d_gpf, d_gpo, d_bg, d_bsp_t, d_gn, d_lng, d_lnb, loss]),
        [w_spatial, w_gate_up, norm_pre_mix, norm_post_mix, norm_pre_ffn, norm_post_ffn, b_gate, b_spatial, gla_norm,
         sgu_ln_g, sgu_ln_b],
        [m_w_spatial, m_w_gate_up, m_norm_pre_mix, m_norm_post_mix, m_norm_pre_ffn, m_norm_post_ffn, m_b_gate,
         m_b_spatial, m_gla_norm, m_sgu_ln_g, m_sgu_ln_b],
        [v_w_spatial, v_w_gate_up, v_norm_pre_mix, v_norm_post_mix, v_norm_pre_ffn, v_norm_post_ffn, v_b_gate,
         v_b_spatial, v_gla_norm, v_sgu_ln_g, v_sgu_ln_b])
    for store, vals in zip((grads, deltas, new_m, new_v), small):
        store.update(zip(small_names, vals))

    order = ["norm_pre_mix", "w_in", "w_gate_up", "b_gate", "gla_norm", "sgu_ln_g", "sgu_ln_b", "w_spatial", "b_spatial",
             "w_branch_gla", "w_branch_sgu", "w_out", "norm_post_mix", "norm_pre_ffn", "w_ffn_in", "w_ffn_out",
             "norm_post_ffn"]
    out = [loss_out, grad_x[None]]
    for store in (grads, deltas, new_m, new_v):
        out.extend(store[n] for n in order)
    return tuple(out)
```

```python
import jax
import jax.numpy as jnp
from jax import lax
from jax.experimental import pallas as pl
from jax.experimental.pallas import tpu as pltpu

F32 = jnp.float32
BF16 = jnp.bfloat16

D_MODEL = 1024
GLA_HEADS = 4
GLA_DK = 128
GLA_DV = 256
GLA_QK = GLA_HEADS * GLA_DK
GLA_V = GLA_HEADS * GLA_DV
GLA_RANK = 16
GLA_TAU = 16.0
CHUNK = 64
SGU_GROUPS = 4
SGU_BLOCK = 128
SGU_DG = 256
D_FF = 2816
EPS = 1e-6
LANES = 128

OFF_Q, OFF_K, OFF_V, OFF_R, OFF_SU, OFF_SV, OFF_GG, OFF_GS, OFF_AL = 0, 512, 1024, 2048, 3072, 4096, 5120, 6144, 7168
W_GLA, W_SGU, W_MRG = 3072, 2048, 2048
N_MAIN = 7168
N_ALL = N_MAIN + LANES
_IN_SPLITS = (GLA_QK, GLA_QK, GLA_V, GLA_V, GLA_RANK, 1024, 1024, 1024, 1024)
_IN_STARTS = tuple(sum(_IN_SPLITS[:i]) for i in range(len(_IN_SPLITS) + 1))
_IN_DST = (OFF_Q, OFF_K, OFF_V, OFF_R, OFF_AL, OFF_SU, OFF_SV, OFF_GG, OFF_GS)
D_IN = _IN_STARTS[-1]

ADAM_LR = 0.001
ADAM_B1 = 0.9
ADAM_B2 = 0.999
ADAM_EPS = 1e-08
ADAM_WD = 0.01
ADAM_STEP = 10

VMEM_LIMIT_BYTES = 56 * 1024 * 1024
N_CHIPS = 4
N_PEER = N_CHIPS - 1
N_DEV = 8
MESH = pl.DeviceIdType.MESH

_NN = (((1,), (0,)), ((), ()))
_NT = (((1,), (1,)), ((), ()))
_TN = (((0,), (0,)), ((), ()))


def _dot(a, b, dims=_NN):
    return lax.dot_general(a, b, dims, preferred_element_type=F32)


def _split(x):
    hi = x.astype(BF16)
    lo = (x - hi.astype(F32)).astype(BF16)
    return hi, lo


def _dot_bf16(a, b, dims=_NN):
    return _dot(a.astype(BF16), b.astype(BF16), dims)


def _dot_exact_lhs(m, x):
    xh, xl = _split(x)
    return _dot(m, xh) + _dot(m, xl)


def _sigmoid(x):
    return 0.5 * jnp.tanh(0.5 * x) + 0.5


def _log_sigmoid(x):
    return jnp.minimum(x, 0.0) - jnp.log(1.0 + jnp.exp(-jnp.abs(x)))


_GELU_C = 0.7978845608028654
_GELU_A = 0.044715


def _gelu_and_grad(x):
    x2 = x * x
    t = jnp.tanh(_GELU_C * (x + _GELU_A * x * x2))
    g = 0.5 * x * (1.0 + t)
    dg = 0.5 * (1.0 + t) + 0.5 * x * (1.0 - t * t) * (_GELU_C * (1.0 + 3.0 * _GELU_A * x2))
    return g, dg


def _gelu(x):
    t = jnp.tanh(_GELU_C * (x + _GELU_A * x * x * x))
    return 0.5 * x * (1.0 + t)


def _rms_stats(x):
    return lax.rsqrt(jnp.mean(x * x, axis=-1, keepdims=True) + EPS)


def _rms_bwd(dout, y, r, g):
    yhat = y * r
    dn = dout * g
    dy = r * (dn - yhat * jnp.mean(dn * yhat, axis=-1, keepdims=True))
    return dy, dout * yhat


def _whole():
    return pl.BlockSpec(memory_space=pltpu.VMEM)


def _row_tile(T, want):
    t = min(T, want)
    assert T % t == 0
    return t


def _chunk_masks(tT, upper):
    row = lax.broadcasted_iota(jnp.int32, (tT, tT), 0)
    col = lax.broadcasted_iota(jnp.int32, (tT, tT), 1)
    same = (row // CHUNK) == (col // CHUNK)
    tri = (col > row) if upper else (col < row)
    return jnp.where(same & tri, 1.0, 0.0).astype(BF16)


class _Job:
    def __init__(self, ins, out_shapes, scratch, start, finish, mid=None):
        self.ins, self.out_shapes, self.scratch = list(ins), list(out_shapes), list(scratch)
        self.start, self.finish, self.mid = start, finish, mid


def _join(*jobs):
    def split(refs, counts):
        out, at = [], 0
        for n in counts:
            out.append(refs[at:at + n])
            at += n
        return out

    ni, no, ns = [len(j.ins) for j in jobs], [len(j.out_shapes) for j in jobs], [len(j.scratch) for j in jobs]

    def start(ins, outs, scr):
        for j, a, b, c in zip(jobs, split(ins, ni), split(outs, no), split(scr, ns)):
            j.start(a, b, c)

    def finish(ins, outs, scr):
        for j, a, b, c in zip(jobs, split(ins, ni), split(outs, no), split(scr, ns)):
            j.finish(a, b, c)

    def mid(ins, outs, scr):
        for j, a, b, c in zip(jobs, split(ins, ni), split(outs, no), split(scr, ns)):
            if j.mid is not None:
                j.mid(a, b, c)

    return _Job(sum((j.ins for j in jobs), []), sum((j.out_shapes for j in jobs), []),
                sum((j.scratch for j in jobs), []), start, finish, mid if any(j.mid for j in jobs) else None)


def _mesh_pos():
    return lax.axis_index("x"), lax.axis_index("y"), lax.axis_index("c")


def _peer_chips(xi, yi):
    return [(1 - xi, yi), (xi, 1 - yi), (1 - xi, 1 - yi)]


def _half(ci, rows):
    return pl.ds(pl.multiple_of(ci * rows, 8), rows)


def _sds(shape, dtype):
    return jax.ShapeDtypeStruct(tuple(shape), dtype)


def _job_gather(arrs):
    n = len(arrs)
    kinds = 12
    Y0, Y1, X1, X0, ON_X, ON_Y, D2D = 0, 1, 2, 3, 4, 5, 6

    def copies(ins, outs, scr):
        send_sems, recv_sems = scr
        xi, yi, ci = _mesh_pos()
        me, cx, cy, cd = 2 * xi + yi, 2 * (1 - xi) + yi, 2 * xi + (1 - yi), 2 * (1 - xi) + (1 - yi)
        to_x, to_y, to_core = (1 - xi, yi, ci), (xi, 1 - yi, ci), (xi, yi, 1 - ci)
        table = []
        for k in range(n):
            qr = arrs[k].shape[0] // 4

            def rows(core, q):
                return pl.ds(pl.multiple_of((2 * core + q) * qr, 8), qr)

            def cp(kind, src, dst, to):
                s = k * kinds + kind
                return pltpu.make_async_remote_copy(src_ref=src, dst_ref=dst, send_sem=send_sems.at[s],
                                                    recv_sem=recv_sems.at[s], device_id=to, device_id_type=MESH)

            def slab(chip, core, q):
                return outs[k].at[chip, rows(core, q)]

            t = {}
            for kind, q, to, frm in ((Y0, 0, to_y, cy), (Y1, 1, to_y, cy), (X1, 1, to_x, cx), (X0, 0, to_x, cx)):
                mine = ins[k].at[rows(ci, q)]
                t[kind] = (cp(kind, mine, slab(me, ci, q), to), cp(kind, mine, slab(frm, ci, q), to))
            t[ON_X] = (cp(ON_X, slab(cy, ci, 0), slab(cy, ci, 0), to_x), cp(ON_X, slab(cy, ci, 0), slab(cd, ci, 0), to_x))
            t[ON_Y] = (cp(ON_Y, slab(cx, ci, 1), slab(cx, ci, 1), to_y), cp(ON_Y, slab(cx, ci, 1), slab(cd, ci, 1), to_y))
            for i, (chip, q) in enumerate(((cy, 0), (cy, 1), (cx, 1), (cx, 0), (cd, 0), (cd, 1))):
                t[D2D + i] = (cp(D2D + i, slab(chip, ci, q), slab(chip, ci, q), to_core),
                              cp(D2D + i, slab(chip, ci, q), slab(chip, 1 - ci, q), to_core))
            table.append(t)
        return table

    def start(ins, outs, scr):
        table = copies(ins, outs, scr)
        for kind in (Y0, X1, Y1, X0):
            for t in table:
                t[kind][0].start()

    def arrived(table, kind, then):
        for t in table:
            t[kind][1].wait_recv()
            for nxt in then:
                t[nxt][0].start()

    def mid(ins, outs, scr):
        table = copies(ins, outs, scr)
        arrived(table, Y0, (ON_X, D2D + 0))
        arrived(table, X1, (ON_Y, D2D + 2))

    def finish(ins, outs, scr):
        table = copies(ins, outs, scr)
        arrived(table, Y1, (D2D + 1,))
        arrived(table, X0, (D2D + 3,))
        arrived(table, ON_X, (D2D + 4,))
        arrived(table, ON_Y, (D2D + 5,))
        for t in table:
            for i in range(6):
                t[D2D + i][1].wait_recv()
            for kind in range(kinds):
                t[kind][0].wait_send()

    dma = pltpu.SemaphoreType.DMA
    return _Job(arrs, [_sds((N_CHIPS,) + a.shape, a.dtype) for a in arrs], [dma((n * kinds,))] * 2, start, finish, mid)


def _job_scatter(parts):
    n = len(parts)

    def copies(ins, outs, scr):
        send_sems, recv_sems = scr
        xi, yi, ci = _mesh_pos()
        res = []
        for k in range(n):
            for j, (px, py) in enumerate(_peer_chips(xi, yi)):
                s = k * N_PEER + j
                res.append(pltpu.make_async_remote_copy(
                    src_ref=ins[k].at[2 * px + py], dst_ref=outs[k].at[j], send_sem=send_sems.at[s],
                    recv_sem=recv_sems.at[s], device_id=(px, py, ci), device_id_type=MESH))
        return res

    def start(ins, outs, scr):
        for cp in copies(ins, outs, scr):
            cp.start()

    def finish(ins, outs, scr):
        for cp in copies(ins, outs, scr):
            cp.wait_recv()
            cp.wait_send()

    dma = pltpu.SemaphoreType.DMA
    return _Job(parts, [_sds((N_PEER,) + p.shape[1:], p.dtype) for p in parts], [dma((n * N_PEER,))] * 2, start, finish)


def _job_to_other_core(groups):
    pieces = [(g, a, off) for g, group in enumerate(groups) for a, off in group]
    n = len(pieces)

    def geometry(group):
        a0, off0 = group[0]
        if off0 is None:
            return a0.shape
        if a0.ndim == 4:
            return (N_CHIPS, a0.shape[2], a0.shape[3])
        return (a0.shape[0] // 2, sum(a.shape[1] for a, _ in group))

    def copies(ins, outs, scr):
        send_sems, recv_sems = scr
        xi, yi, ci = _mesh_pos()
        res = []
        for p, (g, a, off) in enumerate(pieces):
            if off is None:
                give, land = ins[p], outs[g]
            elif a.ndim == 4:
                give, land = ins[p].at[pl.ds(0, N_CHIPS), 1 - ci], outs[g]
            else:
                hr, w = a.shape[0] // 2, a.shape[1]
                give, land = ins[p].at[_half(1 - ci, hr)], outs[g].at[pl.ds(0, hr), pl.ds(off, w)]
            res.append(pltpu.make_async_remote_copy(
                src_ref=give, dst_ref=land, send_sem=send_sems.at[p], recv_sem=recv_sems.at[p],
                device_id=(xi, yi, 1 - ci), device_id_type=MESH))
        return res

    def start(ins, outs, scr):
        for cp in copies(ins, outs, scr):
            cp.start()

    def finish(ins, outs, scr):
        for cp in copies(ins, outs, scr):
            cp.wait_recv()
            cp.wait_send()

    dma = pltpu.SemaphoreType.DMA
    return _Job([a for _, a, _ in pieces], [_sds(geometry(group), group[0][0].dtype) for group in groups],
                [dma((n,))] * 2, start, finish)


def _call(body, *, name, grid, in_specs, out_specs, out_shape, args, scratch_shapes=(), parallel=False, job=None,
          by_core=False, fuse_inputs=False):
    n_in, n_out, n_scr = len(in_specs), len(out_specs), len(scratch_shapes)
    hbm = pl.BlockSpec(memory_space=pl.ANY)
    n_ji, n_jo = (len(job.ins), len(job.out_shapes)) if job is not None else (0, 0)
    lead = 1 if by_core else 0

    def kernel_fn(*refs):
        core, refs = refs[:lead], refs[lead:]
        ins, refs = refs[:n_in], refs[n_in:]
        j_ins, refs = refs[:n_ji], refs[n_ji:]
        outs, refs = refs[:n_out], refs[n_out:]
        j_outs, refs = refs[:n_jo], refs[n_jo:]
        scr, j_scr = refs[:n_scr], refs[n_scr:]
        if job is None:
            body(*core, *ins, *outs, *scr)
            return
        ids = [pl.program_id(d) for d in range(len(grid))]
        first = ids[0] == 0
        last = ids[0] == grid[0] - 1
        for d in range(1, len(grid)):
            first = first & (ids[d] == 0)
            last = last & (ids[d] == grid[d] - 1)

        @pl.when(first)
        def _():
            job.start(j_ins, j_outs, j_scr)

        if job.mid is not None and grid[0] >= 4:
            half_way = ids[0] == grid[0] // 2
            for d in range(1, len(grid)):
                half_way = half_way & (ids[d] == 0)

            @pl.when(half_way)
            def _():
                job.mid(j_ins, j_outs, j_scr)

        body(*core, *ins, *outs, *scr)

        @pl.when(last)
        def _():
            if job.mid is not None and grid[0] < 4:
                job.mid(j_ins, j_outs, j_scr)
            job.finish(j_ins, j_outs, j_scr)

    sem = ("parallel" if parallel and job is None else "arbitrary",) * len(grid)
    all_in = list(in_specs) + [hbm] * n_ji
    all_out = list(out_specs) + [hbm] * n_jo
    all_scratch = list(scratch_shapes) + (job.scratch if job is not None else [])
    all_shapes = list(out_shape) + (job.out_shapes if job is not None else [])
    all_args = list(args) + (job.ins if job is not None else [])
    params = pltpu.CompilerParams(dimension_semantics=sem, vmem_limit_bytes=VMEM_LIMIT_BYTES,
                                  allow_input_fusion=[fuse_inputs] * len(all_args) if fuse_inputs else None)
    if by_core:
        spec = pltpu.PrefetchScalarGridSpec(num_scalar_prefetch=1, grid=grid, in_specs=all_in, out_specs=all_out,
                                            scratch_shapes=all_scratch)
        core = lax.axis_index("c").astype(jnp.int32).reshape(1)
        res = pl.pallas_call(kernel_fn, name=name, grid_spec=spec, out_shape=all_shapes, compiler_params=params)(
            core, *all_args)
    else:
        res = pl.pallas_call(kernel_fn, name=name, grid=grid, in_specs=all_in, out_specs=all_out, out_shape=all_shapes,
                             scratch_shapes=all_scratch, compiler_params=params)(*all_args)
    return list(res[:n_out]), list(res[n_out:])


def _run_job(job, name):
    n_i, n_o = len(job.ins), len(job.out_shapes)

    def body(*refs):
        ins, outs, scr = refs[:n_i], refs[n_i:n_i + n_o], refs[n_i + n_o:]
        job.start(ins, outs, scr)
        if job.mid is not None:
            job.mid(ins, outs, scr)
        job.finish(ins, outs, scr)

    hbm = pl.BlockSpec(memory_space=pl.ANY)
    return list(pl.pallas_call(body, name=name, in_specs=[hbm] * n_i, out_specs=[hbm] * n_o, out_shape=job.out_shapes,
                               scratch_shapes=job.scratch)(*job.ins))


def _adam_values(w, m, v, g):
    m2 = ADAM_B1 * m + (1.0 - ADAM_B1) * g
    v2 = ADAM_B2 * v + (1.0 - ADAM_B2) * (g * g)
    delta = -ADAM_LR * ((m2 / (1.0 - ADAM_B1 ** ADAM_STEP)) / (jnp.sqrt(v2 / (1.0 - ADAM_B2 ** ADAM_STEP)) + ADAM_EPS)
                        + ADAM_WD * w)
    return delta, m2, v2


_P_WSP, _P_WGU, _P_NORM, _P_BG, _P_BSP, _P_HEAD, _P_LOSS, _P_ROWS = 0, 512, 576, 608, 616, 624, 720, 736


def _small_sum(dgrads):
    hr = _P_ROWS // 2

    def body(dwsp, dwgu, dg1, dgpm, dgpf, dgpo, dbg, dbspt, dgn, dlng, dlnb, loss_in, tot, pack, pair, slots, send_sems,
             recv_sems):
        xi, yi, ci = _mesh_pos()
        chip = 2 * xi + yi

        pack[...] = jnp.zeros_like(pack)
        for g in range(SGU_GROUPS):
            pack[_P_WSP + g * SGU_BLOCK:_P_WSP + (g + 1) * SGU_BLOCK] = dwsp[g]
        for j in range(N_CHIPS):
            pack[_P_WGU + GLA_RANK * j:_P_WGU + GLA_RANK * (j + 1)] = dwgu[0:GLA_RANK, LANES * j:LANES * (j + 1)]
        for k, r in enumerate((dg1, dgpm, dgpf, dgpo)):
            for q in range(8):
                pack[_P_NORM + 8 * k + q:_P_NORM + 8 * k + q + 1] = r[:, LANES * q:LANES * (q + 1)]
        for q in range(4):
            pack[_P_BG + q:_P_BG + q + 1] = dbg[:, LANES * q:LANES * (q + 1)]
        pack[_P_BSP:_P_BSP + SGU_GROUPS] = jnp.transpose(dbspt[...])[0:SGU_GROUPS]
        for k, r in enumerate((dgn, dlng, dlnb)):
            for j in range(N_CHIPS):
                for hh in range(4):
                    row = _P_HEAD + 32 * k + 8 * j + hh
                    pack[row:row + 1, 0:64] = r[:, 256 * hh + 64 * j:256 * hh + 64 * (j + 1)]
        pack[_P_LOSS:_P_LOSS + 1] = loss_in[...]

        sibling = dict(device_id=(xi, yi, 1 - ci), device_id_type=MESH)
        to_sibling = pltpu.make_async_remote_copy(src_ref=pack, dst_ref=pair, send_sem=send_sems.at[N_PEER],
                                                  recv_sem=recv_sems.at[N_PEER], **sibling)
        to_sibling.start()
        to_sibling.wait_recv()
        to_sibling.wait_send()
        pack[...] = pack[...] + pair[...]
        mine = pl.ds(pl.multiple_of(ci * hr, 8), hr)
        theirs = pl.ds(pl.multiple_of((1 - ci) * hr, 8), hr)
        slots[chip] = pack[mine, :]

        def copy(j, slot):
            px, py = _peer_chips(xi, yi)[j]
            return pltpu.make_async_remote_copy(
                src_ref=pack.at[mine], dst_ref=slots.at[slot(2 * px + py)], send_sem=send_sems.at[j],
                recv_sem=recv_sems.at[j], device_id=(px, py, ci), device_id_type=MESH)

        sends = [copy(j, lambda peer_chip: chip) for j in range(N_PEER)]
        for cp in sends:
            cp.start()
        for j in range(N_PEER):
            copy(j, lambda peer_chip: peer_chip).wait_recv()
        for cp in sends:
            cp.wait_send()
        acc = slots[0]
        for d in range(1, N_CHIPS):
            acc = acc + slots[d]
        tot[mine, :] = acc
        half_over = pltpu.make_async_remote_copy(src_ref=tot.at[mine], dst_ref=tot.at[mine], send_sem=send_sems.at[N_PEER + 1],
                                                 recv_sem=recv_sems.at[N_PEER + 1], **sibling)
        half_back = pltpu.make_async_remote_copy(src_ref=tot.at[mine], dst_ref=tot.at[theirs], send_sem=send_sems.at[N_PEER + 1],
                                                 recv_sem=recv_sems.at[N_PEER + 1], **sibling)
        half_over.start()
        half_back.wait_recv()
        half_over.wait_send()

    return pl.pallas_call(
        body, name="small_sum", in_specs=[_whole()] * 12, out_specs=_whole(), out_shape=_sds((_P_ROWS, LANES), F32),
        scratch_shapes=[pltpu.VMEM((_P_ROWS, LANES), F32), pltpu.VMEM((_P_ROWS, LANES), F32),
                        pltpu.VMEM((N_CHIPS, hr, LANES), F32),
                        pltpu.SemaphoreType.DMA((N_PEER + 2,)), pltpu.SemaphoreType.DMA((N_PEER + 2,))],
        compiler_params=pltpu.CompilerParams(vmem_limit_bytes=VMEM_LIMIT_BYTES),
    )(*dgrads)


def _small_adamw(tot, ws, ms, vs):
    n = len(ws)

    def body(*refs):
        tot = refs[0]
        w_refs, m_refs, v_refs = refs[1:1 + n], refs[1 + n:1 + 2 * n], refs[1 + 2 * n:1 + 3 * n]
        loss_out = refs[1 + 3 * n]
        outs = refs[2 + 3 * n:]
        chip = 2 * lax.axis_index("x") + lax.axis_index("y")
        loss_out[...] = tot[_P_LOSS:_P_LOSS + 1, 0:1]

        def step(k, g, pick, put):
            d, m2, v2 = _adam_values(pick(w_refs[k]), pick(m_refs[k]), pick(v_refs[k]), g)
            for o, val in zip((outs[k], outs[n + k], outs[2 * n + k], outs[3 * n + k]), (g, d, m2, v2)):
                put(o, val)

        def whole(ref):
            return ref[0]

        def put_whole(ref, val):
            ref[0] = val

        for g in range(SGU_GROUPS):
            def pick_g(ref, g=g):
                return ref[0, g]

            def put_g(ref, val, g=g):
                ref[0, g] = val

            step(0, tot[_P_WSP + g * SGU_BLOCK:_P_WSP + (g + 1) * SGU_BLOCK], pick_g, put_g)
        step(1, tot[pl.ds(pl.multiple_of(_P_WGU + GLA_RANK * chip, GLA_RANK), GLA_RANK), :], whole, put_whole)
        for k, (base, chunks) in enumerate(((_P_NORM, 8), (_P_NORM + 8, 8), (_P_NORM + 16, 8), (_P_NORM + 24, 8), (_P_BG, 4))):
            for q in range(chunks):
                def pick_q(ref, q=q):
                    return ref[:, LANES * q:LANES * (q + 1)]

                def put_q(ref, val, q=q):
                    ref[:, LANES * q:LANES * (q + 1)] = val

                step(2 + k, tot[base + q:base + q + 1], pick_q, put_q)
        step(7, tot[_P_BSP:_P_BSP + SGU_GROUPS], whole, put_whole)
        for k in range(3):
            mine = tot[pl.ds(pl.multiple_of(_P_HEAD + 32 * k + 8 * chip, 8), 8), :]
            step(8 + k, mine[0:4, 0:64], whole, put_whole)

    shapes = [_sds(w.shape, F32) for w in ws]
    res = pl.pallas_call(
        body, name="small_adamw", in_specs=[_whole()] * (1 + 3 * n), out_specs=[_whole()] * (1 + 4 * n),
        out_shape=[_sds((1, 1), F32)] + shapes * 4,
        compiler_params=pltpu.CompilerParams(vmem_limit_bytes=VMEM_LIMIT_BYTES),
    )(tot, *ws, *ms, *vs)
    return res[0].reshape(()), [list(res[1 + i * n:1 + (i + 1) * n]) for i in range(4)]


def _w_in_pieces():
    blk = D_IN // N_CHIPS
    pieces = []
    for s in range(len(_IN_SPLITS)):
        lo_s, hi_s = _IN_STARTS[s], _IN_STARTS[s + 1]
        for j in range(N_CHIPS):
            lo, hi = max(lo_s, j * blk), min(hi_s, (j + 1) * blk)
            if lo < hi:
                pieces.append((j, lo - j * blk, _IN_DST[s] + lo - lo_s, hi - lo))
    return pieces


def _relayout_w_in(gathered):
    _, rows, blk = gathered.shape
    tr = 256

    def body(g_ref, o_ref):
        o_ref[:, OFF_AL:N_ALL] = jnp.zeros((tr, LANES), BF16)
        for j, src, dst, w in _w_in_pieces():
            o_ref[:, dst:dst + w] = g_ref[j, :, src:src + w]

    res, _ = _call(body, name="relayout_w_in", grid=(rows // tr,), parallel=True,
                   in_specs=[pl.BlockSpec((N_CHIPS, tr, blk), lambda i: (0, i, 0))],
                   out_specs=[pl.BlockSpec((tr, N_ALL), lambda i: (i, 0))],
                   out_shape=[_sds((rows, N_ALL), BF16)], args=(gathered,), fuse_inputs=True)
    return res[0]


def _update_row_tile(rows):
    for t in range(min(rows, 256), 7, -8):
        if rows % t == 0:
            return t
    return rows


def _presum_w_in(dws, theirs, row0, rows, name, job=None):
    hr = theirs[0].shape[0]
    blk = D_IN // N_CHIPS
    tr = 64
    assert row0 % tr == 0 and rows % tr == 0
    nh, t0 = hr // tr, row0 // tr
    n = len(dws)

    def body(core_ref, *refs):
        dw_refs, q_refs, (o_ref, s_scr) = refs[:n], refs[n:2 * n], refs[2 * n:]
        for p, (a, off) in enumerate(dws):
            w = a.shape[1]
            s_scr[:, off:off + w] = (dw_refs[p][...] + q_refs[p][...]).astype(BF16)
        for j, src, dst, w in _w_in_pieces():
            o_ref[j, :, src:src + w] = s_scr[:, dst:dst + w]

    in_specs = [pl.BlockSpec((tr, a.shape[1]), lambda i, core: (i + t0 + core[0] * nh, 0)) for a, _ in dws]
    in_specs += [pl.BlockSpec((tr, q.shape[1]), lambda i, core: (i + t0, 0)) for q in theirs]
    res, jres = _call(body, name=name, grid=(rows // tr,), parallel=True, in_specs=in_specs,
                      out_specs=[pl.BlockSpec((N_CHIPS, tr, blk), lambda i, core: (0, i, 0))],
                      out_shape=[_sds((N_CHIPS, rows, blk), BF16)], scratch_shapes=[pltpu.VMEM((tr, N_ALL), BF16)],
                      args=(*[a for a, _ in dws], *theirs), job=job, by_core=True)
    return res[0], jres


def _presum(dws, theirs, name):
    dw, n = dws[0], len(dws)
    assert all(d.shape == dw.shape for d in dws)
    if dw.ndim == 4:
        _, _, hr, c = dw.shape
        tr = _update_row_tile(hr)
        mine = pl.BlockSpec((1, 1, tr, c), lambda j, i, core: (j, core[0], i, 0))
        other = pl.BlockSpec((1, tr, c), lambda j, i, core: (j, i, 0))
    else:
        hr, c = dw.shape[0] // 2, dw.shape[1] // N_CHIPS
        tr = _update_row_tile(hr)
        nh = hr // tr
        mine = pl.BlockSpec((tr, c), lambda j, i, core: (i + core[0] * nh, j))
        other = pl.BlockSpec((tr, c), lambda j, i, core: (i, j))

    def body(core_ref, *refs):
        for a_ref, q_ref, o_ref in zip(refs[:n], refs[n:2 * n], refs[2 * n:]):
            o_ref[...] = (a_ref[...].reshape(tr, c) + q_ref[...].reshape(tr, c)).astype(BF16).reshape(o_ref.shape)

    res, _ = _call(body, name=name, grid=(N_CHIPS, hr // tr), parallel=True, in_specs=[mine] * n + [other] * n,
                   out_specs=[pl.BlockSpec((1, tr, c), lambda j, i, core: (j, i, 0))] * n,
                   out_shape=[_sds((N_CHIPS, hr, c), BF16)] * n, args=(*dws, *theirs), by_core=True)
    return res


def _sum_slots(sums, slots, name):
    n = len(sums)
    _, rows, cols = sums[0].shape
    assert all(s.shape == sums[0].shape for s in sums)
    tr = _update_row_tile(rows)

    def body(chip_ref, *refs):
        for own_ref, s_ref, o_ref in zip(refs[:n], refs[n:2 * n], refs[2 * n:]):
            acc = own_ref[...].astype(F32)
            for j in range(N_PEER):
                acc = acc + s_ref[j].astype(F32)
            o_ref[...] = acc

    chip = (2 * lax.axis_index("x") + lax.axis_index("y")).astype(jnp.int32).reshape(1)
    spec = pltpu.PrefetchScalarGridSpec(
        num_scalar_prefetch=1, grid=(rows // tr,),
        in_specs=[pl.BlockSpec((None, tr, cols), lambda i, chip: (chip[0], i, 0))] * n
        + [pl.BlockSpec((N_PEER, tr, cols), lambda i, chip: (0, i, 0))] * n,
        out_specs=[pl.BlockSpec((tr, cols), lambda i, chip: (i, 0))] * n)
    return list(pl.pallas_call(
        body, name=name, grid_spec=spec, out_shape=[_sds((rows, cols), F32)] * n,
        compiler_params=pltpu.CompilerParams(dimension_semantics=("parallel",), vmem_limit_bytes=VMEM_LIMIT_BYTES),
    )(chip, *sums, *slots))


def _adamw(ws, ms, vs, g_mine, g_theirs, name, job=None):
    n = len(ws)
    rows, cols = ws[0].shape
    part_rows = [p.shape[0] for p in g_mine[0]]
    assert all(w.shape == (rows, cols) for w in ws) and sum(part_rows) == rows // 2
    assert all([p.shape[0] for p in parts] == part_rows for parts in (*g_mine, *g_theirs))
    tr = _update_row_tile(min(part_rows))
    assert all(r % tr == 0 for r in part_rows)
    nh = (rows // 2) // tr
    starts = [sum(part_rows[:k]) // tr for k in range(len(part_rows))]
    n_parts = len(part_rows)

    def body(core_ref, *refs):
        ins, outs = refs[:(3 + 2 * n_parts) * n], refs[(3 + 2 * n_parts) * n:]
        step = pl.program_id(0)
        mine_here = (step // nh) == core_ref[0]
        q = step % nh
        for a in range(n):
            g_refs = ins[3 * n + 2 * n_parts * a:3 * n + 2 * n_parts * (a + 1)]
            g = None
            for k in reversed(range(n_parts)):
                val = jnp.where(mine_here, g_refs[k][...], g_refs[n_parts + k][...])
                g = val if g is None else jnp.where(q < starts[k + 1], val, g)
            d, m2, v2 = _adam_values(ins[a][...], ins[n + a][...], ins[2 * n + a][...], g)
            g_out, d_out, m_out, v_out = outs[4 * a:4 * a + 4]
            g_out[...] = g
            m_out[...] = m2
            v_out[...] = v2
            d_out[...] = d

    def g_spec(k, mine):
        last = part_rows[k] // tr - 1

        def index(i, core):
            half = core[0] if mine else 1 - core[0]
            here = jnp.clip(i % nh - starts[k], 0, last)
            return (jnp.where(i // nh == half, here, jnp.where(i // nh > half, last, 0)), 0)

        return pl.BlockSpec((tr, cols), index)

    spec = pl.BlockSpec((tr, cols), lambda i, core: (i, 0))
    g_specs = [g_spec(k, True) for k in range(n_parts)] + [g_spec(k, False) for k in range(n_parts)]
    g_args = [p for a in range(n) for p in (*g_mine[a], *g_theirs[a])]
    res, jres = _call(body, name=name, grid=(rows // tr,), parallel=True, in_specs=[spec] * (3 * n) + g_specs * n,
                      out_specs=[spec] * (4 * n), out_shape=[_sds((rows, cols), F32)] * (4 * n),
                      args=(*ws, *ms, *vs, *g_args), job=job, by_core=True)
    return [tuple(res[4 * a:4 * a + 4]) for a in range(n)], jres


def _transposed_cast(wt):
    cols, rows = wt.shape
    tile = 4 * LANES

    def body(x_ref, o_ref):
        o_ref[...] = jnp.transpose(x_ref[...]).astype(BF16)

    res, _ = _call(body, name="transpose_w_in", grid=(pl.cdiv(cols, tile),), parallel=True,
                   in_specs=[pl.BlockSpec((tile, rows), lambda j: (j, 0))],
                   out_specs=[pl.BlockSpec((rows, tile), lambda j: (0, j))], out_shape=[_sds((rows, cols), BF16)],
                   args=(wt,))
    return res[0]


def _cast_weights(ws, w_fi, job=None):
    steps = 4
    cols = w_fi.shape[1]
    tile = w_fi.shape[0] // (2 * steps)

    def body(*refs):
        for i_ref, o_ref in zip(refs[:len(refs) // 2], refs[len(refs) // 2:]):
            o_ref[...] = i_ref[...].astype(BF16)

    row_specs = [pl.BlockSpec((w.shape[0] // steps, w.shape[1]), lambda i: (i, 0)) for w in ws]
    half_spec = pl.BlockSpec((tile, cols), lambda i: (i, 0))
    return _call(body, name="cast_weights", grid=(steps,), parallel=True,
                 in_specs=row_specs + [pl.BlockSpec((None, tile, cols), lambda i, k=k: (k, i, 0)) for k in range(2)],
                 out_specs=row_specs + [half_spec, half_spec],
                 out_shape=[_sds(w.shape, BF16) for w in ws] + [_sds((steps * tile, cols), BF16)] * 2,
                 args=(*ws, w_fi.reshape(2, steps * tile, cols), w_fi.reshape(2, steps * tile, cols)), job=job)


def _adamw_transposed(wt, mt, vt, g_mine, g_theirs, name):
    cols, rows = wt.shape
    n_parts = len(g_mine)

    def body(w_ref, m_ref, v_ref, *rest):
        g_refs, (g_out, d_out, m_out, v_out) = rest[:-4], rest[-4:]
        mine = jnp.concatenate([r[...] for r in g_refs[:n_parts]], axis=0)
        theirs = jnp.concatenate([r[...] for r in g_refs[n_parts:]], axis=0)
        first = lax.axis_index("c") == 0
        g = jnp.transpose(jnp.concatenate([jnp.where(first, mine, theirs), jnp.where(first, theirs, mine)], axis=0))
        d, m2, v2 = _adam_values(w_ref[...], m_ref[...], v_ref[...], g)
        g_out[...] = g
        m_out[...] = m2
        v_out[...] = v2
        d_out[...] = d

    spec = pl.BlockSpec((LANES, rows), lambda j: (j, 0))
    g_specs = [pl.BlockSpec((p.shape[0], LANES), lambda j: (0, j)) for p in g_mine] * 2
    res, _ = _call(body, name=name, grid=(pl.cdiv(cols, LANES),), parallel=True, in_specs=[spec] * 3 + g_specs,
                   out_specs=[spec] * 4, out_shape=[_sds((cols, rows), F32)] * 4, args=(wt, mt, vt, *g_mine, *g_theirs))
    return res


def _inproj_fwd(x, g1, w_all, job=None):
    T = x.shape[0]
    tT = _row_tile(T, 512)

    def body(x_ref, g_ref, w_ref, a_ref, proj_ref, alow_ref):
        xv = x_ref[...]
        a = (xv * _rms_stats(xv) * g_ref[...]).astype(BF16)
        a_ref[...] = a
        for j in range(N_MAIN // 1024):
            cols = slice(j * 1024, (j + 1) * 1024)
            proj_ref[:, cols] = _dot(a, w_ref[:, cols]).astype(BF16)
        alow_ref[...] = _dot(a, w_ref[:, N_MAIN:N_ALL])

    row = lambda w: pl.BlockSpec((tT, w), lambda i: (i, 0))
    return _call(
        body, name="inproj_fwd", grid=(T // tT,), parallel=True,
        in_specs=[row(D_MODEL), pl.BlockSpec((1, D_MODEL), lambda i: (0, 0)), _whole()],
        out_specs=[row(D_MODEL), row(N_MAIN), row(LANES)],
        out_shape=[_sds((T, D_MODEL), BF16), _sds((T, N_MAIN), BF16), _sds((T, LANES), F32)],
        args=(x, g1, w_all), job=job)


def _gla_decay_terms(al_ref, wgu_ref, bg_ref, later_ref):
    logit = _dot_bf16(al_ref[...], wgu_ref[...]) + bg_ref[...]
    la = _log_sigmoid(logit) * (1.0 / GLA_TAU)
    delta = _dot_exact_lhs(later_ref[...], la)
    return logit, la, delta


def _gla_fwd(proj, alow, wgu, b_gate, gn, job=None):
    T = proj.shape[0]
    tT = _row_tile(T, 512)
    nc = tT // CHUNK

    def body(q_ref, k_ref, v_ref, r_ref, al_ref, wgu_ref, bg_ref, gn_ref, later_ref, y_ref, st_ref, s_scr):
        @pl.when(pl.program_id(0) == 0)
        def _():
            s_scr[...] = jnp.zeros_like(s_scr)

        _, la, delta = _gla_decay_terms(al_ref, wgu_ref, bg_ref, later_ref)
        kdec = (k_ref[...].astype(F32) * jnp.exp(delta)).astype(BF16)
        heads = range(GLA_HEADS)
        kcs = [slice(h * GLA_DK, (h + 1) * GLA_DK) for h in heads]
        vcs = [slice(h * GLA_DV, (h + 1) * GLA_DV) for h in heads]
        state = [s_scr[h] for h in heads]
        for c in range(nc):
            rows = slice(c * CHUNK, (c + 1) * CHUNK)
            first = slice(c * CHUNK, c * CHUNK + 1)
            dec = jnp.exp(la[first, :] + delta[first, :])
            upd_t = [_dot(v_ref[rows, vcs[h]], kdec[rows, kcs[h]], _TN) for h in heads]
            qs = [(q_ref[rows, kcs[h]].astype(F32) * (GLA_DK ** -0.5)).astype(BF16) for h in heads]
            for h in heads:
                state[h] = state[h] * dec[:, kcs[h]] + upd_t[h]
                st_ref[c, h] = state[h]
            o = [_dot(qs[h], state[h].astype(BF16), _NT) for h in heads]
            for h in heads:
                on = o[h] * _rms_stats(o[h]) * gn_ref[:, vcs[h]]
                rr = r_ref[rows, vcs[h]].astype(F32)
                y_ref[rows, vcs[h]] = (on * (rr * _sigmoid(rr))).astype(BF16)
        for h in heads:
            s_scr[h] = state[h]

    blk = lambda w, j: pl.BlockSpec((tT, w), lambda i: (i, j))
    return _call(
        body, name="gla_fwd", grid=(T // tT,),
        in_specs=[blk(512, 0), blk(512, 1), blk(1024, 1), blk(1024, 2), blk(LANES, 0)] + [_whole()] * 4,
        out_specs=[pl.BlockSpec((tT, GLA_V), lambda i: (i, 0)),
                   pl.BlockSpec((nc, GLA_HEADS, GLA_DV, GLA_DK), lambda i: (i, 0, 0, 0))],
        out_shape=[_sds((T, GLA_V), BF16), _sds((T // CHUNK, GLA_HEADS, GLA_DV, GLA_DK), F32)],
        scratch_shapes=[pltpu.VMEM((GLA_HEADS, GLA_DV, GLA_DK), F32)],
        args=(proj, proj, proj, proj, alow, wgu, b_gate, gn, _chunk_masks(tT, upper=True)), job=job)


def _sgu_mask():
    i = lax.broadcasted_iota(jnp.int32, (SGU_BLOCK, SGU_BLOCK), 0)
    j = lax.broadcasted_iota(jnp.int32, (SGU_BLOCK, SGU_BLOCK), 1)
    return lax.shift_right_logical(j, 6) <= lax.shift_right_logical(i, 6)


def _sgu_merge_fwd(x, proj, y_gla, ln_g, ln_b, w_sp, b_sp_t, w_bg, w_bs, w_o, g_pm, job=None):
    T = x.shape[0]
    tT = _row_tile(T, 512)
    nb = tT // SGU_BLOCK

    def body(x_ref, su_ref, sv_ref, gg_ref, gs_ref, yg_ref, lg_ref, lb_ref, w_ref, b_ref, wbg_ref, wbs_ref, wo_ref,
             g_ref, ys_ref, zg_ref, zs_ref, mg_ref, mix_ref, x1_ref):
        mask = _sgu_mask()
        for g in range(SGU_GROUPS):
            gc = slice(g * SGU_DG, (g + 1) * SGU_DG)
            wm = jnp.where(mask, w_ref[g], 0.0).astype(BF16)
            vf = _gelu(sv_ref[:, gc].astype(F32))
            mu = jnp.mean(vf, axis=-1, keepdims=True)
            vc = vf - mu
            rstd = lax.rsqrt(jnp.mean(vc * vc, axis=-1, keepdims=True) + EPS)
            vn = (vc * rstd * lg_ref[:, gc] + lb_ref[:, gc]).astype(BF16)
            u = _gelu(su_ref[:, gc].astype(F32))
            for b in range(nb):
                rows = slice(b * SGU_BLOCK, (b + 1) * SGU_BLOCK)
                mixed = _dot(wm, vn[rows, :]) + b_ref[:, g:g + 1]
                ys_ref[rows, gc] = (u[rows, :] * mixed).astype(BF16)
        zg = _dot(yg_ref[...], wbg_ref[...])
        zs = _dot(ys_ref[...], wbs_ref[...])
        zg_ref[...] = zg.astype(BF16)
        zs_ref[...] = zs.astype(BF16)
        merged = (_sigmoid(gg_ref[...].astype(F32)) * zg + _sigmoid(gs_ref[...].astype(F32)) * zs).astype(BF16)
        mg_ref[...] = merged
        mix = _dot(merged, wo_ref[...])
        mix_ref[...] = mix.astype(BF16)
        x1_ref[...] = x_ref[...] + mix * _rms_stats(mix) * g_ref[...]

    row = pl.BlockSpec((tT, D_MODEL), lambda i: (i, 0))
    blk = lambda j: pl.BlockSpec((tT, 1024), lambda i: (i, j))
    sds = lambda dt: _sds((T, D_MODEL), dt)
    return _call(body, name="sgu_merge_fwd", grid=(T // tT,), parallel=True,
                 in_specs=[row, blk(3), blk(4), blk(5), blk(6), row] + [_whole()] * 7
                 + [pl.BlockSpec((1, D_MODEL), lambda i: (0, 0))],
                 out_specs=[row] * 6, out_shape=[sds(BF16)] * 5 + [sds(F32)],
                 args=(x, proj, proj, proj, proj, y_gla, ln_g, ln_b, w_sp, b_sp_t, w_bg, w_bs, w_o, g_pm), job=job)


def _ffn_fwd_bwd(x1, tgt, w_fi_top, w_fi_bot, w_fo, g_pf, g_po):
    T = x1.shape[0]
    tT = _row_tile(T, 256)
    half = D_FF // 2
    kh = D_MODEL // 2

    def body(x1_ref, t_ref, top_ref, bot_ref, wfo_ref, gpf_ref, gpo_ref,
             h_ref, f_ref, dgu_ref, dy_ref, dx1_ref, loss_ref, dgpf_ref, dgpo_ref, gu_scr):
        @pl.when(pl.program_id(0) == 0)
        def _():
            loss_ref[...] = jnp.zeros_like(loss_ref)
            dgpf_ref[...] = jnp.zeros_like(dgpf_ref)
            dgpo_ref[...] = jnp.zeros_like(dgpo_ref)

        main = (half // 256) * 256
        pieces = (0, 1, None)

        def w_in_cols(ref, first_slab, p):
            if p is not None:
                return ref[first_slab + p, :, :main]
            return jnp.concatenate([ref[first_slab, :, main:], ref[first_slab + 1, :, main:]], axis=1)

        def w_out_rows(p):
            if p is not None:
                return wfo_ref[p * half:p * half + main, :]
            return jnp.concatenate([wfo_ref[main:half, :], wfo_ref[half + main:2 * half, :]], axis=0)

        def put(ref, base, p, val):
            if p is not None:
                ref[:, base + p * half:base + p * half + main] = val
            else:
                ref[:, base + main:base + half] = val[:, :half - main]
                ref[:, base + half + main:base + 2 * half] = val[:, half - main:]

        def get(ref, base, p):
            if p is not None:
                return ref[:, base + p * half:base + p * half + main]
            return jnp.concatenate([ref[:, base + main:base + half], ref[:, base + half + main:base + 2 * half]], axis=1)

        x1v = x1_ref[...]
        r2 = _rms_stats(x1v)
        h = (x1v * r2 * gpf_ref[...]).astype(BF16)
        h_ref[...] = h
        y = jnp.zeros((tT, D_MODEL), F32)
        for p in pieces:
            gate = _dot(h[:, :kh], w_in_cols(top_ref, 0, p)) + _dot(h[:, kh:], w_in_cols(bot_ref, 0, p))
            up = _dot(h[:, :kh], w_in_cols(top_ref, 2, p)) + _dot(h[:, kh:], w_in_cols(bot_ref, 2, p))
            put(gu_scr, 0, p, gate)
            put(gu_scr, D_FF, p, up)
            f = (gate * _sigmoid(gate) * up).astype(BF16)
            put(f_ref, 0, p, f)
            y = y + _dot(f, w_out_rows(p))
        r3 = _rms_stats(y)
        x2 = x1v + y * r3 * gpo_ref[...]
        err = x2 - t_ref[...]
        loss_ref[...] += jnp.sum(err * err) * (0.5 / D_MODEL)
        dx2 = err * (1.0 / D_MODEL)
        dy, dg = _rms_bwd(dx2, y, r3, gpo_ref[...])
        dgpo_ref[...] += jnp.sum(dg, axis=0, keepdims=True)
        dyb = dy.astype(BF16)
        dy_ref[...] = dyb
        dh_top = jnp.zeros((tT, kh), F32)
        dh_bot = jnp.zeros((tT, kh), F32)
        for p in pieces:
            df = _dot(dyb, w_out_rows(p), _NT)
            gate = get(gu_scr, 0, p)
            up = get(gu_scr, D_FF, p)
            sg = _sigmoid(gate)
            dgate = (df * up * (sg * (1.0 + gate * (1.0 - sg)))).astype(BF16)
            dup = (df * (gate * sg)).astype(BF16)
            put(dgu_ref, 0, p, dgate)
            put(dgu_ref, D_FF, p, dup)
            dh_top = dh_top + _dot(dgate, w_in_cols(top_ref, 0, p), _NT) + _dot(dup, w_in_cols(top_ref, 2, p), _NT)
            dh_bot = dh_bot + _dot(dgate, w_in_cols(bot_ref, 0, p), _NT) + _dot(dup, w_in_cols(bot_ref, 2, p), _NT)
        dh = jnp.concatenate([dh_top, dh_bot], axis=1)
        dx1n, dg2 = _rms_bwd(dh, x1v, r2, gpf_ref[...])
        dgpf_ref[...] += jnp.sum(dg2, axis=0, keepdims=True)
        dx1_ref[...] = dx2 + dx1n

    row = lambda w: pl.BlockSpec((tT, w), lambda i: (i, 0))
    vec = pl.BlockSpec((1, D_MODEL), lambda i: (0, 0))
    res, _ = _call(
        body, name="ffn_fwd_bwd", grid=(T // tT,),
        in_specs=[row(D_MODEL), row(D_MODEL), _whole(), _whole(), _whole(), vec, vec],
        out_specs=[row(D_MODEL), row(D_FF), row(2 * D_FF), row(D_MODEL), row(D_MODEL),
                   pl.BlockSpec((1, LANES), lambda i: (0, 0)), vec, vec],
        out_shape=[_sds((T, D_MODEL), BF16), _sds((T, D_FF), BF16), _sds((T, 2 * D_FF), BF16), _sds((T, D_MODEL), BF16),
                   _sds((T, D_MODEL), F32), _sds((1, LANES), F32), _sds((1, D_MODEL), F32), _sds((1, D_MODEL), F32)],
        scratch_shapes=[pltpu.VMEM((tT, 2 * D_FF), F32)], args=(x1, tgt, w_fi_top, w_fi_bot, w_fo, g_pf, g_po))
    return res


def _merge_sgu_bwd(dx1, mix, proj, zg, zs, w_bg, w_bs, w_o, g_pm, ln_g, ln_b, w_sp, b_sp_t, job=None):
    T = dx1.shape[0]
    tT = _row_tile(T, 256)
    nb = tT // SGU_BLOCK

    def body(dx1_ref, mix_ref, su_ref, sv_ref, gg_ref, gs_ref, zg_ref, zs_ref, wbg_ref, wbs_ref, wo_ref, g_ref,
             lg_ref, lb_ref, w_ref, b_ref,
             dmix_ref, dzg_ref, dzs_ref, dgate_ref, dyg_ref, dp_ref, dgpm_ref, dw_ref, dbt_ref, dlg_ref, dlb_ref):
        @pl.when(pl.program_id(0) == 0)
        def _():
            for ref in (dgpm_ref, dw_ref, dbt_ref, dlg_ref, dlb_ref):
                ref[...] = jnp.zeros_like(ref)

        mix = mix_ref[...].astype(F32)
        dmix, dg = _rms_bwd(dx1_ref[...], mix, _rms_stats(mix), g_ref[...])
        dgpm_ref[...] += jnp.sum(dg, axis=0, keepdims=True)
        dmb = dmix.astype(BF16)
        dmix_ref[...] = dmb
        dmerged = _dot(dmb, wo_ref[...], _NT)
        dys = None
        for k, (gate_ref, z_ref, w_br_ref, dz_ref) in enumerate(((gg_ref, zg_ref, wbg_ref, dzg_ref),
                                                                 (gs_ref, zs_ref, wbs_ref, dzs_ref))):
            sg = _sigmoid(gate_ref[...].astype(F32))
            dz = (dmerged * sg).astype(BF16)
            dz_ref[...] = dz
            dgate_ref[:, k * 1024:(k + 1) * 1024] = (dmerged * z_ref[...].astype(F32) * (sg * (1.0 - sg))).astype(BF16)
            dy_branch = _dot(dz, w_br_ref[...], _NT)
            if k == 0:
                dyg_ref[...] = dy_branch.astype(BF16)
            else:
                dys = dy_branch

        mask = _sgu_mask()
        lane = lax.broadcasted_iota(jnp.int32, (SGU_BLOCK, LANES), 1)
        for g in range(SGU_GROUPS):
            gc = slice(g * SGU_DG, (g + 1) * SGU_DG)
            gc_v = slice(1024 + g * SGU_DG, 1024 + (g + 1) * SGU_DG)
            wm = jnp.where(mask, w_ref[g], 0.0).astype(BF16)
            vf, dvf_dsv = _gelu_and_grad(sv_ref[:, gc].astype(F32))
            mu = jnp.mean(vf, axis=-1, keepdims=True)
            vc = vf - mu
            rstd = lax.rsqrt(jnp.mean(vc * vc, axis=-1, keepdims=True) + EPS)
            vhat = vc * rstd
            vn = (vhat * lg_ref[:, gc] + lb_ref[:, gc]).astype(BF16)
            u, du_dsu = _gelu_and_grad(su_ref[:, gc].astype(F32))
            dy = dys[:, gc]
            dmixed = (dy * u).astype(BF16)
            dvn_parts = []
            dw_acc = jnp.zeros((SGU_BLOCK, SGU_BLOCK), F32)
            db_acc = jnp.zeros((SGU_BLOCK, 1), F32)
            for b in range(nb):
                rows = slice(b * SGU_BLOCK, (b + 1) * SGU_BLOCK)
                mixed = _dot(wm, vn[rows, :]) + b_ref[:, g:g + 1]
                dp_ref[rows, gc] = (dy[rows, :] * mixed * du_dsu[rows, :]).astype(BF16)
                dvn_parts.append(_dot(wm, dmixed[rows, :], _TN))
                dw_acc = dw_acc + _dot(dmixed[rows, :], vn[rows, :], _NT)
                db_acc = db_acc + jnp.sum(dmixed[rows, :].astype(F32), axis=-1, keepdims=True)
            dw_ref[g] += jnp.where(mask, dw_acc, 0.0)
            dbt_ref[...] += jnp.where(lane == g, db_acc, 0.0)
            dvn = jnp.concatenate(dvn_parts, axis=0)
            dlg_ref[:, gc] += jnp.sum(dvn * vhat, axis=0, keepdims=True)
            dlb_ref[:, gc] += jnp.sum(dvn, axis=0, keepdims=True)
            dvh = dvn * lg_ref[:, gc]
            dvf = rstd * (dvh - jnp.mean(dvh, axis=-1, keepdims=True)
                          - vhat * jnp.mean(dvh * vhat, axis=-1, keepdims=True))
            dp_ref[:, gc_v] = (dvf * dvf_dsv).astype(BF16)

    row = pl.BlockSpec((tT, D_MODEL), lambda i: (i, 0))
    blk = lambda j: pl.BlockSpec((tT, 1024), lambda i: (i, j))
    vec = pl.BlockSpec((1, D_MODEL), lambda i: (0, 0))
    wide = lambda w: pl.BlockSpec((tT, w), lambda i: (i, 0))
    sds = _sds((T, D_MODEL), BF16)
    return _call(
        body, name="merge_sgu_bwd", grid=(T // tT,),
        in_specs=[row, row, blk(3), blk(4), blk(5), blk(6), row, row] + [_whole()] * 3 + [vec] + [_whole()] * 4,
        out_specs=[row, row, row, wide(W_MRG), row, wide(W_SGU), vec,
                   pl.BlockSpec((SGU_GROUPS, SGU_BLOCK, SGU_BLOCK), lambda i: (0, 0, 0)),
                   pl.BlockSpec((SGU_BLOCK, LANES), lambda i: (0, 0)), vec, vec],
        out_shape=[sds, sds, sds, _sds((T, W_MRG), BF16), sds, _sds((T, W_SGU), BF16), _sds((1, D_MODEL), F32),
                   _sds((SGU_GROUPS, SGU_BLOCK, SGU_BLOCK), F32), _sds((SGU_BLOCK, LANES), F32),
                   _sds((1, 1024), F32), _sds((1, 1024), F32)],
        args=(dx1, mix, proj, proj, proj, proj, zg, zs, w_bg, w_bs, w_o, g_pm, ln_g, ln_b, w_sp, b_sp_t), job=job)


def _gla_bwd(proj, alow, wgu, b_gate, gn, states, dy_gla, job=None):
    T = proj.shape[0]
    tT = _row_tile(T, 512)
    nc = tT // CHUNK
    nt = T // tT

    def body(q_ref, k_ref, v_ref, r_ref, al_ref, wgu_ref, bg_ref, gn_ref, later_ref, earlier_ref, st_ref, sp_ref, dy_ref,
             dp_ref, dal_ref, dgn_ref, dbg_ref, dwgu_ref, g_scr, dd_scr, dt_scr):
        step = pl.program_id(0)

        @pl.when(step == 0)
        def _():
            g_scr[...] = jnp.zeros_like(g_scr)
            dgn_ref[...] = jnp.zeros_like(dgn_ref)
            dbg_ref[...] = jnp.zeros_like(dbg_ref)
            dwgu_ref[...] = jnp.zeros_like(dwgu_ref)

        has_prev = jnp.where(step == nt - 1, 0.0, 1.0)
        logit, la, delta = _gla_decay_terms(al_ref, wgu_ref, bg_ref, later_ref)
        e = jnp.exp(delta)
        kdec_f = k_ref[...].astype(F32) * e
        kdec = kdec_f.astype(BF16)
        heads = range(GLA_HEADS)
        kcs = [slice(h * GLA_DK, (h + 1) * GLA_DK) for h in heads]
        vcs = [slice(h * GLA_DV, (h + 1) * GLA_DV) for h in heads]
        carry = [g_scr[h] for h in heads]
        dgn_acc = [jnp.zeros((1, GLA_DV), F32) for _ in heads]
        for c in reversed(range(nc)):
            rows = slice(c * CHUNK, (c + 1) * CHUNK)
            first = slice(c * CHUNK, c * CHUNK + 1)
            dec = jnp.exp(la[first, :] + delta[first, :])
            s_b = [st_ref[c, h].astype(BF16) for h in heads]
            qs = [(q_ref[rows, kcs[h]].astype(F32) * (GLA_DK ** -0.5)).astype(BF16) for h in heads]
            o = [_dot(qs[h], s_b[h], _NT) for h in heads]
            do = []
            for h in heads:
                rstd = _rms_stats(o[h])
                ohat = o[h] * rstd
                gnh = gn_ref[:, vcs[h]]
                dy = dy_ref[rows, vcs[h]].astype(F32)
                rr = r_ref[rows, vcs[h]].astype(F32)
                sg = _sigmoid(rr)
                don = dy * (rr * sg)
                dp_ref[rows, OFF_R + h * GLA_DV:OFF_R + (h + 1) * GLA_DV] = (
                    dy * (ohat * gnh) * (sg * (1.0 + rr * (1.0 - sg)))).astype(BF16)
                dgn_acc[h] = dgn_acc[h] + jnp.sum(don * ohat, axis=0, keepdims=True)
                dn = don * gnh
                do.append((rstd * (dn - ohat * jnp.mean(dn * ohat, axis=-1, keepdims=True))).astype(BF16))
            dq = [_dot(do[h], s_b[h]) for h in heads]
            g_t = [_dot(do[h], qs[h], _TN) + carry[h] for h in heads]
            g_b = [g_t[h].astype(BF16) for h in heads]
            dv = [_dot(kdec[rows, kcs[h]], g_b[h], _NT) for h in heads]
            dkdec = [_dot(v_ref[rows, vcs[h]], g_b[h]) for h in heads]
            for h in heads:
                s_prev = st_ref[c - 1, h] if c > 0 else sp_ref[0, h] * has_prev
                ddec = jnp.sum(g_t[h] * s_prev, axis=0, keepdims=True)
                carry[h] = g_t[h] * dec[:, kcs[h]]
                dp_ref[rows, OFF_Q + h * GLA_DK:OFF_Q + (h + 1) * GLA_DK] = (dq[h] * (GLA_DK ** -0.5)).astype(BF16)
                dp_ref[rows, OFF_V + h * GLA_DV:OFF_V + (h + 1) * GLA_DV] = dv[h].astype(BF16)
                dp_ref[rows, OFF_K + h * GLA_DK:OFF_K + (h + 1) * GLA_DK] = (dkdec[h] * e[rows, kcs[h]]).astype(BF16)
                dd_scr[rows, kcs[h]] = dkdec[h] * kdec_f[rows, kcs[h]]
                dt_scr[rows, kcs[h]] = jnp.broadcast_to(ddec * dec[:, kcs[h]], (CHUNK, GLA_DK))
        for h in heads:
            g_scr[h] = carry[h]
            dgn_ref[:, vcs[h]] += dgn_acc[h]
        dla = _dot_exact_lhs(earlier_ref[...], dd_scr[...]) + dt_scr[...]
        dlogit = dla * (1.0 / GLA_TAU) * _sigmoid(-logit)
        dbg_ref[...] += jnp.sum(dlogit, axis=0, keepdims=True)
        dwgu_ref[...] += _dot_bf16(al_ref[...], dlogit, _TN)
        dal_ref[...] = _dot_bf16(dlogit, wgu_ref[...], _NT).astype(BF16)

    rev = lambda i: nt - 1 - i
    blk = lambda w, j: pl.BlockSpec((tT, w), lambda i: (rev(i), j))
    st_blk = pl.BlockSpec((nc, GLA_HEADS, GLA_DV, GLA_DK), lambda i: (rev(i), 0, 0, 0))
    sp_blk = pl.BlockSpec((1, GLA_HEADS, GLA_DV, GLA_DK), lambda i: (jnp.maximum(rev(i) * nc - 1, 0), 0, 0, 0))
    return _call(
        body, name="gla_bwd", grid=(nt,),
        in_specs=[blk(512, 0), blk(512, 1), blk(1024, 1), blk(1024, 2), blk(LANES, 0)] + [_whole()] * 5
        + [st_blk, sp_blk, blk(GLA_V, 0)],
        out_specs=[blk(W_GLA, 0), blk(LANES, 0), pl.BlockSpec((1, GLA_V), lambda i: (0, 0)),
                   pl.BlockSpec((1, GLA_QK), lambda i: (0, 0)), pl.BlockSpec((LANES, GLA_QK), lambda i: (0, 0))],
        out_shape=[_sds((T, W_GLA), BF16), _sds((T, LANES), BF16), _sds((1, GLA_V), F32), _sds((1, GLA_QK), F32),
                   _sds((LANES, GLA_QK), F32)],
        scratch_shapes=[pltpu.VMEM((GLA_HEADS, GLA_DV, GLA_DK), F32), pltpu.VMEM((tT, GLA_QK), F32),
                        pltpu.VMEM((tT, GLA_QK), F32)],
        args=(proj, proj, proj, proj, alow, wgu, b_gate, gn, _chunk_masks(tT, upper=True), _chunk_masks(tT, upper=False),
              states, states, dy_gla), job=job)


def _inproj_bwd(x, dx1, g1, w_all, dparts, job=None):
    T = x.shape[0]
    tT = _row_tile(T, 512)
    offs = (0, W_GLA, W_GLA + W_SGU, N_MAIN)

    def body(x_ref, dx1_ref, g_ref, w_hbm, *rest):
        part_refs, (dx_ref, dg_ref, w_ref, w_sems) = rest[:len(offs)], rest[len(offs):]

        def compute(first):
            if first:
                copies = [pltpu.make_async_copy(w_hbm.at[:, pl.ds(off, p.shape[1])], w_ref.at[:, pl.ds(off, p.shape[1])],
                                                w_sems.at[k]) for k, (off, p) in enumerate(zip(offs, dparts))]
                for cp in copies:
                    cp.start()
            da = jnp.zeros((tT, D_MODEL), F32)
            for k, (off, p_ref) in enumerate(zip(offs, part_refs)):
                if first:
                    copies[k].wait()
                da = da + _dot(p_ref[...], w_ref[:, off:off + p_ref.shape[1]], _NT)
            xv = x_ref[...]
            dx, dg = _rms_bwd(da, xv, _rms_stats(xv), g_ref[...])
            dg_sum = jnp.sum(dg, axis=0, keepdims=True)
            dg_ref[...] = dg_sum if first else dg_ref[...] + dg_sum
            dx_ref[...] = dx1_ref[...] + dx

        first_step = pl.program_id(0) == 0
        pl.when(first_step)(lambda: compute(True))
        pl.when(jnp.logical_not(first_step))(lambda: compute(False))

    row = lambda w: pl.BlockSpec((tT, w), lambda i: (i, 0))
    vec = pl.BlockSpec((1, D_MODEL), lambda i: (0, 0))
    return _call(
        body, name="inproj_bwd", grid=(T // tT,),
        in_specs=[row(D_MODEL), row(D_MODEL), vec, pl.BlockSpec(memory_space=pl.ANY)] + [row(p.shape[1]) for p in dparts],
        out_specs=[row(D_MODEL), vec], out_shape=[_sds((T, D_MODEL), F32), _sds((1, D_MODEL), F32)],
        scratch_shapes=[pltpu.VMEM(w_all.shape, BF16), pltpu.SemaphoreType.DMA((len(offs),))],
        args=(x, dx1, g1, w_all, *dparts), job=job)


def _tn_matmul(a, b, name, job=None):
    T, M = a.shape
    N = b.shape[1]
    tk = _row_tile(T, 1024)
    tm = M if M <= 1024 else 1408
    tn = N // 2 if N > 2048 else N
    assert M % tm == 0 and N % tn == 0

    def body(a_ref, b_ref, o_ref):
        @pl.when(pl.program_id(2) == 0)
        def _():
            o_ref[...] = _dot(a_ref[...], b_ref[...], _TN)

        @pl.when(pl.program_id(2) > 0)
        def _():
            o_ref[...] += _dot(a_ref[...], b_ref[...], _TN)

    res, jres = _call(
        body, name=name, grid=(M // tm, N // tn, T // tk),
        in_specs=[pl.BlockSpec((tk, tm), lambda i, j, k: (k, i)), pl.BlockSpec((tk, tn), lambda i, j, k: (k, j))],
        out_specs=[pl.BlockSpec((tm, tn), lambda i, j, k: (i, j))], out_shape=[_sds((M, N), F32)], args=(a, b), job=job)
    return res[0], jres


def _pad_rows(a, rows=8):
    return jnp.pad(a, ((0, rows - a.shape[0]), (0, LANES - a.shape[1])))


def _halves_view(dw):
    r = dw.shape[0] // N_CHIPS
    return dw.reshape(N_CHIPS, 2, r // 2, dw.shape[1])


def kernel(x, norm_pre_mix, w_in, w_gate_up, b_gate, gla_norm, sgu_ln_g, sgu_ln_b, w_spatial, b_spatial, w_branch_gla, w_branch_sgu, w_out, norm_post_mix, norm_pre_ffn, w_ffn_in, w_ffn_out, norm_post_ffn, loss_target, m_norm_pre_mix, m_w_in, m_w_gate_up, m_b_gate, m_gla_norm, m_sgu_ln_g, m_sgu_ln_b, m_w_spatial, m_b_spatial, m_w_branch_gla, m_w_branch_sgu, m_w_out, m_norm_post_mix, m_norm_pre_ffn, m_w_ffn_in, m_w_ffn_out, m_norm_post_ffn, v_norm_pre_mix, v_w_in, v_w_gate_up, v_b_gate, v_gla_norm, v_sgu_ln_g, v_sgu_ln_b, v_w_spatial, v_b_spatial, v_w_branch_gla, v_w_branch_sgu, v_w_out, v_norm_post_mix, v_norm_pre_ffn, v_w_ffn_in, v_w_ffn_out, v_norm_post_ffn):
    chip = 2 * lax.axis_index("x") + lax.axis_index("y")
    xt, tgt = x[0], loss_target[0]

    tiny = jnp.concatenate([w_gate_up[0], _pad_rows(gla_norm[0]), _pad_rows(sgu_ln_g[0]), _pad_rows(sgu_ln_b[0]),
                            jnp.zeros((24, LANES), F32)], axis=0)

    def with_own(gathered, own):
        return lax.dynamic_update_slice(gathered, own[None], (chip, 0, 0))

    w_in_t, m_in_t, v_in_t = w_in[0].T, m_w_in[0].T, v_w_in[0].T
    w_in_b = _transposed_cast(w_in_t)
    (*own_rows, fi_top, fi_bot), (g_in, g_tiny) = _cast_weights(
        [w_branch_gla[0], w_branch_sgu[0], w_out[0], w_ffn_out[0]], w_ffn_in[0], job=_job_gather([w_in_b, tiny]))
    g_tiny = with_own(g_tiny, tiny)
    w_all = _relayout_w_in(with_own(g_in, w_in_b))
    cols = lambda a: a.transpose(1, 0, 2).reshape(a.shape[1], N_CHIPS * a.shape[2])
    wgu = jnp.pad(cols(g_tiny[:, 0:16]), ((0, LANES - GLA_RANK), (0, 0)))
    gn = cols(g_tiny[:, 16:20, :64]).reshape(1, GLA_V)
    ln_g = cols(g_tiny[:, 24:28, :64]).reshape(1, 1024)
    ln_b = cols(g_tiny[:, 32:36, :64]).reshape(1, 1024)
    b_sp_t = jnp.pad(b_spatial[0].T, ((0, 0), (0, LANES - SGU_GROUPS)))
    w_sp = w_spatial[0]

    (a, proj, alow), g_rows = _inproj_fwd(xt, norm_pre_mix, w_all, job=_job_gather(own_rows))
    m_in_t, alow = lax.optimization_barrier((m_in_t, alow))
    v_in_t, a = lax.optimization_barrier((v_in_t, a))
    rows = lambda g: g.reshape(N_CHIPS * g.shape[1], g.shape[2])
    w_bg, w_bs, w_o, w_fo = [rows(with_own(g, own)) for g, own in zip(g_rows, own_rows)]
    (y_gla, states), (g_top,) = _gla_fwd(proj, alow, wgu, b_gate, gn, job=_job_gather([fi_top]))
    (y_sgu, zg, zs, merged, mix, x1), (g_bot,) = _sgu_merge_fwd(
        xt, proj, y_gla, ln_g, ln_b, w_sp, b_sp_t, w_bg, w_bs, w_o, norm_post_mix, job=_job_gather([fi_bot]))
    h, f, dgu, dy, dx1, loss, d_gpf, d_gpo = _ffn_fwd_bwd(x1, tgt, with_own(g_top, fi_top), with_own(g_bot, fi_bot),
                                                          w_fo, norm_pre_ffn, norm_post_ffn)

    whole = lambda hs: [[(h_, None)] for h_ in hs]
    dw_fo, _ = _tn_matmul(f, dy, "dw_ffn_out")
    dw_fo4 = _halves_view(dw_fo)
    dw_fi, (q_fo,) = _tn_matmul(h, dgu, "dw_ffn_in", job=_job_to_other_core([[(dw_fo4, 0)]]))
    c_fo, = _presum([dw_fo4], [q_fo], "presum_ffn_out")
    (dmix, dzg, dzs, dp_mrg, dyg, dp_sgu, d_gpm, d_wsp, d_bsp_t, d_lng, d_lnb), (s_fo, q_fi) = _merge_sgu_bwd(
        dx1, mix, proj, zg, zs, w_bg, w_bs, w_o, norm_post_mix, ln_g, ln_b, w_sp, b_sp_t,
        job=_join(_job_scatter([c_fo]), _job_to_other_core([[(dw_fi, 0)]])))
    c_fi, = _presum([dw_fi], [q_fi], "presum_ffn_in")
    dw_c, _ = _tn_matmul(a, dp_mrg, "dw_in_merge")
    dw_b, _ = _tn_matmul(a, dp_sgu, "dw_in_sgu")
    dw_o4 = _halves_view(_tn_matmul(merged, dmix, "dw_out")[0])
    dw_bg4 = _halves_view(_tn_matmul(y_gla, dzg, "dw_branch_gla")[0])
    dw_bs4 = _halves_view(_tn_matmul(y_sgu, dzs, "dw_branch_sgu")[0])
    h_fo, = _sum_slots([c_fo], [s_fo], "sum_ffn_out")
    (dp_gla, dal, d_gn, d_bg, d_wgu), (s_fi, t_fo, q_b, q_c, q_o, q_bg, q_bs) = _gla_bwd(
        proj, alow, wgu, b_gate, gn, states, dyg,
        job=_join(_job_scatter([c_fi]), _job_to_other_core(
            whole([h_fo]) + [[(dw_b, 0)], [(dw_c, 0)], [(dw_o4, 0)], [(dw_bg4, 0)], [(dw_bs4, 0)]])))
    c_o, c_bg, c_bs = _presum([dw_o4, dw_bg4, dw_bs4], [q_o, q_bg, q_bs], "presum_out_branches")
    h_fi, = _sum_slots([c_fi], [s_fi], "sum_ffn_in")
    dw_d, _ = _tn_matmul(a, dal, "dw_in_gate")
    dw_a, (s_o, s_bg, s_bs, t_fi, q_d) = _tn_matmul(
        a, dp_gla, "dw_in_gla",
        job=_join(_job_scatter([c_o, c_bg, c_bs]), _job_to_other_core(whole([h_fi]) + [[(dw_d, 0)]])))
    h_o, h_bg, h_bs = _sum_slots([c_o, c_bg, c_bs], [s_o, s_bg, s_bs], "sum_out_branches")

    grads, deltas, new_m, new_v = {}, {}, {}, {}

    def update(call, names, ws, ms, vs, g_mine, g_theirs, job=None):
        res, jres = _adamw([w[0] for w in ws], [m[0] for m in ms], [v[0] for v in vs], g_mine, g_theirs, call, job=job)
        for name, (g, d, m2, v2) in zip(names, res):
            grads[name], deltas[name], new_m[name], new_v[name] = g[None], d[None], m2[None], v2[None]
        return jres

    dw_in = [(dw_a, 0), (dw_b, W_GLA), (dw_c, W_GLA + W_SGU), (dw_d, N_MAIN)]
    q_a, t_o, t_bg, t_bs = update("adamw_w_ffn_out", ["w_ffn_out"], [w_ffn_out], [m_w_ffn_out], [v_w_ffn_out],
                                  [[h_fo]], [[t_fo]],
                                  job=_job_to_other_core([[(dw_a, 0)]] + whole([h_o, h_bg, h_bs])))
    q_in = [q_a, q_b, q_c, q_d]
    hr_in = D_MODEL // 2
    c_in_a, _ = _presum_w_in(dw_in, q_in, 0, hr_in // 8, "presum_w_in_a")
    c_in_b, (s_in_a,) = _presum_w_in(dw_in, q_in, hr_in // 8, 7 * hr_in // 8, "presum_w_in_b",
                                     job=_job_scatter([c_in_a]))
    update("adamw_w_ffn_in", ["w_ffn_in"], [w_ffn_in], [m_w_ffn_in], [v_w_ffn_in], [[h_fi]], [[t_fi]])
    update("adamw_out_branches", ["w_out", "w_branch_gla", "w_branch_sgu"], [w_out, w_branch_gla, w_branch_sgu],
           [m_w_out, m_w_branch_gla, m_w_branch_sgu], [v_w_out, v_w_branch_gla, v_w_branch_sgu],
           [[h_o], [h_bg], [h_bs]], [[t_o], [t_bg], [t_bs]])
    (grad_x, d_g1), (s_in_b,) = _inproj_bwd(xt, dx1, norm_pre_mix, w_all, (dp_gla, dp_sgu, dp_mrg, dal),
                                            job=_job_scatter([c_in_b]))
    h_in = _sum_slots([c_in_a], [s_in_a], "sum_w_in_a") + _sum_slots([c_in_b], [s_in_b], "sum_w_in_b")
    t_in = _run_job(_job_to_other_core(whole(h_in)), "swap_w_in")
    for store, val in zip((grads, deltas, new_m, new_v),
                          _adamw_transposed(w_in_t, m_in_t, v_in_t, h_in, t_in, "adamw_w_in")):
        store["w_in"] = val.T[None]

    small_names = ["w_spatial", "w_gate_up", "norm_pre_mix", "norm_post_mix", "norm_pre_ffn", "norm_post_ffn", "b_gate",
                   "b_spatial", "gla_norm", "sgu_ln_g", "sgu_ln_b"]
    loss_out, small = _small_adamw(
        _small_sum([d_wsp, d_wgu, d_g1, d_gpm, d_gpf, d_gpo, d_bg, d_bsp_t, d_gn, d_lng, d_lnb, loss]),
        [w_spatial, w_gate_up, norm_pre_mix, norm_post_mix, norm_pre_ffn, norm_post_ffn, b_gate, b_spatial, gla_norm,
         sgu_ln_g, sgu_ln_b],
        [m_w_spatial, m_w_gate_up, m_norm_pre_mix, m_norm_post_mix, m_norm_pre_ffn, m_norm_post_ffn, m_b_gate,
         m_b_spatial, m_gla_norm, m_sgu_ln_g, m_sgu_ln_b],
        [v_w_spatial, v_w_gate_up, v_norm_pre_mix, v_norm_post_mix, v_norm_pre_ffn, v_norm_post_ffn, v_b_gate,
         v_b_spatial, v_gla_norm, v_sgu_ln_g, v_sgu_ln_b])
    for store, vals in zip((grads, deltas, new_m, new_v), small):
        store.update(zip(small_names, vals))

    order = ["norm_pre_mix", "w_in", "w_gate_up", "b_gate", "gla_norm", "sgu_ln_g", "sgu_ln_b", "w_spatial", "b_spatial",
             "w_branch_gla", "w_branch_sgu", "w_out", "norm_post_mix", "norm_pre_ffn", "w_ffn_in", "w_ffn_out",
             "norm_post_ffn"]
    out = [loss_out, grad_x[None]]
    for store in (grads, deltas, new_m, new_v):
        out.extend(store[n] for n in order)
    return tuple(out)
```
